```python
import math
import jax, jax.numpy as jnp
from jax import lax
import numpy as np

D_MODEL = 1024
BATCH = 16
SEQ = 2048
DEPTH = 1

MIX_WIDTH = D_MODEL
POOL_WIDTH = MIX_WIDTH // 2
POOL_GROUPS = 4
POOL_GROUP_DIM = POOL_WIDTH // POOL_GROUPS
POOL_WINDOWS = (2, 4, 8, 16)
ATTN_WIDTH = MIX_WIDTH - POOL_WIDTH
HEAD_DIM = 64
N_HEADS = ATTN_WIDTH // HEAD_DIM
D_FF = 2816
Q_BLOCK = 128
IN_COLS = POOL_WIDTH + 3 * ATTN_WIDTH + N_HEADS
EPS = 1e-6

kernel_name = "hymba_pool_fox_macaron_block"


def rmsnorm(x, g):
    xf = x.astype(jnp.float32)
    y = xf * lax.rsqrt(jnp.mean(xf * xf, axis=-1, keepdims=True) + EPS)
    return (y * g.astype(jnp.float32)).astype(x.dtype)


def swiglu(h, w_gate, w_up, w_down):
    return (jax.nn.silu(h @ w_gate) * (h @ w_up)) @ w_down


def causal_window_mean(v, w):
    B, S, C = v.shape
    vf = v.astype(jnp.float32)
    cs = jnp.cumsum(vf, axis=1)
    shifted = jnp.concatenate([jnp.zeros((B, w, C), jnp.float32), cs[:, : S - w]], axis=1)
    count = jnp.minimum(jnp.arange(1, S + 1, dtype=jnp.float32), float(w))
    return ((cs - shifted) / count[None, :, None]).astype(v.dtype)


def pool_mixer(pv, pool_w, pool_scale):
    B, S, _ = pv.shape
    groups = pv.reshape(B, S, POOL_GROUPS, POOL_GROUP_DIM)
    pooled = jnp.stack(
        [causal_window_mean(groups[:, :, g], POOL_WINDOWS[g]) for g in range(POOL_GROUPS)], axis=2
    ) - groups
    mixed = jnp.einsum("bsgc,gcd->bsgd", pooled, pool_w)
    return mixed.reshape(B, S, POOL_WIDTH) * pool_scale


def forgetting_attention(q, k, v, f_logit, b_forget, q_norm, k_norm):
    B, S, H, Dh = q.shape
    q = rmsnorm(q, q_norm).transpose(0, 2, 1, 3)
    k = rmsnorm(k, k_norm).transpose(0, 2, 1, 3)
    v = v.transpose(0, 2, 1, 3)
    log_f = jax.nn.log_sigmoid((f_logit + b_forget).astype(jnp.float32))
    F = jnp.cumsum(log_f, axis=1).transpose(0, 2, 1)
    scale = 1.0 / math.sqrt(Dh)
    outs = []
    for i in range(S // Q_BLOCK):
        q0, end = i * Q_BLOCK, (i + 1) * Q_BLOCK
        qb = q[:, :, q0:end]
        kb, vb = k[:, :, :end], v[:, :, :end]
        logits = jnp.einsum("bhqd,bhkd->bhqk", qb, kb).astype(jnp.float32) * scale
        logits = logits + F[:, :, q0:end, None] - F[:, :, None, :end]
        q_pos = jnp.arange(q0, end)[:, None]
        k_pos = jnp.arange(end)[None, :]
        logits = jnp.where(q_pos >= k_pos, logits, -jnp.inf)
        p = jax.nn.softmax(logits, axis=-1).astype(vb.dtype)
        outs.append(jnp.einsum("bhqk,bhkd->bhqd", p, vb))
    o = jnp.concatenate(outs, axis=2)
    return o.transpose(0, 2, 1, 3).reshape(B, S, H * Dh)


def _fwd_setup_inputs(seed: int = 0) -> dict:
    key = jax.random.key(seed)
    ks = jax.random.split(key, 24)
    f32 = jnp.float32

    def nrm(k, shape, fan_in):
        return jax.random.normal(k, shape, f32) * fan_in ** -0.5

    def gain(k, shape):
        return 1.0 + 0.02 * jax.random.normal(k, shape, f32)

    return {
        "x": jax.random.normal(ks[0], (BATCH, SEQ, D_MODEL), f32),
        "ffn1_norm": gain(ks[1], (D_MODEL,)),
        "ffn1_w_gate": nrm(ks[2], (D_MODEL, D_FF), D_MODEL),
        "ffn1_w_up": nrm(ks[3], (D_MODEL, D_FF), D_MODEL),
        "ffn1_w_down": nrm(ks[4], (D_FF, D_MODEL), D_FF),
        "mix_norm": gain(ks[5], (D_MODEL,)),
        "w_in": nrm(ks[6], (D_MODEL, IN_COLS), D_MODEL),
        "b_forget": jax.random.uniform(ks[7], (N_HEADS,), f32, 1.0, 4.0),
        "pool_w": nrm(ks[8], (POOL_GROUPS, POOL_GROUP_DIM, POOL_GROUP_DIM), POOL_GROUP_DIM),
        "pool_scale": gain(ks[9], (POOL_WIDTH,)),
        "q_norm": gain(ks[10], (HEAD_DIM,)),
        "k_norm": gain(ks[11], (HEAD_DIM,)),
        "out_norm_pool": gain(ks[12], (POOL_WIDTH,)),
        "out_norm_attn": gain(ks[13], (ATTN_WIDTH,)),
        "w_out": nrm(ks[14], (MIX_WIDTH, D_MODEL), MIX_WIDTH),
        "ffn2_norm": gain(ks[15], (D_MODEL,)),
        "ffn2_w_gate": nrm(ks[16], (D_MODEL, D_FF), D_MODEL),
        "ffn2_w_up": nrm(ks[17], (D_MODEL, D_FF), D_MODEL),
        "ffn2_w_down": nrm(ks[18], (D_FF, D_MODEL), D_FF),
    }


def _fwd_reference(x, ffn1_norm, ffn1_w_gate, ffn1_w_up, ffn1_w_down, mix_norm, w_in, b_forget,
              pool_w, pool_scale, q_norm, k_norm, out_norm_pool, out_norm_attn, w_out,
              ffn2_norm, ffn2_w_gate, ffn2_w_up, ffn2_w_down):
    B, S, _ = x.shape
    for _layer in range(DEPTH):
        x = x + 0.5 * swiglu(rmsnorm(x, ffn1_norm), ffn1_w_gate, ffn1_w_up, ffn1_w_down)

        h = rmsnorm(x, mix_norm) @ w_in
        c0 = POOL_WIDTH
        pv = h[..., :c0]
        q = h[..., c0:c0 + ATTN_WIDTH].reshape(B, S, N_HEADS, HEAD_DIM)
        k = h[..., c0 + ATTN_WIDTH:c0 + 2 * ATTN_WIDTH].reshape(B, S, N_HEADS, HEAD_DIM)
        v = h[..., c0 + 2 * ATTN_WIDTH:c0 + 3 * ATTN_WIDTH].reshape(B, S, N_HEADS, HEAD_DIM)
        f_logit = h[..., c0 + 3 * ATTN_WIDTH:]

        y_pool = rmsnorm(pool_mixer(pv, pool_w, pool_scale), out_norm_pool)
        y_attn = rmsnorm(forgetting_attention(q, k, v, f_logit, b_forget, q_norm, k_norm), out_norm_attn)
        x = x + jnp.concatenate([y_pool, y_attn], axis=-1) @ w_out

        x = x + 0.5 * swiglu(rmsnorm(x, ffn2_norm), ffn2_w_gate, ffn2_w_up, ffn2_w_down)
    return x


import jax as _jax
import jax.numpy as _jnp

TWIN_FORMAT = 'train_step'
FWD_PARAMS = ['x', 'ffn1_norm', 'ffn1_w_gate', 'ffn1_w_up', 'ffn1_w_down', 'mix_norm', 'w_in', 'b_forget', 'pool_w', 'pool_scale', 'q_norm', 'k_norm', 'out_norm_pool', 'out_norm_attn', 'w_out', 'ffn2_norm', 'ffn2_w_gate', 'ffn2_w_up', 'ffn2_w_down']
TWIN_WEIGHTS = ['ffn1_norm', 'ffn1_w_gate', 'ffn1_w_up', 'ffn1_w_down', 'mix_norm', 'w_in', 'b_forget', 'pool_w', 'pool_scale', 'q_norm', 'k_norm', 'out_norm_pool', 'out_norm_attn', 'w_out', 'ffn2_norm', 'ffn2_w_gate', 'ffn2_w_up', 'ffn2_w_down']
TWIN_DIFF_INPUT = 'x'
TWIN_INPUTS = ['x', 'ffn1_norm', 'ffn1_w_gate', 'ffn1_w_up', 'ffn1_w_down', 'mix_norm', 'w_in', 'b_forget', 'pool_w', 'pool_scale', 'q_norm', 'k_norm', 'out_norm_pool', 'out_norm_attn', 'w_out', 'ffn2_norm', 'ffn2_w_gate', 'ffn2_w_up', 'ffn2_w_down', 'loss_target', 'm_ffn1_norm', 'm_ffn1_w_gate', 'm_ffn1_w_up', 'm_ffn1_w_down', 'm_mix_norm', 'm_w_in', 'm_b_forget', 'm_pool_w', 'm_pool_scale', 'm_q_norm', 'm_k_norm', 'm_out_norm_pool', 'm_out_norm_attn', 'm_w_out', 'm_ffn2_norm', 'm_ffn2_w_gate', 'm_ffn2_w_up', 'm_ffn2_w_down', 'v_ffn1_norm', 'v_ffn1_w_gate', 'v_ffn1_w_up', 'v_ffn1_w_down', 'v_mix_norm', 'v_w_in', 'v_b_forget', 'v_pool_w', 'v_pool_scale', 'v_q_norm', 'v_k_norm', 'v_out_norm_pool', 'v_out_norm_attn', 'v_w_out', 'v_ffn2_norm', 'v_ffn2_w_gate', 'v_ffn2_w_up', 'v_ffn2_w_down']
TWIN_OUTPUTS = ['loss', 'grad_x', 'grad_ffn1_norm', 'grad_ffn1_w_gate', 'grad_ffn1_w_up', 'grad_ffn1_w_down', 'grad_mix_norm', 'grad_w_in', 'grad_b_forget', 'grad_pool_w', 'grad_pool_scale', 'grad_q_norm', 'grad_k_norm', 'grad_out_norm_pool', 'grad_out_norm_attn', 'grad_w_out', 'grad_ffn2_norm', 'grad_ffn2_w_gate', 'grad_ffn2_w_up', 'grad_ffn2_w_down', 'delta_ffn1_norm', 'delta_ffn1_w_gate', 'delta_ffn1_w_up', 'delta_ffn1_w_down', 'delta_mix_norm', 'delta_w_in', 'delta_b_forget', 'delta_pool_w', 'delta_pool_scale', 'delta_q_norm', 'delta_k_norm', 'delta_out_norm_pool', 'delta_out_norm_attn', 'delta_w_out', 'delta_ffn2_norm', 'delta_ffn2_w_gate', 'delta_ffn2_w_up', 'delta_ffn2_w_down', 'new_m_ffn1_norm', 'new_m_ffn1_w_gate', 'new_m_ffn1_w_up', 'new_m_ffn1_w_down', 'new_m_mix_norm', 'new_m_w_in', 'new_m_b_forget', 'new_m_pool_w', 'new_m_pool_scale', 'new_m_q_norm', 'new_m_k_norm', 'new_m_out_norm_pool', 'new_m_out_norm_attn', 'new_m_w_out', 'new_m_ffn2_norm', 'new_m_ffn2_w_gate', 'new_m_ffn2_w_up', 'new_m_ffn2_w_down', 'new_v_ffn1_norm', 'new_v_ffn1_w_gate', 'new_v_ffn1_w_up', 'new_v_ffn1_w_down', 'new_v_mix_norm', 'new_v_w_in', 'new_v_b_forget', 'new_v_pool_w', 'new_v_pool_scale', 'new_v_q_norm', 'new_v_k_norm', 'new_v_out_norm_pool', 'new_v_out_norm_attn', 'new_v_w_out', 'new_v_ffn2_norm', 'new_v_ffn2_w_gate', 'new_v_ffn2_w_up', 'new_v_ffn2_w_down']
TWIN_LEAF_KINDS = {'loss': 'loss', 'grad_x': 'grad_x', 'grad_ffn1_norm': 'grad_w', 'grad_ffn1_w_gate': 'grad_w', 'grad_ffn1_w_up': 'grad_w', 'grad_ffn1_w_down': 'grad_w', 'grad_mix_norm': 'grad_w', 'grad_w_in': 'grad_w', 'grad_b_forget': 'grad_w', 'grad_pool_w': 'grad_w', 'grad_pool_scale': 'grad_w', 'grad_q_norm': 'grad_w', 'grad_k_norm': 'grad_w', 'grad_out_norm_pool': 'grad_w', 'grad_out_norm_attn': 'grad_w', 'grad_w_out': 'grad_w', 'grad_ffn2_norm': 'grad_w', 'grad_ffn2_w_gate': 'grad_w', 'grad_ffn2_w_up': 'grad_w', 'grad_ffn2_w_down': 'grad_w', 'delta_ffn1_norm': 'delta_w', 'delta_ffn1_w_gate': 'delta_w', 'delta_ffn1_w_up': 'delta_w', 'delta_ffn1_w_down': 'delta_w', 'delta_mix_norm': 'delta_w', 'delta_w_in': 'delta_w', 'delta_b_forget': 'delta_w', 'delta_pool_w': 'delta_w', 'delta_pool_scale': 'delta_w', 'delta_q_norm': 'delta_w', 'delta_k_norm': 'delta_w', 'delta_out_norm_pool': 'delta_w', 'delta_out_norm_attn': 'delta_w', 'delta_w_out': 'delta_w', 'delta_ffn2_norm': 'delta_w', 'delta_ffn2_w_gate': 'delta_w', 'delta_ffn2_w_up': 'delta_w', 'delta_ffn2_w_down': 'delta_w', 'new_m_ffn1_norm': 'new_m', 'new_m_ffn1_w_gate': 'new_m', 'new_m_ffn1_w_up': 'new_m', 'new_m_ffn1_w_down': 'new_m', 'new_m_mix_norm': 'new_m', 'new_m_w_in': 'new_m', 'new_m_b_forget': 'new_m', 'new_m_pool_w': 'new_m', 'new_m_pool_scale': 'new_m', 'new_m_q_norm': 'new_m', 'new_m_k_norm': 'new_m', 'new_m_out_norm_pool': 'new_m', 'new_m_out_norm_attn': 'new_m', 'new_m_w_out': 'new_m', 'new_m_ffn2_norm': 'new_m', 'new_m_ffn2_w_gate': 'new_m', 'new_m_ffn2_w_up': 'new_m', 'new_m_ffn2_w_down': 'new_m', 'new_v_ffn1_norm': 'new_v', 'new_v_ffn1_w_gate': 'new_v', 'new_v_ffn1_w_up': 'new_v', 'new_v_ffn1_w_down': 'new_v', 'new_v_mix_norm': 'new_v', 'new_v_w_in': 'new_v', 'new_v_b_forget': 'new_v', 'new_v_pool_w': 'new_v', 'new_v_pool_scale': 'new_v', 'new_v_q_norm': 'new_v', 'new_v_k_norm': 'new_v', 'new_v_out_norm_pool': 'new_v', 'new_v_out_norm_attn': 'new_v', 'new_v_w_out': 'new_v', 'new_v_ffn2_norm': 'new_v', 'new_v_ffn2_w_gate': 'new_v', 'new_v_ffn2_w_up': 'new_v', 'new_v_ffn2_w_down': 'new_v'}


def _forward(args):
    return _fwd_reference(*[args[k] for k in FWD_PARAMS])


def _output_shape():
    out = _jax.eval_shape(lambda: _forward(_fwd_setup_inputs(0)))
    return out.shape, out.dtype

N_MICROBATCH = 1
ADAM_LR = 0.001
ADAM_B1 = 0.9
ADAM_B2 = 0.999
ADAM_EPS = 1e-08
ADAM_WD = 0.01
ADAM_STEP = 10
PER_EXAMPLE_BATCH_AXIS = {'x': 0, 'loss_target': 0}
SHARED_INPUTS = []
_WEIGHT_DTYPES = {'ffn1_norm': _jnp.float32, 'ffn1_w_gate': _jnp.float32, 'ffn1_w_up': _jnp.float32, 'ffn1_w_down': _jnp.float32, 'mix_norm': _jnp.float32, 'w_in': _jnp.float32, 'b_forget': _jnp.float32, 'pool_w': _jnp.float32, 'pool_scale': _jnp.float32, 'q_norm': _jnp.float32, 'k_norm': _jnp.float32, 'out_norm_pool': _jnp.float32, 'out_norm_attn': _jnp.float32, 'w_out': _jnp.float32, 'ffn2_norm': _jnp.float32, 'ffn2_w_gate': _jnp.float32, 'ffn2_w_up': _jnp.float32, 'ffn2_w_down': _jnp.float32}
MOMENT_SCALE = {'ffn1_norm': 6.213019e+00, 'ffn1_w_gate': 1.009186e-01, 'ffn1_w_up': 1.092816e-01, 'ffn1_w_down': 1.815877e-01, 'mix_norm': 7.645516e-01, 'w_in': 4.955763e-01, 'b_forget': 7.646352e+00, 'pool_w': 1.185735e+00, 'pool_scale': 2.165540e+00, 'q_norm': 7.920406e-01, 'k_norm': 7.946430e-01, 'out_norm_pool': 3.304823e+01, 'out_norm_attn': 3.483089e+01, 'w_out': 1.206391e+00, 'ffn2_norm': 6.159452e+00, 'ffn2_w_gate': 6.747446e-02, 'ffn2_w_up': 9.460434e-02, 'ffn2_w_down': 1.519471e-01}


def _to_microbatches(a, axis):
    t = _jnp.moveaxis(a, axis, 0)
    t = t.reshape((N_MICROBATCH, t.shape[0] // N_MICROBATCH) + t.shape[1:])
    return _jnp.moveaxis(t, 1, axis + 1)


def setup_inputs(seed: int = 0) -> dict:
    inp = _fwd_setup_inputs(seed)
    key = _jax.random.fold_in(_jax.random.key(seed), 7919)
    shape, _ = _output_shape()
    out = dict(inp)
    out["loss_target"] = _jax.random.normal(_jax.random.fold_in(key, 0), shape, _jnp.float32)
    for i, name in enumerate(TWIN_WEIGHTS):
        w = inp[name].astype(_jnp.float32)
        if MOMENT_SCALE is None:
            s = _jnp.sqrt(_jnp.mean(_jnp.square(w)) + 1e-30)
        else:
            s = MOMENT_SCALE[name]
        km, kv = _jax.random.split(_jax.random.fold_in(key, i + 1))
        out[name] = w
        out["m_" + name] = s * _jax.random.normal(km, w.shape, _jnp.float32)
        out["v_" + name] = (s * s) * _jax.random.uniform(kv, w.shape, _jnp.float32, 0.5, 1.5)
    if N_MICROBATCH > 1:
        for name, axis in PER_EXAMPLE_BATCH_AXIS.items():
            out[name] = _to_microbatches(out[name], axis)
    return {'x': out['x'], 'ffn1_norm': out['ffn1_norm'], 'ffn1_w_gate': out['ffn1_w_gate'], 'ffn1_w_up': out['ffn1_w_up'], 'ffn1_w_down': out['ffn1_w_down'], 'mix_norm': out['mix_norm'], 'w_in': out['w_in'], 'b_forget': out['b_forget'], 'pool_w': out['pool_w'], 'pool_scale': out['pool_scale'], 'q_norm': out['q_norm'], 'k_norm': out['k_norm'], 'out_norm_pool': out['out_norm_pool'], 'out_norm_attn': out['out_norm_attn'], 'w_out': out['w_out'], 'ffn2_norm': out['ffn2_norm'], 'ffn2_w_gate': out['ffn2_w_gate'], 'ffn2_w_up': out['ffn2_w_up'], 'ffn2_w_down': out['ffn2_w_down'], 'loss_target': out['loss_target'], 'm_ffn1_norm': out['m_ffn1_norm'], 'm_ffn1_w_gate': out['m_ffn1_w_gate'], 'm_ffn1_w_up': out['m_ffn1_w_up'], 'm_ffn1_w_down': out['m_ffn1_w_down'], 'm_mix_norm': out['m_mix_norm'], 'm_w_in': out['m_w_in'], 'm_b_forget': out['m_b_forget'], 'm_pool_w': out['m_pool_w'], 'm_pool_scale': out['m_pool_scale'], 'm_q_norm': out['m_q_norm'], 'm_k_norm': out['m_k_norm'], 'm_out_norm_pool': out['m_out_norm_pool'], 'm_out_norm_attn': out['m_out_norm_attn'], 'm_w_out': out['m_w_out'], 'm_ffn2_norm': out['m_ffn2_norm'], 'm_ffn2_w_gate': out['m_ffn2_w_gate'], 'm_ffn2_w_up': out['m_ffn2_w_up'], 'm_ffn2_w_down': out['m_ffn2_w_down'], 'v_ffn1_norm': out['v_ffn1_norm'], 'v_ffn1_w_gate': out['v_ffn1_w_gate'], 'v_ffn1_w_up': out['v_ffn1_w_up'], 'v_ffn1_w_down': out['v_ffn1_w_down'], 'v_mix_norm': out['v_mix_norm'], 'v_w_in': out['v_w_in'], 'v_b_forget': out['v_b_forget'], 'v_pool_w': out['v_pool_w'], 'v_pool_scale': out['v_pool_scale'], 'v_q_norm': out['v_q_norm'], 'v_k_norm': out['v_k_norm'], 'v_out_norm_pool': out['v_out_norm_pool'], 'v_out_norm_attn': out['v_out_norm_attn'], 'v_w_out': out['v_w_out'], 'v_ffn2_norm': out['v_ffn2_norm'], 'v_ffn2_w_gate': out['v_ffn2_w_gate'], 'v_ffn2_w_up': out['v_ffn2_w_up'], 'v_ffn2_w_down': out['v_ffn2_w_down']}


def _loss(weights, diff, rest, loss_target):
    with _jax.named_scope("forward"):
        args = {**rest, TWIN_DIFF_INPUT: diff, **{k: w.astype(_WEIGHT_DTYPES[k]) for k, w in weights.items()}}
        y = _forward(args)
    with _jax.named_scope("loss_head"):
        err = _jnp.square(y.astype(_jnp.float32) - loss_target)
        return 0.5 * _jnp.sum(_jnp.mean(err, axis=-1)) if err.ndim else 0.5 * err


def _adamw(w, g, m, v):
    m = ADAM_B1 * m + (1.0 - ADAM_B1) * g
    v = ADAM_B2 * v + (1.0 - ADAM_B2) * _jnp.square(g)
    m_hat = m / (1.0 - ADAM_B1 ** ADAM_STEP)
    v_hat = v / (1.0 - ADAM_B2 ** ADAM_STEP)
    delta = -ADAM_LR * (m_hat / (_jnp.sqrt(v_hat) + ADAM_EPS) + ADAM_WD * w)
    return delta, m, v


def reference(x, ffn1_norm, ffn1_w_gate, ffn1_w_up, ffn1_w_down, mix_norm, w_in, b_forget, pool_w, pool_scale, q_norm, k_norm, out_norm_pool, out_norm_attn, w_out, ffn2_norm, ffn2_w_gate, ffn2_w_up, ffn2_w_down, loss_target, m_ffn1_norm, m_ffn1_w_gate, m_ffn1_w_up, m_ffn1_w_down, m_mix_norm, m_w_in, m_b_forget, m_pool_w, m_pool_scale, m_q_norm, m_k_norm, m_out_norm_pool, m_out_norm_attn, m_w_out, m_ffn2_norm, m_ffn2_w_gate, m_ffn2_w_up, m_ffn2_w_down, v_ffn1_norm, v_ffn1_w_gate, v_ffn1_w_up, v_ffn1_w_down, v_mix_norm, v_w_in, v_b_forget, v_pool_w, v_pool_scale, v_q_norm, v_k_norm, v_out_norm_pool, v_out_norm_attn, v_w_out, v_ffn2_norm, v_ffn2_w_gate, v_ffn2_w_up, v_ffn2_w_down):
    given = dict(x=x, ffn1_norm=ffn1_norm, ffn1_w_gate=ffn1_w_gate, ffn1_w_up=ffn1_w_up, ffn1_w_down=ffn1_w_down, mix_norm=mix_norm, w_in=w_in, b_forget=b_forget, pool_w=pool_w, pool_scale=pool_scale, q_norm=q_norm, k_norm=k_norm, out_norm_pool=out_norm_pool, out_norm_attn=out_norm_attn, w_out=w_out, ffn2_norm=ffn2_norm, ffn2_w_gate=ffn2_w_gate, ffn2_w_up=ffn2_w_up, ffn2_w_down=ffn2_w_down, loss_target=loss_target, m_ffn1_norm=m_ffn1_norm, m_ffn1_w_gate=m_ffn1_w_gate, m_ffn1_w_up=m_ffn1_w_up, m_ffn1_w_down=m_ffn1_w_down, m_mix_norm=m_mix_norm, m_w_in=m_w_in, m_b_forget=m_b_forget, m_pool_w=m_pool_w, m_pool_scale=m_pool_scale, m_q_norm=m_q_norm, m_k_norm=m_k_norm, m_out_norm_pool=m_out_norm_pool, m_out_norm_attn=m_out_norm_attn, m_w_out=m_w_out, m_ffn2_norm=m_ffn2_norm, m_ffn2_w_gate=m_ffn2_w_gate, m_ffn2_w_up=m_ffn2_w_up, m_ffn2_w_down=m_ffn2_w_down, v_ffn1_norm=v_ffn1_norm, v_ffn1_w_gate=v_ffn1_w_gate, v_ffn1_w_up=v_ffn1_w_up, v_ffn1_w_down=v_ffn1_w_down, v_mix_norm=v_mix_norm, v_w_in=v_w_in, v_b_forget=v_b_forget, v_pool_w=v_pool_w, v_pool_scale=v_pool_scale, v_q_norm=v_q_norm, v_k_norm=v_k_norm, v_out_norm_pool=v_out_norm_pool, v_out_norm_attn=v_out_norm_attn, v_w_out=v_w_out, v_ffn2_norm=v_ffn2_norm, v_ffn2_w_gate=v_ffn2_w_gate, v_ffn2_w_up=v_ffn2_w_up, v_ffn2_w_down=v_ffn2_w_down)
    weights = {n: given[n] for n in TWIN_WEIGHTS}
    shared = {n: given[n] for n in SHARED_INPUTS}
    per_example = {n: given[n] for n in ['x']}
    grad_fn = _jax.value_and_grad(_loss, argnums=(0, 1))

    def one_microbatch(ex, loss_target):
        ex = dict(ex)
        diff = ex.pop(TWIN_DIFF_INPUT)
        return grad_fn(weights, diff, {**shared, **ex}, loss_target)

    if N_MICROBATCH == 1:
        loss, (grad_w, grad_x) = one_microbatch(per_example, given["loss_target"])
    else:
        def body(carry, xs):
            loss_sum, grad_sum = carry
            l_k, (gw_k, gx_k) = one_microbatch(xs[0], xs[1])
            with _jax.named_scope("update"):
                return (loss_sum + l_k, _jax.tree.map(_jnp.add, grad_sum, gw_k)), gx_k

        init = (_jnp.zeros((), _jnp.float32), _jax.tree.map(_jnp.zeros_like, weights))
        (loss, grad_w), grad_x = _jax.lax.scan(body, init, (per_example, given["loss_target"]))
    with _jax.named_scope("update"):
        delta_w, new_m, new_v = {}, {}, {}
        for n in TWIN_WEIGHTS:
            delta_w[n], new_m[n], new_v[n] = _adamw(weights[n], grad_w[n], given["m_" + n], given["v_" + n])
    return (loss, grad_x, *[grad_w[n] for n in TWIN_WEIGHTS], *[delta_w[n] for n in TWIN_WEIGHTS],
            *[new_m[n] for n in TWIN_WEIGHTS], *[new_v[n] for n in TWIN_WEIGHTS])
```

```python
import functools

import jax
import jax.numpy as jnp
from jax import lax
from jax.experimental import pallas as pl
from jax.experimental.pallas import tpu as pltpu

F32 = jnp.float32
BF16 = jnp.bfloat16
EPS = 1e-6
NEG = -1e30
ADAM_LR = 0.001
ADAM_B1 = 0.9
ADAM_B2 = 0.999
ADAM_EPS = 1e-08
ADAM_WD = 0.01
ADAM_STEP = 10
POOL_WINDOWS = (2, 4, 8, 16)
HEAD_DIM = 64
N_HEADS = 8
LANES = 128
N_CHIPS = 4
ATT_BLOCK = 256
VMEM_LIMIT = 56 * 1024 * 1024
MESH_AXES = ("x", "y", "c")
ANY = pl.BlockSpec(memory_space=pl.ANY)
VM = pl.BlockSpec(memory_space=pltpu.VMEM)


def _params(**kw):
    return pltpu.CompilerParams(vmem_limit_bytes=VMEM_LIMIT, **kw)


def _dot(a, b):
    return jnp.dot(a, b, preferred_element_type=F32)


def _dot_nt(a, b):
    return lax.dot_general(a, b, (((1,), (1,)), ((), ())), preferred_element_type=F32)


def _dot_tn(a, b):
    return lax.dot_general(a, b, (((0,), (0,)), ((), ())), preferred_element_type=F32)


def _sigmoid(z):
    return 1.0 / (1.0 + jnp.exp(-z))


def _rms(xf):
    return lax.rsqrt(jnp.mean(xf * xf, axis=-1, keepdims=True) + EPS)


def _rms_bwd(xf, r, gain, dh):
    xh = xf * r
    dyg = dh * gain
    return r * (dyg - xh * jnp.mean(dyg * xh, axis=-1, keepdims=True)), dh * xh


def _total(v):
    return jnp.sum(jnp.sum(v, axis=1, keepdims=True), axis=0, keepdims=True)


def _ffn_fwd(x, gain, wg, wu, wd, target=None):
    t, d = x.shape
    nch, _, fc = wg.shape
    tm = min(512, t)
    nt = t // tm
    with_loss = target is not None

    def body(*refs):
        if with_loss:
            x_ref, g_ref, wg_ref, wu_ref, wd_ref, t_ref, o_ref, h_ref, a_ref, b_ref, s_ref, l_ref, acc_ref = refs
        else:
            x_ref, g_ref, wg_ref, wu_ref, wd_ref, o_ref, h_ref, a_ref, b_ref, s_ref, acc_ref = refs
        k = pl.program_id(1)

        @pl.when(k == 0)
        def _():
            xf = x_ref[...]
            h_ref[...] = ((xf * _rms(xf)) * g_ref[...]).astype(BF16)
            acc_ref[...] = jnp.zeros_like(acc_ref)

        h = h_ref[...]
        a = _dot(h, wg_ref[...])
        b = _dot(h, wu_ref[...])
        sb = ((a * _sigmoid(a)) * b).astype(BF16)
        a_ref[...] = a.astype(BF16)
        b_ref[...] = b.astype(BF16)
        s_ref[...] = sb
        acc_ref[...] += _dot(sb, wd_ref[...])

        @pl.when(k == nch - 1)
        def _():
            y = x_ref[...] + 0.5 * acc_ref[...]
            if with_loss:
                e = y - t_ref[...]
                o_ref[...] = e * (1.0 / d)
                l_ref[...] = jnp.broadcast_to(_total(e * e) * (0.5 / d), l_ref.shape)
            else:
                o_ref[...] = y

    row = pl.BlockSpec((tm, d), lambda i, k: (i, 0))
    chunk_in = pl.BlockSpec((None, d, fc), lambda i, k: (k, 0, 0))
    chunk_out = pl.BlockSpec((None, fc, d), lambda i, k: (k, 0, 0))
    act = pl.BlockSpec((None, tm, fc), lambda i, k: (k, i, 0))
    in_specs = [row, pl.BlockSpec((1, d), lambda i, k: (0, 0)), chunk_in, chunk_in, chunk_out]
    out_shape = [jax.ShapeDtypeStruct((t, d), F32), jax.ShapeDtypeStruct((t, d), BF16)]
    out_shape += [jax.ShapeDtypeStruct((nch, t, fc), BF16)] * 3
    out_specs = [row, row, act, act, act]
    args = [x, gain, wg, wu, wd]
    if with_loss:
        in_specs.append(row)
        args.append(target)
        out_shape.append(jax.ShapeDtypeStruct((nt, 8, LANES), F32))
        out_specs.append(pl.BlockSpec((None, 8, LANES), lambda i, k: (i, 0, 0)))
    return pl.pallas_call(
        body, out_shape=out_shape, grid=(nt, nch), in_specs=in_specs, out_specs=out_specs,
        scratch_shapes=[pltpu.VMEM((tm, d), F32)], compiler_params=_params(),
        name="ffn_fwd_loss" if with_loss else "ffn_fwd",
    )(*args)


def _ffn_bwd_x(dy, x, gain, a, b, wg, wu, wd, name):
    t, d = x.shape
    nch, _, fc = wg.shape
    tm = min(512, t)
    nt = t // tm

    def body(dy_ref, x_ref, g_ref, a_ref, b_ref, wg_ref, wu_ref, wd_ref, dx_ref, da_ref, db_ref, dg_ref, acc_ref):
        k = pl.program_id(1)

        @pl.when(k == 0)
        def _():
            acc_ref[...] = jnp.zeros_like(acc_ref)

        ds = 0.5 * _dot_nt(dy_ref[...].astype(BF16), wd_ref[...])
        av = a_ref[...].astype(F32)
        bv = b_ref[...].astype(F32)
        sig = _sigmoid(av)
        dab = (ds * bv * (sig * (1.0 + av * (1.0 - sig)))).astype(BF16)
        dbb = (ds * (av * sig)).astype(BF16)
        da_ref[...] = dab
        db_ref[...] = dbb
        acc_ref[...] += _dot_nt(dab, wg_ref[...]) + _dot_nt(dbb, wu_ref[...])

        @pl.when(k == nch - 1)
        def _():
            xf = x_ref[...]
            dxn, dgr = _rms_bwd(xf, _rms(xf), g_ref[...], acc_ref[...])
            dx_ref[...] = dy_ref[...] + dxn
            dg_ref[...] = jnp.sum(dgr, axis=0, keepdims=True)

    row = pl.BlockSpec((tm, d), lambda i, k: (i, 0))
    chunk_in = pl.BlockSpec((None, d, fc), lambda i, k: (k, 0, 0))
    chunk_out = pl.BlockSpec((None, fc, d), lambda i, k: (k, 0, 0))
    act = pl.BlockSpec((None, tm, fc), lambda i, k: (k, i, 0))
    return pl.pallas_call(
        body,
        out_shape=[jax.ShapeDtypeStruct((t, d), F32), jax.ShapeDtypeStruct((nch, t, fc), BF16),
                   jax.ShapeDtypeStruct((nch, t, fc), BF16), jax.ShapeDtypeStruct((nt, 1, d), F32)],
        grid=(nt, nch),
        in_specs=[row, row, pl.BlockSpec((1, d), lambda i, k: (0, 0)), act, act, chunk_in, chunk_in, chunk_out],
        out_specs=[row, act, act, pl.BlockSpec((None, 1, d), lambda i, k: (i, 0, 0))],
        scratch_shapes=[pltpu.VMEM((tm, d), F32)], compiler_params=_params(), name=name,
    )(dy, x, gain, a, b, wg, wu, wd)


def _ffn_bwd_w(h, s, da, db, dy, name):
    t, d = h.shape
    nch, _, fc = s.shape
    tm = min(512, t)
    nt = t // tm

    def body(h_ref, s_ref, da_ref, db_ref, dy_ref, dwg_ref, dwu_ref, dwd_ref):
        @pl.when(pl.program_id(1) == 0)
        def _():
            dwg_ref[...] = jnp.zeros_like(dwg_ref)
            dwu_ref[...] = jnp.zeros_like(dwu_ref)
            dwd_ref[...] = jnp.zeros_like(dwd_ref)

        hv = h_ref[...]
        dwg_ref[...] += _dot_tn(hv, da_ref[...])
        dwu_ref[...] += _dot_tn(hv, db_ref[...])
        dwd_ref[...] += _dot_tn(s_ref[...], (0.5 * dy_ref[...]).astype(BF16))

    row = pl.BlockSpec((tm, d), lambda k, i: (i, 0))
    act = pl.BlockSpec((None, tm, fc), lambda k, i: (k, i, 0))
    w_in = pl.BlockSpec((None, d, fc), lambda k, i: (k, 0, 0))
    w_out = pl.BlockSpec((None, fc, d), lambda k, i: (k, 0, 0))
    return pl.pallas_call(
        body,
        out_shape=[jax.ShapeDtypeStruct((nch, d, fc), F32), jax.ShapeDtypeStruct((nch, d, fc), F32),
                   jax.ShapeDtypeStruct((nch, fc, d), F32)],
        grid=(nch, nt), in_specs=[row, act, act, act, row], out_specs=[w_in, w_in, w_out],
        compiler_params=_params(), name=name,
    )(h, s, da, db, dy)


def _head_masks():
    lane = lax.broadcasted_iota(jnp.int32, (1, LANES), 1)
    return lane < HEAD_DIM


def _head_rms(x, lo):
    x2 = x * x
    s0 = jnp.sum(jnp.where(lo, x2, 0.0), axis=1, keepdims=True)
    s1 = jnp.sum(jnp.where(lo, 0.0, x2), axis=1, keepdims=True)
    return jnp.where(lo, lax.rsqrt(s0 * (1.0 / HEAD_DIM) + EPS), lax.rsqrt(s1 * (1.0 / HEAD_DIM) + EPS))


def _head_mean(v, lo):
    s0 = jnp.sum(jnp.where(lo, v, 0.0), axis=1, keepdims=True)
    s1 = jnp.sum(jnp.where(lo, 0.0, v), axis=1, keepdims=True)
    return jnp.where(lo, s0, s1) * (1.0 / HEAD_DIM)


def _mix_proj(x1, gain, wp, wf, qn, kn, pool_width, attn_width):
    t, d = x1.shape
    tm = min(512, t)
    nt = t // tm
    scale = HEAD_DIM ** -0.5
    c_q, c_k, c_v = pool_width, pool_width + attn_width, pool_width + 2 * attn_width

    def body(x_ref, g_ref, wp_ref, wf_ref, qn_ref, kn_ref, hm_ref, pv_ref, q_ref, k_ref, qh_ref, kh_ref, vb_ref, f_ref):
        xf = x_ref[...]
        hm = ((xf * _rms(xf)) * g_ref[...]).astype(BF16)
        hm_ref[...] = hm
        f_ref[...] = _dot(hm, wf_ref[...])
        pv_ref[...] = _dot(hm, wp_ref[:, 0:pool_width])
        vb_ref[...] = _dot(hm, wp_ref[:, c_v:c_v + attn_width]).astype(BF16)
        lo = _head_masks()
        for c0, raw_ref, hat_ref, n_ref, mul in ((c_q, q_ref, qh_ref, qn_ref, scale), (c_k, k_ref, kh_ref, kn_ref, 1.0)):
            raw = _dot(hm, wp_ref[:, c0:c0 + attn_width])
            raw_ref[...] = raw
            for blk in range(attn_width // LANES):
                sl = slice(blk * LANES, (blk + 1) * LANES)
                xb = raw[:, sl]
                hat_ref[:, sl] = (((xb * _head_rms(xb, lo)) * n_ref[:, sl]) * mul).astype(BF16)

    row = pl.BlockSpec((tm, d), lambda i: (i, 0))
    half = pl.BlockSpec((tm, attn_width), lambda i: (i, 0))
    const = lambda shape: pl.BlockSpec(shape, lambda i: (0, 0))
    return pl.pallas_call(
        body,
        out_shape=[jax.ShapeDtypeStruct((t, d), BF16), jax.ShapeDtypeStruct((t, pool_width), F32),
                   jax.ShapeDtypeStruct((t, attn_width), F32), jax.ShapeDtypeStruct((t, attn_width), F32),
                   jax.ShapeDtypeStruct((t, attn_width), BF16), jax.ShapeDtypeStruct((t, attn_width), BF16),
                   jax.ShapeDtypeStruct((t, attn_width), BF16), jax.ShapeDtypeStruct((t, LANES), F32)],
        grid=(nt,),
        in_specs=[row, const((1, d)), const(wp.shape), const(wf.shape), const((1, attn_width)), const((1, attn_width))],
        out_specs=[row, pl.BlockSpec((tm, pool_width), lambda i: (i, 0)), half, half, half, half, half,
                   pl.BlockSpec((tm, LANES), lambda i: (i, 0))],
        compiler_params=_params(), name="mix_proj",
    )(x1, gain, wp, wf, qn, kn)


def _shift_down(v, dist, row):
    return jnp.where(row >= dist, pltpu.roll(v, dist, 0), 0.0)


def _shift_up(v, dist, row, n):
    return jnp.where(row + dist < n, pltpu.roll(v, n - dist, 0), 0.0)


def _forget_prefix(f, bias, n_batch, seq):
    blk = min(ATT_BLOCK, seq)

    def body(f_ref, b_ref, fc_ref, fr_ref):
        z = f_ref[...] + b_ref[...]
        acc = jnp.minimum(z, 0.0) - jnp.log(1.0 + jnp.exp(-jnp.abs(z)))
        row = lax.broadcasted_iota(jnp.int32, (seq, 1), 0)
        dist = 1
        while dist < seq:
            acc = acc + _shift_down(acc, dist, row)
            dist *= 2
        fc_ref[...] = acc
        for c in range(seq // LANES):
            jb, off = divmod(c * LANES, blk)
            fr_ref[jb, :, off:off + LANES] = acc[c * LANES:(c + 1) * LANES, :].T[0:8, :]

    return pl.pallas_call(
        body,
        out_shape=[jax.ShapeDtypeStruct((n_batch * seq, LANES), F32),
                   jax.ShapeDtypeStruct((n_batch, seq // blk, 8, blk), F32)],
        grid=(n_batch,),
        in_specs=[pl.BlockSpec((seq, LANES), lambda b: (b, 0)), pl.BlockSpec((1, LANES), lambda b: (0, 0))],
        out_specs=[pl.BlockSpec((seq, LANES), lambda b: (b, 0)),
                   pl.BlockSpec((None, seq // blk, 8, blk), lambda b: (b, 0, 0, 0))],
        compiler_params=_params(), name="forget_prefix",
    )(f, bias)


def _pool_groups(pv_ref, pw_ref, ps_ref, seq):
    row = lax.broadcasted_iota(jnp.int32, (seq, 1), 0)
    pos = (row + 1).astype(F32)
    out = []
    for g, win in enumerate(POOL_WINDOWS):
        sl = slice(g * LANES, (g + 1) * LANES)
        xg = pv_ref[:, sl]
        acc = xg
        dist = 1
        while dist < win:
            acc = acc + _shift_down(acc, dist, row)
            dist *= 2
        pooled = (acc / jnp.minimum(pos, float(win)) - xg).astype(BF16)
        mixed = _dot(pooled, pw_ref[g])
        out.append((pooled, mixed, mixed * ps_ref[:, sl]))
    return out


def _pool_fwd(pv, pw, ps, onp, n_batch, seq):
    width = pv.shape[1]

    def body(pv_ref, pw_ref, ps_ref, on_ref, y_ref):
        groups = _pool_groups(pv_ref, pw_ref, ps_ref, seq)
        ssq = sum(jnp.sum(ms * ms, axis=1, keepdims=True) for _, _, ms in groups)
        r = lax.rsqrt(ssq * (1.0 / width) + EPS)
        for g, (_, _, ms) in enumerate(groups):
            sl = slice(g * LANES, (g + 1) * LANES)
            y_ref[:, sl] = ((ms * r) * on_ref[:, sl]).astype(BF16)

    return pl.pallas_call(
        body, out_shape=jax.ShapeDtypeStruct((n_batch * seq, width), BF16), grid=(n_batch,),
        in_specs=[pl.BlockSpec((seq, width), lambda b: (b, 0)), pl.BlockSpec(pw.shape, lambda b: (0, 0, 0)),
                  pl.BlockSpec((1, width), lambda b: (0, 0)), pl.BlockSpec((1, width), lambda b: (0, 0))],
        out_specs=pl.BlockSpec((seq, width), lambda b: (b, 0)),
        compiler_params=_params(), name="pool_fwd",
    )(pv, pw, ps, onp)


def _pool_bwd(pv, dyp, pw, ps, onp, n_batch, seq):
    width = pv.shape[1]

    def body(pv_ref, dy_ref, pw_ref, ps_ref, on_ref, dpv_ref, dpw_ref, dps_ref, don_ref):
        groups = _pool_groups(pv_ref, pw_ref, ps_ref, seq)
        ssq = sum(jnp.sum(ms * ms, axis=1, keepdims=True) for _, _, ms in groups)
        r = lax.rsqrt(ssq * (1.0 / width) + EPS)
        mean = sum(jnp.sum((dy_ref[:, g * LANES:(g + 1) * LANES] * on_ref[:, g * LANES:(g + 1) * LANES]) * (ms * r),
                           axis=1, keepdims=True) for g, (_, _, ms) in enumerate(groups)) * (1.0 / width)
        row = lax.broadcasted_iota(jnp.int32, (seq, 1), 0)
        pos = (row + 1).astype(F32)
        for g, (pooled, mixed, ms) in enumerate(groups):
            sl = slice(g * LANES, (g + 1) * LANES)
            dy = dy_ref[:, sl]
            xh = ms * r
            don_ref[:, sl] = jnp.sum(dy * xh, axis=0, keepdims=True)
            dms = r * (dy * on_ref[:, sl] - xh * mean)
            dps_ref[:, sl] = jnp.sum(dms * mixed, axis=0, keepdims=True)
            dmix = (dms * ps_ref[:, sl]).astype(BF16)
            dpw_ref[g] = _dot_tn(pooled, dmix)
            dpool = _dot_nt(dmix, pw_ref[g])
            win = POOL_WINDOWS[g]
            acc = dpool / jnp.minimum(pos, float(win))
            dist = 1
            while dist < win:
                acc = acc + _shift_up(acc, dist, row, seq)
                dist *= 2
            dpv_ref[:, sl] = (acc - dpool).astype(BF16)

    tok = pl.BlockSpec((seq, width), lambda b: (b, 0))
    vec = pl.BlockSpec((1, width), lambda b: (0, 0))
    pvec = pl.BlockSpec((None, 1, width), lambda b: (b, 0, 0))
    return pl.pallas_call(
        body,
        out_shape=[jax.ShapeDtypeStruct((n_batch * seq, width), BF16),
                   jax.ShapeDtypeStruct((n_batch,) + pw.shape, F32),
                   jax.ShapeDtypeStruct((n_batch, 1, width), F32), jax.ShapeDtypeStruct((n_batch, 1, width), F32)],
        grid=(n_batch,),
        in_specs=[tok, tok, pl.BlockSpec(pw.shape, lambda b: (0, 0, 0)), vec, vec],
        out_specs=[tok, pl.BlockSpec((None,) + pw.shape, lambda b: (b, 0, 0, 0)), pvec, pvec],
        compiler_params=_params(), name="pool_bwd",
    )(pv, dyp, pw, ps, onp)


def _pick_lane(tile, idx):
    lane = lax.broadcasted_iota(jnp.int32, (1, LANES), 1)
    return jnp.sum(jnp.where(lane == idx, tile, 0.0), axis=1, keepdims=True)


def _pick_row(tile, idx):
    sub = lax.broadcasted_iota(jnp.int32, (tile.shape[0], 1), 0)
    return jnp.sum(jnp.where(sub == idx, tile, 0.0), axis=0, keepdims=True)


def _put_lane(col, idx):
    lane = lax.broadcasted_iota(jnp.int32, (1, LANES), 1)
    return jnp.where(lane == idx, col, 0.0)


def _scores(qe, kj, fq, fk, i, j, blk):
    s = _dot_nt(qe, kj) + (fq - fk)
    row = lax.broadcasted_iota(jnp.int32, (blk, blk), 0) + i * blk
    col = lax.broadcasted_iota(jnp.int32, (blk, blk), 1) + j * blk
    return jnp.where(row >= col, s, NEG)


def _attn_fwd(qh, kh, vb, fc, fr, n_batch, seq):
    blk = min(ATT_BLOCK, seq)
    nq = seq // blk
    width = qh.shape[1]
    pairs = width // LANES

    def body(q_ref, k_ref, v_ref, fc_ref, fr_ref, o_ref, lse_ref):
        i, p = pl.program_id(1), pl.program_id(2)
        lo = _head_masks()
        q = q_ref[...]
        fq_all = fc_ref[...]
        outs, lses = [], []
        for e in range(2):
            head = 2 * p + e
            me = lo if e == 0 else jnp.logical_not(lo)
            qe = jnp.where(me, q, jnp.zeros_like(q))
            fq = _pick_lane(fq_all, head)

            def step(j, carry, qe=qe, fq=fq, head=head):
                m, l, acc = carry
                off = pl.multiple_of(j * blk, blk)
                kj = k_ref[pl.ds(off, blk), :]
                vj = v_ref[pl.ds(off, blk), :]
                fk = _pick_row(fr_ref[j], head)
                s = _scores(qe, kj, fq, fk, i, j, blk)
                m_new = jnp.maximum(m, jnp.max(s, axis=1, keepdims=True))
                pe = jnp.exp(s - m_new)
                alpha = jnp.exp(m - m_new)
                return m_new, alpha * l + jnp.sum(pe, axis=1, keepdims=True), alpha * acc + _dot(pe.astype(BF16), vj)

            init = (jnp.full((blk, 1), NEG, F32), jnp.zeros((blk, 1), F32), jnp.zeros((blk, LANES), F32))
            m, l, acc = lax.fori_loop(0, i + 1, step, init)
            outs.append(acc / l)
            lses.append(_put_lane(m + jnp.log(l), head))
        o_ref[...] = jnp.where(lo, outs[0], outs[1])

        @pl.when(p == 0)
        def _():
            lse_ref[...] = lses[0] + lses[1]

        @pl.when(p > 0)
        def _():
            lse_ref[...] += lses[0] + lses[1]

    qspec = pl.BlockSpec((blk, LANES), lambda b, i, p: (b * nq + i, p))
    kvspec = pl.BlockSpec((seq, LANES), lambda b, i, p: (b, p))
    col = pl.BlockSpec((blk, LANES), lambda b, i, p: (b * nq + i, 0))
    return pl.pallas_call(
        body,
        out_shape=[jax.ShapeDtypeStruct((n_batch * seq, width), F32), jax.ShapeDtypeStruct((n_batch * seq, LANES), F32)],
        grid=(n_batch, nq, pairs),
        in_specs=[qspec, kvspec, kvspec, col, pl.BlockSpec((None, nq, 8, blk), lambda b, i, p: (b, 0, 0, 0))],
        out_specs=[qspec, col],
        compiler_params=_params(), name="attn_fwd",
    )(qh, kh, vb, fc, fr)


def _attn_bwd_q(qh, kh, vb, do, lse, delta, fc, fr, n_batch, seq):
    blk = min(ATT_BLOCK, seq)
    nq = seq // blk
    width = qh.shape[1]
    pairs = width // LANES

    def body(q_ref, k_ref, v_ref, do_ref, lse_ref, dl_ref, fc_ref, fr_ref, dq_ref, dfq_ref):
        i, p = pl.program_id(1), pl.program_id(2)
        lo = _head_masks()
        q = q_ref[...]
        dov = do_ref[...]
        dqs, dfs = [], []
        for e in range(2):
            head = 2 * p + e
            me = lo if e == 0 else jnp.logical_not(lo)
            qe = jnp.where(me, q, jnp.zeros_like(q))
            doe = jnp.where(me, dov, jnp.zeros_like(dov))
            fq = _pick_lane(fc_ref[...], head)
            ls = _pick_lane(lse_ref[...], head)
            dl = _pick_lane(dl_ref[...], head)

            def step(j, carry, qe=qe, doe=doe, fq=fq, ls=ls, dl=dl, head=head):
                dq, dfq = carry
                off = pl.multiple_of(j * blk, blk)
                kj = k_ref[pl.ds(off, blk), :]
                vj = v_ref[pl.ds(off, blk), :]
                fk = _pick_row(fr_ref[j], head)
                pr = jnp.exp(_scores(qe, kj, fq, fk, i, j, blk) - ls)
                ds = pr * (_dot_nt(doe, vj) - dl)
                return dq + _dot(ds.astype(BF16), kj), dfq + jnp.sum(ds, axis=1, keepdims=True)

            dq, dfq = lax.fori_loop(0, i + 1, step, (jnp.zeros((blk, LANES), F32), jnp.zeros((blk, 1), F32)))
            dqs.append(dq)
            dfs.append(_put_lane(dfq, head))
        dq_ref[...] = jnp.where(lo, dqs[0], dqs[1])

        @pl.when(p == 0)
        def _():
            dfq_ref[...] = dfs[0] + dfs[1]

        @pl.when(p > 0)
        def _():
            dfq_ref[...] += dfs[0] + dfs[1]

    qspec = pl.BlockSpec((blk, LANES), lambda b, i, p: (b * nq + i, p))
    kvspec = pl.BlockSpec((seq, LANES), lambda b, i, p: (b, p))
    col = pl.BlockSpec((blk, LANES), lambda b, i, p: (b * nq + i, 0))
    return pl.pallas_call(
        body,
        out_shape=[jax.ShapeDtypeStruct((n_batch * seq, width), F32), jax.ShapeDtypeStruct((n_batch * seq, LANES), F32)],
        grid=(n_batch, nq, pairs),
        in_specs=[qspec, kvspec, kvspec, qspec, col, col, col, pl.BlockSpec((None, nq, 8, blk), lambda b, i, p: (b, 0, 0, 0))],
        out_specs=[qspec, col],
        compiler_params=_params(), name="attn_bwd_q",
    )(qh, kh, vb, do, lse, delta, fc, fr)


def _attn_bwd_kv(qh, kh, vb, do, lse, delta, fc, fr, n_batch, seq):
    blk = min(ATT_BLOCK, seq)
    nq = seq // blk
    width = qh.shape[1]
    pairs = width // LANES

    def body(q_ref, k_ref, v_ref, do_ref, lse_ref, dl_ref, fc_ref, fr_ref, dk_ref, dv_ref, dfk_ref):
        j, p = pl.program_id(1), pl.program_id(2)
        lo = _head_masks()
        kj = k_ref[...]
        vj = v_ref[...]
        dk = jnp.zeros((blk, LANES), F32)
        dv = jnp.zeros((blk, LANES), F32)
        sub = lax.broadcasted_iota(jnp.int32, (LANES, 1), 0)
        dfk_rows = jnp.zeros((LANES, blk), F32)
        for e in range(2):
            head = 2 * p + e
            me = lo if e == 0 else jnp.logical_not(lo)
            fk = _pick_row(fr_ref[...], head)

            def step(i, carry, me=me, fk=fk, head=head):
                dk, dv, dfk = carry
                off = pl.multiple_of(i * blk, blk)
                qi = q_ref[pl.ds(off, blk), :]
                doi = do_ref[pl.ds(off, blk), :]
                qe = jnp.where(me, qi, jnp.zeros_like(qi))
                doe = jnp.where(me, doi, jnp.zeros_like(doi))
                fq = _pick_lane(fc_ref[pl.ds(off, blk), :], head)
                ls = _pick_lane(lse_ref[pl.ds(off, blk), :], head)
                dl = _pick_lane(dl_ref[pl.ds(off, blk), :], head)
                pr = jnp.exp(_scores(qe, kj, fq, fk, i, j, blk) - ls)
                ds = pr * (_dot_nt(doe, vj) - dl)
                return (dk + _dot_tn(ds.astype(BF16), qe), dv + _dot_tn(pr.astype(BF16), doe),
                        dfk - jnp.sum(ds, axis=0, keepdims=True))

            dk, dv, dfk = lax.fori_loop(j, nq, step, (dk, dv, jnp.zeros((1, blk), F32)))
            dfk_rows = dfk_rows + jnp.where(sub == head, dfk, 0.0)
        dk_ref[...] = dk
        dv_ref[...] = dv.astype(BF16)
        dfk_cols = jnp.concatenate([dfk_rows[:, c * LANES:(c + 1) * LANES].T for c in range(blk // LANES)], axis=0)

        @pl.when(p == 0)
        def _():
            dfk_ref[...] = dfk_cols

        @pl.when(p > 0)
        def _():
            dfk_ref[...] += dfk_cols

    kspec = pl.BlockSpec((blk, LANES), lambda b, j, p: (b * nq + j, p))
    seqspec = pl.BlockSpec((seq, LANES), lambda b, j, p: (b, p))
    seqcol = pl.BlockSpec((seq, LANES), lambda b, j, p: (b, 0))
    col = pl.BlockSpec((blk, LANES), lambda b, j, p: (b * nq + j, 0))
    return pl.pallas_call(
        body,
        out_shape=[jax.ShapeDtypeStruct((n_batch * seq, width), F32), jax.ShapeDtypeStruct((n_batch * seq, width), BF16),
                   jax.ShapeDtypeStruct((n_batch * seq, LANES), F32)],
        grid=(n_batch, nq, pairs),
        in_specs=[seqspec, kspec, kspec, seqspec, seqcol, seqcol, seqcol,
                  pl.BlockSpec((None, None, 8, blk), lambda b, j, p: (b, j, 0, 0))],
        out_specs=[kspec, kspec, col],
        compiler_params=_params(), name="attn_bwd_kv",
    )(qh, kh, vb, do, lse, delta, fc, fr)


def _forget_bwd(dfq, dfk, f, bias, n_batch, seq):
    def body(dfq_ref, dfk_ref, f_ref, b_ref, df_ref, db_ref):
        acc = dfq_ref[...] + dfk_ref[...]
        row = lax.broadcasted_iota(jnp.int32, (seq, 1), 0)
        dist = 1
        while dist < seq:
            acc = acc + _shift_up(acc, dist, row, seq)
            dist *= 2
        df = acc * _sigmoid(-(f_ref[...] + b_ref[...]))
        df_ref[...] = df
        db_ref[...] = jnp.sum(df, axis=0, keepdims=True)

    col = pl.BlockSpec((seq, LANES), lambda b: (b, 0))
    return pl.pallas_call(
        body,
        out_shape=[jax.ShapeDtypeStruct((n_batch * seq, LANES), F32), jax.ShapeDtypeStruct((n_batch, 1, LANES), F32)],
        grid=(n_batch,), in_specs=[col, col, col, pl.BlockSpec((1, LANES), lambda b: (0, 0))],
        out_specs=[col, pl.BlockSpec((None, 1, LANES), lambda b: (b, 0, 0))],
        compiler_params=_params(), name="forget_bwd",
    )(dfq, dfk, f, bias)


def _mix_out(x1, yp, o, ona, woa, wob):
    t, d = x1.shape
    width = o.shape[1]
    tm = min(512, t)

    def body(x_ref, yp_ref, o_ref, on_ref, wa_ref, wb_ref, x2_ref, ya_ref):
        of = o_ref[...]
        ya = ((of * _rms(of)) * on_ref[...]).astype(BF16)
        ya_ref[...] = ya
        x2_ref[...] = x_ref[...] + (_dot(yp_ref[...], wa_ref[...]) + _dot(ya, wb_ref[...]))

    row = pl.BlockSpec((tm, d), lambda i: (i, 0))
    half = pl.BlockSpec((tm, width), lambda i: (i, 0))
    wspec = pl.BlockSpec((width, d), lambda i: (0, 0))
    return pl.pallas_call(
        body, out_shape=[jax.ShapeDtypeStruct((t, d), F32), jax.ShapeDtypeStruct((t, width), BF16)],
        grid=(t // tm,), in_specs=[row, half, half, pl.BlockSpec((1, width), lambda i: (0, 0)), wspec, wspec],
        out_specs=[row, half], compiler_params=_params(), name="mix_out",
    )(x1, yp, o, ona, woa, wob)


def _mix_out_bwd(dx2, o, yp, ya, ona, woa, wob):
    t, d = dx2.shape
    width = o.shape[1]
    tm = min(512, t)
    nt = t // tm

    def body(dx_ref, o_ref, yp_ref, ya_ref, on_ref, wa_ref, wb_ref, dyp_ref, do_ref, dl_ref, dwa_ref, dwb_ref, don_ref):
        @pl.when(pl.program_id(0) == 0)
        def _():
            dwa_ref[...] = jnp.zeros_like(dwa_ref)
            dwb_ref[...] = jnp.zeros_like(dwb_ref)

        dxb = dx_ref[...].astype(BF16)
        dwa_ref[...] += _dot_tn(yp_ref[...], dxb)
        dwb_ref[...] += _dot_tn(ya_ref[...], dxb)
        dyp_ref[...] = _dot_nt(dxb, wa_ref[...])
        of = o_ref[...]
        dov, dgr = _rms_bwd(of, _rms(of), on_ref[...], _dot_nt(dxb, wb_ref[...]))
        don_ref[...] = jnp.sum(dgr, axis=0, keepdims=True)
        do_ref[...] = dov.astype(BF16)
        lo = _head_masks()
        prod = dov * of
        delta = jnp.zeros((tm, LANES), F32)
        for blk in range(width // LANES):
            pb = prod[:, blk * LANES:(blk + 1) * LANES]
            delta = delta + _put_lane(jnp.sum(jnp.where(lo, pb, 0.0), axis=1, keepdims=True), 2 * blk)
            delta = delta + _put_lane(jnp.sum(jnp.where(lo, 0.0, pb), axis=1, keepdims=True), 2 * blk + 1)
        dl_ref[...] = delta

    row = pl.BlockSpec((tm, d), lambda i: (i, 0))
    half = pl.BlockSpec((tm, width), lambda i: (i, 0))
    wspec = pl.BlockSpec((width, d), lambda i: (0, 0))
    return pl.pallas_call(
        body,
        out_shape=[jax.ShapeDtypeStruct((t, width), F32), jax.ShapeDtypeStruct((t, width), BF16),
                   jax.ShapeDtypeStruct((t, LANES), F32), jax.ShapeDtypeStruct((width, d), F32),
                   jax.ShapeDtypeStruct((width, d), F32), jax.ShapeDtypeStruct((nt, 1, width), F32)],
        grid=(nt,),
        in_specs=[row, half, half, half, pl.BlockSpec((1, width), lambda i: (0, 0)), wspec, wspec],
        out_specs=[half, half, pl.BlockSpec((tm, LANES), lambda i: (i, 0)), wspec, wspec,
                   pl.BlockSpec((None, 1, width), lambda i: (i, 0, 0))],
        compiler_params=_params(), name="mix_out_bwd",
    )(dx2, o, yp, ya, ona, woa, wob)


def _mix_in_bwd(dx2, x1, gain, hm, dpv, dqh, q, dkh, k, dv, df, qn, kn, wp, wf):
    t, d = x1.shape
    width = q.shape[1]
    pool_width = dpv.shape[1]
    tm = min(512, t)
    nt = t // tm
    scale = HEAD_DIM ** -0.5
    c_q, c_k, c_v = pool_width, pool_width + width, pool_width + 2 * width

    def body(dx2_ref, x_ref, g_ref, hm_ref, dpv_ref, dqh_ref, q_ref, dkh_ref, k_ref, dv_ref, df_ref, qn_ref, kn_ref,
             wp_ref, wf_ref, dx_ref, dwp_ref, dwf_ref, dg_ref, dqn_ref, dkn_ref):
        @pl.when(pl.program_id(0) == 0)
        def _():
            dwp_ref[...] = jnp.zeros_like(dwp_ref)
            dwf_ref[...] = jnp.zeros_like(dwf_ref)

        lo = _head_masks()
        hm = hm_ref[...]
        pieces = [(0, dpv_ref[...])]
        for c0, raw_ref, dh_ref, n_ref, dn_ref, mul in ((c_q, q_ref, dqh_ref, qn_ref, dqn_ref, scale),
                                                       (c_k, k_ref, dkh_ref, kn_ref, dkn_ref, 1.0)):
            cols = []
            for blk in range(width // LANES):
                sl = slice(blk * LANES, (blk + 1) * LANES)
                xb = raw_ref[:, sl]
                gb = dh_ref[:, sl] * mul
                r = _head_rms(xb, lo)
                xh = xb * r
                dyg = gb * n_ref[:, sl]
                cols.append((r * (dyg - xh * _head_mean(dyg * xh, lo))).astype(BF16))
                dn_ref[:, sl] = jnp.sum(gb * xh, axis=0, keepdims=True)
            pieces.append((c0, jnp.concatenate(cols, axis=1)))
        pieces.append((c_v, dv_ref[...]))
        dfb = df_ref[...].astype(BF16)
        dhm = _dot_nt(dfb, wf_ref[...])
        dwf_ref[...] += _dot_tn(hm, dfb)
        for c0, piece in pieces:
            dwp_ref[:, c0:c0 + piece.shape[1]] += _dot_tn(hm, piece)
            dhm = dhm + _dot_nt(piece, wp_ref[:, c0:c0 + piece.shape[1]])
        xf = x_ref[...]
        dxn, dgr = _rms_bwd(xf, _rms(xf), g_ref[...], dhm)
        dx_ref[...] = dx2_ref[...] + dxn
        dg_ref[...] = jnp.sum(dgr, axis=0, keepdims=True)

    row = pl.BlockSpec((tm, d), lambda i: (i, 0))
    half = pl.BlockSpec((tm, width), lambda i: (i, 0))
    const = lambda shape: pl.BlockSpec(shape, lambda i: (0, 0))
    pvec = lambda n: pl.BlockSpec((None, 1, n), lambda i: (i, 0, 0))
    return pl.pallas_call(
        body,
        out_shape=[jax.ShapeDtypeStruct((t, d), F32), jax.ShapeDtypeStruct(wp.shape, F32),
                   jax.ShapeDtypeStruct(wf.shape, F32), jax.ShapeDtypeStruct((nt, 1, d), F32),
                   jax.ShapeDtypeStruct((nt, 1, width), F32), jax.ShapeDtypeStruct((nt, 1, width), F32)],
        grid=(nt,),
        in_specs=[row, row, const((1, d)), row, pl.BlockSpec((tm, pool_width), lambda i: (i, 0)), half, half, half, half,
                  half, pl.BlockSpec((tm, LANES), lambda i: (i, 0)), const((1, width)), const((1, width)),
                  const(wp.shape), const(wf.shape)],
        out_specs=[row, const(wp.shape), const(wf.shape), pvec(d), pvec(width), pvec(width)],
        compiler_params=_params(), name="mix_in_bwd",
    )(dx2, x1, gain, hm, dpv, dqh, q, dkh, k, dv, df, qn, kn, wp, wf)


def _local_step(xf, tgt, small, full, n_batch, seq):
    pool_width = small["pool_scale"].shape[1]
    attn_width = small["out_norm_attn"].shape[1]
    x1, h1, a1, b1, s1 = _ffn_fwd(xf, small["ffn1_norm"], full["wg1"], full["wu1"], full["wd1"])
    hm, pv, q, k, qh, kh, vb, f = _mix_proj(x1, small["mix_norm"], full["wp"], full["wf"], small["qn"], small["kn"],
                                            pool_width, attn_width)
    fc, fr = _forget_prefix(f, small["b_forget"], n_batch, seq)
    yp = _pool_fwd(pv, full["pool_w"], small["pool_scale"], small["out_norm_pool"], n_batch, seq)
    o, lse = _attn_fwd(qh, kh, vb, fc, fr, n_batch, seq)
    x2, ya = _mix_out(x1, yp, o, small["out_norm_attn"], full["woa"], full["wob"])
    dy, h2, a2, b2, s2, lpart = _ffn_fwd(x2, small["ffn2_norm"], full["wg2"], full["wu2"], full["wd2"], target=tgt)

    dx2, da2, db2, dg2 = _ffn_bwd_x(dy, x2, small["ffn2_norm"], a2, b2, full["wg2"], full["wu2"], full["wd2"], "ffn2_bwd_x")
    dwg2, dwu2, dwd2 = _ffn_bwd_w(h2, s2, da2, db2, dy, "ffn2_bwd_w")
    dyp, do, delta, dwoa, dwob, dona = _mix_out_bwd(dx2, o, yp, ya, small["out_norm_attn"], full["woa"], full["wob"])
    dpv, dpw, dps, donp = _pool_bwd(pv, dyp, full["pool_w"], small["pool_scale"], small["out_norm_pool"], n_batch, seq)
    dqh, dfq = _attn_bwd_q(qh, kh, vb, do, lse, delta, fc, fr, n_batch, seq)
    dkh, dv, dfk = _attn_bwd_kv(qh, kh, vb, do, lse, delta, fc, fr, n_batch, seq)
    df, dbf = _forget_bwd(dfq, dfk, f, small["b_forget"], n_batch, seq)
    dx1, dwp, dwf, dgm, dqn, dkn = _mix_in_bwd(dx2, x1, small["mix_norm"], hm, dpv, dqh, q, dkh, k, dv, df,
                                               small["qn"], small["kn"], full["wp"], full["wf"])
    gx, da1, db1, dg1 = _ffn_bwd_x(dx1, xf, small["ffn1_norm"], a1, b1, full["wg1"], full["wu1"], full["wd1"], "ffn1_bwd_x")
    dwg1, dwu1, dwd1 = _ffn_bwd_w(h1, s1, da1, db1, dx1, "ffn1_bwd_w")
    big = dict(wg1=dwg1, wu1=dwu1, wd1=dwd1, wp=dwp, wf=dwf, woa=dwoa, wob=dwob, wg2=dwg2, wu2=dwu2, wd2=dwd2)
    part = dict(ffn1_norm=dg1, mix_norm=dgm, ffn2_norm=dg2, b_forget=dbf, pool_w=dpw, pool_scale=dps,
                out_norm_pool=donp, out_norm_attn=dona, qn=dqn, kn=dkn)
    return lpart, gx, big, part


def _mesh_pos():
    return lax.axis_index("x"), lax.axis_index("y"), lax.axis_index("c")


def _other_chips(x, y):
    return [(1 - x, y), (x, 1 - y), (1 - x, 1 - y)]


def _remote(src, dst, send_sem, recv_sem, device):
    return pltpu.make_async_remote_copy(src_ref=src, dst_ref=dst, send_sem=send_sem, recv_sem=recv_sem,
                                        device_id=device, device_id_type=pl.DeviceIdType.MESH)


def _half_rows(n_rows, which):
    half = n_rows // 2
    return pl.ds(pl.multiple_of(which * half, 8), half)


def _gather_weights(ws):
    n = len(ws)

    def body(*refs):
        ins, outs = refs[:n], refs[n:2 * n]
        ici_send, ici_recv, d2d_send, d2d_recv, loc_sem = refs[2 * n:]
        x, y, c = _mesh_pos()
        mine = 2 * x + y
        sibling = (x, y, 1 - c)
        chips = _other_chips(x, y)
        slots = [2 * cx + cy for cx, cy in chips]
        local = [pltpu.make_async_copy(ins[w], outs[w].at[mine], loc_sem.at[w]) for w in range(n)]
        for cp in local:
            cp.start()
        sends = []
        for w in range(n):
            rows = _half_rows(ws[w].shape[0], c)
            for j, chip in enumerate(chips):
                cp = _remote(ins[w].at[rows], outs[w].at[mine, rows], ici_send.at[w, j], ici_recv.at[w, j], (*chip, c))
                cp.start()
                sends.append(cp)
        for w in range(n):
            rows = _half_rows(ws[w].shape[0], c)
            for j in range(3):
                landed = outs[w].at[slots[j], rows]
                _remote(landed, landed, ici_send.at[w, j], ici_recv.at[w, j], sibling).wait_recv()
                cp = _remote(landed, landed, d2d_send.at[w, j], d2d_recv.at[w, j], sibling)
                cp.start()
                sends.append(cp)
        for w in range(n):
            rows = _half_rows(ws[w].shape[0], 1 - c)
            for j in range(3):
                landed = outs[w].at[slots[j], rows]
                _remote(landed, landed, d2d_send.at[w, j], d2d_recv.at[w, j], sibling).wait_recv()
        for cp in sends:
            cp.wait_send()
        for cp in local:
            cp.wait()

    return pl.pallas_call(
        body, out_shape=[jax.ShapeDtypeStruct((N_CHIPS,) + w.shape, w.dtype) for w in ws],
        in_specs=[ANY] * n, out_specs=[ANY] * n,
        scratch_shapes=[pltpu.SemaphoreType.DMA((n, 3)), pltpu.SemaphoreType.DMA((n, 3)),
                        pltpu.SemaphoreType.DMA((n, 3)), pltpu.SemaphoreType.DMA((n, 3)),
                        pltpu.SemaphoreType.DMA((n,))],
        name="gather_weights",
    )(*ws)


def _sibling_halves(gs):
    n = len(gs)

    def body(*refs):
        ins, outs = refs[:n], refs[n:2 * n]
        send, recv = refs[2 * n:]
        x, y, c = _mesh_pos()
        cps = []
        for w in range(n):
            cp = _remote(ins[w].at[:, _half_rows(gs[w].shape[1], 1 - c), :], outs[w], send.at[w], recv.at[w], (x, y, 1 - c))
            cp.start()
            cps.append(cp)
        for cp in cps:
            cp.wait()

    return pl.pallas_call(
        body, out_shape=[jax.ShapeDtypeStruct((g.shape[0], g.shape[1] // 2, g.shape[2]), g.dtype) for g in gs],
        in_specs=[ANY] * n, out_specs=[ANY] * n,
        scratch_shapes=[pltpu.SemaphoreType.DMA((n,)), pltpu.SemaphoreType.DMA((n,))],
        name="sibling_halves",
    )(*gs)


def _chip_exchange(ps):
    n = len(ps)

    def body(*refs):
        ins, outs = refs[:n], refs[n:2 * n]
        send, recv = refs[2 * n:]
        x, y, c = _mesh_pos()
        chips = _other_chips(x, y)
        cps = []
        for w in range(n):
            for j, (cx, cy) in enumerate(chips):
                cp = _remote(ins[w].at[2 * cx + cy], outs[w].at[j], send.at[w, j], recv.at[w, j], (cx, cy, c))
                cp.start()
                cps.append(cp)
        for cp in cps:
            cp.wait()

    return pl.pallas_call(
        body, out_shape=[jax.ShapeDtypeStruct((3,) + p.shape[1:], p.dtype) for p in ps],
        in_specs=[ANY] * n, out_specs=[ANY] * n,
        scratch_shapes=[pltpu.SemaphoreType.DMA((n, 3)), pltpu.SemaphoreType.DMA((n, 3))],
        name="chip_exchange",
    )(*ps)


def _sibling_share(hs):
    n = len(hs)

    def body(*refs):
        ins, outs = refs[:n], refs[n:2 * n]
        send, recv, loc_sem = refs[2 * n:]
        x, y, c = _mesh_pos()
        cps = []
        for w in range(n):
            rows = _half_rows(2 * hs[w].shape[0], c)
            lc = pltpu.make_async_copy(ins[w], outs[w].at[rows], loc_sem.at[w])
            lc.start()
            cp = _remote(ins[w], outs[w].at[rows], send.at[w], recv.at[w], (x, y, 1 - c))
            cp.start()
            cps.append((lc, cp))
        for lc, cp in cps:
            lc.wait()
            cp.wait()

    return pl.pallas_call(
        body, out_shape=[jax.ShapeDtypeStruct((2 * h.shape[0], h.shape[1]), h.dtype) for h in hs],
        in_specs=[ANY] * n, out_specs=[ANY] * n,
        scratch_shapes=[pltpu.SemaphoreType.DMA((n,)), pltpu.SemaphoreType.DMA((n,)), pltpu.SemaphoreType.DMA((n,))],
        name="sibling_share",
    )(*hs)


def _add_sibling(g, r1, core, tag):
    nch, rh, cols = r1.shape

    def body(c_ref, g_ref, r_ref, o_ref):
        o_ref[...] = g_ref[...] + r_ref[...]

    blk = lambda fn: pl.BlockSpec((None, rh, cols), fn)
    return pl.pallas_call(
        body, out_shape=jax.ShapeDtypeStruct(r1.shape, F32),
        grid_spec=pltpu.PrefetchScalarGridSpec(
            num_scalar_prefetch=1, grid=(nch,),
            in_specs=[blk(lambda k, c: (k, c[0], 0)), blk(lambda k, c: (k, 0, 0))],
            out_specs=blk(lambda k, c: (k, 0, 0))),
        compiler_params=_params(), name="add_sibling_" + tag,
    )(core, g, r1)


def _add_chips(p, r2, chip, tag):
    _, rh, cols = p.shape

    def body(k_ref, p_ref, r_ref, o_ref):
        o_ref[...] = ((p_ref[...] + r_ref[0]) + r_ref[1]) + r_ref[2]

    return pl.pallas_call(
        body, out_shape=jax.ShapeDtypeStruct((rh, cols), F32),
        grid_spec=pltpu.PrefetchScalarGridSpec(
            num_scalar_prefetch=1, grid=(1,),
            in_specs=[pl.BlockSpec((None, rh, cols), lambda i, k: (k[0], 0, 0)),
                      pl.BlockSpec((3, rh, cols), lambda i, k: (0, 0, 0))],
            out_specs=pl.BlockSpec((rh, cols), lambda i, k: (0, 0))),
        compiler_params=_params(), name="add_chips_" + tag,
    )(chip, p, r2)


def _reduce_to_owner(gs, tags):
    x, y, c = _mesh_pos()
    core = jnp.reshape(c, (1,)).astype(jnp.int32)
    chip = jnp.reshape(2 * x + y, (1,)).astype(jnp.int32)
    r1 = _sibling_halves(gs)
    ps = [_add_sibling(g, r, core, t) for g, r, t in zip(gs, r1, tags)]
    r2 = _chip_exchange(ps)
    hs = [_add_chips(p, r, chip, t) for p, r, t in zip(ps, r2, tags)]
    return _sibling_share(hs)


VEC_ROWS = 8


def _small_allreduce(part, d, width):
    names = ("ffn1_norm", "mix_norm", "ffn2_norm", "pool_scale", "out_norm_pool", "out_norm_attn", "qn", "kn", "b_forget", "pool_w")
    args = [part[k] for k in names]
    pw_shape = part["pool_w"].shape[1:]
    n_dev = 8

    def body(g1_ref, gm_ref, g2_ref, ps_ref, onp_ref, ona_ref, qn_ref, kn_ref, bf_ref, pw_ref,
             vec_ref, pwo_ref, vbuf, pbuf, send, recv):
        x, y, c = _mesh_pos()
        me = 4 * x + 2 * y + c
        lo = _head_masks()

        def fold_heads(ref):
            v = jnp.sum(ref[...], axis=0)
            acc = jnp.zeros((VEC_ROWS, LANES), F32)
            for blk in range(width // LANES):
                vb = jnp.broadcast_to(v[:, blk * LANES:(blk + 1) * LANES], (VEC_ROWS, LANES))
                acc = acc + vb + pltpu.roll(vb, HEAD_DIM, 1)
            return jnp.where(lo, acc, 0.0)[0:1, :]

        vbuf[0] = jnp.zeros((VEC_ROWS, d), F32)
        vbuf[0, 0:1, :] = jnp.sum(g1_ref[...], axis=0)
        vbuf[0, 1:2, :] = jnp.sum(gm_ref[...], axis=0)
        vbuf[0, 2:3, :] = jnp.sum(g2_ref[...], axis=0)
        vbuf[0, 3:4, 0:width] = jnp.sum(ps_ref[...], axis=0)
        vbuf[0, 3:4, width:2 * width] = jnp.sum(onp_ref[...], axis=0)
        vbuf[0, 4:5, 0:width] = jnp.sum(ona_ref[...], axis=0)
        vbuf[0, 4:5, width:width + LANES] = fold_heads(qn_ref)
        vbuf[0, 4:5, width + LANES:width + 2 * LANES] = fold_heads(kn_ref)
        vbuf[0, 4:5, width + 2 * LANES:width + 3 * LANES] = jnp.sum(bf_ref[...], axis=0)
        pbuf[0] = jnp.sum(pw_ref[...], axis=0)

        cps = []
        for r in range(1, n_dev):
            peer = (x if not r & 4 else 1 - x, y if not r & 2 else 1 - y, c if not r & 1 else 1 - c)
            for buf, k in ((vbuf, 0), (pbuf, 1)):
                cp = _remote(buf.at[0], buf.at[r], send.at[k, r - 1], recv.at[k, r - 1], peer)
                cp.start()
                cps.append(cp)
        for cp in cps:
            cp.wait()
        vec = vbuf[me]
        pw = pbuf[me]
        for dev in range(1, n_dev):
            vec = vec + vbuf[jnp.bitwise_xor(me, dev)]
            pw = pw + pbuf[jnp.bitwise_xor(me, dev)]
        vec_ref[...] = vec
        pwo_ref[...] = pw

    return pl.pallas_call(
        body, out_shape=[jax.ShapeDtypeStruct((VEC_ROWS, d), F32), jax.ShapeDtypeStruct(pw_shape, F32)],
        in_specs=[VM] * len(args), out_specs=[VM, VM],
        scratch_shapes=[pltpu.VMEM((n_dev, VEC_ROWS, d), F32), pltpu.VMEM((n_dev,) + pw_shape, F32),
                        pltpu.SemaphoreType.DMA((2, n_dev - 1)), pltpu.SemaphoreType.DMA((2, n_dev - 1))],
        compiler_params=_params(), name="small_allreduce",
    )(*args)


def _adamw(w, g, m, v, tag):
    rows, cols = w.shape
    rb = rows
    while rb * cols * 4 > (1 << 20) and rb % 16 == 0:
        rb //= 2

    def body(w_ref, g_ref, m_ref, v_ref, d_ref, mo_ref, vo_ref):
        gv = g_ref[...]
        m2 = ADAM_B1 * m_ref[...] + (1.0 - ADAM_B1) * gv
        v2 = ADAM_B2 * v_ref[...] + (1.0 - ADAM_B2) * (gv * gv)
        m_hat = m2 / (1.0 - ADAM_B1 ** ADAM_STEP)
        v_hat = v2 / (1.0 - ADAM_B2 ** ADAM_STEP)
        d_ref[...] = -ADAM_LR * (m_hat / (jnp.sqrt(v_hat) + ADAM_EPS) + ADAM_WD * w_ref[...])
        mo_ref[...] = m2
        vo_ref[...] = v2

    spec = pl.BlockSpec((rb, cols), lambda i: (i, 0))
    return pl.pallas_call(
        body, out_shape=[jax.ShapeDtypeStruct(w.shape, F32)] * 3, grid=(rows // rb,),
        in_specs=[spec] * 4, out_specs=[spec] * 3, compiler_params=_params(), name="adamw_" + tag,
    )(w, g, m, v)


def _pack_vec(p, d, width):
    pad = lambda v: jnp.pad(v, (0, LANES - v.shape[0]))
    row3 = jnp.concatenate([p["pool_scale"], p["out_norm_pool"]])
    row4 = jnp.concatenate([p["out_norm_attn"], pad(p["q_norm"]), pad(p["k_norm"]), pad(p["b_forget"]),
                            jnp.zeros((d - width - 3 * LANES,), F32)])
    rows = [p["ffn1_norm"], p["mix_norm"], p["ffn2_norm"], row3, row4]
    return jnp.pad(jnp.stack(rows), ((0, VEC_ROWS - len(rows)), (0, 0)))


def _unpack_vec(vec, width):
    return dict(ffn1_norm=vec[0], mix_norm=vec[1], ffn2_norm=vec[2], pool_scale=vec[3, :width],
                out_norm_pool=vec[3, width:2 * width], out_norm_attn=vec[4, :width],
                q_norm=vec[4, width:width + HEAD_DIM], k_norm=vec[4, width + LANES:width + LANES + HEAD_DIM],
                b_forget=vec[4, width + 2 * LANES:width + 2 * LANES + N_HEADS])


WEIGHT_NAMES = ("ffn1_norm", "ffn1_w_gate", "ffn1_w_up", "ffn1_w_down", "mix_norm", "w_in", "b_forget", "pool_w",
                "pool_scale", "q_norm", "k_norm", "out_norm_pool", "out_norm_attn", "w_out", "ffn2_norm",
                "ffn2_w_gate", "ffn2_w_up", "ffn2_w_down")
BIG_NAMES = ("ffn1_w_gate", "ffn1_w_up", "ffn1_w_down", "w_in", "w_out", "ffn2_w_gate", "ffn2_w_up", "ffn2_w_down")


def kernel(x, ffn1_norm, ffn1_w_gate, ffn1_w_up, ffn1_w_down, mix_norm, w_in, b_forget, pool_w, pool_scale, q_norm, k_norm, out_norm_pool, out_norm_attn, w_out, ffn2_norm, ffn2_w_gate, ffn2_w_up, ffn2_w_down, loss_target, m_ffn1_norm, m_ffn1_w_gate, m_ffn1_w_up, m_ffn1_w_down, m_mix_norm, m_w_in, m_b_forget, m_pool_w, m_pool_scale, m_q_norm, m_k_norm, m_out_norm_pool, m_out_norm_attn, m_w_out, m_ffn2_norm, m_ffn2_w_gate, m_ffn2_w_up, m_ffn2_w_down, v_ffn1_norm, v_ffn1_w_gate, v_ffn1_w_up, v_ffn1_w_down, v_mix_norm, v_w_in, v_b_forget, v_pool_w, v_pool_scale, v_q_norm, v_k_norm, v_out_norm_pool, v_out_norm_attn, v_w_out, v_ffn2_norm, v_ffn2_w_gate, v_ffn2_w_up, v_ffn2_w_down):
    given = dict(locals())
    w = {n: given[n] for n in WEIGHT_NAMES}
    m = {n: given["m_" + n] for n in WEIGHT_NAMES}
    v = {n: given["v_" + n] for n in WEIGHT_NAMES}
    n_batch, seq, d = x.shape
    width = pool_scale.shape[0]
    in_cols = N_CHIPS * w_in.shape[1]
    proj_cols = in_cols - N_HEADS

    gathered = dict(zip(BIG_NAMES, _gather_weights([w[n].astype(BF16) for n in BIG_NAMES])))
    w_in_full = jnp.transpose(gathered["w_in"], (1, 0, 2)).reshape(d, in_cols)
    w_out_full = gathered["w_out"].reshape(N_CHIPS * w_out.shape[0], d)
    full = dict(wg1=gathered["ffn1_w_gate"], wu1=gathered["ffn1_w_up"], wd1=gathered["ffn1_w_down"],
                wg2=gathered["ffn2_w_gate"], wu2=gathered["ffn2_w_up"], wd2=gathered["ffn2_w_down"],
                wp=w_in_full[:, :proj_cols], wf=jnp.pad(w_in_full[:, proj_cols:], ((0, 0), (0, LANES - N_HEADS))),
                woa=w_out_full[:width], wob=w_out_full[width:], pool_w=pool_w.astype(BF16))
    row = lambda a: a.reshape(1, -1)
    small = dict(ffn1_norm=row(ffn1_norm), mix_norm=row(mix_norm), ffn2_norm=row(ffn2_norm), pool_scale=row(pool_scale),
                 out_norm_pool=row(out_norm_pool), out_norm_attn=row(out_norm_attn),
                 qn=row(jnp.tile(q_norm, N_HEADS)), kn=row(jnp.tile(k_norm, N_HEADS)),
                 b_forget=row(jnp.pad(b_forget, (0, LANES - N_HEADS))))

    lpart, gx, big, part = _local_step(x.reshape(n_batch * seq, d), loss_target.reshape(n_batch * seq, d),
                                       small, full, n_batch, seq)
    loss = lax.psum(jnp.sum(lpart[:, 0, 0]), MESH_AXES)

    d_w_in = jnp.concatenate([big["wp"], big["wf"][:, :N_HEADS]], axis=1)
    d_w_in = jnp.transpose(d_w_in.reshape(d, N_CHIPS, in_cols // N_CHIPS), (1, 0, 2))
    d_w_out = jnp.concatenate([big["woa"], big["wob"]], axis=0).reshape(N_CHIPS, w_out.shape[0], d)
    stacks = dict(ffn1_w_gate=big["wg1"], ffn1_w_up=big["wu1"], ffn1_w_down=big["wd1"], w_in=d_w_in, w_out=d_w_out,
                  ffn2_w_gate=big["wg2"], ffn2_w_up=big["wu2"], ffn2_w_down=big["wd2"])
    grads = dict(zip(BIG_NAMES, _reduce_to_owner([stacks[n] for n in BIG_NAMES], BIG_NAMES)))

    part = dict(part, pool_w=part["pool_w"].reshape(n_batch, -1, pool_w.shape[-1]))
    g_vec, g_pw = _small_allreduce(part, d, width)
    delta, new_m, new_v = {}, {}, {}
    for n in BIG_NAMES:
        delta[n], new_m[n], new_v[n] = _adamw(w[n], grads[n], m[n], v[n], n)
    flat_pw = lambda a: a.reshape(-1, a.shape[-1])
    d_pw, m_pw, v_pw = _adamw(flat_pw(pool_w), g_pw, flat_pw(m_pool_w), flat_pw(v_pool_w), "pool_w")
    d_vec, m_vec, v_vec = _adamw(_pack_vec(w, d, width), g_vec, _pack_vec(m, d, width), _pack_vec(v, d, width), "vectors")
    grads.update(_unpack_vec(g_vec, width), pool_w=g_pw.reshape(pool_w.shape))
    delta.update(_unpack_vec(d_vec, width), pool_w=d_pw.reshape(pool_w.shape))
    new_m.update(_unpack_vec(m_vec, width), pool_w=m_pw.reshape(pool_w.shape))
    new_v.update(_unpack_vec(v_vec, width), pool_w=v_pw.reshape(pool_w.shape))
    return (loss, gx.reshape(x.shape), *[grads[n] for n in WEIGHT_NAMES], *[delta[n] for n in WEIGHT_NAMES],
            *[new_m[n] for n in WEIGHT_NAMES], *[new_v[n] for n in WEIGHT_NAMES])
```

```python
import functools

import jax
import jax.numpy as jnp
from jax import lax
from jax.experimental import pallas as pl
from jax.experimental.pallas import tpu as pltpu

F32 = jnp.float32
BF16 = jnp.bfloat16
EPS = 1e-6
NEG = -1e30
ADAM_LR = 0.001
ADAM_B1 = 0.9
ADAM_B2 = 0.999
ADAM_EPS = 1e-08
ADAM_WD = 0.01
ADAM_STEP = 10
POOL_WINDOWS = (2, 4, 8, 16)
HEAD_DIM = 64
N_HEADS = 8
LANES = 128
N_CHIPS = 4
ATT_BLOCK = 256
VMEM_LIMIT = 56 * 1024 * 1024
MESH_AXES = ("x", "y", "c")
ANY = pl.BlockSpec(memory_space=pl.ANY)
VM = pl.BlockSpec(memory_space=pltpu.VMEM)


def _params(**kw):
    return pltpu.CompilerParams(vmem_limit_bytes=VMEM_LIMIT, **kw)


def _dot(a, b):
    return jnp.dot(a, b, preferred_element_type=F32)


def _dot_nt(a, b):
    return lax.dot_general(a, b, (((1,), (1,)), ((), ())), preferred_element_type=F32)


def _dot_tn(a, b):
    return lax.dot_general(a, b, (((0,), (0,)), ((), ())), preferred_element_type=F32)


def _sigmoid(z):
    return 1.0 / (1.0 + jnp.exp(-z))


def _rms(xf):
    return lax.rsqrt(jnp.mean(xf * xf, axis=-1, keepdims=True) + EPS)


def _rms_bwd(xf, r, gain, dh):
    xh = xf * r
    dyg = dh * gain
    return r * (dyg - xh * jnp.mean(dyg * xh, axis=-1, keepdims=True)), dh * xh


def _total(v):
    return jnp.sum(jnp.sum(v, axis=1, keepdims=True), axis=0, keepdims=True)


def _ffn_fwd(x, gain, wg, wu, wd, target=None):
    t, d = x.shape
    nch, _, fc = wg.shape
    tm = min(512, t)
    nt = t // tm
    with_loss = target is not None

    def body(*refs):
        if with_loss:
            x_ref, g_ref, wg_ref, wu_ref, wd_ref, t_ref, o_ref, h_ref, a_ref, b_ref, s_ref, l_ref, acc_ref = refs
        else:
            x_ref, g_ref, wg_ref, wu_ref, wd_ref, o_ref, h_ref, a_ref, b_ref, s_ref, acc_ref = refs
        k = pl.program_id(1)

        @pl.when(k == 0)
        def _():
            xf = x_ref[...]
            h_ref[...] = ((xf * _rms(xf)) * g_ref[...]).astype(BF16)
            acc_ref[...] = jnp.zeros_like(acc_ref)

        h = h_ref[...]
        a = _dot(h, wg_ref[...])
        b = _dot(h, wu_ref[...])
        sb = ((a * _sigmoid(a)) * b).astype(BF16)
        a_ref[...] = a.astype(BF16)
        b_ref[...] = b.astype(BF16)
        s_ref[...] = sb
        acc_ref[...] += _dot(sb, wd_ref[...])

        @pl.when(k == nch - 1)
        def _():
            y = x_ref[...] + 0.5 * acc_ref[...]
            if with_loss:
                e = y - t_ref[...]
                o_ref[...] = e * (1.0 / d)
                l_ref[...] = jnp.broadcast_to(_total(e * e) * (0.5 / d), l_ref.shape)
            else:
                o_ref[...] = y

    row = pl.BlockSpec((tm, d), lambda i, k: (i, 0))
    chunk_in = pl.BlockSpec((None, d, fc), lambda i, k: (k, 0, 0))
    chunk_out = pl.BlockSpec((None, fc, d), lambda i, k: (k, 0, 0))
    act = pl.BlockSpec((None, tm, fc), lambda i, k: (k, i, 0))
    in_specs = [row, pl.BlockSpec((1, d), lambda i, k: (0, 0)), chunk_in, chunk_in, chunk_out]
    out_shape = [jax.ShapeDtypeStruct((t, d), F32), jax.ShapeDtypeStruct((t, d), BF16)]
    out_shape += [jax.ShapeDtypeStruct((nch, t, fc), BF16)] * 3
    out_specs = [row, row, act, act, act]
    args = [x, gain, wg, wu, wd]
    if with_loss:
        in_specs.append(row)
        args.append(target)
        out_shape.append(jax.ShapeDtypeStruct((nt, 8, LANES), F32))
        out_specs.append(pl.BlockSpec((None, 8, LANES), lambda i, k: (i, 0, 0)))
    return pl.pallas_call(
        body, out_shape=out_shape, grid=(nt, nch), in_specs=in_specs, out_specs=out_specs,
        scratch_shapes=[pltpu.VMEM((tm, d), F32)], compiler_params=_params(),
        name="ffn_fwd_loss" if with_loss else "ffn_fwd",
    )(*args)


def _ffn_bwd_x(dy, x, gain, a, b, wg, wu, wd, name):
    t, d = x.shape
    nch, _, fc = wg.shape
    tm = min(512, t)
    nt = t // tm

    def body(dy_ref, x_ref, g_ref, a_ref, b_ref, wg_ref, wu_ref, wd_ref, dx_ref, da_ref, db_ref, dg_ref, acc_ref):
        k = pl.program_id(1)

        @pl.when(k == 0)
        def _():
            acc_ref[...] = jnp.zeros_like(acc_ref)

        ds = 0.5 * _dot_nt(dy_ref[...].astype(BF16), wd_ref[...])
        av = a_ref[...].astype(F32)
        bv = b_ref[...].astype(F32)
        sig = _sigmoid(av)
        dab = (ds * bv * (sig * (1.0 + av * (1.0 - sig)))).astype(BF16)
        dbb = (ds * (av * sig)).astype(BF16)
        da_ref[...] = dab
        db_ref[...] = dbb
        acc_ref[...] += _dot_nt(dab, wg_ref[...]) + _dot_nt(dbb, wu_ref[...])

        @pl.when(k == nch - 1)
        def _():
            xf = x_ref[...]
            dxn, dgr = _rms_bwd(xf, _rms(xf), g_ref[...], acc_ref[...])
            dx_ref[...] = dy_ref[...] + dxn
            dg_ref[...] = jnp.sum(dgr, axis=0, keepdims=True)

    row = pl.BlockSpec((tm, d), lambda i, k: (i, 0))
    chunk_in = pl.BlockSpec((None, d, fc), lambda i, k: (k, 0, 0))
    chunk_out = pl.BlockSpec((None, fc, d), lambda i, k: (k, 0, 0))
    act = pl.BlockSpec((None, tm, fc), lambda i, k: (k, i, 0))
    return pl.pallas_call(
        body,
        out_shape=[jax.ShapeDtypeStruct((t, d), F32), jax.ShapeDtypeStruct((nch, t, fc), BF16),
                   jax.ShapeDtypeStruct((nch, t, fc), BF16), jax.ShapeDtypeStruct((nt, 1, d), F32)],
        grid=(nt, nch),
        in_specs=[row, row, pl.BlockSpec((1, d), lambda i, k: (0, 0)), act, act, chunk_in, chunk_in, chunk_out],
        out_specs=[row, act, act, pl.BlockSpec((None, 1, d), lambda i, k: (i, 0, 0))],
        scratch_shapes=[pltpu.VMEM((tm, d), F32)], compiler_params=_params(), name=name,
    )(dy, x, gain, a, b, wg, wu, wd)


def _ffn_bwd_w(h, s, da, db, dy, name):
    t, d = h.shape
    nch, _, fc = s.shape
    tm = min(512, t)
    nt = t // tm

    def body(h_ref, s_ref, da_ref, db_ref, dy_ref, dwg_ref, dwu_ref, dwd_ref):
        @pl.when(pl.program_id(1) == 0)
        def _():
            dwg_ref[...] = jnp.zeros_like(dwg_ref)
            dwu_ref[...] = jnp.zeros_like(dwu_ref)
            dwd_ref[...] = jnp.zeros_like(dwd_ref)

        hv = h_ref[...]
        dwg_ref[...] += _dot_tn(hv, da_ref[...])
        dwu_ref[...] += _dot_tn(hv, db_ref[...])
        dwd_ref[...] += _dot_tn(s_ref[...], (0.5 * dy_ref[...]).astype(BF16))

    row = pl.BlockSpec((tm, d), lambda k, i: (i, 0))
    act = pl.BlockSpec((None, tm, fc), lambda k, i: (k, i, 0))
    w_in = pl.BlockSpec((None, d, fc), lambda k, i: (k, 0, 0))
    w_out = pl.BlockSpec((None, fc, d), lambda k, i: (k, 0, 0))
    return pl.pallas_call(
        body,
        out_shape=[jax.ShapeDtypeStruct((nch, d, fc), F32), jax.ShapeDtypeStruct((nch, d, fc), F32),
                   jax.ShapeDtypeStruct((nch, fc, d), F32)],
        grid=(nch, nt), in_specs=[row, act, act, act, row], out_specs=[w_in, w_in, w_out],
        compiler_params=_params(), name=name,
    )(h, s, da, db, dy)


def _head_masks():
    lane = lax.broadcasted_iota(jnp.int32, (1, LANES), 1)
    return lane < HEAD_DIM


def _head_rms(x, lo):
    x2 = x * x
    s0 = jnp.sum(jnp.where(lo, x2, 0.0), axis=1, keepdims=True)
    s1 = jnp.sum(jnp.where(lo, 0.0, x2), axis=1, keepdims=True)
    return jnp.where(lo, lax.rsqrt(s0 * (1.0 / HEAD_DIM) + EPS), lax.rsqrt(s1 * (1.0 / HEAD_DIM) + EPS))


def _head_mean(v, lo):
    s0 = jnp.sum(jnp.where(lo, v, 0.0), axis=1, keepdims=True)
    s1 = jnp.sum(jnp.where(lo, 0.0, v), axis=1, keepdims=True)
    return jnp.where(lo, s0, s1) * (1.0 / HEAD_DIM)


def _mix_proj(x1, gain, wp, wf, qn, kn, pool_width, attn_width):
    t, d = x1.shape
    tm = min(512, t)
    nt = t // tm
    scale = HEAD_DIM ** -0.5
    c_q, c_k, c_v = pool_width, pool_width + attn_width, pool_width + 2 * attn_width

    def body(x_ref, g_ref, wp_ref, wf_ref, qn_ref, kn_ref, hm_ref, pv_ref, q_ref, k_ref, qh_ref, kh_ref, vb_ref, f_ref):
        xf = x_ref[...]
        hm = ((xf * _rms(xf)) * g_ref[...]).astype(BF16)
        hm_ref[...] = hm
        f_ref[...] = _dot(hm, wf_ref[...])
        pv_ref[...] = _dot(hm, wp_ref[:, 0:pool_width])
        vb_ref[...] = _dot(hm, wp_ref[:, c_v:c_v + attn_width]).astype(BF16)
        lo = _head_masks()
        for c0, raw_ref, hat_ref, n_ref, mul in ((c_q, q_ref, qh_ref, qn_ref, scale), (c_k, k_ref, kh_ref, kn_ref, 1.0)):
            raw = _dot(hm, wp_ref[:, c0:c0 + attn_width])
            raw_ref[...] = raw
            for blk in range(attn_width // LANES):
                sl = slice(blk * LANES, (blk + 1) * LANES)
                xb = raw[:, sl]
                hat_ref[:, sl] = (((xb * _head_rms(xb, lo)) * n_ref[:, sl]) * mul).astype(BF16)

    row = pl.BlockSpec((tm, d), lambda i: (i, 0))
    half = pl.BlockSpec((tm, attn_width), lambda i: (i, 0))
    const = lambda shape: pl.BlockSpec(shape, lambda i: (0, 0))
    return pl.pallas_call(
        body,
        out_shape=[jax.ShapeDtypeStruct((t, d), BF16), jax.ShapeDtypeStruct((t, pool_width), F32),
                   jax.ShapeDtypeStruct((t, attn_width), F32), jax.ShapeDtypeStruct((t, attn_width), F32),
                   jax.ShapeDtypeStruct((t, attn_width), BF16), jax.ShapeDtypeStruct((t, attn_width), BF16),
                   jax.ShapeDtypeStruct((t, attn_width), BF16), jax.ShapeDtypeStruct((t, LANES), F32)],
        grid=(nt,),
        in_specs=[row, const((1, d)), const(wp.shape), const(wf.shape), const((1, attn_width)), const((1, attn_width))],
        out_specs=[row, pl.BlockSpec((tm, pool_width), lambda i: (i, 0)), half, half, half, half, half,
                   pl.BlockSpec((tm, LANES), lambda i: (i, 0))],
        compiler_params=_params(), name="mix_proj",
    )(x1, gain, wp, wf, qn, kn)


def _shift_down(v, dist, row):
    return jnp.where(row >= dist, pltpu.roll(v, dist, 0), 0.0)


def _shift_up(v, dist, row, n):
    return jnp.where(row + dist < n, pltpu.roll(v, n - dist, 0), 0.0)


def _forget_prefix(f, bias, n_batch, seq):
    blk = min(ATT_BLOCK, seq)

    def body(f_ref, b_ref, fc_ref, fr_ref):
        z = f_ref[...] + b_ref[...]
        acc = jnp.minimum(z, 0.0) - jnp.log(1.0 + jnp.exp(-jnp.abs(z)))
        row = lax.broadcasted_iota(jnp.int32, (seq, 1), 0)
        dist = 1
        while dist < seq:
            acc = acc + _shift_down(acc, dist, row)
            dist *= 2
        fc_ref[...] = acc
        for c in range(seq // LANES):
            jb, off = divmod(c * LANES, blk)
            fr_ref[jb, :, off:off + LANES] = acc[c * LANES:(c + 1) * LANES, :].T[0:8, :]

    return pl.pallas_call(
        body,
        out_shape=[jax.ShapeDtypeStruct((n_batch * seq, LANES), F32),
                   jax.ShapeDtypeStruct((n_batch, seq // blk, 8, blk), F32)],
        grid=(n_batch,),
        in_specs=[pl.BlockSpec((seq, LANES), lambda b: (b, 0)), pl.BlockSpec((1, LANES), lambda b: (0, 0))],
        out_specs=[pl.BlockSpec((seq, LANES), lambda b: (b, 0)),
                   pl.BlockSpec((None, seq // blk, 8, blk), lambda b: (b, 0, 0, 0))],
        compiler_params=_params(), name="forget_prefix",
    )(f, bias)


def _pool_groups(pv_ref, pw_ref, ps_ref, seq):
    row = lax.broadcasted_iota(jnp.int32, (seq, 1), 0)
    pos = (row + 1).astype(F32)
    out = []
    for g, win in enumerate(POOL_WINDOWS):
        sl = slice(g * LANES, (g + 1) * LANES)
        xg = pv_ref[:, sl]
        acc = xg
        dist = 1
        while dist < win:
            acc = acc + _shift_down(acc, dist, row)
            dist *= 2
        pooled = (acc / jnp.minimum(pos, float(win)) - xg).astype(BF16)
        mixed = _dot(pooled, pw_ref[g])
        out.append((pooled, mixed, mixed * ps_ref[:, sl]))
    return out


def _pool_fwd(pv, pw, ps, onp, n_batch, seq):
    width = pv.shape[1]

    def body(pv_ref, pw_ref, ps_ref, on_ref, y_ref):
        groups = _pool_groups(pv_ref, pw_ref, ps_ref, seq)
        ssq = sum(jnp.sum(ms * ms, axis=1, keepdims=True) for _, _, ms in groups)
        r = lax.rsqrt(ssq * (1.0 / width) + EPS)
        for g, (_, _, ms) in enumerate(groups):
            sl = slice(g * LANES, (g + 1) * LANES)
            y_ref[:, sl] = ((ms * r) * on_ref[:, sl]).astype(BF16)

    return pl.pallas_call(
        body, out_shape=jax.ShapeDtypeStruct((n_batch * seq, width), BF16), grid=(n_batch,),
        in_specs=[pl.BlockSpec((seq, width), lambda b: (b, 0)), pl.BlockSpec(pw.shape, lambda b: (0, 0, 0)),
                  pl.BlockSpec((1, width), lambda b: (0, 0)), pl.BlockSpec((1, width), lambda b: (0, 0))],
        out_specs=pl.BlockSpec((seq, width), lambda b: (b, 0)),
        compiler_params=_params(), name="pool_fwd",
    )(pv, pw, ps, onp)


def _pool_bwd(pv, dyp, pw, ps, onp, n_batch, seq):
    width = pv.shape[1]

    def body(pv_ref, dy_ref, pw_ref, ps_ref, on_ref, dpv_ref, dpw_ref, dps_ref, don_ref):
        groups = _pool_groups(pv_ref, pw_ref, ps_ref, seq)
        ssq = sum(jnp.sum(ms * ms, axis=1, keepdims=True) for _, _, ms in groups)
        r = lax.rsqrt(ssq * (1.0 / width) + EPS)
        mean = sum(jnp.sum((dy_ref[:, g * LANES:(g + 1) * LANES] * on_ref[:, g * LANES:(g + 1) * LANES]) * (ms * r),
                           axis=1, keepdims=True) for g, (_, _, ms) in enumerate(groups)) * (1.0 / width)
        row = lax.broadcasted_iota(jnp.int32, (seq, 1), 0)
        pos = (row + 1).astype(F32)
        for g, (pooled, mixed, ms) in enumerate(groups):
            sl = slice(g * LANES, (g + 1) * LANES)
            dy = dy_ref[:, sl]
            xh = ms * r
            don_ref[:, sl] = jnp.sum(dy * xh, axis=0, keepdims=True)
            dms = r * (dy * on_ref[:, sl] - xh * mean)
            dps_ref[:, sl] = jnp.sum(dms * mixed, axis=0, keepdims=True)
            dmix = (dms * ps_ref[:, sl]).astype(BF16)
            dpw_ref[g] = _dot_tn(pooled, dmix)
            dpool = _dot_nt(dmix, pw_ref[g])
            win = POOL_WINDOWS[g]
            acc = dpool / jnp.minimum(pos, float(win))
            dist = 1
            while dist < win:
                acc = acc + _shift_up(acc, dist, row, seq)
                dist *= 2
            dpv_ref[:, sl] = (acc - dpool).astype(BF16)

    tok = pl.BlockSpec((seq, width), lambda b: (b, 0))
    vec = pl.BlockSpec((1, width), lambda b: (0, 0))
    pvec = pl.BlockSpec((None, 1, width), lambda b: (b, 0, 0))
    return pl.pallas_call(
        body,
        out_shape=[jax.ShapeDtypeStruct((n_batch * seq, width), BF16),
                   jax.ShapeDtypeStruct((n_batch,) + pw.shape, F32),
                   jax.ShapeDtypeStruct((n_batch, 1, width), F32), jax.ShapeDtypeStruct((n_batch, 1, width), F32)],
        grid=(n_batch,),
        in_specs=[tok, tok, pl.BlockSpec(pw.shape, lambda b: (0, 0, 0)), vec, vec],
        out_specs=[tok, pl.BlockSpec((None,) + pw.shape, lambda b: (b, 0, 0, 0)), pvec, pvec],
        compiler_params=_params(), name="pool_bwd",
    )(pv, dyp, pw, ps, onp)


def _pick_lane(tile, idx):
    lane = lax.broadcasted_iota(jnp.int32, (1, LANES), 1)
    return jnp.sum(jnp.where(lane == idx, tile, 0.0), axis=1, keepdims=True)


def _pick_row(tile, idx):
    sub = lax.broadcasted_iota(jnp.int32, (tile.shape[0], 1), 0)
    return jnp.sum(jnp.where(sub == idx, tile, 0.0), axis=0, keepdims=True)


def _put_lane(col, idx):
    lane = lax.broadcasted_iota(jnp.int32, (1, LANES), 1)
    return jnp.where(lane == idx, col, 0.0)


def _scores(qe, kj, fq, fk, i, j, blk):
    s = _dot_nt(qe, kj) + (fq - fk)
    row = lax.broadcasted_iota(jnp.int32, (blk, blk), 0) + i * blk
    col = lax.broadcasted_iota(jnp.int32, (blk, blk), 1) + j * blk
    return jnp.where(row >= col, s, NEG)


def _attn_fwd(qh, kh, vb, fc, fr, n_batch, seq):
    blk = min(ATT_BLOCK, seq)
    nq = seq // blk
    width = qh.shape[1]
    pairs = width // LANES

    def body(q_ref, k_ref, v_ref, fc_ref, fr_ref, o_ref, lse_ref):
        i, p = pl.program_id(1), pl.program_id(2)
        lo = _head_masks()
        q = q_ref[...]
        fq_all = fc_ref[...]
        outs, lses = [], []
        for e in range(2):
            head = 2 * p + e
            me = lo if e == 0 else jnp.logical_not(lo)
            qe = jnp.where(me, q, jnp.zeros_like(q))
            fq = _pick_lane(fq_all, head)

            def step(j, carry, qe=qe, fq=fq, head=head):
                m, l, acc = carry
                off = pl.multiple_of(j * blk, blk)
                kj = k_ref[pl.ds(off, blk), :]
                vj = v_ref[pl.ds(off, blk), :]
                fk = _pick_row(fr_ref[j], head)
                s = _scores(qe, kj, fq, fk, i, j, blk)
                m_new = jnp.maximum(m, jnp.max(s, axis=1, keepdims=True))
                pe = jnp.exp(s - m_new)
                alpha = jnp.exp(m - m_new)
                return m_new, alpha * l + jnp.sum(pe, axis=1, keepdims=True), alpha * acc + _dot(pe.astype(BF16), vj)

            init = (jnp.full((blk, 1), NEG, F32), jnp.zeros((blk, 1), F32), jnp.zeros((blk, LANES), F32))
            m, l, acc = lax.fori_loop(0, i + 1, step, init)
            outs.append(acc / l)
            lses.append(_put_lane(m + jnp.log(l), head))
        o_ref[...] = jnp.where(lo, outs[0], outs[1])

        @pl.when(p == 0)
        def _():
            lse_ref[...] = lses[0] + lses[1]

        @pl.when(p > 0)
        def _():
            lse_ref[...] += lses[0] + lses[1]

    qspec = pl.BlockSpec((blk, LANES), lambda b, i, p: (b * nq + i, p))
    kvspec = pl.BlockSpec((seq, LANES), lambda b, i, p: (b, p))
    col = pl.BlockSpec((blk, LANES), lambda b, i, p: (b * nq + i, 0))
    return pl.pallas_call(
        body,
        out_shape=[jax.ShapeDtypeStruct((n_batch * seq, width), F32), jax.ShapeDtypeStruct((n_batch * seq, LANES), F32)],
        grid=(n_batch, nq, pairs),
        in_specs=[qspec, kvspec, kvspec, col, pl.BlockSpec((None, nq, 8, blk), lambda b, i, p: (b, 0, 0, 0))],
        out_specs=[qspec, col],
        compiler_params=_params(), name="attn_fwd",
    )(qh, kh, vb, fc, fr)


def _attn_bwd_q(qh, kh, vb, do, lse, delta, fc, fr, n_batch, seq):
    blk = min(ATT_BLOCK, seq)
    nq = seq // blk
    width = qh.shape[1]
    pairs = width // LANES

    def body(q_ref, k_ref, v_ref, do_ref, lse_ref, dl_ref, fc_ref, fr_ref, dq_ref, dfq_ref):
        i, p = pl.program_id(1), pl.program_id(2)
        lo = _head_masks()
        q = q_ref[...]
        dov = do_ref[...]
        dqs, dfs = [], []
        for e in range(2):
            head = 2 * p + e
            me = lo if e == 0 else jnp.logical_not(lo)
            qe = jnp.where(me, q, jnp.zeros_like(q))
            doe = jnp.where(me, dov, jnp.zeros_like(dov))
            fq = _pick_lane(fc_ref[...], head)
            ls = _pick_lane(lse_ref[...], head)
            dl = _pick_lane(dl_ref[...], head)

            def step(j, carry, qe=qe, doe=doe, fq=fq, ls=ls, dl=dl, head=head):
                dq, dfq = carry
                off = pl.multiple_of(j * blk, blk)
                kj = k_ref[pl.ds(off, blk), :]
                vj = v_ref[pl.ds(off, blk), :]
                fk = _pick_row(fr_ref[j], head)
                pr = jnp.exp(_scores(qe, kj, fq, fk, i, j, blk) - ls)
                ds = pr * (_dot_nt(doe, vj) - dl)
                return dq + _dot(ds.astype(BF16), kj), dfq + jnp.sum(ds, axis=1, keepdims=True)

            dq, dfq = lax.fori_loop(0, i + 1, step, (jnp.zeros((blk, LANES), F32), jnp.zeros((blk, 1), F32)))
            dqs.append(dq)
            dfs.append(_put_lane(dfq, head))
        dq_ref[...] = jnp.where(lo, dqs[0], dqs[1])

        @pl.when(p == 0)
        def _():
            dfq_ref[...] = dfs[0] + dfs[1]

        @pl.when(p > 0)
        def _():
            dfq_ref[...] += dfs[0] + dfs[1]

    qspec = pl.BlockSpec((blk, LANES), lambda b, i, p: (b * nq + i, p))
    kvspec = pl.BlockSpec((seq, LANES), lambda b, i, p: (b, p))
    col = pl.BlockSpec((blk, LANES), lambda b, i, p: (b * nq + i, 0))
    return pl.pallas_call(
        body,
        out_shape=[jax.ShapeDtypeStruct((n_batch * seq, width), F32), jax.ShapeDtypeStruct((n_batch * seq, LANES), F32)],
        grid=(n_batch, nq, pairs),
        in_specs=[qspec, kvspec, kvspec, qspec, col, col, col, pl.BlockSpec((None, nq, 8, blk), lambda b, i, p: (b, 0, 0, 0))],
        out_specs=[qspec, col],
        compiler_params=_params(), name="attn_bwd_q",
    )(qh, kh, vb, do, lse, delta, fc, fr)


def _attn_bwd_kv(qh, kh, vb, do, lse, delta, fc, fr, n_batch, seq):
    blk = min(ATT_BLOCK, seq)
    nq = seq // blk
    width = qh.shape[1]
    pairs = width // LANES

    def body(q_ref, k_ref, v_ref, do_ref, lse_ref, dl_ref, fc_ref, fr_ref, dk_ref, dv_ref, dfk_ref):
        j, p = pl.program_id(1), pl.program_id(2)
        lo = _head_masks()
        kj = k_ref[...]
        vj = v_ref[...]
        dk = jnp.zeros((blk, LANES), F32)
        dv = jnp.zeros((blk, LANES), F32)
        sub = lax.broadcasted_iota(jnp.int32, (LANES, 1), 0)
        dfk_rows = jnp.zeros((LANES, blk), F32)
        for e in range(2):
            head = 2 * p + e
            me = lo if e == 0 else jnp.logical_not(lo)
            fk = _pick_row(fr_ref[...], head)

            def step(i, carry, me=me, fk=fk, head=head):
                dk, dv, dfk = carry
                off = pl.multiple_of(i * blk, blk)
                qi = q_ref[pl.ds(off, blk), :]
                doi = do_ref[pl.ds(off, blk), :]
                qe = jnp.where(me, qi, jnp.zeros_like(qi))
                doe = jnp.where(me, doi, jnp.zeros_like(doi))
                fq = _pick_lane(fc_ref[pl.ds(off, blk), :], head)
                ls = _pick_lane(lse_ref[pl.ds(off, blk), :], head)
                dl = _pick_lane(dl_ref[pl.ds(off, blk), :], head)
                pr = jnp.exp(_scores(qe, kj, fq, fk, i, j, blk) - ls)
                ds = pr * (_dot_nt(doe, vj) - dl)
                return (dk + _dot_tn(ds.astype(BF16), qe), dv + _dot_tn(pr.astype(BF16), doe),
                        dfk - jnp.sum(ds, axis=0, keepdims=True))

            dk, dv, dfk = lax.fori_loop(j, nq, step, (dk, dv, jnp.zeros((1, blk), F32)))
            dfk_rows = dfk_rows + jnp.where(sub == head, dfk, 0.0)
        dk_ref[...] = dk
        dv_ref[...] = dv.astype(BF16)
        dfk_cols = jnp.concatenate([dfk_rows[:, c * LANES:(c + 1) * LANES].T for c in range(blk // LANES)], axis=0)

        @pl.when(p == 0)
        def _():
            dfk_ref[...] = dfk_cols

        @pl.when(p > 0)
        def _():
            dfk_ref[...] += dfk_cols

    kspec = pl.BlockSpec((blk, LANES), lambda b, j, p: (b * nq + j, p))
    seqspec = pl.BlockSpec((seq, LANES), lambda b, j, p: (b, p))
    seqcol = pl.BlockSpec((seq, LANES), lambda b, j, p: (b, 0))
    col = pl.BlockSpec((blk, LANES), lambda b, j, p: (b * nq + j, 0))
    return pl.pallas_call(
        body,
        out_shape=[jax.ShapeDtypeStruct((n_batch * seq, width), F32), jax.ShapeDtypeStruct((n_batch * seq, width), BF16),
                   jax.ShapeDtypeStruct((n_batch * seq, LANES), F32)],
        grid=(n_batch, nq, pairs),
        in_specs=[seqspec, kspec, kspec, seqspec, seqcol, seqcol, seqcol,
                  pl.BlockSpec((None, None, 8, blk), lambda b, j, p: (b, j, 0, 0))],
        out_specs=[kspec, kspec, col],
        compiler_params=_params(), name="attn_bwd_kv",
    )(qh, kh, vb, do, lse, delta, fc, fr)


def _forget_bwd(dfq, dfk, f, bias, n_batch, seq):
    def body(dfq_ref, dfk_ref, f_ref, b_ref, df_ref, db_ref):
        acc = dfq_ref[...] + dfk_ref[...]
        row = lax.broadcasted_iota(jnp.int32, (seq, 1), 0)
        dist = 1
        while dist < seq:
            acc = acc + _shift_up(acc, dist, row, seq)
            dist *= 2
        df = acc * _sigmoid(-(f_ref[...] + b_ref[...]))
        df_ref[...] = df
        db_ref[...] = jnp.sum(df, axis=0, keepdims=True)

    col = pl.BlockSpec((seq, LANES), lambda b: (b, 0))
    return pl.pallas_call(
        body,
        out_shape=[jax.ShapeDtypeStruct((n_batch * seq, LANES), F32), jax.ShapeDtypeStruct((n_batch, 1, LANES), F32)],
        grid=(n_batch,), in_specs=[col, col, col, pl.BlockSpec((1, LANES), lambda b: (0, 0))],
        out_specs=[col, pl.BlockSpec((None, 1, LANES), lambda b: (b, 0, 0))],
        compiler_params=_params(), name="forget_bwd",
    )(dfq, dfk, f, bias)


def _mix_out(x1, yp, o, ona, woa, wob):
    t, d = x1.shape
    width = o.shape[1]
    tm = min(512, t)

    def body(x_ref, yp_ref, o_ref, on_ref, wa_ref, wb_ref, x2_ref, ya_ref):
        of = o_ref[...]
        ya = ((of * _rms(of)) * on_ref[...]).astype(BF16)
        ya_ref[...] = ya
        x2_ref[...] = x_ref[...] + (_dot(yp_ref[...], wa_ref[...]) + _dot(ya, wb_ref[...]))

    row = pl.BlockSpec((tm, d), lambda i: (i, 0))
    half = pl.BlockSpec((tm, width), lambda i: (i, 0))
    wspec = pl.BlockSpec((width, d), lambda i: (0, 0))
    return pl.pallas_call(
        body, out_shape=[jax.ShapeDtypeStruct((t, d), F32), jax.ShapeDtypeStruct((t, width), BF16)],
        grid=(t // tm,), in_specs=[row, half, half, pl.BlockSpec((1, width), lambda i: (0, 0)), wspec, wspec],
        out_specs=[row, half], compiler_params=_params(), name="mix_out",
    )(x1, yp, o, ona, woa, wob)


def _mix_out_bwd(dx2, o, yp, ya, ona, woa, wob):
    t, d = dx2.shape
    width = o.shape[1]
    tm = min(512, t)
    nt = t // tm

    def body(dx_ref, o_ref, yp_ref, ya_ref, on_ref, wa_ref, wb_ref, dyp_ref, do_ref, dl_ref, dwa_ref, dwb_ref, don_ref):
        @pl.when(pl.program_id(0) == 0)
        def _():
            dwa_ref[...] = jnp.zeros_like(dwa_ref)
            dwb_ref[...] = jnp.zeros_like(dwb_ref)

        dxb = dx_ref[...].astype(BF16)
        dwa_ref[...] += _dot_tn(yp_ref[...], dxb)
        dwb_ref[...] += _dot_tn(ya_ref[...], dxb)
        dyp_ref[...] = _dot_nt(dxb, wa_ref[...])
        of = o_ref[...]
        dov, dgr = _rms_bwd(of, _rms(of), on_ref[...], _dot_nt(dxb, wb_ref[...]))
        don_ref[...] = jnp.sum(dgr, axis=0, keepdims=True)
        do_ref[...] = dov.astype(BF16)
        lo = _head_masks()
        prod = dov * of
        delta = jnp.zeros((tm, LANES), F32)
        for blk in range(width // LANES):
            pb = prod[:, blk * LANES:(blk + 1) * LANES]
            delta = delta + _put_lane(jnp.sum(jnp.where(lo, pb, 0.0), axis=1, keepdims=True), 2 * blk)
            delta = delta + _put_lane(jnp.sum(jnp.where(lo, 0.0, pb), axis=1, keepdims=True), 2 * blk + 1)
        dl_ref[...] = delta

    row = pl.BlockSpec((tm, d), lambda i: (i, 0))
    half = pl.BlockSpec((tm, width), lambda i: (i, 0))
    wspec = pl.BlockSpec((width, d), lambda i: (0, 0))
    return pl.pallas_call(
        body,
        out_shape=[jax.ShapeDtypeStruct((t, width), F32), jax.ShapeDtypeStruct((t, width), BF16),
                   jax.ShapeDtypeStruct((t, LANES), F32), jax.ShapeDtypeStruct((width, d), F32),
                   jax.ShapeDtypeStruct((width, d), F32), jax.ShapeDtypeStruct((nt, 1, width), F32)],
        grid=(nt,),
        in_specs=[row, half, half, half, pl.BlockSpec((1, width), lambda i: (0, 0)), wspec, wspec],
        out_specs=[half, half, pl.BlockSpec((tm, LANES), lambda i: (i, 0)), wspec, wspec,
                   pl.BlockSpec((None, 1, width), lambda i: (i, 0, 0))],
        compiler_params=_params(), name="mix_out_bwd",
    )(dx2, o, yp, ya, ona, woa, wob)


def _mix_in_bwd(dx2, x1, gain, hm, dpv, dqh, q, dkh, k, dv, df, qn, kn, wp, wf):
    t, d = x1.shape
    width = q.shape[1]
    pool_width = dpv.shape[1]
    tm = min(512, t)
    nt = t // tm
    scale = HEAD_DIM ** -0.5
    c_q, c_k, c_v = pool_width, pool_width + width, pool_width + 2 * width

    def body(dx2_ref, x_ref, g_ref, hm_ref, dpv_ref, dqh_ref, q_ref, dkh_ref, k_ref, dv_ref, df_ref, qn_ref, kn_ref,
             wp_ref, wf_ref, dx_ref, dwp_ref, dwf_ref, dg_ref, dqn_ref, dkn_ref):
        @pl.when(pl.program_id(0) == 0)
        def _():
            dwp_ref[...] = jnp.zeros_like(dwp_ref)
            dwf_ref[...] = jnp.zeros_like(dwf_ref)

        lo = _head_masks()
        hm = hm_ref[...]
        pieces = [(0, dpv_ref[...])]
        for c0, raw_ref, dh_ref, n_ref, dn_ref, mul in ((c_q, q_ref, dqh_ref, qn_ref, dqn_ref, scale),
                                                       (c_k, k_ref, dkh_ref, kn_ref, dkn_ref, 1.0)):
            cols = []
            for blk in range(width // LANES):
                sl = slice(blk * LANES, (blk + 1) * LANES)
                xb = raw_ref[:, sl]
                gb = dh_ref[:, sl] * mul
                r = _head_rms(xb, lo)
                xh = xb * r
                dyg = gb * n_ref[:, sl]
                cols.append((r * (dyg - xh * _head_mean(dyg * xh, lo))).astype(BF16))
                dn_ref[:, sl] = jnp.sum(gb * xh, axis=0, keepdims=True)
            pieces.append((c0, jnp.concatenate(cols, axis=1)))
        pieces.append((c_v, dv_ref[...]))
        dfb = df_ref[...].astype(BF16)
        dhm = _dot_nt(dfb, wf_ref[...])
        dwf_ref[...] += _dot_tn(hm, dfb)
        for c0, piece in pieces:
            dwp_ref[:, c0:c0 + piece.shape[1]] += _dot_tn(hm, piece)
            dhm = dhm + _dot_nt(piece, wp_ref[:, c0:c0 + piece.shape[1]])
        xf = x_ref[...]
        dxn, dgr = _rms_bwd(xf, _rms(xf), g_ref[...], dhm)
        dx_ref[...] = dx2_ref[...] + dxn
        dg_ref[...] = jnp.sum(dgr, axis=0, keepdims=True)

    row = pl.BlockSpec((tm, d), lambda i: (i, 0))
    half = pl.BlockSpec((tm, width), lambda i: (i, 0))
    const = lambda shape: pl.BlockSpec(shape, lambda i: (0, 0))
    pvec = lambda n: pl.BlockSpec((None, 1, n), lambda i: (i, 0, 0))
    return pl.pallas_call(
        body,
        out_shape=[jax.ShapeDtypeStruct((t, d), F32), jax.ShapeDtypeStruct(wp.shape, F32),
                   jax.ShapeDtypeStruct(wf.shape, F32), jax.ShapeDtypeStruct((nt, 1, d), F32),
                   jax.ShapeDtypeStruct((nt, 1, width), F32), jax.ShapeDtypeStruct((nt, 1, width), F32)],
        grid=(nt,),
        in_specs=[row, row, const((1, d)), row, pl.BlockSpec((tm, pool_width), lambda i: (i, 0)), half, half, half, half,
                  half, pl.BlockSpec((tm, LANES), lambda i: (i, 0)), const((1, width)), const((1, width)),
                  const(wp.shape), const(wf.shape)],
        out_specs=[row, const(wp.shape), const(wf.shape), pvec(d), pvec(width), pvec(width)],
        compiler_params=_params(), name="mix_in_bwd",
    )(dx2, x1, gain, hm, dpv, dqh, q, dkh, k, dv, df, qn, kn, wp, wf)


def _local_step(xf, tgt, small, full, n_batch, seq):
    pool_width = small["pool_scale"].shape[1]
    attn_width = small["out_norm_attn"].shape[1]
    x1, h1, a1, b1, s1 = _ffn_fwd(xf, small["ffn1_norm"], full["wg1"], full["wu1"], full["wd1"])
    hm, pv, q, k, qh, kh, vb, f = _mix_proj(x1, small["mix_norm"], full["wp"], full["wf"], small["qn"], small["kn"],
                                            pool_width, attn_width)
    fc, fr = _forget_prefix(f, small["b_forget"], n_batch, seq)
    yp = _pool_fwd(pv, full["pool_w"], small["pool_scale"], small["out_norm_pool"], n_batch, seq)
    o, lse = _attn_fwd(qh, kh, vb, fc, fr, n_batch, seq)
    x2, ya = _mix_out(x1, yp, o, small["out_norm_attn"], full["woa"], full["wob"])
    dy, h2, a2, b2, s2, lpart = _ffn_fwd(x2, small["ffn2_norm"], full["wg2"], full["wu2"], full["wd2"], target=tgt)

    dx2, da2, db2, dg2 = _ffn_bwd_x(dy, x2, small["ffn2_norm"], a2, b2, full["wg2"], full["wu2"], full["wd2"], "ffn2_bwd_x")
    dwg2, dwu2, dwd2 = _ffn_bwd_w(h2, s2, da2, db2, dy, "ffn2_bwd_w")
    dyp, do, delta, dwoa, dwob, dona = _mix_out_bwd(dx2, o, yp, ya, small["out_norm_attn"], full["woa"], full["wob"])
    dpv, dpw, dps, donp = _pool_bwd(pv, dyp, full["pool_w"], small["pool_scale"], small["out_norm_pool"], n_batch, seq)
    dqh, dfq = _attn_bwd_q(qh, kh, vb, do, lse, delta, fc, fr, n_batch, seq)
    dkh, dv, dfk = _attn_bwd_kv(qh, kh, vb, do, lse, delta, fc, fr, n_batch, seq)
    df, dbf = _forget_bwd(dfq, dfk, f, small["b_forget"], n_batch, seq)
    dx1, dwp, dwf, dgm, dqn, dkn = _mix_in_bwd(dx2, x1, small["mix_norm"], hm, dpv, dqh, q, dkh, k, dv, df,
                                               small["qn"], small["kn"], full["wp"], full["wf"])
    gx, da1, db1, dg1 = _ffn_bwd_x(dx1, xf, small["ffn1_norm"], a1, b1, full["wg1"], full["wu1"], full["wd1"], "ffn1_bwd_x")
    dwg1, dwu1, dwd1 = _ffn_bwd_w(h1, s1, da1, db1, dx1, "ffn1_bwd_w")
    big = dict(wg1=dwg1, wu1=dwu1, wd1=dwd1, wp=dwp, wf=dwf, woa=dwoa, wob=dwob, wg2=dwg2, wu2=dwu2, wd2=dwd2)
    part = dict(ffn1_norm=dg1, mix_norm=dgm, ffn2_norm=dg2, b_forget=dbf, pool_w=dpw, pool_scale=dps,
                out_norm_pool=donp, out_norm_attn=dona, qn=dqn, kn=dkn)
    return lpart, gx, big, part


def _mesh_pos():
    return lax.axis_index("x"), lax.axis_index("y"), lax.axis_index("c")


def _other_chips(x, y):
    return [(1 - x, y), (x, 1 - y), (1 - x, 1 - y)]


def _remote(src, dst, send_sem, recv_sem, device):
    return pltpu.make_async_remote_copy(src_ref=src, dst_ref=dst, send_sem=send_sem, recv_sem=recv_sem,
                                        device_id=device, device_id_type=pl.DeviceIdType.MESH)


def _half_rows(n_rows, which):
    half = n_rows // 2
    return pl.ds(pl.multiple_of(which * half, 8), half)


def _row_block(rows, cols, itemsize=4):
    rb = rows
    while rb * cols * itemsize > (1 << 20) and rb % 32 == 0:
        rb //= 2
    return rb


def _place_cast(w, chip, tag):
    rows, cols = w.shape
    rb = _row_block(rows, cols)

    def body(k_ref, w_ref, o_ref):
        o_ref[...] = w_ref[...].astype(BF16)

    return pl.pallas_call(
        body, out_shape=jax.ShapeDtypeStruct((N_CHIPS, rows, cols), BF16),
        grid_spec=pltpu.PrefetchScalarGridSpec(
            num_scalar_prefetch=1, grid=(rows // rb,),
            in_specs=[pl.BlockSpec((rb, cols), lambda i, k: (i, 0))],
            out_specs=pl.BlockSpec((None, rb, cols), lambda i, k: (k[0], i, 0))),
        compiler_params=_params(), name="place_" + tag,
    )(chip, w)


def _gather_weights(stacks):
    n = len(stacks)

    def body(*refs):
        outs = refs[n:2 * n]
        ici_send, ici_recv, d2d_send, d2d_recv = refs[2 * n:]
        x, y, c = _mesh_pos()
        mine = 2 * x + y
        sibling = (x, y, 1 - c)
        chips = _other_chips(x, y)
        slots = [2 * cx + cy for cx, cy in chips]
        sends = []
        for w in range(n):
            own = outs[w].at[mine, _half_rows(stacks[w].shape[1], c)]
            for j, chip in enumerate(chips):
                cp = _remote(own, own, ici_send.at[w, j], ici_recv.at[w, j], (*chip, c))
                cp.start()
                sends.append(cp)
        for w in range(n):
            rows = _half_rows(stacks[w].shape[1], c)
            for j in range(3):
                landed = outs[w].at[slots[j], rows]
                _remote(landed, landed, ici_send.at[w, j], ici_recv.at[w, j], sibling).wait_recv()
                cp = _remote(landed, landed, d2d_send.at[w, j], d2d_recv.at[w, j], sibling)
                cp.start()
                sends.append(cp)
        for w in range(n):
            rows = _half_rows(stacks[w].shape[1], 1 - c)
            for j in range(3):
                landed = outs[w].at[slots[j], rows]
                _remote(landed, landed, d2d_send.at[w, j], d2d_recv.at[w, j], sibling).wait_recv()
        for cp in sends:
            cp.wait_send()

    return pl.pallas_call(
        body, out_shape=[jax.ShapeDtypeStruct(s.shape, s.dtype) for s in stacks],
        in_specs=[ANY] * n, out_specs=[ANY] * n, input_output_aliases={w: w for w in range(n)},
        scratch_shapes=[pltpu.SemaphoreType.DMA((n, 3)), pltpu.SemaphoreType.DMA((n, 3)),
                        pltpu.SemaphoreType.DMA((n, 3)), pltpu.SemaphoreType.DMA((n, 3))],
        name="gather_weights",
    )(*stacks)


def _sibling_halves(gs):
    n = len(gs)

    def body(*refs):
        ins, outs = refs[:n], refs[n:2 * n]
        send, recv = refs[2 * n:]
        x, y, c = _mesh_pos()
        cps = []
        for w in range(n):
            cp = _remote(ins[w].at[:, _half_rows(gs[w].shape[1], 1 - c), :], outs[w], send.at[w], recv.at[w], (x, y, 1 - c))
            cp.start()
            cps.append(cp)
        for cp in cps:
            cp.wait()

    return pl.pallas_call(
        body, out_shape=[jax.ShapeDtypeStruct((g.shape[0], g.shape[1] // 2, g.shape[2]), g.dtype) for g in gs],
        in_specs=[ANY] * n, out_specs=[ANY] * n,
        scratch_shapes=[pltpu.SemaphoreType.DMA((n,)), pltpu.SemaphoreType.DMA((n,))],
        name="sibling_halves",
    )(*gs)


def _chip_exchange(ps):
    n = len(ps)

    def body(*refs):
        ins, outs = refs[:n], refs[n:2 * n]
        send, recv = refs[2 * n:]
        x, y, c = _mesh_pos()
        chips = _other_chips(x, y)
        cps = []
        for w in range(n):
            for j, (cx, cy) in enumerate(chips):
                cp = _remote(ins[w].at[2 * cx + cy], outs[w].at[j], send.at[w, j], recv.at[w, j], (cx, cy, c))
                cp.start()
                cps.append(cp)
        for cp in cps:
            cp.wait()

    return pl.pallas_call(
        body, out_shape=[jax.ShapeDtypeStruct((3,) + p.shape[1:], p.dtype) for p in ps],
        in_specs=[ANY] * n, out_specs=[ANY] * n,
        scratch_shapes=[pltpu.SemaphoreType.DMA((n, 3)), pltpu.SemaphoreType.DMA((n, 3))],
        name="chip_exchange",
    )(*ps)


def _sibling_share(gs):
    n = len(gs)

    def body(*refs):
        outs = refs[n:2 * n]
        send, recv = refs[2 * n:]
        x, y, c = _mesh_pos()
        cps = []
        for w in range(n):
            mine = outs[w].at[_half_rows(gs[w].shape[0], c)]
            cp = _remote(mine, mine, send.at[w], recv.at[w], (x, y, 1 - c))
            cp.start()
            cps.append(cp)
        for w, cp in enumerate(cps):
            cp.wait_send()
            theirs = outs[w].at[_half_rows(gs[w].shape[0], 1 - c)]
            _remote(theirs, theirs, send.at[w], recv.at[w], (x, y, 1 - c)).wait_recv()

    return pl.pallas_call(
        body, out_shape=[jax.ShapeDtypeStruct(g.shape, g.dtype) for g in gs],
        in_specs=[ANY] * n, out_specs=[ANY] * n, input_output_aliases={w: w for w in range(n)},
        scratch_shapes=[pltpu.SemaphoreType.DMA((n,)), pltpu.SemaphoreType.DMA((n,))],
        name="sibling_share",
    )(*gs)


def _add_sibling(g, r1, ids, tag):
    nch, rh, cols = r1.shape

    def body(ids_ref, g_ref, r_ref, o_ref):
        o_ref[...] = (g_ref[...] + r_ref[...]).astype(BF16)

    blk = lambda fn: pl.BlockSpec((None, rh, cols), fn)
    return pl.pallas_call(
        body, out_shape=jax.ShapeDtypeStruct(r1.shape, BF16),
        grid_spec=pltpu.PrefetchScalarGridSpec(
            num_scalar_prefetch=1, grid=(nch,),
            in_specs=[blk(lambda k, ids: (k, ids[1], 0)), blk(lambda k, ids: (k, 0, 0))],
            out_specs=blk(lambda k, ids: (k, 0, 0))),
        compiler_params=_params(), name="add_sibling_" + tag,
    )(ids, g, r1)


def _add_chips(g, r1, r2, ids, tag):
    _, rh, cols = r1.shape

    def body(ids_ref, g_ref, r1_ref, r2_ref, o_ref):
        own = g_ref[...] + r1_ref[...]
        o_ref[...] = ((own + r2_ref[0].astype(F32)) + r2_ref[1].astype(F32)) + r2_ref[2].astype(F32)

    return pl.pallas_call(
        body, out_shape=jax.ShapeDtypeStruct((2 * rh, cols), F32),
        grid_spec=pltpu.PrefetchScalarGridSpec(
            num_scalar_prefetch=1, grid=(1,),
            in_specs=[pl.BlockSpec((None, rh, cols), lambda i, ids: (ids[0], ids[1], 0)),
                      pl.BlockSpec((None, rh, cols), lambda i, ids: (ids[0], 0, 0)),
                      pl.BlockSpec((3, rh, cols), lambda i, ids: (0, 0, 0))],
            out_specs=pl.BlockSpec((rh, cols), lambda i, ids: (ids[1], 0))),
        compiler_params=_params(), name="add_chips_" + tag,
    )(ids, g, r1, r2)


def _reduce_to_owner(gs, ids, tags):
    r1 = _sibling_halves(gs)
    ps = [_add_sibling(g, r, ids, t) for g, r, t in zip(gs, r1, tags)]
    r2 = _chip_exchange(ps)
    return _sibling_share([_add_chips(g, ra, rb, ids, t) for g, ra, rb, t in zip(gs, r1, r2, tags)])


VEC_ROWS = 8


def _small_allreduce(part, d, width):
    names = ("ffn1_norm", "mix_norm", "ffn2_norm", "pool_scale", "out_norm_pool", "out_norm_attn", "qn", "kn", "b_forget", "pool_w")
    args = [part[k] for k in names]
    pw_shape = part["pool_w"].shape[1:]
    n_dev = 8

    def body(g1_ref, gm_ref, g2_ref, ps_ref, onp_ref, ona_ref, qn_ref, kn_ref, bf_ref, pw_ref,
             vec_ref, pwo_ref, vbuf, pbuf, send, recv):
        x, y, c = _mesh_pos()
        me = 4 * x + 2 * y + c
        lo = _head_masks()

        def fold_heads(ref):
            v = jnp.sum(ref[...], axis=0)
            acc = jnp.zeros((VEC_ROWS, LANES), F32)
            for blk in range(width // LANES):
                vb = jnp.broadcast_to(v[:, blk * LANES:(blk + 1) * LANES], (VEC_ROWS, LANES))
                acc = acc + vb + pltpu.roll(vb, HEAD_DIM, 1)
            return jnp.where(lo, acc, 0.0)[0:1, :]

        vbuf[0] = jnp.zeros((VEC_ROWS, d), F32)
        vbuf[0, 0:1, :] = jnp.sum(g1_ref[...], axis=0)
        vbuf[0, 1:2, :] = jnp.sum(gm_ref[...], axis=0)
        vbuf[0, 2:3, :] = jnp.sum(g2_ref[...], axis=0)
        vbuf[0, 3:4, 0:width] = jnp.sum(ps_ref[...], axis=0)
        vbuf[0, 3:4, width:2 * width] = jnp.sum(onp_ref[...], axis=0)
        vbuf[0, 4:5, 0:width] = jnp.sum(ona_ref[...], axis=0)
        vbuf[0, 4:5, width:width + LANES] = fold_heads(qn_ref)
        vbuf[0, 4:5, width + LANES:width + 2 * LANES] = fold_heads(kn_ref)
        vbuf[0, 4:5, width + 2 * LANES:width + 3 * LANES] = jnp.sum(bf_ref[...], axis=0)
        pbuf[0] = jnp.sum(pw_ref[...], axis=0)

        cps = []
        for r in range(1, n_dev):
            peer = (x if not r & 4 else 1 - x, y if not r & 2 else 1 - y, c if not r & 1 else 1 - c)
            for buf, k in ((vbuf, 0), (pbuf, 1)):
                cp = _remote(buf.at[0], buf.at[r], send.at[k, r - 1], recv.at[k, r - 1], peer)
                cp.start()
                cps.append(cp)
        for cp in cps:
            cp.wait()
        vec = vbuf[me]
        pw = pbuf[me]
        for dev in range(1, n_dev):
            vec = vec + vbuf[jnp.bitwise_xor(me, dev)]
            pw = pw + pbuf[jnp.bitwise_xor(me, dev)]
        vec_ref[...] = vec
        pwo_ref[...] = pw

    return pl.pallas_call(
        body, out_shape=[jax.ShapeDtypeStruct((VEC_ROWS, d), F32), jax.ShapeDtypeStruct(pw_shape, F32)],
        in_specs=[VM] * len(args), out_specs=[VM, VM],
        scratch_shapes=[pltpu.VMEM((n_dev, VEC_ROWS, d), F32), pltpu.VMEM((n_dev,) + pw_shape, F32),
                        pltpu.SemaphoreType.DMA((2, n_dev - 1)), pltpu.SemaphoreType.DMA((2, n_dev - 1))],
        compiler_params=_params(), name="small_allreduce",
    )(*args)


def _adamw(w, g, m, v, tag):
    rows, cols = w.shape
    rb = rows
    while rb * cols * 4 > (1 << 20) and rb % 16 == 0:
        rb //= 2

    def body(w_ref, g_ref, m_ref, v_ref, d_ref, mo_ref, vo_ref):
        gv = g_ref[...]
        m2 = ADAM_B1 * m_ref[...] + (1.0 - ADAM_B1) * gv
        v2 = ADAM_B2 * v_ref[...] + (1.0 - ADAM_B2) * (gv * gv)
        m_hat = m2 / (1.0 - ADAM_B1 ** ADAM_STEP)
        v_hat = v2 / (1.0 - ADAM_B2 ** ADAM_STEP)
        d_ref[...] = -ADAM_LR * (m_hat / (jnp.sqrt(v_hat) + ADAM_EPS) + ADAM_WD * w_ref[...])
        mo_ref[...] = m2
        vo_ref[...] = v2

    spec = pl.BlockSpec((rb, cols), lambda i: (i, 0))
    return pl.pallas_call(
        body, out_shape=[jax.ShapeDtypeStruct(w.shape, F32)] * 3, grid=(rows // rb,),
        in_specs=[spec] * 4, out_specs=[spec] * 3, compiler_params=_params(), name="adamw_" + tag,
    )(w, g, m, v)


def _pack_vec(p, d, width):
    pad = lambda v: jnp.pad(v, (0, LANES - v.shape[0]))
    row3 = jnp.concatenate([p["pool_scale"], p["out_norm_pool"]])
    row4 = jnp.concatenate([p["out_norm_attn"], pad(p["q_norm"]), pad(p["k_norm"]), pad(p["b_forget"]),
                            jnp.zeros((d - width - 3 * LANES,), F32)])
    rows = [p["ffn1_norm"], p["mix_norm"], p["ffn2_norm"], row3, row4]
    return jnp.pad(jnp.stack(rows), ((0, VEC_ROWS - len(rows)), (0, 0)))


def _unpack_vec(vec, width):
    return dict(ffn1_norm=vec[0], mix_norm=vec[1], ffn2_norm=vec[2], pool_scale=vec[3, :width],
                out_norm_pool=vec[3, width:2 * width], out_norm_attn=vec[4, :width],
                q_norm=vec[4, width:width + HEAD_DIM], k_norm=vec[4, width + LANES:width + LANES + HEAD_DIM],
                b_forget=vec[4, width + 2 * LANES:width + 2 * LANES + N_HEADS])


WEIGHT_NAMES = ("ffn1_norm", "ffn1_w_gate", "ffn1_w_up", "ffn1_w_down", "mix_norm", "w_in", "b_forget", "pool_w",
                "pool_scale", "q_norm", "k_norm", "out_norm_pool", "out_norm_attn", "w_out", "ffn2_norm",
                "ffn2_w_gate", "ffn2_w_up", "ffn2_w_down")
BIG_NAMES = ("ffn1_w_gate", "ffn1_w_up", "ffn1_w_down", "w_in", "w_out", "ffn2_w_gate", "ffn2_w_up", "ffn2_w_down")


def kernel(x, ffn1_norm, ffn1_w_gate, ffn1_w_up, ffn1_w_down, mix_norm, w_in, b_forget, pool_w, pool_scale, q_norm, k_norm, out_norm_pool, out_norm_attn, w_out, ffn2_norm, ffn2_w_gate, ffn2_w_up, ffn2_w_down, loss_target, m_ffn1_norm, m_ffn1_w_gate, m_ffn1_w_up, m_ffn1_w_down, m_mix_norm, m_w_in, m_b_forget, m_pool_w, m_pool_scale, m_q_norm, m_k_norm, m_out_norm_pool, m_out_norm_attn, m_w_out, m_ffn2_norm, m_ffn2_w_gate, m_ffn2_w_up, m_ffn2_w_down, v_ffn1_norm, v_ffn1_w_gate, v_ffn1_w_up, v_ffn1_w_down, v_mix_norm, v_w_in, v_b_forget, v_pool_w, v_pool_scale, v_q_norm, v_k_norm, v_out_norm_pool, v_out_norm_attn, v_w_out, v_ffn2_norm, v_ffn2_w_gate, v_ffn2_w_up, v_ffn2_w_down):
    given = dict(locals())
    w = {n: given[n] for n in WEIGHT_NAMES}
    m = {n: given["m_" + n] for n in WEIGHT_NAMES}
    v = {n: given["v_" + n] for n in WEIGHT_NAMES}
    n_batch, seq, d = x.shape
    width = pool_scale.shape[0]
    in_cols = N_CHIPS * w_in.shape[1]
    proj_cols = in_cols - N_HEADS

    mesh_x, mesh_y, mesh_c = _mesh_pos()
    ids = jnp.stack([2 * mesh_x + mesh_y, mesh_c]).astype(jnp.int32)

    gathered = dict(zip(BIG_NAMES, _gather_weights([_place_cast(w[n], ids, n) for n in BIG_NAMES])))
    w_in_full = jnp.transpose(gathered["w_in"], (1, 0, 2)).reshape(d, in_cols)
    w_out_full = gathered["w_out"].reshape(N_CHIPS * w_out.shape[0], d)
    full = dict(wg1=gathered["ffn1_w_gate"], wu1=gathered["ffn1_w_up"], wd1=gathered["ffn1_w_down"],
                wg2=gathered["ffn2_w_gate"], wu2=gathered["ffn2_w_up"], wd2=gathered["ffn2_w_down"],
                wp=w_in_full[:, :proj_cols], wf=jnp.pad(w_in_full[:, proj_cols:], ((0, 0), (0, LANES - N_HEADS))),
                woa=w_out_full[:width], wob=w_out_full[width:], pool_w=pool_w.astype(BF16))
    row = lambda a: a.reshape(1, -1)
    small = dict(ffn1_norm=row(ffn1_norm), mix_norm=row(mix_norm), ffn2_norm=row(ffn2_norm), pool_scale=row(pool_scale),
                 out_norm_pool=row(out_norm_pool), out_norm_attn=row(out_norm_attn),
                 qn=row(jnp.tile(q_norm, N_HEADS)), kn=row(jnp.tile(k_norm, N_HEADS)),
                 b_forget=row(jnp.pad(b_forget, (0, LANES - N_HEADS))))

    lpart, gx, big, part = _local_step(x.reshape(n_batch * seq, d), loss_target.reshape(n_batch * seq, d),
                                       small, full, n_batch, seq)
    loss = lax.psum(jnp.sum(lpart[:, 0, 0]), MESH_AXES)

    d_w_in = jnp.concatenate([big["wp"], big["wf"][:, :N_HEADS]], axis=1)
    d_w_in = jnp.transpose(d_w_in.reshape(d, N_CHIPS, in_cols // N_CHIPS), (1, 0, 2))
    d_w_out = jnp.concatenate([big["woa"], big["wob"]], axis=0).reshape(N_CHIPS, w_out.shape[0], d)
    stacks = dict(ffn1_w_gate=big["wg1"], ffn1_w_up=big["wu1"], ffn1_w_down=big["wd1"], w_in=d_w_in, w_out=d_w_out,
                  ffn2_w_gate=big["wg2"], ffn2_w_up=big["wu2"], ffn2_w_down=big["wd2"])
    grads = dict(zip(BIG_NAMES, _reduce_to_owner([stacks[n] for n in BIG_NAMES], ids, BIG_NAMES)))

    part = dict(part, pool_w=part["pool_w"].reshape(n_batch, -1, pool_w.shape[-1]))
    g_vec, g_pw = _small_allreduce(part, d, width)
    delta, new_m, new_v = {}, {}, {}
    for n in BIG_NAMES:
        delta[n], new_m[n], new_v[n] = _adamw(w[n], grads[n], m[n], v[n], n)
    flat_pw = lambda a: a.reshape(-1, a.shape[-1])
    d_pw, m_pw, v_pw = _adamw(flat_pw(pool_w), g_pw, flat_pw(m_pool_w), flat_pw(v_pool_w), "pool_w")
    d_vec, m_vec, v_vec = _adamw(_pack_vec(w, d, width), g_vec, _pack_vec(m, d, width), _pack_vec(v, d, width), "vectors")
    grads.update(_unpack_vec(g_vec, width), pool_w=g_pw.reshape(pool_w.shape))
    delta.update(_unpack_vec(d_vec, width), pool_w=d_pw.reshape(pool_w.shape))
    new_m.update(_unpack_vec(m_vec, width), pool_w=m_pw.reshape(pool_w.shape))
    new_v.update(_unpack_vec(v_vec, width), pool_w=v_pw.reshape(pool_w.shape))
    return (loss, gx.reshape(x.shape), *[grads[n] for n in WEIGHT_NAMES], *[delta[n] for n in WEIGHT_NAMES],
            *[new_m[n] for n in WEIGHT_NAMES], *[new_v[n] for n in WEIGHT_NAMES])
```

```python
import functools

import jax
import jax.numpy as jnp
from jax import lax
from jax.experimental import pallas as pl
from jax.experimental.pallas import tpu as pltpu

F32 = jnp.float32
BF16 = jnp.bfloat16
EPS = 1e-6
NEG = -1e30
ADAM_LR = 0.001
ADAM_B1 = 0.9
ADAM_B2 = 0.999
ADAM_EPS = 1e-08
ADAM_WD = 0.01
ADAM_STEP = 10
POOL_WINDOWS = (2, 4, 8, 16)
HEAD_DIM = 64
N_HEADS = 8
LANES = 128
N_CHIPS = 4
ATT_BLOCK = 512
ATT_SUB = 128
VMEM_LIMIT = 56 * 1024 * 1024
MESH_AXES = ("x", "y", "c")
ANY = pl.BlockSpec(memory_space=pl.ANY)
VM = pl.BlockSpec(memory_space=pltpu.VMEM)


def _params(**kw):
    return pltpu.CompilerParams(vmem_limit_bytes=VMEM_LIMIT, **kw)


def _dot(a, b):
    return jnp.dot(a, b, preferred_element_type=F32)


def _dot_nt(a, b):
    return lax.dot_general(a, b, (((1,), (1,)), ((), ())), preferred_element_type=F32)


def _dot_tn(a, b):
    return lax.dot_general(a, b, (((0,), (0,)), ((), ())), preferred_element_type=F32)


def _sigmoid(z):
    return 1.0 / (1.0 + jnp.exp(-z))


def _rms(xf):
    return lax.rsqrt(jnp.mean(xf * xf, axis=-1, keepdims=True) + EPS)


def _rms_bwd(xf, r, gain, dh):
    xh = xf * r
    dyg = dh * gain
    return r * (dyg - xh * jnp.mean(dyg * xh, axis=-1, keepdims=True)), dh * xh


def _total(v):
    return jnp.sum(jnp.sum(v, axis=1, keepdims=True), axis=0, keepdims=True)


def _ffn_fwd(x, gain, wg, wu, wd, target=None):
    t, d = x.shape
    nch, _, fc = wg.shape
    tm = min(512, t)
    nt = t // tm
    with_loss = target is not None

    def body(*refs):
        if with_loss:
            x_ref, g_ref, wg_ref, wu_ref, wd_ref, t_ref, o_ref, h_ref, a_ref, b_ref, s_ref, l_ref, acc_ref = refs
        else:
            x_ref, g_ref, wg_ref, wu_ref, wd_ref, o_ref, h_ref, a_ref, b_ref, s_ref, acc_ref = refs
        k = pl.program_id(1)

        @pl.when(k == 0)
        def _():
            xf = x_ref[...]
            h_ref[...] = ((xf * _rms(xf)) * g_ref[...]).astype(BF16)
            acc_ref[...] = jnp.zeros_like(acc_ref)

        h = h_ref[...]
        a = _dot(h, wg_ref[...])
        b = _dot(h, wu_ref[...])
        sb = ((a * _sigmoid(a)) * b).astype(BF16)
        a_ref[...] = a.astype(BF16)
        b_ref[...] = b.astype(BF16)
        s_ref[...] = sb
        acc_ref[...] += _dot(sb, wd_ref[...])

        @pl.when(k == nch - 1)
        def _():
            y = x_ref[...] + 0.5 * acc_ref[...]
            if with_loss:
                e = y - t_ref[...]
                o_ref[...] = e * (1.0 / d)
                l_ref[...] = jnp.broadcast_to(_total(e * e) * (0.5 / d), l_ref.shape)
            else:
                o_ref[...] = y

    row = pl.BlockSpec((tm, d), lambda i, k: (i, 0))
    chunk_in = pl.BlockSpec((None, d, fc), lambda i, k: (k, 0, 0))
    chunk_out = pl.BlockSpec((None, fc, d), lambda i, k: (k, 0, 0))
    act = pl.BlockSpec((None, tm, fc), lambda i, k: (k, i, 0))
    in_specs = [row, pl.BlockSpec((1, d), lambda i, k: (0, 0)), chunk_in, chunk_in, chunk_out]
    out_shape = [jax.ShapeDtypeStruct((t, d), F32), jax.ShapeDtypeStruct((t, d), BF16)]
    out_shape += [jax.ShapeDtypeStruct((nch, t, fc), BF16)] * 3
    out_specs = [row, row, act, act, act]
    args = [x, gain, wg, wu, wd]
    if with_loss:
        in_specs.append(row)
        args.append(target)
        out_shape.append(jax.ShapeDtypeStruct((nt, 8, LANES), F32))
        out_specs.append(pl.BlockSpec((None, 8, LANES), lambda i, k: (i, 0, 0)))
    return pl.pallas_call(
        body, out_shape=out_shape, grid=(nt, nch), in_specs=in_specs, out_specs=out_specs,
        scratch_shapes=[pltpu.VMEM((tm, d), F32)], compiler_params=_params(),
        name="ffn_fwd_loss" if with_loss else "ffn_fwd",
    )(*args)


def _ffn_bwd_x(dy, x, gain, a, b, wg, wu, wd, name):
    t, d = x.shape
    nch, _, fc = wg.shape
    tm = min(512, t)
    nt = t // tm

    def body(dy_ref, x_ref, g_ref, a_ref, b_ref, wg_ref, wu_ref, wd_ref, dx_ref, da_ref, db_ref, dg_ref, acc_ref):
        k = pl.program_id(1)

        @pl.when(k == 0)
        def _():
            acc_ref[...] = jnp.zeros_like(acc_ref)

        ds = 0.5 * _dot_nt(dy_ref[...].astype(BF16), wd_ref[...])
        av = a_ref[...].astype(F32)
        bv = b_ref[...].astype(F32)
        sig = _sigmoid(av)
        dab = (ds * bv * (sig * (1.0 + av * (1.0 - sig)))).astype(BF16)
        dbb = (ds * (av * sig)).astype(BF16)
        da_ref[...] = dab
        db_ref[...] = dbb
        acc_ref[...] += _dot_nt(dab, wg_ref[...]) + _dot_nt(dbb, wu_ref[...])

        @pl.when(k == nch - 1)
        def _():
            xf = x_ref[...]
            dxn, dgr = _rms_bwd(xf, _rms(xf), g_ref[...], acc_ref[...])
            dx_ref[...] = dy_ref[...] + dxn
            dg_ref[...] = jnp.sum(dgr, axis=0, keepdims=True)

    row = pl.BlockSpec((tm, d), lambda i, k: (i, 0))
    chunk_in = pl.BlockSpec((None, d, fc), lambda i, k: (k, 0, 0))
    chunk_out = pl.BlockSpec((None, fc, d), lambda i, k: (k, 0, 0))
    act = pl.BlockSpec((None, tm, fc), lambda i, k: (k, i, 0))
    return pl.pallas_call(
        body,
        out_shape=[jax.ShapeDtypeStruct((t, d), F32), jax.ShapeDtypeStruct((nch, t, fc), BF16),
                   jax.ShapeDtypeStruct((nch, t, fc), BF16), jax.ShapeDtypeStruct((nt, 1, d), F32)],
        grid=(nt, nch),
        in_specs=[row, row, pl.BlockSpec((1, d), lambda i, k: (0, 0)), act, act, chunk_in, chunk_in, chunk_out],
        out_specs=[row, act, act, pl.BlockSpec((None, 1, d), lambda i, k: (i, 0, 0))],
        scratch_shapes=[pltpu.VMEM((tm, d), F32)], compiler_params=_params(), name=name,
    )(dy, x, gain, a, b, wg, wu, wd)


def _ffn_bwd_w(h, s, da, db, dy, name):
    t, d = h.shape
    nch, _, fc = s.shape
    tm = min(512, t)
    nt = t // tm

    def body(h_ref, s_ref, da_ref, db_ref, dy_ref, dwg_ref, dwu_ref, dwd_ref):
        @pl.when(pl.program_id(1) == 0)
        def _():
            dwg_ref[...] = jnp.zeros_like(dwg_ref)
            dwu_ref[...] = jnp.zeros_like(dwu_ref)
            dwd_ref[...] = jnp.zeros_like(dwd_ref)

        hv = h_ref[...]
        dwg_ref[...] += _dot_tn(hv, da_ref[...])
        dwu_ref[...] += _dot_tn(hv, db_ref[...])
        dwd_ref[...] += _dot_tn(s_ref[...], (0.5 * dy_ref[...]).astype(BF16))

    row = pl.BlockSpec((tm, d), lambda k, i: (i, 0))
    act = pl.BlockSpec((None, tm, fc), lambda k, i: (k, i, 0))
    w_in = pl.BlockSpec((None, d, fc), lambda k, i: (k, 0, 0))
    w_out = pl.BlockSpec((None, fc, d), lambda k, i: (k, 0, 0))
    return pl.pallas_call(
        body,
        out_shape=[jax.ShapeDtypeStruct((nch, d, fc), F32), jax.ShapeDtypeStruct((nch, d, fc), F32),
                   jax.ShapeDtypeStruct((nch, fc, d), F32)],
        grid=(nch, nt), in_specs=[row, act, act, act, row], out_specs=[w_in, w_in, w_out],
        compiler_params=_params(), name=name,
    )(h, s, da, db, dy)


def _head_masks():
    lane = lax.broadcasted_iota(jnp.int32, (1, LANES), 1)
    return lane < HEAD_DIM


def _head_rms(x, lo):
    x2 = x * x
    s0 = jnp.sum(jnp.where(lo, x2, 0.0), axis=1, keepdims=True)
    s1 = jnp.sum(jnp.where(lo, 0.0, x2), axis=1, keepdims=True)
    return jnp.where(lo, lax.rsqrt(s0 * (1.0 / HEAD_DIM) + EPS), lax.rsqrt(s1 * (1.0 / HEAD_DIM) + EPS))


def _head_mean(v, lo):
    s0 = jnp.sum(jnp.where(lo, v, 0.0), axis=1, keepdims=True)
    s1 = jnp.sum(jnp.where(lo, 0.0, v), axis=1, keepdims=True)
    return jnp.where(lo, s0, s1) * (1.0 / HEAD_DIM)


def _mix_proj(x1, gain, wp, wf, qn, kn, pool_width, attn_width):
    t, d = x1.shape
    tm = min(512, t)
    nt = t // tm
    scale = HEAD_DIM ** -0.5
    c_q, c_k, c_v = pool_width, pool_width + attn_width, pool_width + 2 * attn_width

    def body(x_ref, g_ref, wp_ref, wf_ref, qn_ref, kn_ref, hm_ref, pv_ref, q_ref, k_ref, qh_ref, kh_ref, vb_ref, f_ref):
        xf = x_ref[...]
        hm = ((xf * _rms(xf)) * g_ref[...]).astype(BF16)
        hm_ref[...] = hm
        f_ref[...] = _dot(hm, wf_ref[...])
        pv_ref[...] = _dot(hm, wp_ref[:, 0:pool_width])
        vb_ref[...] = _dot(hm, wp_ref[:, c_v:c_v + attn_width]).astype(BF16)
        lo = _head_masks()
        for c0, raw_ref, hat_ref, n_ref, mul in ((c_q, q_ref, qh_ref, qn_ref, scale), (c_k, k_ref, kh_ref, kn_ref, 1.0)):
            raw = _dot(hm, wp_ref[:, c0:c0 + attn_width])
            raw_ref[...] = raw
            for blk in range(attn_width // LANES):
                sl = slice(blk * LANES, (blk + 1) * LANES)
                xb = raw[:, sl]
                hat_ref[:, sl] = (((xb * _head_rms(xb, lo)) * n_ref[:, sl]) * mul).astype(BF16)

    row = pl.BlockSpec((tm, d), lambda i: (i, 0))
    half = pl.BlockSpec((tm, attn_width), lambda i: (i, 0))
    const = lambda shape: pl.BlockSpec(shape, lambda i: (0, 0))
    return pl.pallas_call(
        body,
        out_shape=[jax.ShapeDtypeStruct((t, d), BF16), jax.ShapeDtypeStruct((t, pool_width), F32),
                   jax.ShapeDtypeStruct((t, attn_width), F32), jax.ShapeDtypeStruct((t, attn_width), F32),
                   jax.ShapeDtypeStruct((t, attn_width), BF16), jax.ShapeDtypeStruct((t, attn_width), BF16),
                   jax.ShapeDtypeStruct((t, attn_width), BF16), jax.ShapeDtypeStruct((t, LANES), F32)],
        grid=(nt,),
        in_specs=[row, const((1, d)), const(wp.shape), const(wf.shape), const((1, attn_width)), const((1, attn_width))],
        out_specs=[row, pl.BlockSpec((tm, pool_width), lambda i: (i, 0)), half, half, half, half, half,
                   pl.BlockSpec((tm, LANES), lambda i: (i, 0))],
        compiler_params=_params(), name="mix_proj",
    )(x1, gain, wp, wf, qn, kn)


def _shift_down(v, dist, row):
    return jnp.where(row >= dist, pltpu.roll(v, dist, 0), 0.0)


def _shift_up(v, dist, row, n):
    return jnp.where(row + dist < n, pltpu.roll(v, n - dist, 0), 0.0)


def _aug_lane(e):
    return HEAD_DIM if e == 0 else 0


def _forget_prefix(f, bias, qh, kh, n_batch, seq):
    def body(f_ref, b_ref, q_ref, k_ref, qa_ref, ka_ref):
        z = f_ref[...] + b_ref[...]
        acc = jnp.minimum(z, 0.0) - jnp.log(1.0 + jnp.exp(-jnp.abs(z)))
        row = lax.broadcasted_iota(jnp.int32, (seq, 1), 0)
        dist = 1
        while dist < seq:
            acc = acc + _shift_down(acc, dist, row)
            dist *= 2
        lane = lax.broadcasted_iota(jnp.int32, (1, LANES), 1)
        for h in range(N_HEADS):
            pair, e = divmod(h, 2)
            a0 = _aug_lane(e)
            own = (lane < HEAD_DIM) if e == 0 else (lane >= HEAD_DIM)
            fh = _pick_lane(acc, h)
            hi = fh.astype(BF16).astype(F32)
            rest = fh - hi
            mid = rest.astype(BF16).astype(F32)
            low = rest - mid
            q_ones = (lane >= a0 + 3) & (lane < a0 + 6)
            k_ones = (lane >= a0) & (lane < a0 + 3)
            q_aug = jnp.where(lane == a0, hi, jnp.where(lane == a0 + 1, mid, jnp.where(lane == a0 + 2, low,
                              jnp.where(q_ones, 1.0, 0.0))))
            k_aug = jnp.where(k_ones, 1.0, jnp.where(lane == a0 + 3, -hi, jnp.where(lane == a0 + 4, -mid,
                              jnp.where(lane == a0 + 5, -low, 0.0))))
            src = slice(pair * LANES, (pair + 1) * LANES)
            dst = slice(h * LANES, (h + 1) * LANES)
            qa_ref[:, dst] = jnp.where(own, q_ref[:, src].astype(F32), q_aug).astype(BF16)
            ka_ref[:, dst] = jnp.where(own, k_ref[:, src].astype(F32), k_aug).astype(BF16)

    width = qh.shape[1]
    tok = pl.BlockSpec((seq, width), lambda b: (b, 0))
    aug = pl.BlockSpec((seq, N_HEADS * LANES), lambda b: (b, 0))
    return pl.pallas_call(
        body, out_shape=[jax.ShapeDtypeStruct((n_batch * seq, N_HEADS * LANES), BF16)] * 2, grid=(n_batch,),
        in_specs=[pl.BlockSpec((seq, LANES), lambda b: (b, 0)), pl.BlockSpec((1, LANES), lambda b: (0, 0)), tok, tok],
        out_specs=[aug, aug], compiler_params=_params(), name="forget_prefix",
    )(f, bias, qh, kh)


def _pool_groups(pv_ref, pw_ref, ps_ref, seq):
    row = lax.broadcasted_iota(jnp.int32, (seq, 1), 0)
    pos = (row + 1).astype(F32)
    out = []
    for g, win in enumerate(POOL_WINDOWS):
        sl = slice(g * LANES, (g + 1) * LANES)
        xg = pv_ref[:, sl]
        acc = xg
        dist = 1
        while dist < win:
            acc = acc + _shift_down(acc, dist, row)
            dist *= 2
        pooled = (acc / jnp.minimum(pos, float(win)) - xg).astype(BF16)
        mixed = _dot(pooled, pw_ref[g])
        out.append((pooled, mixed, mixed * ps_ref[:, sl]))
    return out


def _pool_fwd(pv, pw, ps, onp, n_batch, seq):
    width = pv.shape[1]

    def body(pv_ref, pw_ref, ps_ref, on_ref, y_ref):
        groups = _pool_groups(pv_ref, pw_ref, ps_ref, seq)
        ssq = sum(jnp.sum(ms * ms, axis=1, keepdims=True) for _, _, ms in groups)
        r = lax.rsqrt(ssq * (1.0 / width) + EPS)
        for g, (_, _, ms) in enumerate(groups):
            sl = slice(g * LANES, (g + 1) * LANES)
            y_ref[:, sl] = ((ms * r) * on_ref[:, sl]).astype(BF16)

    return pl.pallas_call(
        body, out_shape=jax.ShapeDtypeStruct((n_batch * seq, width), BF16), grid=(n_batch,),
        in_specs=[pl.BlockSpec((seq, width), lambda b: (b, 0)), pl.BlockSpec(pw.shape, lambda b: (0, 0, 0)),
                  pl.BlockSpec((1, width), lambda b: (0, 0)), pl.BlockSpec((1, width), lambda b: (0, 0))],
        out_specs=pl.BlockSpec((seq, width), lambda b: (b, 0)),
        compiler_params=_params(), name="pool_fwd",
    )(pv, pw, ps, onp)


def _pool_bwd(pv, dyp, pw, ps, onp, n_batch, seq):
    width = pv.shape[1]

    def body(pv_ref, dy_ref, pw_ref, ps_ref, on_ref, dpv_ref, dpw_ref, dps_ref, don_ref):
        groups = _pool_groups(pv_ref, pw_ref, ps_ref, seq)
        ssq = sum(jnp.sum(ms * ms, axis=1, keepdims=True) for _, _, ms in groups)
        r = lax.rsqrt(ssq * (1.0 / width) + EPS)
        mean = sum(jnp.sum((dy_ref[:, g * LANES:(g + 1) * LANES] * on_ref[:, g * LANES:(g + 1) * LANES]) * (ms * r),
                           axis=1, keepdims=True) for g, (_, _, ms) in enumerate(groups)) * (1.0 / width)
        row = lax.broadcasted_iota(jnp.int32, (seq, 1), 0)
        pos = (row + 1).astype(F32)
        for g, (pooled, mixed, ms) in enumerate(groups):
            sl = slice(g * LANES, (g + 1) * LANES)
            dy = dy_ref[:, sl]
            xh = ms * r
            don_ref[:, sl] = jnp.sum(dy * xh, axis=0, keepdims=True)
            dms = r * (dy * on_ref[:, sl] - xh * mean)
            dps_ref[:, sl] = jnp.sum(dms * mixed, axis=0, keepdims=True)
            dmix = (dms * ps_ref[:, sl]).astype(BF16)
            dpw_ref[g] = _dot_tn(pooled, dmix)
            dpool = _dot_nt(dmix, pw_ref[g])
            win = POOL_WINDOWS[g]
            acc = dpool / jnp.minimum(pos, float(win))
            dist = 1
            while dist < win:
                acc = acc + _shift_up(acc, dist, row, seq)
                dist *= 2
            dpv_ref[:, sl] = (acc - dpool).astype(BF16)

    tok = pl.BlockSpec((seq, width), lambda b: (b, 0))
    vec = pl.BlockSpec((1, width), lambda b: (0, 0))
    pvec = pl.BlockSpec((None, 1, width), lambda b: (b, 0, 0))
    return pl.pallas_call(
        body,
        out_shape=[jax.ShapeDtypeStruct((n_batch * seq, width), BF16),
                   jax.ShapeDtypeStruct((n_batch,) + pw.shape, F32),
                   jax.ShapeDtypeStruct((n_batch, 1, width), F32), jax.ShapeDtypeStruct((n_batch, 1, width), F32)],
        grid=(n_batch,),
        in_specs=[tok, tok, pl.BlockSpec(pw.shape, lambda b: (0, 0, 0)), vec, vec],
        out_specs=[tok, pl.BlockSpec((None,) + pw.shape, lambda b: (b, 0, 0, 0)), pvec, pvec],
        compiler_params=_params(), name="pool_bwd",
    )(pv, dyp, pw, ps, onp)


def _pick_lane(tile, idx):
    lane = lax.broadcasted_iota(jnp.int32, (1, LANES), 1)
    return jnp.sum(jnp.where(lane == idx, tile, 0.0), axis=1, keepdims=True)


def _pick_row(tile, idx):
    sub = lax.broadcasted_iota(jnp.int32, (tile.shape[0], 1), 0)
    return jnp.sum(jnp.where(sub == idx, tile, 0.0), axis=0, keepdims=True)


def _put_lane(col, idx):
    lane = lax.broadcasted_iota(jnp.int32, (1, LANES), 1)
    return jnp.where(lane == idx, col, 0.0)


def _head_select(e):
    lo = _head_masks()
    return lo if e == 0 else jnp.logical_not(lo)


def _causal(st, shift):
    row = lax.broadcasted_iota(jnp.int32, st.shape, 0)
    col = lax.broadcasted_iota(jnp.int32, st.shape, 1) + shift
    return jnp.where(col >= row, st, NEG)


def _stat_rows(ref, head, nsub):
    return jnp.concatenate([_pick_row(ref[a], head) for a in range(nsub)], axis=1)


def _accumulate(ref, value, first):
    @pl.when(first)
    def _():
        ref[...] = value

    @pl.when(jnp.logical_not(first))
    def _():
        ref[...] += value


def _attn_fwd(qa, ka, vb, n_batch, seq):
    tq = min(ATT_BLOCK, seq)
    nq, nsub, tk = seq // tq, tq // ATT_SUB, tq
    pairs = vb.shape[1] // LANES

    def body(q_ref, k_ref, v_ref, o_ref, lse_ref, acc_ref):
        i, p = pl.program_id(1), pl.program_id(2)
        row_lo = lax.broadcasted_iota(jnp.int32, (LANES, 1), 0) < HEAD_DIM
        qs = [q_ref[:, e * LANES:(e + 1) * LANES] for e in range(2)]
        acc_ref[...] = jnp.zeros_like(acc_ref)

        def tile(off, stats, diagonal):
            vj = v_ref[pl.ds(off, tk), :]
            new, alphas, pvs = [], [], []
            for e in range(2):
                st = _dot_nt(k_ref[pl.ds(off, tk), e * LANES:(e + 1) * LANES], qs[e])
                if diagonal:
                    st = _causal(st, 0)
                m, l = stats[e]
                m_new = jnp.maximum(m, jnp.max(st, axis=0, keepdims=True))
                alpha = jnp.exp(m - m_new)
                pt = jnp.exp(st - m_new)
                new.append((m_new, alpha * l + jnp.sum(pt, axis=0, keepdims=True)))
                alphas.append(alpha)
                pvs.append(_dot_tn(jnp.where(_head_select(e), vj, jnp.zeros_like(vj)), pt.astype(BF16)))
            acc_ref[...] = acc_ref[...] * jnp.where(row_lo, alphas[0], alphas[1]) + (pvs[0] + pvs[1])
            return tuple(new)

        init = ((jnp.full((1, tq), NEG, F32), jnp.zeros((1, tq), F32)),) * 2
        stats = lax.fori_loop(0, i, lambda j, st: tile(pl.multiple_of(j * tk, tk), st, False), init)
        (m0, l0), (m1, l1) = tile(pl.multiple_of(i * tk, tk), stats, True)
        out_t = acc_ref[...] / jnp.where(row_lo, l0, l1)
        sub = lax.broadcasted_iota(jnp.int32, (8, 1), 0)
        lse0, lse1 = m0 + jnp.log(l0), m1 + jnp.log(l1)
        for a in range(nsub):
            sl = slice(a * ATT_SUB, (a + 1) * ATT_SUB)
            o_ref[sl, :] = out_t[:, sl].T
            rows = jnp.where(sub == 2 * p, lse0[:, sl], 0.0) + jnp.where(sub == 2 * p + 1, lse1[:, sl], 0.0)
            _accumulate(lse_ref.at[a], rows, p == 0)

    return pl.pallas_call(
        body,
        out_shape=[jax.ShapeDtypeStruct((n_batch * seq, pairs * LANES), F32),
                   jax.ShapeDtypeStruct((n_batch * seq // ATT_SUB, 8, ATT_SUB), F32)],
        grid=(n_batch, nq, pairs),
        in_specs=[pl.BlockSpec((tq, 2 * LANES), lambda b, i, p: (b * nq + i, p)),
                  pl.BlockSpec((seq, 2 * LANES), lambda b, i, p: (b, p)),
                  pl.BlockSpec((seq, LANES), lambda b, i, p: (b, p))],
        out_specs=[pl.BlockSpec((tq, LANES), lambda b, i, p: (b * nq + i, p)),
                   pl.BlockSpec((nsub, 8, ATT_SUB), lambda b, i, p: (b * nq + i, 0, 0))],
        scratch_shapes=[pltpu.VMEM((LANES, tq), F32)],
        compiler_params=_params(), name="attn_fwd",
    )(qa, ka, vb)


def _attn_bwd_q(qa, ka, vb, do, lse, delta, n_batch, seq):
    tq = min(ATT_BLOCK, seq)
    nq, nsub, tk = seq // tq, tq // ATT_SUB, tq
    pairs = vb.shape[1] // LANES

    def body(q_ref, k_ref, v_ref, do_ref, lse_ref, dl_ref, dq_ref, dfq_ref, acc0_ref, acc1_ref):
        i, p = pl.program_id(1), pl.program_id(2)
        accs = (acc0_ref, acc1_ref)
        qs = [q_ref[:, e * LANES:(e + 1) * LANES] for e in range(2)]
        dov = do_ref[...]
        ls = [_stat_rows(lse_ref, 2 * p + e, nsub) for e in range(2)]
        dl = [_stat_rows(dl_ref, 2 * p + e, nsub) for e in range(2)]
        for acc in accs:
            acc[...] = jnp.zeros_like(acc)

        def tile(off, diagonal):
            vj = v_ref[pl.ds(off, tk), :]
            for e in range(2):
                kj = k_ref[pl.ds(off, tk), e * LANES:(e + 1) * LANES]
                st = _dot_nt(kj, qs[e])
                if diagonal:
                    st = _causal(st, 0)
                pt = jnp.exp(st - ls[e])
                dpt = _dot_nt(jnp.where(_head_select(e), vj, jnp.zeros_like(vj)), dov)
                accs[e][...] += _dot_tn((pt * (dpt - dl[e])).astype(BF16), kj)

        def step(j, carry):
            tile(pl.multiple_of(j * tk, tk), False)
            return carry

        lax.fori_loop(0, i, step, 0)
        tile(pl.multiple_of(i * tk, tk), True)
        dq0, dq1 = acc0_ref[...], acc1_ref[...]
        dq_ref[...] = jnp.where(_head_masks(), dq0, dq1)
        dfq = _put_lane(_pick_lane(dq0, _aug_lane(0)), 2 * p) + _put_lane(_pick_lane(dq1, _aug_lane(1)), 2 * p + 1)
        _accumulate(dfq_ref, dfq, p == 0)

    stat = pl.BlockSpec((nsub, 8, ATT_SUB), lambda b, i, p: (b * nq + i, 0, 0))
    blk = pl.BlockSpec((tq, LANES), lambda b, i, p: (b * nq + i, p))
    return pl.pallas_call(
        body,
        out_shape=[jax.ShapeDtypeStruct((n_batch * seq, pairs * LANES), F32), jax.ShapeDtypeStruct((n_batch * seq, LANES), F32)],
        grid=(n_batch, nq, pairs),
        in_specs=[pl.BlockSpec((tq, 2 * LANES), lambda b, i, p: (b * nq + i, p)),
                  pl.BlockSpec((seq, 2 * LANES), lambda b, i, p: (b, p)),
                  pl.BlockSpec((seq, LANES), lambda b, i, p: (b, p)), blk, stat, stat],
        out_specs=[blk, pl.BlockSpec((tq, LANES), lambda b, i, p: (b * nq + i, 0))],
        scratch_shapes=[pltpu.VMEM((tq, LANES), F32), pltpu.VMEM((tq, LANES), F32)],
        compiler_params=_params(), name="attn_bwd_q",
    )(qa, ka, vb, do, lse, delta)


def _attn_bwd_kv(qa, ka, vb, do, lse, delta, n_batch, seq):
    tkb = min(ATT_BLOCK, seq)
    nk, nsub, tq = seq // tkb, tkb // ATT_SUB, tkb
    n_tiles = seq // ATT_SUB
    pairs = vb.shape[1] // LANES

    def body(q_ref, k_ref, v_ref, do_ref, lse_ref, dl_ref, dk_ref, dv_ref, dfk_ref, dk0_ref, dk1_ref, dva_ref):
        j, p = pl.program_id(1), pl.program_id(2)
        dks = (dk0_ref, dk1_ref)
        ks = [k_ref[:, e * LANES:(e + 1) * LANES] for e in range(2)]
        vj = v_ref[...]
        vs = [jnp.where(_head_select(e), vj, jnp.zeros_like(vj)) for e in range(2)]
        for acc in (dk0_ref, dk1_ref, dva_ref):
            acc[...] = jnp.zeros_like(acc)

        def tile(t, diagonal):
            off = pl.multiple_of(t * tq, tq)
            dov = do_ref[pl.ds(off, tq), :]
            for e in range(2):
                qe = q_ref[pl.ds(off, tq), e * LANES:(e + 1) * LANES]
                st = _dot_nt(ks[e], qe)
                if diagonal:
                    st = _causal(st, 0)
                rows = lambda ref: jnp.concatenate([_pick_row(ref[t * nsub + a], 2 * p + e) for a in range(nsub)], axis=1)
                pt = jnp.exp(st - rows(lse_ref))
                dva_ref[...] += _dot(pt.astype(BF16), jnp.where(_head_select(e), dov, jnp.zeros_like(dov)))
                dst = pt * (_dot_nt(vs[e], dov) - rows(dl_ref))
                dks[e][...] += _dot(dst.astype(BF16), qe)

        def step(t, carry):
            tile(t, False)
            return carry

        lax.fori_loop(j + 1, nk, step, 0)
        tile(j, True)
        dk0, dk1 = dk0_ref[...], dk1_ref[...]
        dk_ref[...] = jnp.where(_head_masks(), dk0, dk1)
        dv_ref[...] = dva_ref[...].astype(BF16)
        dfk = (_put_lane(_pick_lane(dk0, _aug_lane(0) + 3), 2 * p)
               + _put_lane(_pick_lane(dk1, _aug_lane(1) + 3), 2 * p + 1))
        _accumulate(dfk_ref, -dfk, p == 0)

    stat = pl.BlockSpec((n_tiles, 8, ATT_SUB), lambda b, j, p: (b, 0, 0))
    blk = pl.BlockSpec((tkb, LANES), lambda b, j, p: (b * nk + j, p))
    acc = pltpu.VMEM((tkb, LANES), F32)
    return pl.pallas_call(
        body,
        out_shape=[jax.ShapeDtypeStruct((n_batch * seq, pairs * LANES), F32),
                   jax.ShapeDtypeStruct((n_batch * seq, pairs * LANES), BF16),
                   jax.ShapeDtypeStruct((n_batch * seq, LANES), F32)],
        grid=(n_batch, nk, pairs),
        in_specs=[pl.BlockSpec((seq, 2 * LANES), lambda b, j, p: (b, p)),
                  pl.BlockSpec((tkb, 2 * LANES), lambda b, j, p: (b * nk + j, p)), blk,
                  pl.BlockSpec((seq, LANES), lambda b, j, p: (b, p)), stat, stat],
        out_specs=[blk, blk, pl.BlockSpec((tkb, LANES), lambda b, j, p: (b * nk + j, 0))],
        scratch_shapes=[acc, acc, acc],
        compiler_params=_params(), name="attn_bwd_kv",
    )(qa, ka, vb, do, lse, delta)


def _forget_bwd(dfq, dfk, f, bias, n_batch, seq):
    def body(dfq_ref, dfk_ref, f_ref, b_ref, df_ref, db_ref):
        acc = dfq_ref[...] + dfk_ref[...]
        row = lax.broadcasted_iota(jnp.int32, (seq, 1), 0)
        dist = 1
        while dist < seq:
            acc = acc + _shift_up(acc, dist, row, seq)
            dist *= 2
        df = acc * _sigmoid(-(f_ref[...] + b_ref[...]))
        df_ref[...] = df
        db_ref[...] = jnp.sum(df, axis=0, keepdims=True)

    col = pl.BlockSpec((seq, LANES), lambda b: (b, 0))
    return pl.pallas_call(
        body,
        out_shape=[jax.ShapeDtypeStruct((n_batch * seq, LANES), F32), jax.ShapeDtypeStruct((n_batch, 1, LANES), F32)],
        grid=(n_batch,), in_specs=[col, col, col, pl.BlockSpec((1, LANES), lambda b: (0, 0))],
        out_specs=[col, pl.BlockSpec((None, 1, LANES), lambda b: (b, 0, 0))],
        compiler_params=_params(), name="forget_bwd",
    )(dfq, dfk, f, bias)


def _mix_out(x1, yp, o, ona, woa, wob):
    t, d = x1.shape
    width = o.shape[1]
    tm = min(512, t)

    def body(x_ref, yp_ref, o_ref, on_ref, wa_ref, wb_ref, x2_ref, ya_ref):
        of = o_ref[...]
        ya = ((of * _rms(of)) * on_ref[...]).astype(BF16)
        ya_ref[...] = ya
        x2_ref[...] = x_ref[...] + (_dot(yp_ref[...], wa_ref[...]) + _dot(ya, wb_ref[...]))

    row = pl.BlockSpec((tm, d), lambda i: (i, 0))
    half = pl.BlockSpec((tm, width), lambda i: (i, 0))
    wspec = pl.BlockSpec((width, d), lambda i: (0, 0))
    return pl.pallas_call(
        body, out_shape=[jax.ShapeDtypeStruct((t, d), F32), jax.ShapeDtypeStruct((t, width), BF16)],
        grid=(t // tm,), in_specs=[row, half, half, pl.BlockSpec((1, width), lambda i: (0, 0)), wspec, wspec],
        out_specs=[row, half], compiler_params=_params(), name="mix_out",
    )(x1, yp, o, ona, woa, wob)


def _mix_out_bwd(dx2, o, yp, ya, ona, woa, wob):
    t, d = dx2.shape
    width = o.shape[1]
    tm = min(512, t)
    nt = t // tm

    def body(dx_ref, o_ref, yp_ref, ya_ref, on_ref, wa_ref, wb_ref, dyp_ref, do_ref, dl_ref, dwa_ref, dwb_ref, don_ref):
        @pl.when(pl.program_id(0) == 0)
        def _():
            dwa_ref[...] = jnp.zeros_like(dwa_ref)
            dwb_ref[...] = jnp.zeros_like(dwb_ref)

        dxb = dx_ref[...].astype(BF16)
        dwa_ref[...] += _dot_tn(yp_ref[...], dxb)
        dwb_ref[...] += _dot_tn(ya_ref[...], dxb)
        dyp_ref[...] = _dot_nt(dxb, wa_ref[...])
        of = o_ref[...]
        dov, dgr = _rms_bwd(of, _rms(of), on_ref[...], _dot_nt(dxb, wb_ref[...]))
        don_ref[...] = jnp.sum(dgr, axis=0, keepdims=True)
        do_ref[...] = dov.astype(BF16)
        lo = _head_masks()
        prod = dov * of
        delta = jnp.zeros((tm, LANES), F32)
        for blk in range(width // LANES):
            pb = prod[:, blk * LANES:(blk + 1) * LANES]
            delta = delta + _put_lane(jnp.sum(jnp.where(lo, pb, 0.0), axis=1, keepdims=True), 2 * blk)
            delta = delta + _put_lane(jnp.sum(jnp.where(lo, 0.0, pb), axis=1, keepdims=True), 2 * blk + 1)
        for c in range(tm // ATT_SUB):
            dl_ref[c] = delta[c * ATT_SUB:(c + 1) * ATT_SUB, :].T[0:8, :]

    row = pl.BlockSpec((tm, d), lambda i: (i, 0))
    half = pl.BlockSpec((tm, width), lambda i: (i, 0))
    wspec = pl.BlockSpec((width, d), lambda i: (0, 0))
    return pl.pallas_call(
        body,
        out_shape=[jax.ShapeDtypeStruct((t, width), F32), jax.ShapeDtypeStruct((t, width), BF16),
                   jax.ShapeDtypeStruct((t // ATT_SUB, 8, ATT_SUB), F32), jax.ShapeDtypeStruct((width, d), F32),
                   jax.ShapeDtypeStruct((width, d), F32), jax.ShapeDtypeStruct((nt, 1, width), F32)],
        grid=(nt,),
        in_specs=[row, half, half, half, pl.BlockSpec((1, width), lambda i: (0, 0)), wspec, wspec],
        out_specs=[half, half, pl.BlockSpec((tm // ATT_SUB, 8, ATT_SUB), lambda i: (i, 0, 0)), wspec, wspec,
                   pl.BlockSpec((None, 1, width), lambda i: (i, 0, 0))],
        compiler_params=_params(), name="mix_out_bwd",
    )(dx2, o, yp, ya, ona, woa, wob)


def _mix_in_bwd(dx2, x1, gain, hm, dpv, dqh, q, dkh, k, dv, df, qn, kn, wp, wf):
    t, d = x1.shape
    width = q.shape[1]
    pool_width = dpv.shape[1]
    tm = min(512, t)
    nt = t // tm
    scale = HEAD_DIM ** -0.5
    c_q, c_k, c_v = pool_width, pool_width + width, pool_width + 2 * width

    def body(dx2_ref, x_ref, g_ref, hm_ref, dpv_ref, dqh_ref, q_ref, dkh_ref, k_ref, dv_ref, df_ref, qn_ref, kn_ref,
             wp_ref, wf_ref, dx_ref, dwp_ref, dwf_ref, dg_ref, dqn_ref, dkn_ref):
        @pl.when(pl.program_id(0) == 0)
        def _():
            dwp_ref[...] = jnp.zeros_like(dwp_ref)
            dwf_ref[...] = jnp.zeros_like(dwf_ref)

        lo = _head_masks()
        hm = hm_ref[...]
        pieces = [(0, dpv_ref[...])]
        for c0, raw_ref, dh_ref, n_ref, dn_ref, mul in ((c_q, q_ref, dqh_ref, qn_ref, dqn_ref, scale),
                                                       (c_k, k_ref, dkh_ref, kn_ref, dkn_ref, 1.0)):
            cols = []
            for blk in range(width // LANES):
                sl = slice(blk * LANES, (blk + 1) * LANES)
                xb = raw_ref[:, sl]
                gb = dh_ref[:, sl] * mul
                r = _head_rms(xb, lo)
                xh = xb * r
                dyg = gb * n_ref[:, sl]
                cols.append((r * (dyg - xh * _head_mean(dyg * xh, lo))).astype(BF16))
                dn_ref[:, sl] = jnp.sum(gb * xh, axis=0, keepdims=True)
            pieces.append((c0, jnp.concatenate(cols, axis=1)))
        pieces.append((c_v, dv_ref[...]))
        dfb = df_ref[...].astype(BF16)
        dhm = _dot_nt(dfb, wf_ref[...])
        dwf_ref[...] += _dot_tn(hm, dfb)
        for c0, piece in pieces:
            dwp_ref[:, c0:c0 + piece.shape[1]] += _dot_tn(hm, piece)
            dhm = dhm + _dot_nt(piece, wp_ref[:, c0:c0 + piece.shape[1]])
        xf = x_ref[...]
        dxn, dgr = _rms_bwd(xf, _rms(xf), g_ref[...], dhm)
        dx_ref[...] = dx2_ref[...] + dxn
        dg_ref[...] = jnp.sum(dgr, axis=0, keepdims=True)

    row = pl.BlockSpec((tm, d), lambda i: (i, 0))
    half = pl.BlockSpec((tm, width), lambda i: (i, 0))
    const = lambda shape: pl.BlockSpec(shape, lambda i: (0, 0))
    pvec = lambda n: pl.BlockSpec((None, 1, n), lambda i: (i, 0, 0))
    return pl.pallas_call(
        body,
        out_shape=[jax.ShapeDtypeStruct((t, d), F32), jax.ShapeDtypeStruct(wp.shape, F32),
                   jax.ShapeDtypeStruct(wf.shape, F32), jax.ShapeDtypeStruct((nt, 1, d), F32),
                   jax.ShapeDtypeStruct((nt, 1, width), F32), jax.ShapeDtypeStruct((nt, 1, width), F32)],
        grid=(nt,),
        in_specs=[row, row, const((1, d)), row, pl.BlockSpec((tm, pool_width), lambda i: (i, 0)), half, half, half, half,
                  half, pl.BlockSpec((tm, LANES), lambda i: (i, 0)), const((1, width)), const((1, width)),
                  const(wp.shape), const(wf.shape)],
        out_specs=[row, const(wp.shape), const(wf.shape), pvec(d), pvec(width), pvec(width)],
        compiler_params=_params(), name="mix_in_bwd",
    )(dx2, x1, gain, hm, dpv, dqh, q, dkh, k, dv, df, qn, kn, wp, wf)


def _local_step(xf, tgt, small, full, n_batch, seq):
    pool_width = small["pool_scale"].shape[1]
    attn_width = small["out_norm_attn"].shape[1]
    x1, h1, a1, b1, s1 = _ffn_fwd(xf, small["ffn1_norm"], full["wg1"], full["wu1"], full["wd1"])
    hm, pv, q, k, qh, kh, vb, f = _mix_proj(x1, small["mix_norm"], full["wp"], full["wf"], small["qn"], small["kn"],
                                            pool_width, attn_width)
    qa, ka = _forget_prefix(f, small["b_forget"], qh, kh, n_batch, seq)
    yp = _pool_fwd(pv, full["pool_w"], small["pool_scale"], small["out_norm_pool"], n_batch, seq)
    o, lse = _attn_fwd(qa, ka, vb, n_batch, seq)
    x2, ya = _mix_out(x1, yp, o, small["out_norm_attn"], full["woa"], full["wob"])
    dy, h2, a2, b2, s2, lpart = _ffn_fwd(x2, small["ffn2_norm"], full["wg2"], full["wu2"], full["wd2"], target=tgt)

    dx2, da2, db2, dg2 = _ffn_bwd_x(dy, x2, small["ffn2_norm"], a2, b2, full["wg2"], full["wu2"], full["wd2"], "ffn2_bwd_x")
    dwg2, dwu2, dwd2 = _ffn_bwd_w(h2, s2, da2, db2, dy, "ffn2_bwd_w")
    dyp, do, delta, dwoa, dwob, dona = _mix_out_bwd(dx2, o, yp, ya, small["out_norm_attn"], full["woa"], full["wob"])
    dpv, dpw, dps, donp = _pool_bwd(pv, dyp, full["pool_w"], small["pool_scale"], small["out_norm_pool"], n_batch, seq)
    dqh, dfq = _attn_bwd_q(qa, ka, vb, do, lse, delta, n_batch, seq)
    dkh, dv, dfk = _attn_bwd_kv(qa, ka, vb, do, lse, delta, n_batch, seq)
    df, dbf = _forget_bwd(dfq, dfk, f, small["b_forget"], n_batch, seq)
    dx1, dwp, dwf, dgm, dqn, dkn = _mix_in_bwd(dx2, x1, small["mix_norm"], hm, dpv, dqh, q, dkh, k, dv, df,
                                               small["qn"], small["kn"], full["wp"], full["wf"])
    gx, da1, db1, dg1 = _ffn_bwd_x(dx1, xf, small["ffn1_norm"], a1, b1, full["wg1"], full["wu1"], full["wd1"], "ffn1_bwd_x")
    dwg1, dwu1, dwd1 = _ffn_bwd_w(h1, s1, da1, db1, dx1, "ffn1_bwd_w")
    big = dict(wg1=dwg1, wu1=dwu1, wd1=dwd1, wp=dwp, wf=dwf, woa=dwoa, wob=dwob, wg2=dwg2, wu2=dwu2, wd2=dwd2)
    part = dict(ffn1_norm=dg1, mix_norm=dgm, ffn2_norm=dg2, b_forget=dbf, pool_w=dpw, pool_scale=dps,
                out_norm_pool=donp, out_norm_attn=dona, qn=dqn, kn=dkn)
    return lpart, gx, big, part


def _mesh_pos():
    return lax.axis_index("x"), lax.axis_index("y"), lax.axis_index("c")


def _other_chips(x, y):
    return [(1 - x, y), (x, 1 - y), (1 - x, 1 - y)]


def _remote(src, dst, send_sem, recv_sem, device):
    return pltpu.make_async_remote_copy(src_ref=src, dst_ref=dst, send_sem=send_sem, recv_sem=recv_sem,
                                        device_id=device, device_id_type=pl.DeviceIdType.MESH)


def _half_rows(n_rows, which):
    half = n_rows // 2
    return pl.ds(pl.multiple_of(which * half, 8), half)


def _row_block(rows, cols, itemsize=4):
    rb = rows
    while rb * cols * itemsize > (1 << 20) and rb % 32 == 0:
        rb //= 2
    return rb


def _place_cast(w, chip, tag):
    rows, cols = w.shape
    rb = _row_block(rows, cols)

    def body(k_ref, w_ref, o_ref):
        o_ref[...] = w_ref[...].astype(BF16)

    return pl.pallas_call(
        body, out_shape=jax.ShapeDtypeStruct((N_CHIPS, rows, cols), BF16),
        grid_spec=pltpu.PrefetchScalarGridSpec(
            num_scalar_prefetch=1, grid=(rows // rb,),
            in_specs=[pl.BlockSpec((rb, cols), lambda i, k: (i, 0))],
            out_specs=pl.BlockSpec((None, rb, cols), lambda i, k: (k[0], i, 0))),
        compiler_params=_params(), name="place_" + tag,
    )(chip, w)


def _gather_weights(stacks):
    n = len(stacks)

    def body(*refs):
        outs = refs[n:2 * n]
        ici_send, ici_recv, d2d_send, d2d_recv = refs[2 * n:]
        x, y, c = _mesh_pos()
        mine = 2 * x + y
        sibling = (x, y, 1 - c)
        chips = _other_chips(x, y)
        slots = [2 * cx + cy for cx, cy in chips]
        sends = []
        for w in range(n):
            own = outs[w].at[mine, _half_rows(stacks[w].shape[1], c)]
            for j, chip in enumerate(chips):
                cp = _remote(own, own, ici_send.at[w, j], ici_recv.at[w, j], (*chip, c))
                cp.start()
                sends.append(cp)
        for w in range(n):
            rows = _half_rows(stacks[w].shape[1], c)
            for j in range(3):
                landed = outs[w].at[slots[j], rows]
                _remote(landed, landed, ici_send.at[w, j], ici_recv.at[w, j], sibling).wait_recv()
                cp = _remote(landed, landed, d2d_send.at[w, j], d2d_recv.at[w, j], sibling)
                cp.start()
                sends.append(cp)
        for w in range(n):
            rows = _half_rows(stacks[w].shape[1], 1 - c)
            for j in range(3):
                landed = outs[w].at[slots[j], rows]
                _remote(landed, landed, d2d_send.at[w, j], d2d_recv.at[w, j], sibling).wait_recv()
        for cp in sends:
            cp.wait_send()

    return pl.pallas_call(
        body, out_shape=[jax.ShapeDtypeStruct(s.shape, s.dtype) for s in stacks],
        in_specs=[ANY] * n, out_specs=[ANY] * n, input_output_aliases={w: w for w in range(n)},
        scratch_shapes=[pltpu.SemaphoreType.DMA((n, 3)), pltpu.SemaphoreType.DMA((n, 3)),
                        pltpu.SemaphoreType.DMA((n, 3)), pltpu.SemaphoreType.DMA((n, 3))],
        name="gather_weights",
    )(*stacks)


def _sibling_halves(gs):
    n = len(gs)

    def body(*refs):
        ins, outs = refs[:n], refs[n:2 * n]
        send, recv = refs[2 * n:]
        x, y, c = _mesh_pos()
        cps = []
        for w in range(n):
            cp = _remote(ins[w].at[:, _half_rows(gs[w].shape[1], 1 - c), :], outs[w], send.at[w], recv.at[w], (x, y, 1 - c))
            cp.start()
            cps.append(cp)
        for cp in cps:
            cp.wait()

    return pl.pallas_call(
        body, out_shape=[jax.ShapeDtypeStruct((g.shape[0], g.shape[1] // 2, g.shape[2]), g.dtype) for g in gs],
        in_specs=[ANY] * n, out_specs=[ANY] * n,
        scratch_shapes=[pltpu.SemaphoreType.DMA((n,)), pltpu.SemaphoreType.DMA((n,))],
        name="sibling_halves",
    )(*gs)


def _chip_exchange(ps):
    n = len(ps)

    def body(*refs):
        ins, outs = refs[:n], refs[n:2 * n]
        send, recv = refs[2 * n:]
        x, y, c = _mesh_pos()
        chips = _other_chips(x, y)
        cps = []
        for w in range(n):
            for j, (cx, cy) in enumerate(chips):
                cp = _remote(ins[w].at[2 * cx + cy], outs[w].at[j], send.at[w, j], recv.at[w, j], (cx, cy, c))
                cp.start()
                cps.append(cp)
        for cp in cps:
            cp.wait()

    return pl.pallas_call(
        body, out_shape=[jax.ShapeDtypeStruct((3,) + p.shape[1:], p.dtype) for p in ps],
        in_specs=[ANY] * n, out_specs=[ANY] * n,
        scratch_shapes=[pltpu.SemaphoreType.DMA((n, 3)), pltpu.SemaphoreType.DMA((n, 3))],
        name="chip_exchange",
    )(*ps)


def _sibling_share(gs):
    n = len(gs)

    def body(*refs):
        outs = refs[n:2 * n]
        send, recv = refs[2 * n:]
        x, y, c = _mesh_pos()
        cps = []
        for w in range(n):
            mine = outs[w].at[_half_rows(gs[w].shape[0], c)]
            cp = _remote(mine, mine, send.at[w], recv.at[w], (x, y, 1 - c))
            cp.start()
            cps.append(cp)
        for w, cp in enumerate(cps):
            cp.wait_send()
            theirs = outs[w].at[_half_rows(gs[w].shape[0], 1 - c)]
            _remote(theirs, theirs, send.at[w], recv.at[w], (x, y, 1 - c)).wait_recv()

    return pl.pallas_call(
        body, out_shape=[jax.ShapeDtypeStruct(g.shape, g.dtype) for g in gs],
        in_specs=[ANY] * n, out_specs=[ANY] * n, input_output_aliases={w: w for w in range(n)},
        scratch_shapes=[pltpu.SemaphoreType.DMA((n,)), pltpu.SemaphoreType.DMA((n,))],
        name="sibling_share",
    )(*gs)


def _add_sibling(g, r1, ids, tag):
    nch, rh, cols = r1.shape

    def body(ids_ref, g_ref, r_ref, o_ref):
        o_ref[...] = (g_ref[...] + r_ref[...]).astype(BF16)

    blk = lambda fn: pl.BlockSpec((None, rh, cols), fn)
    return pl.pallas_call(
        body, out_shape=jax.ShapeDtypeStruct(r1.shape, BF16),
        grid_spec=pltpu.PrefetchScalarGridSpec(
            num_scalar_prefetch=1, grid=(nch,),
            in_specs=[blk(lambda k, ids: (k, ids[1], 0)), blk(lambda k, ids: (k, 0, 0))],
            out_specs=blk(lambda k, ids: (k, 0, 0))),
        compiler_params=_params(), name="add_sibling_" + tag,
    )(ids, g, r1)


def _add_chips(g, r1, r2, ids, tag):
    _, rh, cols = r1.shape

    def body(ids_ref, g_ref, r1_ref, r2_ref, o_ref):
        own = g_ref[...] + r1_ref[...]
        o_ref[...] = ((own + r2_ref[0].astype(F32)) + r2_ref[1].astype(F32)) + r2_ref[2].astype(F32)

    return pl.pallas_call(
        body, out_shape=jax.ShapeDtypeStruct((2 * rh, cols), F32),
        grid_spec=pltpu.PrefetchScalarGridSpec(
            num_scalar_prefetch=1, grid=(1,),
            in_specs=[pl.BlockSpec((None, rh, cols), lambda i, ids: (ids[0], ids[1], 0)),
                      pl.BlockSpec((None, rh, cols), lambda i, ids: (ids[0], 0, 0)),
                      pl.BlockSpec((3, rh, cols), lambda i, ids: (0, 0, 0))],
            out_specs=pl.BlockSpec((rh, cols), lambda i, ids: (ids[1], 0))),
        compiler_params=_params(), name="add_chips_" + tag,
    )(ids, g, r1, r2)


def _reduce_to_owner(gs, ids, tags):
    r1 = _sibling_halves(gs)
    ps = [_add_sibling(g, r, ids, t) for g, r, t in zip(gs, r1, tags)]
    r2 = _chip_exchange(ps)
    return _sibling_share([_add_chips(g, ra, rb, ids, t) for g, ra, rb, t in zip(gs, r1, r2, tags)])


VEC_ROWS = 8


def _small_allreduce(part, d, width):
    names = ("ffn1_norm", "mix_norm", "ffn2_norm", "pool_scale", "out_norm_pool", "out_norm_attn", "qn", "kn", "b_forget", "pool_w")
    args = [part[k] for k in names]
    pw_shape = part["pool_w"].shape[1:]
    n_dev = 8

    def body(g1_ref, gm_ref, g2_ref, ps_ref, onp_ref, ona_ref, qn_ref, kn_ref, bf_ref, pw_ref,
             vec_ref, pwo_ref, vbuf, pbuf, send, recv):
        x, y, c = _mesh_pos()
        me = 4 * x + 2 * y + c
        lo = _head_masks()

        def fold_heads(ref):
            v = jnp.sum(ref[...], axis=0)
            acc = jnp.zeros((VEC_ROWS, LANES), F32)
            for blk in range(width // LANES):
                vb = jnp.broadcast_to(v[:, blk * LANES:(blk + 1) * LANES], (VEC_ROWS, LANES))
                acc = acc + vb + pltpu.roll(vb, HEAD_DIM, 1)
            return jnp.where(lo, acc, 0.0)[0:1, :]

        vbuf[0] = jnp.zeros((VEC_ROWS, d), F32)
        vbuf[0, 0:1, :] = jnp.sum(g1_ref[...], axis=0)
        vbuf[0, 1:2, :] = jnp.sum(gm_ref[...], axis=0)
        vbuf[0, 2:3, :] = jnp.sum(g2_ref[...], axis=0)
        vbuf[0, 3:4, 0:width] = jnp.sum(ps_ref[...], axis=0)
        vbuf[0, 3:4, width:2 * width] = jnp.sum(onp_ref[...], axis=0)
        vbuf[0, 4:5, 0:width] = jnp.sum(ona_ref[...], axis=0)
        vbuf[0, 4:5, width:width + LANES] = fold_heads(qn_ref)
        vbuf[0, 4:5, width + LANES:width + 2 * LANES] = fold_heads(kn_ref)
        vbuf[0, 4:5, width + 2 * LANES:width + 3 * LANES] = jnp.sum(bf_ref[...], axis=0)
        pbuf[0] = jnp.sum(pw_ref[...], axis=0)

        cps = []
        for r in range(1, n_dev):
            peer = (x if not r & 4 else 1 - x, y if not r & 2 else 1 - y, c if not r & 1 else 1 - c)
            for buf, k in ((vbuf, 0), (pbuf, 1)):
                cp = _remote(buf.at[0], buf.at[r], send.at[k, r - 1], recv.at[k, r - 1], peer)
                cp.start()
                cps.append(cp)
        for cp in cps:
            cp.wait()
        vec = vbuf[me]
        pw = pbuf[me]
        for dev in range(1, n_dev):
            vec = vec + vbuf[jnp.bitwise_xor(me, dev)]
            pw = pw + pbuf[jnp.bitwise_xor(me, dev)]
        vec_ref[...] = vec
        pwo_ref[...] = pw

    return pl.pallas_call(
        body, out_shape=[jax.ShapeDtypeStruct((VEC_ROWS, d), F32), jax.ShapeDtypeStruct(pw_shape, F32)],
        in_specs=[VM] * len(args), out_specs=[VM, VM],
        scratch_shapes=[pltpu.VMEM((n_dev, VEC_ROWS, d), F32), pltpu.VMEM((n_dev,) + pw_shape, F32),
                        pltpu.SemaphoreType.DMA((2, n_dev - 1)), pltpu.SemaphoreType.DMA((2, n_dev - 1))],
        compiler_params=_params(), name="small_allreduce",
    )(*args)


def _adamw(w, g, m, v, tag):
    rows, cols = w.shape
    rb = rows
    while rb * cols * 4 > (1 << 20) and rb % 16 == 0:
        rb //= 2

    def body(w_ref, g_ref, m_ref, v_ref, d_ref, mo_ref, vo_ref):
        gv = g_ref[...]
        m2 = ADAM_B1 * m_ref[...] + (1.0 - ADAM_B1) * gv
        v2 = ADAM_B2 * v_ref[...] + (1.0 - ADAM_B2) * (gv * gv)
        m_hat = m2 / (1.0 - ADAM_B1 ** ADAM_STEP)
        v_hat = v2 / (1.0 - ADAM_B2 ** ADAM_STEP)
        d_ref[...] = -ADAM_LR * (m_hat / (jnp.sqrt(v_hat) + ADAM_EPS) + ADAM_WD * w_ref[...])
        mo_ref[...] = m2
        vo_ref[...] = v2

    spec = pl.BlockSpec((rb, cols), lambda i: (i, 0))
    return pl.pallas_call(
        body, out_shape=[jax.ShapeDtypeStruct(w.shape, F32)] * 3, grid=(rows // rb,),
        in_specs=[spec] * 4, out_specs=[spec] * 3, compiler_params=_params(), name="adamw_" + tag,
    )(w, g, m, v)


def _pack_vec(p, d, width):
    pad = lambda v: jnp.pad(v, (0, LANES - v.shape[0]))
    row3 = jnp.concatenate([p["pool_scale"], p["out_norm_pool"]])
    row4 = jnp.concatenate([p["out_norm_attn"], pad(p["q_norm"]), pad(p["k_norm"]), pad(p["b_forget"]),
                            jnp.zeros((d - width - 3 * LANES,), F32)])
    rows = [p["ffn1_norm"], p["mix_norm"], p["ffn2_norm"], row3, row4]
    return jnp.pad(jnp.stack(rows), ((0, VEC_ROWS - len(rows)), (0, 0)))


def _unpack_vec(vec, width):
    return dict(ffn1_norm=vec[0], mix_norm=vec[1], ffn2_norm=vec[2], pool_scale=vec[3, :width],
                out_norm_pool=vec[3, width:2 * width], out_norm_attn=vec[4, :width],
                q_norm=vec[4, width:width + HEAD_DIM], k_norm=vec[4, width + LANES:width + LANES + HEAD_DIM],
                b_forget=vec[4, width + 2 * LANES:width + 2 * LANES + N_HEADS])


WEIGHT_NAMES = ("ffn1_norm", "ffn1_w_gate", "ffn1_w_up", "ffn1_w_down", "mix_norm", "w_in", "b_forget", "pool_w",
                "pool_scale", "q_norm", "k_norm", "out_norm_pool", "out_norm_attn", "w_out", "ffn2_norm",
                "ffn2_w_gate", "ffn2_w_up", "ffn2_w_down")
BIG_NAMES = ("ffn1_w_gate", "ffn1_w_up", "ffn1_w_down", "w_in", "w_out", "ffn2_w_gate", "ffn2_w_up", "ffn2_w_down")


def kernel(x, ffn1_norm, ffn1_w_gate, ffn1_w_up, ffn1_w_down, mix_norm, w_in, b_forget, pool_w, pool_scale, q_norm, k_norm, out_norm_pool, out_norm_attn, w_out, ffn2_norm, ffn2_w_gate, ffn2_w_up, ffn2_w_down, loss_target, m_ffn1_norm, m_ffn1_w_gate, m_ffn1_w_up, m_ffn1_w_down, m_mix_norm, m_w_in, m_b_forget, m_pool_w, m_pool_scale, m_q_norm, m_k_norm, m_out_norm_pool, m_out_norm_attn, m_w_out, m_ffn2_norm, m_ffn2_w_gate, m_ffn2_w_up, m_ffn2_w_down, v_ffn1_norm, v_ffn1_w_gate, v_ffn1_w_up, v_ffn1_w_down, v_mix_norm, v_w_in, v_b_forget, v_pool_w, v_pool_scale, v_q_norm, v_k_norm, v_out_norm_pool, v_out_norm_attn, v_w_out, v_ffn2_norm, v_ffn2_w_gate, v_ffn2_w_up, v_ffn2_w_down):
    given = dict(locals())
    w = {n: given[n] for n in WEIGHT_NAMES}
    m = {n: given["m_" + n] for n in WEIGHT_NAMES}
    v = {n: given["v_" + n] for n in WEIGHT_NAMES}
    n_batch, seq, d = x.shape
    width = pool_scale.shape[0]
    in_cols = N_CHIPS * w_in.shape[1]
    proj_cols = in_cols - N_HEADS

    mesh_x, mesh_y, mesh_c = _mesh_pos()
    ids = jnp.stack([2 * mesh_x + mesh_y, mesh_c]).astype(jnp.int32)

    gathered = dict(zip(BIG_NAMES, _gather_weights([_place_cast(w[n], ids, n) for n in BIG_NAMES])))
    w_in_full = jnp.transpose(gathered["w_in"], (1, 0, 2)).reshape(d, in_cols)
    w_out_full = gathered["w_out"].reshape(N_CHIPS * w_out.shape[0], d)
    full = dict(wg1=gathered["ffn1_w_gate"], wu1=gathered["ffn1_w_up"], wd1=gathered["ffn1_w_down"],
                wg2=gathered["ffn2_w_gate"], wu2=gathered["ffn2_w_up"], wd2=gathered["ffn2_w_down"],
                wp=w_in_full[:, :proj_cols], wf=jnp.pad(w_in_full[:, proj_cols:], ((0, 0), (0, LANES - N_HEADS))),
                woa=w_out_full[:width], wob=w_out_full[width:], pool_w=pool_w.astype(BF16))
    row = lambda a: a.reshape(1, -1)
    small = dict(ffn1_norm=row(ffn1_norm), mix_norm=row(mix_norm), ffn2_norm=row(ffn2_norm), pool_scale=row(pool_scale),
                 out_norm_pool=row(out_norm_pool), out_norm_attn=row(out_norm_attn),
                 qn=row(jnp.tile(q_norm, N_HEADS)), kn=row(jnp.tile(k_norm, N_HEADS)),
                 b_forget=row(jnp.pad(b_forget, (0, LANES - N_HEADS))))

    lpart, gx, big, part = _local_step(x.reshape(n_batch * seq, d), loss_target.reshape(n_batch * seq, d),
                                       small, full, n_batch, seq)
    loss = lax.psum(jnp.sum(lpart[:, 0, 0]), MESH_AXES)

    d_w_in = jnp.concatenate([big["wp"], big["wf"][:, :N_HEADS]], axis=1)
    d_w_in = jnp.transpose(d_w_in.reshape(d, N_CHIPS, in_cols // N_CHIPS), (1, 0, 2))
    d_w_out = jnp.concatenate([big["woa"], big["wob"]], axis=0).reshape(N_CHIPS, w_out.shape[0], d)
    stacks = dict(ffn1_w_gate=big["wg1"], ffn1_w_up=big["wu1"], ffn1_w_down=big["wd1"], w_in=d_w_in, w_out=d_w_out,
                  ffn2_w_gate=big["wg2"], ffn2_w_up=big["wu2"], ffn2_w_down=big["wd2"])
    grads = dict(zip(BIG_NAMES, _reduce_to_owner([stacks[n] for n in BIG_NAMES], ids, BIG_NAMES)))

    part = dict(part, pool_w=part["pool_w"].reshape(n_batch, -1, pool_w.shape[-1]))
    g_vec, g_pw = _small_allreduce(part, d, width)
    delta, new_m, new_v = {}, {}, {}
    for n in BIG_NAMES:
        delta[n], new_m[n], new_v[n] = _adamw(w[n], grads[n], m[n], v[n], n)
    flat_pw = lambda a: a.reshape(-1, a.shape[-1])
    d_pw, m_pw, v_pw = _adamw(flat_pw(pool_w), g_pw, flat_pw(m_pool_w), flat_pw(v_pool_w), "pool_w")
    d_vec, m_vec, v_vec = _adamw(_pack_vec(w, d, width), g_vec, _pack_vec(m, d, width), _pack_vec(v, d, width), "vectors")
    grads.update(_unpack_vec(g_vec, width), pool_w=g_pw.reshape(pool_w.shape))
    delta.update(_unpack_vec(d_vec, width), pool_w=d_pw.reshape(pool_w.shape))
    new_m.update(_unpack_vec(m_vec, width), pool_w=m_pw.reshape(pool_w.shape))
    new_v.update(_unpack_vec(v_vec, width), pool_w=v_pw.reshape(pool_w.shape))
    return (loss, gx.reshape(x.shape), *[grads[n] for n in WEIGHT_NAMES], *[delta[n] for n in WEIGHT_NAMES],
            *[new_m[n] for n in WEIGHT_NAMES], *[new_v[n] for n in WEIGHT_NAMES])
```

```python
import functools

import jax
import jax.numpy as jnp
from jax import lax
from jax.experimental import pallas as pl
from jax.experimental.pallas import tpu as pltpu

F32 = jnp.float32
BF16 = jnp.bfloat16
EPS = 1e-6
NEG = -1e30
ADAM_LR = 0.001
ADAM_B1 = 0.9
ADAM_B2 = 0.999
ADAM_EPS = 1e-08
ADAM_WD = 0.01
ADAM_STEP = 10
POOL_WINDOWS = (2, 4, 8, 16)
HEAD_DIM = 64
N_HEADS = 8
LANES = 128
N_CHIPS = 4
ATT_BLOCK = 512
ATT_SUB = 128
VMEM_LIMIT = 56 * 1024 * 1024
MESH_AXES = ("x", "y", "c")
ANY = pl.BlockSpec(memory_space=pl.ANY)
VM = pl.BlockSpec(memory_space=pltpu.VMEM)


def _params(**kw):
    return pltpu.CompilerParams(vmem_limit_bytes=VMEM_LIMIT, **kw)


def _dot(a, b):
    return jnp.dot(a, b, preferred_element_type=F32)


def _dot_nt(a, b):
    return lax.dot_general(a, b, (((1,), (1,)), ((), ())), preferred_element_type=F32)


def _dot_tn(a, b):
    return lax.dot_general(a, b, (((0,), (0,)), ((), ())), preferred_element_type=F32)


def _sigmoid(z):
    return 1.0 / (1.0 + jnp.exp(-z))


def _rms(xf):
    return lax.rsqrt(jnp.mean(xf * xf, axis=-1, keepdims=True) + EPS)


def _rms_bwd(xf, r, gain, dh):
    xh = xf * r
    dyg = dh * gain
    return r * (dyg - xh * jnp.mean(dyg * xh, axis=-1, keepdims=True)), dh * xh


def _total(v):
    return jnp.sum(jnp.sum(v, axis=1, keepdims=True), axis=0, keepdims=True)


def _ffn_fwd(x, gain, wg, wu, wd, target=None):
    t, d = x.shape
    nch, fc, _ = wg.shape
    tm = min(512, t)
    nt = t // tm
    with_loss = target is not None

    def body(*refs):
        if with_loss:
            x_ref, g_ref, wg_ref, wu_ref, wd_ref, t_ref, o_ref, h_ref, a_ref, b_ref, s_ref, l_ref, acc_ref = refs
        else:
            x_ref, g_ref, wg_ref, wu_ref, wd_ref, o_ref, h_ref, a_ref, b_ref, s_ref, acc_ref = refs
        k = pl.program_id(1)

        @pl.when(k == 0)
        def _():
            xf = x_ref[...]
            h_ref[...] = ((xf * _rms(xf)) * g_ref[...]).astype(BF16)
            acc_ref[...] = jnp.zeros_like(acc_ref)

        h = h_ref[...]
        a = _dot_nt(h, wg_ref[...])
        b = _dot_nt(h, wu_ref[...])
        sb = ((a * _sigmoid(a)) * b).astype(BF16)
        a_ref[...] = a.astype(BF16)
        b_ref[...] = b.astype(BF16)
        s_ref[...] = sb
        acc_ref[...] += _dot(sb, wd_ref[...])

        @pl.when(k == nch - 1)
        def _():
            y = x_ref[...] + 0.5 * acc_ref[...]
            if with_loss:
                e = y - t_ref[...]
                o_ref[...] = e * (1.0 / d)
                l_ref[...] = jnp.broadcast_to(_total(e * e) * (0.5 / d), l_ref.shape)
            else:
                o_ref[...] = y

    row = pl.BlockSpec((tm, d), lambda i, k: (i, 0))
    chunk = pl.BlockSpec((None, fc, d), lambda i, k: (k, 0, 0))
    act = pl.BlockSpec((None, tm, fc), lambda i, k: (k, i, 0))
    in_specs = [row, pl.BlockSpec((1, d), lambda i, k: (0, 0)), chunk, chunk, chunk]
    out_shape = [jax.ShapeDtypeStruct((t, d), F32), jax.ShapeDtypeStruct((t, d), BF16)]
    out_shape += [jax.ShapeDtypeStruct((nch, t, fc), BF16)] * 3
    out_specs = [row, row, act, act, act]
    args = [x, gain, wg, wu, wd]
    if with_loss:
        in_specs.append(row)
        args.append(target)
        out_shape.append(jax.ShapeDtypeStruct((nt, 8, LANES), F32))
        out_specs.append(pl.BlockSpec((None, 8, LANES), lambda i, k: (i, 0, 0)))
    return pl.pallas_call(
        body, out_shape=out_shape, grid=(nt, nch), in_specs=in_specs, out_specs=out_specs,
        scratch_shapes=[pltpu.VMEM((tm, d), F32)], compiler_params=_params(),
        name="ffn_fwd_loss" if with_loss else "ffn_fwd",
    )(*args)


def _ffn_bwd_x(dy, x, gain, a, b, wg, wu, wd, name):
    t, d = x.shape
    nch, fc, _ = wg.shape
    tm = min(512, t)
    nt = t // tm

    def body(dy_ref, x_ref, g_ref, a_ref, b_ref, wg_ref, wu_ref, wd_ref, dx_ref, da_ref, db_ref, dg_ref, acc_ref):
        k = pl.program_id(1)

        @pl.when(k == 0)
        def _():
            acc_ref[...] = jnp.zeros_like(acc_ref)

        ds = 0.5 * _dot_nt(dy_ref[...].astype(BF16), wd_ref[...])
        av = a_ref[...].astype(F32)
        bv = b_ref[...].astype(F32)
        sig = _sigmoid(av)
        dab = (ds * bv * (sig * (1.0 + av * (1.0 - sig)))).astype(BF16)
        dbb = (ds * (av * sig)).astype(BF16)
        da_ref[...] = dab
        db_ref[...] = dbb
        acc_ref[...] += _dot(dab, wg_ref[...]) + _dot(dbb, wu_ref[...])

        @pl.when(k == nch - 1)
        def _():
            xf = x_ref[...]
            dxn, dgr = _rms_bwd(xf, _rms(xf), g_ref[...], acc_ref[...])
            dx_ref[...] = dy_ref[...] + dxn
            dg_ref[...] = jnp.sum(dgr, axis=0, keepdims=True)

    row = pl.BlockSpec((tm, d), lambda i, k: (i, 0))
    chunk = pl.BlockSpec((None, fc, d), lambda i, k: (k, 0, 0))
    act = pl.BlockSpec((None, tm, fc), lambda i, k: (k, i, 0))
    return pl.pallas_call(
        body,
        out_shape=[jax.ShapeDtypeStruct((t, d), F32), jax.ShapeDtypeStruct((nch, t, fc), BF16),
                   jax.ShapeDtypeStruct((nch, t, fc), BF16), jax.ShapeDtypeStruct((nt, 1, d), F32)],
        grid=(nt, nch),
        in_specs=[row, row, pl.BlockSpec((1, d), lambda i, k: (0, 0)), act, act, chunk, chunk, chunk],
        out_specs=[row, act, act, pl.BlockSpec((None, 1, d), lambda i, k: (i, 0, 0))],
        scratch_shapes=[pltpu.VMEM((tm, d), F32)], compiler_params=_params(), name=name,
    )(dy, x, gain, a, b, wg, wu, wd)


def _ffn_bwd_w(h, s, da, db, dy, name):
    t, d = h.shape
    nch, _, fc = s.shape
    tm = min(512, t)
    nt = t // tm

    def body(h_ref, s_ref, da_ref, db_ref, dy_ref, dwg_ref, dwu_ref, dwd_ref):
        @pl.when(pl.program_id(1) == 0)
        def _():
            dwg_ref[...] = jnp.zeros_like(dwg_ref)
            dwu_ref[...] = jnp.zeros_like(dwu_ref)
            dwd_ref[...] = jnp.zeros_like(dwd_ref)

        hv = h_ref[...]
        dwg_ref[...] += _dot_tn(da_ref[...], hv)
        dwu_ref[...] += _dot_tn(db_ref[...], hv)
        dwd_ref[...] += _dot_tn(s_ref[...], (0.5 * dy_ref[...]).astype(BF16))

    row = pl.BlockSpec((tm, d), lambda k, i: (i, 0))
    act = pl.BlockSpec((None, tm, fc), lambda k, i: (k, i, 0))
    chunk = pl.BlockSpec((None, fc, d), lambda k, i: (k, 0, 0))
    return pl.pallas_call(
        body, out_shape=[jax.ShapeDtypeStruct((nch, fc, d), F32)] * 3,
        grid=(nch, nt), in_specs=[row, act, act, act, row], out_specs=[chunk, chunk, chunk],
        compiler_params=_params(), name=name,
    )(h, s, da, db, dy)


def _head_masks():
    lane = lax.broadcasted_iota(jnp.int32, (1, LANES), 1)
    return lane < HEAD_DIM


def _head_rms(x, lo):
    x2 = x * x
    s0 = jnp.sum(jnp.where(lo, x2, 0.0), axis=1, keepdims=True)
    s1 = jnp.sum(jnp.where(lo, 0.0, x2), axis=1, keepdims=True)
    return jnp.where(lo, lax.rsqrt(s0 * (1.0 / HEAD_DIM) + EPS), lax.rsqrt(s1 * (1.0 / HEAD_DIM) + EPS))


def _head_mean(v, lo):
    s0 = jnp.sum(jnp.where(lo, v, 0.0), axis=1, keepdims=True)
    s1 = jnp.sum(jnp.where(lo, 0.0, v), axis=1, keepdims=True)
    return jnp.where(lo, s0, s1) * (1.0 / HEAD_DIM)


def _mix_proj(x1, gain, wt, qn, kn, pool_width, attn_width):
    t, d = x1.shape
    tm = min(512, t)
    nt = t // tm
    scale = HEAD_DIM ** -0.5
    c_q, c_k, c_v = pool_width, pool_width + attn_width, pool_width + 2 * attn_width
    c_f = c_v + attn_width

    def body(x_ref, g_ref, wt_ref, qn_ref, kn_ref, hm_ref, pv_ref, q_ref, k_ref, qh_ref, kh_ref, vb_ref, f_ref):
        xf = x_ref[...]
        hm = ((xf * _rms(xf)) * g_ref[...]).astype(BF16)
        hm_ref[...] = hm
        f_ref[...] = _dot_nt(hm, wt_ref[c_f:c_f + LANES, :])
        pv_ref[...] = _dot_nt(hm, wt_ref[0:pool_width, :])
        vb_ref[...] = _dot_nt(hm, wt_ref[c_v:c_v + attn_width, :]).astype(BF16)
        lo = _head_masks()
        for c0, raw_ref, hat_ref, n_ref, mul in ((c_q, q_ref, qh_ref, qn_ref, scale), (c_k, k_ref, kh_ref, kn_ref, 1.0)):
            raw = _dot_nt(hm, wt_ref[c0:c0 + attn_width, :])
            raw_ref[...] = raw
            for blk in range(attn_width // LANES):
                sl = slice(blk * LANES, (blk + 1) * LANES)
                xb = raw[:, sl]
                hat_ref[:, sl] = (((xb * _head_rms(xb, lo)) * n_ref[:, sl]) * mul).astype(BF16)

    row = pl.BlockSpec((tm, d), lambda i: (i, 0))
    half = pl.BlockSpec((tm, attn_width), lambda i: (i, 0))
    const = lambda shape: pl.BlockSpec(shape, lambda i: (0, 0))
    return pl.pallas_call(
        body,
        out_shape=[jax.ShapeDtypeStruct((t, d), BF16), jax.ShapeDtypeStruct((t, pool_width), F32),
                   jax.ShapeDtypeStruct((t, attn_width), F32), jax.ShapeDtypeStruct((t, attn_width), F32),
                   jax.ShapeDtypeStruct((t, attn_width), BF16), jax.ShapeDtypeStruct((t, attn_width), BF16),
                   jax.ShapeDtypeStruct((t, attn_width), BF16), jax.ShapeDtypeStruct((t, LANES), F32)],
        grid=(nt,),
        in_specs=[row, const((1, d)), const(wt.shape), const((1, attn_width)), const((1, attn_width))],
        out_specs=[row, pl.BlockSpec((tm, pool_width), lambda i: (i, 0)), half, half, half, half, half,
                   pl.BlockSpec((tm, LANES), lambda i: (i, 0))],
        compiler_params=_params(), name="mix_proj",
    )(x1, gain, wt, qn, kn)


def _shift_down(v, dist, row):
    return jnp.where(row >= dist, pltpu.roll(v, dist, 0), 0.0)


def _shift_up(v, dist, row, n):
    return jnp.where(row + dist < n, pltpu.roll(v, n - dist, 0), 0.0)


def _aug_lane(e):
    return HEAD_DIM if e == 0 else 0


def _forget_prefix(f, bias, qh, kh, n_batch, seq):
    def body(f_ref, b_ref, q_ref, k_ref, qa_ref, ka_ref):
        z = f_ref[...] + b_ref[...]
        acc = jnp.minimum(z, 0.0) - jnp.log(1.0 + jnp.exp(-jnp.abs(z)))
        row = lax.broadcasted_iota(jnp.int32, (seq, 1), 0)
        dist = 1
        while dist < seq:
            acc = acc + _shift_down(acc, dist, row)
            dist *= 2
        lane = lax.broadcasted_iota(jnp.int32, (1, LANES), 1)
        for h in range(N_HEADS):
            pair, e = divmod(h, 2)
            a0 = _aug_lane(e)
            own = (lane < HEAD_DIM) if e == 0 else (lane >= HEAD_DIM)
            fh = _pick_lane(acc, h)
            hi = fh.astype(BF16).astype(F32)
            rest = fh - hi
            mid = rest.astype(BF16).astype(F32)
            low = rest - mid
            q_ones = (lane >= a0 + 3) & (lane < a0 + 6)
            k_ones = (lane >= a0) & (lane < a0 + 3)
            q_aug = jnp.where(lane == a0, hi, jnp.where(lane == a0 + 1, mid, jnp.where(lane == a0 + 2, low,
                              jnp.where(q_ones, 1.0, 0.0))))
            k_aug = jnp.where(k_ones, 1.0, jnp.where(lane == a0 + 3, -hi, jnp.where(lane == a0 + 4, -mid,
                              jnp.where(lane == a0 + 5, -low, 0.0))))
            src = slice(pair * LANES, (pair + 1) * LANES)
            dst = slice(h * LANES, (h + 1) * LANES)
            qa_ref[:, dst] = jnp.where(own, q_ref[:, src].astype(F32), q_aug).astype(BF16)
            ka_ref[:, dst] = jnp.where(own, k_ref[:, src].astype(F32), k_aug).astype(BF16)

    width = qh.shape[1]
    tok = pl.BlockSpec((seq, width), lambda b: (b, 0))
    aug = pl.BlockSpec((seq, N_HEADS * LANES), lambda b: (b, 0))
    return pl.pallas_call(
        body, out_shape=[jax.ShapeDtypeStruct((n_batch * seq, N_HEADS * LANES), BF16)] * 2, grid=(n_batch,),
        in_specs=[pl.BlockSpec((seq, LANES), lambda b: (b, 0)), pl.BlockSpec((1, LANES), lambda b: (0, 0)), tok, tok],
        out_specs=[aug, aug], compiler_params=_params(), name="forget_prefix",
    )(f, bias, qh, kh)


def _pool_groups(pv_ref, pw_ref, ps_ref, seq):
    row = lax.broadcasted_iota(jnp.int32, (seq, 1), 0)
    pos = (row + 1).astype(F32)
    out = []
    for g, win in enumerate(POOL_WINDOWS):
        sl = slice(g * LANES, (g + 1) * LANES)
        xg = pv_ref[:, sl]
        acc = xg
        dist = 1
        while dist < win:
            acc = acc + _shift_down(acc, dist, row)
            dist *= 2
        pooled = (acc / jnp.minimum(pos, float(win)) - xg).astype(BF16)
        mixed = _dot(pooled, pw_ref[g])
        out.append((pooled, mixed, mixed * ps_ref[:, sl]))
    return out


def _pool_fwd(pv, pw, ps, onp, n_batch, seq):
    width = pv.shape[1]

    def body(pv_ref, pw_ref, ps_ref, on_ref, y_ref):
        groups = _pool_groups(pv_ref, pw_ref, ps_ref, seq)
        ssq = sum(jnp.sum(ms * ms, axis=1, keepdims=True) for _, _, ms in groups)
        r = lax.rsqrt(ssq * (1.0 / width) + EPS)
        for g, (_, _, ms) in enumerate(groups):
            sl = slice(g * LANES, (g + 1) * LANES)
            y_ref[:, sl] = ((ms * r) * on_ref[:, sl]).astype(BF16)

    return pl.pallas_call(
        body, out_shape=jax.ShapeDtypeStruct((n_batch * seq, width), BF16), grid=(n_batch,),
        in_specs=[pl.BlockSpec((seq, width), lambda b: (b, 0)), pl.BlockSpec(pw.shape, lambda b: (0, 0, 0)),
                  pl.BlockSpec((1, width), lambda b: (0, 0)), pl.BlockSpec((1, width), lambda b: (0, 0))],
        out_specs=pl.BlockSpec((seq, width), lambda b: (b, 0)),
        compiler_params=_params(), name="pool_fwd",
    )(pv, pw, ps, onp)


def _pool_bwd(pv, dyp, pw, ps, onp, n_batch, seq):
    width = pv.shape[1]

    def body(pv_ref, dy_ref, pw_ref, ps_ref, on_ref, dpv_ref, dpw_ref, dps_ref, don_ref):
        groups = _pool_groups(pv_ref, pw_ref, ps_ref, seq)
        ssq = sum(jnp.sum(ms * ms, axis=1, keepdims=True) for _, _, ms in groups)
        r = lax.rsqrt(ssq * (1.0 / width) + EPS)
        mean = sum(jnp.sum((dy_ref[:, g * LANES:(g + 1) * LANES] * on_ref[:, g * LANES:(g + 1) * LANES]) * (ms * r),
                           axis=1, keepdims=True) for g, (_, _, ms) in enumerate(groups)) * (1.0 / width)
        row = lax.broadcasted_iota(jnp.int32, (seq, 1), 0)
        pos = (row + 1).astype(F32)
        for g, (pooled, mixed, ms) in enumerate(groups):
            sl = slice(g * LANES, (g + 1) * LANES)
            dy = dy_ref[:, sl]
            xh = ms * r
            don_ref[:, sl] = jnp.sum(dy * xh, axis=0, keepdims=True)
            dms = r * (dy * on_ref[:, sl] - xh * mean)
            dps_ref[:, sl] = jnp.sum(dms * mixed, axis=0, keepdims=True)
            dmix = (dms * ps_ref[:, sl]).astype(BF16)
            dpw_ref[g] = _dot_tn(pooled, dmix)
            dpool = _dot_nt(dmix, pw_ref[g])
            win = POOL_WINDOWS[g]
            acc = dpool / jnp.minimum(pos, float(win))
            dist = 1
            while dist < win:
                acc = acc + _shift_up(acc, dist, row, seq)
                dist *= 2
            dpv_ref[:, sl] = (acc - dpool).astype(BF16)

    tok = pl.BlockSpec((seq, width), lambda b: (b, 0))
    vec = pl.BlockSpec((1, width), lambda b: (0, 0))
    pvec = pl.BlockSpec((None, 1, width), lambda b: (b, 0, 0))
    return pl.pallas_call(
        body,
        out_shape=[jax.ShapeDtypeStruct((n_batch * seq, width), BF16),
                   jax.ShapeDtypeStruct((n_batch,) + pw.shape, F32),
                   jax.ShapeDtypeStruct((n_batch, 1, width), F32), jax.ShapeDtypeStruct((n_batch, 1, width), F32)],
        grid=(n_batch,),
        in_specs=[tok, tok, pl.BlockSpec(pw.shape, lambda b: (0, 0, 0)), vec, vec],
        out_specs=[tok, pl.BlockSpec((None,) + pw.shape, lambda b: (b, 0, 0, 0)), pvec, pvec],
        compiler_params=_params(), name="pool_bwd",
    )(pv, dyp, pw, ps, onp)


def _pick_lane(tile, idx):
    lane = lax.broadcasted_iota(jnp.int32, (1, LANES), 1)
    return jnp.sum(jnp.where(lane == idx, tile, 0.0), axis=1, keepdims=True)


def _pick_row(tile, idx):
    sub = lax.broadcasted_iota(jnp.int32, (tile.shape[0], 1), 0)
    return jnp.sum(jnp.where(sub == idx, tile, 0.0), axis=0, keepdims=True)


def _put_lane(col, idx):
    lane = lax.broadcasted_iota(jnp.int32, (1, LANES), 1)
    return jnp.where(lane == idx, col, 0.0)


def _head_select(e):
    lo = _head_masks()
    return lo if e == 0 else jnp.logical_not(lo)


def _causal(st, shift):
    row = lax.broadcasted_iota(jnp.int32, st.shape, 0)
    col = lax.broadcasted_iota(jnp.int32, st.shape, 1) + shift
    return jnp.where(col >= row, st, NEG)


def _stat_rows(ref, head, nsub):
    return jnp.concatenate([_pick_row(ref[a], head) for a in range(nsub)], axis=1)


def _accumulate(ref, value, first):
    @pl.when(first)
    def _():
        ref[...] = value

    @pl.when(jnp.logical_not(first))
    def _():
        ref[...] += value


def _attn_fwd(qa, ka, vb, n_batch, seq):
    tq = min(ATT_BLOCK, seq)
    nq, nsub, tk = seq // tq, tq // ATT_SUB, tq
    pairs = vb.shape[1] // LANES

    def body(q_ref, k_ref, v_ref, o_ref, lse_ref, acc_ref):
        i, p = pl.program_id(1), pl.program_id(2)
        row_lo = lax.broadcasted_iota(jnp.int32, (LANES, 1), 0) < HEAD_DIM
        qs = [q_ref[:, e * LANES:(e + 1) * LANES] for e in range(2)]
        acc_ref[...] = jnp.zeros_like(acc_ref)

        def tile(off, stats, diagonal):
            vj = v_ref[pl.ds(off, tk), :]
            new, alphas, pvs = [], [], []
            for e in range(2):
                st = _dot_nt(k_ref[pl.ds(off, tk), e * LANES:(e + 1) * LANES], qs[e])
                if diagonal:
                    st = _causal(st, 0)
                m, l = stats[e]
                m_new = jnp.maximum(m, jnp.max(st, axis=0, keepdims=True))
                alpha = jnp.exp(m - m_new)
                pt = jnp.exp(st - m_new)
                new.append((m_new, alpha * l + jnp.sum(pt, axis=0, keepdims=True)))
                alphas.append(alpha)
                pvs.append(_dot_tn(jnp.where(_head_select(e), vj, jnp.zeros_like(vj)), pt.astype(BF16)))
            acc_ref[...] = acc_ref[...] * jnp.where(row_lo, alphas[0], alphas[1]) + (pvs[0] + pvs[1])
            return tuple(new)

        init = ((jnp.full((1, tq), NEG, F32), jnp.zeros((1, tq), F32)),) * 2
        stats = lax.fori_loop(0, i, lambda j, st: tile(pl.multiple_of(j * tk, tk), st, False), init)
        (m0, l0), (m1, l1) = tile(pl.multiple_of(i * tk, tk), stats, True)
        out_t = acc_ref[...] / jnp.where(row_lo, l0, l1)
        sub = lax.broadcasted_iota(jnp.int32, (8, 1), 0)
        lse0, lse1 = m0 + jnp.log(l0), m1 + jnp.log(l1)
        for a in range(nsub):
            sl = slice(a * ATT_SUB, (a + 1) * ATT_SUB)
            o_ref[sl, :] = out_t[:, sl].T
            rows = jnp.where(sub == 2 * p, lse0[:, sl], 0.0) + jnp.where(sub == 2 * p + 1, lse1[:, sl], 0.0)
            _accumulate(lse_ref.at[a], rows, p == 0)

    return pl.pallas_call(
        body,
        out_shape=[jax.ShapeDtypeStruct((n_batch * seq, pairs * LANES), F32),
                   jax.ShapeDtypeStruct((n_batch * seq // ATT_SUB, 8, ATT_SUB), F32)],
        grid=(n_batch, nq, pairs),
        in_specs=[pl.BlockSpec((tq, 2 * LANES), lambda b, i, p: (b * nq + i, p)),
                  pl.BlockSpec((seq, 2 * LANES), lambda b, i, p: (b, p)),
                  pl.BlockSpec((seq, LANES), lambda b, i, p: (b, p))],
        out_specs=[pl.BlockSpec((tq, LANES), lambda b, i, p: (b * nq + i, p)),
                   pl.BlockSpec((nsub, 8, ATT_SUB), lambda b, i, p: (b * nq + i, 0, 0))],
        scratch_shapes=[pltpu.VMEM((LANES, tq), F32)],
        compiler_params=_params(), name="attn_fwd",
    )(qa, ka, vb)


def _attn_bwd_q(qa, ka, vb, do, lse, delta, n_batch, seq):
    tq = min(ATT_BLOCK, seq)
    nq, nsub, tk = seq // tq, tq // ATT_SUB, tq
    pairs = vb.shape[1] // LANES

    def body(q_ref, k_ref, v_ref, do_ref, lse_ref, dl_ref, dq_ref, dfq_ref, acc0_ref, acc1_ref):
        i, p = pl.program_id(1), pl.program_id(2)
        accs = (acc0_ref, acc1_ref)
        qs = [q_ref[:, e * LANES:(e + 1) * LANES] for e in range(2)]
        dov = do_ref[...]
        ls = [_stat_rows(lse_ref, 2 * p + e, nsub) for e in range(2)]
        dl = [_stat_rows(dl_ref, 2 * p + e, nsub) for e in range(2)]
        for acc in accs:
            acc[...] = jnp.zeros_like(acc)

        def tile(off, diagonal):
            vj = v_ref[pl.ds(off, tk), :]
            for e in range(2):
                kj = k_ref[pl.ds(off, tk), e * LANES:(e + 1) * LANES]
                st = _dot_nt(kj, qs[e])
                if diagonal:
                    st = _causal(st, 0)
                pt = jnp.exp(st - ls[e])
                dpt = _dot_nt(jnp.where(_head_select(e), vj, jnp.zeros_like(vj)), dov)
                accs[e][...] += _dot_tn((pt * (dpt - dl[e])).astype(BF16), kj)

        def step(j, carry):
            tile(pl.multiple_of(j * tk, tk), False)
            return carry

        lax.fori_loop(0, i, step, 0)
        tile(pl.multiple_of(i * tk, tk), True)
        dq0, dq1 = acc0_ref[...], acc1_ref[...]
        dq_ref[...] = jnp.where(_head_masks(), dq0, dq1)
        dfq = _put_lane(_pick_lane(dq0, _aug_lane(0)), 2 * p) + _put_lane(_pick_lane(dq1, _aug_lane(1)), 2 * p + 1)
        _accumulate(dfq_ref, dfq, p == 0)

    stat = pl.BlockSpec((nsub, 8, ATT_SUB), lambda b, i, p: (b * nq + i, 0, 0))
    blk = pl.BlockSpec((tq, LANES), lambda b, i, p: (b * nq + i, p))
    return pl.pallas_call(
        body,
        out_shape=[jax.ShapeDtypeStruct((n_batch * seq, pairs * LANES), F32), jax.ShapeDtypeStruct((n_batch * seq, LANES), F32)],
        grid=(n_batch, nq, pairs),
        in_specs=[pl.BlockSpec((tq, 2 * LANES), lambda b, i, p: (b * nq + i, p)),
                  pl.BlockSpec((seq, 2 * LANES), lambda b, i, p: (b, p)),
                  pl.BlockSpec((seq, LANES), lambda b, i, p: (b, p)), blk, stat, stat],
        out_specs=[blk, pl.BlockSpec((tq, LANES), lambda b, i, p: (b * nq + i, 0))],
        scratch_shapes=[pltpu.VMEM((tq, LANES), F32), pltpu.VMEM((tq, LANES), F32)],
        compiler_params=_params(), name="attn_bwd_q",
    )(qa, ka, vb, do, lse, delta)


def _attn_bwd_kv(qa, ka, vb, do, lse, delta, n_batch, seq):
    tkb = min(ATT_BLOCK, seq)
    nk, nsub, tq = seq // tkb, tkb // ATT_SUB, tkb
    n_tiles = seq // ATT_SUB
    pairs = vb.shape[1] // LANES

    def body(q_ref, k_ref, v_ref, do_ref, lse_ref, dl_ref, dk_ref, dv_ref, dfk_ref, dk0_ref, dk1_ref, dva_ref):
        j, p = pl.program_id(1), pl.program_id(2)
        dks = (dk0_ref, dk1_ref)
        ks = [k_ref[:, e * LANES:(e + 1) * LANES] for e in range(2)]
        vj = v_ref[...]
        vs = [jnp.where(_head_select(e), vj, jnp.zeros_like(vj)) for e in range(2)]
        for acc in (dk0_ref, dk1_ref, dva_ref):
            acc[...] = jnp.zeros_like(acc)

        def tile(t, diagonal):
            off = pl.multiple_of(t * tq, tq)
            dov = do_ref[pl.ds(off, tq), :]
            for e in range(2):
                qe = q_ref[pl.ds(off, tq), e * LANES:(e + 1) * LANES]
                st = _dot_nt(ks[e], qe)
                if diagonal:
                    st = _causal(st, 0)
                rows = lambda ref: jnp.concatenate([_pick_row(ref[t * nsub + a], 2 * p + e) for a in range(nsub)], axis=1)
                pt = jnp.exp(st - rows(lse_ref))
                dva_ref[...] += _dot(pt.astype(BF16), jnp.where(_head_select(e), dov, jnp.zeros_like(dov)))
                dst = pt * (_dot_nt(vs[e], dov) - rows(dl_ref))
                dks[e][...] += _dot(dst.astype(BF16), qe)

        def step(t, carry):
            tile(t, False)
            return carry

        lax.fori_loop(j + 1, nk, step, 0)
        tile(j, True)
        dk0, dk1 = dk0_ref[...], dk1_ref[...]
        dk_ref[...] = jnp.where(_head_masks(), dk0, dk1)
        dv_ref[...] = dva_ref[...].astype(BF16)
        dfk = (_put_lane(_pick_lane(dk0, _aug_lane(0) + 3), 2 * p)
               + _put_lane(_pick_lane(dk1, _aug_lane(1) + 3), 2 * p + 1))
        _accumulate(dfk_ref, -dfk, p == 0)

    stat = pl.BlockSpec((n_tiles, 8, ATT_SUB), lambda b, j, p: (b, 0, 0))
    blk = pl.BlockSpec((tkb, LANES), lambda b, j, p: (b * nk + j, p))
    acc = pltpu.VMEM((tkb, LANES), F32)
    return pl.pallas_call(
        body,
        out_shape=[jax.ShapeDtypeStruct((n_batch * seq, pairs * LANES), F32),
                   jax.ShapeDtypeStruct((n_batch * seq, pairs * LANES), BF16),
                   jax.ShapeDtypeStruct((n_batch * seq, LANES), F32)],
        grid=(n_batch, nk, pairs),
        in_specs=[pl.BlockSpec((seq, 2 * LANES), lambda b, j, p: (b, p)),
                  pl.BlockSpec((tkb, 2 * LANES), lambda b, j, p: (b * nk + j, p)), blk,
                  pl.BlockSpec((seq, LANES), lambda b, j, p: (b, p)), stat, stat],
        out_specs=[blk, blk, pl.BlockSpec((tkb, LANES), lambda b, j, p: (b * nk + j, 0))],
        scratch_shapes=[acc, acc, acc],
        compiler_params=_params(), name="attn_bwd_kv",
    )(qa, ka, vb, do, lse, delta)


def _forget_bwd(dfq, dfk, f, bias, n_batch, seq):
    def body(dfq_ref, dfk_ref, f_ref, b_ref, df_ref, db_ref):
        acc = dfq_ref[...] + dfk_ref[...]
        row = lax.broadcasted_iota(jnp.int32, (seq, 1), 0)
        dist = 1
        while dist < seq:
            acc = acc + _shift_up(acc, dist, row, seq)
            dist *= 2
        df = acc * _sigmoid(-(f_ref[...] + b_ref[...]))
        df_ref[...] = df
        db_ref[...] = jnp.sum(df, axis=0, keepdims=True)

    col = pl.BlockSpec((seq, LANES), lambda b: (b, 0))
    return pl.pallas_call(
        body,
        out_shape=[jax.ShapeDtypeStruct((n_batch * seq, LANES), F32), jax.ShapeDtypeStruct((n_batch, 1, LANES), F32)],
        grid=(n_batch,), in_specs=[col, col, col, pl.BlockSpec((1, LANES), lambda b: (0, 0))],
        out_specs=[col, pl.BlockSpec((None, 1, LANES), lambda b: (b, 0, 0))],
        compiler_params=_params(), name="forget_bwd",
    )(dfq, dfk, f, bias)


def _mix_out(x1, yp, o, ona, woa, wob):
    t, d = x1.shape
    width = o.shape[1]
    tm = min(512, t)

    def body(x_ref, yp_ref, o_ref, on_ref, wa_ref, wb_ref, x2_ref, ya_ref):
        of = o_ref[...]
        ya = ((of * _rms(of)) * on_ref[...]).astype(BF16)
        ya_ref[...] = ya
        x2_ref[...] = x_ref[...] + (_dot(yp_ref[...], wa_ref[...]) + _dot(ya, wb_ref[...]))

    row = pl.BlockSpec((tm, d), lambda i: (i, 0))
    half = pl.BlockSpec((tm, width), lambda i: (i, 0))
    wspec = pl.BlockSpec((width, d), lambda i: (0, 0))
    return pl.pallas_call(
        body, out_shape=[jax.ShapeDtypeStruct((t, d), F32), jax.ShapeDtypeStruct((t, width), BF16)],
        grid=(t // tm,), in_specs=[row, half, half, pl.BlockSpec((1, width), lambda i: (0, 0)), wspec, wspec],
        out_specs=[row, half], compiler_params=_params(), name="mix_out",
    )(x1, yp, o, ona, woa, wob)


def _mix_out_bwd(dx2, o, yp, ya, ona, woa, wob):
    t, d = dx2.shape
    width = o.shape[1]
    tm = min(512, t)
    nt = t // tm

    def body(dx_ref, o_ref, yp_ref, ya_ref, on_ref, wa_ref, wb_ref, dyp_ref, do_ref, dl_ref, dwa_ref, dwb_ref, don_ref):
        @pl.when(pl.program_id(0) == 0)
        def _():
            dwa_ref[...] = jnp.zeros_like(dwa_ref)
            dwb_ref[...] = jnp.zeros_like(dwb_ref)

        dxb = dx_ref[...].astype(BF16)
        dwa_ref[...] += _dot_tn(yp_ref[...], dxb)
        dwb_ref[...] += _dot_tn(ya_ref[...], dxb)
        dyp_ref[...] = _dot_nt(dxb, wa_ref[...])
        of = o_ref[...]
        dov, dgr = _rms_bwd(of, _rms(of), on_ref[...], _dot_nt(dxb, wb_ref[...]))
        don_ref[...] = jnp.sum(dgr, axis=0, keepdims=True)
        do_ref[...] = dov.astype(BF16)
        lo = _head_masks()
        prod = dov * of
        delta = jnp.zeros((tm, LANES), F32)
        for blk in range(width // LANES):
            pb = prod[:, blk * LANES:(blk + 1) * LANES]
            delta = delta + _put_lane(jnp.sum(jnp.where(lo, pb, 0.0), axis=1, keepdims=True), 2 * blk)
            delta = delta + _put_lane(jnp.sum(jnp.where(lo, 0.0, pb), axis=1, keepdims=True), 2 * blk + 1)
        for c in range(tm // ATT_SUB):
            dl_ref[c] = delta[c * ATT_SUB:(c + 1) * ATT_SUB, :].T[0:8, :]

    row = pl.BlockSpec((tm, d), lambda i: (i, 0))
    half = pl.BlockSpec((tm, width), lambda i: (i, 0))
    wspec = pl.BlockSpec((width, d), lambda i: (0, 0))
    return pl.pallas_call(
        body,
        out_shape=[jax.ShapeDtypeStruct((t, width), F32), jax.ShapeDtypeStruct((t, width), BF16),
                   jax.ShapeDtypeStruct((t // ATT_SUB, 8, ATT_SUB), F32), jax.ShapeDtypeStruct((width, d), F32),
                   jax.ShapeDtypeStruct((width, d), F32), jax.ShapeDtypeStruct((nt, 1, width), F32)],
        grid=(nt,),
        in_specs=[row, half, half, half, pl.BlockSpec((1, width), lambda i: (0, 0)), wspec, wspec],
        out_specs=[half, half, pl.BlockSpec((tm // ATT_SUB, 8, ATT_SUB), lambda i: (i, 0, 0)), wspec, wspec,
                   pl.BlockSpec((None, 1, width), lambda i: (i, 0, 0))],
        compiler_params=_params(), name="mix_out_bwd",
    )(dx2, o, yp, ya, ona, woa, wob)


def _mix_in_bwd(dx2, x1, gain, hm, dpv, dqh, q, dkh, k, dv, df, qn, kn, wt):
    t, d = x1.shape
    width = q.shape[1]
    pool_width = dpv.shape[1]
    tm = min(512, t)
    nt = t // tm
    scale = HEAD_DIM ** -0.5
    c_q, c_k, c_v = pool_width, pool_width + width, pool_width + 2 * width
    c_f = c_v + width

    def body(dx2_ref, x_ref, g_ref, hm_ref, dpv_ref, dqh_ref, q_ref, dkh_ref, k_ref, dv_ref, df_ref, qn_ref, kn_ref,
             wt_ref, dx_ref, dwt_ref, dg_ref, dqn_ref, dkn_ref):
        @pl.when(pl.program_id(0) == 0)
        def _():
            dwt_ref[...] = jnp.zeros_like(dwt_ref)

        lo = _head_masks()
        hm = hm_ref[...]
        pieces = [(0, dpv_ref[...])]
        for c0, raw_ref, dh_ref, n_ref, dn_ref, mul in ((c_q, q_ref, dqh_ref, qn_ref, dqn_ref, scale),
                                                       (c_k, k_ref, dkh_ref, kn_ref, dkn_ref, 1.0)):
            cols = []
            for blk in range(width // LANES):
                sl = slice(blk * LANES, (blk + 1) * LANES)
                xb = raw_ref[:, sl]
                gb = dh_ref[:, sl] * mul
                r = _head_rms(xb, lo)
                xh = xb * r
                dyg = gb * n_ref[:, sl]
                cols.append((r * (dyg - xh * _head_mean(dyg * xh, lo))).astype(BF16))
                dn_ref[:, sl] = jnp.sum(gb * xh, axis=0, keepdims=True)
            pieces.append((c0, jnp.concatenate(cols, axis=1)))
        pieces.append((c_v, dv_ref[...]))
        pieces.append((c_f, df_ref[...].astype(BF16)))
        dhm = jnp.zeros((tm, d), F32)
        for c0, piece in pieces:
            dwt_ref[c0:c0 + piece.shape[1], :] += _dot_tn(piece, hm)
            dhm = dhm + _dot(piece, wt_ref[c0:c0 + piece.shape[1], :])
        xf = x_ref[...]
        dxn, dgr = _rms_bwd(xf, _rms(xf), g_ref[...], dhm)
        dx_ref[...] = dx2_ref[...] + dxn
        dg_ref[...] = jnp.sum(dgr, axis=0, keepdims=True)

    row = pl.BlockSpec((tm, d), lambda i: (i, 0))
    half = pl.BlockSpec((tm, width), lambda i: (i, 0))
    const = lambda shape: pl.BlockSpec(shape, lambda i: (0, 0))
    pvec = lambda n: pl.BlockSpec((None, 1, n), lambda i: (i, 0, 0))
    return pl.pallas_call(
        body,
        out_shape=[jax.ShapeDtypeStruct((t, d), F32), jax.ShapeDtypeStruct(wt.shape, F32),
                   jax.ShapeDtypeStruct((nt, 1, d), F32),
                   jax.ShapeDtypeStruct((nt, 1, width), F32), jax.ShapeDtypeStruct((nt, 1, width), F32)],
        grid=(nt,),
        in_specs=[row, row, const((1, d)), row, pl.BlockSpec((tm, pool_width), lambda i: (i, 0)), half, half, half, half,
                  half, pl.BlockSpec((tm, LANES), lambda i: (i, 0)), const((1, width)), const((1, width)),
                  const(wt.shape)],
        out_specs=[row, const(wt.shape), pvec(d), pvec(width), pvec(width)],
        compiler_params=_params(), name="mix_in_bwd",
    )(dx2, x1, gain, hm, dpv, dqh, q, dkh, k, dv, df, qn, kn, wt)


def _local_step(xf, tgt, small, full, n_batch, seq):
    pool_width = small["pool_scale"].shape[1]
    attn_width = small["out_norm_attn"].shape[1]
    x1, h1, a1, b1, s1 = _ffn_fwd(xf, small["ffn1_norm"], full["wg1"], full["wu1"], full["wd1"])
    hm, pv, q, k, qh, kh, vb, f = _mix_proj(x1, small["mix_norm"], full["w_in_t"], small["qn"], small["kn"],
                                            pool_width, attn_width)
    qa, ka = _forget_prefix(f, small["b_forget"], qh, kh, n_batch, seq)
    yp = _pool_fwd(pv, full["pool_w"], small["pool_scale"], small["out_norm_pool"], n_batch, seq)
    o, lse = _attn_fwd(qa, ka, vb, n_batch, seq)
    x2, ya = _mix_out(x1, yp, o, small["out_norm_attn"], full["woa"], full["wob"])
    dy, h2, a2, b2, s2, lpart = _ffn_fwd(x2, small["ffn2_norm"], full["wg2"], full["wu2"], full["wd2"], target=tgt)

    dx2, da2, db2, dg2 = _ffn_bwd_x(dy, x2, small["ffn2_norm"], a2, b2, full["wg2"], full["wu2"], full["wd2"], "ffn2_bwd_x")
    dwg2, dwu2, dwd2 = _ffn_bwd_w(h2, s2, da2, db2, dy, "ffn2_bwd_w")
    dyp, do, delta, dwoa, dwob, dona = _mix_out_bwd(dx2, o, yp, ya, small["out_norm_attn"], full["woa"], full["wob"])
    dpv, dpw, dps, donp = _pool_bwd(pv, dyp, full["pool_w"], small["pool_scale"], small["out_norm_pool"], n_batch, seq)
    dqh, dfq = _attn_bwd_q(qa, ka, vb, do, lse, delta, n_batch, seq)
    dkh, dv, dfk = _attn_bwd_kv(qa, ka, vb, do, lse, delta, n_batch, seq)
    df, dbf = _forget_bwd(dfq, dfk, f, small["b_forget"], n_batch, seq)
    dx1, dw_in_t, dgm, dqn, dkn = _mix_in_bwd(dx2, x1, small["mix_norm"], hm, dpv, dqh, q, dkh, k, dv, df,
                                              small["qn"], small["kn"], full["w_in_t"])
    gx, da1, db1, dg1 = _ffn_bwd_x(dx1, xf, small["ffn1_norm"], a1, b1, full["wg1"], full["wu1"], full["wd1"], "ffn1_bwd_x")
    dwg1, dwu1, dwd1 = _ffn_bwd_w(h1, s1, da1, db1, dx1, "ffn1_bwd_w")
    big = dict(wg1=dwg1, wu1=dwu1, wd1=dwd1, w_in_t=dw_in_t, woa=dwoa, wob=dwob, wg2=dwg2, wu2=dwu2, wd2=dwd2)
    part = dict(ffn1_norm=dg1, mix_norm=dgm, ffn2_norm=dg2, b_forget=dbf, pool_w=dpw, pool_scale=dps,
                out_norm_pool=donp, out_norm_attn=dona, qn=dqn, kn=dkn)
    return lpart, gx, big, part


def _mesh_pos():
    return lax.axis_index("x"), lax.axis_index("y"), lax.axis_index("c")


def _other_chips(x, y):
    return [(1 - x, y), (x, 1 - y), (1 - x, 1 - y)]


def _remote(src, dst, send_sem, recv_sem, device):
    return pltpu.make_async_remote_copy(src_ref=src, dst_ref=dst, send_sem=send_sem, recv_sem=recv_sem,
                                        device_id=device, device_id_type=pl.DeviceIdType.MESH)


def _half_rows(n_rows, which):
    half = n_rows // 2
    return pl.ds(pl.multiple_of(which * half, 8), half)


def _row_block(rows, cols, itemsize=4):
    rb = rows
    while rb * cols * itemsize > (1 << 20) and rb % 32 == 0:
        rb //= 2
    return rb


def _place_cast(w, chip, tag):
    rows, cols = w.shape
    rb = _row_block(rows, cols)

    def body(k_ref, w_ref, o_ref):
        o_ref[...] = w_ref[...].astype(BF16)

    return pl.pallas_call(
        body, out_shape=jax.ShapeDtypeStruct((N_CHIPS, rows, cols), BF16),
        grid_spec=pltpu.PrefetchScalarGridSpec(
            num_scalar_prefetch=1, grid=(rows // rb,),
            in_specs=[pl.BlockSpec((rb, cols), lambda i, k: (i, 0))],
            out_specs=pl.BlockSpec((None, rb, cols), lambda i, k: (k[0], i, 0))),
        compiler_params=_params(), name="place_" + tag,
    )(chip, w)


def _gather_weights(stacks):
    n = len(stacks)

    def body(*refs):
        outs = refs[n:2 * n]
        ici_send, ici_recv, d2d_send, d2d_recv = refs[2 * n:]
        x, y, c = _mesh_pos()
        mine = 2 * x + y
        sibling = (x, y, 1 - c)
        chips = _other_chips(x, y)
        slots = [2 * cx + cy for cx, cy in chips]
        sends = []
        for w in range(n):
            own = outs[w].at[mine, _half_rows(stacks[w].shape[1], c)]
            for j, chip in enumerate(chips):
                cp = _remote(own, own, ici_send.at[w, j], ici_recv.at[w, j], (*chip, c))
                cp.start()
                sends.append(cp)
        for w in range(n):
            rows = _half_rows(stacks[w].shape[1], c)
            for j in range(3):
                landed = outs[w].at[slots[j], rows]
                _remote(landed, landed, ici_send.at[w, j], ici_recv.at[w, j], sibling).wait_recv()
                cp = _remote(landed, landed, d2d_send.at[w, j], d2d_recv.at[w, j], sibling)
                cp.start()
                sends.append(cp)
        for w in range(n):
            rows = _half_rows(stacks[w].shape[1], 1 - c)
            for j in range(3):
                landed = outs[w].at[slots[j], rows]
                _remote(landed, landed, d2d_send.at[w, j], d2d_recv.at[w, j], sibling).wait_recv()
        for cp in sends:
            cp.wait_send()

    return pl.pallas_call(
        body, out_shape=[jax.ShapeDtypeStruct(s.shape, s.dtype) for s in stacks],
        in_specs=[ANY] * n, out_specs=[ANY] * n, input_output_aliases={w: w for w in range(n)},
        scratch_shapes=[pltpu.SemaphoreType.DMA((n, 3)), pltpu.SemaphoreType.DMA((n, 3)),
                        pltpu.SemaphoreType.DMA((n, 3)), pltpu.SemaphoreType.DMA((n, 3))],
        name="gather_weights",
    )(*stacks)


def _sibling_halves(gs):
    n = len(gs)

    def body(*refs):
        ins, outs = refs[:n], refs[n:2 * n]
        send, recv = refs[2 * n:]
        x, y, c = _mesh_pos()
        cps = []
        for w in range(n):
            cp = _remote(ins[w].at[:, _half_rows(gs[w].shape[1], 1 - c), :], outs[w], send.at[w], recv.at[w], (x, y, 1 - c))
            cp.start()
            cps.append(cp)
        for cp in cps:
            cp.wait()

    return pl.pallas_call(
        body, out_shape=[jax.ShapeDtypeStruct((g.shape[0], g.shape[1] // 2, g.shape[2]), g.dtype) for g in gs],
        in_specs=[ANY] * n, out_specs=[ANY] * n,
        scratch_shapes=[pltpu.SemaphoreType.DMA((n,)), pltpu.SemaphoreType.DMA((n,))],
        name="sibling_halves",
    )(*gs)


def _chip_exchange(ps):
    n = len(ps)

    def body(*refs):
        ins, outs = refs[:n], refs[n:2 * n]
        send, recv = refs[2 * n:]
        x, y, c = _mesh_pos()
        chips = _other_chips(x, y)
        cps = []
        for w in range(n):
            for j, (cx, cy) in enumerate(chips):
                cp = _remote(ins[w].at[2 * cx + cy], outs[w].at[j], send.at[w, j], recv.at[w, j], (cx, cy, c))
                cp.start()
                cps.append(cp)
        for cp in cps:
            cp.wait()

    return pl.pallas_call(
        body, out_shape=[jax.ShapeDtypeStruct((3,) + p.shape[1:], p.dtype) for p in ps],
        in_specs=[ANY] * n, out_specs=[ANY] * n,
        scratch_shapes=[pltpu.SemaphoreType.DMA((n, 3)), pltpu.SemaphoreType.DMA((n, 3))],
        name="chip_exchange",
    )(*ps)


def _sibling_share(gs):
    n = len(gs)

    def body(*refs):
        outs = refs[n:2 * n]
        send, recv = refs[2 * n:]
        x, y, c = _mesh_pos()
        cps = []
        for w in range(n):
            mine = outs[w].at[_half_rows(gs[w].shape[0], c)]
            cp = _remote(mine, mine, send.at[w], recv.at[w], (x, y, 1 - c))
            cp.start()
            cps.append(cp)
        for w, cp in enumerate(cps):
            cp.wait_send()
            theirs = outs[w].at[_half_rows(gs[w].shape[0], 1 - c)]
            _remote(theirs, theirs, send.at[w], recv.at[w], (x, y, 1 - c)).wait_recv()

    return pl.pallas_call(
        body, out_shape=[jax.ShapeDtypeStruct(g.shape, g.dtype) for g in gs],
        in_specs=[ANY] * n, out_specs=[ANY] * n, input_output_aliases={w: w for w in range(n)},
        scratch_shapes=[pltpu.SemaphoreType.DMA((n,)), pltpu.SemaphoreType.DMA((n,))],
        name="sibling_share",
    )(*gs)


def _add_sibling(g, r1, ids, tag):
    nch, rh, cols = r1.shape

    def body(ids_ref, g_ref, r_ref, o_ref):
        o_ref[...] = (g_ref[...] + r_ref[...]).astype(BF16)

    blk = lambda fn: pl.BlockSpec((None, rh, cols), fn)
    return pl.pallas_call(
        body, out_shape=jax.ShapeDtypeStruct(r1.shape, BF16),
        grid_spec=pltpu.PrefetchScalarGridSpec(
            num_scalar_prefetch=1, grid=(nch,),
            in_specs=[blk(lambda k, ids: (k, ids[1], 0)), blk(lambda k, ids: (k, 0, 0))],
            out_specs=blk(lambda k, ids: (k, 0, 0))),
        compiler_params=_params(), name="add_sibling_" + tag,
    )(ids, g, r1)


def _add_chips(g, r1, r2, ids, tag):
    _, rh, cols = r1.shape

    def body(ids_ref, g_ref, r1_ref, r2_ref, o_ref):
        own = g_ref[...] + r1_ref[...]
        o_ref[...] = ((own + r2_ref[0].astype(F32)) + r2_ref[1].astype(F32)) + r2_ref[2].astype(F32)

    return pl.pallas_call(
        body, out_shape=jax.ShapeDtypeStruct((2 * rh, cols), F32),
        grid_spec=pltpu.PrefetchScalarGridSpec(
            num_scalar_prefetch=1, grid=(1,),
            in_specs=[pl.BlockSpec((None, rh, cols), lambda i, ids: (ids[0], ids[1], 0)),
                      pl.BlockSpec((None, rh, cols), lambda i, ids: (ids[0], 0, 0)),
                      pl.BlockSpec((3, rh, cols), lambda i, ids: (0, 0, 0))],
            out_specs=pl.BlockSpec((rh, cols), lambda i, ids: (ids[1], 0))),
        compiler_params=_params(), name="add_chips_" + tag,
    )(ids, g, r1, r2)


def _reduce_to_owner(gs, ids, tags):
    r1 = _sibling_halves(gs)
    ps = [_add_sibling(g, r, ids, t) for g, r, t in zip(gs, r1, tags)]
    r2 = _chip_exchange(ps)
    return _sibling_share([_add_chips(g, ra, rb, ids, t) for g, ra, rb, t in zip(gs, r1, r2, tags)])


VEC_ROWS = 8


def _small_allreduce(part, d, width):
    names = ("ffn1_norm", "mix_norm", "ffn2_norm", "pool_scale", "out_norm_pool", "out_norm_attn", "qn", "kn", "b_forget", "pool_w")
    args = [part[k] for k in names]
    pw_shape = part["pool_w"].shape[1:]
    n_dev = 8

    def body(g1_ref, gm_ref, g2_ref, ps_ref, onp_ref, ona_ref, qn_ref, kn_ref, bf_ref, pw_ref,
             vec_ref, pwo_ref, vbuf, pbuf, send, recv):
        x, y, c = _mesh_pos()
        me = 4 * x + 2 * y + c
        lo = _head_masks()

        def fold_heads(ref):
            v = jnp.sum(ref[...], axis=0)
            acc = jnp.zeros((VEC_ROWS, LANES), F32)
            for blk in range(width // LANES):
                vb = jnp.broadcast_to(v[:, blk * LANES:(blk + 1) * LANES], (VEC_ROWS, LANES))
                acc = acc + vb + pltpu.roll(vb, HEAD_DIM, 1)
            return jnp.where(lo, acc, 0.0)[0:1, :]

        vbuf[0] = jnp.zeros((VEC_ROWS, d), F32)
        vbuf[0, 0:1, :] = jnp.sum(g1_ref[...], axis=0)
        vbuf[0, 1:2, :] = jnp.sum(gm_ref[...], axis=0)
        vbuf[0, 2:3, :] = jnp.sum(g2_ref[...], axis=0)
        vbuf[0, 3:4, 0:width] = jnp.sum(ps_ref[...], axis=0)
        vbuf[0, 3:4, width:2 * width] = jnp.sum(onp_ref[...], axis=0)
        vbuf[0, 4:5, 0:width] = jnp.sum(ona_ref[...], axis=0)
        vbuf[0, 4:5, width:width + LANES] = fold_heads(qn_ref)
        vbuf[0, 4:5, width + LANES:width + 2 * LANES] = fold_heads(kn_ref)
        vbuf[0, 4:5, width + 2 * LANES:width + 3 * LANES] = jnp.sum(bf_ref[...], axis=0)
        pbuf[0] = jnp.sum(pw_ref[...], axis=0)

        cps = []
        for r in range(1, n_dev):
            peer = (x if not r & 4 else 1 - x, y if not r & 2 else 1 - y, c if not r & 1 else 1 - c)
            for buf, k in ((vbuf, 0), (pbuf, 1)):
                cp = _remote(buf.at[0], buf.at[r], send.at[k, r - 1], recv.at[k, r - 1], peer)
                cp.start()
                cps.append(cp)
        for cp in cps:
            cp.wait()
        vec = vbuf[me]
        pw = pbuf[me]
        for dev in range(1, n_dev):
            vec = vec + vbuf[jnp.bitwise_xor(me, dev)]
            pw = pw + pbuf[jnp.bitwise_xor(me, dev)]
        vec_ref[...] = vec
        pwo_ref[...] = pw

    return pl.pallas_call(
        body, out_shape=[jax.ShapeDtypeStruct((VEC_ROWS, d), F32), jax.ShapeDtypeStruct(pw_shape, F32)],
        in_specs=[VM] * len(args), out_specs=[VM, VM],
        scratch_shapes=[pltpu.VMEM((n_dev, VEC_ROWS, d), F32), pltpu.VMEM((n_dev,) + pw_shape, F32),
                        pltpu.SemaphoreType.DMA((2, n_dev - 1)), pltpu.SemaphoreType.DMA((2, n_dev - 1))],
        compiler_params=_params(), name="small_allreduce",
    )(*args)


def _adamw(w, g, m, v, tag):
    rows, cols = w.shape
    rb = rows
    while rb * cols * 4 > (1 << 20) and rb % 16 == 0:
        rb //= 2

    def body(w_ref, g_ref, m_ref, v_ref, d_ref, mo_ref, vo_ref):
        gv = g_ref[...]
        m2 = ADAM_B1 * m_ref[...] + (1.0 - ADAM_B1) * gv
        v2 = ADAM_B2 * v_ref[...] + (1.0 - ADAM_B2) * (gv * gv)
        m_hat = m2 / (1.0 - ADAM_B1 ** ADAM_STEP)
        v_hat = v2 / (1.0 - ADAM_B2 ** ADAM_STEP)
        d_ref[...] = -ADAM_LR * (m_hat / (jnp.sqrt(v_hat) + ADAM_EPS) + ADAM_WD * w_ref[...])
        mo_ref[...] = m2
        vo_ref[...] = v2

    spec = pl.BlockSpec((rb, cols), lambda i: (i, 0))
    return pl.pallas_call(
        body, out_shape=[jax.ShapeDtypeStruct(w.shape, F32)] * 3, grid=(rows // rb,),
        in_specs=[spec] * 4, out_specs=[spec] * 3, compiler_params=_params(), name="adamw_" + tag,
    )(w, g, m, v)


def _pack_vec(p, d, width):
    pad = lambda v: jnp.pad(v, (0, LANES - v.shape[0]))
    row3 = jnp.concatenate([p["pool_scale"], p["out_norm_pool"]])
    row4 = jnp.concatenate([p["out_norm_attn"], pad(p["q_norm"]), pad(p["k_norm"]), pad(p["b_forget"]),
                            jnp.zeros((d - width - 3 * LANES,), F32)])
    rows = [p["ffn1_norm"], p["mix_norm"], p["ffn2_norm"], row3, row4]
    return jnp.pad(jnp.stack(rows), ((0, VEC_ROWS - len(rows)), (0, 0)))


def _unpack_vec(vec, width):
    return dict(ffn1_norm=vec[0], mix_norm=vec[1], ffn2_norm=vec[2], pool_scale=vec[3, :width],
                out_norm_pool=vec[3, width:2 * width], out_norm_attn=vec[4, :width],
                q_norm=vec[4, width:width + HEAD_DIM], k_norm=vec[4, width + LANES:width + LANES + HEAD_DIM],
                b_forget=vec[4, width + 2 * LANES:width + 2 * LANES + N_HEADS])


WEIGHT_NAMES = ("ffn1_norm", "ffn1_w_gate", "ffn1_w_up", "ffn1_w_down", "mix_norm", "w_in", "b_forget", "pool_w",
                "pool_scale", "q_norm", "k_norm", "out_norm_pool", "out_norm_attn", "w_out", "ffn2_norm",
                "ffn2_w_gate", "ffn2_w_up", "ffn2_w_down")
BIG_NAMES = ("ffn1_w_gate", "ffn1_w_up", "ffn1_w_down", "w_in", "w_out", "ffn2_w_gate", "ffn2_w_up", "ffn2_w_down")
TRANSPOSED_NAMES = ("ffn1_w_gate", "ffn1_w_up", "w_in", "ffn2_w_gate", "ffn2_w_up")


def kernel(x, ffn1_norm, ffn1_w_gate, ffn1_w_up, ffn1_w_down, mix_norm, w_in, b_forget, pool_w, pool_scale, q_norm, k_norm, out_norm_pool, out_norm_attn, w_out, ffn2_norm, ffn2_w_gate, ffn2_w_up, ffn2_w_down, loss_target, m_ffn1_norm, m_ffn1_w_gate, m_ffn1_w_up, m_ffn1_w_down, m_mix_norm, m_w_in, m_b_forget, m_pool_w, m_pool_scale, m_q_norm, m_k_norm, m_out_norm_pool, m_out_norm_attn, m_w_out, m_ffn2_norm, m_ffn2_w_gate, m_ffn2_w_up, m_ffn2_w_down, v_ffn1_norm, v_ffn1_w_gate, v_ffn1_w_up, v_ffn1_w_down, v_mix_norm, v_w_in, v_b_forget, v_pool_w, v_pool_scale, v_q_norm, v_k_norm, v_out_norm_pool, v_out_norm_attn, v_w_out, v_ffn2_norm, v_ffn2_w_gate, v_ffn2_w_up, v_ffn2_w_down):
    given = dict(locals())
    w = {n: given[n] for n in WEIGHT_NAMES}
    m = {n: given["m_" + n] for n in WEIGHT_NAMES}
    v = {n: given["v_" + n] for n in WEIGHT_NAMES}
    n_batch, seq, d = x.shape
    width = pool_scale.shape[0]
    in_rows = w_in.shape[1]
    in_cols = N_CHIPS * in_rows
    in_pad = -(-in_rows // 32) * 32
    in_cols_pad = in_cols - N_HEADS + LANES

    work = lambda a, n: a.T if n in TRANSPOSED_NAMES else a
    exchanged = lambda a, n: jnp.pad(a, ((0, in_pad - in_rows), (0, 0))) if n == "w_in" else a

    mesh_x, mesh_y, mesh_c = _mesh_pos()
    ids = jnp.stack([2 * mesh_x + mesh_y, mesh_c]).astype(jnp.int32)

    gathered = dict(zip(BIG_NAMES, _gather_weights([_place_cast(exchanged(work(w[n], n), n), ids, n) for n in BIG_NAMES])))
    w_in_t = jnp.pad(gathered["w_in"][:, :in_rows].reshape(in_cols, d), ((0, in_cols_pad - in_cols), (0, 0)))
    w_out_full = gathered["w_out"].reshape(N_CHIPS * w_out.shape[0], d)
    full = dict(wg1=gathered["ffn1_w_gate"], wu1=gathered["ffn1_w_up"], wd1=gathered["ffn1_w_down"],
                wg2=gathered["ffn2_w_gate"], wu2=gathered["ffn2_w_up"], wd2=gathered["ffn2_w_down"],
                w_in_t=w_in_t, woa=w_out_full[:width], wob=w_out_full[width:], pool_w=pool_w.astype(BF16))
    row = lambda a: a.reshape(1, -1)
    small = dict(ffn1_norm=row(ffn1_norm), mix_norm=row(mix_norm), ffn2_norm=row(ffn2_norm), pool_scale=row(pool_scale),
                 out_norm_pool=row(out_norm_pool), out_norm_attn=row(out_norm_attn),
                 qn=row(jnp.tile(q_norm, N_HEADS)), kn=row(jnp.tile(k_norm, N_HEADS)),
                 b_forget=row(jnp.pad(b_forget, (0, LANES - N_HEADS))))

    lpart, gx, big, part = _local_step(x.reshape(n_batch * seq, d), loss_target.reshape(n_batch * seq, d),
                                       small, full, n_batch, seq)
    loss = lax.psum(jnp.sum(lpart[:, 0, 0]), MESH_AXES)

    d_w_in = jnp.pad(big["w_in_t"][:in_cols].reshape(N_CHIPS, in_rows, d), ((0, 0), (0, in_pad - in_rows), (0, 0)))
    d_w_out = jnp.concatenate([big["woa"], big["wob"]], axis=0).reshape(N_CHIPS, w_out.shape[0], d)
    stacks = dict(ffn1_w_gate=big["wg1"], ffn1_w_up=big["wu1"], ffn1_w_down=big["wd1"], w_in=d_w_in, w_out=d_w_out,
                  ffn2_w_gate=big["wg2"], ffn2_w_up=big["wu2"], ffn2_w_down=big["wd2"])
    reduced = dict(zip(BIG_NAMES, _reduce_to_owner([stacks[n] for n in BIG_NAMES], ids, BIG_NAMES)))
    reduced["w_in"] = reduced["w_in"][:in_rows]

    part = dict(part, pool_w=part["pool_w"].reshape(n_batch, -1, pool_w.shape[-1]))
    g_vec, g_pw = _small_allreduce(part, d, width)
    grads, delta, new_m, new_v = {}, {}, {}, {}
    for n in BIG_NAMES:
        stepped = _adamw(work(w[n], n), reduced[n], work(m[n], n), work(v[n], n), n)
        grads[n], delta[n], new_m[n], new_v[n] = (work(a, n) for a in (reduced[n], *stepped))
    flat_pw = lambda a: a.reshape(-1, a.shape[-1])
    d_pw, m_pw, v_pw = _adamw(flat_pw(pool_w), g_pw, flat_pw(m_pool_w), flat_pw(v_pool_w), "pool_w")
    d_vec, m_vec, v_vec = _adamw(_pack_vec(w, d, width), g_vec, _pack_vec(m, d, width), _pack_vec(v, d, width), "vectors")
    grads.update(_unpack_vec(g_vec, width), pool_w=g_pw.reshape(pool_w.shape))
    delta.update(_unpack_vec(d_vec, width), pool_w=d_pw.reshape(pool_w.shape))
    new_m.update(_unpack_vec(m_vec, width), pool_w=m_pw.reshape(pool_w.shape))
    new_v.update(_unpack_vec(v_vec, width), pool_w=v_pw.reshape(pool_w.shape))
    return (loss, gx.reshape(x.shape), *[grads[n] for n in WEIGHT_NAMES], *[delta[n] for n in WEIGHT_NAMES],
            *[new_m[n] for n in WEIGHT_NAMES], *[new_v[n] for n in WEIGHT_NAMES])
```

```python
import functools

import jax
import jax.numpy as jnp
from jax import lax
from jax.experimental import pallas as pl
from jax.experimental.pallas import tpu as pltpu

F32 = jnp.float32
BF16 = jnp.bfloat16
EPS = 1e-6
NEG = -1e30
ADAM_LR = 0.001
ADAM_B1 = 0.9
ADAM_B2 = 0.999
ADAM_EPS = 1e-08
ADAM_WD = 0.01
ADAM_STEP = 10
POOL_WINDOWS = (2, 4, 8, 16)
HEAD_DIM = 64
N_HEADS = 8
LANES = 128
N_CHIPS = 4
ATT_BLOCK = 512
ATT_SUB = 128
VMEM_LIMIT = 56 * 1024 * 1024
MESH_AXES = ("x", "y", "c")
ANY = pl.BlockSpec(memory_space=pl.ANY)
VM = pl.BlockSpec(memory_space=pltpu.VMEM)


def _params(**kw):
    return pltpu.CompilerParams(vmem_limit_bytes=VMEM_LIMIT, **kw)


def _dot(a, b):
    return jnp.dot(a, b, preferred_element_type=F32)


def _dot_nt(a, b):
    return lax.dot_general(a, b, (((1,), (1,)), ((), ())), preferred_element_type=F32)


def _dot_tn(a, b):
    return lax.dot_general(a, b, (((0,), (0,)), ((), ())), preferred_element_type=F32)


def _sigmoid(z):
    return 1.0 / (1.0 + jnp.exp(-z))


def _rms(xf):
    return lax.rsqrt(jnp.mean(xf * xf, axis=-1, keepdims=True) + EPS)


def _rms_bwd(xf, r, gain, dh):
    xh = xf * r
    dyg = dh * gain
    return r * (dyg - xh * jnp.mean(dyg * xh, axis=-1, keepdims=True)), dh * xh


def _total(v):
    return jnp.sum(jnp.sum(v, axis=1, keepdims=True), axis=0, keepdims=True)


def _ffn_fwd(x, gain, wg, wu, wd, target=None):
    t, d = x.shape
    nch, fc, _ = wg.shape
    tm = min(512, t)
    nt = t // tm
    with_loss = target is not None

    def body(*refs):
        if with_loss:
            x_ref, g_ref, wg_ref, wu_ref, wd_ref, t_ref, o_ref, h_ref, a_ref, b_ref, s_ref, l_ref, acc_ref = refs
        else:
            x_ref, g_ref, wg_ref, wu_ref, wd_ref, o_ref, h_ref, a_ref, b_ref, s_ref, acc_ref = refs
        k = pl.program_id(1)

        @pl.when(k == 0)
        def _():
            xf = x_ref[...]
            h_ref[...] = ((xf * _rms(xf)) * g_ref[...]).astype(BF16)
            acc_ref[...] = jnp.zeros_like(acc_ref)

        h = h_ref[...]
        a = _dot_nt(h, wg_ref[...])
        b = _dot_nt(h, wu_ref[...])
        sb = ((a * (0.5 * jnp.tanh(0.5 * a) + 0.5)) * b).astype(BF16)
        a_ref[...] = a.astype(BF16)
        b_ref[...] = b.astype(BF16)
        s_ref[...] = sb
        acc_ref[...] += _dot(sb, wd_ref[...])

        @pl.when(k == nch - 1)
        def _():
            y = x_ref[...] + 0.5 * acc_ref[...]
            if with_loss:
                e = y - t_ref[...]
                o_ref[...] = e * (1.0 / d)
                l_ref[...] = jnp.broadcast_to(_total(e * e) * (0.5 / d), l_ref.shape)
            else:
                o_ref[...] = y

    row = pl.BlockSpec((tm, d), lambda i, k: (i, 0))
    chunk = pl.BlockSpec((None, fc, d), lambda i, k: (k, 0, 0))
    act = pl.BlockSpec((None, tm, fc), lambda i, k: (k, i, 0))
    in_specs = [row, pl.BlockSpec((1, d), lambda i, k: (0, 0)), chunk, chunk, chunk]
    out_shape = [jax.ShapeDtypeStruct((t, d), F32), jax.ShapeDtypeStruct((t, d), BF16)]
    out_shape += [jax.ShapeDtypeStruct((nch, t, fc), BF16)] * 3
    out_specs = [row, row, act, act, act]
    args = [x, gain, wg, wu, wd]
    if with_loss:
        in_specs.append(row)
        args.append(target)
        out_shape.append(jax.ShapeDtypeStruct((nt, 8, LANES), F32))
        out_specs.append(pl.BlockSpec((None, 8, LANES), lambda i, k: (i, 0, 0)))
    return pl.pallas_call(
        body, out_shape=out_shape, grid=(nt, nch), in_specs=in_specs, out_specs=out_specs,
        scratch_shapes=[pltpu.VMEM((tm, d), F32)], compiler_params=_params(),
        name="ffn_fwd_loss" if with_loss else "ffn_fwd",
    )(*args)


def _ffn_bwd_x(dy, x, gain, a, b, wg, wu, wd, name):
    t, d = x.shape
    nch, fc, _ = wg.shape
    tm = min(512, t)
    nt = t // tm

    def body(dy_ref, x_ref, g_ref, a_ref, b_ref, wg_ref, wu_ref, wd_ref, dx_ref, da_ref, db_ref, dg_ref, acc_ref):
        k = pl.program_id(1)

        @pl.when(k == 0)
        def _():
            acc_ref[...] = jnp.zeros_like(acc_ref)

        ds = _dot_nt(dy_ref[...].astype(BF16), wd_ref[...])
        av = a_ref[...].astype(F32)
        bv = b_ref[...].astype(F32)
        th = jnp.tanh(0.5 * av)
        half_sig = 0.25 * th + 0.25
        dab = ((ds * bv) * (half_sig * (1.0 + av * (0.5 - 0.5 * th)))).astype(BF16)
        dbb = (ds * (av * half_sig)).astype(BF16)
        da_ref[...] = dab
        db_ref[...] = dbb
        acc_ref[...] += _dot(dab, wg_ref[...]) + _dot(dbb, wu_ref[...])

        @pl.when(k == nch - 1)
        def _():
            xf = x_ref[...]
            dxn, dgr = _rms_bwd(xf, _rms(xf), g_ref[...], acc_ref[...])
            dx_ref[...] = dy_ref[...] + dxn
            dg_ref[...] = jnp.sum(dgr, axis=0, keepdims=True)

    row = pl.BlockSpec((tm, d), lambda i, k: (i, 0))
    chunk = pl.BlockSpec((None, fc, d), lambda i, k: (k, 0, 0))
    act = pl.BlockSpec((None, tm, fc), lambda i, k: (k, i, 0))
    return pl.pallas_call(
        body,
        out_shape=[jax.ShapeDtypeStruct((t, d), F32), jax.ShapeDtypeStruct((nch, t, fc), BF16),
                   jax.ShapeDtypeStruct((nch, t, fc), BF16), jax.ShapeDtypeStruct((nt, 1, d), F32)],
        grid=(nt, nch),
        in_specs=[row, row, pl.BlockSpec((1, d), lambda i, k: (0, 0)), act, act, chunk, chunk, chunk],
        out_specs=[row, act, act, pl.BlockSpec((None, 1, d), lambda i, k: (i, 0, 0))],
        scratch_shapes=[pltpu.VMEM((tm, d), F32)], compiler_params=_params(), name=name,
    )(dy, x, gain, a, b, wg, wu, wd)


def _ffn_bwd_w(h, s, da, db, dy, name):
    t, d = h.shape
    nch, _, fc = s.shape
    tm = min(1024, t)
    nt = t // tm

    def body(h_ref, s_ref, da_ref, db_ref, dy_ref, dwg_ref, dwu_ref, dwd_ref):
        @pl.when(pl.program_id(1) == 0)
        def _():
            dwg_ref[...] = jnp.zeros_like(dwg_ref)
            dwu_ref[...] = jnp.zeros_like(dwu_ref)
            dwd_ref[...] = jnp.zeros_like(dwd_ref)

        hv = h_ref[...]
        dwg_ref[...] += _dot_tn(da_ref[...], hv)
        dwu_ref[...] += _dot_tn(db_ref[...], hv)
        dwd_ref[...] += _dot_tn(s_ref[...], (0.5 * dy_ref[...]).astype(BF16))

    row = pl.BlockSpec((tm, d), lambda k, i: (i, 0))
    act = pl.BlockSpec((None, tm, fc), lambda k, i: (k, i, 0))
    chunk = pl.BlockSpec((None, fc, d), lambda k, i: (k, 0, 0))
    return pl.pallas_call(
        body, out_shape=[jax.ShapeDtypeStruct((nch, fc, d), F32)] * 3,
        grid=(nch, nt), in_specs=[row, act, act, act, row], out_specs=[chunk, chunk, chunk],
        compiler_params=_params(), name=name,
    )(h, s, da, db, dy)


def _head_masks():
    lane = lax.broadcasted_iota(jnp.int32, (1, LANES), 1)
    return lane < HEAD_DIM


def _head_rms(x, lo):
    x2 = x * x
    s0 = jnp.sum(jnp.where(lo, x2, 0.0), axis=1, keepdims=True)
    s1 = jnp.sum(jnp.where(lo, 0.0, x2), axis=1, keepdims=True)
    return jnp.where(lo, lax.rsqrt(s0 * (1.0 / HEAD_DIM) + EPS), lax.rsqrt(s1 * (1.0 / HEAD_DIM) + EPS))


def _head_mean(v, lo):
    s0 = jnp.sum(jnp.where(lo, v, 0.0), axis=1, keepdims=True)
    s1 = jnp.sum(jnp.where(lo, 0.0, v), axis=1, keepdims=True)
    return jnp.where(lo, s0, s1) * (1.0 / HEAD_DIM)


def _mix_proj(x1, gain, wt, qn, kn, pool_width, attn_width):
    t, d = x1.shape
    tm = min(512, t)
    nt = t // tm
    scale = HEAD_DIM ** -0.5
    c_q, c_k, c_v = pool_width, pool_width + attn_width, pool_width + 2 * attn_width
    c_f = c_v + attn_width

    def body(x_ref, g_ref, wt_ref, qn_ref, kn_ref, hm_ref, pv_ref, q_ref, k_ref, qh_ref, kh_ref, vb_ref, f_ref):
        xf = x_ref[...]
        hm = ((xf * _rms(xf)) * g_ref[...]).astype(BF16)
        hm_ref[...] = hm
        f_ref[...] = _dot_nt(hm, wt_ref[c_f:c_f + LANES, :])
        pv_ref[...] = _dot_nt(hm, wt_ref[0:pool_width, :])
        vb_ref[...] = _dot_nt(hm, wt_ref[c_v:c_v + attn_width, :]).astype(BF16)
        lo = _head_masks()
        for c0, raw_ref, hat_ref, n_ref, mul in ((c_q, q_ref, qh_ref, qn_ref, scale), (c_k, k_ref, kh_ref, kn_ref, 1.0)):
            raw = _dot_nt(hm, wt_ref[c0:c0 + attn_width, :])
            raw_ref[...] = raw
            for blk in range(attn_width // LANES):
                sl = slice(blk * LANES, (blk + 1) * LANES)
                xb = raw[:, sl]
                hat_ref[:, sl] = (((xb * _head_rms(xb, lo)) * n_ref[:, sl]) * mul).astype(BF16)

    row = pl.BlockSpec((tm, d), lambda i: (i, 0))
    half = pl.BlockSpec((tm, attn_width), lambda i: (i, 0))
    const = lambda shape: pl.BlockSpec(shape, lambda i: (0, 0))
    return pl.pallas_call(
        body,
        out_shape=[jax.ShapeDtypeStruct((t, d), BF16), jax.ShapeDtypeStruct((t, pool_width), F32),
                   jax.ShapeDtypeStruct((t, attn_width), F32), jax.ShapeDtypeStruct((t, attn_width), F32),
                   jax.ShapeDtypeStruct((t, attn_width), BF16), jax.ShapeDtypeStruct((t, attn_width), BF16),
                   jax.ShapeDtypeStruct((t, attn_width), BF16), jax.ShapeDtypeStruct((t, LANES), F32)],
        grid=(nt,),
        in_specs=[row, const((1, d)), const(wt.shape), const((1, attn_width)), const((1, attn_width))],
        out_specs=[row, pl.BlockSpec((tm, pool_width), lambda i: (i, 0)), half, half, half, half, half,
                   pl.BlockSpec((tm, LANES), lambda i: (i, 0))],
        compiler_params=_params(), name="mix_proj",
    )(x1, gain, wt, qn, kn)


def _shift_down(v, dist, row):
    return jnp.where(row >= dist, pltpu.roll(v, dist, 0), 0.0)


def _shift_up(v, dist, row, n):
    return jnp.where(row + dist < n, pltpu.roll(v, n - dist, 0), 0.0)


def _aug_lane(e):
    return HEAD_DIM if e == 0 else 0


def _forget_prefix(f, bias, qh, kh, n_batch, seq):
    def body(f_ref, b_ref, q_ref, k_ref, qa_ref, ka_ref):
        z = f_ref[...] + b_ref[...]
        acc = jnp.minimum(z, 0.0) - jnp.log(1.0 + jnp.exp(-jnp.abs(z)))
        row = lax.broadcasted_iota(jnp.int32, (seq, 1), 0)
        dist = 1
        while dist < seq:
            acc = acc + _shift_down(acc, dist, row)
            dist *= 2
        lane = lax.broadcasted_iota(jnp.int32, (1, LANES), 1)
        for h in range(N_HEADS):
            pair, e = divmod(h, 2)
            a0 = _aug_lane(e)
            own = (lane < HEAD_DIM) if e == 0 else (lane >= HEAD_DIM)
            fh = _pick_lane(acc, h)
            hi = fh.astype(BF16).astype(F32)
            rest = fh - hi
            mid = rest.astype(BF16).astype(F32)
            low = rest - mid
            q_ones = (lane >= a0 + 3) & (lane < a0 + 6)
            k_ones = (lane >= a0) & (lane < a0 + 3)
            q_aug = jnp.where(lane == a0, hi, jnp.where(lane == a0 + 1, mid, jnp.where(lane == a0 + 2, low,
                              jnp.where(q_ones, 1.0, 0.0))))
            k_aug = jnp.where(k_ones, 1.0, jnp.where(lane == a0 + 3, -hi, jnp.where(lane == a0 + 4, -mid,
                              jnp.where(lane == a0 + 5, -low, 0.0))))
            src = slice(pair * LANES, (pair + 1) * LANES)
            dst = slice(h * LANES, (h + 1) * LANES)
            qa_ref[:, dst] = jnp.where(own, q_ref[:, src].astype(F32), q_aug).astype(BF16)
            ka_ref[:, dst] = jnp.where(own, k_ref[:, src].astype(F32), k_aug).astype(BF16)

    width = qh.shape[1]
    tok = pl.BlockSpec((seq, width), lambda b: (b, 0))
    aug = pl.BlockSpec((seq, N_HEADS * LANES), lambda b: (b, 0))
    return pl.pallas_call(
        body, out_shape=[jax.ShapeDtypeStruct((n_batch * seq, N_HEADS * LANES), BF16)] * 2, grid=(n_batch,),
        in_specs=[pl.BlockSpec((seq, LANES), lambda b: (b, 0)), pl.BlockSpec((1, LANES), lambda b: (0, 0)), tok, tok],
        out_specs=[aug, aug], compiler_params=_params(), name="forget_prefix",
    )(f, bias, qh, kh)


def _pool_groups(pv_ref, pw_ref, ps_ref, seq):
    row = lax.broadcasted_iota(jnp.int32, (seq, 1), 0)
    pos = (row + 1).astype(F32)
    out = []
    for g, win in enumerate(POOL_WINDOWS):
        sl = slice(g * LANES, (g + 1) * LANES)
        xg = pv_ref[:, sl]
        acc = xg
        dist = 1
        while dist < win:
            acc = acc + _shift_down(acc, dist, row)
            dist *= 2
        pooled = (acc / jnp.minimum(pos, float(win)) - xg).astype(BF16)
        mixed = _dot(pooled, pw_ref[g])
        out.append((pooled, mixed, mixed * ps_ref[:, sl]))
    return out


def _pool_fwd(pv, pw, ps, onp, n_batch, seq):
    width = pv.shape[1]

    def body(pv_ref, pw_ref, ps_ref, on_ref, y_ref):
        groups = _pool_groups(pv_ref, pw_ref, ps_ref, seq)
        ssq = sum(jnp.sum(ms * ms, axis=1, keepdims=True) for _, _, ms in groups)
        r = lax.rsqrt(ssq * (1.0 / width) + EPS)
        for g, (_, _, ms) in enumerate(groups):
            sl = slice(g * LANES, (g + 1) * LANES)
            y_ref[:, sl] = ((ms * r) * on_ref[:, sl]).astype(BF16)

    return pl.pallas_call(
        body, out_shape=jax.ShapeDtypeStruct((n_batch * seq, width), BF16), grid=(n_batch,),
        in_specs=[pl.BlockSpec((seq, width), lambda b: (b, 0)), pl.BlockSpec(pw.shape, lambda b: (0, 0, 0)),
                  pl.BlockSpec((1, width), lambda b: (0, 0)), pl.BlockSpec((1, width), lambda b: (0, 0))],
        out_specs=pl.BlockSpec((seq, width), lambda b: (b, 0)),
        compiler_params=_params(), name="pool_fwd",
    )(pv, pw, ps, onp)


def _pool_bwd(pv, dyp, pw, ps, onp, n_batch, seq):
    width = pv.shape[1]

    def body(pv_ref, dy_ref, pw_ref, ps_ref, on_ref, dpv_ref, dpw_ref, dps_ref, don_ref):
        groups = _pool_groups(pv_ref, pw_ref, ps_ref, seq)
        ssq = sum(jnp.sum(ms * ms, axis=1, keepdims=True) for _, _, ms in groups)
        r = lax.rsqrt(ssq * (1.0 / width) + EPS)
        mean = sum(jnp.sum((dy_ref[:, g * LANES:(g + 1) * LANES] * on_ref[:, g * LANES:(g + 1) * LANES]) * (ms * r),
                           axis=1, keepdims=True) for g, (_, _, ms) in enumerate(groups)) * (1.0 / width)
        row = lax.broadcasted_iota(jnp.int32, (seq, 1), 0)
        pos = (row + 1).astype(F32)
        for g, (pooled, mixed, ms) in enumerate(groups):
            sl = slice(g * LANES, (g + 1) * LANES)
            dy = dy_ref[:, sl]
            xh = ms * r
            don_ref[:, sl] = jnp.sum(dy * xh, axis=0, keepdims=True)
            dms = r * (dy * on_ref[:, sl] - xh * mean)
            dps_ref[:, sl] = jnp.sum(dms * mixed, axis=0, keepdims=True)
            dmix = (dms * ps_ref[:, sl]).astype(BF16)
            dpw_ref[g] = _dot_tn(pooled, dmix)
            dpool = _dot_nt(dmix, pw_ref[g])
            win = POOL_WINDOWS[g]
            acc = dpool / jnp.minimum(pos, float(win))
            dist = 1
            while dist < win:
                acc = acc + _shift_up(acc, dist, row, seq)
                dist *= 2
            dpv_ref[:, sl] = (acc - dpool).astype(BF16)

    tok = pl.BlockSpec((seq, width), lambda b: (b, 0))
    vec = pl.BlockSpec((1, width), lambda b: (0, 0))
    pvec = pl.BlockSpec((None, 1, width), lambda b: (b, 0, 0))
    return pl.pallas_call(
        body,
        out_shape=[jax.ShapeDtypeStruct((n_batch * seq, width), BF16),
                   jax.ShapeDtypeStruct((n_batch,) + pw.shape, F32),
                   jax.ShapeDtypeStruct((n_batch, 1, width), F32), jax.ShapeDtypeStruct((n_batch, 1, width), F32)],
        grid=(n_batch,),
        in_specs=[tok, tok, pl.BlockSpec(pw.shape, lambda b: (0, 0, 0)), vec, vec],
        out_specs=[tok, pl.BlockSpec((None,) + pw.shape, lambda b: (b, 0, 0, 0)), pvec, pvec],
        compiler_params=_params(), name="pool_bwd",
    )(pv, dyp, pw, ps, onp)


def _pick_lane(tile, idx):
    lane = lax.broadcasted_iota(jnp.int32, (1, LANES), 1)
    return jnp.sum(jnp.where(lane == idx, tile, 0.0), axis=1, keepdims=True)


def _pick_row(tile, idx):
    sub = lax.broadcasted_iota(jnp.int32, (tile.shape[0], 1), 0)
    return jnp.sum(jnp.where(sub == idx, tile, 0.0), axis=0, keepdims=True)


def _put_lane(col, idx):
    lane = lax.broadcasted_iota(jnp.int32, (1, LANES), 1)
    return jnp.where(lane == idx, col, 0.0)


def _head_select(e):
    lo = _head_masks()
    return lo if e == 0 else jnp.logical_not(lo)


def _causal(st, shift):
    row = lax.broadcasted_iota(jnp.int32, st.shape, 0)
    col = lax.broadcasted_iota(jnp.int32, st.shape, 1) + shift
    return jnp.where(col >= row, st, NEG)


def _stat_rows(ref, head, nsub):
    return jnp.concatenate([_pick_row(ref[a], head) for a in range(nsub)], axis=1)


def _accumulate(ref, value, first):
    @pl.when(first)
    def _():
        ref[...] = value

    @pl.when(jnp.logical_not(first))
    def _():
        ref[...] += value


def _attn_fwd(qa, ka, vb, n_batch, seq):
    tq = min(ATT_BLOCK, seq)
    nq, nsub, tk = seq // tq, tq // ATT_SUB, tq
    pairs = vb.shape[1] // LANES

    def body(q_ref, k_ref, v_ref, o_ref, lse_ref, acc_ref):
        i, p = pl.program_id(1), pl.program_id(2)
        row_lo = lax.broadcasted_iota(jnp.int32, (LANES, 1), 0) < HEAD_DIM
        qs = [q_ref[:, e * LANES:(e + 1) * LANES] for e in range(2)]
        acc_ref[...] = jnp.zeros_like(acc_ref)

        def tile(off, stats, diagonal):
            vj = v_ref[pl.ds(off, tk), :]
            new, alphas, pvs = [], [], []
            for e in range(2):
                st = _dot_nt(k_ref[pl.ds(off, tk), e * LANES:(e + 1) * LANES], qs[e])
                if diagonal:
                    st = _causal(st, 0)
                m, l = stats[e]
                m_new = jnp.maximum(m, jnp.max(st, axis=0, keepdims=True))
                alpha = jnp.exp(m - m_new)
                pt = jnp.exp(st - m_new)
                new.append((m_new, alpha * l + jnp.sum(pt, axis=0, keepdims=True)))
                alphas.append(alpha)
                pvs.append(_dot_tn(jnp.where(_head_select(e), vj, jnp.zeros_like(vj)), pt.astype(BF16)))
            acc_ref[...] = acc_ref[...] * jnp.where(row_lo, alphas[0], alphas[1]) + (pvs[0] + pvs[1])
            return tuple(new)

        init = ((jnp.full((1, tq), NEG, F32), jnp.zeros((1, tq), F32)),) * 2
        stats = lax.fori_loop(0, i, lambda j, st: tile(pl.multiple_of(j * tk, tk), st, False), init)
        (m0, l0), (m1, l1) = tile(pl.multiple_of(i * tk, tk), stats, True)
        out_t = acc_ref[...] / jnp.where(row_lo, l0, l1)
        sub = lax.broadcasted_iota(jnp.int32, (8, 1), 0)
        lse0, lse1 = m0 + jnp.log(l0), m1 + jnp.log(l1)
        for a in range(nsub):
            sl = slice(a * ATT_SUB, (a + 1) * ATT_SUB)
            o_ref[sl, :] = out_t[:, sl].T
            rows = jnp.where(sub == 2 * p, lse0[:, sl], 0.0) + jnp.where(sub == 2 * p + 1, lse1[:, sl], 0.0)
            _accumulate(lse_ref.at[a], rows, p == 0)

    return pl.pallas_call(
        body,
        out_shape=[jax.ShapeDtypeStruct((n_batch * seq, pairs * LANES), F32),
                   jax.ShapeDtypeStruct((n_batch * seq // ATT_SUB, 8, ATT_SUB), F32)],
        grid=(n_batch, nq, pairs),
        in_specs=[pl.BlockSpec((tq, 2 * LANES), lambda b, i, p: (b * nq + i, p)),
                  pl.BlockSpec((seq, 2 * LANES), lambda b, i, p: (b, p)),
                  pl.BlockSpec((seq, LANES), lambda b, i, p: (b, p))],
        out_specs=[pl.BlockSpec((tq, LANES), lambda b, i, p: (b * nq + i, p)),
                   pl.BlockSpec((nsub, 8, ATT_SUB), lambda b, i, p: (b * nq + i, 0, 0))],
        scratch_shapes=[pltpu.VMEM((LANES, tq), F32)],
        compiler_params=_params(), name="attn_fwd",
    )(qa, ka, vb)


def _attn_bwd_q(qa, ka, vb, do, lse, delta, n_batch, seq):
    tq = min(ATT_BLOCK, seq)
    nq, nsub, tk = seq // tq, tq // ATT_SUB, tq
    pairs = vb.shape[1] // LANES

    def body(q_ref, k_ref, v_ref, do_ref, lse_ref, dl_ref, dq_ref, dfq_ref, acc0_ref, acc1_ref):
        i, p = pl.program_id(1), pl.program_id(2)
        accs = (acc0_ref, acc1_ref)
        qs = [q_ref[:, e * LANES:(e + 1) * LANES] for e in range(2)]
        dov = do_ref[...]
        ls = [_stat_rows(lse_ref, 2 * p + e, nsub) for e in range(2)]
        dl = [_stat_rows(dl_ref, 2 * p + e, nsub) for e in range(2)]
        for acc in accs:
            acc[...] = jnp.zeros_like(acc)

        def tile(off, diagonal):
            vj = v_ref[pl.ds(off, tk), :]
            for e in range(2):
                kj = k_ref[pl.ds(off, tk), e * LANES:(e + 1) * LANES]
                st = _dot_nt(kj, qs[e])
                if diagonal:
                    st = _causal(st, 0)
                pt = jnp.exp(st - ls[e])
                dpt = _dot_nt(jnp.where(_head_select(e), vj, jnp.zeros_like(vj)), dov)
                accs[e][...] += _dot_tn((pt * (dpt - dl[e])).astype(BF16), kj)

        def step(j, carry):
            tile(pl.multiple_of(j * tk, tk), False)
            return carry

        lax.fori_loop(0, i, step, 0)
        tile(pl.multiple_of(i * tk, tk), True)
        dq0, dq1 = acc0_ref[...], acc1_ref[...]
        dq_ref[...] = jnp.where(_head_masks(), dq0, dq1)
        dfq = _put_lane(_pick_lane(dq0, _aug_lane(0)), 2 * p) + _put_lane(_pick_lane(dq1, _aug_lane(1)), 2 * p + 1)
        _accumulate(dfq_ref, dfq, p == 0)

    stat = pl.BlockSpec((nsub, 8, ATT_SUB), lambda b, i, p: (b * nq + i, 0, 0))
    blk = pl.BlockSpec((tq, LANES), lambda b, i, p: (b * nq + i, p))
    return pl.pallas_call(
        body,
        out_shape=[jax.ShapeDtypeStruct((n_batch * seq, pairs * LANES), F32), jax.ShapeDtypeStruct((n_batch * seq, LANES), F32)],
        grid=(n_batch, nq, pairs),
        in_specs=[pl.BlockSpec((tq, 2 * LANES), lambda b, i, p: (b * nq + i, p)),
                  pl.BlockSpec((seq, 2 * LANES), lambda b, i, p: (b, p)),
                  pl.BlockSpec((seq, LANES), lambda b, i, p: (b, p)), blk, stat, stat],
        out_specs=[blk, pl.BlockSpec((tq, LANES), lambda b, i, p: (b * nq + i, 0))],
        scratch_shapes=[pltpu.VMEM((tq, LANES), F32), pltpu.VMEM((tq, LANES), F32)],
        compiler_params=_params(), name="attn_bwd_q",
    )(qa, ka, vb, do, lse, delta)


def _attn_bwd_kv(qa, ka, vb, do, lse, delta, n_batch, seq):
    tkb = min(ATT_BLOCK, seq)
    nk, nsub, tq = seq // tkb, tkb // ATT_SUB, tkb
    n_tiles = seq // ATT_SUB
    pairs = vb.shape[1] // LANES

    def body(q_ref, k_ref, v_ref, do_ref, lse_ref, dl_ref, dk_ref, dv_ref, dfk_ref, dk0_ref, dk1_ref, dva_ref):
        j, p = pl.program_id(1), pl.program_id(2)
        dks = (dk0_ref, dk1_ref)
        ks = [k_ref[:, e * LANES:(e + 1) * LANES] for e in range(2)]
        vj = v_ref[...]
        vs = [jnp.where(_head_select(e), vj, jnp.zeros_like(vj)) for e in range(2)]
        for acc in (dk0_ref, dk1_ref, dva_ref):
            acc[...] = jnp.zeros_like(acc)

        def tile(t, diagonal):
            off = pl.multiple_of(t * tq, tq)
            dov = do_ref[pl.ds(off, tq), :]
            for e in range(2):
                qe = q_ref[pl.ds(off, tq), e * LANES:(e + 1) * LANES]
                st = _dot_nt(ks[e], qe)
                if diagonal:
                    st = _causal(st, 0)
                rows = lambda ref: jnp.concatenate([_pick_row(ref[t * nsub + a], 2 * p + e) for a in range(nsub)], axis=1)
                pt = jnp.exp(st - rows(lse_ref))
                dva_ref[...] += _dot(pt.astype(BF16), jnp.where(_head_select(e), dov, jnp.zeros_like(dov)))
                dst = pt * (_dot_nt(vs[e], dov) - rows(dl_ref))
                dks[e][...] += _dot(dst.astype(BF16), qe)

        def step(t, carry):
            tile(t, False)
            return carry

        lax.fori_loop(j + 1, nk, step, 0)
        tile(j, True)
        dk0, dk1 = dk0_ref[...], dk1_ref[...]
        dk_ref[...] = jnp.where(_head_masks(), dk0, dk1)
        dv_ref[...] = dva_ref[...].astype(BF16)
        dfk = (_put_lane(_pick_lane(dk0, _aug_lane(0) + 3), 2 * p)
               + _put_lane(_pick_lane(dk1, _aug_lane(1) + 3), 2 * p + 1))
        _accumulate(dfk_ref, -dfk, p == 0)

    stat = pl.BlockSpec((n_tiles, 8, ATT_SUB), lambda b, j, p: (b, 0, 0))
    blk = pl.BlockSpec((tkb, LANES), lambda b, j, p: (b * nk + j, p))
    acc = pltpu.VMEM((tkb, LANES), F32)
    return pl.pallas_call(
        body,
        out_shape=[jax.ShapeDtypeStruct((n_batch * seq, pairs * LANES), F32),
                   jax.ShapeDtypeStruct((n_batch * seq, pairs * LANES), BF16),
                   jax.ShapeDtypeStruct((n_batch * seq, LANES), F32)],
        grid=(n_batch, nk, pairs),
        in_specs=[pl.BlockSpec((seq, 2 * LANES), lambda b, j, p: (b, p)),
                  pl.BlockSpec((tkb, 2 * LANES), lambda b, j, p: (b * nk + j, p)), blk,
                  pl.BlockSpec((seq, LANES), lambda b, j, p: (b, p)), stat, stat],
        out_specs=[blk, blk, pl.BlockSpec((tkb, LANES), lambda b, j, p: (b * nk + j, 0))],
        scratch_shapes=[acc, acc, acc],
        compiler_params=_params(), name="attn_bwd_kv",
    )(qa, ka, vb, do, lse, delta)


def _forget_bwd(dfq, dfk, f, bias, n_batch, seq):
    def body(dfq_ref, dfk_ref, f_ref, b_ref, df_ref, db_ref):
        acc = dfq_ref[...] + dfk_ref[...]
        row = lax.broadcasted_iota(jnp.int32, (seq, 1), 0)
        dist = 1
        while dist < seq:
            acc = acc + _shift_up(acc, dist, row, seq)
            dist *= 2
        df = acc * _sigmoid(-(f_ref[...] + b_ref[...]))
        df_ref[...] = df
        db_ref[...] = jnp.sum(df, axis=0, keepdims=True)

    col = pl.BlockSpec((seq, LANES), lambda b: (b, 0))
    return pl.pallas_call(
        body,
        out_shape=[jax.ShapeDtypeStruct((n_batch * seq, LANES), F32), jax.ShapeDtypeStruct((n_batch, 1, LANES), F32)],
        grid=(n_batch,), in_specs=[col, col, col, pl.BlockSpec((1, LANES), lambda b: (0, 0))],
        out_specs=[col, pl.BlockSpec((None, 1, LANES), lambda b: (b, 0, 0))],
        compiler_params=_params(), name="forget_bwd",
    )(dfq, dfk, f, bias)


def _mix_out(x1, yp, o, ona, woa, wob):
    t, d = x1.shape
    width = o.shape[1]
    tm = min(512, t)

    def body(x_ref, yp_ref, o_ref, on_ref, wa_ref, wb_ref, x2_ref, ya_ref):
        of = o_ref[...]
        ya = ((of * _rms(of)) * on_ref[...]).astype(BF16)
        ya_ref[...] = ya
        x2_ref[...] = x_ref[...] + (_dot(yp_ref[...], wa_ref[...]) + _dot(ya, wb_ref[...]))

    row = pl.BlockSpec((tm, d), lambda i: (i, 0))
    half = pl.BlockSpec((tm, width), lambda i: (i, 0))
    wspec = pl.BlockSpec((width, d), lambda i: (0, 0))
    return pl.pallas_call(
        body, out_shape=[jax.ShapeDtypeStruct((t, d), F32), jax.ShapeDtypeStruct((t, width), BF16)],
        grid=(t // tm,), in_specs=[row, half, half, pl.BlockSpec((1, width), lambda i: (0, 0)), wspec, wspec],
        out_specs=[row, half], compiler_params=_params(), name="mix_out",
    )(x1, yp, o, ona, woa, wob)


def _mix_out_bwd(dx2, o, yp, ya, ona, woa, wob):
    t, d = dx2.shape
    width = o.shape[1]
    tm = min(512, t)
    nt = t // tm

    def body(dx_ref, o_ref, yp_ref, ya_ref, on_ref, wa_ref, wb_ref, dyp_ref, do_ref, dl_ref, dwa_ref, dwb_ref, don_ref):
        @pl.when(pl.program_id(0) == 0)
        def _():
            dwa_ref[...] = jnp.zeros_like(dwa_ref)
            dwb_ref[...] = jnp.zeros_like(dwb_ref)

        dxb = dx_ref[...].astype(BF16)
        dwa_ref[...] += _dot_tn(yp_ref[...], dxb)
        dwb_ref[...] += _dot_tn(ya_ref[...], dxb)
        dyp_ref[...] = _dot_nt(dxb, wa_ref[...])
        of = o_ref[...]
        dov, dgr = _rms_bwd(of, _rms(of), on_ref[...], _dot_nt(dxb, wb_ref[...]))
        don_ref[...] = jnp.sum(dgr, axis=0, keepdims=True)
        do_ref[...] = dov.astype(BF16)
        lo = _head_masks()
        prod = dov * of
        delta = jnp.zeros((tm, LANES), F32)
        for blk in range(width // LANES):
            pb = prod[:, blk * LANES:(blk + 1) * LANES]
            delta = delta + _put_lane(jnp.sum(jnp.where(lo, pb, 0.0), axis=1, keepdims=True), 2 * blk)
            delta = delta + _put_lane(jnp.sum(jnp.where(lo, 0.0, pb), axis=1, keepdims=True), 2 * blk + 1)
        for c in range(tm // ATT_SUB):
            dl_ref[c] = delta[c * ATT_SUB:(c + 1) * ATT_SUB, :].T[0:8, :]

    row = pl.BlockSpec((tm, d), lambda i: (i, 0))
    half = pl.BlockSpec((tm, width), lambda i: (i, 0))
    wspec = pl.BlockSpec((width, d), lambda i: (0, 0))
    return pl.pallas_call(
        body,
        out_shape=[jax.ShapeDtypeStruct((t, width), F32), jax.ShapeDtypeStruct((t, width), BF16),
                   jax.ShapeDtypeStruct((t // ATT_SUB, 8, ATT_SUB), F32), jax.ShapeDtypeStruct((width, d), F32),
                   jax.ShapeDtypeStruct((width, d), F32), jax.ShapeDtypeStruct((nt, 1, width), F32)],
        grid=(nt,),
        in_specs=[row, half, half, half, pl.BlockSpec((1, width), lambda i: (0, 0)), wspec, wspec],
        out_specs=[half, half, pl.BlockSpec((tm // ATT_SUB, 8, ATT_SUB), lambda i: (i, 0, 0)), wspec, wspec,
                   pl.BlockSpec((None, 1, width), lambda i: (i, 0, 0))],
        compiler_params=_params(), name="mix_out_bwd",
    )(dx2, o, yp, ya, ona, woa, wob)


def _mix_in_bwd(dx2, x1, gain, hm, dpv, dqh, q, dkh, k, dv, df, qn, kn, wt):
    t, d = x1.shape
    width = q.shape[1]
    pool_width = dpv.shape[1]
    tm = min(512, t)
    nt = t // tm
    scale = HEAD_DIM ** -0.5
    c_q, c_k, c_v = pool_width, pool_width + width, pool_width + 2 * width
    c_f = c_v + width

    def body(dx2_ref, x_ref, g_ref, hm_ref, dpv_ref, dqh_ref, q_ref, dkh_ref, k_ref, dv_ref, df_ref, qn_ref, kn_ref,
             wt_ref, dx_ref, dwt_ref, dg_ref, dqn_ref, dkn_ref):
        @pl.when(pl.program_id(0) == 0)
        def _():
            dwt_ref[...] = jnp.zeros_like(dwt_ref)

        lo = _head_masks()
        hm = hm_ref[...]
        pieces = [(0, dpv_ref[...])]
        for c0, raw_ref, dh_ref, n_ref, dn_ref, mul in ((c_q, q_ref, dqh_ref, qn_ref, dqn_ref, scale),
                                                       (c_k, k_ref, dkh_ref, kn_ref, dkn_ref, 1.0)):
            cols = []
            for blk in range(width // LANES):
                sl = slice(blk * LANES, (blk + 1) * LANES)
                xb = raw_ref[:, sl]
                gb = dh_ref[:, sl] * mul
                r = _head_rms(xb, lo)
                xh = xb * r
                dyg = gb * n_ref[:, sl]
                cols.append((r * (dyg - xh * _head_mean(dyg * xh, lo))).astype(BF16))
                dn_ref[:, sl] = jnp.sum(gb * xh, axis=0, keepdims=True)
            pieces.append((c0, jnp.concatenate(cols, axis=1)))
        pieces.append((c_v, dv_ref[...]))
        pieces.append((c_f, df_ref[...].astype(BF16)))
        dhm = jnp.zeros((tm, d), F32)
        for c0, piece in pieces:
            dwt_ref[c0:c0 + piece.shape[1], :] += _dot_tn(piece, hm)
            dhm = dhm + _dot(piece, wt_ref[c0:c0 + piece.shape[1], :])
        xf = x_ref[...]
        dxn, dgr = _rms_bwd(xf, _rms(xf), g_ref[...], dhm)
        dx_ref[...] = dx2_ref[...] + dxn
        dg_ref[...] = jnp.sum(dgr, axis=0, keepdims=True)

    row = pl.BlockSpec((tm, d), lambda i: (i, 0))
    half = pl.BlockSpec((tm, width), lambda i: (i, 0))
    const = lambda shape: pl.BlockSpec(shape, lambda i: (0, 0))
    pvec = lambda n: pl.BlockSpec((None, 1, n), lambda i: (i, 0, 0))
    return pl.pallas_call(
        body,
        out_shape=[jax.ShapeDtypeStruct((t, d), F32), jax.ShapeDtypeStruct(wt.shape, F32),
                   jax.ShapeDtypeStruct((nt, 1, d), F32),
                   jax.ShapeDtypeStruct((nt, 1, width), F32), jax.ShapeDtypeStruct((nt, 1, width), F32)],
        grid=(nt,),
        in_specs=[row, row, const((1, d)), row, pl.BlockSpec((tm, pool_width), lambda i: (i, 0)), half, half, half, half,
                  half, pl.BlockSpec((tm, LANES), lambda i: (i, 0)), const((1, width)), const((1, width)),
                  const(wt.shape)],
        out_specs=[row, const(wt.shape), pvec(d), pvec(width), pvec(width)],
        compiler_params=_params(), name="mix_in_bwd",
    )(dx2, x1, gain, hm, dpv, dqh, q, dkh, k, dv, df, qn, kn, wt)


def _local_step(xf, tgt, small, full, n_batch, seq):
    pool_width = small["pool_scale"].shape[1]
    attn_width = small["out_norm_attn"].shape[1]
    x1, h1, a1, b1, s1 = _ffn_fwd(xf, small["ffn1_norm"], full["wg1"], full["wu1"], full["wd1"])
    hm, pv, q, k, qh, kh, vb, f = _mix_proj(x1, small["mix_norm"], full["w_in_t"], small["qn"], small["kn"],
                                            pool_width, attn_width)
    qa, ka = _forget_prefix(f, small["b_forget"], qh, kh, n_batch, seq)
    yp = _pool_fwd(pv, full["pool_w"], small["pool_scale"], small["out_norm_pool"], n_batch, seq)
    o, lse = _attn_fwd(qa, ka, vb, n_batch, seq)
    x2, ya = _mix_out(x1, yp, o, small["out_norm_attn"], full["woa"], full["wob"])
    dy, h2, a2, b2, s2, lpart = _ffn_fwd(x2, small["ffn2_norm"], full["wg2"], full["wu2"], full["wd2"], target=tgt)

    dx2, da2, db2, dg2 = _ffn_bwd_x(dy, x2, small["ffn2_norm"], a2, b2, full["wg2"], full["wu2"], full["wd2"], "ffn2_bwd_x")
    dwg2, dwu2, dwd2 = _ffn_bwd_w(h2, s2, da2, db2, dy, "ffn2_bwd_w")
    dyp, do, delta, dwoa, dwob, dona = _mix_out_bwd(dx2, o, yp, ya, small["out_norm_attn"], full["woa"], full["wob"])
    dpv, dpw, dps, donp = _pool_bwd(pv, dyp, full["pool_w"], small["pool_scale"], small["out_norm_pool"], n_batch, seq)
    dqh, dfq = _attn_bwd_q(qa, ka, vb, do, lse, delta, n_batch, seq)
    dkh, dv, dfk = _attn_bwd_kv(qa, ka, vb, do, lse, delta, n_batch, seq)
    df, dbf = _forget_bwd(dfq, dfk, f, small["b_forget"], n_batch, seq)
    dx1, dw_in_t, dgm, dqn, dkn = _mix_in_bwd(dx2, x1, small["mix_norm"], hm, dpv, dqh, q, dkh, k, dv, df,
                                              small["qn"], small["kn"], full["w_in_t"])
    gx, da1, db1, dg1 = _ffn_bwd_x(dx1, xf, small["ffn1_norm"], a1, b1, full["wg1"], full["wu1"], full["wd1"], "ffn1_bwd_x")
    dwg1, dwu1, dwd1 = _ffn_bwd_w(h1, s1, da1, db1, dx1, "ffn1_bwd_w")
    big = dict(wg1=dwg1, wu1=dwu1, wd1=dwd1, w_in_t=dw_in_t, woa=dwoa, wob=dwob, wg2=dwg2, wu2=dwu2, wd2=dwd2)
    part = dict(ffn1_norm=dg1, mix_norm=dgm, ffn2_norm=dg2, b_forget=dbf, pool_w=dpw, pool_scale=dps,
                out_norm_pool=donp, out_norm_attn=dona, qn=dqn, kn=dkn)
    return lpart, gx, big, part


def _mesh_pos():
    return lax.axis_index("x"), lax.axis_index("y"), lax.axis_index("c")


def _other_chips(x, y):
    return [(1 - x, y), (x, 1 - y), (1 - x, 1 - y)]


def _remote(src, dst, send_sem, recv_sem, device):
    return pltpu.make_async_remote_copy(src_ref=src, dst_ref=dst, send_sem=send_sem, recv_sem=recv_sem,
                                        device_id=device, device_id_type=pl.DeviceIdType.MESH)


def _half_rows(n_rows, which):
    half = n_rows // 2
    return pl.ds(pl.multiple_of(which * half, 8), half)


def _row_block(rows, cols, itemsize=4):
    rb = rows
    while rb * cols * itemsize > (1 << 20) and rb % 32 == 0:
        rb //= 2
    return rb


def _place_cast(w, chip, tag):
    rows, cols = w.shape
    rb = _row_block(rows, cols)

    def body(k_ref, w_ref, o_ref):
        o_ref[...] = w_ref[...].astype(BF16)

    return pl.pallas_call(
        body, out_shape=jax.ShapeDtypeStruct((N_CHIPS, rows, cols), BF16),
        grid_spec=pltpu.PrefetchScalarGridSpec(
            num_scalar_prefetch=1, grid=(rows // rb,),
            in_specs=[pl.BlockSpec((rb, cols), lambda i, k: (i, 0))],
            out_specs=pl.BlockSpec((None, rb, cols), lambda i, k: (k[0], i, 0))),
        compiler_params=_params(), name="place_" + tag,
    )(chip, w)


def _gather_weights(stacks):
    n = len(stacks)

    def body(*refs):
        outs = refs[n:2 * n]
        ici_send, ici_recv, d2d_send, d2d_recv = refs[2 * n:]
        x, y, c = _mesh_pos()
        mine = 2 * x + y
        sibling = (x, y, 1 - c)
        chips = _other_chips(x, y)
        slots = [2 * cx + cy for cx, cy in chips]
        sends = []
        for w in range(n):
            own = outs[w].at[mine, _half_rows(stacks[w].shape[1], c)]
            for j, chip in enumerate(chips):
                cp = _remote(own, own, ici_send.at[w, j], ici_recv.at[w, j], (*chip, c))
                cp.start()
                sends.append(cp)
        for w in range(n):
            rows = _half_rows(stacks[w].shape[1], c)
            for j in range(3):
                landed = outs[w].at[slots[j], rows]
                _remote(landed, landed, ici_send.at[w, j], ici_recv.at[w, j], sibling).wait_recv()
                cp = _remote(landed, landed, d2d_send.at[w, j], d2d_recv.at[w, j], sibling)
                cp.start()
                sends.append(cp)
        for w in range(n):
            rows = _half_rows(stacks[w].shape[1], 1 - c)
            for j in range(3):
                landed = outs[w].at[slots[j], rows]
                _remote(landed, landed, d2d_send.at[w, j], d2d_recv.at[w, j], sibling).wait_recv()
        for cp in sends:
            cp.wait_send()

    return pl.pallas_call(
        body, out_shape=[jax.ShapeDtypeStruct(s.shape, s.dtype) for s in stacks],
        in_specs=[ANY] * n, out_specs=[ANY] * n, input_output_aliases={w: w for w in range(n)},
        scratch_shapes=[pltpu.SemaphoreType.DMA((n, 3)), pltpu.SemaphoreType.DMA((n, 3)),
                        pltpu.SemaphoreType.DMA((n, 3)), pltpu.SemaphoreType.DMA((n, 3))],
        name="gather_weights",
    )(*stacks)


def _sibling_halves(gs):
    n = len(gs)

    def body(*refs):
        ins, outs = refs[:n], refs[n:2 * n]
        send, recv = refs[2 * n:]
        x, y, c = _mesh_pos()
        cps = []
        for w in range(n):
            cp = _remote(ins[w].at[:, _half_rows(gs[w].shape[1], 1 - c), :], outs[w], send.at[w], recv.at[w], (x, y, 1 - c))
            cp.start()
            cps.append(cp)
        for cp in cps:
            cp.wait()

    return pl.pallas_call(
        body, out_shape=[jax.ShapeDtypeStruct((g.shape[0], g.shape[1] // 2, g.shape[2]), g.dtype) for g in gs],
        in_specs=[ANY] * n, out_specs=[ANY] * n,
        scratch_shapes=[pltpu.SemaphoreType.DMA((n,)), pltpu.SemaphoreType.DMA((n,))],
        name="sibling_halves",
    )(*gs)


def _chip_exchange(ps):
    n = len(ps)

    def body(*refs):
        ins, outs = refs[:n], refs[n:2 * n]
        send, recv = refs[2 * n:]
        x, y, c = _mesh_pos()
        chips = _other_chips(x, y)
        cps = []
        for w in range(n):
            for j, (cx, cy) in enumerate(chips):
                cp = _remote(ins[w].at[2 * cx + cy], outs[w].at[j], send.at[w, j], recv.at[w, j], (cx, cy, c))
                cp.start()
                cps.append(cp)
        for cp in cps:
            cp.wait()

    return pl.pallas_call(
        body, out_shape=[jax.ShapeDtypeStruct((3,) + p.shape[1:], p.dtype) for p in ps],
        in_specs=[ANY] * n, out_specs=[ANY] * n,
        scratch_shapes=[pltpu.SemaphoreType.DMA((n, 3)), pltpu.SemaphoreType.DMA((n, 3))],
        name="chip_exchange",
    )(*ps)


def _sibling_share(gs):
    n = len(gs)

    def body(*refs):
        outs = refs[n:2 * n]
        send, recv = refs[2 * n:]
        x, y, c = _mesh_pos()
        cps = []
        for w in range(n):
            mine = outs[w].at[_half_rows(gs[w].shape[0], c)]
            cp = _remote(mine, mine, send.at[w], recv.at[w], (x, y, 1 - c))
            cp.start()
            cps.append(cp)
        for w, cp in enumerate(cps):
            cp.wait_send()
            theirs = outs[w].at[_half_rows(gs[w].shape[0], 1 - c)]
            _remote(theirs, theirs, send.at[w], recv.at[w], (x, y, 1 - c)).wait_recv()

    return pl.pallas_call(
        body, out_shape=[jax.ShapeDtypeStruct(g.shape, g.dtype) for g in gs],
        in_specs=[ANY] * n, out_specs=[ANY] * n, input_output_aliases={w: w for w in range(n)},
        scratch_shapes=[pltpu.SemaphoreType.DMA((n,)), pltpu.SemaphoreType.DMA((n,))],
        name="sibling_share",
    )(*gs)


def _add_sibling(g, r1, ids, tag):
    nch, rh, cols = r1.shape

    def body(ids_ref, g_ref, r_ref, o_ref):
        o_ref[...] = (g_ref[...] + r_ref[...]).astype(BF16)

    blk = lambda fn: pl.BlockSpec((None, rh, cols), fn)
    return pl.pallas_call(
        body, out_shape=jax.ShapeDtypeStruct(r1.shape, BF16),
        grid_spec=pltpu.PrefetchScalarGridSpec(
            num_scalar_prefetch=1, grid=(nch,),
            in_specs=[blk(lambda k, ids: (k, ids[1], 0)), blk(lambda k, ids: (k, 0, 0))],
            out_specs=blk(lambda k, ids: (k, 0, 0))),
        compiler_params=_params(), name="add_sibling_" + tag,
    )(ids, g, r1)


def _add_chips(g, r1, r2, ids, tag):
    _, rh, cols = r1.shape

    def body(ids_ref, g_ref, r1_ref, r2_ref, o_ref):
        own = g_ref[...] + r1_ref[...]
        o_ref[...] = ((own + r2_ref[0].astype(F32)) + r2_ref[1].astype(F32)) + r2_ref[2].astype(F32)

    return pl.pallas_call(
        body, out_shape=jax.ShapeDtypeStruct((2 * rh, cols), F32),
        grid_spec=pltpu.PrefetchScalarGridSpec(
            num_scalar_prefetch=1, grid=(1,),
            in_specs=[pl.BlockSpec((None, rh, cols), lambda i, ids: (ids[0], ids[1], 0)),
                      pl.BlockSpec((None, rh, cols), lambda i, ids: (ids[0], 0, 0)),
                      pl.BlockSpec((3, rh, cols), lambda i, ids: (0, 0, 0))],
            out_specs=pl.BlockSpec((rh, cols), lambda i, ids: (ids[1], 0))),
        compiler_params=_params(), name="add_chips_" + tag,
    )(ids, g, r1, r2)


def _reduce_to_owner(gs, ids, tags):
    r1 = _sibling_halves(gs)
    ps = [_add_sibling(g, r, ids, t) for g, r, t in zip(gs, r1, tags)]
    r2 = _chip_exchange(ps)
    return _sibling_share([_add_chips(g, ra, rb, ids, t) for g, ra, rb, t in zip(gs, r1, r2, tags)])


VEC_ROWS = 8


def _small_allreduce(part, d, width):
    names = ("ffn1_norm", "mix_norm", "ffn2_norm", "pool_scale", "out_norm_pool", "out_norm_attn", "qn", "kn", "b_forget",
             "pool_w", "loss")
    args = [part[k] for k in names]
    pw_shape = part["pool_w"].shape[1:]
    n_dev = 8

    def body(g1_ref, gm_ref, g2_ref, ps_ref, onp_ref, ona_ref, qn_ref, kn_ref, bf_ref, pw_ref, loss_ref,
             vec_ref, pwo_ref, vbuf, pbuf, send, recv):
        x, y, c = _mesh_pos()
        me = 4 * x + 2 * y + c
        lo = _head_masks()

        def fold_heads(ref):
            v = jnp.sum(ref[...], axis=0)
            acc = jnp.zeros((VEC_ROWS, LANES), F32)
            for blk in range(width // LANES):
                vb = jnp.broadcast_to(v[:, blk * LANES:(blk + 1) * LANES], (VEC_ROWS, LANES))
                acc = acc + vb + pltpu.roll(vb, HEAD_DIM, 1)
            return jnp.where(lo, acc, 0.0)[0:1, :]

        vbuf[0] = jnp.zeros((VEC_ROWS, d), F32)
        vbuf[0, 0:1, :] = jnp.sum(g1_ref[...], axis=0)
        vbuf[0, 1:2, :] = jnp.sum(gm_ref[...], axis=0)
        vbuf[0, 2:3, :] = jnp.sum(g2_ref[...], axis=0)
        vbuf[0, 5:6, 0:LANES] = jnp.sum(loss_ref[...], axis=0)[0:1, :]
        vbuf[0, 3:4, 0:width] = jnp.sum(ps_ref[...], axis=0)
        vbuf[0, 3:4, width:2 * width] = jnp.sum(onp_ref[...], axis=0)
        vbuf[0, 4:5, 0:width] = jnp.sum(ona_ref[...], axis=0)
        vbuf[0, 4:5, width:width + LANES] = fold_heads(qn_ref)
        vbuf[0, 4:5, width + LANES:width + 2 * LANES] = fold_heads(kn_ref)
        vbuf[0, 4:5, width + 2 * LANES:width + 3 * LANES] = jnp.sum(bf_ref[...], axis=0)
        pbuf[0] = jnp.sum(pw_ref[...], axis=0)

        cps = []
        for r in range(1, n_dev):
            peer = (x if not r & 4 else 1 - x, y if not r & 2 else 1 - y, c if not r & 1 else 1 - c)
            for buf, k in ((vbuf, 0), (pbuf, 1)):
                cp = _remote(buf.at[0], buf.at[r], send.at[k, r - 1], recv.at[k, r - 1], peer)
                cp.start()
                cps.append(cp)
        for cp in cps:
            cp.wait()
        vec = vbuf[me]
        pw = pbuf[me]
        for dev in range(1, n_dev):
            vec = vec + vbuf[jnp.bitwise_xor(me, dev)]
            pw = pw + pbuf[jnp.bitwise_xor(me, dev)]
        vec_ref[...] = vec
        pwo_ref[...] = pw

    return pl.pallas_call(
        body, out_shape=[jax.ShapeDtypeStruct((VEC_ROWS, d), F32), jax.ShapeDtypeStruct(pw_shape, F32)],
        in_specs=[VM] * len(args), out_specs=[VM, VM],
        scratch_shapes=[pltpu.VMEM((n_dev, VEC_ROWS, d), F32), pltpu.VMEM((n_dev,) + pw_shape, F32),
                        pltpu.SemaphoreType.DMA((2, n_dev - 1)), pltpu.SemaphoreType.DMA((2, n_dev - 1))],
        compiler_params=_params(), name="small_allreduce",
    )(*args)


def _adamw(w, g, m, v, tag):
    rows, cols = w.shape
    rb = rows
    while rb * cols * 4 > (1 << 20) and rb % 16 == 0:
        rb //= 2

    def body(w_ref, g_ref, m_ref, v_ref, d_ref, mo_ref, vo_ref):
        gv = g_ref[...]
        m2 = ADAM_B1 * m_ref[...] + (1.0 - ADAM_B1) * gv
        v2 = ADAM_B2 * v_ref[...] + (1.0 - ADAM_B2) * (gv * gv)
        m_hat = m2 / (1.0 - ADAM_B1 ** ADAM_STEP)
        v_hat = v2 / (1.0 - ADAM_B2 ** ADAM_STEP)
        d_ref[...] = -ADAM_LR * (m_hat / (jnp.sqrt(v_hat) + ADAM_EPS) + ADAM_WD * w_ref[...])
        mo_ref[...] = m2
        vo_ref[...] = v2

    spec = pl.BlockSpec((rb, cols), lambda i: (i, 0))
    return pl.pallas_call(
        body, out_shape=[jax.ShapeDtypeStruct(w.shape, F32)] * 3, grid=(rows // rb,),
        in_specs=[spec] * 4, out_specs=[spec] * 3, compiler_params=_params(), name="adamw_" + tag,
    )(w, g, m, v)


def _pack_vec(p, d, width):
    pad = lambda v: jnp.pad(v, (0, LANES - v.shape[0]))
    row3 = jnp.concatenate([p["pool_scale"], p["out_norm_pool"]])
    row4 = jnp.concatenate([p["out_norm_attn"], pad(p["q_norm"]), pad(p["k_norm"]), pad(p["b_forget"]),
                            jnp.zeros((d - width - 3 * LANES,), F32)])
    rows = [p["ffn1_norm"], p["mix_norm"], p["ffn2_norm"], row3, row4]
    return jnp.pad(jnp.stack(rows), ((0, VEC_ROWS - len(rows)), (0, 0)))


def _unpack_vec(vec, width):
    return dict(ffn1_norm=vec[0], mix_norm=vec[1], ffn2_norm=vec[2], pool_scale=vec[3, :width],
                out_norm_pool=vec[3, width:2 * width], out_norm_attn=vec[4, :width],
                q_norm=vec[4, width:width + HEAD_DIM], k_norm=vec[4, width + LANES:width + LANES + HEAD_DIM],
                b_forget=vec[4, width + 2 * LANES:width + 2 * LANES + N_HEADS])


WEIGHT_NAMES = ("ffn1_norm", "ffn1_w_gate", "ffn1_w_up", "ffn1_w_down", "mix_norm", "w_in", "b_forget", "pool_w",
                "pool_scale", "q_norm", "k_norm", "out_norm_pool", "out_norm_attn", "w_out", "ffn2_norm",
                "ffn2_w_gate", "ffn2_w_up", "ffn2_w_down")
BIG_NAMES = ("ffn1_w_gate", "ffn1_w_up", "ffn1_w_down", "w_in", "w_out", "ffn2_w_gate", "ffn2_w_up", "ffn2_w_down")
TRANSPOSED_NAMES = ("ffn1_w_gate", "ffn1_w_up", "w_in", "ffn2_w_gate", "ffn2_w_up")


def kernel(x, ffn1_norm, ffn1_w_gate, ffn1_w_up, ffn1_w_down, mix_norm, w_in, b_forget, pool_w, pool_scale, q_norm, k_norm, out_norm_pool, out_norm_attn, w_out, ffn2_norm, ffn2_w_gate, ffn2_w_up, ffn2_w_down, loss_target, m_ffn1_norm, m_ffn1_w_gate, m_ffn1_w_up, m_ffn1_w_down, m_mix_norm, m_w_in, m_b_forget, m_pool_w, m_pool_scale, m_q_norm, m_k_norm, m_out_norm_pool, m_out_norm_attn, m_w_out, m_ffn2_norm, m_ffn2_w_gate, m_ffn2_w_up, m_ffn2_w_down, v_ffn1_norm, v_ffn1_w_gate, v_ffn1_w_up, v_ffn1_w_down, v_mix_norm, v_w_in, v_b_forget, v_pool_w, v_pool_scale, v_q_norm, v_k_norm, v_out_norm_pool, v_out_norm_attn, v_w_out, v_ffn2_norm, v_ffn2_w_gate, v_ffn2_w_up, v_ffn2_w_down):
    given = dict(locals())
    w = {n: given[n] for n in WEIGHT_NAMES}
    m = {n: given["m_" + n] for n in WEIGHT_NAMES}
    v = {n: given["v_" + n] for n in WEIGHT_NAMES}
    n_batch, seq, d = x.shape
    width = pool_scale.shape[0]
    in_rows = w_in.shape[1]
    in_cols = N_CHIPS * in_rows
    in_pad = -(-in_rows // 32) * 32
    in_cols_pad = in_cols - N_HEADS + LANES

    work = lambda a, n: a.T if n in TRANSPOSED_NAMES else a
    exchanged = lambda a, n: jnp.pad(a, ((0, in_pad - in_rows), (0, 0))) if n == "w_in" else a

    mesh_x, mesh_y, mesh_c = _mesh_pos()
    ids = jnp.stack([2 * mesh_x + mesh_y, mesh_c]).astype(jnp.int32)

    gathered = dict(zip(BIG_NAMES, _gather_weights([_place_cast(exchanged(work(w[n], n), n), ids, n) for n in BIG_NAMES])))
    w_in_t = jnp.pad(gathered["w_in"][:, :in_rows].reshape(in_cols, d), ((0, in_cols_pad - in_cols), (0, 0)))
    w_out_full = gathered["w_out"].reshape(N_CHIPS * w_out.shape[0], d)
    full = dict(wg1=gathered["ffn1_w_gate"], wu1=gathered["ffn1_w_up"], wd1=gathered["ffn1_w_down"],
                wg2=gathered["ffn2_w_gate"], wu2=gathered["ffn2_w_up"], wd2=gathered["ffn2_w_down"],
                w_in_t=w_in_t, woa=w_out_full[:width], wob=w_out_full[width:], pool_w=pool_w.astype(BF16))
    row = lambda a: a.reshape(1, -1)
    small = dict(ffn1_norm=row(ffn1_norm), mix_norm=row(mix_norm), ffn2_norm=row(ffn2_norm), pool_scale=row(pool_scale),
                 out_norm_pool=row(out_norm_pool), out_norm_attn=row(out_norm_attn),
                 qn=row(jnp.tile(q_norm, N_HEADS)), kn=row(jnp.tile(k_norm, N_HEADS)),
                 b_forget=row(jnp.pad(b_forget, (0, LANES - N_HEADS))))

    lpart, gx, big, part = _local_step(x.reshape(n_batch * seq, d), loss_target.reshape(n_batch * seq, d),
                                       small, full, n_batch, seq)

    d_w_in = jnp.pad(big["w_in_t"][:in_cols].reshape(N_CHIPS, in_rows, d), ((0, 0), (0, in_pad - in_rows), (0, 0)))
    d_w_out = jnp.concatenate([big["woa"], big["wob"]], axis=0).reshape(N_CHIPS, w_out.shape[0], d)
    stacks = dict(ffn1_w_gate=big["wg1"], ffn1_w_up=big["wu1"], ffn1_w_down=big["wd1"], w_in=d_w_in, w_out=d_w_out,
                  ffn2_w_gate=big["wg2"], ffn2_w_up=big["wu2"], ffn2_w_down=big["wd2"])
    reduced = dict(zip(BIG_NAMES, _reduce_to_owner([stacks[n] for n in BIG_NAMES], ids, BIG_NAMES)))
    reduced["w_in"] = reduced["w_in"][:in_rows]

    part = dict(part, pool_w=part["pool_w"].reshape(n_batch, -1, pool_w.shape[-1]), loss=lpart)
    g_vec, g_pw = _small_allreduce(part, d, width)
    loss = g_vec[5, 0]
    grads, delta, new_m, new_v = {}, {}, {}, {}
    for n in BIG_NAMES:
        stepped = _adamw(work(w[n], n), reduced[n], work(m[n], n), work(v[n], n), n)
        grads[n], delta[n], new_m[n], new_v[n] = (work(a, n) for a in (reduced[n], *stepped))
    flat_pw = lambda a: a.reshape(-1, a.shape[-1])
    d_pw, m_pw, v_pw = _adamw(flat_pw(pool_w), g_pw, flat_pw(m_pool_w), flat_pw(v_pool_w), "pool_w")
    d_vec, m_vec, v_vec = _adamw(_pack_vec(w, d, width), g_vec, _pack_vec(m, d, width), _pack_vec(v, d, width), "vectors")
    grads.update(_unpack_vec(g_vec, width), pool_w=g_pw.reshape(pool_w.shape))
    delta.update(_unpack_vec(d_vec, width), pool_w=d_pw.reshape(pool_w.shape))
    new_m.update(_unpack_vec(m_vec, width), pool_w=m_pw.reshape(pool_w.shape))
    new_v.update(_unpack_vec(v_vec, width), pool_w=v_pw.reshape(pool_w.shape))
    return (loss, gx.reshape(x.shape), *[grads[n] for n in WEIGHT_NAMES], *[delta[n] for n in WEIGHT_NAMES],
            *[new_m[n] for n in WEIGHT_NAMES], *[new_v[n] for n in WEIGHT_NAMES])
```

```python
import functools

import jax
import jax.numpy as jnp
from jax import lax
from jax.experimental import pallas as pl
from jax.experimental.pallas import tpu as pltpu

F32 = jnp.float32
BF16 = jnp.bfloat16
EPS = 1e-6
NEG = -1e30
ADAM_LR = 0.001
ADAM_B1 = 0.9
ADAM_B2 = 0.999
ADAM_EPS = 1e-08
ADAM_WD = 0.01
ADAM_STEP = 10
POOL_WINDOWS = (2, 4, 8, 16)
HEAD_DIM = 64
N_HEADS = 8
LANES = 128
N_CHIPS = 4
ATT_BLOCK = 512
ATT_SUB = 128
VMEM_LIMIT = 56 * 1024 * 1024
MESH_AXES = ("x", "y", "c")
ANY = pl.BlockSpec(memory_space=pl.ANY)
VM = pl.BlockSpec(memory_space=pltpu.VMEM)


def _params(**kw):
    return pltpu.CompilerParams(vmem_limit_bytes=VMEM_LIMIT, **kw)


def _dot(a, b):
    return jnp.dot(a, b, preferred_element_type=F32)


def _dot_nt(a, b):
    return lax.dot_general(a, b, (((1,), (1,)), ((), ())), preferred_element_type=F32)


def _dot_tn(a, b):
    return lax.dot_general(a, b, (((0,), (0,)), ((), ())), preferred_element_type=F32)


def _sigmoid(z):
    return 1.0 / (1.0 + jnp.exp(-z))


def _rms(xf):
    return lax.rsqrt(jnp.mean(xf * xf, axis=-1, keepdims=True) + EPS)


def _rms_bwd(xf, r, gain, dh):
    xh = xf * r
    dyg = dh * gain
    return r * (dyg - xh * jnp.mean(dyg * xh, axis=-1, keepdims=True)), dh * xh


def _total(v):
    return jnp.sum(jnp.sum(v, axis=1, keepdims=True), axis=0, keepdims=True)


def _ffn_fwd(x, gain, wg, wu, wd, target=None, plan=None):
    t, d = x.shape
    nch, fc, _ = wg.shape
    tm = min(512, t)
    nt = t // tm
    with_loss = target is not None

    def body(*refs):
        if with_loss:
            x_ref, g_ref, wg_ref, wu_ref, wd_ref, t_ref, o_ref, h_ref, a_ref, b_ref, s_ref, l_ref, acc_ref = refs
        else:
            x_ref, g_ref, wg_ref, wu_ref, wd_ref, o_ref, h_ref, a_ref, b_ref, s_ref, acc_ref = refs
        k = pl.program_id(1)

        @pl.when(k == 0)
        def _():
            xf = x_ref[...]
            h_ref[...] = ((xf * _rms(xf)) * g_ref[...]).astype(BF16)
            acc_ref[...] = jnp.zeros_like(acc_ref)

        h = h_ref[...]
        a = _dot_nt(h, wg_ref[...])
        b = _dot_nt(h, wu_ref[...])
        sb = ((a * (0.5 * jnp.tanh(0.5 * a) + 0.5)) * b).astype(BF16)
        a_ref[...] = a.astype(BF16)
        b_ref[...] = b.astype(BF16)
        s_ref[...] = sb
        acc_ref[...] += _dot(sb, wd_ref[...])

        @pl.when(k == nch - 1)
        def _():
            y = x_ref[...] + 0.5 * acc_ref[...]
            if with_loss:
                e = y - t_ref[...]
                o_ref[...] = e * (1.0 / d)
                l_ref[...] = jnp.broadcast_to(_total(e * e) * (0.5 / d), l_ref.shape)
            else:
                o_ref[...] = y

    row = pl.BlockSpec((tm, d), lambda i, k: (i, 0))
    chunk = pl.BlockSpec((None, fc, d), lambda i, k: (k, 0, 0))
    act = pl.BlockSpec((None, tm, fc), lambda i, k: (k, i, 0))
    in_specs = [row, pl.BlockSpec((1, d), lambda i, k: (0, 0)), chunk, chunk, chunk]
    out_shape = [jax.ShapeDtypeStruct((t, d), F32), jax.ShapeDtypeStruct((t, d), BF16)]
    out_shape += [jax.ShapeDtypeStruct((nch, t, fc), BF16)] * 3
    out_specs = [row, row, act, act, act]
    args = [x, gain, wg, wu, wd]
    if with_loss:
        in_specs.append(row)
        args.append(target)
        out_shape.append(jax.ShapeDtypeStruct((nt, 8, LANES), F32))
        out_specs.append(pl.BlockSpec((None, 8, LANES), lambda i, k: (i, 0, 0)))
    return _pallas(body, name="ffn_fwd_loss" if with_loss else "ffn_fwd", args=args, in_specs=in_specs,
                   out_shape=out_shape, out_specs=out_specs, grid=(nt, nch),
                   scratch_shapes=[pltpu.VMEM((tm, d), F32)], plan=plan)


def _ffn_bwd_x(dy, x, gain, a, b, wg, wu, wd, name, plan=None):
    t, d = x.shape
    nch, fc, _ = wg.shape
    tm = min(512, t)
    nt = t // tm

    def body(dy_ref, x_ref, g_ref, a_ref, b_ref, wg_ref, wu_ref, wd_ref, dx_ref, da_ref, db_ref, dg_ref, acc_ref):
        k = pl.program_id(1)

        @pl.when(k == 0)
        def _():
            acc_ref[...] = jnp.zeros_like(acc_ref)

        ds = _dot_nt(dy_ref[...].astype(BF16), wd_ref[...])
        av = a_ref[...].astype(F32)
        bv = b_ref[...].astype(F32)
        th = jnp.tanh(0.5 * av)
        half_sig = 0.25 * th + 0.25
        dab = ((ds * bv) * (half_sig * (1.0 + av * (0.5 - 0.5 * th)))).astype(BF16)
        dbb = (ds * (av * half_sig)).astype(BF16)
        da_ref[...] = dab
        db_ref[...] = dbb
        acc_ref[...] += _dot(dab, wg_ref[...]) + _dot(dbb, wu_ref[...])

        @pl.when(k == nch - 1)
        def _():
            xf = x_ref[...]
            dxn, dgr = _rms_bwd(xf, _rms(xf), g_ref[...], acc_ref[...])
            dx_ref[...] = dy_ref[...] + dxn
            dg_ref[...] = jnp.sum(dgr, axis=0, keepdims=True)

    row = pl.BlockSpec((tm, d), lambda i, k: (i, 0))
    chunk = pl.BlockSpec((None, fc, d), lambda i, k: (k, 0, 0))
    act = pl.BlockSpec((None, tm, fc), lambda i, k: (k, i, 0))
    return _pallas(
        body, name=name, args=[dy, x, gain, a, b, wg, wu, wd],
        out_shape=[jax.ShapeDtypeStruct((t, d), F32), jax.ShapeDtypeStruct((nch, t, fc), BF16),
                   jax.ShapeDtypeStruct((nch, t, fc), BF16), jax.ShapeDtypeStruct((nt, 1, d), F32)],
        grid=(nt, nch),
        in_specs=[row, row, pl.BlockSpec((1, d), lambda i, k: (0, 0)), act, act, chunk, chunk, chunk],
        out_specs=[row, act, act, pl.BlockSpec((None, 1, d), lambda i, k: (i, 0, 0))],
        scratch_shapes=[pltpu.VMEM((tm, d), F32)], plan=plan)


def _ffn_bwd_w(h, s, da, db, dy, name, plan=None):
    t, d = h.shape
    nch, _, fc = s.shape
    tm = min(1024, t)
    nt = t // tm

    def body(h_ref, s_ref, da_ref, db_ref, dy_ref, dwg_ref, dwu_ref, dwd_ref):
        @pl.when(pl.program_id(1) == 0)
        def _():
            dwg_ref[...] = jnp.zeros_like(dwg_ref)
            dwu_ref[...] = jnp.zeros_like(dwu_ref)
            dwd_ref[...] = jnp.zeros_like(dwd_ref)

        hv = h_ref[...]
        dwg_ref[...] += _dot_tn(da_ref[...], hv)
        dwu_ref[...] += _dot_tn(db_ref[...], hv)
        dwd_ref[...] += _dot_tn(s_ref[...], (0.5 * dy_ref[...]).astype(BF16))

    row = pl.BlockSpec((tm, d), lambda k, i: (i, 0))
    act = pl.BlockSpec((None, tm, fc), lambda k, i: (k, i, 0))
    chunk = pl.BlockSpec((None, fc, d), lambda k, i: (k, 0, 0))
    return _pallas(body, name=name, args=[h, s, da, db, dy], out_shape=[jax.ShapeDtypeStruct((nch, fc, d), F32)] * 3,
                   grid=(nch, nt), in_specs=[row, act, act, act, row], out_specs=[chunk, chunk, chunk], plan=plan)


def _head_masks():
    lane = lax.broadcasted_iota(jnp.int32, (1, LANES), 1)
    return lane < HEAD_DIM


def _head_rms(x, lo):
    x2 = x * x
    s0 = jnp.sum(jnp.where(lo, x2, 0.0), axis=1, keepdims=True)
    s1 = jnp.sum(jnp.where(lo, 0.0, x2), axis=1, keepdims=True)
    return jnp.where(lo, lax.rsqrt(s0 * (1.0 / HEAD_DIM) + EPS), lax.rsqrt(s1 * (1.0 / HEAD_DIM) + EPS))


def _head_mean(v, lo):
    s0 = jnp.sum(jnp.where(lo, v, 0.0), axis=1, keepdims=True)
    s1 = jnp.sum(jnp.where(lo, 0.0, v), axis=1, keepdims=True)
    return jnp.where(lo, s0, s1) * (1.0 / HEAD_DIM)


def _mix_proj(x1, gain, wt, qn, kn, pool_width, attn_width):
    t, d = x1.shape
    tm = min(512, t)
    nt = t // tm
    scale = HEAD_DIM ** -0.5
    c_q, c_k, c_v = pool_width, pool_width + attn_width, pool_width + 2 * attn_width
    c_f = c_v + attn_width

    def body(x_ref, g_ref, wt_ref, qn_ref, kn_ref, hm_ref, pv_ref, q_ref, k_ref, qh_ref, kh_ref, vb_ref, f_ref):
        xf = x_ref[...]
        hm = ((xf * _rms(xf)) * g_ref[...]).astype(BF16)
        hm_ref[...] = hm
        f_ref[...] = _dot_nt(hm, wt_ref[c_f:c_f + LANES, :])
        pv_ref[...] = _dot_nt(hm, wt_ref[0:pool_width, :])
        vb_ref[...] = _dot_nt(hm, wt_ref[c_v:c_v + attn_width, :]).astype(BF16)
        lo = _head_masks()
        for c0, raw_ref, hat_ref, n_ref, mul in ((c_q, q_ref, qh_ref, qn_ref, scale), (c_k, k_ref, kh_ref, kn_ref, 1.0)):
            raw = _dot_nt(hm, wt_ref[c0:c0 + attn_width, :])
            raw_ref[...] = raw
            for blk in range(attn_width // LANES):
                sl = slice(blk * LANES, (blk + 1) * LANES)
                xb = raw[:, sl]
                hat_ref[:, sl] = (((xb * _head_rms(xb, lo)) * n_ref[:, sl]) * mul).astype(BF16)

    row = pl.BlockSpec((tm, d), lambda i: (i, 0))
    half = pl.BlockSpec((tm, attn_width), lambda i: (i, 0))
    const = lambda shape: pl.BlockSpec(shape, lambda i: (0, 0))
    return pl.pallas_call(
        body,
        out_shape=[jax.ShapeDtypeStruct((t, d), BF16), jax.ShapeDtypeStruct((t, pool_width), F32),
                   jax.ShapeDtypeStruct((t, attn_width), F32), jax.ShapeDtypeStruct((t, attn_width), F32),
                   jax.ShapeDtypeStruct((t, attn_width), BF16), jax.ShapeDtypeStruct((t, attn_width), BF16),
                   jax.ShapeDtypeStruct((t, attn_width), BF16), jax.ShapeDtypeStruct((t, LANES), F32)],
        grid=(nt,),
        in_specs=[row, const((1, d)), const(wt.shape), const((1, attn_width)), const((1, attn_width))],
        out_specs=[row, pl.BlockSpec((tm, pool_width), lambda i: (i, 0)), half, half, half, half, half,
                   pl.BlockSpec((tm, LANES), lambda i: (i, 0))],
        compiler_params=_params(), name="mix_proj",
    )(x1, gain, wt, qn, kn)


def _shift_down(v, dist, row):
    return jnp.where(row >= dist, pltpu.roll(v, dist, 0), 0.0)


def _shift_up(v, dist, row, n):
    return jnp.where(row + dist < n, pltpu.roll(v, n - dist, 0), 0.0)


def _aug_lane(e):
    return HEAD_DIM if e == 0 else 0


def _forget_prefix(f, bias, qh, kh, n_batch, seq):
    def body(f_ref, b_ref, q_ref, k_ref, qa_ref, ka_ref):
        z = f_ref[...] + b_ref[...]
        acc = jnp.minimum(z, 0.0) - jnp.log(1.0 + jnp.exp(-jnp.abs(z)))
        row = lax.broadcasted_iota(jnp.int32, (seq, 1), 0)
        dist = 1
        while dist < seq:
            acc = acc + _shift_down(acc, dist, row)
            dist *= 2
        lane = lax.broadcasted_iota(jnp.int32, (1, LANES), 1)
        for h in range(N_HEADS):
            pair, e = divmod(h, 2)
            a0 = _aug_lane(e)
            own = (lane < HEAD_DIM) if e == 0 else (lane >= HEAD_DIM)
            fh = _pick_lane(acc, h)
            hi = fh.astype(BF16).astype(F32)
            rest = fh - hi
            mid = rest.astype(BF16).astype(F32)
            low = rest - mid
            q_ones = (lane >= a0 + 3) & (lane < a0 + 6)
            k_ones = (lane >= a0) & (lane < a0 + 3)
            q_aug = jnp.where(lane == a0, hi, jnp.where(lane == a0 + 1, mid, jnp.where(lane == a0 + 2, low,
                              jnp.where(q_ones, 1.0, 0.0))))
            k_aug = jnp.where(k_ones, 1.0, jnp.where(lane == a0 + 3, -hi, jnp.where(lane == a0 + 4, -mid,
                              jnp.where(lane == a0 + 5, -low, 0.0))))
            src = slice(pair * LANES, (pair + 1) * LANES)
            dst = slice(h * LANES, (h + 1) * LANES)
            qa_ref[:, dst] = jnp.where(own, q_ref[:, src].astype(F32), q_aug).astype(BF16)
            ka_ref[:, dst] = jnp.where(own, k_ref[:, src].astype(F32), k_aug).astype(BF16)

    width = qh.shape[1]
    tok = pl.BlockSpec((seq, width), lambda b: (b, 0))
    aug = pl.BlockSpec((seq, N_HEADS * LANES), lambda b: (b, 0))
    return pl.pallas_call(
        body, out_shape=[jax.ShapeDtypeStruct((n_batch * seq, N_HEADS * LANES), BF16)] * 2, grid=(n_batch,),
        in_specs=[pl.BlockSpec((seq, LANES), lambda b: (b, 0)), pl.BlockSpec((1, LANES), lambda b: (0, 0)), tok, tok],
        out_specs=[aug, aug], compiler_params=_params(), name="forget_prefix",
    )(f, bias, qh, kh)


def _pool_groups(pv_ref, pw_ref, ps_ref, seq):
    row = lax.broadcasted_iota(jnp.int32, (seq, 1), 0)
    pos = (row + 1).astype(F32)
    out = []
    for g, win in enumerate(POOL_WINDOWS):
        sl = slice(g * LANES, (g + 1) * LANES)
        xg = pv_ref[:, sl]
        acc = xg
        dist = 1
        while dist < win:
            acc = acc + _shift_down(acc, dist, row)
            dist *= 2
        pooled = (acc / jnp.minimum(pos, float(win)) - xg).astype(BF16)
        mixed = _dot(pooled, pw_ref[g])
        out.append((pooled, mixed, mixed * ps_ref[:, sl]))
    return out


def _pool_fwd(pv, pw, ps, onp, n_batch, seq):
    width = pv.shape[1]

    def body(pv_ref, pw_ref, ps_ref, on_ref, y_ref):
        groups = _pool_groups(pv_ref, pw_ref, ps_ref, seq)
        ssq = sum(jnp.sum(ms * ms, axis=1, keepdims=True) for _, _, ms in groups)
        r = lax.rsqrt(ssq * (1.0 / width) + EPS)
        for g, (_, _, ms) in enumerate(groups):
            sl = slice(g * LANES, (g + 1) * LANES)
            y_ref[:, sl] = ((ms * r) * on_ref[:, sl]).astype(BF16)

    return pl.pallas_call(
        body, out_shape=jax.ShapeDtypeStruct((n_batch * seq, width), BF16), grid=(n_batch,),
        in_specs=[pl.BlockSpec((seq, width), lambda b: (b, 0)), pl.BlockSpec(pw.shape, lambda b: (0, 0, 0)),
                  pl.BlockSpec((1, width), lambda b: (0, 0)), pl.BlockSpec((1, width), lambda b: (0, 0))],
        out_specs=pl.BlockSpec((seq, width), lambda b: (b, 0)),
        compiler_params=_params(), name="pool_fwd",
    )(pv, pw, ps, onp)


def _pool_bwd(pv, dyp, pw, ps, onp, n_batch, seq):
    width = pv.shape[1]

    def body(pv_ref, dy_ref, pw_ref, ps_ref, on_ref, dpv_ref, dpw_ref, dps_ref, don_ref):
        groups = _pool_groups(pv_ref, pw_ref, ps_ref, seq)
        ssq = sum(jnp.sum(ms * ms, axis=1, keepdims=True) for _, _, ms in groups)
        r = lax.rsqrt(ssq * (1.0 / width) + EPS)
        mean = sum(jnp.sum((dy_ref[:, g * LANES:(g + 1) * LANES] * on_ref[:, g * LANES:(g + 1) * LANES]) * (ms * r),
                           axis=1, keepdims=True) for g, (_, _, ms) in enumerate(groups)) * (1.0 / width)
        row = lax.broadcasted_iota(jnp.int32, (seq, 1), 0)
        pos = (row + 1).astype(F32)
        for g, (pooled, mixed, ms) in enumerate(groups):
            sl = slice(g * LANES, (g + 1) * LANES)
            dy = dy_ref[:, sl]
            xh = ms * r
            don_ref[:, sl] = jnp.sum(dy * xh, axis=0, keepdims=True)
            dms = r * (dy * on_ref[:, sl] - xh * mean)
            dps_ref[:, sl] = jnp.sum(dms * mixed, axis=0, keepdims=True)
            dmix = (dms * ps_ref[:, sl]).astype(BF16)
            dpw_ref[g] = _dot_tn(pooled, dmix)
            dpool = _dot_nt(dmix, pw_ref[g])
            win = POOL_WINDOWS[g]
            acc = dpool / jnp.minimum(pos, float(win))
            dist = 1
            while dist < win:
                acc = acc + _shift_up(acc, dist, row, seq)
                dist *= 2
            dpv_ref[:, sl] = (acc - dpool).astype(BF16)

    tok = pl.BlockSpec((seq, width), lambda b: (b, 0))
    vec = pl.BlockSpec((1, width), lambda b: (0, 0))
    pvec = pl.BlockSpec((None, 1, width), lambda b: (b, 0, 0))
    return pl.pallas_call(
        body,
        out_shape=[jax.ShapeDtypeStruct((n_batch * seq, width), BF16),
                   jax.ShapeDtypeStruct((n_batch,) + pw.shape, F32),
                   jax.ShapeDtypeStruct((n_batch, 1, width), F32), jax.ShapeDtypeStruct((n_batch, 1, width), F32)],
        grid=(n_batch,),
        in_specs=[tok, tok, pl.BlockSpec(pw.shape, lambda b: (0, 0, 0)), vec, vec],
        out_specs=[tok, pl.BlockSpec((None,) + pw.shape, lambda b: (b, 0, 0, 0)), pvec, pvec],
        compiler_params=_params(), name="pool_bwd",
    )(pv, dyp, pw, ps, onp)


def _pick_lane(tile, idx):
    lane = lax.broadcasted_iota(jnp.int32, (1, LANES), 1)
    return jnp.sum(jnp.where(lane == idx, tile, 0.0), axis=1, keepdims=True)


def _pick_row(tile, idx):
    sub = lax.broadcasted_iota(jnp.int32, (tile.shape[0], 1), 0)
    return jnp.sum(jnp.where(sub == idx, tile, 0.0), axis=0, keepdims=True)


def _put_lane(col, idx):
    lane = lax.broadcasted_iota(jnp.int32, (1, LANES), 1)
    return jnp.where(lane == idx, col, 0.0)


def _head_select(e):
    lo = _head_masks()
    return lo if e == 0 else jnp.logical_not(lo)


def _causal(st, shift):
    row = lax.broadcasted_iota(jnp.int32, st.shape, 0)
    col = lax.broadcasted_iota(jnp.int32, st.shape, 1) + shift
    return jnp.where(col >= row, st, NEG)


def _stat_rows(ref, head, nsub):
    return jnp.concatenate([_pick_row(ref[a], head) for a in range(nsub)], axis=1)


def _accumulate(ref, value, first):
    @pl.when(first)
    def _():
        ref[...] = value

    @pl.when(jnp.logical_not(first))
    def _():
        ref[...] += value


def _attn_fwd(qa, ka, vb, n_batch, seq):
    tq = min(ATT_BLOCK, seq)
    nq, nsub, tk = seq // tq, tq // ATT_SUB, tq
    pairs = vb.shape[1] // LANES

    def body(q_ref, k_ref, v_ref, o_ref, lse_ref, acc_ref):
        i, p = pl.program_id(1), pl.program_id(2)
        row_lo = lax.broadcasted_iota(jnp.int32, (LANES, 1), 0) < HEAD_DIM
        qs = [q_ref[:, e * LANES:(e + 1) * LANES] for e in range(2)]
        acc_ref[...] = jnp.zeros_like(acc_ref)

        def tile(off, stats, diagonal):
            vj = v_ref[pl.ds(off, tk), :]
            new, alphas, pvs = [], [], []
            for e in range(2):
                st = _dot_nt(k_ref[pl.ds(off, tk), e * LANES:(e + 1) * LANES], qs[e])
                if diagonal:
                    st = _causal(st, 0)
                m, l = stats[e]
                m_new = jnp.maximum(m, jnp.max(st, axis=0, keepdims=True))
                alpha = jnp.exp(m - m_new)
                pt = jnp.exp(st - m_new)
                new.append((m_new, alpha * l + jnp.sum(pt, axis=0, keepdims=True)))
                alphas.append(alpha)
                pvs.append(_dot_tn(jnp.where(_head_select(e), vj, jnp.zeros_like(vj)), pt.astype(BF16)))
            acc_ref[...] = acc_ref[...] * jnp.where(row_lo, alphas[0], alphas[1]) + (pvs[0] + pvs[1])
            return tuple(new)

        init = ((jnp.full((1, tq), NEG, F32), jnp.zeros((1, tq), F32)),) * 2
        stats = lax.fori_loop(0, i, lambda j, st: tile(pl.multiple_of(j * tk, tk), st, False), init)
        (m0, l0), (m1, l1) = tile(pl.multiple_of(i * tk, tk), stats, True)
        out_t = acc_ref[...] / jnp.where(row_lo, l0, l1)
        sub = lax.broadcasted_iota(jnp.int32, (8, 1), 0)
        lse0, lse1 = m0 + jnp.log(l0), m1 + jnp.log(l1)
        for a in range(nsub):
            sl = slice(a * ATT_SUB, (a + 1) * ATT_SUB)
            o_ref[sl, :] = out_t[:, sl].T
            rows = jnp.where(sub == 2 * p, lse0[:, sl], 0.0) + jnp.where(sub == 2 * p + 1, lse1[:, sl], 0.0)
            _accumulate(lse_ref.at[a], rows, p == 0)

    return pl.pallas_call(
        body,
        out_shape=[jax.ShapeDtypeStruct((n_batch * seq, pairs * LANES), F32),
                   jax.ShapeDtypeStruct((n_batch * seq // ATT_SUB, 8, ATT_SUB), F32)],
        grid=(n_batch, nq, pairs),
        in_specs=[pl.BlockSpec((tq, 2 * LANES), lambda b, i, p: (b * nq + i, p)),
                  pl.BlockSpec((seq, 2 * LANES), lambda b, i, p: (b, p)),
                  pl.BlockSpec((seq, LANES), lambda b, i, p: (b, p))],
        out_specs=[pl.BlockSpec((tq, LANES), lambda b, i, p: (b * nq + i, p)),
                   pl.BlockSpec((nsub, 8, ATT_SUB), lambda b, i, p: (b * nq + i, 0, 0))],
        scratch_shapes=[pltpu.VMEM((LANES, tq), F32)],
        compiler_params=_params(), name="attn_fwd",
    )(qa, ka, vb)


def _attn_bwd_q(qa, ka, vb, do, lse, delta, n_batch, seq, plan=None):
    tq = min(ATT_BLOCK, seq)
    nq, nsub, tk = seq // tq, tq // ATT_SUB, tq
    pairs = vb.shape[1] // LANES

    def body(q_ref, k_ref, v_ref, do_ref, lse_ref, dl_ref, dq_ref, dfq_ref, acc0_ref, acc1_ref):
        i, p = pl.program_id(1), pl.program_id(2)
        accs = (acc0_ref, acc1_ref)
        qs = [q_ref[:, e * LANES:(e + 1) * LANES] for e in range(2)]
        dov = do_ref[...]
        ls = [_stat_rows(lse_ref, 2 * p + e, nsub) for e in range(2)]
        dl = [_stat_rows(dl_ref, 2 * p + e, nsub) for e in range(2)]
        for acc in accs:
            acc[...] = jnp.zeros_like(acc)

        def tile(off, diagonal):
            vj = v_ref[pl.ds(off, tk), :]
            for e in range(2):
                kj = k_ref[pl.ds(off, tk), e * LANES:(e + 1) * LANES]
                st = _dot_nt(kj, qs[e])
                if diagonal:
                    st = _causal(st, 0)
                pt = jnp.exp(st - ls[e])
                dpt = _dot_nt(jnp.where(_head_select(e), vj, jnp.zeros_like(vj)), dov)
                accs[e][...] += _dot_tn((pt * (dpt - dl[e])).astype(BF16), kj)

        def step(j, carry):
            tile(pl.multiple_of(j * tk, tk), False)
            return carry

        lax.fori_loop(0, i, step, 0)
        tile(pl.multiple_of(i * tk, tk), True)
        dq0, dq1 = acc0_ref[...], acc1_ref[...]
        dq_ref[...] = jnp.where(_head_masks(), dq0, dq1)
        dfq = _put_lane(_pick_lane(dq0, _aug_lane(0)), 2 * p) + _put_lane(_pick_lane(dq1, _aug_lane(1)), 2 * p + 1)
        _accumulate(dfq_ref, dfq, p == 0)

    stat = pl.BlockSpec((nsub, 8, ATT_SUB), lambda b, i, p: (b * nq + i, 0, 0))
    blk = pl.BlockSpec((tq, LANES), lambda b, i, p: (b * nq + i, p))
    return _pallas(
        body, name="attn_bwd_q", args=[qa, ka, vb, do, lse, delta],
        out_shape=[jax.ShapeDtypeStruct((n_batch * seq, pairs * LANES), F32), jax.ShapeDtypeStruct((n_batch * seq, LANES), F32)],
        grid=(n_batch, nq, pairs),
        in_specs=[pl.BlockSpec((tq, 2 * LANES), lambda b, i, p: (b * nq + i, p)),
                  pl.BlockSpec((seq, 2 * LANES), lambda b, i, p: (b, p)),
                  pl.BlockSpec((seq, LANES), lambda b, i, p: (b, p)), blk, stat, stat],
        out_specs=[blk, pl.BlockSpec((tq, LANES), lambda b, i, p: (b * nq + i, 0))],
        scratch_shapes=[pltpu.VMEM((tq, LANES), F32), pltpu.VMEM((tq, LANES), F32)], plan=plan)


def _attn_bwd_kv(qa, ka, vb, do, lse, delta, n_batch, seq, plan=None):
    tkb = min(ATT_BLOCK, seq)
    nk, nsub, tq = seq // tkb, tkb // ATT_SUB, tkb
    n_tiles = seq // ATT_SUB
    pairs = vb.shape[1] // LANES

    def body(q_ref, k_ref, v_ref, do_ref, lse_ref, dl_ref, dk_ref, dv_ref, dfk_ref, dk0_ref, dk1_ref, dva_ref):
        j, p = pl.program_id(1), pl.program_id(2)
        dks = (dk0_ref, dk1_ref)
        ks = [k_ref[:, e * LANES:(e + 1) * LANES] for e in range(2)]
        vj = v_ref[...]
        vs = [jnp.where(_head_select(e), vj, jnp.zeros_like(vj)) for e in range(2)]
        for acc in (dk0_ref, dk1_ref, dva_ref):
            acc[...] = jnp.zeros_like(acc)

        def tile(t, diagonal):
            off = pl.multiple_of(t * tq, tq)
            dov = do_ref[pl.ds(off, tq), :]
            for e in range(2):
                qe = q_ref[pl.ds(off, tq), e * LANES:(e + 1) * LANES]
                st = _dot_nt(ks[e], qe)
                if diagonal:
                    st = _causal(st, 0)
                rows = lambda ref: jnp.concatenate([_pick_row(ref[t * nsub + a], 2 * p + e) for a in range(nsub)], axis=1)
                pt = jnp.exp(st - rows(lse_ref))
                dva_ref[...] += _dot(pt.astype(BF16), jnp.where(_head_select(e), dov, jnp.zeros_like(dov)))
                dst = pt * (_dot_nt(vs[e], dov) - rows(dl_ref))
                dks[e][...] += _dot(dst.astype(BF16), qe)

        def step(t, carry):
            tile(t, False)
            return carry

        lax.fori_loop(j + 1, nk, step, 0)
        tile(j, True)
        dk0, dk1 = dk0_ref[...], dk1_ref[...]
        dk_ref[...] = jnp.where(_head_masks(), dk0, dk1)
        dv_ref[...] = dva_ref[...].astype(BF16)
        dfk = (_put_lane(_pick_lane(dk0, _aug_lane(0) + 3), 2 * p)
               + _put_lane(_pick_lane(dk1, _aug_lane(1) + 3), 2 * p + 1))
        _accumulate(dfk_ref, -dfk, p == 0)

    stat = pl.BlockSpec((n_tiles, 8, ATT_SUB), lambda b, j, p: (b, 0, 0))
    blk = pl.BlockSpec((tkb, LANES), lambda b, j, p: (b * nk + j, p))
    acc = pltpu.VMEM((tkb, LANES), F32)
    return _pallas(
        body, name="attn_bwd_kv", args=[qa, ka, vb, do, lse, delta],
        out_shape=[jax.ShapeDtypeStruct((n_batch * seq, pairs * LANES), F32),
                   jax.ShapeDtypeStruct((n_batch * seq, pairs * LANES), BF16),
                   jax.ShapeDtypeStruct((n_batch * seq, LANES), F32)],
        grid=(n_batch, nk, pairs),
        in_specs=[pl.BlockSpec((seq, 2 * LANES), lambda b, j, p: (b, p)),
                  pl.BlockSpec((tkb, 2 * LANES), lambda b, j, p: (b * nk + j, p)), blk,
                  pl.BlockSpec((seq, LANES), lambda b, j, p: (b, p)), stat, stat],
        out_specs=[blk, blk, pl.BlockSpec((tkb, LANES), lambda b, j, p: (b * nk + j, 0))],
        scratch_shapes=[acc, acc, acc], plan=plan)


def _forget_bwd(dfq, dfk, f, bias, n_batch, seq):
    def body(dfq_ref, dfk_ref, f_ref, b_ref, df_ref, db_ref):
        acc = dfq_ref[...] + dfk_ref[...]
        row = lax.broadcasted_iota(jnp.int32, (seq, 1), 0)
        dist = 1
        while dist < seq:
            acc = acc + _shift_up(acc, dist, row, seq)
            dist *= 2
        df = acc * _sigmoid(-(f_ref[...] + b_ref[...]))
        df_ref[...] = df
        db_ref[...] = jnp.sum(df, axis=0, keepdims=True)

    col = pl.BlockSpec((seq, LANES), lambda b: (b, 0))
    return pl.pallas_call(
        body,
        out_shape=[jax.ShapeDtypeStruct((n_batch * seq, LANES), F32), jax.ShapeDtypeStruct((n_batch, 1, LANES), F32)],
        grid=(n_batch,), in_specs=[col, col, col, pl.BlockSpec((1, LANES), lambda b: (0, 0))],
        out_specs=[col, pl.BlockSpec((None, 1, LANES), lambda b: (b, 0, 0))],
        compiler_params=_params(), name="forget_bwd",
    )(dfq, dfk, f, bias)


def _mix_out(x1, yp, o, ona, woa, wob):
    t, d = x1.shape
    width = o.shape[1]
    tm = min(512, t)

    def body(x_ref, yp_ref, o_ref, on_ref, wa_ref, wb_ref, x2_ref, ya_ref):
        of = o_ref[...]
        ya = ((of * _rms(of)) * on_ref[...]).astype(BF16)
        ya_ref[...] = ya
        x2_ref[...] = x_ref[...] + (_dot(yp_ref[...], wa_ref[...]) + _dot(ya, wb_ref[...]))

    row = pl.BlockSpec((tm, d), lambda i: (i, 0))
    half = pl.BlockSpec((tm, width), lambda i: (i, 0))
    wspec = pl.BlockSpec((width, d), lambda i: (0, 0))
    return pl.pallas_call(
        body, out_shape=[jax.ShapeDtypeStruct((t, d), F32), jax.ShapeDtypeStruct((t, width), BF16)],
        grid=(t // tm,), in_specs=[row, half, half, pl.BlockSpec((1, width), lambda i: (0, 0)), wspec, wspec],
        out_specs=[row, half], compiler_params=_params(), name="mix_out",
    )(x1, yp, o, ona, woa, wob)


def _mix_out_bwd(dx2, o, yp, ya, ona, woa, wob, plan=None):
    t, d = dx2.shape
    width = o.shape[1]
    tm = min(512, t)
    nt = t // tm

    def body(dx_ref, o_ref, yp_ref, ya_ref, on_ref, wa_ref, wb_ref, dyp_ref, do_ref, dl_ref, dwa_ref, dwb_ref, don_ref):
        @pl.when(pl.program_id(0) == 0)
        def _():
            dwa_ref[...] = jnp.zeros_like(dwa_ref)
            dwb_ref[...] = jnp.zeros_like(dwb_ref)

        dxb = dx_ref[...].astype(BF16)
        dwa_ref[...] += _dot_tn(yp_ref[...], dxb)
        dwb_ref[...] += _dot_tn(ya_ref[...], dxb)
        dyp_ref[...] = _dot_nt(dxb, wa_ref[...])
        of = o_ref[...]
        dov, dgr = _rms_bwd(of, _rms(of), on_ref[...], _dot_nt(dxb, wb_ref[...]))
        don_ref[...] = jnp.sum(dgr, axis=0, keepdims=True)
        do_ref[...] = dov.astype(BF16)
        lo = _head_masks()
        prod = dov * of
        delta = jnp.zeros((tm, LANES), F32)
        for blk in range(width // LANES):
            pb = prod[:, blk * LANES:(blk + 1) * LANES]
            delta = delta + _put_lane(jnp.sum(jnp.where(lo, pb, 0.0), axis=1, keepdims=True), 2 * blk)
            delta = delta + _put_lane(jnp.sum(jnp.where(lo, 0.0, pb), axis=1, keepdims=True), 2 * blk + 1)
        for c in range(tm // ATT_SUB):
            dl_ref[c] = delta[c * ATT_SUB:(c + 1) * ATT_SUB, :].T[0:8, :]

    row = pl.BlockSpec((tm, d), lambda i: (i, 0))
    half = pl.BlockSpec((tm, width), lambda i: (i, 0))
    wspec = pl.BlockSpec((width, d), lambda i: (0, 0))
    return _pallas(
        body, name="mix_out_bwd", args=[dx2, o, yp, ya, ona, woa, wob],
        out_shape=[jax.ShapeDtypeStruct((t, width), F32), jax.ShapeDtypeStruct((t, width), BF16),
                   jax.ShapeDtypeStruct((t // ATT_SUB, 8, ATT_SUB), F32), jax.ShapeDtypeStruct((width, d), F32),
                   jax.ShapeDtypeStruct((width, d), F32), jax.ShapeDtypeStruct((nt, 1, width), F32)],
        grid=(nt,),
        in_specs=[row, half, half, half, pl.BlockSpec((1, width), lambda i: (0, 0)), wspec, wspec],
        out_specs=[half, half, pl.BlockSpec((tm // ATT_SUB, 8, ATT_SUB), lambda i: (i, 0, 0)), wspec, wspec,
                   pl.BlockSpec((None, 1, width), lambda i: (i, 0, 0))], plan=plan)


def _mix_in_bwd(dx2, x1, gain, hm, dpv, dqh, q, dkh, k, dv, df, qn, kn, wt):
    t, d = x1.shape
    width = q.shape[1]
    pool_width = dpv.shape[1]
    tm = min(512, t)
    nt = t // tm
    scale = HEAD_DIM ** -0.5
    c_q, c_k, c_v = pool_width, pool_width + width, pool_width + 2 * width
    c_f = c_v + width

    def body(dx2_ref, x_ref, g_ref, hm_ref, dpv_ref, dqh_ref, q_ref, dkh_ref, k_ref, dv_ref, df_ref, qn_ref, kn_ref,
             wt_ref, dx_ref, dwt_ref, dg_ref, dqn_ref, dkn_ref):
        @pl.when(pl.program_id(0) == 0)
        def _():
            dwt_ref[...] = jnp.zeros_like(dwt_ref)

        lo = _head_masks()
        hm = hm_ref[...]
        pieces = [(0, dpv_ref[...])]
        for c0, raw_ref, dh_ref, n_ref, dn_ref, mul in ((c_q, q_ref, dqh_ref, qn_ref, dqn_ref, scale),
                                                       (c_k, k_ref, dkh_ref, kn_ref, dkn_ref, 1.0)):
            cols = []
            for blk in range(width // LANES):
                sl = slice(blk * LANES, (blk + 1) * LANES)
                xb = raw_ref[:, sl]
                gb = dh_ref[:, sl] * mul
                r = _head_rms(xb, lo)
                xh = xb * r
                dyg = gb * n_ref[:, sl]
                cols.append((r * (dyg - xh * _head_mean(dyg * xh, lo))).astype(BF16))
                dn_ref[:, sl] = jnp.sum(gb * xh, axis=0, keepdims=True)
            pieces.append((c0, jnp.concatenate(cols, axis=1)))
        pieces.append((c_v, dv_ref[...]))
        pieces.append((c_f, df_ref[...].astype(BF16)))
        dhm = jnp.zeros((tm, d), F32)
        for c0, piece in pieces:
            dwt_ref[c0:c0 + piece.shape[1], :] += _dot_tn(piece, hm)
            dhm = dhm + _dot(piece, wt_ref[c0:c0 + piece.shape[1], :])
        xf = x_ref[...]
        dxn, dgr = _rms_bwd(xf, _rms(xf), g_ref[...], dhm)
        dx_ref[...] = dx2_ref[...] + dxn
        dg_ref[...] = jnp.sum(dgr, axis=0, keepdims=True)

    row = pl.BlockSpec((tm, d), lambda i: (i, 0))
    half = pl.BlockSpec((tm, width), lambda i: (i, 0))
    const = lambda shape: pl.BlockSpec(shape, lambda i: (0, 0))
    pvec = lambda n: pl.BlockSpec((None, 1, n), lambda i: (i, 0, 0))
    return pl.pallas_call(
        body,
        out_shape=[jax.ShapeDtypeStruct((t, d), F32), jax.ShapeDtypeStruct(wt.shape, F32),
                   jax.ShapeDtypeStruct((nt, 1, d), F32),
                   jax.ShapeDtypeStruct((nt, 1, width), F32), jax.ShapeDtypeStruct((nt, 1, width), F32)],
        grid=(nt,),
        in_specs=[row, row, const((1, d)), row, pl.BlockSpec((tm, pool_width), lambda i: (i, 0)), half, half, half, half,
                  half, pl.BlockSpec((tm, LANES), lambda i: (i, 0)), const((1, width)), const((1, width)),
                  const(wt.shape)],
        out_specs=[row, const(wt.shape), pvec(d), pvec(width), pvec(width)],
        compiler_params=_params(), name="mix_in_bwd",
    )(dx2, x1, gain, hm, dpv, dqh, q, dkh, k, dv, df, qn, kn, wt)


def _mesh_pos():
    return lax.axis_index("x"), lax.axis_index("y"), lax.axis_index("c")


def _other_chips(x, y):
    return [(1 - x, y), (x, 1 - y), (1 - x, 1 - y)]


def _remote(src, dst, send_sem, recv_sem, device):
    return pltpu.make_async_remote_copy(src_ref=src, dst_ref=dst, send_sem=send_sem, recv_sem=recv_sem,
                                        device_id=device, device_id_type=pl.DeviceIdType.MESH)


def _half_rows(n_rows, which):
    half = n_rows // 2
    return pl.ds(pl.multiple_of(which * half, 8), half)


def _row_block(rows, cols, itemsize=4):
    rb = rows
    while rb * cols * itemsize > (1 << 20) and rb % 32 == 0:
        rb //= 2
    return rb


def _place_cast(w, chip, tag):
    rows, cols = w.shape
    rb = _row_block(rows, cols)

    def body(k_ref, w_ref, o_ref):
        o_ref[...] = w_ref[...].astype(BF16)

    return pl.pallas_call(
        body, out_shape=jax.ShapeDtypeStruct((N_CHIPS, rows, cols), BF16),
        grid_spec=pltpu.PrefetchScalarGridSpec(
            num_scalar_prefetch=1, grid=(rows // rb,),
            in_specs=[pl.BlockSpec((rb, cols), lambda i, k: (i, 0))],
            out_specs=pl.BlockSpec((None, rb, cols), lambda i, k: (k[0], i, 0))),
        compiler_params=_params(), name="place_" + tag,
    )(chip, w)


class _Plan:
    def __init__(self, ins, outs, alias, sems, start, finish):
        self.ins, self.outs, self.alias, self.sems, self.start, self.finish = ins, outs, alias, sems, start, finish


def _run_plan(plan, name):
    n_in, n_out = len(plan.ins), len(plan.outs)

    def body(*refs):
        parts = refs[:n_in], refs[n_in:n_in + n_out], refs[n_in + n_out:]
        plan.start(*parts)
        plan.finish(*parts)

    return pl.pallas_call(
        body, out_shape=plan.outs, in_specs=[ANY] * n_in, out_specs=[ANY] * n_out, scratch_shapes=plan.sems,
        input_output_aliases=plan.alias, name=name,
    )(*plan.ins)


def _pallas(body, *, name, args, in_specs, out_shape, out_specs, grid, scratch_shapes=(), plan=None):
    n_in, n_out, n_scr = len(args), len(out_shape), len(scratch_shapes)
    if plan is None:
        res = pl.pallas_call(body, out_shape=out_shape, grid=grid, in_specs=in_specs, out_specs=out_specs,
                             scratch_shapes=scratch_shapes, compiler_params=_params(), name=name)(*args)
        return list(res), []
    p_in, p_out = len(plan.ins), len(plan.outs)

    def carrying(*refs):
        ins, p_ins = refs[:n_in], refs[n_in:n_in + p_in]
        o0 = n_in + p_in
        outs, p_outs = refs[o0:o0 + n_out], refs[o0 + n_out:o0 + n_out + p_out]
        s0 = o0 + n_out + p_out
        scr, p_sems = refs[s0:s0 + n_scr], refs[s0 + n_scr:]
        ids = [pl.program_id(a) for a in range(len(grid))]
        first = functools.reduce(jnp.logical_and, [i == 0 for i in ids])
        last = functools.reduce(jnp.logical_and, [i == g - 1 for i, g in zip(ids, grid)])

        @pl.when(first)
        def _():
            plan.start(p_ins, p_outs, p_sems)

        body(*ins, *outs, *scr)

        @pl.when(last)
        def _():
            plan.finish(p_ins, p_outs, p_sems)

    res = pl.pallas_call(
        carrying, out_shape=list(out_shape) + list(plan.outs), grid=grid,
        in_specs=list(in_specs) + [ANY] * p_in, out_specs=list(out_specs) + [ANY] * p_out,
        scratch_shapes=list(scratch_shapes) + list(plan.sems),
        input_output_aliases={n_in + i: n_out + o for i, o in plan.alias.items()},
        compiler_params=_params(), name=name,
    )(*args, *plan.ins)
    return list(res[:n_out]), list(res[n_out:])


def _plan_gather(stacks):
    n = len(stacks)

    def ici_copies(outs, sems):
        x, y, c = _mesh_pos()
        cps = []
        for w in range(n):
            own = outs[w].at[2 * x + y, _half_rows(stacks[w].shape[1], c)]
            cps += [_remote(own, own, sems[0].at[w, j], sems[1].at[w, j], (*chip, c)) for j, chip in enumerate(_other_chips(x, y))]
        return cps

    def start(ins, outs, sems):
        for cp in ici_copies(outs, sems):
            cp.start()

    def finish(ins, outs, sems):
        ici_send, ici_recv, d2d_send, d2d_recv = sems
        x, y, c = _mesh_pos()
        sibling = (x, y, 1 - c)
        slots = [2 * cx + cy for cx, cy in _other_chips(x, y)]
        forwards = []
        for w in range(n):
            rows = _half_rows(stacks[w].shape[1], c)
            for j in range(3):
                landed = outs[w].at[slots[j], rows]
                _remote(landed, landed, ici_send.at[w, j], ici_recv.at[w, j], sibling).wait_recv()
                cp = _remote(landed, landed, d2d_send.at[w, j], d2d_recv.at[w, j], sibling)
                cp.start()
                forwards.append(cp)
        for w in range(n):
            rows = _half_rows(stacks[w].shape[1], 1 - c)
            for j in range(3):
                landed = outs[w].at[slots[j], rows]
                _remote(landed, landed, d2d_send.at[w, j], d2d_recv.at[w, j], sibling).wait_recv()
        for cp in ici_copies(outs, sems) + forwards:
            cp.wait_send()

    return _Plan(stacks, [jax.ShapeDtypeStruct(s.shape, s.dtype) for s in stacks], {w: w for w in range(n)},
                 [pltpu.SemaphoreType.DMA((n, 3))] * 4, start, finish)


def _plan_sibling_halves(gs):
    n = len(gs)

    def copies(ins, outs, sems):
        x, y, c = _mesh_pos()
        return [_remote(ins[w].at[:, _half_rows(gs[w].shape[1], 1 - c), :], outs[w], sems[0].at[w], sems[1].at[w],
                        (x, y, 1 - c)) for w in range(n)]

    def start(ins, outs, sems):
        for cp in copies(ins, outs, sems):
            cp.start()

    def finish(ins, outs, sems):
        for cp in copies(ins, outs, sems):
            cp.wait()

    return _Plan(gs, [jax.ShapeDtypeStruct((g.shape[0], g.shape[1] // 2, g.shape[2]), g.dtype) for g in gs], {},
                 [pltpu.SemaphoreType.DMA((n,))] * 2, start, finish)


def _plan_chip_exchange(ps):
    n = len(ps)

    def copies(ins, outs, sems):
        x, y, c = _mesh_pos()
        return [_remote(ins[w].at[2 * cx + cy], outs[w].at[j], sems[0].at[w, j], sems[1].at[w, j], (cx, cy, c))
                for w in range(n) for j, (cx, cy) in enumerate(_other_chips(x, y))]

    def start(ins, outs, sems):
        for cp in copies(ins, outs, sems):
            cp.start()

    def finish(ins, outs, sems):
        for cp in copies(ins, outs, sems):
            cp.wait()

    return _Plan(ps, [jax.ShapeDtypeStruct((3,) + p.shape[1:], p.dtype) for p in ps], {},
                 [pltpu.SemaphoreType.DMA((n, 3))] * 2, start, finish)


def _plan_sibling_share(gs):
    n = len(gs)

    def copies(outs, sems, which):
        x, y, c = _mesh_pos()
        cps = []
        for w in range(n):
            rows = outs[w].at[_half_rows(gs[w].shape[0], c if which == "mine" else 1 - c)]
            cps.append(_remote(rows, rows, sems[0].at[w], sems[1].at[w], (x, y, 1 - c)))
        return cps

    def start(ins, outs, sems):
        for cp in copies(outs, sems, "mine"):
            cp.start()

    def finish(ins, outs, sems):
        for cp in copies(outs, sems, "mine"):
            cp.wait_send()
        for cp in copies(outs, sems, "theirs"):
            cp.wait_recv()

    return _Plan(gs, [jax.ShapeDtypeStruct(g.shape, g.dtype) for g in gs], {w: w for w in range(n)},
                 [pltpu.SemaphoreType.DMA((n,))] * 2, start, finish)


def _add_sibling(g, r1, ids, tag):
    nch, rh, cols = r1.shape

    def body(ids_ref, g_ref, r_ref, o_ref):
        o_ref[...] = (g_ref[...] + r_ref[...]).astype(BF16)

    blk = lambda fn: pl.BlockSpec((None, rh, cols), fn)
    return pl.pallas_call(
        body, out_shape=jax.ShapeDtypeStruct(r1.shape, BF16),
        grid_spec=pltpu.PrefetchScalarGridSpec(
            num_scalar_prefetch=1, grid=(nch,),
            in_specs=[blk(lambda k, ids: (k, ids[1], 0)), blk(lambda k, ids: (k, 0, 0))],
            out_specs=blk(lambda k, ids: (k, 0, 0))),
        compiler_params=_params(), name="add_sibling_" + tag,
    )(ids, g, r1)


def _add_chips(g, r1, r2, ids, tag):
    _, rh, cols = r1.shape

    def body(ids_ref, g_ref, r1_ref, r2_ref, o_ref):
        own = g_ref[...] + r1_ref[...]
        o_ref[...] = ((own + r2_ref[0].astype(F32)) + r2_ref[1].astype(F32)) + r2_ref[2].astype(F32)

    return pl.pallas_call(
        body, out_shape=jax.ShapeDtypeStruct((2 * rh, cols), F32),
        grid_spec=pltpu.PrefetchScalarGridSpec(
            num_scalar_prefetch=1, grid=(1,),
            in_specs=[pl.BlockSpec((None, rh, cols), lambda i, ids: (ids[0], ids[1], 0)),
                      pl.BlockSpec((None, rh, cols), lambda i, ids: (ids[0], 0, 0)),
                      pl.BlockSpec((3, rh, cols), lambda i, ids: (0, 0, 0))],
            out_specs=pl.BlockSpec((rh, cols), lambda i, ids: (ids[1], 0))),
        compiler_params=_params(), name="add_chips_" + tag,
    )(ids, g, r1, r2)


VEC_ROWS = 8


def _small_allreduce(part, d, width):
    names = ("ffn1_norm", "mix_norm", "ffn2_norm", "pool_scale", "out_norm_pool", "out_norm_attn", "qn", "kn", "b_forget",
             "pool_w", "loss")
    args = [part[k] for k in names]
    pw_shape = part["pool_w"].shape[1:]
    n_dev = 8

    def body(g1_ref, gm_ref, g2_ref, ps_ref, onp_ref, ona_ref, qn_ref, kn_ref, bf_ref, pw_ref, loss_ref,
             vec_ref, pwo_ref, vbuf, pbuf, send, recv):
        x, y, c = _mesh_pos()
        me = 4 * x + 2 * y + c
        lo = _head_masks()

        def fold_heads(ref):
            v = jnp.sum(ref[...], axis=0)
            acc = jnp.zeros((VEC_ROWS, LANES), F32)
            for blk in range(width // LANES):
                vb = jnp.broadcast_to(v[:, blk * LANES:(blk + 1) * LANES], (VEC_ROWS, LANES))
                acc = acc + vb + pltpu.roll(vb, HEAD_DIM, 1)
            return jnp.where(lo, acc, 0.0)[0:1, :]

        vbuf[0] = jnp.zeros((VEC_ROWS, d), F32)
        vbuf[0, 0:1, :] = jnp.sum(g1_ref[...], axis=0)
        vbuf[0, 1:2, :] = jnp.sum(gm_ref[...], axis=0)
        vbuf[0, 2:3, :] = jnp.sum(g2_ref[...], axis=0)
        vbuf[0, 5:6, 0:LANES] = jnp.sum(loss_ref[...], axis=0)[0:1, :]
        vbuf[0, 3:4, 0:width] = jnp.sum(ps_ref[...], axis=0)
        vbuf[0, 3:4, width:2 * width] = jnp.sum(onp_ref[...], axis=0)
        vbuf[0, 4:5, 0:width] = jnp.sum(ona_ref[...], axis=0)
        vbuf[0, 4:5, width:width + LANES] = fold_heads(qn_ref)
        vbuf[0, 4:5, width + LANES:width + 2 * LANES] = fold_heads(kn_ref)
        vbuf[0, 4:5, width + 2 * LANES:width + 3 * LANES] = jnp.sum(bf_ref[...], axis=0)
        pbuf[0] = jnp.sum(pw_ref[...], axis=0)

        cps = []
        for r in range(1, n_dev):
            peer = (x if not r & 4 else 1 - x, y if not r & 2 else 1 - y, c if not r & 1 else 1 - c)
            for buf, k in ((vbuf, 0), (pbuf, 1)):
                cp = _remote(buf.at[0], buf.at[r], send.at[k, r - 1], recv.at[k, r - 1], peer)
                cp.start()
                cps.append(cp)
        for cp in cps:
            cp.wait()
        vec = vbuf[me]
        pw = pbuf[me]
        for dev in range(1, n_dev):
            vec = vec + vbuf[jnp.bitwise_xor(me, dev)]
            pw = pw + pbuf[jnp.bitwise_xor(me, dev)]
        vec_ref[...] = vec
        pwo_ref[...] = pw

    return pl.pallas_call(
        body, out_shape=[jax.ShapeDtypeStruct((VEC_ROWS, d), F32), jax.ShapeDtypeStruct(pw_shape, F32)],
        in_specs=[VM] * len(args), out_specs=[VM, VM],
        scratch_shapes=[pltpu.VMEM((n_dev, VEC_ROWS, d), F32), pltpu.VMEM((n_dev,) + pw_shape, F32),
                        pltpu.SemaphoreType.DMA((2, n_dev - 1)), pltpu.SemaphoreType.DMA((2, n_dev - 1))],
        compiler_params=_params(), name="small_allreduce",
    )(*args)


def _adamw(w, g, m, v, tag):
    rows, cols = w.shape
    rb = rows
    while rb * cols * 4 > (1 << 20) and rb % 16 == 0:
        rb //= 2

    def body(w_ref, g_ref, m_ref, v_ref, d_ref, mo_ref, vo_ref):
        gv = g_ref[...]
        m2 = ADAM_B1 * m_ref[...] + (1.0 - ADAM_B1) * gv
        v2 = ADAM_B2 * v_ref[...] + (1.0 - ADAM_B2) * (gv * gv)
        m_hat = m2 / (1.0 - ADAM_B1 ** ADAM_STEP)
        v_hat = v2 / (1.0 - ADAM_B2 ** ADAM_STEP)
        d_ref[...] = -ADAM_LR * (m_hat / (jnp.sqrt(v_hat) + ADAM_EPS) + ADAM_WD * w_ref[...])
        mo_ref[...] = m2
        vo_ref[...] = v2

    spec = pl.BlockSpec((rb, cols), lambda i: (i, 0))
    return pl.pallas_call(
        body, out_shape=[jax.ShapeDtypeStruct(w.shape, F32)] * 3, grid=(rows // rb,),
        in_specs=[spec] * 4, out_specs=[spec] * 3, compiler_params=_params(), name="adamw_" + tag,
    )(w, g, m, v)


def _pack_vec(p, d, width):
    pad = lambda v: jnp.pad(v, (0, LANES - v.shape[0]))
    row3 = jnp.concatenate([p["pool_scale"], p["out_norm_pool"]])
    row4 = jnp.concatenate([p["out_norm_attn"], pad(p["q_norm"]), pad(p["k_norm"]), pad(p["b_forget"]),
                            jnp.zeros((d - width - 3 * LANES,), F32)])
    rows = [p["ffn1_norm"], p["mix_norm"], p["ffn2_norm"], row3, row4]
    return jnp.pad(jnp.stack(rows), ((0, VEC_ROWS - len(rows)), (0, 0)))


def _unpack_vec(vec, width):
    return dict(ffn1_norm=vec[0], mix_norm=vec[1], ffn2_norm=vec[2], pool_scale=vec[3, :width],
                out_norm_pool=vec[3, width:2 * width], out_norm_attn=vec[4, :width],
                q_norm=vec[4, width:width + HEAD_DIM], k_norm=vec[4, width + LANES:width + LANES + HEAD_DIM],
                b_forget=vec[4, width + 2 * LANES:width + 2 * LANES + N_HEADS])


WEIGHT_NAMES = ("ffn1_norm", "ffn1_w_gate", "ffn1_w_up", "ffn1_w_down", "mix_norm", "w_in", "b_forget", "pool_w",
                "pool_scale", "q_norm", "k_norm", "out_norm_pool", "out_norm_attn", "w_out", "ffn2_norm",
                "ffn2_w_gate", "ffn2_w_up", "ffn2_w_down")
BIG_NAMES = ("ffn1_w_gate", "ffn1_w_up", "ffn1_w_down", "w_in", "w_out", "ffn2_w_gate", "ffn2_w_up", "ffn2_w_down")
TRANSPOSED_NAMES = ("ffn1_w_gate", "ffn1_w_up", "w_in", "ffn2_w_gate", "ffn2_w_up")
FFN1_NAMES = ("ffn1_w_gate", "ffn1_w_up", "ffn1_w_down")
MIX_NAMES = ("w_in", "w_out")
FFN2_NAMES = ("ffn2_w_gate", "ffn2_w_up", "ffn2_w_down")
REST_NAMES = MIX_NAMES + FFN2_NAMES


def kernel(x, ffn1_norm, ffn1_w_gate, ffn1_w_up, ffn1_w_down, mix_norm, w_in, b_forget, pool_w, pool_scale, q_norm, k_norm, out_norm_pool, out_norm_attn, w_out, ffn2_norm, ffn2_w_gate, ffn2_w_up, ffn2_w_down, loss_target, m_ffn1_norm, m_ffn1_w_gate, m_ffn1_w_up, m_ffn1_w_down, m_mix_norm, m_w_in, m_b_forget, m_pool_w, m_pool_scale, m_q_norm, m_k_norm, m_out_norm_pool, m_out_norm_attn, m_w_out, m_ffn2_norm, m_ffn2_w_gate, m_ffn2_w_up, m_ffn2_w_down, v_ffn1_norm, v_ffn1_w_gate, v_ffn1_w_up, v_ffn1_w_down, v_mix_norm, v_w_in, v_b_forget, v_pool_w, v_pool_scale, v_q_norm, v_k_norm, v_out_norm_pool, v_out_norm_attn, v_w_out, v_ffn2_norm, v_ffn2_w_gate, v_ffn2_w_up, v_ffn2_w_down):
    given = dict(locals())
    w = {n: given[n] for n in WEIGHT_NAMES}
    m = {n: given["m_" + n] for n in WEIGHT_NAMES}
    v = {n: given["v_" + n] for n in WEIGHT_NAMES}
    n_batch, seq, d = x.shape
    width = pool_scale.shape[0]
    in_rows = w_in.shape[1]
    in_cols = N_CHIPS * in_rows
    in_pad = -(-in_rows // 32) * 32
    in_cols_pad = in_cols - N_HEADS + LANES

    work = lambda a, n: a.T if n in TRANSPOSED_NAMES else a
    exchanged = lambda a, n: jnp.pad(a, ((0, in_pad - in_rows), (0, 0))) if n == "w_in" else a

    mesh_x, mesh_y, mesh_c = _mesh_pos()
    ids = jnp.stack([2 * mesh_x + mesh_y, mesh_c]).astype(jnp.int32)

    row = lambda a: a.reshape(1, -1)
    g1, gm, g2, ps, onp, ona = (row(a) for a in (ffn1_norm, mix_norm, ffn2_norm, pool_scale, out_norm_pool, out_norm_attn))
    qn, kn = row(jnp.tile(q_norm, N_HEADS)), row(jnp.tile(k_norm, N_HEADS))
    bf = row(jnp.pad(b_forget, (0, LANES - N_HEADS)))
    pwb = pool_w.astype(BF16)
    xf, tgt = x.reshape(n_batch * seq, d), loss_target.reshape(n_batch * seq, d)

    placed = {n: _place_cast(exchanged(work(w[n], n), n), ids, n) for n in BIG_NAMES}
    wg1, wu1, wd1 = _run_plan(_plan_gather([placed[n] for n in FFN1_NAMES]), "gather_ffn1")
    (x1, h1, a1, b1, s1), rest = _ffn_fwd(xf, g1, wg1, wu1, wd1, plan=_plan_gather([placed[n] for n in REST_NAMES]))
    w_in_all, w_out_all, wg2, wu2, wd2 = rest
    w_in_t = jnp.pad(w_in_all[:, :in_rows].reshape(in_cols, d), ((0, in_cols_pad - in_cols), (0, 0)))
    w_out_full = w_out_all.reshape(N_CHIPS * w_out.shape[0], d)
    woa, wob = w_out_full[:width], w_out_full[width:]

    hm, pv, q, k, qh, kh, vb, f = _mix_proj(x1, gm, w_in_t, qn, kn, width, width)
    qa, ka = _forget_prefix(f, bf, qh, kh, n_batch, seq)
    yp = _pool_fwd(pv, pwb, ps, onp, n_batch, seq)
    o, lse = _attn_fwd(qa, ka, vb, n_batch, seq)
    x2, ya = _mix_out(x1, yp, o, ona, woa, wob)
    (dy, h2, a2, b2, s2, lpart), _ = _ffn_fwd(x2, g2, wg2, wu2, wd2, target=tgt)

    def to_chips(gs, arrived, tags):
        return [_add_sibling(g, r, ids, t) for g, r, t in zip(gs, arrived, tags)]

    def own_rows(gs, from_sibling, from_chips, tags):
        return [_add_chips(g, ra, rb, ids, t) for g, ra, rb, t in zip(gs, from_sibling, from_chips, tags)]

    (dx2, da2, db2, dg2), _ = _ffn_bwd_x(dy, x2, g2, a2, b2, wg2, wu2, wd2, "ffn2_bwd_x")
    dw2, _ = _ffn_bwd_w(h2, s2, da2, db2, dy, "ffn2_bwd_w")
    (dyp, do, delta, dwoa, dwob, dona), sib2 = _mix_out_bwd(dx2, o, yp, ya, ona, woa, wob, plan=_plan_sibling_halves(dw2))
    dpv, dpw, dps, donp = _pool_bwd(pv, dyp, pwb, ps, onp, n_batch, seq)
    (dqh, dfq), chips2 = _attn_bwd_q(qa, ka, vb, do, lse, delta, n_batch, seq,
                                     plan=_plan_chip_exchange(to_chips(dw2, sib2, FFN2_NAMES)))
    (dkh, dv, dfk), red2 = _attn_bwd_kv(qa, ka, vb, do, lse, delta, n_batch, seq,
                                        plan=_plan_sibling_share(own_rows(dw2, sib2, chips2, FFN2_NAMES)))
    df, dbf = _forget_bwd(dfq, dfk, f, bf, n_batch, seq)
    dx1, dw_in_t, dgm, dqn, dkn = _mix_in_bwd(dx2, x1, gm, hm, dpv, dqh, q, dkh, k, dv, df, qn, kn, w_in_t)
    d_w_in = jnp.pad(dw_in_t[:in_cols].reshape(N_CHIPS, in_rows, d), ((0, 0), (0, in_pad - in_rows), (0, 0)))
    d_w_out = jnp.concatenate([dwoa, dwob], axis=0).reshape(N_CHIPS, w_out.shape[0], d)
    dwm = [d_w_in, d_w_out]
    (gx, da1, db1, dg1), sibm = _ffn_bwd_x(dx1, xf, g1, a1, b1, wg1, wu1, wd1, "ffn1_bwd_x", plan=_plan_sibling_halves(dwm))
    dw1, chipsm = _ffn_bwd_w(h1, s1, da1, db1, dx1, "ffn1_bwd_w", plan=_plan_chip_exchange(to_chips(dwm, sibm, MIX_NAMES)))
    sib1 = _run_plan(_plan_sibling_halves(dw1), "sibling_halves")
    chips1 = _run_plan(_plan_chip_exchange(to_chips(dw1, sib1, FFN1_NAMES)), "chip_exchange")
    last = _run_plan(_plan_sibling_share(own_rows(dw1, sib1, chips1, FFN1_NAMES) + own_rows(dwm, sibm, chipsm, MIX_NAMES)),
                     "sibling_share")
    reduced = dict(zip(FFN1_NAMES + MIX_NAMES + FFN2_NAMES, list(last) + list(red2)))
    reduced["w_in"] = reduced["w_in"][:in_rows]

    part = dict(ffn1_norm=dg1, mix_norm=dgm, ffn2_norm=dg2, b_forget=dbf, pool_scale=dps, out_norm_pool=donp,
                out_norm_attn=dona, qn=dqn, kn=dkn, pool_w=dpw.reshape(n_batch, -1, pool_w.shape[-1]), loss=lpart)
    g_vec, g_pw = _small_allreduce(part, d, width)
    loss = g_vec[5, 0]
    grads, delta, new_m, new_v = {}, {}, {}, {}
    for n in BIG_NAMES:
        stepped = _adamw(work(w[n], n), reduced[n], work(m[n], n), work(v[n], n), n)
        grads[n], delta[n], new_m[n], new_v[n] = (work(a, n) for a in (reduced[n], *stepped))
    flat_pw = lambda a: a.reshape(-1, a.shape[-1])
    d_pw, m_pw, v_pw = _adamw(flat_pw(pool_w), g_pw, flat_pw(m_pool_w), flat_pw(v_pool_w), "pool_w")
    d_vec, m_vec, v_vec = _adamw(_pack_vec(w, d, width), g_vec, _pack_vec(m, d, width), _pack_vec(v, d, width), "vectors")
    grads.update(_unpack_vec(g_vec, width), pool_w=g_pw.reshape(pool_w.shape))
    delta.update(_unpack_vec(d_vec, width), pool_w=d_pw.reshape(pool_w.shape))
    new_m.update(_unpack_vec(m_vec, width), pool_w=m_pw.reshape(pool_w.shape))
    new_v.update(_unpack_vec(v_vec, width), pool_w=v_pw.reshape(pool_w.shape))
    return (loss, gx.reshape(x.shape), *[grads[n] for n in WEIGHT_NAMES], *[delta[n] for n in WEIGHT_NAMES],
            *[new_m[n] for n in WEIGHT_NAMES], *[new_v[n] for n in WEIGHT_NAMES])
```

```python
import functools

import jax
import jax.numpy as jnp
from jax import lax
from jax.experimental import pallas as pl
from jax.experimental.pallas import tpu as pltpu

F32 = jnp.float32
BF16 = jnp.bfloat16
EPS = 1e-6
NEG = -1e30
ADAM_LR = 0.001
ADAM_B1 = 0.9
ADAM_B2 = 0.999
ADAM_EPS = 1e-08
ADAM_WD = 0.01
ADAM_STEP = 10
POOL_WINDOWS = (2, 4, 8, 16)
HEAD_DIM = 64
N_HEADS = 8
LANES = 128
N_CHIPS = 4
ATT_BLOCK = 512
ATT_SUB = 128
VMEM_LIMIT = 56 * 1024 * 1024
MESH_AXES = ("x", "y", "c")
ANY = pl.BlockSpec(memory_space=pl.ANY)
VM = pl.BlockSpec(memory_space=pltpu.VMEM)


def _params(**kw):
    return pltpu.CompilerParams(vmem_limit_bytes=VMEM_LIMIT, **kw)


def _dot(a, b):
    return jnp.dot(a, b, preferred_element_type=F32)


def _dot_nt(a, b):
    return lax.dot_general(a, b, (((1,), (1,)), ((), ())), preferred_element_type=F32)


def _dot_tn(a, b):
    return lax.dot_general(a, b, (((0,), (0,)), ((), ())), preferred_element_type=F32)


def _sigmoid(z):
    return 1.0 / (1.0 + jnp.exp(-z))


def _rms(xf):
    return lax.rsqrt(jnp.mean(xf * xf, axis=-1, keepdims=True) + EPS)


def _rms_bwd(xf, r, gain, dh):
    xh = xf * r
    dyg = dh * gain
    return r * (dyg - xh * jnp.mean(dyg * xh, axis=-1, keepdims=True)), dh * xh


def _total(v):
    return jnp.sum(jnp.sum(v, axis=1, keepdims=True), axis=0, keepdims=True)


def _ffn_fwd(x, gain, wg, wu, wd, target=None, plan=None):
    t, d = x.shape
    nch, fc, _ = wg.shape
    tm = min(512, t)
    nt = t // tm
    with_loss = target is not None

    def body(*refs):
        if with_loss:
            x_ref, g_ref, wg_ref, wu_ref, wd_ref, t_ref, o_ref, h_ref, a_ref, b_ref, s_ref, l_ref, acc_ref = refs
        else:
            x_ref, g_ref, wg_ref, wu_ref, wd_ref, o_ref, h_ref, a_ref, b_ref, s_ref, acc_ref = refs
        k = pl.program_id(1)

        @pl.when(k == 0)
        def _():
            xf = x_ref[...]
            h_ref[...] = ((xf * _rms(xf)) * g_ref[...]).astype(BF16)
            acc_ref[...] = jnp.zeros_like(acc_ref)

        h = h_ref[...]
        a = _dot_nt(h, wg_ref[...])
        b = _dot_nt(h, wu_ref[...])
        sb = ((a * (0.5 * jnp.tanh(0.5 * a) + 0.5)) * b).astype(BF16)
        a_ref[...] = a.astype(BF16)
        b_ref[...] = b.astype(BF16)
        s_ref[...] = sb
        acc_ref[...] += _dot(sb, wd_ref[...])

        @pl.when(k == nch - 1)
        def _():
            y = x_ref[...] + 0.5 * acc_ref[...]
            if with_loss:
                e = y - t_ref[...]
                o_ref[...] = e * (1.0 / d)
                l_ref[...] = jnp.broadcast_to(_total(e * e) * (0.5 / d), l_ref.shape)
            else:
                o_ref[...] = y

    row = pl.BlockSpec((tm, d), lambda i, k: (i, 0))
    chunk = pl.BlockSpec((None, fc, d), lambda i, k: (k, 0, 0))
    act = pl.BlockSpec((None, tm, fc), lambda i, k: (k, i, 0))
    in_specs = [row, pl.BlockSpec((1, d), lambda i, k: (0, 0)), chunk, chunk, chunk]
    out_shape = [jax.ShapeDtypeStruct((t, d), F32), jax.ShapeDtypeStruct((t, d), BF16)]
    out_shape += [jax.ShapeDtypeStruct((nch, t, fc), BF16)] * 3
    out_specs = [row, row, act, act, act]
    args = [x, gain, wg, wu, wd]
    if with_loss:
        in_specs.append(row)
        args.append(target)
        out_shape.append(jax.ShapeDtypeStruct((nt, 8, LANES), F32))
        out_specs.append(pl.BlockSpec((None, 8, LANES), lambda i, k: (i, 0, 0)))
    return _pallas(body, name="ffn_fwd_loss" if with_loss else "ffn_fwd", args=args, in_specs=in_specs,
                   out_shape=out_shape, out_specs=out_specs, grid=(nt, nch),
                   scratch_shapes=[pltpu.VMEM((tm, d), F32)], plan=plan)


def _ffn_bwd_x(dy, x, gain, a, b, wg, wu, wd, name, plan=None):
    t, d = x.shape
    nch, fc, _ = wg.shape
    tm = min(512, t)
    nt = t // tm

    def body(dy_ref, x_ref, g_ref, a_ref, b_ref, wg_ref, wu_ref, wd_ref, dx_ref, da_ref, db_ref, dg_ref, acc_ref):
        k = pl.program_id(1)

        @pl.when(k == 0)
        def _():
            acc_ref[...] = jnp.zeros_like(acc_ref)

        ds = _dot_nt(dy_ref[...].astype(BF16), wd_ref[...])
        av = a_ref[...].astype(F32)
        bv = b_ref[...].astype(F32)
        th = jnp.tanh(0.5 * av)
        half_sig = 0.25 * th + 0.25
        dab = ((ds * bv) * (half_sig * (1.0 + av * (0.5 - 0.5 * th)))).astype(BF16)
        dbb = (ds * (av * half_sig)).astype(BF16)
        da_ref[...] = dab
        db_ref[...] = dbb
        acc_ref[...] += _dot(dab, wg_ref[...]) + _dot(dbb, wu_ref[...])

        @pl.when(k == nch - 1)
        def _():
            xf = x_ref[...]
            dxn, dgr = _rms_bwd(xf, _rms(xf), g_ref[...], acc_ref[...])
            dx_ref[...] = dy_ref[...] + dxn
            dg_ref[...] = jnp.sum(dgr, axis=0, keepdims=True)

    row = pl.BlockSpec((tm, d), lambda i, k: (i, 0))
    chunk = pl.BlockSpec((None, fc, d), lambda i, k: (k, 0, 0))
    act = pl.BlockSpec((None, tm, fc), lambda i, k: (k, i, 0))
    return _pallas(
        body, name=name, args=[dy, x, gain, a, b, wg, wu, wd],
        out_shape=[jax.ShapeDtypeStruct((t, d), F32), jax.ShapeDtypeStruct((nch, t, fc), BF16),
                   jax.ShapeDtypeStruct((nch, t, fc), BF16), jax.ShapeDtypeStruct((nt, 1, d), F32)],
        grid=(nt, nch),
        in_specs=[row, row, pl.BlockSpec((1, d), lambda i, k: (0, 0)), act, act, chunk, chunk, chunk],
        out_specs=[row, act, act, pl.BlockSpec((None, 1, d), lambda i, k: (i, 0, 0))],
        scratch_shapes=[pltpu.VMEM((tm, d), F32)], plan=plan)


def _ffn_bwd_w(h, s, da, db, dy, name, plan=None):
    t, d = h.shape
    nch, _, fc = s.shape
    tm = min(1024, t)
    nt = t // tm

    def body(h_ref, s_ref, da_ref, db_ref, dy_ref, dwg_ref, dwu_ref, dwd_ref):
        @pl.when(pl.program_id(1) == 0)
        def _():
            dwg_ref[...] = jnp.zeros_like(dwg_ref)
            dwu_ref[...] = jnp.zeros_like(dwu_ref)
            dwd_ref[...] = jnp.zeros_like(dwd_ref)

        hv = h_ref[...]
        dwg_ref[...] += _dot_tn(da_ref[...], hv)
        dwu_ref[...] += _dot_tn(db_ref[...], hv)
        dwd_ref[...] += _dot_tn(s_ref[...], (0.5 * dy_ref[...]).astype(BF16))

    row = pl.BlockSpec((tm, d), lambda k, i: (i, 0))
    act = pl.BlockSpec((None, tm, fc), lambda k, i: (k, i, 0))
    chunk = pl.BlockSpec((None, fc, d), lambda k, i: (k, 0, 0))
    return _pallas(body, name=name, args=[h, s, da, db, dy], out_shape=[jax.ShapeDtypeStruct((nch, fc, d), F32)] * 3,
                   grid=(nch, nt), in_specs=[row, act, act, act, row], out_specs=[chunk, chunk, chunk], plan=plan)


def _head_masks():
    lane = lax.broadcasted_iota(jnp.int32, (1, LANES), 1)
    return lane < HEAD_DIM


def _head_rms(x, lo):
    x2 = x * x
    s0 = jnp.sum(jnp.where(lo, x2, 0.0), axis=1, keepdims=True)
    s1 = jnp.sum(jnp.where(lo, 0.0, x2), axis=1, keepdims=True)
    return jnp.where(lo, lax.rsqrt(s0 * (1.0 / HEAD_DIM) + EPS), lax.rsqrt(s1 * (1.0 / HEAD_DIM) + EPS))


def _head_mean(v, lo):
    s0 = jnp.sum(jnp.where(lo, v, 0.0), axis=1, keepdims=True)
    s1 = jnp.sum(jnp.where(lo, 0.0, v), axis=1, keepdims=True)
    return jnp.where(lo, s0, s1) * (1.0 / HEAD_DIM)


def _mix_proj(x1, gain, wt, qn, kn, pool_width, attn_width):
    t, d = x1.shape
    tm = min(512, t)
    nt = t // tm
    scale = HEAD_DIM ** -0.5
    c_q, c_k, c_v = pool_width, pool_width + attn_width, pool_width + 2 * attn_width
    c_f = c_v + attn_width

    def body(x_ref, g_ref, wt_ref, qn_ref, kn_ref, hm_ref, pv_ref, q_ref, k_ref, qh_ref, kh_ref, vb_ref, f_ref):
        xf = x_ref[...]
        hm = ((xf * _rms(xf)) * g_ref[...]).astype(BF16)
        hm_ref[...] = hm
        f_ref[...] = _dot_nt(hm, wt_ref[c_f:c_f + LANES, :])
        pv_ref[...] = _dot_nt(hm, wt_ref[0:pool_width, :])
        vb_ref[...] = _dot_nt(hm, wt_ref[c_v:c_v + attn_width, :]).astype(BF16)
        lo = _head_masks()
        for c0, raw_ref, hat_ref, n_ref, mul in ((c_q, q_ref, qh_ref, qn_ref, scale), (c_k, k_ref, kh_ref, kn_ref, 1.0)):
            raw = _dot_nt(hm, wt_ref[c0:c0 + attn_width, :])
            raw_ref[...] = raw
            for blk in range(attn_width // LANES):
                sl = slice(blk * LANES, (blk + 1) * LANES)
                xb = raw[:, sl]
                hat_ref[:, sl] = (((xb * _head_rms(xb, lo)) * n_ref[:, sl]) * mul).astype(BF16)

    row = pl.BlockSpec((tm, d), lambda i: (i, 0))
    half = pl.BlockSpec((tm, attn_width), lambda i: (i, 0))
    const = lambda shape: pl.BlockSpec(shape, lambda i: (0, 0))
    return pl.pallas_call(
        body,
        out_shape=[jax.ShapeDtypeStruct((t, d), BF16), jax.ShapeDtypeStruct((t, pool_width), F32),
                   jax.ShapeDtypeStruct((t, attn_width), F32), jax.ShapeDtypeStruct((t, attn_width), F32),
                   jax.ShapeDtypeStruct((t, attn_width), BF16), jax.ShapeDtypeStruct((t, attn_width), BF16),
                   jax.ShapeDtypeStruct((t, attn_width), BF16), jax.ShapeDtypeStruct((t, LANES), F32)],
        grid=(nt,),
        in_specs=[row, const((1, d)), const(wt.shape), const((1, attn_width)), const((1, attn_width))],
        out_specs=[row, pl.BlockSpec((tm, pool_width), lambda i: (i, 0)), half, half, half, half, half,
                   pl.BlockSpec((tm, LANES), lambda i: (i, 0))],
        compiler_params=_params(), name="mix_proj",
    )(x1, gain, wt, qn, kn)


def _shift_down(v, dist, row):
    return jnp.where(row >= dist, pltpu.roll(v, dist, 0), 0.0)


def _shift_up(v, dist, row, n):
    return jnp.where(row + dist < n, pltpu.roll(v, n - dist, 0), 0.0)


def _aug_lane(e):
    return HEAD_DIM if e == 0 else 0


def _forget_prefix(f, bias, qh, kh, n_batch, seq):
    def body(f_ref, b_ref, q_ref, k_ref, qa_ref, ka_ref):
        z = f_ref[...] + b_ref[...]
        acc = jnp.minimum(z, 0.0) - jnp.log(1.0 + jnp.exp(-jnp.abs(z)))
        row = lax.broadcasted_iota(jnp.int32, (seq, 1), 0)
        dist = 1
        while dist < seq:
            acc = acc + _shift_down(acc, dist, row)
            dist *= 2
        lane = lax.broadcasted_iota(jnp.int32, (1, LANES), 1)
        for h in range(N_HEADS):
            pair, e = divmod(h, 2)
            a0 = _aug_lane(e)
            own = (lane < HEAD_DIM) if e == 0 else (lane >= HEAD_DIM)
            fh = _pick_lane(acc, h)
            hi = fh.astype(BF16).astype(F32)
            rest = fh - hi
            mid = rest.astype(BF16).astype(F32)
            low = rest - mid
            q_ones = (lane >= a0 + 3) & (lane < a0 + 6)
            k_ones = (lane >= a0) & (lane < a0 + 3)
            q_aug = jnp.where(lane == a0, hi, jnp.where(lane == a0 + 1, mid, jnp.where(lane == a0 + 2, low,
                              jnp.where(q_ones, 1.0, 0.0))))
            k_aug = jnp.where(k_ones, 1.0, jnp.where(lane == a0 + 3, -hi, jnp.where(lane == a0 + 4, -mid,
                              jnp.where(lane == a0 + 5, -low, 0.0))))
            src = slice(pair * LANES, (pair + 1) * LANES)
            dst = slice(h * LANES, (h + 1) * LANES)
            qa_ref[:, dst] = jnp.where(own, q_ref[:, src].astype(F32), q_aug).astype(BF16)
            ka_ref[:, dst] = jnp.where(own, k_ref[:, src].astype(F32), k_aug).astype(BF16)

    width = qh.shape[1]
    tok = pl.BlockSpec((seq, width), lambda b: (b, 0))
    aug = pl.BlockSpec((seq, N_HEADS * LANES), lambda b: (b, 0))
    return pl.pallas_call(
        body, out_shape=[jax.ShapeDtypeStruct((n_batch * seq, N_HEADS * LANES), BF16)] * 2, grid=(n_batch,),
        in_specs=[pl.BlockSpec((seq, LANES), lambda b: (b, 0)), pl.BlockSpec((1, LANES), lambda b: (0, 0)), tok, tok],
        out_specs=[aug, aug], compiler_params=_params(), name="forget_prefix",
    )(f, bias, qh, kh)


def _pool_groups(pv_ref, pw_ref, ps_ref, seq):
    row = lax.broadcasted_iota(jnp.int32, (seq, 1), 0)
    pos = (row + 1).astype(F32)
    out = []
    for g, win in enumerate(POOL_WINDOWS):
        sl = slice(g * LANES, (g + 1) * LANES)
        xg = pv_ref[:, sl]
        acc = xg
        dist = 1
        while dist < win:
            acc = acc + _shift_down(acc, dist, row)
            dist *= 2
        pooled = (acc / jnp.minimum(pos, float(win)) - xg).astype(BF16)
        mixed = _dot(pooled, pw_ref[g])
        out.append((pooled, mixed, mixed * ps_ref[:, sl]))
    return out


def _pool_fwd(pv, pw, ps, onp, n_batch, seq):
    width = pv.shape[1]

    def body(pv_ref, pw_ref, ps_ref, on_ref, y_ref):
        groups = _pool_groups(pv_ref, pw_ref, ps_ref, seq)
        ssq = sum(jnp.sum(ms * ms, axis=1, keepdims=True) for _, _, ms in groups)
        r = lax.rsqrt(ssq * (1.0 / width) + EPS)
        for g, (_, _, ms) in enumerate(groups):
            sl = slice(g * LANES, (g + 1) * LANES)
            y_ref[:, sl] = ((ms * r) * on_ref[:, sl]).astype(BF16)

    return pl.pallas_call(
        body, out_shape=jax.ShapeDtypeStruct((n_batch * seq, width), BF16), grid=(n_batch,),
        in_specs=[pl.BlockSpec((seq, width), lambda b: (b, 0)), pl.BlockSpec(pw.shape, lambda b: (0, 0, 0)),
                  pl.BlockSpec((1, width), lambda b: (0, 0)), pl.BlockSpec((1, width), lambda b: (0, 0))],
        out_specs=pl.BlockSpec((seq, width), lambda b: (b, 0)),
        compiler_params=_params(), name="pool_fwd",
    )(pv, pw, ps, onp)


def _pool_bwd(pv, dyp, pw, ps, onp, n_batch, seq):
    width = pv.shape[1]

    def body(pv_ref, dy_ref, pw_ref, ps_ref, on_ref, dpv_ref, dpw_ref, dps_ref, don_ref):
        groups = _pool_groups(pv_ref, pw_ref, ps_ref, seq)
        ssq = sum(jnp.sum(ms * ms, axis=1, keepdims=True) for _, _, ms in groups)
        r = lax.rsqrt(ssq * (1.0 / width) + EPS)
        mean = sum(jnp.sum((dy_ref[:, g * LANES:(g + 1) * LANES] * on_ref[:, g * LANES:(g + 1) * LANES]) * (ms * r),
                           axis=1, keepdims=True) for g, (_, _, ms) in enumerate(groups)) * (1.0 / width)
        row = lax.broadcasted_iota(jnp.int32, (seq, 1), 0)
        pos = (row + 1).astype(F32)
        for g, (pooled, mixed, ms) in enumerate(groups):
            sl = slice(g * LANES, (g + 1) * LANES)
            dy = dy_ref[:, sl]
            xh = ms * r
            don_ref[:, sl] = jnp.sum(dy * xh, axis=0, keepdims=True)
            dms = r * (dy * on_ref[:, sl] - xh * mean)
            dps_ref[:, sl] = jnp.sum(dms * mixed, axis=0, keepdims=True)
            dmix = (dms * ps_ref[:, sl]).astype(BF16)
            dpw_ref[g] = _dot_tn(pooled, dmix)
            dpool = _dot_nt(dmix, pw_ref[g])
            win = POOL_WINDOWS[g]
            acc = dpool / jnp.minimum(pos, float(win))
            dist = 1
            while dist < win:
                acc = acc + _shift_up(acc, dist, row, seq)
                dist *= 2
            dpv_ref[:, sl] = (acc - dpool).astype(BF16)

    tok = pl.BlockSpec((seq, width), lambda b: (b, 0))
    vec = pl.BlockSpec((1, width), lambda b: (0, 0))
    pvec = pl.BlockSpec((None, 1, width), lambda b: (b, 0, 0))
    return pl.pallas_call(
        body,
        out_shape=[jax.ShapeDtypeStruct((n_batch * seq, width), BF16),
                   jax.ShapeDtypeStruct((n_batch,) + pw.shape, F32),
                   jax.ShapeDtypeStruct((n_batch, 1, width), F32), jax.ShapeDtypeStruct((n_batch, 1, width), F32)],
        grid=(n_batch,),
        in_specs=[tok, tok, pl.BlockSpec(pw.shape, lambda b: (0, 0, 0)), vec, vec],
        out_specs=[tok, pl.BlockSpec((None,) + pw.shape, lambda b: (b, 0, 0, 0)), pvec, pvec],
        compiler_params=_params(), name="pool_bwd",
    )(pv, dyp, pw, ps, onp)


def _pick_lane(tile, idx):
    lane = lax.broadcasted_iota(jnp.int32, (1, LANES), 1)
    return jnp.sum(jnp.where(lane == idx, tile, 0.0), axis=1, keepdims=True)


def _pick_row(tile, idx):
    sub = lax.broadcasted_iota(jnp.int32, (tile.shape[0], 1), 0)
    return jnp.sum(jnp.where(sub == idx, tile, 0.0), axis=0, keepdims=True)


def _put_lane(col, idx):
    lane = lax.broadcasted_iota(jnp.int32, (1, LANES), 1)
    return jnp.where(lane == idx, col, 0.0)


def _head_select(e):
    lo = _head_masks()
    return lo if e == 0 else jnp.logical_not(lo)


def _causal(st, shift):
    row = lax.broadcasted_iota(jnp.int32, st.shape, 0)
    col = lax.broadcasted_iota(jnp.int32, st.shape, 1) + shift
    return jnp.where(col >= row, st, NEG)


def _stat_rows(ref, head, nsub):
    return jnp.concatenate([_pick_row(ref[a], head) for a in range(nsub)], axis=1)


def _accumulate(ref, value, first):
    @pl.when(first)
    def _():
        ref[...] = value

    @pl.when(jnp.logical_not(first))
    def _():
        ref[...] += value


def _attn_fwd(qa, ka, vb, n_batch, seq):
    tq = min(ATT_BLOCK, seq)
    nq, nsub, tk = seq // tq, tq // ATT_SUB, tq
    pairs = vb.shape[1] // LANES

    def body(q_ref, k_ref, v_ref, o_ref, lse_ref, acc_ref):
        i, p = pl.program_id(1), pl.program_id(2)
        row_lo = lax.broadcasted_iota(jnp.int32, (LANES, 1), 0) < HEAD_DIM
        qs = [q_ref[:, e * LANES:(e + 1) * LANES] for e in range(2)]
        acc_ref[...] = jnp.zeros_like(acc_ref)

        def tile(off, stats, diagonal):
            vj = v_ref[pl.ds(off, tk), :]
            new, alphas, pvs = [], [], []
            for e in range(2):
                st = _dot_nt(k_ref[pl.ds(off, tk), e * LANES:(e + 1) * LANES], qs[e])
                if diagonal:
                    st = _causal(st, 0)
                m, l = stats[e]
                m_new = jnp.maximum(m, jnp.max(st, axis=0, keepdims=True))
                alpha = jnp.exp(m - m_new)
                pt = jnp.exp(st - m_new)
                new.append((m_new, alpha * l + jnp.sum(pt, axis=0, keepdims=True)))
                alphas.append(alpha)
                pvs.append(_dot_tn(jnp.where(_head_select(e), vj, jnp.zeros_like(vj)), pt.astype(BF16)))
            acc_ref[...] = acc_ref[...] * jnp.where(row_lo, alphas[0], alphas[1]) + (pvs[0] + pvs[1])
            return tuple(new)

        init = ((jnp.full((1, tq), NEG, F32), jnp.zeros((1, tq), F32)),) * 2
        stats = lax.fori_loop(0, i, lambda j, st: tile(pl.multiple_of(j * tk, tk), st, False), init)
        (m0, l0), (m1, l1) = tile(pl.multiple_of(i * tk, tk), stats, True)
        out_t = acc_ref[...] / jnp.where(row_lo, l0, l1)
        sub = lax.broadcasted_iota(jnp.int32, (8, 1), 0)
        lse0, lse1 = m0 + jnp.log(l0), m1 + jnp.log(l1)
        for a in range(nsub):
            sl = slice(a * ATT_SUB, (a + 1) * ATT_SUB)
            o_ref[sl, :] = out_t[:, sl].T
            rows = jnp.where(sub == 2 * p, lse0[:, sl], 0.0) + jnp.where(sub == 2 * p + 1, lse1[:, sl], 0.0)
            _accumulate(lse_ref.at[a], rows, p == 0)

    return pl.pallas_call(
        body,
        out_shape=[jax.ShapeDtypeStruct((n_batch * seq, pairs * LANES), F32),
                   jax.ShapeDtypeStruct((n_batch * seq // ATT_SUB, 8, ATT_SUB), F32)],
        grid=(n_batch, nq, pairs),
        in_specs=[pl.BlockSpec((tq, 2 * LANES), lambda b, i, p: (b * nq + i, p)),
                  pl.BlockSpec((seq, 2 * LANES), lambda b, i, p: (b, p)),
                  pl.BlockSpec((seq, LANES), lambda b, i, p: (b, p))],
        out_specs=[pl.BlockSpec((tq, LANES), lambda b, i, p: (b * nq + i, p)),
                   pl.BlockSpec((nsub, 8, ATT_SUB), lambda b, i, p: (b * nq + i, 0, 0))],
        scratch_shapes=[pltpu.VMEM((LANES, tq), F32)],
        compiler_params=_params(), name="attn_fwd",
    )(qa, ka, vb)


def _attn_bwd_q(qa, ka, vb, do, lse, delta, n_batch, seq, plan=None):
    tq = min(ATT_BLOCK, seq)
    nq, nsub, tk = seq // tq, tq // ATT_SUB, tq
    pairs = vb.shape[1] // LANES

    def body(q_ref, k_ref, v_ref, do_ref, lse_ref, dl_ref, dq_ref, dfq_ref, acc0_ref, acc1_ref):
        i, p = pl.program_id(1), pl.program_id(2)
        accs = (acc0_ref, acc1_ref)
        qs = [q_ref[:, e * LANES:(e + 1) * LANES] for e in range(2)]
        dov = do_ref[...]
        ls = [_stat_rows(lse_ref, 2 * p + e, nsub) for e in range(2)]
        dl = [_stat_rows(dl_ref, 2 * p + e, nsub) for e in range(2)]
        for acc in accs:
            acc[...] = jnp.zeros_like(acc)

        def tile(off, diagonal):
            vj = v_ref[pl.ds(off, tk), :]
            for e in range(2):
                kj = k_ref[pl.ds(off, tk), e * LANES:(e + 1) * LANES]
                st = _dot_nt(kj, qs[e])
                if diagonal:
                    st = _causal(st, 0)
                pt = jnp.exp(st - ls[e])
                dpt = _dot_nt(jnp.where(_head_select(e), vj, jnp.zeros_like(vj)), dov)
                accs[e][...] += _dot_tn((pt * (dpt - dl[e])).astype(BF16), kj)

        def step(j, carry):
            tile(pl.multiple_of(j * tk, tk), False)
            return carry

        lax.fori_loop(0, i, step, 0)
        tile(pl.multiple_of(i * tk, tk), True)
        dq0, dq1 = acc0_ref[...], acc1_ref[...]
        dq_ref[...] = jnp.where(_head_masks(), dq0, dq1)
        dfq = _put_lane(_pick_lane(dq0, _aug_lane(0)), 2 * p) + _put_lane(_pick_lane(dq1, _aug_lane(1)), 2 * p + 1)
        _accumulate(dfq_ref, dfq, p == 0)

    stat = pl.BlockSpec((nsub, 8, ATT_SUB), lambda b, i, p: (b * nq + i, 0, 0))
    blk = pl.BlockSpec((tq, LANES), lambda b, i, p: (b * nq + i, p))
    return _pallas(
        body, name="attn_bwd_q", args=[qa, ka, vb, do, lse, delta],
        out_shape=[jax.ShapeDtypeStruct((n_batch * seq, pairs * LANES), F32), jax.ShapeDtypeStruct((n_batch * seq, LANES), F32)],
        grid=(n_batch, nq, pairs),
        in_specs=[pl.BlockSpec((tq, 2 * LANES), lambda b, i, p: (b * nq + i, p)),
                  pl.BlockSpec((seq, 2 * LANES), lambda b, i, p: (b, p)),
                  pl.BlockSpec((seq, LANES), lambda b, i, p: (b, p)), blk, stat, stat],
        out_specs=[blk, pl.BlockSpec((tq, LANES), lambda b, i, p: (b * nq + i, 0))],
        scratch_shapes=[pltpu.VMEM((tq, LANES), F32), pltpu.VMEM((tq, LANES), F32)], plan=plan)


def _attn_bwd_kv(qa, ka, vb, do, lse, delta, n_batch, seq, plan=None):
    tkb = min(ATT_BLOCK, seq)
    nk, nsub, tq = seq // tkb, tkb // ATT_SUB, tkb
    n_tiles = seq // ATT_SUB
    pairs = vb.shape[1] // LANES

    def body(q_ref, k_ref, v_ref, do_ref, lse_ref, dl_ref, dk_ref, dv_ref, dfk_ref, dk0_ref, dk1_ref, dva_ref):
        j, p = pl.program_id(1), pl.program_id(2)
        dks = (dk0_ref, dk1_ref)
        ks = [k_ref[:, e * LANES:(e + 1) * LANES] for e in range(2)]
        vj = v_ref[...]
        vs = [jnp.where(_head_select(e), vj, jnp.zeros_like(vj)) for e in range(2)]
        for acc in (dk0_ref, dk1_ref, dva_ref):
            acc[...] = jnp.zeros_like(acc)

        def tile(t, diagonal):
            off = pl.multiple_of(t * tq, tq)
            dov = do_ref[pl.ds(off, tq), :]
            for e in range(2):
                qe = q_ref[pl.ds(off, tq), e * LANES:(e + 1) * LANES]
                st = _dot_nt(ks[e], qe)
                if diagonal:
                    st = _causal(st, 0)
                rows = lambda ref: jnp.concatenate([_pick_row(ref[t * nsub + a], 2 * p + e) for a in range(nsub)], axis=1)
                pt = jnp.exp(st - rows(lse_ref))
                dva_ref[...] += _dot(pt.astype(BF16), jnp.where(_head_select(e), dov, jnp.zeros_like(dov)))
                dst = pt * (_dot_nt(vs[e], dov) - rows(dl_ref))
                dks[e][...] += _dot(dst.astype(BF16), qe)

        def step(t, carry):
            tile(t, False)
            return carry

        lax.fori_loop(j + 1, nk, step, 0)
        tile(j, True)
        dk0, dk1 = dk0_ref[...], dk1_ref[...]
        dk_ref[...] = jnp.where(_head_masks(), dk0, dk1)
        dv_ref[...] = dva_ref[...].astype(BF16)
        dfk = (_put_lane(_pick_lane(dk0, _aug_lane(0) + 3), 2 * p)
               + _put_lane(_pick_lane(dk1, _aug_lane(1) + 3), 2 * p + 1))
        _accumulate(dfk_ref, -dfk, p == 0)

    stat = pl.BlockSpec((n_tiles, 8, ATT_SUB), lambda b, j, p: (b, 0, 0))
    blk = pl.BlockSpec((tkb, LANES), lambda b, j, p: (b * nk + j, p))
    acc = pltpu.VMEM((tkb, LANES), F32)
    return _pallas(
        body, name="attn_bwd_kv", args=[qa, ka, vb, do, lse, delta],
        out_shape=[jax.ShapeDtypeStruct((n_batch * seq, pairs * LANES), F32),
                   jax.ShapeDtypeStruct((n_batch * seq, pairs * LANES), BF16),
                   jax.ShapeDtypeStruct((n_batch * seq, LANES), F32)],
        grid=(n_batch, nk, pairs),
        in_specs=[pl.BlockSpec((seq, 2 * LANES), lambda b, j, p: (b, p)),
                  pl.BlockSpec((tkb, 2 * LANES), lambda b, j, p: (b * nk + j, p)), blk,
                  pl.BlockSpec((seq, LANES), lambda b, j, p: (b, p)), stat, stat],
        out_specs=[blk, blk, pl.BlockSpec((tkb, LANES), lambda b, j, p: (b * nk + j, 0))],
        scratch_shapes=[acc, acc, acc], plan=plan)


def _forget_bwd(dfq, dfk, f, bias, n_batch, seq):
    def body(dfq_ref, dfk_ref, f_ref, b_ref, df_ref, db_ref):
        acc = dfq_ref[...] + dfk_ref[...]
        row = lax.broadcasted_iota(jnp.int32, (seq, 1), 0)
        dist = 1
        while dist < seq:
            acc = acc + _shift_up(acc, dist, row, seq)
            dist *= 2
        df = acc * _sigmoid(-(f_ref[...] + b_ref[...]))
        df_ref[...] = df
        db_ref[...] = jnp.sum(df, axis=0, keepdims=True)

    col = pl.BlockSpec((seq, LANES), lambda b: (b, 0))
    return pl.pallas_call(
        body,
        out_shape=[jax.ShapeDtypeStruct((n_batch * seq, LANES), F32), jax.ShapeDtypeStruct((n_batch, 1, LANES), F32)],
        grid=(n_batch,), in_specs=[col, col, col, pl.BlockSpec((1, LANES), lambda b: (0, 0))],
        out_specs=[col, pl.BlockSpec((None, 1, LANES), lambda b: (b, 0, 0))],
        compiler_params=_params(), name="forget_bwd",
    )(dfq, dfk, f, bias)


def _mix_out(x1, yp, o, ona, woa, wob):
    t, d = x1.shape
    width = o.shape[1]
    tm = min(512, t)

    def body(x_ref, yp_ref, o_ref, on_ref, wa_ref, wb_ref, x2_ref, ya_ref):
        of = o_ref[...]
        ya = ((of * _rms(of)) * on_ref[...]).astype(BF16)
        ya_ref[...] = ya
        x2_ref[...] = x_ref[...] + (_dot(yp_ref[...], wa_ref[...]) + _dot(ya, wb_ref[...]))

    row = pl.BlockSpec((tm, d), lambda i: (i, 0))
    half = pl.BlockSpec((tm, width), lambda i: (i, 0))
    wspec = pl.BlockSpec((width, d), lambda i: (0, 0))
    return pl.pallas_call(
        body, out_shape=[jax.ShapeDtypeStruct((t, d), F32), jax.ShapeDtypeStruct((t, width), BF16)],
        grid=(t // tm,), in_specs=[row, half, half, pl.BlockSpec((1, width), lambda i: (0, 0)), wspec, wspec],
        out_specs=[row, half], compiler_params=_params(), name="mix_out",
    )(x1, yp, o, ona, woa, wob)


def _mix_out_bwd(dx2, o, yp, ya, ona, woa, wob, plan=None):
    t, d = dx2.shape
    width = o.shape[1]
    tm = min(512, t)
    nt = t // tm

    def body(dx_ref, o_ref, yp_ref, ya_ref, on_ref, wa_ref, wb_ref, dyp_ref, do_ref, dl_ref, dwa_ref, dwb_ref, don_ref):
        @pl.when(pl.program_id(0) == 0)
        def _():
            dwa_ref[...] = jnp.zeros_like(dwa_ref)
            dwb_ref[...] = jnp.zeros_like(dwb_ref)

        dxb = dx_ref[...].astype(BF16)
        dwa_ref[...] += _dot_tn(yp_ref[...], dxb)
        dwb_ref[...] += _dot_tn(ya_ref[...], dxb)
        dyp_ref[...] = _dot_nt(dxb, wa_ref[...])
        of = o_ref[...]
        dov, dgr = _rms_bwd(of, _rms(of), on_ref[...], _dot_nt(dxb, wb_ref[...]))
        don_ref[...] = jnp.sum(dgr, axis=0, keepdims=True)
        do_ref[...] = dov.astype(BF16)
        lo = _head_masks()
        prod = dov * of
        delta = jnp.zeros((tm, LANES), F32)
        for blk in range(width // LANES):
            pb = prod[:, blk * LANES:(blk + 1) * LANES]
            delta = delta + _put_lane(jnp.sum(jnp.where(lo, pb, 0.0), axis=1, keepdims=True), 2 * blk)
            delta = delta + _put_lane(jnp.sum(jnp.where(lo, 0.0, pb), axis=1, keepdims=True), 2 * blk + 1)
        for c in range(tm // ATT_SUB):
            dl_ref[c] = delta[c * ATT_SUB:(c + 1) * ATT_SUB, :].T[0:8, :]

    row = pl.BlockSpec((tm, d), lambda i: (i, 0))
    half = pl.BlockSpec((tm, width), lambda i: (i, 0))
    wspec = pl.BlockSpec((width, d), lambda i: (0, 0))
    return _pallas(
        body, name="mix_out_bwd", args=[dx2, o, yp, ya, ona, woa, wob],
        out_shape=[jax.ShapeDtypeStruct((t, width), F32), jax.ShapeDtypeStruct((t, width), BF16),
                   jax.ShapeDtypeStruct((t // ATT_SUB, 8, ATT_SUB), F32), jax.ShapeDtypeStruct((width, d), F32),
                   jax.ShapeDtypeStruct((width, d), F32), jax.ShapeDtypeStruct((nt, 1, width), F32)],
        grid=(nt,),
        in_specs=[row, half, half, half, pl.BlockSpec((1, width), lambda i: (0, 0)), wspec, wspec],
        out_specs=[half, half, pl.BlockSpec((tm // ATT_SUB, 8, ATT_SUB), lambda i: (i, 0, 0)), wspec, wspec,
                   pl.BlockSpec((None, 1, width), lambda i: (i, 0, 0))], plan=plan)


def _mix_in_bwd(dx2, x1, gain, hm, dpv, dqh, q, dkh, k, dv, df, qn, kn, wt):
    t, d = x1.shape
    width = q.shape[1]
    pool_width = dpv.shape[1]
    tm = min(512, t)
    nt = t // tm
    scale = HEAD_DIM ** -0.5
    c_q, c_k, c_v = pool_width, pool_width + width, pool_width + 2 * width
    c_f = c_v + width

    def body(dx2_ref, x_ref, g_ref, hm_ref, dpv_ref, dqh_ref, q_ref, dkh_ref, k_ref, dv_ref, df_ref, qn_ref, kn_ref,
             wt_ref, dx_ref, dwt_ref, dg_ref, dqn_ref, dkn_ref):
        @pl.when(pl.program_id(0) == 0)
        def _():
            dwt_ref[...] = jnp.zeros_like(dwt_ref)

        lo = _head_masks()
        hm = hm_ref[...]
        pieces = [(0, dpv_ref[...])]
        for c0, raw_ref, dh_ref, n_ref, dn_ref, mul in ((c_q, q_ref, dqh_ref, qn_ref, dqn_ref, scale),
                                                       (c_k, k_ref, dkh_ref, kn_ref, dkn_ref, 1.0)):
            cols = []
            for blk in range(width // LANES):
                sl = slice(blk * LANES, (blk + 1) * LANES)
                xb = raw_ref[:, sl]
                gb = dh_ref[:, sl] * mul
                r = _head_rms(xb, lo)
                xh = xb * r
                dyg = gb * n_ref[:, sl]
                cols.append((r * (dyg - xh * _head_mean(dyg * xh, lo))).astype(BF16))
                dn_ref[:, sl] = jnp.sum(gb * xh, axis=0, keepdims=True)
            pieces.append((c0, jnp.concatenate(cols, axis=1)))
        pieces.append((c_v, dv_ref[...]))
        pieces.append((c_f, df_ref[...].astype(BF16)))
        dhm = jnp.zeros((tm, d), F32)
        for c0, piece in pieces:
            dwt_ref[c0:c0 + piece.shape[1], :] += _dot_tn(piece, hm)
            dhm = dhm + _dot(piece, wt_ref[c0:c0 + piece.shape[1], :])
        xf = x_ref[...]
        dxn, dgr = _rms_bwd(xf, _rms(xf), g_ref[...], dhm)
        dx_ref[...] = dx2_ref[...] + dxn
        dg_ref[...] = jnp.sum(dgr, axis=0, keepdims=True)

    row = pl.BlockSpec((tm, d), lambda i: (i, 0))
    half = pl.BlockSpec((tm, width), lambda i: (i, 0))
    const = lambda shape: pl.BlockSpec(shape, lambda i: (0, 0))
    pvec = lambda n: pl.BlockSpec((None, 1, n), lambda i: (i, 0, 0))
    return pl.pallas_call(
        body,
        out_shape=[jax.ShapeDtypeStruct((t, d), F32), jax.ShapeDtypeStruct(wt.shape, F32),
                   jax.ShapeDtypeStruct((nt, 1, d), F32),
                   jax.ShapeDtypeStruct((nt, 1, width), F32), jax.ShapeDtypeStruct((nt, 1, width), F32)],
        grid=(nt,),
        in_specs=[row, row, const((1, d)), row, pl.BlockSpec((tm, pool_width), lambda i: (i, 0)), half, half, half, half,
                  half, pl.BlockSpec((tm, LANES), lambda i: (i, 0)), const((1, width)), const((1, width)),
                  const(wt.shape)],
        out_specs=[row, const(wt.shape), pvec(d), pvec(width), pvec(width)],
        compiler_params=_params(), name="mix_in_bwd",
    )(dx2, x1, gain, hm, dpv, dqh, q, dkh, k, dv, df, qn, kn, wt)


def _mesh_pos():
    return lax.axis_index("x"), lax.axis_index("y"), lax.axis_index("c")


def _other_chips(x, y):
    return [(1 - x, y), (x, 1 - y), (1 - x, 1 - y)]


def _remote(src, dst, send_sem, recv_sem, device):
    return pltpu.make_async_remote_copy(src_ref=src, dst_ref=dst, send_sem=send_sem, recv_sem=recv_sem,
                                        device_id=device, device_id_type=pl.DeviceIdType.MESH)


def _half_rows(n_rows, which):
    half = n_rows // 2
    return pl.ds(pl.multiple_of(which * half, 8), half)


def _row_block(rows, cols, itemsize=4):
    rb = rows
    while rb * cols * itemsize > (1 << 20) and rb % 32 == 0:
        rb //= 2
    return rb


def _place_cast(ws, chip, tag):
    n = len(ws)
    rows, cols = ws[0].shape
    rb = _row_block(rows, cols)

    def body(k_ref, *refs):
        for w_ref, o_ref in zip(refs[:n], refs[n:]):
            o_ref[...] = w_ref[...].astype(BF16)

    return pl.pallas_call(
        body, out_shape=[jax.ShapeDtypeStruct((N_CHIPS, rows, cols), BF16)] * n,
        grid_spec=pltpu.PrefetchScalarGridSpec(
            num_scalar_prefetch=1, grid=(rows // rb,),
            in_specs=[pl.BlockSpec((rb, cols), lambda i, k: (i, 0))] * n,
            out_specs=[pl.BlockSpec((None, rb, cols), lambda i, k: (k[0], i, 0))] * n),
        compiler_params=_params(), name="place_" + tag,
    )(chip, *ws)


class _Plan:
    def __init__(self, ins, outs, alias, sems, start, finish):
        self.ins, self.outs, self.alias, self.sems, self.start, self.finish = ins, outs, alias, sems, start, finish


def _run_plan(plan, name):
    n_in, n_out = len(plan.ins), len(plan.outs)

    def body(*refs):
        parts = refs[:n_in], refs[n_in:n_in + n_out], refs[n_in + n_out:]
        plan.start(*parts)
        plan.finish(*parts)

    return pl.pallas_call(
        body, out_shape=plan.outs, in_specs=[ANY] * n_in, out_specs=[ANY] * n_out, scratch_shapes=plan.sems,
        input_output_aliases=plan.alias, name=name,
    )(*plan.ins)


def _pallas(body, *, name, args, in_specs, out_shape, out_specs, grid, scratch_shapes=(), plan=None):
    n_in, n_out, n_scr = len(args), len(out_shape), len(scratch_shapes)
    if plan is None:
        res = pl.pallas_call(body, out_shape=out_shape, grid=grid, in_specs=in_specs, out_specs=out_specs,
                             scratch_shapes=scratch_shapes, compiler_params=_params(), name=name)(*args)
        return list(res), []
    p_in, p_out = len(plan.ins), len(plan.outs)

    def carrying(*refs):
        ins, p_ins = refs[:n_in], refs[n_in:n_in + p_in]
        o0 = n_in + p_in
        outs, p_outs = refs[o0:o0 + n_out], refs[o0 + n_out:o0 + n_out + p_out]
        s0 = o0 + n_out + p_out
        scr, p_sems = refs[s0:s0 + n_scr], refs[s0 + n_scr:]
        ids = [pl.program_id(a) for a in range(len(grid))]
        first = functools.reduce(jnp.logical_and, [i == 0 for i in ids])
        last = functools.reduce(jnp.logical_and, [i == g - 1 for i, g in zip(ids, grid)])

        @pl.when(first)
        def _():
            plan.start(p_ins, p_outs, p_sems)

        body(*ins, *outs, *scr)

        @pl.when(last)
        def _():
            plan.finish(p_ins, p_outs, p_sems)

    res = pl.pallas_call(
        carrying, out_shape=list(out_shape) + list(plan.outs), grid=grid,
        in_specs=list(in_specs) + [ANY] * p_in, out_specs=list(out_specs) + [ANY] * p_out,
        scratch_shapes=list(scratch_shapes) + list(plan.sems),
        input_output_aliases={n_in + i: n_out + o for i, o in plan.alias.items()},
        compiler_params=_params(), name=name,
    )(*args, *plan.ins)
    return list(res[:n_out]), list(res[n_out:])


def _plan_gather(stacks):
    n = len(stacks)

    def ici_copies(outs, sems):
        x, y, c = _mesh_pos()
        cps = []
        for w in range(n):
            own = outs[w].at[2 * x + y, _half_rows(stacks[w].shape[1], c)]
            cps += [_remote(own, own, sems[0].at[w, j], sems[1].at[w, j], (*chip, c)) for j, chip in enumerate(_other_chips(x, y))]
        return cps

    def start(ins, outs, sems):
        for cp in ici_copies(outs, sems):
            cp.start()

    def finish(ins, outs, sems):
        ici_send, ici_recv, d2d_send, d2d_recv = sems
        x, y, c = _mesh_pos()
        sibling = (x, y, 1 - c)
        slots = [2 * cx + cy for cx, cy in _other_chips(x, y)]
        forwards = []
        for w in range(n):
            rows = _half_rows(stacks[w].shape[1], c)
            for j in range(3):
                landed = outs[w].at[slots[j], rows]
                _remote(landed, landed, ici_send.at[w, j], ici_recv.at[w, j], sibling).wait_recv()
                cp = _remote(landed, landed, d2d_send.at[w, j], d2d_recv.at[w, j], sibling)
                cp.start()
                forwards.append(cp)
        for w in range(n):
            rows = _half_rows(stacks[w].shape[1], 1 - c)
            for j in range(3):
                landed = outs[w].at[slots[j], rows]
                _remote(landed, landed, d2d_send.at[w, j], d2d_recv.at[w, j], sibling).wait_recv()
        for cp in ici_copies(outs, sems) + forwards:
            cp.wait_send()

    return _Plan(stacks, [jax.ShapeDtypeStruct(s.shape, s.dtype) for s in stacks], {w: w for w in range(n)},
                 [pltpu.SemaphoreType.DMA((n, 3))] * 4, start, finish)


def _plan_sibling_halves(gs):
    n = len(gs)

    def copies(ins, outs, sems):
        x, y, c = _mesh_pos()
        return [_remote(ins[w].at[:, _half_rows(gs[w].shape[1], 1 - c), :], outs[w], sems[0].at[w], sems[1].at[w],
                        (x, y, 1 - c)) for w in range(n)]

    def start(ins, outs, sems):
        for cp in copies(ins, outs, sems):
            cp.start()

    def finish(ins, outs, sems):
        for cp in copies(ins, outs, sems):
            cp.wait()

    return _Plan(gs, [jax.ShapeDtypeStruct((g.shape[0], g.shape[1] // 2, g.shape[2]), g.dtype) for g in gs], {},
                 [pltpu.SemaphoreType.DMA((n,))] * 2, start, finish)


def _plan_chip_exchange(ps):
    n = len(ps)

    def copies(ins, outs, sems):
        x, y, c = _mesh_pos()
        return [_remote(ins[w].at[2 * cx + cy], outs[w].at[j], sems[0].at[w, j], sems[1].at[w, j], (cx, cy, c))
                for w in range(n) for j, (cx, cy) in enumerate(_other_chips(x, y))]

    def start(ins, outs, sems):
        for cp in copies(ins, outs, sems):
            cp.start()

    def finish(ins, outs, sems):
        for cp in copies(ins, outs, sems):
            cp.wait()

    return _Plan(ps, [jax.ShapeDtypeStruct((3,) + p.shape[1:], p.dtype) for p in ps], {},
                 [pltpu.SemaphoreType.DMA((n, 3))] * 2, start, finish)


def _plan_sibling_share(gs):
    n = len(gs)

    def copies(outs, sems, which):
        x, y, c = _mesh_pos()
        cps = []
        for w in range(n):
            rows = outs[w].at[_half_rows(gs[w].shape[0], c if which == "mine" else 1 - c)]
            cps.append(_remote(rows, rows, sems[0].at[w], sems[1].at[w], (x, y, 1 - c)))
        return cps

    def start(ins, outs, sems):
        for cp in copies(outs, sems, "mine"):
            cp.start()

    def finish(ins, outs, sems):
        for cp in copies(outs, sems, "mine"):
            cp.wait_send()
        for cp in copies(outs, sems, "theirs"):
            cp.wait_recv()

    return _Plan(gs, [jax.ShapeDtypeStruct(g.shape, g.dtype) for g in gs], {w: w for w in range(n)},
                 [pltpu.SemaphoreType.DMA((n,))] * 2, start, finish)


def _same_shape_groups(arrays):
    groups = {}
    for i, a in enumerate(arrays):
        groups.setdefault(a.shape, []).append(i)
    return list(groups.values())


def _add_sibling(gs, r1s, ids, tag):
    n = len(gs)
    nch, rh, cols = r1s[0].shape

    def body(ids_ref, *refs):
        for g_ref, r_ref, o_ref in zip(refs[:n], refs[n:2 * n], refs[2 * n:]):
            o_ref[...] = (g_ref[...] + r_ref[...]).astype(BF16)

    blk = lambda fn: pl.BlockSpec((None, rh, cols), fn)
    return pl.pallas_call(
        body, out_shape=[jax.ShapeDtypeStruct(r1s[0].shape, BF16)] * n,
        grid_spec=pltpu.PrefetchScalarGridSpec(
            num_scalar_prefetch=1, grid=(nch,),
            in_specs=[blk(lambda k, ids: (k, ids[1], 0))] * n + [blk(lambda k, ids: (k, 0, 0))] * n,
            out_specs=[blk(lambda k, ids: (k, 0, 0))] * n),
        compiler_params=_params(), name="add_sibling_" + tag,
    )(ids, *gs, *r1s)


def _add_chips(gs, r1s, r2s, ids, tag):
    n = len(gs)
    _, rh, cols = r1s[0].shape
    nb = 2 if rh % 32 == 0 else 1
    rb = rh // nb

    def body(ids_ref, *refs):
        for g_ref, r1_ref, r2_ref, o_ref in zip(refs[:n], refs[n:2 * n], refs[2 * n:3 * n], refs[3 * n:]):
            own = g_ref[...] + r1_ref[...]
            o_ref[...] = ((own + r2_ref[0].astype(F32)) + r2_ref[1].astype(F32)) + r2_ref[2].astype(F32)

    return pl.pallas_call(
        body, out_shape=[jax.ShapeDtypeStruct((2 * rh, cols), F32)] * n,
        grid_spec=pltpu.PrefetchScalarGridSpec(
            num_scalar_prefetch=1, grid=(nb,),
            in_specs=[pl.BlockSpec((None, rb, cols), lambda i, ids: (ids[0], ids[1] * nb + i, 0))] * n
            + [pl.BlockSpec((None, rb, cols), lambda i, ids: (ids[0], i, 0))] * n
            + [pl.BlockSpec((3, rb, cols), lambda i, ids: (0, i, 0))] * n,
            out_specs=[pl.BlockSpec((rb, cols), lambda i, ids: (ids[1] * nb + i, 0))] * n),
        compiler_params=_params(), name="add_chips_" + tag,
    )(ids, *gs, *r1s, *r2s)


VEC_ROWS = 8


def _small_allreduce(part, d, width):
    names = ("ffn1_norm", "mix_norm", "ffn2_norm", "pool_scale", "out_norm_pool", "out_norm_attn", "qn", "kn", "b_forget",
             "pool_w", "loss")
    args = [part[k] for k in names]
    pw_shape = part["pool_w"].shape[1:]
    n_dev = 8

    def body(g1_ref, gm_ref, g2_ref, ps_ref, onp_ref, ona_ref, qn_ref, kn_ref, bf_ref, pw_ref, loss_ref,
             vec_ref, pwo_ref, vbuf, pbuf, send, recv):
        x, y, c = _mesh_pos()
        me = 4 * x + 2 * y + c
        lo = _head_masks()

        def fold_heads(ref):
            v = jnp.sum(ref[...], axis=0)
            acc = jnp.zeros((VEC_ROWS, LANES), F32)
            for blk in range(width // LANES):
                vb = jnp.broadcast_to(v[:, blk * LANES:(blk + 1) * LANES], (VEC_ROWS, LANES))
                acc = acc + vb + pltpu.roll(vb, HEAD_DIM, 1)
            return jnp.where(lo, acc, 0.0)[0:1, :]

        vbuf[0] = jnp.zeros((VEC_ROWS, d), F32)
        vbuf[0, 0:1, :] = jnp.sum(g1_ref[...], axis=0)
        vbuf[0, 1:2, :] = jnp.sum(gm_ref[...], axis=0)
        vbuf[0, 2:3, :] = jnp.sum(g2_ref[...], axis=0)
        vbuf[0, 5:6, 0:LANES] = jnp.sum(loss_ref[...], axis=0)[0:1, :]
        vbuf[0, 3:4, 0:width] = jnp.sum(ps_ref[...], axis=0)
        vbuf[0, 3:4, width:2 * width] = jnp.sum(onp_ref[...], axis=0)
        vbuf[0, 4:5, 0:width] = jnp.sum(ona_ref[...], axis=0)
        vbuf[0, 4:5, width:width + LANES] = fold_heads(qn_ref)
        vbuf[0, 4:5, width + LANES:width + 2 * LANES] = fold_heads(kn_ref)
        vbuf[0, 4:5, width + 2 * LANES:width + 3 * LANES] = jnp.sum(bf_ref[...], axis=0)
        pbuf[0] = jnp.sum(pw_ref[...], axis=0)

        cps = []
        for r in range(1, n_dev):
            peer = (x if not r & 4 else 1 - x, y if not r & 2 else 1 - y, c if not r & 1 else 1 - c)
            for buf, k in ((vbuf, 0), (pbuf, 1)):
                cp = _remote(buf.at[0], buf.at[r], send.at[k, r - 1], recv.at[k, r - 1], peer)
                cp.start()
                cps.append(cp)
        for cp in cps:
            cp.wait()
        vec = vbuf[me]
        pw = pbuf[me]
        for dev in range(1, n_dev):
            vec = vec + vbuf[jnp.bitwise_xor(me, dev)]
            pw = pw + pbuf[jnp.bitwise_xor(me, dev)]
        vec_ref[...] = vec
        pwo_ref[...] = pw

    return pl.pallas_call(
        body, out_shape=[jax.ShapeDtypeStruct((VEC_ROWS, d), F32), jax.ShapeDtypeStruct(pw_shape, F32)],
        in_specs=[VM] * len(args), out_specs=[VM, VM],
        scratch_shapes=[pltpu.VMEM((n_dev, VEC_ROWS, d), F32), pltpu.VMEM((n_dev,) + pw_shape, F32),
                        pltpu.SemaphoreType.DMA((2, n_dev - 1)), pltpu.SemaphoreType.DMA((2, n_dev - 1))],
        compiler_params=_params(), name="small_allreduce",
    )(*args)


def _adamw(ws, gs, ms, vs, tag):
    n = len(ws)
    rows, cols = ws[0].shape
    rb = rows
    while rb * cols * 4 * n > (1 << 20) and rb % 16 == 0:
        rb //= 2

    def body(*refs):
        for j in range(n):
            w_ref, g_ref, m_ref, v_ref = (refs[k * n + j] for k in range(4))
            d_ref, mo_ref, vo_ref = (refs[(4 + k) * n + j] for k in range(3))
            gv = g_ref[...]
            m2 = ADAM_B1 * m_ref[...] + (1.0 - ADAM_B1) * gv
            v2 = ADAM_B2 * v_ref[...] + (1.0 - ADAM_B2) * (gv * gv)
            m_hat = m2 / (1.0 - ADAM_B1 ** ADAM_STEP)
            v_hat = v2 / (1.0 - ADAM_B2 ** ADAM_STEP)
            d_ref[...] = -ADAM_LR * (m_hat / (jnp.sqrt(v_hat) + ADAM_EPS) + ADAM_WD * w_ref[...])
            mo_ref[...] = m2
            vo_ref[...] = v2

    spec = pl.BlockSpec((rb, cols), lambda i: (i, 0))
    res = pl.pallas_call(
        body, out_shape=[jax.ShapeDtypeStruct(ws[0].shape, F32)] * (3 * n), grid=(rows // rb,),
        in_specs=[spec] * (4 * n), out_specs=[spec] * (3 * n), compiler_params=_params(), name="adamw_" + tag,
    )(*ws, *gs, *ms, *vs)
    return [(res[j], res[n + j], res[2 * n + j]) for j in range(n)]


def _pack_vec(p, d, width):
    pad = lambda v: jnp.pad(v, (0, LANES - v.shape[0]))
    row3 = jnp.concatenate([p["pool_scale"], p["out_norm_pool"]])
    row4 = jnp.concatenate([p["out_norm_attn"], pad(p["q_norm"]), pad(p["k_norm"]), pad(p["b_forget"]),
                            jnp.zeros((d - width - 3 * LANES,), F32)])
    rows = [p["ffn1_norm"], p["mix_norm"], p["ffn2_norm"], row3, row4]
    return jnp.pad(jnp.stack(rows), ((0, VEC_ROWS - len(rows)), (0, 0)))


def _unpack_vec(vec, width):
    return dict(ffn1_norm=vec[0], mix_norm=vec[1], ffn2_norm=vec[2], pool_scale=vec[3, :width],
                out_norm_pool=vec[3, width:2 * width], out_norm_attn=vec[4, :width],
                q_norm=vec[4, width:width + HEAD_DIM], k_norm=vec[4, width + LANES:width + LANES + HEAD_DIM],
                b_forget=vec[4, width + 2 * LANES:width + 2 * LANES + N_HEADS])


WEIGHT_NAMES = ("ffn1_norm", "ffn1_w_gate", "ffn1_w_up", "ffn1_w_down", "mix_norm", "w_in", "b_forget", "pool_w",
                "pool_scale", "q_norm", "k_norm", "out_norm_pool", "out_norm_attn", "w_out", "ffn2_norm",
                "ffn2_w_gate", "ffn2_w_up", "ffn2_w_down")
BIG_NAMES = ("ffn1_w_gate", "ffn1_w_up", "ffn1_w_down", "w_in", "w_out", "ffn2_w_gate", "ffn2_w_up", "ffn2_w_down")
TRANSPOSED_NAMES = ("ffn1_w_gate", "ffn1_w_up", "w_in", "ffn2_w_gate", "ffn2_w_up")
FFN1_NAMES = ("ffn1_w_gate", "ffn1_w_up", "ffn1_w_down")
MIX_NAMES = ("w_in", "w_out")
FFN2_NAMES = ("ffn2_w_gate", "ffn2_w_up", "ffn2_w_down")
REST_NAMES = MIX_NAMES + FFN2_NAMES


def kernel(x, ffn1_norm, ffn1_w_gate, ffn1_w_up, ffn1_w_down, mix_norm, w_in, b_forget, pool_w, pool_scale, q_norm, k_norm, out_norm_pool, out_norm_attn, w_out, ffn2_norm, ffn2_w_gate, ffn2_w_up, ffn2_w_down, loss_target, m_ffn1_norm, m_ffn1_w_gate, m_ffn1_w_up, m_ffn1_w_down, m_mix_norm, m_w_in, m_b_forget, m_pool_w, m_pool_scale, m_q_norm, m_k_norm, m_out_norm_pool, m_out_norm_attn, m_w_out, m_ffn2_norm, m_ffn2_w_gate, m_ffn2_w_up, m_ffn2_w_down, v_ffn1_norm, v_ffn1_w_gate, v_ffn1_w_up, v_ffn1_w_down, v_mix_norm, v_w_in, v_b_forget, v_pool_w, v_pool_scale, v_q_norm, v_k_norm, v_out_norm_pool, v_out_norm_attn, v_w_out, v_ffn2_norm, v_ffn2_w_gate, v_ffn2_w_up, v_ffn2_w_down):
    given = dict(locals())
    w = {n: given[n] for n in WEIGHT_NAMES}
    m = {n: given["m_" + n] for n in WEIGHT_NAMES}
    v = {n: given["v_" + n] for n in WEIGHT_NAMES}
    n_batch, seq, d = x.shape
    width = pool_scale.shape[0]
    in_rows = w_in.shape[1]
    in_cols = N_CHIPS * in_rows
    in_pad = -(-in_rows // 32) * 32
    in_cols_pad = in_cols - N_HEADS + LANES

    work = lambda a, n: a.T if n in TRANSPOSED_NAMES else a
    exchanged = lambda a, n: jnp.pad(a, ((0, in_pad - in_rows), (0, 0))) if n == "w_in" else a

    mesh_x, mesh_y, mesh_c = _mesh_pos()
    ids = jnp.stack([2 * mesh_x + mesh_y, mesh_c]).astype(jnp.int32)

    row = lambda a: a.reshape(1, -1)
    g1, gm, g2, ps, onp, ona = (row(a) for a in (ffn1_norm, mix_norm, ffn2_norm, pool_scale, out_norm_pool, out_norm_attn))
    qn, kn = row(jnp.tile(q_norm, N_HEADS)), row(jnp.tile(k_norm, N_HEADS))
    bf = row(jnp.pad(b_forget, (0, LANES - N_HEADS)))
    pwb = pool_w.astype(BF16)
    xf, tgt = x.reshape(n_batch * seq, d), loss_target.reshape(n_batch * seq, d)

    def grouped(call, names, *lists):
        out = [None] * len(names)
        for idx in _same_shape_groups(lists[0]):
            res = call(*[[lst[i] for i in idx] for lst in lists], names[idx[0]])
            for i, r in zip(idx, res):
                out[i] = r
        return out

    placed = dict(zip(BIG_NAMES, grouped(lambda ws, tag: _place_cast(ws, ids, tag), BIG_NAMES,
                                         [exchanged(work(w[n], n), n) for n in BIG_NAMES])))
    wg1, wu1, wd1 = _run_plan(_plan_gather([placed[n] for n in FFN1_NAMES]), "gather_ffn1")
    (x1, h1, a1, b1, s1), rest = _ffn_fwd(xf, g1, wg1, wu1, wd1, plan=_plan_gather([placed[n] for n in REST_NAMES]))
    w_in_all, w_out_all, wg2, wu2, wd2 = rest
    w_in_t = jnp.pad(w_in_all[:, :in_rows].reshape(in_cols, d), ((0, in_cols_pad - in_cols), (0, 0)))
    w_out_full = w_out_all.reshape(N_CHIPS * w_out.shape[0], d)
    woa, wob = w_out_full[:width], w_out_full[width:]

    hm, pv, q, k, qh, kh, vb, f = _mix_proj(x1, gm, w_in_t, qn, kn, width, width)
    qa, ka = _forget_prefix(f, bf, qh, kh, n_batch, seq)
    yp = _pool_fwd(pv, pwb, ps, onp, n_batch, seq)
    o, lse = _attn_fwd(qa, ka, vb, n_batch, seq)
    x2, ya = _mix_out(x1, yp, o, ona, woa, wob)
    (dy, h2, a2, b2, s2, lpart), _ = _ffn_fwd(x2, g2, wg2, wu2, wd2, target=tgt)

    def to_chips(gs, arrived, tags):
        return grouped(lambda g, r, tag: _add_sibling(g, r, ids, tag), tags, gs, arrived)

    def own_rows(gs, from_sibling, from_chips, tags):
        return grouped(lambda g, ra, rb, tag: _add_chips(g, ra, rb, ids, tag), tags, gs, from_sibling, from_chips)

    (dx2, da2, db2, dg2), _ = _ffn_bwd_x(dy, x2, g2, a2, b2, wg2, wu2, wd2, "ffn2_bwd_x")
    dw2, _ = _ffn_bwd_w(h2, s2, da2, db2, dy, "ffn2_bwd_w")
    (dyp, do, delta, dwoa, dwob, dona), sib2 = _mix_out_bwd(dx2, o, yp, ya, ona, woa, wob, plan=_plan_sibling_halves(dw2))
    dpv, dpw, dps, donp = _pool_bwd(pv, dyp, pwb, ps, onp, n_batch, seq)
    (dqh, dfq), chips2 = _attn_bwd_q(qa, ka, vb, do, lse, delta, n_batch, seq,
                                     plan=_plan_chip_exchange(to_chips(dw2, sib2, FFN2_NAMES)))
    (dkh, dv, dfk), red2 = _attn_bwd_kv(qa, ka, vb, do, lse, delta, n_batch, seq,
                                        plan=_plan_sibling_share(own_rows(dw2, sib2, chips2, FFN2_NAMES)))
    df, dbf = _forget_bwd(dfq, dfk, f, bf, n_batch, seq)
    dx1, dw_in_t, dgm, dqn, dkn = _mix_in_bwd(dx2, x1, gm, hm, dpv, dqh, q, dkh, k, dv, df, qn, kn, w_in_t)
    d_w_in = jnp.pad(dw_in_t[:in_cols].reshape(N_CHIPS, in_rows, d), ((0, 0), (0, in_pad - in_rows), (0, 0)))
    d_w_out = jnp.concatenate([dwoa, dwob], axis=0).reshape(N_CHIPS, w_out.shape[0], d)
    dwm = [d_w_in, d_w_out]
    (gx, da1, db1, dg1), sibm = _ffn_bwd_x(dx1, xf, g1, a1, b1, wg1, wu1, wd1, "ffn1_bwd_x", plan=_plan_sibling_halves(dwm))
    dw1, chipsm = _ffn_bwd_w(h1, s1, da1, db1, dx1, "ffn1_bwd_w", plan=_plan_chip_exchange(to_chips(dwm, sibm, MIX_NAMES)))
    sib1 = _run_plan(_plan_sibling_halves(dw1), "sibling_halves")
    chips1 = _run_plan(_plan_chip_exchange(to_chips(dw1, sib1, FFN1_NAMES)), "chip_exchange")
    last = _run_plan(_plan_sibling_share(own_rows(dw1, sib1, chips1, FFN1_NAMES) + own_rows(dwm, sibm, chipsm, MIX_NAMES)),
                     "sibling_share")
    reduced = dict(zip(FFN1_NAMES + MIX_NAMES + FFN2_NAMES, list(last) + list(red2)))
    reduced["w_in"] = reduced["w_in"][:in_rows]

    part = dict(ffn1_norm=dg1, mix_norm=dgm, ffn2_norm=dg2, b_forget=dbf, pool_scale=dps, out_norm_pool=donp,
                out_norm_attn=dona, qn=dqn, kn=dkn, pool_w=dpw.reshape(n_batch, -1, pool_w.shape[-1]), loss=lpart)
    g_vec, g_pw = _small_allreduce(part, d, width)
    loss = g_vec[5, 0]
    grads, delta, new_m, new_v = {}, {}, {}, {}
    for names in (FFN2_NAMES, FFN1_NAMES, ("w_in",), ("w_out",)):
        stepped = _adamw([work(w[n], n) for n in names], [reduced[n] for n in names], [work(m[n], n) for n in names],
                         [work(v[n], n) for n in names], names[0])
        for n, step in zip(names, stepped):
            grads[n], delta[n], new_m[n], new_v[n] = (work(a, n) for a in (reduced[n], *step))
    flat_pw = lambda a: a.reshape(-1, a.shape[-1])
    (d_pw, m_pw, v_pw), = _adamw([flat_pw(pool_w)], [g_pw], [flat_pw(m_pool_w)], [flat_pw(v_pool_w)], "pool_w")
    (d_vec, m_vec, v_vec), = _adamw([_pack_vec(w, d, width)], [g_vec], [_pack_vec(m, d, width)], [_pack_vec(v, d, width)],
                                    "vectors")
    grads.update(_unpack_vec(g_vec, width), pool_w=g_pw.reshape(pool_w.shape))
    delta.update(_unpack_vec(d_vec, width), pool_w=d_pw.reshape(pool_w.shape))
    new_m.update(_unpack_vec(m_vec, width), pool_w=m_pw.reshape(pool_w.shape))
    new_v.update(_unpack_vec(v_vec, width), pool_w=v_pw.reshape(pool_w.shape))
    return (loss, gx.reshape(x.shape), *[grads[n] for n in WEIGHT_NAMES], *[delta[n] for n in WEIGHT_NAMES],
            *[new_m[n] for n in WEIGHT_NAMES], *[new_v[n] for n in WEIGHT_NAMES])
```

```python
import functools

import jax
import jax.numpy as jnp
from jax import lax
from jax.experimental import pallas as pl
from jax.experimental.pallas import tpu as pltpu

F32 = jnp.float32
BF16 = jnp.bfloat16
EPS = 1e-6
NEG = -1e30
ADAM_LR = 0.001
ADAM_B1 = 0.9
ADAM_B2 = 0.999
ADAM_EPS = 1e-08
ADAM_WD = 0.01
ADAM_STEP = 10
POOL_WINDOWS = (2, 4, 8, 16)
HEAD_DIM = 64
N_HEADS = 8
LANES = 128
N_CHIPS = 4
ATT_BLOCK = 512
ATT_SUB = 128
VMEM_LIMIT = 56 * 1024 * 1024
MESH_AXES = ("x", "y", "c")
ANY = pl.BlockSpec(memory_space=pl.ANY)
VM = pl.BlockSpec(memory_space=pltpu.VMEM)


def _params(**kw):
    return pltpu.CompilerParams(vmem_limit_bytes=VMEM_LIMIT, **kw)


def _dot(a, b):
    return jnp.dot(a, b, preferred_element_type=F32)


def _dot_nt(a, b):
    return lax.dot_general(a, b, (((1,), (1,)), ((), ())), preferred_element_type=F32)


def _dot_tn(a, b):
    return lax.dot_general(a, b, (((0,), (0,)), ((), ())), preferred_element_type=F32)


def _sigmoid(z):
    return 1.0 / (1.0 + jnp.exp(-z))


def _rms(xf):
    return lax.rsqrt(jnp.mean(xf * xf, axis=-1, keepdims=True) + EPS)


def _rms_bwd(xf, r, gain, dh):
    xh = xf * r
    dyg = dh * gain
    return r * (dyg - xh * jnp.mean(dyg * xh, axis=-1, keepdims=True)), dh * xh


def _total(v):
    return jnp.sum(jnp.sum(v, axis=1, keepdims=True), axis=0, keepdims=True)


def _ffn_fwd(x, gain, wg, wu, wd, target=None, plan=None):
    t, d = x.shape
    nch, fc, _ = wg.shape
    tm = min(512, t)
    nt = t // tm
    with_loss = target is not None

    def body(*refs):
        if with_loss:
            x_ref, g_ref, wg_ref, wu_ref, wd_ref, t_ref, o_ref, h_ref, a_ref, b_ref, s_ref, l_ref, acc_ref = refs
        else:
            x_ref, g_ref, wg_ref, wu_ref, wd_ref, o_ref, h_ref, a_ref, b_ref, s_ref, acc_ref = refs
        k = pl.program_id(1)

        @pl.when(k == 0)
        def _():
            xf = x_ref[...]
            h_ref[...] = ((xf * _rms(xf)) * g_ref[...]).astype(BF16)
            acc_ref[...] = jnp.zeros_like(acc_ref)

        h = h_ref[...]
        a = _dot_nt(h, wg_ref[...])
        b = _dot_nt(h, wu_ref[...])
        sb = ((a * (0.5 * jnp.tanh(0.5 * a) + 0.5)) * b).astype(BF16)
        a_ref[...] = a.astype(BF16)
        b_ref[...] = b.astype(BF16)
        s_ref[...] = sb
        acc_ref[...] += _dot(sb, wd_ref[...])

        @pl.when(k == nch - 1)
        def _():
            y = x_ref[...] + 0.5 * acc_ref[...]
            if with_loss:
                e = y - t_ref[...]
                o_ref[...] = e * (1.0 / d)
                l_ref[...] = jnp.broadcast_to(_total(e * e) * (0.5 / d), l_ref.shape)
            else:
                o_ref[...] = y

    row = pl.BlockSpec((tm, d), lambda i, k: (i, 0))
    chunk = pl.BlockSpec((None, fc, d), lambda i, k: (k, 0, 0))
    act = pl.BlockSpec((None, tm, fc), lambda i, k: (k, i, 0))
    in_specs = [row, pl.BlockSpec((1, d), lambda i, k: (0, 0)), chunk, chunk, chunk]
    out_shape = [jax.ShapeDtypeStruct((t, d), F32), jax.ShapeDtypeStruct((t, d), BF16)]
    out_shape += [jax.ShapeDtypeStruct((nch, t, fc), BF16)] * 3
    out_specs = [row, row, act, act, act]
    args = [x, gain, wg, wu, wd]
    if with_loss:
        in_specs.append(row)
        args.append(target)
        out_shape.append(jax.ShapeDtypeStruct((nt, 8, LANES), F32))
        out_specs.append(pl.BlockSpec((None, 8, LANES), lambda i, k: (i, 0, 0)))
    return _pallas(body, name="ffn_fwd_loss" if with_loss else "ffn_fwd", args=args, in_specs=in_specs,
                   out_shape=out_shape, out_specs=out_specs, grid=(nt, nch),
                   scratch_shapes=[pltpu.VMEM((tm, d), F32)], plan=plan)


def _ffn_bwd_x(dy, x, gain, a, b, wg, wu, wd, name, plan=None):
    t, d = x.shape
    nch, fc, _ = wg.shape
    tm = min(512, t)
    nt = t // tm

    def body(dy_ref, x_ref, g_ref, a_ref, b_ref, wg_ref, wu_ref, wd_ref, dx_ref, da_ref, db_ref, dg_ref, acc_ref):
        k = pl.program_id(1)

        @pl.when(k == 0)
        def _():
            acc_ref[...] = jnp.zeros_like(acc_ref)

        ds = _dot_nt(dy_ref[...].astype(BF16), wd_ref[...])
        av = a_ref[...].astype(F32)
        bv = b_ref[...].astype(F32)
        th = jnp.tanh(0.5 * av)
        half_sig = 0.25 * th + 0.25
        dab = ((ds * bv) * (half_sig * (1.0 + av * (0.5 - 0.5 * th)))).astype(BF16)
        dbb = (ds * (av * half_sig)).astype(BF16)
        da_ref[...] = dab
        db_ref[...] = dbb
        acc_ref[...] += _dot(dab, wg_ref[...]) + _dot(dbb, wu_ref[...])

        @pl.when(k == nch - 1)
        def _():
            xf = x_ref[...]
            dxn, dgr = _rms_bwd(xf, _rms(xf), g_ref[...], acc_ref[...])
            dx_ref[...] = dy_ref[...] + dxn
            dg_ref[...] = jnp.sum(dgr, axis=0, keepdims=True)

    row = pl.BlockSpec((tm, d), lambda i, k: (i, 0))
    chunk = pl.BlockSpec((None, fc, d), lambda i, k: (k, 0, 0))
    act = pl.BlockSpec((None, tm, fc), lambda i, k: (k, i, 0))
    return _pallas(
        body, name=name, args=[dy, x, gain, a, b, wg, wu, wd],
        out_shape=[jax.ShapeDtypeStruct((t, d), F32), jax.ShapeDtypeStruct((nch, t, fc), BF16),
                   jax.ShapeDtypeStruct((nch, t, fc), BF16), jax.ShapeDtypeStruct((nt, 1, d), F32)],
        grid=(nt, nch),
        in_specs=[row, row, pl.BlockSpec((1, d), lambda i, k: (0, 0)), act, act, chunk, chunk, chunk],
        out_specs=[row, act, act, pl.BlockSpec((None, 1, d), lambda i, k: (i, 0, 0))],
        scratch_shapes=[pltpu.VMEM((tm, d), F32)], plan=plan)


def _ffn_bwd_w(h, s, da, db, dy, name, plan=None):
    t, d = h.shape
    nch, _, fc = s.shape
    tm = min(1024, t)
    nt = t // tm

    def body(h_ref, s_ref, da_ref, db_ref, dy_ref, dwg_ref, dwu_ref, dwd_ref):
        @pl.when(pl.program_id(1) == 0)
        def _():
            dwg_ref[...] = jnp.zeros_like(dwg_ref)
            dwu_ref[...] = jnp.zeros_like(dwu_ref)
            dwd_ref[...] = jnp.zeros_like(dwd_ref)

        hv = h_ref[...]
        dwg_ref[...] += _dot_tn(da_ref[...], hv)
        dwu_ref[...] += _dot_tn(db_ref[...], hv)
        dwd_ref[...] += _dot_tn(s_ref[...], (0.5 * dy_ref[...]).astype(BF16))

    row = pl.BlockSpec((tm, d), lambda k, i: (i, 0))
    act = pl.BlockSpec((None, tm, fc), lambda k, i: (k, i, 0))
    chunk = pl.BlockSpec((None, fc, d), lambda k, i: (k, 0, 0))
    return _pallas(body, name=name, args=[h, s, da, db, dy], out_shape=[jax.ShapeDtypeStruct((nch, fc, d), F32)] * 3,
                   grid=(nch, nt), in_specs=[row, act, act, act, row], out_specs=[chunk, chunk, chunk], plan=plan)


def _head_masks():
    lane = lax.broadcasted_iota(jnp.int32, (1, LANES), 1)
    return lane < HEAD_DIM


def _head_rms(x, lo):
    x2 = x * x
    s0 = jnp.sum(jnp.where(lo, x2, 0.0), axis=1, keepdims=True)
    s1 = jnp.sum(jnp.where(lo, 0.0, x2), axis=1, keepdims=True)
    return jnp.where(lo, lax.rsqrt(s0 * (1.0 / HEAD_DIM) + EPS), lax.rsqrt(s1 * (1.0 / HEAD_DIM) + EPS))


def _head_mean(v, lo):
    s0 = jnp.sum(jnp.where(lo, v, 0.0), axis=1, keepdims=True)
    s1 = jnp.sum(jnp.where(lo, 0.0, v), axis=1, keepdims=True)
    return jnp.where(lo, s0, s1) * (1.0 / HEAD_DIM)


def _mix_proj(x1, gain, wt, qn, kn, pool_width, attn_width):
    t, d = x1.shape
    tm = min(512, t)
    nt = t // tm
    scale = HEAD_DIM ** -0.5
    c_q, c_k, c_v = pool_width, pool_width + attn_width, pool_width + 2 * attn_width
    c_f = c_v + attn_width

    def body(x_ref, g_ref, wt_ref, qn_ref, kn_ref, hm_ref, pv_ref, q_ref, k_ref, qh_ref, kh_ref, vb_ref, f_ref):
        xf = x_ref[...]
        hm = ((xf * _rms(xf)) * g_ref[...]).astype(BF16)
        hm_ref[...] = hm
        f_ref[...] = _dot_nt(hm, wt_ref[c_f:c_f + LANES, :])
        pv_ref[...] = _dot_nt(hm, wt_ref[0:pool_width, :])
        vb_ref[...] = _dot_nt(hm, wt_ref[c_v:c_v + attn_width, :]).astype(BF16)
        lo = _head_masks()
        for c0, raw_ref, hat_ref, n_ref, mul in ((c_q, q_ref, qh_ref, qn_ref, scale), (c_k, k_ref, kh_ref, kn_ref, 1.0)):
            raw = _dot_nt(hm, wt_ref[c0:c0 + attn_width, :])
            raw_ref[...] = raw
            for blk in range(attn_width // LANES):
                sl = slice(blk * LANES, (blk + 1) * LANES)
                xb = raw[:, sl]
                hat_ref[:, sl] = (((xb * _head_rms(xb, lo)) * n_ref[:, sl]) * mul).astype(BF16)

    row = pl.BlockSpec((tm, d), lambda i: (i, 0))
    half = pl.BlockSpec((tm, attn_width), lambda i: (i, 0))
    const = lambda shape: pl.BlockSpec(shape, lambda i: (0, 0))
    return pl.pallas_call(
        body,
        out_shape=[jax.ShapeDtypeStruct((t, d), BF16), jax.ShapeDtypeStruct((t, pool_width), F32),
                   jax.ShapeDtypeStruct((t, attn_width), F32), jax.ShapeDtypeStruct((t, attn_width), F32),
                   jax.ShapeDtypeStruct((t, attn_width), BF16), jax.ShapeDtypeStruct((t, attn_width), BF16),
                   jax.ShapeDtypeStruct((t, attn_width), BF16), jax.ShapeDtypeStruct((t, LANES), F32)],
        grid=(nt,),
        in_specs=[row, const((1, d)), const(wt.shape), const((1, attn_width)), const((1, attn_width))],
        out_specs=[row, pl.BlockSpec((tm, pool_width), lambda i: (i, 0)), half, half, half, half, half,
                   pl.BlockSpec((tm, LANES), lambda i: (i, 0))],
        compiler_params=_params(), name="mix_proj",
    )(x1, gain, wt, qn, kn)


def _shift_down(v, dist, row):
    return jnp.where(row >= dist, pltpu.roll(v, dist, 0), 0.0)


def _shift_up(v, dist, row, n):
    return jnp.where(row + dist < n, pltpu.roll(v, n - dist, 0), 0.0)


def _aug_lane(e):
    return HEAD_DIM if e == 0 else 0


def _forget_prefix(f, bias, qh, kh, n_batch, seq):
    def body(f_ref, b_ref, q_ref, k_ref, qa_ref, ka_ref):
        z = f_ref[...] + b_ref[...]
        acc = jnp.minimum(z, 0.0) - jnp.log(1.0 + jnp.exp(-jnp.abs(z)))
        row = lax.broadcasted_iota(jnp.int32, (seq, 1), 0)
        dist = 1
        while dist < seq:
            acc = acc + _shift_down(acc, dist, row)
            dist *= 2
        lane = lax.broadcasted_iota(jnp.int32, (1, LANES), 1)
        for h in range(N_HEADS):
            pair, e = divmod(h, 2)
            a0 = _aug_lane(e)
            own = (lane < HEAD_DIM) if e == 0 else (lane >= HEAD_DIM)
            fh = _pick_lane(acc, h)
            hi = fh.astype(BF16).astype(F32)
            rest = fh - hi
            mid = rest.astype(BF16).astype(F32)
            low = rest - mid
            q_ones = (lane >= a0 + 3) & (lane < a0 + 6)
            k_ones = (lane >= a0) & (lane < a0 + 3)
            q_aug = jnp.where(lane == a0, hi, jnp.where(lane == a0 + 1, mid, jnp.where(lane == a0 + 2, low,
                              jnp.where(q_ones, 1.0, 0.0))))
            k_aug = jnp.where(k_ones, 1.0, jnp.where(lane == a0 + 3, -hi, jnp.where(lane == a0 + 4, -mid,
                              jnp.where(lane == a0 + 5, -low, 0.0))))
            src = slice(pair * LANES, (pair + 1) * LANES)
            dst = slice(h * LANES, (h + 1) * LANES)
            qa_ref[:, dst] = jnp.where(own, q_ref[:, src].astype(F32), q_aug).astype(BF16)
            ka_ref[:, dst] = jnp.where(own, k_ref[:, src].astype(F32), k_aug).astype(BF16)

    width = qh.shape[1]
    tok = pl.BlockSpec((seq, width), lambda b: (b, 0))
    aug = pl.BlockSpec((seq, N_HEADS * LANES), lambda b: (b, 0))
    return pl.pallas_call(
        body, out_shape=[jax.ShapeDtypeStruct((n_batch * seq, N_HEADS * LANES), BF16)] * 2, grid=(n_batch,),
        in_specs=[pl.BlockSpec((seq, LANES), lambda b: (b, 0)), pl.BlockSpec((1, LANES), lambda b: (0, 0)), tok, tok],
        out_specs=[aug, aug], compiler_params=_params(), name="forget_prefix",
    )(f, bias, qh, kh)


def _pool_groups(pv_ref, pw_ref, ps_ref, seq):
    row = lax.broadcasted_iota(jnp.int32, (seq, 1), 0)
    pos = (row + 1).astype(F32)
    out = []
    for g, win in enumerate(POOL_WINDOWS):
        sl = slice(g * LANES, (g + 1) * LANES)
        xg = pv_ref[:, sl]
        acc = xg
        dist = 1
        while dist < win:
            acc = acc + _shift_down(acc, dist, row)
            dist *= 2
        pooled = (acc / jnp.minimum(pos, float(win)) - xg).astype(BF16)
        mixed = _dot(pooled, pw_ref[g])
        out.append((pooled, mixed, mixed * ps_ref[:, sl]))
    return out


def _pool_fwd(pv, pw, ps, onp, n_batch, seq):
    width = pv.shape[1]

    def body(pv_ref, pw_ref, ps_ref, on_ref, y_ref):
        groups = _pool_groups(pv_ref, pw_ref, ps_ref, seq)
        ssq = sum(jnp.sum(ms * ms, axis=1, keepdims=True) for _, _, ms in groups)
        r = lax.rsqrt(ssq * (1.0 / width) + EPS)
        for g, (_, _, ms) in enumerate(groups):
            sl = slice(g * LANES, (g + 1) * LANES)
            y_ref[:, sl] = ((ms * r) * on_ref[:, sl]).astype(BF16)

    return pl.pallas_call(
        body, out_shape=jax.ShapeDtypeStruct((n_batch * seq, width), BF16), grid=(n_batch,),
        in_specs=[pl.BlockSpec((seq, width), lambda b: (b, 0)), pl.BlockSpec(pw.shape, lambda b: (0, 0, 0)),
                  pl.BlockSpec((1, width), lambda b: (0, 0)), pl.BlockSpec((1, width), lambda b: (0, 0))],
        out_specs=pl.BlockSpec((seq, width), lambda b: (b, 0)),
        compiler_params=_params(), name="pool_fwd",
    )(pv, pw, ps, onp)


def _pool_bwd(pv, dyp, pw, ps, onp, n_batch, seq):
    width = pv.shape[1]

    def body(pv_ref, dy_ref, pw_ref, ps_ref, on_ref, dpv_ref, dpw_ref, dps_ref, don_ref):
        groups = _pool_groups(pv_ref, pw_ref, ps_ref, seq)
        ssq = sum(jnp.sum(ms * ms, axis=1, keepdims=True) for _, _, ms in groups)
        r = lax.rsqrt(ssq * (1.0 / width) + EPS)
        mean = sum(jnp.sum((dy_ref[:, g * LANES:(g + 1) * LANES] * on_ref[:, g * LANES:(g + 1) * LANES]) * (ms * r),
                           axis=1, keepdims=True) for g, (_, _, ms) in enumerate(groups)) * (1.0 / width)
        row = lax.broadcasted_iota(jnp.int32, (seq, 1), 0)
        pos = (row + 1).astype(F32)
        for g, (pooled, mixed, ms) in enumerate(groups):
            sl = slice(g * LANES, (g + 1) * LANES)
            dy = dy_ref[:, sl]
            xh = ms * r
            don_ref[:, sl] = jnp.sum(dy * xh, axis=0, keepdims=True)
            dms = r * (dy * on_ref[:, sl] - xh * mean)
            dps_ref[:, sl] = jnp.sum(dms * mixed, axis=0, keepdims=True)
            dmix = (dms * ps_ref[:, sl]).astype(BF16)
            dpw_ref[g] = _dot_tn(pooled, dmix)
            dpool = _dot_nt(dmix, pw_ref[g])
            win = POOL_WINDOWS[g]
            acc = dpool / jnp.minimum(pos, float(win))
            dist = 1
            while dist < win:
                acc = acc + _shift_up(acc, dist, row, seq)
                dist *= 2
            dpv_ref[:, sl] = (acc - dpool).astype(BF16)

    tok = pl.BlockSpec((seq, width), lambda b: (b, 0))
    vec = pl.BlockSpec((1, width), lambda b: (0, 0))
    pvec = pl.BlockSpec((None, 1, width), lambda b: (b, 0, 0))
    return pl.pallas_call(
        body,
        out_shape=[jax.ShapeDtypeStruct((n_batch * seq, width), BF16),
                   jax.ShapeDtypeStruct((n_batch,) + pw.shape, F32),
                   jax.ShapeDtypeStruct((n_batch, 1, width), F32), jax.ShapeDtypeStruct((n_batch, 1, width), F32)],
        grid=(n_batch,),
        in_specs=[tok, tok, pl.BlockSpec(pw.shape, lambda b: (0, 0, 0)), vec, vec],
        out_specs=[tok, pl.BlockSpec((None,) + pw.shape, lambda b: (b, 0, 0, 0)), pvec, pvec],
        compiler_params=_params(), name="pool_bwd",
    )(pv, dyp, pw, ps, onp)


def _pick_lane(tile, idx):
    lane = lax.broadcasted_iota(jnp.int32, (1, LANES), 1)
    return jnp.sum(jnp.where(lane == idx, tile, 0.0), axis=1, keepdims=True)


def _pick_row(tile, idx):
    sub = lax.broadcasted_iota(jnp.int32, (tile.shape[0], 1), 0)
    return jnp.sum(jnp.where(sub == idx, tile, 0.0), axis=0, keepdims=True)


def _put_lane(col, idx):
    lane = lax.broadcasted_iota(jnp.int32, (1, LANES), 1)
    return jnp.where(lane == idx, col, 0.0)


def _head_select(e):
    lo = _head_masks()
    return lo if e == 0 else jnp.logical_not(lo)


def _causal(st, shift):
    row = lax.broadcasted_iota(jnp.int32, st.shape, 0)
    col = lax.broadcasted_iota(jnp.int32, st.shape, 1) + shift
    return jnp.where(col >= row, st, NEG)


def _transpose_blocks(a):
    rows, cols = a.shape
    return jnp.concatenate(
        [jnp.concatenate([a[r:r + LANES, c:c + LANES].T for r in range(0, rows, LANES)], axis=1)
         for c in range(0, cols, LANES)], axis=0)


def _stat_rows(ref, head, nsub):
    return jnp.concatenate([_pick_row(ref[a], head) for a in range(nsub)], axis=1)


def _accumulate(ref, value, first):
    @pl.when(first)
    def _():
        ref[...] = value

    @pl.when(jnp.logical_not(first))
    def _():
        ref[...] += value


def _attn_fwd(qa, ka, vb, n_batch, seq):
    tq = min(ATT_BLOCK, seq)
    nq, nsub, tk = seq // tq, tq // ATT_SUB, tq
    pairs = vb.shape[1] // LANES

    def body(q_ref, k_ref, v_ref, o_ref, lse_ref, acc_ref):
        i, p = pl.program_id(1), pl.program_id(2)
        row_lo = lax.broadcasted_iota(jnp.int32, (LANES, 1), 0) < HEAD_DIM
        qs = [q_ref[:, e * LANES:(e + 1) * LANES] for e in range(2)]
        acc_ref[...] = jnp.zeros_like(acc_ref)

        def tile(off, stats, diagonal):
            vj = v_ref[pl.ds(off, tk), :]
            new, alphas, pvs = [], [], []
            for e in range(2):
                st = _dot_nt(k_ref[pl.ds(off, tk), e * LANES:(e + 1) * LANES], qs[e])
                if diagonal:
                    st = _causal(st, 0)
                m, l = stats[e]
                m_new = jnp.maximum(m, jnp.max(st, axis=0, keepdims=True))
                alpha = jnp.exp(m - m_new)
                pt = jnp.exp(st - m_new)
                new.append((m_new, alpha * l + jnp.sum(pt, axis=0, keepdims=True)))
                alphas.append(alpha)
                pvs.append(_dot_tn(jnp.where(_head_select(e), vj, jnp.zeros_like(vj)), pt.astype(BF16)))
            acc_ref[...] = acc_ref[...] * jnp.where(row_lo, alphas[0], alphas[1]) + (pvs[0] + pvs[1])
            return tuple(new)

        init = ((jnp.full((1, tq), NEG, F32), jnp.zeros((1, tq), F32)),) * 2
        stats = lax.fori_loop(0, i, lambda j, st: tile(pl.multiple_of(j * tk, tk), st, False), init)
        (m0, l0), (m1, l1) = tile(pl.multiple_of(i * tk, tk), stats, True)
        out_t = acc_ref[...] / jnp.where(row_lo, l0, l1)
        sub = lax.broadcasted_iota(jnp.int32, (8, 1), 0)
        lse0, lse1 = m0 + jnp.log(l0), m1 + jnp.log(l1)
        for a in range(nsub):
            sl = slice(a * ATT_SUB, (a + 1) * ATT_SUB)
            o_ref[sl, :] = out_t[:, sl].T
            rows = jnp.where(sub == 2 * p, lse0[:, sl], 0.0) + jnp.where(sub == 2 * p + 1, lse1[:, sl], 0.0)
            _accumulate(lse_ref.at[a], rows, p == 0)

    return pl.pallas_call(
        body,
        out_shape=[jax.ShapeDtypeStruct((n_batch * seq, pairs * LANES), F32),
                   jax.ShapeDtypeStruct((n_batch * seq // ATT_SUB, 8, ATT_SUB), F32)],
        grid=(n_batch, nq, pairs),
        in_specs=[pl.BlockSpec((tq, 2 * LANES), lambda b, i, p: (b * nq + i, p)),
                  pl.BlockSpec((seq, 2 * LANES), lambda b, i, p: (b, p)),
                  pl.BlockSpec((seq, LANES), lambda b, i, p: (b, p))],
        out_specs=[pl.BlockSpec((tq, LANES), lambda b, i, p: (b * nq + i, p)),
                   pl.BlockSpec((nsub, 8, ATT_SUB), lambda b, i, p: (b * nq + i, 0, 0))],
        scratch_shapes=[pltpu.VMEM((LANES, tq), F32)],
        compiler_params=_params(), name="attn_fwd",
    )(qa, ka, vb)


def _attn_bwd_q(qa, ka, vb, do, lse, delta, n_batch, seq, plan=None):
    tq = min(ATT_BLOCK, seq)
    nq, nsub, tk = seq // tq, tq // ATT_SUB, tq
    pairs = vb.shape[1] // LANES

    def body(q_ref, k_ref, v_ref, do_ref, lse_ref, dl_ref, dq_ref, dfq_ref, acc0_ref, acc1_ref):
        i, p = pl.program_id(1), pl.program_id(2)
        accs = (acc0_ref, acc1_ref)
        qs = [q_ref[:, e * LANES:(e + 1) * LANES] for e in range(2)]
        dov = do_ref[...]
        ls = [_stat_rows(lse_ref, 2 * p + e, nsub) for e in range(2)]
        dl = [_stat_rows(dl_ref, 2 * p + e, nsub) for e in range(2)]
        for acc in accs:
            acc[...] = jnp.zeros_like(acc)

        def tile(off, diagonal):
            vj = v_ref[pl.ds(off, tk), :]
            for e in range(2):
                kj = k_ref[pl.ds(off, tk), e * LANES:(e + 1) * LANES]
                st = _dot_nt(kj, qs[e])
                if diagonal:
                    st = _causal(st, 0)
                pt = jnp.exp(st - ls[e])
                dpt = _dot_nt(jnp.where(_head_select(e), vj, jnp.zeros_like(vj)), dov)
                accs[e][...] += _dot(_transpose_blocks(kj), (pt * (dpt - dl[e])).astype(BF16))

        def step(j, carry):
            tile(pl.multiple_of(j * tk, tk), False)
            return carry

        lax.fori_loop(0, i, step, 0)
        tile(pl.multiple_of(i * tk, tk), True)
        dq0, dq1 = _transpose_blocks(acc0_ref[...]), _transpose_blocks(acc1_ref[...])
        dq_ref[...] = jnp.where(_head_masks(), dq0, dq1)
        dfq = _put_lane(_pick_lane(dq0, _aug_lane(0)), 2 * p) + _put_lane(_pick_lane(dq1, _aug_lane(1)), 2 * p + 1)
        _accumulate(dfq_ref, dfq, p == 0)

    stat = pl.BlockSpec((nsub, 8, ATT_SUB), lambda b, i, p: (b * nq + i, 0, 0))
    blk = pl.BlockSpec((tq, LANES), lambda b, i, p: (b * nq + i, p))
    return _pallas(
        body, name="attn_bwd_q", args=[qa, ka, vb, do, lse, delta],
        out_shape=[jax.ShapeDtypeStruct((n_batch * seq, pairs * LANES), F32), jax.ShapeDtypeStruct((n_batch * seq, LANES), F32)],
        grid=(n_batch, nq, pairs),
        in_specs=[pl.BlockSpec((tq, 2 * LANES), lambda b, i, p: (b * nq + i, p)),
                  pl.BlockSpec((seq, 2 * LANES), lambda b, i, p: (b, p)),
                  pl.BlockSpec((seq, LANES), lambda b, i, p: (b, p)), blk, stat, stat],
        out_specs=[blk, pl.BlockSpec((tq, LANES), lambda b, i, p: (b * nq + i, 0))],
        scratch_shapes=[pltpu.VMEM((LANES, tq), F32), pltpu.VMEM((LANES, tq), F32)], plan=plan)


def _attn_bwd_kv(qa, ka, vb, do, lse, delta, n_batch, seq, plan=None):
    tkb = min(ATT_BLOCK, seq)
    nk, nsub, tq = seq // tkb, tkb // ATT_SUB, tkb
    n_tiles = seq // ATT_SUB
    pairs = vb.shape[1] // LANES

    def body(q_ref, k_ref, v_ref, do_ref, lse_ref, dl_ref, dk_ref, dv_ref, dfk_ref, dk0_ref, dk1_ref, dva_ref):
        j, p = pl.program_id(1), pl.program_id(2)
        dks = (dk0_ref, dk1_ref)
        ks = [k_ref[:, e * LANES:(e + 1) * LANES] for e in range(2)]
        vj = v_ref[...]
        vs = [jnp.where(_head_select(e), vj, jnp.zeros_like(vj)) for e in range(2)]
        for acc in (dk0_ref, dk1_ref, dva_ref):
            acc[...] = jnp.zeros_like(acc)

        def tile(t, diagonal):
            off = pl.multiple_of(t * tq, tq)
            dov = do_ref[pl.ds(off, tq), :]
            for e in range(2):
                qe = q_ref[pl.ds(off, tq), e * LANES:(e + 1) * LANES]
                st = _dot_nt(ks[e], qe)
                if diagonal:
                    st = _causal(st, 0)
                rows = lambda ref: jnp.concatenate([_pick_row(ref[t * nsub + a], 2 * p + e) for a in range(nsub)], axis=1)
                pt = jnp.exp(st - rows(lse_ref))
                dva_ref[...] += _dot(pt.astype(BF16), jnp.where(_head_select(e), dov, jnp.zeros_like(dov)))
                dst = pt * (_dot_nt(vs[e], dov) - rows(dl_ref))
                dks[e][...] += _dot(dst.astype(BF16), qe)

        def step(t, carry):
            tile(t, False)
            return carry

        lax.fori_loop(j + 1, nk, step, 0)
        tile(j, True)
        dk0, dk1 = dk0_ref[...], dk1_ref[...]
        dk_ref[...] = jnp.where(_head_masks(), dk0, dk1)
        dv_ref[...] = dva_ref[...].astype(BF16)
        dfk = (_put_lane(_pick_lane(dk0, _aug_lane(0) + 3), 2 * p)
               + _put_lane(_pick_lane(dk1, _aug_lane(1) + 3), 2 * p + 1))
        _accumulate(dfk_ref, -dfk, p == 0)

    stat = pl.BlockSpec((n_tiles, 8, ATT_SUB), lambda b, j, p: (b, 0, 0))
    blk = pl.BlockSpec((tkb, LANES), lambda b, j, p: (b * nk + j, p))
    acc = pltpu.VMEM((tkb, LANES), F32)
    return _pallas(
        body, name="attn_bwd_kv", args=[qa, ka, vb, do, lse, delta],
        out_shape=[jax.ShapeDtypeStruct((n_batch * seq, pairs * LANES), F32),
                   jax.ShapeDtypeStruct((n_batch * seq, pairs * LANES), BF16),
                   jax.ShapeDtypeStruct((n_batch * seq, LANES), F32)],
        grid=(n_batch, nk, pairs),
        in_specs=[pl.BlockSpec((seq, 2 * LANES), lambda b, j, p: (b, p)),
                  pl.BlockSpec((tkb, 2 * LANES), lambda b, j, p: (b * nk + j, p)), blk,
                  pl.BlockSpec((seq, LANES), lambda b, j, p: (b, p)), stat, stat],
        out_specs=[blk, blk, pl.BlockSpec((tkb, LANES), lambda b, j, p: (b * nk + j, 0))],
        scratch_shapes=[acc, acc, acc], plan=plan)


def _forget_bwd(dfq, dfk, f, bias, n_batch, seq):
    def body(dfq_ref, dfk_ref, f_ref, b_ref, df_ref, db_ref):
        acc = dfq_ref[...] + dfk_ref[...]
        row = lax.broadcasted_iota(jnp.int32, (seq, 1), 0)
        dist = 1
        while dist < seq:
            acc = acc + _shift_up(acc, dist, row, seq)
            dist *= 2
        df = acc * _sigmoid(-(f_ref[...] + b_ref[...]))
        df_ref[...] = df
        db_ref[...] = jnp.sum(df, axis=0, keepdims=True)

    col = pl.BlockSpec((seq, LANES), lambda b: (b, 0))
    return pl.pallas_call(
        body,
        out_shape=[jax.ShapeDtypeStruct((n_batch * seq, LANES), F32), jax.ShapeDtypeStruct((n_batch, 1, LANES), F32)],
        grid=(n_batch,), in_specs=[col, col, col, pl.BlockSpec((1, LANES), lambda b: (0, 0))],
        out_specs=[col, pl.BlockSpec((None, 1, LANES), lambda b: (b, 0, 0))],
        compiler_params=_params(), name="forget_bwd",
    )(dfq, dfk, f, bias)


def _mix_out(x1, yp, o, ona, woa, wob):
    t, d = x1.shape
    width = o.shape[1]
    tm = min(512, t)

    def body(x_ref, yp_ref, o_ref, on_ref, wa_ref, wb_ref, x2_ref, ya_ref):
        of = o_ref[...]
        ya = ((of * _rms(of)) * on_ref[...]).astype(BF16)
        ya_ref[...] = ya
        x2_ref[...] = x_ref[...] + (_dot(yp_ref[...], wa_ref[...]) + _dot(ya, wb_ref[...]))

    row = pl.BlockSpec((tm, d), lambda i: (i, 0))
    half = pl.BlockSpec((tm, width), lambda i: (i, 0))
    wspec = pl.BlockSpec((width, d), lambda i: (0, 0))
    return pl.pallas_call(
        body, out_shape=[jax.ShapeDtypeStruct((t, d), F32), jax.ShapeDtypeStruct((t, width), BF16)],
        grid=(t // tm,), in_specs=[row, half, half, pl.BlockSpec((1, width), lambda i: (0, 0)), wspec, wspec],
        out_specs=[row, half], compiler_params=_params(), name="mix_out",
    )(x1, yp, o, ona, woa, wob)


def _mix_out_bwd(dx2, o, yp, ya, ona, woa, wob, plan=None):
    t, d = dx2.shape
    width = o.shape[1]
    tm = min(512, t)
    nt = t // tm

    def body(dx_ref, o_ref, yp_ref, ya_ref, on_ref, wa_ref, wb_ref, dyp_ref, do_ref, dl_ref, dwa_ref, dwb_ref, don_ref):
        @pl.when(pl.program_id(0) == 0)
        def _():
            dwa_ref[...] = jnp.zeros_like(dwa_ref)
            dwb_ref[...] = jnp.zeros_like(dwb_ref)

        dxb = dx_ref[...].astype(BF16)
        dwa_ref[...] += _dot_tn(yp_ref[...], dxb)
        dwb_ref[...] += _dot_tn(ya_ref[...], dxb)
        dyp_ref[...] = _dot_nt(dxb, wa_ref[...])
        of = o_ref[...]
        dov, dgr = _rms_bwd(of, _rms(of), on_ref[...], _dot_nt(dxb, wb_ref[...]))
        don_ref[...] = jnp.sum(dgr, axis=0, keepdims=True)
        do_ref[...] = dov.astype(BF16)
        lo = _head_masks()
        prod = dov * of
        delta = jnp.zeros((tm, LANES), F32)
        for blk in range(width // LANES):
            pb = prod[:, blk * LANES:(blk + 1) * LANES]
            delta = delta + _put_lane(jnp.sum(jnp.where(lo, pb, 0.0), axis=1, keepdims=True), 2 * blk)
            delta = delta + _put_lane(jnp.sum(jnp.where(lo, 0.0, pb), axis=1, keepdims=True), 2 * blk + 1)
        for c in range(tm // ATT_SUB):
            dl_ref[c] = delta[c * ATT_SUB:(c + 1) * ATT_SUB, :].T[0:8, :]

    row = pl.BlockSpec((tm, d), lambda i: (i, 0))
    half = pl.BlockSpec((tm, width), lambda i: (i, 0))
    wspec = pl.BlockSpec((width, d), lambda i: (0, 0))
    return _pallas(
        body, name="mix_out_bwd", args=[dx2, o, yp, ya, ona, woa, wob],
        out_shape=[jax.ShapeDtypeStruct((t, width), F32), jax.ShapeDtypeStruct((t, width), BF16),
                   jax.ShapeDtypeStruct((t // ATT_SUB, 8, ATT_SUB), F32), jax.ShapeDtypeStruct((width, d), F32),
                   jax.ShapeDtypeStruct((width, d), F32), jax.ShapeDtypeStruct((nt, 1, width), F32)],
        grid=(nt,),
        in_specs=[row, half, half, half, pl.BlockSpec((1, width), lambda i: (0, 0)), wspec, wspec],
        out_specs=[half, half, pl.BlockSpec((tm // ATT_SUB, 8, ATT_SUB), lambda i: (i, 0, 0)), wspec, wspec,
                   pl.BlockSpec((None, 1, width), lambda i: (i, 0, 0))], plan=plan)


def _mix_in_bwd(dx2, x1, gain, hm, dpv, dqh, q, dkh, k, dv, df, qn, kn, wt):
    t, d = x1.shape
    width = q.shape[1]
    pool_width = dpv.shape[1]
    tm = min(512, t)
    nt = t // tm
    scale = HEAD_DIM ** -0.5
    c_q, c_k, c_v = pool_width, pool_width + width, pool_width + 2 * width
    c_f = c_v + width

    def body(dx2_ref, x_ref, g_ref, hm_ref, dpv_ref, dqh_ref, q_ref, dkh_ref, k_ref, dv_ref, df_ref, qn_ref, kn_ref,
             wt_ref, dx_ref, dwt_ref, dg_ref, dqn_ref, dkn_ref):
        @pl.when(pl.program_id(0) == 0)
        def _():
            dwt_ref[...] = jnp.zeros_like(dwt_ref)

        lo = _head_masks()
        hm = hm_ref[...]
        pieces = [(0, dpv_ref[...])]
        for c0, raw_ref, dh_ref, n_ref, dn_ref, mul in ((c_q, q_ref, dqh_ref, qn_ref, dqn_ref, scale),
                                                       (c_k, k_ref, dkh_ref, kn_ref, dkn_ref, 1.0)):
            cols = []
            for blk in range(width // LANES):
                sl = slice(blk * LANES, (blk + 1) * LANES)
                xb = raw_ref[:, sl]
                gb = dh_ref[:, sl] * mul
                r = _head_rms(xb, lo)
                xh = xb * r
                dyg = gb * n_ref[:, sl]
                cols.append((r * (dyg - xh * _head_mean(dyg * xh, lo))).astype(BF16))
                dn_ref[:, sl] = jnp.sum(gb * xh, axis=0, keepdims=True)
            pieces.append((c0, jnp.concatenate(cols, axis=1)))
        pieces.append((c_v, dv_ref[...]))
        pieces.append((c_f, df_ref[...].astype(BF16)))
        dhm = jnp.zeros((tm, d), F32)
        for c0, piece in pieces:
            dwt_ref[c0:c0 + piece.shape[1], :] += _dot_tn(piece, hm)
            dhm = dhm + _dot(piece, wt_ref[c0:c0 + piece.shape[1], :])
        xf = x_ref[...]
        dxn, dgr = _rms_bwd(xf, _rms(xf), g_ref[...], dhm)
        dx_ref[...] = dx2_ref[...] + dxn
        dg_ref[...] = jnp.sum(dgr, axis=0, keepdims=True)

    row = pl.BlockSpec((tm, d), lambda i: (i, 0))
    half = pl.BlockSpec((tm, width), lambda i: (i, 0))
    const = lambda shape: pl.BlockSpec(shape, lambda i: (0, 0))
    pvec = lambda n: pl.BlockSpec((None, 1, n), lambda i: (i, 0, 0))
    return pl.pallas_call(
        body,
        out_shape=[jax.ShapeDtypeStruct((t, d), F32), jax.ShapeDtypeStruct(wt.shape, F32),
                   jax.ShapeDtypeStruct((nt, 1, d), F32),
                   jax.ShapeDtypeStruct((nt, 1, width), F32), jax.ShapeDtypeStruct((nt, 1, width), F32)],
        grid=(nt,),
        in_specs=[row, row, const((1, d)), row, pl.BlockSpec((tm, pool_width), lambda i: (i, 0)), half, half, half, half,
                  half, pl.BlockSpec((tm, LANES), lambda i: (i, 0)), const((1, width)), const((1, width)),
                  const(wt.shape)],
        out_specs=[row, const(wt.shape), pvec(d), pvec(width), pvec(width)],
        compiler_params=_params(), name="mix_in_bwd",
    )(dx2, x1, gain, hm, dpv, dqh, q, dkh, k, dv, df, qn, kn, wt)


def _mesh_pos():
    return lax.axis_index("x"), lax.axis_index("y"), lax.axis_index("c")


def _other_chips(x, y):
    return [(1 - x, y), (x, 1 - y), (1 - x, 1 - y)]


def _remote(src, dst, send_sem, recv_sem, device):
    return pltpu.make_async_remote_copy(src_ref=src, dst_ref=dst, send_sem=send_sem, recv_sem=recv_sem,
                                        device_id=device, device_id_type=pl.DeviceIdType.MESH)


def _half_rows(n_rows, which):
    half = n_rows // 2
    return pl.ds(pl.multiple_of(which * half, 8), half)


def _row_block(rows, cols, itemsize=4):
    rb = rows
    while rb * cols * itemsize > (1 << 20) and rb % 32 == 0:
        rb //= 2
    return rb


def _place_cast(ws, chip, tag):
    n = len(ws)
    rows, cols = ws[0].shape
    rb = _row_block(rows, cols)

    def body(k_ref, *refs):
        for w_ref, o_ref in zip(refs[:n], refs[n:]):
            o_ref[...] = w_ref[...].astype(BF16)

    return pl.pallas_call(
        body, out_shape=[jax.ShapeDtypeStruct((N_CHIPS, rows, cols), BF16)] * n,
        grid_spec=pltpu.PrefetchScalarGridSpec(
            num_scalar_prefetch=1, grid=(rows // rb,),
            in_specs=[pl.BlockSpec((rb, cols), lambda i, k: (i, 0))] * n,
            out_specs=[pl.BlockSpec((None, rb, cols), lambda i, k: (k[0], i, 0))] * n),
        compiler_params=_params(), name="place_" + tag,
    )(chip, *ws)


class _Plan:
    def __init__(self, ins, outs, alias, sems, start, finish):
        self.ins, self.outs, self.alias, self.sems, self.start, self.finish = ins, outs, alias, sems, start, finish


def _merge_plans(a, b):
    ni, no, ns = len(a.ins), len(a.outs), len(a.sems)
    alias = dict(a.alias)
    alias.update({ni + i: no + o for i, o in b.alias.items()})

    def both(which):
        def run(ins, outs, sems):
            getattr(a, which)(ins[:ni], outs[:no], sems[:ns])
            getattr(b, which)(ins[ni:], outs[no:], sems[ns:])
        return run

    return _Plan(list(a.ins) + list(b.ins), list(a.outs) + list(b.outs), alias, list(a.sems) + list(b.sems),
                 both("start"), both("finish"))


def _run_plan(plan, name):
    n_in, n_out = len(plan.ins), len(plan.outs)

    def body(*refs):
        parts = refs[:n_in], refs[n_in:n_in + n_out], refs[n_in + n_out:]
        plan.start(*parts)
        plan.finish(*parts)

    return pl.pallas_call(
        body, out_shape=plan.outs, in_specs=[ANY] * n_in, out_specs=[ANY] * n_out, scratch_shapes=plan.sems,
        input_output_aliases=plan.alias, name=name,
    )(*plan.ins)


def _pallas(body, *, name, args, in_specs, out_shape, out_specs, grid, scratch_shapes=(), plan=None):
    n_in, n_out, n_scr = len(args), len(out_shape), len(scratch_shapes)
    if plan is None:
        res = pl.pallas_call(body, out_shape=out_shape, grid=grid, in_specs=in_specs, out_specs=out_specs,
                             scratch_shapes=scratch_shapes, compiler_params=_params(), name=name)(*args)
        return list(res), []
    p_in, p_out = len(plan.ins), len(plan.outs)

    def carrying(*refs):
        ins, p_ins = refs[:n_in], refs[n_in:n_in + p_in]
        o0 = n_in + p_in
        outs, p_outs = refs[o0:o0 + n_out], refs[o0 + n_out:o0 + n_out + p_out]
        s0 = o0 + n_out + p_out
        scr, p_sems = refs[s0:s0 + n_scr], refs[s0 + n_scr:]
        ids = [pl.program_id(a) for a in range(len(grid))]
        first = functools.reduce(jnp.logical_and, [i == 0 for i in ids])
        last = functools.reduce(jnp.logical_and, [i == g - 1 for i, g in zip(ids, grid)])

        @pl.when(first)
        def _():
            plan.start(p_ins, p_outs, p_sems)

        body(*ins, *outs, *scr)

        @pl.when(last)
        def _():
            plan.finish(p_ins, p_outs, p_sems)

    res = pl.pallas_call(
        carrying, out_shape=list(out_shape) + list(plan.outs), grid=grid,
        in_specs=list(in_specs) + [ANY] * p_in, out_specs=list(out_specs) + [ANY] * p_out,
        scratch_shapes=list(scratch_shapes) + list(plan.sems),
        input_output_aliases={n_in + i: n_out + o for i, o in plan.alias.items()},
        compiler_params=_params(), name=name,
    )(*args, *plan.ins)
    return list(res[:n_out]), list(res[n_out:])


def _plan_gather(stacks):
    n = len(stacks)

    def ici_copies(outs, sems):
        x, y, c = _mesh_pos()
        cps = []
        for w in range(n):
            own = outs[w].at[2 * x + y, _half_rows(stacks[w].shape[1], c)]
            cps += [_remote(own, own, sems[0].at[w, j], sems[1].at[w, j], (*chip, c)) for j, chip in enumerate(_other_chips(x, y))]
        return cps

    def start(ins, outs, sems):
        for cp in ici_copies(outs, sems):
            cp.start()

    def finish(ins, outs, sems):
        ici_send, ici_recv, d2d_send, d2d_recv = sems
        x, y, c = _mesh_pos()
        sibling = (x, y, 1 - c)
        slots = [2 * cx + cy for cx, cy in _other_chips(x, y)]
        forwards = []
        for w in range(n):
            rows = _half_rows(stacks[w].shape[1], c)
            for j in range(3):
                landed = outs[w].at[slots[j], rows]
                _remote(landed, landed, ici_send.at[w, j], ici_recv.at[w, j], sibling).wait_recv()
                cp = _remote(landed, landed, d2d_send.at[w, j], d2d_recv.at[w, j], sibling)
                cp.start()
                forwards.append(cp)
        for w in range(n):
            rows = _half_rows(stacks[w].shape[1], 1 - c)
            for j in range(3):
                landed = outs[w].at[slots[j], rows]
                _remote(landed, landed, d2d_send.at[w, j], d2d_recv.at[w, j], sibling).wait_recv()
        for cp in ici_copies(outs, sems) + forwards:
            cp.wait_send()

    return _Plan(stacks, [jax.ShapeDtypeStruct(s.shape, s.dtype) for s in stacks], {w: w for w in range(n)},
                 [pltpu.SemaphoreType.DMA((n, 3))] * 4, start, finish)


def _plan_sibling_halves(gs):
    n = len(gs)

    def copies(ins, outs, sems):
        x, y, c = _mesh_pos()
        return [_remote(ins[w].at[:, _half_rows(gs[w].shape[1], 1 - c), :], outs[w], sems[0].at[w], sems[1].at[w],
                        (x, y, 1 - c)) for w in range(n)]

    def start(ins, outs, sems):
        for cp in copies(ins, outs, sems):
            cp.start()

    def finish(ins, outs, sems):
        for cp in copies(ins, outs, sems):
            cp.wait()

    return _Plan(gs, [jax.ShapeDtypeStruct((g.shape[0], g.shape[1] // 2, g.shape[2]), g.dtype) for g in gs], {},
                 [pltpu.SemaphoreType.DMA((n,))] * 2, start, finish)


def _plan_chip_exchange(ps):
    n = len(ps)

    def copies(ins, outs, sems):
        x, y, c = _mesh_pos()
        return [_remote(ins[w].at[2 * cx + cy], outs[w].at[j], sems[0].at[w, j], sems[1].at[w, j], (cx, cy, c))
                for w in range(n) for j, (cx, cy) in enumerate(_other_chips(x, y))]

    def start(ins, outs, sems):
        for cp in copies(ins, outs, sems):
            cp.start()

    def finish(ins, outs, sems):
        for cp in copies(ins, outs, sems):
            cp.wait()

    return _Plan(ps, [jax.ShapeDtypeStruct((3,) + p.shape[1:], p.dtype) for p in ps], {},
                 [pltpu.SemaphoreType.DMA((n, 3))] * 2, start, finish)


def _plan_sibling_share(gs):
    n = len(gs)

    def copies(outs, sems, which):
        x, y, c = _mesh_pos()
        cps = []
        for w in range(n):
            rows = outs[w].at[_half_rows(gs[w].shape[0], c if which == "mine" else 1 - c)]
            cps.append(_remote(rows, rows, sems[0].at[w], sems[1].at[w], (x, y, 1 - c)))
        return cps

    def start(ins, outs, sems):
        for cp in copies(outs, sems, "mine"):
            cp.start()

    def finish(ins, outs, sems):
        for cp in copies(outs, sems, "mine"):
            cp.wait_send()
        for cp in copies(outs, sems, "theirs"):
            cp.wait_recv()

    return _Plan(gs, [jax.ShapeDtypeStruct(g.shape, g.dtype) for g in gs], {w: w for w in range(n)},
                 [pltpu.SemaphoreType.DMA((n,))] * 2, start, finish)


def _same_shape_groups(arrays):
    groups = {}
    for i, a in enumerate(arrays):
        groups.setdefault(a.shape, []).append(i)
    return list(groups.values())


def _add_sibling(gs, r1s, ids, tag):
    n = len(gs)
    nch, rh, cols = r1s[0].shape

    def body(ids_ref, *refs):
        for g_ref, r_ref, o_ref in zip(refs[:n], refs[n:2 * n], refs[2 * n:]):
            o_ref[...] = (g_ref[...] + r_ref[...]).astype(BF16)

    blk = lambda fn: pl.BlockSpec((None, rh, cols), fn)
    return pl.pallas_call(
        body, out_shape=[jax.ShapeDtypeStruct(r1s[0].shape, BF16)] * n,
        grid_spec=pltpu.PrefetchScalarGridSpec(
            num_scalar_prefetch=1, grid=(nch,),
            in_specs=[blk(lambda k, ids: (k, ids[1], 0))] * n + [blk(lambda k, ids: (k, 0, 0))] * n,
            out_specs=[blk(lambda k, ids: (k, 0, 0))] * n),
        compiler_params=_params(), name="add_sibling_" + tag,
    )(ids, *gs, *r1s)


def _add_chips(gs, r1s, r2s, ids, tag):
    n = len(gs)
    _, rh, cols = r1s[0].shape
    nb = 2 if rh % 32 == 0 else 1
    rb = rh // nb

    def body(ids_ref, *refs):
        for g_ref, r1_ref, r2_ref, o_ref in zip(refs[:n], refs[n:2 * n], refs[2 * n:3 * n], refs[3 * n:]):
            own = g_ref[...] + r1_ref[...]
            o_ref[...] = ((own + r2_ref[0].astype(F32)) + r2_ref[1].astype(F32)) + r2_ref[2].astype(F32)

    return pl.pallas_call(
        body, out_shape=[jax.ShapeDtypeStruct((2 * rh, cols), F32)] * n,
        grid_spec=pltpu.PrefetchScalarGridSpec(
            num_scalar_prefetch=1, grid=(nb,),
            in_specs=[pl.BlockSpec((None, rb, cols), lambda i, ids: (ids[0], ids[1] * nb + i, 0))] * n
            + [pl.BlockSpec((None, rb, cols), lambda i, ids: (ids[0], i, 0))] * n
            + [pl.BlockSpec((3, rb, cols), lambda i, ids: (0, i, 0))] * n,
            out_specs=[pl.BlockSpec((rb, cols), lambda i, ids: (ids[1] * nb + i, 0))] * n),
        compiler_params=_params(), name="add_chips_" + tag,
    )(ids, *gs, *r1s, *r2s)


VEC_ROWS = 8


N_DEVICES = 8


def _small_pack(part, d, width):
    names = ("ffn1_norm", "mix_norm", "ffn2_norm", "pool_scale", "out_norm_pool", "out_norm_attn", "qn", "kn", "b_forget",
             "pool_w", "loss")
    args = [part[k] for k in names]
    pw_shape = part["pool_w"].shape[1:]

    def body(g1_ref, gm_ref, g2_ref, ps_ref, onp_ref, ona_ref, qn_ref, kn_ref, bf_ref, pw_ref, loss_ref, vbuf, pbuf):
        lo = _head_masks()

        def fold_heads(ref):
            v = jnp.sum(ref[...], axis=0)
            acc = jnp.zeros((VEC_ROWS, LANES), F32)
            for blk in range(width // LANES):
                vb = jnp.broadcast_to(v[:, blk * LANES:(blk + 1) * LANES], (VEC_ROWS, LANES))
                acc = acc + vb + pltpu.roll(vb, HEAD_DIM, 1)
            return jnp.where(lo, acc, 0.0)[0:1, :]

        vbuf[0] = jnp.zeros((VEC_ROWS, d), F32)
        vbuf[0, 0:1, :] = jnp.sum(g1_ref[...], axis=0)
        vbuf[0, 1:2, :] = jnp.sum(gm_ref[...], axis=0)
        vbuf[0, 2:3, :] = jnp.sum(g2_ref[...], axis=0)
        vbuf[0, 5:6, 0:LANES] = jnp.sum(loss_ref[...], axis=0)[0:1, :]
        vbuf[0, 3:4, 0:width] = jnp.sum(ps_ref[...], axis=0)
        vbuf[0, 3:4, width:2 * width] = jnp.sum(onp_ref[...], axis=0)
        vbuf[0, 4:5, 0:width] = jnp.sum(ona_ref[...], axis=0)
        vbuf[0, 4:5, width:width + LANES] = fold_heads(qn_ref)
        vbuf[0, 4:5, width + LANES:width + 2 * LANES] = fold_heads(kn_ref)
        vbuf[0, 4:5, width + 2 * LANES:width + 3 * LANES] = jnp.sum(bf_ref[...], axis=0)
        pbuf[0] = jnp.sum(pw_ref[...], axis=0)

    return pl.pallas_call(
        body, out_shape=[jax.ShapeDtypeStruct((N_DEVICES, VEC_ROWS, d), F32), jax.ShapeDtypeStruct((N_DEVICES,) + pw_shape, F32)],
        in_specs=[VM] * len(args), out_specs=[VM, VM], compiler_params=_params(), name="small_pack",
    )(*args)


def _plan_all_to_all(stacks):
    n = len(stacks)

    def copies(outs, sems):
        x, y, c = _mesh_pos()
        cps = []
        for r in range(1, N_DEVICES):
            peer = (x if not r & 4 else 1 - x, y if not r & 2 else 1 - y, c if not r & 1 else 1 - c)
            cps += [_remote(outs[w].at[0], outs[w].at[r], sems[0].at[w, r - 1], sems[1].at[w, r - 1], peer) for w in range(n)]
        return cps

    def start(ins, outs, sems):
        for cp in copies(outs, sems):
            cp.start()

    def finish(ins, outs, sems):
        for cp in copies(outs, sems):
            cp.wait()

    return _Plan(stacks, [jax.ShapeDtypeStruct(s.shape, s.dtype) for s in stacks], {w: w for w in range(n)},
                 [pltpu.SemaphoreType.DMA((n, N_DEVICES - 1))] * 2, start, finish)


def _small_sum(vstack, pstack, me):
    def body(me_ref, vbuf, pbuf, vec_ref, pw_ref):
        vec = vbuf[me_ref[0]]
        pw = pbuf[me_ref[0]]
        for dev in range(1, N_DEVICES):
            vec = vec + vbuf[jnp.bitwise_xor(me_ref[0], dev)]
            pw = pw + pbuf[jnp.bitwise_xor(me_ref[0], dev)]
        vec_ref[...] = vec
        pw_ref[...] = pw

    full = lambda s: pl.BlockSpec(s.shape, lambda i, me: (0,) * len(s.shape))
    outs = [jax.ShapeDtypeStruct(vstack.shape[1:], F32), jax.ShapeDtypeStruct(pstack.shape[1:], F32)]
    return pl.pallas_call(
        body, out_shape=outs,
        grid_spec=pltpu.PrefetchScalarGridSpec(num_scalar_prefetch=1, grid=(1,), in_specs=[full(vstack), full(pstack)],
                                               out_specs=[full(o) for o in outs]),
        compiler_params=_params(), name="small_sum",
    )(me, vstack, pstack)


def _adamw(ws, gs, ms, vs, tag):
    n = len(ws)
    rows, cols = ws[0].shape
    rb = rows
    while rb * cols * 4 * n > (1 << 20) and rb % 16 == 0:
        rb //= 2

    def body(*refs):
        for j in range(n):
            w_ref, g_ref, m_ref, v_ref = (refs[k * n + j] for k in range(4))
            d_ref, mo_ref, vo_ref = (refs[(4 + k) * n + j] for k in range(3))
            gv = g_ref[...]
            m2 = ADAM_B1 * m_ref[...] + (1.0 - ADAM_B1) * gv
            v2 = ADAM_B2 * v_ref[...] + (1.0 - ADAM_B2) * (gv * gv)
            m_hat = m2 / (1.0 - ADAM_B1 ** ADAM_STEP)
            v_hat = v2 / (1.0 - ADAM_B2 ** ADAM_STEP)
            d_ref[...] = -ADAM_LR * (m_hat / (jnp.sqrt(v_hat) + ADAM_EPS) + ADAM_WD * w_ref[...])
            mo_ref[...] = m2
            vo_ref[...] = v2

    spec = pl.BlockSpec((rb, cols), lambda i: (i, 0))
    res = pl.pallas_call(
        body, out_shape=[jax.ShapeDtypeStruct(ws[0].shape, F32)] * (3 * n), grid=(rows // rb,),
        in_specs=[spec] * (4 * n), out_specs=[spec] * (3 * n), compiler_params=_params(), name="adamw_" + tag,
    )(*ws, *gs, *ms, *vs)
    return [(res[j], res[n + j], res[2 * n + j]) for j in range(n)]


def _pack_vec(p, d, width):
    pad = lambda v: jnp.pad(v, (0, LANES - v.shape[0]))
    row3 = jnp.concatenate([p["pool_scale"], p["out_norm_pool"]])
    row4 = jnp.concatenate([p["out_norm_attn"], pad(p["q_norm"]), pad(p["k_norm"]), pad(p["b_forget"]),
                            jnp.zeros((d - width - 3 * LANES,), F32)])
    rows = [p["ffn1_norm"], p["mix_norm"], p["ffn2_norm"], row3, row4]
    return jnp.pad(jnp.stack(rows), ((0, VEC_ROWS - len(rows)), (0, 0)))


def _unpack_vec(vec, width):
    return dict(ffn1_norm=vec[0], mix_norm=vec[1], ffn2_norm=vec[2], pool_scale=vec[3, :width],
                out_norm_pool=vec[3, width:2 * width], out_norm_attn=vec[4, :width],
                q_norm=vec[4, width:width + HEAD_DIM], k_norm=vec[4, width + LANES:width + LANES + HEAD_DIM],
                b_forget=vec[4, width + 2 * LANES:width + 2 * LANES + N_HEADS])


WEIGHT_NAMES = ("ffn1_norm", "ffn1_w_gate", "ffn1_w_up", "ffn1_w_down", "mix_norm", "w_in", "b_forget", "pool_w",
                "pool_scale", "q_norm", "k_norm", "out_norm_pool", "out_norm_attn", "w_out", "ffn2_norm",
                "ffn2_w_gate", "ffn2_w_up", "ffn2_w_down")
BIG_NAMES = ("ffn1_w_gate", "ffn1_w_up", "ffn1_w_down", "w_in", "w_out", "ffn2_w_gate", "ffn2_w_up", "ffn2_w_down")
TRANSPOSED_NAMES = ("ffn1_w_gate", "ffn1_w_up", "w_in", "ffn2_w_gate", "ffn2_w_up")
FFN1_NAMES = ("ffn1_w_gate", "ffn1_w_up", "ffn1_w_down")
MIX_NAMES = ("w_in", "w_out")
FFN2_NAMES = ("ffn2_w_gate", "ffn2_w_up", "ffn2_w_down")
REST_NAMES = MIX_NAMES + FFN2_NAMES


def kernel(x, ffn1_norm, ffn1_w_gate, ffn1_w_up, ffn1_w_down, mix_norm, w_in, b_forget, pool_w, pool_scale, q_norm, k_norm, out_norm_pool, out_norm_attn, w_out, ffn2_norm, ffn2_w_gate, ffn2_w_up, ffn2_w_down, loss_target, m_ffn1_norm, m_ffn1_w_gate, m_ffn1_w_up, m_ffn1_w_down, m_mix_norm, m_w_in, m_b_forget, m_pool_w, m_pool_scale, m_q_norm, m_k_norm, m_out_norm_pool, m_out_norm_attn, m_w_out, m_ffn2_norm, m_ffn2_w_gate, m_ffn2_w_up, m_ffn2_w_down, v_ffn1_norm, v_ffn1_w_gate, v_ffn1_w_up, v_ffn1_w_down, v_mix_norm, v_w_in, v_b_forget, v_pool_w, v_pool_scale, v_q_norm, v_k_norm, v_out_norm_pool, v_out_norm_attn, v_w_out, v_ffn2_norm, v_ffn2_w_gate, v_ffn2_w_up, v_ffn2_w_down):
    given = dict(locals())
    w = {n: given[n] for n in WEIGHT_NAMES}
    m = {n: given["m_" + n] for n in WEIGHT_NAMES}
    v = {n: given["v_" + n] for n in WEIGHT_NAMES}
    n_batch, seq, d = x.shape
    width = pool_scale.shape[0]
    in_rows = w_in.shape[1]
    in_cols = N_CHIPS * in_rows
    in_pad = -(-in_rows // 32) * 32
    in_cols_pad = in_cols - N_HEADS + LANES

    work = lambda a, n: a.T if n in TRANSPOSED_NAMES else a
    exchanged = lambda a, n: jnp.pad(a, ((0, in_pad - in_rows), (0, 0))) if n == "w_in" else a

    mesh_x, mesh_y, mesh_c = _mesh_pos()
    ids = jnp.stack([2 * mesh_x + mesh_y, mesh_c]).astype(jnp.int32)

    row = lambda a: a.reshape(1, -1)
    g1, gm, g2, ps, onp, ona = (row(a) for a in (ffn1_norm, mix_norm, ffn2_norm, pool_scale, out_norm_pool, out_norm_attn))
    qn, kn = row(jnp.tile(q_norm, N_HEADS)), row(jnp.tile(k_norm, N_HEADS))
    bf = row(jnp.pad(b_forget, (0, LANES - N_HEADS)))
    pwb = pool_w.astype(BF16)
    xf, tgt = x.reshape(n_batch * seq, d), loss_target.reshape(n_batch * seq, d)

    def grouped(call, names, *lists):
        out = [None] * len(names)
        for idx in _same_shape_groups(lists[0]):
            res = call(*[[lst[i] for i in idx] for lst in lists], names[idx[0]])
            for i, r in zip(idx, res):
                out[i] = r
        return out

    placed = dict(zip(BIG_NAMES, grouped(lambda ws, tag: _place_cast(ws, ids, tag), BIG_NAMES,
                                         [exchanged(work(w[n], n), n) for n in BIG_NAMES])))
    wg1, wu1, wd1 = _run_plan(_plan_gather([placed[n] for n in FFN1_NAMES]), "gather_ffn1")
    (x1, h1, a1, b1, s1), rest = _ffn_fwd(xf, g1, wg1, wu1, wd1, plan=_plan_gather([placed[n] for n in REST_NAMES]))
    w_in_all, w_out_all, wg2, wu2, wd2 = rest
    w_in_t = jnp.pad(w_in_all[:, :in_rows].reshape(in_cols, d), ((0, in_cols_pad - in_cols), (0, 0)))
    w_out_full = w_out_all.reshape(N_CHIPS * w_out.shape[0], d)
    woa, wob = w_out_full[:width], w_out_full[width:]

    hm, pv, q, k, qh, kh, vb, f = _mix_proj(x1, gm, w_in_t, qn, kn, width, width)
    qa, ka = _forget_prefix(f, bf, qh, kh, n_batch, seq)
    yp = _pool_fwd(pv, pwb, ps, onp, n_batch, seq)
    o, lse = _attn_fwd(qa, ka, vb, n_batch, seq)
    x2, ya = _mix_out(x1, yp, o, ona, woa, wob)
    (dy, h2, a2, b2, s2, lpart), _ = _ffn_fwd(x2, g2, wg2, wu2, wd2, target=tgt)

    def to_chips(gs, arrived, tags):
        return grouped(lambda g, r, tag: _add_sibling(g, r, ids, tag), tags, gs, arrived)

    def own_rows(gs, from_sibling, from_chips, tags):
        return grouped(lambda g, ra, rb, tag: _add_chips(g, ra, rb, ids, tag), tags, gs, from_sibling, from_chips)

    (dx2, da2, db2, dg2), _ = _ffn_bwd_x(dy, x2, g2, a2, b2, wg2, wu2, wd2, "ffn2_bwd_x")
    dw2, _ = _ffn_bwd_w(h2, s2, da2, db2, dy, "ffn2_bwd_w")
    (dyp, do, delta, dwoa, dwob, dona), sib2 = _mix_out_bwd(dx2, o, yp, ya, ona, woa, wob, plan=_plan_sibling_halves(dw2))
    dpv, dpw, dps, donp = _pool_bwd(pv, dyp, pwb, ps, onp, n_batch, seq)
    (dqh, dfq), chips2 = _attn_bwd_q(qa, ka, vb, do, lse, delta, n_batch, seq,
                                     plan=_plan_chip_exchange(to_chips(dw2, sib2, FFN2_NAMES)))
    (dkh, dv, dfk), red2 = _attn_bwd_kv(qa, ka, vb, do, lse, delta, n_batch, seq,
                                        plan=_plan_sibling_share(own_rows(dw2, sib2, chips2, FFN2_NAMES)))
    df, dbf = _forget_bwd(dfq, dfk, f, bf, n_batch, seq)
    dx1, dw_in_t, dgm, dqn, dkn = _mix_in_bwd(dx2, x1, gm, hm, dpv, dqh, q, dkh, k, dv, df, qn, kn, w_in_t)
    in_base = [in_rows * k // 8 * 8 for k in range(N_CHIPS)]
    d_w_in = jnp.stack([dw_in_t[b:b + in_pad] for b in in_base])
    d_w_out = jnp.concatenate([dwoa, dwob], axis=0).reshape(N_CHIPS, w_out.shape[0], d)
    dwm = [d_w_in, d_w_out]
    (gx, da1, db1, dg1), sibm = _ffn_bwd_x(dx1, xf, g1, a1, b1, wg1, wu1, wd1, "ffn1_bwd_x", plan=_plan_sibling_halves(dwm))

    part = dict(ffn1_norm=dg1, mix_norm=dgm, ffn2_norm=dg2, b_forget=dbf, pool_scale=dps, out_norm_pool=donp,
                out_norm_attn=dona, qn=dqn, kn=dkn, pool_w=dpw.reshape(n_batch, -1, pool_w.shape[-1]), loss=lpart)
    dw1, arrived = _ffn_bwd_w(h1, s1, da1, db1, dx1, "ffn1_bwd_w",
                              plan=_merge_plans(_plan_chip_exchange(to_chips(dwm, sibm, MIX_NAMES)),
                                                _plan_all_to_all(_small_pack(part, d, width))))
    chipsm, (vstack, pstack) = arrived[:2], arrived[2:]
    g_vec, g_pw = _small_sum(vstack, pstack, jnp.reshape(4 * mesh_x + 2 * mesh_y + mesh_c, (1,)).astype(jnp.int32))
    loss = g_vec[5, 0]
    sib1 = _run_plan(_plan_sibling_halves(dw1), "sibling_halves")
    chips1 = _run_plan(_plan_chip_exchange(to_chips(dw1, sib1, FFN1_NAMES)), "chip_exchange")
    last = _run_plan(_plan_sibling_share(own_rows(dw1, sib1, chips1, FFN1_NAMES) + own_rows(dwm, sibm, chipsm, MIX_NAMES)),
                     "sibling_share")
    reduced = dict(zip(FFN1_NAMES + MIX_NAMES + FFN2_NAMES, list(last) + list(red2)))
    reduced["w_in"] = lax.dynamic_slice(reduced["w_in"], ((in_rows * ids[0]) % 8, 0), (in_rows, d))

    grads, delta, new_m, new_v = {}, {}, {}, {}
    for names in (FFN2_NAMES, FFN1_NAMES, ("w_in",), ("w_out",)):
        stepped = _adamw([work(w[n], n) for n in names], [reduced[n] for n in names], [work(m[n], n) for n in names],
                         [work(v[n], n) for n in names], names[0])
        for n, step in zip(names, stepped):
            grads[n], delta[n], new_m[n], new_v[n] = (work(a, n) for a in (reduced[n], *step))
    flat_pw = lambda a: a.reshape(-1, a.shape[-1])
    (d_pw, m_pw, v_pw), = _adamw([flat_pw(pool_w)], [g_pw], [flat_pw(m_pool_w)], [flat_pw(v_pool_w)], "pool_w")
    (d_vec, m_vec, v_vec), = _adamw([_pack_vec(w, d, width)], [g_vec], [_pack_vec(m, d, width)], [_pack_vec(v, d, width)],
                                    "vectors")
    grads.update(_unpack_vec(g_vec, width), pool_w=g_pw.reshape(pool_w.shape))
    delta.update(_unpack_vec(d_vec, width), pool_w=d_pw.reshape(pool_w.shape))
    new_m.update(_unpack_vec(m_vec, width), pool_w=m_pw.reshape(pool_w.shape))
    new_v.update(_unpack_vec(v_vec, width), pool_w=v_pw.reshape(pool_w.shape))
    return (loss, gx.reshape(x.shape), *[grads[n] for n in WEIGHT_NAMES], *[delta[n] for n in WEIGHT_NAMES],
            *[new_m[n] for n in WEIGHT_NAMES], *[new_v[n] for n in WEIGHT_NAMES])
```

```python
import functools

import jax
import jax.numpy as jnp
from jax import lax
from jax.experimental import pallas as pl
from jax.experimental.pallas import tpu as pltpu

F32 = jnp.float32
BF16 = jnp.bfloat16
EPS = 1e-6
NEG = -1e30
ADAM_LR = 0.001
ADAM_B1 = 0.9
ADAM_B2 = 0.999
ADAM_EPS = 1e-08
ADAM_WD = 0.01
ADAM_STEP = 10
POOL_WINDOWS = (2, 4, 8, 16)
HEAD_DIM = 64
N_HEADS = 8
LANES = 128
N_CHIPS = 4
ATT_BLOCK = 512
ATT_SUB = 128
VMEM_LIMIT = 56 * 1024 * 1024
MESH_AXES = ("x", "y", "c")
ANY = pl.BlockSpec(memory_space=pl.ANY)
VM = pl.BlockSpec(memory_space=pltpu.VMEM)


def _params(**kw):
    return pltpu.CompilerParams(vmem_limit_bytes=VMEM_LIMIT, **kw)


def _dot(a, b):
    return jnp.dot(a, b, preferred_element_type=F32)


def _dot_nt(a, b):
    return lax.dot_general(a, b, (((1,), (1,)), ((), ())), preferred_element_type=F32)


def _dot_tn(a, b):
    return lax.dot_general(a, b, (((0,), (0,)), ((), ())), preferred_element_type=F32)


def _sigmoid(z):
    return 1.0 / (1.0 + jnp.exp(-z))


def _rms(xf):
    return lax.rsqrt(jnp.mean(xf * xf, axis=-1, keepdims=True) + EPS)


def _rms_bwd(xf, r, gain, dh):
    xh = xf * r
    dyg = dh * gain
    return r * (dyg - xh * jnp.mean(dyg * xh, axis=-1, keepdims=True)), dh * xh


def _total(v):
    return jnp.sum(jnp.sum(v, axis=1, keepdims=True), axis=0, keepdims=True)


def _ffn_fwd(x, gain, wg, wu, wd, target=None, plan=None):
    t, d = x.shape
    nch, fc, _ = wg.shape
    tm = min(512, t)
    nt = t // tm
    with_loss = target is not None

    def body(*refs):
        if with_loss:
            x_ref, g_ref, wg_ref, wu_ref, wd_ref, t_ref, o_ref, h_ref, a_ref, b_ref, s_ref, l_ref, acc_ref = refs
        else:
            x_ref, g_ref, wg_ref, wu_ref, wd_ref, o_ref, h_ref, a_ref, b_ref, s_ref, acc_ref = refs
        k = pl.program_id(1)

        @pl.when(k == 0)
        def _():
            xf = x_ref[...]
            h_ref[...] = ((xf * _rms(xf)) * g_ref[...]).astype(BF16)
            acc_ref[...] = jnp.zeros_like(acc_ref)

        h = h_ref[...]
        a = _dot_nt(h, wg_ref[...])
        b = _dot_nt(h, wu_ref[...])
        sb = ((a * (0.5 * jnp.tanh(0.5 * a) + 0.5)) * b).astype(BF16)
        a_ref[...] = a.astype(BF16)
        b_ref[...] = b.astype(BF16)
        s_ref[...] = sb
        acc_ref[...] += _dot(sb, wd_ref[...])

        @pl.when(k == nch - 1)
        def _():
            y = x_ref[...] + 0.5 * acc_ref[...]
            if with_loss:
                e = y - t_ref[...]
                o_ref[...] = e * (1.0 / d)
                l_ref[...] = jnp.broadcast_to(_total(e * e) * (0.5 / d), l_ref.shape)
            else:
                o_ref[...] = y

    row = pl.BlockSpec((tm, d), lambda i, k: (i, 0))
    chunk = pl.BlockSpec((None, fc, d), lambda i, k: (k, 0, 0))
    act = pl.BlockSpec((None, tm, fc), lambda i, k: (k, i, 0))
    in_specs = [row, pl.BlockSpec((1, d), lambda i, k: (0, 0)), chunk, chunk, chunk]
    out_shape = [jax.ShapeDtypeStruct((t, d), F32), jax.ShapeDtypeStruct((t, d), BF16)]
    out_shape += [jax.ShapeDtypeStruct((nch, t, fc), BF16)] * 3
    out_specs = [row, row, act, act, act]
    args = [x, gain, wg, wu, wd]
    if with_loss:
        in_specs.append(row)
        args.append(target)
        out_shape.append(jax.ShapeDtypeStruct((nt, 8, LANES), F32))
        out_specs.append(pl.BlockSpec((None, 8, LANES), lambda i, k: (i, 0, 0)))
    return _pallas(body, name="ffn_fwd_loss" if with_loss else "ffn_fwd", args=args, in_specs=in_specs,
                   out_shape=out_shape, out_specs=out_specs, grid=(nt, nch),
                   scratch_shapes=[pltpu.VMEM((tm, d), F32)], plan=plan)


def _swiglu_grads(dy_ref, a_ref, b_ref, wd_ref):
    ds = _dot_nt(dy_ref[...].astype(BF16), wd_ref[...])
    av = a_ref[...].astype(F32)
    bv = b_ref[...].astype(F32)
    th = jnp.tanh(0.5 * av)
    half_sig = 0.25 * th + 0.25
    dab = ((ds * bv) * (half_sig * (1.0 + av * (0.5 - 0.5 * th)))).astype(BF16)
    return dab, (ds * (av * half_sig)).astype(BF16)


def _ffn_bwd_a(dy, a, b, wd, name, plan=None):
    t, d = dy.shape
    nch, fc, _ = wd.shape
    tm = min(512, t)

    def body(dy_ref, a_ref, b_ref, wd_ref, da_ref, db_ref):
        da_ref[...], db_ref[...] = _swiglu_grads(dy_ref, a_ref, b_ref, wd_ref)

    act = pl.BlockSpec((None, tm, fc), lambda i, k: (k, i, 0))
    return _pallas(
        body, name=name, args=[dy, a, b, wd], out_shape=[jax.ShapeDtypeStruct((nch, t, fc), BF16)] * 2, grid=(t // tm, nch),
        in_specs=[pl.BlockSpec((tm, d), lambda i, k: (i, 0)), act, act, pl.BlockSpec((None, fc, d), lambda i, k: (k, 0, 0))],
        out_specs=[act, act], plan=plan)


def _ffn_bwd_h(dy, x, gain, da, db, wg, wu, name, tiles, prev=None, plan=None):
    t, d = x.shape
    nch, fc, _ = wg.shape
    tm = min(512, t)
    nt = t // tm
    t0, t1 = tiles

    def body(*refs):
        dy_ref, x_ref, g_ref, da_ref, db_ref, wg_ref, wu_ref = refs[:7]
        dx_ref, dg_ref, acc_ref = refs[-3:]
        k = pl.program_id(1)

        @pl.when(k == 0)
        def _():
            acc_ref[...] = jnp.zeros_like(acc_ref)

        acc_ref[...] += _dot(da_ref[...], wg_ref[...]) + _dot(db_ref[...], wu_ref[...])

        @pl.when(k == nch - 1)
        def _():
            xf = x_ref[...]
            dxn, dgr = _rms_bwd(xf, _rms(xf), g_ref[...], acc_ref[...])
            dx_ref[...] = dy_ref[...] + dxn
            dg_ref[...] = jnp.sum(dgr, axis=0, keepdims=True)

    row = pl.BlockSpec((tm, d), lambda i, k: (i + t0, 0))
    chunk = pl.BlockSpec((None, fc, d), lambda i, k: (k, 0, 0))
    act = pl.BlockSpec((None, tm, fc), lambda i, k: (k, i + t0, 0))
    args = [dy, x, gain, da, db, wg, wu]
    in_specs = [row, row, pl.BlockSpec((1, d), lambda i, k: (0, 0)), act, act, chunk, chunk]
    aliases = {}
    if prev is not None:
        aliases = {len(args): 0, len(args) + 1: 1}
        args += list(prev)
        in_specs += [ANY, ANY]
    return _pallas(
        body, name=name, args=args, out_shape=[jax.ShapeDtypeStruct((t, d), F32), jax.ShapeDtypeStruct((nt, 1, d), F32)],
        grid=(t1 - t0, nch), in_specs=in_specs,
        out_specs=[row, pl.BlockSpec((None, 1, d), lambda i, k: (i + t0, 0, 0))],
        scratch_shapes=[pltpu.VMEM((tm, d), F32)], plan=plan, aliases=aliases)


def _ffn_bwd_x(dy, x, gain, a, b, wg, wu, wd, name, plan=None):
    t, d = x.shape
    nch, fc, _ = wg.shape
    tm = min(512, t)
    nt = t // tm

    def body(dy_ref, x_ref, g_ref, a_ref, b_ref, wg_ref, wu_ref, wd_ref, dx_ref, da_ref, db_ref, dg_ref, acc_ref):
        k = pl.program_id(1)

        @pl.when(k == 0)
        def _():
            acc_ref[...] = jnp.zeros_like(acc_ref)

        dab, dbb = _swiglu_grads(dy_ref, a_ref, b_ref, wd_ref)
        da_ref[...] = dab
        db_ref[...] = dbb
        acc_ref[...] += _dot(dab, wg_ref[...]) + _dot(dbb, wu_ref[...])

        @pl.when(k == nch - 1)
        def _():
            xf = x_ref[...]
            dxn, dgr = _rms_bwd(xf, _rms(xf), g_ref[...], acc_ref[...])
            dx_ref[...] = dy_ref[...] + dxn
            dg_ref[...] = jnp.sum(dgr, axis=0, keepdims=True)

    row = pl.BlockSpec((tm, d), lambda i, k: (i, 0))
    chunk = pl.BlockSpec((None, fc, d), lambda i, k: (k, 0, 0))
    act = pl.BlockSpec((None, tm, fc), lambda i, k: (k, i, 0))
    return _pallas(
        body, name=name, args=[dy, x, gain, a, b, wg, wu, wd],
        out_shape=[jax.ShapeDtypeStruct((t, d), F32), jax.ShapeDtypeStruct((nch, t, fc), BF16),
                   jax.ShapeDtypeStruct((nch, t, fc), BF16), jax.ShapeDtypeStruct((nt, 1, d), F32)],
        grid=(nt, nch),
        in_specs=[row, row, pl.BlockSpec((1, d), lambda i, k: (0, 0)), act, act, chunk, chunk, chunk],
        out_specs=[row, act, act, pl.BlockSpec((None, 1, d), lambda i, k: (i, 0, 0))],
        scratch_shapes=[pltpu.VMEM((tm, d), F32)], plan=plan)


def _ffn_bwd_w(pairs, name, plan=None):
    n = len(pairs)
    nch, t, fc = pairs[0][0].shape
    d = pairs[0][1].shape[1]
    tm = min(1024, t)

    def body(*refs):
        @pl.when(pl.program_id(1) == 0)
        def _():
            for o_ref in refs[2 * n:]:
                o_ref[...] = jnp.zeros_like(o_ref)

        for j, (_, _, scale) in enumerate(pairs):
            other = refs[n + j][...]
            if other.dtype != BF16:
                other = (scale * other).astype(BF16)
            refs[2 * n + j][...] += _dot_tn(refs[j][...], other)

    row = pl.BlockSpec((tm, d), lambda k, i: (i, 0))
    act = pl.BlockSpec((None, tm, fc), lambda k, i: (k, i, 0))
    chunk = pl.BlockSpec((None, fc, d), lambda k, i: (k, 0, 0))
    return _pallas(body, name=name, args=[p[0] for p in pairs] + [p[1] for p in pairs],
                   out_shape=[jax.ShapeDtypeStruct((nch, fc, d), F32)] * n, grid=(nch, t // tm),
                   in_specs=[act] * n + [row] * n, out_specs=[chunk] * n, plan=plan)


def _head_masks():
    lane = lax.broadcasted_iota(jnp.int32, (1, LANES), 1)
    return lane < HEAD_DIM


def _head_rms(x, lo):
    x2 = x * x
    s0 = jnp.sum(jnp.where(lo, x2, 0.0), axis=1, keepdims=True)
    s1 = jnp.sum(jnp.where(lo, 0.0, x2), axis=1, keepdims=True)
    return jnp.where(lo, lax.rsqrt(s0 * (1.0 / HEAD_DIM) + EPS), lax.rsqrt(s1 * (1.0 / HEAD_DIM) + EPS))


def _head_mean(v, lo):
    s0 = jnp.sum(jnp.where(lo, v, 0.0), axis=1, keepdims=True)
    s1 = jnp.sum(jnp.where(lo, 0.0, v), axis=1, keepdims=True)
    return jnp.where(lo, s0, s1) * (1.0 / HEAD_DIM)


def _mix_proj(x1, gain, wt, qn, kn, pool_width, attn_width):
    t, d = x1.shape
    tm = min(512, t)
    nt = t // tm
    scale = HEAD_DIM ** -0.5
    c_q, c_k, c_v = pool_width, pool_width + attn_width, pool_width + 2 * attn_width
    c_f = c_v + attn_width

    def body(x_ref, g_ref, wt_ref, qn_ref, kn_ref, hm_ref, pv_ref, q_ref, k_ref, qh_ref, kh_ref, vb_ref, f_ref):
        xf = x_ref[...]
        hm = ((xf * _rms(xf)) * g_ref[...]).astype(BF16)
        hm_ref[...] = hm
        f_ref[...] = _dot_nt(hm, wt_ref[c_f:c_f + LANES, :])
        pv_ref[...] = _dot_nt(hm, wt_ref[0:pool_width, :])
        vb_ref[...] = _dot_nt(hm, wt_ref[c_v:c_v + attn_width, :]).astype(BF16)
        lo = _head_masks()
        for c0, raw_ref, hat_ref, n_ref, mul in ((c_q, q_ref, qh_ref, qn_ref, scale), (c_k, k_ref, kh_ref, kn_ref, 1.0)):
            raw = _dot_nt(hm, wt_ref[c0:c0 + attn_width, :])
            raw_ref[...] = raw
            for blk in range(attn_width // LANES):
                sl = slice(blk * LANES, (blk + 1) * LANES)
                xb = raw[:, sl]
                hat_ref[:, sl] = (((xb * _head_rms(xb, lo)) * n_ref[:, sl]) * mul).astype(BF16)

    row = pl.BlockSpec((tm, d), lambda i: (i, 0))
    half = pl.BlockSpec((tm, attn_width), lambda i: (i, 0))
    const = lambda shape: pl.BlockSpec(shape, lambda i: (0, 0))
    return pl.pallas_call(
        body,
        out_shape=[jax.ShapeDtypeStruct((t, d), BF16), jax.ShapeDtypeStruct((t, pool_width), F32),
                   jax.ShapeDtypeStruct((t, attn_width), F32), jax.ShapeDtypeStruct((t, attn_width), F32),
                   jax.ShapeDtypeStruct((t, attn_width), BF16), jax.ShapeDtypeStruct((t, attn_width), BF16),
                   jax.ShapeDtypeStruct((t, attn_width), BF16), jax.ShapeDtypeStruct((t, LANES), F32)],
        grid=(nt,),
        in_specs=[row, const((1, d)), const(wt.shape), const((1, attn_width)), const((1, attn_width))],
        out_specs=[row, pl.BlockSpec((tm, pool_width), lambda i: (i, 0)), half, half, half, half, half,
                   pl.BlockSpec((tm, LANES), lambda i: (i, 0))],
        compiler_params=_params(), name="mix_proj",
    )(x1, gain, wt, qn, kn)


def _shift_down(v, dist, row):
    return jnp.where(row >= dist, pltpu.roll(v, dist, 0), 0.0)


def _shift_up(v, dist, row, n):
    return jnp.where(row + dist < n, pltpu.roll(v, n - dist, 0), 0.0)


def _aug_lane(e):
    return HEAD_DIM if e == 0 else 0


def _forget_prefix(f, bias, qh, kh, n_batch, seq):
    def body(f_ref, b_ref, q_ref, k_ref, qa_ref, ka_ref):
        z = f_ref[...] + b_ref[...]
        acc = jnp.minimum(z, 0.0) - jnp.log(1.0 + jnp.exp(-jnp.abs(z)))
        row = lax.broadcasted_iota(jnp.int32, (seq, 1), 0)
        dist = 1
        while dist < seq:
            acc = acc + _shift_down(acc, dist, row)
            dist *= 2
        lane = lax.broadcasted_iota(jnp.int32, (1, LANES), 1)
        for h in range(N_HEADS):
            pair, e = divmod(h, 2)
            a0 = _aug_lane(e)
            own = (lane < HEAD_DIM) if e == 0 else (lane >= HEAD_DIM)
            fh = _pick_lane(acc, h)
            hi = fh.astype(BF16).astype(F32)
            rest = fh - hi
            mid = rest.astype(BF16).astype(F32)
            low = rest - mid
            q_ones = (lane >= a0 + 3) & (lane < a0 + 6)
            k_ones = (lane >= a0) & (lane < a0 + 3)
            q_aug = jnp.where(lane == a0, hi, jnp.where(lane == a0 + 1, mid, jnp.where(lane == a0 + 2, low,
                              jnp.where(q_ones, 1.0, 0.0))))
            k_aug = jnp.where(k_ones, 1.0, jnp.where(lane == a0 + 3, -hi, jnp.where(lane == a0 + 4, -mid,
                              jnp.where(lane == a0 + 5, -low, 0.0))))
            src = slice(pair * LANES, (pair + 1) * LANES)
            dst = slice(h * LANES, (h + 1) * LANES)
            qa_ref[:, dst] = jnp.where(own, q_ref[:, src].astype(F32), q_aug).astype(BF16)
            ka_ref[:, dst] = jnp.where(own, k_ref[:, src].astype(F32), k_aug).astype(BF16)

    width = qh.shape[1]
    tok = pl.BlockSpec((seq, width), lambda b: (b, 0))
    aug = pl.BlockSpec((seq, N_HEADS * LANES), lambda b: (b, 0))
    return pl.pallas_call(
        body, out_shape=[jax.ShapeDtypeStruct((n_batch * seq, N_HEADS * LANES), BF16)] * 2, grid=(n_batch,),
        in_specs=[pl.BlockSpec((seq, LANES), lambda b: (b, 0)), pl.BlockSpec((1, LANES), lambda b: (0, 0)), tok, tok],
        out_specs=[aug, aug], compiler_params=_params(), name="forget_prefix",
    )(f, bias, qh, kh)


def _pool_groups(pv_ref, pw_ref, ps_ref, seq):
    row = lax.broadcasted_iota(jnp.int32, (seq, 1), 0)
    pos = (row + 1).astype(F32)
    out = []
    for g, win in enumerate(POOL_WINDOWS):
        sl = slice(g * LANES, (g + 1) * LANES)
        xg = pv_ref[:, sl]
        acc = xg
        dist = 1
        while dist < win:
            acc = acc + _shift_down(acc, dist, row)
            dist *= 2
        pooled = (acc / jnp.minimum(pos, float(win)) - xg).astype(BF16)
        mixed = _dot(pooled, pw_ref[g])
        out.append((pooled, mixed, mixed * ps_ref[:, sl]))
    return out


def _pool_fwd(pv, pw, ps, onp, n_batch, seq):
    width = pv.shape[1]

    def body(pv_ref, pw_ref, ps_ref, on_ref, y_ref):
        groups = _pool_groups(pv_ref, pw_ref, ps_ref, seq)
        ssq = sum(jnp.sum(ms * ms, axis=1, keepdims=True) for _, _, ms in groups)
        r = lax.rsqrt(ssq * (1.0 / width) + EPS)
        for g, (_, _, ms) in enumerate(groups):
            sl = slice(g * LANES, (g + 1) * LANES)
            y_ref[:, sl] = ((ms * r) * on_ref[:, sl]).astype(BF16)

    return pl.pallas_call(
        body, out_shape=jax.ShapeDtypeStruct((n_batch * seq, width), BF16), grid=(n_batch,),
        in_specs=[pl.BlockSpec((seq, width), lambda b: (b, 0)), pl.BlockSpec(pw.shape, lambda b: (0, 0, 0)),
                  pl.BlockSpec((1, width), lambda b: (0, 0)), pl.BlockSpec((1, width), lambda b: (0, 0))],
        out_specs=pl.BlockSpec((seq, width), lambda b: (b, 0)),
        compiler_params=_params(), name="pool_fwd",
    )(pv, pw, ps, onp)


def _pool_bwd(pv, dyp, pw, ps, onp, n_batch, seq):
    width = pv.shape[1]

    def body(pv_ref, dy_ref, pw_ref, ps_ref, on_ref, dpv_ref, dpw_ref, dps_ref, don_ref):
        groups = _pool_groups(pv_ref, pw_ref, ps_ref, seq)
        ssq = sum(jnp.sum(ms * ms, axis=1, keepdims=True) for _, _, ms in groups)
        r = lax.rsqrt(ssq * (1.0 / width) + EPS)
        mean = sum(jnp.sum((dy_ref[:, g * LANES:(g + 1) * LANES] * on_ref[:, g * LANES:(g + 1) * LANES]) * (ms * r),
                           axis=1, keepdims=True) for g, (_, _, ms) in enumerate(groups)) * (1.0 / width)
        row = lax.broadcasted_iota(jnp.int32, (seq, 1), 0)
        pos = (row + 1).astype(F32)
        for g, (pooled, mixed, ms) in enumerate(groups):
            sl = slice(g * LANES, (g + 1) * LANES)
            dy = dy_ref[:, sl]
            xh = ms * r
            don_ref[:, sl] = jnp.sum(dy * xh, axis=0, keepdims=True)
            dms = r * (dy * on_ref[:, sl] - xh * mean)
            dps_ref[:, sl] = jnp.sum(dms * mixed, axis=0, keepdims=True)
            dmix = (dms * ps_ref[:, sl]).astype(BF16)
            dpw_ref[g] = _dot_tn(pooled, dmix)
            dpool = _dot_nt(dmix, pw_ref[g])
            win = POOL_WINDOWS[g]
            acc = dpool / jnp.minimum(pos, float(win))
            dist = 1
            while dist < win:
                acc = acc + _shift_up(acc, dist, row, seq)
                dist *= 2
            dpv_ref[:, sl] = (acc - dpool).astype(BF16)

    tok = pl.BlockSpec((seq, width), lambda b: (b, 0))
    vec = pl.BlockSpec((1, width), lambda b: (0, 0))
    pvec = pl.BlockSpec((None, 1, width), lambda b: (b, 0, 0))
    return pl.pallas_call(
        body,
        out_shape=[jax.ShapeDtypeStruct((n_batch * seq, width), BF16),
                   jax.ShapeDtypeStruct((n_batch,) + pw.shape, F32),
                   jax.ShapeDtypeStruct((n_batch, 1, width), F32), jax.ShapeDtypeStruct((n_batch, 1, width), F32)],
        grid=(n_batch,),
        in_specs=[tok, tok, pl.BlockSpec(pw.shape, lambda b: (0, 0, 0)), vec, vec],
        out_specs=[tok, pl.BlockSpec((None,) + pw.shape, lambda b: (b, 0, 0, 0)), pvec, pvec],
        compiler_params=_params(), name="pool_bwd",
    )(pv, dyp, pw, ps, onp)


def _pick_lane(tile, idx):
    lane = lax.broadcasted_iota(jnp.int32, (1, LANES), 1)
    return jnp.sum(jnp.where(lane == idx, tile, 0.0), axis=1, keepdims=True)


def _pick_row(tile, idx):
    sub = lax.broadcasted_iota(jnp.int32, (tile.shape[0], 1), 0)
    return jnp.sum(jnp.where(sub == idx, tile, 0.0), axis=0, keepdims=True)


def _put_lane(col, idx):
    lane = lax.broadcasted_iota(jnp.int32, (1, LANES), 1)
    return jnp.where(lane == idx, col, 0.0)


def _head_select(e):
    lo = _head_masks()
    return lo if e == 0 else jnp.logical_not(lo)


def _causal(st, shift):
    row = lax.broadcasted_iota(jnp.int32, st.shape, 0)
    col = lax.broadcasted_iota(jnp.int32, st.shape, 1) + shift
    return jnp.where(col >= row, st, NEG)


def _transpose_blocks(a):
    rows, cols = a.shape
    return jnp.concatenate(
        [jnp.concatenate([a[r:r + LANES, c:c + LANES].T for r in range(0, rows, LANES)], axis=1)
         for c in range(0, cols, LANES)], axis=0)


def _stat_rows(ref, head, nsub):
    return jnp.concatenate([_pick_row(ref[a], head) for a in range(nsub)], axis=1)


def _accumulate(ref, value, first):
    @pl.when(first)
    def _():
        ref[...] = value

    @pl.when(jnp.logical_not(first))
    def _():
        ref[...] += value


def _attn_fwd(qa, ka, vb, n_batch, seq):
    tq = min(ATT_BLOCK, seq)
    nq, nsub, tk = seq // tq, tq // ATT_SUB, tq
    pairs = vb.shape[1] // LANES

    def body(q_ref, k_ref, v_ref, o_ref, lse_ref, acc_ref):
        i, p = pl.program_id(1), pl.program_id(2)
        row_lo = lax.broadcasted_iota(jnp.int32, (LANES, 1), 0) < HEAD_DIM
        qs = [q_ref[:, e * LANES:(e + 1) * LANES] for e in range(2)]
        acc_ref[...] = jnp.zeros_like(acc_ref)

        def tile(off, stats, diagonal):
            vj = v_ref[pl.ds(off, tk), :]
            new, alphas, pvs = [], [], []
            for e in range(2):
                st = _dot_nt(k_ref[pl.ds(off, tk), e * LANES:(e + 1) * LANES], qs[e])
                if diagonal:
                    st = _causal(st, 0)
                m, l = stats[e]
                m_new = jnp.maximum(m, jnp.max(st, axis=0, keepdims=True))
                alpha = jnp.exp(m - m_new)
                pt = jnp.exp(st - m_new)
                new.append((m_new, alpha * l + jnp.sum(pt, axis=0, keepdims=True)))
                alphas.append(alpha)
                pvs.append(_dot_tn(jnp.where(_head_select(e), vj, jnp.zeros_like(vj)), pt.astype(BF16)))
            acc_ref[...] = acc_ref[...] * jnp.where(row_lo, alphas[0], alphas[1]) + (pvs[0] + pvs[1])
            return tuple(new)

        init = ((jnp.full((1, tq), NEG, F32), jnp.zeros((1, tq), F32)),) * 2
        stats = lax.fori_loop(0, i, lambda j, st: tile(pl.multiple_of(j * tk, tk), st, False), init)
        (m0, l0), (m1, l1) = tile(pl.multiple_of(i * tk, tk), stats, True)
        out_t = acc_ref[...] / jnp.where(row_lo, l0, l1)
        sub = lax.broadcasted_iota(jnp.int32, (8, 1), 0)
        lse0, lse1 = m0 + jnp.log(l0), m1 + jnp.log(l1)
        for a in range(nsub):
            sl = slice(a * ATT_SUB, (a + 1) * ATT_SUB)
            o_ref[sl, :] = out_t[:, sl].T
            rows = jnp.where(sub == 2 * p, lse0[:, sl], 0.0) + jnp.where(sub == 2 * p + 1, lse1[:, sl], 0.0)
            _accumulate(lse_ref.at[a], rows, p == 0)

    return pl.pallas_call(
        body,
        out_shape=[jax.ShapeDtypeStruct((n_batch * seq, pairs * LANES), F32),
                   jax.ShapeDtypeStruct((n_batch * seq // ATT_SUB, 8, ATT_SUB), F32)],
        grid=(n_batch, nq, pairs),
        in_specs=[pl.BlockSpec((tq, 2 * LANES), lambda b, i, p: (b * nq + i, p)),
                  pl.BlockSpec((seq, 2 * LANES), lambda b, i, p: (b, p)),
                  pl.BlockSpec((seq, LANES), lambda b, i, p: (b, p))],
        out_specs=[pl.BlockSpec((tq, LANES), lambda b, i, p: (b * nq + i, p)),
                   pl.BlockSpec((nsub, 8, ATT_SUB), lambda b, i, p: (b * nq + i, 0, 0))],
        scratch_shapes=[pltpu.VMEM((LANES, tq), F32)],
        compiler_params=_params(), name="attn_fwd",
    )(qa, ka, vb)


def _attn_bwd_q(qa, ka, vb, do, lse, delta, n_batch, seq, plan=None):
    tq = min(ATT_BLOCK, seq)
    nq, nsub, tk = seq // tq, tq // ATT_SUB, tq
    pairs = vb.shape[1] // LANES

    def body(q_ref, k_ref, v_ref, do_ref, lse_ref, dl_ref, dq_ref, dfq_ref, acc0_ref, acc1_ref):
        i, p = pl.program_id(1), pl.program_id(2)
        accs = (acc0_ref, acc1_ref)
        qs = [q_ref[:, e * LANES:(e + 1) * LANES] for e in range(2)]
        dov = do_ref[...]
        ls = [_stat_rows(lse_ref, 2 * p + e, nsub) for e in range(2)]
        dl = [_stat_rows(dl_ref, 2 * p + e, nsub) for e in range(2)]
        for acc in accs:
            acc[...] = jnp.zeros_like(acc)

        def tile(off, diagonal):
            vj = v_ref[pl.ds(off, tk), :]
            for e in range(2):
                kj = k_ref[pl.ds(off, tk), e * LANES:(e + 1) * LANES]
                st = _dot_nt(kj, qs[e])
                if diagonal:
                    st = _causal(st, 0)
                pt = jnp.exp(st - ls[e])
                dpt = _dot_nt(jnp.where(_head_select(e), vj, jnp.zeros_like(vj)), dov)
                accs[e][...] += _dot(_transpose_blocks(kj), (pt * (dpt - dl[e])).astype(BF16))

        def step(j, carry):
            tile(pl.multiple_of(j * tk, tk), False)
            return carry

        lax.fori_loop(0, i, step, 0)
        tile(pl.multiple_of(i * tk, tk), True)
        dq0, dq1 = _transpose_blocks(acc0_ref[...]), _transpose_blocks(acc1_ref[...])
        dq_ref[...] = jnp.where(_head_masks(), dq0, dq1)
        dfq = _put_lane(_pick_lane(dq0, _aug_lane(0)), 2 * p) + _put_lane(_pick_lane(dq1, _aug_lane(1)), 2 * p + 1)
        _accumulate(dfq_ref, dfq, p == 0)

    stat = pl.BlockSpec((nsub, 8, ATT_SUB), lambda b, i, p: (b * nq + i, 0, 0))
    blk = pl.BlockSpec((tq, LANES), lambda b, i, p: (b * nq + i, p))
    return _pallas(
        body, name="attn_bwd_q", args=[qa, ka, vb, do, lse, delta],
        out_shape=[jax.ShapeDtypeStruct((n_batch * seq, pairs * LANES), F32), jax.ShapeDtypeStruct((n_batch * seq, LANES), F32)],
        grid=(n_batch, nq, pairs),
        in_specs=[pl.BlockSpec((tq, 2 * LANES), lambda b, i, p: (b * nq + i, p)),
                  pl.BlockSpec((seq, 2 * LANES), lambda b, i, p: (b, p)),
                  pl.BlockSpec((seq, LANES), lambda b, i, p: (b, p)), blk, stat, stat],
        out_specs=[blk, pl.BlockSpec((tq, LANES), lambda b, i, p: (b * nq + i, 0))],
        scratch_shapes=[pltpu.VMEM((LANES, tq), F32), pltpu.VMEM((LANES, tq), F32)], plan=plan)


def _attn_bwd_kv(qa, ka, vb, do, lse, delta, n_batch, seq, plan=None):
    tkb = min(ATT_BLOCK, seq)
    nk, nsub, tq = seq // tkb, tkb // ATT_SUB, tkb
    n_tiles = seq // ATT_SUB
    pairs = vb.shape[1] // LANES

    def body(q_ref, k_ref, v_ref, do_ref, lse_ref, dl_ref, dk_ref, dv_ref, dfk_ref, dk0_ref, dk1_ref, dva_ref):
        j, p = pl.program_id(1), pl.program_id(2)
        dks = (dk0_ref, dk1_ref)
        ks = [k_ref[:, e * LANES:(e + 1) * LANES] for e in range(2)]
        vj = v_ref[...]
        vs = [jnp.where(_head_select(e), vj, jnp.zeros_like(vj)) for e in range(2)]
        for acc in (dk0_ref, dk1_ref, dva_ref):
            acc[...] = jnp.zeros_like(acc)

        def tile(t, diagonal):
            off = pl.multiple_of(t * tq, tq)
            dov = do_ref[pl.ds(off, tq), :]
            for e in range(2):
                qe = q_ref[pl.ds(off, tq), e * LANES:(e + 1) * LANES]
                st = _dot_nt(ks[e], qe)
                if diagonal:
                    st = _causal(st, 0)
                rows = lambda ref: jnp.concatenate([_pick_row(ref[t * nsub + a], 2 * p + e) for a in range(nsub)], axis=1)
                pt = jnp.exp(st - rows(lse_ref))
                dva_ref[...] += _dot(pt.astype(BF16), jnp.where(_head_select(e), dov, jnp.zeros_like(dov)))
                dst = pt * (_dot_nt(vs[e], dov) - rows(dl_ref))
                dks[e][...] += _dot(dst.astype(BF16), qe)

        def step(t, carry):
            tile(t, False)
            return carry

        lax.fori_loop(j + 1, nk, step, 0)
        tile(j, True)
        dk0, dk1 = dk0_ref[...], dk1_ref[...]
        dk_ref[...] = jnp.where(_head_masks(), dk0, dk1)
        dv_ref[...] = dva_ref[...].astype(BF16)
        dfk = (_put_lane(_pick_lane(dk0, _aug_lane(0) + 3), 2 * p)
               + _put_lane(_pick_lane(dk1, _aug_lane(1) + 3), 2 * p + 1))
        _accumulate(dfk_ref, -dfk, p == 0)

    stat = pl.BlockSpec((n_tiles, 8, ATT_SUB), lambda b, j, p: (b, 0, 0))
    blk = pl.BlockSpec((tkb, LANES), lambda b, j, p: (b * nk + j, p))
    acc = pltpu.VMEM((tkb, LANES), F32)
    return _pallas(
        body, name="attn_bwd_kv", args=[qa, ka, vb, do, lse, delta],
        out_shape=[jax.ShapeDtypeStruct((n_batch * seq, pairs * LANES), F32),
                   jax.ShapeDtypeStruct((n_batch * seq, pairs * LANES), BF16),
                   jax.ShapeDtypeStruct((n_batch * seq, LANES), F32)],
        grid=(n_batch, nk, pairs),
        in_specs=[pl.BlockSpec((seq, 2 * LANES), lambda b, j, p: (b, p)),
                  pl.BlockSpec((tkb, 2 * LANES), lambda b, j, p: (b * nk + j, p)), blk,
                  pl.BlockSpec((seq, LANES), lambda b, j, p: (b, p)), stat, stat],
        out_specs=[blk, blk, pl.BlockSpec((tkb, LANES), lambda b, j, p: (b * nk + j, 0))],
        scratch_shapes=[acc, acc, acc], plan=plan)


def _forget_bwd(dfq, dfk, f, bias, n_batch, seq):
    def body(dfq_ref, dfk_ref, f_ref, b_ref, df_ref, db_ref):
        acc = dfq_ref[...] + dfk_ref[...]
        row = lax.broadcasted_iota(jnp.int32, (seq, 1), 0)
        dist = 1
        while dist < seq:
            acc = acc + _shift_up(acc, dist, row, seq)
            dist *= 2
        df = acc * _sigmoid(-(f_ref[...] + b_ref[...]))
        df_ref[...] = df
        db_ref[...] = jnp.sum(df, axis=0, keepdims=True)

    col = pl.BlockSpec((seq, LANES), lambda b: (b, 0))
    return pl.pallas_call(
        body,
        out_shape=[jax.ShapeDtypeStruct((n_batch * seq, LANES), F32), jax.ShapeDtypeStruct((n_batch, 1, LANES), F32)],
        grid=(n_batch,), in_specs=[col, col, col, pl.BlockSpec((1, LANES), lambda b: (0, 0))],
        out_specs=[col, pl.BlockSpec((None, 1, LANES), lambda b: (b, 0, 0))],
        compiler_params=_params(), name="forget_bwd",
    )(dfq, dfk, f, bias)


def _mix_out(x1, yp, o, ona, woa, wob):
    t, d = x1.shape
    width = o.shape[1]
    tm = min(512, t)

    def body(x_ref, yp_ref, o_ref, on_ref, wa_ref, wb_ref, x2_ref, ya_ref):
        of = o_ref[...]
        ya = ((of * _rms(of)) * on_ref[...]).astype(BF16)
        ya_ref[...] = ya
        x2_ref[...] = x_ref[...] + (_dot(yp_ref[...], wa_ref[...]) + _dot(ya, wb_ref[...]))

    row = pl.BlockSpec((tm, d), lambda i: (i, 0))
    half = pl.BlockSpec((tm, width), lambda i: (i, 0))
    wspec = pl.BlockSpec((width, d), lambda i: (0, 0))
    return pl.pallas_call(
        body, out_shape=[jax.ShapeDtypeStruct((t, d), F32), jax.ShapeDtypeStruct((t, width), BF16)],
        grid=(t // tm,), in_specs=[row, half, half, pl.BlockSpec((1, width), lambda i: (0, 0)), wspec, wspec],
        out_specs=[row, half], compiler_params=_params(), name="mix_out",
    )(x1, yp, o, ona, woa, wob)


def _mix_out_bwd(dx2, o, yp, ya, ona, woa, wob, plan=None):
    t, d = dx2.shape
    width = o.shape[1]
    tm = min(512, t)
    nt = t // tm

    def body(dx_ref, o_ref, yp_ref, ya_ref, on_ref, wa_ref, wb_ref, dyp_ref, do_ref, dl_ref, dwa_ref, dwb_ref, don_ref):
        @pl.when(pl.program_id(0) == 0)
        def _():
            dwa_ref[...] = jnp.zeros_like(dwa_ref)
            dwb_ref[...] = jnp.zeros_like(dwb_ref)

        dxb = dx_ref[...].astype(BF16)
        dwa_ref[...] += _dot_tn(yp_ref[...], dxb)
        dwb_ref[...] += _dot_tn(ya_ref[...], dxb)
        dyp_ref[...] = _dot_nt(dxb, wa_ref[...])
        of = o_ref[...]
        dov, dgr = _rms_bwd(of, _rms(of), on_ref[...], _dot_nt(dxb, wb_ref[...]))
        don_ref[...] = jnp.sum(dgr, axis=0, keepdims=True)
        do_ref[...] = dov.astype(BF16)
        lo = _head_masks()
        prod = dov * of
        delta = jnp.zeros((tm, LANES), F32)
        for blk in range(width // LANES):
            pb = prod[:, blk * LANES:(blk + 1) * LANES]
            delta = delta + _put_lane(jnp.sum(jnp.where(lo, pb, 0.0), axis=1, keepdims=True), 2 * blk)
            delta = delta + _put_lane(jnp.sum(jnp.where(lo, 0.0, pb), axis=1, keepdims=True), 2 * blk + 1)
        for c in range(tm // ATT_SUB):
            dl_ref[c] = delta[c * ATT_SUB:(c + 1) * ATT_SUB, :].T[0:8, :]

    row = pl.BlockSpec((tm, d), lambda i: (i, 0))
    half = pl.BlockSpec((tm, width), lambda i: (i, 0))
    wspec = pl.BlockSpec((width, d), lambda i: (0, 0))
    return _pallas(
        body, name="mix_out_bwd", args=[dx2, o, yp, ya, ona, woa, wob],
        out_shape=[jax.ShapeDtypeStruct((t, width), F32), jax.ShapeDtypeStruct((t, width), BF16),
                   jax.ShapeDtypeStruct((t // ATT_SUB, 8, ATT_SUB), F32), jax.ShapeDtypeStruct((width, d), F32),
                   jax.ShapeDtypeStruct((width, d), F32), jax.ShapeDtypeStruct((nt, 1, width), F32)],
        grid=(nt,),
        in_specs=[row, half, half, half, pl.BlockSpec((1, width), lambda i: (0, 0)), wspec, wspec],
        out_specs=[half, half, pl.BlockSpec((tm // ATT_SUB, 8, ATT_SUB), lambda i: (i, 0, 0)), wspec, wspec,
                   pl.BlockSpec((None, 1, width), lambda i: (i, 0, 0))], plan=plan)


def _mix_in_bwd(dx2, x1, gain, hm, dpv, dqh, q, dkh, k, dv, df, qn, kn, wt):
    t, d = x1.shape
    width = q.shape[1]
    pool_width = dpv.shape[1]
    tm = min(512, t)
    nt = t // tm
    scale = HEAD_DIM ** -0.5
    c_q, c_k, c_v = pool_width, pool_width + width, pool_width + 2 * width
    c_f = c_v + width

    def body(dx2_ref, x_ref, g_ref, hm_ref, dpv_ref, dqh_ref, q_ref, dkh_ref, k_ref, dv_ref, df_ref, qn_ref, kn_ref,
             wt_ref, dx_ref, dwt_ref, dg_ref, dqn_ref, dkn_ref):
        @pl.when(pl.program_id(0) == 0)
        def _():
            dwt_ref[...] = jnp.zeros_like(dwt_ref)

        lo = _head_masks()
        hm = hm_ref[...]
        pieces = [(0, dpv_ref[...])]
        for c0, raw_ref, dh_ref, n_ref, dn_ref, mul in ((c_q, q_ref, dqh_ref, qn_ref, dqn_ref, scale),
                                                       (c_k, k_ref, dkh_ref, kn_ref, dkn_ref, 1.0)):
            cols = []
            for blk in range(width // LANES):
                sl = slice(blk * LANES, (blk + 1) * LANES)
                xb = raw_ref[:, sl]
                gb = dh_ref[:, sl] * mul
                r = _head_rms(xb, lo)
                xh = xb * r
                dyg = gb * n_ref[:, sl]
                cols.append((r * (dyg - xh * _head_mean(dyg * xh, lo))).astype(BF16))
                dn_ref[:, sl] = jnp.sum(gb * xh, axis=0, keepdims=True)
            pieces.append((c0, jnp.concatenate(cols, axis=1)))
        pieces.append((c_v, dv_ref[...]))
        pieces.append((c_f, df_ref[...].astype(BF16)))
        dhm = jnp.zeros((tm, d), F32)
        for c0, piece in pieces:
            dwt_ref[c0:c0 + piece.shape[1], :] += _dot_tn(piece, hm)
            dhm = dhm + _dot(piece, wt_ref[c0:c0 + piece.shape[1], :])
        xf = x_ref[...]
        dxn, dgr = _rms_bwd(xf, _rms(xf), g_ref[...], dhm)
        dx_ref[...] = dx2_ref[...] + dxn
        dg_ref[...] = jnp.sum(dgr, axis=0, keepdims=True)

    row = pl.BlockSpec((tm, d), lambda i: (i, 0))
    half = pl.BlockSpec((tm, width), lambda i: (i, 0))
    const = lambda shape: pl.BlockSpec(shape, lambda i: (0, 0))
    pvec = lambda n: pl.BlockSpec((None, 1, n), lambda i: (i, 0, 0))
    return pl.pallas_call(
        body,
        out_shape=[jax.ShapeDtypeStruct((t, d), F32), jax.ShapeDtypeStruct(wt.shape, F32),
                   jax.ShapeDtypeStruct((nt, 1, d), F32),
                   jax.ShapeDtypeStruct((nt, 1, width), F32), jax.ShapeDtypeStruct((nt, 1, width), F32)],
        grid=(nt,),
        in_specs=[row, row, const((1, d)), row, pl.BlockSpec((tm, pool_width), lambda i: (i, 0)), half, half, half, half,
                  half, pl.BlockSpec((tm, LANES), lambda i: (i, 0)), const((1, width)), const((1, width)),
                  const(wt.shape)],
        out_specs=[row, const(wt.shape), pvec(d), pvec(width), pvec(width)],
        compiler_params=_params(), name="mix_in_bwd",
    )(dx2, x1, gain, hm, dpv, dqh, q, dkh, k, dv, df, qn, kn, wt)


def _mesh_pos():
    return lax.axis_index("x"), lax.axis_index("y"), lax.axis_index("c")


def _other_chips(x, y):
    return [(1 - x, y), (x, 1 - y), (1 - x, 1 - y)]


def _remote(src, dst, send_sem, recv_sem, device):
    return pltpu.make_async_remote_copy(src_ref=src, dst_ref=dst, send_sem=send_sem, recv_sem=recv_sem,
                                        device_id=device, device_id_type=pl.DeviceIdType.MESH)


def _half_rows(n_rows, which):
    half = n_rows // 2
    return pl.ds(pl.multiple_of(which * half, 8), half)


def _row_block(rows, cols, itemsize=4):
    rb = rows
    while rb * cols * itemsize > (1 << 20) and rb % 32 == 0:
        rb //= 2
    return rb


def _place_cast(ws, chip, tag):
    n = len(ws)
    rows, cols = ws[0].shape
    rb = _row_block(rows, cols)

    def body(k_ref, *refs):
        for w_ref, o_ref in zip(refs[:n], refs[n:]):
            o_ref[...] = w_ref[...].astype(BF16)

    return pl.pallas_call(
        body, out_shape=[jax.ShapeDtypeStruct((N_CHIPS, rows, cols), BF16)] * n,
        grid_spec=pltpu.PrefetchScalarGridSpec(
            num_scalar_prefetch=1, grid=(rows // rb,),
            in_specs=[pl.BlockSpec((rb, cols), lambda i, k: (i, 0))] * n,
            out_specs=[pl.BlockSpec((None, rb, cols), lambda i, k: (k[0], i, 0))] * n),
        compiler_params=_params(), name="place_" + tag,
    )(chip, *ws)


class _Plan:
    def __init__(self, ins, outs, alias, sems, start, finish):
        self.ins, self.outs, self.alias, self.sems, self.start, self.finish = ins, outs, alias, sems, start, finish


def _merge_plans(a, b):
    ni, no, ns = len(a.ins), len(a.outs), len(a.sems)
    alias = dict(a.alias)
    alias.update({ni + i: no + o for i, o in b.alias.items()})

    def both(which):
        def run(ins, outs, sems):
            getattr(a, which)(ins[:ni], outs[:no], sems[:ns])
            getattr(b, which)(ins[ni:], outs[no:], sems[ns:])
        return run

    return _Plan(list(a.ins) + list(b.ins), list(a.outs) + list(b.outs), alias, list(a.sems) + list(b.sems),
                 both("start"), both("finish"))


def _run_plan(plan, name):
    n_in, n_out = len(plan.ins), len(plan.outs)

    def body(*refs):
        parts = refs[:n_in], refs[n_in:n_in + n_out], refs[n_in + n_out:]
        plan.start(*parts)
        plan.finish(*parts)

    return pl.pallas_call(
        body, out_shape=plan.outs, in_specs=[ANY] * n_in, out_specs=[ANY] * n_out, scratch_shapes=plan.sems,
        input_output_aliases=plan.alias, name=name,
    )(*plan.ins)


def _pallas(body, *, name, args, in_specs, out_shape, out_specs, grid, scratch_shapes=(), plan=None, aliases=None):
    n_in, n_out, n_scr = len(args), len(out_shape), len(scratch_shapes)
    aliases = dict(aliases or {})
    if plan is None:
        res = pl.pallas_call(body, out_shape=out_shape, grid=grid, in_specs=in_specs, out_specs=out_specs,
                             scratch_shapes=scratch_shapes, input_output_aliases=aliases,
                             compiler_params=_params(), name=name)(*args)
        return list(res), []
    p_in, p_out = len(plan.ins), len(plan.outs)

    def carrying(*refs):
        ins, p_ins = refs[:n_in], refs[n_in:n_in + p_in]
        o0 = n_in + p_in
        outs, p_outs = refs[o0:o0 + n_out], refs[o0 + n_out:o0 + n_out + p_out]
        s0 = o0 + n_out + p_out
        scr, p_sems = refs[s0:s0 + n_scr], refs[s0 + n_scr:]
        ids = [pl.program_id(a) for a in range(len(grid))]
        first = functools.reduce(jnp.logical_and, [i == 0 for i in ids])
        last = functools.reduce(jnp.logical_and, [i == g - 1 for i, g in zip(ids, grid)])

        @pl.when(first)
        def _():
            plan.start(p_ins, p_outs, p_sems)

        body(*ins, *outs, *scr)

        @pl.when(last)
        def _():
            plan.finish(p_ins, p_outs, p_sems)

    res = pl.pallas_call(
        carrying, out_shape=list(out_shape) + list(plan.outs), grid=grid,
        in_specs=list(in_specs) + [ANY] * p_in, out_specs=list(out_specs) + [ANY] * p_out,
        scratch_shapes=list(scratch_shapes) + list(plan.sems),
        input_output_aliases={**aliases, **{n_in + i: n_out + o for i, o in plan.alias.items()}},
        compiler_params=_params(), name=name,
    )(*args, *plan.ins)
    return list(res[:n_out]), list(res[n_out:])


def _plan_gather(stacks):
    n = len(stacks)

    def ici_copies(outs, sems):
        x, y, c = _mesh_pos()
        cps = []
        for w in range(n):
            own = outs[w].at[2 * x + y, _half_rows(stacks[w].shape[1], c)]
            cps += [_remote(own, own, sems[0].at[w, j], sems[1].at[w, j], (*chip, c)) for j, chip in enumerate(_other_chips(x, y))]
        return cps

    def start(ins, outs, sems):
        for cp in ici_copies(outs, sems):
            cp.start()

    def finish(ins, outs, sems):
        ici_send, ici_recv, d2d_send, d2d_recv = sems
        x, y, c = _mesh_pos()
        sibling = (x, y, 1 - c)
        slots = [2 * cx + cy for cx, cy in _other_chips(x, y)]
        forwards = []
        for w in range(n):
            rows = _half_rows(stacks[w].shape[1], c)
            for j in range(3):
                landed = outs[w].at[slots[j], rows]
                _remote(landed, landed, ici_send.at[w, j], ici_recv.at[w, j], sibling).wait_recv()
                cp = _remote(landed, landed, d2d_send.at[w, j], d2d_recv.at[w, j], sibling)
                cp.start()
                forwards.append(cp)
        for w in range(n):
            rows = _half_rows(stacks[w].shape[1], 1 - c)
            for j in range(3):
                landed = outs[w].at[slots[j], rows]
                _remote(landed, landed, d2d_send.at[w, j], d2d_recv.at[w, j], sibling).wait_recv()
        for cp in ici_copies(outs, sems) + forwards:
            cp.wait_send()

    return _Plan(stacks, [jax.ShapeDtypeStruct(s.shape, s.dtype) for s in stacks], {w: w for w in range(n)},
                 [pltpu.SemaphoreType.DMA((n, 3))] * 4, start, finish)


def _plan_sibling_halves(gs):
    n = len(gs)

    def copies(ins, outs, sems):
        x, y, c = _mesh_pos()
        return [_remote(ins[w].at[:, _half_rows(gs[w].shape[1], 1 - c), :], outs[w], sems[0].at[w], sems[1].at[w],
                        (x, y, 1 - c)) for w in range(n)]

    def start(ins, outs, sems):
        for cp in copies(ins, outs, sems):
            cp.start()

    def finish(ins, outs, sems):
        for cp in copies(ins, outs, sems):
            cp.wait()

    return _Plan(gs, [jax.ShapeDtypeStruct((g.shape[0], g.shape[1] // 2, g.shape[2]), g.dtype) for g in gs], {},
                 [pltpu.SemaphoreType.DMA((n,))] * 2, start, finish)


def _plan_chip_exchange(ps):
    n = len(ps)

    def copies(ins, outs, sems):
        x, y, c = _mesh_pos()
        return [_remote(ins[w].at[2 * cx + cy], outs[w].at[j], sems[0].at[w, j], sems[1].at[w, j], (cx, cy, c))
                for w in range(n) for j, (cx, cy) in enumerate(_other_chips(x, y))]

    def start(ins, outs, sems):
        for cp in copies(ins, outs, sems):
            cp.start()

    def finish(ins, outs, sems):
        for cp in copies(ins, outs, sems):
            cp.wait()

    return _Plan(ps, [jax.ShapeDtypeStruct((3,) + p.shape[1:], p.dtype) for p in ps], {},
                 [pltpu.SemaphoreType.DMA((n, 3))] * 2, start, finish)


def _plan_sibling_share(gs):
    n = len(gs)

    def copies(outs, sems, which):
        x, y, c = _mesh_pos()
        cps = []
        for w in range(n):
            rows = outs[w].at[_half_rows(gs[w].shape[0], c if which == "mine" else 1 - c)]
            cps.append(_remote(rows, rows, sems[0].at[w], sems[1].at[w], (x, y, 1 - c)))
        return cps

    def start(ins, outs, sems):
        for cp in copies(outs, sems, "mine"):
            cp.start()

    def finish(ins, outs, sems):
        for cp in copies(outs, sems, "mine"):
            cp.wait_send()
        for cp in copies(outs, sems, "theirs"):
            cp.wait_recv()

    return _Plan(gs, [jax.ShapeDtypeStruct(g.shape, g.dtype) for g in gs], {w: w for w in range(n)},
                 [pltpu.SemaphoreType.DMA((n,))] * 2, start, finish)


def _same_shape_groups(arrays):
    groups = {}
    for i, a in enumerate(arrays):
        groups.setdefault(a.shape, []).append(i)
    return list(groups.values())


def _add_sibling(gs, r1s, ids, tag):
    n = len(gs)
    nch, rh, cols = r1s[0].shape

    def body(ids_ref, *refs):
        for g_ref, r_ref, o_ref in zip(refs[:n], refs[n:2 * n], refs[2 * n:]):
            o_ref[...] = (g_ref[...] + r_ref[...]).astype(BF16)

    blk = lambda fn: pl.BlockSpec((None, rh, cols), fn)
    return pl.pallas_call(
        body, out_shape=[jax.ShapeDtypeStruct(r1s[0].shape, BF16)] * n,
        grid_spec=pltpu.PrefetchScalarGridSpec(
            num_scalar_prefetch=1, grid=(nch,),
            in_specs=[blk(lambda k, ids: (k, ids[1], 0))] * n + [blk(lambda k, ids: (k, 0, 0))] * n,
            out_specs=[blk(lambda k, ids: (k, 0, 0))] * n),
        compiler_params=_params(), name="add_sibling_" + tag,
    )(ids, *gs, *r1s)


def _add_chips(gs, r1s, r2s, ids, tag):
    n = len(gs)
    _, rh, cols = r1s[0].shape
    nb = 2 if rh % 32 == 0 else 1
    rb = rh // nb

    def body(ids_ref, *refs):
        for g_ref, r1_ref, r2_ref, o_ref in zip(refs[:n], refs[n:2 * n], refs[2 * n:3 * n], refs[3 * n:]):
            own = g_ref[...] + r1_ref[...]
            o_ref[...] = ((own + r2_ref[0].astype(F32)) + r2_ref[1].astype(F32)) + r2_ref[2].astype(F32)

    return pl.pallas_call(
        body, out_shape=[jax.ShapeDtypeStruct((2 * rh, cols), F32)] * n,
        grid_spec=pltpu.PrefetchScalarGridSpec(
            num_scalar_prefetch=1, grid=(nb,),
            in_specs=[pl.BlockSpec((None, rb, cols), lambda i, ids: (ids[0], ids[1] * nb + i, 0))] * n
            + [pl.BlockSpec((None, rb, cols), lambda i, ids: (ids[0], i, 0))] * n
            + [pl.BlockSpec((3, rb, cols), lambda i, ids: (0, i, 0))] * n,
            out_specs=[pl.BlockSpec((rb, cols), lambda i, ids: (ids[1] * nb + i, 0))] * n),
        compiler_params=_params(), name="add_chips_" + tag,
    )(ids, *gs, *r1s, *r2s)


VEC_ROWS = 8


N_DEVICES = 8


def _small_pack(part, d, width):
    names = ("ffn1_norm", "mix_norm", "ffn2_norm", "pool_scale", "out_norm_pool", "out_norm_attn", "qn", "kn", "b_forget",
             "pool_w", "loss")
    args = [part[k] for k in names]
    pw_shape = part["pool_w"].shape[1:]

    def body(g1_ref, gm_ref, g2_ref, ps_ref, onp_ref, ona_ref, qn_ref, kn_ref, bf_ref, pw_ref, loss_ref, vbuf, pbuf):
        lo = _head_masks()

        def fold_heads(ref):
            v = jnp.sum(ref[...], axis=0)
            acc = jnp.zeros((VEC_ROWS, LANES), F32)
            for blk in range(width // LANES):
                vb = jnp.broadcast_to(v[:, blk * LANES:(blk + 1) * LANES], (VEC_ROWS, LANES))
                acc = acc + vb + pltpu.roll(vb, HEAD_DIM, 1)
            return jnp.where(lo, acc, 0.0)[0:1, :]

        vbuf[0] = jnp.zeros((VEC_ROWS, d), F32)
        vbuf[0, 0:1, :] = jnp.sum(g1_ref[...], axis=0)
        vbuf[0, 1:2, :] = jnp.sum(gm_ref[...], axis=0)
        vbuf[0, 2:3, :] = jnp.sum(g2_ref[...], axis=0)
        vbuf[0, 5:6, 0:LANES] = jnp.sum(loss_ref[...], axis=0)[0:1, :]
        vbuf[0, 3:4, 0:width] = jnp.sum(ps_ref[...], axis=0)
        vbuf[0, 3:4, width:2 * width] = jnp.sum(onp_ref[...], axis=0)
        vbuf[0, 4:5, 0:width] = jnp.sum(ona_ref[...], axis=0)
        vbuf[0, 4:5, width:width + LANES] = fold_heads(qn_ref)
        vbuf[0, 4:5, width + LANES:width + 2 * LANES] = fold_heads(kn_ref)
        vbuf[0, 4:5, width + 2 * LANES:width + 3 * LANES] = jnp.sum(bf_ref[...], axis=0)
        pbuf[0] = jnp.sum(pw_ref[...], axis=0)

    return pl.pallas_call(
        body, out_shape=[jax.ShapeDtypeStruct((N_DEVICES, VEC_ROWS, d), F32), jax.ShapeDtypeStruct((N_DEVICES,) + pw_shape, F32)],
        in_specs=[VM] * len(args), out_specs=[VM, VM], compiler_params=_params(), name="small_pack",
    )(*args)


def _plan_all_to_all(stacks):
    n = len(stacks)

    def copies(outs, sems):
        x, y, c = _mesh_pos()
        cps = []
        for r in range(1, N_DEVICES):
            peer = (x if not r & 4 else 1 - x, y if not r & 2 else 1 - y, c if not r & 1 else 1 - c)
            cps += [_remote(outs[w].at[0], outs[w].at[r], sems[0].at[w, r - 1], sems[1].at[w, r - 1], peer) for w in range(n)]
        return cps

    def start(ins, outs, sems):
        for cp in copies(outs, sems):
            cp.start()

    def finish(ins, outs, sems):
        for cp in copies(outs, sems):
            cp.wait()

    return _Plan(stacks, [jax.ShapeDtypeStruct(s.shape, s.dtype) for s in stacks], {w: w for w in range(n)},
                 [pltpu.SemaphoreType.DMA((n, N_DEVICES - 1))] * 2, start, finish)


def _small_sum(vstack, pstack, me):
    def body(me_ref, vbuf, pbuf, vec_ref, pw_ref):
        vec = vbuf[me_ref[0]]
        pw = pbuf[me_ref[0]]
        for dev in range(1, N_DEVICES):
            vec = vec + vbuf[jnp.bitwise_xor(me_ref[0], dev)]
            pw = pw + pbuf[jnp.bitwise_xor(me_ref[0], dev)]
        vec_ref[...] = vec
        pw_ref[...] = pw

    full = lambda s: pl.BlockSpec(s.shape, lambda i, me: (0,) * len(s.shape))
    outs = [jax.ShapeDtypeStruct(vstack.shape[1:], F32), jax.ShapeDtypeStruct(pstack.shape[1:], F32)]
    return pl.pallas_call(
        body, out_shape=outs,
        grid_spec=pltpu.PrefetchScalarGridSpec(num_scalar_prefetch=1, grid=(1,), in_specs=[full(vstack), full(pstack)],
                                               out_specs=[full(o) for o in outs]),
        compiler_params=_params(), name="small_sum",
    )(me, vstack, pstack)


def _adamw(ws, gs, ms, vs, tag):
    n = len(ws)
    rows, cols = ws[0].shape
    rb = rows
    while rb * cols * 4 * n > (1 << 20) and rb % 16 == 0:
        rb //= 2

    def body(*refs):
        for j in range(n):
            w_ref, g_ref, m_ref, v_ref = (refs[k * n + j] for k in range(4))
            d_ref, mo_ref, vo_ref = (refs[(4 + k) * n + j] for k in range(3))
            gv = g_ref[...]
            m2 = ADAM_B1 * m_ref[...] + (1.0 - ADAM_B1) * gv
            v2 = ADAM_B2 * v_ref[...] + (1.0 - ADAM_B2) * (gv * gv)
            m_hat = m2 / (1.0 - ADAM_B1 ** ADAM_STEP)
            v_hat = v2 / (1.0 - ADAM_B2 ** ADAM_STEP)
            d_ref[...] = -ADAM_LR * (m_hat / (jnp.sqrt(v_hat) + ADAM_EPS) + ADAM_WD * w_ref[...])
            mo_ref[...] = m2
            vo_ref[...] = v2

    spec = pl.BlockSpec((rb, cols), lambda i: (i, 0))
    res = pl.pallas_call(
        body, out_shape=[jax.ShapeDtypeStruct(ws[0].shape, F32)] * (3 * n), grid=(rows // rb,),
        in_specs=[spec] * (4 * n), out_specs=[spec] * (3 * n), compiler_params=_params(), name="adamw_" + tag,
    )(*ws, *gs, *ms, *vs)
    return [(res[j], res[n + j], res[2 * n + j]) for j in range(n)]


def _pack_vec(p, d, width):
    pad = lambda v: jnp.pad(v, (0, LANES - v.shape[0]))
    row3 = jnp.concatenate([p["pool_scale"], p["out_norm_pool"]])
    row4 = jnp.concatenate([p["out_norm_attn"], pad(p["q_norm"]), pad(p["k_norm"]), pad(p["b_forget"]),
                            jnp.zeros((d - width - 3 * LANES,), F32)])
    rows = [p["ffn1_norm"], p["mix_norm"], p["ffn2_norm"], row3, row4]
    return jnp.pad(jnp.stack(rows), ((0, VEC_ROWS - len(rows)), (0, 0)))


def _unpack_vec(vec, width):
    return dict(ffn1_norm=vec[0], mix_norm=vec[1], ffn2_norm=vec[2], pool_scale=vec[3, :width],
                out_norm_pool=vec[3, width:2 * width], out_norm_attn=vec[4, :width],
                q_norm=vec[4, width:width + HEAD_DIM], k_norm=vec[4, width + LANES:width + LANES + HEAD_DIM],
                b_forget=vec[4, width + 2 * LANES:width + 2 * LANES + N_HEADS])


WEIGHT_NAMES = ("ffn1_norm", "ffn1_w_gate", "ffn1_w_up", "ffn1_w_down", "mix_norm", "w_in", "b_forget", "pool_w",
                "pool_scale", "q_norm", "k_norm", "out_norm_pool", "out_norm_attn", "w_out", "ffn2_norm",
                "ffn2_w_gate", "ffn2_w_up", "ffn2_w_down")
BIG_NAMES = ("ffn1_w_gate", "ffn1_w_up", "ffn1_w_down", "w_in", "w_out", "ffn2_w_gate", "ffn2_w_up", "ffn2_w_down")
TRANSPOSED_NAMES = ("ffn1_w_gate", "ffn1_w_up", "w_in", "ffn2_w_gate", "ffn2_w_up")
FFN1_NAMES = ("ffn1_w_gate", "ffn1_w_up", "ffn1_w_down")
MIX_NAMES = ("w_in", "w_out")
FFN2_NAMES = ("ffn2_w_gate", "ffn2_w_up", "ffn2_w_down")
REST_NAMES = MIX_NAMES + FFN2_NAMES


def kernel(x, ffn1_norm, ffn1_w_gate, ffn1_w_up, ffn1_w_down, mix_norm, w_in, b_forget, pool_w, pool_scale, q_norm, k_norm, out_norm_pool, out_norm_attn, w_out, ffn2_norm, ffn2_w_gate, ffn2_w_up, ffn2_w_down, loss_target, m_ffn1_norm, m_ffn1_w_gate, m_ffn1_w_up, m_ffn1_w_down, m_mix_norm, m_w_in, m_b_forget, m_pool_w, m_pool_scale, m_q_norm, m_k_norm, m_out_norm_pool, m_out_norm_attn, m_w_out, m_ffn2_norm, m_ffn2_w_gate, m_ffn2_w_up, m_ffn2_w_down, v_ffn1_norm, v_ffn1_w_gate, v_ffn1_w_up, v_ffn1_w_down, v_mix_norm, v_w_in, v_b_forget, v_pool_w, v_pool_scale, v_q_norm, v_k_norm, v_out_norm_pool, v_out_norm_attn, v_w_out, v_ffn2_norm, v_ffn2_w_gate, v_ffn2_w_up, v_ffn2_w_down):
    given = dict(locals())
    w = {n: given[n] for n in WEIGHT_NAMES}
    m = {n: given["m_" + n] for n in WEIGHT_NAMES}
    v = {n: given["v_" + n] for n in WEIGHT_NAMES}
    n_batch, seq, d = x.shape
    width = pool_scale.shape[0]
    in_rows = w_in.shape[1]
    in_cols = N_CHIPS * in_rows
    in_pad = -(-in_rows // 32) * 32
    in_cols_pad = in_cols - N_HEADS + LANES

    work = lambda a, n: a.T if n in TRANSPOSED_NAMES else a
    exchanged = lambda a, n: jnp.pad(a, ((0, in_pad - in_rows), (0, 0))) if n == "w_in" else a

    mesh_x, mesh_y, mesh_c = _mesh_pos()
    ids = jnp.stack([2 * mesh_x + mesh_y, mesh_c]).astype(jnp.int32)

    row = lambda a: a.reshape(1, -1)
    g1, gm, g2, ps, onp, ona = (row(a) for a in (ffn1_norm, mix_norm, ffn2_norm, pool_scale, out_norm_pool, out_norm_attn))
    qn, kn = row(jnp.tile(q_norm, N_HEADS)), row(jnp.tile(k_norm, N_HEADS))
    bf = row(jnp.pad(b_forget, (0, LANES - N_HEADS)))
    pwb = pool_w.astype(BF16)
    xf, tgt = x.reshape(n_batch * seq, d), loss_target.reshape(n_batch * seq, d)

    def grouped(call, names, *lists):
        out = [None] * len(names)
        for idx in _same_shape_groups(lists[0]):
            res = call(*[[lst[i] for i in idx] for lst in lists], names[idx[0]])
            for i, r in zip(idx, res):
                out[i] = r
        return out

    placed = dict(zip(BIG_NAMES, grouped(lambda ws, tag: _place_cast(ws, ids, tag), BIG_NAMES,
                                         [exchanged(work(w[n], n), n) for n in BIG_NAMES])))
    wg1, wu1, wd1 = _run_plan(_plan_gather([placed[n] for n in FFN1_NAMES]), "gather_ffn1")
    (x1, h1, a1, b1, s1), rest = _ffn_fwd(xf, g1, wg1, wu1, wd1, plan=_plan_gather([placed[n] for n in REST_NAMES]))
    w_in_all, w_out_all, wg2, wu2, wd2 = rest
    w_in_t = jnp.pad(w_in_all[:, :in_rows].reshape(in_cols, d), ((0, in_cols_pad - in_cols), (0, 0)))
    w_out_full = w_out_all.reshape(N_CHIPS * w_out.shape[0], d)
    woa, wob = w_out_full[:width], w_out_full[width:]

    hm, pv, q, k, qh, kh, vb, f = _mix_proj(x1, gm, w_in_t, qn, kn, width, width)
    qa, ka = _forget_prefix(f, bf, qh, kh, n_batch, seq)
    yp = _pool_fwd(pv, pwb, ps, onp, n_batch, seq)
    o, lse = _attn_fwd(qa, ka, vb, n_batch, seq)
    x2, ya = _mix_out(x1, yp, o, ona, woa, wob)
    (dy, h2, a2, b2, s2, lpart), _ = _ffn_fwd(x2, g2, wg2, wu2, wd2, target=tgt)

    def to_chips(gs, arrived, tags):
        return grouped(lambda g, r, tag: _add_sibling(g, r, ids, tag), tags, gs, arrived)

    def own_rows(gs, from_sibling, from_chips, tags):
        return grouped(lambda g, ra, rb, tag: _add_chips(g, ra, rb, ids, tag), tags, gs, from_sibling, from_chips)

    (dx2, da2, db2, dg2), _ = _ffn_bwd_x(dy, x2, g2, a2, b2, wg2, wu2, wd2, "ffn2_bwd_x")
    dw2, _ = _ffn_bwd_w([(da2, h2, 1.0), (db2, h2, 1.0), (s2, dy, 0.5)], "ffn2_bwd_w")
    (dyp, do, delta, dwoa, dwob, dona), sib2 = _mix_out_bwd(dx2, o, yp, ya, ona, woa, wob, plan=_plan_sibling_halves(dw2))
    dpv, dpw, dps, donp = _pool_bwd(pv, dyp, pwb, ps, onp, n_batch, seq)
    (dqh, dfq), chips2 = _attn_bwd_q(qa, ka, vb, do, lse, delta, n_batch, seq,
                                     plan=_plan_chip_exchange(to_chips(dw2, sib2, FFN2_NAMES)))
    (dkh, dv, dfk), red2 = _attn_bwd_kv(qa, ka, vb, do, lse, delta, n_batch, seq,
                                        plan=_plan_sibling_share(own_rows(dw2, sib2, chips2, FFN2_NAMES)))
    df, dbf = _forget_bwd(dfq, dfk, f, bf, n_batch, seq)
    dx1, dw_in_t, dgm, dqn, dkn = _mix_in_bwd(dx2, x1, gm, hm, dpv, dqh, q, dkh, k, dv, df, qn, kn, w_in_t)
    in_base = [in_rows * k // 8 * 8 for k in range(N_CHIPS)]
    d_w_in = jnp.stack([dw_in_t[b:b + in_pad] for b in in_base])
    d_w_out = jnp.concatenate([dwoa, dwob], axis=0).reshape(N_CHIPS, w_out.shape[0], d)
    dwm = [d_w_in, d_w_out]
    down, gate_up = FFN1_NAMES[2:], FFN1_NAMES[:2]
    dwd1, sibm = _ffn_bwd_w([(s1, dx1, 0.5)], "ffn1_bwd_w_down", plan=_plan_sibling_halves(dwm))
    (da1, db1), arrived = _ffn_bwd_a(dx1, a1, b1, wd1, "ffn1_bwd_a",
                                     plan=_merge_plans(_plan_sibling_halves(dwd1),
                                                       _plan_chip_exchange(to_chips(dwm, sibm, MIX_NAMES))))
    sibd, chipsm = arrived[:1], arrived[1:]
    dwgu1, chipsd = _ffn_bwd_w([(da1, h1, 1.0), (db1, h1, 1.0)], "ffn1_bwd_w_gate_up",
                               plan=_plan_chip_exchange(to_chips(dwd1, sibd, down)))
    n_tiles = (n_batch * seq) // min(512, n_batch * seq)
    first = max(n_tiles // 4, 1)
    begun, sibgu = _ffn_bwd_h(dx1, xf, g1, da1, db1, wg1, wu1, "ffn1_bwd_h_first", (0, first),
                              plan=_plan_sibling_halves(dwgu1))
    (gx, dg1), chipsgu = _ffn_bwd_h(dx1, xf, g1, da1, db1, wg1, wu1, "ffn1_bwd_h_rest", (first, n_tiles), prev=begun,
                                    plan=_plan_chip_exchange(to_chips(dwgu1, sibgu, gate_up)))

    part = dict(ffn1_norm=dg1, mix_norm=dgm, ffn2_norm=dg2, b_forget=dbf, pool_scale=dps, out_norm_pool=donp,
                out_norm_attn=dona, qn=dqn, kn=dkn, pool_w=dpw.reshape(n_batch, -1, pool_w.shape[-1]), loss=lpart)
    mine = (own_rows(dwgu1, sibgu, chipsgu, gate_up) + own_rows(dwd1, sibd, chipsd, down)
            + own_rows(dwm, sibm, chipsm, MIX_NAMES))
    last = _run_plan(_merge_plans(_plan_sibling_share(mine), _plan_all_to_all(_small_pack(part, d, width))), "last_exchange")
    vstack, pstack = last[len(mine):]
    g_vec, g_pw = _small_sum(vstack, pstack, jnp.reshape(4 * mesh_x + 2 * mesh_y + mesh_c, (1,)).astype(jnp.int32))
    loss = g_vec[5, 0]
    reduced = dict(zip(FFN1_NAMES + MIX_NAMES + FFN2_NAMES, list(last[:len(mine)]) + list(red2)))
    reduced["w_in"] = lax.dynamic_slice(reduced["w_in"], ((in_rows * ids[0]) % 8, 0), (in_rows, d))

    grads, delta, new_m, new_v = {}, {}, {}, {}
    for names in (FFN2_NAMES, FFN1_NAMES, ("w_in",), ("w_out",)):
        stepped = _adamw([work(w[n], n) for n in names], [reduced[n] for n in names], [work(m[n], n) for n in names],
                         [work(v[n], n) for n in names], names[0])
        for n, step in zip(names, stepped):
            grads[n], delta[n], new_m[n], new_v[n] = (work(a, n) for a in (reduced[n], *step))
    flat_pw = lambda a: a.reshape(-1, a.shape[-1])
    (d_pw, m_pw, v_pw), = _adamw([flat_pw(pool_w)], [g_pw], [flat_pw(m_pool_w)], [flat_pw(v_pool_w)], "pool_w")
    (d_vec, m_vec, v_vec), = _adamw([_pack_vec(w, d, width)], [g_vec], [_pack_vec(m, d, width)], [_pack_vec(v, d, width)],
                                    "vectors")
    grads.update(_unpack_vec(g_vec, width), pool_w=g_pw.reshape(pool_w.shape))
    delta.update(_unpack_vec(d_vec, width), pool_w=d_pw.reshape(pool_w.shape))
    new_m.update(_unpack_vec(m_vec, width), pool_w=m_pw.reshape(pool_w.shape))
    new_v.update(_unpack_vec(v_vec, width), pool_w=v_pw.reshape(pool_w.shape))
    return (loss, gx.reshape(x.shape), *[grads[n] for n in WEIGHT_NAMES], *[delta[n] for n in WEIGHT_NAMES],
            *[new_m[n] for n in WEIGHT_NAMES], *[new_v[n] for n in WEIGHT_NAMES])
```

```python
import functools

import jax
import jax.numpy as jnp
from jax import lax
from jax.experimental import pallas as pl
from jax.experimental.pallas import tpu as pltpu

F32 = jnp.float32
BF16 = jnp.bfloat16
EPS = 1e-6
NEG = -1e30
ADAM_LR = 0.001
ADAM_B1 = 0.9
ADAM_B2 = 0.999
ADAM_EPS = 1e-08
ADAM_WD = 0.01
ADAM_STEP = 10
POOL_WINDOWS = (2, 4, 8, 16)
HEAD_DIM = 64
N_HEADS = 8
LANES = 128
N_CHIPS = 4
ATT_BLOCK = 512
ATT_SUB = 128
VMEM_LIMIT = 56 * 1024 * 1024
MESH_AXES = ("x", "y", "c")
ANY = pl.BlockSpec(memory_space=pl.ANY)
VM = pl.BlockSpec(memory_space=pltpu.VMEM)


def _params(**kw):
    return pltpu.CompilerParams(vmem_limit_bytes=VMEM_LIMIT, **kw)


def _dot(a, b):
    return jnp.dot(a, b, preferred_element_type=F32)


def _dot_nt(a, b):
    return lax.dot_general(a, b, (((1,), (1,)), ((), ())), preferred_element_type=F32)


def _dot_tn(a, b):
    return lax.dot_general(a, b, (((0,), (0,)), ((), ())), preferred_element_type=F32)


def _sigmoid(z):
    return 1.0 / (1.0 + jnp.exp(-z))


def _rms(xf):
    return lax.rsqrt(jnp.mean(xf * xf, axis=-1, keepdims=True) + EPS)


def _rms_bwd(xf, r, gain, dh):
    xh = xf * r
    dyg = dh * gain
    return r * (dyg - xh * jnp.mean(dyg * xh, axis=-1, keepdims=True)), dh * xh


def _total(v):
    return jnp.sum(jnp.sum(v, axis=1, keepdims=True), axis=0, keepdims=True)


def _ffn_fwd(x, gain, wg, wu, wd, target=None, plan=None):
    t, d = x.shape
    nch, fc, _ = wg.shape
    tm = min(512, t)
    nt = t // tm
    with_loss = target is not None

    def body(*refs):
        if with_loss:
            x_ref, g_ref, wg_ref, wu_ref, wd_ref, t_ref, o_ref, h_ref, a_ref, b_ref, s_ref, l_ref, acc_ref = refs
        else:
            x_ref, g_ref, wg_ref, wu_ref, wd_ref, o_ref, h_ref, a_ref, b_ref, s_ref, acc_ref = refs
        k = pl.program_id(1)

        @pl.when(k == 0)
        def _():
            xf = x_ref[...]
            h_ref[...] = ((xf * _rms(xf)) * g_ref[...]).astype(BF16)
            acc_ref[...] = jnp.zeros_like(acc_ref)

        h = h_ref[...]
        a = _dot_nt(h, wg_ref[...])
        b = _dot_nt(h, wu_ref[...])
        sb = ((a * (0.5 * jnp.tanh(0.5 * a) + 0.5)) * b).astype(BF16)
        a_ref[...] = a.astype(BF16)
        b_ref[...] = b.astype(BF16)
        s_ref[...] = sb
        acc_ref[...] += _dot(sb, wd_ref[...])

        @pl.when(k == nch - 1)
        def _():
            y = x_ref[...] + 0.5 * acc_ref[...]
            if with_loss:
                e = y - t_ref[...]
                o_ref[...] = e * (1.0 / d)
                l_ref[...] = jnp.broadcast_to(_total(e * e) * (0.5 / d), l_ref.shape)
            else:
                o_ref[...] = y

    row = pl.BlockSpec((tm, d), lambda i, k: (i, 0))
    chunk = pl.BlockSpec((None, fc, d), lambda i, k: (k, 0, 0))
    act = pl.BlockSpec((None, tm, fc), lambda i, k: (k, i, 0))
    in_specs = [row, pl.BlockSpec((1, d), lambda i, k: (0, 0)), chunk, chunk, chunk]
    out_shape = [jax.ShapeDtypeStruct((t, d), F32), jax.ShapeDtypeStruct((t, d), BF16)]
    out_shape += [jax.ShapeDtypeStruct((nch, t, fc), BF16)] * 3
    out_specs = [row, row, act, act, act]
    args = [x, gain, wg, wu, wd]
    if with_loss:
        in_specs.append(row)
        args.append(target)
        out_shape.append(jax.ShapeDtypeStruct((nt, 8, LANES), F32))
        out_specs.append(pl.BlockSpec((None, 8, LANES), lambda i, k: (i, 0, 0)))
    return _pallas(body, name="ffn_fwd_loss" if with_loss else "ffn_fwd", args=args, in_specs=in_specs,
                   out_shape=out_shape, out_specs=out_specs, grid=(nt, nch),
                   scratch_shapes=[pltpu.VMEM((tm, d), F32)], plan=plan)


def _swiglu_grads(dy_ref, a_ref, b_ref, wd_ref):
    ds = _dot_nt(dy_ref[...].astype(BF16), wd_ref[...])
    av = a_ref[...].astype(F32)
    bv = b_ref[...].astype(F32)
    th = jnp.tanh(0.5 * av)
    half_sig = 0.25 * th + 0.25
    dab = ((ds * bv) * (half_sig * (1.0 + av * (0.5 - 0.5 * th)))).astype(BF16)
    return dab, (ds * (av * half_sig)).astype(BF16)


def _ffn_bwd_a(dy, a, b, wd, name, plan=None):
    t, d = dy.shape
    nch, fc, _ = wd.shape
    tm = min(512, t)

    def body(dy_ref, a_ref, b_ref, wd_ref, da_ref, db_ref):
        da_ref[...], db_ref[...] = _swiglu_grads(dy_ref, a_ref, b_ref, wd_ref)

    act = pl.BlockSpec((None, tm, fc), lambda i, k: (k, i, 0))
    return _pallas(
        body, name=name, args=[dy, a, b, wd], out_shape=[jax.ShapeDtypeStruct((nch, t, fc), BF16)] * 2, grid=(t // tm, nch),
        in_specs=[pl.BlockSpec((tm, d), lambda i, k: (i, 0)), act, act, pl.BlockSpec((None, fc, d), lambda i, k: (k, 0, 0))],
        out_specs=[act, act], plan=plan)


def _ffn_bwd_h(dy, x, gain, da, db, wg, wu, name, tiles, prev=None, plan=None):
    t, d = x.shape
    nch, fc, _ = wg.shape
    tm = min(512, t)
    nt = t // tm
    t0, t1 = tiles

    def body(*refs):
        dy_ref, x_ref, g_ref, da_ref, db_ref, wg_ref, wu_ref = refs[:7]
        dx_ref, dg_ref, acc_ref = refs[-3:]
        k = pl.program_id(1)

        @pl.when(k == 0)
        def _():
            acc_ref[...] = jnp.zeros_like(acc_ref)

        acc_ref[...] += _dot(da_ref[...], wg_ref[...]) + _dot(db_ref[...], wu_ref[...])

        @pl.when(k == nch - 1)
        def _():
            xf = x_ref[...]
            dxn, dgr = _rms_bwd(xf, _rms(xf), g_ref[...], acc_ref[...])
            dx_ref[...] = dy_ref[...] + dxn
            dg_ref[...] = jnp.sum(dgr, axis=0, keepdims=True)

    row = pl.BlockSpec((tm, d), lambda i, k: (i + t0, 0))
    chunk = pl.BlockSpec((None, fc, d), lambda i, k: (k, 0, 0))
    act = pl.BlockSpec((None, tm, fc), lambda i, k: (k, i + t0, 0))
    args = [dy, x, gain, da, db, wg, wu]
    in_specs = [row, row, pl.BlockSpec((1, d), lambda i, k: (0, 0)), act, act, chunk, chunk]
    aliases = {}
    if prev is not None:
        aliases = {len(args): 0, len(args) + 1: 1}
        args += list(prev)
        in_specs += [ANY, ANY]
    return _pallas(
        body, name=name, args=args, out_shape=[jax.ShapeDtypeStruct((t, d), F32), jax.ShapeDtypeStruct((nt, 1, d), F32)],
        grid=(t1 - t0, nch), in_specs=in_specs,
        out_specs=[row, pl.BlockSpec((None, 1, d), lambda i, k: (i + t0, 0, 0))],
        scratch_shapes=[pltpu.VMEM((tm, d), F32)], plan=plan, aliases=aliases)


def _ffn_bwd_x(dy, x, gain, a, b, wg, wu, wd, name, plan=None):
    t, d = x.shape
    nch, fc, _ = wg.shape
    tm = min(512, t)
    nt = t // tm

    def body(dy_ref, x_ref, g_ref, a_ref, b_ref, wg_ref, wu_ref, wd_ref, dx_ref, da_ref, db_ref, dg_ref, acc_ref):
        k = pl.program_id(1)

        @pl.when(k == 0)
        def _():
            acc_ref[...] = jnp.zeros_like(acc_ref)

        dab, dbb = _swiglu_grads(dy_ref, a_ref, b_ref, wd_ref)
        da_ref[...] = dab
        db_ref[...] = dbb
        acc_ref[...] += _dot(dab, wg_ref[...]) + _dot(dbb, wu_ref[...])

        @pl.when(k == nch - 1)
        def _():
            xf = x_ref[...]
            dxn, dgr = _rms_bwd(xf, _rms(xf), g_ref[...], acc_ref[...])
            dx_ref[...] = dy_ref[...] + dxn
            dg_ref[...] = jnp.sum(dgr, axis=0, keepdims=True)

    row = pl.BlockSpec((tm, d), lambda i, k: (i, 0))
    chunk = pl.BlockSpec((None, fc, d), lambda i, k: (k, 0, 0))
    act = pl.BlockSpec((None, tm, fc), lambda i, k: (k, i, 0))
    return _pallas(
        body, name=name, args=[dy, x, gain, a, b, wg, wu, wd],
        out_shape=[jax.ShapeDtypeStruct((t, d), F32), jax.ShapeDtypeStruct((nch, t, fc), BF16),
                   jax.ShapeDtypeStruct((nch, t, fc), BF16), jax.ShapeDtypeStruct((nt, 1, d), F32)],
        grid=(nt, nch),
        in_specs=[row, row, pl.BlockSpec((1, d), lambda i, k: (0, 0)), act, act, chunk, chunk, chunk],
        out_specs=[row, act, act, pl.BlockSpec((None, 1, d), lambda i, k: (i, 0, 0))],
        scratch_shapes=[pltpu.VMEM((tm, d), F32)], plan=plan)


def _ffn_bwd_w(pairs, name, plan=None):
    n = len(pairs)
    nch, t, fc = pairs[0][0].shape
    d = pairs[0][1].shape[1]
    tm = min(1024, t)

    def body(*refs):
        @pl.when(pl.program_id(1) == 0)
        def _():
            for o_ref in refs[2 * n:]:
                o_ref[...] = jnp.zeros_like(o_ref)

        for j, (_, _, scale) in enumerate(pairs):
            other = refs[n + j][...]
            if other.dtype != BF16:
                other = (scale * other).astype(BF16)
            refs[2 * n + j][...] += _dot_tn(refs[j][...], other)

    row = pl.BlockSpec((tm, d), lambda k, i: (i, 0))
    act = pl.BlockSpec((None, tm, fc), lambda k, i: (k, i, 0))
    chunk = pl.BlockSpec((None, fc, d), lambda k, i: (k, 0, 0))
    return _pallas(body, name=name, args=[p[0] for p in pairs] + [p[1] for p in pairs],
                   out_shape=[jax.ShapeDtypeStruct((nch, fc, d), F32)] * n, grid=(nch, t // tm),
                   in_specs=[act] * n + [row] * n, out_specs=[chunk] * n, plan=plan)


def _head_masks():
    lane = lax.broadcasted_iota(jnp.int32, (1, LANES), 1)
    return lane < HEAD_DIM


def _head_rms(x, lo):
    x2 = x * x
    s0 = jnp.sum(jnp.where(lo, x2, 0.0), axis=1, keepdims=True)
    s1 = jnp.sum(jnp.where(lo, 0.0, x2), axis=1, keepdims=True)
    return jnp.where(lo, lax.rsqrt(s0 * (1.0 / HEAD_DIM) + EPS), lax.rsqrt(s1 * (1.0 / HEAD_DIM) + EPS))


def _head_mean(v, lo):
    s0 = jnp.sum(jnp.where(lo, v, 0.0), axis=1, keepdims=True)
    s1 = jnp.sum(jnp.where(lo, 0.0, v), axis=1, keepdims=True)
    return jnp.where(lo, s0, s1) * (1.0 / HEAD_DIM)


def _mix_proj(x1, gain, wt, qn, kn, pool_width, attn_width):
    t, d = x1.shape
    tm = min(512, t)
    nt = t // tm
    scale = HEAD_DIM ** -0.5
    c_q, c_k, c_v = pool_width, pool_width + attn_width, pool_width + 2 * attn_width
    c_f = c_v + attn_width

    def body(x_ref, g_ref, wt_ref, qn_ref, kn_ref, hm_ref, pv_ref, q_ref, k_ref, qh_ref, kh_ref, vb_ref, f_ref):
        xf = x_ref[...]
        hm = ((xf * _rms(xf)) * g_ref[...]).astype(BF16)
        hm_ref[...] = hm
        f_ref[...] = _dot_nt(hm, wt_ref[c_f:c_f + LANES, :])
        pv_ref[...] = _dot_nt(hm, wt_ref[0:pool_width, :])
        vb_ref[...] = _dot_nt(hm, wt_ref[c_v:c_v + attn_width, :]).astype(BF16)
        lo = _head_masks()
        for c0, raw_ref, hat_ref, n_ref, mul in ((c_q, q_ref, qh_ref, qn_ref, scale), (c_k, k_ref, kh_ref, kn_ref, 1.0)):
            raw = _dot_nt(hm, wt_ref[c0:c0 + attn_width, :])
            raw_ref[...] = raw
            for blk in range(attn_width // LANES):
                sl = slice(blk * LANES, (blk + 1) * LANES)
                xb = raw[:, sl]
                hat_ref[:, sl] = (((xb * _head_rms(xb, lo)) * n_ref[:, sl]) * mul).astype(BF16)

    row = pl.BlockSpec((tm, d), lambda i: (i, 0))
    half = pl.BlockSpec((tm, attn_width), lambda i: (i, 0))
    const = lambda shape: pl.BlockSpec(shape, lambda i: (0, 0))
    return pl.pallas_call(
        body,
        out_shape=[jax.ShapeDtypeStruct((t, d), BF16), jax.ShapeDtypeStruct((t, pool_width), F32),
                   jax.ShapeDtypeStruct((t, attn_width), F32), jax.ShapeDtypeStruct((t, attn_width), F32),
                   jax.ShapeDtypeStruct((t, attn_width), BF16), jax.ShapeDtypeStruct((t, attn_width), BF16),
                   jax.ShapeDtypeStruct((t, attn_width), BF16), jax.ShapeDtypeStruct((t, LANES), F32)],
        grid=(nt,),
        in_specs=[row, const((1, d)), const(wt.shape), const((1, attn_width)), const((1, attn_width))],
        out_specs=[row, pl.BlockSpec((tm, pool_width), lambda i: (i, 0)), half, half, half, half, half,
                   pl.BlockSpec((tm, LANES), lambda i: (i, 0))],
        compiler_params=_params(), name="mix_proj",
    )(x1, gain, wt, qn, kn)


def _shift_down(v, dist, row):
    return jnp.where(row >= dist, pltpu.roll(v, dist, 0), 0.0)


def _shift_up(v, dist, row, n):
    return jnp.where(row + dist < n, pltpu.roll(v, n - dist, 0), 0.0)


def _aug_lane(e):
    return HEAD_DIM if e == 0 else 0


def _forget_prefix(f, bias, qh, kh, n_batch, seq):
    def body(f_ref, b_ref, q_ref, k_ref, qa_ref, ka_ref):
        z = f_ref[...] + b_ref[...]
        acc = jnp.minimum(z, 0.0) - jnp.log(1.0 + jnp.exp(-jnp.abs(z)))
        row = lax.broadcasted_iota(jnp.int32, (seq, 1), 0)
        dist = 1
        while dist < seq:
            acc = acc + _shift_down(acc, dist, row)
            dist *= 2
        lane = lax.broadcasted_iota(jnp.int32, (1, LANES), 1)
        for h in range(N_HEADS):
            pair, e = divmod(h, 2)
            a0 = _aug_lane(e)
            own = (lane < HEAD_DIM) if e == 0 else (lane >= HEAD_DIM)
            fh = _pick_lane(acc, h)
            hi = fh.astype(BF16).astype(F32)
            rest = fh - hi
            mid = rest.astype(BF16).astype(F32)
            low = rest - mid
            q_ones = (lane >= a0 + 3) & (lane < a0 + 6)
            k_ones = (lane >= a0) & (lane < a0 + 3)
            q_aug = jnp.where(lane == a0, hi, jnp.where(lane == a0 + 1, mid, jnp.where(lane == a0 + 2, low,
                              jnp.where(q_ones, 1.0, 0.0))))
            k_aug = jnp.where(k_ones, 1.0, jnp.where(lane == a0 + 3, -hi, jnp.where(lane == a0 + 4, -mid,
                              jnp.where(lane == a0 + 5, -low, 0.0))))
            src = slice(pair * LANES, (pair + 1) * LANES)
            dst = slice(h * LANES, (h + 1) * LANES)
            qa_ref[:, dst] = jnp.where(own, q_ref[:, src].astype(F32), q_aug).astype(BF16)
            ka_ref[:, dst] = jnp.where(own, k_ref[:, src].astype(F32), k_aug).astype(BF16)

    width = qh.shape[1]
    tok = pl.BlockSpec((seq, width), lambda b: (b, 0))
    aug = pl.BlockSpec((seq, N_HEADS * LANES), lambda b: (b, 0))
    return pl.pallas_call(
        body, out_shape=[jax.ShapeDtypeStruct((n_batch * seq, N_HEADS * LANES), BF16)] * 2, grid=(n_batch,),
        in_specs=[pl.BlockSpec((seq, LANES), lambda b: (b, 0)), pl.BlockSpec((1, LANES), lambda b: (0, 0)), tok, tok],
        out_specs=[aug, aug], compiler_params=_params(), name="forget_prefix",
    )(f, bias, qh, kh)


def _pool_groups(pv_ref, pw_ref, ps_ref, seq):
    row = lax.broadcasted_iota(jnp.int32, (seq, 1), 0)
    pos = (row + 1).astype(F32)
    out = []
    for g, win in enumerate(POOL_WINDOWS):
        sl = slice(g * LANES, (g + 1) * LANES)
        xg = pv_ref[:, sl]
        acc = xg
        dist = 1
        while dist < win:
            acc = acc + _shift_down(acc, dist, row)
            dist *= 2
        pooled = (acc / jnp.minimum(pos, float(win)) - xg).astype(BF16)
        mixed = _dot(pooled, pw_ref[g])
        out.append((pooled, mixed, mixed * ps_ref[:, sl]))
    return out


def _pool_fwd(pv, pw, ps, onp, n_batch, seq):
    width = pv.shape[1]

    def body(pv_ref, pw_ref, ps_ref, on_ref, y_ref):
        groups = _pool_groups(pv_ref, pw_ref, ps_ref, seq)
        ssq = sum(jnp.sum(ms * ms, axis=1, keepdims=True) for _, _, ms in groups)
        r = lax.rsqrt(ssq * (1.0 / width) + EPS)
        for g, (_, _, ms) in enumerate(groups):
            sl = slice(g * LANES, (g + 1) * LANES)
            y_ref[:, sl] = ((ms * r) * on_ref[:, sl]).astype(BF16)

    return pl.pallas_call(
        body, out_shape=jax.ShapeDtypeStruct((n_batch * seq, width), BF16), grid=(n_batch,),
        in_specs=[pl.BlockSpec((seq, width), lambda b: (b, 0)), pl.BlockSpec(pw.shape, lambda b: (0, 0, 0)),
                  pl.BlockSpec((1, width), lambda b: (0, 0)), pl.BlockSpec((1, width), lambda b: (0, 0))],
        out_specs=pl.BlockSpec((seq, width), lambda b: (b, 0)),
        compiler_params=_params(), name="pool_fwd",
    )(pv, pw, ps, onp)


def _pool_bwd(pv, dyp, pw, ps, onp, n_batch, seq):
    width = pv.shape[1]

    def body(pv_ref, dy_ref, pw_ref, ps_ref, on_ref, dpv_ref, dpw_ref, dps_ref, don_ref):
        groups = _pool_groups(pv_ref, pw_ref, ps_ref, seq)
        ssq = sum(jnp.sum(ms * ms, axis=1, keepdims=True) for _, _, ms in groups)
        r = lax.rsqrt(ssq * (1.0 / width) + EPS)
        mean = sum(jnp.sum((dy_ref[:, g * LANES:(g + 1) * LANES] * on_ref[:, g * LANES:(g + 1) * LANES]) * (ms * r),
                           axis=1, keepdims=True) for g, (_, _, ms) in enumerate(groups)) * (1.0 / width)
        row = lax.broadcasted_iota(jnp.int32, (seq, 1), 0)
        pos = (row + 1).astype(F32)
        for g, (pooled, mixed, ms) in enumerate(groups):
            sl = slice(g * LANES, (g + 1) * LANES)
            dy = dy_ref[:, sl]
            xh = ms * r
            don_ref[:, sl] = jnp.sum(dy * xh, axis=0, keepdims=True)
            dms = r * (dy * on_ref[:, sl] - xh * mean)
            dps_ref[:, sl] = jnp.sum(dms * mixed, axis=0, keepdims=True)
            dmix = (dms * ps_ref[:, sl]).astype(BF16)
            dpw_ref[g] = _dot_tn(pooled, dmix)
            dpool = _dot_nt(dmix, pw_ref[g])
            win = POOL_WINDOWS[g]
            acc = dpool / jnp.minimum(pos, float(win))
            dist = 1
            while dist < win:
                acc = acc + _shift_up(acc, dist, row, seq)
                dist *= 2
            dpv_ref[:, sl] = (acc - dpool).astype(BF16)

    tok = pl.BlockSpec((seq, width), lambda b: (b, 0))
    vec = pl.BlockSpec((1, width), lambda b: (0, 0))
    pvec = pl.BlockSpec((None, 1, width), lambda b: (b, 0, 0))
    return pl.pallas_call(
        body,
        out_shape=[jax.ShapeDtypeStruct((n_batch * seq, width), BF16),
                   jax.ShapeDtypeStruct((n_batch,) + pw.shape, F32),
                   jax.ShapeDtypeStruct((n_batch, 1, width), F32), jax.ShapeDtypeStruct((n_batch, 1, width), F32)],
        grid=(n_batch,),
        in_specs=[tok, tok, pl.BlockSpec(pw.shape, lambda b: (0, 0, 0)), vec, vec],
        out_specs=[tok, pl.BlockSpec((None,) + pw.shape, lambda b: (b, 0, 0, 0)), pvec, pvec],
        compiler_params=_params(), name="pool_bwd",
    )(pv, dyp, pw, ps, onp)


def _pick_lane(tile, idx):
    lane = lax.broadcasted_iota(jnp.int32, (1, LANES), 1)
    return jnp.sum(jnp.where(lane == idx, tile, 0.0), axis=1, keepdims=True)


def _pick_row(tile, idx):
    sub = lax.broadcasted_iota(jnp.int32, (tile.shape[0], 1), 0)
    return jnp.sum(jnp.where(sub == idx, tile, 0.0), axis=0, keepdims=True)


def _put_lane(col, idx):
    lane = lax.broadcasted_iota(jnp.int32, (1, LANES), 1)
    return jnp.where(lane == idx, col, 0.0)


def _head_select(e):
    lo = _head_masks()
    return lo if e == 0 else jnp.logical_not(lo)


def _causal(st, shift):
    row = lax.broadcasted_iota(jnp.int32, st.shape, 0)
    col = lax.broadcasted_iota(jnp.int32, st.shape, 1) + shift
    return jnp.where(col >= row, st, NEG)


def _transpose_blocks(a):
    rows, cols = a.shape
    return jnp.concatenate(
        [jnp.concatenate([a[r:r + LANES, c:c + LANES].T for r in range(0, rows, LANES)], axis=1)
         for c in range(0, cols, LANES)], axis=0)


def _stat_rows(ref, head, nsub):
    return jnp.concatenate([_pick_row(ref[a], head) for a in range(nsub)], axis=1)


def _accumulate(ref, value, first):
    @pl.when(first)
    def _():
        ref[...] = value

    @pl.when(jnp.logical_not(first))
    def _():
        ref[...] += value


def _attn_fwd(qa, ka, vb, n_batch, seq, plan=None):
    tq = min(ATT_BLOCK, seq)
    nq, nsub, tk = seq // tq, tq // ATT_SUB, tq
    pairs = vb.shape[1] // LANES

    def body(q_ref, k_ref, v_ref, o_ref, lse_ref, acc_ref):
        i, p = pl.program_id(1), pl.program_id(2)
        row_lo = lax.broadcasted_iota(jnp.int32, (LANES, 1), 0) < HEAD_DIM
        qs = [q_ref[:, e * LANES:(e + 1) * LANES] for e in range(2)]
        acc_ref[...] = jnp.zeros_like(acc_ref)

        def tile(off, stats, diagonal):
            vj = v_ref[pl.ds(off, tk), :]
            new, alphas, pvs = [], [], []
            for e in range(2):
                st = _dot_nt(k_ref[pl.ds(off, tk), e * LANES:(e + 1) * LANES], qs[e])
                if diagonal:
                    st = _causal(st, 0)
                m, l = stats[e]
                m_new = jnp.maximum(m, jnp.max(st, axis=0, keepdims=True))
                alpha = jnp.exp(m - m_new)
                pt = jnp.exp(st - m_new)
                new.append((m_new, alpha * l + jnp.sum(pt, axis=0, keepdims=True)))
                alphas.append(alpha)
                pvs.append(_dot_tn(jnp.where(_head_select(e), vj, jnp.zeros_like(vj)), pt.astype(BF16)))
            acc_ref[...] = acc_ref[...] * jnp.where(row_lo, alphas[0], alphas[1]) + (pvs[0] + pvs[1])
            return tuple(new)

        init = ((jnp.full((1, tq), NEG, F32), jnp.zeros((1, tq), F32)),) * 2
        stats = lax.fori_loop(0, i, lambda j, st: tile(pl.multiple_of(j * tk, tk), st, False), init)
        (m0, l0), (m1, l1) = tile(pl.multiple_of(i * tk, tk), stats, True)
        out_t = acc_ref[...] / jnp.where(row_lo, l0, l1)
        sub = lax.broadcasted_iota(jnp.int32, (8, 1), 0)
        lse0, lse1 = m0 + jnp.log(l0), m1 + jnp.log(l1)
        for a in range(nsub):
            sl = slice(a * ATT_SUB, (a + 1) * ATT_SUB)
            o_ref[sl, :] = out_t[:, sl].T
            rows = jnp.where(sub == 2 * p, lse0[:, sl], 0.0) + jnp.where(sub == 2 * p + 1, lse1[:, sl], 0.0)
            _accumulate(lse_ref.at[a], rows, p == 0)

    return _pallas(
        body, name="attn_fwd", args=[qa, ka, vb],
        out_shape=[jax.ShapeDtypeStruct((n_batch * seq, pairs * LANES), F32),
                   jax.ShapeDtypeStruct((n_batch * seq // ATT_SUB, 8, ATT_SUB), F32)],
        grid=(n_batch, nq, pairs),
        in_specs=[pl.BlockSpec((tq, 2 * LANES), lambda b, i, p: (b * nq + i, p)),
                  pl.BlockSpec((seq, 2 * LANES), lambda b, i, p: (b, p)),
                  pl.BlockSpec((seq, LANES), lambda b, i, p: (b, p))],
        out_specs=[pl.BlockSpec((tq, LANES), lambda b, i, p: (b * nq + i, p)),
                   pl.BlockSpec((nsub, 8, ATT_SUB), lambda b, i, p: (b * nq + i, 0, 0))],
        scratch_shapes=[pltpu.VMEM((LANES, tq), F32)], plan=plan)


def _attn_bwd_q(qa, ka, vb, do, lse, delta, n_batch, seq, plan=None):
    tq = min(ATT_BLOCK, seq)
    nq, nsub, tk = seq // tq, tq // ATT_SUB, tq
    pairs = vb.shape[1] // LANES

    def body(q_ref, k_ref, v_ref, do_ref, lse_ref, dl_ref, dq_ref, dfq_ref, acc0_ref, acc1_ref):
        i, p = pl.program_id(1), pl.program_id(2)
        accs = (acc0_ref, acc1_ref)
        qs = [q_ref[:, e * LANES:(e + 1) * LANES] for e in range(2)]
        dov = do_ref[...]
        ls = [_stat_rows(lse_ref, 2 * p + e, nsub) for e in range(2)]
        dl = [_stat_rows(dl_ref, 2 * p + e, nsub) for e in range(2)]
        for acc in accs:
            acc[...] = jnp.zeros_like(acc)

        def tile(off, diagonal):
            vj = v_ref[pl.ds(off, tk), :]
            for e in range(2):
                kj = k_ref[pl.ds(off, tk), e * LANES:(e + 1) * LANES]
                st = _dot_nt(kj, qs[e])
                if diagonal:
                    st = _causal(st, 0)
                pt = jnp.exp(st - ls[e])
                dpt = _dot_nt(jnp.where(_head_select(e), vj, jnp.zeros_like(vj)), dov)
                accs[e][...] += _dot(_transpose_blocks(kj), (pt * (dpt - dl[e])).astype(BF16))

        def step(j, carry):
            tile(pl.multiple_of(j * tk, tk), False)
            return carry

        lax.fori_loop(0, i, step, 0)
        tile(pl.multiple_of(i * tk, tk), True)
        dq0, dq1 = _transpose_blocks(acc0_ref[...]), _transpose_blocks(acc1_ref[...])
        dq_ref[...] = jnp.where(_head_masks(), dq0, dq1)
        dfq = _put_lane(_pick_lane(dq0, _aug_lane(0)), 2 * p) + _put_lane(_pick_lane(dq1, _aug_lane(1)), 2 * p + 1)
        _accumulate(dfq_ref, dfq, p == 0)

    stat = pl.BlockSpec((nsub, 8, ATT_SUB), lambda b, i, p: (b * nq + i, 0, 0))
    blk = pl.BlockSpec((tq, LANES), lambda b, i, p: (b * nq + i, p))
    return _pallas(
        body, name="attn_bwd_q", args=[qa, ka, vb, do, lse, delta],
        out_shape=[jax.ShapeDtypeStruct((n_batch * seq, pairs * LANES), F32), jax.ShapeDtypeStruct((n_batch * seq, LANES), F32)],
        grid=(n_batch, nq, pairs),
        in_specs=[pl.BlockSpec((tq, 2 * LANES), lambda b, i, p: (b * nq + i, p)),
                  pl.BlockSpec((seq, 2 * LANES), lambda b, i, p: (b, p)),
                  pl.BlockSpec((seq, LANES), lambda b, i, p: (b, p)), blk, stat, stat],
        out_specs=[blk, pl.BlockSpec((tq, LANES), lambda b, i, p: (b * nq + i, 0))],
        scratch_shapes=[pltpu.VMEM((LANES, tq), F32), pltpu.VMEM((LANES, tq), F32)], plan=plan)


def _attn_bwd_kv(qa, ka, vb, do, lse, delta, n_batch, seq, plan=None):
    tkb = min(ATT_BLOCK, seq)
    nk, nsub, tq = seq // tkb, tkb // ATT_SUB, tkb
    n_tiles = seq // ATT_SUB
    pairs = vb.shape[1] // LANES

    def body(q_ref, k_ref, v_ref, do_ref, lse_ref, dl_ref, dk_ref, dv_ref, dfk_ref, dk0_ref, dk1_ref, dva_ref):
        j, p = pl.program_id(1), pl.program_id(2)
        dks = (dk0_ref, dk1_ref)
        ks = [k_ref[:, e * LANES:(e + 1) * LANES] for e in range(2)]
        vj = v_ref[...]
        vs = [jnp.where(_head_select(e), vj, jnp.zeros_like(vj)) for e in range(2)]
        for acc in (dk0_ref, dk1_ref, dva_ref):
            acc[...] = jnp.zeros_like(acc)

        def tile(t, diagonal):
            off = pl.multiple_of(t * tq, tq)
            dov = do_ref[pl.ds(off, tq), :]
            for e in range(2):
                qe = q_ref[pl.ds(off, tq), e * LANES:(e + 1) * LANES]
                st = _dot_nt(ks[e], qe)
                if diagonal:
                    st = _causal(st, 0)
                rows = lambda ref: jnp.concatenate([_pick_row(ref[t * nsub + a], 2 * p + e) for a in range(nsub)], axis=1)
                pt = jnp.exp(st - rows(lse_ref))
                dva_ref[...] += _dot(pt.astype(BF16), jnp.where(_head_select(e), dov, jnp.zeros_like(dov)))
                dst = pt * (_dot_nt(vs[e], dov) - rows(dl_ref))
                dks[e][...] += _dot(dst.astype(BF16), qe)

        def step(t, carry):
            tile(t, False)
            return carry

        lax.fori_loop(j + 1, nk, step, 0)
        tile(j, True)
        dk0, dk1 = dk0_ref[...], dk1_ref[...]
        dk_ref[...] = jnp.where(_head_masks(), dk0, dk1)
        dv_ref[...] = dva_ref[...].astype(BF16)
        dfk = (_put_lane(_pick_lane(dk0, _aug_lane(0) + 3), 2 * p)
               + _put_lane(_pick_lane(dk1, _aug_lane(1) + 3), 2 * p + 1))
        _accumulate(dfk_ref, -dfk, p == 0)

    stat = pl.BlockSpec((n_tiles, 8, ATT_SUB), lambda b, j, p: (b, 0, 0))
    blk = pl.BlockSpec((tkb, LANES), lambda b, j, p: (b * nk + j, p))
    acc = pltpu.VMEM((tkb, LANES), F32)
    return _pallas(
        body, name="attn_bwd_kv", args=[qa, ka, vb, do, lse, delta],
        out_shape=[jax.ShapeDtypeStruct((n_batch * seq, pairs * LANES), F32),
                   jax.ShapeDtypeStruct((n_batch * seq, pairs * LANES), BF16),
                   jax.ShapeDtypeStruct((n_batch * seq, LANES), F32)],
        grid=(n_batch, nk, pairs),
        in_specs=[pl.BlockSpec((seq, 2 * LANES), lambda b, j, p: (b, p)),
                  pl.BlockSpec((tkb, 2 * LANES), lambda b, j, p: (b * nk + j, p)), blk,
                  pl.BlockSpec((seq, LANES), lambda b, j, p: (b, p)), stat, stat],
        out_specs=[blk, blk, pl.BlockSpec((tkb, LANES), lambda b, j, p: (b * nk + j, 0))],
        scratch_shapes=[acc, acc, acc], plan=plan)


def _forget_bwd(dfq, dfk, f, bias, n_batch, seq):
    def body(dfq_ref, dfk_ref, f_ref, b_ref, df_ref, db_ref):
        acc = dfq_ref[...] + dfk_ref[...]
        row = lax.broadcasted_iota(jnp.int32, (seq, 1), 0)
        dist = 1
        while dist < seq:
            acc = acc + _shift_up(acc, dist, row, seq)
            dist *= 2
        df = acc * _sigmoid(-(f_ref[...] + b_ref[...]))
        df_ref[...] = df
        db_ref[...] = jnp.sum(df, axis=0, keepdims=True)

    col = pl.BlockSpec((seq, LANES), lambda b: (b, 0))
    return pl.pallas_call(
        body,
        out_shape=[jax.ShapeDtypeStruct((n_batch * seq, LANES), F32), jax.ShapeDtypeStruct((n_batch, 1, LANES), F32)],
        grid=(n_batch,), in_specs=[col, col, col, pl.BlockSpec((1, LANES), lambda b: (0, 0))],
        out_specs=[col, pl.BlockSpec((None, 1, LANES), lambda b: (b, 0, 0))],
        compiler_params=_params(), name="forget_bwd",
    )(dfq, dfk, f, bias)


def _mix_out(x1, yp, o, ona, woa, wob):
    t, d = x1.shape
    width = o.shape[1]
    tm = min(512, t)

    def body(x_ref, yp_ref, o_ref, on_ref, wa_ref, wb_ref, x2_ref, ya_ref):
        of = o_ref[...]
        ya = ((of * _rms(of)) * on_ref[...]).astype(BF16)
        ya_ref[...] = ya
        x2_ref[...] = x_ref[...] + (_dot(yp_ref[...], wa_ref[...]) + _dot(ya, wb_ref[...]))

    row = pl.BlockSpec((tm, d), lambda i: (i, 0))
    half = pl.BlockSpec((tm, width), lambda i: (i, 0))
    wspec = pl.BlockSpec((width, d), lambda i: (0, 0))
    return pl.pallas_call(
        body, out_shape=[jax.ShapeDtypeStruct((t, d), F32), jax.ShapeDtypeStruct((t, width), BF16)],
        grid=(t // tm,), in_specs=[row, half, half, pl.BlockSpec((1, width), lambda i: (0, 0)), wspec, wspec],
        out_specs=[row, half], compiler_params=_params(), name="mix_out",
    )(x1, yp, o, ona, woa, wob)


def _mix_out_bwd(dx2, o, yp, ya, ona, woa, wob, plan=None):
    t, d = dx2.shape
    width = o.shape[1]
    tm = min(512, t)
    nt = t // tm

    def body(dx_ref, o_ref, yp_ref, ya_ref, on_ref, wa_ref, wb_ref, dyp_ref, do_ref, dl_ref, dwa_ref, dwb_ref, don_ref):
        @pl.when(pl.program_id(0) == 0)
        def _():
            dwa_ref[...] = jnp.zeros_like(dwa_ref)
            dwb_ref[...] = jnp.zeros_like(dwb_ref)

        dxb = dx_ref[...].astype(BF16)
        dwa_ref[...] += _dot_tn(yp_ref[...], dxb)
        dwb_ref[...] += _dot_tn(ya_ref[...], dxb)
        dyp_ref[...] = _dot_nt(dxb, wa_ref[...])
        of = o_ref[...]
        dov, dgr = _rms_bwd(of, _rms(of), on_ref[...], _dot_nt(dxb, wb_ref[...]))
        don_ref[...] = jnp.sum(dgr, axis=0, keepdims=True)
        do_ref[...] = dov.astype(BF16)
        lo = _head_masks()
        prod = dov * of
        delta = jnp.zeros((tm, LANES), F32)
        for blk in range(width // LANES):
            pb = prod[:, blk * LANES:(blk + 1) * LANES]
            delta = delta + _put_lane(jnp.sum(jnp.where(lo, pb, 0.0), axis=1, keepdims=True), 2 * blk)
            delta = delta + _put_lane(jnp.sum(jnp.where(lo, 0.0, pb), axis=1, keepdims=True), 2 * blk + 1)
        for c in range(tm // ATT_SUB):
            dl_ref[c] = delta[c * ATT_SUB:(c + 1) * ATT_SUB, :].T[0:8, :]

    row = pl.BlockSpec((tm, d), lambda i: (i, 0))
    half = pl.BlockSpec((tm, width), lambda i: (i, 0))
    wspec = pl.BlockSpec((width, d), lambda i: (0, 0))
    return _pallas(
        body, name="mix_out_bwd", args=[dx2, o, yp, ya, ona, woa, wob],
        out_shape=[jax.ShapeDtypeStruct((t, width), F32), jax.ShapeDtypeStruct((t, width), BF16),
                   jax.ShapeDtypeStruct((t // ATT_SUB, 8, ATT_SUB), F32), jax.ShapeDtypeStruct((width, d), F32),
                   jax.ShapeDtypeStruct((width, d), F32), jax.ShapeDtypeStruct((nt, 1, width), F32)],
        grid=(nt,),
        in_specs=[row, half, half, half, pl.BlockSpec((1, width), lambda i: (0, 0)), wspec, wspec],
        out_specs=[half, half, pl.BlockSpec((tm // ATT_SUB, 8, ATT_SUB), lambda i: (i, 0, 0)), wspec, wspec,
                   pl.BlockSpec((None, 1, width), lambda i: (i, 0, 0))], plan=plan)


def _mix_in_bwd(dx2, x1, gain, hm, dpv, dqh, q, dkh, k, dv, df, qn, kn, wt):
    t, d = x1.shape
    width = q.shape[1]
    pool_width = dpv.shape[1]
    tm = min(512, t)
    nt = t // tm
    scale = HEAD_DIM ** -0.5
    c_q, c_k, c_v = pool_width, pool_width + width, pool_width + 2 * width
    c_f = c_v + width

    def body(dx2_ref, x_ref, g_ref, hm_ref, dpv_ref, dqh_ref, q_ref, dkh_ref, k_ref, dv_ref, df_ref, qn_ref, kn_ref,
             wt_ref, dx_ref, dwt_ref, dg_ref, dqn_ref, dkn_ref):
        @pl.when(pl.program_id(0) == 0)
        def _():
            dwt_ref[...] = jnp.zeros_like(dwt_ref)

        lo = _head_masks()
        hm = hm_ref[...]
        pieces = [(0, dpv_ref[...])]
        for c0, raw_ref, dh_ref, n_ref, dn_ref, mul in ((c_q, q_ref, dqh_ref, qn_ref, dqn_ref, scale),
                                                       (c_k, k_ref, dkh_ref, kn_ref, dkn_ref, 1.0)):
            cols = []
            for blk in range(width // LANES):
                sl = slice(blk * LANES, (blk + 1) * LANES)
                xb = raw_ref[:, sl]
                gb = dh_ref[:, sl] * mul
                r = _head_rms(xb, lo)
                xh = xb * r
                dyg = gb * n_ref[:, sl]
                cols.append((r * (dyg - xh * _head_mean(dyg * xh, lo))).astype(BF16))
                dn_ref[:, sl] = jnp.sum(gb * xh, axis=0, keepdims=True)
            pieces.append((c0, jnp.concatenate(cols, axis=1)))
        pieces.append((c_v, dv_ref[...]))
        pieces.append((c_f, df_ref[...].astype(BF16)))
        dhm = jnp.zeros((tm, d), F32)
        for c0, piece in pieces:
            dwt_ref[c0:c0 + piece.shape[1], :] += _dot_tn(piece, hm)
            dhm = dhm + _dot(piece, wt_ref[c0:c0 + piece.shape[1], :])
        xf = x_ref[...]
        dxn, dgr = _rms_bwd(xf, _rms(xf), g_ref[...], dhm)
        dx_ref[...] = dx2_ref[...] + dxn
        dg_ref[...] = jnp.sum(dgr, axis=0, keepdims=True)

    row = pl.BlockSpec((tm, d), lambda i: (i, 0))
    half = pl.BlockSpec((tm, width), lambda i: (i, 0))
    const = lambda shape: pl.BlockSpec(shape, lambda i: (0, 0))
    pvec = lambda n: pl.BlockSpec((None, 1, n), lambda i: (i, 0, 0))
    return pl.pallas_call(
        body,
        out_shape=[jax.ShapeDtypeStruct((t, d), F32), jax.ShapeDtypeStruct(wt.shape, F32),
                   jax.ShapeDtypeStruct((nt, 1, d), F32),
                   jax.ShapeDtypeStruct((nt, 1, width), F32), jax.ShapeDtypeStruct((nt, 1, width), F32)],
        grid=(nt,),
        in_specs=[row, row, const((1, d)), row, pl.BlockSpec((tm, pool_width), lambda i: (i, 0)), half, half, half, half,
                  half, pl.BlockSpec((tm, LANES), lambda i: (i, 0)), const((1, width)), const((1, width)),
                  const(wt.shape)],
        out_specs=[row, const(wt.shape), pvec(d), pvec(width), pvec(width)],
        compiler_params=_params(), name="mix_in_bwd",
    )(dx2, x1, gain, hm, dpv, dqh, q, dkh, k, dv, df, qn, kn, wt)


def _mesh_pos():
    return lax.axis_index("x"), lax.axis_index("y"), lax.axis_index("c")


def _other_chips(x, y):
    return [(1 - x, y), (x, 1 - y), (1 - x, 1 - y)]


def _remote(src, dst, send_sem, recv_sem, device):
    return pltpu.make_async_remote_copy(src_ref=src, dst_ref=dst, send_sem=send_sem, recv_sem=recv_sem,
                                        device_id=device, device_id_type=pl.DeviceIdType.MESH)


def _half_rows(n_rows, which):
    half = n_rows // 2
    return pl.ds(pl.multiple_of(which * half, 8), half)


def _row_block(rows, cols, itemsize=4):
    rb = rows
    while rb * cols * itemsize > (1 << 20) and rb % 32 == 0:
        rb //= 2
    return rb


def _place_cast(ws, chip, tag):
    n = len(ws)
    rows, cols = ws[0].shape
    rb = _row_block(rows, cols)

    def body(k_ref, *refs):
        for w_ref, o_ref in zip(refs[:n], refs[n:]):
            o_ref[...] = w_ref[...].astype(BF16)

    return pl.pallas_call(
        body, out_shape=[jax.ShapeDtypeStruct((N_CHIPS, rows, cols), BF16)] * n,
        grid_spec=pltpu.PrefetchScalarGridSpec(
            num_scalar_prefetch=1, grid=(rows // rb,),
            in_specs=[pl.BlockSpec((rb, cols), lambda i, k: (i, 0))] * n,
            out_specs=[pl.BlockSpec((None, rb, cols), lambda i, k: (k[0], i, 0))] * n),
        compiler_params=_params(), name="place_" + tag,
    )(chip, *ws)


class _Plan:
    def __init__(self, ins, outs, alias, sems, start, finish):
        self.ins, self.outs, self.alias, self.sems, self.start, self.finish = ins, outs, alias, sems, start, finish


def _merge_plans(a, b):
    ni, no, ns = len(a.ins), len(a.outs), len(a.sems)
    alias = dict(a.alias)
    alias.update({ni + i: no + o for i, o in b.alias.items()})

    def both(which):
        def run(ins, outs, sems):
            getattr(a, which)(ins[:ni], outs[:no], sems[:ns])
            getattr(b, which)(ins[ni:], outs[no:], sems[ns:])
        return run

    return _Plan(list(a.ins) + list(b.ins), list(a.outs) + list(b.outs), alias, list(a.sems) + list(b.sems),
                 both("start"), both("finish"))


def _run_plan(plan, name):
    n_in, n_out = len(plan.ins), len(plan.outs)

    def body(*refs):
        parts = refs[:n_in], refs[n_in:n_in + n_out], refs[n_in + n_out:]
        plan.start(*parts)
        plan.finish(*parts)

    return pl.pallas_call(
        body, out_shape=plan.outs, in_specs=[ANY] * n_in, out_specs=[ANY] * n_out, scratch_shapes=plan.sems,
        input_output_aliases=plan.alias, name=name,
    )(*plan.ins)


def _pallas(body, *, name, args, in_specs, out_shape, out_specs, grid, scratch_shapes=(), plan=None, aliases=None):
    n_in, n_out, n_scr = len(args), len(out_shape), len(scratch_shapes)
    aliases = dict(aliases or {})
    if plan is None:
        res = pl.pallas_call(body, out_shape=out_shape, grid=grid, in_specs=in_specs, out_specs=out_specs,
                             scratch_shapes=scratch_shapes, input_output_aliases=aliases,
                             compiler_params=_params(), name=name)(*args)
        return list(res), []
    p_in, p_out = len(plan.ins), len(plan.outs)

    def carrying(*refs):
        ins, p_ins = refs[:n_in], refs[n_in:n_in + p_in]
        o0 = n_in + p_in
        outs, p_outs = refs[o0:o0 + n_out], refs[o0 + n_out:o0 + n_out + p_out]
        s0 = o0 + n_out + p_out
        scr, p_sems = refs[s0:s0 + n_scr], refs[s0 + n_scr:]
        ids = [pl.program_id(a) for a in range(len(grid))]
        first = functools.reduce(jnp.logical_and, [i == 0 for i in ids])
        last = functools.reduce(jnp.logical_and, [i == g - 1 for i, g in zip(ids, grid)])

        @pl.when(first)
        def _():
            plan.start(p_ins, p_outs, p_sems)

        body(*ins, *outs, *scr)

        @pl.when(last)
        def _():
            plan.finish(p_ins, p_outs, p_sems)

    res = pl.pallas_call(
        carrying, out_shape=list(out_shape) + list(plan.outs), grid=grid,
        in_specs=list(in_specs) + [ANY] * p_in, out_specs=list(out_specs) + [ANY] * p_out,
        scratch_shapes=list(scratch_shapes) + list(plan.sems),
        input_output_aliases={**aliases, **{n_in + i: n_out + o for i, o in plan.alias.items()}},
        compiler_params=_params(), name=name,
    )(*args, *plan.ins)
    return list(res[:n_out]), list(res[n_out:])


def _plan_gather(stacks):
    n = len(stacks)

    def ici_copies(outs, sems):
        x, y, c = _mesh_pos()
        cps = []
        for w in range(n):
            own = outs[w].at[2 * x + y, _half_rows(stacks[w].shape[1], c)]
            cps += [_remote(own, own, sems[0].at[w, j], sems[1].at[w, j], (*chip, c)) for j, chip in enumerate(_other_chips(x, y))]
        return cps

    def start(ins, outs, sems):
        for cp in ici_copies(outs, sems):
            cp.start()

    def finish(ins, outs, sems):
        ici_send, ici_recv, d2d_send, d2d_recv = sems
        x, y, c = _mesh_pos()
        sibling = (x, y, 1 - c)
        slots = [2 * cx + cy for cx, cy in _other_chips(x, y)]
        forwards = []
        for w in range(n):
            rows = _half_rows(stacks[w].shape[1], c)
            for j in range(3):
                landed = outs[w].at[slots[j], rows]
                _remote(landed, landed, ici_send.at[w, j], ici_recv.at[w, j], sibling).wait_recv()
                cp = _remote(landed, landed, d2d_send.at[w, j], d2d_recv.at[w, j], sibling)
                cp.start()
                forwards.append(cp)
        for w in range(n):
            rows = _half_rows(stacks[w].shape[1], 1 - c)
            for j in range(3):
                landed = outs[w].at[slots[j], rows]
                _remote(landed, landed, d2d_send.at[w, j], d2d_recv.at[w, j], sibling).wait_recv()
        for cp in ici_copies(outs, sems) + forwards:
            cp.wait_send()

    return _Plan(stacks, [jax.ShapeDtypeStruct(s.shape, s.dtype) for s in stacks], {w: w for w in range(n)},
                 [pltpu.SemaphoreType.DMA((n, 3))] * 4, start, finish)


def _plan_sibling_halves(gs):
    n = len(gs)

    def copies(ins, outs, sems):
        x, y, c = _mesh_pos()
        return [_remote(ins[w].at[:, _half_rows(gs[w].shape[1], 1 - c), :], outs[w], sems[0].at[w], sems[1].at[w],
                        (x, y, 1 - c)) for w in range(n)]

    def start(ins, outs, sems):
        for cp in copies(ins, outs, sems):
            cp.start()

    def finish(ins, outs, sems):
        for cp in copies(ins, outs, sems):
            cp.wait()

    return _Plan(gs, [jax.ShapeDtypeStruct((g.shape[0], g.shape[1] // 2, g.shape[2]), g.dtype) for g in gs], {},
                 [pltpu.SemaphoreType.DMA((n,))] * 2, start, finish)


def _plan_chip_exchange(ps):
    n = len(ps)

    def copies(ins, outs, sems):
        x, y, c = _mesh_pos()
        return [_remote(ins[w].at[2 * cx + cy], outs[w].at[j], sems[0].at[w, j], sems[1].at[w, j], (cx, cy, c))
                for w in range(n) for j, (cx, cy) in enumerate(_other_chips(x, y))]

    def start(ins, outs, sems):
        for cp in copies(ins, outs, sems):
            cp.start()

    def finish(ins, outs, sems):
        for cp in copies(ins, outs, sems):
            cp.wait()

    return _Plan(ps, [jax.ShapeDtypeStruct((3,) + p.shape[1:], p.dtype) for p in ps], {},
                 [pltpu.SemaphoreType.DMA((n, 3))] * 2, start, finish)


def _plan_sibling_share(gs):
    n = len(gs)

    def copies(outs, sems, which):
        x, y, c = _mesh_pos()
        cps = []
        for w in range(n):
            rows = outs[w].at[_half_rows(gs[w].shape[0], c if which == "mine" else 1 - c)]
            cps.append(_remote(rows, rows, sems[0].at[w], sems[1].at[w], (x, y, 1 - c)))
        return cps

    def start(ins, outs, sems):
        for cp in copies(outs, sems, "mine"):
            cp.start()

    def finish(ins, outs, sems):
        for cp in copies(outs, sems, "mine"):
            cp.wait_send()
        for cp in copies(outs, sems, "theirs"):
            cp.wait_recv()

    return _Plan(gs, [jax.ShapeDtypeStruct(g.shape, g.dtype) for g in gs], {w: w for w in range(n)},
                 [pltpu.SemaphoreType.DMA((n,))] * 2, start, finish)


def _same_shape_groups(arrays):
    groups = {}
    for i, a in enumerate(arrays):
        groups.setdefault(a.shape, []).append(i)
    return list(groups.values())


def _add_sibling(gs, r1s, ids, tag):
    n = len(gs)
    nch, rh, cols = r1s[0].shape

    def body(ids_ref, *refs):
        for g_ref, r_ref, o_ref in zip(refs[:n], refs[n:2 * n], refs[2 * n:]):
            o_ref[...] = (g_ref[...] + r_ref[...]).astype(BF16)

    blk = lambda fn: pl.BlockSpec((None, rh, cols), fn)
    return pl.pallas_call(
        body, out_shape=[jax.ShapeDtypeStruct(r1s[0].shape, BF16)] * n,
        grid_spec=pltpu.PrefetchScalarGridSpec(
            num_scalar_prefetch=1, grid=(nch,),
            in_specs=[blk(lambda k, ids: (k, ids[1], 0))] * n + [blk(lambda k, ids: (k, 0, 0))] * n,
            out_specs=[blk(lambda k, ids: (k, 0, 0))] * n),
        compiler_params=_params(), name="add_sibling_" + tag,
    )(ids, *gs, *r1s)


def _add_chips(gs, r1s, r2s, ids, tag):
    n = len(gs)
    _, rh, cols = r1s[0].shape
    nb = 2 if rh % 32 == 0 else 1
    rb = rh // nb

    def body(ids_ref, *refs):
        for g_ref, r1_ref, r2_ref, o_ref in zip(refs[:n], refs[n:2 * n], refs[2 * n:3 * n], refs[3 * n:]):
            own = g_ref[...] + r1_ref[...]
            o_ref[...] = ((own + r2_ref[0].astype(F32)) + r2_ref[1].astype(F32)) + r2_ref[2].astype(F32)

    return pl.pallas_call(
        body, out_shape=[jax.ShapeDtypeStruct((2 * rh, cols), F32)] * n,
        grid_spec=pltpu.PrefetchScalarGridSpec(
            num_scalar_prefetch=1, grid=(nb,),
            in_specs=[pl.BlockSpec((None, rb, cols), lambda i, ids: (ids[0], ids[1] * nb + i, 0))] * n
            + [pl.BlockSpec((None, rb, cols), lambda i, ids: (ids[0], i, 0))] * n
            + [pl.BlockSpec((3, rb, cols), lambda i, ids: (0, i, 0))] * n,
            out_specs=[pl.BlockSpec((rb, cols), lambda i, ids: (ids[1] * nb + i, 0))] * n),
        compiler_params=_params(), name="add_chips_" + tag,
    )(ids, *gs, *r1s, *r2s)


VEC_ROWS = 8


N_DEVICES = 8


def _small_pack(part, d, width):
    names = ("ffn1_norm", "mix_norm", "ffn2_norm", "pool_scale", "out_norm_pool", "out_norm_attn", "qn", "kn", "b_forget",
             "pool_w", "loss")
    args = [part[k] for k in names]
    pw_shape = part["pool_w"].shape[1:]

    def body(g1_ref, gm_ref, g2_ref, ps_ref, onp_ref, ona_ref, qn_ref, kn_ref, bf_ref, pw_ref, loss_ref, vbuf, pbuf):
        lo = _head_masks()

        def fold_heads(ref):
            v = jnp.sum(ref[...], axis=0)
            acc = jnp.zeros((VEC_ROWS, LANES), F32)
            for blk in range(width // LANES):
                vb = jnp.broadcast_to(v[:, blk * LANES:(blk + 1) * LANES], (VEC_ROWS, LANES))
                acc = acc + vb + pltpu.roll(vb, HEAD_DIM, 1)
            return jnp.where(lo, acc, 0.0)[0:1, :]

        vbuf[0] = jnp.zeros((VEC_ROWS, d), F32)
        vbuf[0, 0:1, :] = jnp.sum(g1_ref[...], axis=0)
        vbuf[0, 1:2, :] = jnp.sum(gm_ref[...], axis=0)
        vbuf[0, 2:3, :] = jnp.sum(g2_ref[...], axis=0)
        vbuf[0, 5:6, 0:LANES] = jnp.sum(loss_ref[...], axis=0)[0:1, :]
        vbuf[0, 3:4, 0:width] = jnp.sum(ps_ref[...], axis=0)
        vbuf[0, 3:4, width:2 * width] = jnp.sum(onp_ref[...], axis=0)
        vbuf[0, 4:5, 0:width] = jnp.sum(ona_ref[...], axis=0)
        vbuf[0, 4:5, width:width + LANES] = fold_heads(qn_ref)
        vbuf[0, 4:5, width + LANES:width + 2 * LANES] = fold_heads(kn_ref)
        vbuf[0, 4:5, width + 2 * LANES:width + 3 * LANES] = jnp.sum(bf_ref[...], axis=0)
        pbuf[0] = jnp.sum(pw_ref[...], axis=0)

    return pl.pallas_call(
        body, out_shape=[jax.ShapeDtypeStruct((N_DEVICES, VEC_ROWS, d), F32), jax.ShapeDtypeStruct((N_DEVICES,) + pw_shape, F32)],
        in_specs=[VM] * len(args), out_specs=[VM, VM], compiler_params=_params(), name="small_pack",
    )(*args)


def _plan_all_to_all(stacks):
    n = len(stacks)

    def copies(outs, sems):
        x, y, c = _mesh_pos()
        cps = []
        for r in range(1, N_DEVICES):
            peer = (x if not r & 4 else 1 - x, y if not r & 2 else 1 - y, c if not r & 1 else 1 - c)
            cps += [_remote(outs[w].at[0], outs[w].at[r], sems[0].at[w, r - 1], sems[1].at[w, r - 1], peer) for w in range(n)]
        return cps

    def start(ins, outs, sems):
        for cp in copies(outs, sems):
            cp.start()

    def finish(ins, outs, sems):
        for cp in copies(outs, sems):
            cp.wait()

    return _Plan(stacks, [jax.ShapeDtypeStruct(s.shape, s.dtype) for s in stacks], {w: w for w in range(n)},
                 [pltpu.SemaphoreType.DMA((n, N_DEVICES - 1))] * 2, start, finish)


def _small_sum(vstack, pstack, me):
    def body(me_ref, vbuf, pbuf, vec_ref, pw_ref):
        vec = vbuf[me_ref[0]]
        pw = pbuf[me_ref[0]]
        for dev in range(1, N_DEVICES):
            vec = vec + vbuf[jnp.bitwise_xor(me_ref[0], dev)]
            pw = pw + pbuf[jnp.bitwise_xor(me_ref[0], dev)]
        vec_ref[...] = vec
        pw_ref[...] = pw

    full = lambda s: pl.BlockSpec(s.shape, lambda i, me: (0,) * len(s.shape))
    outs = [jax.ShapeDtypeStruct(vstack.shape[1:], F32), jax.ShapeDtypeStruct(pstack.shape[1:], F32)]
    return pl.pallas_call(
        body, out_shape=outs,
        grid_spec=pltpu.PrefetchScalarGridSpec(num_scalar_prefetch=1, grid=(1,), in_specs=[full(vstack), full(pstack)],
                                               out_specs=[full(o) for o in outs]),
        compiler_params=_params(), name="small_sum",
    )(me, vstack, pstack)


def _adamw(ws, gs, ms, vs, tag):
    n = len(ws)
    rows, cols = ws[0].shape
    rb = rows
    while rb * cols * 4 * n > (1 << 20) and rb % 16 == 0:
        rb //= 2

    def body(*refs):
        for j in range(n):
            w_ref, g_ref, m_ref, v_ref = (refs[k * n + j] for k in range(4))
            d_ref, mo_ref, vo_ref = (refs[(4 + k) * n + j] for k in range(3))
            gv = g_ref[...]
            m2 = ADAM_B1 * m_ref[...] + (1.0 - ADAM_B1) * gv
            v2 = ADAM_B2 * v_ref[...] + (1.0 - ADAM_B2) * (gv * gv)
            m_hat = m2 / (1.0 - ADAM_B1 ** ADAM_STEP)
            v_hat = v2 / (1.0 - ADAM_B2 ** ADAM_STEP)
            d_ref[...] = -ADAM_LR * (m_hat / (jnp.sqrt(v_hat) + ADAM_EPS) + ADAM_WD * w_ref[...])
            mo_ref[...] = m2
            vo_ref[...] = v2

    spec = pl.BlockSpec((rb, cols), lambda i: (i, 0))
    res = pl.pallas_call(
        body, out_shape=[jax.ShapeDtypeStruct(ws[0].shape, F32)] * (3 * n), grid=(rows // rb,),
        in_specs=[spec] * (4 * n), out_specs=[spec] * (3 * n), compiler_params=_params(), name="adamw_" + tag,
    )(*ws, *gs, *ms, *vs)
    return [(res[j], res[n + j], res[2 * n + j]) for j in range(n)]


def _pack_vec(p, d, width):
    pad = lambda v: jnp.pad(v, (0, LANES - v.shape[0]))
    row3 = jnp.concatenate([p["pool_scale"], p["out_norm_pool"]])
    row4 = jnp.concatenate([p["out_norm_attn"], pad(p["q_norm"]), pad(p["k_norm"]), pad(p["b_forget"]),
                            jnp.zeros((d - width - 3 * LANES,), F32)])
    rows = [p["ffn1_norm"], p["mix_norm"], p["ffn2_norm"], row3, row4]
    return jnp.pad(jnp.stack(rows), ((0, VEC_ROWS - len(rows)), (0, 0)))


def _unpack_vec(vec, width):
    return dict(ffn1_norm=vec[0], mix_norm=vec[1], ffn2_norm=vec[2], pool_scale=vec[3, :width],
                out_norm_pool=vec[3, width:2 * width], out_norm_attn=vec[4, :width],
                q_norm=vec[4, width:width + HEAD_DIM], k_norm=vec[4, width + LANES:width + LANES + HEAD_DIM],
                b_forget=vec[4, width + 2 * LANES:width + 2 * LANES + N_HEADS])


WEIGHT_NAMES = ("ffn1_norm", "ffn1_w_gate", "ffn1_w_up", "ffn1_w_down", "mix_norm", "w_in", "b_forget", "pool_w",
                "pool_scale", "q_norm", "k_norm", "out_norm_pool", "out_norm_attn", "w_out", "ffn2_norm",
                "ffn2_w_gate", "ffn2_w_up", "ffn2_w_down")
BIG_NAMES = ("ffn1_w_gate", "ffn1_w_up", "ffn1_w_down", "w_in", "w_out", "ffn2_w_gate", "ffn2_w_up", "ffn2_w_down")
TRANSPOSED_NAMES = ("ffn1_w_gate", "ffn1_w_up", "w_in", "ffn2_w_gate", "ffn2_w_up")
FFN1_NAMES = ("ffn1_w_gate", "ffn1_w_up", "ffn1_w_down")
MIX_NAMES = ("w_in", "w_out")
FFN2_NAMES = ("ffn2_w_gate", "ffn2_w_up", "ffn2_w_down")
REST_NAMES = MIX_NAMES + FFN2_NAMES


def kernel(x, ffn1_norm, ffn1_w_gate, ffn1_w_up, ffn1_w_down, mix_norm, w_in, b_forget, pool_w, pool_scale, q_norm, k_norm, out_norm_pool, out_norm_attn, w_out, ffn2_norm, ffn2_w_gate, ffn2_w_up, ffn2_w_down, loss_target, m_ffn1_norm, m_ffn1_w_gate, m_ffn1_w_up, m_ffn1_w_down, m_mix_norm, m_w_in, m_b_forget, m_pool_w, m_pool_scale, m_q_norm, m_k_norm, m_out_norm_pool, m_out_norm_attn, m_w_out, m_ffn2_norm, m_ffn2_w_gate, m_ffn2_w_up, m_ffn2_w_down, v_ffn1_norm, v_ffn1_w_gate, v_ffn1_w_up, v_ffn1_w_down, v_mix_norm, v_w_in, v_b_forget, v_pool_w, v_pool_scale, v_q_norm, v_k_norm, v_out_norm_pool, v_out_norm_attn, v_w_out, v_ffn2_norm, v_ffn2_w_gate, v_ffn2_w_up, v_ffn2_w_down):
    given = dict(locals())
    w = {n: given[n] for n in WEIGHT_NAMES}
    m = {n: given["m_" + n] for n in WEIGHT_NAMES}
    v = {n: given["v_" + n] for n in WEIGHT_NAMES}
    n_batch, seq, d = x.shape
    width = pool_scale.shape[0]
    in_rows = w_in.shape[1]
    in_cols = N_CHIPS * in_rows
    in_pad = -(-in_rows // 32) * 32
    in_cols_pad = in_cols - N_HEADS + LANES

    work = lambda a, n: a.T if n in TRANSPOSED_NAMES else a
    exchanged = lambda a, n: jnp.pad(a, ((0, in_pad - in_rows), (0, 0))) if n == "w_in" else a

    mesh_x, mesh_y, mesh_c = _mesh_pos()
    ids = jnp.stack([2 * mesh_x + mesh_y, mesh_c]).astype(jnp.int32)

    row = lambda a: a.reshape(1, -1)
    g1, gm, g2, ps, onp, ona = (row(a) for a in (ffn1_norm, mix_norm, ffn2_norm, pool_scale, out_norm_pool, out_norm_attn))
    qn, kn = row(jnp.tile(q_norm, N_HEADS)), row(jnp.tile(k_norm, N_HEADS))
    bf = row(jnp.pad(b_forget, (0, LANES - N_HEADS)))
    pwb = pool_w.astype(BF16)
    xf, tgt = x.reshape(n_batch * seq, d), loss_target.reshape(n_batch * seq, d)

    def grouped(call, names, *lists):
        out = [None] * len(names)
        for idx in _same_shape_groups(lists[0]):
            res = call(*[[lst[i] for i in idx] for lst in lists], names[idx[0]])
            for i, r in zip(idx, res):
                out[i] = r
        return out

    placed = dict(zip(BIG_NAMES, grouped(lambda ws, tag: _place_cast(ws, ids, tag), BIG_NAMES,
                                         [exchanged(work(w[n], n), n) for n in BIG_NAMES])))
    wg1, wu1, wd1 = _run_plan(_plan_gather([placed[n] for n in FFN1_NAMES]), "gather_ffn1")
    (x1, h1, a1, b1, s1), (w_in_all, w_out_all) = _ffn_fwd(xf, g1, wg1, wu1, wd1,
                                                           plan=_plan_gather([placed[n] for n in MIX_NAMES]))
    w_in_t = jnp.pad(w_in_all[:, :in_rows].reshape(in_cols, d), ((0, in_cols_pad - in_cols), (0, 0)))
    w_out_full = w_out_all.reshape(N_CHIPS * w_out.shape[0], d)
    woa, wob = w_out_full[:width], w_out_full[width:]

    hm, pv, q, k, qh, kh, vb, f = _mix_proj(x1, gm, w_in_t, qn, kn, width, width)
    qa, ka = _forget_prefix(f, bf, qh, kh, n_batch, seq)
    yp = _pool_fwd(pv, pwb, ps, onp, n_batch, seq)
    (o, lse), (wg2, wu2, wd2) = _attn_fwd(qa, ka, vb, n_batch, seq, plan=_plan_gather([placed[n] for n in FFN2_NAMES]))
    x2, ya = _mix_out(x1, yp, o, ona, woa, wob)
    (dy, h2, a2, b2, s2, lpart), _ = _ffn_fwd(x2, g2, wg2, wu2, wd2, target=tgt)

    def to_chips(gs, arrived, tags):
        return grouped(lambda g, r, tag: _add_sibling(g, r, ids, tag), tags, gs, arrived)

    def own_rows(gs, from_sibling, from_chips, tags):
        return grouped(lambda g, ra, rb, tag: _add_chips(g, ra, rb, ids, tag), tags, gs, from_sibling, from_chips)

    (dx2, da2, db2, dg2), _ = _ffn_bwd_x(dy, x2, g2, a2, b2, wg2, wu2, wd2, "ffn2_bwd_x")
    dw2, _ = _ffn_bwd_w([(da2, h2, 1.0), (db2, h2, 1.0), (s2, dy, 0.5)], "ffn2_bwd_w")
    (dyp, do, delta, dwoa, dwob, dona), sib2 = _mix_out_bwd(dx2, o, yp, ya, ona, woa, wob, plan=_plan_sibling_halves(dw2))
    dpv, dpw, dps, donp = _pool_bwd(pv, dyp, pwb, ps, onp, n_batch, seq)
    (dqh, dfq), chips2 = _attn_bwd_q(qa, ka, vb, do, lse, delta, n_batch, seq,
                                     plan=_plan_chip_exchange(to_chips(dw2, sib2, FFN2_NAMES)))
    (dkh, dv, dfk), red2 = _attn_bwd_kv(qa, ka, vb, do, lse, delta, n_batch, seq,
                                        plan=_plan_sibling_share(own_rows(dw2, sib2, chips2, FFN2_NAMES)))
    df, dbf = _forget_bwd(dfq, dfk, f, bf, n_batch, seq)
    dx1, dw_in_t, dgm, dqn, dkn = _mix_in_bwd(dx2, x1, gm, hm, dpv, dqh, q, dkh, k, dv, df, qn, kn, w_in_t)
    in_base = [in_rows * k // 8 * 8 for k in range(N_CHIPS)]
    d_w_in = jnp.stack([dw_in_t[b:b + in_pad] for b in in_base])
    d_w_out = jnp.concatenate([dwoa, dwob], axis=0).reshape(N_CHIPS, w_out.shape[0], d)
    dwm = [d_w_in, d_w_out]
    down, gate_up = FFN1_NAMES[2:], FFN1_NAMES[:2]
    dwd1, sibm = _ffn_bwd_w([(s1, dx1, 0.5)], "ffn1_bwd_w_down", plan=_plan_sibling_halves(dwm))
    (da1, db1), arrived = _ffn_bwd_a(dx1, a1, b1, wd1, "ffn1_bwd_a",
                                     plan=_merge_plans(_plan_sibling_halves(dwd1),
                                                       _plan_chip_exchange(to_chips(dwm, sibm, MIX_NAMES))))
    sibd, chipsm = arrived[:1], arrived[1:]
    dwgu1, chipsd = _ffn_bwd_w([(da1, h1, 1.0), (db1, h1, 1.0)], "ffn1_bwd_w_gate_up",
                               plan=_plan_chip_exchange(to_chips(dwd1, sibd, down)))
    n_tiles = (n_batch * seq) // min(512, n_batch * seq)
    first = max(n_tiles // 4, 1)
    begun, sibgu = _ffn_bwd_h(dx1, xf, g1, da1, db1, wg1, wu1, "ffn1_bwd_h_first", (0, first),
                              plan=_plan_sibling_halves(dwgu1))
    (gx, dg1), chipsgu = _ffn_bwd_h(dx1, xf, g1, da1, db1, wg1, wu1, "ffn1_bwd_h_rest", (first, n_tiles), prev=begun,
                                    plan=_plan_chip_exchange(to_chips(dwgu1, sibgu, gate_up)))

    part = dict(ffn1_norm=dg1, mix_norm=dgm, ffn2_norm=dg2, b_forget=dbf, pool_scale=dps, out_norm_pool=donp,
                out_norm_attn=dona, qn=dqn, kn=dkn, pool_w=dpw.reshape(n_batch, -1, pool_w.shape[-1]), loss=lpart)
    mine = (own_rows(dwgu1, sibgu, chipsgu, gate_up) + own_rows(dwd1, sibd, chipsd, down)
            + own_rows(dwm, sibm, chipsm, MIX_NAMES))
    last = _run_plan(_merge_plans(_plan_sibling_share(mine), _plan_all_to_all(_small_pack(part, d, width))), "last_exchange")
    vstack, pstack = last[len(mine):]
    g_vec, g_pw = _small_sum(vstack, pstack, jnp.reshape(4 * mesh_x + 2 * mesh_y + mesh_c, (1,)).astype(jnp.int32))
    loss = g_vec[5, 0]
    reduced = dict(zip(FFN1_NAMES + MIX_NAMES + FFN2_NAMES, list(last[:len(mine)]) + list(red2)))
    reduced["w_in"] = lax.dynamic_slice(reduced["w_in"], ((in_rows * ids[0]) % 8, 0), (in_rows, d))

    grads, delta, new_m, new_v = {}, {}, {}, {}
    for names in (FFN2_NAMES, FFN1_NAMES, ("w_in",), ("w_out",)):
        stepped = _adamw([work(w[n], n) for n in names], [reduced[n] for n in names], [work(m[n], n) for n in names],
                         [work(v[n], n) for n in names], names[0])
        for n, step in zip(names, stepped):
            grads[n], delta[n], new_m[n], new_v[n] = (work(a, n) for a in (reduced[n], *step))
    flat_pw = lambda a: a.reshape(-1, a.shape[-1])
    (d_pw, m_pw, v_pw), = _adamw([flat_pw(pool_w)], [g_pw], [flat_pw(m_pool_w)], [flat_pw(v_pool_w)], "pool_w")
    (d_vec, m_vec, v_vec), = _adamw([_pack_vec(w, d, width)], [g_vec], [_pack_vec(m, d, width)], [_pack_vec(v, d, width)],
                                    "vectors")
    grads.update(_unpack_vec(g_vec, width), pool_w=g_pw.reshape(pool_w.shape))
    delta.update(_unpack_vec(d_vec, width), pool_w=d_pw.reshape(pool_w.shape))
    new_m.update(_unpack_vec(m_vec, width), pool_w=m_pw.reshape(pool_w.shape))
    new_v.update(_unpack_vec(v_vec, width), pool_w=v_pw.reshape(pool_w.shape))
    return (loss, gx.reshape(x.shape), *[grads[n] for n in WEIGHT_NAMES], *[delta[n] for n in WEIGHT_NAMES],
            *[new_m[n] for n in WEIGHT_NAMES], *[new_v[n] for n in WEIGHT_NAMES])
```

```python
import functools

import jax
import jax.numpy as jnp
from jax import lax
from jax.experimental import pallas as pl
from jax.experimental.pallas import tpu as pltpu

F32 = jnp.float32
BF16 = jnp.bfloat16
EPS = 1e-6
NEG = -1e30
ADAM_LR = 0.001
ADAM_B1 = 0.9
ADAM_B2 = 0.999
ADAM_EPS = 1e-08
ADAM_WD = 0.01
ADAM_STEP = 10
POOL_WINDOWS = (2, 4, 8, 16)
HEAD_DIM = 64
N_HEADS = 8
LANES = 128
N_CHIPS = 4
ATT_BLOCK = 512
ATT_SUB = 128
VMEM_LIMIT = 56 * 1024 * 1024
MESH_AXES = ("x", "y", "c")
ANY = pl.BlockSpec(memory_space=pl.ANY)
VM = pl.BlockSpec(memory_space=pltpu.VMEM)


def _params(**kw):
    return pltpu.CompilerParams(vmem_limit_bytes=VMEM_LIMIT, **kw)


def _dot(a, b):
    return jnp.dot(a, b, preferred_element_type=F32)


def _dot_nt(a, b):
    return lax.dot_general(a, b, (((1,), (1,)), ((), ())), preferred_element_type=F32)


def _dot_tn(a, b):
    return lax.dot_general(a, b, (((0,), (0,)), ((), ())), preferred_element_type=F32)


def _sigmoid(z):
    return 1.0 / (1.0 + jnp.exp(-z))


def _rms(xf):
    return lax.rsqrt(jnp.mean(xf * xf, axis=-1, keepdims=True) + EPS)


def _rms_bwd(xf, r, gain, dh):
    xh = xf * r
    dyg = dh * gain
    return r * (dyg - xh * jnp.mean(dyg * xh, axis=-1, keepdims=True)), dh * xh


def _total(v):
    return jnp.sum(jnp.sum(v, axis=1, keepdims=True), axis=0, keepdims=True)


def _ffn_fwd(x, gain, wg, wu, wd, target=None, plan=None):
    t, d = x.shape
    nch, fc, _ = wg.shape
    tm = min(512, t)
    nt = t // tm
    with_loss = target is not None

    def body(*refs):
        if with_loss:
            x_ref, g_ref, wg_ref, wu_ref, wd_ref, t_ref, o_ref, h_ref, a_ref, b_ref, s_ref, l_ref, acc_ref = refs
        else:
            x_ref, g_ref, wg_ref, wu_ref, wd_ref, o_ref, h_ref, a_ref, b_ref, s_ref, acc_ref = refs
        k = pl.program_id(1)

        @pl.when(k == 0)
        def _():
            xf = x_ref[...]
            h_ref[...] = ((xf * _rms(xf)) * g_ref[...]).astype(BF16)
            acc_ref[...] = jnp.zeros_like(acc_ref)

        for rows in _row_halves(tm):
            h = h_ref[rows, :]
            a = _dot_nt(h, wg_ref[...])
            b = _dot_nt(h, wu_ref[...])
            sb = ((a * (0.5 * jnp.tanh(0.5 * a) + 0.5)) * b).astype(BF16)
            a_ref[rows, :] = a.astype(BF16)
            b_ref[rows, :] = b.astype(BF16)
            s_ref[rows, :] = sb
            acc_ref[rows, :] += _dot(sb, wd_ref[...])

        @pl.when(k == nch - 1)
        def _():
            y = x_ref[...] + 0.5 * acc_ref[...]
            if with_loss:
                e = y - t_ref[...]
                o_ref[...] = e * (1.0 / d)
                l_ref[...] = jnp.broadcast_to(_total(e * e) * (0.5 / d), l_ref.shape)
            else:
                o_ref[...] = y

    row = pl.BlockSpec((tm, d), lambda i, k: (i, 0))
    chunk = pl.BlockSpec((None, fc, d), lambda i, k: (k, 0, 0))
    act = pl.BlockSpec((None, tm, fc), lambda i, k: (k, i, 0))
    in_specs = [row, pl.BlockSpec((1, d), lambda i, k: (0, 0)), chunk, chunk, chunk]
    out_shape = [jax.ShapeDtypeStruct((t, d), F32), jax.ShapeDtypeStruct((t, d), BF16)]
    out_shape += [jax.ShapeDtypeStruct((nch, t, fc), BF16)] * 3
    out_specs = [row, row, act, act, act]
    args = [x, gain, wg, wu, wd]
    if with_loss:
        in_specs.append(row)
        args.append(target)
        out_shape.append(jax.ShapeDtypeStruct((nt, 8, LANES), F32))
        out_specs.append(pl.BlockSpec((None, 8, LANES), lambda i, k: (i, 0, 0)))
    return _pallas(body, name="ffn_fwd_loss" if with_loss else "ffn_fwd", args=args, in_specs=in_specs,
                   out_shape=out_shape, out_specs=out_specs, grid=(nt, nch),
                   scratch_shapes=[pltpu.VMEM((tm, d), F32)], plan=plan)


def _row_halves(n):
    return [slice(0, n // 2), slice(n // 2, n)]


def _swiglu_grads(dy_ref, a_ref, b_ref, wd_ref, rows):
    ds = _dot_nt(dy_ref[rows, :].astype(BF16), wd_ref[...])
    av = a_ref[rows, :].astype(F32)
    bv = b_ref[rows, :].astype(F32)
    th = jnp.tanh(0.5 * av)
    half_sig = 0.25 * th + 0.25
    dab = ((ds * bv) * (half_sig * (1.0 + av * (0.5 - 0.5 * th)))).astype(BF16)
    return dab, (ds * (av * half_sig)).astype(BF16)


def _ffn_bwd_a(dy, a, b, wd, name, plan=None):
    t, d = dy.shape
    nch, fc, _ = wd.shape
    tm = min(512, t)

    def body(dy_ref, a_ref, b_ref, wd_ref, da_ref, db_ref):
        for rows in _row_halves(tm):
            da_ref[rows, :], db_ref[rows, :] = _swiglu_grads(dy_ref, a_ref, b_ref, wd_ref, rows)

    act = pl.BlockSpec((None, tm, fc), lambda i, k: (k, i, 0))
    return _pallas(
        body, name=name, args=[dy, a, b, wd], out_shape=[jax.ShapeDtypeStruct((nch, t, fc), BF16)] * 2, grid=(t // tm, nch),
        in_specs=[pl.BlockSpec((tm, d), lambda i, k: (i, 0)), act, act, pl.BlockSpec((None, fc, d), lambda i, k: (k, 0, 0))],
        out_specs=[act, act], plan=plan)


def _ffn_bwd_h(dy, x, gain, da, db, wg, wu, name, tiles, prev=None, plan=None):
    t, d = x.shape
    nch, fc, _ = wg.shape
    tm = min(512, t)
    nt = t // tm
    t0, t1 = tiles

    def body(*refs):
        dy_ref, x_ref, g_ref, da_ref, db_ref, wg_ref, wu_ref = refs[:7]
        dx_ref, dg_ref, acc_ref = refs[-3:]
        k = pl.program_id(1)

        @pl.when(k == 0)
        def _():
            acc_ref[...] = jnp.zeros_like(acc_ref)

        acc_ref[...] += _dot(da_ref[...], wg_ref[...]) + _dot(db_ref[...], wu_ref[...])

        @pl.when(k == nch - 1)
        def _():
            xf = x_ref[...]
            dxn, dgr = _rms_bwd(xf, _rms(xf), g_ref[...], acc_ref[...])
            dx_ref[...] = dy_ref[...] + dxn
            dg_ref[...] = jnp.sum(dgr, axis=0, keepdims=True)

    row = pl.BlockSpec((tm, d), lambda i, k: (i + t0, 0))
    chunk = pl.BlockSpec((None, fc, d), lambda i, k: (k, 0, 0))
    act = pl.BlockSpec((None, tm, fc), lambda i, k: (k, i + t0, 0))
    args = [dy, x, gain, da, db, wg, wu]
    in_specs = [row, row, pl.BlockSpec((1, d), lambda i, k: (0, 0)), act, act, chunk, chunk]
    aliases = {}
    if prev is not None:
        aliases = {len(args): 0, len(args) + 1: 1}
        args += list(prev)
        in_specs += [ANY, ANY]
    return _pallas(
        body, name=name, args=args, out_shape=[jax.ShapeDtypeStruct((t, d), F32), jax.ShapeDtypeStruct((nt, 1, d), F32)],
        grid=(t1 - t0, nch), in_specs=in_specs,
        out_specs=[row, pl.BlockSpec((None, 1, d), lambda i, k: (i + t0, 0, 0))],
        scratch_shapes=[pltpu.VMEM((tm, d), F32)], plan=plan, aliases=aliases)


def _ffn_bwd_x(dy, x, gain, a, b, wg, wu, wd, name, plan=None):
    t, d = x.shape
    nch, fc, _ = wg.shape
    tm = min(512, t)
    nt = t // tm

    def body(dy_ref, x_ref, g_ref, a_ref, b_ref, wg_ref, wu_ref, wd_ref, dx_ref, da_ref, db_ref, dg_ref, acc_ref):
        k = pl.program_id(1)

        @pl.when(k == 0)
        def _():
            acc_ref[...] = jnp.zeros_like(acc_ref)

        for rows in _row_halves(tm):
            dab, dbb = _swiglu_grads(dy_ref, a_ref, b_ref, wd_ref, rows)
            da_ref[rows, :] = dab
            db_ref[rows, :] = dbb
            acc_ref[rows, :] += _dot(dab, wg_ref[...]) + _dot(dbb, wu_ref[...])

        @pl.when(k == nch - 1)
        def _():
            xf = x_ref[...]
            dxn, dgr = _rms_bwd(xf, _rms(xf), g_ref[...], acc_ref[...])
            dx_ref[...] = dy_ref[...] + dxn
            dg_ref[...] = jnp.sum(dgr, axis=0, keepdims=True)

    row = pl.BlockSpec((tm, d), lambda i, k: (i, 0))
    chunk = pl.BlockSpec((None, fc, d), lambda i, k: (k, 0, 0))
    act = pl.BlockSpec((None, tm, fc), lambda i, k: (k, i, 0))
    return _pallas(
        body, name=name, args=[dy, x, gain, a, b, wg, wu, wd],
        out_shape=[jax.ShapeDtypeStruct((t, d), F32), jax.ShapeDtypeStruct((nch, t, fc), BF16),
                   jax.ShapeDtypeStruct((nch, t, fc), BF16), jax.ShapeDtypeStruct((nt, 1, d), F32)],
        grid=(nt, nch),
        in_specs=[row, row, pl.BlockSpec((1, d), lambda i, k: (0, 0)), act, act, chunk, chunk, chunk],
        out_specs=[row, act, act, pl.BlockSpec((None, 1, d), lambda i, k: (i, 0, 0))],
        scratch_shapes=[pltpu.VMEM((tm, d), F32)], plan=plan)


def _ffn_bwd_w(pairs, name, plan=None):
    n = len(pairs)
    nch, t, fc = pairs[0][0].shape
    d = pairs[0][1].shape[1]
    tm = min(1024, t)

    def body(*refs):
        @pl.when(pl.program_id(1) == 0)
        def _():
            for o_ref in refs[2 * n:]:
                o_ref[...] = jnp.zeros_like(o_ref)

        for j, (_, _, scale) in enumerate(pairs):
            other = refs[n + j][...]
            if other.dtype != BF16:
                other = (scale * other).astype(BF16)
            refs[2 * n + j][...] += _dot_tn(refs[j][...], other)

    row = pl.BlockSpec((tm, d), lambda k, i: (i, 0))
    act = pl.BlockSpec((None, tm, fc), lambda k, i: (k, i, 0))
    chunk = pl.BlockSpec((None, fc, d), lambda k, i: (k, 0, 0))
    return _pallas(body, name=name, args=[p[0] for p in pairs] + [p[1] for p in pairs],
                   out_shape=[jax.ShapeDtypeStruct((nch, fc, d), F32)] * n, grid=(nch, t // tm),
                   in_specs=[act] * n + [row] * n, out_specs=[chunk] * n, plan=plan)


def _head_masks():
    lane = lax.broadcasted_iota(jnp.int32, (1, LANES), 1)
    return lane < HEAD_DIM


def _head_rms(x, lo):
    x2 = x * x
    s0 = jnp.sum(jnp.where(lo, x2, 0.0), axis=1, keepdims=True)
    s1 = jnp.sum(jnp.where(lo, 0.0, x2), axis=1, keepdims=True)
    return jnp.where(lo, lax.rsqrt(s0 * (1.0 / HEAD_DIM) + EPS), lax.rsqrt(s1 * (1.0 / HEAD_DIM) + EPS))


def _head_mean(v, lo):
    s0 = jnp.sum(jnp.where(lo, v, 0.0), axis=1, keepdims=True)
    s1 = jnp.sum(jnp.where(lo, 0.0, v), axis=1, keepdims=True)
    return jnp.where(lo, s0, s1) * (1.0 / HEAD_DIM)


def _mix_proj(x1, gain, wt, qn, kn, pool_width, attn_width):
    t, d = x1.shape
    tm = min(512, t)
    nt = t // tm
    scale = HEAD_DIM ** -0.5
    c_q, c_k, c_v = pool_width, pool_width + attn_width, pool_width + 2 * attn_width
    c_f = c_v + attn_width

    def body(x_ref, g_ref, wt_ref, qn_ref, kn_ref, hm_ref, pv_ref, q_ref, k_ref, qh_ref, kh_ref, vb_ref, f_ref):
        xf = x_ref[...]
        hm = ((xf * _rms(xf)) * g_ref[...]).astype(BF16)
        hm_ref[...] = hm
        f_ref[...] = _dot_nt(hm, wt_ref[c_f:c_f + LANES, :])
        pv_ref[...] = _dot_nt(hm, wt_ref[0:pool_width, :])
        vb_ref[...] = _dot_nt(hm, wt_ref[c_v:c_v + attn_width, :]).astype(BF16)
        lo = _head_masks()
        for c0, raw_ref, hat_ref, n_ref, mul in ((c_q, q_ref, qh_ref, qn_ref, scale), (c_k, k_ref, kh_ref, kn_ref, 1.0)):
            raw = _dot_nt(hm, wt_ref[c0:c0 + attn_width, :])
            raw_ref[...] = raw
            for blk in range(attn_width // LANES):
                sl = slice(blk * LANES, (blk + 1) * LANES)
                xb = raw[:, sl]
                hat_ref[:, sl] = (((xb * _head_rms(xb, lo)) * n_ref[:, sl]) * mul).astype(BF16)

    row = pl.BlockSpec((tm, d), lambda i: (i, 0))
    half = pl.BlockSpec((tm, attn_width), lambda i: (i, 0))
    const = lambda shape: pl.BlockSpec(shape, lambda i: (0, 0))
    return pl.pallas_call(
        body,
        out_shape=[jax.ShapeDtypeStruct((t, d), BF16), jax.ShapeDtypeStruct((t, pool_width), F32),
                   jax.ShapeDtypeStruct((t, attn_width), F32), jax.ShapeDtypeStruct((t, attn_width), F32),
                   jax.ShapeDtypeStruct((t, attn_width), BF16), jax.ShapeDtypeStruct((t, attn_width), BF16),
                   jax.ShapeDtypeStruct((t, attn_width), BF16), jax.ShapeDtypeStruct((t, LANES), F32)],
        grid=(nt,),
        in_specs=[row, const((1, d)), const(wt.shape), const((1, attn_width)), const((1, attn_width))],
        out_specs=[row, pl.BlockSpec((tm, pool_width), lambda i: (i, 0)), half, half, half, half, half,
                   pl.BlockSpec((tm, LANES), lambda i: (i, 0))],
        compiler_params=_params(), name="mix_proj",
    )(x1, gain, wt, qn, kn)


def _shift_down(v, dist, row):
    return jnp.where(row >= dist, pltpu.roll(v, dist, 0), 0.0)


def _shift_up(v, dist, row, n):
    return jnp.where(row + dist < n, pltpu.roll(v, n - dist, 0), 0.0)


def _aug_lane(e):
    return HEAD_DIM if e == 0 else 0


def _forget_prefix(f, bias, qh, kh, n_batch, seq):
    def body(f_ref, b_ref, q_ref, k_ref, qa_ref, ka_ref):
        z = f_ref[...] + b_ref[...]
        acc = jnp.minimum(z, 0.0) - jnp.log(1.0 + jnp.exp(-jnp.abs(z)))
        row = lax.broadcasted_iota(jnp.int32, (seq, 1), 0)
        dist = 1
        while dist < seq:
            acc = acc + _shift_down(acc, dist, row)
            dist *= 2
        lane = lax.broadcasted_iota(jnp.int32, (1, LANES), 1)
        for h in range(N_HEADS):
            pair, e = divmod(h, 2)
            a0 = _aug_lane(e)
            own = (lane < HEAD_DIM) if e == 0 else (lane >= HEAD_DIM)
            fh = _pick_lane(acc, h)
            hi = fh.astype(BF16).astype(F32)
            rest = fh - hi
            mid = rest.astype(BF16).astype(F32)
            low = rest - mid
            q_ones = (lane >= a0 + 3) & (lane < a0 + 6)
            k_ones = (lane >= a0) & (lane < a0 + 3)
            q_aug = jnp.where(lane == a0, hi, jnp.where(lane == a0 + 1, mid, jnp.where(lane == a0 + 2, low,
                              jnp.where(q_ones, 1.0, 0.0))))
            k_aug = jnp.where(k_ones, 1.0, jnp.where(lane == a0 + 3, -hi, jnp.where(lane == a0 + 4, -mid,
                              jnp.where(lane == a0 + 5, -low, 0.0))))
            src = slice(pair * LANES, (pair + 1) * LANES)
            dst = slice(h * LANES, (h + 1) * LANES)
            qa_ref[:, dst] = jnp.where(own, q_ref[:, src].astype(F32), q_aug).astype(BF16)
            ka_ref[:, dst] = jnp.where(own, k_ref[:, src].astype(F32), k_aug).astype(BF16)

    width = qh.shape[1]
    tok = pl.BlockSpec((seq, width), lambda b: (b, 0))
    aug = pl.BlockSpec((seq, N_HEADS * LANES), lambda b: (b, 0))
    return pl.pallas_call(
        body, out_shape=[jax.ShapeDtypeStruct((n_batch * seq, N_HEADS * LANES), BF16)] * 2, grid=(n_batch,),
        in_specs=[pl.BlockSpec((seq, LANES), lambda b: (b, 0)), pl.BlockSpec((1, LANES), lambda b: (0, 0)), tok, tok],
        out_specs=[aug, aug], compiler_params=_params(), name="forget_prefix",
    )(f, bias, qh, kh)


def _pool_groups(pv_ref, pw_ref, ps_ref, seq):
    row = lax.broadcasted_iota(jnp.int32, (seq, 1), 0)
    pos = (row + 1).astype(F32)
    out = []
    for g, win in enumerate(POOL_WINDOWS):
        sl = slice(g * LANES, (g + 1) * LANES)
        xg = pv_ref[:, sl]
        acc = xg
        dist = 1
        while dist < win:
            acc = acc + _shift_down(acc, dist, row)
            dist *= 2
        pooled = (acc / jnp.minimum(pos, float(win)) - xg).astype(BF16)
        mixed = _dot(pooled, pw_ref[g])
        out.append((pooled, mixed, mixed * ps_ref[:, sl]))
    return out


def _pool_fwd(pv, pw, ps, onp, n_batch, seq):
    width = pv.shape[1]

    def body(pv_ref, pw_ref, ps_ref, on_ref, y_ref):
        groups = _pool_groups(pv_ref, pw_ref, ps_ref, seq)
        ssq = sum(jnp.sum(ms * ms, axis=1, keepdims=True) for _, _, ms in groups)
        r = lax.rsqrt(ssq * (1.0 / width) + EPS)
        for g, (_, _, ms) in enumerate(groups):
            sl = slice(g * LANES, (g + 1) * LANES)
            y_ref[:, sl] = ((ms * r) * on_ref[:, sl]).astype(BF16)

    return pl.pallas_call(
        body, out_shape=jax.ShapeDtypeStruct((n_batch * seq, width), BF16), grid=(n_batch,),
        in_specs=[pl.BlockSpec((seq, width), lambda b: (b, 0)), pl.BlockSpec(pw.shape, lambda b: (0, 0, 0)),
                  pl.BlockSpec((1, width), lambda b: (0, 0)), pl.BlockSpec((1, width), lambda b: (0, 0))],
        out_specs=pl.BlockSpec((seq, width), lambda b: (b, 0)),
        compiler_params=_params(), name="pool_fwd",
    )(pv, pw, ps, onp)


def _pool_bwd(pv, dyp, pw, ps, onp, n_batch, seq):
    width = pv.shape[1]

    def body(pv_ref, dy_ref, pw_ref, ps_ref, on_ref, dpv_ref, dpw_ref, dps_ref, don_ref):
        groups = _pool_groups(pv_ref, pw_ref, ps_ref, seq)
        ssq = sum(jnp.sum(ms * ms, axis=1, keepdims=True) for _, _, ms in groups)
        r = lax.rsqrt(ssq * (1.0 / width) + EPS)
        mean = sum(jnp.sum((dy_ref[:, g * LANES:(g + 1) * LANES] * on_ref[:, g * LANES:(g + 1) * LANES]) * (ms * r),
                           axis=1, keepdims=True) for g, (_, _, ms) in enumerate(groups)) * (1.0 / width)
        row = lax.broadcasted_iota(jnp.int32, (seq, 1), 0)
        pos = (row + 1).astype(F32)
        for g, (pooled, mixed, ms) in enumerate(groups):
            sl = slice(g * LANES, (g + 1) * LANES)
            dy = dy_ref[:, sl]
            xh = ms * r
            don_ref[:, sl] = jnp.sum(dy * xh, axis=0, keepdims=True)
            dms = r * (dy * on_ref[:, sl] - xh * mean)
            dps_ref[:, sl] = jnp.sum(dms * mixed, axis=0, keepdims=True)
            dmix = (dms * ps_ref[:, sl]).astype(BF16)
            dpw_ref[g] = _dot_tn(pooled, dmix)
            dpool = _dot_nt(dmix, pw_ref[g])
            win = POOL_WINDOWS[g]
            acc = dpool / jnp.minimum(pos, float(win))
            dist = 1
            while dist < win:
                acc = acc + _shift_up(acc, dist, row, seq)
                dist *= 2
            dpv_ref[:, sl] = (acc - dpool).astype(BF16)

    tok = pl.BlockSpec((seq, width), lambda b: (b, 0))
    vec = pl.BlockSpec((1, width), lambda b: (0, 0))
    pvec = pl.BlockSpec((None, 1, width), lambda b: (b, 0, 0))
    return pl.pallas_call(
        body,
        out_shape=[jax.ShapeDtypeStruct((n_batch * seq, width), BF16),
                   jax.ShapeDtypeStruct((n_batch,) + pw.shape, F32),
                   jax.ShapeDtypeStruct((n_batch, 1, width), F32), jax.ShapeDtypeStruct((n_batch, 1, width), F32)],
        grid=(n_batch,),
        in_specs=[tok, tok, pl.BlockSpec(pw.shape, lambda b: (0, 0, 0)), vec, vec],
        out_specs=[tok, pl.BlockSpec((None,) + pw.shape, lambda b: (b, 0, 0, 0)), pvec, pvec],
        compiler_params=_params(), name="pool_bwd",
    )(pv, dyp, pw, ps, onp)


def _pick_lane(tile, idx):
    lane = lax.broadcasted_iota(jnp.int32, (1, LANES), 1)
    return jnp.sum(jnp.where(lane == idx, tile, 0.0), axis=1, keepdims=True)


def _pick_row(tile, idx):
    sub = lax.broadcasted_iota(jnp.int32, (tile.shape[0], 1), 0)
    return jnp.sum(jnp.where(sub == idx, tile, 0.0), axis=0, keepdims=True)


def _put_lane(col, idx):
    lane = lax.broadcasted_iota(jnp.int32, (1, LANES), 1)
    return jnp.where(lane == idx, col, 0.0)


def _head_select(e):
    lo = _head_masks()
    return lo if e == 0 else jnp.logical_not(lo)


def _causal(st, shift):
    row = lax.broadcasted_iota(jnp.int32, st.shape, 0)
    col = lax.broadcasted_iota(jnp.int32, st.shape, 1) + shift
    return jnp.where(col >= row, st, NEG)


def _transpose_blocks(a):
    rows, cols = a.shape
    return jnp.concatenate(
        [jnp.concatenate([a[r:r + LANES, c:c + LANES].T for r in range(0, rows, LANES)], axis=1)
         for c in range(0, cols, LANES)], axis=0)


def _stat_rows(ref, head, nsub):
    return jnp.concatenate([_pick_row(ref[a], head) for a in range(nsub)], axis=1)


def _accumulate(ref, value, first):
    @pl.when(first)
    def _():
        ref[...] = value

    @pl.when(jnp.logical_not(first))
    def _():
        ref[...] += value


def _attn_fwd(qa, ka, vb, n_batch, seq, plan=None):
    tq = min(ATT_BLOCK, seq)
    nq, nsub, tk = seq // tq, tq // ATT_SUB, tq
    pairs = vb.shape[1] // LANES

    def body(q_ref, k_ref, v_ref, o_ref, lse_ref, acc_ref):
        i, p = pl.program_id(1), pl.program_id(2)
        row_lo = lax.broadcasted_iota(jnp.int32, (LANES, 1), 0) < HEAD_DIM
        qs = [q_ref[:, e * LANES:(e + 1) * LANES] for e in range(2)]
        acc_ref[...] = jnp.zeros_like(acc_ref)

        def tile(off, stats, diagonal):
            vj = v_ref[pl.ds(off, tk), :]
            new, alphas, pvs = [], [], []
            for e in range(2):
                st = _dot_nt(k_ref[pl.ds(off, tk), e * LANES:(e + 1) * LANES], qs[e])
                if diagonal:
                    st = _causal(st, 0)
                m, l = stats[e]
                m_new = jnp.maximum(m, jnp.max(st, axis=0, keepdims=True))
                alpha = jnp.exp(m - m_new)
                pt = jnp.exp(st - m_new)
                new.append((m_new, alpha * l + jnp.sum(pt, axis=0, keepdims=True)))
                alphas.append(alpha)
                pvs.append(_dot_tn(jnp.where(_head_select(e), vj, jnp.zeros_like(vj)), pt.astype(BF16)))
            acc_ref[...] = acc_ref[...] * jnp.where(row_lo, alphas[0], alphas[1]) + (pvs[0] + pvs[1])
            return tuple(new)

        init = ((jnp.full((1, tq), NEG, F32), jnp.zeros((1, tq), F32)),) * 2
        stats = lax.fori_loop(0, i, lambda j, st: tile(pl.multiple_of(j * tk, tk), st, False), init)
        (m0, l0), (m1, l1) = tile(pl.multiple_of(i * tk, tk), stats, True)
        out_t = acc_ref[...] / jnp.where(row_lo, l0, l1)
        sub = lax.broadcasted_iota(jnp.int32, (8, 1), 0)
        lse0, lse1 = m0 + jnp.log(l0), m1 + jnp.log(l1)
        for a in range(nsub):
            sl = slice(a * ATT_SUB, (a + 1) * ATT_SUB)
            o_ref[sl, :] = out_t[:, sl].T
            rows = jnp.where(sub == 2 * p, lse0[:, sl], 0.0) + jnp.where(sub == 2 * p + 1, lse1[:, sl], 0.0)
            _accumulate(lse_ref.at[a], rows, p == 0)

    return _pallas(
        body, name="attn_fwd", args=[qa, ka, vb],
        out_shape=[jax.ShapeDtypeStruct((n_batch * seq, pairs * LANES), F32),
                   jax.ShapeDtypeStruct((n_batch * seq // ATT_SUB, 8, ATT_SUB), F32)],
        grid=(n_batch, nq, pairs),
        in_specs=[pl.BlockSpec((tq, 2 * LANES), lambda b, i, p: (b * nq + i, p)),
                  pl.BlockSpec((seq, 2 * LANES), lambda b, i, p: (b, p)),
                  pl.BlockSpec((seq, LANES), lambda b, i, p: (b, p))],
        out_specs=[pl.BlockSpec((tq, LANES), lambda b, i, p: (b * nq + i, p)),
                   pl.BlockSpec((nsub, 8, ATT_SUB), lambda b, i, p: (b * nq + i, 0, 0))],
        scratch_shapes=[pltpu.VMEM((LANES, tq), F32)], plan=plan)


def _attn_bwd_q(qa, ka, vb, do, lse, delta, n_batch, seq, plan=None):
    tq = min(ATT_BLOCK, seq)
    nq, nsub, tk = seq // tq, tq // ATT_SUB, tq
    pairs = vb.shape[1] // LANES

    def body(q_ref, k_ref, v_ref, do_ref, lse_ref, dl_ref, dq_ref, dfq_ref, acc0_ref, acc1_ref):
        i, p = pl.program_id(1), pl.program_id(2)
        accs = (acc0_ref, acc1_ref)
        qs = [q_ref[:, e * LANES:(e + 1) * LANES] for e in range(2)]
        dov = do_ref[...]
        ls = [_stat_rows(lse_ref, 2 * p + e, nsub) for e in range(2)]
        dl = [_stat_rows(dl_ref, 2 * p + e, nsub) for e in range(2)]
        for acc in accs:
            acc[...] = jnp.zeros_like(acc)

        def tile(off, diagonal):
            vj = v_ref[pl.ds(off, tk), :]
            for e in range(2):
                kj = k_ref[pl.ds(off, tk), e * LANES:(e + 1) * LANES]
                st = _dot_nt(kj, qs[e])
                if diagonal:
                    st = _causal(st, 0)
                pt = jnp.exp(st - ls[e])
                dpt = _dot_nt(jnp.where(_head_select(e), vj, jnp.zeros_like(vj)), dov)
                accs[e][...] += _dot(_transpose_blocks(kj), (pt * (dpt - dl[e])).astype(BF16))

        def step(j, carry):
            tile(pl.multiple_of(j * tk, tk), False)
            return carry

        lax.fori_loop(0, i, step, 0)
        tile(pl.multiple_of(i * tk, tk), True)
        dq0, dq1 = _transpose_blocks(acc0_ref[...]), _transpose_blocks(acc1_ref[...])
        dq_ref[...] = jnp.where(_head_masks(), dq0, dq1)
        dfq = _put_lane(_pick_lane(dq0, _aug_lane(0)), 2 * p) + _put_lane(_pick_lane(dq1, _aug_lane(1)), 2 * p + 1)
        _accumulate(dfq_ref, dfq, p == 0)

    stat = pl.BlockSpec((nsub, 8, ATT_SUB), lambda b, i, p: (b * nq + i, 0, 0))
    blk = pl.BlockSpec((tq, LANES), lambda b, i, p: (b * nq + i, p))
    return _pallas(
        body, name="attn_bwd_q", args=[qa, ka, vb, do, lse, delta],
        out_shape=[jax.ShapeDtypeStruct((n_batch * seq, pairs * LANES), F32), jax.ShapeDtypeStruct((n_batch * seq, LANES), F32)],
        grid=(n_batch, nq, pairs),
        in_specs=[pl.BlockSpec((tq, 2 * LANES), lambda b, i, p: (b * nq + i, p)),
                  pl.BlockSpec((seq, 2 * LANES), lambda b, i, p: (b, p)),
                  pl.BlockSpec((seq, LANES), lambda b, i, p: (b, p)), blk, stat, stat],
        out_specs=[blk, pl.BlockSpec((tq, LANES), lambda b, i, p: (b * nq + i, 0))],
        scratch_shapes=[pltpu.VMEM((LANES, tq), F32), pltpu.VMEM((LANES, tq), F32)], plan=plan)


def _attn_bwd_kv(qa, ka, vb, do, lse, delta, n_batch, seq, plan=None):
    tkb = min(ATT_BLOCK, seq)
    nk, nsub, tq = seq // tkb, tkb // ATT_SUB, tkb
    n_tiles = seq // ATT_SUB
    pairs = vb.shape[1] // LANES

    def body(q_ref, k_ref, v_ref, do_ref, lse_ref, dl_ref, dk_ref, dv_ref, dfk_ref, dk0_ref, dk1_ref, dva_ref):
        j, p = pl.program_id(1), pl.program_id(2)
        dks = (dk0_ref, dk1_ref)
        ks = [k_ref[:, e * LANES:(e + 1) * LANES] for e in range(2)]
        vj = v_ref[...]
        vs = [jnp.where(_head_select(e), vj, jnp.zeros_like(vj)) for e in range(2)]
        for acc in (dk0_ref, dk1_ref, dva_ref):
            acc[...] = jnp.zeros_like(acc)

        def tile(t, diagonal):
            off = pl.multiple_of(t * tq, tq)
            dov = do_ref[pl.ds(off, tq), :]
            for e in range(2):
                qe = q_ref[pl.ds(off, tq), e * LANES:(e + 1) * LANES]
                st = _dot_nt(ks[e], qe)
                if diagonal:
                    st = _causal(st, 0)
                rows = lambda ref: jnp.concatenate([_pick_row(ref[t * nsub + a], 2 * p + e) for a in range(nsub)], axis=1)
                pt = jnp.exp(st - rows(lse_ref))
                dva_ref[...] += _dot(pt.astype(BF16), jnp.where(_head_select(e), dov, jnp.zeros_like(dov)))
                dst = pt * (_dot_nt(vs[e], dov) - rows(dl_ref))
                dks[e][...] += _dot(dst.astype(BF16), qe)

        def step(t, carry):
            tile(t, False)
            return carry

        lax.fori_loop(j + 1, nk, step, 0)
        tile(j, True)
        dk0, dk1 = dk0_ref[...], dk1_ref[...]
        dk_ref[...] = jnp.where(_head_masks(), dk0, dk1)
        dv_ref[...] = dva_ref[...].astype(BF16)
        dfk = (_put_lane(_pick_lane(dk0, _aug_lane(0) + 3), 2 * p)
               + _put_lane(_pick_lane(dk1, _aug_lane(1) + 3), 2 * p + 1))
        _accumulate(dfk_ref, -dfk, p == 0)

    stat = pl.BlockSpec((n_tiles, 8, ATT_SUB), lambda b, j, p: (b, 0, 0))
    blk = pl.BlockSpec((tkb, LANES), lambda b, j, p: (b * nk + j, p))
    acc = pltpu.VMEM((tkb, LANES), F32)
    return _pallas(
        body, name="attn_bwd_kv", args=[qa, ka, vb, do, lse, delta],
        out_shape=[jax.ShapeDtypeStruct((n_batch * seq, pairs * LANES), F32),
                   jax.ShapeDtypeStruct((n_batch * seq, pairs * LANES), BF16),
                   jax.ShapeDtypeStruct((n_batch * seq, LANES), F32)],
        grid=(n_batch, nk, pairs),
        in_specs=[pl.BlockSpec((seq, 2 * LANES), lambda b, j, p: (b, p)),
                  pl.BlockSpec((tkb, 2 * LANES), lambda b, j, p: (b * nk + j, p)), blk,
                  pl.BlockSpec((seq, LANES), lambda b, j, p: (b, p)), stat, stat],
        out_specs=[blk, blk, pl.BlockSpec((tkb, LANES), lambda b, j, p: (b * nk + j, 0))],
        scratch_shapes=[acc, acc, acc], plan=plan)


def _forget_bwd(dfq, dfk, f, bias, n_batch, seq):
    def body(dfq_ref, dfk_ref, f_ref, b_ref, df_ref, db_ref):
        acc = dfq_ref[...] + dfk_ref[...]
        row = lax.broadcasted_iota(jnp.int32, (seq, 1), 0)
        dist = 1
        while dist < seq:
            acc = acc + _shift_up(acc, dist, row, seq)
            dist *= 2
        df = acc * _sigmoid(-(f_ref[...] + b_ref[...]))
        df_ref[...] = df
        db_ref[...] = jnp.sum(df, axis=0, keepdims=True)

    col = pl.BlockSpec((seq, LANES), lambda b: (b, 0))
    return pl.pallas_call(
        body,
        out_shape=[jax.ShapeDtypeStruct((n_batch * seq, LANES), F32), jax.ShapeDtypeStruct((n_batch, 1, LANES), F32)],
        grid=(n_batch,), in_specs=[col, col, col, pl.BlockSpec((1, LANES), lambda b: (0, 0))],
        out_specs=[col, pl.BlockSpec((None, 1, LANES), lambda b: (b, 0, 0))],
        compiler_params=_params(), name="forget_bwd",
    )(dfq, dfk, f, bias)


def _mix_out(x1, yp, o, ona, woa, wob):
    t, d = x1.shape
    width = o.shape[1]
    tm = min(512, t)

    def body(x_ref, yp_ref, o_ref, on_ref, wa_ref, wb_ref, x2_ref, ya_ref):
        of = o_ref[...]
        ya = ((of * _rms(of)) * on_ref[...]).astype(BF16)
        ya_ref[...] = ya
        x2_ref[...] = x_ref[...] + (_dot(yp_ref[...], wa_ref[...]) + _dot(ya, wb_ref[...]))

    row = pl.BlockSpec((tm, d), lambda i: (i, 0))
    half = pl.BlockSpec((tm, width), lambda i: (i, 0))
    wspec = pl.BlockSpec((width, d), lambda i: (0, 0))
    return pl.pallas_call(
        body, out_shape=[jax.ShapeDtypeStruct((t, d), F32), jax.ShapeDtypeStruct((t, width), BF16)],
        grid=(t // tm,), in_specs=[row, half, half, pl.BlockSpec((1, width), lambda i: (0, 0)), wspec, wspec],
        out_specs=[row, half], compiler_params=_params(), name="mix_out",
    )(x1, yp, o, ona, woa, wob)


def _mix_out_bwd(dx2, o, yp, ya, ona, woa, wob, plan=None):
    t, d = dx2.shape
    width = o.shape[1]
    tm = min(512, t)
    nt = t // tm

    def body(dx_ref, o_ref, yp_ref, ya_ref, on_ref, wa_ref, wb_ref, dyp_ref, do_ref, dl_ref, dwa_ref, dwb_ref, don_ref):
        @pl.when(pl.program_id(0) == 0)
        def _():
            dwa_ref[...] = jnp.zeros_like(dwa_ref)
            dwb_ref[...] = jnp.zeros_like(dwb_ref)

        dxb = dx_ref[...].astype(BF16)
        dwa_ref[...] += _dot_tn(yp_ref[...], dxb)
        dwb_ref[...] += _dot_tn(ya_ref[...], dxb)
        dyp_ref[...] = _dot_nt(dxb, wa_ref[...])
        of = o_ref[...]
        dov, dgr = _rms_bwd(of, _rms(of), on_ref[...], _dot_nt(dxb, wb_ref[...]))
        don_ref[...] = jnp.sum(dgr, axis=0, keepdims=True)
        do_ref[...] = dov.astype(BF16)
        lo = _head_masks()
        prod = dov * of
        delta = jnp.zeros((tm, LANES), F32)
        for blk in range(width // LANES):
            pb = prod[:, blk * LANES:(blk + 1) * LANES]
            delta = delta + _put_lane(jnp.sum(jnp.where(lo, pb, 0.0), axis=1, keepdims=True), 2 * blk)
            delta = delta + _put_lane(jnp.sum(jnp.where(lo, 0.0, pb), axis=1, keepdims=True), 2 * blk + 1)
        for c in range(tm // ATT_SUB):
            dl_ref[c] = delta[c * ATT_SUB:(c + 1) * ATT_SUB, :].T[0:8, :]

    row = pl.BlockSpec((tm, d), lambda i: (i, 0))
    half = pl.BlockSpec((tm, width), lambda i: (i, 0))
    wspec = pl.BlockSpec((width, d), lambda i: (0, 0))
    return _pallas(
        body, name="mix_out_bwd", args=[dx2, o, yp, ya, ona, woa, wob],
        out_shape=[jax.ShapeDtypeStruct((t, width), F32), jax.ShapeDtypeStruct((t, width), BF16),
                   jax.ShapeDtypeStruct((t // ATT_SUB, 8, ATT_SUB), F32), jax.ShapeDtypeStruct((width, d), F32),
                   jax.ShapeDtypeStruct((width, d), F32), jax.ShapeDtypeStruct((nt, 1, width), F32)],
        grid=(nt,),
        in_specs=[row, half, half, half, pl.BlockSpec((1, width), lambda i: (0, 0)), wspec, wspec],
        out_specs=[half, half, pl.BlockSpec((tm // ATT_SUB, 8, ATT_SUB), lambda i: (i, 0, 0)), wspec, wspec,
                   pl.BlockSpec((None, 1, width), lambda i: (i, 0, 0))], plan=plan)


def _mix_in_bwd(dx2, x1, gain, hm, dpv, dqh, q, dkh, k, dv, df, qn, kn, wt):
    t, d = x1.shape
    width = q.shape[1]
    pool_width = dpv.shape[1]
    tm = min(512, t)
    nt = t // tm
    scale = HEAD_DIM ** -0.5
    c_q, c_k, c_v = pool_width, pool_width + width, pool_width + 2 * width
    c_f = c_v + width

    def body(dx2_ref, x_ref, g_ref, hm_ref, dpv_ref, dqh_ref, q_ref, dkh_ref, k_ref, dv_ref, df_ref, qn_ref, kn_ref,
             wt_ref, dx_ref, dwt_ref, dg_ref, dqn_ref, dkn_ref):
        @pl.when(pl.program_id(0) == 0)
        def _():
            dwt_ref[...] = jnp.zeros_like(dwt_ref)

        lo = _head_masks()
        hm = hm_ref[...]
        pieces = [(0, dpv_ref[...])]
        for c0, raw_ref, dh_ref, n_ref, dn_ref, mul in ((c_q, q_ref, dqh_ref, qn_ref, dqn_ref, scale),
                                                       (c_k, k_ref, dkh_ref, kn_ref, dkn_ref, 1.0)):
            cols = []
            for blk in range(width // LANES):
                sl = slice(blk * LANES, (blk + 1) * LANES)
                xb = raw_ref[:, sl]
                gb = dh_ref[:, sl] * mul
                r = _head_rms(xb, lo)
                xh = xb * r
                dyg = gb * n_ref[:, sl]
                cols.append((r * (dyg - xh * _head_mean(dyg * xh, lo))).astype(BF16))
                dn_ref[:, sl] = jnp.sum(gb * xh, axis=0, keepdims=True)
            pieces.append((c0, jnp.concatenate(cols, axis=1)))
        pieces.append((c_v, dv_ref[...]))
        pieces.append((c_f, df_ref[...].astype(BF16)))
        dhm = jnp.zeros((tm, d), F32)
        for c0, piece in pieces:
            dwt_ref[c0:c0 + piece.shape[1], :] += _dot_tn(piece, hm)
            dhm = dhm + _dot(piece, wt_ref[c0:c0 + piece.shape[1], :])
        xf = x_ref[...]
        dxn, dgr = _rms_bwd(xf, _rms(xf), g_ref[...], dhm)
        dx_ref[...] = dx2_ref[...] + dxn
        dg_ref[...] = jnp.sum(dgr, axis=0, keepdims=True)

    row = pl.BlockSpec((tm, d), lambda i: (i, 0))
    half = pl.BlockSpec((tm, width), lambda i: (i, 0))
    const = lambda shape: pl.BlockSpec(shape, lambda i: (0, 0))
    pvec = lambda n: pl.BlockSpec((None, 1, n), lambda i: (i, 0, 0))
    return pl.pallas_call(
        body,
        out_shape=[jax.ShapeDtypeStruct((t, d), F32), jax.ShapeDtypeStruct(wt.shape, F32),
                   jax.ShapeDtypeStruct((nt, 1, d), F32),
                   jax.ShapeDtypeStruct((nt, 1, width), F32), jax.ShapeDtypeStruct((nt, 1, width), F32)],
        grid=(nt,),
        in_specs=[row, row, const((1, d)), row, pl.BlockSpec((tm, pool_width), lambda i: (i, 0)), half, half, half, half,
                  half, pl.BlockSpec((tm, LANES), lambda i: (i, 0)), const((1, width)), const((1, width)),
                  const(wt.shape)],
        out_specs=[row, const(wt.shape), pvec(d), pvec(width), pvec(width)],
        compiler_params=_params(), name="mix_in_bwd",
    )(dx2, x1, gain, hm, dpv, dqh, q, dkh, k, dv, df, qn, kn, wt)


def _mesh_pos():
    return lax.axis_index("x"), lax.axis_index("y"), lax.axis_index("c")


def _other_chips(x, y):
    return [(1 - x, y), (x, 1 - y), (1 - x, 1 - y)]


def _remote(src, dst, send_sem, recv_sem, device):
    return pltpu.make_async_remote_copy(src_ref=src, dst_ref=dst, send_sem=send_sem, recv_sem=recv_sem,
                                        device_id=device, device_id_type=pl.DeviceIdType.MESH)


def _half_rows(n_rows, which):
    half = n_rows // 2
    return pl.ds(pl.multiple_of(which * half, 8), half)


def _row_block(rows, cols, itemsize=4):
    rb = rows
    while rb * cols * itemsize > (1 << 20) and rb % 32 == 0:
        rb //= 2
    return rb


def _place_cast(ws, chip, tag):
    n = len(ws)
    rows, cols = ws[0].shape
    rb = _row_block(rows, cols)

    def body(k_ref, *refs):
        for w_ref, o_ref in zip(refs[:n], refs[n:]):
            o_ref[...] = w_ref[...].astype(BF16)

    return pl.pallas_call(
        body, out_shape=[jax.ShapeDtypeStruct((N_CHIPS, rows, cols), BF16)] * n,
        grid_spec=pltpu.PrefetchScalarGridSpec(
            num_scalar_prefetch=1, grid=(rows // rb,),
            in_specs=[pl.BlockSpec((rb, cols), lambda i, k: (i, 0))] * n,
            out_specs=[pl.BlockSpec((None, rb, cols), lambda i, k: (k[0], i, 0))] * n),
        compiler_params=_params(), name="place_" + tag,
    )(chip, *ws)


class _Plan:
    def __init__(self, ins, outs, alias, sems, start, finish):
        self.ins, self.outs, self.alias, self.sems, self.start, self.finish = ins, outs, alias, sems, start, finish


def _merge_plans(a, b):
    ni, no, ns = len(a.ins), len(a.outs), len(a.sems)
    alias = dict(a.alias)
    alias.update({ni + i: no + o for i, o in b.alias.items()})

    def both(which):
        def run(ins, outs, sems):
            getattr(a, which)(ins[:ni], outs[:no], sems[:ns])
            getattr(b, which)(ins[ni:], outs[no:], sems[ns:])
        return run

    return _Plan(list(a.ins) + list(b.ins), list(a.outs) + list(b.outs), alias, list(a.sems) + list(b.sems),
                 both("start"), both("finish"))


def _run_plan(plan, name):
    n_in, n_out = len(plan.ins), len(plan.outs)

    def body(*refs):
        parts = refs[:n_in], refs[n_in:n_in + n_out], refs[n_in + n_out:]
        plan.start(*parts)
        plan.finish(*parts)

    return pl.pallas_call(
        body, out_shape=plan.outs, in_specs=[ANY] * n_in, out_specs=[ANY] * n_out, scratch_shapes=plan.sems,
        input_output_aliases=plan.alias, name=name,
    )(*plan.ins)


def _pallas(body, *, name, args, in_specs, out_shape, out_specs, grid, scratch_shapes=(), plan=None, aliases=None):
    n_in, n_out, n_scr = len(args), len(out_shape), len(scratch_shapes)
    aliases = dict(aliases or {})
    if plan is None:
        res = pl.pallas_call(body, out_shape=out_shape, grid=grid, in_specs=in_specs, out_specs=out_specs,
                             scratch_shapes=scratch_shapes, input_output_aliases=aliases,
                             compiler_params=_params(), name=name)(*args)
        return list(res), []
    p_in, p_out = len(plan.ins), len(plan.outs)

    def carrying(*refs):
        ins, p_ins = refs[:n_in], refs[n_in:n_in + p_in]
        o0 = n_in + p_in
        outs, p_outs = refs[o0:o0 + n_out], refs[o0 + n_out:o0 + n_out + p_out]
        s0 = o0 + n_out + p_out
        scr, p_sems = refs[s0:s0 + n_scr], refs[s0 + n_scr:]
        ids = [pl.program_id(a) for a in range(len(grid))]
        first = functools.reduce(jnp.logical_and, [i == 0 for i in ids])
        last = functools.reduce(jnp.logical_and, [i == g - 1 for i, g in zip(ids, grid)])

        @pl.when(first)
        def _():
            plan.start(p_ins, p_outs, p_sems)

        body(*ins, *outs, *scr)

        @pl.when(last)
        def _():
            plan.finish(p_ins, p_outs, p_sems)

    res = pl.pallas_call(
        carrying, out_shape=list(out_shape) + list(plan.outs), grid=grid,
        in_specs=list(in_specs) + [ANY] * p_in, out_specs=list(out_specs) + [ANY] * p_out,
        scratch_shapes=list(scratch_shapes) + list(plan.sems),
        input_output_aliases={**aliases, **{n_in + i: n_out + o for i, o in plan.alias.items()}},
        compiler_params=_params(), name=name,
    )(*args, *plan.ins)
    return list(res[:n_out]), list(res[n_out:])


def _plan_gather(stacks):
    n = len(stacks)

    def ici_copies(outs, sems):
        x, y, c = _mesh_pos()
        cps = []
        for w in range(n):
            own = outs[w].at[2 * x + y, _half_rows(stacks[w].shape[1], c)]
            cps += [_remote(own, own, sems[0].at[w, j], sems[1].at[w, j], (*chip, c)) for j, chip in enumerate(_other_chips(x, y))]
        return cps

    def start(ins, outs, sems):
        for cp in ici_copies(outs, sems):
            cp.start()

    def finish(ins, outs, sems):
        ici_send, ici_recv, d2d_send, d2d_recv = sems
        x, y, c = _mesh_pos()
        sibling = (x, y, 1 - c)
        slots = [2 * cx + cy for cx, cy in _other_chips(x, y)]
        forwards = []
        for w in range(n):
            rows = _half_rows(stacks[w].shape[1], c)
            for j in range(3):
                landed = outs[w].at[slots[j], rows]
                _remote(landed, landed, ici_send.at[w, j], ici_recv.at[w, j], sibling).wait_recv()
                cp = _remote(landed, landed, d2d_send.at[w, j], d2d_recv.at[w, j], sibling)
                cp.start()
                forwards.append(cp)
        for w in range(n):
            rows = _half_rows(stacks[w].shape[1], 1 - c)
            for j in range(3):
                landed = outs[w].at[slots[j], rows]
                _remote(landed, landed, d2d_send.at[w, j], d2d_recv.at[w, j], sibling).wait_recv()
        for cp in ici_copies(outs, sems) + forwards:
            cp.wait_send()

    return _Plan(stacks, [jax.ShapeDtypeStruct(s.shape, s.dtype) for s in stacks], {w: w for w in range(n)},
                 [pltpu.SemaphoreType.DMA((n, 3))] * 4, start, finish)


def _plan_sibling_halves(gs):
    n = len(gs)

    def copies(ins, outs, sems):
        x, y, c = _mesh_pos()
        return [_remote(ins[w].at[:, _half_rows(gs[w].shape[1], 1 - c), :], outs[w], sems[0].at[w], sems[1].at[w],
                        (x, y, 1 - c)) for w in range(n)]

    def start(ins, outs, sems):
        for cp in copies(ins, outs, sems):
            cp.start()

    def finish(ins, outs, sems):
        for cp in copies(ins, outs, sems):
            cp.wait()

    return _Plan(gs, [jax.ShapeDtypeStruct((g.shape[0], g.shape[1] // 2, g.shape[2]), g.dtype) for g in gs], {},
                 [pltpu.SemaphoreType.DMA((n,))] * 2, start, finish)


def _plan_chip_exchange(ps):
    n = len(ps)

    def copies(ins, outs, sems):
        x, y, c = _mesh_pos()
        return [_remote(ins[w].at[2 * cx + cy], outs[w].at[j], sems[0].at[w, j], sems[1].at[w, j], (cx, cy, c))
                for w in range(n) for j, (cx, cy) in enumerate(_other_chips(x, y))]

    def start(ins, outs, sems):
        for cp in copies(ins, outs, sems):
            cp.start()

    def finish(ins, outs, sems):
        for cp in copies(ins, outs, sems):
            cp.wait()

    return _Plan(ps, [jax.ShapeDtypeStruct((3,) + p.shape[1:], p.dtype) for p in ps], {},
                 [pltpu.SemaphoreType.DMA((n, 3))] * 2, start, finish)


def _plan_sibling_share(gs):
    n = len(gs)

    def copies(outs, sems, which):
        x, y, c = _mesh_pos()
        cps = []
        for w in range(n):
            rows = outs[w].at[_half_rows(gs[w].shape[0], c if which == "mine" else 1 - c)]
            cps.append(_remote(rows, rows, sems[0].at[w], sems[1].at[w], (x, y, 1 - c)))
        return cps

    def start(ins, outs, sems):
        for cp in copies(outs, sems, "mine"):
            cp.start()

    def finish(ins, outs, sems):
        for cp in copies(outs, sems, "mine"):
            cp.wait_send()
        for cp in copies(outs, sems, "theirs"):
            cp.wait_recv()

    return _Plan(gs, [jax.ShapeDtypeStruct(g.shape, g.dtype) for g in gs], {w: w for w in range(n)},
                 [pltpu.SemaphoreType.DMA((n,))] * 2, start, finish)


def _same_shape_groups(arrays):
    groups = {}
    for i, a in enumerate(arrays):
        groups.setdefault(a.shape, []).append(i)
    return list(groups.values())


def _add_sibling(gs, r1s, ids, tag):
    n = len(gs)
    nch, rh, cols = r1s[0].shape

    def body(ids_ref, *refs):
        for g_ref, r_ref, o_ref in zip(refs[:n], refs[n:2 * n], refs[2 * n:]):
            o_ref[...] = (g_ref[...] + r_ref[...]).astype(BF16)

    blk = lambda fn: pl.BlockSpec((None, rh, cols), fn)
    return pl.pallas_call(
        body, out_shape=[jax.ShapeDtypeStruct(r1s[0].shape, BF16)] * n,
        grid_spec=pltpu.PrefetchScalarGridSpec(
            num_scalar_prefetch=1, grid=(nch,),
            in_specs=[blk(lambda k, ids: (k, ids[1], 0))] * n + [blk(lambda k, ids: (k, 0, 0))] * n,
            out_specs=[blk(lambda k, ids: (k, 0, 0))] * n),
        compiler_params=_params(), name="add_sibling_" + tag,
    )(ids, *gs, *r1s)


def _add_chips(gs, r1s, r2s, ids, tag):
    n = len(gs)
    _, rh, cols = r1s[0].shape
    nb = 2 if rh % 32 == 0 else 1
    rb = rh // nb

    def body(ids_ref, *refs):
        for g_ref, r1_ref, r2_ref, o_ref in zip(refs[:n], refs[n:2 * n], refs[2 * n:3 * n], refs[3 * n:]):
            own = g_ref[...] + r1_ref[...]
            o_ref[...] = ((own + r2_ref[0].astype(F32)) + r2_ref[1].astype(F32)) + r2_ref[2].astype(F32)

    return pl.pallas_call(
        body, out_shape=[jax.ShapeDtypeStruct((2 * rh, cols), F32)] * n,
        grid_spec=pltpu.PrefetchScalarGridSpec(
            num_scalar_prefetch=1, grid=(nb,),
            in_specs=[pl.BlockSpec((None, rb, cols), lambda i, ids: (ids[0], ids[1] * nb + i, 0))] * n
            + [pl.BlockSpec((None, rb, cols), lambda i, ids: (ids[0], i, 0))] * n
            + [pl.BlockSpec((3, rb, cols), lambda i, ids: (0, i, 0))] * n,
            out_specs=[pl.BlockSpec((rb, cols), lambda i, ids: (ids[1] * nb + i, 0))] * n),
        compiler_params=_params(), name="add_chips_" + tag,
    )(ids, *gs, *r1s, *r2s)


VEC_ROWS = 8


N_DEVICES = 8


def _small_pack(part, d, width):
    names = ("ffn1_norm", "mix_norm", "ffn2_norm", "pool_scale", "out_norm_pool", "out_norm_attn", "qn", "kn", "b_forget",
             "pool_w", "loss")
    args = [part[k] for k in names]
    pw_shape = part["pool_w"].shape[1:]

    def body(g1_ref, gm_ref, g2_ref, ps_ref, onp_ref, ona_ref, qn_ref, kn_ref, bf_ref, pw_ref, loss_ref, vbuf, pbuf):
        lo = _head_masks()

        def fold_heads(ref):
            v = jnp.sum(ref[...], axis=0)
            acc = jnp.zeros((VEC_ROWS, LANES), F32)
            for blk in range(width // LANES):
                vb = jnp.broadcast_to(v[:, blk * LANES:(blk + 1) * LANES], (VEC_ROWS, LANES))
                acc = acc + vb + pltpu.roll(vb, HEAD_DIM, 1)
            return jnp.where(lo, acc, 0.0)[0:1, :]

        vbuf[0] = jnp.zeros((VEC_ROWS, d), F32)
        vbuf[0, 0:1, :] = jnp.sum(g1_ref[...], axis=0)
        vbuf[0, 1:2, :] = jnp.sum(gm_ref[...], axis=0)
        vbuf[0, 2:3, :] = jnp.sum(g2_ref[...], axis=0)
        vbuf[0, 5:6, 0:LANES] = jnp.sum(loss_ref[...], axis=0)[0:1, :]
        vbuf[0, 3:4, 0:width] = jnp.sum(ps_ref[...], axis=0)
        vbuf[0, 3:4, width:2 * width] = jnp.sum(onp_ref[...], axis=0)
        vbuf[0, 4:5, 0:width] = jnp.sum(ona_ref[...], axis=0)
        vbuf[0, 4:5, width:width + LANES] = fold_heads(qn_ref)
        vbuf[0, 4:5, width + LANES:width + 2 * LANES] = fold_heads(kn_ref)
        vbuf[0, 4:5, width + 2 * LANES:width + 3 * LANES] = jnp.sum(bf_ref[...], axis=0)
        pbuf[0] = jnp.sum(pw_ref[...], axis=0)

    return pl.pallas_call(
        body, out_shape=[jax.ShapeDtypeStruct((N_DEVICES, VEC_ROWS, d), F32), jax.ShapeDtypeStruct((N_DEVICES,) + pw_shape, F32)],
        in_specs=[VM] * len(args), out_specs=[VM, VM], compiler_params=_params(), name="small_pack",
    )(*args)


def _plan_all_to_all(stacks):
    n = len(stacks)

    def copies(outs, sems):
        x, y, c = _mesh_pos()
        cps = []
        for r in range(1, N_DEVICES):
            peer = (x if not r & 4 else 1 - x, y if not r & 2 else 1 - y, c if not r & 1 else 1 - c)
            cps += [_remote(outs[w].at[0], outs[w].at[r], sems[0].at[w, r - 1], sems[1].at[w, r - 1], peer) for w in range(n)]
        return cps

    def start(ins, outs, sems):
        for cp in copies(outs, sems):
            cp.start()

    def finish(ins, outs, sems):
        for cp in copies(outs, sems):
            cp.wait()

    return _Plan(stacks, [jax.ShapeDtypeStruct(s.shape, s.dtype) for s in stacks], {w: w for w in range(n)},
                 [pltpu.SemaphoreType.DMA((n, N_DEVICES - 1))] * 2, start, finish)


def _small_sum(vstack, pstack, me):
    def body(me_ref, vbuf, pbuf, vec_ref, pw_ref):
        vec = vbuf[me_ref[0]]
        pw = pbuf[me_ref[0]]
        for dev in range(1, N_DEVICES):
            vec = vec + vbuf[jnp.bitwise_xor(me_ref[0], dev)]
            pw = pw + pbuf[jnp.bitwise_xor(me_ref[0], dev)]
        vec_ref[...] = vec
        pw_ref[...] = pw

    full = lambda s: pl.BlockSpec(s.shape, lambda i, me: (0,) * len(s.shape))
    outs = [jax.ShapeDtypeStruct(vstack.shape[1:], F32), jax.ShapeDtypeStruct(pstack.shape[1:], F32)]
    return pl.pallas_call(
        body, out_shape=outs,
        grid_spec=pltpu.PrefetchScalarGridSpec(num_scalar_prefetch=1, grid=(1,), in_specs=[full(vstack), full(pstack)],
                                               out_specs=[full(o) for o in outs]),
        compiler_params=_params(), name="small_sum",
    )(me, vstack, pstack)


def _adamw(ws, gs, ms, vs, tag):
    n = len(ws)
    rows, cols = ws[0].shape
    rb = rows
    while rb * cols * 4 * n > (1 << 20) and rb % 16 == 0:
        rb //= 2

    def body(*refs):
        for j in range(n):
            w_ref, g_ref, m_ref, v_ref = (refs[k * n + j] for k in range(4))
            d_ref, mo_ref, vo_ref = (refs[(4 + k) * n + j] for k in range(3))
            gv = g_ref[...]
            m2 = ADAM_B1 * m_ref[...] + (1.0 - ADAM_B1) * gv
            v2 = ADAM_B2 * v_ref[...] + (1.0 - ADAM_B2) * (gv * gv)
            m_hat = m2 / (1.0 - ADAM_B1 ** ADAM_STEP)
            v_hat = v2 / (1.0 - ADAM_B2 ** ADAM_STEP)
            d_ref[...] = -ADAM_LR * (m_hat / (jnp.sqrt(v_hat) + ADAM_EPS) + ADAM_WD * w_ref[...])
            mo_ref[...] = m2
            vo_ref[...] = v2

    spec = pl.BlockSpec((rb, cols), lambda i: (i, 0))
    res = pl.pallas_call(
        body, out_shape=[jax.ShapeDtypeStruct(ws[0].shape, F32)] * (3 * n), grid=(rows // rb,),
        in_specs=[spec] * (4 * n), out_specs=[spec] * (3 * n), compiler_params=_params(), name="adamw_" + tag,
    )(*ws, *gs, *ms, *vs)
    return [(res[j], res[n + j], res[2 * n + j]) for j in range(n)]


def _pack_vec(p, d, width):
    pad = lambda v: jnp.pad(v, (0, LANES - v.shape[0]))
    row3 = jnp.concatenate([p["pool_scale"], p["out_norm_pool"]])
    row4 = jnp.concatenate([p["out_norm_attn"], pad(p["q_norm"]), pad(p["k_norm"]), pad(p["b_forget"]),
                            jnp.zeros((d - width - 3 * LANES,), F32)])
    rows = [p["ffn1_norm"], p["mix_norm"], p["ffn2_norm"], row3, row4]
    return jnp.pad(jnp.stack(rows), ((0, VEC_ROWS - len(rows)), (0, 0)))


def _unpack_vec(vec, width):
    return dict(ffn1_norm=vec[0], mix_norm=vec[1], ffn2_norm=vec[2], pool_scale=vec[3, :width],
                out_norm_pool=vec[3, width:2 * width], out_norm_attn=vec[4, :width],
                q_norm=vec[4, width:width + HEAD_DIM], k_norm=vec[4, width + LANES:width + LANES + HEAD_DIM],
                b_forget=vec[4, width + 2 * LANES:width + 2 * LANES + N_HEADS])


WEIGHT_NAMES = ("ffn1_norm", "ffn1_w_gate", "ffn1_w_up", "ffn1_w_down", "mix_norm", "w_in", "b_forget", "pool_w",
                "pool_scale", "q_norm", "k_norm", "out_norm_pool", "out_norm_attn", "w_out", "ffn2_norm",
                "ffn2_w_gate", "ffn2_w_up", "ffn2_w_down")
BIG_NAMES = ("ffn1_w_gate", "ffn1_w_up", "ffn1_w_down", "w_in", "w_out", "ffn2_w_gate", "ffn2_w_up", "ffn2_w_down")
TRANSPOSED_NAMES = ("ffn1_w_gate", "ffn1_w_up", "w_in", "ffn2_w_gate", "ffn2_w_up")
FFN1_NAMES = ("ffn1_w_gate", "ffn1_w_up", "ffn1_w_down")
MIX_NAMES = ("w_in", "w_out")
FFN2_NAMES = ("ffn2_w_gate", "ffn2_w_up", "ffn2_w_down")


def kernel(x, ffn1_norm, ffn1_w_gate, ffn1_w_up, ffn1_w_down, mix_norm, w_in, b_forget, pool_w, pool_scale, q_norm, k_norm, out_norm_pool, out_norm_attn, w_out, ffn2_norm, ffn2_w_gate, ffn2_w_up, ffn2_w_down, loss_target, m_ffn1_norm, m_ffn1_w_gate, m_ffn1_w_up, m_ffn1_w_down, m_mix_norm, m_w_in, m_b_forget, m_pool_w, m_pool_scale, m_q_norm, m_k_norm, m_out_norm_pool, m_out_norm_attn, m_w_out, m_ffn2_norm, m_ffn2_w_gate, m_ffn2_w_up, m_ffn2_w_down, v_ffn1_norm, v_ffn1_w_gate, v_ffn1_w_up, v_ffn1_w_down, v_mix_norm, v_w_in, v_b_forget, v_pool_w, v_pool_scale, v_q_norm, v_k_norm, v_out_norm_pool, v_out_norm_attn, v_w_out, v_ffn2_norm, v_ffn2_w_gate, v_ffn2_w_up, v_ffn2_w_down):
    given = dict(locals())
    w = {n: given[n] for n in WEIGHT_NAMES}
    m = {n: given["m_" + n] for n in WEIGHT_NAMES}
    v = {n: given["v_" + n] for n in WEIGHT_NAMES}
    n_batch, seq, d = x.shape
    width = pool_scale.shape[0]
    in_rows = w_in.shape[1]
    in_cols = N_CHIPS * in_rows
    in_pad = -(-in_rows // 32) * 32
    in_cols_pad = in_cols - N_HEADS + LANES

    work = lambda a, n: a.T if n in TRANSPOSED_NAMES else a
    exchanged = lambda a, n: jnp.pad(a, ((0, in_pad - in_rows), (0, 0))) if n == "w_in" else a

    mesh_x, mesh_y, mesh_c = _mesh_pos()
    ids = jnp.stack([2 * mesh_x + mesh_y, mesh_c]).astype(jnp.int32)

    row = lambda a: a.reshape(1, -1)
    g1, gm, g2, ps, onp, ona = (row(a) for a in (ffn1_norm, mix_norm, ffn2_norm, pool_scale, out_norm_pool, out_norm_attn))
    qn, kn = row(jnp.tile(q_norm, N_HEADS)), row(jnp.tile(k_norm, N_HEADS))
    bf = row(jnp.pad(b_forget, (0, LANES - N_HEADS)))
    pwb = pool_w.astype(BF16)
    xf, tgt = x.reshape(n_batch * seq, d), loss_target.reshape(n_batch * seq, d)

    def grouped(call, names, *lists):
        out = [None] * len(names)
        for idx in _same_shape_groups(lists[0]):
            res = call(*[[lst[i] for i in idx] for lst in lists], names[idx[0]])
            for i, r in zip(idx, res):
                out[i] = r
        return out

    placed = dict(zip(BIG_NAMES, grouped(lambda ws, tag: _place_cast(ws, ids, tag), BIG_NAMES,
                                         [exchanged(work(w[n], n), n) for n in BIG_NAMES])))
    wg1, wu1, wd1 = _run_plan(_plan_gather([placed[n] for n in FFN1_NAMES]), "gather_ffn1")
    (x1, h1, a1, b1, s1), (w_in_all, w_out_all, wd2) = _ffn_fwd(
        xf, g1, wg1, wu1, wd1, plan=_plan_gather([placed[n] for n in MIX_NAMES + FFN2_NAMES[2:]]))
    w_in_t = jnp.pad(w_in_all[:, :in_rows].reshape(in_cols, d), ((0, in_cols_pad - in_cols), (0, 0)))
    w_out_full = w_out_all.reshape(N_CHIPS * w_out.shape[0], d)
    woa, wob = w_out_full[:width], w_out_full[width:]

    hm, pv, q, k, qh, kh, vb, f = _mix_proj(x1, gm, w_in_t, qn, kn, width, width)
    qa, ka = _forget_prefix(f, bf, qh, kh, n_batch, seq)
    yp = _pool_fwd(pv, pwb, ps, onp, n_batch, seq)
    (o, lse), (wg2, wu2) = _attn_fwd(qa, ka, vb, n_batch, seq, plan=_plan_gather([placed[n] for n in FFN2_NAMES[:2]]))
    x2, ya = _mix_out(x1, yp, o, ona, woa, wob)
    (dy, h2, a2, b2, s2, lpart), _ = _ffn_fwd(x2, g2, wg2, wu2, wd2, target=tgt)

    def to_chips(gs, arrived, tags):
        return grouped(lambda g, r, tag: _add_sibling(g, r, ids, tag), tags, gs, arrived)

    def own_rows(gs, from_sibling, from_chips, tags):
        return grouped(lambda g, ra, rb, tag: _add_chips(g, ra, rb, ids, tag), tags, gs, from_sibling, from_chips)

    (dx2, da2, db2, dg2), _ = _ffn_bwd_x(dy, x2, g2, a2, b2, wg2, wu2, wd2, "ffn2_bwd_x")
    dw2, _ = _ffn_bwd_w([(da2, h2, 1.0), (db2, h2, 1.0), (s2, dy, 0.5)], "ffn2_bwd_w")
    (dyp, do, delta, dwoa, dwob, dona), sib2 = _mix_out_bwd(dx2, o, yp, ya, ona, woa, wob, plan=_plan_sibling_halves(dw2))
    dpv, dpw, dps, donp = _pool_bwd(pv, dyp, pwb, ps, onp, n_batch, seq)
    (dqh, dfq), chips2 = _attn_bwd_q(qa, ka, vb, do, lse, delta, n_batch, seq,
                                     plan=_plan_chip_exchange(to_chips(dw2, sib2, FFN2_NAMES)))
    (dkh, dv, dfk), red2 = _attn_bwd_kv(qa, ka, vb, do, lse, delta, n_batch, seq,
                                        plan=_plan_sibling_share(own_rows(dw2, sib2, chips2, FFN2_NAMES)))
    df, dbf = _forget_bwd(dfq, dfk, f, bf, n_batch, seq)
    dx1, dw_in_t, dgm, dqn, dkn = _mix_in_bwd(dx2, x1, gm, hm, dpv, dqh, q, dkh, k, dv, df, qn, kn, w_in_t)
    in_base = [in_rows * k // 8 * 8 for k in range(N_CHIPS)]
    d_w_in = jnp.stack([dw_in_t[b:b + in_pad] for b in in_base])
    d_w_out = jnp.concatenate([dwoa, dwob], axis=0).reshape(N_CHIPS, w_out.shape[0], d)
    dwm = [d_w_in, d_w_out]
    down, gate_up = FFN1_NAMES[2:], FFN1_NAMES[:2]
    dwd1, sibm = _ffn_bwd_w([(s1, dx1, 0.5)], "ffn1_bwd_w_down", plan=_plan_sibling_halves(dwm))
    (da1, db1), arrived = _ffn_bwd_a(dx1, a1, b1, wd1, "ffn1_bwd_a",
                                     plan=_merge_plans(_plan_sibling_halves(dwd1),
                                                       _plan_chip_exchange(to_chips(dwm, sibm, MIX_NAMES))))
    sibd, chipsm = arrived[:1], arrived[1:]
    dwgu1, chipsd = _ffn_bwd_w([(da1, h1, 1.0), (db1, h1, 1.0)], "ffn1_bwd_w_gate_up",
                               plan=_plan_chip_exchange(to_chips(dwd1, sibd, down)))
    n_tiles = (n_batch * seq) // min(512, n_batch * seq)
    first = max(n_tiles // 4, 1)
    begun, sibgu = _ffn_bwd_h(dx1, xf, g1, da1, db1, wg1, wu1, "ffn1_bwd_h_first", (0, first),
                              plan=_plan_sibling_halves(dwgu1))
    (gx, dg1), chipsgu = _ffn_bwd_h(dx1, xf, g1, da1, db1, wg1, wu1, "ffn1_bwd_h_rest", (first, n_tiles), prev=begun,
                                    plan=_plan_chip_exchange(to_chips(dwgu1, sibgu, gate_up)))

    part = dict(ffn1_norm=dg1, mix_norm=dgm, ffn2_norm=dg2, b_forget=dbf, pool_scale=dps, out_norm_pool=donp,
                out_norm_attn=dona, qn=dqn, kn=dkn, pool_w=dpw.reshape(n_batch, -1, pool_w.shape[-1]), loss=lpart)
    mine = (own_rows(dwgu1, sibgu, chipsgu, gate_up) + own_rows(dwd1, sibd, chipsd, down)
            + own_rows(dwm, sibm, chipsm, MIX_NAMES))
    last = _run_plan(_merge_plans(_plan_sibling_share(mine), _plan_all_to_all(_small_pack(part, d, width))), "last_exchange")
    vstack, pstack = last[len(mine):]
    g_vec, g_pw = _small_sum(vstack, pstack, jnp.reshape(4 * mesh_x + 2 * mesh_y + mesh_c, (1,)).astype(jnp.int32))
    loss = g_vec[5, 0]
    reduced = dict(zip(FFN1_NAMES + MIX_NAMES + FFN2_NAMES, list(last[:len(mine)]) + list(red2)))
    reduced["w_in"] = lax.dynamic_slice(reduced["w_in"], ((in_rows * ids[0]) % 8, 0), (in_rows, d))

    grads, delta, new_m, new_v = {}, {}, {}, {}
    for names in (FFN2_NAMES, FFN1_NAMES, ("w_in",), ("w_out",)):
        stepped = _adamw([work(w[n], n) for n in names], [reduced[n] for n in names], [work(m[n], n) for n in names],
                         [work(v[n], n) for n in names], names[0])
        for n, step in zip(names, stepped):
            grads[n], delta[n], new_m[n], new_v[n] = (work(a, n) for a in (reduced[n], *step))
    flat_pw = lambda a: a.reshape(-1, a.shape[-1])
    (d_pw, m_pw, v_pw), = _adamw([flat_pw(pool_w)], [g_pw], [flat_pw(m_pool_w)], [flat_pw(v_pool_w)], "pool_w")
    (d_vec, m_vec, v_vec), = _adamw([_pack_vec(w, d, width)], [g_vec], [_pack_vec(m, d, width)], [_pack_vec(v, d, width)],
                                    "vectors")
    grads.update(_unpack_vec(g_vec, width), pool_w=g_pw.reshape(pool_w.shape))
    delta.update(_unpack_vec(d_vec, width), pool_w=d_pw.reshape(pool_w.shape))
    new_m.update(_unpack_vec(m_vec, width), pool_w=m_pw.reshape(pool_w.shape))
    new_v.update(_unpack_vec(v_vec, width), pool_w=v_pw.reshape(pool_w.shape))
    return (loss, gx.reshape(x.shape), *[grads[n] for n in WEIGHT_NAMES], *[delta[n] for n in WEIGHT_NAMES],
            *[new_m[n] for n in WEIGHT_NAMES], *[new_v[n] for n in WEIGHT_NAMES])
```

```python
import functools

import jax
import jax.numpy as jnp
from jax import lax
from jax.experimental import pallas as pl
from jax.experimental.pallas import tpu as pltpu

F32 = jnp.float32
BF16 = jnp.bfloat16
EPS = 1e-6
NEG = -1e30
ADAM_LR = 0.001
ADAM_B1 = 0.9
ADAM_B2 = 0.999
ADAM_EPS = 1e-08
ADAM_WD = 0.01
ADAM_STEP = 10
POOL_WINDOWS = (2, 4, 8, 16)
HEAD_DIM = 64
N_HEADS = 8
LANES = 128
N_CHIPS = 4
ATT_BLOCK = 512
ATT_SUB = 128
FFN_TILE = 1024
VMEM_LIMIT = 62 * 1024 * 1024
MESH_AXES = ("x", "y", "c")
ANY = pl.BlockSpec(memory_space=pl.ANY)
VM = pl.BlockSpec(memory_space=pltpu.VMEM)


def _params(**kw):
    return pltpu.CompilerParams(vmem_limit_bytes=VMEM_LIMIT, **kw)


def _dot(a, b):
    return jnp.dot(a, b, preferred_element_type=F32)


def _dot_nt(a, b):
    return lax.dot_general(a, b, (((1,), (1,)), ((), ())), preferred_element_type=F32)


def _dot_tn(a, b):
    return lax.dot_general(a, b, (((0,), (0,)), ((), ())), preferred_element_type=F32)


def _sigmoid(z):
    return 1.0 / (1.0 + jnp.exp(-z))


def _rms(xf):
    return lax.rsqrt(jnp.mean(xf * xf, axis=-1, keepdims=True) + EPS)


def _rms_bwd(xf, r, gain, dh):
    xh = xf * r
    dyg = dh * gain
    return r * (dyg - xh * jnp.mean(dyg * xh, axis=-1, keepdims=True)), dh * xh


def _total(v):
    return jnp.sum(jnp.sum(v, axis=1, keepdims=True), axis=0, keepdims=True)


def _ffn_fwd(x, gain, wg, wu, wd, target=None, plan=None):
    t, d = x.shape
    nch, fc, _ = wg.shape
    tm = min(FFN_TILE, t)
    nt = t // tm
    with_loss = target is not None

    def body(*refs):
        if with_loss:
            x_ref, g_ref, wg_ref, wu_ref, wd_ref, t_ref, o_ref, h_ref, a_ref, b_ref, s_ref, l_ref, acc_ref = refs
        else:
            x_ref, g_ref, wg_ref, wu_ref, wd_ref, o_ref, h_ref, a_ref, b_ref, s_ref, acc_ref = refs
        k = pl.program_id(1)

        @pl.when(k == 0)
        def _():
            xf = x_ref[...]
            h_ref[...] = ((xf * _rms(xf)) * g_ref[...]).astype(BF16)
            acc_ref[...] = jnp.zeros_like(acc_ref)

        for rows in _row_halves(tm):
            h = h_ref[rows, :]
            a = _dot_nt(h, wg_ref[...])
            b = _dot_nt(h, wu_ref[...])
            sb = ((a * (0.5 * jnp.tanh(0.5 * a) + 0.5)) * b).astype(BF16)
            a_ref[rows, :] = a.astype(BF16)
            b_ref[rows, :] = b.astype(BF16)
            s_ref[rows, :] = sb
            acc_ref[rows, :] += _dot(sb, wd_ref[...])

        @pl.when(k == nch - 1)
        def _():
            y = x_ref[...] + 0.5 * acc_ref[...]
            if with_loss:
                e = y - t_ref[...]
                o_ref[...] = e * (1.0 / d)
                l_ref[...] = jnp.broadcast_to(_total(e * e) * (0.5 / d), l_ref.shape)
            else:
                o_ref[...] = y

    row = pl.BlockSpec((tm, d), lambda i, k: (i, 0))
    chunk = pl.BlockSpec((None, fc, d), lambda i, k: (k, 0, 0))
    act = pl.BlockSpec((None, tm, fc), lambda i, k: (k, i, 0))
    in_specs = [row, pl.BlockSpec((1, d), lambda i, k: (0, 0)), chunk, chunk, chunk]
    out_shape = [jax.ShapeDtypeStruct((t, d), F32), jax.ShapeDtypeStruct((t, d), BF16)]
    out_shape += [jax.ShapeDtypeStruct((nch, t, fc), BF16)] * 3
    out_specs = [row, row, act, act, act]
    args = [x, gain, wg, wu, wd]
    if with_loss:
        in_specs.append(row)
        args.append(target)
        out_shape.append(jax.ShapeDtypeStruct((nt, 8, LANES), F32))
        out_specs.append(pl.BlockSpec((None, 8, LANES), lambda i, k: (i, 0, 0)))
    return _pallas(body, name="ffn_fwd_loss" if with_loss else "ffn_fwd", args=args, in_specs=in_specs,
                   out_shape=out_shape, out_specs=out_specs, grid=(nt, nch),
                   scratch_shapes=[pltpu.VMEM((tm, d), F32)], plan=plan)


def _row_halves(n):
    return [slice(0, n // 2), slice(n // 2, n)]


def _swiglu_grads(dy_ref, a_ref, b_ref, wd_ref, rows):
    ds = _dot_nt(dy_ref[rows, :].astype(BF16), wd_ref[...])
    av = a_ref[rows, :].astype(F32)
    bv = b_ref[rows, :].astype(F32)
    th = jnp.tanh(0.5 * av)
    half_sig = 0.25 * th + 0.25
    dab = ((ds * bv) * (half_sig * (1.0 + av * (0.5 - 0.5 * th)))).astype(BF16)
    return dab, (ds * (av * half_sig)).astype(BF16)


def _ffn_bwd_a(dy, a, b, wd, name, plan=None):
    t, d = dy.shape
    nch, fc, _ = wd.shape
    tm = min(FFN_TILE, t)

    def body(dy_ref, a_ref, b_ref, wd_ref, da_ref, db_ref):
        for rows in _row_halves(tm):
            da_ref[rows, :], db_ref[rows, :] = _swiglu_grads(dy_ref, a_ref, b_ref, wd_ref, rows)

    act = pl.BlockSpec((None, tm, fc), lambda i, k: (k, i, 0))
    return _pallas(
        body, name=name, args=[dy, a, b, wd], out_shape=[jax.ShapeDtypeStruct((nch, t, fc), BF16)] * 2, grid=(t // tm, nch),
        in_specs=[pl.BlockSpec((tm, d), lambda i, k: (i, 0)), act, act, pl.BlockSpec((None, fc, d), lambda i, k: (k, 0, 0))],
        out_specs=[act, act], plan=plan)


def _ffn_bwd_h(dy, x, gain, da, db, wg, wu, name, tiles, prev=None, plan=None):
    t, d = x.shape
    nch, fc, _ = wg.shape
    tm = min(FFN_TILE, t)
    nt = t // tm
    t0, t1 = tiles

    def body(*refs):
        dy_ref, x_ref, g_ref, da_ref, db_ref, wg_ref, wu_ref = refs[:7]
        dx_ref, dg_ref, acc_ref = refs[-3:]
        k = pl.program_id(1)

        @pl.when(k == 0)
        def _():
            acc_ref[...] = jnp.zeros_like(acc_ref)

        acc_ref[...] += _dot(da_ref[...], wg_ref[...]) + _dot(db_ref[...], wu_ref[...])

        @pl.when(k == nch - 1)
        def _():
            xf = x_ref[...]
            dxn, dgr = _rms_bwd(xf, _rms(xf), g_ref[...], acc_ref[...])
            dx_ref[...] = dy_ref[...] + dxn
            dg_ref[...] = jnp.sum(dgr, axis=0, keepdims=True)

    row = pl.BlockSpec((tm, d), lambda i, k: (i + t0, 0))
    chunk = pl.BlockSpec((None, fc, d), lambda i, k: (k, 0, 0))
    act = pl.BlockSpec((None, tm, fc), lambda i, k: (k, i + t0, 0))
    args = [dy, x, gain, da, db, wg, wu]
    in_specs = [row, row, pl.BlockSpec((1, d), lambda i, k: (0, 0)), act, act, chunk, chunk]
    aliases = {}
    if prev is not None:
        aliases = {len(args): 0, len(args) + 1: 1}
        args += list(prev)
        in_specs += [ANY, ANY]
    return _pallas(
        body, name=name, args=args, out_shape=[jax.ShapeDtypeStruct((t, d), F32), jax.ShapeDtypeStruct((nt, 1, d), F32)],
        grid=(t1 - t0, nch), in_specs=in_specs,
        out_specs=[row, pl.BlockSpec((None, 1, d), lambda i, k: (i + t0, 0, 0))],
        scratch_shapes=[pltpu.VMEM((tm, d), F32)], plan=plan, aliases=aliases)


def _ffn_bwd_x(dy, x, gain, a, b, wg, wu, wd, name, plan=None):
    t, d = x.shape
    nch, fc, _ = wg.shape
    tm = min(FFN_TILE, t)
    nt = t // tm

    def body(dy_ref, x_ref, g_ref, a_ref, b_ref, wg_ref, wu_ref, wd_ref, dx_ref, da_ref, db_ref, dg_ref, acc_ref):
        k = pl.program_id(1)

        @pl.when(k == 0)
        def _():
            acc_ref[...] = jnp.zeros_like(acc_ref)

        for rows in _row_halves(tm):
            dab, dbb = _swiglu_grads(dy_ref, a_ref, b_ref, wd_ref, rows)
            da_ref[rows, :] = dab
            db_ref[rows, :] = dbb
            acc_ref[rows, :] += _dot(dab, wg_ref[...]) + _dot(dbb, wu_ref[...])

        @pl.when(k == nch - 1)
        def _():
            xf = x_ref[...]
            dxn, dgr = _rms_bwd(xf, _rms(xf), g_ref[...], acc_ref[...])
            dx_ref[...] = dy_ref[...] + dxn
            dg_ref[...] = jnp.sum(dgr, axis=0, keepdims=True)

    row = pl.BlockSpec((tm, d), lambda i, k: (i, 0))
    chunk = pl.BlockSpec((None, fc, d), lambda i, k: (k, 0, 0))
    act = pl.BlockSpec((None, tm, fc), lambda i, k: (k, i, 0))
    return _pallas(
        body, name=name, args=[dy, x, gain, a, b, wg, wu, wd],
        out_shape=[jax.ShapeDtypeStruct((t, d), F32), jax.ShapeDtypeStruct((nch, t, fc), BF16),
                   jax.ShapeDtypeStruct((nch, t, fc), BF16), jax.ShapeDtypeStruct((nt, 1, d), F32)],
        grid=(nt, nch),
        in_specs=[row, row, pl.BlockSpec((1, d), lambda i, k: (0, 0)), act, act, chunk, chunk, chunk],
        out_specs=[row, act, act, pl.BlockSpec((None, 1, d), lambda i, k: (i, 0, 0))],
        scratch_shapes=[pltpu.VMEM((tm, d), F32)], plan=plan)


def _ffn_bwd_w(pairs, name, plan=None):
    n = len(pairs)
    nch, t, fc = pairs[0][0].shape
    d = pairs[0][1].shape[1]
    tm = min(1024, t)

    def body(*refs):
        @pl.when(pl.program_id(1) == 0)
        def _():
            for o_ref in refs[2 * n:]:
                o_ref[...] = jnp.zeros_like(o_ref)

        for j, (_, _, scale) in enumerate(pairs):
            other = refs[n + j][...]
            if other.dtype != BF16:
                other = (scale * other).astype(BF16)
            refs[2 * n + j][...] += _dot_tn(refs[j][...], other)

    row = pl.BlockSpec((tm, d), lambda k, i: (i, 0))
    act = pl.BlockSpec((None, tm, fc), lambda k, i: (k, i, 0))
    chunk = pl.BlockSpec((None, fc, d), lambda k, i: (k, 0, 0))
    return _pallas(body, name=name, args=[p[0] for p in pairs] + [p[1] for p in pairs],
                   out_shape=[jax.ShapeDtypeStruct((nch, fc, d), F32)] * n, grid=(nch, t // tm),
                   in_specs=[act] * n + [row] * n, out_specs=[chunk] * n, plan=plan)


def _head_masks():
    lane = lax.broadcasted_iota(jnp.int32, (1, LANES), 1)
    return lane < HEAD_DIM


def _head_rms(x, lo):
    x2 = x * x
    s0 = jnp.sum(jnp.where(lo, x2, 0.0), axis=1, keepdims=True)
    s1 = jnp.sum(jnp.where(lo, 0.0, x2), axis=1, keepdims=True)
    return jnp.where(lo, lax.rsqrt(s0 * (1.0 / HEAD_DIM) + EPS), lax.rsqrt(s1 * (1.0 / HEAD_DIM) + EPS))


def _head_mean(v, lo):
    s0 = jnp.sum(jnp.where(lo, v, 0.0), axis=1, keepdims=True)
    s1 = jnp.sum(jnp.where(lo, 0.0, v), axis=1, keepdims=True)
    return jnp.where(lo, s0, s1) * (1.0 / HEAD_DIM)


def _mix_proj(x1, gain, wt, qn, kn, pool_width, attn_width):
    t, d = x1.shape
    tm = min(512, t)
    nt = t // tm
    scale = HEAD_DIM ** -0.5
    c_q, c_k, c_v = pool_width, pool_width + attn_width, pool_width + 2 * attn_width
    c_f = c_v + attn_width

    def body(x_ref, g_ref, wt_ref, qn_ref, kn_ref, hm_ref, pv_ref, q_ref, k_ref, qh_ref, kh_ref, vb_ref, f_ref):
        xf = x_ref[...]
        hm = ((xf * _rms(xf)) * g_ref[...]).astype(BF16)
        hm_ref[...] = hm
        f_ref[...] = _dot_nt(hm, wt_ref[c_f:c_f + LANES, :])
        pv_ref[...] = _dot_nt(hm, wt_ref[0:pool_width, :])
        vb_ref[...] = _dot_nt(hm, wt_ref[c_v:c_v + attn_width, :]).astype(BF16)
        lo = _head_masks()
        for c0, raw_ref, hat_ref, n_ref, mul in ((c_q, q_ref, qh_ref, qn_ref, scale), (c_k, k_ref, kh_ref, kn_ref, 1.0)):
            raw = _dot_nt(hm, wt_ref[c0:c0 + attn_width, :])
            raw_ref[...] = raw
            for blk in range(attn_width // LANES):
                sl = slice(blk * LANES, (blk + 1) * LANES)
                xb = raw[:, sl]
                hat_ref[:, sl] = (((xb * _head_rms(xb, lo)) * n_ref[:, sl]) * mul).astype(BF16)

    row = pl.BlockSpec((tm, d), lambda i: (i, 0))
    half = pl.BlockSpec((tm, attn_width), lambda i: (i, 0))
    const = lambda shape: pl.BlockSpec(shape, lambda i: (0, 0))
    return pl.pallas_call(
        body,
        out_shape=[jax.ShapeDtypeStruct((t, d), BF16), jax.ShapeDtypeStruct((t, pool_width), F32),
                   jax.ShapeDtypeStruct((t, attn_width), F32), jax.ShapeDtypeStruct((t, attn_width), F32),
                   jax.ShapeDtypeStruct((t, attn_width), BF16), jax.ShapeDtypeStruct((t, attn_width), BF16),
                   jax.ShapeDtypeStruct((t, attn_width), BF16), jax.ShapeDtypeStruct((t, LANES), F32)],
        grid=(nt,),
        in_specs=[row, const((1, d)), const(wt.shape), const((1, attn_width)), const((1, attn_width))],
        out_specs=[row, pl.BlockSpec((tm, pool_width), lambda i: (i, 0)), half, half, half, half, half,
                   pl.BlockSpec((tm, LANES), lambda i: (i, 0))],
        compiler_params=_params(), name="mix_proj",
    )(x1, gain, wt, qn, kn)


def _shift_down(v, dist, row):
    return jnp.where(row >= dist, pltpu.roll(v, dist, 0), 0.0)


def _shift_up(v, dist, row, n):
    return jnp.where(row + dist < n, pltpu.roll(v, n - dist, 0), 0.0)


def _aug_lane(e):
    return HEAD_DIM if e == 0 else 0


def _forget_prefix(f, bias, qh, kh, n_batch, seq):
    def body(f_ref, b_ref, q_ref, k_ref, qa_ref, ka_ref):
        z = f_ref[...] + b_ref[...]
        acc = jnp.minimum(z, 0.0) - jnp.log(1.0 + jnp.exp(-jnp.abs(z)))
        row = lax.broadcasted_iota(jnp.int32, (seq, 1), 0)
        dist = 1
        while dist < seq:
            acc = acc + _shift_down(acc, dist, row)
            dist *= 2
        lane = lax.broadcasted_iota(jnp.int32, (1, LANES), 1)
        for h in range(N_HEADS):
            pair, e = divmod(h, 2)
            a0 = _aug_lane(e)
            own = (lane < HEAD_DIM) if e == 0 else (lane >= HEAD_DIM)
            fh = _pick_lane(acc, h)
            hi = fh.astype(BF16).astype(F32)
            rest = fh - hi
            mid = rest.astype(BF16).astype(F32)
            low = rest - mid
            q_ones = (lane >= a0 + 3) & (lane < a0 + 6)
            k_ones = (lane >= a0) & (lane < a0 + 3)
            q_aug = jnp.where(lane == a0, hi, jnp.where(lane == a0 + 1, mid, jnp.where(lane == a0 + 2, low,
                              jnp.where(q_ones, 1.0, 0.0))))
            k_aug = jnp.where(k_ones, 1.0, jnp.where(lane == a0 + 3, -hi, jnp.where(lane == a0 + 4, -mid,
                              jnp.where(lane == a0 + 5, -low, 0.0))))
            src = slice(pair * LANES, (pair + 1) * LANES)
            dst = slice(h * LANES, (h + 1) * LANES)
            qa_ref[:, dst] = jnp.where(own, q_ref[:, src].astype(F32), q_aug).astype(BF16)
            ka_ref[:, dst] = jnp.where(own, k_ref[:, src].astype(F32), k_aug).astype(BF16)

    width = qh.shape[1]
    tok = pl.BlockSpec((seq, width), lambda b: (b, 0))
    aug = pl.BlockSpec((seq, N_HEADS * LANES), lambda b: (b, 0))
    return pl.pallas_call(
        body, out_shape=[jax.ShapeDtypeStruct((n_batch * seq, N_HEADS * LANES), BF16)] * 2, grid=(n_batch,),
        in_specs=[pl.BlockSpec((seq, LANES), lambda b: (b, 0)), pl.BlockSpec((1, LANES), lambda b: (0, 0)), tok, tok],
        out_specs=[aug, aug], compiler_params=_params(), name="forget_prefix",
    )(f, bias, qh, kh)


def _pool_groups(pv_ref, pw_ref, ps_ref, seq):
    row = lax.broadcasted_iota(jnp.int32, (seq, 1), 0)
    pos = (row + 1).astype(F32)
    out = []
    for g, win in enumerate(POOL_WINDOWS):
        sl = slice(g * LANES, (g + 1) * LANES)
        xg = pv_ref[:, sl]
        acc = xg
        dist = 1
        while dist < win:
            acc = acc + _shift_down(acc, dist, row)
            dist *= 2
        pooled = (acc / jnp.minimum(pos, float(win)) - xg).astype(BF16)
        mixed = _dot(pooled, pw_ref[g])
        out.append((pooled, mixed, mixed * ps_ref[:, sl]))
    return out


def _pool_fwd(pv, pw, ps, onp, n_batch, seq):
    width = pv.shape[1]

    def body(pv_ref, pw_ref, ps_ref, on_ref, y_ref):
        groups = _pool_groups(pv_ref, pw_ref, ps_ref, seq)
        ssq = sum(jnp.sum(ms * ms, axis=1, keepdims=True) for _, _, ms in groups)
        r = lax.rsqrt(ssq * (1.0 / width) + EPS)
        for g, (_, _, ms) in enumerate(groups):
            sl = slice(g * LANES, (g + 1) * LANES)
            y_ref[:, sl] = ((ms * r) * on_ref[:, sl]).astype(BF16)

    return pl.pallas_call(
        body, out_shape=jax.ShapeDtypeStruct((n_batch * seq, width), BF16), grid=(n_batch,),
        in_specs=[pl.BlockSpec((seq, width), lambda b: (b, 0)), pl.BlockSpec(pw.shape, lambda b: (0, 0, 0)),
                  pl.BlockSpec((1, width), lambda b: (0, 0)), pl.BlockSpec((1, width), lambda b: (0, 0))],
        out_specs=pl.BlockSpec((seq, width), lambda b: (b, 0)),
        compiler_params=_params(), name="pool_fwd",
    )(pv, pw, ps, onp)


def _pool_bwd(pv, dyp, pw, ps, onp, n_batch, seq):
    width = pv.shape[1]

    def body(pv_ref, dy_ref, pw_ref, ps_ref, on_ref, dpv_ref, dpw_ref, dps_ref, don_ref):
        groups = _pool_groups(pv_ref, pw_ref, ps_ref, seq)
        ssq = sum(jnp.sum(ms * ms, axis=1, keepdims=True) for _, _, ms in groups)
        r = lax.rsqrt(ssq * (1.0 / width) + EPS)
        mean = sum(jnp.sum((dy_ref[:, g * LANES:(g + 1) * LANES] * on_ref[:, g * LANES:(g + 1) * LANES]) * (ms * r),
                           axis=1, keepdims=True) for g, (_, _, ms) in enumerate(groups)) * (1.0 / width)
        row = lax.broadcasted_iota(jnp.int32, (seq, 1), 0)
        pos = (row + 1).astype(F32)
        for g, (pooled, mixed, ms) in enumerate(groups):
            sl = slice(g * LANES, (g + 1) * LANES)
            dy = dy_ref[:, sl]
            xh = ms * r
            don_ref[:, sl] = jnp.sum(dy * xh, axis=0, keepdims=True)
            dms = r * (dy * on_ref[:, sl] - xh * mean)
            dps_ref[:, sl] = jnp.sum(dms * mixed, axis=0, keepdims=True)
            dmix = (dms * ps_ref[:, sl]).astype(BF16)
            dpw_ref[g] = _dot_tn(pooled, dmix)
            dpool = _dot_nt(dmix, pw_ref[g])
            win = POOL_WINDOWS[g]
            acc = dpool / jnp.minimum(pos, float(win))
            dist = 1
            while dist < win:
                acc = acc + _shift_up(acc, dist, row, seq)
                dist *= 2
            dpv_ref[:, sl] = (acc - dpool).astype(BF16)

    tok = pl.BlockSpec((seq, width), lambda b: (b, 0))
    vec = pl.BlockSpec((1, width), lambda b: (0, 0))
    pvec = pl.BlockSpec((None, 1, width), lambda b: (b, 0, 0))
    return pl.pallas_call(
        body,
        out_shape=[jax.ShapeDtypeStruct((n_batch * seq, width), BF16),
                   jax.ShapeDtypeStruct((n_batch,) + pw.shape, F32),
                   jax.ShapeDtypeStruct((n_batch, 1, width), F32), jax.ShapeDtypeStruct((n_batch, 1, width), F32)],
        grid=(n_batch,),
        in_specs=[tok, tok, pl.BlockSpec(pw.shape, lambda b: (0, 0, 0)), vec, vec],
        out_specs=[tok, pl.BlockSpec((None,) + pw.shape, lambda b: (b, 0, 0, 0)), pvec, pvec],
        compiler_params=_params(), name="pool_bwd",
    )(pv, dyp, pw, ps, onp)


def _pick_lane(tile, idx):
    lane = lax.broadcasted_iota(jnp.int32, (1, LANES), 1)
    return jnp.sum(jnp.where(lane == idx, tile, 0.0), axis=1, keepdims=True)


def _pick_row(tile, idx):
    sub = lax.broadcasted_iota(jnp.int32, (tile.shape[0], 1), 0)
    return jnp.sum(jnp.where(sub == idx, tile, 0.0), axis=0, keepdims=True)


def _put_lane(col, idx):
    lane = lax.broadcasted_iota(jnp.int32, (1, LANES), 1)
    return jnp.where(lane == idx, col, 0.0)


def _head_select(e):
    lo = _head_masks()
    return lo if e == 0 else jnp.logical_not(lo)


def _causal(st, shift):
    row = lax.broadcasted_iota(jnp.int32, st.shape, 0)
    col = lax.broadcasted_iota(jnp.int32, st.shape, 1) + shift
    return jnp.where(col >= row, st, NEG)


def _transpose_blocks(a):
    rows, cols = a.shape
    return jnp.concatenate(
        [jnp.concatenate([a[r:r + LANES, c:c + LANES].T for r in range(0, rows, LANES)], axis=1)
         for c in range(0, cols, LANES)], axis=0)


def _stat_rows(ref, head, nsub):
    return jnp.concatenate([_pick_row(ref[a], head) for a in range(nsub)], axis=1)


def _accumulate(ref, value, first):
    @pl.when(first)
    def _():
        ref[...] = value

    @pl.when(jnp.logical_not(first))
    def _():
        ref[...] += value


def _attn_fwd(qa, ka, vb, n_batch, seq, plan=None):
    tq = min(ATT_BLOCK, seq)
    nq, nsub, tk = seq // tq, tq // ATT_SUB, tq
    pairs = vb.shape[1] // LANES

    def body(q_ref, k_ref, v_ref, o_ref, lse_ref, acc_ref):
        i, p = pl.program_id(1), pl.program_id(2)
        row_lo = lax.broadcasted_iota(jnp.int32, (LANES, 1), 0) < HEAD_DIM
        qs = [q_ref[:, e * LANES:(e + 1) * LANES] for e in range(2)]
        acc_ref[...] = jnp.zeros_like(acc_ref)

        def tile(off, stats, diagonal):
            vj = v_ref[pl.ds(off, tk), :]
            new, alphas, pvs = [], [], []
            for e in range(2):
                st = _dot_nt(k_ref[pl.ds(off, tk), e * LANES:(e + 1) * LANES], qs[e])
                if diagonal:
                    st = _causal(st, 0)
                m, l = stats[e]
                m_new = jnp.maximum(m, jnp.max(st, axis=0, keepdims=True))
                alpha = jnp.exp(m - m_new)
                pt = jnp.exp(st - m_new)
                new.append((m_new, alpha * l + jnp.sum(pt, axis=0, keepdims=True)))
                alphas.append(alpha)
                pvs.append(_dot_tn(jnp.where(_head_select(e), vj, jnp.zeros_like(vj)), pt.astype(BF16)))
            acc_ref[...] = acc_ref[...] * jnp.where(row_lo, alphas[0], alphas[1]) + (pvs[0] + pvs[1])
            return tuple(new)

        init = ((jnp.full((1, tq), NEG, F32), jnp.zeros((1, tq), F32)),) * 2
        stats = lax.fori_loop(0, i, lambda j, st: tile(pl.multiple_of(j * tk, tk), st, False), init)
        (m0, l0), (m1, l1) = tile(pl.multiple_of(i * tk, tk), stats, True)
        out_t = acc_ref[...] / jnp.where(row_lo, l0, l1)
        sub = lax.broadcasted_iota(jnp.int32, (8, 1), 0)
        lse0, lse1 = m0 + jnp.log(l0), m1 + jnp.log(l1)
        for a in range(nsub):
            sl = slice(a * ATT_SUB, (a + 1) * ATT_SUB)
            o_ref[sl, :] = out_t[:, sl].T
            rows = jnp.where(sub == 2 * p, lse0[:, sl], 0.0) + jnp.where(sub == 2 * p + 1, lse1[:, sl], 0.0)
            _accumulate(lse_ref.at[a], rows, p == 0)

    return _pallas(
        body, name="attn_fwd", args=[qa, ka, vb],
        out_shape=[jax.ShapeDtypeStruct((n_batch * seq, pairs * LANES), F32),
                   jax.ShapeDtypeStruct((n_batch * seq // ATT_SUB, 8, ATT_SUB), F32)],
        grid=(n_batch, nq, pairs),
        in_specs=[pl.BlockSpec((tq, 2 * LANES), lambda b, i, p: (b * nq + i, p)),
                  pl.BlockSpec((seq, 2 * LANES), lambda b, i, p: (b, p)),
                  pl.BlockSpec((seq, LANES), lambda b, i, p: (b, p))],
        out_specs=[pl.BlockSpec((tq, LANES), lambda b, i, p: (b * nq + i, p)),
                   pl.BlockSpec((nsub, 8, ATT_SUB), lambda b, i, p: (b * nq + i, 0, 0))],
        scratch_shapes=[pltpu.VMEM((LANES, tq), F32)], plan=plan)


def _attn_bwd_q(qa, ka, vb, do, lse, delta, n_batch, seq, plan=None):
    tq = min(ATT_BLOCK, seq)
    nq, nsub, tk = seq // tq, tq // ATT_SUB, tq
    pairs = vb.shape[1] // LANES

    def body(q_ref, k_ref, v_ref, do_ref, lse_ref, dl_ref, dq_ref, dfq_ref, acc0_ref, acc1_ref):
        i, p = pl.program_id(1), pl.program_id(2)
        accs = (acc0_ref, acc1_ref)
        qs = [q_ref[:, e * LANES:(e + 1) * LANES] for e in range(2)]
        dov = do_ref[...]
        ls = [_stat_rows(lse_ref, 2 * p + e, nsub) for e in range(2)]
        dl = [_stat_rows(dl_ref, 2 * p + e, nsub) for e in range(2)]
        for acc in accs:
            acc[...] = jnp.zeros_like(acc)

        def tile(off, diagonal):
            vj = v_ref[pl.ds(off, tk), :]
            for e in range(2):
                kj = k_ref[pl.ds(off, tk), e * LANES:(e + 1) * LANES]
                st = _dot_nt(kj, qs[e])
                if diagonal:
                    st = _causal(st, 0)
                pt = jnp.exp(st - ls[e])
                dpt = _dot_nt(jnp.where(_head_select(e), vj, jnp.zeros_like(vj)), dov)
                accs[e][...] += _dot(_transpose_blocks(kj), (pt * (dpt - dl[e])).astype(BF16))

        def step(j, carry):
            tile(pl.multiple_of(j * tk, tk), False)
            return carry

        lax.fori_loop(0, i, step, 0)
        tile(pl.multiple_of(i * tk, tk), True)
        dq0, dq1 = _transpose_blocks(acc0_ref[...]), _transpose_blocks(acc1_ref[...])
        dq_ref[...] = jnp.where(_head_masks(), dq0, dq1)
        dfq = _put_lane(_pick_lane(dq0, _aug_lane(0)), 2 * p) + _put_lane(_pick_lane(dq1, _aug_lane(1)), 2 * p + 1)
        _accumulate(dfq_ref, dfq, p == 0)

    stat = pl.BlockSpec((nsub, 8, ATT_SUB), lambda b, i, p: (b * nq + i, 0, 0))
    blk = pl.BlockSpec((tq, LANES), lambda b, i, p: (b * nq + i, p))
    return _pallas(
        body, name="attn_bwd_q", args=[qa, ka, vb, do, lse, delta],
        out_shape=[jax.ShapeDtypeStruct((n_batch * seq, pairs * LANES), F32), jax.ShapeDtypeStruct((n_batch * seq, LANES), F32)],
        grid=(n_batch, nq, pairs),
        in_specs=[pl.BlockSpec((tq, 2 * LANES), lambda b, i, p: (b * nq + i, p)),
                  pl.BlockSpec((seq, 2 * LANES), lambda b, i, p: (b, p)),
                  pl.BlockSpec((seq, LANES), lambda b, i, p: (b, p)), blk, stat, stat],
        out_specs=[blk, pl.BlockSpec((tq, LANES), lambda b, i, p: (b * nq + i, 0))],
        scratch_shapes=[pltpu.VMEM((LANES, tq), F32), pltpu.VMEM((LANES, tq), F32)], plan=plan)


def _attn_bwd_kv(qa, ka, vb, do, lse, delta, n_batch, seq, plan=None):
    tkb = min(ATT_BLOCK, seq)
    nk, nsub, tq = seq // tkb, tkb // ATT_SUB, tkb
    n_tiles = seq // ATT_SUB
    pairs = vb.shape[1] // LANES

    def body(q_ref, k_ref, v_ref, do_ref, lse_ref, dl_ref, dk_ref, dv_ref, dfk_ref, dk0_ref, dk1_ref, dva_ref):
        j, p = pl.program_id(1), pl.program_id(2)
        dks = (dk0_ref, dk1_ref)
        ks = [k_ref[:, e * LANES:(e + 1) * LANES] for e in range(2)]
        vj = v_ref[...]
        vs = [jnp.where(_head_select(e), vj, jnp.zeros_like(vj)) for e in range(2)]
        for acc in (dk0_ref, dk1_ref, dva_ref):
            acc[...] = jnp.zeros_like(acc)

        def tile(t, diagonal):
            off = pl.multiple_of(t * tq, tq)
            dov = do_ref[pl.ds(off, tq), :]
            for e in range(2):
                qe = q_ref[pl.ds(off, tq), e * LANES:(e + 1) * LANES]
                st = _dot_nt(ks[e], qe)
                if diagonal:
                    st = _causal(st, 0)
                rows = lambda ref: jnp.concatenate([_pick_row(ref[t * nsub + a], 2 * p + e) for a in range(nsub)], axis=1)
                pt = jnp.exp(st - rows(lse_ref))
                dva_ref[...] += _dot(pt.astype(BF16), jnp.where(_head_select(e), dov, jnp.zeros_like(dov)))
                dst = pt * (_dot_nt(vs[e], dov) - rows(dl_ref))
                dks[e][...] += _dot(dst.astype(BF16), qe)

        def step(t, carry):
            tile(t, False)
            return carry

        lax.fori_loop(j + 1, nk, step, 0)
        tile(j, True)
        dk0, dk1 = dk0_ref[...], dk1_ref[...]
        dk_ref[...] = jnp.where(_head_masks(), dk0, dk1)
        dv_ref[...] = dva_ref[...].astype(BF16)
        dfk = (_put_lane(_pick_lane(dk0, _aug_lane(0) + 3), 2 * p)
               + _put_lane(_pick_lane(dk1, _aug_lane(1) + 3), 2 * p + 1))
        _accumulate(dfk_ref, -dfk, p == 0)

    stat = pl.BlockSpec((n_tiles, 8, ATT_SUB), lambda b, j, p: (b, 0, 0))
    blk = pl.BlockSpec((tkb, LANES), lambda b, j, p: (b * nk + j, p))
    acc = pltpu.VMEM((tkb, LANES), F32)
    return _pallas(
        body, name="attn_bwd_kv", args=[qa, ka, vb, do, lse, delta],
        out_shape=[jax.ShapeDtypeStruct((n_batch * seq, pairs * LANES), F32),
                   jax.ShapeDtypeStruct((n_batch * seq, pairs * LANES), BF16),
                   jax.ShapeDtypeStruct((n_batch * seq, LANES), F32)],
        grid=(n_batch, nk, pairs),
        in_specs=[pl.BlockSpec((seq, 2 * LANES), lambda b, j, p: (b, p)),
                  pl.BlockSpec((tkb, 2 * LANES), lambda b, j, p: (b * nk + j, p)), blk,
                  pl.BlockSpec((seq, LANES), lambda b, j, p: (b, p)), stat, stat],
        out_specs=[blk, blk, pl.BlockSpec((tkb, LANES), lambda b, j, p: (b * nk + j, 0))],
        scratch_shapes=[acc, acc, acc], plan=plan)


def _forget_bwd(dfq, dfk, f, bias, n_batch, seq):
    def body(dfq_ref, dfk_ref, f_ref, b_ref, df_ref, db_ref):
        acc = dfq_ref[...] + dfk_ref[...]
        row = lax.broadcasted_iota(jnp.int32, (seq, 1), 0)
        dist = 1
        while dist < seq:
            acc = acc + _shift_up(acc, dist, row, seq)
            dist *= 2
        df = acc * _sigmoid(-(f_ref[...] + b_ref[...]))
        df_ref[...] = df
        db_ref[...] = jnp.sum(df, axis=0, keepdims=True)

    col = pl.BlockSpec((seq, LANES), lambda b: (b, 0))
    return pl.pallas_call(
        body,
        out_shape=[jax.ShapeDtypeStruct((n_batch * seq, LANES), F32), jax.ShapeDtypeStruct((n_batch, 1, LANES), F32)],
        grid=(n_batch,), in_specs=[col, col, col, pl.BlockSpec((1, LANES), lambda b: (0, 0))],
        out_specs=[col, pl.BlockSpec((None, 1, LANES), lambda b: (b, 0, 0))],
        compiler_params=_params(), name="forget_bwd",
    )(dfq, dfk, f, bias)


def _mix_out(x1, yp, o, ona, woa, wob):
    t, d = x1.shape
    width = o.shape[1]
    tm = min(512, t)

    def body(x_ref, yp_ref, o_ref, on_ref, wa_ref, wb_ref, x2_ref, ya_ref):
        of = o_ref[...]
        ya = ((of * _rms(of)) * on_ref[...]).astype(BF16)
        ya_ref[...] = ya
        x2_ref[...] = x_ref[...] + (_dot(yp_ref[...], wa_ref[...]) + _dot(ya, wb_ref[...]))

    row = pl.BlockSpec((tm, d), lambda i: (i, 0))
    half = pl.BlockSpec((tm, width), lambda i: (i, 0))
    wspec = pl.BlockSpec((width, d), lambda i: (0, 0))
    return pl.pallas_call(
        body, out_shape=[jax.ShapeDtypeStruct((t, d), F32), jax.ShapeDtypeStruct((t, width), BF16)],
        grid=(t // tm,), in_specs=[row, half, half, pl.BlockSpec((1, width), lambda i: (0, 0)), wspec, wspec],
        out_specs=[row, half], compiler_params=_params(), name="mix_out",
    )(x1, yp, o, ona, woa, wob)


def _mix_out_bwd(dx2, o, yp, ya, ona, woa, wob, plan=None):
    t, d = dx2.shape
    width = o.shape[1]
    tm = min(512, t)
    nt = t // tm

    def body(dx_ref, o_ref, yp_ref, ya_ref, on_ref, wa_ref, wb_ref, dyp_ref, do_ref, dl_ref, dwa_ref, dwb_ref, don_ref):
        @pl.when(pl.program_id(0) == 0)
        def _():
            dwa_ref[...] = jnp.zeros_like(dwa_ref)
            dwb_ref[...] = jnp.zeros_like(dwb_ref)

        dxb = dx_ref[...].astype(BF16)
        dwa_ref[...] += _dot_tn(yp_ref[...], dxb)
        dwb_ref[...] += _dot_tn(ya_ref[...], dxb)
        dyp_ref[...] = _dot_nt(dxb, wa_ref[...])
        of = o_ref[...]
        dov, dgr = _rms_bwd(of, _rms(of), on_ref[...], _dot_nt(dxb, wb_ref[...]))
        don_ref[...] = jnp.sum(dgr, axis=0, keepdims=True)
        do_ref[...] = dov.astype(BF16)
        lo = _head_masks()
        prod = dov * of
        delta = jnp.zeros((tm, LANES), F32)
        for blk in range(width // LANES):
            pb = prod[:, blk * LANES:(blk + 1) * LANES]
            delta = delta + _put_lane(jnp.sum(jnp.where(lo, pb, 0.0), axis=1, keepdims=True), 2 * blk)
            delta = delta + _put_lane(jnp.sum(jnp.where(lo, 0.0, pb), axis=1, keepdims=True), 2 * blk + 1)
        for c in range(tm // ATT_SUB):
            dl_ref[c] = delta[c * ATT_SUB:(c + 1) * ATT_SUB, :].T[0:8, :]

    row = pl.BlockSpec((tm, d), lambda i: (i, 0))
    half = pl.BlockSpec((tm, width), lambda i: (i, 0))
    wspec = pl.BlockSpec((width, d), lambda i: (0, 0))
    return _pallas(
        body, name="mix_out_bwd", args=[dx2, o, yp, ya, ona, woa, wob],
        out_shape=[jax.ShapeDtypeStruct((t, width), F32), jax.ShapeDtypeStruct((t, width), BF16),
                   jax.ShapeDtypeStruct((t // ATT_SUB, 8, ATT_SUB), F32), jax.ShapeDtypeStruct((width, d), F32),
                   jax.ShapeDtypeStruct((width, d), F32), jax.ShapeDtypeStruct((nt, 1, width), F32)],
        grid=(nt,),
        in_specs=[row, half, half, half, pl.BlockSpec((1, width), lambda i: (0, 0)), wspec, wspec],
        out_specs=[half, half, pl.BlockSpec((tm // ATT_SUB, 8, ATT_SUB), lambda i: (i, 0, 0)), wspec, wspec,
                   pl.BlockSpec((None, 1, width), lambda i: (i, 0, 0))], plan=plan)


def _mix_in_bwd(dx2, x1, gain, hm, dpv, dqh, q, dkh, k, dv, df, qn, kn, wt):
    t, d = x1.shape
    width = q.shape[1]
    pool_width = dpv.shape[1]
    tm = min(512, t)
    nt = t // tm
    scale = HEAD_DIM ** -0.5
    c_q, c_k, c_v = pool_width, pool_width + width, pool_width + 2 * width
    c_f = c_v + width

    def body(dx2_ref, x_ref, g_ref, hm_ref, dpv_ref, dqh_ref, q_ref, dkh_ref, k_ref, dv_ref, df_ref, qn_ref, kn_ref,
             wt_ref, dx_ref, dwt_ref, dg_ref, dqn_ref, dkn_ref):
        @pl.when(pl.program_id(0) == 0)
        def _():
            dwt_ref[...] = jnp.zeros_like(dwt_ref)

        lo = _head_masks()
        hm = hm_ref[...]
        pieces = [(0, dpv_ref[...])]
        for c0, raw_ref, dh_ref, n_ref, dn_ref, mul in ((c_q, q_ref, dqh_ref, qn_ref, dqn_ref, scale),
                                                       (c_k, k_ref, dkh_ref, kn_ref, dkn_ref, 1.0)):
            cols = []
            for blk in range(width // LANES):
                sl = slice(blk * LANES, (blk + 1) * LANES)
                xb = raw_ref[:, sl]
                gb = dh_ref[:, sl] * mul
                r = _head_rms(xb, lo)
                xh = xb * r
                dyg = gb * n_ref[:, sl]
                cols.append((r * (dyg - xh * _head_mean(dyg * xh, lo))).astype(BF16))
                dn_ref[:, sl] = jnp.sum(gb * xh, axis=0, keepdims=True)
            pieces.append((c0, jnp.concatenate(cols, axis=1)))
        pieces.append((c_v, dv_ref[...]))
        pieces.append((c_f, df_ref[...].astype(BF16)))
        dhm = jnp.zeros((tm, d), F32)
        for c0, piece in pieces:
            dwt_ref[c0:c0 + piece.shape[1], :] += _dot_tn(piece, hm)
            dhm = dhm + _dot(piece, wt_ref[c0:c0 + piece.shape[1], :])
        xf = x_ref[...]
        dxn, dgr = _rms_bwd(xf, _rms(xf), g_ref[...], dhm)
        dx_ref[...] = dx2_ref[...] + dxn
        dg_ref[...] = jnp.sum(dgr, axis=0, keepdims=True)

    row = pl.BlockSpec((tm, d), lambda i: (i, 0))
    half = pl.BlockSpec((tm, width), lambda i: (i, 0))
    const = lambda shape: pl.BlockSpec(shape, lambda i: (0, 0))
    pvec = lambda n: pl.BlockSpec((None, 1, n), lambda i: (i, 0, 0))
    return pl.pallas_call(
        body,
        out_shape=[jax.ShapeDtypeStruct((t, d), F32), jax.ShapeDtypeStruct(wt.shape, F32),
                   jax.ShapeDtypeStruct((nt, 1, d), F32),
                   jax.ShapeDtypeStruct((nt, 1, width), F32), jax.ShapeDtypeStruct((nt, 1, width), F32)],
        grid=(nt,),
        in_specs=[row, row, const((1, d)), row, pl.BlockSpec((tm, pool_width), lambda i: (i, 0)), half, half, half, half,
                  half, pl.BlockSpec((tm, LANES), lambda i: (i, 0)), const((1, width)), const((1, width)),
                  const(wt.shape)],
        out_specs=[row, const(wt.shape), pvec(d), pvec(width), pvec(width)],
        compiler_params=_params(), name="mix_in_bwd",
    )(dx2, x1, gain, hm, dpv, dqh, q, dkh, k, dv, df, qn, kn, wt)


def _mesh_pos():
    return lax.axis_index("x"), lax.axis_index("y"), lax.axis_index("c")


def _other_chips(x, y):
    return [(1 - x, y), (x, 1 - y), (1 - x, 1 - y)]


def _remote(src, dst, send_sem, recv_sem, device):
    return pltpu.make_async_remote_copy(src_ref=src, dst_ref=dst, send_sem=send_sem, recv_sem=recv_sem,
                                        device_id=device, device_id_type=pl.DeviceIdType.MESH)


def _half_rows(n_rows, which):
    half = n_rows // 2
    return pl.ds(pl.multiple_of(which * half, 8), half)


def _row_block(rows, cols, itemsize=4):
    rb = rows
    while rb * cols * itemsize > (1 << 20) and rb % 32 == 0:
        rb //= 2
    return rb


def _place_cast(ws, chip, tag):
    n = len(ws)
    rows, cols = ws[0].shape
    rb = _row_block(rows, cols)

    def body(k_ref, *refs):
        for w_ref, o_ref in zip(refs[:n], refs[n:]):
            o_ref[...] = w_ref[...].astype(BF16)

    return pl.pallas_call(
        body, out_shape=[jax.ShapeDtypeStruct((N_CHIPS, rows, cols), BF16)] * n,
        grid_spec=pltpu.PrefetchScalarGridSpec(
            num_scalar_prefetch=1, grid=(rows // rb,),
            in_specs=[pl.BlockSpec((rb, cols), lambda i, k: (i, 0))] * n,
            out_specs=[pl.BlockSpec((None, rb, cols), lambda i, k: (k[0], i, 0))] * n),
        compiler_params=_params(), name="place_" + tag,
    )(chip, *ws)


class _Plan:
    def __init__(self, ins, outs, alias, sems, start, finish):
        self.ins, self.outs, self.alias, self.sems, self.start, self.finish = ins, outs, alias, sems, start, finish


def _merge_plans(a, b):
    ni, no, ns = len(a.ins), len(a.outs), len(a.sems)
    alias = dict(a.alias)
    alias.update({ni + i: no + o for i, o in b.alias.items()})

    def both(which):
        def run(ins, outs, sems):
            getattr(a, which)(ins[:ni], outs[:no], sems[:ns])
            getattr(b, which)(ins[ni:], outs[no:], sems[ns:])
        return run

    return _Plan(list(a.ins) + list(b.ins), list(a.outs) + list(b.outs), alias, list(a.sems) + list(b.sems),
                 both("start"), both("finish"))


def _run_plan(plan, name):
    n_in, n_out = len(plan.ins), len(plan.outs)

    def body(*refs):
        parts = refs[:n_in], refs[n_in:n_in + n_out], refs[n_in + n_out:]
        plan.start(*parts)
        plan.finish(*parts)

    return pl.pallas_call(
        body, out_shape=plan.outs, in_specs=[ANY] * n_in, out_specs=[ANY] * n_out, scratch_shapes=plan.sems,
        input_output_aliases=plan.alias, name=name,
    )(*plan.ins)


def _pallas(body, *, name, args, in_specs, out_shape, out_specs, grid, scratch_shapes=(), plan=None, aliases=None):
    n_in, n_out, n_scr = len(args), len(out_shape), len(scratch_shapes)
    aliases = dict(aliases or {})
    if plan is None:
        res = pl.pallas_call(body, out_shape=out_shape, grid=grid, in_specs=in_specs, out_specs=out_specs,
                             scratch_shapes=scratch_shapes, input_output_aliases=aliases,
                             compiler_params=_params(), name=name)(*args)
        return list(res), []
    p_in, p_out = len(plan.ins), len(plan.outs)

    def carrying(*refs):
        ins, p_ins = refs[:n_in], refs[n_in:n_in + p_in]
        o0 = n_in + p_in
        outs, p_outs = refs[o0:o0 + n_out], refs[o0 + n_out:o0 + n_out + p_out]
        s0 = o0 + n_out + p_out
        scr, p_sems = refs[s0:s0 + n_scr], refs[s0 + n_scr:]
        ids = [pl.program_id(a) for a in range(len(grid))]
        first = functools.reduce(jnp.logical_and, [i == 0 for i in ids])
        last = functools.reduce(jnp.logical_and, [i == g - 1 for i, g in zip(ids, grid)])

        @pl.when(first)
        def _():
            plan.start(p_ins, p_outs, p_sems)

        body(*ins, *outs, *scr)

        @pl.when(last)
        def _():
            plan.finish(p_ins, p_outs, p_sems)

    res = pl.pallas_call(
        carrying, out_shape=list(out_shape) + list(plan.outs), grid=grid,
        in_specs=list(in_specs) + [ANY] * p_in, out_specs=list(out_specs) + [ANY] * p_out,
        scratch_shapes=list(scratch_shapes) + list(plan.sems),
        input_output_aliases={**aliases, **{n_in + i: n_out + o for i, o in plan.alias.items()}},
        compiler_params=_params(), name=name,
    )(*args, *plan.ins)
    return list(res[:n_out]), list(res[n_out:])


def _plan_gather(stacks):
    n = len(stacks)

    def ici_copies(outs, sems):
        x, y, c = _mesh_pos()
        cps = []
        for w in range(n):
            own = outs[w].at[2 * x + y, _half_rows(stacks[w].shape[1], c)]
            cps += [_remote(own, own, sems[0].at[w, j], sems[1].at[w, j], (*chip, c)) for j, chip in enumerate(_other_chips(x, y))]
        return cps

    def start(ins, outs, sems):
        for cp in ici_copies(outs, sems):
            cp.start()

    def finish(ins, outs, sems):
        ici_send, ici_recv, d2d_send, d2d_recv = sems
        x, y, c = _mesh_pos()
        sibling = (x, y, 1 - c)
        slots = [2 * cx + cy for cx, cy in _other_chips(x, y)]
        forwards = []
        for w in range(n):
            rows = _half_rows(stacks[w].shape[1], c)
            for j in range(3):
                landed = outs[w].at[slots[j], rows]
                _remote(landed, landed, ici_send.at[w, j], ici_recv.at[w, j], sibling).wait_recv()
                cp = _remote(landed, landed, d2d_send.at[w, j], d2d_recv.at[w, j], sibling)
                cp.start()
                forwards.append(cp)
        for w in range(n):
            rows = _half_rows(stacks[w].shape[1], 1 - c)
            for j in range(3):
                landed = outs[w].at[slots[j], rows]
                _remote(landed, landed, d2d_send.at[w, j], d2d_recv.at[w, j], sibling).wait_recv()
        for cp in ici_copies(outs, sems) + forwards:
            cp.wait_send()

    return _Plan(stacks, [jax.ShapeDtypeStruct(s.shape, s.dtype) for s in stacks], {w: w for w in range(n)},
                 [pltpu.SemaphoreType.DMA((n, 3))] * 4, start, finish)


def _plan_sibling_halves(gs):
    n = len(gs)

    def copies(ins, outs, sems):
        x, y, c = _mesh_pos()
        return [_remote(ins[w].at[:, _half_rows(gs[w].shape[1], 1 - c), :], outs[w], sems[0].at[w], sems[1].at[w],
                        (x, y, 1 - c)) for w in range(n)]

    def start(ins, outs, sems):
        for cp in copies(ins, outs, sems):
            cp.start()

    def finish(ins, outs, sems):
        for cp in copies(ins, outs, sems):
            cp.wait()

    return _Plan(gs, [jax.ShapeDtypeStruct((g.shape[0], g.shape[1] // 2, g.shape[2]), g.dtype) for g in gs], {},
                 [pltpu.SemaphoreType.DMA((n,))] * 2, start, finish)


def _plan_chip_exchange(ps):
    n = len(ps)

    def copies(ins, outs, sems):
        x, y, c = _mesh_pos()
        return [_remote(ins[w].at[2 * cx + cy], outs[w].at[j], sems[0].at[w, j], sems[1].at[w, j], (cx, cy, c))
                for w in range(n) for j, (cx, cy) in enumerate(_other_chips(x, y))]

    def start(ins, outs, sems):
        for cp in copies(ins, outs, sems):
            cp.start()

    def finish(ins, outs, sems):
        for cp in copies(ins, outs, sems):
            cp.wait()

    return _Plan(ps, [jax.ShapeDtypeStruct((3,) + p.shape[1:], p.dtype) for p in ps], {},
                 [pltpu.SemaphoreType.DMA((n, 3))] * 2, start, finish)


def _plan_sibling_share(gs):
    n = len(gs)

    def copies(outs, sems, which):
        x, y, c = _mesh_pos()
        cps = []
        for w in range(n):
            rows = outs[w].at[_half_rows(gs[w].shape[0], c if which == "mine" else 1 - c)]
            cps.append(_remote(rows, rows, sems[0].at[w], sems[1].at[w], (x, y, 1 - c)))
        return cps

    def start(ins, outs, sems):
        for cp in copies(outs, sems, "mine"):
            cp.start()

    def finish(ins, outs, sems):
        for cp in copies(outs, sems, "mine"):
            cp.wait_send()
        for cp in copies(outs, sems, "theirs"):
            cp.wait_recv()

    return _Plan(gs, [jax.ShapeDtypeStruct(g.shape, g.dtype) for g in gs], {w: w for w in range(n)},
                 [pltpu.SemaphoreType.DMA((n,))] * 2, start, finish)


def _same_shape_groups(arrays):
    groups = {}
    for i, a in enumerate(arrays):
        groups.setdefault(a.shape, []).append(i)
    return list(groups.values())


def _add_sibling(gs, r1s, ids, tag):
    n = len(gs)
    nch, rh, cols = r1s[0].shape

    def body(ids_ref, *refs):
        for g_ref, r_ref, o_ref in zip(refs[:n], refs[n:2 * n], refs[2 * n:]):
            o_ref[...] = (g_ref[...] + r_ref[...]).astype(BF16)

    blk = lambda fn: pl.BlockSpec((None, rh, cols), fn)
    return pl.pallas_call(
        body, out_shape=[jax.ShapeDtypeStruct(r1s[0].shape, BF16)] * n,
        grid_spec=pltpu.PrefetchScalarGridSpec(
            num_scalar_prefetch=1, grid=(nch,),
            in_specs=[blk(lambda k, ids: (k, ids[1], 0))] * n + [blk(lambda k, ids: (k, 0, 0))] * n,
            out_specs=[blk(lambda k, ids: (k, 0, 0))] * n),
        compiler_params=_params(), name="add_sibling_" + tag,
    )(ids, *gs, *r1s)


def _add_chips(gs, r1s, r2s, ids, tag):
    n = len(gs)
    _, rh, cols = r1s[0].shape
    nb = 2 if rh % 32 == 0 else 1
    rb = rh // nb

    def body(ids_ref, *refs):
        for g_ref, r1_ref, r2_ref, o_ref in zip(refs[:n], refs[n:2 * n], refs[2 * n:3 * n], refs[3 * n:]):
            own = g_ref[...] + r1_ref[...]
            o_ref[...] = ((own + r2_ref[0].astype(F32)) + r2_ref[1].astype(F32)) + r2_ref[2].astype(F32)

    return pl.pallas_call(
        body, out_shape=[jax.ShapeDtypeStruct((2 * rh, cols), F32)] * n,
        grid_spec=pltpu.PrefetchScalarGridSpec(
            num_scalar_prefetch=1, grid=(nb,),
            in_specs=[pl.BlockSpec((None, rb, cols), lambda i, ids: (ids[0], ids[1] * nb + i, 0))] * n
            + [pl.BlockSpec((None, rb, cols), lambda i, ids: (ids[0], i, 0))] * n
            + [pl.BlockSpec((3, rb, cols), lambda i, ids: (0, i, 0))] * n,
            out_specs=[pl.BlockSpec((rb, cols), lambda i, ids: (ids[1] * nb + i, 0))] * n),
        compiler_params=_params(), name="add_chips_" + tag,
    )(ids, *gs, *r1s, *r2s)


VEC_ROWS = 8


N_DEVICES = 8


def _small_pack(part, d, width):
    names = ("ffn1_norm", "mix_norm", "ffn2_norm", "pool_scale", "out_norm_pool", "out_norm_attn", "qn", "kn", "b_forget",
             "pool_w", "loss")
    args = [part[k] for k in names]
    pw_shape = part["pool_w"].shape[1:]

    def body(g1_ref, gm_ref, g2_ref, ps_ref, onp_ref, ona_ref, qn_ref, kn_ref, bf_ref, pw_ref, loss_ref, vbuf, pbuf):
        lo = _head_masks()

        def fold_heads(ref):
            v = jnp.sum(ref[...], axis=0)
            acc = jnp.zeros((VEC_ROWS, LANES), F32)
            for blk in range(width // LANES):
                vb = jnp.broadcast_to(v[:, blk * LANES:(blk + 1) * LANES], (VEC_ROWS, LANES))
                acc = acc + vb + pltpu.roll(vb, HEAD_DIM, 1)
            return jnp.where(lo, acc, 0.0)[0:1, :]

        vbuf[0] = jnp.zeros((VEC_ROWS, d), F32)
        vbuf[0, 0:1, :] = jnp.sum(g1_ref[...], axis=0)
        vbuf[0, 1:2, :] = jnp.sum(gm_ref[...], axis=0)
        vbuf[0, 2:3, :] = jnp.sum(g2_ref[...], axis=0)
        vbuf[0, 5:6, 0:LANES] = jnp.sum(loss_ref[...], axis=0)[0:1, :]
        vbuf[0, 3:4, 0:width] = jnp.sum(ps_ref[...], axis=0)
        vbuf[0, 3:4, width:2 * width] = jnp.sum(onp_ref[...], axis=0)
        vbuf[0, 4:5, 0:width] = jnp.sum(ona_ref[...], axis=0)
        vbuf[0, 4:5, width:width + LANES] = fold_heads(qn_ref)
        vbuf[0, 4:5, width + LANES:width + 2 * LANES] = fold_heads(kn_ref)
        vbuf[0, 4:5, width + 2 * LANES:width + 3 * LANES] = jnp.sum(bf_ref[...], axis=0)
        pbuf[0] = jnp.sum(pw_ref[...], axis=0)

    return pl.pallas_call(
        body, out_shape=[jax.ShapeDtypeStruct((N_DEVICES, VEC_ROWS, d), F32), jax.ShapeDtypeStruct((N_DEVICES,) + pw_shape, F32)],
        in_specs=[VM] * len(args), out_specs=[VM, VM], compiler_params=_params(), name="small_pack",
    )(*args)


def _plan_all_to_all(stacks):
    n = len(stacks)

    def copies(outs, sems):
        x, y, c = _mesh_pos()
        cps = []
        for r in range(1, N_DEVICES):
            peer = (x if not r & 4 else 1 - x, y if not r & 2 else 1 - y, c if not r & 1 else 1 - c)
            cps += [_remote(outs[w].at[0], outs[w].at[r], sems[0].at[w, r - 1], sems[1].at[w, r - 1], peer) for w in range(n)]
        return cps

    def start(ins, outs, sems):
        for cp in copies(outs, sems):
            cp.start()

    def finish(ins, outs, sems):
        for cp in copies(outs, sems):
            cp.wait()

    return _Plan(stacks, [jax.ShapeDtypeStruct(s.shape, s.dtype) for s in stacks], {w: w for w in range(n)},
                 [pltpu.SemaphoreType.DMA((n, N_DEVICES - 1))] * 2, start, finish)


def _small_sum(vstack, pstack, me):
    def body(me_ref, vbuf, pbuf, vec_ref, pw_ref):
        vec = vbuf[me_ref[0]]
        pw = pbuf[me_ref[0]]
        for dev in range(1, N_DEVICES):
            vec = vec + vbuf[jnp.bitwise_xor(me_ref[0], dev)]
            pw = pw + pbuf[jnp.bitwise_xor(me_ref[0], dev)]
        vec_ref[...] = vec
        pw_ref[...] = pw

    full = lambda s: pl.BlockSpec(s.shape, lambda i, me: (0,) * len(s.shape))
    outs = [jax.ShapeDtypeStruct(vstack.shape[1:], F32), jax.ShapeDtypeStruct(pstack.shape[1:], F32)]
    return pl.pallas_call(
        body, out_shape=outs,
        grid_spec=pltpu.PrefetchScalarGridSpec(num_scalar_prefetch=1, grid=(1,), in_specs=[full(vstack), full(pstack)],
                                               out_specs=[full(o) for o in outs]),
        compiler_params=_params(), name="small_sum",
    )(me, vstack, pstack)


def _adamw(ws, gs, ms, vs, tag):
    n = len(ws)
    rows, cols = ws[0].shape
    rb = rows
    while rb * cols * 4 * n > (1 << 20) and rb % 16 == 0:
        rb //= 2

    def body(*refs):
        for j in range(n):
            w_ref, g_ref, m_ref, v_ref = (refs[k * n + j] for k in range(4))
            d_ref, mo_ref, vo_ref = (refs[(4 + k) * n + j] for k in range(3))
            gv = g_ref[...]
            m2 = ADAM_B1 * m_ref[...] + (1.0 - ADAM_B1) * gv
            v2 = ADAM_B2 * v_ref[...] + (1.0 - ADAM_B2) * (gv * gv)
            m_hat = m2 / (1.0 - ADAM_B1 ** ADAM_STEP)
            v_hat = v2 / (1.0 - ADAM_B2 ** ADAM_STEP)
            d_ref[...] = -ADAM_LR * (m_hat / (jnp.sqrt(v_hat) + ADAM_EPS) + ADAM_WD * w_ref[...])
            mo_ref[...] = m2
            vo_ref[...] = v2

    spec = pl.BlockSpec((rb, cols), lambda i: (i, 0))
    res = pl.pallas_call(
        body, out_shape=[jax.ShapeDtypeStruct(ws[0].shape, F32)] * (3 * n), grid=(rows // rb,),
        in_specs=[spec] * (4 * n), out_specs=[spec] * (3 * n), compiler_params=_params(), name="adamw_" + tag,
    )(*ws, *gs, *ms, *vs)
    return [(res[j], res[n + j], res[2 * n + j]) for j in range(n)]


def _pack_vec(p, d, width):
    pad = lambda v: jnp.pad(v, (0, LANES - v.shape[0]))
    row3 = jnp.concatenate([p["pool_scale"], p["out_norm_pool"]])
    row4 = jnp.concatenate([p["out_norm_attn"], pad(p["q_norm"]), pad(p["k_norm"]), pad(p["b_forget"]),
                            jnp.zeros((d - width - 3 * LANES,), F32)])
    rows = [p["ffn1_norm"], p["mix_norm"], p["ffn2_norm"], row3, row4]
    return jnp.pad(jnp.stack(rows), ((0, VEC_ROWS - len(rows)), (0, 0)))


def _unpack_vec(vec, width):
    return dict(ffn1_norm=vec[0], mix_norm=vec[1], ffn2_norm=vec[2], pool_scale=vec[3, :width],
                out_norm_pool=vec[3, width:2 * width], out_norm_attn=vec[4, :width],
                q_norm=vec[4, width:width + HEAD_DIM], k_norm=vec[4, width + LANES:width + LANES + HEAD_DIM],
                b_forget=vec[4, width + 2 * LANES:width + 2 * LANES + N_HEADS])


WEIGHT_NAMES = ("ffn1_norm", "ffn1_w_gate", "ffn1_w_up", "ffn1_w_down", "mix_norm", "w_in", "b_forget", "pool_w",
                "pool_scale", "q_norm", "k_norm", "out_norm_pool", "out_norm_attn", "w_out", "ffn2_norm",
                "ffn2_w_gate", "ffn2_w_up", "ffn2_w_down")
BIG_NAMES = ("ffn1_w_gate", "ffn1_w_up", "ffn1_w_down", "w_in", "w_out", "ffn2_w_gate", "ffn2_w_up", "ffn2_w_down")
TRANSPOSED_NAMES = ("ffn1_w_gate", "ffn1_w_up", "w_in", "ffn2_w_gate", "ffn2_w_up")
FFN1_NAMES = ("ffn1_w_gate", "ffn1_w_up", "ffn1_w_down")
MIX_NAMES = ("w_in", "w_out")
FFN2_NAMES = ("ffn2_w_gate", "ffn2_w_up", "ffn2_w_down")


def kernel(x, ffn1_norm, ffn1_w_gate, ffn1_w_up, ffn1_w_down, mix_norm, w_in, b_forget, pool_w, pool_scale, q_norm, k_norm, out_norm_pool, out_norm_attn, w_out, ffn2_norm, ffn2_w_gate, ffn2_w_up, ffn2_w_down, loss_target, m_ffn1_norm, m_ffn1_w_gate, m_ffn1_w_up, m_ffn1_w_down, m_mix_norm, m_w_in, m_b_forget, m_pool_w, m_pool_scale, m_q_norm, m_k_norm, m_out_norm_pool, m_out_norm_attn, m_w_out, m_ffn2_norm, m_ffn2_w_gate, m_ffn2_w_up, m_ffn2_w_down, v_ffn1_norm, v_ffn1_w_gate, v_ffn1_w_up, v_ffn1_w_down, v_mix_norm, v_w_in, v_b_forget, v_pool_w, v_pool_scale, v_q_norm, v_k_norm, v_out_norm_pool, v_out_norm_attn, v_w_out, v_ffn2_norm, v_ffn2_w_gate, v_ffn2_w_up, v_ffn2_w_down):
    given = dict(locals())
    w = {n: given[n] for n in WEIGHT_NAMES}
    m = {n: given["m_" + n] for n in WEIGHT_NAMES}
    v = {n: given["v_" + n] for n in WEIGHT_NAMES}
    n_batch, seq, d = x.shape
    width = pool_scale.shape[0]
    in_rows = w_in.shape[1]
    in_cols = N_CHIPS * in_rows
    in_pad = -(-in_rows // 32) * 32
    in_cols_pad = in_cols - N_HEADS + LANES

    work = lambda a, n: a.T if n in TRANSPOSED_NAMES else a
    exchanged = lambda a, n: jnp.pad(a, ((0, in_pad - in_rows), (0, 0))) if n == "w_in" else a

    mesh_x, mesh_y, mesh_c = _mesh_pos()
    ids = jnp.stack([2 * mesh_x + mesh_y, mesh_c]).astype(jnp.int32)

    row = lambda a: a.reshape(1, -1)
    g1, gm, g2, ps, onp, ona = (row(a) for a in (ffn1_norm, mix_norm, ffn2_norm, pool_scale, out_norm_pool, out_norm_attn))
    qn, kn = row(jnp.tile(q_norm, N_HEADS)), row(jnp.tile(k_norm, N_HEADS))
    bf = row(jnp.pad(b_forget, (0, LANES - N_HEADS)))
    pwb = pool_w.astype(BF16)
    xf, tgt = x.reshape(n_batch * seq, d), loss_target.reshape(n_batch * seq, d)

    def grouped(call, names, *lists):
        out = [None] * len(names)
        for idx in _same_shape_groups(lists[0]):
            res = call(*[[lst[i] for i in idx] for lst in lists], names[idx[0]])
            for i, r in zip(idx, res):
                out[i] = r
        return out

    placed = dict(zip(BIG_NAMES, grouped(lambda ws, tag: _place_cast(ws, ids, tag), BIG_NAMES,
                                         [exchanged(work(w[n], n), n) for n in BIG_NAMES])))
    wg1, wu1, wd1 = _run_plan(_plan_gather([placed[n] for n in FFN1_NAMES]), "gather_ffn1")
    (x1, h1, a1, b1, s1), (w_in_all, w_out_all, wd2) = _ffn_fwd(
        xf, g1, wg1, wu1, wd1, plan=_plan_gather([placed[n] for n in MIX_NAMES + FFN2_NAMES[2:]]))
    w_in_t = jnp.pad(w_in_all[:, :in_rows].reshape(in_cols, d), ((0, in_cols_pad - in_cols), (0, 0)))
    w_out_full = w_out_all.reshape(N_CHIPS * w_out.shape[0], d)
    woa, wob = w_out_full[:width], w_out_full[width:]

    hm, pv, q, k, qh, kh, vb, f = _mix_proj(x1, gm, w_in_t, qn, kn, width, width)
    qa, ka = _forget_prefix(f, bf, qh, kh, n_batch, seq)
    yp = _pool_fwd(pv, pwb, ps, onp, n_batch, seq)
    (o, lse), (wg2, wu2) = _attn_fwd(qa, ka, vb, n_batch, seq, plan=_plan_gather([placed[n] for n in FFN2_NAMES[:2]]))
    x2, ya = _mix_out(x1, yp, o, ona, woa, wob)
    (dy, h2, a2, b2, s2, lpart), _ = _ffn_fwd(x2, g2, wg2, wu2, wd2, target=tgt)

    def to_chips(gs, arrived, tags):
        return grouped(lambda g, r, tag: _add_sibling(g, r, ids, tag), tags, gs, arrived)

    def own_rows(gs, from_sibling, from_chips, tags):
        return grouped(lambda g, ra, rb, tag: _add_chips(g, ra, rb, ids, tag), tags, gs, from_sibling, from_chips)

    (dx2, da2, db2, dg2), _ = _ffn_bwd_x(dy, x2, g2, a2, b2, wg2, wu2, wd2, "ffn2_bwd_x")
    dw2, _ = _ffn_bwd_w([(da2, h2, 1.0), (db2, h2, 1.0), (s2, dy, 0.5)], "ffn2_bwd_w")
    (dyp, do, delta, dwoa, dwob, dona), sib2 = _mix_out_bwd(dx2, o, yp, ya, ona, woa, wob, plan=_plan_sibling_halves(dw2))
    dpv, dpw, dps, donp = _pool_bwd(pv, dyp, pwb, ps, onp, n_batch, seq)
    (dqh, dfq), chips2 = _attn_bwd_q(qa, ka, vb, do, lse, delta, n_batch, seq,
                                     plan=_plan_chip_exchange(to_chips(dw2, sib2, FFN2_NAMES)))
    (dkh, dv, dfk), red2 = _attn_bwd_kv(qa, ka, vb, do, lse, delta, n_batch, seq,
                                        plan=_plan_sibling_share(own_rows(dw2, sib2, chips2, FFN2_NAMES)))
    df, dbf = _forget_bwd(dfq, dfk, f, bf, n_batch, seq)
    dx1, dw_in_t, dgm, dqn, dkn = _mix_in_bwd(dx2, x1, gm, hm, dpv, dqh, q, dkh, k, dv, df, qn, kn, w_in_t)
    in_base = [in_rows * k // 8 * 8 for k in range(N_CHIPS)]
    d_w_in = jnp.stack([dw_in_t[b:b + in_pad] for b in in_base])
    d_w_out = jnp.concatenate([dwoa, dwob], axis=0).reshape(N_CHIPS, w_out.shape[0], d)
    dwm = [d_w_in, d_w_out]
    down, gate_up = FFN1_NAMES[2:], FFN1_NAMES[:2]
    dwd1, sibm = _ffn_bwd_w([(s1, dx1, 0.5)], "ffn1_bwd_w_down", plan=_plan_sibling_halves(dwm))
    (da1, db1), arrived = _ffn_bwd_a(dx1, a1, b1, wd1, "ffn1_bwd_a",
                                     plan=_merge_plans(_plan_sibling_halves(dwd1),
                                                       _plan_chip_exchange(to_chips(dwm, sibm, MIX_NAMES))))
    sibd, chipsm = arrived[:1], arrived[1:]
    dwgu1, chipsd = _ffn_bwd_w([(da1, h1, 1.0), (db1, h1, 1.0)], "ffn1_bwd_w_gate_up",
                               plan=_plan_chip_exchange(to_chips(dwd1, sibd, down)))
    n_tiles = (n_batch * seq) // min(FFN_TILE, n_batch * seq)
    first = max(n_tiles // 4, 1)
    begun, sibgu = _ffn_bwd_h(dx1, xf, g1, da1, db1, wg1, wu1, "ffn1_bwd_h_first", (0, first),
                              plan=_plan_sibling_halves(dwgu1))
    (gx, dg1), chipsgu = _ffn_bwd_h(dx1, xf, g1, da1, db1, wg1, wu1, "ffn1_bwd_h_rest", (first, n_tiles), prev=begun,
                                    plan=_plan_chip_exchange(to_chips(dwgu1, sibgu, gate_up)))

    part = dict(ffn1_norm=dg1, mix_norm=dgm, ffn2_norm=dg2, b_forget=dbf, pool_scale=dps, out_norm_pool=donp,
                out_norm_attn=dona, qn=dqn, kn=dkn, pool_w=dpw.reshape(n_batch, -1, pool_w.shape[-1]), loss=lpart)
    mine = (own_rows(dwgu1, sibgu, chipsgu, gate_up) + own_rows(dwd1, sibd, chipsd, down)
            + own_rows(dwm, sibm, chipsm, MIX_NAMES))
    last = _run_plan(_merge_plans(_plan_sibling_share(mine), _plan_all_to_all(_small_pack(part, d, width))), "last_exchange")
    vstack, pstack = last[len(mine):]
    g_vec, g_pw = _small_sum(vstack, pstack, jnp.reshape(4 * mesh_x + 2 * mesh_y + mesh_c, (1,)).astype(jnp.int32))
    loss = g_vec[5, 0]
    reduced = dict(zip(FFN1_NAMES + MIX_NAMES + FFN2_NAMES, list(last[:len(mine)]) + list(red2)))
    reduced["w_in"] = lax.dynamic_slice(reduced["w_in"], ((in_rows * ids[0]) % 8, 0), (in_rows, d))

    grads, delta, new_m, new_v = {}, {}, {}, {}
    for names in (FFN2_NAMES, FFN1_NAMES, ("w_in",), ("w_out",)):
        stepped = _adamw([work(w[n], n) for n in names], [reduced[n] for n in names], [work(m[n], n) for n in names],
                         [work(v[n], n) for n in names], names[0])
        for n, step in zip(names, stepped):
            grads[n], delta[n], new_m[n], new_v[n] = (work(a, n) for a in (reduced[n], *step))
    flat_pw = lambda a: a.reshape(-1, a.shape[-1])
    (d_pw, m_pw, v_pw), = _adamw([flat_pw(pool_w)], [g_pw], [flat_pw(m_pool_w)], [flat_pw(v_pool_w)], "pool_w")
    (d_vec, m_vec, v_vec), = _adamw([_pack_vec(w, d, width)], [g_vec], [_pack_vec(m, d, width)], [_pack_vec(v, d, width)],
                                    "vectors")
    grads.update(_unpack_vec(g_vec, width), pool_w=g_pw.reshape(pool_w.shape))
    delta.update(_unpack_vec(d_vec, width), pool_w=d_pw.reshape(pool_w.shape))
    new_m.update(_unpack_vec(m_vec, width), pool_w=m_pw.reshape(pool_w.shape))
    new_v.update(_unpack_vec(v_vec, width), pool_w=v_pw.reshape(pool_w.shape))
    return (loss, gx.reshape(x.shape), *[grads[n] for n in WEIGHT_NAMES], *[delta[n] for n in WEIGHT_NAMES],
            *[new_m[n] for n in WEIGHT_NAMES], *[new_v[n] for n in WEIGHT_NAMES])
```

```python
import functools

import jax
import jax.numpy as jnp
from jax import lax
from jax.experimental import pallas as pl
from jax.experimental.pallas import tpu as pltpu

F32 = jnp.float32
BF16 = jnp.bfloat16
EPS = 1e-6
NEG = -1e30
ADAM_LR = 0.001
ADAM_B1 = 0.9
ADAM_B2 = 0.999
ADAM_EPS = 1e-08
ADAM_WD = 0.01
ADAM_STEP = 10
POOL_WINDOWS = (2, 4, 8, 16)
HEAD_DIM = 64
N_HEADS = 8
LANES = 128
N_CHIPS = 4
ATT_BLOCK = 512
ATT_SUB = 128
FFN_TILE = 1024
VMEM_LIMIT = 62 * 1024 * 1024
MESH_AXES = ("x", "y", "c")
ANY = pl.BlockSpec(memory_space=pl.ANY)
VM = pl.BlockSpec(memory_space=pltpu.VMEM)


def _params(**kw):
    return pltpu.CompilerParams(vmem_limit_bytes=VMEM_LIMIT, **kw)


def _dot(a, b):
    return jnp.dot(a, b, preferred_element_type=F32)


def _dot_nt(a, b):
    return lax.dot_general(a, b, (((1,), (1,)), ((), ())), preferred_element_type=F32)


def _dot_tn(a, b):
    return lax.dot_general(a, b, (((0,), (0,)), ((), ())), preferred_element_type=F32)


def _sigmoid(z):
    return 1.0 / (1.0 + jnp.exp(-z))


def _rms(xf):
    return lax.rsqrt(jnp.mean(xf * xf, axis=-1, keepdims=True) + EPS)


def _rms_bwd(xf, r, gain, dh):
    xh = xf * r
    dyg = dh * gain
    return r * (dyg - xh * jnp.mean(dyg * xh, axis=-1, keepdims=True)), dh * xh


def _total(v):
    return jnp.sum(jnp.sum(v, axis=1, keepdims=True), axis=0, keepdims=True)


def _ffn_fwd(x, gain, wg, wu, wd, target=None, plan=None):
    t, d = x.shape
    nch, fc, _ = wg.shape
    tm = min(FFN_TILE, t)
    nt = t // tm
    with_loss = target is not None

    def body(*refs):
        if with_loss:
            x_ref, g_ref, wg_ref, wu_ref, wd_ref, t_ref, o_ref, h_ref, a_ref, b_ref, s_ref, l_ref, acc_ref = refs
        else:
            x_ref, g_ref, wg_ref, wu_ref, wd_ref, o_ref, h_ref, a_ref, b_ref, s_ref, acc_ref = refs
        k = pl.program_id(1)

        @pl.when(k == 0)
        def _():
            xf = x_ref[...]
            h_ref[...] = ((xf * _rms(xf)) * g_ref[...]).astype(BF16)
            acc_ref[...] = jnp.zeros_like(acc_ref)

        for rows in _row_halves(tm):
            h = h_ref[rows, :]
            a = _dot_nt(h, wg_ref[...])
            b = _dot_nt(h, wu_ref[...])
            sb = ((a * (0.5 * jnp.tanh(0.5 * a) + 0.5)) * b).astype(BF16)
            a_ref[rows, :] = a.astype(BF16)
            b_ref[rows, :] = b.astype(BF16)
            s_ref[rows, :] = sb
            acc_ref[rows, :] += _dot(sb, wd_ref[...])

        @pl.when(k == nch - 1)
        def _():
            y = x_ref[...] + 0.5 * acc_ref[...]
            if with_loss:
                e = y - t_ref[...]
                o_ref[...] = e * (1.0 / d)
                l_ref[...] = jnp.broadcast_to(_total(e * e) * (0.5 / d), l_ref.shape)
            else:
                o_ref[...] = y

    row = pl.BlockSpec((tm, d), lambda i, k: (i, 0))
    chunk = pl.BlockSpec((None, fc, d), lambda i, k: (k, 0, 0))
    act = pl.BlockSpec((None, tm, fc), lambda i, k: (k, i, 0))
    in_specs = [row, pl.BlockSpec((1, d), lambda i, k: (0, 0)), chunk, chunk, chunk]
    out_shape = [jax.ShapeDtypeStruct((t, d), F32), jax.ShapeDtypeStruct((t, d), BF16)]
    out_shape += [jax.ShapeDtypeStruct((nch, t, fc), BF16)] * 3
    out_specs = [row, row, act, act, act]
    args = [x, gain, wg, wu, wd]
    if with_loss:
        in_specs.append(row)
        args.append(target)
        out_shape.append(jax.ShapeDtypeStruct((nt, 8, LANES), F32))
        out_specs.append(pl.BlockSpec((None, 8, LANES), lambda i, k: (i, 0, 0)))
    return _pallas(body, name="ffn_fwd_loss" if with_loss else "ffn_fwd", args=args, in_specs=in_specs,
                   out_shape=out_shape, out_specs=out_specs, grid=(nt, nch),
                   scratch_shapes=[pltpu.VMEM((tm, d), F32)], plan=plan)


def _row_halves(n):
    return [slice(0, n // 2), slice(n // 2, n)]


def _swiglu_grads(dy_ref, a_ref, b_ref, wd_ref, rows):
    ds = _dot_nt(dy_ref[rows, :].astype(BF16), wd_ref[...])
    av = a_ref[rows, :].astype(F32)
    bv = b_ref[rows, :].astype(F32)
    th = jnp.tanh(0.5 * av)
    half_sig = 0.25 * th + 0.25
    dab = ((ds * bv) * (half_sig * (1.0 + av * (0.5 - 0.5 * th)))).astype(BF16)
    return dab, (ds * (av * half_sig)).astype(BF16)


def _ffn_bwd_a(dy, a, b, wd, name, plan=None):
    t, d = dy.shape
    nch, fc, _ = wd.shape
    tm = min(FFN_TILE, t)

    def body(dy_ref, a_ref, b_ref, wd_ref, da_ref, db_ref):
        for rows in _row_halves(tm):
            da_ref[rows, :], db_ref[rows, :] = _swiglu_grads(dy_ref, a_ref, b_ref, wd_ref, rows)

    act = pl.BlockSpec((None, tm, fc), lambda i, k: (k, i, 0))
    return _pallas(
        body, name=name, args=[dy, a, b, wd], out_shape=[jax.ShapeDtypeStruct((nch, t, fc), BF16)] * 2, grid=(t // tm, nch),
        in_specs=[pl.BlockSpec((tm, d), lambda i, k: (i, 0)), act, act, pl.BlockSpec((None, fc, d), lambda i, k: (k, 0, 0))],
        out_specs=[act, act], plan=plan)


def _ffn_bwd_h(dy, x, gain, da, db, wg, wu, name, tiles, prev=None, plan=None):
    t, d = x.shape
    nch, fc, _ = wg.shape
    tm = min(FFN_TILE, t)
    nt = t // tm
    t0, t1 = tiles

    def body(*refs):
        dy_ref, x_ref, g_ref, da_ref, db_ref, wg_ref, wu_ref = refs[:7]
        dx_ref, dg_ref, acc_ref = refs[-3:]
        k = pl.program_id(1)

        @pl.when(k == 0)
        def _():
            acc_ref[...] = jnp.zeros_like(acc_ref)

        acc_ref[...] += _dot(da_ref[...], wg_ref[...]) + _dot(db_ref[...], wu_ref[...])

        @pl.when(k == nch - 1)
        def _():
            xf = x_ref[...]
            dxn, dgr = _rms_bwd(xf, _rms(xf), g_ref[...], acc_ref[...])
            dx_ref[...] = dy_ref[...] + dxn
            dg_ref[...] = jnp.sum(dgr, axis=0, keepdims=True)

    row = pl.BlockSpec((tm, d), lambda i, k: (i + t0, 0))
    chunk = pl.BlockSpec((None, fc, d), lambda i, k: (k, 0, 0))
    act = pl.BlockSpec((None, tm, fc), lambda i, k: (k, i + t0, 0))
    args = [dy, x, gain, da, db, wg, wu]
    in_specs = [row, row, pl.BlockSpec((1, d), lambda i, k: (0, 0)), act, act, chunk, chunk]
    aliases = {}
    if prev is not None:
        aliases = {len(args): 0, len(args) + 1: 1}
        args += list(prev)
        in_specs += [ANY, ANY]
    return _pallas(
        body, name=name, args=args, out_shape=[jax.ShapeDtypeStruct((t, d), F32), jax.ShapeDtypeStruct((nt, 1, d), F32)],
        grid=(t1 - t0, nch), in_specs=in_specs,
        out_specs=[row, pl.BlockSpec((None, 1, d), lambda i, k: (i + t0, 0, 0))],
        scratch_shapes=[pltpu.VMEM((tm, d), F32)], plan=plan, aliases=aliases)


def _ffn_bwd_x(dy, x, gain, a, b, wg, wu, wd, name, plan=None):
    t, d = x.shape
    nch, fc, _ = wg.shape
    tm = min(FFN_TILE, t)
    nt = t // tm

    def body(dy_ref, x_ref, g_ref, a_ref, b_ref, wg_ref, wu_ref, wd_ref, dx_ref, da_ref, db_ref, dg_ref, acc_ref):
        k = pl.program_id(1)

        @pl.when(k == 0)
        def _():
            acc_ref[...] = jnp.zeros_like(acc_ref)

        for rows in _row_halves(tm):
            dab, dbb = _swiglu_grads(dy_ref, a_ref, b_ref, wd_ref, rows)
            da_ref[rows, :] = dab
            db_ref[rows, :] = dbb
            acc_ref[rows, :] += _dot(dab, wg_ref[...]) + _dot(dbb, wu_ref[...])

        @pl.when(k == nch - 1)
        def _():
            xf = x_ref[...]
            dxn, dgr = _rms_bwd(xf, _rms(xf), g_ref[...], acc_ref[...])
            dx_ref[...] = dy_ref[...] + dxn
            dg_ref[...] = jnp.sum(dgr, axis=0, keepdims=True)

    row = pl.BlockSpec((tm, d), lambda i, k: (i, 0))
    chunk = pl.BlockSpec((None, fc, d), lambda i, k: (k, 0, 0))
    act = pl.BlockSpec((None, tm, fc), lambda i, k: (k, i, 0))
    return _pallas(
        body, name=name, args=[dy, x, gain, a, b, wg, wu, wd],
        out_shape=[jax.ShapeDtypeStruct((t, d), F32), jax.ShapeDtypeStruct((nch, t, fc), BF16),
                   jax.ShapeDtypeStruct((nch, t, fc), BF16), jax.ShapeDtypeStruct((nt, 1, d), F32)],
        grid=(nt, nch),
        in_specs=[row, row, pl.BlockSpec((1, d), lambda i, k: (0, 0)), act, act, chunk, chunk, chunk],
        out_specs=[row, act, act, pl.BlockSpec((None, 1, d), lambda i, k: (i, 0, 0))],
        scratch_shapes=[pltpu.VMEM((tm, d), F32)], plan=plan)


def _ffn_bwd_w(pairs, name, plan=None):
    n = len(pairs)
    nch, t, fc = pairs[0][0].shape
    d = pairs[0][1].shape[1]
    tm = min(1024, t)

    def body(*refs):
        @pl.when(pl.program_id(1) == 0)
        def _():
            for o_ref in refs[2 * n:]:
                o_ref[...] = jnp.zeros_like(o_ref)

        for j, (_, _, scale) in enumerate(pairs):
            other = refs[n + j][...]
            if other.dtype != BF16:
                other = (scale * other).astype(BF16)
            refs[2 * n + j][...] += _dot_tn(refs[j][...], other)

    row = pl.BlockSpec((tm, d), lambda k, i: (i, 0))
    act = pl.BlockSpec((None, tm, fc), lambda k, i: (k, i, 0))
    chunk = pl.BlockSpec((None, fc, d), lambda k, i: (k, 0, 0))
    return _pallas(body, name=name, args=[p[0] for p in pairs] + [p[1] for p in pairs],
                   out_shape=[jax.ShapeDtypeStruct((nch, fc, d), F32)] * n, grid=(nch, t // tm),
                   in_specs=[act] * n + [row] * n, out_specs=[chunk] * n, plan=plan)


def _head_masks():
    lane = lax.broadcasted_iota(jnp.int32, (1, LANES), 1)
    return lane < HEAD_DIM


def _head_rms(x, lo):
    x2 = x * x
    s0 = jnp.sum(jnp.where(lo, x2, 0.0), axis=1, keepdims=True)
    s1 = jnp.sum(jnp.where(lo, 0.0, x2), axis=1, keepdims=True)
    return jnp.where(lo, lax.rsqrt(s0 * (1.0 / HEAD_DIM) + EPS), lax.rsqrt(s1 * (1.0 / HEAD_DIM) + EPS))


def _head_mean(v, lo):
    s0 = jnp.sum(jnp.where(lo, v, 0.0), axis=1, keepdims=True)
    s1 = jnp.sum(jnp.where(lo, 0.0, v), axis=1, keepdims=True)
    return jnp.where(lo, s0, s1) * (1.0 / HEAD_DIM)


def _mix_proj(x1, gain, wt, qn, kn, pool_width, attn_width, plan=None):
    t, d = x1.shape
    tm = min(512, t)
    nt = t // tm
    scale = HEAD_DIM ** -0.5
    c_q, c_k, c_v = pool_width, pool_width + attn_width, pool_width + 2 * attn_width
    c_f = c_v + attn_width

    def body(x_ref, g_ref, wt_ref, qn_ref, kn_ref, hm_ref, pv_ref, q_ref, k_ref, qh_ref, kh_ref, vb_ref, f_ref):
        lo = _head_masks()
        for rows in _row_halves(tm):
            xf = x_ref[rows, :]
            hm = ((xf * _rms(xf)) * g_ref[...]).astype(BF16)
            hm_ref[rows, :] = hm
            f_ref[rows, :] = _dot_nt(hm, wt_ref[c_f:c_f + LANES, :])
            pv_ref[rows, :] = _dot_nt(hm, wt_ref[0:pool_width, :])
            vb_ref[rows, :] = _dot_nt(hm, wt_ref[c_v:c_v + attn_width, :]).astype(BF16)
            for c0, raw_ref, hat_ref, n_ref, mul in ((c_q, q_ref, qh_ref, qn_ref, scale), (c_k, k_ref, kh_ref, kn_ref, 1.0)):
                raw = _dot_nt(hm, wt_ref[c0:c0 + attn_width, :])
                raw_ref[rows, :] = raw
                for blk in range(attn_width // LANES):
                    sl = slice(blk * LANES, (blk + 1) * LANES)
                    xb = raw[:, sl]
                    hat_ref[rows, sl] = (((xb * _head_rms(xb, lo)) * n_ref[:, sl]) * mul).astype(BF16)

    row = pl.BlockSpec((tm, d), lambda i: (i, 0))
    half = pl.BlockSpec((tm, attn_width), lambda i: (i, 0))
    const = lambda shape: pl.BlockSpec(shape, lambda i: (0, 0))
    return _pallas(
        body, name="mix_proj", args=[x1, gain, wt, qn, kn],
        out_shape=[jax.ShapeDtypeStruct((t, d), BF16), jax.ShapeDtypeStruct((t, pool_width), F32),
                   jax.ShapeDtypeStruct((t, attn_width), F32), jax.ShapeDtypeStruct((t, attn_width), F32),
                   jax.ShapeDtypeStruct((t, attn_width), BF16), jax.ShapeDtypeStruct((t, attn_width), BF16),
                   jax.ShapeDtypeStruct((t, attn_width), BF16), jax.ShapeDtypeStruct((t, LANES), F32)],
        grid=(nt,),
        in_specs=[row, const((1, d)), const(wt.shape), const((1, attn_width)), const((1, attn_width))],
        out_specs=[row, pl.BlockSpec((tm, pool_width), lambda i: (i, 0)), half, half, half, half, half,
                   pl.BlockSpec((tm, LANES), lambda i: (i, 0))], plan=plan)


def _shift_down(v, dist, row):
    return jnp.where(row >= dist, pltpu.roll(v, dist, 0), 0.0)


def _shift_up(v, dist, row, n):
    return jnp.where(row + dist < n, pltpu.roll(v, n - dist, 0), 0.0)


def _aug_lane(e):
    return HEAD_DIM if e == 0 else 0


def _forget_prefix(f, bias, qh, kh, n_batch, seq):
    def body(f_ref, b_ref, q_ref, k_ref, qa_ref, ka_ref):
        z = f_ref[...] + b_ref[...]
        acc = jnp.minimum(z, 0.0) - jnp.log(1.0 + jnp.exp(-jnp.abs(z)))
        row = lax.broadcasted_iota(jnp.int32, (seq, 1), 0)
        dist = 1
        while dist < seq:
            acc = acc + _shift_down(acc, dist, row)
            dist *= 2
        lane = lax.broadcasted_iota(jnp.int32, (1, LANES), 1)
        for h in range(N_HEADS):
            pair, e = divmod(h, 2)
            a0 = _aug_lane(e)
            own = (lane < HEAD_DIM) if e == 0 else (lane >= HEAD_DIM)
            fh = _pick_lane(acc, h)
            hi = fh.astype(BF16).astype(F32)
            rest = fh - hi
            mid = rest.astype(BF16).astype(F32)
            low = rest - mid
            q_ones = (lane >= a0 + 3) & (lane < a0 + 6)
            k_ones = (lane >= a0) & (lane < a0 + 3)
            q_aug = jnp.where(lane == a0, hi, jnp.where(lane == a0 + 1, mid, jnp.where(lane == a0 + 2, low,
                              jnp.where(q_ones, 1.0, 0.0))))
            k_aug = jnp.where(k_ones, 1.0, jnp.where(lane == a0 + 3, -hi, jnp.where(lane == a0 + 4, -mid,
                              jnp.where(lane == a0 + 5, -low, 0.0))))
            src = slice(pair * LANES, (pair + 1) * LANES)
            dst = slice(h * LANES, (h + 1) * LANES)
            qa_ref[:, dst] = jnp.where(own, q_ref[:, src].astype(F32), q_aug).astype(BF16)
            ka_ref[:, dst] = jnp.where(own, k_ref[:, src].astype(F32), k_aug).astype(BF16)

    width = qh.shape[1]
    tok = pl.BlockSpec((seq, width), lambda b: (b, 0))
    aug = pl.BlockSpec((seq, N_HEADS * LANES), lambda b: (b, 0))
    return pl.pallas_call(
        body, out_shape=[jax.ShapeDtypeStruct((n_batch * seq, N_HEADS * LANES), BF16)] * 2, grid=(n_batch,),
        in_specs=[pl.BlockSpec((seq, LANES), lambda b: (b, 0)), pl.BlockSpec((1, LANES), lambda b: (0, 0)), tok, tok],
        out_specs=[aug, aug], compiler_params=_params(), name="forget_prefix",
    )(f, bias, qh, kh)


def _pool_groups(pv_ref, pw_ref, ps_ref, seq):
    row = lax.broadcasted_iota(jnp.int32, (seq, 1), 0)
    pos = (row + 1).astype(F32)
    out = []
    for g, win in enumerate(POOL_WINDOWS):
        sl = slice(g * LANES, (g + 1) * LANES)
        xg = pv_ref[:, sl]
        acc = xg
        dist = 1
        while dist < win:
            acc = acc + _shift_down(acc, dist, row)
            dist *= 2
        pooled = (acc / jnp.minimum(pos, float(win)) - xg).astype(BF16)
        mixed = _dot(pooled, pw_ref[g])
        out.append((pooled, mixed, mixed * ps_ref[:, sl]))
    return out


def _pool_fwd(pv, pw, ps, onp, n_batch, seq):
    width = pv.shape[1]

    def body(pv_ref, pw_ref, ps_ref, on_ref, y_ref):
        groups = _pool_groups(pv_ref, pw_ref, ps_ref, seq)
        ssq = sum(jnp.sum(ms * ms, axis=1, keepdims=True) for _, _, ms in groups)
        r = lax.rsqrt(ssq * (1.0 / width) + EPS)
        for g, (_, _, ms) in enumerate(groups):
            sl = slice(g * LANES, (g + 1) * LANES)
            y_ref[:, sl] = ((ms * r) * on_ref[:, sl]).astype(BF16)

    return pl.pallas_call(
        body, out_shape=jax.ShapeDtypeStruct((n_batch * seq, width), BF16), grid=(n_batch,),
        in_specs=[pl.BlockSpec((seq, width), lambda b: (b, 0)), pl.BlockSpec(pw.shape, lambda b: (0, 0, 0)),
                  pl.BlockSpec((1, width), lambda b: (0, 0)), pl.BlockSpec((1, width), lambda b: (0, 0))],
        out_specs=pl.BlockSpec((seq, width), lambda b: (b, 0)),
        compiler_params=_params(), name="pool_fwd",
    )(pv, pw, ps, onp)


def _pool_bwd(pv, dyp, pw, ps, onp, n_batch, seq):
    width = pv.shape[1]

    def body(pv_ref, dy_ref, pw_ref, ps_ref, on_ref, dpv_ref, dpw_ref, dps_ref, don_ref):
        groups = _pool_groups(pv_ref, pw_ref, ps_ref, seq)
        ssq = sum(jnp.sum(ms * ms, axis=1, keepdims=True) for _, _, ms in groups)
        r = lax.rsqrt(ssq * (1.0 / width) + EPS)
        mean = sum(jnp.sum((dy_ref[:, g * LANES:(g + 1) * LANES] * on_ref[:, g * LANES:(g + 1) * LANES]) * (ms * r),
                           axis=1, keepdims=True) for g, (_, _, ms) in enumerate(groups)) * (1.0 / width)
        row = lax.broadcasted_iota(jnp.int32, (seq, 1), 0)
        pos = (row + 1).astype(F32)
        for g, (pooled, mixed, ms) in enumerate(groups):
            sl = slice(g * LANES, (g + 1) * LANES)
            dy = dy_ref[:, sl]
            xh = ms * r
            don_ref[:, sl] = jnp.sum(dy * xh, axis=0, keepdims=True)
            dms = r * (dy * on_ref[:, sl] - xh * mean)
            dps_ref[:, sl] = jnp.sum(dms * mixed, axis=0, keepdims=True)
            dmix = (dms * ps_ref[:, sl]).astype(BF16)
            dpw_ref[g] = _dot_tn(pooled, dmix)
            dpool = _dot_nt(dmix, pw_ref[g])
            win = POOL_WINDOWS[g]
            acc = dpool / jnp.minimum(pos, float(win))
            dist = 1
            while dist < win:
                acc = acc + _shift_up(acc, dist, row, seq)
                dist *= 2
            dpv_ref[:, sl] = (acc - dpool).astype(BF16)

    tok = pl.BlockSpec((seq, width), lambda b: (b, 0))
    vec = pl.BlockSpec((1, width), lambda b: (0, 0))
    pvec = pl.BlockSpec((None, 1, width), lambda b: (b, 0, 0))
    return pl.pallas_call(
        body,
        out_shape=[jax.ShapeDtypeStruct((n_batch * seq, width), BF16),
                   jax.ShapeDtypeStruct((n_batch,) + pw.shape, F32),
                   jax.ShapeDtypeStruct((n_batch, 1, width), F32), jax.ShapeDtypeStruct((n_batch, 1, width), F32)],
        grid=(n_batch,),
        in_specs=[tok, tok, pl.BlockSpec(pw.shape, lambda b: (0, 0, 0)), vec, vec],
        out_specs=[tok, pl.BlockSpec((None,) + pw.shape, lambda b: (b, 0, 0, 0)), pvec, pvec],
        compiler_params=_params(), name="pool_bwd",
    )(pv, dyp, pw, ps, onp)


def _pick_lane(tile, idx):
    lane = lax.broadcasted_iota(jnp.int32, (1, LANES), 1)
    return jnp.sum(jnp.where(lane == idx, tile, 0.0), axis=1, keepdims=True)


def _pick_row(tile, idx):
    sub = lax.broadcasted_iota(jnp.int32, (tile.shape[0], 1), 0)
    return jnp.sum(jnp.where(sub == idx, tile, 0.0), axis=0, keepdims=True)


def _put_lane(col, idx):
    lane = lax.broadcasted_iota(jnp.int32, (1, LANES), 1)
    return jnp.where(lane == idx, col, 0.0)


def _head_select(e):
    lo = _head_masks()
    return lo if e == 0 else jnp.logical_not(lo)


def _causal(st, shift):
    row = lax.broadcasted_iota(jnp.int32, st.shape, 0)
    col = lax.broadcasted_iota(jnp.int32, st.shape, 1) + shift
    return jnp.where(col >= row, st, NEG)


def _transpose_blocks(a):
    rows, cols = a.shape
    return jnp.concatenate(
        [jnp.concatenate([a[r:r + LANES, c:c + LANES].T for r in range(0, rows, LANES)], axis=1)
         for c in range(0, cols, LANES)], axis=0)


def _stat_rows(ref, head, nsub):
    return jnp.concatenate([_pick_row(ref[a], head) for a in range(nsub)], axis=1)


def _accumulate(ref, value, first):
    @pl.when(first)
    def _():
        ref[...] = value

    @pl.when(jnp.logical_not(first))
    def _():
        ref[...] += value


def _attn_fwd(qa, ka, vb, n_batch, seq, plan=None):
    tq = min(ATT_BLOCK, seq)
    nq, nsub, tk = seq // tq, tq // ATT_SUB, tq
    pairs = vb.shape[1] // LANES

    def body(q_ref, k_ref, v_ref, o_ref, lse_ref, acc_ref):
        i, p = pl.program_id(1), pl.program_id(2)
        row_lo = lax.broadcasted_iota(jnp.int32, (LANES, 1), 0) < HEAD_DIM
        qs = [q_ref[:, e * LANES:(e + 1) * LANES] for e in range(2)]
        acc_ref[...] = jnp.zeros_like(acc_ref)

        def tile(off, stats, diagonal):
            vj = v_ref[pl.ds(off, tk), :]
            new, alphas, pvs = [], [], []
            for e in range(2):
                st = _dot_nt(k_ref[pl.ds(off, tk), e * LANES:(e + 1) * LANES], qs[e])
                if diagonal:
                    st = _causal(st, 0)
                m, l = stats[e]
                m_new = jnp.maximum(m, jnp.max(st, axis=0, keepdims=True))
                alpha = jnp.exp(m - m_new)
                pt = jnp.exp(st - m_new)
                new.append((m_new, alpha * l + jnp.sum(pt, axis=0, keepdims=True)))
                alphas.append(alpha)
                pvs.append(_dot_tn(jnp.where(_head_select(e), vj, jnp.zeros_like(vj)), pt.astype(BF16)))
            acc_ref[...] = acc_ref[...] * jnp.where(row_lo, alphas[0], alphas[1]) + (pvs[0] + pvs[1])
            return tuple(new)

        init = ((jnp.full((1, tq), NEG, F32), jnp.zeros((1, tq), F32)),) * 2
        stats = lax.fori_loop(0, i, lambda j, st: tile(pl.multiple_of(j * tk, tk), st, False), init)
        (m0, l0), (m1, l1) = tile(pl.multiple_of(i * tk, tk), stats, True)
        out_t = acc_ref[...] / jnp.where(row_lo, l0, l1)
        sub = lax.broadcasted_iota(jnp.int32, (8, 1), 0)
        lse0, lse1 = m0 + jnp.log(l0), m1 + jnp.log(l1)
        for a in range(nsub):
            sl = slice(a * ATT_SUB, (a + 1) * ATT_SUB)
            o_ref[sl, :] = out_t[:, sl].T
            rows = jnp.where(sub == 2 * p, lse0[:, sl], 0.0) + jnp.where(sub == 2 * p + 1, lse1[:, sl], 0.0)
            _accumulate(lse_ref.at[a], rows, p == 0)

    return _pallas(
        body, name="attn_fwd", args=[qa, ka, vb],
        out_shape=[jax.ShapeDtypeStruct((n_batch * seq, pairs * LANES), F32),
                   jax.ShapeDtypeStruct((n_batch * seq // ATT_SUB, 8, ATT_SUB), F32)],
        grid=(n_batch, nq, pairs),
        in_specs=[pl.BlockSpec((tq, 2 * LANES), lambda b, i, p: (b * nq + i, p)),
                  pl.BlockSpec((seq, 2 * LANES), lambda b, i, p: (b, p)),
                  pl.BlockSpec((seq, LANES), lambda b, i, p: (b, p))],
        out_specs=[pl.BlockSpec((tq, LANES), lambda b, i, p: (b * nq + i, p)),
                   pl.BlockSpec((nsub, 8, ATT_SUB), lambda b, i, p: (b * nq + i, 0, 0))],
        scratch_shapes=[pltpu.VMEM((LANES, tq), F32)], plan=plan)


def _attn_bwd_q(qa, ka, vb, do, lse, delta, n_batch, seq, plan=None):
    tq = min(ATT_BLOCK, seq)
    nq, nsub, tk = seq // tq, tq // ATT_SUB, tq
    pairs = vb.shape[1] // LANES

    def body(q_ref, k_ref, v_ref, do_ref, lse_ref, dl_ref, dq_ref, dfq_ref, acc0_ref, acc1_ref):
        i, p = pl.program_id(1), pl.program_id(2)
        accs = (acc0_ref, acc1_ref)
        qs = [q_ref[:, e * LANES:(e + 1) * LANES] for e in range(2)]
        dov = do_ref[...]
        ls = [_stat_rows(lse_ref, 2 * p + e, nsub) for e in range(2)]
        dl = [_stat_rows(dl_ref, 2 * p + e, nsub) for e in range(2)]
        for acc in accs:
            acc[...] = jnp.zeros_like(acc)

        def tile(off, diagonal):
            vj = v_ref[pl.ds(off, tk), :]
            for e in range(2):
                kj = k_ref[pl.ds(off, tk), e * LANES:(e + 1) * LANES]
                st = _dot_nt(kj, qs[e])
                if diagonal:
                    st = _causal(st, 0)
                pt = jnp.exp(st - ls[e])
                dpt = _dot_nt(jnp.where(_head_select(e), vj, jnp.zeros_like(vj)), dov)
                accs[e][...] += _dot(_transpose_blocks(kj), (pt * (dpt - dl[e])).astype(BF16))

        def step(j, carry):
            tile(pl.multiple_of(j * tk, tk), False)
            return carry

        lax.fori_loop(0, i, step, 0)
        tile(pl.multiple_of(i * tk, tk), True)
        dq0, dq1 = _transpose_blocks(acc0_ref[...]), _transpose_blocks(acc1_ref[...])
        dq_ref[...] = jnp.where(_head_masks(), dq0, dq1)
        dfq = _put_lane(_pick_lane(dq0, _aug_lane(0)), 2 * p) + _put_lane(_pick_lane(dq1, _aug_lane(1)), 2 * p + 1)
        _accumulate(dfq_ref, dfq, p == 0)

    stat = pl.BlockSpec((nsub, 8, ATT_SUB), lambda b, i, p: (b * nq + i, 0, 0))
    blk = pl.BlockSpec((tq, LANES), lambda b, i, p: (b * nq + i, p))
    return _pallas(
        body, name="attn_bwd_q", args=[qa, ka, vb, do, lse, delta],
        out_shape=[jax.ShapeDtypeStruct((n_batch * seq, pairs * LANES), F32), jax.ShapeDtypeStruct((n_batch * seq, LANES), F32)],
        grid=(n_batch, nq, pairs),
        in_specs=[pl.BlockSpec((tq, 2 * LANES), lambda b, i, p: (b * nq + i, p)),
                  pl.BlockSpec((seq, 2 * LANES), lambda b, i, p: (b, p)),
                  pl.BlockSpec((seq, LANES), lambda b, i, p: (b, p)), blk, stat, stat],
        out_specs=[blk, pl.BlockSpec((tq, LANES), lambda b, i, p: (b * nq + i, 0))],
        scratch_shapes=[pltpu.VMEM((LANES, tq), F32), pltpu.VMEM((LANES, tq), F32)], plan=plan)


def _attn_bwd_kv(qa, ka, vb, do, lse, delta, n_batch, seq, plan=None):
    tkb = min(ATT_BLOCK, seq)
    nk, nsub, tq = seq // tkb, tkb // ATT_SUB, tkb
    n_tiles = seq // ATT_SUB
    pairs = vb.shape[1] // LANES

    def body(q_ref, k_ref, v_ref, do_ref, lse_ref, dl_ref, dk_ref, dv_ref, dfk_ref, dk0_ref, dk1_ref, dva_ref):
        j, p = pl.program_id(1), pl.program_id(2)
        dks = (dk0_ref, dk1_ref)
        ks = [k_ref[:, e * LANES:(e + 1) * LANES] for e in range(2)]
        vj = v_ref[...]
        vs = [jnp.where(_head_select(e), vj, jnp.zeros_like(vj)) for e in range(2)]
        for acc in (dk0_ref, dk1_ref, dva_ref):
            acc[...] = jnp.zeros_like(acc)

        def tile(t, diagonal):
            off = pl.multiple_of(t * tq, tq)
            dov = do_ref[pl.ds(off, tq), :]
            for e in range(2):
                qe = q_ref[pl.ds(off, tq), e * LANES:(e + 1) * LANES]
                st = _dot_nt(ks[e], qe)
                if diagonal:
                    st = _causal(st, 0)
                rows = lambda ref: jnp.concatenate([_pick_row(ref[t * nsub + a], 2 * p + e) for a in range(nsub)], axis=1)
                pt = jnp.exp(st - rows(lse_ref))
                dva_ref[...] += _dot(pt.astype(BF16), jnp.where(_head_select(e), dov, jnp.zeros_like(dov)))
                dst = pt * (_dot_nt(vs[e], dov) - rows(dl_ref))
                dks[e][...] += _dot(dst.astype(BF16), qe)

        def step(t, carry):
            tile(t, False)
            return carry

        lax.fori_loop(j + 1, nk, step, 0)
        tile(j, True)
        dk0, dk1 = dk0_ref[...], dk1_ref[...]
        dk_ref[...] = jnp.where(_head_masks(), dk0, dk1)
        dv_ref[...] = dva_ref[...].astype(BF16)
        dfk = (_put_lane(_pick_lane(dk0, _aug_lane(0) + 3), 2 * p)
               + _put_lane(_pick_lane(dk1, _aug_lane(1) + 3), 2 * p + 1))
        _accumulate(dfk_ref, -dfk, p == 0)

    stat = pl.BlockSpec((n_tiles, 8, ATT_SUB), lambda b, j, p: (b, 0, 0))
    blk = pl.BlockSpec((tkb, LANES), lambda b, j, p: (b * nk + j, p))
    acc = pltpu.VMEM((tkb, LANES), F32)
    return _pallas(
        body, name="attn_bwd_kv", args=[qa, ka, vb, do, lse, delta],
        out_shape=[jax.ShapeDtypeStruct((n_batch * seq, pairs * LANES), F32),
                   jax.ShapeDtypeStruct((n_batch * seq, pairs * LANES), BF16),
                   jax.ShapeDtypeStruct((n_batch * seq, LANES), F32)],
        grid=(n_batch, nk, pairs),
        in_specs=[pl.BlockSpec((seq, 2 * LANES), lambda b, j, p: (b, p)),
                  pl.BlockSpec((tkb, 2 * LANES), lambda b, j, p: (b * nk + j, p)), blk,
                  pl.BlockSpec((seq, LANES), lambda b, j, p: (b, p)), stat, stat],
        out_specs=[blk, blk, pl.BlockSpec((tkb, LANES), lambda b, j, p: (b * nk + j, 0))],
        scratch_shapes=[acc, acc, acc], plan=plan)


def _forget_bwd(dfq, dfk, f, bias, n_batch, seq):
    def body(dfq_ref, dfk_ref, f_ref, b_ref, df_ref, db_ref):
        acc = dfq_ref[...] + dfk_ref[...]
        row = lax.broadcasted_iota(jnp.int32, (seq, 1), 0)
        dist = 1
        while dist < seq:
            acc = acc + _shift_up(acc, dist, row, seq)
            dist *= 2
        df = acc * _sigmoid(-(f_ref[...] + b_ref[...]))
        df_ref[...] = df
        db_ref[...] = jnp.sum(df, axis=0, keepdims=True)

    col = pl.BlockSpec((seq, LANES), lambda b: (b, 0))
    return pl.pallas_call(
        body,
        out_shape=[jax.ShapeDtypeStruct((n_batch * seq, LANES), F32), jax.ShapeDtypeStruct((n_batch, 1, LANES), F32)],
        grid=(n_batch,), in_specs=[col, col, col, pl.BlockSpec((1, LANES), lambda b: (0, 0))],
        out_specs=[col, pl.BlockSpec((None, 1, LANES), lambda b: (b, 0, 0))],
        compiler_params=_params(), name="forget_bwd",
    )(dfq, dfk, f, bias)


def _mix_out(x1, yp, o, ona, woa, wob):
    t, d = x1.shape
    width = o.shape[1]
    tm = min(512, t)

    def body(x_ref, yp_ref, o_ref, on_ref, wa_ref, wb_ref, x2_ref, ya_ref):
        of = o_ref[...]
        ya = ((of * _rms(of)) * on_ref[...]).astype(BF16)
        ya_ref[...] = ya
        x2_ref[...] = x_ref[...] + (_dot(yp_ref[...], wa_ref[...]) + _dot(ya, wb_ref[...]))

    row = pl.BlockSpec((tm, d), lambda i: (i, 0))
    half = pl.BlockSpec((tm, width), lambda i: (i, 0))
    wspec = pl.BlockSpec((width, d), lambda i: (0, 0))
    return pl.pallas_call(
        body, out_shape=[jax.ShapeDtypeStruct((t, d), F32), jax.ShapeDtypeStruct((t, width), BF16)],
        grid=(t // tm,), in_specs=[row, half, half, pl.BlockSpec((1, width), lambda i: (0, 0)), wspec, wspec],
        out_specs=[row, half], compiler_params=_params(), name="mix_out",
    )(x1, yp, o, ona, woa, wob)


def _mix_out_bwd(dx2, o, yp, ya, ona, woa, wob, plan=None):
    t, d = dx2.shape
    width = o.shape[1]
    tm = min(512, t)
    nt = t // tm

    def body(dx_ref, o_ref, yp_ref, ya_ref, on_ref, wa_ref, wb_ref, dyp_ref, do_ref, dl_ref, dwa_ref, dwb_ref, don_ref):
        @pl.when(pl.program_id(0) == 0)
        def _():
            dwa_ref[...] = jnp.zeros_like(dwa_ref)
            dwb_ref[...] = jnp.zeros_like(dwb_ref)

        dxb = dx_ref[...].astype(BF16)
        dwa_ref[...] += _dot_tn(yp_ref[...], dxb)
        dwb_ref[...] += _dot_tn(ya_ref[...], dxb)
        dyp_ref[...] = _dot_nt(dxb, wa_ref[...])
        of = o_ref[...]
        dov, dgr = _rms_bwd(of, _rms(of), on_ref[...], _dot_nt(dxb, wb_ref[...]))
        don_ref[...] = jnp.sum(dgr, axis=0, keepdims=True)
        do_ref[...] = dov.astype(BF16)
        lo = _head_masks()
        prod = dov * of
        delta = jnp.zeros((tm, LANES), F32)
        for blk in range(width // LANES):
            pb = prod[:, blk * LANES:(blk + 1) * LANES]
            delta = delta + _put_lane(jnp.sum(jnp.where(lo, pb, 0.0), axis=1, keepdims=True), 2 * blk)
            delta = delta + _put_lane(jnp.sum(jnp.where(lo, 0.0, pb), axis=1, keepdims=True), 2 * blk + 1)
        for c in range(tm // ATT_SUB):
            dl_ref[c] = delta[c * ATT_SUB:(c + 1) * ATT_SUB, :].T[0:8, :]

    row = pl.BlockSpec((tm, d), lambda i: (i, 0))
    half = pl.BlockSpec((tm, width), lambda i: (i, 0))
    wspec = pl.BlockSpec((width, d), lambda i: (0, 0))
    return _pallas(
        body, name="mix_out_bwd", args=[dx2, o, yp, ya, ona, woa, wob],
        out_shape=[jax.ShapeDtypeStruct((t, width), F32), jax.ShapeDtypeStruct((t, width), BF16),
                   jax.ShapeDtypeStruct((t // ATT_SUB, 8, ATT_SUB), F32), jax.ShapeDtypeStruct((width, d), F32),
                   jax.ShapeDtypeStruct((width, d), F32), jax.ShapeDtypeStruct((nt, 1, width), F32)],
        grid=(nt,),
        in_specs=[row, half, half, half, pl.BlockSpec((1, width), lambda i: (0, 0)), wspec, wspec],
        out_specs=[half, half, pl.BlockSpec((tm // ATT_SUB, 8, ATT_SUB), lambda i: (i, 0, 0)), wspec, wspec,
                   pl.BlockSpec((None, 1, width), lambda i: (i, 0, 0))], plan=plan)


def _mix_in_bwd(dx2, x1, gain, hm, dpv, dqh, q, dkh, k, dv, df, qn, kn, wt):
    t, d = x1.shape
    width = q.shape[1]
    pool_width = dpv.shape[1]
    tm = min(512, t)
    nt = t // tm
    scale = HEAD_DIM ** -0.5
    c_q, c_k, c_v = pool_width, pool_width + width, pool_width + 2 * width
    c_f = c_v + width

    def body(dx2_ref, x_ref, g_ref, hm_ref, dpv_ref, dqh_ref, q_ref, dkh_ref, k_ref, dv_ref, df_ref, qn_ref, kn_ref,
             wt_ref, dx_ref, dwt_ref, dg_ref, dqn_ref, dkn_ref):
        @pl.when(pl.program_id(0) == 0)
        def _():
            dwt_ref[...] = jnp.zeros_like(dwt_ref)

        lo = _head_masks()
        for part, rows in enumerate(_row_halves(tm)):
            def put(ref, sl, value):
                ref[:, sl] = value if part == 0 else ref[:, sl] + value

            hm = hm_ref[rows, :]
            pieces = [(0, dpv_ref[rows, :])]
            for c0, raw_ref, dh_ref, n_ref, dn_ref, mul in ((c_q, q_ref, dqh_ref, qn_ref, dqn_ref, scale),
                                                           (c_k, k_ref, dkh_ref, kn_ref, dkn_ref, 1.0)):
                cols = []
                for blk in range(width // LANES):
                    sl = slice(blk * LANES, (blk + 1) * LANES)
                    xb = raw_ref[rows, sl]
                    gb = dh_ref[rows, sl] * mul
                    r = _head_rms(xb, lo)
                    xh = xb * r
                    dyg = gb * n_ref[:, sl]
                    cols.append((r * (dyg - xh * _head_mean(dyg * xh, lo))).astype(BF16))
                    put(dn_ref, sl, jnp.sum(gb * xh, axis=0, keepdims=True))
                pieces.append((c0, jnp.concatenate(cols, axis=1)))
            pieces.append((c_v, dv_ref[rows, :]))
            pieces.append((c_f, df_ref[rows, :].astype(BF16)))
            dhm = jnp.zeros((tm // 2, d), F32)
            for c0, piece in pieces:
                dwt_ref[c0:c0 + piece.shape[1], :] += _dot_tn(piece, hm)
                dhm = dhm + _dot(piece, wt_ref[c0:c0 + piece.shape[1], :])
            xf = x_ref[rows, :]
            dxn, dgr = _rms_bwd(xf, _rms(xf), g_ref[...], dhm)
            dx_ref[rows, :] = dx2_ref[rows, :] + dxn
            put(dg_ref, slice(None), jnp.sum(dgr, axis=0, keepdims=True))

    row = pl.BlockSpec((tm, d), lambda i: (i, 0))
    half = pl.BlockSpec((tm, width), lambda i: (i, 0))
    const = lambda shape: pl.BlockSpec(shape, lambda i: (0, 0))
    pvec = lambda n: pl.BlockSpec((None, 1, n), lambda i: (i, 0, 0))
    return pl.pallas_call(
        body,
        out_shape=[jax.ShapeDtypeStruct((t, d), F32), jax.ShapeDtypeStruct(wt.shape, F32),
                   jax.ShapeDtypeStruct((nt, 1, d), F32),
                   jax.ShapeDtypeStruct((nt, 1, width), F32), jax.ShapeDtypeStruct((nt, 1, width), F32)],
        grid=(nt,),
        in_specs=[row, row, const((1, d)), row, pl.BlockSpec((tm, pool_width), lambda i: (i, 0)), half, half, half, half,
                  half, pl.BlockSpec((tm, LANES), lambda i: (i, 0)), const((1, width)), const((1, width)),
                  const(wt.shape)],
        out_specs=[row, const(wt.shape), pvec(d), pvec(width), pvec(width)],
        compiler_params=_params(), name="mix_in_bwd",
    )(dx2, x1, gain, hm, dpv, dqh, q, dkh, k, dv, df, qn, kn, wt)


def _mesh_pos():
    return lax.axis_index("x"), lax.axis_index("y"), lax.axis_index("c")


def _other_chips(x, y):
    return [(1 - x, y), (x, 1 - y), (1 - x, 1 - y)]


def _remote(src, dst, send_sem, recv_sem, device):
    return pltpu.make_async_remote_copy(src_ref=src, dst_ref=dst, send_sem=send_sem, recv_sem=recv_sem,
                                        device_id=device, device_id_type=pl.DeviceIdType.MESH)


def _half_rows(n_rows, which):
    half = n_rows // 2
    return pl.ds(pl.multiple_of(which * half, 8), half)


def _row_block(rows, cols, itemsize=4):
    rb = rows
    while rb * cols * itemsize > (1 << 20) and rb % 32 == 0:
        rb //= 2
    return rb


def _place_cast(ws, chip, tag):
    n = len(ws)
    rows, cols = ws[0].shape
    rb = _row_block(rows, cols)

    def body(k_ref, *refs):
        for w_ref, o_ref in zip(refs[:n], refs[n:]):
            o_ref[...] = w_ref[...].astype(BF16)

    return pl.pallas_call(
        body, out_shape=[jax.ShapeDtypeStruct((N_CHIPS, rows, cols), BF16)] * n,
        grid_spec=pltpu.PrefetchScalarGridSpec(
            num_scalar_prefetch=1, grid=(rows // rb,),
            in_specs=[pl.BlockSpec((rb, cols), lambda i, k: (i, 0))] * n,
            out_specs=[pl.BlockSpec((None, rb, cols), lambda i, k: (k[0], i, 0))] * n),
        compiler_params=_params(), name="place_" + tag,
    )(chip, *ws)


class _Plan:
    def __init__(self, ins, outs, alias, sems, start, finish):
        self.ins, self.outs, self.alias, self.sems, self.start, self.finish = ins, outs, alias, sems, start, finish


def _merge_plans(a, b):
    ni, no, ns = len(a.ins), len(a.outs), len(a.sems)
    alias = dict(a.alias)
    alias.update({ni + i: no + o for i, o in b.alias.items()})

    def both(which):
        def run(ins, outs, sems):
            getattr(a, which)(ins[:ni], outs[:no], sems[:ns])
            getattr(b, which)(ins[ni:], outs[no:], sems[ns:])
        return run

    return _Plan(list(a.ins) + list(b.ins), list(a.outs) + list(b.outs), alias, list(a.sems) + list(b.sems),
                 both("start"), both("finish"))


def _run_plan(plan, name):
    n_in, n_out = len(plan.ins), len(plan.outs)

    def body(*refs):
        parts = refs[:n_in], refs[n_in:n_in + n_out], refs[n_in + n_out:]
        plan.start(*parts)
        plan.finish(*parts)

    return pl.pallas_call(
        body, out_shape=plan.outs, in_specs=[ANY] * n_in, out_specs=[ANY] * n_out, scratch_shapes=plan.sems,
        input_output_aliases=plan.alias, name=name,
    )(*plan.ins)


def _pallas(body, *, name, args, in_specs, out_shape, out_specs, grid, scratch_shapes=(), plan=None, aliases=None):
    n_in, n_out, n_scr = len(args), len(out_shape), len(scratch_shapes)
    aliases = dict(aliases or {})
    if plan is None:
        res = pl.pallas_call(body, out_shape=out_shape, grid=grid, in_specs=in_specs, out_specs=out_specs,
                             scratch_shapes=scratch_shapes, input_output_aliases=aliases,
                             compiler_params=_params(), name=name)(*args)
        return list(res), []
    p_in, p_out = len(plan.ins), len(plan.outs)

    def carrying(*refs):
        ins, p_ins = refs[:n_in], refs[n_in:n_in + p_in]
        o0 = n_in + p_in
        outs, p_outs = refs[o0:o0 + n_out], refs[o0 + n_out:o0 + n_out + p_out]
        s0 = o0 + n_out + p_out
        scr, p_sems = refs[s0:s0 + n_scr], refs[s0 + n_scr:]
        ids = [pl.program_id(a) for a in range(len(grid))]
        first = functools.reduce(jnp.logical_and, [i == 0 for i in ids])
        last = functools.reduce(jnp.logical_and, [i == g - 1 for i, g in zip(ids, grid)])

        @pl.when(first)
        def _():
            plan.start(p_ins, p_outs, p_sems)

        body(*ins, *outs, *scr)

        @pl.when(last)
        def _():
            plan.finish(p_ins, p_outs, p_sems)

    res = pl.pallas_call(
        carrying, out_shape=list(out_shape) + list(plan.outs), grid=grid,
        in_specs=list(in_specs) + [ANY] * p_in, out_specs=list(out_specs) + [ANY] * p_out,
        scratch_shapes=list(scratch_shapes) + list(plan.sems),
        input_output_aliases={**aliases, **{n_in + i: n_out + o for i, o in plan.alias.items()}},
        compiler_params=_params(), name=name,
    )(*args, *plan.ins)
    return list(res[:n_out]), list(res[n_out:])


def _plan_gather(stacks):
    n = len(stacks)

    def ici_copies(outs, sems):
        x, y, c = _mesh_pos()
        cps = []
        for w in range(n):
            own = outs[w].at[2 * x + y, _half_rows(stacks[w].shape[1], c)]
            cps += [_remote(own, own, sems[0].at[w, j], sems[1].at[w, j], (*chip, c)) for j, chip in enumerate(_other_chips(x, y))]
        return cps

    def start(ins, outs, sems):
        for cp in ici_copies(outs, sems):
            cp.start()

    def finish(ins, outs, sems):
        ici_send, ici_recv, d2d_send, d2d_recv = sems
        x, y, c = _mesh_pos()
        sibling = (x, y, 1 - c)
        slots = [2 * cx + cy for cx, cy in _other_chips(x, y)]
        forwards = []
        for w in range(n):
            rows = _half_rows(stacks[w].shape[1], c)
            for j in range(3):
                landed = outs[w].at[slots[j], rows]
                _remote(landed, landed, ici_send.at[w, j], ici_recv.at[w, j], sibling).wait_recv()
                cp = _remote(landed, landed, d2d_send.at[w, j], d2d_recv.at[w, j], sibling)
                cp.start()
                forwards.append(cp)
        for w in range(n):
            rows = _half_rows(stacks[w].shape[1], 1 - c)
            for j in range(3):
                landed = outs[w].at[slots[j], rows]
                _remote(landed, landed, d2d_send.at[w, j], d2d_recv.at[w, j], sibling).wait_recv()
        for cp in ici_copies(outs, sems) + forwards:
            cp.wait_send()

    return _Plan(stacks, [jax.ShapeDtypeStruct(s.shape, s.dtype) for s in stacks], {w: w for w in range(n)},
                 [pltpu.SemaphoreType.DMA((n, 3))] * 4, start, finish)


def _plan_sibling_halves(gs):
    n = len(gs)

    def copies(ins, outs, sems):
        x, y, c = _mesh_pos()
        return [_remote(ins[w].at[:, _half_rows(gs[w].shape[1], 1 - c), :], outs[w], sems[0].at[w], sems[1].at[w],
                        (x, y, 1 - c)) for w in range(n)]

    def start(ins, outs, sems):
        for cp in copies(ins, outs, sems):
            cp.start()

    def finish(ins, outs, sems):
        for cp in copies(ins, outs, sems):
            cp.wait()

    return _Plan(gs, [jax.ShapeDtypeStruct((g.shape[0], g.shape[1] // 2, g.shape[2]), g.dtype) for g in gs], {},
                 [pltpu.SemaphoreType.DMA((n,))] * 2, start, finish)


def _plan_chip_exchange(ps):
    n = len(ps)

    def copies(ins, outs, sems):
        x, y, c = _mesh_pos()
        return [_remote(ins[w].at[2 * cx + cy], outs[w].at[j], sems[0].at[w, j], sems[1].at[w, j], (cx, cy, c))
                for w in range(n) for j, (cx, cy) in enumerate(_other_chips(x, y))]

    def start(ins, outs, sems):
        for cp in copies(ins, outs, sems):
            cp.start()

    def finish(ins, outs, sems):
        for cp in copies(ins, outs, sems):
            cp.wait()

    return _Plan(ps, [jax.ShapeDtypeStruct((3,) + p.shape[1:], p.dtype) for p in ps], {},
                 [pltpu.SemaphoreType.DMA((n, 3))] * 2, start, finish)


def _plan_sibling_share(gs):
    n = len(gs)

    def copies(outs, sems, which):
        x, y, c = _mesh_pos()
        cps = []
        for w in range(n):
            rows = outs[w].at[_half_rows(gs[w].shape[0], c if which == "mine" else 1 - c)]
            cps.append(_remote(rows, rows, sems[0].at[w], sems[1].at[w], (x, y, 1 - c)))
        return cps

    def start(ins, outs, sems):
        for cp in copies(outs, sems, "mine"):
            cp.start()

    def finish(ins, outs, sems):
        for cp in copies(outs, sems, "mine"):
            cp.wait_send()
        for cp in copies(outs, sems, "theirs"):
            cp.wait_recv()

    return _Plan(gs, [jax.ShapeDtypeStruct(g.shape, g.dtype) for g in gs], {w: w for w in range(n)},
                 [pltpu.SemaphoreType.DMA((n,))] * 2, start, finish)


def _same_shape_groups(arrays):
    groups = {}
    for i, a in enumerate(arrays):
        groups.setdefault(a.shape, []).append(i)
    return list(groups.values())


def _add_sibling(gs, r1s, ids, tag):
    n = len(gs)
    nch, rh, cols = r1s[0].shape

    def body(ids_ref, *refs):
        for g_ref, r_ref, o_ref in zip(refs[:n], refs[n:2 * n], refs[2 * n:]):
            o_ref[...] = (g_ref[...] + r_ref[...]).astype(BF16)

    blk = lambda fn: pl.BlockSpec((None, rh, cols), fn)
    return pl.pallas_call(
        body, out_shape=[jax.ShapeDtypeStruct(r1s[0].shape, BF16)] * n,
        grid_spec=pltpu.PrefetchScalarGridSpec(
            num_scalar_prefetch=1, grid=(nch,),
            in_specs=[blk(lambda k, ids: (k, ids[1], 0))] * n + [blk(lambda k, ids: (k, 0, 0))] * n,
            out_specs=[blk(lambda k, ids: (k, 0, 0))] * n),
        compiler_params=_params(), name="add_sibling_" + tag,
    )(ids, *gs, *r1s)


def _add_chips(gs, r1s, r2s, ids, tag):
    n = len(gs)
    _, rh, cols = r1s[0].shape
    nb = 2 if rh % 32 == 0 else 1
    rb = rh // nb

    def body(ids_ref, *refs):
        for g_ref, r1_ref, r2_ref, o_ref in zip(refs[:n], refs[n:2 * n], refs[2 * n:3 * n], refs[3 * n:]):
            own = g_ref[...] + r1_ref[...]
            o_ref[...] = ((own + r2_ref[0].astype(F32)) + r2_ref[1].astype(F32)) + r2_ref[2].astype(F32)

    return pl.pallas_call(
        body, out_shape=[jax.ShapeDtypeStruct((2 * rh, cols), F32)] * n,
        grid_spec=pltpu.PrefetchScalarGridSpec(
            num_scalar_prefetch=1, grid=(nb,),
            in_specs=[pl.BlockSpec((None, rb, cols), lambda i, ids: (ids[0], ids[1] * nb + i, 0))] * n
            + [pl.BlockSpec((None, rb, cols), lambda i, ids: (ids[0], i, 0))] * n
            + [pl.BlockSpec((3, rb, cols), lambda i, ids: (0, i, 0))] * n,
            out_specs=[pl.BlockSpec((rb, cols), lambda i, ids: (ids[1] * nb + i, 0))] * n),
        compiler_params=_params(), name="add_chips_" + tag,
    )(ids, *gs, *r1s, *r2s)


VEC_ROWS = 8


N_DEVICES = 8


def _small_pack(part, d, width):
    names = ("ffn1_norm", "mix_norm", "ffn2_norm", "pool_scale", "out_norm_pool", "out_norm_attn", "qn", "kn", "b_forget",
             "pool_w", "loss")
    args = [part[k] for k in names]
    pw_shape = part["pool_w"].shape[1:]

    def body(g1_ref, gm_ref, g2_ref, ps_ref, onp_ref, ona_ref, qn_ref, kn_ref, bf_ref, pw_ref, loss_ref, vbuf, pbuf):
        lo = _head_masks()

        def fold_heads(ref):
            v = jnp.sum(ref[...], axis=0)
            acc = jnp.zeros((VEC_ROWS, LANES), F32)
            for blk in range(width // LANES):
                vb = jnp.broadcast_to(v[:, blk * LANES:(blk + 1) * LANES], (VEC_ROWS, LANES))
                acc = acc + vb + pltpu.roll(vb, HEAD_DIM, 1)
            return jnp.where(lo, acc, 0.0)[0:1, :]

        vbuf[0] = jnp.zeros((VEC_ROWS, d), F32)
        vbuf[0, 0:1, :] = jnp.sum(g1_ref[...], axis=0)
        vbuf[0, 1:2, :] = jnp.sum(gm_ref[...], axis=0)
        vbuf[0, 2:3, :] = jnp.sum(g2_ref[...], axis=0)
        vbuf[0, 5:6, 0:LANES] = jnp.sum(loss_ref[...], axis=0)[0:1, :]
        vbuf[0, 3:4, 0:width] = jnp.sum(ps_ref[...], axis=0)
        vbuf[0, 3:4, width:2 * width] = jnp.sum(onp_ref[...], axis=0)
        vbuf[0, 4:5, 0:width] = jnp.sum(ona_ref[...], axis=0)
        vbuf[0, 4:5, width:width + LANES] = fold_heads(qn_ref)
        vbuf[0, 4:5, width + LANES:width + 2 * LANES] = fold_heads(kn_ref)
        vbuf[0, 4:5, width + 2 * LANES:width + 3 * LANES] = jnp.sum(bf_ref[...], axis=0)
        pbuf[0] = jnp.sum(pw_ref[...], axis=0)

    return pl.pallas_call(
        body, out_shape=[jax.ShapeDtypeStruct((N_DEVICES, VEC_ROWS, d), F32), jax.ShapeDtypeStruct((N_DEVICES,) + pw_shape, F32)],
        in_specs=[VM] * len(args), out_specs=[VM, VM], compiler_params=_params(), name="small_pack",
    )(*args)


def _plan_all_to_all(stacks):
    n = len(stacks)

    def copies(outs, sems):
        x, y, c = _mesh_pos()
        cps = []
        for r in range(1, N_DEVICES):
            peer = (x if not r & 4 else 1 - x, y if not r & 2 else 1 - y, c if not r & 1 else 1 - c)
            cps += [_remote(outs[w].at[0], outs[w].at[r], sems[0].at[w, r - 1], sems[1].at[w, r - 1], peer) for w in range(n)]
        return cps

    def start(ins, outs, sems):
        for cp in copies(outs, sems):
            cp.start()

    def finish(ins, outs, sems):
        for cp in copies(outs, sems):
            cp.wait()

    return _Plan(stacks, [jax.ShapeDtypeStruct(s.shape, s.dtype) for s in stacks], {w: w for w in range(n)},
                 [pltpu.SemaphoreType.DMA((n, N_DEVICES - 1))] * 2, start, finish)


def _small_sum(vstack, pstack, me):
    def body(me_ref, vbuf, pbuf, vec_ref, pw_ref):
        vec = vbuf[me_ref[0]]
        pw = pbuf[me_ref[0]]
        for dev in range(1, N_DEVICES):
            vec = vec + vbuf[jnp.bitwise_xor(me_ref[0], dev)]
            pw = pw + pbuf[jnp.bitwise_xor(me_ref[0], dev)]
        vec_ref[...] = vec
        pw_ref[...] = pw

    full = lambda s: pl.BlockSpec(s.shape, lambda i, me: (0,) * len(s.shape))
    outs = [jax.ShapeDtypeStruct(vstack.shape[1:], F32), jax.ShapeDtypeStruct(pstack.shape[1:], F32)]
    return pl.pallas_call(
        body, out_shape=outs,
        grid_spec=pltpu.PrefetchScalarGridSpec(num_scalar_prefetch=1, grid=(1,), in_specs=[full(vstack), full(pstack)],
                                               out_specs=[full(o) for o in outs]),
        compiler_params=_params(), name="small_sum",
    )(me, vstack, pstack)


def _adamw(ws, gs, ms, vs, tag):
    n = len(ws)
    rows, cols = ws[0].shape
    rb = rows
    while rb * cols * 4 * n > (1 << 20) and rb % 16 == 0:
        rb //= 2

    def body(*refs):
        for j in range(n):
            w_ref, g_ref, m_ref, v_ref = (refs[k * n + j] for k in range(4))
            d_ref, mo_ref, vo_ref = (refs[(4 + k) * n + j] for k in range(3))
            gv = g_ref[...]
            m2 = ADAM_B1 * m_ref[...] + (1.0 - ADAM_B1) * gv
            v2 = ADAM_B2 * v_ref[...] + (1.0 - ADAM_B2) * (gv * gv)
            m_hat = m2 / (1.0 - ADAM_B1 ** ADAM_STEP)
            v_hat = v2 / (1.0 - ADAM_B2 ** ADAM_STEP)
            d_ref[...] = -ADAM_LR * (m_hat / (jnp.sqrt(v_hat) + ADAM_EPS) + ADAM_WD * w_ref[...])
            mo_ref[...] = m2
            vo_ref[...] = v2

    spec = pl.BlockSpec((rb, cols), lambda i: (i, 0))
    res = pl.pallas_call(
        body, out_shape=[jax.ShapeDtypeStruct(ws[0].shape, F32)] * (3 * n), grid=(rows // rb,),
        in_specs=[spec] * (4 * n), out_specs=[spec] * (3 * n), compiler_params=_params(), name="adamw_" + tag,
    )(*ws, *gs, *ms, *vs)
    return [(res[j], res[n + j], res[2 * n + j]) for j in range(n)]


def _pack_vec(p, d, width):
    pad = lambda v: jnp.pad(v, (0, LANES - v.shape[0]))
    row3 = jnp.concatenate([p["pool_scale"], p["out_norm_pool"]])
    row4 = jnp.concatenate([p["out_norm_attn"], pad(p["q_norm"]), pad(p["k_norm"]), pad(p["b_forget"]),
                            jnp.zeros((d - width - 3 * LANES,), F32)])
    rows = [p["ffn1_norm"], p["mix_norm"], p["ffn2_norm"], row3, row4]
    return jnp.pad(jnp.stack(rows), ((0, VEC_ROWS - len(rows)), (0, 0)))


def _unpack_vec(vec, width):
    return dict(ffn1_norm=vec[0], mix_norm=vec[1], ffn2_norm=vec[2], pool_scale=vec[3, :width],
                out_norm_pool=vec[3, width:2 * width], out_norm_attn=vec[4, :width],
                q_norm=vec[4, width:width + HEAD_DIM], k_norm=vec[4, width + LANES:width + LANES + HEAD_DIM],
                b_forget=vec[4, width + 2 * LANES:width + 2 * LANES + N_HEADS])


WEIGHT_NAMES = ("ffn1_norm", "ffn1_w_gate", "ffn1_w_up", "ffn1_w_down", "mix_norm", "w_in", "b_forget", "pool_w",
                "pool_scale", "q_norm", "k_norm", "out_norm_pool", "out_norm_attn", "w_out", "ffn2_norm",
                "ffn2_w_gate", "ffn2_w_up", "ffn2_w_down")
BIG_NAMES = ("ffn1_w_gate", "ffn1_w_up", "ffn1_w_down", "w_in", "w_out", "ffn2_w_gate", "ffn2_w_up", "ffn2_w_down")
TRANSPOSED_NAMES = ("ffn1_w_gate", "ffn1_w_up", "w_in", "ffn2_w_gate", "ffn2_w_up")
FFN1_NAMES = ("ffn1_w_gate", "ffn1_w_up", "ffn1_w_down")
MIX_NAMES = ("w_in", "w_out")
FFN2_NAMES = ("ffn2_w_gate", "ffn2_w_up", "ffn2_w_down")


def kernel(x, ffn1_norm, ffn1_w_gate, ffn1_w_up, ffn1_w_down, mix_norm, w_in, b_forget, pool_w, pool_scale, q_norm, k_norm, out_norm_pool, out_norm_attn, w_out, ffn2_norm, ffn2_w_gate, ffn2_w_up, ffn2_w_down, loss_target, m_ffn1_norm, m_ffn1_w_gate, m_ffn1_w_up, m_ffn1_w_down, m_mix_norm, m_w_in, m_b_forget, m_pool_w, m_pool_scale, m_q_norm, m_k_norm, m_out_norm_pool, m_out_norm_attn, m_w_out, m_ffn2_norm, m_ffn2_w_gate, m_ffn2_w_up, m_ffn2_w_down, v_ffn1_norm, v_ffn1_w_gate, v_ffn1_w_up, v_ffn1_w_down, v_mix_norm, v_w_in, v_b_forget, v_pool_w, v_pool_scale, v_q_norm, v_k_norm, v_out_norm_pool, v_out_norm_attn, v_w_out, v_ffn2_norm, v_ffn2_w_gate, v_ffn2_w_up, v_ffn2_w_down):
    given = dict(locals())
    w = {n: given[n] for n in WEIGHT_NAMES}
    m = {n: given["m_" + n] for n in WEIGHT_NAMES}
    v = {n: given["v_" + n] for n in WEIGHT_NAMES}
    n_batch, seq, d = x.shape
    width = pool_scale.shape[0]
    in_rows = w_in.shape[1]
    in_cols = N_CHIPS * in_rows
    in_pad = -(-in_rows // 32) * 32
    in_cols_pad = in_cols - N_HEADS + LANES

    work = lambda a, n: a.T if n in TRANSPOSED_NAMES else a
    exchanged = lambda a, n: jnp.pad(a, ((0, in_pad - in_rows), (0, 0))) if n == "w_in" else a

    mesh_x, mesh_y, mesh_c = _mesh_pos()
    ids = jnp.stack([2 * mesh_x + mesh_y, mesh_c]).astype(jnp.int32)

    row = lambda a: a.reshape(1, -1)
    g1, gm, g2, ps, onp, ona = (row(a) for a in (ffn1_norm, mix_norm, ffn2_norm, pool_scale, out_norm_pool, out_norm_attn))
    qn, kn = row(jnp.tile(q_norm, N_HEADS)), row(jnp.tile(k_norm, N_HEADS))
    bf = row(jnp.pad(b_forget, (0, LANES - N_HEADS)))
    pwb = pool_w.astype(BF16)
    xf, tgt = x.reshape(n_batch * seq, d), loss_target.reshape(n_batch * seq, d)

    def grouped(call, names, *lists):
        out = [None] * len(names)
        for idx in _same_shape_groups(lists[0]):
            res = call(*[[lst[i] for i in idx] for lst in lists], names[idx[0]])
            for i, r in zip(idx, res):
                out[i] = r
        return out

    placed = dict(zip(BIG_NAMES, grouped(lambda ws, tag: _place_cast(ws, ids, tag), BIG_NAMES,
                                         [exchanged(work(w[n], n), n) for n in BIG_NAMES])))
    wg1, wu1, wd1 = _run_plan(_plan_gather([placed[n] for n in FFN1_NAMES]), "gather_ffn1")
    (x1, h1, a1, b1, s1), (w_in_all, w_out_all, wd2) = _ffn_fwd(
        xf, g1, wg1, wu1, wd1, plan=_plan_gather([placed[n] for n in MIX_NAMES + FFN2_NAMES[2:]]))
    w_in_t = jnp.pad(w_in_all[:, :in_rows].reshape(in_cols, d), ((0, in_cols_pad - in_cols), (0, 0)))
    w_out_full = w_out_all.reshape(N_CHIPS * w_out.shape[0], d)
    woa, wob = w_out_full[:width], w_out_full[width:]

    (hm, pv, q, k, qh, kh, vb, f), (wu2,) = _mix_proj(x1, gm, w_in_t, qn, kn, width, width,
                                                      plan=_plan_gather([placed["ffn2_w_up"]]))
    qa, ka = _forget_prefix(f, bf, qh, kh, n_batch, seq)
    yp = _pool_fwd(pv, pwb, ps, onp, n_batch, seq)
    (o, lse), (wg2,) = _attn_fwd(qa, ka, vb, n_batch, seq, plan=_plan_gather([placed["ffn2_w_gate"]]))
    x2, ya = _mix_out(x1, yp, o, ona, woa, wob)
    (dy, h2, a2, b2, s2, lpart), _ = _ffn_fwd(x2, g2, wg2, wu2, wd2, target=tgt)

    def to_chips(gs, arrived, tags):
        return grouped(lambda g, r, tag: _add_sibling(g, r, ids, tag), tags, gs, arrived)

    def own_rows(gs, from_sibling, from_chips, tags):
        return grouped(lambda g, ra, rb, tag: _add_chips(g, ra, rb, ids, tag), tags, gs, from_sibling, from_chips)

    (dx2, da2, db2, dg2), _ = _ffn_bwd_x(dy, x2, g2, a2, b2, wg2, wu2, wd2, "ffn2_bwd_x")
    dw2, _ = _ffn_bwd_w([(da2, h2, 1.0), (db2, h2, 1.0), (s2, dy, 0.5)], "ffn2_bwd_w")
    (dyp, do, delta, dwoa, dwob, dona), sib2 = _mix_out_bwd(dx2, o, yp, ya, ona, woa, wob, plan=_plan_sibling_halves(dw2))
    dpv, dpw, dps, donp = _pool_bwd(pv, dyp, pwb, ps, onp, n_batch, seq)
    (dqh, dfq), chips2 = _attn_bwd_q(qa, ka, vb, do, lse, delta, n_batch, seq,
                                     plan=_plan_chip_exchange(to_chips(dw2, sib2, FFN2_NAMES)))
    (dkh, dv, dfk), red2 = _attn_bwd_kv(qa, ka, vb, do, lse, delta, n_batch, seq,
                                        plan=_plan_sibling_share(own_rows(dw2, sib2, chips2, FFN2_NAMES)))
    df, dbf = _forget_bwd(dfq, dfk, f, bf, n_batch, seq)
    dx1, dw_in_t, dgm, dqn, dkn = _mix_in_bwd(dx2, x1, gm, hm, dpv, dqh, q, dkh, k, dv, df, qn, kn, w_in_t)
    in_base = [in_rows * k // 8 * 8 for k in range(N_CHIPS)]
    d_w_in = jnp.stack([dw_in_t[b:b + in_pad] for b in in_base])
    d_w_out = jnp.concatenate([dwoa, dwob], axis=0).reshape(N_CHIPS, w_out.shape[0], d)
    dwm = [d_w_in, d_w_out]
    down, gate_up = FFN1_NAMES[2:], FFN1_NAMES[:2]
    dwd1, sibm = _ffn_bwd_w([(s1, dx1, 0.5)], "ffn1_bwd_w_down", plan=_plan_sibling_halves(dwm))
    (da1, db1), arrived = _ffn_bwd_a(dx1, a1, b1, wd1, "ffn1_bwd_a",
                                     plan=_merge_plans(_plan_sibling_halves(dwd1),
                                                       _plan_chip_exchange(to_chips(dwm, sibm, MIX_NAMES))))
    sibd, chipsm = arrived[:1], arrived[1:]
    dwgu1, chipsd = _ffn_bwd_w([(da1, h1, 1.0), (db1, h1, 1.0)], "ffn1_bwd_w_gate_up",
                               plan=_plan_chip_exchange(to_chips(dwd1, sibd, down)))
    n_tiles = (n_batch * seq) // min(FFN_TILE, n_batch * seq)
    first = max(n_tiles // 4, 1)
    begun, sibgu = _ffn_bwd_h(dx1, xf, g1, da1, db1, wg1, wu1, "ffn1_bwd_h_first", (0, first),
                              plan=_plan_sibling_halves(dwgu1))
    (gx, dg1), chipsgu = _ffn_bwd_h(dx1, xf, g1, da1, db1, wg1, wu1, "ffn1_bwd_h_rest", (first, n_tiles), prev=begun,
                                    plan=_plan_chip_exchange(to_chips(dwgu1, sibgu, gate_up)))

    part = dict(ffn1_norm=dg1, mix_norm=dgm, ffn2_norm=dg2, b_forget=dbf, pool_scale=dps, out_norm_pool=donp,
                out_norm_attn=dona, qn=dqn, kn=dkn, pool_w=dpw.reshape(n_batch, -1, pool_w.shape[-1]), loss=lpart)
    mine = (own_rows(dwgu1, sibgu, chipsgu, gate_up) + own_rows(dwd1, sibd, chipsd, down)
            + own_rows(dwm, sibm, chipsm, MIX_NAMES))
    last = _run_plan(_merge_plans(_plan_sibling_share(mine), _plan_all_to_all(_small_pack(part, d, width))), "last_exchange")
    vstack, pstack = last[len(mine):]
    g_vec, g_pw = _small_sum(vstack, pstack, jnp.reshape(4 * mesh_x + 2 * mesh_y + mesh_c, (1,)).astype(jnp.int32))
    loss = g_vec[5, 0]
    reduced = dict(zip(FFN1_NAMES + MIX_NAMES + FFN2_NAMES, list(last[:len(mine)]) + list(red2)))
    reduced["w_in"] = lax.dynamic_slice(reduced["w_in"], ((in_rows * ids[0]) % 8, 0), (in_rows, d))

    grads, delta, new_m, new_v = {}, {}, {}, {}
    for names in (FFN2_NAMES, FFN1_NAMES, ("w_in",), ("w_out",)):
        stepped = _adamw([work(w[n], n) for n in names], [reduced[n] for n in names], [work(m[n], n) for n in names],
                         [work(v[n], n) for n in names], names[0])
        for n, step in zip(names, stepped):
            grads[n], delta[n], new_m[n], new_v[n] = (work(a, n) for a in (reduced[n], *step))
    flat_pw = lambda a: a.reshape(-1, a.shape[-1])
    (d_pw, m_pw, v_pw), = _adamw([flat_pw(pool_w)], [g_pw], [flat_pw(m_pool_w)], [flat_pw(v_pool_w)], "pool_w")
    (d_vec, m_vec, v_vec), = _adamw([_pack_vec(w, d, width)], [g_vec], [_pack_vec(m, d, width)], [_pack_vec(v, d, width)],
                                    "vectors")
    grads.update(_unpack_vec(g_vec, width), pool_w=g_pw.reshape(pool_w.shape))
    delta.update(_unpack_vec(d_vec, width), pool_w=d_pw.reshape(pool_w.shape))
    new_m.update(_unpack_vec(m_vec, width), pool_w=m_pw.reshape(pool_w.shape))
    new_v.update(_unpack_vec(v_vec, width), pool_w=v_pw.reshape(pool_w.shape))
    return (loss, gx.reshape(x.shape), *[grads[n] for n in WEIGHT_NAMES], *[delta[n] for n in WEIGHT_NAMES],
            *[new_m[n] for n in WEIGHT_NAMES], *[new_v[n] for n in WEIGHT_NAMES])
```

```python
import functools

import jax
import jax.numpy as jnp
from jax import lax
from jax.experimental import pallas as pl
from jax.experimental.pallas import tpu as pltpu

F32 = jnp.float32
BF16 = jnp.bfloat16
EPS = 1e-6
NEG = -1e30
ADAM_LR = 0.001
ADAM_B1 = 0.9
ADAM_B2 = 0.999
ADAM_EPS = 1e-08
ADAM_WD = 0.01
ADAM_STEP = 10
POOL_WINDOWS = (2, 4, 8, 16)
HEAD_DIM = 64
N_HEADS = 8
LANES = 128
N_CHIPS = 4
ATT_BLOCK = 512
ATT_SUB = 128
FFN_TILE = 1024
VMEM_LIMIT = 62 * 1024 * 1024
MESH_AXES = ("x", "y", "c")
ANY = pl.BlockSpec(memory_space=pl.ANY)
VM = pl.BlockSpec(memory_space=pltpu.VMEM)


def _params(**kw):
    return pltpu.CompilerParams(vmem_limit_bytes=VMEM_LIMIT, **kw)


def _dot(a, b):
    return jnp.dot(a, b, preferred_element_type=F32)


def _dot_nt(a, b):
    return lax.dot_general(a, b, (((1,), (1,)), ((), ())), preferred_element_type=F32)


def _dot_tn(a, b):
    return lax.dot_general(a, b, (((0,), (0,)), ((), ())), preferred_element_type=F32)


def _sigmoid(z):
    return 1.0 / (1.0 + jnp.exp(-z))


def _rms(xf):
    return lax.rsqrt(jnp.mean(xf * xf, axis=-1, keepdims=True) + EPS)


def _rms_bwd(xf, r, gain, dh):
    xh = xf * r
    dyg = dh * gain
    return r * (dyg - xh * jnp.mean(dyg * xh, axis=-1, keepdims=True)), dh * xh


def _total(v):
    return jnp.sum(jnp.sum(v, axis=1, keepdims=True), axis=0, keepdims=True)


def _ffn_fwd(x, gain, wg, wu, wd, target=None, plan=None, part=None):
    t, d = x.shape
    nch, fc, _ = wg.shape
    tm = min(FFN_TILE, t)
    nt = t // tm
    with_loss = target is not None
    prev = part[2] if part is not None else None
    n_steps = nch if part is None else len(part[1])
    prefetch = () if part is None else (part[0], jnp.asarray(part[1], jnp.int32))
    chunk_of = lambda k, pre: k if not pre else jnp.bitwise_xor(pre[0][0], pre[1][k])

    def body(*refs):
        refs = refs[len(prefetch):]
        x_ref, g_ref, wg_ref, wu_ref, wd_ref = refs[:5]
        t_ref = refs[5] if with_loss else None
        outs = refs[(6 if with_loss else 5) + (4 if prev is not None else 0):]
        if prev is None:
            o_ref, h_ref, a_ref, b_ref, s_ref = outs[:5]
        else:
            h_ref = refs[5]
            o_ref, a_ref, b_ref, s_ref = outs[:4]
        l_ref, acc_ref = (outs[-2] if with_loss else None), outs[-1]
        k = pl.program_id(1)

        @pl.when(k == 0)
        def _():
            if prev is None:
                xf = x_ref[...]
                h_ref[...] = ((xf * _rms(xf)) * g_ref[...]).astype(BF16)
            acc_ref[...] = jnp.zeros_like(acc_ref)

        for rows in _row_halves(tm):
            h = h_ref[rows, :]
            a = _dot_nt(h, wg_ref[...])
            b = _dot_nt(h, wu_ref[...])
            sb = ((a * (0.5 * jnp.tanh(0.5 * a) + 0.5)) * b).astype(BF16)
            a_ref[rows, :] = a.astype(BF16)
            b_ref[rows, :] = b.astype(BF16)
            s_ref[rows, :] = sb
            acc_ref[rows, :] += _dot(sb, wd_ref[...])

        @pl.when(k == n_steps - 1)
        def _():
            y = x_ref[...] + 0.5 * acc_ref[...]
            if with_loss:
                e = y - t_ref[...]
                o_ref[...] = e * (1.0 / d)
                l_ref[...] = jnp.broadcast_to(_total(e * e) * (0.5 / d), l_ref.shape)
            else:
                o_ref[...] = y

    row = pl.BlockSpec((tm, d), lambda i, k, *pre: (i, 0))
    chunk = pl.BlockSpec((None, fc, d), lambda i, k, *pre: (chunk_of(k, pre), 0, 0))
    act = pl.BlockSpec((None, tm, fc), lambda i, k, *pre: (chunk_of(k, pre), i, 0))
    in_specs = [row, pl.BlockSpec((1, d), lambda i, k, *pre: (0, 0)), chunk, chunk, chunk]
    args = [x, gain, wg, wu, wd]
    out_shape, out_specs = [jax.ShapeDtypeStruct((t, d), F32)], [row]
    aliases = {}
    if with_loss:
        in_specs.append(row)
        args.append(target)
    if prev is None:
        out_shape.append(jax.ShapeDtypeStruct((t, d), BF16))
        out_specs.append(row)
    else:
        aliases = {len(args) + 1 + j: 1 + j for j in range(3)}
        in_specs += [row, ANY, ANY, ANY]
        args += list(prev)
    out_shape += [jax.ShapeDtypeStruct((nch, t, fc), BF16)] * 3
    out_specs += [act, act, act]
    if with_loss:
        out_shape.append(jax.ShapeDtypeStruct((nt, 8, LANES), F32))
        out_specs.append(pl.BlockSpec((None, 8, LANES), lambda i, k, *pre: (i, 0, 0)))
    name = "ffn_fwd_loss" if with_loss else "ffn_fwd" + ("" if part is None else "_" + "".join(str(r) for r in part[1]))
    res, carried = _pallas(body, name=name, args=args, in_specs=in_specs, out_shape=out_shape, out_specs=out_specs,
                           grid=(nt, n_steps), scratch_shapes=[pltpu.VMEM((tm, d), F32)], plan=plan, aliases=aliases,
                           prefetch=prefetch)
    if prev is not None:
        res.insert(1, prev[0])
    return res, carried


def _row_halves(n):
    return [slice(0, n // 2), slice(n // 2, n)]


def _swiglu_grads(dy_ref, a_ref, b_ref, wd_ref, rows):
    ds = _dot_nt(dy_ref[rows, :].astype(BF16), wd_ref[...])
    av = a_ref[rows, :].astype(F32)
    bv = b_ref[rows, :].astype(F32)
    th = jnp.tanh(0.5 * av)
    half_sig = 0.25 * th + 0.25
    dab = ((ds * bv) * (half_sig * (1.0 + av * (0.5 - 0.5 * th)))).astype(BF16)
    return dab, (ds * (av * half_sig)).astype(BF16)


def _ffn_bwd_a(dy, a, b, wd, name, plan=None):
    t, d = dy.shape
    nch, fc, _ = wd.shape
    tm = min(FFN_TILE, t)

    def body(dy_ref, a_ref, b_ref, wd_ref, da_ref, db_ref):
        for rows in _row_halves(tm):
            da_ref[rows, :], db_ref[rows, :] = _swiglu_grads(dy_ref, a_ref, b_ref, wd_ref, rows)

    act = pl.BlockSpec((None, tm, fc), lambda i, k: (k, i, 0))
    return _pallas(
        body, name=name, args=[dy, a, b, wd], out_shape=[jax.ShapeDtypeStruct((nch, t, fc), BF16)] * 2, grid=(t // tm, nch),
        in_specs=[pl.BlockSpec((tm, d), lambda i, k: (i, 0)), act, act, pl.BlockSpec((None, fc, d), lambda i, k: (k, 0, 0))],
        out_specs=[act, act], plan=plan)


def _ffn_bwd_h(dy, x, gain, da, db, wg, wu, name, tiles, prev=None, plan=None):
    t, d = x.shape
    nch, fc, _ = wg.shape
    tm = min(FFN_TILE, t)
    nt = t // tm
    t0, t1 = tiles

    def body(*refs):
        dy_ref, x_ref, g_ref, da_ref, db_ref, wg_ref, wu_ref = refs[:7]
        dx_ref, dg_ref, acc_ref = refs[-3:]
        k = pl.program_id(1)

        @pl.when(k == 0)
        def _():
            acc_ref[...] = jnp.zeros_like(acc_ref)

        acc_ref[...] += _dot(da_ref[...], wg_ref[...]) + _dot(db_ref[...], wu_ref[...])

        @pl.when(k == nch - 1)
        def _():
            xf = x_ref[...]
            dxn, dgr = _rms_bwd(xf, _rms(xf), g_ref[...], acc_ref[...])
            dx_ref[...] = dy_ref[...] + dxn
            dg_ref[...] = jnp.sum(dgr, axis=0, keepdims=True)

    row = pl.BlockSpec((tm, d), lambda i, k: (i + t0, 0))
    chunk = pl.BlockSpec((None, fc, d), lambda i, k: (k, 0, 0))
    act = pl.BlockSpec((None, tm, fc), lambda i, k: (k, i + t0, 0))
    args = [dy, x, gain, da, db, wg, wu]
    in_specs = [row, row, pl.BlockSpec((1, d), lambda i, k: (0, 0)), act, act, chunk, chunk]
    aliases = {}
    if prev is not None:
        aliases = {len(args): 0, len(args) + 1: 1}
        args += list(prev)
        in_specs += [ANY, ANY]
    return _pallas(
        body, name=name, args=args, out_shape=[jax.ShapeDtypeStruct((t, d), F32), jax.ShapeDtypeStruct((nt, 1, d), F32)],
        grid=(t1 - t0, nch), in_specs=in_specs,
        out_specs=[row, pl.BlockSpec((None, 1, d), lambda i, k: (i + t0, 0, 0))],
        scratch_shapes=[pltpu.VMEM((tm, d), F32)], plan=plan, aliases=aliases)


def _ffn_bwd_x(dy, x, gain, a, b, wg, wu, wd, name, plan=None):
    t, d = x.shape
    nch, fc, _ = wg.shape
    tm = min(FFN_TILE, t)
    nt = t // tm

    def body(dy_ref, x_ref, g_ref, a_ref, b_ref, wg_ref, wu_ref, wd_ref, dx_ref, da_ref, db_ref, dg_ref, acc_ref):
        k = pl.program_id(1)

        @pl.when(k == 0)
        def _():
            acc_ref[...] = jnp.zeros_like(acc_ref)

        for rows in _row_halves(tm):
            dab, dbb = _swiglu_grads(dy_ref, a_ref, b_ref, wd_ref, rows)
            da_ref[rows, :] = dab
            db_ref[rows, :] = dbb
            acc_ref[rows, :] += _dot(dab, wg_ref[...]) + _dot(dbb, wu_ref[...])

        @pl.when(k == nch - 1)
        def _():
            xf = x_ref[...]
            dxn, dgr = _rms_bwd(xf, _rms(xf), g_ref[...], acc_ref[...])
            dx_ref[...] = dy_ref[...] + dxn
            dg_ref[...] = jnp.sum(dgr, axis=0, keepdims=True)

    row = pl.BlockSpec((tm, d), lambda i, k: (i, 0))
    chunk = pl.BlockSpec((None, fc, d), lambda i, k: (k, 0, 0))
    act = pl.BlockSpec((None, tm, fc), lambda i, k: (k, i, 0))
    return _pallas(
        body, name=name, args=[dy, x, gain, a, b, wg, wu, wd],
        out_shape=[jax.ShapeDtypeStruct((t, d), F32), jax.ShapeDtypeStruct((nch, t, fc), BF16),
                   jax.ShapeDtypeStruct((nch, t, fc), BF16), jax.ShapeDtypeStruct((nt, 1, d), F32)],
        grid=(nt, nch),
        in_specs=[row, row, pl.BlockSpec((1, d), lambda i, k: (0, 0)), act, act, chunk, chunk, chunk],
        out_specs=[row, act, act, pl.BlockSpec((None, 1, d), lambda i, k: (i, 0, 0))],
        scratch_shapes=[pltpu.VMEM((tm, d), F32)], plan=plan)


def _ffn_bwd_w(pairs, name, plan=None):
    n = len(pairs)
    nch, t, fc = pairs[0][0].shape
    d = pairs[0][1].shape[1]
    tm = min(1024, t)

    def body(*refs):
        @pl.when(pl.program_id(1) == 0)
        def _():
            for o_ref in refs[2 * n:]:
                o_ref[...] = jnp.zeros_like(o_ref)

        for j, (_, _, scale) in enumerate(pairs):
            other = refs[n + j][...]
            if other.dtype != BF16:
                other = (scale * other).astype(BF16)
            refs[2 * n + j][...] += _dot_tn(refs[j][...], other)

    row = pl.BlockSpec((tm, d), lambda k, i: (i, 0))
    act = pl.BlockSpec((None, tm, fc), lambda k, i: (k, i, 0))
    chunk = pl.BlockSpec((None, fc, d), lambda k, i: (k, 0, 0))
    return _pallas(body, name=name, args=[p[0] for p in pairs] + [p[1] for p in pairs],
                   out_shape=[jax.ShapeDtypeStruct((nch, fc, d), F32)] * n, grid=(nch, t // tm),
                   in_specs=[act] * n + [row] * n, out_specs=[chunk] * n, plan=plan)


def _head_masks():
    lane = lax.broadcasted_iota(jnp.int32, (1, LANES), 1)
    return lane < HEAD_DIM


def _head_rms(x, lo):
    x2 = x * x
    s0 = jnp.sum(jnp.where(lo, x2, 0.0), axis=1, keepdims=True)
    s1 = jnp.sum(jnp.where(lo, 0.0, x2), axis=1, keepdims=True)
    return jnp.where(lo, lax.rsqrt(s0 * (1.0 / HEAD_DIM) + EPS), lax.rsqrt(s1 * (1.0 / HEAD_DIM) + EPS))


def _head_mean(v, lo):
    s0 = jnp.sum(jnp.where(lo, v, 0.0), axis=1, keepdims=True)
    s1 = jnp.sum(jnp.where(lo, 0.0, v), axis=1, keepdims=True)
    return jnp.where(lo, s0, s1) * (1.0 / HEAD_DIM)


def _mix_proj(x1, gain, wt, qn, kn, pool_width, attn_width, plan=None):
    t, d = x1.shape
    tm = min(512, t)
    nt = t // tm
    scale = HEAD_DIM ** -0.5
    c_q, c_k, c_v = pool_width, pool_width + attn_width, pool_width + 2 * attn_width
    c_f = c_v + attn_width

    def body(x_ref, g_ref, wt_ref, qn_ref, kn_ref, hm_ref, pv_ref, q_ref, k_ref, qh_ref, kh_ref, vb_ref, f_ref):
        lo = _head_masks()
        for rows in _row_halves(tm):
            xf = x_ref[rows, :]
            hm = ((xf * _rms(xf)) * g_ref[...]).astype(BF16)
            hm_ref[rows, :] = hm
            f_ref[rows, :] = _dot_nt(hm, wt_ref[c_f:c_f + LANES, :])
            pv_ref[rows, :] = _dot_nt(hm, wt_ref[0:pool_width, :])
            vb_ref[rows, :] = _dot_nt(hm, wt_ref[c_v:c_v + attn_width, :]).astype(BF16)
            for c0, raw_ref, hat_ref, n_ref, mul in ((c_q, q_ref, qh_ref, qn_ref, scale), (c_k, k_ref, kh_ref, kn_ref, 1.0)):
                raw = _dot_nt(hm, wt_ref[c0:c0 + attn_width, :])
                raw_ref[rows, :] = raw
                for blk in range(attn_width // LANES):
                    sl = slice(blk * LANES, (blk + 1) * LANES)
                    xb = raw[:, sl]
                    hat_ref[rows, sl] = (((xb * _head_rms(xb, lo)) * n_ref[:, sl]) * mul).astype(BF16)

    row = pl.BlockSpec((tm, d), lambda i: (i, 0))
    half = pl.BlockSpec((tm, attn_width), lambda i: (i, 0))
    const = lambda shape: pl.BlockSpec(shape, lambda i: (0, 0))
    return _pallas(
        body, name="mix_proj", args=[x1, gain, wt, qn, kn],
        out_shape=[jax.ShapeDtypeStruct((t, d), BF16), jax.ShapeDtypeStruct((t, pool_width), F32),
                   jax.ShapeDtypeStruct((t, attn_width), F32), jax.ShapeDtypeStruct((t, attn_width), F32),
                   jax.ShapeDtypeStruct((t, attn_width), BF16), jax.ShapeDtypeStruct((t, attn_width), BF16),
                   jax.ShapeDtypeStruct((t, attn_width), BF16), jax.ShapeDtypeStruct((t, LANES), F32)],
        grid=(nt,),
        in_specs=[row, const((1, d)), const(wt.shape), const((1, attn_width)), const((1, attn_width))],
        out_specs=[row, pl.BlockSpec((tm, pool_width), lambda i: (i, 0)), half, half, half, half, half,
                   pl.BlockSpec((tm, LANES), lambda i: (i, 0))], plan=plan)


def _shift_down(v, dist, row):
    return jnp.where(row >= dist, pltpu.roll(v, dist, 0), 0.0)


def _shift_up(v, dist, row, n):
    return jnp.where(row + dist < n, pltpu.roll(v, n - dist, 0), 0.0)


def _aug_lane(e):
    return HEAD_DIM if e == 0 else 0


def _forget_prefix(f, bias, qh, kh, n_batch, seq):
    def body(f_ref, b_ref, q_ref, k_ref, qa_ref, ka_ref):
        z = f_ref[...] + b_ref[...]
        acc = jnp.minimum(z, 0.0) - jnp.log(1.0 + jnp.exp(-jnp.abs(z)))
        row = lax.broadcasted_iota(jnp.int32, (seq, 1), 0)
        dist = 1
        while dist < seq:
            acc = acc + _shift_down(acc, dist, row)
            dist *= 2
        lane = lax.broadcasted_iota(jnp.int32, (1, LANES), 1)
        for h in range(N_HEADS):
            pair, e = divmod(h, 2)
            a0 = _aug_lane(e)
            own = (lane < HEAD_DIM) if e == 0 else (lane >= HEAD_DIM)
            fh = _pick_lane(acc, h)
            hi = fh.astype(BF16).astype(F32)
            rest = fh - hi
            mid = rest.astype(BF16).astype(F32)
            low = rest - mid
            q_ones = (lane >= a0 + 3) & (lane < a0 + 6)
            k_ones = (lane >= a0) & (lane < a0 + 3)
            q_aug = jnp.where(lane == a0, hi, jnp.where(lane == a0 + 1, mid, jnp.where(lane == a0 + 2, low,
                              jnp.where(q_ones, 1.0, 0.0))))
            k_aug = jnp.where(k_ones, 1.0, jnp.where(lane == a0 + 3, -hi, jnp.where(lane == a0 + 4, -mid,
                              jnp.where(lane == a0 + 5, -low, 0.0))))
            src = slice(pair * LANES, (pair + 1) * LANES)
            dst = slice(h * LANES, (h + 1) * LANES)
            qa_ref[:, dst] = jnp.where(own, q_ref[:, src].astype(F32), q_aug).astype(BF16)
            ka_ref[:, dst] = jnp.where(own, k_ref[:, src].astype(F32), k_aug).astype(BF16)

    width = qh.shape[1]
    tok = pl.BlockSpec((seq, width), lambda b: (b, 0))
    aug = pl.BlockSpec((seq, N_HEADS * LANES), lambda b: (b, 0))
    return pl.pallas_call(
        body, out_shape=[jax.ShapeDtypeStruct((n_batch * seq, N_HEADS * LANES), BF16)] * 2, grid=(n_batch,),
        in_specs=[pl.BlockSpec((seq, LANES), lambda b: (b, 0)), pl.BlockSpec((1, LANES), lambda b: (0, 0)), tok, tok],
        out_specs=[aug, aug], compiler_params=_params(), name="forget_prefix",
    )(f, bias, qh, kh)


def _pool_groups(pv_ref, pw_ref, ps_ref, seq):
    row = lax.broadcasted_iota(jnp.int32, (seq, 1), 0)
    pos = (row + 1).astype(F32)
    out = []
    for g, win in enumerate(POOL_WINDOWS):
        sl = slice(g * LANES, (g + 1) * LANES)
        xg = pv_ref[:, sl]
        acc = xg
        dist = 1
        while dist < win:
            acc = acc + _shift_down(acc, dist, row)
            dist *= 2
        pooled = (acc / jnp.minimum(pos, float(win)) - xg).astype(BF16)
        mixed = _dot(pooled, pw_ref[g])
        out.append((pooled, mixed, mixed * ps_ref[:, sl]))
    return out


def _pool_fwd(pv, pw, ps, onp, n_batch, seq):
    width = pv.shape[1]

    def body(pv_ref, pw_ref, ps_ref, on_ref, y_ref):
        groups = _pool_groups(pv_ref, pw_ref, ps_ref, seq)
        ssq = sum(jnp.sum(ms * ms, axis=1, keepdims=True) for _, _, ms in groups)
        r = lax.rsqrt(ssq * (1.0 / width) + EPS)
        for g, (_, _, ms) in enumerate(groups):
            sl = slice(g * LANES, (g + 1) * LANES)
            y_ref[:, sl] = ((ms * r) * on_ref[:, sl]).astype(BF16)

    return pl.pallas_call(
        body, out_shape=jax.ShapeDtypeStruct((n_batch * seq, width), BF16), grid=(n_batch,),
        in_specs=[pl.BlockSpec((seq, width), lambda b: (b, 0)), pl.BlockSpec(pw.shape, lambda b: (0, 0, 0)),
                  pl.BlockSpec((1, width), lambda b: (0, 0)), pl.BlockSpec((1, width), lambda b: (0, 0))],
        out_specs=pl.BlockSpec((seq, width), lambda b: (b, 0)),
        compiler_params=_params(), name="pool_fwd",
    )(pv, pw, ps, onp)


def _pool_bwd(pv, dyp, pw, ps, onp, n_batch, seq):
    width = pv.shape[1]

    def body(pv_ref, dy_ref, pw_ref, ps_ref, on_ref, dpv_ref, dpw_ref, dps_ref, don_ref):
        groups = _pool_groups(pv_ref, pw_ref, ps_ref, seq)
        ssq = sum(jnp.sum(ms * ms, axis=1, keepdims=True) for _, _, ms in groups)
        r = lax.rsqrt(ssq * (1.0 / width) + EPS)
        mean = sum(jnp.sum((dy_ref[:, g * LANES:(g + 1) * LANES] * on_ref[:, g * LANES:(g + 1) * LANES]) * (ms * r),
                           axis=1, keepdims=True) for g, (_, _, ms) in enumerate(groups)) * (1.0 / width)
        row = lax.broadcasted_iota(jnp.int32, (seq, 1), 0)
        pos = (row + 1).astype(F32)
        for g, (pooled, mixed, ms) in enumerate(groups):
            sl = slice(g * LANES, (g + 1) * LANES)
            dy = dy_ref[:, sl]
            xh = ms * r
            don_ref[:, sl] = jnp.sum(dy * xh, axis=0, keepdims=True)
            dms = r * (dy * on_ref[:, sl] - xh * mean)
            dps_ref[:, sl] = jnp.sum(dms * mixed, axis=0, keepdims=True)
            dmix = (dms * ps_ref[:, sl]).astype(BF16)
            dpw_ref[g] = _dot_tn(pooled, dmix)
            dpool = _dot_nt(dmix, pw_ref[g])
            win = POOL_WINDOWS[g]
            acc = dpool / jnp.minimum(pos, float(win))
            dist = 1
            while dist < win:
                acc = acc + _shift_up(acc, dist, row, seq)
                dist *= 2
            dpv_ref[:, sl] = (acc - dpool).astype(BF16)

    tok = pl.BlockSpec((seq, width), lambda b: (b, 0))
    vec = pl.BlockSpec((1, width), lambda b: (0, 0))
    pvec = pl.BlockSpec((None, 1, width), lambda b: (b, 0, 0))
    return pl.pallas_call(
        body,
        out_shape=[jax.ShapeDtypeStruct((n_batch * seq, width), BF16),
                   jax.ShapeDtypeStruct((n_batch,) + pw.shape, F32),
                   jax.ShapeDtypeStruct((n_batch, 1, width), F32), jax.ShapeDtypeStruct((n_batch, 1, width), F32)],
        grid=(n_batch,),
        in_specs=[tok, tok, pl.BlockSpec(pw.shape, lambda b: (0, 0, 0)), vec, vec],
        out_specs=[tok, pl.BlockSpec((None,) + pw.shape, lambda b: (b, 0, 0, 0)), pvec, pvec],
        compiler_params=_params(), name="pool_bwd",
    )(pv, dyp, pw, ps, onp)


def _pick_lane(tile, idx):
    lane = lax.broadcasted_iota(jnp.int32, (1, LANES), 1)
    return jnp.sum(jnp.where(lane == idx, tile, 0.0), axis=1, keepdims=True)


def _pick_row(tile, idx):
    sub = lax.broadcasted_iota(jnp.int32, (tile.shape[0], 1), 0)
    return jnp.sum(jnp.where(sub == idx, tile, 0.0), axis=0, keepdims=True)


def _put_lane(col, idx):
    lane = lax.broadcasted_iota(jnp.int32, (1, LANES), 1)
    return jnp.where(lane == idx, col, 0.0)


def _head_select(e):
    lo = _head_masks()
    return lo if e == 0 else jnp.logical_not(lo)


def _causal(st, shift):
    row = lax.broadcasted_iota(jnp.int32, st.shape, 0)
    col = lax.broadcasted_iota(jnp.int32, st.shape, 1) + shift
    return jnp.where(col >= row, st, NEG)


def _transpose_blocks(a):
    rows, cols = a.shape
    return jnp.concatenate(
        [jnp.concatenate([a[r:r + LANES, c:c + LANES].T for r in range(0, rows, LANES)], axis=1)
         for c in range(0, cols, LANES)], axis=0)


def _stat_rows(ref, head, nsub):
    return jnp.concatenate([_pick_row(ref[a], head) for a in range(nsub)], axis=1)


def _accumulate(ref, value, first):
    @pl.when(first)
    def _():
        ref[...] = value

    @pl.when(jnp.logical_not(first))
    def _():
        ref[...] += value


def _attn_fwd(qa, ka, vb, n_batch, seq, plan=None):
    tq = min(ATT_BLOCK, seq)
    nq, nsub, tk = seq // tq, tq // ATT_SUB, tq
    pairs = vb.shape[1] // LANES

    def body(q_ref, k_ref, v_ref, o_ref, lse_ref, acc_ref):
        i, p = pl.program_id(1), pl.program_id(2)
        row_lo = lax.broadcasted_iota(jnp.int32, (LANES, 1), 0) < HEAD_DIM
        qs = [q_ref[:, e * LANES:(e + 1) * LANES] for e in range(2)]
        acc_ref[...] = jnp.zeros_like(acc_ref)

        def tile(off, stats, diagonal):
            vj = v_ref[pl.ds(off, tk), :]
            new, alphas, pvs = [], [], []
            for e in range(2):
                st = _dot_nt(k_ref[pl.ds(off, tk), e * LANES:(e + 1) * LANES], qs[e])
                if diagonal:
                    st = _causal(st, 0)
                m, l = stats[e]
                m_new = jnp.maximum(m, jnp.max(st, axis=0, keepdims=True))
                alpha = jnp.exp(m - m_new)
                pt = jnp.exp(st - m_new)
                new.append((m_new, alpha * l + jnp.sum(pt, axis=0, keepdims=True)))
                alphas.append(alpha)
                pvs.append(_dot_tn(jnp.where(_head_select(e), vj, jnp.zeros_like(vj)), pt.astype(BF16)))
            acc_ref[...] = acc_ref[...] * jnp.where(row_lo, alphas[0], alphas[1]) + (pvs[0] + pvs[1])
            return tuple(new)

        init = ((jnp.full((1, tq), NEG, F32), jnp.zeros((1, tq), F32)),) * 2
        stats = lax.fori_loop(0, i, lambda j, st: tile(pl.multiple_of(j * tk, tk), st, False), init)
        (m0, l0), (m1, l1) = tile(pl.multiple_of(i * tk, tk), stats, True)
        out_t = acc_ref[...] / jnp.where(row_lo, l0, l1)
        sub = lax.broadcasted_iota(jnp.int32, (8, 1), 0)
        lse0, lse1 = m0 + jnp.log(l0), m1 + jnp.log(l1)
        for a in range(nsub):
            sl = slice(a * ATT_SUB, (a + 1) * ATT_SUB)
            o_ref[sl, :] = out_t[:, sl].T
            rows = jnp.where(sub == 2 * p, lse0[:, sl], 0.0) + jnp.where(sub == 2 * p + 1, lse1[:, sl], 0.0)
            _accumulate(lse_ref.at[a], rows, p == 0)

    return _pallas(
        body, name="attn_fwd", args=[qa, ka, vb],
        out_shape=[jax.ShapeDtypeStruct((n_batch * seq, pairs * LANES), F32),
                   jax.ShapeDtypeStruct((n_batch * seq // ATT_SUB, 8, ATT_SUB), F32)],
        grid=(n_batch, nq, pairs),
        in_specs=[pl.BlockSpec((tq, 2 * LANES), lambda b, i, p: (b * nq + i, p)),
                  pl.BlockSpec((seq, 2 * LANES), lambda b, i, p: (b, p)),
                  pl.BlockSpec((seq, LANES), lambda b, i, p: (b, p))],
        out_specs=[pl.BlockSpec((tq, LANES), lambda b, i, p: (b * nq + i, p)),
                   pl.BlockSpec((nsub, 8, ATT_SUB), lambda b, i, p: (b * nq + i, 0, 0))],
        scratch_shapes=[pltpu.VMEM((LANES, tq), F32)], plan=plan)


def _attn_bwd_q(qa, ka, vb, do, lse, delta, n_batch, seq, plan=None):
    tq = min(ATT_BLOCK, seq)
    nq, nsub, tk = seq // tq, tq // ATT_SUB, tq
    pairs = vb.shape[1] // LANES

    def body(q_ref, k_ref, v_ref, do_ref, lse_ref, dl_ref, dq_ref, dfq_ref, acc0_ref, acc1_ref):
        i, p = pl.program_id(1), pl.program_id(2)
        accs = (acc0_ref, acc1_ref)
        qs = [q_ref[:, e * LANES:(e + 1) * LANES] for e in range(2)]
        dov = do_ref[...]
        ls = [_stat_rows(lse_ref, 2 * p + e, nsub) for e in range(2)]
        dl = [_stat_rows(dl_ref, 2 * p + e, nsub) for e in range(2)]
        for acc in accs:
            acc[...] = jnp.zeros_like(acc)

        def tile(off, diagonal):
            vj = v_ref[pl.ds(off, tk), :]
            for e in range(2):
                kj = k_ref[pl.ds(off, tk), e * LANES:(e + 1) * LANES]
                st = _dot_nt(kj, qs[e])
                if diagonal:
                    st = _causal(st, 0)
                pt = jnp.exp(st - ls[e])
                dpt = _dot_nt(jnp.where(_head_select(e), vj, jnp.zeros_like(vj)), dov)
                accs[e][...] += _dot(_transpose_blocks(kj), (pt * (dpt - dl[e])).astype(BF16))

        def step(j, carry):
            tile(pl.multiple_of(j * tk, tk), False)
            return carry

        lax.fori_loop(0, i, step, 0)
        tile(pl.multiple_of(i * tk, tk), True)
        dq0, dq1 = _transpose_blocks(acc0_ref[...]), _transpose_blocks(acc1_ref[...])
        dq_ref[...] = jnp.where(_head_masks(), dq0, dq1)
        dfq = _put_lane(_pick_lane(dq0, _aug_lane(0)), 2 * p) + _put_lane(_pick_lane(dq1, _aug_lane(1)), 2 * p + 1)
        _accumulate(dfq_ref, dfq, p == 0)

    stat = pl.BlockSpec((nsub, 8, ATT_SUB), lambda b, i, p: (b * nq + i, 0, 0))
    blk = pl.BlockSpec((tq, LANES), lambda b, i, p: (b * nq + i, p))
    return _pallas(
        body, name="attn_bwd_q", args=[qa, ka, vb, do, lse, delta],
        out_shape=[jax.ShapeDtypeStruct((n_batch * seq, pairs * LANES), F32), jax.ShapeDtypeStruct((n_batch * seq, LANES), F32)],
        grid=(n_batch, nq, pairs),
        in_specs=[pl.BlockSpec((tq, 2 * LANES), lambda b, i, p: (b * nq + i, p)),
                  pl.BlockSpec((seq, 2 * LANES), lambda b, i, p: (b, p)),
                  pl.BlockSpec((seq, LANES), lambda b, i, p: (b, p)), blk, stat, stat],
        out_specs=[blk, pl.BlockSpec((tq, LANES), lambda b, i, p: (b * nq + i, 0))],
        scratch_shapes=[pltpu.VMEM((LANES, tq), F32), pltpu.VMEM((LANES, tq), F32)], plan=plan)


def _attn_bwd_kv(qa, ka, vb, do, lse, delta, n_batch, seq, plan=None):
    tkb = min(ATT_BLOCK, seq)
    nk, nsub, tq = seq // tkb, tkb // ATT_SUB, tkb
    n_tiles = seq // ATT_SUB
    pairs = vb.shape[1] // LANES

    def body(q_ref, k_ref, v_ref, do_ref, lse_ref, dl_ref, dk_ref, dv_ref, dfk_ref, dk0_ref, dk1_ref, dva_ref):
        j, p = pl.program_id(1), pl.program_id(2)
        dks = (dk0_ref, dk1_ref)
        ks = [k_ref[:, e * LANES:(e + 1) * LANES] for e in range(2)]
        vj = v_ref[...]
        vs = [jnp.where(_head_select(e), vj, jnp.zeros_like(vj)) for e in range(2)]
        for acc in (dk0_ref, dk1_ref, dva_ref):
            acc[...] = jnp.zeros_like(acc)

        def tile(t, diagonal):
            off = pl.multiple_of(t * tq, tq)
            dov = do_ref[pl.ds(off, tq), :]
            for e in range(2):
                qe = q_ref[pl.ds(off, tq), e * LANES:(e + 1) * LANES]
                st = _dot_nt(ks[e], qe)
                if diagonal:
                    st = _causal(st, 0)
                rows = lambda ref: jnp.concatenate([_pick_row(ref[t * nsub + a], 2 * p + e) for a in range(nsub)], axis=1)
                pt = jnp.exp(st - rows(lse_ref))
                dva_ref[...] += _dot(pt.astype(BF16), jnp.where(_head_select(e), dov, jnp.zeros_like(dov)))
                dst = pt * (_dot_nt(vs[e], dov) - rows(dl_ref))
                dks[e][...] += _dot(dst.astype(BF16), qe)

        def step(t, carry):
            tile(t, False)
            return carry

        lax.fori_loop(j + 1, nk, step, 0)
        tile(j, True)
        dk0, dk1 = dk0_ref[...], dk1_ref[...]
        dk_ref[...] = jnp.where(_head_masks(), dk0, dk1)
        dv_ref[...] = dva_ref[...].astype(BF16)
        dfk = (_put_lane(_pick_lane(dk0, _aug_lane(0) + 3), 2 * p)
               + _put_lane(_pick_lane(dk1, _aug_lane(1) + 3), 2 * p + 1))
        _accumulate(dfk_ref, -dfk, p == 0)

    stat = pl.BlockSpec((n_tiles, 8, ATT_SUB), lambda b, j, p: (b, 0, 0))
    blk = pl.BlockSpec((tkb, LANES), lambda b, j, p: (b * nk + j, p))
    acc = pltpu.VMEM((tkb, LANES), F32)
    return _pallas(
        body, name="attn_bwd_kv", args=[qa, ka, vb, do, lse, delta],
        out_shape=[jax.ShapeDtypeStruct((n_batch * seq, pairs * LANES), F32),
                   jax.ShapeDtypeStruct((n_batch * seq, pairs * LANES), BF16),
                   jax.ShapeDtypeStruct((n_batch * seq, LANES), F32)],
        grid=(n_batch, nk, pairs),
        in_specs=[pl.BlockSpec((seq, 2 * LANES), lambda b, j, p: (b, p)),
                  pl.BlockSpec((tkb, 2 * LANES), lambda b, j, p: (b * nk + j, p)), blk,
                  pl.BlockSpec((seq, LANES), lambda b, j, p: (b, p)), stat, stat],
        out_specs=[blk, blk, pl.BlockSpec((tkb, LANES), lambda b, j, p: (b * nk + j, 0))],
        scratch_shapes=[acc, acc, acc], plan=plan)


def _forget_bwd(dfq, dfk, f, bias, n_batch, seq):
    def body(dfq_ref, dfk_ref, f_ref, b_ref, df_ref, db_ref):
        acc = dfq_ref[...] + dfk_ref[...]
        row = lax.broadcasted_iota(jnp.int32, (seq, 1), 0)
        dist = 1
        while dist < seq:
            acc = acc + _shift_up(acc, dist, row, seq)
            dist *= 2
        df = acc * _sigmoid(-(f_ref[...] + b_ref[...]))
        df_ref[...] = df
        db_ref[...] = jnp.sum(df, axis=0, keepdims=True)

    col = pl.BlockSpec((seq, LANES), lambda b: (b, 0))
    return pl.pallas_call(
        body,
        out_shape=[jax.ShapeDtypeStruct((n_batch * seq, LANES), F32), jax.ShapeDtypeStruct((n_batch, 1, LANES), F32)],
        grid=(n_batch,), in_specs=[col, col, col, pl.BlockSpec((1, LANES), lambda b: (0, 0))],
        out_specs=[col, pl.BlockSpec((None, 1, LANES), lambda b: (b, 0, 0))],
        compiler_params=_params(), name="forget_bwd",
    )(dfq, dfk, f, bias)


def _mix_out(x1, yp, o, ona, woa, wob):
    t, d = x1.shape
    width = o.shape[1]
    tm = min(512, t)

    def body(x_ref, yp_ref, o_ref, on_ref, wa_ref, wb_ref, x2_ref, ya_ref):
        of = o_ref[...]
        ya = ((of * _rms(of)) * on_ref[...]).astype(BF16)
        ya_ref[...] = ya
        x2_ref[...] = x_ref[...] + (_dot(yp_ref[...], wa_ref[...]) + _dot(ya, wb_ref[...]))

    row = pl.BlockSpec((tm, d), lambda i: (i, 0))
    half = pl.BlockSpec((tm, width), lambda i: (i, 0))
    wspec = pl.BlockSpec((width, d), lambda i: (0, 0))
    return pl.pallas_call(
        body, out_shape=[jax.ShapeDtypeStruct((t, d), F32), jax.ShapeDtypeStruct((t, width), BF16)],
        grid=(t // tm,), in_specs=[row, half, half, pl.BlockSpec((1, width), lambda i: (0, 0)), wspec, wspec],
        out_specs=[row, half], compiler_params=_params(), name="mix_out",
    )(x1, yp, o, ona, woa, wob)


def _mix_out_bwd(dx2, o, yp, ya, ona, woa, wob, plan=None):
    t, d = dx2.shape
    width = o.shape[1]
    tm = min(512, t)
    nt = t // tm

    def body(dx_ref, o_ref, yp_ref, ya_ref, on_ref, wa_ref, wb_ref, dyp_ref, do_ref, dl_ref, dwa_ref, dwb_ref, don_ref):
        @pl.when(pl.program_id(0) == 0)
        def _():
            dwa_ref[...] = jnp.zeros_like(dwa_ref)
            dwb_ref[...] = jnp.zeros_like(dwb_ref)

        dxb = dx_ref[...].astype(BF16)
        dwa_ref[...] += _dot_tn(yp_ref[...], dxb)
        dwb_ref[...] += _dot_tn(ya_ref[...], dxb)
        dyp_ref[...] = _dot_nt(dxb, wa_ref[...])
        of = o_ref[...]
        dov, dgr = _rms_bwd(of, _rms(of), on_ref[...], _dot_nt(dxb, wb_ref[...]))
        don_ref[...] = jnp.sum(dgr, axis=0, keepdims=True)
        do_ref[...] = dov.astype(BF16)
        lo = _head_masks()
        prod = dov * of
        delta = jnp.zeros((tm, LANES), F32)
        for blk in range(width // LANES):
            pb = prod[:, blk * LANES:(blk + 1) * LANES]
            delta = delta + _put_lane(jnp.sum(jnp.where(lo, pb, 0.0), axis=1, keepdims=True), 2 * blk)
            delta = delta + _put_lane(jnp.sum(jnp.where(lo, 0.0, pb), axis=1, keepdims=True), 2 * blk + 1)
        for c in range(tm // ATT_SUB):
            dl_ref[c] = delta[c * ATT_SUB:(c + 1) * ATT_SUB, :].T[0:8, :]

    row = pl.BlockSpec((tm, d), lambda i: (i, 0))
    half = pl.BlockSpec((tm, width), lambda i: (i, 0))
    wspec = pl.BlockSpec((width, d), lambda i: (0, 0))
    return _pallas(
        body, name="mix_out_bwd", args=[dx2, o, yp, ya, ona, woa, wob],
        out_shape=[jax.ShapeDtypeStruct((t, width), F32), jax.ShapeDtypeStruct((t, width), BF16),
                   jax.ShapeDtypeStruct((t // ATT_SUB, 8, ATT_SUB), F32), jax.ShapeDtypeStruct((width, d), F32),
                   jax.ShapeDtypeStruct((width, d), F32), jax.ShapeDtypeStruct((nt, 1, width), F32)],
        grid=(nt,),
        in_specs=[row, half, half, half, pl.BlockSpec((1, width), lambda i: (0, 0)), wspec, wspec],
        out_specs=[half, half, pl.BlockSpec((tm // ATT_SUB, 8, ATT_SUB), lambda i: (i, 0, 0)), wspec, wspec,
                   pl.BlockSpec((None, 1, width), lambda i: (i, 0, 0))], plan=plan)


def _mix_in_bwd(dx2, x1, gain, hm, dpv, dqh, q, dkh, k, dv, df, qn, kn, wt):
    t, d = x1.shape
    width = q.shape[1]
    pool_width = dpv.shape[1]
    tm = min(512, t)
    nt = t // tm
    scale = HEAD_DIM ** -0.5
    c_q, c_k, c_v = pool_width, pool_width + width, pool_width + 2 * width
    c_f = c_v + width

    def body(dx2_ref, x_ref, g_ref, hm_ref, dpv_ref, dqh_ref, q_ref, dkh_ref, k_ref, dv_ref, df_ref, qn_ref, kn_ref,
             wt_ref, dx_ref, dwt_ref, dg_ref, dqn_ref, dkn_ref):
        @pl.when(pl.program_id(0) == 0)
        def _():
            dwt_ref[...] = jnp.zeros_like(dwt_ref)

        lo = _head_masks()
        for part, rows in enumerate(_row_halves(tm)):
            def put(ref, sl, value):
                ref[:, sl] = value if part == 0 else ref[:, sl] + value

            hm = hm_ref[rows, :]
            pieces = [(0, dpv_ref[rows, :])]
            for c0, raw_ref, dh_ref, n_ref, dn_ref, mul in ((c_q, q_ref, dqh_ref, qn_ref, dqn_ref, scale),
                                                           (c_k, k_ref, dkh_ref, kn_ref, dkn_ref, 1.0)):
                cols = []
                for blk in range(width // LANES):
                    sl = slice(blk * LANES, (blk + 1) * LANES)
                    xb = raw_ref[rows, sl]
                    gb = dh_ref[rows, sl] * mul
                    r = _head_rms(xb, lo)
                    xh = xb * r
                    dyg = gb * n_ref[:, sl]
                    cols.append((r * (dyg - xh * _head_mean(dyg * xh, lo))).astype(BF16))
                    put(dn_ref, sl, jnp.sum(gb * xh, axis=0, keepdims=True))
                pieces.append((c0, jnp.concatenate(cols, axis=1)))
            pieces.append((c_v, dv_ref[rows, :]))
            pieces.append((c_f, df_ref[rows, :].astype(BF16)))
            dhm = jnp.zeros((tm // 2, d), F32)
            for c0, piece in pieces:
                dwt_ref[c0:c0 + piece.shape[1], :] += _dot_tn(piece, hm)
                dhm = dhm + _dot(piece, wt_ref[c0:c0 + piece.shape[1], :])
            xf = x_ref[rows, :]
            dxn, dgr = _rms_bwd(xf, _rms(xf), g_ref[...], dhm)
            dx_ref[rows, :] = dx2_ref[rows, :] + dxn
            put(dg_ref, slice(None), jnp.sum(dgr, axis=0, keepdims=True))

    row = pl.BlockSpec((tm, d), lambda i: (i, 0))
    half = pl.BlockSpec((tm, width), lambda i: (i, 0))
    const = lambda shape: pl.BlockSpec(shape, lambda i: (0, 0))
    pvec = lambda n: pl.BlockSpec((None, 1, n), lambda i: (i, 0, 0))
    return pl.pallas_call(
        body,
        out_shape=[jax.ShapeDtypeStruct((t, d), F32), jax.ShapeDtypeStruct(wt.shape, F32),
                   jax.ShapeDtypeStruct((nt, 1, d), F32),
                   jax.ShapeDtypeStruct((nt, 1, width), F32), jax.ShapeDtypeStruct((nt, 1, width), F32)],
        grid=(nt,),
        in_specs=[row, row, const((1, d)), row, pl.BlockSpec((tm, pool_width), lambda i: (i, 0)), half, half, half, half,
                  half, pl.BlockSpec((tm, LANES), lambda i: (i, 0)), const((1, width)), const((1, width)),
                  const(wt.shape)],
        out_specs=[row, const(wt.shape), pvec(d), pvec(width), pvec(width)],
        compiler_params=_params(), name="mix_in_bwd",
    )(dx2, x1, gain, hm, dpv, dqh, q, dkh, k, dv, df, qn, kn, wt)


def _mesh_pos():
    return lax.axis_index("x"), lax.axis_index("y"), lax.axis_index("c")


def _other_chips(x, y):
    return [(1 - x, y), (x, 1 - y), (1 - x, 1 - y)]


def _remote(src, dst, send_sem, recv_sem, device):
    return pltpu.make_async_remote_copy(src_ref=src, dst_ref=dst, send_sem=send_sem, recv_sem=recv_sem,
                                        device_id=device, device_id_type=pl.DeviceIdType.MESH)


def _half_rows(n_rows, which):
    half = n_rows // 2
    return pl.ds(pl.multiple_of(which * half, 8), half)


def _row_block(rows, cols, itemsize=4):
    rb = rows
    while rb * cols * itemsize > (1 << 20) and rb % 32 == 0:
        rb //= 2
    return rb


def _place_cast(ws, chip, tag):
    n = len(ws)
    rows, cols = ws[0].shape
    rb = _row_block(rows, cols)

    def body(k_ref, *refs):
        for w_ref, o_ref in zip(refs[:n], refs[n:]):
            o_ref[...] = w_ref[...].astype(BF16)

    return pl.pallas_call(
        body, out_shape=[jax.ShapeDtypeStruct((N_CHIPS, rows, cols), BF16)] * n,
        grid_spec=pltpu.PrefetchScalarGridSpec(
            num_scalar_prefetch=1, grid=(rows // rb,),
            in_specs=[pl.BlockSpec((rb, cols), lambda i, k: (i, 0))] * n,
            out_specs=[pl.BlockSpec((None, rb, cols), lambda i, k: (k[0], i, 0))] * n),
        compiler_params=_params(), name="place_" + tag,
    )(chip, *ws)


class _Plan:
    def __init__(self, ins, outs, alias, sems, start, finish):
        self.ins, self.outs, self.alias, self.sems, self.start, self.finish = ins, outs, alias, sems, start, finish


def _merge_plans(a, b):
    ni, no, ns = len(a.ins), len(a.outs), len(a.sems)
    alias = dict(a.alias)
    alias.update({ni + i: no + o for i, o in b.alias.items()})

    def both(which):
        def run(ins, outs, sems):
            getattr(a, which)(ins[:ni], outs[:no], sems[:ns])
            getattr(b, which)(ins[ni:], outs[no:], sems[ns:])
        return run

    return _Plan(list(a.ins) + list(b.ins), list(a.outs) + list(b.outs), alias, list(a.sems) + list(b.sems),
                 both("start"), both("finish"))


def _run_plan(plan, name):
    n_in, n_out = len(plan.ins), len(plan.outs)

    def body(*refs):
        parts = refs[:n_in], refs[n_in:n_in + n_out], refs[n_in + n_out:]
        plan.start(*parts)
        plan.finish(*parts)

    return pl.pallas_call(
        body, out_shape=plan.outs, in_specs=[ANY] * n_in, out_specs=[ANY] * n_out, scratch_shapes=plan.sems,
        input_output_aliases=plan.alias, name=name,
    )(*plan.ins)


def _pallas(body, *, name, args, in_specs, out_shape, out_specs, grid, scratch_shapes=(), plan=None, aliases=None,
            prefetch=()):
    n_pre, n_in, n_out, n_scr = len(prefetch), len(args), len(out_shape), len(scratch_shapes)
    plan = plan or _Plan([], [], {}, [], None, None)
    p_in, p_out = len(plan.ins), len(plan.outs)

    def carrying(*refs):
        pre, refs = refs[:n_pre], refs[n_pre:]
        ins, p_ins = refs[:n_in], refs[n_in:n_in + p_in]
        o0 = n_in + p_in
        outs, p_outs = refs[o0:o0 + n_out], refs[o0 + n_out:o0 + n_out + p_out]
        s0 = o0 + n_out + p_out
        scr, p_sems = refs[s0:s0 + n_scr], refs[s0 + n_scr:]
        ids = [pl.program_id(a) for a in range(len(grid))]

        if plan.start is not None:
            @pl.when(functools.reduce(jnp.logical_and, [i == 0 for i in ids]))
            def _():
                plan.start(p_ins, p_outs, p_sems)

        body(*pre, *ins, *outs, *scr)

        if plan.finish is not None:
            @pl.when(functools.reduce(jnp.logical_and, [i == g - 1 for i, g in zip(ids, grid)]))
            def _():
                plan.finish(p_ins, p_outs, p_sems)

    aliases = {n_pre + i: o for i, o in (aliases or {}).items()}
    aliases.update({n_pre + n_in + i: n_out + o for i, o in plan.alias.items()})
    res = pl.pallas_call(
        carrying, out_shape=list(out_shape) + list(plan.outs),
        grid_spec=pltpu.PrefetchScalarGridSpec(
            num_scalar_prefetch=n_pre, grid=grid, in_specs=list(in_specs) + [ANY] * p_in,
            out_specs=list(out_specs) + [ANY] * p_out, scratch_shapes=list(scratch_shapes) + list(plan.sems)),
        input_output_aliases=aliases, compiler_params=_params(), name=name,
    )(*prefetch, *args, *plan.ins)
    return list(res[:n_out]), list(res[n_out:])


def _plan_gather(stacks, relations=(0, 1, 2)):
    n = len(stacks)

    def ici_copies(outs, sems):
        x, y, c = _mesh_pos()
        chips = _other_chips(x, y)
        cps = []
        for w in range(n):
            own = outs[w].at[2 * x + y, _half_rows(stacks[w].shape[1], c)]
            cps += [_remote(own, own, sems[0].at[w, j], sems[1].at[w, j], (*chips[j], c)) for j in relations]
        return cps

    def start(ins, outs, sems):
        for cp in ici_copies(outs, sems):
            cp.start()

    def finish(ins, outs, sems):
        ici_send, ici_recv, d2d_send, d2d_recv = sems
        x, y, c = _mesh_pos()
        sibling = (x, y, 1 - c)
        slots = [2 * cx + cy for cx, cy in _other_chips(x, y)]
        forwards = []
        for w in range(n):
            rows = _half_rows(stacks[w].shape[1], c)
            for j in relations:
                landed = outs[w].at[slots[j], rows]
                _remote(landed, landed, ici_send.at[w, j], ici_recv.at[w, j], sibling).wait_recv()
                cp = _remote(landed, landed, d2d_send.at[w, j], d2d_recv.at[w, j], sibling)
                cp.start()
                forwards.append(cp)
        for w in range(n):
            rows = _half_rows(stacks[w].shape[1], 1 - c)
            for j in relations:
                landed = outs[w].at[slots[j], rows]
                _remote(landed, landed, d2d_send.at[w, j], d2d_recv.at[w, j], sibling).wait_recv()
        for cp in ici_copies(outs, sems) + forwards:
            cp.wait_send()

    return _Plan(stacks, [jax.ShapeDtypeStruct(s.shape, s.dtype) for s in stacks], {w: w for w in range(n)},
                 [pltpu.SemaphoreType.DMA((n, 3))] * 4, start, finish)


def _plan_sibling_halves(gs):
    n = len(gs)

    def copies(ins, outs, sems):
        x, y, c = _mesh_pos()
        return [_remote(ins[w].at[:, _half_rows(gs[w].shape[1], 1 - c), :], outs[w], sems[0].at[w], sems[1].at[w],
                        (x, y, 1 - c)) for w in range(n)]

    def start(ins, outs, sems):
        for cp in copies(ins, outs, sems):
            cp.start()

    def finish(ins, outs, sems):
        for cp in copies(ins, outs, sems):
            cp.wait()

    return _Plan(gs, [jax.ShapeDtypeStruct((g.shape[0], g.shape[1] // 2, g.shape[2]), g.dtype) for g in gs], {},
                 [pltpu.SemaphoreType.DMA((n,))] * 2, start, finish)


def _plan_chip_exchange(ps):
    n = len(ps)

    def copies(ins, outs, sems):
        x, y, c = _mesh_pos()
        return [_remote(ins[w].at[2 * cx + cy], outs[w].at[j], sems[0].at[w, j], sems[1].at[w, j], (cx, cy, c))
                for w in range(n) for j, (cx, cy) in enumerate(_other_chips(x, y))]

    def start(ins, outs, sems):
        for cp in copies(ins, outs, sems):
            cp.start()

    def finish(ins, outs, sems):
        for cp in copies(ins, outs, sems):
            cp.wait()

    return _Plan(ps, [jax.ShapeDtypeStruct((3,) + p.shape[1:], p.dtype) for p in ps], {},
                 [pltpu.SemaphoreType.DMA((n, 3))] * 2, start, finish)


def _plan_sibling_share(gs):
    n = len(gs)

    def copies(outs, sems, which):
        x, y, c = _mesh_pos()
        cps = []
        for w in range(n):
            rows = outs[w].at[_half_rows(gs[w].shape[0], c if which == "mine" else 1 - c)]
            cps.append(_remote(rows, rows, sems[0].at[w], sems[1].at[w], (x, y, 1 - c)))
        return cps

    def start(ins, outs, sems):
        for cp in copies(outs, sems, "mine"):
            cp.start()

    def finish(ins, outs, sems):
        for cp in copies(outs, sems, "mine"):
            cp.wait_send()
        for cp in copies(outs, sems, "theirs"):
            cp.wait_recv()

    return _Plan(gs, [jax.ShapeDtypeStruct(g.shape, g.dtype) for g in gs], {w: w for w in range(n)},
                 [pltpu.SemaphoreType.DMA((n,))] * 2, start, finish)


def _same_shape_groups(arrays):
    groups = {}
    for i, a in enumerate(arrays):
        groups.setdefault(a.shape, []).append(i)
    return list(groups.values())


def _add_sibling(gs, r1s, ids, tag):
    n = len(gs)
    nch, rh, cols = r1s[0].shape

    def body(ids_ref, *refs):
        for g_ref, r_ref, o_ref in zip(refs[:n], refs[n:2 * n], refs[2 * n:]):
            o_ref[...] = (g_ref[...] + r_ref[...]).astype(BF16)

    blk = lambda fn: pl.BlockSpec((None, rh, cols), fn)
    return pl.pallas_call(
        body, out_shape=[jax.ShapeDtypeStruct(r1s[0].shape, BF16)] * n,
        grid_spec=pltpu.PrefetchScalarGridSpec(
            num_scalar_prefetch=1, grid=(nch,),
            in_specs=[blk(lambda k, ids: (k, ids[1], 0))] * n + [blk(lambda k, ids: (k, 0, 0))] * n,
            out_specs=[blk(lambda k, ids: (k, 0, 0))] * n),
        compiler_params=_params(), name="add_sibling_" + tag,
    )(ids, *gs, *r1s)


def _add_chips(gs, r1s, r2s, ids, tag):
    n = len(gs)
    _, rh, cols = r1s[0].shape
    nb = 2 if rh % 32 == 0 else 1
    rb = rh // nb

    def body(ids_ref, *refs):
        for g_ref, r1_ref, r2_ref, o_ref in zip(refs[:n], refs[n:2 * n], refs[2 * n:3 * n], refs[3 * n:]):
            own = g_ref[...] + r1_ref[...]
            o_ref[...] = ((own + r2_ref[0].astype(F32)) + r2_ref[1].astype(F32)) + r2_ref[2].astype(F32)

    return pl.pallas_call(
        body, out_shape=[jax.ShapeDtypeStruct((2 * rh, cols), F32)] * n,
        grid_spec=pltpu.PrefetchScalarGridSpec(
            num_scalar_prefetch=1, grid=(nb,),
            in_specs=[pl.BlockSpec((None, rb, cols), lambda i, ids: (ids[0], ids[1] * nb + i, 0))] * n
            + [pl.BlockSpec((None, rb, cols), lambda i, ids: (ids[0], i, 0))] * n
            + [pl.BlockSpec((3, rb, cols), lambda i, ids: (0, i, 0))] * n,
            out_specs=[pl.BlockSpec((rb, cols), lambda i, ids: (ids[1] * nb + i, 0))] * n),
        compiler_params=_params(), name="add_chips_" + tag,
    )(ids, *gs, *r1s, *r2s)


VEC_ROWS = 8


N_DEVICES = 8


def _small_pack(part, d, width):
    names = ("ffn1_norm", "mix_norm", "ffn2_norm", "pool_scale", "out_norm_pool", "out_norm_attn", "qn", "kn", "b_forget",
             "pool_w", "loss")
    args = [part[k] for k in names]
    pw_shape = part["pool_w"].shape[1:]

    def body(g1_ref, gm_ref, g2_ref, ps_ref, onp_ref, ona_ref, qn_ref, kn_ref, bf_ref, pw_ref, loss_ref, vbuf, pbuf):
        lo = _head_masks()

        def fold_heads(ref):
            v = jnp.sum(ref[...], axis=0)
            acc = jnp.zeros((VEC_ROWS, LANES), F32)
            for blk in range(width // LANES):
                vb = jnp.broadcast_to(v[:, blk * LANES:(blk + 1) * LANES], (VEC_ROWS, LANES))
                acc = acc + vb + pltpu.roll(vb, HEAD_DIM, 1)
            return jnp.where(lo, acc, 0.0)[0:1, :]

        vbuf[0] = jnp.zeros((VEC_ROWS, d), F32)
        vbuf[0, 0:1, :] = jnp.sum(g1_ref[...], axis=0)
        vbuf[0, 1:2, :] = jnp.sum(gm_ref[...], axis=0)
        vbuf[0, 2:3, :] = jnp.sum(g2_ref[...], axis=0)
        vbuf[0, 5:6, 0:LANES] = jnp.sum(loss_ref[...], axis=0)[0:1, :]
        vbuf[0, 3:4, 0:width] = jnp.sum(ps_ref[...], axis=0)
        vbuf[0, 3:4, width:2 * width] = jnp.sum(onp_ref[...], axis=0)
        vbuf[0, 4:5, 0:width] = jnp.sum(ona_ref[...], axis=0)
        vbuf[0, 4:5, width:width + LANES] = fold_heads(qn_ref)
        vbuf[0, 4:5, width + LANES:width + 2 * LANES] = fold_heads(kn_ref)
        vbuf[0, 4:5, width + 2 * LANES:width + 3 * LANES] = jnp.sum(bf_ref[...], axis=0)
        pbuf[0] = jnp.sum(pw_ref[...], axis=0)

    return pl.pallas_call(
        body, out_shape=[jax.ShapeDtypeStruct((N_DEVICES, VEC_ROWS, d), F32), jax.ShapeDtypeStruct((N_DEVICES,) + pw_shape, F32)],
        in_specs=[VM] * len(args), out_specs=[VM, VM], compiler_params=_params(), name="small_pack",
    )(*args)


def _plan_all_to_all(stacks):
    n = len(stacks)

    def copies(outs, sems):
        x, y, c = _mesh_pos()
        cps = []
        for r in range(1, N_DEVICES):
            peer = (x if not r & 4 else 1 - x, y if not r & 2 else 1 - y, c if not r & 1 else 1 - c)
            cps += [_remote(outs[w].at[0], outs[w].at[r], sems[0].at[w, r - 1], sems[1].at[w, r - 1], peer) for w in range(n)]
        return cps

    def start(ins, outs, sems):
        for cp in copies(outs, sems):
            cp.start()

    def finish(ins, outs, sems):
        for cp in copies(outs, sems):
            cp.wait()

    return _Plan(stacks, [jax.ShapeDtypeStruct(s.shape, s.dtype) for s in stacks], {w: w for w in range(n)},
                 [pltpu.SemaphoreType.DMA((n, N_DEVICES - 1))] * 2, start, finish)


def _small_sum(vstack, pstack, me):
    def body(me_ref, vbuf, pbuf, vec_ref, pw_ref):
        vec = vbuf[me_ref[0]]
        pw = pbuf[me_ref[0]]
        for dev in range(1, N_DEVICES):
            vec = vec + vbuf[jnp.bitwise_xor(me_ref[0], dev)]
            pw = pw + pbuf[jnp.bitwise_xor(me_ref[0], dev)]
        vec_ref[...] = vec
        pw_ref[...] = pw

    full = lambda s: pl.BlockSpec(s.shape, lambda i, me: (0,) * len(s.shape))
    outs = [jax.ShapeDtypeStruct(vstack.shape[1:], F32), jax.ShapeDtypeStruct(pstack.shape[1:], F32)]
    return pl.pallas_call(
        body, out_shape=outs,
        grid_spec=pltpu.PrefetchScalarGridSpec(num_scalar_prefetch=1, grid=(1,), in_specs=[full(vstack), full(pstack)],
                                               out_specs=[full(o) for o in outs]),
        compiler_params=_params(), name="small_sum",
    )(me, vstack, pstack)


def _adamw(ws, gs, ms, vs, tag, plan=None):
    n = len(ws)
    rows, cols = ws[0].shape
    rb = rows
    while rb * cols * 4 * n > (1 << 20) and rb % 16 == 0:
        rb //= 2

    def body(*refs):
        for j in range(n):
            w_ref, g_ref, m_ref, v_ref = (refs[k * n + j] for k in range(4))
            d_ref, mo_ref, vo_ref = (refs[(4 + k) * n + j] for k in range(3))
            gv = g_ref[...]
            m2 = ADAM_B1 * m_ref[...] + (1.0 - ADAM_B1) * gv
            v2 = ADAM_B2 * v_ref[...] + (1.0 - ADAM_B2) * (gv * gv)
            m_hat = m2 / (1.0 - ADAM_B1 ** ADAM_STEP)
            v_hat = v2 / (1.0 - ADAM_B2 ** ADAM_STEP)
            d_ref[...] = -ADAM_LR * (m_hat / (jnp.sqrt(v_hat) + ADAM_EPS) + ADAM_WD * w_ref[...])
            mo_ref[...] = m2
            vo_ref[...] = v2

    spec = pl.BlockSpec((rb, cols), lambda i: (i, 0))
    res, carried = _pallas(
        body, name="adamw_" + tag, args=[*ws, *gs, *ms, *vs], out_shape=[jax.ShapeDtypeStruct(ws[0].shape, F32)] * (3 * n),
        grid=(rows // rb,), in_specs=[spec] * (4 * n), out_specs=[spec] * (3 * n), plan=plan)
    return [(res[j], res[n + j], res[2 * n + j]) for j in range(n)], carried


def _pack_vec(p, d, width):
    pad = lambda v: jnp.pad(v, (0, LANES - v.shape[0]))
    row3 = jnp.concatenate([p["pool_scale"], p["out_norm_pool"]])
    row4 = jnp.concatenate([p["out_norm_attn"], pad(p["q_norm"]), pad(p["k_norm"]), pad(p["b_forget"]),
                            jnp.zeros((d - width - 3 * LANES,), F32)])
    rows = [p["ffn1_norm"], p["mix_norm"], p["ffn2_norm"], row3, row4]
    return jnp.pad(jnp.stack(rows), ((0, VEC_ROWS - len(rows)), (0, 0)))


def _unpack_vec(vec, width):
    return dict(ffn1_norm=vec[0], mix_norm=vec[1], ffn2_norm=vec[2], pool_scale=vec[3, :width],
                out_norm_pool=vec[3, width:2 * width], out_norm_attn=vec[4, :width],
                q_norm=vec[4, width:width + HEAD_DIM], k_norm=vec[4, width + LANES:width + LANES + HEAD_DIM],
                b_forget=vec[4, width + 2 * LANES:width + 2 * LANES + N_HEADS])


WEIGHT_NAMES = ("ffn1_norm", "ffn1_w_gate", "ffn1_w_up", "ffn1_w_down", "mix_norm", "w_in", "b_forget", "pool_w",
                "pool_scale", "q_norm", "k_norm", "out_norm_pool", "out_norm_attn", "w_out", "ffn2_norm",
                "ffn2_w_gate", "ffn2_w_up", "ffn2_w_down")
BIG_NAMES = ("ffn1_w_gate", "ffn1_w_up", "ffn1_w_down", "w_in", "w_out", "ffn2_w_gate", "ffn2_w_up", "ffn2_w_down")
TRANSPOSED_NAMES = ("ffn1_w_gate", "ffn1_w_up", "w_in", "ffn2_w_gate", "ffn2_w_up")
FFN1_NAMES = ("ffn1_w_gate", "ffn1_w_up", "ffn1_w_down")
MIX_NAMES = ("w_in", "w_out")
FFN2_NAMES = ("ffn2_w_gate", "ffn2_w_up", "ffn2_w_down")


def kernel(x, ffn1_norm, ffn1_w_gate, ffn1_w_up, ffn1_w_down, mix_norm, w_in, b_forget, pool_w, pool_scale, q_norm, k_norm, out_norm_pool, out_norm_attn, w_out, ffn2_norm, ffn2_w_gate, ffn2_w_up, ffn2_w_down, loss_target, m_ffn1_norm, m_ffn1_w_gate, m_ffn1_w_up, m_ffn1_w_down, m_mix_norm, m_w_in, m_b_forget, m_pool_w, m_pool_scale, m_q_norm, m_k_norm, m_out_norm_pool, m_out_norm_attn, m_w_out, m_ffn2_norm, m_ffn2_w_gate, m_ffn2_w_up, m_ffn2_w_down, v_ffn1_norm, v_ffn1_w_gate, v_ffn1_w_up, v_ffn1_w_down, v_mix_norm, v_w_in, v_b_forget, v_pool_w, v_pool_scale, v_q_norm, v_k_norm, v_out_norm_pool, v_out_norm_attn, v_w_out, v_ffn2_norm, v_ffn2_w_gate, v_ffn2_w_up, v_ffn2_w_down):
    given = dict(locals())
    w = {n: given[n] for n in WEIGHT_NAMES}
    m = {n: given["m_" + n] for n in WEIGHT_NAMES}
    v = {n: given["v_" + n] for n in WEIGHT_NAMES}
    n_batch, seq, d = x.shape
    width = pool_scale.shape[0]
    in_rows = w_in.shape[1]
    in_cols = N_CHIPS * in_rows
    in_pad = -(-in_rows // 32) * 32
    in_cols_pad = in_cols - N_HEADS + LANES

    work = lambda a, n: a.T if n in TRANSPOSED_NAMES else a
    exchanged = lambda a, n: jnp.pad(a, ((0, in_pad - in_rows), (0, 0))) if n == "w_in" else a

    mesh_x, mesh_y, mesh_c = _mesh_pos()
    ids = jnp.stack([2 * mesh_x + mesh_y, mesh_c]).astype(jnp.int32)

    row = lambda a: a.reshape(1, -1)
    g1, gm, g2, ps, onp, ona = (row(a) for a in (ffn1_norm, mix_norm, ffn2_norm, pool_scale, out_norm_pool, out_norm_attn))
    qn, kn = row(jnp.tile(q_norm, N_HEADS)), row(jnp.tile(k_norm, N_HEADS))
    bf = row(jnp.pad(b_forget, (0, LANES - N_HEADS)))
    pwb = pool_w.astype(BF16)
    xf, tgt = x.reshape(n_batch * seq, d), loss_target.reshape(n_batch * seq, d)

    def grouped(call, names, *lists):
        out = [None] * len(names)
        for idx in _same_shape_groups(lists[0]):
            res = call(*[[lst[i] for i in idx] for lst in lists], names[idx[0]])
            for i, r in zip(idx, res):
                out[i] = r
        return out

    placed = dict(zip(BIG_NAMES, grouped(lambda ws, tag: _place_cast(ws, ids, tag), BIG_NAMES,
                                         [exchanged(work(w[n], n), n) for n in BIG_NAMES])))
    ffn1 = [placed[n] for n in FFN1_NAMES]
    part, ffn1 = _ffn_fwd(xf, g1, *ffn1, plan=_plan_gather(ffn1, relations=(0, 1)), part=(ids, (0,), None))
    part, arrived = _ffn_fwd(part[0], g1, *ffn1, part=(ids, (2, 1), part[1:]),
                             plan=_merge_plans(_plan_gather(ffn1, relations=(2,)),
                                               _plan_gather([placed[n] for n in MIX_NAMES])))
    (wg1, wu1, wd1), (w_in_all, w_out_all) = arrived[:3], arrived[3:]
    (x1, h1, a1, b1, s1), _ = _ffn_fwd(part[0], g1, wg1, wu1, wd1, part=(ids, (3,), part[1:]))
    w_in_t = jnp.pad(w_in_all[:, :in_rows].reshape(in_cols, d), ((0, in_cols_pad - in_cols), (0, 0)))
    w_out_full = w_out_all.reshape(N_CHIPS * w_out.shape[0], d)
    woa, wob = w_out_full[:width], w_out_full[width:]

    (hm, pv, q, k, qh, kh, vb, f), _ = _mix_proj(x1, gm, w_in_t, qn, kn, width, width)
    qa, ka = _forget_prefix(f, bf, qh, kh, n_batch, seq)
    yp = _pool_fwd(pv, pwb, ps, onp, n_batch, seq)
    (o, lse), (wg2, wu2, wd2) = _attn_fwd(qa, ka, vb, n_batch, seq, plan=_plan_gather([placed[n] for n in FFN2_NAMES]))
    x2, ya = _mix_out(x1, yp, o, ona, woa, wob)
    (dy, h2, a2, b2, s2, lpart), _ = _ffn_fwd(x2, g2, wg2, wu2, wd2, target=tgt)

    def to_chips(gs, arrived, tags):
        return grouped(lambda g, r, tag: _add_sibling(g, r, ids, tag), tags, gs, arrived)

    def own_rows(gs, from_sibling, from_chips, tags):
        return grouped(lambda g, ra, rb, tag: _add_chips(g, ra, rb, ids, tag), tags, gs, from_sibling, from_chips)

    (dx2, da2, db2, dg2), _ = _ffn_bwd_x(dy, x2, g2, a2, b2, wg2, wu2, wd2, "ffn2_bwd_x")
    dw2, _ = _ffn_bwd_w([(da2, h2, 1.0), (db2, h2, 1.0), (s2, dy, 0.5)], "ffn2_bwd_w")
    (dyp, do, delta, dwoa, dwob, dona), sib2 = _mix_out_bwd(dx2, o, yp, ya, ona, woa, wob, plan=_plan_sibling_halves(dw2))
    dpv, dpw, dps, donp = _pool_bwd(pv, dyp, pwb, ps, onp, n_batch, seq)
    (dqh, dfq), chips2 = _attn_bwd_q(qa, ka, vb, do, lse, delta, n_batch, seq,
                                     plan=_plan_chip_exchange(to_chips(dw2, sib2, FFN2_NAMES)))
    (dkh, dv, dfk), red2 = _attn_bwd_kv(qa, ka, vb, do, lse, delta, n_batch, seq,
                                        plan=_plan_sibling_share(own_rows(dw2, sib2, chips2, FFN2_NAMES)))
    df, dbf = _forget_bwd(dfq, dfk, f, bf, n_batch, seq)
    dx1, dw_in_t, dgm, dqn, dkn = _mix_in_bwd(dx2, x1, gm, hm, dpv, dqh, q, dkh, k, dv, df, qn, kn, w_in_t)
    in_base = [in_rows * k // 8 * 8 for k in range(N_CHIPS)]
    d_w_in = jnp.stack([dw_in_t[b:b + in_pad] for b in in_base])
    d_w_out = jnp.concatenate([dwoa, dwob], axis=0).reshape(N_CHIPS, w_out.shape[0], d)
    dwm = [d_w_in, d_w_out]
    down, gate_up = FFN1_NAMES[2:], FFN1_NAMES[:2]
    dwd1, sibm = _ffn_bwd_w([(s1, dx1, 0.5)], "ffn1_bwd_w_down", plan=_plan_sibling_halves(dwm))
    (da1, db1), arrived = _ffn_bwd_a(dx1, a1, b1, wd1, "ffn1_bwd_a",
                                     plan=_merge_plans(_plan_sibling_halves(dwd1),
                                                       _plan_chip_exchange(to_chips(dwm, sibm, MIX_NAMES))))
    sibd, chipsm = arrived[:1], arrived[1:]
    dwgu1, chipsd = _ffn_bwd_w([(da1, h1, 1.0), (db1, h1, 1.0)], "ffn1_bwd_w_gate_up",
                               plan=_plan_chip_exchange(to_chips(dwd1, sibd, down)))
    n_tiles = (n_batch * seq) // min(FFN_TILE, n_batch * seq)
    first = max(n_tiles // 4, 1)
    begun, sibgu = _ffn_bwd_h(dx1, xf, g1, da1, db1, wg1, wu1, "ffn1_bwd_h_first", (0, first),
                              plan=_plan_sibling_halves(dwgu1))
    (gx, dg1), chipsgu = _ffn_bwd_h(dx1, xf, g1, da1, db1, wg1, wu1, "ffn1_bwd_h_rest", (first, n_tiles), prev=begun,
                                    plan=_plan_chip_exchange(to_chips(dwgu1, sibgu, gate_up)))

    part = dict(ffn1_norm=dg1, mix_norm=dgm, ffn2_norm=dg2, b_forget=dbf, pool_scale=dps, out_norm_pool=donp,
                out_norm_attn=dona, qn=dqn, kn=dkn, pool_w=dpw.reshape(n_batch, -1, pool_w.shape[-1]), loss=lpart)
    mine = (own_rows(dwgu1, sibgu, chipsgu, gate_up) + own_rows(dwd1, sibd, chipsd, down)
            + own_rows(dwm, sibm, chipsm, MIX_NAMES))
    last = _run_plan(_merge_plans(_plan_sibling_share(mine), _plan_all_to_all(_small_pack(part, d, width))), "last_exchange")
    vstack, pstack = last[len(mine):]
    g_vec, g_pw = _small_sum(vstack, pstack, jnp.reshape(4 * mesh_x + 2 * mesh_y + mesh_c, (1,)).astype(jnp.int32))
    loss = g_vec[5, 0]
    reduced = dict(zip(FFN1_NAMES + MIX_NAMES + FFN2_NAMES, list(last[:len(mine)]) + list(red2)))
    reduced["w_in"] = lax.dynamic_slice(reduced["w_in"], ((in_rows * ids[0]) % 8, 0), (in_rows, d))

    grads, delta, new_m, new_v = {}, {}, {}, {}
    for names in (FFN2_NAMES, FFN1_NAMES, ("w_in",), ("w_out",)):
        stepped, _ = _adamw([work(w[n], n) for n in names], [reduced[n] for n in names], [work(m[n], n) for n in names],
                            [work(v[n], n) for n in names], names[0])
        for n, step in zip(names, stepped):
            grads[n], delta[n], new_m[n], new_v[n] = (work(a, n) for a in (reduced[n], *step))
    flat_pw = lambda a: a.reshape(-1, a.shape[-1])
    ((d_pw, m_pw, v_pw),), _ = _adamw([flat_pw(pool_w)], [g_pw], [flat_pw(m_pool_w)], [flat_pw(v_pool_w)], "pool_w")
    ((d_vec, m_vec, v_vec),), _ = _adamw([_pack_vec(w, d, width)], [g_vec], [_pack_vec(m, d, width)],
                                         [_pack_vec(v, d, width)], "vectors")
    grads.update(_unpack_vec(g_vec, width), pool_w=g_pw.reshape(pool_w.shape))
    delta.update(_unpack_vec(d_vec, width), pool_w=d_pw.reshape(pool_w.shape))
    new_m.update(_unpack_vec(m_vec, width), pool_w=m_pw.reshape(pool_w.shape))
    new_v.update(_unpack_vec(v_vec, width), pool_w=v_pw.reshape(pool_w.shape))
    return (loss, gx.reshape(x.shape), *[grads[n] for n in WEIGHT_NAMES], *[delta[n] for n in WEIGHT_NAMES],
            *[new_m[n] for n in WEIGHT_NAMES], *[new_v[n] for n in WEIGHT_NAMES])
```

```python
import functools

import jax
import jax.numpy as jnp
from jax import lax
from jax.experimental import pallas as pl
from jax.experimental.pallas import tpu as pltpu

F32 = jnp.float32
BF16 = jnp.bfloat16
EPS = 1e-6
NEG = -1e30
ADAM_LR = 0.001
ADAM_B1 = 0.9
ADAM_B2 = 0.999
ADAM_EPS = 1e-08
ADAM_WD = 0.01
ADAM_STEP = 10
POOL_WINDOWS = (2, 4, 8, 16)
HEAD_DIM = 64
N_HEADS = 8
LANES = 128
N_CHIPS = 4
ATT_BLOCK = 512
ATT_SUB = 128
FFN_TILE = 1024
VMEM_LIMIT = 62 * 1024 * 1024
MESH_AXES = ("x", "y", "c")
ANY = pl.BlockSpec(memory_space=pl.ANY)
VM = pl.BlockSpec(memory_space=pltpu.VMEM)


def _params(**kw):
    return pltpu.CompilerParams(vmem_limit_bytes=VMEM_LIMIT, **kw)


def _dot(a, b):
    return jnp.dot(a, b, preferred_element_type=F32)


def _dot_nt(a, b):
    return lax.dot_general(a, b, (((1,), (1,)), ((), ())), preferred_element_type=F32)


def _dot_tn(a, b):
    return lax.dot_general(a, b, (((0,), (0,)), ((), ())), preferred_element_type=F32)


def _sigmoid(z):
    return 1.0 / (1.0 + jnp.exp(-z))


def _rms(xf):
    return lax.rsqrt(jnp.mean(xf * xf, axis=-1, keepdims=True) + EPS)


def _rms_bwd(xf, r, gain, dh):
    xh = xf * r
    dyg = dh * gain
    return r * (dyg - xh * jnp.mean(dyg * xh, axis=-1, keepdims=True)), dh * xh


def _total(v):
    return jnp.sum(jnp.sum(v, axis=1, keepdims=True), axis=0, keepdims=True)


def _ffn_fwd(x, gain, wg, wu, wd, target=None, plan=None):
    t, d = x.shape
    nch, fc, _ = wg.shape
    tm = min(FFN_TILE, t)
    nt = t // tm
    with_loss = target is not None

    def body(*refs):
        if with_loss:
            x_ref, g_ref, wg_ref, wu_ref, wd_ref, t_ref, o_ref, h_ref, a_ref, b_ref, s_ref, l_ref, acc_ref = refs
        else:
            x_ref, g_ref, wg_ref, wu_ref, wd_ref, o_ref, h_ref, a_ref, b_ref, s_ref, acc_ref = refs
        k = pl.program_id(1)

        @pl.when(k == 0)
        def _():
            xf = x_ref[...]
            h_ref[...] = ((xf * _rms(xf)) * g_ref[...]).astype(BF16)
            acc_ref[...] = jnp.zeros_like(acc_ref)

        for rows in _row_halves(tm):
            h = h_ref[rows, :]
            a = _dot_nt(h, wg_ref[...])
            b = _dot_nt(h, wu_ref[...])
            sb = ((a * (0.5 * jnp.tanh(0.5 * a) + 0.5)) * b).astype(BF16)
            a_ref[rows, :] = a.astype(BF16)
            b_ref[rows, :] = b.astype(BF16)
            s_ref[rows, :] = sb
            acc_ref[rows, :] += _dot(sb, wd_ref[...])

        @pl.when(k == nch - 1)
        def _():
            y = x_ref[...] + 0.5 * acc_ref[...]
            if with_loss:
                e = y - t_ref[...]
                o_ref[...] = e * (1.0 / d)
                l_ref[...] = jnp.broadcast_to(_total(e * e) * (0.5 / d), l_ref.shape)
            else:
                o_ref[...] = y

    row = pl.BlockSpec((tm, d), lambda i, k: (i, 0))
    chunk = pl.BlockSpec((None, fc, d), lambda i, k: (k, 0, 0))
    act = pl.BlockSpec((None, tm, fc), lambda i, k: (k, i, 0))
    in_specs = [row, pl.BlockSpec((1, d), lambda i, k: (0, 0)), chunk, chunk, chunk]
    out_shape = [jax.ShapeDtypeStruct((t, d), F32), jax.ShapeDtypeStruct((t, d), BF16)]
    out_shape += [jax.ShapeDtypeStruct((nch, t, fc), BF16)] * 3
    out_specs = [row, row, act, act, act]
    args = [x, gain, wg, wu, wd]
    if with_loss:
        in_specs.append(row)
        args.append(target)
        out_shape.append(jax.ShapeDtypeStruct((nt, 8, LANES), F32))
        out_specs.append(pl.BlockSpec((None, 8, LANES), lambda i, k: (i, 0, 0)))
    return _pallas(body, name="ffn_fwd_loss" if with_loss else "ffn_fwd", args=args, in_specs=in_specs,
                   out_shape=out_shape, out_specs=out_specs, grid=(nt, nch),
                   scratch_shapes=[pltpu.VMEM((tm, d), F32)], plan=plan)


def _row_halves(n):
    return [slice(0, n // 2), slice(n // 2, n)]


def _swiglu_grads(dy_ref, a_ref, b_ref, wd_ref, rows):
    ds = _dot_nt(dy_ref[rows, :].astype(BF16), wd_ref[...])
    av = a_ref[rows, :].astype(F32)
    bv = b_ref[rows, :].astype(F32)
    th = jnp.tanh(0.5 * av)
    half_sig = 0.25 * th + 0.25
    dab = ((ds * bv) * (half_sig * (1.0 + av * (0.5 - 0.5 * th)))).astype(BF16)
    return dab, (ds * (av * half_sig)).astype(BF16)


def _ffn_bwd_a(dy, a, b, wd, name, plan=None):
    t, d = dy.shape
    nch, fc, _ = wd.shape
    tm = min(FFN_TILE, t)

    def body(dy_ref, a_ref, b_ref, wd_ref, da_ref, db_ref):
        for rows in _row_halves(tm):
            da_ref[rows, :], db_ref[rows, :] = _swiglu_grads(dy_ref, a_ref, b_ref, wd_ref, rows)

    act = pl.BlockSpec((None, tm, fc), lambda i, k: (k, i, 0))
    return _pallas(
        body, name=name, args=[dy, a, b, wd], out_shape=[jax.ShapeDtypeStruct((nch, t, fc), BF16)] * 2, grid=(t // tm, nch),
        in_specs=[pl.BlockSpec((tm, d), lambda i, k: (i, 0)), act, act, pl.BlockSpec((None, fc, d), lambda i, k: (k, 0, 0))],
        out_specs=[act, act], plan=plan)


def _ffn_bwd_h(dy, x, gain, da, db, wg, wu, name, tiles, prev=None, plan=None):
    t, d = x.shape
    nch, fc, _ = wg.shape
    tm = min(FFN_TILE, t)
    nt = t // tm
    t0, t1 = tiles

    def body(*refs):
        dy_ref, x_ref, g_ref, da_ref, db_ref, wg_ref, wu_ref = refs[:7]
        dx_ref, dg_ref, acc_ref = refs[-3:]
        k = pl.program_id(1)

        @pl.when(k == 0)
        def _():
            acc_ref[...] = jnp.zeros_like(acc_ref)

        acc_ref[...] += _dot(da_ref[...], wg_ref[...]) + _dot(db_ref[...], wu_ref[...])

        @pl.when(k == nch - 1)
        def _():
            xf = x_ref[...]
            dxn, dgr = _rms_bwd(xf, _rms(xf), g_ref[...], acc_ref[...])
            dx_ref[...] = dy_ref[...] + dxn
            dg_ref[...] = jnp.sum(dgr, axis=0, keepdims=True)

    row = pl.BlockSpec((tm, d), lambda i, k: (i + t0, 0))
    chunk = pl.BlockSpec((None, fc, d), lambda i, k: (k, 0, 0))
    act = pl.BlockSpec((None, tm, fc), lambda i, k: (k, i + t0, 0))
    args = [dy, x, gain, da, db, wg, wu]
    in_specs = [row, row, pl.BlockSpec((1, d), lambda i, k: (0, 0)), act, act, chunk, chunk]
    aliases = {}
    if prev is not None:
        aliases = {len(args): 0, len(args) + 1: 1}
        args += list(prev)
        in_specs += [ANY, ANY]
    return _pallas(
        body, name=name, args=args, out_shape=[jax.ShapeDtypeStruct((t, d), F32), jax.ShapeDtypeStruct((nt, 1, d), F32)],
        grid=(t1 - t0, nch), in_specs=in_specs,
        out_specs=[row, pl.BlockSpec((None, 1, d), lambda i, k: (i + t0, 0, 0))],
        scratch_shapes=[pltpu.VMEM((tm, d), F32)], plan=plan, aliases=aliases)


def _ffn_bwd_x(dy, x, gain, a, b, wg, wu, wd, name, plan=None):
    t, d = x.shape
    nch, fc, _ = wg.shape
    tm = min(FFN_TILE, t)
    nt = t // tm

    def body(dy_ref, x_ref, g_ref, a_ref, b_ref, wg_ref, wu_ref, wd_ref, dx_ref, da_ref, db_ref, dg_ref, acc_ref):
        k = pl.program_id(1)

        @pl.when(k == 0)
        def _():
            acc_ref[...] = jnp.zeros_like(acc_ref)

        for rows in _row_halves(tm):
            dab, dbb = _swiglu_grads(dy_ref, a_ref, b_ref, wd_ref, rows)
            da_ref[rows, :] = dab
            db_ref[rows, :] = dbb
            acc_ref[rows, :] += _dot(dab, wg_ref[...]) + _dot(dbb, wu_ref[...])

        @pl.when(k == nch - 1)
        def _():
            xf = x_ref[...]
            dxn, dgr = _rms_bwd(xf, _rms(xf), g_ref[...], acc_ref[...])
            dx_ref[...] = dy_ref[...] + dxn
            dg_ref[...] = jnp.sum(dgr, axis=0, keepdims=True)

    row = pl.BlockSpec((tm, d), lambda i, k: (i, 0))
    chunk = pl.BlockSpec((None, fc, d), lambda i, k: (k, 0, 0))
    act = pl.BlockSpec((None, tm, fc), lambda i, k: (k, i, 0))
    return _pallas(
        body, name=name, args=[dy, x, gain, a, b, wg, wu, wd],
        out_shape=[jax.ShapeDtypeStruct((t, d), F32), jax.ShapeDtypeStruct((nch, t, fc), BF16),
                   jax.ShapeDtypeStruct((nch, t, fc), BF16), jax.ShapeDtypeStruct((nt, 1, d), F32)],
        grid=(nt, nch),
        in_specs=[row, row, pl.BlockSpec((1, d), lambda i, k: (0, 0)), act, act, chunk, chunk, chunk],
        out_specs=[row, act, act, pl.BlockSpec((None, 1, d), lambda i, k: (i, 0, 0))],
        scratch_shapes=[pltpu.VMEM((tm, d), F32)], plan=plan)


def _ffn_bwd_w(pairs, name, plan=None):
    n = len(pairs)
    nch, t, fc = pairs[0][0].shape
    d = pairs[0][1].shape[1]
    tm = min(1024, t)

    def body(*refs):
        @pl.when(pl.program_id(1) == 0)
        def _():
            for o_ref in refs[2 * n:]:
                o_ref[...] = jnp.zeros_like(o_ref)

        for j, (_, _, scale) in enumerate(pairs):
            other = refs[n + j][...]
            if other.dtype != BF16:
                other = (scale * other).astype(BF16)
            refs[2 * n + j][...] += _dot_tn(refs[j][...], other)

    row = pl.BlockSpec((tm, d), lambda k, i: (i, 0))
    act = pl.BlockSpec((None, tm, fc), lambda k, i: (k, i, 0))
    chunk = pl.BlockSpec((None, fc, d), lambda k, i: (k, 0, 0))
    return _pallas(body, name=name, args=[p[0] for p in pairs] + [p[1] for p in pairs],
                   out_shape=[jax.ShapeDtypeStruct((nch, fc, d), F32)] * n, grid=(nch, t // tm),
                   in_specs=[act] * n + [row] * n, out_specs=[chunk] * n, plan=plan)


def _head_masks():
    lane = lax.broadcasted_iota(jnp.int32, (1, LANES), 1)
    return lane < HEAD_DIM


def _head_rms(x, lo):
    x2 = x * x
    s0 = jnp.sum(jnp.where(lo, x2, 0.0), axis=1, keepdims=True)
    s1 = jnp.sum(jnp.where(lo, 0.0, x2), axis=1, keepdims=True)
    return jnp.where(lo, lax.rsqrt(s0 * (1.0 / HEAD_DIM) + EPS), lax.rsqrt(s1 * (1.0 / HEAD_DIM) + EPS))


def _head_mean(v, lo):
    s0 = jnp.sum(jnp.where(lo, v, 0.0), axis=1, keepdims=True)
    s1 = jnp.sum(jnp.where(lo, 0.0, v), axis=1, keepdims=True)
    return jnp.where(lo, s0, s1) * (1.0 / HEAD_DIM)


def _mix_proj(x1, gain, wt, qn, kn, pool_width, attn_width, plan=None):
    t, d = x1.shape
    tm = min(512, t)
    nt = t // tm
    scale = HEAD_DIM ** -0.5
    c_q, c_k, c_v = pool_width, pool_width + attn_width, pool_width + 2 * attn_width
    c_f = c_v + attn_width

    def body(x_ref, g_ref, wt_ref, qn_ref, kn_ref, hm_ref, pv_ref, q_ref, k_ref, qh_ref, kh_ref, vb_ref, f_ref):
        lo = _head_masks()
        for rows in _row_halves(tm):
            xf = x_ref[rows, :]
            hm = ((xf * _rms(xf)) * g_ref[...]).astype(BF16)
            hm_ref[rows, :] = hm
            f_ref[rows, :] = _dot_nt(hm, wt_ref[c_f:c_f + LANES, :])
            pv_ref[rows, :] = _dot_nt(hm, wt_ref[0:pool_width, :])
            vb_ref[rows, :] = _dot_nt(hm, wt_ref[c_v:c_v + attn_width, :]).astype(BF16)
            for c0, raw_ref, hat_ref, n_ref, mul in ((c_q, q_ref, qh_ref, qn_ref, scale), (c_k, k_ref, kh_ref, kn_ref, 1.0)):
                raw = _dot_nt(hm, wt_ref[c0:c0 + attn_width, :])
                raw_ref[rows, :] = raw
                for blk in range(attn_width // LANES):
                    sl = slice(blk * LANES, (blk + 1) * LANES)
                    xb = raw[:, sl]
                    hat_ref[rows, sl] = (((xb * _head_rms(xb, lo)) * n_ref[:, sl]) * mul).astype(BF16)

    row = pl.BlockSpec((tm, d), lambda i: (i, 0))
    half = pl.BlockSpec((tm, attn_width), lambda i: (i, 0))
    const = lambda shape: pl.BlockSpec(shape, lambda i: (0, 0))
    return _pallas(
        body, name="mix_proj", args=[x1, gain, wt, qn, kn],
        out_shape=[jax.ShapeDtypeStruct((t, d), BF16), jax.ShapeDtypeStruct((t, pool_width), F32),
                   jax.ShapeDtypeStruct((t, attn_width), F32), jax.ShapeDtypeStruct((t, attn_width), F32),
                   jax.ShapeDtypeStruct((t, attn_width), BF16), jax.ShapeDtypeStruct((t, attn_width), BF16),
                   jax.ShapeDtypeStruct((t, attn_width), BF16), jax.ShapeDtypeStruct((t, LANES), F32)],
        grid=(nt,),
        in_specs=[row, const((1, d)), const(wt.shape), const((1, attn_width)), const((1, attn_width))],
        out_specs=[row, pl.BlockSpec((tm, pool_width), lambda i: (i, 0)), half, half, half, half, half,
                   pl.BlockSpec((tm, LANES), lambda i: (i, 0))], plan=plan)


def _shift_down(v, dist, row):
    return jnp.where(row >= dist, pltpu.roll(v, dist, 0), 0.0)


def _shift_up(v, dist, row, n):
    return jnp.where(row + dist < n, pltpu.roll(v, n - dist, 0), 0.0)


def _aug_lane(e):
    return HEAD_DIM if e == 0 else 0


def _forget_prefix(f, bias, qh, kh, n_batch, seq):
    def body(f_ref, b_ref, q_ref, k_ref, qa_ref, ka_ref):
        z = f_ref[...] + b_ref[...]
        acc = jnp.minimum(z, 0.0) - jnp.log(1.0 + jnp.exp(-jnp.abs(z)))
        row = lax.broadcasted_iota(jnp.int32, (seq, 1), 0)
        dist = 1
        while dist < seq:
            acc = acc + _shift_down(acc, dist, row)
            dist *= 2
        lane = lax.broadcasted_iota(jnp.int32, (1, LANES), 1)
        for h in range(N_HEADS):
            pair, e = divmod(h, 2)
            a0 = _aug_lane(e)
            own = (lane < HEAD_DIM) if e == 0 else (lane >= HEAD_DIM)
            fh = _pick_lane(acc, h)
            hi = fh.astype(BF16).astype(F32)
            rest = fh - hi
            mid = rest.astype(BF16).astype(F32)
            low = rest - mid
            q_ones = (lane >= a0 + 3) & (lane < a0 + 6)
            k_ones = (lane >= a0) & (lane < a0 + 3)
            q_aug = jnp.where(lane == a0, hi, jnp.where(lane == a0 + 1, mid, jnp.where(lane == a0 + 2, low,
                              jnp.where(q_ones, 1.0, 0.0))))
            k_aug = jnp.where(k_ones, 1.0, jnp.where(lane == a0 + 3, -hi, jnp.where(lane == a0 + 4, -mid,
                              jnp.where(lane == a0 + 5, -low, 0.0))))
            src = slice(pair * LANES, (pair + 1) * LANES)
            dst = slice(h * LANES, (h + 1) * LANES)
            qa_ref[:, dst] = jnp.where(own, q_ref[:, src].astype(F32), q_aug).astype(BF16)
            ka_ref[:, dst] = jnp.where(own, k_ref[:, src].astype(F32), k_aug).astype(BF16)

    width = qh.shape[1]
    tok = pl.BlockSpec((seq, width), lambda b: (b, 0))
    aug = pl.BlockSpec((seq, N_HEADS * LANES), lambda b: (b, 0))
    return pl.pallas_call(
        body, out_shape=[jax.ShapeDtypeStruct((n_batch * seq, N_HEADS * LANES), BF16)] * 2, grid=(n_batch,),
        in_specs=[pl.BlockSpec((seq, LANES), lambda b: (b, 0)), pl.BlockSpec((1, LANES), lambda b: (0, 0)), tok, tok],
        out_specs=[aug, aug], compiler_params=_params(), name="forget_prefix",
    )(f, bias, qh, kh)


def _pool_groups(pv_ref, pw_ref, ps_ref, seq):
    row = lax.broadcasted_iota(jnp.int32, (seq, 1), 0)
    pos = (row + 1).astype(F32)
    out = []
    for g, win in enumerate(POOL_WINDOWS):
        sl = slice(g * LANES, (g + 1) * LANES)
        xg = pv_ref[:, sl]
        acc = xg
        dist = 1
        while dist < win:
            acc = acc + _shift_down(acc, dist, row)
            dist *= 2
        pooled = (acc / jnp.minimum(pos, float(win)) - xg).astype(BF16)
        mixed = _dot(pooled, pw_ref[g])
        out.append((pooled, mixed, mixed * ps_ref[:, sl]))
    return out


def _pool_fwd(pv, pw, ps, onp, n_batch, seq):
    width = pv.shape[1]

    def body(pv_ref, pw_ref, ps_ref, on_ref, y_ref):
        groups = _pool_groups(pv_ref, pw_ref, ps_ref, seq)
        ssq = sum(jnp.sum(ms * ms, axis=1, keepdims=True) for _, _, ms in groups)
        r = lax.rsqrt(ssq * (1.0 / width) + EPS)
        for g, (_, _, ms) in enumerate(groups):
            sl = slice(g * LANES, (g + 1) * LANES)
            y_ref[:, sl] = ((ms * r) * on_ref[:, sl]).astype(BF16)

    return pl.pallas_call(
        body, out_shape=jax.ShapeDtypeStruct((n_batch * seq, width), BF16), grid=(n_batch,),
        in_specs=[pl.BlockSpec((seq, width), lambda b: (b, 0)), pl.BlockSpec(pw.shape, lambda b: (0, 0, 0)),
                  pl.BlockSpec((1, width), lambda b: (0, 0)), pl.BlockSpec((1, width), lambda b: (0, 0))],
        out_specs=pl.BlockSpec((seq, width), lambda b: (b, 0)),
        compiler_params=_params(), name="pool_fwd",
    )(pv, pw, ps, onp)


def _pool_bwd(pv, dyp, pw, ps, onp, n_batch, seq):
    width = pv.shape[1]

    def body(pv_ref, dy_ref, pw_ref, ps_ref, on_ref, dpv_ref, dpw_ref, dps_ref, don_ref):
        groups = _pool_groups(pv_ref, pw_ref, ps_ref, seq)
        ssq = sum(jnp.sum(ms * ms, axis=1, keepdims=True) for _, _, ms in groups)
        r = lax.rsqrt(ssq * (1.0 / width) + EPS)
        mean = sum(jnp.sum((dy_ref[:, g * LANES:(g + 1) * LANES] * on_ref[:, g * LANES:(g + 1) * LANES]) * (ms * r),
                           axis=1, keepdims=True) for g, (_, _, ms) in enumerate(groups)) * (1.0 / width)
        row = lax.broadcasted_iota(jnp.int32, (seq, 1), 0)
        pos = (row + 1).astype(F32)
        for g, (pooled, mixed, ms) in enumerate(groups):
            sl = slice(g * LANES, (g + 1) * LANES)
            dy = dy_ref[:, sl]
            xh = ms * r
            don_ref[:, sl] = jnp.sum(dy * xh, axis=0, keepdims=True)
            dms = r * (dy * on_ref[:, sl] - xh * mean)
            dps_ref[:, sl] = jnp.sum(dms * mixed, axis=0, keepdims=True)
            dmix = (dms * ps_ref[:, sl]).astype(BF16)
            dpw_ref[g] = _dot_tn(pooled, dmix)
            dpool = _dot_nt(dmix, pw_ref[g])
            win = POOL_WINDOWS[g]
            acc = dpool / jnp.minimum(pos, float(win))
            dist = 1
            while dist < win:
                acc = acc + _shift_up(acc, dist, row, seq)
                dist *= 2
            dpv_ref[:, sl] = (acc - dpool).astype(BF16)

    tok = pl.BlockSpec((seq, width), lambda b: (b, 0))
    vec = pl.BlockSpec((1, width), lambda b: (0, 0))
    pvec = pl.BlockSpec((None, 1, width), lambda b: (b, 0, 0))
    return pl.pallas_call(
        body,
        out_shape=[jax.ShapeDtypeStruct((n_batch * seq, width), BF16),
                   jax.ShapeDtypeStruct((n_batch,) + pw.shape, F32),
                   jax.ShapeDtypeStruct((n_batch, 1, width), F32), jax.ShapeDtypeStruct((n_batch, 1, width), F32)],
        grid=(n_batch,),
        in_specs=[tok, tok, pl.BlockSpec(pw.shape, lambda b: (0, 0, 0)), vec, vec],
        out_specs=[tok, pl.BlockSpec((None,) + pw.shape, lambda b: (b, 0, 0, 0)), pvec, pvec],
        compiler_params=_params(), name="pool_bwd",
    )(pv, dyp, pw, ps, onp)


def _pick_lane(tile, idx):
    lane = lax.broadcasted_iota(jnp.int32, (1, LANES), 1)
    return jnp.sum(jnp.where(lane == idx, tile, 0.0), axis=1, keepdims=True)


def _pick_row(tile, idx):
    sub = lax.broadcasted_iota(jnp.int32, (tile.shape[0], 1), 0)
    return jnp.sum(jnp.where(sub == idx, tile, 0.0), axis=0, keepdims=True)


def _put_lane(col, idx):
    lane = lax.broadcasted_iota(jnp.int32, (1, LANES), 1)
    return jnp.where(lane == idx, col, 0.0)


def _head_select(e):
    lo = _head_masks()
    return lo if e == 0 else jnp.logical_not(lo)


def _causal(st, shift):
    row = lax.broadcasted_iota(jnp.int32, st.shape, 0)
    col = lax.broadcasted_iota(jnp.int32, st.shape, 1) + shift
    return jnp.where(col >= row, st, NEG)


def _transpose_blocks(a):
    rows, cols = a.shape
    return jnp.concatenate(
        [jnp.concatenate([a[r:r + LANES, c:c + LANES].T for r in range(0, rows, LANES)], axis=1)
         for c in range(0, cols, LANES)], axis=0)


def _stat_rows(ref, head, nsub):
    return jnp.concatenate([_pick_row(ref[a], head) for a in range(nsub)], axis=1)


def _accumulate(ref, value, first):
    @pl.when(first)
    def _():
        ref[...] = value

    @pl.when(jnp.logical_not(first))
    def _():
        ref[...] += value


def _attn_fwd(qa, ka, vb, n_batch, seq, plan=None):
    tq = min(ATT_BLOCK, seq)
    nq, nsub, tk = seq // tq, tq // ATT_SUB, tq
    pairs = vb.shape[1] // LANES

    def body(q_ref, k_ref, v_ref, o_ref, lse_ref, acc_ref):
        i, p = pl.program_id(1), pl.program_id(2)
        row_lo = lax.broadcasted_iota(jnp.int32, (LANES, 1), 0) < HEAD_DIM
        qs = [q_ref[:, e * LANES:(e + 1) * LANES] for e in range(2)]
        acc_ref[...] = jnp.zeros_like(acc_ref)

        def tile(off, stats, diagonal):
            vj = v_ref[pl.ds(off, tk), :]
            new, alphas, pvs = [], [], []
            for e in range(2):
                st = _dot_nt(k_ref[pl.ds(off, tk), e * LANES:(e + 1) * LANES], qs[e])
                if diagonal:
                    st = _causal(st, 0)
                m, l = stats[e]
                m_new = jnp.maximum(m, jnp.max(st, axis=0, keepdims=True))
                alpha = jnp.exp(m - m_new)
                pt = jnp.exp(st - m_new)
                new.append((m_new, alpha * l + jnp.sum(pt, axis=0, keepdims=True)))
                alphas.append(alpha)
                pvs.append(_dot_tn(jnp.where(_head_select(e), vj, jnp.zeros_like(vj)), pt.astype(BF16)))
            acc_ref[...] = acc_ref[...] * jnp.where(row_lo, alphas[0], alphas[1]) + (pvs[0] + pvs[1])
            return tuple(new)

        init = ((jnp.full((1, tq), NEG, F32), jnp.zeros((1, tq), F32)),) * 2
        stats = lax.fori_loop(0, i, lambda j, st: tile(pl.multiple_of(j * tk, tk), st, False), init)
        (m0, l0), (m1, l1) = tile(pl.multiple_of(i * tk, tk), stats, True)
        out_t = acc_ref[...] / jnp.where(row_lo, l0, l1)
        sub = lax.broadcasted_iota(jnp.int32, (8, 1), 0)
        lse0, lse1 = m0 + jnp.log(l0), m1 + jnp.log(l1)
        for a in range(nsub):
            sl = slice(a * ATT_SUB, (a + 1) * ATT_SUB)
            o_ref[sl, :] = out_t[:, sl].T
            rows = jnp.where(sub == 2 * p, lse0[:, sl], 0.0) + jnp.where(sub == 2 * p + 1, lse1[:, sl], 0.0)
            _accumulate(lse_ref.at[a], rows, p == 0)

    return _pallas(
        body, name="attn_fwd", args=[qa, ka, vb],
        out_shape=[jax.ShapeDtypeStruct((n_batch * seq, pairs * LANES), F32),
                   jax.ShapeDtypeStruct((n_batch * seq // ATT_SUB, 8, ATT_SUB), F32)],
        grid=(n_batch, nq, pairs),
        in_specs=[pl.BlockSpec((tq, 2 * LANES), lambda b, i, p: (b * nq + i, p)),
                  pl.BlockSpec((seq, 2 * LANES), lambda b, i, p: (b, p)),
                  pl.BlockSpec((seq, LANES), lambda b, i, p: (b, p))],
        out_specs=[pl.BlockSpec((tq, LANES), lambda b, i, p: (b * nq + i, p)),
                   pl.BlockSpec((nsub, 8, ATT_SUB), lambda b, i, p: (b * nq + i, 0, 0))],
        scratch_shapes=[pltpu.VMEM((LANES, tq), F32)], plan=plan)


def _attn_bwd_q(qa, ka, vb, do, lse, delta, n_batch, seq, plan=None):
    tq = min(ATT_BLOCK, seq)
    nq, nsub, tk = seq // tq, tq // ATT_SUB, tq
    pairs = vb.shape[1] // LANES

    def body(q_ref, k_ref, v_ref, do_ref, lse_ref, dl_ref, dq_ref, dfq_ref, acc0_ref, acc1_ref):
        i, p = pl.program_id(1), pl.program_id(2)
        accs = (acc0_ref, acc1_ref)
        qs = [q_ref[:, e * LANES:(e + 1) * LANES] for e in range(2)]
        dov = do_ref[...]
        ls = [_stat_rows(lse_ref, 2 * p + e, nsub) for e in range(2)]
        dl = [_stat_rows(dl_ref, 2 * p + e, nsub) for e in range(2)]
        for acc in accs:
            acc[...] = jnp.zeros_like(acc)

        def tile(off, diagonal):
            vj = v_ref[pl.ds(off, tk), :]
            for e in range(2):
                kj = k_ref[pl.ds(off, tk), e * LANES:(e + 1) * LANES]
                st = _dot_nt(kj, qs[e])
                if diagonal:
                    st = _causal(st, 0)
                pt = jnp.exp(st - ls[e])
                dpt = _dot_nt(jnp.where(_head_select(e), vj, jnp.zeros_like(vj)), dov)
                accs[e][...] += _dot(_transpose_blocks(kj), (pt * (dpt - dl[e])).astype(BF16))

        def step(j, carry):
            tile(pl.multiple_of(j * tk, tk), False)
            return carry

        lax.fori_loop(0, i, step, 0)
        tile(pl.multiple_of(i * tk, tk), True)
        dq0, dq1 = _transpose_blocks(acc0_ref[...]), _transpose_blocks(acc1_ref[...])
        dq_ref[...] = jnp.where(_head_masks(), dq0, dq1)
        dfq = _put_lane(_pick_lane(dq0, _aug_lane(0)), 2 * p) + _put_lane(_pick_lane(dq1, _aug_lane(1)), 2 * p + 1)
        _accumulate(dfq_ref, dfq, p == 0)

    stat = pl.BlockSpec((nsub, 8, ATT_SUB), lambda b, i, p: (b * nq + i, 0, 0))
    blk = pl.BlockSpec((tq, LANES), lambda b, i, p: (b * nq + i, p))
    return _pallas(
        body, name="attn_bwd_q", args=[qa, ka, vb, do, lse, delta],
        out_shape=[jax.ShapeDtypeStruct((n_batch * seq, pairs * LANES), F32), jax.ShapeDtypeStruct((n_batch * seq, LANES), F32)],
        grid=(n_batch, nq, pairs),
        in_specs=[pl.BlockSpec((tq, 2 * LANES), lambda b, i, p: (b * nq + i, p)),
                  pl.BlockSpec((seq, 2 * LANES), lambda b, i, p: (b, p)),
                  pl.BlockSpec((seq, LANES), lambda b, i, p: (b, p)), blk, stat, stat],
        out_specs=[blk, pl.BlockSpec((tq, LANES), lambda b, i, p: (b * nq + i, 0))],
        scratch_shapes=[pltpu.VMEM((LANES, tq), F32), pltpu.VMEM((LANES, tq), F32)], plan=plan)


def _attn_bwd_kv(qa, ka, vb, do, lse, delta, n_batch, seq, plan=None):
    tkb = min(ATT_BLOCK, seq)
    nk, nsub, tq = seq // tkb, tkb // ATT_SUB, tkb
    n_tiles = seq // ATT_SUB
    pairs = vb.shape[1] // LANES

    def body(q_ref, k_ref, v_ref, do_ref, lse_ref, dl_ref, dk_ref, dv_ref, dfk_ref, dk0_ref, dk1_ref, dva_ref):
        j, p = pl.program_id(1), pl.program_id(2)
        dks = (dk0_ref, dk1_ref)
        ks = [k_ref[:, e * LANES:(e + 1) * LANES] for e in range(2)]
        vj = v_ref[...]
        vs = [jnp.where(_head_select(e), vj, jnp.zeros_like(vj)) for e in range(2)]
        for acc in (dk0_ref, dk1_ref, dva_ref):
            acc[...] = jnp.zeros_like(acc)

        def tile(t, diagonal):
            off = pl.multiple_of(t * tq, tq)
            dov = do_ref[pl.ds(off, tq), :]
            for e in range(2):
                qe = q_ref[pl.ds(off, tq), e * LANES:(e + 1) * LANES]
                st = _dot_nt(ks[e], qe)
                if diagonal:
                    st = _causal(st, 0)
                rows = lambda ref: jnp.concatenate([_pick_row(ref[t * nsub + a], 2 * p + e) for a in range(nsub)], axis=1)
                pt = jnp.exp(st - rows(lse_ref))
                dva_ref[...] += _dot(pt.astype(BF16), jnp.where(_head_select(e), dov, jnp.zeros_like(dov)))
                dst = pt * (_dot_nt(vs[e], dov) - rows(dl_ref))
                dks[e][...] += _dot(dst.astype(BF16), qe)

        def step(t, carry):
            tile(t, False)
            return carry

        lax.fori_loop(j + 1, nk, step, 0)
        tile(j, True)
        dk0, dk1 = dk0_ref[...], dk1_ref[...]
        dk_ref[...] = jnp.where(_head_masks(), dk0, dk1)
        dv_ref[...] = dva_ref[...].astype(BF16)
        dfk = (_put_lane(_pick_lane(dk0, _aug_lane(0) + 3), 2 * p)
               + _put_lane(_pick_lane(dk1, _aug_lane(1) + 3), 2 * p + 1))
        _accumulate(dfk_ref, -dfk, p == 0)

    stat = pl.BlockSpec((n_tiles, 8, ATT_SUB), lambda b, j, p: (b, 0, 0))
    blk = pl.BlockSpec((tkb, LANES), lambda b, j, p: (b * nk + j, p))
    acc = pltpu.VMEM((tkb, LANES), F32)
    return _pallas(
        body, name="attn_bwd_kv", args=[qa, ka, vb, do, lse, delta],
        out_shape=[jax.ShapeDtypeStruct((n_batch * seq, pairs * LANES), F32),
                   jax.ShapeDtypeStruct((n_batch * seq, pairs * LANES), BF16),
                   jax.ShapeDtypeStruct((n_batch * seq, LANES), F32)],
        grid=(n_batch, nk, pairs),
        in_specs=[pl.BlockSpec((seq, 2 * LANES), lambda b, j, p: (b, p)),
                  pl.BlockSpec((tkb, 2 * LANES), lambda b, j, p: (b * nk + j, p)), blk,
                  pl.BlockSpec((seq, LANES), lambda b, j, p: (b, p)), stat, stat],
        out_specs=[blk, blk, pl.BlockSpec((tkb, LANES), lambda b, j, p: (b * nk + j, 0))],
        scratch_shapes=[acc, acc, acc], plan=plan)


def _forget_bwd(dfq, dfk, f, bias, n_batch, seq):
    def body(dfq_ref, dfk_ref, f_ref, b_ref, df_ref, db_ref):
        acc = dfq_ref[...] + dfk_ref[...]
        row = lax.broadcasted_iota(jnp.int32, (seq, 1), 0)
        dist = 1
        while dist < seq:
            acc = acc + _shift_up(acc, dist, row, seq)
            dist *= 2
        df = acc * _sigmoid(-(f_ref[...] + b_ref[...]))
        df_ref[...] = df
        db_ref[...] = jnp.sum(df, axis=0, keepdims=True)

    col = pl.BlockSpec((seq, LANES), lambda b: (b, 0))
    return pl.pallas_call(
        body,
        out_shape=[jax.ShapeDtypeStruct((n_batch * seq, LANES), F32), jax.ShapeDtypeStruct((n_batch, 1, LANES), F32)],
        grid=(n_batch,), in_specs=[col, col, col, pl.BlockSpec((1, LANES), lambda b: (0, 0))],
        out_specs=[col, pl.BlockSpec((None, 1, LANES), lambda b: (b, 0, 0))],
        compiler_params=_params(), name="forget_bwd",
    )(dfq, dfk, f, bias)


def _mix_out(x1, yp, o, ona, woa, wob):
    t, d = x1.shape
    width = o.shape[1]
    tm = min(512, t)

    def body(x_ref, yp_ref, o_ref, on_ref, wa_ref, wb_ref, x2_ref, ya_ref):
        of = o_ref[...]
        ya = ((of * _rms(of)) * on_ref[...]).astype(BF16)
        ya_ref[...] = ya
        x2_ref[...] = x_ref[...] + (_dot(yp_ref[...], wa_ref[...]) + _dot(ya, wb_ref[...]))

    row = pl.BlockSpec((tm, d), lambda i: (i, 0))
    half = pl.BlockSpec((tm, width), lambda i: (i, 0))
    wspec = pl.BlockSpec((width, d), lambda i: (0, 0))
    return pl.pallas_call(
        body, out_shape=[jax.ShapeDtypeStruct((t, d), F32), jax.ShapeDtypeStruct((t, width), BF16)],
        grid=(t // tm,), in_specs=[row, half, half, pl.BlockSpec((1, width), lambda i: (0, 0)), wspec, wspec],
        out_specs=[row, half], compiler_params=_params(), name="mix_out",
    )(x1, yp, o, ona, woa, wob)


def _mix_out_bwd(dx2, o, yp, ya, ona, woa, wob, plan=None):
    t, d = dx2.shape
    width = o.shape[1]
    tm = min(512, t)
    nt = t // tm

    def body(dx_ref, o_ref, yp_ref, ya_ref, on_ref, wa_ref, wb_ref, dyp_ref, do_ref, dl_ref, dwa_ref, dwb_ref, don_ref):
        @pl.when(pl.program_id(0) == 0)
        def _():
            dwa_ref[...] = jnp.zeros_like(dwa_ref)
            dwb_ref[...] = jnp.zeros_like(dwb_ref)

        dxb = dx_ref[...].astype(BF16)
        dwa_ref[...] += _dot_tn(yp_ref[...], dxb)
        dwb_ref[...] += _dot_tn(ya_ref[...], dxb)
        dyp_ref[...] = _dot_nt(dxb, wa_ref[...])
        of = o_ref[...]
        dov, dgr = _rms_bwd(of, _rms(of), on_ref[...], _dot_nt(dxb, wb_ref[...]))
        don_ref[...] = jnp.sum(dgr, axis=0, keepdims=True)
        do_ref[...] = dov.astype(BF16)
        lo = _head_masks()
        prod = dov * of
        delta = jnp.zeros((tm, LANES), F32)
        for blk in range(width // LANES):
            pb = prod[:, blk * LANES:(blk + 1) * LANES]
            delta = delta + _put_lane(jnp.sum(jnp.where(lo, pb, 0.0), axis=1, keepdims=True), 2 * blk)
            delta = delta + _put_lane(jnp.sum(jnp.where(lo, 0.0, pb), axis=1, keepdims=True), 2 * blk + 1)
        for c in range(tm // ATT_SUB):
            dl_ref[c] = delta[c * ATT_SUB:(c + 1) * ATT_SUB, :].T[0:8, :]

    row = pl.BlockSpec((tm, d), lambda i: (i, 0))
    half = pl.BlockSpec((tm, width), lambda i: (i, 0))
    wspec = pl.BlockSpec((width, d), lambda i: (0, 0))
    return _pallas(
        body, name="mix_out_bwd", args=[dx2, o, yp, ya, ona, woa, wob],
        out_shape=[jax.ShapeDtypeStruct((t, width), F32), jax.ShapeDtypeStruct((t, width), BF16),
                   jax.ShapeDtypeStruct((t // ATT_SUB, 8, ATT_SUB), F32), jax.ShapeDtypeStruct((width, d), F32),
                   jax.ShapeDtypeStruct((width, d), F32), jax.ShapeDtypeStruct((nt, 1, width), F32)],
        grid=(nt,),
        in_specs=[row, half, half, half, pl.BlockSpec((1, width), lambda i: (0, 0)), wspec, wspec],
        out_specs=[half, half, pl.BlockSpec((tm // ATT_SUB, 8, ATT_SUB), lambda i: (i, 0, 0)), wspec, wspec,
                   pl.BlockSpec((None, 1, width), lambda i: (i, 0, 0))], plan=plan)


def _mix_in_bwd(dx2, x1, gain, hm, dpv, dqh, q, dkh, k, dv, df, qn, kn, wt):
    t, d = x1.shape
    width = q.shape[1]
    pool_width = dpv.shape[1]
    tm = min(512, t)
    nt = t // tm
    scale = HEAD_DIM ** -0.5
    c_q, c_k, c_v = pool_width, pool_width + width, pool_width + 2 * width
    c_f = c_v + width

    def body(dx2_ref, x_ref, g_ref, hm_ref, dpv_ref, dqh_ref, q_ref, dkh_ref, k_ref, dv_ref, df_ref, qn_ref, kn_ref,
             wt_ref, dx_ref, dwt_ref, dg_ref, dqn_ref, dkn_ref):
        @pl.when(pl.program_id(0) == 0)
        def _():
            dwt_ref[...] = jnp.zeros_like(dwt_ref)

        lo = _head_masks()
        for part, rows in enumerate(_row_halves(tm)):
            def put(ref, sl, value):
                ref[:, sl] = value if part == 0 else ref[:, sl] + value

            hm = hm_ref[rows, :]
            pieces = [(0, dpv_ref[rows, :])]
            for c0, raw_ref, dh_ref, n_ref, dn_ref, mul in ((c_q, q_ref, dqh_ref, qn_ref, dqn_ref, scale),
                                                           (c_k, k_ref, dkh_ref, kn_ref, dkn_ref, 1.0)):
                cols = []
                for blk in range(width // LANES):
                    sl = slice(blk * LANES, (blk + 1) * LANES)
                    xb = raw_ref[rows, sl]
                    gb = dh_ref[rows, sl] * mul
                    r = _head_rms(xb, lo)
                    xh = xb * r
                    dyg = gb * n_ref[:, sl]
                    cols.append((r * (dyg - xh * _head_mean(dyg * xh, lo))).astype(BF16))
                    put(dn_ref, sl, jnp.sum(gb * xh, axis=0, keepdims=True))
                pieces.append((c0, jnp.concatenate(cols, axis=1)))
            pieces.append((c_v, dv_ref[rows, :]))
            pieces.append((c_f, df_ref[rows, :].astype(BF16)))
            dhm = jnp.zeros((tm // 2, d), F32)
            for c0, piece in pieces:
                dwt_ref[c0:c0 + piece.shape[1], :] += _dot_tn(piece, hm)
                dhm = dhm + _dot(piece, wt_ref[c0:c0 + piece.shape[1], :])
            xf = x_ref[rows, :]
            dxn, dgr = _rms_bwd(xf, _rms(xf), g_ref[...], dhm)
            dx_ref[rows, :] = dx2_ref[rows, :] + dxn
            put(dg_ref, slice(None), jnp.sum(dgr, axis=0, keepdims=True))

    row = pl.BlockSpec((tm, d), lambda i: (i, 0))
    half = pl.BlockSpec((tm, width), lambda i: (i, 0))
    const = lambda shape: pl.BlockSpec(shape, lambda i: (0, 0))
    pvec = lambda n: pl.BlockSpec((None, 1, n), lambda i: (i, 0, 0))
    return pl.pallas_call(
        body,
        out_shape=[jax.ShapeDtypeStruct((t, d), F32), jax.ShapeDtypeStruct(wt.shape, F32),
                   jax.ShapeDtypeStruct((nt, 1, d), F32),
                   jax.ShapeDtypeStruct((nt, 1, width), F32), jax.ShapeDtypeStruct((nt, 1, width), F32)],
        grid=(nt,),
        in_specs=[row, row, const((1, d)), row, pl.BlockSpec((tm, pool_width), lambda i: (i, 0)), half, half, half, half,
                  half, pl.BlockSpec((tm, LANES), lambda i: (i, 0)), const((1, width)), const((1, width)),
                  const(wt.shape)],
        out_specs=[row, const(wt.shape), pvec(d), pvec(width), pvec(width)],
        compiler_params=_params(), name="mix_in_bwd",
    )(dx2, x1, gain, hm, dpv, dqh, q, dkh, k, dv, df, qn, kn, wt)


def _mesh_pos():
    return lax.axis_index("x"), lax.axis_index("y"), lax.axis_index("c")


def _other_chips(x, y):
    return [(1 - x, y), (x, 1 - y), (1 - x, 1 - y)]


def _remote(src, dst, send_sem, recv_sem, device):
    return pltpu.make_async_remote_copy(src_ref=src, dst_ref=dst, send_sem=send_sem, recv_sem=recv_sem,
                                        device_id=device, device_id_type=pl.DeviceIdType.MESH)


def _half_rows(n_rows, which):
    half = n_rows // 2
    return pl.ds(pl.multiple_of(which * half, 8), half)


def _row_block(rows, cols, itemsize=4):
    rb = rows
    while rb * cols * itemsize > (1 << 20) and rb % 32 == 0:
        rb //= 2
    return rb


def _place_cast(ws, chip, tag):
    n = len(ws)
    rows, cols = ws[0].shape
    rb = _row_block(rows, cols)

    def body(k_ref, *refs):
        for w_ref, o_ref in zip(refs[:n], refs[n:]):
            o_ref[...] = w_ref[...].astype(BF16)

    return pl.pallas_call(
        body, out_shape=[jax.ShapeDtypeStruct((N_CHIPS, rows, cols), BF16)] * n,
        grid_spec=pltpu.PrefetchScalarGridSpec(
            num_scalar_prefetch=1, grid=(rows // rb,),
            in_specs=[pl.BlockSpec((rb, cols), lambda i, k: (i, 0))] * n,
            out_specs=[pl.BlockSpec((None, rb, cols), lambda i, k: (k[0], i, 0))] * n),
        compiler_params=_params(), name="place_" + tag,
    )(chip, *ws)


class _Plan:
    def __init__(self, ins, outs, alias, sems, start, finish):
        self.ins, self.outs, self.alias, self.sems, self.start, self.finish = ins, outs, alias, sems, start, finish


def _merge_plans(a, b):
    ni, no, ns = len(a.ins), len(a.outs), len(a.sems)
    alias = dict(a.alias)
    alias.update({ni + i: no + o for i, o in b.alias.items()})

    def both(which):
        def run(ins, outs, sems):
            getattr(a, which)(ins[:ni], outs[:no], sems[:ns])
            getattr(b, which)(ins[ni:], outs[no:], sems[ns:])
        return run

    return _Plan(list(a.ins) + list(b.ins), list(a.outs) + list(b.outs), alias, list(a.sems) + list(b.sems),
                 both("start"), both("finish"))


def _run_plan(plan, name):
    n_in, n_out = len(plan.ins), len(plan.outs)

    def body(*refs):
        parts = refs[:n_in], refs[n_in:n_in + n_out], refs[n_in + n_out:]
        plan.start(*parts)
        plan.finish(*parts)

    return pl.pallas_call(
        body, out_shape=plan.outs, in_specs=[ANY] * n_in, out_specs=[ANY] * n_out, scratch_shapes=plan.sems,
        input_output_aliases=plan.alias, name=name,
    )(*plan.ins)


def _pallas(body, *, name, args, in_specs, out_shape, out_specs, grid, scratch_shapes=(), plan=None, aliases=None,
            prefetch=()):
    n_pre, n_in, n_out, n_scr = len(prefetch), len(args), len(out_shape), len(scratch_shapes)
    plan = plan or _Plan([], [], {}, [], None, None)
    p_in, p_out = len(plan.ins), len(plan.outs)

    def carrying(*refs):
        pre, refs = refs[:n_pre], refs[n_pre:]
        ins, p_ins = refs[:n_in], refs[n_in:n_in + p_in]
        o0 = n_in + p_in
        outs, p_outs = refs[o0:o0 + n_out], refs[o0 + n_out:o0 + n_out + p_out]
        s0 = o0 + n_out + p_out
        scr, p_sems = refs[s0:s0 + n_scr], refs[s0 + n_scr:]
        ids = [pl.program_id(a) for a in range(len(grid))]

        if plan.start is not None:
            @pl.when(functools.reduce(jnp.logical_and, [i == 0 for i in ids]))
            def _():
                plan.start(p_ins, p_outs, p_sems)

        body(*pre, *ins, *outs, *scr)

        if plan.finish is not None:
            @pl.when(functools.reduce(jnp.logical_and, [i == g - 1 for i, g in zip(ids, grid)]))
            def _():
                plan.finish(p_ins, p_outs, p_sems)

    aliases = {n_pre + i: o for i, o in (aliases or {}).items()}
    aliases.update({n_pre + n_in + i: n_out + o for i, o in plan.alias.items()})
    res = pl.pallas_call(
        carrying, out_shape=list(out_shape) + list(plan.outs),
        grid_spec=pltpu.PrefetchScalarGridSpec(
            num_scalar_prefetch=n_pre, grid=grid, in_specs=list(in_specs) + [ANY] * p_in,
            out_specs=list(out_specs) + [ANY] * p_out, scratch_shapes=list(scratch_shapes) + list(plan.sems)),
        input_output_aliases=aliases, compiler_params=_params(), name=name,
    )(*prefetch, *args, *plan.ins)
    return list(res[:n_out]), list(res[n_out:])


def _plan_gather(stacks, relations=(0, 1, 2)):
    n = len(stacks)

    def ici_copies(outs, sems):
        x, y, c = _mesh_pos()
        chips = _other_chips(x, y)
        cps = []
        for w in range(n):
            own = outs[w].at[2 * x + y, _half_rows(stacks[w].shape[1], c)]
            cps += [_remote(own, own, sems[0].at[w, j], sems[1].at[w, j], (*chips[j], c)) for j in relations]
        return cps

    def start(ins, outs, sems):
        for cp in ici_copies(outs, sems):
            cp.start()

    def finish(ins, outs, sems):
        ici_send, ici_recv, d2d_send, d2d_recv = sems
        x, y, c = _mesh_pos()
        sibling = (x, y, 1 - c)
        slots = [2 * cx + cy for cx, cy in _other_chips(x, y)]
        forwards = []
        for w in range(n):
            rows = _half_rows(stacks[w].shape[1], c)
            for j in relations:
                landed = outs[w].at[slots[j], rows]
                _remote(landed, landed, ici_send.at[w, j], ici_recv.at[w, j], sibling).wait_recv()
                cp = _remote(landed, landed, d2d_send.at[w, j], d2d_recv.at[w, j], sibling)
                cp.start()
                forwards.append(cp)
        for w in range(n):
            rows = _half_rows(stacks[w].shape[1], 1 - c)
            for j in relations:
                landed = outs[w].at[slots[j], rows]
                _remote(landed, landed, d2d_send.at[w, j], d2d_recv.at[w, j], sibling).wait_recv()
        for cp in ici_copies(outs, sems) + forwards:
            cp.wait_send()

    return _Plan(stacks, [jax.ShapeDtypeStruct(s.shape, s.dtype) for s in stacks], {w: w for w in range(n)},
                 [pltpu.SemaphoreType.DMA((n, 3))] * 4, start, finish)


def _plan_gather_relay(stacks):
    n = len(stacks)

    def finish(ins, outs, sems):
        send, recv, relay_send, relay_recv, d2d_send, d2d_recv = sems
        x, y, c = _mesh_pos()
        sibling = (x, y, 1 - c)
        near = [(1 - x, y), (x, 1 - y)]
        far = 2 * (1 - x) + (1 - y)
        started = []

        def go(cp):
            cp.start()
            started.append(cp)

        def piece(w, slot, core, quarter=None):
            rh = stacks[w].shape[1] // 2
            if quarter is None:
                return outs[w].at[slot, _half_rows(2 * rh, core)]
            return outs[w].at[slot, pl.ds(pl.multiple_of(core * rh + quarter * (rh // 2), 8), rh // 2)]

        for w in range(n):
            own = piece(w, 2 * x + y, c)
            for j, chip in enumerate(near):
                go(_remote(own, own, send.at[w, j], recv.at[w, j], (*chip, c)))
        for w in range(n):
            for j, (cx, cy) in enumerate(near):
                landed = piece(w, 2 * cx + cy, c)
                _remote(landed, landed, send.at[w, j], recv.at[w, j], sibling).wait_recv()
                part = piece(w, 2 * cx + cy, c, quarter=j)
                go(_remote(part, part, relay_send.at[w, j], relay_recv.at[w, j], (*near[1 - j], c)))
                go(_remote(landed, landed, d2d_send.at[w, j], d2d_recv.at[w, j], sibling))
        for w in range(n):
            for j in range(2):
                part = piece(w, far, c, quarter=j)
                _remote(part, part, relay_send.at[w, j], relay_recv.at[w, j], sibling).wait_recv()
            landed = piece(w, far, c)
            go(_remote(landed, landed, d2d_send.at[w, 2], d2d_recv.at[w, 2], sibling))
        for w in range(n):
            for j, slot in enumerate([2 * cx + cy for cx, cy in near] + [far]):
                landed = piece(w, slot, 1 - c)
                _remote(landed, landed, d2d_send.at[w, j], d2d_recv.at[w, j], sibling).wait_recv()
        for cp in started:
            cp.wait_send()

    return _Plan(stacks, [jax.ShapeDtypeStruct(s.shape, s.dtype) for s in stacks], {w: w for w in range(n)},
                 [pltpu.SemaphoreType.DMA((n, 2))] * 4 + [pltpu.SemaphoreType.DMA((n, 3))] * 2,
                 lambda ins, outs, sems: None, finish)


def _plan_sibling_halves(gs):
    n = len(gs)

    def copies(ins, outs, sems):
        x, y, c = _mesh_pos()
        return [_remote(ins[w].at[:, _half_rows(gs[w].shape[1], 1 - c), :], outs[w], sems[0].at[w], sems[1].at[w],
                        (x, y, 1 - c)) for w in range(n)]

    def start(ins, outs, sems):
        for cp in copies(ins, outs, sems):
            cp.start()

    def finish(ins, outs, sems):
        for cp in copies(ins, outs, sems):
            cp.wait()

    return _Plan(gs, [jax.ShapeDtypeStruct((g.shape[0], g.shape[1] // 2, g.shape[2]), g.dtype) for g in gs], {},
                 [pltpu.SemaphoreType.DMA((n,))] * 2, start, finish)


def _plan_chip_exchange(ps):
    n = len(ps)

    def copies(ins, outs, sems):
        x, y, c = _mesh_pos()
        return [_remote(ins[w].at[2 * cx + cy], outs[w].at[j], sems[0].at[w, j], sems[1].at[w, j], (cx, cy, c))
                for w in range(n) for j, (cx, cy) in enumerate(_other_chips(x, y))]

    def start(ins, outs, sems):
        for cp in copies(ins, outs, sems):
            cp.start()

    def finish(ins, outs, sems):
        for cp in copies(ins, outs, sems):
            cp.wait()

    return _Plan(ps, [jax.ShapeDtypeStruct((3,) + p.shape[1:], p.dtype) for p in ps], {},
                 [pltpu.SemaphoreType.DMA((n, 3))] * 2, start, finish)


def _plan_sibling_share(gs):
    n = len(gs)

    def copies(outs, sems, which):
        x, y, c = _mesh_pos()
        cps = []
        for w in range(n):
            rows = outs[w].at[_half_rows(gs[w].shape[0], c if which == "mine" else 1 - c)]
            cps.append(_remote(rows, rows, sems[0].at[w], sems[1].at[w], (x, y, 1 - c)))
        return cps

    def start(ins, outs, sems):
        for cp in copies(outs, sems, "mine"):
            cp.start()

    def finish(ins, outs, sems):
        for cp in copies(outs, sems, "mine"):
            cp.wait_send()
        for cp in copies(outs, sems, "theirs"):
            cp.wait_recv()

    return _Plan(gs, [jax.ShapeDtypeStruct(g.shape, g.dtype) for g in gs], {w: w for w in range(n)},
                 [pltpu.SemaphoreType.DMA((n,))] * 2, start, finish)


def _same_shape_groups(arrays):
    groups = {}
    for i, a in enumerate(arrays):
        groups.setdefault(a.shape, []).append(i)
    return list(groups.values())


def _add_sibling(gs, r1s, ids, tag):
    n = len(gs)
    nch, rh, cols = r1s[0].shape

    def body(ids_ref, *refs):
        for g_ref, r_ref, o_ref in zip(refs[:n], refs[n:2 * n], refs[2 * n:]):
            o_ref[...] = (g_ref[...] + r_ref[...]).astype(BF16)

    blk = lambda fn: pl.BlockSpec((None, rh, cols), fn)
    return pl.pallas_call(
        body, out_shape=[jax.ShapeDtypeStruct(r1s[0].shape, BF16)] * n,
        grid_spec=pltpu.PrefetchScalarGridSpec(
            num_scalar_prefetch=1, grid=(nch,),
            in_specs=[blk(lambda k, ids: (k, ids[1], 0))] * n + [blk(lambda k, ids: (k, 0, 0))] * n,
            out_specs=[blk(lambda k, ids: (k, 0, 0))] * n),
        compiler_params=_params(), name="add_sibling_" + tag,
    )(ids, *gs, *r1s)


def _add_chips(gs, r1s, r2s, ids, tag):
    n = len(gs)
    _, rh, cols = r1s[0].shape
    nb = 2 if rh % 32 == 0 else 1
    rb = rh // nb

    def body(ids_ref, *refs):
        for g_ref, r1_ref, r2_ref, o_ref in zip(refs[:n], refs[n:2 * n], refs[2 * n:3 * n], refs[3 * n:]):
            own = g_ref[...] + r1_ref[...]
            o_ref[...] = ((own + r2_ref[0].astype(F32)) + r2_ref[1].astype(F32)) + r2_ref[2].astype(F32)

    return pl.pallas_call(
        body, out_shape=[jax.ShapeDtypeStruct((2 * rh, cols), F32)] * n,
        grid_spec=pltpu.PrefetchScalarGridSpec(
            num_scalar_prefetch=1, grid=(nb,),
            in_specs=[pl.BlockSpec((None, rb, cols), lambda i, ids: (ids[0], ids[1] * nb + i, 0))] * n
            + [pl.BlockSpec((None, rb, cols), lambda i, ids: (ids[0], i, 0))] * n
            + [pl.BlockSpec((3, rb, cols), lambda i, ids: (0, i, 0))] * n,
            out_specs=[pl.BlockSpec((rb, cols), lambda i, ids: (ids[1] * nb + i, 0))] * n),
        compiler_params=_params(), name="add_chips_" + tag,
    )(ids, *gs, *r1s, *r2s)


VEC_ROWS = 8


N_DEVICES = 8


def _small_pack(part, d, width):
    names = ("ffn1_norm", "mix_norm", "ffn2_norm", "pool_scale", "out_norm_pool", "out_norm_attn", "qn", "kn", "b_forget",
             "pool_w", "loss")
    args = [part[k] for k in names]
    pw_shape = part["pool_w"].shape[1:]

    def body(g1_ref, gm_ref, g2_ref, ps_ref, onp_ref, ona_ref, qn_ref, kn_ref, bf_ref, pw_ref, loss_ref, vbuf, pbuf):
        lo = _head_masks()

        def fold_heads(ref):
            v = jnp.sum(ref[...], axis=0)
            acc = jnp.zeros((VEC_ROWS, LANES), F32)
            for blk in range(width // LANES):
                vb = jnp.broadcast_to(v[:, blk * LANES:(blk + 1) * LANES], (VEC_ROWS, LANES))
                acc = acc + vb + pltpu.roll(vb, HEAD_DIM, 1)
            return jnp.where(lo, acc, 0.0)[0:1, :]

        vbuf[0] = jnp.zeros((VEC_ROWS, d), F32)
        vbuf[0, 0:1, :] = jnp.sum(g1_ref[...], axis=0)
        vbuf[0, 1:2, :] = jnp.sum(gm_ref[...], axis=0)
        vbuf[0, 2:3, :] = jnp.sum(g2_ref[...], axis=0)
        vbuf[0, 5:6, 0:LANES] = jnp.sum(loss_ref[...], axis=0)[0:1, :]
        vbuf[0, 3:4, 0:width] = jnp.sum(ps_ref[...], axis=0)
        vbuf[0, 3:4, width:2 * width] = jnp.sum(onp_ref[...], axis=0)
        vbuf[0, 4:5, 0:width] = jnp.sum(ona_ref[...], axis=0)
        vbuf[0, 4:5, width:width + LANES] = fold_heads(qn_ref)
        vbuf[0, 4:5, width + LANES:width + 2 * LANES] = fold_heads(kn_ref)
        vbuf[0, 4:5, width + 2 * LANES:width + 3 * LANES] = jnp.sum(bf_ref[...], axis=0)
        pbuf[0] = jnp.sum(pw_ref[...], axis=0)

    return pl.pallas_call(
        body, out_shape=[jax.ShapeDtypeStruct((N_DEVICES, VEC_ROWS, d), F32), jax.ShapeDtypeStruct((N_DEVICES,) + pw_shape, F32)],
        in_specs=[VM] * len(args), out_specs=[VM, VM], compiler_params=_params(), name="small_pack",
    )(*args)


def _plan_all_to_all(stacks):
    n = len(stacks)

    def copies(outs, sems):
        x, y, c = _mesh_pos()
        cps = []
        for r in range(1, N_DEVICES):
            peer = (x if not r & 4 else 1 - x, y if not r & 2 else 1 - y, c if not r & 1 else 1 - c)
            cps += [_remote(outs[w].at[0], outs[w].at[r], sems[0].at[w, r - 1], sems[1].at[w, r - 1], peer) for w in range(n)]
        return cps

    def start(ins, outs, sems):
        for cp in copies(outs, sems):
            cp.start()

    def finish(ins, outs, sems):
        for cp in copies(outs, sems):
            cp.wait()

    return _Plan(stacks, [jax.ShapeDtypeStruct(s.shape, s.dtype) for s in stacks], {w: w for w in range(n)},
                 [pltpu.SemaphoreType.DMA((n, N_DEVICES - 1))] * 2, start, finish)


def _small_sum(vstack, pstack, me):
    def body(me_ref, vbuf, pbuf, vec_ref, pw_ref):
        vec = vbuf[me_ref[0]]
        pw = pbuf[me_ref[0]]
        for dev in range(1, N_DEVICES):
            vec = vec + vbuf[jnp.bitwise_xor(me_ref[0], dev)]
            pw = pw + pbuf[jnp.bitwise_xor(me_ref[0], dev)]
        vec_ref[...] = vec
        pw_ref[...] = pw

    full = lambda s: pl.BlockSpec(s.shape, lambda i, me: (0,) * len(s.shape))
    outs = [jax.ShapeDtypeStruct(vstack.shape[1:], F32), jax.ShapeDtypeStruct(pstack.shape[1:], F32)]
    return pl.pallas_call(
        body, out_shape=outs,
        grid_spec=pltpu.PrefetchScalarGridSpec(num_scalar_prefetch=1, grid=(1,), in_specs=[full(vstack), full(pstack)],
                                               out_specs=[full(o) for o in outs]),
        compiler_params=_params(), name="small_sum",
    )(me, vstack, pstack)


def _adamw(ws, gs, ms, vs, tag, plan=None):
    n = len(ws)
    rows, cols = ws[0].shape
    rb = rows
    while rb * cols * 4 * n > (1 << 20) and rb % 16 == 0:
        rb //= 2

    def body(*refs):
        for j in range(n):
            w_ref, g_ref, m_ref, v_ref = (refs[k * n + j] for k in range(4))
            d_ref, mo_ref, vo_ref = (refs[(4 + k) * n + j] for k in range(3))
            gv = g_ref[...]
            m2 = ADAM_B1 * m_ref[...] + (1.0 - ADAM_B1) * gv
            v2 = ADAM_B2 * v_ref[...] + (1.0 - ADAM_B2) * (gv * gv)
            m_hat = m2 / (1.0 - ADAM_B1 ** ADAM_STEP)
            v_hat = v2 / (1.0 - ADAM_B2 ** ADAM_STEP)
            d_ref[...] = -ADAM_LR * (m_hat / (jnp.sqrt(v_hat) + ADAM_EPS) + ADAM_WD * w_ref[...])
            mo_ref[...] = m2
            vo_ref[...] = v2

    spec = pl.BlockSpec((rb, cols), lambda i: (i, 0))
    res, carried = _pallas(
        body, name="adamw_" + tag, args=[*ws, *gs, *ms, *vs], out_shape=[jax.ShapeDtypeStruct(ws[0].shape, F32)] * (3 * n),
        grid=(rows // rb,), in_specs=[spec] * (4 * n), out_specs=[spec] * (3 * n), plan=plan)
    return [(res[j], res[n + j], res[2 * n + j]) for j in range(n)], carried


def _pack_vec(p, d, width):
    pad = lambda v: jnp.pad(v, (0, LANES - v.shape[0]))
    row3 = jnp.concatenate([p["pool_scale"], p["out_norm_pool"]])
    row4 = jnp.concatenate([p["out_norm_attn"], pad(p["q_norm"]), pad(p["k_norm"]), pad(p["b_forget"]),
                            jnp.zeros((d - width - 3 * LANES,), F32)])
    rows = [p["ffn1_norm"], p["mix_norm"], p["ffn2_norm"], row3, row4]
    return jnp.pad(jnp.stack(rows), ((0, VEC_ROWS - len(rows)), (0, 0)))


def _unpack_vec(vec, width):
    return dict(ffn1_norm=vec[0], mix_norm=vec[1], ffn2_norm=vec[2], pool_scale=vec[3, :width],
                out_norm_pool=vec[3, width:2 * width], out_norm_attn=vec[4, :width],
                q_norm=vec[4, width:width + HEAD_DIM], k_norm=vec[4, width + LANES:width + LANES + HEAD_DIM],
                b_forget=vec[4, width + 2 * LANES:width + 2 * LANES + N_HEADS])


WEIGHT_NAMES = ("ffn1_norm", "ffn1_w_gate", "ffn1_w_up", "ffn1_w_down", "mix_norm", "w_in", "b_forget", "pool_w",
                "pool_scale", "q_norm", "k_norm", "out_norm_pool", "out_norm_attn", "w_out", "ffn2_norm",
                "ffn2_w_gate", "ffn2_w_up", "ffn2_w_down")
BIG_NAMES = ("ffn1_w_gate", "ffn1_w_up", "ffn1_w_down", "w_in", "w_out", "ffn2_w_gate", "ffn2_w_up", "ffn2_w_down")
TRANSPOSED_NAMES = ("ffn1_w_gate", "ffn1_w_up", "w_in", "ffn2_w_gate", "ffn2_w_up")
FFN1_NAMES = ("ffn1_w_gate", "ffn1_w_up", "ffn1_w_down")
MIX_NAMES = ("w_in", "w_out")
FFN2_NAMES = ("ffn2_w_gate", "ffn2_w_up", "ffn2_w_down")


def kernel(x, ffn1_norm, ffn1_w_gate, ffn1_w_up, ffn1_w_down, mix_norm, w_in, b_forget, pool_w, pool_scale, q_norm, k_norm, out_norm_pool, out_norm_attn, w_out, ffn2_norm, ffn2_w_gate, ffn2_w_up, ffn2_w_down, loss_target, m_ffn1_norm, m_ffn1_w_gate, m_ffn1_w_up, m_ffn1_w_down, m_mix_norm, m_w_in, m_b_forget, m_pool_w, m_pool_scale, m_q_norm, m_k_norm, m_out_norm_pool, m_out_norm_attn, m_w_out, m_ffn2_norm, m_ffn2_w_gate, m_ffn2_w_up, m_ffn2_w_down, v_ffn1_norm, v_ffn1_w_gate, v_ffn1_w_up, v_ffn1_w_down, v_mix_norm, v_w_in, v_b_forget, v_pool_w, v_pool_scale, v_q_norm, v_k_norm, v_out_norm_pool, v_out_norm_attn, v_w_out, v_ffn2_norm, v_ffn2_w_gate, v_ffn2_w_up, v_ffn2_w_down):
    given = dict(locals())
    w = {n: given[n] for n in WEIGHT_NAMES}
    m = {n: given["m_" + n] for n in WEIGHT_NAMES}
    v = {n: given["v_" + n] for n in WEIGHT_NAMES}
    n_batch, seq, d = x.shape
    width = pool_scale.shape[0]
    in_rows = w_in.shape[1]
    in_cols = N_CHIPS * in_rows
    in_pad = -(-in_rows // 32) * 32
    in_cols_pad = in_cols - N_HEADS + LANES

    work = lambda a, n: a.T if n in TRANSPOSED_NAMES else a
    exchanged = lambda a, n: jnp.pad(a, ((0, in_pad - in_rows), (0, 0))) if n == "w_in" else a

    mesh_x, mesh_y, mesh_c = _mesh_pos()
    ids = jnp.stack([2 * mesh_x + mesh_y, mesh_c]).astype(jnp.int32)

    row = lambda a: a.reshape(1, -1)
    g1, gm, g2, ps, onp, ona = (row(a) for a in (ffn1_norm, mix_norm, ffn2_norm, pool_scale, out_norm_pool, out_norm_attn))
    qn, kn = row(jnp.tile(q_norm, N_HEADS)), row(jnp.tile(k_norm, N_HEADS))
    bf = row(jnp.pad(b_forget, (0, LANES - N_HEADS)))
    pwb = pool_w.astype(BF16)
    xf, tgt = x.reshape(n_batch * seq, d), loss_target.reshape(n_batch * seq, d)

    def grouped(call, names, *lists):
        out = [None] * len(names)
        for idx in _same_shape_groups(lists[0]):
            res = call(*[[lst[i] for i in idx] for lst in lists], names[idx[0]])
            for i, r in zip(idx, res):
                out[i] = r
        return out

    placed = dict(zip(BIG_NAMES, grouped(lambda ws, tag: _place_cast(ws, ids, tag), BIG_NAMES,
                                         [exchanged(work(w[n], n), n) for n in BIG_NAMES])))
    wg1, wu1, wd1 = _run_plan(_plan_gather_relay([placed[n] for n in FFN1_NAMES]), "gather_ffn1")
    (x1, h1, a1, b1, s1), (w_in_all, w_out_all, wd2) = _ffn_fwd(
        xf, g1, wg1, wu1, wd1, plan=_plan_gather([placed[n] for n in MIX_NAMES + FFN2_NAMES[2:]]))
    w_in_t = jnp.pad(w_in_all[:, :in_rows].reshape(in_cols, d), ((0, in_cols_pad - in_cols), (0, 0)))
    w_out_full = w_out_all.reshape(N_CHIPS * w_out.shape[0], d)
    woa, wob = w_out_full[:width], w_out_full[width:]

    (hm, pv, q, k, qh, kh, vb, f), _ = _mix_proj(x1, gm, w_in_t, qn, kn, width, width)
    qa, ka = _forget_prefix(f, bf, qh, kh, n_batch, seq)
    yp = _pool_fwd(pv, pwb, ps, onp, n_batch, seq)
    (o, lse), (wg2, wu2) = _attn_fwd(qa, ka, vb, n_batch, seq, plan=_plan_gather([placed[n] for n in FFN2_NAMES[:2]]))
    x2, ya = _mix_out(x1, yp, o, ona, woa, wob)
    (dy, h2, a2, b2, s2, lpart), _ = _ffn_fwd(x2, g2, wg2, wu2, wd2, target=tgt)

    def to_chips(gs, arrived, tags):
        return grouped(lambda g, r, tag: _add_sibling(g, r, ids, tag), tags, gs, arrived)

    def own_rows(gs, from_sibling, from_chips, tags):
        return grouped(lambda g, ra, rb, tag: _add_chips(g, ra, rb, ids, tag), tags, gs, from_sibling, from_chips)

    (dx2, da2, db2, dg2), _ = _ffn_bwd_x(dy, x2, g2, a2, b2, wg2, wu2, wd2, "ffn2_bwd_x")
    dw2, _ = _ffn_bwd_w([(da2, h2, 1.0), (db2, h2, 1.0), (s2, dy, 0.5)], "ffn2_bwd_w")
    (dyp, do, delta, dwoa, dwob, dona), sib2 = _mix_out_bwd(dx2, o, yp, ya, ona, woa, wob, plan=_plan_sibling_halves(dw2))
    dpv, dpw, dps, donp = _pool_bwd(pv, dyp, pwb, ps, onp, n_batch, seq)
    (dqh, dfq), chips2 = _attn_bwd_q(qa, ka, vb, do, lse, delta, n_batch, seq,
                                     plan=_plan_chip_exchange(to_chips(dw2, sib2, FFN2_NAMES)))
    (dkh, dv, dfk), red2 = _attn_bwd_kv(qa, ka, vb, do, lse, delta, n_batch, seq,
                                        plan=_plan_sibling_share(own_rows(dw2, sib2, chips2, FFN2_NAMES)))
    df, dbf = _forget_bwd(dfq, dfk, f, bf, n_batch, seq)
    dx1, dw_in_t, dgm, dqn, dkn = _mix_in_bwd(dx2, x1, gm, hm, dpv, dqh, q, dkh, k, dv, df, qn, kn, w_in_t)
    in_base = [in_rows * k // 8 * 8 for k in range(N_CHIPS)]
    d_w_in = jnp.stack([dw_in_t[b:b + in_pad] for b in in_base])
    d_w_out = jnp.concatenate([dwoa, dwob], axis=0).reshape(N_CHIPS, w_out.shape[0], d)
    dwm = [d_w_in, d_w_out]
    down, gate_up = FFN1_NAMES[2:], FFN1_NAMES[:2]
    dwd1, sibm = _ffn_bwd_w([(s1, dx1, 0.5)], "ffn1_bwd_w_down", plan=_plan_sibling_halves(dwm))
    (da1, db1), arrived = _ffn_bwd_a(dx1, a1, b1, wd1, "ffn1_bwd_a",
                                     plan=_merge_plans(_plan_sibling_halves(dwd1),
                                                       _plan_chip_exchange(to_chips(dwm, sibm, MIX_NAMES))))
    sibd, chipsm = arrived[:1], arrived[1:]
    dwgu1, chipsd = _ffn_bwd_w([(da1, h1, 1.0), (db1, h1, 1.0)], "ffn1_bwd_w_gate_up",
                               plan=_plan_chip_exchange(to_chips(dwd1, sibd, down)))
    n_tiles = (n_batch * seq) // min(FFN_TILE, n_batch * seq)
    first = max(n_tiles // 4, 1)
    begun, sibgu = _ffn_bwd_h(dx1, xf, g1, da1, db1, wg1, wu1, "ffn1_bwd_h_first", (0, first),
                              plan=_plan_sibling_halves(dwgu1))
    (gx, dg1), chipsgu = _ffn_bwd_h(dx1, xf, g1, da1, db1, wg1, wu1, "ffn1_bwd_h_rest", (first, n_tiles), prev=begun,
                                    plan=_plan_chip_exchange(to_chips(dwgu1, sibgu, gate_up)))

    part = dict(ffn1_norm=dg1, mix_norm=dgm, ffn2_norm=dg2, b_forget=dbf, pool_scale=dps, out_norm_pool=donp,
                out_norm_attn=dona, qn=dqn, kn=dkn, pool_w=dpw.reshape(n_batch, -1, pool_w.shape[-1]), loss=lpart)
    mine = (own_rows(dwgu1, sibgu, chipsgu, gate_up) + own_rows(dwd1, sibd, chipsd, down)
            + own_rows(dwm, sibm, chipsm, MIX_NAMES))
    last = _run_plan(_merge_plans(_plan_sibling_share(mine), _plan_all_to_all(_small_pack(part, d, width))), "last_exchange")
    vstack, pstack = last[len(mine):]
    g_vec, g_pw = _small_sum(vstack, pstack, jnp.reshape(4 * mesh_x + 2 * mesh_y + mesh_c, (1,)).astype(jnp.int32))
    loss = g_vec[5, 0]
    reduced = dict(zip(FFN1_NAMES + MIX_NAMES + FFN2_NAMES, list(last[:len(mine)]) + list(red2)))
    reduced["w_in"] = lax.dynamic_slice(reduced["w_in"], ((in_rows * ids[0]) % 8, 0), (in_rows, d))

    grads, delta, new_m, new_v = {}, {}, {}, {}
    for names in (FFN2_NAMES, FFN1_NAMES, ("w_in",), ("w_out",)):
        stepped, _ = _adamw([work(w[n], n) for n in names], [reduced[n] for n in names], [work(m[n], n) for n in names],
                            [work(v[n], n) for n in names], names[0])
        for n, step in zip(names, stepped):
            grads[n], delta[n], new_m[n], new_v[n] = (work(a, n) for a in (reduced[n], *step))
    flat_pw = lambda a: a.reshape(-1, a.shape[-1])
    ((d_pw, m_pw, v_pw),), _ = _adamw([flat_pw(pool_w)], [g_pw], [flat_pw(m_pool_w)], [flat_pw(v_pool_w)], "pool_w")
    ((d_vec, m_vec, v_vec),), _ = _adamw([_pack_vec(w, d, width)], [g_vec], [_pack_vec(m, d, width)],
                                         [_pack_vec(v, d, width)], "vectors")
    grads.update(_unpack_vec(g_vec, width), pool_w=g_pw.reshape(pool_w.shape))
    delta.update(_unpack_vec(d_vec, width), pool_w=d_pw.reshape(pool_w.shape))
    new_m.update(_unpack_vec(m_vec, width), pool_w=m_pw.reshape(pool_w.shape))
    new_v.update(_unpack_vec(v_vec, width), pool_w=v_pw.reshape(pool_w.shape))
    return (loss, gx.reshape(x.shape), *[grads[n] for n in WEIGHT_NAMES], *[delta[n] for n in WEIGHT_NAMES],
            *[new_m[n] for n in WEIGHT_NAMES], *[new_v[n] for n in WEIGHT_NAMES])
```

```python
import functools

import jax
import jax.numpy as jnp
from jax import lax
from jax.experimental import pallas as pl
from jax.experimental.pallas import tpu as pltpu

F32 = jnp.float32
BF16 = jnp.bfloat16
EPS = 1e-6
NEG = -1e30
ADAM_LR = 0.001
ADAM_B1 = 0.9
ADAM_B2 = 0.999
ADAM_EPS = 1e-08
ADAM_WD = 0.01
ADAM_STEP = 10
POOL_WINDOWS = (2, 4, 8, 16)
HEAD_DIM = 64
N_HEADS = 8
LANES = 128
N_CHIPS = 4
ATT_BLOCK = 512
ATT_SUB = 128
FFN_TILE = 1024
VMEM_LIMIT = 62 * 1024 * 1024
ANY = pl.BlockSpec(memory_space=pl.ANY)
VM = pl.BlockSpec(memory_space=pltpu.VMEM)


def _params(**kw):
    return pltpu.CompilerParams(vmem_limit_bytes=VMEM_LIMIT, **kw)


def _dot(a, b):
    return jnp.dot(a, b, preferred_element_type=F32)


def _dot_nt(a, b):
    return lax.dot_general(a, b, (((1,), (1,)), ((), ())), preferred_element_type=F32)


def _dot_tn(a, b):
    return lax.dot_general(a, b, (((0,), (0,)), ((), ())), preferred_element_type=F32)


def _sigmoid(z):
    return 1.0 / (1.0 + jnp.exp(-z))


def _rms(xf):
    return lax.rsqrt(jnp.mean(xf * xf, axis=-1, keepdims=True) + EPS)


def _rms_bwd(xf, r, gain, dh):
    xh = xf * r
    dyg = dh * gain
    return r * (dyg - xh * jnp.mean(dyg * xh, axis=-1, keepdims=True)), dh * xh


def _total(v):
    return jnp.sum(jnp.sum(v, axis=1, keepdims=True), axis=0, keepdims=True)


def _ffn_fwd(x, gain, wg, wu, wd, target=None, plan=None):
    t, d = x.shape
    nch, fc, _ = wg.shape
    tm = min(FFN_TILE, t)
    nt = t // tm
    with_loss = target is not None

    def body(*refs):
        if with_loss:
            x_ref, g_ref, wg_ref, wu_ref, wd_ref, t_ref, o_ref, h_ref, a_ref, b_ref, s_ref, l_ref, acc_ref = refs
        else:
            x_ref, g_ref, wg_ref, wu_ref, wd_ref, o_ref, h_ref, a_ref, b_ref, s_ref, acc_ref = refs
        k = pl.program_id(1)

        @pl.when(k == 0)
        def _():
            xf = x_ref[...]
            h_ref[...] = ((xf * _rms(xf)) * g_ref[...]).astype(BF16)
            acc_ref[...] = jnp.zeros_like(acc_ref)

        for rows in _row_halves(tm):
            h = h_ref[rows, :]
            a = _dot_nt(h, wg_ref[...])
            b = _dot_nt(h, wu_ref[...])
            sb = ((a * (0.5 * jnp.tanh(0.5 * a) + 0.5)) * b).astype(BF16)
            a_ref[rows, :] = a.astype(BF16)
            b_ref[rows, :] = b.astype(BF16)
            s_ref[rows, :] = sb
            acc_ref[rows, :] += _dot(sb, wd_ref[...])

        @pl.when(k == nch - 1)
        def _():
            y = x_ref[...] + 0.5 * acc_ref[...]
            if with_loss:
                e = y - t_ref[...]
                o_ref[...] = e * (1.0 / d)
                l_ref[...] = jnp.broadcast_to(_total(e * e) * (0.5 / d), l_ref.shape)
            else:
                o_ref[...] = y

    row = pl.BlockSpec((tm, d), lambda i, k: (i, 0))
    chunk = pl.BlockSpec((None, fc, d), lambda i, k: (k, 0, 0))
    act = pl.BlockSpec((None, tm, fc), lambda i, k: (k, i, 0))
    in_specs = [row, pl.BlockSpec((1, d), lambda i, k: (0, 0)), chunk, chunk, chunk]
    out_shape = [jax.ShapeDtypeStruct((t, d), F32), jax.ShapeDtypeStruct((t, d), BF16)]
    out_shape += [jax.ShapeDtypeStruct((nch, t, fc), BF16)] * 3
    out_specs = [row, row, act, act, act]
    args = [x, gain, wg, wu, wd]
    if with_loss:
        in_specs.append(row)
        args.append(target)
        out_shape.append(jax.ShapeDtypeStruct((nt, 8, LANES), F32))
        out_specs.append(pl.BlockSpec((None, 8, LANES), lambda i, k: (i, 0, 0)))
    return _pallas(body, name="ffn_fwd_loss" if with_loss else "ffn_fwd", args=args, in_specs=in_specs,
                   out_shape=out_shape, out_specs=out_specs, grid=(nt, nch),
                   scratch_shapes=[pltpu.VMEM((tm, d), F32)], plan=plan)


def _row_halves(n):
    return [slice(0, n // 2), slice(n // 2, n)]


def _swiglu_grads(dy_ref, a_ref, b_ref, wd_ref, rows):
    ds = _dot_nt(dy_ref[rows, :].astype(BF16), wd_ref[...])
    av = a_ref[rows, :].astype(F32)
    bv = b_ref[rows, :].astype(F32)
    th = jnp.tanh(0.5 * av)
    half_sig = 0.25 * th + 0.25
    dab = ((ds * bv) * (half_sig * (1.0 + av * (0.5 - 0.5 * th)))).astype(BF16)
    return dab, (ds * (av * half_sig)).astype(BF16)


def _ffn_bwd_a(dy, a, b, wd, name, plan=None):
    t, d = dy.shape
    nch, fc, _ = wd.shape
    tm = min(FFN_TILE, t)

    def body(dy_ref, a_ref, b_ref, wd_ref, da_ref, db_ref):
        for rows in _row_halves(tm):
            da_ref[rows, :], db_ref[rows, :] = _swiglu_grads(dy_ref, a_ref, b_ref, wd_ref, rows)

    act = pl.BlockSpec((None, tm, fc), lambda i, k: (k, i, 0))
    return _pallas(
        body, name=name, args=[dy, a, b, wd], out_shape=[jax.ShapeDtypeStruct((nch, t, fc), BF16)] * 2, grid=(t // tm, nch),
        in_specs=[pl.BlockSpec((tm, d), lambda i, k: (i, 0)), act, act, pl.BlockSpec((None, fc, d), lambda i, k: (k, 0, 0))],
        out_specs=[act, act], plan=plan)


def _ffn_bwd_h(dy, x, gain, da, db, wg, wu, name, tiles, prev=None, plan=None):
    t, d = x.shape
    nch, fc, _ = wg.shape
    tm = min(FFN_TILE, t)
    nt = t // tm
    t0, t1 = tiles

    def body(*refs):
        dy_ref, x_ref, g_ref, da_ref, db_ref, wg_ref, wu_ref = refs[:7]
        dx_ref, dg_ref, acc_ref = refs[-3:]
        k = pl.program_id(1)

        @pl.when(k == 0)
        def _():
            acc_ref[...] = jnp.zeros_like(acc_ref)

        acc_ref[...] += _dot(da_ref[...], wg_ref[...]) + _dot(db_ref[...], wu_ref[...])

        @pl.when(k == nch - 1)
        def _():
            xf = x_ref[...]
            dxn, dgr = _rms_bwd(xf, _rms(xf), g_ref[...], acc_ref[...])
            dx_ref[...] = dy_ref[...] + dxn
            dg_ref[...] = jnp.sum(dgr, axis=0, keepdims=True)

    row = pl.BlockSpec((tm, d), lambda i, k: (i + t0, 0))
    chunk = pl.BlockSpec((None, fc, d), lambda i, k: (k, 0, 0))
    act = pl.BlockSpec((None, tm, fc), lambda i, k: (k, i + t0, 0))
    args = [dy, x, gain, da, db, wg, wu]
    in_specs = [row, row, pl.BlockSpec((1, d), lambda i, k: (0, 0)), act, act, chunk, chunk]
    aliases = {}
    if prev is not None:
        aliases = {len(args): 0, len(args) + 1: 1}
        args += list(prev)
        in_specs += [ANY, ANY]
    return _pallas(
        body, name=name, args=args, out_shape=[jax.ShapeDtypeStruct((t, d), F32), jax.ShapeDtypeStruct((nt, 1, d), F32)],
        grid=(t1 - t0, nch), in_specs=in_specs,
        out_specs=[row, pl.BlockSpec((None, 1, d), lambda i, k: (i + t0, 0, 0))],
        scratch_shapes=[pltpu.VMEM((tm, d), F32)], plan=plan, aliases=aliases)


def _ffn_bwd_x(dy, x, gain, a, b, wg, wu, wd, name, plan=None):
    t, d = x.shape
    nch, fc, _ = wg.shape
    tm = min(FFN_TILE, t)
    nt = t // tm

    def body(dy_ref, x_ref, g_ref, a_ref, b_ref, wg_ref, wu_ref, wd_ref, dx_ref, da_ref, db_ref, dg_ref, acc_ref):
        k = pl.program_id(1)

        @pl.when(k == 0)
        def _():
            acc_ref[...] = jnp.zeros_like(acc_ref)

        for rows in _row_halves(tm):
            dab, dbb = _swiglu_grads(dy_ref, a_ref, b_ref, wd_ref, rows)
            da_ref[rows, :] = dab
            db_ref[rows, :] = dbb
            acc_ref[rows, :] += _dot(dab, wg_ref[...]) + _dot(dbb, wu_ref[...])

        @pl.when(k == nch - 1)
        def _():
            xf = x_ref[...]
            dxn, dgr = _rms_bwd(xf, _rms(xf), g_ref[...], acc_ref[...])
            dx_ref[...] = dy_ref[...] + dxn
            dg_ref[...] = jnp.sum(dgr, axis=0, keepdims=True)

    row = pl.BlockSpec((tm, d), lambda i, k: (i, 0))
    chunk = pl.BlockSpec((None, fc, d), lambda i, k: (k, 0, 0))
    act = pl.BlockSpec((None, tm, fc), lambda i, k: (k, i, 0))
    return _pallas(
        body, name=name, args=[dy, x, gain, a, b, wg, wu, wd],
        out_shape=[jax.ShapeDtypeStruct((t, d), F32), jax.ShapeDtypeStruct((nch, t, fc), BF16),
                   jax.ShapeDtypeStruct((nch, t, fc), BF16), jax.ShapeDtypeStruct((nt, 1, d), F32)],
        grid=(nt, nch),
        in_specs=[row, row, pl.BlockSpec((1, d), lambda i, k: (0, 0)), act, act, chunk, chunk, chunk],
        out_specs=[row, act, act, pl.BlockSpec((None, 1, d), lambda i, k: (i, 0, 0))],
        scratch_shapes=[pltpu.VMEM((tm, d), F32)], plan=plan)


def _ffn_bwd_w(pairs, name, plan=None):
    n = len(pairs)
    nch, t, fc = pairs[0][0].shape
    d = pairs[0][1].shape[1]
    tm = min(1024, t)

    def body(*refs):
        @pl.when(pl.program_id(1) == 0)
        def _():
            for o_ref in refs[2 * n:]:
                o_ref[...] = jnp.zeros_like(o_ref)

        for j, (_, _, scale) in enumerate(pairs):
            other = refs[n + j][...]
            if other.dtype != BF16:
                other = (scale * other).astype(BF16)
            refs[2 * n + j][...] += _dot_tn(refs[j][...], other)

    row = pl.BlockSpec((tm, d), lambda k, i: (i, 0))
    act = pl.BlockSpec((None, tm, fc), lambda k, i: (k, i, 0))
    chunk = pl.BlockSpec((None, fc, d), lambda k, i: (k, 0, 0))
    return _pallas(body, name=name, args=[p[0] for p in pairs] + [p[1] for p in pairs],
                   out_shape=[jax.ShapeDtypeStruct((nch, fc, d), F32)] * n, grid=(nch, t // tm),
                   in_specs=[act] * n + [row] * n, out_specs=[chunk] * n, plan=plan)


def _head_masks():
    lane = lax.broadcasted_iota(jnp.int32, (1, LANES), 1)
    return lane < HEAD_DIM


def _head_rms(x, lo):
    x2 = x * x
    s0 = jnp.sum(jnp.where(lo, x2, 0.0), axis=1, keepdims=True)
    s1 = jnp.sum(jnp.where(lo, 0.0, x2), axis=1, keepdims=True)
    return jnp.where(lo, lax.rsqrt(s0 * (1.0 / HEAD_DIM) + EPS), lax.rsqrt(s1 * (1.0 / HEAD_DIM) + EPS))


def _head_mean(v, lo):
    s0 = jnp.sum(jnp.where(lo, v, 0.0), axis=1, keepdims=True)
    s1 = jnp.sum(jnp.where(lo, 0.0, v), axis=1, keepdims=True)
    return jnp.where(lo, s0, s1) * (1.0 / HEAD_DIM)


def _mix_proj(x1, gain, wt, qn, kn, pool_width, attn_width):
    t, d = x1.shape
    tm = min(512, t)
    nt = t // tm
    scale = HEAD_DIM ** -0.5
    c_q, c_k, c_v = pool_width, pool_width + attn_width, pool_width + 2 * attn_width
    c_f = c_v + attn_width

    def body(x_ref, g_ref, wt_ref, qn_ref, kn_ref, hm_ref, pv_ref, q_ref, k_ref, qh_ref, kh_ref, vb_ref, f_ref):
        lo = _head_masks()
        for rows in _row_halves(tm):
            xf = x_ref[rows, :]
            hm = ((xf * _rms(xf)) * g_ref[...]).astype(BF16)
            hm_ref[rows, :] = hm
            f_ref[rows, :] = _dot_nt(hm, wt_ref[c_f:c_f + LANES, :])
            pv_ref[rows, :] = _dot_nt(hm, wt_ref[0:pool_width, :])
            vb_ref[rows, :] = _dot_nt(hm, wt_ref[c_v:c_v + attn_width, :]).astype(BF16)
            for c0, raw_ref, hat_ref, n_ref, mul in ((c_q, q_ref, qh_ref, qn_ref, scale), (c_k, k_ref, kh_ref, kn_ref, 1.0)):
                raw = _dot_nt(hm, wt_ref[c0:c0 + attn_width, :])
                raw_ref[rows, :] = raw
                for blk in range(attn_width // LANES):
                    sl = slice(blk * LANES, (blk + 1) * LANES)
                    xb = raw[:, sl]
                    hat_ref[rows, sl] = (((xb * _head_rms(xb, lo)) * n_ref[:, sl]) * mul).astype(BF16)

    row = pl.BlockSpec((tm, d), lambda i: (i, 0))
    half = pl.BlockSpec((tm, attn_width), lambda i: (i, 0))
    const = lambda shape: pl.BlockSpec(shape, lambda i: (0, 0))
    return _pallas(
        body, name="mix_proj", args=[x1, gain, wt, qn, kn],
        out_shape=[jax.ShapeDtypeStruct((t, d), BF16), jax.ShapeDtypeStruct((t, pool_width), F32),
                   jax.ShapeDtypeStruct((t, attn_width), F32), jax.ShapeDtypeStruct((t, attn_width), F32),
                   jax.ShapeDtypeStruct((t, attn_width), BF16), jax.ShapeDtypeStruct((t, attn_width), BF16),
                   jax.ShapeDtypeStruct((t, attn_width), BF16), jax.ShapeDtypeStruct((t, LANES), F32)],
        grid=(nt,),
        in_specs=[row, const((1, d)), const(wt.shape), const((1, attn_width)), const((1, attn_width))],
        out_specs=[row, pl.BlockSpec((tm, pool_width), lambda i: (i, 0)), half, half, half, half, half,
                   pl.BlockSpec((tm, LANES), lambda i: (i, 0))])[0]


def _shift_down(v, dist, row):
    return jnp.where(row >= dist, pltpu.roll(v, dist, 0), 0.0)


def _shift_up(v, dist, row, n):
    return jnp.where(row + dist < n, pltpu.roll(v, n - dist, 0), 0.0)


def _aug_lane(e):
    return HEAD_DIM if e == 0 else 0


def _forget_prefix(f, bias, qh, kh, n_batch, seq):
    def body(f_ref, b_ref, q_ref, k_ref, qa_ref, ka_ref):
        z = f_ref[...] + b_ref[...]
        acc = jnp.minimum(z, 0.0) - jnp.log(1.0 + jnp.exp(-jnp.abs(z)))
        row = lax.broadcasted_iota(jnp.int32, (seq, 1), 0)
        dist = 1
        while dist < seq:
            acc = acc + _shift_down(acc, dist, row)
            dist *= 2
        lane = lax.broadcasted_iota(jnp.int32, (1, LANES), 1)
        for h in range(N_HEADS):
            pair, e = divmod(h, 2)
            a0 = _aug_lane(e)
            own = (lane < HEAD_DIM) if e == 0 else (lane >= HEAD_DIM)
            fh = _pick_lane(acc, h)
            hi = fh.astype(BF16).astype(F32)
            rest = fh - hi
            mid = rest.astype(BF16).astype(F32)
            low = rest - mid
            q_ones = (lane >= a0 + 3) & (lane < a0 + 6)
            k_ones = (lane >= a0) & (lane < a0 + 3)
            q_aug = jnp.where(lane == a0, hi, jnp.where(lane == a0 + 1, mid, jnp.where(lane == a0 + 2, low,
                              jnp.where(q_ones, 1.0, 0.0))))
            k_aug = jnp.where(k_ones, 1.0, jnp.where(lane == a0 + 3, -hi, jnp.where(lane == a0 + 4, -mid,
                              jnp.where(lane == a0 + 5, -low, 0.0))))
            src = slice(pair * LANES, (pair + 1) * LANES)
            dst = slice(h * LANES, (h + 1) * LANES)
            qa_ref[:, dst] = jnp.where(own, q_ref[:, src].astype(F32), q_aug).astype(BF16)
            ka_ref[:, dst] = jnp.where(own, k_ref[:, src].astype(F32), k_aug).astype(BF16)

    width = qh.shape[1]
    tok = pl.BlockSpec((seq, width), lambda b: (b, 0))
    aug = pl.BlockSpec((seq, N_HEADS * LANES), lambda b: (b, 0))
    return pl.pallas_call(
        body, out_shape=[jax.ShapeDtypeStruct((n_batch * seq, N_HEADS * LANES), BF16)] * 2, grid=(n_batch,),
        in_specs=[pl.BlockSpec((seq, LANES), lambda b: (b, 0)), pl.BlockSpec((1, LANES), lambda b: (0, 0)), tok, tok],
        out_specs=[aug, aug], compiler_params=_params(), name="forget_prefix",
    )(f, bias, qh, kh)


def _pool_groups(pv_ref, pw_ref, ps_ref, seq):
    row = lax.broadcasted_iota(jnp.int32, (seq, 1), 0)
    pos = (row + 1).astype(F32)
    out = []
    for g, win in enumerate(POOL_WINDOWS):
        sl = slice(g * LANES, (g + 1) * LANES)
        xg = pv_ref[:, sl]
        acc = xg
        dist = 1
        while dist < win:
            acc = acc + _shift_down(acc, dist, row)
            dist *= 2
        pooled = (acc / jnp.minimum(pos, float(win)) - xg).astype(BF16)
        mixed = _dot(pooled, pw_ref[g])
        out.append((pooled, mixed, mixed * ps_ref[:, sl]))
    return out


def _pool_fwd(pv, pw, ps, onp, n_batch, seq):
    width = pv.shape[1]

    def body(pv_ref, pw_ref, ps_ref, on_ref, y_ref):
        groups = _pool_groups(pv_ref, pw_ref, ps_ref, seq)
        ssq = sum(jnp.sum(ms * ms, axis=1, keepdims=True) for _, _, ms in groups)
        r = lax.rsqrt(ssq * (1.0 / width) + EPS)
        for g, (_, _, ms) in enumerate(groups):
            sl = slice(g * LANES, (g + 1) * LANES)
            y_ref[:, sl] = ((ms * r) * on_ref[:, sl]).astype(BF16)

    return pl.pallas_call(
        body, out_shape=jax.ShapeDtypeStruct((n_batch * seq, width), BF16), grid=(n_batch,),
        in_specs=[pl.BlockSpec((seq, width), lambda b: (b, 0)), pl.BlockSpec(pw.shape, lambda b: (0, 0, 0)),
                  pl.BlockSpec((1, width), lambda b: (0, 0)), pl.BlockSpec((1, width), lambda b: (0, 0))],
        out_specs=pl.BlockSpec((seq, width), lambda b: (b, 0)),
        compiler_params=_params(), name="pool_fwd",
    )(pv, pw, ps, onp)


def _pool_bwd(pv, dyp, pw, ps, onp, n_batch, seq):
    width = pv.shape[1]

    def body(pv_ref, dy_ref, pw_ref, ps_ref, on_ref, dpv_ref, dpw_ref, dps_ref, don_ref):
        groups = _pool_groups(pv_ref, pw_ref, ps_ref, seq)
        ssq = sum(jnp.sum(ms * ms, axis=1, keepdims=True) for _, _, ms in groups)
        r = lax.rsqrt(ssq * (1.0 / width) + EPS)
        mean = sum(jnp.sum((dy_ref[:, g * LANES:(g + 1) * LANES] * on_ref[:, g * LANES:(g + 1) * LANES]) * (ms * r),
                           axis=1, keepdims=True) for g, (_, _, ms) in enumerate(groups)) * (1.0 / width)
        row = lax.broadcasted_iota(jnp.int32, (seq, 1), 0)
        pos = (row + 1).astype(F32)
        for g, (pooled, mixed, ms) in enumerate(groups):
            sl = slice(g * LANES, (g + 1) * LANES)
            dy = dy_ref[:, sl]
            xh = ms * r
            don_ref[:, sl] = jnp.sum(dy * xh, axis=0, keepdims=True)
            dms = r * (dy * on_ref[:, sl] - xh * mean)
            dps_ref[:, sl] = jnp.sum(dms * mixed, axis=0, keepdims=True)
            dmix = (dms * ps_ref[:, sl]).astype(BF16)
            dpw_ref[g] = _dot_tn(pooled, dmix)
            dpool = _dot_nt(dmix, pw_ref[g])
            win = POOL_WINDOWS[g]
            acc = dpool / jnp.minimum(pos, float(win))
            dist = 1
            while dist < win:
                acc = acc + _shift_up(acc, dist, row, seq)
                dist *= 2
            dpv_ref[:, sl] = (acc - dpool).astype(BF16)

    tok = pl.BlockSpec((seq, width), lambda b: (b, 0))
    vec = pl.BlockSpec((1, width), lambda b: (0, 0))
    pvec = pl.BlockSpec((None, 1, width), lambda b: (b, 0, 0))
    return pl.pallas_call(
        body,
        out_shape=[jax.ShapeDtypeStruct((n_batch * seq, width), BF16),
                   jax.ShapeDtypeStruct((n_batch,) + pw.shape, F32),
                   jax.ShapeDtypeStruct((n_batch, 1, width), F32), jax.ShapeDtypeStruct((n_batch, 1, width), F32)],
        grid=(n_batch,),
        in_specs=[tok, tok, pl.BlockSpec(pw.shape, lambda b: (0, 0, 0)), vec, vec],
        out_specs=[tok, pl.BlockSpec((None,) + pw.shape, lambda b: (b, 0, 0, 0)), pvec, pvec],
        compiler_params=_params(), name="pool_bwd",
    )(pv, dyp, pw, ps, onp)


def _pick_lane(tile, idx):
    lane = lax.broadcasted_iota(jnp.int32, (1, LANES), 1)
    return jnp.sum(jnp.where(lane == idx, tile, 0.0), axis=1, keepdims=True)


def _pick_row(tile, idx):
    sub = lax.broadcasted_iota(jnp.int32, (tile.shape[0], 1), 0)
    return jnp.sum(jnp.where(sub == idx, tile, 0.0), axis=0, keepdims=True)


def _put_lane(col, idx):
    lane = lax.broadcasted_iota(jnp.int32, (1, LANES), 1)
    return jnp.where(lane == idx, col, 0.0)


def _head_select(e):
    lo = _head_masks()
    return lo if e == 0 else jnp.logical_not(lo)


def _causal(st, shift):
    row = lax.broadcasted_iota(jnp.int32, st.shape, 0)
    col = lax.broadcasted_iota(jnp.int32, st.shape, 1) + shift
    return jnp.where(col >= row, st, NEG)


def _transpose_blocks(a):
    rows, cols = a.shape
    return jnp.concatenate(
        [jnp.concatenate([a[r:r + LANES, c:c + LANES].T for r in range(0, rows, LANES)], axis=1)
         for c in range(0, cols, LANES)], axis=0)


def _stat_rows(ref, head, nsub):
    return jnp.concatenate([_pick_row(ref[a], head) for a in range(nsub)], axis=1)


def _accumulate(ref, value, first):
    @pl.when(first)
    def _():
        ref[...] = value

    @pl.when(jnp.logical_not(first))
    def _():
        ref[...] += value


def _attn_fwd(qa, ka, vb, n_batch, seq, plan=None):
    tq = min(ATT_BLOCK, seq)
    nq, nsub, tk = seq // tq, tq // ATT_SUB, tq
    pairs = vb.shape[1] // LANES

    def body(q_ref, k_ref, v_ref, o_ref, lse_ref, acc_ref):
        i, p = pl.program_id(1), pl.program_id(2)
        row_lo = lax.broadcasted_iota(jnp.int32, (LANES, 1), 0) < HEAD_DIM
        qs = [q_ref[:, e * LANES:(e + 1) * LANES] for e in range(2)]
        acc_ref[...] = jnp.zeros_like(acc_ref)

        def tile(off, stats, diagonal):
            vj = v_ref[pl.ds(off, tk), :]
            new, alphas, pvs = [], [], []
            for e in range(2):
                st = _dot_nt(k_ref[pl.ds(off, tk), e * LANES:(e + 1) * LANES], qs[e])
                if diagonal:
                    st = _causal(st, 0)
                m, l = stats[e]
                m_new = jnp.maximum(m, jnp.max(st, axis=0, keepdims=True))
                alpha = jnp.exp(m - m_new)
                pt = jnp.exp(st - m_new)
                new.append((m_new, alpha * l + jnp.sum(pt, axis=0, keepdims=True)))
                alphas.append(alpha)
                pvs.append(_dot_tn(jnp.where(_head_select(e), vj, jnp.zeros_like(vj)), pt.astype(BF16)))
            acc_ref[...] = acc_ref[...] * jnp.where(row_lo, alphas[0], alphas[1]) + (pvs[0] + pvs[1])
            return tuple(new)

        init = ((jnp.full((1, tq), NEG, F32), jnp.zeros((1, tq), F32)),) * 2
        stats = lax.fori_loop(0, i, lambda j, st: tile(pl.multiple_of(j * tk, tk), st, False), init)
        (m0, l0), (m1, l1) = tile(pl.multiple_of(i * tk, tk), stats, True)
        out_t = acc_ref[...] / jnp.where(row_lo, l0, l1)
        sub = lax.broadcasted_iota(jnp.int32, (8, 1), 0)
        lse0, lse1 = m0 + jnp.log(l0), m1 + jnp.log(l1)
        for a in range(nsub):
            sl = slice(a * ATT_SUB, (a + 1) * ATT_SUB)
            o_ref[sl, :] = out_t[:, sl].T
            rows = jnp.where(sub == 2 * p, lse0[:, sl], 0.0) + jnp.where(sub == 2 * p + 1, lse1[:, sl], 0.0)
            _accumulate(lse_ref.at[a], rows, p == 0)

    return _pallas(
        body, name="attn_fwd", args=[qa, ka, vb],
        out_shape=[jax.ShapeDtypeStruct((n_batch * seq, pairs * LANES), F32),
                   jax.ShapeDtypeStruct((n_batch * seq // ATT_SUB, 8, ATT_SUB), F32)],
        grid=(n_batch, nq, pairs),
        in_specs=[pl.BlockSpec((tq, 2 * LANES), lambda b, i, p: (b * nq + i, p)),
                  pl.BlockSpec((seq, 2 * LANES), lambda b, i, p: (b, p)),
                  pl.BlockSpec((seq, LANES), lambda b, i, p: (b, p))],
        out_specs=[pl.BlockSpec((tq, LANES), lambda b, i, p: (b * nq + i, p)),
                   pl.BlockSpec((nsub, 8, ATT_SUB), lambda b, i, p: (b * nq + i, 0, 0))],
        scratch_shapes=[pltpu.VMEM((LANES, tq), F32)], plan=plan)


def _attn_bwd_q(qa, ka, vb, do, lse, delta, n_batch, seq, plan=None):
    tq = min(ATT_BLOCK, seq)
    nq, nsub, tk = seq // tq, tq // ATT_SUB, tq
    pairs = vb.shape[1] // LANES

    def body(q_ref, k_ref, v_ref, do_ref, lse_ref, dl_ref, dq_ref, dfq_ref, acc0_ref, acc1_ref):
        i, p = pl.program_id(1), pl.program_id(2)
        accs = (acc0_ref, acc1_ref)
        qs = [q_ref[:, e * LANES:(e + 1) * LANES] for e in range(2)]
        dov = do_ref[...]
        ls = [_stat_rows(lse_ref, 2 * p + e, nsub) for e in range(2)]
        dl = [_stat_rows(dl_ref, 2 * p + e, nsub) for e in range(2)]
        for acc in accs:
            acc[...] = jnp.zeros_like(acc)

        def tile(off, diagonal):
            vj = v_ref[pl.ds(off, tk), :]
            for e in range(2):
                kj = k_ref[pl.ds(off, tk), e * LANES:(e + 1) * LANES]
                st = _dot_nt(kj, qs[e])
                if diagonal:
                    st = _causal(st, 0)
                pt = jnp.exp(st - ls[e])
                dpt = _dot_nt(jnp.where(_head_select(e), vj, jnp.zeros_like(vj)), dov)
                accs[e][...] += _dot(_transpose_blocks(kj), (pt * (dpt - dl[e])).astype(BF16))

        def step(j, carry):
            tile(pl.multiple_of(j * tk, tk), False)
            return carry

        lax.fori_loop(0, i, step, 0)
        tile(pl.multiple_of(i * tk, tk), True)
        dq0, dq1 = _transpose_blocks(acc0_ref[...]), _transpose_blocks(acc1_ref[...])
        dq_ref[...] = jnp.where(_head_masks(), dq0, dq1)
        dfq = _put_lane(_pick_lane(dq0, _aug_lane(0)), 2 * p) + _put_lane(_pick_lane(dq1, _aug_lane(1)), 2 * p + 1)
        _accumulate(dfq_ref, dfq, p == 0)

    stat = pl.BlockSpec((nsub, 8, ATT_SUB), lambda b, i, p: (b * nq + i, 0, 0))
    blk = pl.BlockSpec((tq, LANES), lambda b, i, p: (b * nq + i, p))
    return _pallas(
        body, name="attn_bwd_q", args=[qa, ka, vb, do, lse, delta],
        out_shape=[jax.ShapeDtypeStruct((n_batch * seq, pairs * LANES), F32), jax.ShapeDtypeStruct((n_batch * seq, LANES), F32)],
        grid=(n_batch, nq, pairs),
        in_specs=[pl.BlockSpec((tq, 2 * LANES), lambda b, i, p: (b * nq + i, p)),
                  pl.BlockSpec((seq, 2 * LANES), lambda b, i, p: (b, p)),
                  pl.BlockSpec((seq, LANES), lambda b, i, p: (b, p)), blk, stat, stat],
        out_specs=[blk, pl.BlockSpec((tq, LANES), lambda b, i, p: (b * nq + i, 0))],
        scratch_shapes=[pltpu.VMEM((LANES, tq), F32), pltpu.VMEM((LANES, tq), F32)], plan=plan)


def _attn_bwd_kv(qa, ka, vb, do, lse, delta, n_batch, seq, plan=None):
    tkb = min(ATT_BLOCK, seq)
    nk, nsub, tq = seq // tkb, tkb // ATT_SUB, tkb
    n_tiles = seq // ATT_SUB
    pairs = vb.shape[1] // LANES

    def body(q_ref, k_ref, v_ref, do_ref, lse_ref, dl_ref, dk_ref, dv_ref, dfk_ref, dk0_ref, dk1_ref, dva_ref):
        j, p = pl.program_id(1), pl.program_id(2)
        dks = (dk0_ref, dk1_ref)
        ks = [k_ref[:, e * LANES:(e + 1) * LANES] for e in range(2)]
        vj = v_ref[...]
        vs = [jnp.where(_head_select(e), vj, jnp.zeros_like(vj)) for e in range(2)]
        for acc in (dk0_ref, dk1_ref, dva_ref):
            acc[...] = jnp.zeros_like(acc)

        def tile(t, diagonal):
            off = pl.multiple_of(t * tq, tq)
            dov = do_ref[pl.ds(off, tq), :]
            for e in range(2):
                qe = q_ref[pl.ds(off, tq), e * LANES:(e + 1) * LANES]
                st = _dot_nt(ks[e], qe)
                if diagonal:
                    st = _causal(st, 0)
                rows = lambda ref: jnp.concatenate([_pick_row(ref[t * nsub + a], 2 * p + e) for a in range(nsub)], axis=1)
                pt = jnp.exp(st - rows(lse_ref))
                dva_ref[...] += _dot(pt.astype(BF16), jnp.where(_head_select(e), dov, jnp.zeros_like(dov)))
                dst = pt * (_dot_nt(vs[e], dov) - rows(dl_ref))
                dks[e][...] += _dot(dst.astype(BF16), qe)

        def step(t, carry):
            tile(t, False)
            return carry

        lax.fori_loop(j + 1, nk, step, 0)
        tile(j, True)
        dk0, dk1 = dk0_ref[...], dk1_ref[...]
        dk_ref[...] = jnp.where(_head_masks(), dk0, dk1)
        dv_ref[...] = dva_ref[...].astype(BF16)
        dfk = (_put_lane(_pick_lane(dk0, _aug_lane(0) + 3), 2 * p)
               + _put_lane(_pick_lane(dk1, _aug_lane(1) + 3), 2 * p + 1))
        _accumulate(dfk_ref, -dfk, p == 0)

    stat = pl.BlockSpec((n_tiles, 8, ATT_SUB), lambda b, j, p: (b, 0, 0))
    blk = pl.BlockSpec((tkb, LANES), lambda b, j, p: (b * nk + j, p))
    acc = pltpu.VMEM((tkb, LANES), F32)
    return _pallas(
        body, name="attn_bwd_kv", args=[qa, ka, vb, do, lse, delta],
        out_shape=[jax.ShapeDtypeStruct((n_batch * seq, pairs * LANES), F32),
                   jax.ShapeDtypeStruct((n_batch * seq, pairs * LANES), BF16),
                   jax.ShapeDtypeStruct((n_batch * seq, LANES), F32)],
        grid=(n_batch, nk, pairs),
        in_specs=[pl.BlockSpec((seq, 2 * LANES), lambda b, j, p: (b, p)),
                  pl.BlockSpec((tkb, 2 * LANES), lambda b, j, p: (b * nk + j, p)), blk,
                  pl.BlockSpec((seq, LANES), lambda b, j, p: (b, p)), stat, stat],
        out_specs=[blk, blk, pl.BlockSpec((tkb, LANES), lambda b, j, p: (b * nk + j, 0))],
        scratch_shapes=[acc, acc, acc], plan=plan)


def _forget_bwd(dfq, dfk, f, bias, n_batch, seq):
    def body(dfq_ref, dfk_ref, f_ref, b_ref, df_ref, db_ref):
        acc = dfq_ref[...] + dfk_ref[...]
        row = lax.broadcasted_iota(jnp.int32, (seq, 1), 0)
        dist = 1
        while dist < seq:
            acc = acc + _shift_up(acc, dist, row, seq)
            dist *= 2
        df = acc * _sigmoid(-(f_ref[...] + b_ref[...]))
        df_ref[...] = df
        db_ref[...] = jnp.sum(df, axis=0, keepdims=True)

    col = pl.BlockSpec((seq, LANES), lambda b: (b, 0))
    return pl.pallas_call(
        body,
        out_shape=[jax.ShapeDtypeStruct((n_batch * seq, LANES), F32), jax.ShapeDtypeStruct((n_batch, 1, LANES), F32)],
        grid=(n_batch,), in_specs=[col, col, col, pl.BlockSpec((1, LANES), lambda b: (0, 0))],
        out_specs=[col, pl.BlockSpec((None, 1, LANES), lambda b: (b, 0, 0))],
        compiler_params=_params(), name="forget_bwd",
    )(dfq, dfk, f, bias)


def _mix_out(x1, yp, o, ona, woa, wob):
    t, d = x1.shape
    width = o.shape[1]
    tm = min(512, t)

    def body(x_ref, yp_ref, o_ref, on_ref, wa_ref, wb_ref, x2_ref, ya_ref):
        of = o_ref[...]
        ya = ((of * _rms(of)) * on_ref[...]).astype(BF16)
        ya_ref[...] = ya
        x2_ref[...] = x_ref[...] + (_dot(yp_ref[...], wa_ref[...]) + _dot(ya, wb_ref[...]))

    row = pl.BlockSpec((tm, d), lambda i: (i, 0))
    half = pl.BlockSpec((tm, width), lambda i: (i, 0))
    wspec = pl.BlockSpec((width, d), lambda i: (0, 0))
    return pl.pallas_call(
        body, out_shape=[jax.ShapeDtypeStruct((t, d), F32), jax.ShapeDtypeStruct((t, width), BF16)],
        grid=(t // tm,), in_specs=[row, half, half, pl.BlockSpec((1, width), lambda i: (0, 0)), wspec, wspec],
        out_specs=[row, half], compiler_params=_params(), name="mix_out",
    )(x1, yp, o, ona, woa, wob)


def _mix_out_bwd(dx2, o, yp, ya, ona, woa, wob, plan=None):
    t, d = dx2.shape
    width = o.shape[1]
    tm = min(512, t)
    nt = t // tm

    def body(dx_ref, o_ref, yp_ref, ya_ref, on_ref, wa_ref, wb_ref, dyp_ref, do_ref, dl_ref, dwa_ref, dwb_ref, don_ref):
        @pl.when(pl.program_id(0) == 0)
        def _():
            dwa_ref[...] = jnp.zeros_like(dwa_ref)
            dwb_ref[...] = jnp.zeros_like(dwb_ref)

        dxb = dx_ref[...].astype(BF16)
        dwa_ref[...] += _dot_tn(yp_ref[...], dxb)
        dwb_ref[...] += _dot_tn(ya_ref[...], dxb)
        dyp_ref[...] = _dot_nt(dxb, wa_ref[...])
        of = o_ref[...]
        dov, dgr = _rms_bwd(of, _rms(of), on_ref[...], _dot_nt(dxb, wb_ref[...]))
        don_ref[...] = jnp.sum(dgr, axis=0, keepdims=True)
        do_ref[...] = dov.astype(BF16)
        lo = _head_masks()
        prod = dov * of
        delta = jnp.zeros((tm, LANES), F32)
        for blk in range(width // LANES):
            pb = prod[:, blk * LANES:(blk + 1) * LANES]
            delta = delta + _put_lane(jnp.sum(jnp.where(lo, pb, 0.0), axis=1, keepdims=True), 2 * blk)
            delta = delta + _put_lane(jnp.sum(jnp.where(lo, 0.0, pb), axis=1, keepdims=True), 2 * blk + 1)
        for c in range(tm // ATT_SUB):
            dl_ref[c] = delta[c * ATT_SUB:(c + 1) * ATT_SUB, :].T[0:8, :]

    row = pl.BlockSpec((tm, d), lambda i: (i, 0))
    half = pl.BlockSpec((tm, width), lambda i: (i, 0))
    wspec = pl.BlockSpec((width, d), lambda i: (0, 0))
    return _pallas(
        body, name="mix_out_bwd", args=[dx2, o, yp, ya, ona, woa, wob],
        out_shape=[jax.ShapeDtypeStruct((t, width), F32), jax.ShapeDtypeStruct((t, width), BF16),
                   jax.ShapeDtypeStruct((t // ATT_SUB, 8, ATT_SUB), F32), jax.ShapeDtypeStruct((width, d), F32),
                   jax.ShapeDtypeStruct((width, d), F32), jax.ShapeDtypeStruct((nt, 1, width), F32)],
        grid=(nt,),
        in_specs=[row, half, half, half, pl.BlockSpec((1, width), lambda i: (0, 0)), wspec, wspec],
        out_specs=[half, half, pl.BlockSpec((tm // ATT_SUB, 8, ATT_SUB), lambda i: (i, 0, 0)), wspec, wspec,
                   pl.BlockSpec((None, 1, width), lambda i: (i, 0, 0))], plan=plan)


def _mix_in_bwd(dx2, x1, gain, hm, dpv, dqh, q, dkh, k, dv, df, qn, kn, wt):
    t, d = x1.shape
    width = q.shape[1]
    pool_width = dpv.shape[1]
    tm = min(512, t)
    nt = t // tm
    scale = HEAD_DIM ** -0.5
    c_q, c_k, c_v = pool_width, pool_width + width, pool_width + 2 * width
    c_f = c_v + width

    def body(dx2_ref, x_ref, g_ref, hm_ref, dpv_ref, dqh_ref, q_ref, dkh_ref, k_ref, dv_ref, df_ref, qn_ref, kn_ref,
             wt_ref, dx_ref, dwt_ref, dg_ref, dqn_ref, dkn_ref):
        @pl.when(pl.program_id(0) == 0)
        def _():
            dwt_ref[...] = jnp.zeros_like(dwt_ref)

        lo = _head_masks()
        for part, rows in enumerate(_row_halves(tm)):
            def put(ref, sl, value):
                ref[:, sl] = value if part == 0 else ref[:, sl] + value

            hm = hm_ref[rows, :]
            pieces = [(0, dpv_ref[rows, :])]
            for c0, raw_ref, dh_ref, n_ref, dn_ref, mul in ((c_q, q_ref, dqh_ref, qn_ref, dqn_ref, scale),
                                                           (c_k, k_ref, dkh_ref, kn_ref, dkn_ref, 1.0)):
                cols = []
                for blk in range(width // LANES):
                    sl = slice(blk * LANES, (blk + 1) * LANES)
                    xb = raw_ref[rows, sl]
                    gb = dh_ref[rows, sl] * mul
                    r = _head_rms(xb, lo)
                    xh = xb * r
                    dyg = gb * n_ref[:, sl]
                    cols.append((r * (dyg - xh * _head_mean(dyg * xh, lo))).astype(BF16))
                    put(dn_ref, sl, jnp.sum(gb * xh, axis=0, keepdims=True))
                pieces.append((c0, jnp.concatenate(cols, axis=1)))
            pieces.append((c_v, dv_ref[rows, :]))
            pieces.append((c_f, df_ref[rows, :].astype(BF16)))
            dhm = jnp.zeros((tm // 2, d), F32)
            for c0, piece in pieces:
                dwt_ref[c0:c0 + piece.shape[1], :] += _dot_tn(piece, hm)
                dhm = dhm + _dot(piece, wt_ref[c0:c0 + piece.shape[1], :])
            xf = x_ref[rows, :]
            dxn, dgr = _rms_bwd(xf, _rms(xf), g_ref[...], dhm)
            dx_ref[rows, :] = dx2_ref[rows, :] + dxn
            put(dg_ref, slice(None), jnp.sum(dgr, axis=0, keepdims=True))

    row = pl.BlockSpec((tm, d), lambda i: (i, 0))
    half = pl.BlockSpec((tm, width), lambda i: (i, 0))
    const = lambda shape: pl.BlockSpec(shape, lambda i: (0, 0))
    pvec = lambda n: pl.BlockSpec((None, 1, n), lambda i: (i, 0, 0))
    return pl.pallas_call(
        body,
        out_shape=[jax.ShapeDtypeStruct((t, d), F32), jax.ShapeDtypeStruct(wt.shape, F32),
                   jax.ShapeDtypeStruct((nt, 1, d), F32),
                   jax.ShapeDtypeStruct((nt, 1, width), F32), jax.ShapeDtypeStruct((nt, 1, width), F32)],
        grid=(nt,),
        in_specs=[row, row, const((1, d)), row, pl.BlockSpec((tm, pool_width), lambda i: (i, 0)), half, half, half, half,
                  half, pl.BlockSpec((tm, LANES), lambda i: (i, 0)), const((1, width)), const((1, width)),
                  const(wt.shape)],
        out_specs=[row, const(wt.shape), pvec(d), pvec(width), pvec(width)],
        compiler_params=_params(), name="mix_in_bwd",
    )(dx2, x1, gain, hm, dpv, dqh, q, dkh, k, dv, df, qn, kn, wt)


def _mesh_pos():
    return lax.axis_index("x"), lax.axis_index("y"), lax.axis_index("c")


def _other_chips(x, y):
    return [(1 - x, y), (x, 1 - y), (1 - x, 1 - y)]


def _remote(src, dst, send_sem, recv_sem, device):
    return pltpu.make_async_remote_copy(src_ref=src, dst_ref=dst, send_sem=send_sem, recv_sem=recv_sem,
                                        device_id=device, device_id_type=pl.DeviceIdType.MESH)


def _half_rows(n_rows, which):
    half = n_rows // 2
    return pl.ds(pl.multiple_of(which * half, 8), half)


def _row_block(rows, cols, itemsize=4):
    rb = rows
    while rb * cols * itemsize > (1 << 20) and rb % 32 == 0:
        rb //= 2
    return rb


def _place_cast(ws, chip, tag):
    n = len(ws)
    rows, cols = ws[0].shape
    rb = _row_block(rows, cols)

    def body(k_ref, *refs):
        for w_ref, o_ref in zip(refs[:n], refs[n:]):
            o_ref[...] = w_ref[...].astype(BF16)

    return pl.pallas_call(
        body, out_shape=[jax.ShapeDtypeStruct((N_CHIPS, rows, cols), BF16)] * n,
        grid_spec=pltpu.PrefetchScalarGridSpec(
            num_scalar_prefetch=1, grid=(rows // rb,),
            in_specs=[pl.BlockSpec((rb, cols), lambda i, k: (i, 0))] * n,
            out_specs=[pl.BlockSpec((None, rb, cols), lambda i, k: (k[0], i, 0))] * n),
        compiler_params=_params(), name="place_" + tag,
    )(chip, *ws)


class _Plan:
    def __init__(self, ins, outs, alias, sems, start, finish):
        self.ins, self.outs, self.alias, self.sems, self.start, self.finish = ins, outs, alias, sems, start, finish


def _merge_plans(a, b):
    ni, no, ns = len(a.ins), len(a.outs), len(a.sems)
    alias = dict(a.alias)
    alias.update({ni + i: no + o for i, o in b.alias.items()})

    def both(which):
        def run(ins, outs, sems):
            getattr(a, which)(ins[:ni], outs[:no], sems[:ns])
            getattr(b, which)(ins[ni:], outs[no:], sems[ns:])
        return run

    return _Plan(list(a.ins) + list(b.ins), list(a.outs) + list(b.outs), alias, list(a.sems) + list(b.sems),
                 both("start"), both("finish"))


def _run_plan(plan, name):
    n_in, n_out = len(plan.ins), len(plan.outs)

    def body(*refs):
        parts = refs[:n_in], refs[n_in:n_in + n_out], refs[n_in + n_out:]
        plan.start(*parts)
        plan.finish(*parts)

    return pl.pallas_call(
        body, out_shape=plan.outs, in_specs=[ANY] * n_in, out_specs=[ANY] * n_out, scratch_shapes=plan.sems,
        input_output_aliases=plan.alias, name=name,
    )(*plan.ins)


def _pallas(body, *, name, args, in_specs, out_shape, out_specs, grid, scratch_shapes=(), plan=None, aliases=None):
    n_in, n_out, n_scr = len(args), len(out_shape), len(scratch_shapes)
    plan = plan or _Plan([], [], {}, [], None, None)
    p_in, p_out = len(plan.ins), len(plan.outs)

    def carrying(*refs):
        ins, p_ins = refs[:n_in], refs[n_in:n_in + p_in]
        o0 = n_in + p_in
        outs, p_outs = refs[o0:o0 + n_out], refs[o0 + n_out:o0 + n_out + p_out]
        s0 = o0 + n_out + p_out
        scr, p_sems = refs[s0:s0 + n_scr], refs[s0 + n_scr:]
        ids = [pl.program_id(a) for a in range(len(grid))]

        if plan.start is not None:
            @pl.when(functools.reduce(jnp.logical_and, [i == 0 for i in ids]))
            def _():
                plan.start(p_ins, p_outs, p_sems)

        body(*ins, *outs, *scr)

        if plan.finish is not None:
            @pl.when(functools.reduce(jnp.logical_and, [i == g - 1 for i, g in zip(ids, grid)]))
            def _():
                plan.finish(p_ins, p_outs, p_sems)

    aliases = dict(aliases or {})
    aliases.update({n_in + i: n_out + o for i, o in plan.alias.items()})
    res = pl.pallas_call(
        carrying, out_shape=list(out_shape) + list(plan.outs), grid=grid,
        in_specs=list(in_specs) + [ANY] * p_in, out_specs=list(out_specs) + [ANY] * p_out,
        scratch_shapes=list(scratch_shapes) + list(plan.sems),
        input_output_aliases=aliases, compiler_params=_params(), name=name,
    )(*args, *plan.ins)
    return list(res[:n_out]), list(res[n_out:])


def _plan_gather(stacks):
    n = len(stacks)
    relations = range(3)

    def ici_copies(outs, sems):
        x, y, c = _mesh_pos()
        chips = _other_chips(x, y)
        cps = []
        for w in range(n):
            own = outs[w].at[2 * x + y, _half_rows(stacks[w].shape[1], c)]
            cps += [_remote(own, own, sems[0].at[w, j], sems[1].at[w, j], (*chips[j], c)) for j in relations]
        return cps

    def start(ins, outs, sems):
        for cp in ici_copies(outs, sems):
            cp.start()

    def finish(ins, outs, sems):
        ici_send, ici_recv, d2d_send, d2d_recv = sems
        x, y, c = _mesh_pos()
        sibling = (x, y, 1 - c)
        slots = [2 * cx + cy for cx, cy in _other_chips(x, y)]
        forwards = []
        for w in range(n):
            rows = _half_rows(stacks[w].shape[1], c)
            for j in relations:
                landed = outs[w].at[slots[j], rows]
                _remote(landed, landed, ici_send.at[w, j], ici_recv.at[w, j], sibling).wait_recv()
                cp = _remote(landed, landed, d2d_send.at[w, j], d2d_recv.at[w, j], sibling)
                cp.start()
                forwards.append(cp)
        for w in range(n):
            rows = _half_rows(stacks[w].shape[1], 1 - c)
            for j in relations:
                landed = outs[w].at[slots[j], rows]
                _remote(landed, landed, d2d_send.at[w, j], d2d_recv.at[w, j], sibling).wait_recv()
        for cp in ici_copies(outs, sems) + forwards:
            cp.wait_send()

    return _Plan(stacks, [jax.ShapeDtypeStruct(s.shape, s.dtype) for s in stacks], {w: w for w in range(n)},
                 [pltpu.SemaphoreType.DMA((n, 3))] * 4, start, finish)


def _plan_gather_relay(stacks):
    n = len(stacks)

    def finish(ins, outs, sems):
        send, recv, relay_send, relay_recv, d2d_send, d2d_recv = sems
        x, y, c = _mesh_pos()
        sibling = (x, y, 1 - c)
        near = [(1 - x, y), (x, 1 - y)]
        far = 2 * (1 - x) + (1 - y)
        started = []

        def go(cp):
            cp.start()
            started.append(cp)

        def piece(w, slot, core, quarter=None):
            rh = stacks[w].shape[1] // 2
            if quarter is None:
                return outs[w].at[slot, _half_rows(2 * rh, core)]
            return outs[w].at[slot, pl.ds(pl.multiple_of(core * rh + quarter * (rh // 2), 8), rh // 2)]

        for w in range(n):
            own = piece(w, 2 * x + y, c)
            for j, chip in enumerate(near):
                go(_remote(own, own, send.at[w, j], recv.at[w, j], (*chip, c)))
        for w in range(n):
            for j, (cx, cy) in enumerate(near):
                landed = piece(w, 2 * cx + cy, c)
                _remote(landed, landed, send.at[w, j], recv.at[w, j], sibling).wait_recv()
                part = piece(w, 2 * cx + cy, c, quarter=j)
                go(_remote(part, part, relay_send.at[w, j], relay_recv.at[w, j], (*near[1 - j], c)))
                go(_remote(landed, landed, d2d_send.at[w, j], d2d_recv.at[w, j], sibling))
        for w in range(n):
            for j in range(2):
                part = piece(w, far, c, quarter=j)
                _remote(part, part, relay_send.at[w, j], relay_recv.at[w, j], sibling).wait_recv()
            landed = piece(w, far, c)
            go(_remote(landed, landed, d2d_send.at[w, 2], d2d_recv.at[w, 2], sibling))
        for w in range(n):
            for j, slot in enumerate([2 * cx + cy for cx, cy in near] + [far]):
                landed = piece(w, slot, 1 - c)
                _remote(landed, landed, d2d_send.at[w, j], d2d_recv.at[w, j], sibling).wait_recv()
        for cp in started:
            cp.wait_send()

    return _Plan(stacks, [jax.ShapeDtypeStruct(s.shape, s.dtype) for s in stacks], {w: w for w in range(n)},
                 [pltpu.SemaphoreType.DMA((n, 2))] * 4 + [pltpu.SemaphoreType.DMA((n, 3))] * 2,
                 lambda ins, outs, sems: None, finish)


def _plan_sibling_halves(gs):
    n = len(gs)

    def copies(ins, outs, sems):
        x, y, c = _mesh_pos()
        return [_remote(ins[w].at[:, _half_rows(gs[w].shape[1], 1 - c), :], outs[w], sems[0].at[w], sems[1].at[w],
                        (x, y, 1 - c)) for w in range(n)]

    def start(ins, outs, sems):
        for cp in copies(ins, outs, sems):
            cp.start()

    def finish(ins, outs, sems):
        for cp in copies(ins, outs, sems):
            cp.wait()

    return _Plan(gs, [jax.ShapeDtypeStruct((g.shape[0], g.shape[1] // 2, g.shape[2]), g.dtype) for g in gs], {},
                 [pltpu.SemaphoreType.DMA((n,))] * 2, start, finish)


def _plan_chip_exchange(ps):
    n = len(ps)

    def copies(ins, outs, sems):
        x, y, c = _mesh_pos()
        return [_remote(ins[w].at[2 * cx + cy], outs[w].at[j], sems[0].at[w, j], sems[1].at[w, j], (cx, cy, c))
                for w in range(n) for j, (cx, cy) in enumerate(_other_chips(x, y))]

    def start(ins, outs, sems):
        for cp in copies(ins, outs, sems):
            cp.start()

    def finish(ins, outs, sems):
        for cp in copies(ins, outs, sems):
            cp.wait()

    return _Plan(ps, [jax.ShapeDtypeStruct((3,) + p.shape[1:], p.dtype) for p in ps], {},
                 [pltpu.SemaphoreType.DMA((n, 3))] * 2, start, finish)


def _plan_sibling_share(gs):
    n = len(gs)

    def copies(outs, sems, which):
        x, y, c = _mesh_pos()
        cps = []
        for w in range(n):
            rows = outs[w].at[_half_rows(gs[w].shape[0], c if which == "mine" else 1 - c)]
            cps.append(_remote(rows, rows, sems[0].at[w], sems[1].at[w], (x, y, 1 - c)))
        return cps

    def start(ins, outs, sems):
        for cp in copies(outs, sems, "mine"):
            cp.start()

    def finish(ins, outs, sems):
        for cp in copies(outs, sems, "mine"):
            cp.wait_send()
        for cp in copies(outs, sems, "theirs"):
            cp.wait_recv()

    return _Plan(gs, [jax.ShapeDtypeStruct(g.shape, g.dtype) for g in gs], {w: w for w in range(n)},
                 [pltpu.SemaphoreType.DMA((n,))] * 2, start, finish)


def _same_shape_groups(arrays):
    groups = {}
    for i, a in enumerate(arrays):
        groups.setdefault(a.shape, []).append(i)
    return list(groups.values())


def _add_sibling(gs, r1s, ids, tag):
    n = len(gs)
    nch, rh, cols = r1s[0].shape

    def body(ids_ref, *refs):
        for g_ref, r_ref, o_ref in zip(refs[:n], refs[n:2 * n], refs[2 * n:]):
            o_ref[...] = (g_ref[...] + r_ref[...]).astype(BF16)

    blk = lambda fn: pl.BlockSpec((None, rh, cols), fn)
    return pl.pallas_call(
        body, out_shape=[jax.ShapeDtypeStruct(r1s[0].shape, BF16)] * n,
        grid_spec=pltpu.PrefetchScalarGridSpec(
            num_scalar_prefetch=1, grid=(nch,),
            in_specs=[blk(lambda k, ids: (k, ids[1], 0))] * n + [blk(lambda k, ids: (k, 0, 0))] * n,
            out_specs=[blk(lambda k, ids: (k, 0, 0))] * n),
        compiler_params=_params(), name="add_sibling_" + tag,
    )(ids, *gs, *r1s)


def _add_chips(gs, r1s, r2s, ids, tag):
    n = len(gs)
    _, rh, cols = r1s[0].shape
    nb = 2 if rh % 32 == 0 else 1
    rb = rh // nb

    def body(ids_ref, *refs):
        for g_ref, r1_ref, r2_ref, o_ref in zip(refs[:n], refs[n:2 * n], refs[2 * n:3 * n], refs[3 * n:]):
            own = g_ref[...] + r1_ref[...]
            o_ref[...] = ((own + r2_ref[0].astype(F32)) + r2_ref[1].astype(F32)) + r2_ref[2].astype(F32)

    return pl.pallas_call(
        body, out_shape=[jax.ShapeDtypeStruct((2 * rh, cols), F32)] * n,
        grid_spec=pltpu.PrefetchScalarGridSpec(
            num_scalar_prefetch=1, grid=(nb,),
            in_specs=[pl.BlockSpec((None, rb, cols), lambda i, ids: (ids[0], ids[1] * nb + i, 0))] * n
            + [pl.BlockSpec((None, rb, cols), lambda i, ids: (ids[0], i, 0))] * n
            + [pl.BlockSpec((3, rb, cols), lambda i, ids: (0, i, 0))] * n,
            out_specs=[pl.BlockSpec((rb, cols), lambda i, ids: (ids[1] * nb + i, 0))] * n),
        compiler_params=_params(), name="add_chips_" + tag,
    )(ids, *gs, *r1s, *r2s)


VEC_ROWS = 8


N_DEVICES = 8


def _small_pack(part, d, width):
    names = ("ffn1_norm", "mix_norm", "ffn2_norm", "pool_scale", "out_norm_pool", "out_norm_attn", "qn", "kn", "b_forget",
             "pool_w", "loss")
    args = [part[k] for k in names]
    pw_shape = part["pool_w"].shape[1:]

    def body(g1_ref, gm_ref, g2_ref, ps_ref, onp_ref, ona_ref, qn_ref, kn_ref, bf_ref, pw_ref, loss_ref, vbuf, pbuf):
        lo = _head_masks()

        def fold_heads(ref):
            v = jnp.sum(ref[...], axis=0)
            acc = jnp.zeros((VEC_ROWS, LANES), F32)
            for blk in range(width // LANES):
                vb = jnp.broadcast_to(v[:, blk * LANES:(blk + 1) * LANES], (VEC_ROWS, LANES))
                acc = acc + vb + pltpu.roll(vb, HEAD_DIM, 1)
            return jnp.where(lo, acc, 0.0)[0:1, :]

        vbuf[0] = jnp.zeros((VEC_ROWS, d), F32)
        vbuf[0, 0:1, :] = jnp.sum(g1_ref[...], axis=0)
        vbuf[0, 1:2, :] = jnp.sum(gm_ref[...], axis=0)
        vbuf[0, 2:3, :] = jnp.sum(g2_ref[...], axis=0)
        vbuf[0, 5:6, 0:LANES] = jnp.sum(loss_ref[...], axis=0)[0:1, :]
        vbuf[0, 3:4, 0:width] = jnp.sum(ps_ref[...], axis=0)
        vbuf[0, 3:4, width:2 * width] = jnp.sum(onp_ref[...], axis=0)
        vbuf[0, 4:5, 0:width] = jnp.sum(ona_ref[...], axis=0)
        vbuf[0, 4:5, width:width + LANES] = fold_heads(qn_ref)
        vbuf[0, 4:5, width + LANES:width + 2 * LANES] = fold_heads(kn_ref)
        vbuf[0, 4:5, width + 2 * LANES:width + 3 * LANES] = jnp.sum(bf_ref[...], axis=0)
        pbuf[0] = jnp.sum(pw_ref[...], axis=0)

    return pl.pallas_call(
        body, out_shape=[jax.ShapeDtypeStruct((N_DEVICES, VEC_ROWS, d), F32), jax.ShapeDtypeStruct((N_DEVICES,) + pw_shape, F32)],
        in_specs=[VM] * len(args), out_specs=[VM, VM], compiler_params=_params(), name="small_pack",
    )(*args)


def _plan_all_to_all(stacks):
    n = len(stacks)

    def copies(outs, sems):
        x, y, c = _mesh_pos()
        cps = []
        for r in range(1, N_DEVICES):
            peer = (x if not r & 4 else 1 - x, y if not r & 2 else 1 - y, c if not r & 1 else 1 - c)
            cps += [_remote(outs[w].at[0], outs[w].at[r], sems[0].at[w, r - 1], sems[1].at[w, r - 1], peer) for w in range(n)]
        return cps

    def start(ins, outs, sems):
        for cp in copies(outs, sems):
            cp.start()

    def finish(ins, outs, sems):
        for cp in copies(outs, sems):
            cp.wait()

    return _Plan(stacks, [jax.ShapeDtypeStruct(s.shape, s.dtype) for s in stacks], {w: w for w in range(n)},
                 [pltpu.SemaphoreType.DMA((n, N_DEVICES - 1))] * 2, start, finish)


def _small_sum(vstack, pstack, me):
    def body(me_ref, vbuf, pbuf, vec_ref, pw_ref):
        vec = vbuf[me_ref[0]]
        pw = pbuf[me_ref[0]]
        for dev in range(1, N_DEVICES):
            vec = vec + vbuf[jnp.bitwise_xor(me_ref[0], dev)]
            pw = pw + pbuf[jnp.bitwise_xor(me_ref[0], dev)]
        vec_ref[...] = vec
        pw_ref[...] = pw

    full = lambda s: pl.BlockSpec(s.shape, lambda i, me: (0,) * len(s.shape))
    outs = [jax.ShapeDtypeStruct(vstack.shape[1:], F32), jax.ShapeDtypeStruct(pstack.shape[1:], F32)]
    return pl.pallas_call(
        body, out_shape=outs,
        grid_spec=pltpu.PrefetchScalarGridSpec(num_scalar_prefetch=1, grid=(1,), in_specs=[full(vstack), full(pstack)],
                                               out_specs=[full(o) for o in outs]),
        compiler_params=_params(), name="small_sum",
    )(me, vstack, pstack)


def _adamw(ws, gs, ms, vs, tag):
    n = len(ws)
    rows, cols = ws[0].shape
    rb = rows
    while rb * cols * 4 * n > (1 << 20) and rb % 16 == 0:
        rb //= 2

    def body(*refs):
        for j in range(n):
            w_ref, g_ref, m_ref, v_ref = (refs[k * n + j] for k in range(4))
            go_ref, d_ref, mo_ref, vo_ref = (refs[(4 + k) * n + j] for k in range(4))
            gv = g_ref[...]
            go_ref[...] = gv
            m2 = ADAM_B1 * m_ref[...] + (1.0 - ADAM_B1) * gv
            v2 = ADAM_B2 * v_ref[...] + (1.0 - ADAM_B2) * (gv * gv)
            m_hat = m2 / (1.0 - ADAM_B1 ** ADAM_STEP)
            v_hat = v2 / (1.0 - ADAM_B2 ** ADAM_STEP)
            d_ref[...] = -ADAM_LR * (m_hat / (jnp.sqrt(v_hat) + ADAM_EPS) + ADAM_WD * w_ref[...])
            mo_ref[...] = m2
            vo_ref[...] = v2

    spec = pl.BlockSpec((rb, cols), lambda i: (i, 0))
    res, _ = _pallas(
        body, name="adamw_" + tag, args=[*ws, *gs, *ms, *vs], out_shape=[jax.ShapeDtypeStruct(ws[0].shape, F32)] * (4 * n),
        grid=(rows // rb,), in_specs=[spec] * (4 * n), out_specs=[spec] * (4 * n))
    return [tuple(res[k * n + j] for k in range(4)) for j in range(n)]


def _pack_vec(p, d, width):
    pad = lambda v: jnp.pad(v, (0, LANES - v.shape[0]))
    row3 = jnp.concatenate([p["pool_scale"], p["out_norm_pool"]])
    row4 = jnp.concatenate([p["out_norm_attn"], pad(p["q_norm"]), pad(p["k_norm"]), pad(p["b_forget"]),
                            jnp.zeros((d - width - 3 * LANES,), F32)])
    rows = [p["ffn1_norm"], p["mix_norm"], p["ffn2_norm"], row3, row4]
    return jnp.pad(jnp.stack(rows), ((0, VEC_ROWS - len(rows)), (0, 0)))


def _unpack_vec(vec, width):
    return dict(ffn1_norm=vec[0], mix_norm=vec[1], ffn2_norm=vec[2], pool_scale=vec[3, :width],
                out_norm_pool=vec[3, width:2 * width], out_norm_attn=vec[4, :width],
                q_norm=vec[4, width:width + HEAD_DIM], k_norm=vec[4, width + LANES:width + LANES + HEAD_DIM],
                b_forget=vec[4, width + 2 * LANES:width + 2 * LANES + N_HEADS])


WEIGHT_NAMES = ("ffn1_norm", "ffn1_w_gate", "ffn1_w_up", "ffn1_w_down", "mix_norm", "w_in", "b_forget", "pool_w",
                "pool_scale", "q_norm", "k_norm", "out_norm_pool", "out_norm_attn", "w_out", "ffn2_norm",
                "ffn2_w_gate", "ffn2_w_up", "ffn2_w_down")
BIG_NAMES = ("ffn1_w_gate", "ffn1_w_up", "ffn1_w_down", "w_in", "w_out", "ffn2_w_gate", "ffn2_w_up", "ffn2_w_down")
TRANSPOSED_NAMES = ("ffn1_w_gate", "ffn1_w_up", "w_in", "ffn2_w_gate", "ffn2_w_up")
FFN1_NAMES = ("ffn1_w_gate", "ffn1_w_up", "ffn1_w_down")
MIX_NAMES = ("w_in", "w_out")
FFN2_NAMES = ("ffn2_w_gate", "ffn2_w_up", "ffn2_w_down")


def kernel(x, ffn1_norm, ffn1_w_gate, ffn1_w_up, ffn1_w_down, mix_norm, w_in, b_forget, pool_w, pool_scale, q_norm, k_norm, out_norm_pool, out_norm_attn, w_out, ffn2_norm, ffn2_w_gate, ffn2_w_up, ffn2_w_down, loss_target, m_ffn1_norm, m_ffn1_w_gate, m_ffn1_w_up, m_ffn1_w_down, m_mix_norm, m_w_in, m_b_forget, m_pool_w, m_pool_scale, m_q_norm, m_k_norm, m_out_norm_pool, m_out_norm_attn, m_w_out, m_ffn2_norm, m_ffn2_w_gate, m_ffn2_w_up, m_ffn2_w_down, v_ffn1_norm, v_ffn1_w_gate, v_ffn1_w_up, v_ffn1_w_down, v_mix_norm, v_w_in, v_b_forget, v_pool_w, v_pool_scale, v_q_norm, v_k_norm, v_out_norm_pool, v_out_norm_attn, v_w_out, v_ffn2_norm, v_ffn2_w_gate, v_ffn2_w_up, v_ffn2_w_down):
    given = dict(locals())
    w = {n: given[n] for n in WEIGHT_NAMES}
    m = {n: given["m_" + n] for n in WEIGHT_NAMES}
    v = {n: given["v_" + n] for n in WEIGHT_NAMES}
    n_batch, seq, d = x.shape
    width = pool_scale.shape[0]
    in_rows = w_in.shape[1]
    in_cols = N_CHIPS * in_rows
    in_pad = -(-in_rows // 32) * 32
    in_cols_pad = in_cols - N_HEADS + LANES

    work = lambda a, n: a.T if n in TRANSPOSED_NAMES else a
    exchanged = lambda a, n: jnp.pad(a, ((0, in_pad - in_rows), (0, 0))) if n == "w_in" else a

    mesh_x, mesh_y, mesh_c = _mesh_pos()
    ids = jnp.stack([2 * mesh_x + mesh_y, mesh_c]).astype(jnp.int32)

    row = lambda a: a.reshape(1, -1)
    g1, gm, g2, ps, onp, ona = (row(a) for a in (ffn1_norm, mix_norm, ffn2_norm, pool_scale, out_norm_pool, out_norm_attn))
    qn, kn = row(jnp.tile(q_norm, N_HEADS)), row(jnp.tile(k_norm, N_HEADS))
    bf = row(jnp.pad(b_forget, (0, LANES - N_HEADS)))
    pwb = pool_w.astype(BF16)
    xf, tgt = x.reshape(n_batch * seq, d), loss_target.reshape(n_batch * seq, d)

    def grouped(call, names, *lists):
        out = [None] * len(names)
        for idx in _same_shape_groups(lists[0]):
            res = call(*[[lst[i] for i in idx] for lst in lists], names[idx[0]])
            for i, r in zip(idx, res):
                out[i] = r
        return out

    placed = dict(zip(BIG_NAMES, grouped(lambda ws, tag: _place_cast(ws, ids, tag), BIG_NAMES,
                                         [exchanged(work(w[n], n), n) for n in BIG_NAMES])))
    wg1, wu1, wd1 = _run_plan(_plan_gather_relay([placed[n] for n in FFN1_NAMES]), "gather_ffn1")
    (x1, h1, a1, b1, s1), (w_in_all, w_out_all, wd2) = _ffn_fwd(
        xf, g1, wg1, wu1, wd1, plan=_plan_gather([placed[n] for n in MIX_NAMES + FFN2_NAMES[2:]]))
    w_in_t = jnp.pad(w_in_all[:, :in_rows].reshape(in_cols, d), ((0, in_cols_pad - in_cols), (0, 0)))
    w_out_full = w_out_all.reshape(N_CHIPS * w_out.shape[0], d)
    woa, wob = w_out_full[:width], w_out_full[width:]

    hm, pv, q, k, qh, kh, vb, f = _mix_proj(x1, gm, w_in_t, qn, kn, width, width)
    qa, ka = _forget_prefix(f, bf, qh, kh, n_batch, seq)
    yp = _pool_fwd(pv, pwb, ps, onp, n_batch, seq)
    (o, lse), (wg2, wu2) = _attn_fwd(qa, ka, vb, n_batch, seq, plan=_plan_gather([placed[n] for n in FFN2_NAMES[:2]]))
    x2, ya = _mix_out(x1, yp, o, ona, woa, wob)
    (dy, h2, a2, b2, s2, lpart), _ = _ffn_fwd(x2, g2, wg2, wu2, wd2, target=tgt)

    def to_chips(gs, arrived, tags):
        return grouped(lambda g, r, tag: _add_sibling(g, r, ids, tag), tags, gs, arrived)

    def own_rows(gs, from_sibling, from_chips, tags):
        return grouped(lambda g, ra, rb, tag: _add_chips(g, ra, rb, ids, tag), tags, gs, from_sibling, from_chips)

    (dx2, da2, db2, dg2), _ = _ffn_bwd_x(dy, x2, g2, a2, b2, wg2, wu2, wd2, "ffn2_bwd_x")
    dw2, _ = _ffn_bwd_w([(da2, h2, 1.0), (db2, h2, 1.0), (s2, dy, 0.5)], "ffn2_bwd_w")
    (dyp, do, delta, dwoa, dwob, dona), sib2 = _mix_out_bwd(dx2, o, yp, ya, ona, woa, wob, plan=_plan_sibling_halves(dw2))
    dpv, dpw, dps, donp = _pool_bwd(pv, dyp, pwb, ps, onp, n_batch, seq)
    (dqh, dfq), chips2 = _attn_bwd_q(qa, ka, vb, do, lse, delta, n_batch, seq,
                                     plan=_plan_chip_exchange(to_chips(dw2, sib2, FFN2_NAMES)))
    (dkh, dv, dfk), red2 = _attn_bwd_kv(qa, ka, vb, do, lse, delta, n_batch, seq,
                                        plan=_plan_sibling_share(own_rows(dw2, sib2, chips2, FFN2_NAMES)))
    df, dbf = _forget_bwd(dfq, dfk, f, bf, n_batch, seq)
    dx1, dw_in_t, dgm, dqn, dkn = _mix_in_bwd(dx2, x1, gm, hm, dpv, dqh, q, dkh, k, dv, df, qn, kn, w_in_t)
    in_base = [in_rows * k // 8 * 8 for k in range(N_CHIPS)]
    d_w_in = jnp.stack([dw_in_t[b:b + in_pad] for b in in_base])
    d_w_out = jnp.concatenate([dwoa, dwob], axis=0).reshape(N_CHIPS, w_out.shape[0], d)
    dwm = [d_w_in, d_w_out]
    down, gate_up = FFN1_NAMES[2:], FFN1_NAMES[:2]
    dwd1, sibm = _ffn_bwd_w([(s1, dx1, 0.5)], "ffn1_bwd_w_down", plan=_plan_sibling_halves(dwm))
    (da1, db1), arrived = _ffn_bwd_a(dx1, a1, b1, wd1, "ffn1_bwd_a",
                                     plan=_merge_plans(_plan_sibling_halves(dwd1),
                                                       _plan_chip_exchange(to_chips(dwm, sibm, MIX_NAMES))))
    sibd, chipsm = arrived[:1], arrived[1:]
    dwgu1, chipsd = _ffn_bwd_w([(da1, h1, 1.0), (db1, h1, 1.0)], "ffn1_bwd_w_gate_up",
                               plan=_plan_chip_exchange(to_chips(dwd1, sibd, down)))
    n_tiles = (n_batch * seq) // min(FFN_TILE, n_batch * seq)
    first = max(n_tiles // 4, 1)
    begun, sibgu = _ffn_bwd_h(dx1, xf, g1, da1, db1, wg1, wu1, "ffn1_bwd_h_first", (0, first),
                              plan=_plan_sibling_halves(dwgu1))
    (gx, dg1), chipsgu = _ffn_bwd_h(dx1, xf, g1, da1, db1, wg1, wu1, "ffn1_bwd_h_rest", (first, n_tiles), prev=begun,
                                    plan=_plan_chip_exchange(to_chips(dwgu1, sibgu, gate_up)))

    part = dict(ffn1_norm=dg1, mix_norm=dgm, ffn2_norm=dg2, b_forget=dbf, pool_scale=dps, out_norm_pool=donp,
                out_norm_attn=dona, qn=dqn, kn=dkn, pool_w=dpw.reshape(n_batch, -1, pool_w.shape[-1]), loss=lpart)
    mine = (own_rows(dwgu1, sibgu, chipsgu, gate_up) + own_rows(dwd1, sibd, chipsd, down)
            + own_rows(dwm, sibm, chipsm, MIX_NAMES))
    last = _run_plan(_merge_plans(_plan_sibling_share(mine), _plan_all_to_all(_small_pack(part, d, width))), "last_exchange")
    vstack, pstack = last[len(mine):]
    g_vec, g_pw = _small_sum(vstack, pstack, jnp.reshape(4 * mesh_x + 2 * mesh_y + mesh_c, (1,)).astype(jnp.int32))
    loss = g_vec[5, 0]
    reduced = dict(zip(FFN1_NAMES + MIX_NAMES + FFN2_NAMES, list(last[:len(mine)]) + list(red2)))
    reduced["w_in"] = lax.dynamic_slice(reduced["w_in"], ((in_rows * ids[0]) % 8, 0), (in_rows, d))

    grads, delta, new_m, new_v = {}, {}, {}, {}
    for names in (FFN2_NAMES, FFN1_NAMES, ("w_in",), ("w_out",)):
        stepped = _adamw([work(w[n], n) for n in names], [reduced[n] for n in names], [work(m[n], n) for n in names],
                         [work(v[n], n) for n in names], names[0])
        for n, step in zip(names, stepped):
            grads[n], delta[n], new_m[n], new_v[n] = (work(a, n) for a in step)
    flat_pw = lambda a: a.reshape(-1, a.shape[-1])
    (_, d_pw, m_pw, v_pw), = _adamw([flat_pw(pool_w)], [g_pw], [flat_pw(m_pool_w)], [flat_pw(v_pool_w)], "pool_w")
    (_, d_vec, m_vec, v_vec), = _adamw([_pack_vec(w, d, width)], [g_vec], [_pack_vec(m, d, width)],
                                       [_pack_vec(v, d, width)], "vectors")
    grads.update(_unpack_vec(g_vec, width), pool_w=g_pw.reshape(pool_w.shape))
    delta.update(_unpack_vec(d_vec, width), pool_w=d_pw.reshape(pool_w.shape))
    new_m.update(_unpack_vec(m_vec, width), pool_w=m_pw.reshape(pool_w.shape))
    new_v.update(_unpack_vec(v_vec, width), pool_w=v_pw.reshape(pool_w.shape))
    return (loss, gx.reshape(x.shape), *[grads[n] for n in WEIGHT_NAMES], *[delta[n] for n in WEIGHT_NAMES],
            *[new_m[n] for n in WEIGHT_NAMES], *[new_v[n] for n in WEIGHT_NAMES])
```

```python
import functools

import jax
import jax.numpy as jnp
from jax import lax
from jax.experimental import pallas as pl
from jax.experimental.pallas import tpu as pltpu

F32 = jnp.float32
BF16 = jnp.bfloat16
EPS = 1e-6
NEG = -1e30
ADAM_LR = 0.001
ADAM_B1 = 0.9
ADAM_B2 = 0.999
ADAM_EPS = 1e-08
ADAM_WD = 0.01
ADAM_STEP = 10
POOL_WINDOWS = (2, 4, 8, 16)
HEAD_DIM = 64
N_HEADS = 8
LANES = 128
N_CHIPS = 4
ATT_BLOCK = 512
ATT_SUB = 128
FFN_TILE = 1024
VMEM_LIMIT = 62 * 1024 * 1024
ANY = pl.BlockSpec(memory_space=pl.ANY)
VM = pl.BlockSpec(memory_space=pltpu.VMEM)


def _params(**kw):
    return pltpu.CompilerParams(vmem_limit_bytes=VMEM_LIMIT, **kw)


def _dot(a, b):
    return jnp.dot(a, b, preferred_element_type=F32)


def _dot_nt(a, b):
    return lax.dot_general(a, b, (((1,), (1,)), ((), ())), preferred_element_type=F32)


def _dot_tn(a, b):
    return lax.dot_general(a, b, (((0,), (0,)), ((), ())), preferred_element_type=F32)


def _sigmoid(z):
    return 1.0 / (1.0 + jnp.exp(-z))


def _rms(xf):
    return lax.rsqrt(jnp.mean(xf * xf, axis=-1, keepdims=True) + EPS)


def _rms_bwd(xf, r, gain, dh):
    xh = xf * r
    dyg = dh * gain
    return r * (dyg - xh * jnp.mean(dyg * xh, axis=-1, keepdims=True)), dh * xh


def _total(v):
    return jnp.sum(jnp.sum(v, axis=1, keepdims=True), axis=0, keepdims=True)


def _ffn_fwd(x, gain, wg, wu, wd, target=None, plan=None):
    t, d = x.shape
    nch, fc, _ = wg.shape
    tm = min(FFN_TILE, t)
    nt = t // tm
    with_loss = target is not None

    def body(*refs):
        if with_loss:
            x_ref, g_ref, wg_ref, wu_ref, wd_ref, t_ref, o_ref, h_ref, a_ref, b_ref, s_ref, l_ref, oh_ref, acc_ref = refs
        else:
            x_ref, g_ref, wg_ref, wu_ref, wd_ref, o_ref, h_ref, a_ref, b_ref, s_ref, acc_ref = refs
        k = pl.program_id(1)

        @pl.when(k == 0)
        def _():
            xf = x_ref[...]
            h_ref[...] = ((xf * _rms(xf)) * g_ref[...]).astype(BF16)
            acc_ref[...] = jnp.zeros_like(acc_ref)

        for rows in _row_halves(tm):
            h = h_ref[rows, :]
            a = _dot_nt(h, wg_ref[...])
            b = _dot_nt(h, wu_ref[...])
            sb = ((a * (0.5 * jnp.tanh(0.5 * a) + 0.5)) * b).astype(BF16)
            a_ref[rows, :] = a.astype(BF16)
            b_ref[rows, :] = b.astype(BF16)
            s_ref[rows, :] = sb
            acc_ref[rows, :] += _dot(sb, wd_ref[...])

        @pl.when(k == nch - 1)
        def _():
            y = x_ref[...] + 0.5 * acc_ref[...]
            if with_loss:
                e = y - t_ref[...]
                o_ref[...] = e * (1.0 / d)
                oh_ref[...] = (e * (0.5 / d)).astype(BF16)
                l_ref[...] = jnp.broadcast_to(_total(e * e) * (0.5 / d), l_ref.shape)
            else:
                o_ref[...] = y

    row = pl.BlockSpec((tm, d), lambda i, k: (i, 0))
    chunk = pl.BlockSpec((None, fc, d), lambda i, k: (k, 0, 0))
    act = pl.BlockSpec((None, tm, fc), lambda i, k: (k, i, 0))
    in_specs = [row, pl.BlockSpec((1, d), lambda i, k: (0, 0)), chunk, chunk, chunk]
    out_shape = [jax.ShapeDtypeStruct((t, d), F32), jax.ShapeDtypeStruct((t, d), BF16)]
    out_shape += [jax.ShapeDtypeStruct((nch, t, fc), BF16)] * 3
    out_specs = [row, row, act, act, act]
    args = [x, gain, wg, wu, wd]
    if with_loss:
        in_specs.append(row)
        args.append(target)
        out_shape += [jax.ShapeDtypeStruct((nt, 8, LANES), F32), jax.ShapeDtypeStruct((t, d), BF16)]
        out_specs += [pl.BlockSpec((None, 8, LANES), lambda i, k: (i, 0, 0)), row]
    return _pallas(body, name="ffn_fwd_loss" if with_loss else "ffn_fwd", args=args, in_specs=in_specs,
                   out_shape=out_shape, out_specs=out_specs, grid=(nt, nch),
                   scratch_shapes=[pltpu.VMEM((tm, d), F32)], plan=plan)


def _row_halves(n):
    return [slice(0, n // 2), slice(n // 2, n)]


def _swiglu_grads(dyh, a_ref, b_ref, wd_ref, rows):
    ds = _dot_nt(dyh, wd_ref[...])
    av = a_ref[rows, :].astype(F32)
    bv = b_ref[rows, :].astype(F32)
    th = jnp.tanh(0.5 * av)
    sig = 0.5 * th + 0.5
    dab = ((ds * bv) * (sig * (1.0 + av * (0.5 - 0.5 * th)))).astype(BF16)
    return dab, (ds * (av * sig)).astype(BF16)


def _ffn_bwd_a(dyh, a, b, wd, name, plan=None):
    t, d = dyh.shape
    nch, fc, _ = wd.shape
    tm = min(FFN_TILE, t)

    def body(dyh_ref, a_ref, b_ref, wd_ref, da_ref, db_ref):
        for rows in _row_halves(tm):
            da_ref[rows, :], db_ref[rows, :] = _swiglu_grads(dyh_ref[rows, :], a_ref, b_ref, wd_ref, rows)

    act = pl.BlockSpec((None, tm, fc), lambda i, k: (k, i, 0))
    return _pallas(
        body, name=name, args=[dyh, a, b, wd], out_shape=[jax.ShapeDtypeStruct((nch, t, fc), BF16)] * 2, grid=(t // tm, nch),
        in_specs=[pl.BlockSpec((tm, d), lambda i, k: (i, 0)), act, act, pl.BlockSpec((None, fc, d), lambda i, k: (k, 0, 0))],
        out_specs=[act, act], plan=plan)


def _ffn_bwd_h(dy, x, gain, da, db, wg, wu, name, tiles, prev=None, plan=None):
    t, d = x.shape
    nch, fc, _ = wg.shape
    tm = min(FFN_TILE, t)
    nt = t // tm
    t0, t1 = tiles

    def body(*refs):
        dy_ref, x_ref, g_ref, da_ref, db_ref, wg_ref, wu_ref = refs[:7]
        dx_ref, dg_ref, acc_ref = refs[-3:]
        k = pl.program_id(1)

        @pl.when(k == 0)
        def _():
            acc_ref[...] = jnp.zeros_like(acc_ref)

        acc_ref[...] += _dot(da_ref[...], wg_ref[...]) + _dot(db_ref[...], wu_ref[...])

        @pl.when(k == nch - 1)
        def _():
            xf = x_ref[...]
            dxn, dgr = _rms_bwd(xf, _rms(xf), g_ref[...], acc_ref[...])
            dx_ref[...] = dy_ref[...] + dxn
            dg_ref[...] = jnp.sum(dgr, axis=0, keepdims=True)

    row = pl.BlockSpec((tm, d), lambda i, k: (i + t0, 0))
    chunk = pl.BlockSpec((None, fc, d), lambda i, k: (k, 0, 0))
    act = pl.BlockSpec((None, tm, fc), lambda i, k: (k, i + t0, 0))
    args = [dy, x, gain, da, db, wg, wu]
    in_specs = [row, row, pl.BlockSpec((1, d), lambda i, k: (0, 0)), act, act, chunk, chunk]
    aliases = {}
    if prev is not None:
        aliases = {len(args): 0, len(args) + 1: 1}
        args += list(prev)
        in_specs += [ANY, ANY]
    return _pallas(
        body, name=name, args=args, out_shape=[jax.ShapeDtypeStruct((t, d), F32), jax.ShapeDtypeStruct((nt, 1, d), F32)],
        grid=(t1 - t0, nch), in_specs=in_specs,
        out_specs=[row, pl.BlockSpec((None, 1, d), lambda i, k: (i + t0, 0, 0))],
        scratch_shapes=[pltpu.VMEM((tm, d), F32)], plan=plan, aliases=aliases)


def _ffn_bwd_x(dy, x, gain, a, b, wg, wu, wd, name, plan=None):
    t, d = x.shape
    nch, fc, _ = wg.shape
    tm = min(FFN_TILE, t)
    nt = t // tm

    def body(dy_ref, x_ref, g_ref, a_ref, b_ref, wg_ref, wu_ref, wd_ref, dx_ref, da_ref, db_ref, dg_ref, acc_ref):
        k = pl.program_id(1)

        @pl.when(k == 0)
        def _():
            acc_ref[...] = jnp.zeros_like(acc_ref)

        for rows in _row_halves(tm):
            dab, dbb = _swiglu_grads((0.5 * dy_ref[rows, :]).astype(BF16), a_ref, b_ref, wd_ref, rows)
            da_ref[rows, :] = dab
            db_ref[rows, :] = dbb
            acc_ref[rows, :] += _dot(dab, wg_ref[...]) + _dot(dbb, wu_ref[...])

        @pl.when(k == nch - 1)
        def _():
            xf = x_ref[...]
            dxn, dgr = _rms_bwd(xf, _rms(xf), g_ref[...], acc_ref[...])
            dx_ref[...] = dy_ref[...] + dxn
            dg_ref[...] = jnp.sum(dgr, axis=0, keepdims=True)

    row = pl.BlockSpec((tm, d), lambda i, k: (i, 0))
    chunk = pl.BlockSpec((None, fc, d), lambda i, k: (k, 0, 0))
    act = pl.BlockSpec((None, tm, fc), lambda i, k: (k, i, 0))
    return _pallas(
        body, name=name, args=[dy, x, gain, a, b, wg, wu, wd],
        out_shape=[jax.ShapeDtypeStruct((t, d), F32), jax.ShapeDtypeStruct((nch, t, fc), BF16),
                   jax.ShapeDtypeStruct((nch, t, fc), BF16), jax.ShapeDtypeStruct((nt, 1, d), F32)],
        grid=(nt, nch),
        in_specs=[row, row, pl.BlockSpec((1, d), lambda i, k: (0, 0)), act, act, chunk, chunk, chunk],
        out_specs=[row, act, act, pl.BlockSpec((None, 1, d), lambda i, k: (i, 0, 0))],
        scratch_shapes=[pltpu.VMEM((tm, d), F32)], plan=plan)


def _ffn_bwd_w(pairs, name, plan=None):
    n = len(pairs)
    nch, t, fc = pairs[0][0].shape
    d = pairs[0][1].shape[1]
    tm = min(FFN_TILE, t)

    def body(*refs):
        @pl.when(pl.program_id(1) == 0)
        def _():
            for o_ref in refs[2 * n:]:
                o_ref[...] = jnp.zeros_like(o_ref)

        for j in range(n):
            refs[2 * n + j][...] += _dot_tn(refs[j][...], refs[n + j][...])

    row = pl.BlockSpec((tm, d), lambda k, i: (i, 0))
    act = pl.BlockSpec((None, tm, fc), lambda k, i: (k, i, 0))
    chunk = pl.BlockSpec((None, fc, d), lambda k, i: (k, 0, 0))
    return _pallas(body, name=name, args=[p[0] for p in pairs] + [p[1] for p in pairs],
                   out_shape=[jax.ShapeDtypeStruct((nch, fc, d), F32)] * n, grid=(nch, t // tm),
                   in_specs=[act] * n + [row] * n, out_specs=[chunk] * n, plan=plan)


def _head_masks():
    lane = lax.broadcasted_iota(jnp.int32, (1, LANES), 1)
    return lane < HEAD_DIM


def _head_rms(x, lo):
    x2 = x * x
    s0 = jnp.sum(jnp.where(lo, x2, 0.0), axis=1, keepdims=True)
    s1 = jnp.sum(jnp.where(lo, 0.0, x2), axis=1, keepdims=True)
    return jnp.where(lo, lax.rsqrt(s0 * (1.0 / HEAD_DIM) + EPS), lax.rsqrt(s1 * (1.0 / HEAD_DIM) + EPS))


def _head_mean(v, lo):
    s0 = jnp.sum(jnp.where(lo, v, 0.0), axis=1, keepdims=True)
    s1 = jnp.sum(jnp.where(lo, 0.0, v), axis=1, keepdims=True)
    return jnp.where(lo, s0, s1) * (1.0 / HEAD_DIM)


def _mix_proj(x1, gain, wt, qn, kn, pool_width, attn_width):
    t, d = x1.shape
    tm = min(512, t)
    nt = t // tm
    scale = HEAD_DIM ** -0.5
    c_q, c_k, c_v = pool_width, pool_width + attn_width, pool_width + 2 * attn_width
    c_f = c_v + attn_width

    def body(x_ref, g_ref, wt_ref, qn_ref, kn_ref, hm_ref, pv_ref, q_ref, k_ref, qh_ref, kh_ref, vb_ref, f_ref):
        lo = _head_masks()
        for rows in _row_halves(tm):
            xf = x_ref[rows, :]
            hm = ((xf * _rms(xf)) * g_ref[...]).astype(BF16)
            hm_ref[rows, :] = hm
            f_ref[rows, :] = _dot_nt(hm, wt_ref[c_f:c_f + LANES, :])
            pv_ref[rows, :] = _dot_nt(hm, wt_ref[0:pool_width, :])
            vb_ref[rows, :] = _dot_nt(hm, wt_ref[c_v:c_v + attn_width, :]).astype(BF16)
            for c0, raw_ref, hat_ref, n_ref, mul in ((c_q, q_ref, qh_ref, qn_ref, scale), (c_k, k_ref, kh_ref, kn_ref, 1.0)):
                raw = _dot_nt(hm, wt_ref[c0:c0 + attn_width, :])
                raw_ref[rows, :] = raw
                for blk in range(attn_width // LANES):
                    sl = slice(blk * LANES, (blk + 1) * LANES)
                    xb = raw[:, sl]
                    hat_ref[rows, sl] = (((xb * _head_rms(xb, lo)) * n_ref[:, sl]) * mul).astype(BF16)

    row = pl.BlockSpec((tm, d), lambda i: (i, 0))
    half = pl.BlockSpec((tm, attn_width), lambda i: (i, 0))
    const = lambda shape: pl.BlockSpec(shape, lambda i: (0, 0))
    return _pallas(
        body, name="mix_proj", args=[x1, gain, wt, qn, kn],
        out_shape=[jax.ShapeDtypeStruct((t, d), BF16), jax.ShapeDtypeStruct((t, pool_width), F32),
                   jax.ShapeDtypeStruct((t, attn_width), F32), jax.ShapeDtypeStruct((t, attn_width), F32),
                   jax.ShapeDtypeStruct((t, attn_width), BF16), jax.ShapeDtypeStruct((t, attn_width), BF16),
                   jax.ShapeDtypeStruct((t, attn_width), BF16), jax.ShapeDtypeStruct((t, LANES), F32)],
        grid=(nt,),
        in_specs=[row, const((1, d)), const(wt.shape), const((1, attn_width)), const((1, attn_width))],
        out_specs=[row, pl.BlockSpec((tm, pool_width), lambda i: (i, 0)), half, half, half, half, half,
                   pl.BlockSpec((tm, LANES), lambda i: (i, 0))])[0]


def _shift_down(v, dist, row):
    return jnp.where(row >= dist, pltpu.roll(v, dist, 0), 0.0)


def _shift_up(v, dist, row, n):
    return jnp.where(row + dist < n, pltpu.roll(v, n - dist, 0), 0.0)


def _aug_lane(e):
    return HEAD_DIM if e == 0 else 0


def _forget_prefix(f, bias, qh, kh, n_batch, seq):
    def body(f_ref, b_ref, q_ref, k_ref, qa_ref, ka_ref):
        z = f_ref[...] + b_ref[...]
        acc = jnp.minimum(z, 0.0) - jnp.log(1.0 + jnp.exp(-jnp.abs(z)))
        row = lax.broadcasted_iota(jnp.int32, (seq, 1), 0)
        dist = 1
        while dist < seq:
            acc = acc + _shift_down(acc, dist, row)
            dist *= 2
        lane = lax.broadcasted_iota(jnp.int32, (1, LANES), 1)
        for h in range(N_HEADS):
            pair, e = divmod(h, 2)
            a0 = _aug_lane(e)
            own = (lane < HEAD_DIM) if e == 0 else (lane >= HEAD_DIM)
            fh = _pick_lane(acc, h)
            hi = fh.astype(BF16).astype(F32)
            rest = fh - hi
            mid = rest.astype(BF16).astype(F32)
            low = rest - mid
            q_ones = (lane >= a0 + 3) & (lane < a0 + 6)
            k_ones = (lane >= a0) & (lane < a0 + 3)
            q_aug = jnp.where(lane == a0, hi, jnp.where(lane == a0 + 1, mid, jnp.where(lane == a0 + 2, low,
                              jnp.where(q_ones, 1.0, 0.0))))
            k_aug = jnp.where(k_ones, 1.0, jnp.where(lane == a0 + 3, -hi, jnp.where(lane == a0 + 4, -mid,
                              jnp.where(lane == a0 + 5, -low, 0.0))))
            src = slice(pair * LANES, (pair + 1) * LANES)
            dst = slice(h * LANES, (h + 1) * LANES)
            qa_ref[:, dst] = jnp.where(own, q_ref[:, src].astype(F32), q_aug).astype(BF16)
            ka_ref[:, dst] = jnp.where(own, k_ref[:, src].astype(F32), k_aug).astype(BF16)

    width = qh.shape[1]
    tok = pl.BlockSpec((seq, width), lambda b: (b, 0))
    aug = pl.BlockSpec((seq, N_HEADS * LANES), lambda b: (b, 0))
    return pl.pallas_call(
        body, out_shape=[jax.ShapeDtypeStruct((n_batch * seq, N_HEADS * LANES), BF16)] * 2, grid=(n_batch,),
        in_specs=[pl.BlockSpec((seq, LANES), lambda b: (b, 0)), pl.BlockSpec((1, LANES), lambda b: (0, 0)), tok, tok],
        out_specs=[aug, aug], compiler_params=_params(), name="forget_prefix",
    )(f, bias, qh, kh)


def _pool_groups(pv_ref, pw_ref, ps_ref, seq):
    row = lax.broadcasted_iota(jnp.int32, (seq, 1), 0)
    pos = (row + 1).astype(F32)
    out = []
    for g, win in enumerate(POOL_WINDOWS):
        sl = slice(g * LANES, (g + 1) * LANES)
        xg = pv_ref[:, sl]
        acc = xg
        dist = 1
        while dist < win:
            acc = acc + _shift_down(acc, dist, row)
            dist *= 2
        pooled = (acc / jnp.minimum(pos, float(win)) - xg).astype(BF16)
        mixed = _dot(pooled, pw_ref[g])
        out.append((pooled, mixed, mixed * ps_ref[:, sl]))
    return out


def _pool_fwd(pv, pw, ps, onp, n_batch, seq):
    width = pv.shape[1]

    def body(pv_ref, pw_ref, ps_ref, on_ref, y_ref):
        groups = _pool_groups(pv_ref, pw_ref, ps_ref, seq)
        ssq = sum(jnp.sum(ms * ms, axis=1, keepdims=True) for _, _, ms in groups)
        r = lax.rsqrt(ssq * (1.0 / width) + EPS)
        for g, (_, _, ms) in enumerate(groups):
            sl = slice(g * LANES, (g + 1) * LANES)
            y_ref[:, sl] = ((ms * r) * on_ref[:, sl]).astype(BF16)

    return pl.pallas_call(
        body, out_shape=jax.ShapeDtypeStruct((n_batch * seq, width), BF16), grid=(n_batch,),
        in_specs=[pl.BlockSpec((seq, width), lambda b: (b, 0)), pl.BlockSpec(pw.shape, lambda b: (0, 0, 0)),
                  pl.BlockSpec((1, width), lambda b: (0, 0)), pl.BlockSpec((1, width), lambda b: (0, 0))],
        out_specs=pl.BlockSpec((seq, width), lambda b: (b, 0)),
        compiler_params=_params(), name="pool_fwd",
    )(pv, pw, ps, onp)


def _pool_bwd(pv, dyp, pw, ps, onp, n_batch, seq):
    width = pv.shape[1]

    def body(pv_ref, dy_ref, pw_ref, ps_ref, on_ref, dpv_ref, dpw_ref, dps_ref, don_ref):
        groups = _pool_groups(pv_ref, pw_ref, ps_ref, seq)
        ssq = sum(jnp.sum(ms * ms, axis=1, keepdims=True) for _, _, ms in groups)
        r = lax.rsqrt(ssq * (1.0 / width) + EPS)
        mean = sum(jnp.sum((dy_ref[:, g * LANES:(g + 1) * LANES] * on_ref[:, g * LANES:(g + 1) * LANES]) * (ms * r),
                           axis=1, keepdims=True) for g, (_, _, ms) in enumerate(groups)) * (1.0 / width)
        row = lax.broadcasted_iota(jnp.int32, (seq, 1), 0)
        pos = (row + 1).astype(F32)
        for g, (pooled, mixed, ms) in enumerate(groups):
            sl = slice(g * LANES, (g + 1) * LANES)
            dy = dy_ref[:, sl]
            xh = ms * r
            don_ref[:, sl] = jnp.sum(dy * xh, axis=0, keepdims=True)
            dms = r * (dy * on_ref[:, sl] - xh * mean)
            dps_ref[:, sl] = jnp.sum(dms * mixed, axis=0, keepdims=True)
            dmix = (dms * ps_ref[:, sl]).astype(BF16)
            dpw_ref[g] = _dot_tn(pooled, dmix)
            dpool = _dot_nt(dmix, pw_ref[g])
            win = POOL_WINDOWS[g]
            acc = dpool / jnp.minimum(pos, float(win))
            dist = 1
            while dist < win:
                acc = acc + _shift_up(acc, dist, row, seq)
                dist *= 2
            dpv_ref[:, sl] = (acc - dpool).astype(BF16)

    tok = pl.BlockSpec((seq, width), lambda b: (b, 0))
    vec = pl.BlockSpec((1, width), lambda b: (0, 0))
    pvec = pl.BlockSpec((None, 1, width), lambda b: (b, 0, 0))
    return pl.pallas_call(
        body,
        out_shape=[jax.ShapeDtypeStruct((n_batch * seq, width), BF16),
                   jax.ShapeDtypeStruct((n_batch,) + pw.shape, F32),
                   jax.ShapeDtypeStruct((n_batch, 1, width), F32), jax.ShapeDtypeStruct((n_batch, 1, width), F32)],
        grid=(n_batch,),
        in_specs=[tok, tok, pl.BlockSpec(pw.shape, lambda b: (0, 0, 0)), vec, vec],
        out_specs=[tok, pl.BlockSpec((None,) + pw.shape, lambda b: (b, 0, 0, 0)), pvec, pvec],
        compiler_params=_params(), name="pool_bwd",
    )(pv, dyp, pw, ps, onp)


def _pick_lane(tile, idx):
    lane = lax.broadcasted_iota(jnp.int32, (1, LANES), 1)
    return jnp.sum(jnp.where(lane == idx, tile, 0.0), axis=1, keepdims=True)


def _pick_row(tile, idx):
    sub = lax.broadcasted_iota(jnp.int32, (tile.shape[0], 1), 0)
    return jnp.sum(jnp.where(sub == idx, tile, 0.0), axis=0, keepdims=True)


def _put_lane(col, idx):
    lane = lax.broadcasted_iota(jnp.int32, (1, LANES), 1)
    return jnp.where(lane == idx, col, 0.0)


def _head_select(e):
    lo = _head_masks()
    return lo if e == 0 else jnp.logical_not(lo)


def _causal(st, shift):
    row = lax.broadcasted_iota(jnp.int32, st.shape, 0)
    col = lax.broadcasted_iota(jnp.int32, st.shape, 1) + shift
    return jnp.where(col >= row, st, NEG)


def _transpose_blocks(a):
    rows, cols = a.shape
    return jnp.concatenate(
        [jnp.concatenate([a[r:r + LANES, c:c + LANES].T for r in range(0, rows, LANES)], axis=1)
         for c in range(0, cols, LANES)], axis=0)


def _stat_rows(ref, head, nsub):
    return jnp.concatenate([_pick_row(ref[a], head) for a in range(nsub)], axis=1)


def _accumulate(ref, value, first):
    @pl.when(first)
    def _():
        ref[...] = value

    @pl.when(jnp.logical_not(first))
    def _():
        ref[...] += value


def _attn_fwd(qa, ka, vb, n_batch, seq, plan=None):
    tq = min(ATT_BLOCK, seq)
    nq, nsub, tk = seq // tq, tq // ATT_SUB, tq
    pairs = vb.shape[1] // LANES

    def body(q_ref, k_ref, v_ref, o_ref, lse_ref, acc_ref):
        i, p = pl.program_id(1), pl.program_id(2)
        row_lo = lax.broadcasted_iota(jnp.int32, (LANES, 1), 0) < HEAD_DIM
        qs = [q_ref[:, e * LANES:(e + 1) * LANES] for e in range(2)]
        acc_ref[...] = jnp.zeros_like(acc_ref)

        def tile(off, stats, diagonal):
            vj = v_ref[pl.ds(off, tk), :]
            new, alphas, pvs = [], [], []
            for e in range(2):
                st = _dot_nt(k_ref[pl.ds(off, tk), e * LANES:(e + 1) * LANES], qs[e])
                if diagonal:
                    st = _causal(st, 0)
                m, l = stats[e]
                m_new = jnp.maximum(m, jnp.max(st, axis=0, keepdims=True))
                alpha = jnp.exp(m - m_new)
                pt = jnp.exp(st - m_new)
                new.append((m_new, alpha * l + jnp.sum(pt, axis=0, keepdims=True)))
                alphas.append(alpha)
                pvs.append(_dot_tn(jnp.where(_head_select(e), vj, jnp.zeros_like(vj)), pt.astype(BF16)))
            acc_ref[...] = acc_ref[...] * jnp.where(row_lo, alphas[0], alphas[1]) + (pvs[0] + pvs[1])
            return tuple(new)

        init = ((jnp.full((1, tq), NEG, F32), jnp.zeros((1, tq), F32)),) * 2
        stats = lax.fori_loop(0, i, lambda j, st: tile(pl.multiple_of(j * tk, tk), st, False), init)
        (m0, l0), (m1, l1) = tile(pl.multiple_of(i * tk, tk), stats, True)
        out_t = acc_ref[...] / jnp.where(row_lo, l0, l1)
        sub = lax.broadcasted_iota(jnp.int32, (8, 1), 0)
        lse0, lse1 = m0 + jnp.log(l0), m1 + jnp.log(l1)
        for a in range(nsub):
            sl = slice(a * ATT_SUB, (a + 1) * ATT_SUB)
            o_ref[sl, :] = out_t[:, sl].T
            rows = jnp.where(sub == 2 * p, lse0[:, sl], 0.0) + jnp.where(sub == 2 * p + 1, lse1[:, sl], 0.0)
            _accumulate(lse_ref.at[a], rows, p == 0)

    return _pallas(
        body, name="attn_fwd", args=[qa, ka, vb],
        out_shape=[jax.ShapeDtypeStruct((n_batch * seq, pairs * LANES), F32),
                   jax.ShapeDtypeStruct((n_batch * seq // ATT_SUB, 8, ATT_SUB), F32)],
        grid=(n_batch, nq, pairs),
        in_specs=[pl.BlockSpec((tq, 2 * LANES), lambda b, i, p: (b * nq + i, p)),
                  pl.BlockSpec((seq, 2 * LANES), lambda b, i, p: (b, p)),
                  pl.BlockSpec((seq, LANES), lambda b, i, p: (b, p))],
        out_specs=[pl.BlockSpec((tq, LANES), lambda b, i, p: (b * nq + i, p)),
                   pl.BlockSpec((nsub, 8, ATT_SUB), lambda b, i, p: (b * nq + i, 0, 0))],
        scratch_shapes=[pltpu.VMEM((LANES, tq), F32)], plan=plan)


def _attn_bwd_q(qa, ka, vb, do, lse, delta, n_batch, seq, plan=None):
    tq = min(ATT_BLOCK, seq)
    nq, nsub, tk = seq // tq, tq // ATT_SUB, tq
    pairs = vb.shape[1] // LANES

    def body(q_ref, k_ref, v_ref, do_ref, lse_ref, dl_ref, dq_ref, dfq_ref, acc0_ref, acc1_ref):
        i, p = pl.program_id(1), pl.program_id(2)
        accs = (acc0_ref, acc1_ref)
        qs = [q_ref[:, e * LANES:(e + 1) * LANES] for e in range(2)]
        dov = do_ref[...]
        ls = [_stat_rows(lse_ref, 2 * p + e, nsub) for e in range(2)]
        dl = [_stat_rows(dl_ref, 2 * p + e, nsub) for e in range(2)]
        for acc in accs:
            acc[...] = jnp.zeros_like(acc)

        def tile(off, diagonal):
            vj = v_ref[pl.ds(off, tk), :]
            for e in range(2):
                kj = k_ref[pl.ds(off, tk), e * LANES:(e + 1) * LANES]
                st = _dot_nt(kj, qs[e])
                if diagonal:
                    st = _causal(st, 0)
                pt = jnp.exp(st - ls[e])
                dpt = _dot_nt(jnp.where(_head_select(e), vj, jnp.zeros_like(vj)), dov)
                accs[e][...] += _dot(_transpose_blocks(kj), (pt * (dpt - dl[e])).astype(BF16))

        def step(j, carry):
            tile(pl.multiple_of(j * tk, tk), False)
            return carry

        lax.fori_loop(0, i, step, 0)
        tile(pl.multiple_of(i * tk, tk), True)
        dq0, dq1 = _transpose_blocks(acc0_ref[...]), _transpose_blocks(acc1_ref[...])
        dq_ref[...] = jnp.where(_head_masks(), dq0, dq1)
        dfq = _put_lane(_pick_lane(dq0, _aug_lane(0)), 2 * p) + _put_lane(_pick_lane(dq1, _aug_lane(1)), 2 * p + 1)
        _accumulate(dfq_ref, dfq, p == 0)

    stat = pl.BlockSpec((nsub, 8, ATT_SUB), lambda b, i, p: (b * nq + i, 0, 0))
    blk = pl.BlockSpec((tq, LANES), lambda b, i, p: (b * nq + i, p))
    return _pallas(
        body, name="attn_bwd_q", args=[qa, ka, vb, do, lse, delta],
        out_shape=[jax.ShapeDtypeStruct((n_batch * seq, pairs * LANES), F32), jax.ShapeDtypeStruct((n_batch * seq, LANES), F32)],
        grid=(n_batch, nq, pairs),
        in_specs=[pl.BlockSpec((tq, 2 * LANES), lambda b, i, p: (b * nq + i, p)),
                  pl.BlockSpec((seq, 2 * LANES), lambda b, i, p: (b, p)),
                  pl.BlockSpec((seq, LANES), lambda b, i, p: (b, p)), blk, stat, stat],
        out_specs=[blk, pl.BlockSpec((tq, LANES), lambda b, i, p: (b * nq + i, 0))],
        scratch_shapes=[pltpu.VMEM((LANES, tq), F32), pltpu.VMEM((LANES, tq), F32)], plan=plan)


def _attn_bwd_kv(qa, ka, vb, do, lse, delta, n_batch, seq, plan=None):
    tkb = min(ATT_BLOCK, seq)
    nk, nsub, tq = seq // tkb, tkb // ATT_SUB, tkb
    n_tiles = seq // ATT_SUB
    pairs = vb.shape[1] // LANES

    def body(q_ref, k_ref, v_ref, do_ref, lse_ref, dl_ref, dk_ref, dv_ref, dfk_ref, dk0_ref, dk1_ref, dva_ref):
        j, p = pl.program_id(1), pl.program_id(2)
        dks = (dk0_ref, dk1_ref)
        ks = [k_ref[:, e * LANES:(e + 1) * LANES] for e in range(2)]
        vj = v_ref[...]
        vs = [jnp.where(_head_select(e), vj, jnp.zeros_like(vj)) for e in range(2)]
        for acc in (dk0_ref, dk1_ref, dva_ref):
            acc[...] = jnp.zeros_like(acc)

        def tile(t, diagonal):
            off = pl.multiple_of(t * tq, tq)
            dov = do_ref[pl.ds(off, tq), :]
            for e in range(2):
                qe = q_ref[pl.ds(off, tq), e * LANES:(e + 1) * LANES]
                st = _dot_nt(ks[e], qe)
                if diagonal:
                    st = _causal(st, 0)
                rows = lambda ref: jnp.concatenate([_pick_row(ref[t * nsub + a], 2 * p + e) for a in range(nsub)], axis=1)
                pt = jnp.exp(st - rows(lse_ref))
                dva_ref[...] += _dot(pt.astype(BF16), jnp.where(_head_select(e), dov, jnp.zeros_like(dov)))
                dst = pt * (_dot_nt(vs[e], dov) - rows(dl_ref))
                dks[e][...] += _dot(dst.astype(BF16), qe)

        def step(t, carry):
            tile(t, False)
            return carry

        lax.fori_loop(j + 1, nk, step, 0)
        tile(j, True)
        dk0, dk1 = dk0_ref[...], dk1_ref[...]
        dk_ref[...] = jnp.where(_head_masks(), dk0, dk1)
        dv_ref[...] = dva_ref[...].astype(BF16)
        dfk = (_put_lane(_pick_lane(dk0, _aug_lane(0) + 3), 2 * p)
               + _put_lane(_pick_lane(dk1, _aug_lane(1) + 3), 2 * p + 1))
        _accumulate(dfk_ref, -dfk, p == 0)

    stat = pl.BlockSpec((n_tiles, 8, ATT_SUB), lambda b, j, p: (b, 0, 0))
    blk = pl.BlockSpec((tkb, LANES), lambda b, j, p: (b * nk + j, p))
    acc = pltpu.VMEM((tkb, LANES), F32)
    return _pallas(
        body, name="attn_bwd_kv", args=[qa, ka, vb, do, lse, delta],
        out_shape=[jax.ShapeDtypeStruct((n_batch * seq, pairs * LANES), F32),
                   jax.ShapeDtypeStruct((n_batch * seq, pairs * LANES), BF16),
                   jax.ShapeDtypeStruct((n_batch * seq, LANES), F32)],
        grid=(n_batch, nk, pairs),
        in_specs=[pl.BlockSpec((seq, 2 * LANES), lambda b, j, p: (b, p)),
                  pl.BlockSpec((tkb, 2 * LANES), lambda b, j, p: (b * nk + j, p)), blk,
                  pl.BlockSpec((seq, LANES), lambda b, j, p: (b, p)), stat, stat],
        out_specs=[blk, blk, pl.BlockSpec((tkb, LANES), lambda b, j, p: (b * nk + j, 0))],
        scratch_shapes=[acc, acc, acc], plan=plan)


def _forget_bwd(dfq, dfk, f, bias, n_batch, seq):
    def body(dfq_ref, dfk_ref, f_ref, b_ref, df_ref, db_ref):
        acc = dfq_ref[...] + dfk_ref[...]
        row = lax.broadcasted_iota(jnp.int32, (seq, 1), 0)
        dist = 1
        while dist < seq:
            acc = acc + _shift_up(acc, dist, row, seq)
            dist *= 2
        df = acc * _sigmoid(-(f_ref[...] + b_ref[...]))
        df_ref[...] = df
        db_ref[...] = jnp.sum(df, axis=0, keepdims=True)

    col = pl.BlockSpec((seq, LANES), lambda b: (b, 0))
    return pl.pallas_call(
        body,
        out_shape=[jax.ShapeDtypeStruct((n_batch * seq, LANES), F32), jax.ShapeDtypeStruct((n_batch, 1, LANES), F32)],
        grid=(n_batch,), in_specs=[col, col, col, pl.BlockSpec((1, LANES), lambda b: (0, 0))],
        out_specs=[col, pl.BlockSpec((None, 1, LANES), lambda b: (b, 0, 0))],
        compiler_params=_params(), name="forget_bwd",
    )(dfq, dfk, f, bias)


def _mix_out(x1, yp, o, ona, woa, wob):
    t, d = x1.shape
    width = o.shape[1]
    tm = min(512, t)

    def body(x_ref, yp_ref, o_ref, on_ref, wa_ref, wb_ref, x2_ref, ya_ref):
        of = o_ref[...]
        ya = ((of * _rms(of)) * on_ref[...]).astype(BF16)
        ya_ref[...] = ya
        x2_ref[...] = x_ref[...] + (_dot(yp_ref[...], wa_ref[...]) + _dot(ya, wb_ref[...]))

    row = pl.BlockSpec((tm, d), lambda i: (i, 0))
    half = pl.BlockSpec((tm, width), lambda i: (i, 0))
    wspec = pl.BlockSpec((width, d), lambda i: (0, 0))
    return pl.pallas_call(
        body, out_shape=[jax.ShapeDtypeStruct((t, d), F32), jax.ShapeDtypeStruct((t, width), BF16)],
        grid=(t // tm,), in_specs=[row, half, half, pl.BlockSpec((1, width), lambda i: (0, 0)), wspec, wspec],
        out_specs=[row, half], compiler_params=_params(), name="mix_out",
    )(x1, yp, o, ona, woa, wob)


def _mix_out_bwd(dx2, o, yp, ya, ona, woa, wob, plan=None):
    t, d = dx2.shape
    width = o.shape[1]
    tm = min(512, t)
    nt = t // tm

    def body(dx_ref, o_ref, yp_ref, ya_ref, on_ref, wa_ref, wb_ref, dyp_ref, do_ref, dl_ref, dwa_ref, dwb_ref, don_ref):
        @pl.when(pl.program_id(0) == 0)
        def _():
            dwa_ref[...] = jnp.zeros_like(dwa_ref)
            dwb_ref[...] = jnp.zeros_like(dwb_ref)

        dxb = dx_ref[...].astype(BF16)
        dwa_ref[...] += _dot_tn(yp_ref[...], dxb)
        dwb_ref[...] += _dot_tn(ya_ref[...], dxb)
        dyp_ref[...] = _dot_nt(dxb, wa_ref[...])
        of = o_ref[...]
        dov, dgr = _rms_bwd(of, _rms(of), on_ref[...], _dot_nt(dxb, wb_ref[...]))
        don_ref[...] = jnp.sum(dgr, axis=0, keepdims=True)
        do_ref[...] = dov.astype(BF16)
        lo = _head_masks()
        prod = dov * of
        delta = jnp.zeros((tm, LANES), F32)
        for blk in range(width // LANES):
            pb = prod[:, blk * LANES:(blk + 1) * LANES]
            delta = delta + _put_lane(jnp.sum(jnp.where(lo, pb, 0.0), axis=1, keepdims=True), 2 * blk)
            delta = delta + _put_lane(jnp.sum(jnp.where(lo, 0.0, pb), axis=1, keepdims=True), 2 * blk + 1)
        for c in range(tm // ATT_SUB):
            dl_ref[c] = delta[c * ATT_SUB:(c + 1) * ATT_SUB, :].T[0:8, :]

    row = pl.BlockSpec((tm, d), lambda i: (i, 0))
    half = pl.BlockSpec((tm, width), lambda i: (i, 0))
    wspec = pl.BlockSpec((width, d), lambda i: (0, 0))
    return _pallas(
        body, name="mix_out_bwd", args=[dx2, o, yp, ya, ona, woa, wob],
        out_shape=[jax.ShapeDtypeStruct((t, width), F32), jax.ShapeDtypeStruct((t, width), BF16),
                   jax.ShapeDtypeStruct((t // ATT_SUB, 8, ATT_SUB), F32), jax.ShapeDtypeStruct((width, d), F32),
                   jax.ShapeDtypeStruct((width, d), F32), jax.ShapeDtypeStruct((nt, 1, width), F32)],
        grid=(nt,),
        in_specs=[row, half, half, half, pl.BlockSpec((1, width), lambda i: (0, 0)), wspec, wspec],
        out_specs=[half, half, pl.BlockSpec((tm // ATT_SUB, 8, ATT_SUB), lambda i: (i, 0, 0)), wspec, wspec,
                   pl.BlockSpec((None, 1, width), lambda i: (i, 0, 0))], plan=plan)


def _mix_in_bwd(dx2, x1, gain, hm, dpv, dqh, q, dkh, k, dv, df, qn, kn, wt):
    t, d = x1.shape
    width = q.shape[1]
    pool_width = dpv.shape[1]
    tm = min(512, t)
    nt = t // tm
    scale = HEAD_DIM ** -0.5
    c_q, c_k, c_v = pool_width, pool_width + width, pool_width + 2 * width
    c_f = c_v + width

    def body(dx2_ref, x_ref, g_ref, hm_ref, dpv_ref, dqh_ref, q_ref, dkh_ref, k_ref, dv_ref, df_ref, qn_ref, kn_ref,
             wt_ref, dx_ref, dxh_ref, dwt_ref, dg_ref, dqn_ref, dkn_ref):
        @pl.when(pl.program_id(0) == 0)
        def _():
            dwt_ref[...] = jnp.zeros_like(dwt_ref)

        lo = _head_masks()
        for part, rows in enumerate(_row_halves(tm)):
            def put(ref, sl, value):
                ref[:, sl] = value if part == 0 else ref[:, sl] + value

            hm = hm_ref[rows, :]
            pieces = [(0, dpv_ref[rows, :])]
            for c0, raw_ref, dh_ref, n_ref, dn_ref, mul in ((c_q, q_ref, dqh_ref, qn_ref, dqn_ref, scale),
                                                           (c_k, k_ref, dkh_ref, kn_ref, dkn_ref, 1.0)):
                cols = []
                for blk in range(width // LANES):
                    sl = slice(blk * LANES, (blk + 1) * LANES)
                    xb = raw_ref[rows, sl]
                    gb = dh_ref[rows, sl] * mul
                    r = _head_rms(xb, lo)
                    xh = xb * r
                    dyg = gb * n_ref[:, sl]
                    cols.append((r * (dyg - xh * _head_mean(dyg * xh, lo))).astype(BF16))
                    put(dn_ref, sl, jnp.sum(gb * xh, axis=0, keepdims=True))
                pieces.append((c0, jnp.concatenate(cols, axis=1)))
            pieces.append((c_v, dv_ref[rows, :]))
            pieces.append((c_f, df_ref[rows, :].astype(BF16)))
            dhm = jnp.zeros((tm // 2, d), F32)
            for c0, piece in pieces:
                dwt_ref[c0:c0 + piece.shape[1], :] += _dot_tn(piece, hm)
                dhm = dhm + _dot(piece, wt_ref[c0:c0 + piece.shape[1], :])
            xf = x_ref[rows, :]
            dxn, dgr = _rms_bwd(xf, _rms(xf), g_ref[...], dhm)
            dx = dx2_ref[rows, :] + dxn
            dx_ref[rows, :] = dx
            dxh_ref[rows, :] = (0.5 * dx).astype(BF16)
            put(dg_ref, slice(None), jnp.sum(dgr, axis=0, keepdims=True))

    row = pl.BlockSpec((tm, d), lambda i: (i, 0))
    half = pl.BlockSpec((tm, width), lambda i: (i, 0))
    const = lambda shape: pl.BlockSpec(shape, lambda i: (0, 0))
    pvec = lambda n: pl.BlockSpec((None, 1, n), lambda i: (i, 0, 0))
    return pl.pallas_call(
        body,
        out_shape=[jax.ShapeDtypeStruct((t, d), F32), jax.ShapeDtypeStruct((t, d), BF16), jax.ShapeDtypeStruct(wt.shape, F32),
                   jax.ShapeDtypeStruct((nt, 1, d), F32),
                   jax.ShapeDtypeStruct((nt, 1, width), F32), jax.ShapeDtypeStruct((nt, 1, width), F32)],
        grid=(nt,),
        in_specs=[row, row, const((1, d)), row, pl.BlockSpec((tm, pool_width), lambda i: (i, 0)), half, half, half, half,
                  half, pl.BlockSpec((tm, LANES), lambda i: (i, 0)), const((1, width)), const((1, width)),
                  const(wt.shape)],
        out_specs=[row, row, const(wt.shape), pvec(d), pvec(width), pvec(width)],
        compiler_params=_params(), name="mix_in_bwd",
    )(dx2, x1, gain, hm, dpv, dqh, q, dkh, k, dv, df, qn, kn, wt)


def _mesh_pos():
    return lax.axis_index("x"), lax.axis_index("y"), lax.axis_index("c")


def _other_chips(x, y):
    return [(1 - x, y), (x, 1 - y), (1 - x, 1 - y)]


def _remote(src, dst, send_sem, recv_sem, device):
    return pltpu.make_async_remote_copy(src_ref=src, dst_ref=dst, send_sem=send_sem, recv_sem=recv_sem,
                                        device_id=device, device_id_type=pl.DeviceIdType.MESH)


def _half_rows(n_rows, which):
    half = n_rows // 2
    return pl.ds(pl.multiple_of(which * half, 8), half)


def _row_block(rows, cols, itemsize=4):
    rb = rows
    while rb * cols * itemsize > (1 << 20) and rb % 32 == 0:
        rb //= 2
    return rb


def _place_cast(ws, chip, tag):
    n = len(ws)
    rows, cols = ws[0].shape
    rb = _row_block(rows, cols)

    def body(k_ref, *refs):
        for w_ref, o_ref in zip(refs[:n], refs[n:]):
            o_ref[...] = w_ref[...].astype(BF16)

    return pl.pallas_call(
        body, out_shape=[jax.ShapeDtypeStruct((N_CHIPS, rows, cols), BF16)] * n,
        grid_spec=pltpu.PrefetchScalarGridSpec(
            num_scalar_prefetch=1, grid=(rows // rb,),
            in_specs=[pl.BlockSpec((rb, cols), lambda i, k: (i, 0))] * n,
            out_specs=[pl.BlockSpec((None, rb, cols), lambda i, k: (k[0], i, 0))] * n),
        compiler_params=_params(), name="place_" + tag,
    )(chip, *ws)


class _Plan:
    def __init__(self, ins, outs, alias, sems, start, finish):
        self.ins, self.outs, self.alias, self.sems, self.start, self.finish = ins, outs, alias, sems, start, finish


def _merge_plans(a, b):
    ni, no, ns = len(a.ins), len(a.outs), len(a.sems)
    alias = dict(a.alias)
    alias.update({ni + i: no + o for i, o in b.alias.items()})

    def both(which):
        def run(ins, outs, sems):
            getattr(a, which)(ins[:ni], outs[:no], sems[:ns])
            getattr(b, which)(ins[ni:], outs[no:], sems[ns:])
        return run

    return _Plan(list(a.ins) + list(b.ins), list(a.outs) + list(b.outs), alias, list(a.sems) + list(b.sems),
                 both("start"), both("finish"))


def _run_plan(plan, name):
    n_in, n_out = len(plan.ins), len(plan.outs)

    def body(*refs):
        parts = refs[:n_in], refs[n_in:n_in + n_out], refs[n_in + n_out:]
        plan.start(*parts)
        plan.finish(*parts)

    return pl.pallas_call(
        body, out_shape=plan.outs, in_specs=[ANY] * n_in, out_specs=[ANY] * n_out, scratch_shapes=plan.sems,
        input_output_aliases=plan.alias, name=name,
    )(*plan.ins)


def _pallas(body, *, name, args, in_specs, out_shape, out_specs, grid, scratch_shapes=(), plan=None, aliases=None):
    n_in, n_out, n_scr = len(args), len(out_shape), len(scratch_shapes)
    plan = plan or _Plan([], [], {}, [], None, None)
    p_in, p_out = len(plan.ins), len(plan.outs)

    def carrying(*refs):
        ins, p_ins = refs[:n_in], refs[n_in:n_in + p_in]
        o0 = n_in + p_in
        outs, p_outs = refs[o0:o0 + n_out], refs[o0 + n_out:o0 + n_out + p_out]
        s0 = o0 + n_out + p_out
        scr, p_sems = refs[s0:s0 + n_scr], refs[s0 + n_scr:]
        ids = [pl.program_id(a) for a in range(len(grid))]

        if plan.start is not None:
            @pl.when(functools.reduce(jnp.logical_and, [i == 0 for i in ids]))
            def _():
                plan.start(p_ins, p_outs, p_sems)

        body(*ins, *outs, *scr)

        if plan.finish is not None:
            @pl.when(functools.reduce(jnp.logical_and, [i == g - 1 for i, g in zip(ids, grid)]))
            def _():
                plan.finish(p_ins, p_outs, p_sems)

    aliases = dict(aliases or {})
    aliases.update({n_in + i: n_out + o for i, o in plan.alias.items()})
    res = pl.pallas_call(
        carrying, out_shape=list(out_shape) + list(plan.outs), grid=grid,
        in_specs=list(in_specs) + [ANY] * p_in, out_specs=list(out_specs) + [ANY] * p_out,
        scratch_shapes=list(scratch_shapes) + list(plan.sems),
        input_output_aliases=aliases, compiler_params=_params(), name=name,
    )(*args, *plan.ins)
    return list(res[:n_out]), list(res[n_out:])


def _plan_gather(stacks):
    n = len(stacks)
    relations = range(3)

    def ici_copies(outs, sems):
        x, y, c = _mesh_pos()
        chips = _other_chips(x, y)
        cps = []
        for w in range(n):
            own = outs[w].at[2 * x + y, _half_rows(stacks[w].shape[1], c)]
            cps += [_remote(own, own, sems[0].at[w, j], sems[1].at[w, j], (*chips[j], c)) for j in relations]
        return cps

    def start(ins, outs, sems):
        for cp in ici_copies(outs, sems):
            cp.start()

    def finish(ins, outs, sems):
        ici_send, ici_recv, d2d_send, d2d_recv = sems
        x, y, c = _mesh_pos()
        sibling = (x, y, 1 - c)
        slots = [2 * cx + cy for cx, cy in _other_chips(x, y)]
        forwards = []
        for w in range(n):
            rows = _half_rows(stacks[w].shape[1], c)
            for j in relations:
                landed = outs[w].at[slots[j], rows]
                _remote(landed, landed, ici_send.at[w, j], ici_recv.at[w, j], sibling).wait_recv()
                cp = _remote(landed, landed, d2d_send.at[w, j], d2d_recv.at[w, j], sibling)
                cp.start()
                forwards.append(cp)
        for w in range(n):
            rows = _half_rows(stacks[w].shape[1], 1 - c)
            for j in relations:
                landed = outs[w].at[slots[j], rows]
                _remote(landed, landed, d2d_send.at[w, j], d2d_recv.at[w, j], sibling).wait_recv()
        for cp in ici_copies(outs, sems) + forwards:
            cp.wait_send()

    return _Plan(stacks, [jax.ShapeDtypeStruct(s.shape, s.dtype) for s in stacks], {w: w for w in range(n)},
                 [pltpu.SemaphoreType.DMA((n, 3))] * 4, start, finish)


def _plan_gather_relay(stacks):
    n = len(stacks)

    def finish(ins, outs, sems):
        send, recv, relay_send, relay_recv, d2d_send, d2d_recv = sems
        x, y, c = _mesh_pos()
        sibling = (x, y, 1 - c)
        near = [(1 - x, y), (x, 1 - y)]
        far = 2 * (1 - x) + (1 - y)
        started = []

        def go(cp):
            cp.start()
            started.append(cp)

        def piece(w, slot, core, quarter=None):
            rh = stacks[w].shape[1] // 2
            if quarter is None:
                return outs[w].at[slot, _half_rows(2 * rh, core)]
            return outs[w].at[slot, pl.ds(pl.multiple_of(core * rh + quarter * (rh // 2), 8), rh // 2)]

        for w in range(n):
            own = piece(w, 2 * x + y, c)
            for j, chip in enumerate(near):
                go(_remote(own, own, send.at[w, j], recv.at[w, j], (*chip, c)))
        for w in range(n):
            for j, (cx, cy) in enumerate(near):
                landed = piece(w, 2 * cx + cy, c)
                _remote(landed, landed, send.at[w, j], recv.at[w, j], sibling).wait_recv()
                part = piece(w, 2 * cx + cy, c, quarter=j)
                go(_remote(part, part, relay_send.at[w, j], relay_recv.at[w, j], (*near[1 - j], c)))
                go(_remote(landed, landed, d2d_send.at[w, j], d2d_recv.at[w, j], sibling))
        for w in range(n):
            for j in range(2):
                part = piece(w, far, c, quarter=j)
                _remote(part, part, relay_send.at[w, j], relay_recv.at[w, j], sibling).wait_recv()
            landed = piece(w, far, c)
            go(_remote(landed, landed, d2d_send.at[w, 2], d2d_recv.at[w, 2], sibling))
        for w in range(n):
            for j, slot in enumerate([2 * cx + cy for cx, cy in near] + [far]):
                landed = piece(w, slot, 1 - c)
                _remote(landed, landed, d2d_send.at[w, j], d2d_recv.at[w, j], sibling).wait_recv()
        for cp in started:
            cp.wait_send()

    return _Plan(stacks, [jax.ShapeDtypeStruct(s.shape, s.dtype) for s in stacks], {w: w for w in range(n)},
                 [pltpu.SemaphoreType.DMA((n, 2))] * 4 + [pltpu.SemaphoreType.DMA((n, 3))] * 2,
                 lambda ins, outs, sems: None, finish)


def _plan_sibling_halves(gs):
    n = len(gs)

    def copies(ins, outs, sems):
        x, y, c = _mesh_pos()
        return [_remote(ins[w].at[:, _half_rows(gs[w].shape[1], 1 - c), :], outs[w], sems[0].at[w], sems[1].at[w],
                        (x, y, 1 - c)) for w in range(n)]

    def start(ins, outs, sems):
        for cp in copies(ins, outs, sems):
            cp.start()

    def finish(ins, outs, sems):
        for cp in copies(ins, outs, sems):
            cp.wait()

    return _Plan(gs, [jax.ShapeDtypeStruct((g.shape[0], g.shape[1] // 2, g.shape[2]), g.dtype) for g in gs], {},
                 [pltpu.SemaphoreType.DMA((n,))] * 2, start, finish)


def _plan_chip_exchange(ps):
    n = len(ps)

    def copies(ins, outs, sems):
        x, y, c = _mesh_pos()
        return [_remote(ins[w].at[2 * cx + cy], outs[w].at[j], sems[0].at[w, j], sems[1].at[w, j], (cx, cy, c))
                for w in range(n) for j, (cx, cy) in enumerate(_other_chips(x, y))]

    def start(ins, outs, sems):
        for cp in copies(ins, outs, sems):
            cp.start()

    def finish(ins, outs, sems):
        for cp in copies(ins, outs, sems):
            cp.wait()

    return _Plan(ps, [jax.ShapeDtypeStruct((3,) + p.shape[1:], p.dtype) for p in ps], {},
                 [pltpu.SemaphoreType.DMA((n, 3))] * 2, start, finish)


def _plan_sibling_share(gs):
    n = len(gs)

    def copies(outs, sems, which):
        x, y, c = _mesh_pos()
        cps = []
        for w in range(n):
            rows = outs[w].at[_half_rows(gs[w].shape[0], c if which == "mine" else 1 - c)]
            cps.append(_remote(rows, rows, sems[0].at[w], sems[1].at[w], (x, y, 1 - c)))
        return cps

    def start(ins, outs, sems):
        for cp in copies(outs, sems, "mine"):
            cp.start()

    def finish(ins, outs, sems):
        for cp in copies(outs, sems, "mine"):
            cp.wait_send()
        for cp in copies(outs, sems, "theirs"):
            cp.wait_recv()

    return _Plan(gs, [jax.ShapeDtypeStruct(g.shape, g.dtype) for g in gs], {w: w for w in range(n)},
                 [pltpu.SemaphoreType.DMA((n,))] * 2, start, finish)


def _same_shape_groups(arrays):
    groups = {}
    for i, a in enumerate(arrays):
        groups.setdefault(a.shape, []).append(i)
    return list(groups.values())


def _add_sibling(gs, r1s, ids, tag):
    n = len(gs)
    nch, rh, cols = r1s[0].shape

    def body(ids_ref, *refs):
        for g_ref, r_ref, o_ref in zip(refs[:n], refs[n:2 * n], refs[2 * n:]):
            o_ref[...] = (g_ref[...] + r_ref[...]).astype(BF16)

    blk = lambda fn: pl.BlockSpec((None, rh, cols), fn)
    return pl.pallas_call(
        body, out_shape=[jax.ShapeDtypeStruct(r1s[0].shape, BF16)] * n,
        grid_spec=pltpu.PrefetchScalarGridSpec(
            num_scalar_prefetch=1, grid=(nch,),
            in_specs=[blk(lambda k, ids: (k, ids[1], 0))] * n + [blk(lambda k, ids: (k, 0, 0))] * n,
            out_specs=[blk(lambda k, ids: (k, 0, 0))] * n),
        compiler_params=_params(), name="add_sibling_" + tag,
    )(ids, *gs, *r1s)


def _add_chips(gs, r1s, r2s, ids, tag):
    n = len(gs)
    _, rh, cols = r1s[0].shape
    nb = 2 if rh % 32 == 0 else 1
    rb = rh // nb

    def body(ids_ref, *refs):
        for g_ref, r1_ref, r2_ref, o_ref in zip(refs[:n], refs[n:2 * n], refs[2 * n:3 * n], refs[3 * n:]):
            own = g_ref[...] + r1_ref[...]
            o_ref[...] = ((own + r2_ref[0].astype(F32)) + r2_ref[1].astype(F32)) + r2_ref[2].astype(F32)

    return pl.pallas_call(
        body, out_shape=[jax.ShapeDtypeStruct((2 * rh, cols), F32)] * n,
        grid_spec=pltpu.PrefetchScalarGridSpec(
            num_scalar_prefetch=1, grid=(nb,),
            in_specs=[pl.BlockSpec((None, rb, cols), lambda i, ids: (ids[0], ids[1] * nb + i, 0))] * n
            + [pl.BlockSpec((None, rb, cols), lambda i, ids: (ids[0], i, 0))] * n
            + [pl.BlockSpec((3, rb, cols), lambda i, ids: (0, i, 0))] * n,
            out_specs=[pl.BlockSpec((rb, cols), lambda i, ids: (ids[1] * nb + i, 0))] * n),
        compiler_params=_params(), name="add_chips_" + tag,
    )(ids, *gs, *r1s, *r2s)


VEC_ROWS = 8


N_DEVICES = 8


def _small_pack(part, d, width):
    names = ("ffn1_norm", "mix_norm", "ffn2_norm", "pool_scale", "out_norm_pool", "out_norm_attn", "qn", "kn", "b_forget",
             "pool_w", "loss")
    args = [part[k] for k in names]
    pw_shape = part["pool_w"].shape[1:]

    def body(g1_ref, gm_ref, g2_ref, ps_ref, onp_ref, ona_ref, qn_ref, kn_ref, bf_ref, pw_ref, loss_ref, vbuf, pbuf):
        lo = _head_masks()

        def fold_heads(ref):
            v = jnp.sum(ref[...], axis=0)
            acc = jnp.zeros((VEC_ROWS, LANES), F32)
            for blk in range(width // LANES):
                vb = jnp.broadcast_to(v[:, blk * LANES:(blk + 1) * LANES], (VEC_ROWS, LANES))
                acc = acc + vb + pltpu.roll(vb, HEAD_DIM, 1)
            return jnp.where(lo, acc, 0.0)[0:1, :]

        vbuf[0] = jnp.zeros((VEC_ROWS, d), F32)
        vbuf[0, 0:1, :] = jnp.sum(g1_ref[...], axis=0)
        vbuf[0, 1:2, :] = jnp.sum(gm_ref[...], axis=0)
        vbuf[0, 2:3, :] = jnp.sum(g2_ref[...], axis=0)
        vbuf[0, 5:6, 0:LANES] = jnp.sum(loss_ref[...], axis=0)[0:1, :]
        vbuf[0, 3:4, 0:width] = jnp.sum(ps_ref[...], axis=0)
        vbuf[0, 3:4, width:2 * width] = jnp.sum(onp_ref[...], axis=0)
        vbuf[0, 4:5, 0:width] = jnp.sum(ona_ref[...], axis=0)
        vbuf[0, 4:5, width:width + LANES] = fold_heads(qn_ref)
        vbuf[0, 4:5, width + LANES:width + 2 * LANES] = fold_heads(kn_ref)
        vbuf[0, 4:5, width + 2 * LANES:width + 3 * LANES] = jnp.sum(bf_ref[...], axis=0)
        pbuf[0] = jnp.sum(pw_ref[...], axis=0)

    return pl.pallas_call(
        body, out_shape=[jax.ShapeDtypeStruct((N_DEVICES, VEC_ROWS, d), F32), jax.ShapeDtypeStruct((N_DEVICES,) + pw_shape, F32)],
        in_specs=[VM] * len(args), out_specs=[VM, VM], compiler_params=_params(), name="small_pack",
    )(*args)


def _plan_all_to_all(stacks):
    n = len(stacks)

    def copies(outs, sems):
        x, y, c = _mesh_pos()
        cps = []
        for r in range(1, N_DEVICES):
            peer = (x if not r & 4 else 1 - x, y if not r & 2 else 1 - y, c if not r & 1 else 1 - c)
            cps += [_remote(outs[w].at[0], outs[w].at[r], sems[0].at[w, r - 1], sems[1].at[w, r - 1], peer) for w in range(n)]
        return cps

    def start(ins, outs, sems):
        for cp in copies(outs, sems):
            cp.start()

    def finish(ins, outs, sems):
        for cp in copies(outs, sems):
            cp.wait()

    return _Plan(stacks, [jax.ShapeDtypeStruct(s.shape, s.dtype) for s in stacks], {w: w for w in range(n)},
                 [pltpu.SemaphoreType.DMA((n, N_DEVICES - 1))] * 2, start, finish)


def _small_sum(vstack, pstack, me):
    def body(me_ref, vbuf, pbuf, vec_ref, pw_ref):
        vec = vbuf[me_ref[0]]
        pw = pbuf[me_ref[0]]
        for dev in range(1, N_DEVICES):
            vec = vec + vbuf[jnp.bitwise_xor(me_ref[0], dev)]
            pw = pw + pbuf[jnp.bitwise_xor(me_ref[0], dev)]
        vec_ref[...] = vec
        pw_ref[...] = pw

    full = lambda s: pl.BlockSpec(s.shape, lambda i, me: (0,) * len(s.shape))
    outs = [jax.ShapeDtypeStruct(vstack.shape[1:], F32), jax.ShapeDtypeStruct(pstack.shape[1:], F32)]
    return pl.pallas_call(
        body, out_shape=outs,
        grid_spec=pltpu.PrefetchScalarGridSpec(num_scalar_prefetch=1, grid=(1,), in_specs=[full(vstack), full(pstack)],
                                               out_specs=[full(o) for o in outs]),
        compiler_params=_params(), name="small_sum",
    )(me, vstack, pstack)


def _adamw(ws, gs, ms, vs, tag):
    n = len(ws)
    rows, cols = ws[0].shape
    rb = rows
    while rb * cols * 4 * n > (1 << 20) and rb % 16 == 0:
        rb //= 2

    def body(*refs):
        for j in range(n):
            w_ref, g_ref, m_ref, v_ref = (refs[k * n + j] for k in range(4))
            go_ref, d_ref, mo_ref, vo_ref = (refs[(4 + k) * n + j] for k in range(4))
            gv = g_ref[...]
            go_ref[...] = gv
            m2 = ADAM_B1 * m_ref[...] + (1.0 - ADAM_B1) * gv
            v2 = ADAM_B2 * v_ref[...] + (1.0 - ADAM_B2) * (gv * gv)
            m_hat = m2 / (1.0 - ADAM_B1 ** ADAM_STEP)
            v_hat = v2 / (1.0 - ADAM_B2 ** ADAM_STEP)
            d_ref[...] = -ADAM_LR * (m_hat / (jnp.sqrt(v_hat) + ADAM_EPS) + ADAM_WD * w_ref[...])
            mo_ref[...] = m2
            vo_ref[...] = v2

    spec = pl.BlockSpec((rb, cols), lambda i: (i, 0))
    res, _ = _pallas(
        body, name="adamw_" + tag, args=[*ws, *gs, *ms, *vs], out_shape=[jax.ShapeDtypeStruct(ws[0].shape, F32)] * (4 * n),
        grid=(rows // rb,), in_specs=[spec] * (4 * n), out_specs=[spec] * (4 * n))
    return [tuple(res[k * n + j] for k in range(4)) for j in range(n)]


def _pack_vec(p, d, width):
    pad = lambda v: jnp.pad(v, (0, LANES - v.shape[0]))
    row3 = jnp.concatenate([p["pool_scale"], p["out_norm_pool"]])
    row4 = jnp.concatenate([p["out_norm_attn"], pad(p["q_norm"]), pad(p["k_norm"]), pad(p["b_forget"]),
                            jnp.zeros((d - width - 3 * LANES,), F32)])
    rows = [p["ffn1_norm"], p["mix_norm"], p["ffn2_norm"], row3, row4]
    return jnp.pad(jnp.stack(rows), ((0, VEC_ROWS - len(rows)), (0, 0)))


def _unpack_vec(vec, width):
    return dict(ffn1_norm=vec[0], mix_norm=vec[1], ffn2_norm=vec[2], pool_scale=vec[3, :width],
                out_norm_pool=vec[3, width:2 * width], out_norm_attn=vec[4, :width],
                q_norm=vec[4, width:width + HEAD_DIM], k_norm=vec[4, width + LANES:width + LANES + HEAD_DIM],
                b_forget=vec[4, width + 2 * LANES:width + 2 * LANES + N_HEADS])


WEIGHT_NAMES = ("ffn1_norm", "ffn1_w_gate", "ffn1_w_up", "ffn1_w_down", "mix_norm", "w_in", "b_forget", "pool_w",
                "pool_scale", "q_norm", "k_norm", "out_norm_pool", "out_norm_attn", "w_out", "ffn2_norm",
                "ffn2_w_gate", "ffn2_w_up", "ffn2_w_down")
BIG_NAMES = ("ffn1_w_gate", "ffn1_w_up", "ffn1_w_down", "w_in", "w_out", "ffn2_w_gate", "ffn2_w_up", "ffn2_w_down")
TRANSPOSED_NAMES = ("ffn1_w_gate", "ffn1_w_up", "w_in", "ffn2_w_gate", "ffn2_w_up")
FFN1_NAMES = ("ffn1_w_gate", "ffn1_w_up", "ffn1_w_down")
MIX_NAMES = ("w_in", "w_out")
FFN2_NAMES = ("ffn2_w_gate", "ffn2_w_up", "ffn2_w_down")


def kernel(x, ffn1_norm, ffn1_w_gate, ffn1_w_up, ffn1_w_down, mix_norm, w_in, b_forget, pool_w, pool_scale, q_norm, k_norm, out_norm_pool, out_norm_attn, w_out, ffn2_norm, ffn2_w_gate, ffn2_w_up, ffn2_w_down, loss_target, m_ffn1_norm, m_ffn1_w_gate, m_ffn1_w_up, m_ffn1_w_down, m_mix_norm, m_w_in, m_b_forget, m_pool_w, m_pool_scale, m_q_norm, m_k_norm, m_out_norm_pool, m_out_norm_attn, m_w_out, m_ffn2_norm, m_ffn2_w_gate, m_ffn2_w_up, m_ffn2_w_down, v_ffn1_norm, v_ffn1_w_gate, v_ffn1_w_up, v_ffn1_w_down, v_mix_norm, v_w_in, v_b_forget, v_pool_w, v_pool_scale, v_q_norm, v_k_norm, v_out_norm_pool, v_out_norm_attn, v_w_out, v_ffn2_norm, v_ffn2_w_gate, v_ffn2_w_up, v_ffn2_w_down):
    given = dict(locals())
    w = {n: given[n] for n in WEIGHT_NAMES}
    m = {n: given["m_" + n] for n in WEIGHT_NAMES}
    v = {n: given["v_" + n] for n in WEIGHT_NAMES}
    n_batch, seq, d = x.shape
    width = pool_scale.shape[0]
    in_rows = w_in.shape[1]
    in_cols = N_CHIPS * in_rows
    in_pad = -(-in_rows // 32) * 32
    in_cols_pad = in_cols - N_HEADS + LANES

    work = lambda a, n: a.T if n in TRANSPOSED_NAMES else a
    exchanged = lambda a, n: jnp.pad(a, ((0, in_pad - in_rows), (0, 0))) if n == "w_in" else a

    mesh_x, mesh_y, mesh_c = _mesh_pos()
    ids = jnp.stack([2 * mesh_x + mesh_y, mesh_c]).astype(jnp.int32)

    row = lambda a: a.reshape(1, -1)
    g1, gm, g2, ps, onp, ona = (row(a) for a in (ffn1_norm, mix_norm, ffn2_norm, pool_scale, out_norm_pool, out_norm_attn))
    qn, kn = row(jnp.tile(q_norm, N_HEADS)), row(jnp.tile(k_norm, N_HEADS))
    bf = row(jnp.pad(b_forget, (0, LANES - N_HEADS)))
    pwb = pool_w.astype(BF16)
    xf, tgt = x.reshape(n_batch * seq, d), loss_target.reshape(n_batch * seq, d)

    def grouped(call, names, *lists):
        out = [None] * len(names)
        for idx in _same_shape_groups(lists[0]):
            res = call(*[[lst[i] for i in idx] for lst in lists], names[idx[0]])
            for i, r in zip(idx, res):
                out[i] = r
        return out

    placed = dict(zip(BIG_NAMES, grouped(lambda ws, tag: _place_cast(ws, ids, tag), BIG_NAMES,
                                         [exchanged(work(w[n], n), n) for n in BIG_NAMES])))
    wg1, wu1, wd1 = _run_plan(_plan_gather_relay([placed[n] for n in FFN1_NAMES]), "gather_ffn1")
    (x1, h1, a1, b1, s1), (w_in_all, w_out_all, wd2) = _ffn_fwd(
        xf, g1, wg1, wu1, wd1, plan=_plan_gather([placed[n] for n in MIX_NAMES + FFN2_NAMES[2:]]))
    w_in_t = jnp.pad(w_in_all[:, :in_rows].reshape(in_cols, d), ((0, in_cols_pad - in_cols), (0, 0)))
    w_out_full = w_out_all.reshape(N_CHIPS * w_out.shape[0], d)
    woa, wob = w_out_full[:width], w_out_full[width:]

    hm, pv, q, k, qh, kh, vb, f = _mix_proj(x1, gm, w_in_t, qn, kn, width, width)
    qa, ka = _forget_prefix(f, bf, qh, kh, n_batch, seq)
    yp = _pool_fwd(pv, pwb, ps, onp, n_batch, seq)
    (o, lse), (wg2, wu2) = _attn_fwd(qa, ka, vb, n_batch, seq, plan=_plan_gather([placed[n] for n in FFN2_NAMES[:2]]))
    x2, ya = _mix_out(x1, yp, o, ona, woa, wob)
    (dy, h2, a2, b2, s2, lpart, dyh), _ = _ffn_fwd(x2, g2, wg2, wu2, wd2, target=tgt)

    def to_chips(gs, arrived, tags):
        return grouped(lambda g, r, tag: _add_sibling(g, r, ids, tag), tags, gs, arrived)

    def own_rows(gs, from_sibling, from_chips, tags):
        return grouped(lambda g, ra, rb, tag: _add_chips(g, ra, rb, ids, tag), tags, gs, from_sibling, from_chips)

    (dx2, da2, db2, dg2), _ = _ffn_bwd_x(dy, x2, g2, a2, b2, wg2, wu2, wd2, "ffn2_bwd_x")
    dw2, _ = _ffn_bwd_w([(da2, h2), (db2, h2), (s2, dyh)], "ffn2_bwd_w")
    (dyp, do, delta, dwoa, dwob, dona), sib2 = _mix_out_bwd(dx2, o, yp, ya, ona, woa, wob, plan=_plan_sibling_halves(dw2))
    dpv, dpw, dps, donp = _pool_bwd(pv, dyp, pwb, ps, onp, n_batch, seq)
    (dqh, dfq), chips2 = _attn_bwd_q(qa, ka, vb, do, lse, delta, n_batch, seq,
                                     plan=_plan_chip_exchange(to_chips(dw2, sib2, FFN2_NAMES)))
    (dkh, dv, dfk), red2 = _attn_bwd_kv(qa, ka, vb, do, lse, delta, n_batch, seq,
                                        plan=_plan_sibling_share(own_rows(dw2, sib2, chips2, FFN2_NAMES)))
    df, dbf = _forget_bwd(dfq, dfk, f, bf, n_batch, seq)
    dx1, dx1h, dw_in_t, dgm, dqn, dkn = _mix_in_bwd(dx2, x1, gm, hm, dpv, dqh, q, dkh, k, dv, df, qn, kn, w_in_t)
    in_base = [in_rows * k // 8 * 8 for k in range(N_CHIPS)]
    d_w_in = jnp.stack([dw_in_t[b:b + in_pad] for b in in_base])
    d_w_out = jnp.concatenate([dwoa, dwob], axis=0).reshape(N_CHIPS, w_out.shape[0], d)
    dwm = [d_w_in, d_w_out]
    down, gate_up = FFN1_NAMES[2:], FFN1_NAMES[:2]
    dwd1, sibm = _ffn_bwd_w([(s1, dx1h)], "ffn1_bwd_w_down", plan=_plan_sibling_halves(dwm))
    (da1, db1), arrived = _ffn_bwd_a(dx1h, a1, b1, wd1, "ffn1_bwd_a",
                                     plan=_merge_plans(_plan_sibling_halves(dwd1),
                                                       _plan_chip_exchange(to_chips(dwm, sibm, MIX_NAMES))))
    sibd, chipsm = arrived[:1], arrived[1:]
    dwgu1, chipsd = _ffn_bwd_w([(da1, h1), (db1, h1)], "ffn1_bwd_w_gate_up",
                               plan=_plan_chip_exchange(to_chips(dwd1, sibd, down)))
    n_tiles = (n_batch * seq) // min(FFN_TILE, n_batch * seq)
    first = max(n_tiles // 4, 1)
    begun, sibgu = _ffn_bwd_h(dx1, xf, g1, da1, db1, wg1, wu1, "ffn1_bwd_h_first", (0, first),
                              plan=_plan_sibling_halves(dwgu1))
    (gx, dg1), chipsgu = _ffn_bwd_h(dx1, xf, g1, da1, db1, wg1, wu1, "ffn1_bwd_h_rest", (first, n_tiles), prev=begun,
                                    plan=_plan_chip_exchange(to_chips(dwgu1, sibgu, gate_up)))

    part = dict(ffn1_norm=dg1, mix_norm=dgm, ffn2_norm=dg2, b_forget=dbf, pool_scale=dps, out_norm_pool=donp,
                out_norm_attn=dona, qn=dqn, kn=dkn, pool_w=dpw.reshape(n_batch, -1, pool_w.shape[-1]), loss=lpart)
    mine = (own_rows(dwgu1, sibgu, chipsgu, gate_up) + own_rows(dwd1, sibd, chipsd, down)
            + own_rows(dwm, sibm, chipsm, MIX_NAMES))
    last = _run_plan(_merge_plans(_plan_sibling_share(mine), _plan_all_to_all(_small_pack(part, d, width))), "last_exchange")
    vstack, pstack = last[len(mine):]
    g_vec, g_pw = _small_sum(vstack, pstack, jnp.reshape(4 * mesh_x + 2 * mesh_y + mesh_c, (1,)).astype(jnp.int32))
    loss = g_vec[5, 0]
    reduced = dict(zip(FFN1_NAMES + MIX_NAMES + FFN2_NAMES, list(last[:len(mine)]) + list(red2)))
    reduced["w_in"] = lax.dynamic_slice(reduced["w_in"], ((in_rows * ids[0]) % 8, 0), (in_rows, d))

    grads, delta, new_m, new_v = {}, {}, {}, {}
    for names in (FFN2_NAMES, FFN1_NAMES, ("w_in",), ("w_out",)):
        stepped = _adamw([work(w[n], n) for n in names], [reduced[n] for n in names], [work(m[n], n) for n in names],
                         [work(v[n], n) for n in names], names[0])
        for n, step in zip(names, stepped):
            grads[n], delta[n], new_m[n], new_v[n] = (work(a, n) for a in step)
    flat_pw = lambda a: a.reshape(-1, a.shape[-1])
    (_, d_pw, m_pw, v_pw), = _adamw([flat_pw(pool_w)], [g_pw], [flat_pw(m_pool_w)], [flat_pw(v_pool_w)], "pool_w")
    (_, d_vec, m_vec, v_vec), = _adamw([_pack_vec(w, d, width)], [g_vec], [_pack_vec(m, d, width)],
                                       [_pack_vec(v, d, width)], "vectors")
    grads.update(_unpack_vec(g_vec, width), pool_w=g_pw.reshape(pool_w.shape))
    delta.update(_unpack_vec(d_vec, width), pool_w=d_pw.reshape(pool_w.shape))
    new_m.update(_unpack_vec(m_vec, width), pool_w=m_pw.reshape(pool_w.shape))
    new_v.update(_unpack_vec(v_vec, width), pool_w=v_pw.reshape(pool_w.shape))
    return (loss, gx.reshape(x.shape), *[grads[n] for n in WEIGHT_NAMES], *[delta[n] for n in WEIGHT_NAMES],
            *[new_m[n] for n in WEIGHT_NAMES], *[new_v[n] for n in WEIGHT_NAMES])
```

```python
import functools

import jax
import jax.numpy as jnp
from jax import lax
from jax.experimental import pallas as pl
from jax.experimental.pallas import tpu as pltpu

F32 = jnp.float32
BF16 = jnp.bfloat16
EPS = 1e-6
NEG = -1e30
ADAM_LR = 0.001
ADAM_B1 = 0.9
ADAM_B2 = 0.999
ADAM_EPS = 1e-08
ADAM_WD = 0.01
ADAM_STEP = 10
POOL_WINDOWS = (2, 4, 8, 16)
HEAD_DIM = 64
N_HEADS = 8
LANES = 128
N_CHIPS = 4
ATT_BLOCK = 512
ATT_SUB = 128
FFN_TILE = 1024
VMEM_LIMIT = 62 * 1024 * 1024
ANY = pl.BlockSpec(memory_space=pl.ANY)
VM = pl.BlockSpec(memory_space=pltpu.VMEM)


def _params(**kw):
    return pltpu.CompilerParams(vmem_limit_bytes=VMEM_LIMIT, **kw)


def _dot(a, b):
    return jnp.dot(a, b, preferred_element_type=F32)


def _dot_nt(a, b):
    return lax.dot_general(a, b, (((1,), (1,)), ((), ())), preferred_element_type=F32)


def _dot_tn(a, b):
    return lax.dot_general(a, b, (((0,), (0,)), ((), ())), preferred_element_type=F32)


def _sigmoid(z):
    return 1.0 / (1.0 + jnp.exp(-z))


def _rms(xf):
    return lax.rsqrt(jnp.mean(xf * xf, axis=-1, keepdims=True) + EPS)


def _rms_bwd(xf, r, gain, dh):
    xh = xf * r
    dyg = dh * gain
    return r * (dyg - xh * jnp.mean(dyg * xh, axis=-1, keepdims=True)), dh * xh


def _total(v):
    return jnp.sum(jnp.sum(v, axis=1, keepdims=True), axis=0, keepdims=True)


def _ffn_fwd(x, gain, wg, wu, wd, target=None, plan=None):
    t, d = x.shape
    nch, fc, _ = wg.shape
    tm = min(FFN_TILE, t)
    nt = t // tm
    with_loss = target is not None

    def body(*refs):
        if with_loss:
            x_ref, g_ref, wg_ref, wu_ref, wd_ref, t_ref, o_ref, h_ref, a_ref, b_ref, s_ref, l_ref, oh_ref, acc_ref = refs
        else:
            x_ref, g_ref, wg_ref, wu_ref, wd_ref, o_ref, h_ref, a_ref, b_ref, s_ref, acc_ref = refs
        k = pl.program_id(1)

        @pl.when(k == 0)
        def _():
            xf = x_ref[...]
            h_ref[...] = ((xf * _rms(xf)) * g_ref[...]).astype(BF16)
            acc_ref[...] = jnp.zeros_like(acc_ref)

        for rows in _row_halves(tm):
            h = h_ref[rows, :]
            a = _dot_nt(h, wg_ref[...])
            b = _dot_nt(h, wu_ref[...])
            sb = ((a * (0.5 * jnp.tanh(0.5 * a) + 0.5)) * b).astype(BF16)
            a_ref[rows, :] = a.astype(BF16)
            b_ref[rows, :] = b.astype(BF16)
            s_ref[rows, :] = sb
            acc_ref[rows, :] += _dot(sb, wd_ref[...])

        @pl.when(k == nch - 1)
        def _():
            y = x_ref[...] + 0.5 * acc_ref[...]
            if with_loss:
                e = y - t_ref[...]
                o_ref[...] = e * (1.0 / d)
                oh_ref[...] = (e * (0.5 / d)).astype(BF16)
                l_ref[...] = jnp.broadcast_to(_total(e * e) * (0.5 / d), l_ref.shape)
            else:
                o_ref[...] = y

    row = pl.BlockSpec((tm, d), lambda i, k: (i, 0))
    chunk = pl.BlockSpec((None, fc, d), lambda i, k: (k, 0, 0))
    act = pl.BlockSpec((None, tm, fc), lambda i, k: (k, i, 0))
    in_specs = [row, pl.BlockSpec((1, d), lambda i, k: (0, 0)), chunk, chunk, chunk]
    out_shape = [jax.ShapeDtypeStruct((t, d), F32), jax.ShapeDtypeStruct((t, d), BF16)]
    out_shape += [jax.ShapeDtypeStruct((nch, t, fc), BF16)] * 3
    out_specs = [row, row, act, act, act]
    args = [x, gain, wg, wu, wd]
    if with_loss:
        in_specs.append(row)
        args.append(target)
        out_shape += [jax.ShapeDtypeStruct((nt, 8, LANES), F32), jax.ShapeDtypeStruct((t, d), BF16)]
        out_specs += [pl.BlockSpec((None, 8, LANES), lambda i, k: (i, 0, 0)), row]
    return _pallas(body, name="ffn_fwd_loss" if with_loss else "ffn_fwd", args=args, in_specs=in_specs,
                   out_shape=out_shape, out_specs=out_specs, grid=(nt, nch),
                   scratch_shapes=[pltpu.VMEM((tm, d), F32)], plan=plan)


def _row_halves(n):
    return [slice(0, n // 2), slice(n // 2, n)]


def _swiglu_grads(dyh, a_ref, b_ref, wd_ref, rows):
    ds = _dot_nt(dyh, wd_ref[...])
    av = a_ref[rows, :].astype(F32)
    bv = b_ref[rows, :].astype(F32)
    th = jnp.tanh(0.5 * av)
    sig = 0.5 * th + 0.5
    dab = ((ds * bv) * (sig * (1.0 + av * (0.5 - 0.5 * th)))).astype(BF16)
    return dab, (ds * (av * sig)).astype(BF16)


def _ffn_bwd_a(dyh, a, b, wd, name, plan=None):
    t, d = dyh.shape
    nch, fc, _ = wd.shape
    tm = min(FFN_TILE, t)

    def body(dyh_ref, a_ref, b_ref, wd_ref, da_ref, db_ref):
        for rows in _row_halves(tm):
            da_ref[rows, :], db_ref[rows, :] = _swiglu_grads(dyh_ref[rows, :], a_ref, b_ref, wd_ref, rows)

    act = pl.BlockSpec((None, tm, fc), lambda i, k: (k, i, 0))
    return _pallas(
        body, name=name, args=[dyh, a, b, wd], out_shape=[jax.ShapeDtypeStruct((nch, t, fc), BF16)] * 2, grid=(t // tm, nch),
        in_specs=[pl.BlockSpec((tm, d), lambda i, k: (i, 0)), act, act, pl.BlockSpec((None, fc, d), lambda i, k: (k, 0, 0))],
        out_specs=[act, act], plan=plan)


def _ffn_bwd_h(dy, x, gain, da, db, wg, wu, name, tiles, prev=None, plan=None):
    t, d = x.shape
    nch, fc, _ = wg.shape
    tm = min(FFN_TILE, t)
    nt = t // tm
    t0, t1 = tiles

    def body(*refs):
        dy_ref, x_ref, g_ref, da_ref, db_ref, wg_ref, wu_ref = refs[:7]
        dx_ref, dg_ref, acc_ref = refs[-3:]
        k = pl.program_id(1)

        @pl.when(k == 0)
        def _():
            acc_ref[...] = jnp.zeros_like(acc_ref)

        acc_ref[...] += _dot(da_ref[...], wg_ref[...]) + _dot(db_ref[...], wu_ref[...])

        @pl.when(k == nch - 1)
        def _():
            xf = x_ref[...]
            dxn, dgr = _rms_bwd(xf, _rms(xf), g_ref[...], acc_ref[...])
            dx_ref[...] = dy_ref[...] + dxn
            dg_ref[...] = jnp.sum(dgr, axis=0, keepdims=True)

    row = pl.BlockSpec((tm, d), lambda i, k: (i + t0, 0))
    chunk = pl.BlockSpec((None, fc, d), lambda i, k: (k, 0, 0))
    act = pl.BlockSpec((None, tm, fc), lambda i, k: (k, i + t0, 0))
    args = [dy, x, gain, da, db, wg, wu]
    in_specs = [row, row, pl.BlockSpec((1, d), lambda i, k: (0, 0)), act, act, chunk, chunk]
    aliases = {}
    if prev is not None:
        aliases = {len(args): 0, len(args) + 1: 1}
        args += list(prev)
        in_specs += [ANY, ANY]
    return _pallas(
        body, name=name, args=args, out_shape=[jax.ShapeDtypeStruct((t, d), F32), jax.ShapeDtypeStruct((nt, 1, d), F32)],
        grid=(t1 - t0, nch), in_specs=in_specs,
        out_specs=[row, pl.BlockSpec((None, 1, d), lambda i, k: (i + t0, 0, 0))],
        scratch_shapes=[pltpu.VMEM((tm, d), F32)], plan=plan, aliases=aliases)


def _ffn_bwd_x(dy, x, gain, a, b, wg, wu, wd, name, plan=None):
    t, d = x.shape
    nch, fc, _ = wg.shape
    tm = min(FFN_TILE, t)
    nt = t // tm

    def body(dy_ref, x_ref, g_ref, a_ref, b_ref, wg_ref, wu_ref, wd_ref, dx_ref, da_ref, db_ref, dg_ref, acc_ref):
        k = pl.program_id(1)

        @pl.when(k == 0)
        def _():
            acc_ref[...] = jnp.zeros_like(acc_ref)

        for rows in _row_halves(tm):
            dab, dbb = _swiglu_grads((0.5 * dy_ref[rows, :]).astype(BF16), a_ref, b_ref, wd_ref, rows)
            da_ref[rows, :] = dab
            db_ref[rows, :] = dbb
            acc_ref[rows, :] += _dot(dab, wg_ref[...]) + _dot(dbb, wu_ref[...])

        @pl.when(k == nch - 1)
        def _():
            xf = x_ref[...]
            dxn, dgr = _rms_bwd(xf, _rms(xf), g_ref[...], acc_ref[...])
            dx_ref[...] = dy_ref[...] + dxn
            dg_ref[...] = jnp.sum(dgr, axis=0, keepdims=True)

    row = pl.BlockSpec((tm, d), lambda i, k: (i, 0))
    chunk = pl.BlockSpec((None, fc, d), lambda i, k: (k, 0, 0))
    act = pl.BlockSpec((None, tm, fc), lambda i, k: (k, i, 0))
    return _pallas(
        body, name=name, args=[dy, x, gain, a, b, wg, wu, wd],
        out_shape=[jax.ShapeDtypeStruct((t, d), F32), jax.ShapeDtypeStruct((nch, t, fc), BF16),
                   jax.ShapeDtypeStruct((nch, t, fc), BF16), jax.ShapeDtypeStruct((nt, 1, d), F32)],
        grid=(nt, nch),
        in_specs=[row, row, pl.BlockSpec((1, d), lambda i, k: (0, 0)), act, act, chunk, chunk, chunk],
        out_specs=[row, act, act, pl.BlockSpec((None, 1, d), lambda i, k: (i, 0, 0))],
        scratch_shapes=[pltpu.VMEM((tm, d), F32)], plan=plan)


def _ffn_bwd_w(pairs, name, plan=None):
    n = len(pairs)
    nch, t, fc = pairs[0][0].shape
    d = pairs[0][1].shape[1]
    tm = min(FFN_TILE, t)

    def body(*refs):
        @pl.when(pl.program_id(1) == 0)
        def _():
            for o_ref in refs[2 * n:]:
                o_ref[...] = jnp.zeros_like(o_ref)

        for j in range(n):
            refs[2 * n + j][...] += _dot_tn(refs[j][...], refs[n + j][...])

    row = pl.BlockSpec((tm, d), lambda k, i: (i, 0))
    act = pl.BlockSpec((None, tm, fc), lambda k, i: (k, i, 0))
    chunk = pl.BlockSpec((None, fc, d), lambda k, i: (k, 0, 0))
    return _pallas(body, name=name, args=[p[0] for p in pairs] + [p[1] for p in pairs],
                   out_shape=[jax.ShapeDtypeStruct((nch, fc, d), F32)] * n, grid=(nch, t // tm),
                   in_specs=[act] * n + [row] * n, out_specs=[chunk] * n, plan=plan)


def _head_masks():
    lane = lax.broadcasted_iota(jnp.int32, (1, LANES), 1)
    return lane < HEAD_DIM


def _head_rms(x, lo):
    x2 = x * x
    s0 = jnp.sum(jnp.where(lo, x2, 0.0), axis=1, keepdims=True)
    s1 = jnp.sum(jnp.where(lo, 0.0, x2), axis=1, keepdims=True)
    return jnp.where(lo, lax.rsqrt(s0 * (1.0 / HEAD_DIM) + EPS), lax.rsqrt(s1 * (1.0 / HEAD_DIM) + EPS))


def _head_mean(v, lo):
    s0 = jnp.sum(jnp.where(lo, v, 0.0), axis=1, keepdims=True)
    s1 = jnp.sum(jnp.where(lo, 0.0, v), axis=1, keepdims=True)
    return jnp.where(lo, s0, s1) * (1.0 / HEAD_DIM)


def _mix_proj(x1, gain, wt, qn, kn, pool_width, attn_width):
    t, d = x1.shape
    tm = min(512, t)
    nt = t // tm
    scale = HEAD_DIM ** -0.5
    c_q, c_k, c_v = pool_width, pool_width + attn_width, pool_width + 2 * attn_width
    c_f = c_v + attn_width

    def body(x_ref, g_ref, wt_ref, qn_ref, kn_ref, hm_ref, pv_ref, q_ref, k_ref, qh_ref, kh_ref, vb_ref, f_ref):
        lo = _head_masks()
        for rows in _row_halves(tm):
            xf = x_ref[rows, :]
            hm = ((xf * _rms(xf)) * g_ref[...]).astype(BF16)
            hm_ref[rows, :] = hm
            f_ref[rows, :] = _dot_nt(hm, wt_ref[c_f:c_f + LANES, :])
            pv_ref[rows, :] = _dot_nt(hm, wt_ref[0:pool_width, :])
            vb_ref[rows, :] = _dot_nt(hm, wt_ref[c_v:c_v + attn_width, :]).astype(BF16)
            for c0, raw_ref, hat_ref, n_ref, mul in ((c_q, q_ref, qh_ref, qn_ref, scale), (c_k, k_ref, kh_ref, kn_ref, 1.0)):
                raw = _dot_nt(hm, wt_ref[c0:c0 + attn_width, :])
                raw_ref[rows, :] = raw
                for blk in range(attn_width // LANES):
                    sl = slice(blk * LANES, (blk + 1) * LANES)
                    xb = raw[:, sl]
                    hat_ref[rows, sl] = (((xb * _head_rms(xb, lo)) * n_ref[:, sl]) * mul).astype(BF16)

    row = pl.BlockSpec((tm, d), lambda i: (i, 0))
    half = pl.BlockSpec((tm, attn_width), lambda i: (i, 0))
    const = lambda shape: pl.BlockSpec(shape, lambda i: (0, 0))
    return _pallas(
        body, name="mix_proj", args=[x1, gain, wt, qn, kn],
        out_shape=[jax.ShapeDtypeStruct((t, d), BF16), jax.ShapeDtypeStruct((t, pool_width), F32),
                   jax.ShapeDtypeStruct((t, attn_width), F32), jax.ShapeDtypeStruct((t, attn_width), F32),
                   jax.ShapeDtypeStruct((t, attn_width), BF16), jax.ShapeDtypeStruct((t, attn_width), BF16),
                   jax.ShapeDtypeStruct((t, attn_width), BF16), jax.ShapeDtypeStruct((t, LANES), F32)],
        grid=(nt,),
        in_specs=[row, const((1, d)), const(wt.shape), const((1, attn_width)), const((1, attn_width))],
        out_specs=[row, pl.BlockSpec((tm, pool_width), lambda i: (i, 0)), half, half, half, half, half,
                   pl.BlockSpec((tm, LANES), lambda i: (i, 0))])[0]


def _shift_down(v, dist, row):
    return jnp.where(row >= dist, pltpu.roll(v, dist, 0), 0.0)


def _shift_up(v, dist, row, n):
    return jnp.where(row + dist < n, pltpu.roll(v, n - dist, 0), 0.0)


def _aug_lane(e):
    return HEAD_DIM if e == 0 else 0


def _forget_prefix(f, bias, qh, kh, n_batch, seq):
    def body(f_ref, b_ref, q_ref, k_ref, qa_ref, ka_ref):
        z = f_ref[...] + b_ref[...]
        acc = jnp.minimum(z, 0.0) - jnp.log(1.0 + jnp.exp(-jnp.abs(z)))
        row = lax.broadcasted_iota(jnp.int32, (seq, 1), 0)
        dist = 1
        while dist < seq:
            acc = acc + _shift_down(acc, dist, row)
            dist *= 2
        lane = lax.broadcasted_iota(jnp.int32, (1, LANES), 1)
        for h in range(N_HEADS):
            pair, e = divmod(h, 2)
            a0 = _aug_lane(e)
            own = (lane < HEAD_DIM) if e == 0 else (lane >= HEAD_DIM)
            fh = _pick_lane(acc, h)
            hi = fh.astype(BF16).astype(F32)
            rest = fh - hi
            mid = rest.astype(BF16).astype(F32)
            low = rest - mid
            q_ones = (lane >= a0 + 3) & (lane < a0 + 6)
            k_ones = (lane >= a0) & (lane < a0 + 3)
            q_aug = jnp.where(lane == a0, hi, jnp.where(lane == a0 + 1, mid, jnp.where(lane == a0 + 2, low,
                              jnp.where(q_ones, 1.0, 0.0))))
            k_aug = jnp.where(k_ones, 1.0, jnp.where(lane == a0 + 3, -hi, jnp.where(lane == a0 + 4, -mid,
                              jnp.where(lane == a0 + 5, -low, 0.0))))
            src = slice(pair * LANES, (pair + 1) * LANES)
            dst = slice(h * LANES, (h + 1) * LANES)
            qa_ref[:, dst] = jnp.where(own, q_ref[:, src].astype(F32), q_aug).astype(BF16)
            ka_ref[:, dst] = jnp.where(own, k_ref[:, src].astype(F32), k_aug).astype(BF16)

    width = qh.shape[1]
    tok = pl.BlockSpec((seq, width), lambda b: (b, 0))
    aug = pl.BlockSpec((seq, N_HEADS * LANES), lambda b: (b, 0))
    return pl.pallas_call(
        body, out_shape=[jax.ShapeDtypeStruct((n_batch * seq, N_HEADS * LANES), BF16)] * 2, grid=(n_batch,),
        in_specs=[pl.BlockSpec((seq, LANES), lambda b: (b, 0)), pl.BlockSpec((1, LANES), lambda b: (0, 0)), tok, tok],
        out_specs=[aug, aug], compiler_params=_params(), name="forget_prefix",
    )(f, bias, qh, kh)


def _pool_groups(pv_ref, pw_ref, ps_ref, seq):
    row = lax.broadcasted_iota(jnp.int32, (seq, 1), 0)
    pos = (row + 1).astype(F32)
    out = []
    for g, win in enumerate(POOL_WINDOWS):
        sl = slice(g * LANES, (g + 1) * LANES)
        xg = pv_ref[:, sl]
        acc = xg
        dist = 1
        while dist < win:
            acc = acc + _shift_down(acc, dist, row)
            dist *= 2
        pooled = (acc / jnp.minimum(pos, float(win)) - xg).astype(BF16)
        mixed = _dot(pooled, pw_ref[g])
        out.append((pooled, mixed, mixed * ps_ref[:, sl]))
    return out


def _pool_fwd(pv, pw, ps, onp, n_batch, seq):
    width = pv.shape[1]

    def body(pv_ref, pw_ref, ps_ref, on_ref, y_ref):
        groups = _pool_groups(pv_ref, pw_ref, ps_ref, seq)
        ssq = sum(jnp.sum(ms * ms, axis=1, keepdims=True) for _, _, ms in groups)
        r = lax.rsqrt(ssq * (1.0 / width) + EPS)
        for g, (_, _, ms) in enumerate(groups):
            sl = slice(g * LANES, (g + 1) * LANES)
            y_ref[:, sl] = ((ms * r) * on_ref[:, sl]).astype(BF16)

    return pl.pallas_call(
        body, out_shape=jax.ShapeDtypeStruct((n_batch * seq, width), BF16), grid=(n_batch,),
        in_specs=[pl.BlockSpec((seq, width), lambda b: (b, 0)), pl.BlockSpec(pw.shape, lambda b: (0, 0, 0)),
                  pl.BlockSpec((1, width), lambda b: (0, 0)), pl.BlockSpec((1, width), lambda b: (0, 0))],
        out_specs=pl.BlockSpec((seq, width), lambda b: (b, 0)),
        compiler_params=_params(), name="pool_fwd",
    )(pv, pw, ps, onp)


def _pool_bwd(pv, dyp, pw, ps, onp, n_batch, seq):
    width = pv.shape[1]

    def body(pv_ref, dy_ref, pw_ref, ps_ref, on_ref, dpv_ref, dpw_ref, dps_ref, don_ref):
        groups = _pool_groups(pv_ref, pw_ref, ps_ref, seq)
        ssq = sum(jnp.sum(ms * ms, axis=1, keepdims=True) for _, _, ms in groups)
        r = lax.rsqrt(ssq * (1.0 / width) + EPS)
        mean = sum(jnp.sum((dy_ref[:, g * LANES:(g + 1) * LANES] * on_ref[:, g * LANES:(g + 1) * LANES]) * (ms * r),
                           axis=1, keepdims=True) for g, (_, _, ms) in enumerate(groups)) * (1.0 / width)
        row = lax.broadcasted_iota(jnp.int32, (seq, 1), 0)
        pos = (row + 1).astype(F32)
        for g, (pooled, mixed, ms) in enumerate(groups):
            sl = slice(g * LANES, (g + 1) * LANES)
            dy = dy_ref[:, sl]
            xh = ms * r
            don_ref[:, sl] = jnp.sum(dy * xh, axis=0, keepdims=True)
            dms = r * (dy * on_ref[:, sl] - xh * mean)
            dps_ref[:, sl] = jnp.sum(dms * mixed, axis=0, keepdims=True)
            dmix = (dms * ps_ref[:, sl]).astype(BF16)
            dpw_ref[g] = _dot_tn(pooled, dmix)
            dpool = _dot_nt(dmix, pw_ref[g])
            win = POOL_WINDOWS[g]
            acc = dpool / jnp.minimum(pos, float(win))
            dist = 1
            while dist < win:
                acc = acc + _shift_up(acc, dist, row, seq)
                dist *= 2
            dpv_ref[:, sl] = (acc - dpool).astype(BF16)

    tok = pl.BlockSpec((seq, width), lambda b: (b, 0))
    vec = pl.BlockSpec((1, width), lambda b: (0, 0))
    pvec = pl.BlockSpec((None, 1, width), lambda b: (b, 0, 0))
    return pl.pallas_call(
        body,
        out_shape=[jax.ShapeDtypeStruct((n_batch * seq, width), BF16),
                   jax.ShapeDtypeStruct((n_batch,) + pw.shape, F32),
                   jax.ShapeDtypeStruct((n_batch, 1, width), F32), jax.ShapeDtypeStruct((n_batch, 1, width), F32)],
        grid=(n_batch,),
        in_specs=[tok, tok, pl.BlockSpec(pw.shape, lambda b: (0, 0, 0)), vec, vec],
        out_specs=[tok, pl.BlockSpec((None,) + pw.shape, lambda b: (b, 0, 0, 0)), pvec, pvec],
        compiler_params=_params(), name="pool_bwd",
    )(pv, dyp, pw, ps, onp)


def _pick_lane(tile, idx):
    lane = lax.broadcasted_iota(jnp.int32, (1, LANES), 1)
    return jnp.sum(jnp.where(lane == idx, tile, 0.0), axis=1, keepdims=True)


def _pick_row(tile, idx):
    sub = lax.broadcasted_iota(jnp.int32, (tile.shape[0], 1), 0)
    return jnp.sum(jnp.where(sub == idx, tile, 0.0), axis=0, keepdims=True)


def _put_lane(col, idx):
    lane = lax.broadcasted_iota(jnp.int32, (1, LANES), 1)
    return jnp.where(lane == idx, col, 0.0)


def _head_select(e):
    lo = _head_masks()
    return lo if e == 0 else jnp.logical_not(lo)


def _causal(st, shift):
    row = lax.broadcasted_iota(jnp.int32, st.shape, 0)
    col = lax.broadcasted_iota(jnp.int32, st.shape, 1) + shift
    return jnp.where(col >= row, st, NEG)


def _transpose_blocks(a):
    rows, cols = a.shape
    return jnp.concatenate(
        [jnp.concatenate([a[r:r + LANES, c:c + LANES].T for r in range(0, rows, LANES)], axis=1)
         for c in range(0, cols, LANES)], axis=0)


def _stat_rows(ref, head, nsub):
    return jnp.concatenate([_pick_row(ref[a], head) for a in range(nsub)], axis=1)


def _accumulate(ref, value, first):
    @pl.when(first)
    def _():
        ref[...] = value

    @pl.when(jnp.logical_not(first))
    def _():
        ref[...] += value


def _attn_fwd(qa, ka, vb, n_batch, seq, plan=None):
    tq = min(ATT_BLOCK, seq)
    nq, nsub, tk = seq // tq, tq // ATT_SUB, tq
    pairs = vb.shape[1] // LANES

    def body(q_ref, k_ref, v_ref, o_ref, lse_ref, acc_ref):
        i, p = pl.program_id(1), pl.program_id(2)
        row_lo = lax.broadcasted_iota(jnp.int32, (LANES, 1), 0) < HEAD_DIM
        qs = [q_ref[:, e * LANES:(e + 1) * LANES] for e in range(2)]
        acc_ref[...] = jnp.zeros_like(acc_ref)

        def tile(off, stats, diagonal):
            vj = v_ref[pl.ds(off, tk), :]
            new, alphas, pvs = [], [], []
            for e in range(2):
                st = _dot_nt(k_ref[pl.ds(off, tk), e * LANES:(e + 1) * LANES], qs[e])
                if diagonal:
                    st = _causal(st, 0)
                m, l = stats[e]
                m_new = jnp.maximum(m, jnp.max(st, axis=0, keepdims=True))
                alpha = jnp.exp(m - m_new)
                pt = jnp.exp(st - m_new)
                new.append((m_new, alpha * l + jnp.sum(pt, axis=0, keepdims=True)))
                alphas.append(alpha)
                pvs.append(_dot_tn(jnp.where(_head_select(e), vj, jnp.zeros_like(vj)), pt.astype(BF16)))
            acc_ref[...] = acc_ref[...] * jnp.where(row_lo, alphas[0], alphas[1]) + (pvs[0] + pvs[1])
            return tuple(new)

        init = ((jnp.full((1, tq), NEG, F32), jnp.zeros((1, tq), F32)),) * 2
        stats = lax.fori_loop(0, i, lambda j, st: tile(pl.multiple_of(j * tk, tk), st, False), init)
        (m0, l0), (m1, l1) = tile(pl.multiple_of(i * tk, tk), stats, True)
        out_t = acc_ref[...] / jnp.where(row_lo, l0, l1)
        sub = lax.broadcasted_iota(jnp.int32, (8, 1), 0)
        lse0, lse1 = m0 + jnp.log(l0), m1 + jnp.log(l1)
        for a in range(nsub):
            sl = slice(a * ATT_SUB, (a + 1) * ATT_SUB)
            o_ref[sl, :] = out_t[:, sl].T
            rows = jnp.where(sub == 2 * p, lse0[:, sl], 0.0) + jnp.where(sub == 2 * p + 1, lse1[:, sl], 0.0)
            _accumulate(lse_ref.at[a], rows, p == 0)

    return _pallas(
        body, name="attn_fwd", args=[qa, ka, vb],
        out_shape=[jax.ShapeDtypeStruct((n_batch * seq, pairs * LANES), F32),
                   jax.ShapeDtypeStruct((n_batch * seq // ATT_SUB, 8, ATT_SUB), F32)],
        grid=(n_batch, nq, pairs),
        in_specs=[pl.BlockSpec((tq, 2 * LANES), lambda b, i, p: (b * nq + i, p)),
                  pl.BlockSpec((seq, 2 * LANES), lambda b, i, p: (b, p)),
                  pl.BlockSpec((seq, LANES), lambda b, i, p: (b, p))],
        out_specs=[pl.BlockSpec((tq, LANES), lambda b, i, p: (b * nq + i, p)),
                   pl.BlockSpec((nsub, 8, ATT_SUB), lambda b, i, p: (b * nq + i, 0, 0))],
        scratch_shapes=[pltpu.VMEM((LANES, tq), F32)], plan=plan)


def _attn_bwd_q(qa, ka, vb, do, lse, delta, n_batch, seq, plan=None):
    tq = min(ATT_BLOCK, seq)
    nq, nsub, tk = seq // tq, tq // ATT_SUB, tq
    pairs = vb.shape[1] // LANES

    def body(q_ref, k_ref, v_ref, do_ref, lse_ref, dl_ref, dq_ref, dfq_ref, acc0_ref, acc1_ref):
        i, p = pl.program_id(1), pl.program_id(2)
        accs = (acc0_ref, acc1_ref)
        qs = [q_ref[:, e * LANES:(e + 1) * LANES] for e in range(2)]
        dov = do_ref[...]
        ls = [_stat_rows(lse_ref, 2 * p + e, nsub) for e in range(2)]
        dl = [_stat_rows(dl_ref, 2 * p + e, nsub) for e in range(2)]
        for acc in accs:
            acc[...] = jnp.zeros_like(acc)

        def tile(off, diagonal):
            vj = v_ref[pl.ds(off, tk), :]
            for e in range(2):
                kj = k_ref[pl.ds(off, tk), e * LANES:(e + 1) * LANES]
                st = _dot_nt(kj, qs[e])
                if diagonal:
                    st = _causal(st, 0)
                pt = jnp.exp(st - ls[e])
                dpt = _dot_nt(jnp.where(_head_select(e), vj, jnp.zeros_like(vj)), dov)
                accs[e][...] += _dot(_transpose_blocks(kj), (pt * (dpt - dl[e])).astype(BF16))

        def step(j, carry):
            tile(pl.multiple_of(j * tk, tk), False)
            return carry

        lax.fori_loop(0, i, step, 0)
        tile(pl.multiple_of(i * tk, tk), True)
        dq0, dq1 = _transpose_blocks(acc0_ref[...]), _transpose_blocks(acc1_ref[...])
        dq_ref[...] = jnp.where(_head_masks(), dq0, dq1)
        dfq = _put_lane(_pick_lane(dq0, _aug_lane(0)), 2 * p) + _put_lane(_pick_lane(dq1, _aug_lane(1)), 2 * p + 1)
        _accumulate(dfq_ref, dfq, p == 0)

    stat = pl.BlockSpec((nsub, 8, ATT_SUB), lambda b, i, p: (b * nq + i, 0, 0))
    blk = pl.BlockSpec((tq, LANES), lambda b, i, p: (b * nq + i, p))
    return _pallas(
        body, name="attn_bwd_q", args=[qa, ka, vb, do, lse, delta],
        out_shape=[jax.ShapeDtypeStruct((n_batch * seq, pairs * LANES), F32), jax.ShapeDtypeStruct((n_batch * seq, LANES), F32)],
        grid=(n_batch, nq, pairs),
        in_specs=[pl.BlockSpec((tq, 2 * LANES), lambda b, i, p: (b * nq + i, p)),
                  pl.BlockSpec((seq, 2 * LANES), lambda b, i, p: (b, p)),
                  pl.BlockSpec((seq, LANES), lambda b, i, p: (b, p)), blk, stat, stat],
        out_specs=[blk, pl.BlockSpec((tq, LANES), lambda b, i, p: (b * nq + i, 0))],
        scratch_shapes=[pltpu.VMEM((LANES, tq), F32), pltpu.VMEM((LANES, tq), F32)], plan=plan)


def _attn_bwd_kv(qa, ka, vb, do, lse, delta, n_batch, seq, plan=None):
    tkb = min(ATT_BLOCK, seq)
    nk, nsub, tq = seq // tkb, tkb // ATT_SUB, tkb
    n_tiles = seq // ATT_SUB
    pairs = vb.shape[1] // LANES

    def body(q_ref, k_ref, v_ref, do_ref, lse_ref, dl_ref, dk_ref, dv_ref, dfk_ref, dk0_ref, dk1_ref, dva_ref):
        j, p = pl.program_id(1), pl.program_id(2)
        dks = (dk0_ref, dk1_ref)
        ks = [k_ref[:, e * LANES:(e + 1) * LANES] for e in range(2)]
        vj = v_ref[...]
        vs = [jnp.where(_head_select(e), vj, jnp.zeros_like(vj)) for e in range(2)]
        for acc in (dk0_ref, dk1_ref, dva_ref):
            acc[...] = jnp.zeros_like(acc)

        def tile(t, diagonal):
            off = pl.multiple_of(t * tq, tq)
            dov = do_ref[pl.ds(off, tq), :]
            for e in range(2):
                qe = q_ref[pl.ds(off, tq), e * LANES:(e + 1) * LANES]
                st = _dot_nt(ks[e], qe)
                if diagonal:
                    st = _causal(st, 0)
                rows = lambda ref: jnp.concatenate([_pick_row(ref[t * nsub + a], 2 * p + e) for a in range(nsub)], axis=1)
                pt = jnp.exp(st - rows(lse_ref))
                dva_ref[...] += _dot(pt.astype(BF16), jnp.where(_head_select(e), dov, jnp.zeros_like(dov)))
                dst = pt * (_dot_nt(vs[e], dov) - rows(dl_ref))
                dks[e][...] += _dot(dst.astype(BF16), qe)

        def step(t, carry):
            tile(t, False)
            return carry

        lax.fori_loop(j + 1, nk, step, 0)
        tile(j, True)
        dk0, dk1 = dk0_ref[...], dk1_ref[...]
        dk_ref[...] = jnp.where(_head_masks(), dk0, dk1)
        dv_ref[...] = dva_ref[...].astype(BF16)
        dfk = (_put_lane(_pick_lane(dk0, _aug_lane(0) + 3), 2 * p)
               + _put_lane(_pick_lane(dk1, _aug_lane(1) + 3), 2 * p + 1))
        _accumulate(dfk_ref, -dfk, p == 0)

    stat = pl.BlockSpec((n_tiles, 8, ATT_SUB), lambda b, j, p: (b, 0, 0))
    blk = pl.BlockSpec((tkb, LANES), lambda b, j, p: (b * nk + j, p))
    acc = pltpu.VMEM((tkb, LANES), F32)
    return _pallas(
        body, name="attn_bwd_kv", args=[qa, ka, vb, do, lse, delta],
        out_shape=[jax.ShapeDtypeStruct((n_batch * seq, pairs * LANES), F32),
                   jax.ShapeDtypeStruct((n_batch * seq, pairs * LANES), BF16),
                   jax.ShapeDtypeStruct((n_batch * seq, LANES), F32)],
        grid=(n_batch, nk, pairs),
        in_specs=[pl.BlockSpec((seq, 2 * LANES), lambda b, j, p: (b, p)),
                  pl.BlockSpec((tkb, 2 * LANES), lambda b, j, p: (b * nk + j, p)), blk,
                  pl.BlockSpec((seq, LANES), lambda b, j, p: (b, p)), stat, stat],
        out_specs=[blk, blk, pl.BlockSpec((tkb, LANES), lambda b, j, p: (b * nk + j, 0))],
        scratch_shapes=[acc, acc, acc], plan=plan)


def _forget_bwd(dfq, dfk, f, bias, n_batch, seq):
    def body(dfq_ref, dfk_ref, f_ref, b_ref, df_ref, db_ref):
        acc = dfq_ref[...] + dfk_ref[...]
        row = lax.broadcasted_iota(jnp.int32, (seq, 1), 0)
        dist = 1
        while dist < seq:
            acc = acc + _shift_up(acc, dist, row, seq)
            dist *= 2
        df = acc * _sigmoid(-(f_ref[...] + b_ref[...]))
        df_ref[...] = df
        db_ref[...] = jnp.sum(df, axis=0, keepdims=True)

    col = pl.BlockSpec((seq, LANES), lambda b: (b, 0))
    return pl.pallas_call(
        body,
        out_shape=[jax.ShapeDtypeStruct((n_batch * seq, LANES), F32), jax.ShapeDtypeStruct((n_batch, 1, LANES), F32)],
        grid=(n_batch,), in_specs=[col, col, col, pl.BlockSpec((1, LANES), lambda b: (0, 0))],
        out_specs=[col, pl.BlockSpec((None, 1, LANES), lambda b: (b, 0, 0))],
        compiler_params=_params(), name="forget_bwd",
    )(dfq, dfk, f, bias)


def _mix_out(x1, yp, o, ona, woa, wob):
    t, d = x1.shape
    width = o.shape[1]
    tm = min(512, t)

    def body(x_ref, yp_ref, o_ref, on_ref, wa_ref, wb_ref, x2_ref, ya_ref):
        of = o_ref[...]
        ya = ((of * _rms(of)) * on_ref[...]).astype(BF16)
        ya_ref[...] = ya
        x2_ref[...] = x_ref[...] + (_dot(yp_ref[...], wa_ref[...]) + _dot(ya, wb_ref[...]))

    row = pl.BlockSpec((tm, d), lambda i: (i, 0))
    half = pl.BlockSpec((tm, width), lambda i: (i, 0))
    wspec = pl.BlockSpec((width, d), lambda i: (0, 0))
    return pl.pallas_call(
        body, out_shape=[jax.ShapeDtypeStruct((t, d), F32), jax.ShapeDtypeStruct((t, width), BF16)],
        grid=(t // tm,), in_specs=[row, half, half, pl.BlockSpec((1, width), lambda i: (0, 0)), wspec, wspec],
        out_specs=[row, half], compiler_params=_params(), name="mix_out",
    )(x1, yp, o, ona, woa, wob)


def _mix_out_bwd(dx2, o, yp, ya, ona, woa, wob, plan=None):
    t, d = dx2.shape
    width = o.shape[1]
    tm = min(512, t)
    nt = t // tm

    def body(dx_ref, o_ref, yp_ref, ya_ref, on_ref, wa_ref, wb_ref, dyp_ref, do_ref, dl_ref, dwa_ref, dwb_ref, don_ref):
        @pl.when(pl.program_id(0) == 0)
        def _():
            dwa_ref[...] = jnp.zeros_like(dwa_ref)
            dwb_ref[...] = jnp.zeros_like(dwb_ref)

        dxb = dx_ref[...].astype(BF16)
        dwa_ref[...] += _dot_tn(yp_ref[...], dxb)
        dwb_ref[...] += _dot_tn(ya_ref[...], dxb)
        dyp_ref[...] = _dot_nt(dxb, wa_ref[...])
        of = o_ref[...]
        dov, dgr = _rms_bwd(of, _rms(of), on_ref[...], _dot_nt(dxb, wb_ref[...]))
        don_ref[...] = jnp.sum(dgr, axis=0, keepdims=True)
        do_ref[...] = dov.astype(BF16)
        lo = _head_masks()
        prod = dov * of
        delta = jnp.zeros((tm, LANES), F32)
        for blk in range(width // LANES):
            pb = prod[:, blk * LANES:(blk + 1) * LANES]
            delta = delta + _put_lane(jnp.sum(jnp.where(lo, pb, 0.0), axis=1, keepdims=True), 2 * blk)
            delta = delta + _put_lane(jnp.sum(jnp.where(lo, 0.0, pb), axis=1, keepdims=True), 2 * blk + 1)
        for c in range(tm // ATT_SUB):
            dl_ref[c] = delta[c * ATT_SUB:(c + 1) * ATT_SUB, :].T[0:8, :]

    row = pl.BlockSpec((tm, d), lambda i: (i, 0))
    half = pl.BlockSpec((tm, width), lambda i: (i, 0))
    wspec = pl.BlockSpec((width, d), lambda i: (0, 0))
    return _pallas(
        body, name="mix_out_bwd", args=[dx2, o, yp, ya, ona, woa, wob],
        out_shape=[jax.ShapeDtypeStruct((t, width), F32), jax.ShapeDtypeStruct((t, width), BF16),
                   jax.ShapeDtypeStruct((t // ATT_SUB, 8, ATT_SUB), F32), jax.ShapeDtypeStruct((width, d), F32),
                   jax.ShapeDtypeStruct((width, d), F32), jax.ShapeDtypeStruct((nt, 1, width), F32)],
        grid=(nt,),
        in_specs=[row, half, half, half, pl.BlockSpec((1, width), lambda i: (0, 0)), wspec, wspec],
        out_specs=[half, half, pl.BlockSpec((tm // ATT_SUB, 8, ATT_SUB), lambda i: (i, 0, 0)), wspec, wspec,
                   pl.BlockSpec((None, 1, width), lambda i: (i, 0, 0))], plan=plan)


def _mix_in_bwd(dx2, x1, gain, hm, dpv, dqh, q, dkh, k, dv, df, qn, kn, wt):
    t, d = x1.shape
    width = q.shape[1]
    pool_width = dpv.shape[1]
    tm = min(512, t)
    nt = t // tm
    scale = HEAD_DIM ** -0.5
    c_q, c_k, c_v = pool_width, pool_width + width, pool_width + 2 * width
    c_f = c_v + width

    def body(dx2_ref, x_ref, g_ref, hm_ref, dpv_ref, dqh_ref, q_ref, dkh_ref, k_ref, dv_ref, df_ref, qn_ref, kn_ref,
             wt_ref, dx_ref, dxh_ref, dwt_ref, dg_ref, dqn_ref, dkn_ref):
        @pl.when(pl.program_id(0) == 0)
        def _():
            dwt_ref[...] = jnp.zeros_like(dwt_ref)

        lo = _head_masks()
        for part, rows in enumerate(_row_halves(tm)):
            def put(ref, sl, value):
                ref[:, sl] = value if part == 0 else ref[:, sl] + value

            hm = hm_ref[rows, :]
            pieces = [(0, dpv_ref[rows, :])]
            for c0, raw_ref, dh_ref, n_ref, dn_ref, mul in ((c_q, q_ref, dqh_ref, qn_ref, dqn_ref, scale),
                                                           (c_k, k_ref, dkh_ref, kn_ref, dkn_ref, 1.0)):
                cols = []
                for blk in range(width // LANES):
                    sl = slice(blk * LANES, (blk + 1) * LANES)
                    xb = raw_ref[rows, sl]
                    gb = dh_ref[rows, sl] * mul
                    r = _head_rms(xb, lo)
                    xh = xb * r
                    dyg = gb * n_ref[:, sl]
                    cols.append((r * (dyg - xh * _head_mean(dyg * xh, lo))).astype(BF16))
                    put(dn_ref, sl, jnp.sum(gb * xh, axis=0, keepdims=True))
                pieces.append((c0, jnp.concatenate(cols, axis=1)))
            pieces.append((c_v, dv_ref[rows, :]))
            pieces.append((c_f, df_ref[rows, :].astype(BF16)))
            dhm = jnp.zeros((tm // 2, d), F32)
            for c0, piece in pieces:
                dwt_ref[c0:c0 + piece.shape[1], :] += _dot_tn(piece, hm)
                dhm = dhm + _dot(piece, wt_ref[c0:c0 + piece.shape[1], :])
            xf = x_ref[rows, :]
            dxn, dgr = _rms_bwd(xf, _rms(xf), g_ref[...], dhm)
            dx = dx2_ref[rows, :] + dxn
            dx_ref[rows, :] = dx
            dxh_ref[rows, :] = (0.5 * dx).astype(BF16)
            put(dg_ref, slice(None), jnp.sum(dgr, axis=0, keepdims=True))

    row = pl.BlockSpec((tm, d), lambda i: (i, 0))
    half = pl.BlockSpec((tm, width), lambda i: (i, 0))
    const = lambda shape: pl.BlockSpec(shape, lambda i: (0, 0))
    pvec = lambda n: pl.BlockSpec((None, 1, n), lambda i: (i, 0, 0))
    return pl.pallas_call(
        body,
        out_shape=[jax.ShapeDtypeStruct((t, d), F32), jax.ShapeDtypeStruct((t, d), BF16), jax.ShapeDtypeStruct(wt.shape, F32),
                   jax.ShapeDtypeStruct((nt, 1, d), F32),
                   jax.ShapeDtypeStruct((nt, 1, width), F32), jax.ShapeDtypeStruct((nt, 1, width), F32)],
        grid=(nt,),
        in_specs=[row, row, const((1, d)), row, pl.BlockSpec((tm, pool_width), lambda i: (i, 0)), half, half, half, half,
                  half, pl.BlockSpec((tm, LANES), lambda i: (i, 0)), const((1, width)), const((1, width)),
                  const(wt.shape)],
        out_specs=[row, row, const(wt.shape), pvec(d), pvec(width), pvec(width)],
        compiler_params=_params(), name="mix_in_bwd",
    )(dx2, x1, gain, hm, dpv, dqh, q, dkh, k, dv, df, qn, kn, wt)


def _mesh_pos():
    return lax.axis_index("x"), lax.axis_index("y"), lax.axis_index("c")


def _other_chips(x, y):
    return [(1 - x, y), (x, 1 - y), (1 - x, 1 - y)]


def _remote(src, dst, send_sem, recv_sem, device):
    return pltpu.make_async_remote_copy(src_ref=src, dst_ref=dst, send_sem=send_sem, recv_sem=recv_sem,
                                        device_id=device, device_id_type=pl.DeviceIdType.MESH)


def _half_rows(n_rows, which):
    half = n_rows // 2
    return pl.ds(pl.multiple_of(which * half, 8), half)


def _row_block(rows, cols, itemsize=4):
    rb = rows
    while rb * cols * itemsize > (1 << 20) and rb % 32 == 0:
        rb //= 2
    return rb


def _place_cast(ws, chip, tag):
    n = len(ws)
    rows, cols = ws[0].shape
    rb = _row_block(rows, cols)

    def body(k_ref, *refs):
        for w_ref, o_ref in zip(refs[:n], refs[n:]):
            o_ref[...] = w_ref[...].astype(BF16)

    return pl.pallas_call(
        body, out_shape=[jax.ShapeDtypeStruct((N_CHIPS, rows, cols), BF16)] * n,
        grid_spec=pltpu.PrefetchScalarGridSpec(
            num_scalar_prefetch=1, grid=(rows // rb,),
            in_specs=[pl.BlockSpec((rb, cols), lambda i, k: (i, 0))] * n,
            out_specs=[pl.BlockSpec((None, rb, cols), lambda i, k: (k[0], i, 0))] * n),
        compiler_params=_params(), name="place_" + tag,
    )(chip, *ws)


class _Plan:
    def __init__(self, ins, outs, alias, sems, start, finish, middle=None):
        self.ins, self.outs, self.alias, self.sems = ins, outs, alias, sems
        self.start, self.middle, self.finish = start, middle, finish


def _merge_plans(a, b):
    ni, no, ns = len(a.ins), len(a.outs), len(a.sems)
    alias = dict(a.alias)
    alias.update({ni + i: no + o for i, o in b.alias.items()})

    def both(which):
        stage_a, stage_b = getattr(a, which), getattr(b, which)
        if stage_a is None and stage_b is None:
            return None

        def run(ins, outs, sems):
            if stage_a is not None:
                stage_a(ins[:ni], outs[:no], sems[:ns])
            if stage_b is not None:
                stage_b(ins[ni:], outs[no:], sems[ns:])
        return run

    return _Plan(list(a.ins) + list(b.ins), list(a.outs) + list(b.outs), alias, list(a.sems) + list(b.sems),
                 both("start"), both("finish"), both("middle"))


def _run_plan(plan, name):
    n_in, n_out = len(plan.ins), len(plan.outs)

    def body(*refs):
        parts = refs[:n_in], refs[n_in:n_in + n_out], refs[n_in + n_out:]
        plan.start(*parts)
        if plan.middle is not None:
            plan.middle(*parts)
        plan.finish(*parts)

    return pl.pallas_call(
        body, out_shape=plan.outs, in_specs=[ANY] * n_in, out_specs=[ANY] * n_out, scratch_shapes=plan.sems,
        input_output_aliases=plan.alias, name=name,
    )(*plan.ins)


def _pallas(body, *, name, args, in_specs, out_shape, out_specs, grid, scratch_shapes=(), plan=None, aliases=None):
    n_in, n_out, n_scr = len(args), len(out_shape), len(scratch_shapes)
    plan = plan or _Plan([], [], {}, [], None, None)
    p_in, p_out = len(plan.ins), len(plan.outs)

    def carrying(*refs):
        ins, p_ins = refs[:n_in], refs[n_in:n_in + p_in]
        o0 = n_in + p_in
        outs, p_outs = refs[o0:o0 + n_out], refs[o0 + n_out:o0 + n_out + p_out]
        s0 = o0 + n_out + p_out
        scr, p_sems = refs[s0:s0 + n_scr], refs[s0 + n_scr:]
        ids = [pl.program_id(a) for a in range(len(grid))]

        if plan.start is not None:
            @pl.when(functools.reduce(jnp.logical_and, [i == 0 for i in ids]))
            def _():
                plan.start(p_ins, p_outs, p_sems)

        body(*ins, *outs, *scr)

        if plan.middle is not None:
            step, n_steps = 0, 1
            for i, g in zip(ids, grid):
                step, n_steps = step * g + i, n_steps * g

            @pl.when(step == (3 * n_steps) // 4)
            def _():
                plan.middle(p_ins, p_outs, p_sems)

        if plan.finish is not None:
            @pl.when(functools.reduce(jnp.logical_and, [i == g - 1 for i, g in zip(ids, grid)]))
            def _():
                plan.finish(p_ins, p_outs, p_sems)

    aliases = dict(aliases or {})
    aliases.update({n_in + i: n_out + o for i, o in plan.alias.items()})
    res = pl.pallas_call(
        carrying, out_shape=list(out_shape) + list(plan.outs), grid=grid,
        in_specs=list(in_specs) + [ANY] * p_in, out_specs=list(out_specs) + [ANY] * p_out,
        scratch_shapes=list(scratch_shapes) + list(plan.sems),
        input_output_aliases=aliases, compiler_params=_params(), name=name,
    )(*args, *plan.ins)
    return list(res[:n_out]), list(res[n_out:])


def _plan_gather(stacks):
    n = len(stacks)
    relations = range(3)

    def ici_copies(outs, sems):
        x, y, c = _mesh_pos()
        chips = _other_chips(x, y)
        cps = []
        for w in range(n):
            own = outs[w].at[2 * x + y, _half_rows(stacks[w].shape[1], c)]
            cps += [_remote(own, own, sems[0].at[w, j], sems[1].at[w, j], (*chips[j], c)) for j in relations]
        return cps

    def start(ins, outs, sems):
        for cp in ici_copies(outs, sems):
            cp.start()

    def forwards(outs, sems, core):
        x, y, c = _mesh_pos()
        slots = [2 * cx + cy for cx, cy in _other_chips(x, y)]
        cps = []
        for w in range(n):
            rows = _half_rows(stacks[w].shape[1], core)
            for j in relations:
                landed = outs[w].at[slots[j], rows]
                cps.append((_remote(landed, landed, sems[0].at[w, j], sems[1].at[w, j], (x, y, 1 - c)),
                            _remote(landed, landed, sems[2].at[w, j], sems[3].at[w, j], (x, y, 1 - c))))
        return cps

    def middle(ins, outs, sems):
        c = _mesh_pos()[2]
        for arrival, forward in forwards(outs, sems, c):
            arrival.wait_recv()
            forward.start()

    def finish(ins, outs, sems):
        c = _mesh_pos()[2]
        for _, forward in forwards(outs, sems, 1 - c):
            forward.wait_recv()
        for cp in ici_copies(outs, sems) + [forward for _, forward in forwards(outs, sems, c)]:
            cp.wait_send()

    return _Plan(stacks, [jax.ShapeDtypeStruct(s.shape, s.dtype) for s in stacks], {w: w for w in range(n)},
                 [pltpu.SemaphoreType.DMA((n, 3))] * 4, start, finish, middle)


def _plan_gather_relay(stacks):
    n = len(stacks)

    def finish(ins, outs, sems):
        send, recv, relay_send, relay_recv, d2d_send, d2d_recv = sems
        x, y, c = _mesh_pos()
        sibling = (x, y, 1 - c)
        near = [(1 - x, y), (x, 1 - y)]
        far = 2 * (1 - x) + (1 - y)
        started = []

        def go(cp):
            cp.start()
            started.append(cp)

        def piece(w, slot, core, quarter=None):
            rh = stacks[w].shape[1] // 2
            if quarter is None:
                return outs[w].at[slot, _half_rows(2 * rh, core)]
            return outs[w].at[slot, pl.ds(pl.multiple_of(core * rh + quarter * (rh // 2), 8), rh // 2)]

        for w in range(n):
            own = piece(w, 2 * x + y, c)
            for j, chip in enumerate(near):
                go(_remote(own, own, send.at[w, j], recv.at[w, j], (*chip, c)))
        for w in range(n):
            for j, (cx, cy) in enumerate(near):
                landed = piece(w, 2 * cx + cy, c)
                _remote(landed, landed, send.at[w, j], recv.at[w, j], sibling).wait_recv()
                part = piece(w, 2 * cx + cy, c, quarter=j)
                go(_remote(part, part, relay_send.at[w, j], relay_recv.at[w, j], (*near[1 - j], c)))
                go(_remote(landed, landed, d2d_send.at[w, j], d2d_recv.at[w, j], sibling))
        for w in range(n):
            for j in range(2):
                part = piece(w, far, c, quarter=j)
                _remote(part, part, relay_send.at[w, j], relay_recv.at[w, j], sibling).wait_recv()
            landed = piece(w, far, c)
            go(_remote(landed, landed, d2d_send.at[w, 2], d2d_recv.at[w, 2], sibling))
        for w in range(n):
            for j, slot in enumerate([2 * cx + cy for cx, cy in near] + [far]):
                landed = piece(w, slot, 1 - c)
                _remote(landed, landed, d2d_send.at[w, j], d2d_recv.at[w, j], sibling).wait_recv()
        for cp in started:
            cp.wait_send()

    return _Plan(stacks, [jax.ShapeDtypeStruct(s.shape, s.dtype) for s in stacks], {w: w for w in range(n)},
                 [pltpu.SemaphoreType.DMA((n, 2))] * 4 + [pltpu.SemaphoreType.DMA((n, 3))] * 2,
                 lambda ins, outs, sems: None, finish)


def _plan_sibling_halves(gs):
    n = len(gs)

    def copies(ins, outs, sems):
        x, y, c = _mesh_pos()
        return [_remote(ins[w].at[:, _half_rows(gs[w].shape[1], 1 - c), :], outs[w], sems[0].at[w], sems[1].at[w],
                        (x, y, 1 - c)) for w in range(n)]

    def start(ins, outs, sems):
        for cp in copies(ins, outs, sems):
            cp.start()

    def finish(ins, outs, sems):
        for cp in copies(ins, outs, sems):
            cp.wait()

    return _Plan(gs, [jax.ShapeDtypeStruct((g.shape[0], g.shape[1] // 2, g.shape[2]), g.dtype) for g in gs], {},
                 [pltpu.SemaphoreType.DMA((n,))] * 2, start, finish)


def _plan_chip_exchange(ps):
    n = len(ps)

    def copies(ins, outs, sems):
        x, y, c = _mesh_pos()
        return [_remote(ins[w].at[2 * cx + cy], outs[w].at[j], sems[0].at[w, j], sems[1].at[w, j], (cx, cy, c))
                for w in range(n) for j, (cx, cy) in enumerate(_other_chips(x, y))]

    def start(ins, outs, sems):
        for cp in copies(ins, outs, sems):
            cp.start()

    def finish(ins, outs, sems):
        for cp in copies(ins, outs, sems):
            cp.wait()

    return _Plan(ps, [jax.ShapeDtypeStruct((3,) + p.shape[1:], p.dtype) for p in ps], {},
                 [pltpu.SemaphoreType.DMA((n, 3))] * 2, start, finish)


def _plan_sibling_share(gs):
    n = len(gs)

    def copies(outs, sems, which):
        x, y, c = _mesh_pos()
        cps = []
        for w in range(n):
            rows = outs[w].at[_half_rows(gs[w].shape[0], c if which == "mine" else 1 - c)]
            cps.append(_remote(rows, rows, sems[0].at[w], sems[1].at[w], (x, y, 1 - c)))
        return cps

    def start(ins, outs, sems):
        for cp in copies(outs, sems, "mine"):
            cp.start()

    def finish(ins, outs, sems):
        for cp in copies(outs, sems, "mine"):
            cp.wait_send()
        for cp in copies(outs, sems, "theirs"):
            cp.wait_recv()

    return _Plan(gs, [jax.ShapeDtypeStruct(g.shape, g.dtype) for g in gs], {w: w for w in range(n)},
                 [pltpu.SemaphoreType.DMA((n,))] * 2, start, finish)


def _same_shape_groups(arrays):
    groups = {}
    for i, a in enumerate(arrays):
        groups.setdefault(a.shape, []).append(i)
    return list(groups.values())


def _add_sibling(gs, r1s, ids, tag):
    n = len(gs)
    nch, rh, cols = r1s[0].shape

    def body(ids_ref, *refs):
        for g_ref, r_ref, o_ref in zip(refs[:n], refs[n:2 * n], refs[2 * n:]):
            o_ref[...] = (g_ref[...] + r_ref[...]).astype(BF16)

    blk = lambda fn: pl.BlockSpec((None, rh, cols), fn)
    return pl.pallas_call(
        body, out_shape=[jax.ShapeDtypeStruct(r1s[0].shape, BF16)] * n,
        grid_spec=pltpu.PrefetchScalarGridSpec(
            num_scalar_prefetch=1, grid=(nch,),
            in_specs=[blk(lambda k, ids: (k, ids[1], 0))] * n + [blk(lambda k, ids: (k, 0, 0))] * n,
            out_specs=[blk(lambda k, ids: (k, 0, 0))] * n),
        compiler_params=_params(), name="add_sibling_" + tag,
    )(ids, *gs, *r1s)


def _add_chips(gs, r1s, r2s, ids, tag):
    n = len(gs)
    _, rh, cols = r1s[0].shape
    nb = 2 if rh % 32 == 0 else 1
    rb = rh // nb

    def body(ids_ref, *refs):
        for g_ref, r1_ref, r2_ref, o_ref in zip(refs[:n], refs[n:2 * n], refs[2 * n:3 * n], refs[3 * n:]):
            own = g_ref[...] + r1_ref[...]
            o_ref[...] = ((own + r2_ref[0].astype(F32)) + r2_ref[1].astype(F32)) + r2_ref[2].astype(F32)

    return pl.pallas_call(
        body, out_shape=[jax.ShapeDtypeStruct((2 * rh, cols), F32)] * n,
        grid_spec=pltpu.PrefetchScalarGridSpec(
            num_scalar_prefetch=1, grid=(nb,),
            in_specs=[pl.BlockSpec((None, rb, cols), lambda i, ids: (ids[0], ids[1] * nb + i, 0))] * n
            + [pl.BlockSpec((None, rb, cols), lambda i, ids: (ids[0], i, 0))] * n
            + [pl.BlockSpec((3, rb, cols), lambda i, ids: (0, i, 0))] * n,
            out_specs=[pl.BlockSpec((rb, cols), lambda i, ids: (ids[1] * nb + i, 0))] * n),
        compiler_params=_params(), name="add_chips_" + tag,
    )(ids, *gs, *r1s, *r2s)


VEC_ROWS = 8


N_DEVICES = 8


def _small_pack(part, d, width):
    names = ("ffn1_norm", "mix_norm", "ffn2_norm", "pool_scale", "out_norm_pool", "out_norm_attn", "qn", "kn", "b_forget",
             "pool_w", "loss")
    args = [part[k] for k in names]
    pw_shape = part["pool_w"].shape[1:]

    def body(g1_ref, gm_ref, g2_ref, ps_ref, onp_ref, ona_ref, qn_ref, kn_ref, bf_ref, pw_ref, loss_ref, vbuf, pbuf):
        lo = _head_masks()

        def fold_heads(ref):
            v = jnp.sum(ref[...], axis=0)
            acc = jnp.zeros((VEC_ROWS, LANES), F32)
            for blk in range(width // LANES):
                vb = jnp.broadcast_to(v[:, blk * LANES:(blk + 1) * LANES], (VEC_ROWS, LANES))
                acc = acc + vb + pltpu.roll(vb, HEAD_DIM, 1)
            return jnp.where(lo, acc, 0.0)[0:1, :]

        vbuf[0] = jnp.zeros((VEC_ROWS, d), F32)
        vbuf[0, 0:1, :] = jnp.sum(g1_ref[...], axis=0)
        vbuf[0, 1:2, :] = jnp.sum(gm_ref[...], axis=0)
        vbuf[0, 2:3, :] = jnp.sum(g2_ref[...], axis=0)
        vbuf[0, 5:6, 0:LANES] = jnp.sum(loss_ref[...], axis=0)[0:1, :]
        vbuf[0, 3:4, 0:width] = jnp.sum(ps_ref[...], axis=0)
        vbuf[0, 3:4, width:2 * width] = jnp.sum(onp_ref[...], axis=0)
        vbuf[0, 4:5, 0:width] = jnp.sum(ona_ref[...], axis=0)
        vbuf[0, 4:5, width:width + LANES] = fold_heads(qn_ref)
        vbuf[0, 4:5, width + LANES:width + 2 * LANES] = fold_heads(kn_ref)
        vbuf[0, 4:5, width + 2 * LANES:width + 3 * LANES] = jnp.sum(bf_ref[...], axis=0)
        pbuf[0] = jnp.sum(pw_ref[...], axis=0)

    return pl.pallas_call(
        body, out_shape=[jax.ShapeDtypeStruct((N_DEVICES, VEC_ROWS, d), F32), jax.ShapeDtypeStruct((N_DEVICES,) + pw_shape, F32)],
        in_specs=[VM] * len(args), out_specs=[VM, VM], compiler_params=_params(), name="small_pack",
    )(*args)


def _plan_all_to_all(stacks):
    n = len(stacks)

    def copies(outs, sems):
        x, y, c = _mesh_pos()
        cps = []
        for r in range(1, N_DEVICES):
            peer = (x if not r & 4 else 1 - x, y if not r & 2 else 1 - y, c if not r & 1 else 1 - c)
            cps += [_remote(outs[w].at[0], outs[w].at[r], sems[0].at[w, r - 1], sems[1].at[w, r - 1], peer) for w in range(n)]
        return cps

    def start(ins, outs, sems):
        for cp in copies(outs, sems):
            cp.start()

    def finish(ins, outs, sems):
        for cp in copies(outs, sems):
            cp.wait()

    return _Plan(stacks, [jax.ShapeDtypeStruct(s.shape, s.dtype) for s in stacks], {w: w for w in range(n)},
                 [pltpu.SemaphoreType.DMA((n, N_DEVICES - 1))] * 2, start, finish)


def _small_sum(vstack, pstack, me):
    def body(me_ref, vbuf, pbuf, vec_ref, pw_ref):
        vec = vbuf[me_ref[0]]
        pw = pbuf[me_ref[0]]
        for dev in range(1, N_DEVICES):
            vec = vec + vbuf[jnp.bitwise_xor(me_ref[0], dev)]
            pw = pw + pbuf[jnp.bitwise_xor(me_ref[0], dev)]
        vec_ref[...] = vec
        pw_ref[...] = pw

    full = lambda s: pl.BlockSpec(s.shape, lambda i, me: (0,) * len(s.shape))
    outs = [jax.ShapeDtypeStruct(vstack.shape[1:], F32), jax.ShapeDtypeStruct(pstack.shape[1:], F32)]
    return pl.pallas_call(
        body, out_shape=outs,
        grid_spec=pltpu.PrefetchScalarGridSpec(num_scalar_prefetch=1, grid=(1,), in_specs=[full(vstack), full(pstack)],
                                               out_specs=[full(o) for o in outs]),
        compiler_params=_params(), name="small_sum",
    )(me, vstack, pstack)


def _adamw(ws, gs, ms, vs, tag):
    n = len(ws)
    rows, cols = ws[0].shape
    rb = rows
    while rb * cols * 4 * n > (1 << 20) and rb % 16 == 0:
        rb //= 2

    def body(*refs):
        for j in range(n):
            w_ref, g_ref, m_ref, v_ref = (refs[k * n + j] for k in range(4))
            go_ref, d_ref, mo_ref, vo_ref = (refs[(4 + k) * n + j] for k in range(4))
            gv = g_ref[...]
            go_ref[...] = gv
            m2 = ADAM_B1 * m_ref[...] + (1.0 - ADAM_B1) * gv
            v2 = ADAM_B2 * v_ref[...] + (1.0 - ADAM_B2) * (gv * gv)
            m_hat = m2 / (1.0 - ADAM_B1 ** ADAM_STEP)
            v_hat = v2 / (1.0 - ADAM_B2 ** ADAM_STEP)
            d_ref[...] = -ADAM_LR * (m_hat / (jnp.sqrt(v_hat) + ADAM_EPS) + ADAM_WD * w_ref[...])
            mo_ref[...] = m2
            vo_ref[...] = v2

    spec = pl.BlockSpec((rb, cols), lambda i: (i, 0))
    res, _ = _pallas(
        body, name="adamw_" + tag, args=[*ws, *gs, *ms, *vs], out_shape=[jax.ShapeDtypeStruct(ws[0].shape, F32)] * (4 * n),
        grid=(rows // rb,), in_specs=[spec] * (4 * n), out_specs=[spec] * (4 * n))
    return [tuple(res[k * n + j] for k in range(4)) for j in range(n)]


def _pack_vec(p, d, width):
    pad = lambda v: jnp.pad(v, (0, LANES - v.shape[0]))
    row3 = jnp.concatenate([p["pool_scale"], p["out_norm_pool"]])
    row4 = jnp.concatenate([p["out_norm_attn"], pad(p["q_norm"]), pad(p["k_norm"]), pad(p["b_forget"]),
                            jnp.zeros((d - width - 3 * LANES,), F32)])
    rows = [p["ffn1_norm"], p["mix_norm"], p["ffn2_norm"], row3, row4]
    return jnp.pad(jnp.stack(rows), ((0, VEC_ROWS - len(rows)), (0, 0)))


def _unpack_vec(vec, width):
    return dict(ffn1_norm=vec[0], mix_norm=vec[1], ffn2_norm=vec[2], pool_scale=vec[3, :width],
                out_norm_pool=vec[3, width:2 * width], out_norm_attn=vec[4, :width],
                q_norm=vec[4, width:width + HEAD_DIM], k_norm=vec[4, width + LANES:width + LANES + HEAD_DIM],
                b_forget=vec[4, width + 2 * LANES:width + 2 * LANES + N_HEADS])


WEIGHT_NAMES = ("ffn1_norm", "ffn1_w_gate", "ffn1_w_up", "ffn1_w_down", "mix_norm", "w_in", "b_forget", "pool_w",
                "pool_scale", "q_norm", "k_norm", "out_norm_pool", "out_norm_attn", "w_out", "ffn2_norm",
                "ffn2_w_gate", "ffn2_w_up", "ffn2_w_down")
BIG_NAMES = ("ffn1_w_gate", "ffn1_w_up", "ffn1_w_down", "w_in", "w_out", "ffn2_w_gate", "ffn2_w_up", "ffn2_w_down")
TRANSPOSED_NAMES = ("ffn1_w_gate", "ffn1_w_up", "w_in", "ffn2_w_gate", "ffn2_w_up")
FFN1_NAMES = ("ffn1_w_gate", "ffn1_w_up", "ffn1_w_down")
MIX_NAMES = ("w_in", "w_out")
FFN2_NAMES = ("ffn2_w_gate", "ffn2_w_up", "ffn2_w_down")


def kernel(x, ffn1_norm, ffn1_w_gate, ffn1_w_up, ffn1_w_down, mix_norm, w_in, b_forget, pool_w, pool_scale, q_norm, k_norm, out_norm_pool, out_norm_attn, w_out, ffn2_norm, ffn2_w_gate, ffn2_w_up, ffn2_w_down, loss_target, m_ffn1_norm, m_ffn1_w_gate, m_ffn1_w_up, m_ffn1_w_down, m_mix_norm, m_w_in, m_b_forget, m_pool_w, m_pool_scale, m_q_norm, m_k_norm, m_out_norm_pool, m_out_norm_attn, m_w_out, m_ffn2_norm, m_ffn2_w_gate, m_ffn2_w_up, m_ffn2_w_down, v_ffn1_norm, v_ffn1_w_gate, v_ffn1_w_up, v_ffn1_w_down, v_mix_norm, v_w_in, v_b_forget, v_pool_w, v_pool_scale, v_q_norm, v_k_norm, v_out_norm_pool, v_out_norm_attn, v_w_out, v_ffn2_norm, v_ffn2_w_gate, v_ffn2_w_up, v_ffn2_w_down):
    given = dict(locals())
    w = {n: given[n] for n in WEIGHT_NAMES}
    m = {n: given["m_" + n] for n in WEIGHT_NAMES}
    v = {n: given["v_" + n] for n in WEIGHT_NAMES}
    n_batch, seq, d = x.shape
    width = pool_scale.shape[0]
    in_rows = w_in.shape[1]
    in_cols = N_CHIPS * in_rows
    in_pad = -(-in_rows // 32) * 32
    in_cols_pad = in_cols - N_HEADS + LANES

    work = lambda a, n: a.T if n in TRANSPOSED_NAMES else a
    exchanged = lambda a, n: jnp.pad(a, ((0, in_pad - in_rows), (0, 0))) if n == "w_in" else a

    mesh_x, mesh_y, mesh_c = _mesh_pos()
    ids = jnp.stack([2 * mesh_x + mesh_y, mesh_c]).astype(jnp.int32)

    row = lambda a: a.reshape(1, -1)
    g1, gm, g2, ps, onp, ona = (row(a) for a in (ffn1_norm, mix_norm, ffn2_norm, pool_scale, out_norm_pool, out_norm_attn))
    qn, kn = row(jnp.tile(q_norm, N_HEADS)), row(jnp.tile(k_norm, N_HEADS))
    bf = row(jnp.pad(b_forget, (0, LANES - N_HEADS)))
    pwb = pool_w.astype(BF16)
    xf, tgt = x.reshape(n_batch * seq, d), loss_target.reshape(n_batch * seq, d)

    def grouped(call, names, *lists):
        out = [None] * len(names)
        for idx in _same_shape_groups(lists[0]):
            res = call(*[[lst[i] for i in idx] for lst in lists], names[idx[0]])
            for i, r in zip(idx, res):
                out[i] = r
        return out

    placed = dict(zip(BIG_NAMES, grouped(lambda ws, tag: _place_cast(ws, ids, tag), BIG_NAMES,
                                         [exchanged(work(w[n], n), n) for n in BIG_NAMES])))
    wg1, wu1, wd1 = _run_plan(_plan_gather_relay([placed[n] for n in FFN1_NAMES]), "gather_ffn1")
    (x1, h1, a1, b1, s1), (w_in_all, w_out_all, wd2) = _ffn_fwd(
        xf, g1, wg1, wu1, wd1, plan=_plan_gather([placed[n] for n in MIX_NAMES + FFN2_NAMES[2:]]))
    w_in_t = jnp.pad(w_in_all[:, :in_rows].reshape(in_cols, d), ((0, in_cols_pad - in_cols), (0, 0)))
    w_out_full = w_out_all.reshape(N_CHIPS * w_out.shape[0], d)
    woa, wob = w_out_full[:width], w_out_full[width:]

    hm, pv, q, k, qh, kh, vb, f = _mix_proj(x1, gm, w_in_t, qn, kn, width, width)
    qa, ka = _forget_prefix(f, bf, qh, kh, n_batch, seq)
    yp = _pool_fwd(pv, pwb, ps, onp, n_batch, seq)
    (o, lse), (wg2, wu2) = _attn_fwd(qa, ka, vb, n_batch, seq, plan=_plan_gather([placed[n] for n in FFN2_NAMES[:2]]))
    x2, ya = _mix_out(x1, yp, o, ona, woa, wob)
    (dy, h2, a2, b2, s2, lpart, dyh), _ = _ffn_fwd(x2, g2, wg2, wu2, wd2, target=tgt)

    def to_chips(gs, arrived, tags):
        return grouped(lambda g, r, tag: _add_sibling(g, r, ids, tag), tags, gs, arrived)

    def own_rows(gs, from_sibling, from_chips, tags):
        return grouped(lambda g, ra, rb, tag: _add_chips(g, ra, rb, ids, tag), tags, gs, from_sibling, from_chips)

    (dx2, da2, db2, dg2), _ = _ffn_bwd_x(dy, x2, g2, a2, b2, wg2, wu2, wd2, "ffn2_bwd_x")
    dw2, _ = _ffn_bwd_w([(da2, h2), (db2, h2), (s2, dyh)], "ffn2_bwd_w")
    (dyp, do, delta, dwoa, dwob, dona), sib2 = _mix_out_bwd(dx2, o, yp, ya, ona, woa, wob, plan=_plan_sibling_halves(dw2))
    dpv, dpw, dps, donp = _pool_bwd(pv, dyp, pwb, ps, onp, n_batch, seq)
    (dqh, dfq), chips2 = _attn_bwd_q(qa, ka, vb, do, lse, delta, n_batch, seq,
                                     plan=_plan_chip_exchange(to_chips(dw2, sib2, FFN2_NAMES)))
    (dkh, dv, dfk), red2 = _attn_bwd_kv(qa, ka, vb, do, lse, delta, n_batch, seq,
                                        plan=_plan_sibling_share(own_rows(dw2, sib2, chips2, FFN2_NAMES)))
    df, dbf = _forget_bwd(dfq, dfk, f, bf, n_batch, seq)
    dx1, dx1h, dw_in_t, dgm, dqn, dkn = _mix_in_bwd(dx2, x1, gm, hm, dpv, dqh, q, dkh, k, dv, df, qn, kn, w_in_t)
    in_base = [in_rows * k // 8 * 8 for k in range(N_CHIPS)]
    d_w_in = jnp.stack([dw_in_t[b:b + in_pad] for b in in_base])
    d_w_out = jnp.concatenate([dwoa, dwob], axis=0).reshape(N_CHIPS, w_out.shape[0], d)
    dwm = [d_w_in, d_w_out]
    down, gate_up = FFN1_NAMES[2:], FFN1_NAMES[:2]
    dwd1, sibm = _ffn_bwd_w([(s1, dx1h)], "ffn1_bwd_w_down", plan=_plan_sibling_halves(dwm))
    (da1, db1), arrived = _ffn_bwd_a(dx1h, a1, b1, wd1, "ffn1_bwd_a",
                                     plan=_merge_plans(_plan_sibling_halves(dwd1),
                                                       _plan_chip_exchange(to_chips(dwm, sibm, MIX_NAMES))))
    sibd, chipsm = arrived[:1], arrived[1:]
    dwgu1, chipsd = _ffn_bwd_w([(da1, h1), (db1, h1)], "ffn1_bwd_w_gate_up",
                               plan=_plan_chip_exchange(to_chips(dwd1, sibd, down)))
    n_tiles = (n_batch * seq) // min(FFN_TILE, n_batch * seq)
    first = max(n_tiles // 4, 1)
    begun, sibgu = _ffn_bwd_h(dx1, xf, g1, da1, db1, wg1, wu1, "ffn1_bwd_h_first", (0, first),
                              plan=_plan_sibling_halves(dwgu1))
    (gx, dg1), chipsgu = _ffn_bwd_h(dx1, xf, g1, da1, db1, wg1, wu1, "ffn1_bwd_h_rest", (first, n_tiles), prev=begun,
                                    plan=_plan_chip_exchange(to_chips(dwgu1, sibgu, gate_up)))

    part = dict(ffn1_norm=dg1, mix_norm=dgm, ffn2_norm=dg2, b_forget=dbf, pool_scale=dps, out_norm_pool=donp,
                out_norm_attn=dona, qn=dqn, kn=dkn, pool_w=dpw.reshape(n_batch, -1, pool_w.shape[-1]), loss=lpart)
    mine = (own_rows(dwgu1, sibgu, chipsgu, gate_up) + own_rows(dwd1, sibd, chipsd, down)
            + own_rows(dwm, sibm, chipsm, MIX_NAMES))
    last = _run_plan(_merge_plans(_plan_sibling_share(mine), _plan_all_to_all(_small_pack(part, d, width))), "last_exchange")
    vstack, pstack = last[len(mine):]
    g_vec, g_pw = _small_sum(vstack, pstack, jnp.reshape(4 * mesh_x + 2 * mesh_y + mesh_c, (1,)).astype(jnp.int32))
    loss = g_vec[5, 0]
    reduced = dict(zip(FFN1_NAMES + MIX_NAMES + FFN2_NAMES, list(last[:len(mine)]) + list(red2)))
    reduced["w_in"] = lax.dynamic_slice(reduced["w_in"], ((in_rows * ids[0]) % 8, 0), (in_rows, d))

    grads, delta, new_m, new_v = {}, {}, {}, {}
    for names in (FFN2_NAMES, FFN1_NAMES, ("w_in",), ("w_out",)):
        stepped = _adamw([work(w[n], n) for n in names], [reduced[n] for n in names], [work(m[n], n) for n in names],
                         [work(v[n], n) for n in names], names[0])
        for n, step in zip(names, stepped):
            grads[n], delta[n], new_m[n], new_v[n] = (work(a, n) for a in step)
    flat_pw = lambda a: a.reshape(-1, a.shape[-1])
    (_, d_pw, m_pw, v_pw), = _adamw([flat_pw(pool_w)], [g_pw], [flat_pw(m_pool_w)], [flat_pw(v_pool_w)], "pool_w")
    (_, d_vec, m_vec, v_vec), = _adamw([_pack_vec(w, d, width)], [g_vec], [_pack_vec(m, d, width)],
                                       [_pack_vec(v, d, width)], "vectors")
    grads.update(_unpack_vec(g_vec, width), pool_w=g_pw.reshape(pool_w.shape))
    delta.update(_unpack_vec(d_vec, width), pool_w=d_pw.reshape(pool_w.shape))
    new_m.update(_unpack_vec(m_vec, width), pool_w=m_pw.reshape(pool_w.shape))
    new_v.update(_unpack_vec(v_vec, width), pool_w=v_pw.reshape(pool_w.shape))
    return (loss, gx.reshape(x.shape), *[grads[n] for n in WEIGHT_NAMES], *[delta[n] for n in WEIGHT_NAMES],
            *[new_m[n] for n in WEIGHT_NAMES], *[new_v[n] for n in WEIGHT_NAMES])
```

```python
import functools

import jax
import jax.numpy as jnp
from jax import lax
from jax.experimental import pallas as pl
from jax.experimental.pallas import tpu as pltpu

F32 = jnp.float32
BF16 = jnp.bfloat16
EPS = 1e-6
NEG = -1e30
ADAM_LR = 0.001
ADAM_B1 = 0.9
ADAM_B2 = 0.999
ADAM_EPS = 1e-08
ADAM_WD = 0.01
ADAM_STEP = 10
POOL_WINDOWS = (2, 4, 8, 16)
HEAD_DIM = 64
N_HEADS = 8
LANES = 128
N_CHIPS = 4
ATT_BLOCK = 512
ATT_SUB = 128
FFN_TILE = 1024
VMEM_LIMIT = 62 * 1024 * 1024
ANY = pl.BlockSpec(memory_space=pl.ANY)
VM = pl.BlockSpec(memory_space=pltpu.VMEM)


def _params(**kw):
    return pltpu.CompilerParams(vmem_limit_bytes=VMEM_LIMIT, **kw)


def _dot(a, b):
    return jnp.dot(a, b, preferred_element_type=F32)


def _dot_nt(a, b):
    return lax.dot_general(a, b, (((1,), (1,)), ((), ())), preferred_element_type=F32)


def _dot_tn(a, b):
    return lax.dot_general(a, b, (((0,), (0,)), ((), ())), preferred_element_type=F32)


def _sigmoid(z):
    return 1.0 / (1.0 + jnp.exp(-z))


def _rms(xf):
    return lax.rsqrt(jnp.mean(xf * xf, axis=-1, keepdims=True) + EPS)


def _rms_bwd(xf, r, gain, dh):
    xh = xf * r
    dyg = dh * gain
    return r * (dyg - xh * jnp.mean(dyg * xh, axis=-1, keepdims=True)), dh * xh


def _total(v):
    return jnp.sum(jnp.sum(v, axis=1, keepdims=True), axis=0, keepdims=True)


def _ffn_fwd(x, gain, wg, wu, wd, target=None, plan=None):
    t, d = x.shape
    nch, fc, _ = wg.shape
    tm = min(FFN_TILE, t)
    nt = t // tm
    with_loss = target is not None

    def body(*refs):
        if with_loss:
            x_ref, g_ref, wg_ref, wu_ref, wd_ref, t_ref, o_ref, h_ref, a_ref, b_ref, s_ref, l_ref, oh_ref, acc_ref = refs
        else:
            x_ref, g_ref, wg_ref, wu_ref, wd_ref, o_ref, h_ref, a_ref, b_ref, s_ref, acc_ref = refs
        k = pl.program_id(1)

        @pl.when(k == 0)
        def _():
            xf = x_ref[...]
            h_ref[...] = ((xf * _rms(xf)) * g_ref[...]).astype(BF16)
            acc_ref[...] = jnp.zeros_like(acc_ref)

        for rows in _row_halves(tm):
            h = h_ref[rows, :]
            a = _dot_nt(h, wg_ref[...])
            b = _dot_nt(h, wu_ref[...])
            sb = ((a * (0.5 * jnp.tanh(0.5 * a) + 0.5)) * b).astype(BF16)
            a_ref[rows, :] = a.astype(BF16)
            b_ref[rows, :] = b.astype(BF16)
            s_ref[rows, :] = sb
            acc_ref[rows, :] += _dot(sb, wd_ref[...])

        @pl.when(k == nch - 1)
        def _():
            y = x_ref[...] + 0.5 * acc_ref[...]
            if with_loss:
                e = y - t_ref[...]
                o_ref[...] = e * (1.0 / d)
                oh_ref[...] = (e * (0.5 / d)).astype(BF16)
                l_ref[...] = jnp.broadcast_to(_total(e * e) * (0.5 / d), l_ref.shape)
            else:
                o_ref[...] = y

    row = pl.BlockSpec((tm, d), lambda i, k: (i, 0))
    chunk = pl.BlockSpec((None, fc, d), lambda i, k: (k, 0, 0))
    act = pl.BlockSpec((None, tm, fc), lambda i, k: (k, i, 0))
    in_specs = [row, pl.BlockSpec((1, d), lambda i, k: (0, 0)), chunk, chunk, chunk]
    out_shape = [jax.ShapeDtypeStruct((t, d), F32), jax.ShapeDtypeStruct((t, d), BF16)]
    out_shape += [jax.ShapeDtypeStruct((nch, t, fc), BF16)] * 3
    out_specs = [row, row, act, act, act]
    args = [x, gain, wg, wu, wd]
    if with_loss:
        in_specs.append(row)
        args.append(target)
        out_shape += [jax.ShapeDtypeStruct((nt, 8, LANES), F32), jax.ShapeDtypeStruct((t, d), BF16)]
        out_specs += [pl.BlockSpec((None, 8, LANES), lambda i, k: (i, 0, 0)), row]
    return _pallas(body, name="ffn_fwd_loss" if with_loss else "ffn_fwd", args=args, in_specs=in_specs,
                   out_shape=out_shape, out_specs=out_specs, grid=(nt, nch),
                   scratch_shapes=[pltpu.VMEM((tm, d), F32)], plan=plan)


def _row_halves(n):
    return [slice(0, n // 2), slice(n // 2, n)]


def _swiglu_grads(dyh, a_ref, b_ref, wd_ref, rows):
    ds = _dot_nt(dyh, wd_ref[...])
    av = a_ref[rows, :].astype(F32)
    bv = b_ref[rows, :].astype(F32)
    th = jnp.tanh(0.5 * av)
    sig = 0.5 * th + 0.5
    dab = ((ds * bv) * (sig * (1.0 + av * (0.5 - 0.5 * th)))).astype(BF16)
    return dab, (ds * (av * sig)).astype(BF16)


def _ffn_bwd_a(dyh, a, b, wd, name, plan=None):
    t, d = dyh.shape
    nch, fc, _ = wd.shape
    tm = min(FFN_TILE, t)

    def body(dyh_ref, a_ref, b_ref, wd_ref, da_ref, db_ref):
        for rows in _row_halves(tm):
            da_ref[rows, :], db_ref[rows, :] = _swiglu_grads(dyh_ref[rows, :], a_ref, b_ref, wd_ref, rows)

    act = pl.BlockSpec((None, tm, fc), lambda i, k: (k, i, 0))
    return _pallas(
        body, name=name, args=[dyh, a, b, wd], out_shape=[jax.ShapeDtypeStruct((nch, t, fc), BF16)] * 2, grid=(t // tm, nch),
        in_specs=[pl.BlockSpec((tm, d), lambda i, k: (i, 0)), act, act, pl.BlockSpec((None, fc, d), lambda i, k: (k, 0, 0))],
        out_specs=[act, act], plan=plan)


def _ffn_bwd_h(dy, x, gain, da, db, wg, wu, name, tiles, prev=None, plan=None):
    t, d = x.shape
    nch, fc, _ = wg.shape
    tm = min(FFN_TILE, t)
    nt = t // tm
    t0, t1 = tiles

    def body(*refs):
        dy_ref, x_ref, g_ref, da_ref, db_ref, wg_ref, wu_ref = refs[:7]
        dx_ref, dg_ref, acc_ref = refs[-3:]
        k = pl.program_id(1)

        @pl.when(k == 0)
        def _():
            acc_ref[...] = jnp.zeros_like(acc_ref)

        acc_ref[...] += _dot(da_ref[...], wg_ref[...]) + _dot(db_ref[...], wu_ref[...])

        @pl.when(k == nch - 1)
        def _():
            xf = x_ref[...]
            dxn, dgr = _rms_bwd(xf, _rms(xf), g_ref[...], acc_ref[...])
            dx_ref[...] = dy_ref[...] + dxn
            dg_ref[...] = jnp.sum(dgr, axis=0, keepdims=True)

    row = pl.BlockSpec((tm, d), lambda i, k: (i + t0, 0))
    chunk = pl.BlockSpec((None, fc, d), lambda i, k: (k, 0, 0))
    act = pl.BlockSpec((None, tm, fc), lambda i, k: (k, i + t0, 0))
    args = [dy, x, gain, da, db, wg, wu]
    in_specs = [row, row, pl.BlockSpec((1, d), lambda i, k: (0, 0)), act, act, chunk, chunk]
    aliases = {}
    if prev is not None:
        aliases = {len(args): 0, len(args) + 1: 1}
        args += list(prev)
        in_specs += [ANY, ANY]
    return _pallas(
        body, name=name, args=args, out_shape=[jax.ShapeDtypeStruct((t, d), F32), jax.ShapeDtypeStruct((nt, 1, d), F32)],
        grid=(t1 - t0, nch), in_specs=in_specs,
        out_specs=[row, pl.BlockSpec((None, 1, d), lambda i, k: (i + t0, 0, 0))],
        scratch_shapes=[pltpu.VMEM((tm, d), F32)], plan=plan, aliases=aliases)


def _ffn_bwd_x(dy, x, gain, a, b, wg, wu, wd, name, plan=None):
    t, d = x.shape
    nch, fc, _ = wg.shape
    tm = min(FFN_TILE, t)
    nt = t // tm

    def body(dy_ref, x_ref, g_ref, a_ref, b_ref, wg_ref, wu_ref, wd_ref, dx_ref, da_ref, db_ref, dg_ref, acc_ref):
        k = pl.program_id(1)

        @pl.when(k == 0)
        def _():
            acc_ref[...] = jnp.zeros_like(acc_ref)

        for rows in _row_halves(tm):
            dab, dbb = _swiglu_grads((0.5 * dy_ref[rows, :]).astype(BF16), a_ref, b_ref, wd_ref, rows)
            da_ref[rows, :] = dab
            db_ref[rows, :] = dbb
            acc_ref[rows, :] += _dot(dab, wg_ref[...]) + _dot(dbb, wu_ref[...])

        @pl.when(k == nch - 1)
        def _():
            xf = x_ref[...]
            dxn, dgr = _rms_bwd(xf, _rms(xf), g_ref[...], acc_ref[...])
            dx_ref[...] = dy_ref[...] + dxn
            dg_ref[...] = jnp.sum(dgr, axis=0, keepdims=True)

    row = pl.BlockSpec((tm, d), lambda i, k: (i, 0))
    chunk = pl.BlockSpec((None, fc, d), lambda i, k: (k, 0, 0))
    act = pl.BlockSpec((None, tm, fc), lambda i, k: (k, i, 0))
    return _pallas(
        body, name=name, args=[dy, x, gain, a, b, wg, wu, wd],
        out_shape=[jax.ShapeDtypeStruct((t, d), F32), jax.ShapeDtypeStruct((nch, t, fc), BF16),
                   jax.ShapeDtypeStruct((nch, t, fc), BF16), jax.ShapeDtypeStruct((nt, 1, d), F32)],
        grid=(nt, nch),
        in_specs=[row, row, pl.BlockSpec((1, d), lambda i, k: (0, 0)), act, act, chunk, chunk, chunk],
        out_specs=[row, act, act, pl.BlockSpec((None, 1, d), lambda i, k: (i, 0, 0))],
        scratch_shapes=[pltpu.VMEM((tm, d), F32)], plan=plan)


def _ffn_bwd_w(pairs, name, plan=None):
    n = len(pairs)
    nch, t, fc = pairs[0][0].shape
    d = pairs[0][1].shape[1]
    tm = min(FFN_TILE, t)

    def body(*refs):
        @pl.when(pl.program_id(1) == 0)
        def _():
            for o_ref in refs[2 * n:]:
                o_ref[...] = jnp.zeros_like(o_ref)

        for j in range(n):
            refs[2 * n + j][...] += _dot_tn(refs[j][...], refs[n + j][...])

    row = pl.BlockSpec((tm, d), lambda k, i: (i, 0))
    act = pl.BlockSpec((None, tm, fc), lambda k, i: (k, i, 0))
    chunk = pl.BlockSpec((None, fc, d), lambda k, i: (k, 0, 0))
    return _pallas(body, name=name, args=[p[0] for p in pairs] + [p[1] for p in pairs],
                   out_shape=[jax.ShapeDtypeStruct((nch, fc, d), F32)] * n, grid=(nch, t // tm),
                   in_specs=[act] * n + [row] * n, out_specs=[chunk] * n, plan=plan)


def _head_masks():
    lane = lax.broadcasted_iota(jnp.int32, (1, LANES), 1)
    return lane < HEAD_DIM


def _head_rms(x, lo):
    x2 = x * x
    s0 = jnp.sum(jnp.where(lo, x2, 0.0), axis=1, keepdims=True)
    s1 = jnp.sum(jnp.where(lo, 0.0, x2), axis=1, keepdims=True)
    return jnp.where(lo, lax.rsqrt(s0 * (1.0 / HEAD_DIM) + EPS), lax.rsqrt(s1 * (1.0 / HEAD_DIM) + EPS))


def _head_mean(v, lo):
    s0 = jnp.sum(jnp.where(lo, v, 0.0), axis=1, keepdims=True)
    s1 = jnp.sum(jnp.where(lo, 0.0, v), axis=1, keepdims=True)
    return jnp.where(lo, s0, s1) * (1.0 / HEAD_DIM)


def _mix_proj(x1, gain, wt, qn, kn, pool_width, attn_width):
    t, d = x1.shape
    tm = min(512, t)
    nt = t // tm
    scale = HEAD_DIM ** -0.5
    c_q, c_k, c_v = pool_width, pool_width + attn_width, pool_width + 2 * attn_width
    c_f = c_v + attn_width

    def body(x_ref, g_ref, wt_ref, qn_ref, kn_ref, hm_ref, pv_ref, q_ref, k_ref, qh_ref, kh_ref, vb_ref, f_ref):
        lo = _head_masks()
        for rows in _row_halves(tm):
            xf = x_ref[rows, :]
            hm = ((xf * _rms(xf)) * g_ref[...]).astype(BF16)
            hm_ref[rows, :] = hm
            f_ref[rows, :] = _dot_nt(hm, wt_ref[c_f:c_f + LANES, :])
            pv_ref[rows, :] = _dot_nt(hm, wt_ref[0:pool_width, :])
            vb_ref[rows, :] = _dot_nt(hm, wt_ref[c_v:c_v + attn_width, :]).astype(BF16)
            for c0, raw_ref, hat_ref, n_ref, mul in ((c_q, q_ref, qh_ref, qn_ref, scale), (c_k, k_ref, kh_ref, kn_ref, 1.0)):
                raw = _dot_nt(hm, wt_ref[c0:c0 + attn_width, :])
                raw_ref[rows, :] = raw
                for blk in range(attn_width // LANES):
                    sl = slice(blk * LANES, (blk + 1) * LANES)
                    xb = raw[:, sl]
                    hat_ref[rows, sl] = (((xb * _head_rms(xb, lo)) * n_ref[:, sl]) * mul).astype(BF16)

    row = pl.BlockSpec((tm, d), lambda i: (i, 0))
    half = pl.BlockSpec((tm, attn_width), lambda i: (i, 0))
    const = lambda shape: pl.BlockSpec(shape, lambda i: (0, 0))
    return _pallas(
        body, name="mix_proj", args=[x1, gain, wt, qn, kn],
        out_shape=[jax.ShapeDtypeStruct((t, d), BF16), jax.ShapeDtypeStruct((t, pool_width), F32),
                   jax.ShapeDtypeStruct((t, attn_width), F32), jax.ShapeDtypeStruct((t, attn_width), F32),
                   jax.ShapeDtypeStruct((t, attn_width), BF16), jax.ShapeDtypeStruct((t, attn_width), BF16),
                   jax.ShapeDtypeStruct((t, attn_width), BF16), jax.ShapeDtypeStruct((t, LANES), F32)],
        grid=(nt,),
        in_specs=[row, const((1, d)), const(wt.shape), const((1, attn_width)), const((1, attn_width))],
        out_specs=[row, pl.BlockSpec((tm, pool_width), lambda i: (i, 0)), half, half, half, half, half,
                   pl.BlockSpec((tm, LANES), lambda i: (i, 0))])[0]


def _shift_down(v, dist, row):
    return jnp.where(row >= dist, pltpu.roll(v, dist, 0), 0.0)


def _shift_up(v, dist, row, n):
    return jnp.where(row + dist < n, pltpu.roll(v, n - dist, 0), 0.0)


def _aug_lane(e):
    return HEAD_DIM if e == 0 else 0


def _forget_prefix(f, bias, qh, kh, n_batch, seq):
    def body(f_ref, b_ref, q_ref, k_ref, qa_ref, ka_ref):
        z = f_ref[...] + b_ref[...]
        acc = jnp.minimum(z, 0.0) - jnp.log(1.0 + jnp.exp(-jnp.abs(z)))
        row = lax.broadcasted_iota(jnp.int32, (seq, 1), 0)
        dist = 1
        while dist < seq:
            acc = acc + _shift_down(acc, dist, row)
            dist *= 2
        lane = lax.broadcasted_iota(jnp.int32, (1, LANES), 1)
        for h in range(N_HEADS):
            pair, e = divmod(h, 2)
            a0 = _aug_lane(e)
            own = (lane < HEAD_DIM) if e == 0 else (lane >= HEAD_DIM)
            fh = _pick_lane(acc, h)
            hi = fh.astype(BF16).astype(F32)
            rest = fh - hi
            mid = rest.astype(BF16).astype(F32)
            low = rest - mid
            q_ones = (lane >= a0 + 3) & (lane < a0 + 6)
            k_ones = (lane >= a0) & (lane < a0 + 3)
            q_aug = jnp.where(lane == a0, hi, jnp.where(lane == a0 + 1, mid, jnp.where(lane == a0 + 2, low,
                              jnp.where(q_ones, 1.0, 0.0))))
            k_aug = jnp.where(k_ones, 1.0, jnp.where(lane == a0 + 3, -hi, jnp.where(lane == a0 + 4, -mid,
                              jnp.where(lane == a0 + 5, -low, 0.0))))
            src = slice(pair * LANES, (pair + 1) * LANES)
            dst = slice(h * LANES, (h + 1) * LANES)
            qa_ref[:, dst] = jnp.where(own, q_ref[:, src].astype(F32), q_aug).astype(BF16)
            ka_ref[:, dst] = jnp.where(own, k_ref[:, src].astype(F32), k_aug).astype(BF16)

    width = qh.shape[1]
    tok = pl.BlockSpec((seq, width), lambda b: (b, 0))
    aug = pl.BlockSpec((seq, N_HEADS * LANES), lambda b: (b, 0))
    return pl.pallas_call(
        body, out_shape=[jax.ShapeDtypeStruct((n_batch * seq, N_HEADS * LANES), BF16)] * 2, grid=(n_batch,),
        in_specs=[pl.BlockSpec((seq, LANES), lambda b: (b, 0)), pl.BlockSpec((1, LANES), lambda b: (0, 0)), tok, tok],
        out_specs=[aug, aug], compiler_params=_params(), name="forget_prefix",
    )(f, bias, qh, kh)


def _pool_groups(pv_ref, pw_ref, ps_ref, seq):
    row = lax.broadcasted_iota(jnp.int32, (seq, 1), 0)
    pos = (row + 1).astype(F32)
    out = []
    for g, win in enumerate(POOL_WINDOWS):
        sl = slice(g * LANES, (g + 1) * LANES)
        xg = pv_ref[:, sl]
        acc = xg
        dist = 1
        while dist < win:
            acc = acc + _shift_down(acc, dist, row)
            dist *= 2
        pooled = (acc / jnp.minimum(pos, float(win)) - xg).astype(BF16)
        mixed = _dot(pooled, pw_ref[g])
        out.append((pooled, mixed, mixed * ps_ref[:, sl]))
    return out


def _pool_fwd(pv, pw, ps, onp, n_batch, seq):
    width = pv.shape[1]

    def body(pv_ref, pw_ref, ps_ref, on_ref, y_ref):
        groups = _pool_groups(pv_ref, pw_ref, ps_ref, seq)
        ssq = sum(jnp.sum(ms * ms, axis=1, keepdims=True) for _, _, ms in groups)
        r = lax.rsqrt(ssq * (1.0 / width) + EPS)
        for g, (_, _, ms) in enumerate(groups):
            sl = slice(g * LANES, (g + 1) * LANES)
            y_ref[:, sl] = ((ms * r) * on_ref[:, sl]).astype(BF16)

    return pl.pallas_call(
        body, out_shape=jax.ShapeDtypeStruct((n_batch * seq, width), BF16), grid=(n_batch,),
        in_specs=[pl.BlockSpec((seq, width), lambda b: (b, 0)), pl.BlockSpec(pw.shape, lambda b: (0, 0, 0)),
                  pl.BlockSpec((1, width), lambda b: (0, 0)), pl.BlockSpec((1, width), lambda b: (0, 0))],
        out_specs=pl.BlockSpec((seq, width), lambda b: (b, 0)),
        compiler_params=_params(), name="pool_fwd",
    )(pv, pw, ps, onp)


def _pool_bwd(pv, dyp, pw, ps, onp, n_batch, seq):
    width = pv.shape[1]

    def body(pv_ref, dy_ref, pw_ref, ps_ref, on_ref, dpv_ref, dpw_ref, dps_ref, don_ref):
        groups = _pool_groups(pv_ref, pw_ref, ps_ref, seq)
        ssq = sum(jnp.sum(ms * ms, axis=1, keepdims=True) for _, _, ms in groups)
        r = lax.rsqrt(ssq * (1.0 / width) + EPS)
        mean = sum(jnp.sum((dy_ref[:, g * LANES:(g + 1) * LANES] * on_ref[:, g * LANES:(g + 1) * LANES]) * (ms * r),
                           axis=1, keepdims=True) for g, (_, _, ms) in enumerate(groups)) * (1.0 / width)
        row = lax.broadcasted_iota(jnp.int32, (seq, 1), 0)
        pos = (row + 1).astype(F32)
        for g, (pooled, mixed, ms) in enumerate(groups):
            sl = slice(g * LANES, (g + 1) * LANES)
            dy = dy_ref[:, sl]
            xh = ms * r
            don_ref[:, sl] = jnp.sum(dy * xh, axis=0, keepdims=True)
            dms = r * (dy * on_ref[:, sl] - xh * mean)
            dps_ref[:, sl] = jnp.sum(dms * mixed, axis=0, keepdims=True)
            dmix = (dms * ps_ref[:, sl]).astype(BF16)
            dpw_ref[g] = _dot_tn(pooled, dmix)
            dpool = _dot_nt(dmix, pw_ref[g])
            win = POOL_WINDOWS[g]
            acc = dpool / jnp.minimum(pos, float(win))
            dist = 1
            while dist < win:
                acc = acc + _shift_up(acc, dist, row, seq)
                dist *= 2
            dpv_ref[:, sl] = (acc - dpool).astype(BF16)

    tok = pl.BlockSpec((seq, width), lambda b: (b, 0))
    vec = pl.BlockSpec((1, width), lambda b: (0, 0))
    pvec = pl.BlockSpec((None, 1, width), lambda b: (b, 0, 0))
    return pl.pallas_call(
        body,
        out_shape=[jax.ShapeDtypeStruct((n_batch * seq, width), BF16),
                   jax.ShapeDtypeStruct((n_batch,) + pw.shape, F32),
                   jax.ShapeDtypeStruct((n_batch, 1, width), F32), jax.ShapeDtypeStruct((n_batch, 1, width), F32)],
        grid=(n_batch,),
        in_specs=[tok, tok, pl.BlockSpec(pw.shape, lambda b: (0, 0, 0)), vec, vec],
        out_specs=[tok, pl.BlockSpec((None,) + pw.shape, lambda b: (b, 0, 0, 0)), pvec, pvec],
        compiler_params=_params(), name="pool_bwd",
    )(pv, dyp, pw, ps, onp)


def _pick_lane(tile, idx):
    lane = lax.broadcasted_iota(jnp.int32, (1, LANES), 1)
    return jnp.sum(jnp.where(lane == idx, tile, 0.0), axis=1, keepdims=True)


def _pick_row(tile, idx):
    sub = lax.broadcasted_iota(jnp.int32, (tile.shape[0], 1), 0)
    return jnp.sum(jnp.where(sub == idx, tile, 0.0), axis=0, keepdims=True)


def _put_lane(col, idx):
    lane = lax.broadcasted_iota(jnp.int32, (1, LANES), 1)
    return jnp.where(lane == idx, col, 0.0)


def _head_select(e):
    lo = _head_masks()
    return lo if e == 0 else jnp.logical_not(lo)


def _causal(st, shift):
    row = lax.broadcasted_iota(jnp.int32, st.shape, 0)
    col = lax.broadcasted_iota(jnp.int32, st.shape, 1) + shift
    return jnp.where(col >= row, st, NEG)


def _transpose_blocks(a):
    rows, cols = a.shape
    return jnp.concatenate(
        [jnp.concatenate([a[r:r + LANES, c:c + LANES].T for r in range(0, rows, LANES)], axis=1)
         for c in range(0, cols, LANES)], axis=0)


def _stat_rows(ref, head, nsub):
    return jnp.concatenate([_pick_row(ref[a], head) for a in range(nsub)], axis=1)


def _accumulate(ref, value, first):
    @pl.when(first)
    def _():
        ref[...] = value

    @pl.when(jnp.logical_not(first))
    def _():
        ref[...] += value


def _attn_fwd(qa, ka, vb, n_batch, seq, plan=None):
    tq = min(ATT_BLOCK, seq)
    nq, nsub, tk = seq // tq, tq // ATT_SUB, tq
    pairs = vb.shape[1] // LANES

    def body(q_ref, k_ref, v_ref, o_ref, lse_ref, acc_ref):
        i, p = pl.program_id(1), pl.program_id(2)
        row_lo = lax.broadcasted_iota(jnp.int32, (LANES, 1), 0) < HEAD_DIM
        qs = [q_ref[:, e * LANES:(e + 1) * LANES] for e in range(2)]
        acc_ref[...] = jnp.zeros_like(acc_ref)

        def tile(off, stats, diagonal):
            vj = v_ref[pl.ds(off, tk), :]
            new, alphas, pvs = [], [], []
            for e in range(2):
                st = _dot_nt(k_ref[pl.ds(off, tk), e * LANES:(e + 1) * LANES], qs[e])
                if diagonal:
                    st = _causal(st, 0)
                m, l = stats[e]
                m_new = jnp.maximum(m, jnp.max(st, axis=0, keepdims=True))
                alpha = jnp.exp(m - m_new)
                pt = jnp.exp(st - m_new)
                new.append((m_new, alpha * l + jnp.sum(pt, axis=0, keepdims=True)))
                alphas.append(alpha)
                pvs.append(_dot_tn(jnp.where(_head_select(e), vj, jnp.zeros_like(vj)), pt.astype(BF16)))
            acc_ref[...] = acc_ref[...] * jnp.where(row_lo, alphas[0], alphas[1]) + (pvs[0] + pvs[1])
            return tuple(new)

        init = ((jnp.full((1, tq), NEG, F32), jnp.zeros((1, tq), F32)),) * 2
        stats = lax.fori_loop(0, i, lambda j, st: tile(pl.multiple_of(j * tk, tk), st, False), init)
        (m0, l0), (m1, l1) = tile(pl.multiple_of(i * tk, tk), stats, True)
        out_t = acc_ref[...] / jnp.where(row_lo, l0, l1)
        sub = lax.broadcasted_iota(jnp.int32, (8, 1), 0)
        lse0, lse1 = m0 + jnp.log(l0), m1 + jnp.log(l1)
        for a in range(nsub):
            sl = slice(a * ATT_SUB, (a + 1) * ATT_SUB)
            o_ref[sl, :] = out_t[:, sl].T
            rows = jnp.where(sub == 2 * p, lse0[:, sl], 0.0) + jnp.where(sub == 2 * p + 1, lse1[:, sl], 0.0)
            _accumulate(lse_ref.at[a], rows, p == 0)

    return _pallas(
        body, name="attn_fwd", args=[qa, ka, vb],
        out_shape=[jax.ShapeDtypeStruct((n_batch * seq, pairs * LANES), F32),
                   jax.ShapeDtypeStruct((n_batch * seq // ATT_SUB, 8, ATT_SUB), F32)],
        grid=(n_batch, nq, pairs),
        in_specs=[pl.BlockSpec((tq, 2 * LANES), lambda b, i, p: (b * nq + i, p)),
                  pl.BlockSpec((seq, 2 * LANES), lambda b, i, p: (b, p)),
                  pl.BlockSpec((seq, LANES), lambda b, i, p: (b, p))],
        out_specs=[pl.BlockSpec((tq, LANES), lambda b, i, p: (b * nq + i, p)),
                   pl.BlockSpec((nsub, 8, ATT_SUB), lambda b, i, p: (b * nq + i, 0, 0))],
        scratch_shapes=[pltpu.VMEM((LANES, tq), F32)], plan=plan)


def _attn_bwd_q(qa, ka, vb, do, lse, delta, n_batch, seq, plan=None):
    tq = min(ATT_BLOCK, seq)
    nq, nsub, tk = seq // tq, tq // ATT_SUB, tq
    pairs = vb.shape[1] // LANES

    def body(q_ref, k_ref, v_ref, do_ref, lse_ref, dl_ref, dq_ref, dfq_ref, acc0_ref, acc1_ref):
        i, p = pl.program_id(1), pl.program_id(2)
        accs = (acc0_ref, acc1_ref)
        qs = [q_ref[:, e * LANES:(e + 1) * LANES] for e in range(2)]
        dov = do_ref[...]
        ls = [_stat_rows(lse_ref, 2 * p + e, nsub) for e in range(2)]
        dl = [_stat_rows(dl_ref, 2 * p + e, nsub) for e in range(2)]
        for acc in accs:
            acc[...] = jnp.zeros_like(acc)

        def tile(off, diagonal):
            vj = v_ref[pl.ds(off, tk), :]
            for e in range(2):
                kj = k_ref[pl.ds(off, tk), e * LANES:(e + 1) * LANES]
                st = _dot_nt(kj, qs[e])
                if diagonal:
                    st = _causal(st, 0)
                pt = jnp.exp(st - ls[e])
                dpt = _dot_nt(jnp.where(_head_select(e), vj, jnp.zeros_like(vj)), dov)
                accs[e][...] += _dot(_transpose_blocks(kj), (pt * (dpt - dl[e])).astype(BF16))

        def step(j, carry):
            tile(pl.multiple_of(j * tk, tk), False)
            return carry

        lax.fori_loop(0, i, step, 0)
        tile(pl.multiple_of(i * tk, tk), True)
        dq0, dq1 = _transpose_blocks(acc0_ref[...]), _transpose_blocks(acc1_ref[...])
        dq_ref[...] = jnp.where(_head_masks(), dq0, dq1)
        dfq = _put_lane(_pick_lane(dq0, _aug_lane(0)), 2 * p) + _put_lane(_pick_lane(dq1, _aug_lane(1)), 2 * p + 1)
        _accumulate(dfq_ref, dfq, p == 0)

    stat = pl.BlockSpec((nsub, 8, ATT_SUB), lambda b, i, p: (b * nq + i, 0, 0))
    blk = pl.BlockSpec((tq, LANES), lambda b, i, p: (b * nq + i, p))
    return _pallas(
        body, name="attn_bwd_q", args=[qa, ka, vb, do, lse, delta],
        out_shape=[jax.ShapeDtypeStruct((n_batch * seq, pairs * LANES), F32), jax.ShapeDtypeStruct((n_batch * seq, LANES), F32)],
        grid=(n_batch, nq, pairs),
        in_specs=[pl.BlockSpec((tq, 2 * LANES), lambda b, i, p: (b * nq + i, p)),
                  pl.BlockSpec((seq, 2 * LANES), lambda b, i, p: (b, p)),
                  pl.BlockSpec((seq, LANES), lambda b, i, p: (b, p)), blk, stat, stat],
        out_specs=[blk, pl.BlockSpec((tq, LANES), lambda b, i, p: (b * nq + i, 0))],
        scratch_shapes=[pltpu.VMEM((LANES, tq), F32), pltpu.VMEM((LANES, tq), F32)], plan=plan)


def _attn_bwd_kv(qa, ka, vb, do, lse, delta, n_batch, seq, plan=None):
    tkb = min(ATT_BLOCK, seq)
    nk, nsub, tq = seq // tkb, tkb // ATT_SUB, tkb
    n_tiles = seq // ATT_SUB
    pairs = vb.shape[1] // LANES

    def body(q_ref, k_ref, v_ref, do_ref, lse_ref, dl_ref, dk_ref, dv_ref, dfk_ref, dk0_ref, dk1_ref, dva_ref):
        j, p = pl.program_id(1), pl.program_id(2)
        dks = (dk0_ref, dk1_ref)
        ks = [k_ref[:, e * LANES:(e + 1) * LANES] for e in range(2)]
        vj = v_ref[...]
        vs = [jnp.where(_head_select(e), vj, jnp.zeros_like(vj)) for e in range(2)]
        for acc in (dk0_ref, dk1_ref, dva_ref):
            acc[...] = jnp.zeros_like(acc)

        def tile(t, diagonal):
            off = pl.multiple_of(t * tq, tq)
            dov = do_ref[pl.ds(off, tq), :]
            for e in range(2):
                qe = q_ref[pl.ds(off, tq), e * LANES:(e + 1) * LANES]
                st = _dot_nt(ks[e], qe)
                if diagonal:
                    st = _causal(st, 0)
                rows = lambda ref: jnp.concatenate([_pick_row(ref[t * nsub + a], 2 * p + e) for a in range(nsub)], axis=1)
                pt = jnp.exp(st - rows(lse_ref))
                dva_ref[...] += _dot(pt.astype(BF16), jnp.where(_head_select(e), dov, jnp.zeros_like(dov)))
                dst = pt * (_dot_nt(vs[e], dov) - rows(dl_ref))
                dks[e][...] += _dot(dst.astype(BF16), qe)

        def step(t, carry):
            tile(t, False)
            return carry

        lax.fori_loop(j + 1, nk, step, 0)
        tile(j, True)
        dk0, dk1 = dk0_ref[...], dk1_ref[...]
        dk_ref[...] = jnp.where(_head_masks(), dk0, dk1)
        dv_ref[...] = dva_ref[...].astype(BF16)
        dfk = (_put_lane(_pick_lane(dk0, _aug_lane(0) + 3), 2 * p)
               + _put_lane(_pick_lane(dk1, _aug_lane(1) + 3), 2 * p + 1))
        _accumulate(dfk_ref, -dfk, p == 0)

    stat = pl.BlockSpec((n_tiles, 8, ATT_SUB), lambda b, j, p: (b, 0, 0))
    blk = pl.BlockSpec((tkb, LANES), lambda b, j, p: (b * nk + j, p))
    acc = pltpu.VMEM((tkb, LANES), F32)
    return _pallas(
        body, name="attn_bwd_kv", args=[qa, ka, vb, do, lse, delta],
        out_shape=[jax.ShapeDtypeStruct((n_batch * seq, pairs * LANES), F32),
                   jax.ShapeDtypeStruct((n_batch * seq, pairs * LANES), BF16),
                   jax.ShapeDtypeStruct((n_batch * seq, LANES), F32)],
        grid=(n_batch, nk, pairs),
        in_specs=[pl.BlockSpec((seq, 2 * LANES), lambda b, j, p: (b, p)),
                  pl.BlockSpec((tkb, 2 * LANES), lambda b, j, p: (b * nk + j, p)), blk,
                  pl.BlockSpec((seq, LANES), lambda b, j, p: (b, p)), stat, stat],
        out_specs=[blk, blk, pl.BlockSpec((tkb, LANES), lambda b, j, p: (b * nk + j, 0))],
        scratch_shapes=[acc, acc, acc], plan=plan)


def _attn_bwd(qa, ka, vb, do, lse, delta, n_batch, seq, plan=None):
    tq = min(ATT_BLOCK, seq)
    nq, nsub = seq // tq, tq // ATT_SUB
    n_tiles = seq // ATT_SUB
    pairs = vb.shape[1] // LANES

    def body(q_ref, k_ref, v_ref, do_ref, lse_ref, dl_ref, dq_ref, dk_ref, dv_ref, dfq_ref, dfk_ref,
             dq0_ref, dq1_ref, dk0_ref, dk1_ref, dva_ref):
        p = pl.program_id(1)
        dqs, dks = (dq0_ref, dq1_ref), (dk0_ref, dk1_ref)
        for acc in (dk0_ref, dk1_ref, dva_ref):
            acc[...] = jnp.zeros_like(acc)
        dfq_cols = []
        for i in range(nq):
            rows_i = slice(i * tq, (i + 1) * tq)
            qs = [q_ref[rows_i, e * LANES:(e + 1) * LANES] for e in range(2)]
            dov = do_ref[rows_i, :]
            does = [jnp.where(_head_select(e), dov, jnp.zeros_like(dov)) for e in range(2)]
            stat = lambda ref, e: jnp.concatenate([_pick_row(ref[i * nsub + a], 2 * p + e) for a in range(nsub)], axis=1)
            ls, dl = [stat(lse_ref, e) for e in range(2)], [stat(dl_ref, e) for e in range(2)]
            for acc in dqs:
                acc[...] = jnp.zeros_like(acc)

            def tile(off, diagonal, qs=qs, dov=dov, does=does, ls=ls, dl=dl):
                vj = v_ref[pl.ds(off, tq), :]
                for e in range(2):
                    kj = k_ref[pl.ds(off, tq), e * LANES:(e + 1) * LANES]
                    st = _dot_nt(kj, qs[e])
                    if diagonal:
                        st = _causal(st, 0)
                    pt = jnp.exp(st - ls[e])
                    dva_ref[pl.ds(off, tq), :] += _dot(pt.astype(BF16), does[e])
                    dpt = _dot_nt(jnp.where(_head_select(e), vj, jnp.zeros_like(vj)), dov)
                    dst = (pt * (dpt - dl[e])).astype(BF16)
                    dks[e][pl.ds(off, tq), :] += _dot(dst, qs[e])
                    dqs[e][...] += _dot(_transpose_blocks(kj), dst)

            def step(j, carry, tile=tile):
                tile(pl.multiple_of(j * tq, tq), False)
                return carry

            lax.fori_loop(0, i, step, 0)
            tile(i * tq, True)
            dq0, dq1 = _transpose_blocks(dq0_ref[...]), _transpose_blocks(dq1_ref[...])
            dq_ref[rows_i, :] = jnp.where(_head_masks(), dq0, dq1)
            dfq_cols.append(_put_lane(_pick_lane(dq0, _aug_lane(0)), 2 * p) + _put_lane(_pick_lane(dq1, _aug_lane(1)), 2 * p + 1))
        dk0, dk1 = dk0_ref[...], dk1_ref[...]
        dk_ref[...] = jnp.where(_head_masks(), dk0, dk1)
        dv_ref[...] = dva_ref[...].astype(BF16)
        dfk = _put_lane(_pick_lane(dk0, _aug_lane(0) + 3), 2 * p) + _put_lane(_pick_lane(dk1, _aug_lane(1) + 3), 2 * p + 1)
        _accumulate(dfq_ref, jnp.concatenate(dfq_cols, axis=0), p == 0)
        _accumulate(dfk_ref, -dfk, p == 0)

    wide = pl.BlockSpec((seq, 2 * LANES), lambda b, p: (b, p))
    blk = pl.BlockSpec((seq, LANES), lambda b, p: (b, p))
    col = pl.BlockSpec((seq, LANES), lambda b, p: (b, 0))
    stat = pl.BlockSpec((n_tiles, 8, ATT_SUB), lambda b, p: (b, 0, 0))
    f32_blk, acc = jax.ShapeDtypeStruct((n_batch * seq, pairs * LANES), F32), pltpu.VMEM((seq, LANES), F32)
    return _pallas(
        body, name="attn_bwd", args=[qa, ka, vb, do, lse, delta],
        out_shape=[f32_blk, f32_blk, jax.ShapeDtypeStruct((n_batch * seq, pairs * LANES), BF16),
                   jax.ShapeDtypeStruct((n_batch * seq, LANES), F32), jax.ShapeDtypeStruct((n_batch * seq, LANES), F32)],
        grid=(n_batch, pairs), in_specs=[wide, wide, blk, blk, stat, stat], out_specs=[blk, blk, blk, col, col],
        scratch_shapes=[pltpu.VMEM((LANES, tq), F32), pltpu.VMEM((LANES, tq), F32), acc, acc, acc], plan=plan)


def _forget_bwd(dfq, dfk, f, bias, n_batch, seq):
    def body(dfq_ref, dfk_ref, f_ref, b_ref, df_ref, db_ref):
        acc = dfq_ref[...] + dfk_ref[...]
        row = lax.broadcasted_iota(jnp.int32, (seq, 1), 0)
        dist = 1
        while dist < seq:
            acc = acc + _shift_up(acc, dist, row, seq)
            dist *= 2
        df = acc * _sigmoid(-(f_ref[...] + b_ref[...]))
        df_ref[...] = df
        db_ref[...] = jnp.sum(df, axis=0, keepdims=True)

    col = pl.BlockSpec((seq, LANES), lambda b: (b, 0))
    return pl.pallas_call(
        body,
        out_shape=[jax.ShapeDtypeStruct((n_batch * seq, LANES), F32), jax.ShapeDtypeStruct((n_batch, 1, LANES), F32)],
        grid=(n_batch,), in_specs=[col, col, col, pl.BlockSpec((1, LANES), lambda b: (0, 0))],
        out_specs=[col, pl.BlockSpec((None, 1, LANES), lambda b: (b, 0, 0))],
        compiler_params=_params(), name="forget_bwd",
    )(dfq, dfk, f, bias)


def _mix_out(x1, yp, o, ona, woa, wob):
    t, d = x1.shape
    width = o.shape[1]
    tm = min(512, t)

    def body(x_ref, yp_ref, o_ref, on_ref, wa_ref, wb_ref, x2_ref, ya_ref):
        of = o_ref[...]
        ya = ((of * _rms(of)) * on_ref[...]).astype(BF16)
        ya_ref[...] = ya
        x2_ref[...] = x_ref[...] + (_dot(yp_ref[...], wa_ref[...]) + _dot(ya, wb_ref[...]))

    row = pl.BlockSpec((tm, d), lambda i: (i, 0))
    half = pl.BlockSpec((tm, width), lambda i: (i, 0))
    wspec = pl.BlockSpec((width, d), lambda i: (0, 0))
    return pl.pallas_call(
        body, out_shape=[jax.ShapeDtypeStruct((t, d), F32), jax.ShapeDtypeStruct((t, width), BF16)],
        grid=(t // tm,), in_specs=[row, half, half, pl.BlockSpec((1, width), lambda i: (0, 0)), wspec, wspec],
        out_specs=[row, half], compiler_params=_params(), name="mix_out",
    )(x1, yp, o, ona, woa, wob)


def _mix_out_bwd(dx2, o, yp, ya, ona, woa, wob, plan=None):
    t, d = dx2.shape
    width = o.shape[1]
    tm = min(512, t)
    nt = t // tm

    def body(dx_ref, o_ref, yp_ref, ya_ref, on_ref, wa_ref, wb_ref, dyp_ref, do_ref, dl_ref, dwa_ref, dwb_ref, don_ref):
        @pl.when(pl.program_id(0) == 0)
        def _():
            dwa_ref[...] = jnp.zeros_like(dwa_ref)
            dwb_ref[...] = jnp.zeros_like(dwb_ref)

        dxb = dx_ref[...].astype(BF16)
        dwa_ref[...] += _dot_tn(yp_ref[...], dxb)
        dwb_ref[...] += _dot_tn(ya_ref[...], dxb)
        dyp_ref[...] = _dot_nt(dxb, wa_ref[...])
        of = o_ref[...]
        dov, dgr = _rms_bwd(of, _rms(of), on_ref[...], _dot_nt(dxb, wb_ref[...]))
        don_ref[...] = jnp.sum(dgr, axis=0, keepdims=True)
        do_ref[...] = dov.astype(BF16)
        lo = _head_masks()
        prod = dov * of
        delta = jnp.zeros((tm, LANES), F32)
        for blk in range(width // LANES):
            pb = prod[:, blk * LANES:(blk + 1) * LANES]
            delta = delta + _put_lane(jnp.sum(jnp.where(lo, pb, 0.0), axis=1, keepdims=True), 2 * blk)
            delta = delta + _put_lane(jnp.sum(jnp.where(lo, 0.0, pb), axis=1, keepdims=True), 2 * blk + 1)
        for c in range(tm // ATT_SUB):
            dl_ref[c] = delta[c * ATT_SUB:(c + 1) * ATT_SUB, :].T[0:8, :]

    row = pl.BlockSpec((tm, d), lambda i: (i, 0))
    half = pl.BlockSpec((tm, width), lambda i: (i, 0))
    wspec = pl.BlockSpec((width, d), lambda i: (0, 0))
    return _pallas(
        body, name="mix_out_bwd", args=[dx2, o, yp, ya, ona, woa, wob],
        out_shape=[jax.ShapeDtypeStruct((t, width), F32), jax.ShapeDtypeStruct((t, width), BF16),
                   jax.ShapeDtypeStruct((t // ATT_SUB, 8, ATT_SUB), F32), jax.ShapeDtypeStruct((width, d), F32),
                   jax.ShapeDtypeStruct((width, d), F32), jax.ShapeDtypeStruct((nt, 1, width), F32)],
        grid=(nt,),
        in_specs=[row, half, half, half, pl.BlockSpec((1, width), lambda i: (0, 0)), wspec, wspec],
        out_specs=[half, half, pl.BlockSpec((tm // ATT_SUB, 8, ATT_SUB), lambda i: (i, 0, 0)), wspec, wspec,
                   pl.BlockSpec((None, 1, width), lambda i: (i, 0, 0))], plan=plan)


def _mix_in_bwd(dx2, x1, gain, hm, dpv, dqh, q, dkh, k, dv, df, qn, kn, wt):
    t, d = x1.shape
    width = q.shape[1]
    pool_width = dpv.shape[1]
    tm = min(512, t)
    nt = t // tm
    scale = HEAD_DIM ** -0.5
    c_q, c_k, c_v = pool_width, pool_width + width, pool_width + 2 * width
    c_f = c_v + width

    def body(dx2_ref, x_ref, g_ref, hm_ref, dpv_ref, dqh_ref, q_ref, dkh_ref, k_ref, dv_ref, df_ref, qn_ref, kn_ref,
             wt_ref, dx_ref, dxh_ref, dwt_ref, dg_ref, dqn_ref, dkn_ref):
        @pl.when(pl.program_id(0) == 0)
        def _():
            dwt_ref[...] = jnp.zeros_like(dwt_ref)

        lo = _head_masks()
        for part, rows in enumerate(_row_halves(tm)):
            def put(ref, sl, value):
                ref[:, sl] = value if part == 0 else ref[:, sl] + value

            hm = hm_ref[rows, :]
            pieces = [(0, dpv_ref[rows, :])]
            for c0, raw_ref, dh_ref, n_ref, dn_ref, mul in ((c_q, q_ref, dqh_ref, qn_ref, dqn_ref, scale),
                                                           (c_k, k_ref, dkh_ref, kn_ref, dkn_ref, 1.0)):
                cols = []
                for blk in range(width // LANES):
                    sl = slice(blk * LANES, (blk + 1) * LANES)
                    xb = raw_ref[rows, sl]
                    gb = dh_ref[rows, sl] * mul
                    r = _head_rms(xb, lo)
                    xh = xb * r
                    dyg = gb * n_ref[:, sl]
                    cols.append((r * (dyg - xh * _head_mean(dyg * xh, lo))).astype(BF16))
                    put(dn_ref, sl, jnp.sum(gb * xh, axis=0, keepdims=True))
                pieces.append((c0, jnp.concatenate(cols, axis=1)))
            pieces.append((c_v, dv_ref[rows, :]))
            pieces.append((c_f, df_ref[rows, :].astype(BF16)))
            dhm = jnp.zeros((tm // 2, d), F32)
            for c0, piece in pieces:
                dwt_ref[c0:c0 + piece.shape[1], :] += _dot_tn(piece, hm)
                dhm = dhm + _dot(piece, wt_ref[c0:c0 + piece.shape[1], :])
            xf = x_ref[rows, :]
            dxn, dgr = _rms_bwd(xf, _rms(xf), g_ref[...], dhm)
            dx = dx2_ref[rows, :] + dxn
            dx_ref[rows, :] = dx
            dxh_ref[rows, :] = (0.5 * dx).astype(BF16)
            put(dg_ref, slice(None), jnp.sum(dgr, axis=0, keepdims=True))

    row = pl.BlockSpec((tm, d), lambda i: (i, 0))
    half = pl.BlockSpec((tm, width), lambda i: (i, 0))
    const = lambda shape: pl.BlockSpec(shape, lambda i: (0, 0))
    pvec = lambda n: pl.BlockSpec((None, 1, n), lambda i: (i, 0, 0))
    return pl.pallas_call(
        body,
        out_shape=[jax.ShapeDtypeStruct((t, d), F32), jax.ShapeDtypeStruct((t, d), BF16), jax.ShapeDtypeStruct(wt.shape, F32),
                   jax.ShapeDtypeStruct((nt, 1, d), F32),
                   jax.ShapeDtypeStruct((nt, 1, width), F32), jax.ShapeDtypeStruct((nt, 1, width), F32)],
        grid=(nt,),
        in_specs=[row, row, const((1, d)), row, pl.BlockSpec((tm, pool_width), lambda i: (i, 0)), half, half, half, half,
                  half, pl.BlockSpec((tm, LANES), lambda i: (i, 0)), const((1, width)), const((1, width)),
                  const(wt.shape)],
        out_specs=[row, row, const(wt.shape), pvec(d), pvec(width), pvec(width)],
        compiler_params=_params(), name="mix_in_bwd",
    )(dx2, x1, gain, hm, dpv, dqh, q, dkh, k, dv, df, qn, kn, wt)


def _mesh_pos():
    return lax.axis_index("x"), lax.axis_index("y"), lax.axis_index("c")


def _other_chips(x, y):
    return [(1 - x, y), (x, 1 - y), (1 - x, 1 - y)]


def _remote(src, dst, send_sem, recv_sem, device):
    return pltpu.make_async_remote_copy(src_ref=src, dst_ref=dst, send_sem=send_sem, recv_sem=recv_sem,
                                        device_id=device, device_id_type=pl.DeviceIdType.MESH)


def _half_rows(n_rows, which):
    half = n_rows // 2
    return pl.ds(pl.multiple_of(which * half, 8), half)


def _row_block(rows, cols, itemsize=4):
    rb = rows
    while rb * cols * itemsize > (1 << 20) and rb % 32 == 0:
        rb //= 2
    return rb


def _place_cast(ws, chip, tag):
    n = len(ws)
    rows, cols = ws[0].shape
    rb = _row_block(rows, cols)

    def body(k_ref, *refs):
        for w_ref, o_ref in zip(refs[:n], refs[n:]):
            o_ref[...] = w_ref[...].astype(BF16)

    return pl.pallas_call(
        body, out_shape=[jax.ShapeDtypeStruct((N_CHIPS, rows, cols), BF16)] * n,
        grid_spec=pltpu.PrefetchScalarGridSpec(
            num_scalar_prefetch=1, grid=(rows // rb,),
            in_specs=[pl.BlockSpec((rb, cols), lambda i, k: (i, 0))] * n,
            out_specs=[pl.BlockSpec((None, rb, cols), lambda i, k: (k[0], i, 0))] * n),
        compiler_params=_params(), name="place_" + tag,
    )(chip, *ws)


class _Plan:
    def __init__(self, ins, outs, alias, sems, start, finish, middle=None):
        self.ins, self.outs, self.alias, self.sems = ins, outs, alias, sems
        self.start, self.middle, self.finish = start, middle, finish


def _merge_plans(a, b):
    ni, no, ns = len(a.ins), len(a.outs), len(a.sems)
    alias = dict(a.alias)
    alias.update({ni + i: no + o for i, o in b.alias.items()})

    def both(which):
        stage_a, stage_b = getattr(a, which), getattr(b, which)
        if stage_a is None and stage_b is None:
            return None

        def run(ins, outs, sems):
            if stage_a is not None:
                stage_a(ins[:ni], outs[:no], sems[:ns])
            if stage_b is not None:
                stage_b(ins[ni:], outs[no:], sems[ns:])
        return run

    return _Plan(list(a.ins) + list(b.ins), list(a.outs) + list(b.outs), alias, list(a.sems) + list(b.sems),
                 both("start"), both("finish"), both("middle"))


def _run_plan(plan, name):
    n_in, n_out = len(plan.ins), len(plan.outs)

    def body(*refs):
        parts = refs[:n_in], refs[n_in:n_in + n_out], refs[n_in + n_out:]
        plan.start(*parts)
        if plan.middle is not None:
            plan.middle(*parts)
        plan.finish(*parts)

    return pl.pallas_call(
        body, out_shape=plan.outs, in_specs=[ANY] * n_in, out_specs=[ANY] * n_out, scratch_shapes=plan.sems,
        input_output_aliases=plan.alias, name=name,
    )(*plan.ins)


def _pallas(body, *, name, args, in_specs, out_shape, out_specs, grid, scratch_shapes=(), plan=None, aliases=None):
    n_in, n_out, n_scr = len(args), len(out_shape), len(scratch_shapes)
    plan = plan or _Plan([], [], {}, [], None, None)
    p_in, p_out = len(plan.ins), len(plan.outs)

    def carrying(*refs):
        ins, p_ins = refs[:n_in], refs[n_in:n_in + p_in]
        o0 = n_in + p_in
        outs, p_outs = refs[o0:o0 + n_out], refs[o0 + n_out:o0 + n_out + p_out]
        s0 = o0 + n_out + p_out
        scr, p_sems = refs[s0:s0 + n_scr], refs[s0 + n_scr:]
        ids = [pl.program_id(a) for a in range(len(grid))]

        if plan.start is not None:
            @pl.when(functools.reduce(jnp.logical_and, [i == 0 for i in ids]))
            def _():
                plan.start(p_ins, p_outs, p_sems)

        body(*ins, *outs, *scr)

        if plan.middle is not None:
            step, n_steps = 0, 1
            for i, g in zip(ids, grid):
                step, n_steps = step * g + i, n_steps * g

            @pl.when(step == (3 * n_steps) // 4)
            def _():
                plan.middle(p_ins, p_outs, p_sems)

        if plan.finish is not None:
            @pl.when(functools.reduce(jnp.logical_and, [i == g - 1 for i, g in zip(ids, grid)]))
            def _():
                plan.finish(p_ins, p_outs, p_sems)

    aliases = dict(aliases or {})
    aliases.update({n_in + i: n_out + o for i, o in plan.alias.items()})
    res = pl.pallas_call(
        carrying, out_shape=list(out_shape) + list(plan.outs), grid=grid,
        in_specs=list(in_specs) + [ANY] * p_in, out_specs=list(out_specs) + [ANY] * p_out,
        scratch_shapes=list(scratch_shapes) + list(plan.sems),
        input_output_aliases=aliases, compiler_params=_params(), name=name,
    )(*args, *plan.ins)
    return list(res[:n_out]), list(res[n_out:])


def _plan_gather(stacks):
    n = len(stacks)
    relations = range(3)

    def ici_copies(outs, sems):
        x, y, c = _mesh_pos()
        chips = _other_chips(x, y)
        cps = []
        for w in range(n):
            own = outs[w].at[2 * x + y, _half_rows(stacks[w].shape[1], c)]
            cps += [_remote(own, own, sems[0].at[w, j], sems[1].at[w, j], (*chips[j], c)) for j in relations]
        return cps

    def start(ins, outs, sems):
        for cp in ici_copies(outs, sems):
            cp.start()

    def forwards(outs, sems, core):
        x, y, c = _mesh_pos()
        slots = [2 * cx + cy for cx, cy in _other_chips(x, y)]
        cps = []
        for w in range(n):
            rows = _half_rows(stacks[w].shape[1], core)
            for j in relations:
                landed = outs[w].at[slots[j], rows]
                cps.append((_remote(landed, landed, sems[0].at[w, j], sems[1].at[w, j], (x, y, 1 - c)),
                            _remote(landed, landed, sems[2].at[w, j], sems[3].at[w, j], (x, y, 1 - c))))
        return cps

    def middle(ins, outs, sems):
        c = _mesh_pos()[2]
        for arrival, forward in forwards(outs, sems, c):
            arrival.wait_recv()
            forward.start()

    def finish(ins, outs, sems):
        c = _mesh_pos()[2]
        for _, forward in forwards(outs, sems, 1 - c):
            forward.wait_recv()
        for cp in ici_copies(outs, sems) + [forward for _, forward in forwards(outs, sems, c)]:
            cp.wait_send()

    return _Plan(stacks, [jax.ShapeDtypeStruct(s.shape, s.dtype) for s in stacks], {w: w for w in range(n)},
                 [pltpu.SemaphoreType.DMA((n, 3))] * 4, start, finish, middle)


def _plan_gather_relay(stacks):
    n = len(stacks)

    def finish(ins, outs, sems):
        send, recv, relay_send, relay_recv, d2d_send, d2d_recv = sems
        x, y, c = _mesh_pos()
        sibling = (x, y, 1 - c)
        near = [(1 - x, y), (x, 1 - y)]
        far = 2 * (1 - x) + (1 - y)
        started = []

        def go(cp):
            cp.start()
            started.append(cp)

        def piece(w, slot, core, quarter=None):
            rh = stacks[w].shape[1] // 2
            if quarter is None:
                return outs[w].at[slot, _half_rows(2 * rh, core)]
            return outs[w].at[slot, pl.ds(pl.multiple_of(core * rh + quarter * (rh // 2), 8), rh // 2)]

        for w in range(n):
            own = piece(w, 2 * x + y, c)
            for j, chip in enumerate(near):
                go(_remote(own, own, send.at[w, j], recv.at[w, j], (*chip, c)))
        for w in range(n):
            for j, (cx, cy) in enumerate(near):
                landed = piece(w, 2 * cx + cy, c)
                _remote(landed, landed, send.at[w, j], recv.at[w, j], sibling).wait_recv()
                part = piece(w, 2 * cx + cy, c, quarter=j)
                go(_remote(part, part, relay_send.at[w, j], relay_recv.at[w, j], (*near[1 - j], c)))
                go(_remote(landed, landed, d2d_send.at[w, j], d2d_recv.at[w, j], sibling))
        for w in range(n):
            for j in range(2):
                part = piece(w, far, c, quarter=j)
                _remote(part, part, relay_send.at[w, j], relay_recv.at[w, j], sibling).wait_recv()
            landed = piece(w, far, c)
            go(_remote(landed, landed, d2d_send.at[w, 2], d2d_recv.at[w, 2], sibling))
        for w in range(n):
            for j, slot in enumerate([2 * cx + cy for cx, cy in near] + [far]):
                landed = piece(w, slot, 1 - c)
                _remote(landed, landed, d2d_send.at[w, j], d2d_recv.at[w, j], sibling).wait_recv()
        for cp in started:
            cp.wait_send()

    return _Plan(stacks, [jax.ShapeDtypeStruct(s.shape, s.dtype) for s in stacks], {w: w for w in range(n)},
                 [pltpu.SemaphoreType.DMA((n, 2))] * 4 + [pltpu.SemaphoreType.DMA((n, 3))] * 2,
                 lambda ins, outs, sems: None, finish)


def _plan_sibling_halves(gs):
    n = len(gs)

    def copies(ins, outs, sems):
        x, y, c = _mesh_pos()
        return [_remote(ins[w].at[:, _half_rows(gs[w].shape[1], 1 - c), :], outs[w], sems[0].at[w], sems[1].at[w],
                        (x, y, 1 - c)) for w in range(n)]

    def start(ins, outs, sems):
        for cp in copies(ins, outs, sems):
            cp.start()

    def finish(ins, outs, sems):
        for cp in copies(ins, outs, sems):
            cp.wait()

    return _Plan(gs, [jax.ShapeDtypeStruct((g.shape[0], g.shape[1] // 2, g.shape[2]), g.dtype) for g in gs], {},
                 [pltpu.SemaphoreType.DMA((n,))] * 2, start, finish)


def _plan_chip_exchange(ps):
    n = len(ps)

    def copies(ins, outs, sems):
        x, y, c = _mesh_pos()
        return [_remote(ins[w].at[2 * cx + cy], outs[w].at[j], sems[0].at[w, j], sems[1].at[w, j], (cx, cy, c))
                for w in range(n) for j, (cx, cy) in enumerate(_other_chips(x, y))]

    def start(ins, outs, sems):
        for cp in copies(ins, outs, sems):
            cp.start()

    def finish(ins, outs, sems):
        for cp in copies(ins, outs, sems):
            cp.wait()

    return _Plan(ps, [jax.ShapeDtypeStruct((3,) + p.shape[1:], p.dtype) for p in ps], {},
                 [pltpu.SemaphoreType.DMA((n, 3))] * 2, start, finish)


def _plan_sibling_share(gs):
    n = len(gs)

    def copies(outs, sems, which):
        x, y, c = _mesh_pos()
        cps = []
        for w in range(n):
            rows = outs[w].at[_half_rows(gs[w].shape[0], c if which == "mine" else 1 - c)]
            cps.append(_remote(rows, rows, sems[0].at[w], sems[1].at[w], (x, y, 1 - c)))
        return cps

    def start(ins, outs, sems):
        for cp in copies(outs, sems, "mine"):
            cp.start()

    def finish(ins, outs, sems):
        for cp in copies(outs, sems, "mine"):
            cp.wait_send()
        for cp in copies(outs, sems, "theirs"):
            cp.wait_recv()

    return _Plan(gs, [jax.ShapeDtypeStruct(g.shape, g.dtype) for g in gs], {w: w for w in range(n)},
                 [pltpu.SemaphoreType.DMA((n,))] * 2, start, finish)


def _same_shape_groups(arrays):
    groups = {}
    for i, a in enumerate(arrays):
        groups.setdefault(a.shape, []).append(i)
    return list(groups.values())


def _add_sibling(gs, r1s, ids, tag):
    n = len(gs)
    nch, rh, cols = r1s[0].shape

    def body(ids_ref, *refs):
        for g_ref, r_ref, o_ref in zip(refs[:n], refs[n:2 * n], refs[2 * n:]):
            o_ref[...] = (g_ref[...] + r_ref[...]).astype(BF16)

    blk = lambda fn: pl.BlockSpec((None, rh, cols), fn)
    return pl.pallas_call(
        body, out_shape=[jax.ShapeDtypeStruct(r1s[0].shape, BF16)] * n,
        grid_spec=pltpu.PrefetchScalarGridSpec(
            num_scalar_prefetch=1, grid=(nch,),
            in_specs=[blk(lambda k, ids: (k, ids[1], 0))] * n + [blk(lambda k, ids: (k, 0, 0))] * n,
            out_specs=[blk(lambda k, ids: (k, 0, 0))] * n),
        compiler_params=_params(), name="add_sibling_" + tag,
    )(ids, *gs, *r1s)


def _add_chips(gs, r1s, r2s, ids, tag):
    n = len(gs)
    _, rh, cols = r1s[0].shape
    nb = 2 if rh % 32 == 0 else 1
    rb = rh // nb

    def body(ids_ref, *refs):
        for g_ref, r1_ref, r2_ref, o_ref in zip(refs[:n], refs[n:2 * n], refs[2 * n:3 * n], refs[3 * n:]):
            own = g_ref[...] + r1_ref[...]
            o_ref[...] = ((own + r2_ref[0].astype(F32)) + r2_ref[1].astype(F32)) + r2_ref[2].astype(F32)

    return pl.pallas_call(
        body, out_shape=[jax.ShapeDtypeStruct((2 * rh, cols), F32)] * n,
        grid_spec=pltpu.PrefetchScalarGridSpec(
            num_scalar_prefetch=1, grid=(nb,),
            in_specs=[pl.BlockSpec((None, rb, cols), lambda i, ids: (ids[0], ids[1] * nb + i, 0))] * n
            + [pl.BlockSpec((None, rb, cols), lambda i, ids: (ids[0], i, 0))] * n
            + [pl.BlockSpec((3, rb, cols), lambda i, ids: (0, i, 0))] * n,
            out_specs=[pl.BlockSpec((rb, cols), lambda i, ids: (ids[1] * nb + i, 0))] * n),
        compiler_params=_params(), name="add_chips_" + tag,
    )(ids, *gs, *r1s, *r2s)


VEC_ROWS = 8


N_DEVICES = 8


def _small_pack(part, d, width):
    names = ("ffn1_norm", "mix_norm", "ffn2_norm", "pool_scale", "out_norm_pool", "out_norm_attn", "qn", "kn", "b_forget",
             "pool_w", "loss")
    args = [part[k] for k in names]
    pw_shape = part["pool_w"].shape[1:]

    def body(g1_ref, gm_ref, g2_ref, ps_ref, onp_ref, ona_ref, qn_ref, kn_ref, bf_ref, pw_ref, loss_ref, vbuf, pbuf):
        lo = _head_masks()

        def fold_heads(ref):
            v = jnp.sum(ref[...], axis=0)
            acc = jnp.zeros((VEC_ROWS, LANES), F32)
            for blk in range(width // LANES):
                vb = jnp.broadcast_to(v[:, blk * LANES:(blk + 1) * LANES], (VEC_ROWS, LANES))
                acc = acc + vb + pltpu.roll(vb, HEAD_DIM, 1)
            return jnp.where(lo, acc, 0.0)[0:1, :]

        vbuf[0] = jnp.zeros((VEC_ROWS, d), F32)
        vbuf[0, 0:1, :] = jnp.sum(g1_ref[...], axis=0)
        vbuf[0, 1:2, :] = jnp.sum(gm_ref[...], axis=0)
        vbuf[0, 2:3, :] = jnp.sum(g2_ref[...], axis=0)
        vbuf[0, 5:6, 0:LANES] = jnp.sum(loss_ref[...], axis=0)[0:1, :]
        vbuf[0, 3:4, 0:width] = jnp.sum(ps_ref[...], axis=0)
        vbuf[0, 3:4, width:2 * width] = jnp.sum(onp_ref[...], axis=0)
        vbuf[0, 4:5, 0:width] = jnp.sum(ona_ref[...], axis=0)
        vbuf[0, 4:5, width:width + LANES] = fold_heads(qn_ref)
        vbuf[0, 4:5, width + LANES:width + 2 * LANES] = fold_heads(kn_ref)
        vbuf[0, 4:5, width + 2 * LANES:width + 3 * LANES] = jnp.sum(bf_ref[...], axis=0)
        pbuf[0] = jnp.sum(pw_ref[...], axis=0)

    return pl.pallas_call(
        body, out_shape=[jax.ShapeDtypeStruct((N_DEVICES, VEC_ROWS, d), F32), jax.ShapeDtypeStruct((N_DEVICES,) + pw_shape, F32)],
        in_specs=[VM] * len(args), out_specs=[VM, VM], compiler_params=_params(), name="small_pack",
    )(*args)


def _plan_all_to_all(stacks):
    n = len(stacks)

    def copies(outs, sems):
        x, y, c = _mesh_pos()
        cps = []
        for r in range(1, N_DEVICES):
            peer = (x if not r & 4 else 1 - x, y if not r & 2 else 1 - y, c if not r & 1 else 1 - c)
            cps += [_remote(outs[w].at[0], outs[w].at[r], sems[0].at[w, r - 1], sems[1].at[w, r - 1], peer) for w in range(n)]
        return cps

    def start(ins, outs, sems):
        for cp in copies(outs, sems):
            cp.start()

    def finish(ins, outs, sems):
        for cp in copies(outs, sems):
            cp.wait()

    return _Plan(stacks, [jax.ShapeDtypeStruct(s.shape, s.dtype) for s in stacks], {w: w for w in range(n)},
                 [pltpu.SemaphoreType.DMA((n, N_DEVICES - 1))] * 2, start, finish)


def _small_sum(vstack, pstack, me):
    def body(me_ref, vbuf, pbuf, vec_ref, pw_ref):
        vec = vbuf[me_ref[0]]
        pw = pbuf[me_ref[0]]
        for dev in range(1, N_DEVICES):
            vec = vec + vbuf[jnp.bitwise_xor(me_ref[0], dev)]
            pw = pw + pbuf[jnp.bitwise_xor(me_ref[0], dev)]
        vec_ref[...] = vec
        pw_ref[...] = pw

    full = lambda s: pl.BlockSpec(s.shape, lambda i, me: (0,) * len(s.shape))
    outs = [jax.ShapeDtypeStruct(vstack.shape[1:], F32), jax.ShapeDtypeStruct(pstack.shape[1:], F32)]
    return pl.pallas_call(
        body, out_shape=outs,
        grid_spec=pltpu.PrefetchScalarGridSpec(num_scalar_prefetch=1, grid=(1,), in_specs=[full(vstack), full(pstack)],
                                               out_specs=[full(o) for o in outs]),
        compiler_params=_params(), name="small_sum",
    )(me, vstack, pstack)


def _adamw(ws, gs, ms, vs, tag):
    n = len(ws)
    rows, cols = ws[0].shape
    rb = rows
    while rb * cols * 4 * n > (1 << 20) and rb % 16 == 0:
        rb //= 2

    def body(*refs):
        for j in range(n):
            w_ref, g_ref, m_ref, v_ref = (refs[k * n + j] for k in range(4))
            go_ref, d_ref, mo_ref, vo_ref = (refs[(4 + k) * n + j] for k in range(4))
            gv = g_ref[...]
            go_ref[...] = gv
            m2 = ADAM_B1 * m_ref[...] + (1.0 - ADAM_B1) * gv
            v2 = ADAM_B2 * v_ref[...] + (1.0 - ADAM_B2) * (gv * gv)
            m_hat = m2 / (1.0 - ADAM_B1 ** ADAM_STEP)
            v_hat = v2 / (1.0 - ADAM_B2 ** ADAM_STEP)
            d_ref[...] = -ADAM_LR * (m_hat / (jnp.sqrt(v_hat) + ADAM_EPS) + ADAM_WD * w_ref[...])
            mo_ref[...] = m2
            vo_ref[...] = v2

    spec = pl.BlockSpec((rb, cols), lambda i: (i, 0))
    res, _ = _pallas(
        body, name="adamw_" + tag, args=[*ws, *gs, *ms, *vs], out_shape=[jax.ShapeDtypeStruct(ws[0].shape, F32)] * (4 * n),
        grid=(rows // rb,), in_specs=[spec] * (4 * n), out_specs=[spec] * (4 * n))
    return [tuple(res[k * n + j] for k in range(4)) for j in range(n)]


def _pack_vec(p, d, width):
    pad = lambda v: jnp.pad(v, (0, LANES - v.shape[0]))
    row3 = jnp.concatenate([p["pool_scale"], p["out_norm_pool"]])
    row4 = jnp.concatenate([p["out_norm_attn"], pad(p["q_norm"]), pad(p["k_norm"]), pad(p["b_forget"]),
                            jnp.zeros((d - width - 3 * LANES,), F32)])
    rows = [p["ffn1_norm"], p["mix_norm"], p["ffn2_norm"], row3, row4]
    return jnp.pad(jnp.stack(rows), ((0, VEC_ROWS - len(rows)), (0, 0)))


def _unpack_vec(vec, width):
    return dict(ffn1_norm=vec[0], mix_norm=vec[1], ffn2_norm=vec[2], pool_scale=vec[3, :width],
                out_norm_pool=vec[3, width:2 * width], out_norm_attn=vec[4, :width],
                q_norm=vec[4, width:width + HEAD_DIM], k_norm=vec[4, width + LANES:width + LANES + HEAD_DIM],
                b_forget=vec[4, width + 2 * LANES:width + 2 * LANES + N_HEADS])


WEIGHT_NAMES = ("ffn1_norm", "ffn1_w_gate", "ffn1_w_up", "ffn1_w_down", "mix_norm", "w_in", "b_forget", "pool_w",
                "pool_scale", "q_norm", "k_norm", "out_norm_pool", "out_norm_attn", "w_out", "ffn2_norm",
                "ffn2_w_gate", "ffn2_w_up", "ffn2_w_down")
BIG_NAMES = ("ffn1_w_gate", "ffn1_w_up", "ffn1_w_down", "w_in", "w_out", "ffn2_w_gate", "ffn2_w_up", "ffn2_w_down")
TRANSPOSED_NAMES = ("ffn1_w_gate", "ffn1_w_up", "w_in", "ffn2_w_gate", "ffn2_w_up")
FFN1_NAMES = ("ffn1_w_gate", "ffn1_w_up", "ffn1_w_down")
MIX_NAMES = ("w_in", "w_out")
FFN2_NAMES = ("ffn2_w_gate", "ffn2_w_up", "ffn2_w_down")


def kernel(x, ffn1_norm, ffn1_w_gate, ffn1_w_up, ffn1_w_down, mix_norm, w_in, b_forget, pool_w, pool_scale, q_norm, k_norm, out_norm_pool, out_norm_attn, w_out, ffn2_norm, ffn2_w_gate, ffn2_w_up, ffn2_w_down, loss_target, m_ffn1_norm, m_ffn1_w_gate, m_ffn1_w_up, m_ffn1_w_down, m_mix_norm, m_w_in, m_b_forget, m_pool_w, m_pool_scale, m_q_norm, m_k_norm, m_out_norm_pool, m_out_norm_attn, m_w_out, m_ffn2_norm, m_ffn2_w_gate, m_ffn2_w_up, m_ffn2_w_down, v_ffn1_norm, v_ffn1_w_gate, v_ffn1_w_up, v_ffn1_w_down, v_mix_norm, v_w_in, v_b_forget, v_pool_w, v_pool_scale, v_q_norm, v_k_norm, v_out_norm_pool, v_out_norm_attn, v_w_out, v_ffn2_norm, v_ffn2_w_gate, v_ffn2_w_up, v_ffn2_w_down):
    given = dict(locals())
    w = {n: given[n] for n in WEIGHT_NAMES}
    m = {n: given["m_" + n] for n in WEIGHT_NAMES}
    v = {n: given["v_" + n] for n in WEIGHT_NAMES}
    n_batch, seq, d = x.shape
    width = pool_scale.shape[0]
    in_rows = w_in.shape[1]
    in_cols = N_CHIPS * in_rows
    in_pad = -(-in_rows // 32) * 32
    in_cols_pad = in_cols - N_HEADS + LANES

    work = lambda a, n: a.T if n in TRANSPOSED_NAMES else a
    exchanged = lambda a, n: jnp.pad(a, ((0, in_pad - in_rows), (0, 0))) if n == "w_in" else a

    mesh_x, mesh_y, mesh_c = _mesh_pos()
    ids = jnp.stack([2 * mesh_x + mesh_y, mesh_c]).astype(jnp.int32)

    row = lambda a: a.reshape(1, -1)
    g1, gm, g2, ps, onp, ona = (row(a) for a in (ffn1_norm, mix_norm, ffn2_norm, pool_scale, out_norm_pool, out_norm_attn))
    qn, kn = row(jnp.tile(q_norm, N_HEADS)), row(jnp.tile(k_norm, N_HEADS))
    bf = row(jnp.pad(b_forget, (0, LANES - N_HEADS)))
    pwb = pool_w.astype(BF16)
    xf, tgt = x.reshape(n_batch * seq, d), loss_target.reshape(n_batch * seq, d)

    def grouped(call, names, *lists):
        out = [None] * len(names)
        for idx in _same_shape_groups(lists[0]):
            res = call(*[[lst[i] for i in idx] for lst in lists], names[idx[0]])
            for i, r in zip(idx, res):
                out[i] = r
        return out

    placed = dict(zip(BIG_NAMES, grouped(lambda ws, tag: _place_cast(ws, ids, tag), BIG_NAMES,
                                         [exchanged(work(w[n], n), n) for n in BIG_NAMES])))
    wg1, wu1, wd1 = _run_plan(_plan_gather_relay([placed[n] for n in FFN1_NAMES]), "gather_ffn1")
    (x1, h1, a1, b1, s1), (w_in_all, w_out_all, wd2) = _ffn_fwd(
        xf, g1, wg1, wu1, wd1, plan=_plan_gather([placed[n] for n in MIX_NAMES + FFN2_NAMES[2:]]))
    w_in_t = jnp.pad(w_in_all[:, :in_rows].reshape(in_cols, d), ((0, in_cols_pad - in_cols), (0, 0)))
    w_out_full = w_out_all.reshape(N_CHIPS * w_out.shape[0], d)
    woa, wob = w_out_full[:width], w_out_full[width:]

    hm, pv, q, k, qh, kh, vb, f = _mix_proj(x1, gm, w_in_t, qn, kn, width, width)
    qa, ka = _forget_prefix(f, bf, qh, kh, n_batch, seq)
    yp = _pool_fwd(pv, pwb, ps, onp, n_batch, seq)
    (o, lse), (wg2, wu2) = _attn_fwd(qa, ka, vb, n_batch, seq, plan=_plan_gather([placed[n] for n in FFN2_NAMES[:2]]))
    x2, ya = _mix_out(x1, yp, o, ona, woa, wob)
    (dy, h2, a2, b2, s2, lpart, dyh), _ = _ffn_fwd(x2, g2, wg2, wu2, wd2, target=tgt)

    def to_chips(gs, arrived, tags):
        return grouped(lambda g, r, tag: _add_sibling(g, r, ids, tag), tags, gs, arrived)

    def own_rows(gs, from_sibling, from_chips, tags):
        return grouped(lambda g, ra, rb, tag: _add_chips(g, ra, rb, ids, tag), tags, gs, from_sibling, from_chips)

    (dx2, da2, db2, dg2), _ = _ffn_bwd_x(dy, x2, g2, a2, b2, wg2, wu2, wd2, "ffn2_bwd_x")
    dw2, _ = _ffn_bwd_w([(da2, h2), (db2, h2), (s2, dyh)], "ffn2_bwd_w")
    (dyp, do, delta, dwoa, dwob, dona), sib2 = _mix_out_bwd(dx2, o, yp, ya, ona, woa, wob, plan=_plan_sibling_halves(dw2))
    dpv, dpw, dps, donp = _pool_bwd(pv, dyp, pwb, ps, onp, n_batch, seq)
    (dqh, dkh, dv, dfq, dfk), chips2 = _attn_bwd(qa, ka, vb, do, lse, delta, n_batch, seq,
                                                 plan=_plan_chip_exchange(to_chips(dw2, sib2, FFN2_NAMES)))
    df, dbf = _forget_bwd(dfq, dfk, f, bf, n_batch, seq)
    dx1, dx1h, dw_in_t, dgm, dqn, dkn = _mix_in_bwd(dx2, x1, gm, hm, dpv, dqh, q, dkh, k, dv, df, qn, kn, w_in_t)
    in_base = [in_rows * k // 8 * 8 for k in range(N_CHIPS)]
    d_w_in = jnp.stack([dw_in_t[b:b + in_pad] for b in in_base])
    d_w_out = jnp.concatenate([dwoa, dwob], axis=0).reshape(N_CHIPS, w_out.shape[0], d)
    dwm = [d_w_in, d_w_out]
    down, gate_up = FFN1_NAMES[2:], FFN1_NAMES[:2]
    dwd1, sibm = _ffn_bwd_w([(s1, dx1h)], "ffn1_bwd_w_down", plan=_plan_sibling_halves(dwm))
    (da1, db1), arrived = _ffn_bwd_a(dx1h, a1, b1, wd1, "ffn1_bwd_a",
                                     plan=_merge_plans(_plan_sibling_halves(dwd1),
                                                       _plan_chip_exchange(to_chips(dwm, sibm, MIX_NAMES))))
    sibd, chipsm = arrived[:1], arrived[1:]
    dwgu1, chipsd = _ffn_bwd_w([(da1, h1), (db1, h1)], "ffn1_bwd_w_gate_up",
                               plan=_plan_chip_exchange(to_chips(dwd1, sibd, down)))
    n_tiles = (n_batch * seq) // min(FFN_TILE, n_batch * seq)
    first = max(n_tiles // 4, 1)
    begun, sibgu = _ffn_bwd_h(dx1, xf, g1, da1, db1, wg1, wu1, "ffn1_bwd_h_first", (0, first),
                              plan=_plan_sibling_halves(dwgu1))
    (gx, dg1), chipsgu = _ffn_bwd_h(dx1, xf, g1, da1, db1, wg1, wu1, "ffn1_bwd_h_rest", (first, n_tiles), prev=begun,
                                    plan=_plan_chip_exchange(to_chips(dwgu1, sibgu, gate_up)))

    part = dict(ffn1_norm=dg1, mix_norm=dgm, ffn2_norm=dg2, b_forget=dbf, pool_scale=dps, out_norm_pool=donp,
                out_norm_attn=dona, qn=dqn, kn=dkn, pool_w=dpw.reshape(n_batch, -1, pool_w.shape[-1]), loss=lpart)
    mine = (own_rows(dwgu1, sibgu, chipsgu, gate_up) + own_rows(dwd1, sibd, chipsd, down)
            + own_rows(dwm, sibm, chipsm, MIX_NAMES) + own_rows(dw2, sib2, chips2, FFN2_NAMES))
    last = _run_plan(_merge_plans(_plan_sibling_share(mine), _plan_all_to_all(_small_pack(part, d, width))), "last_exchange")
    vstack, pstack = last[len(mine):]
    g_vec, g_pw = _small_sum(vstack, pstack, jnp.reshape(4 * mesh_x + 2 * mesh_y + mesh_c, (1,)).astype(jnp.int32))
    loss = g_vec[5, 0]
    reduced = dict(zip(FFN1_NAMES + MIX_NAMES + FFN2_NAMES, last[:len(mine)]))
    reduced["w_in"] = lax.dynamic_slice(reduced["w_in"], ((in_rows * ids[0]) % 8, 0), (in_rows, d))

    grads, delta, new_m, new_v = {}, {}, {}, {}
    for names in (FFN2_NAMES, FFN1_NAMES, ("w_in",), ("w_out",)):
        stepped = _adamw([work(w[n], n) for n in names], [reduced[n] for n in names], [work(m[n], n) for n in names],
                         [work(v[n], n) for n in names], names[0])
        for n, step in zip(names, stepped):
            grads[n], delta[n], new_m[n], new_v[n] = (work(a, n) for a in step)
    flat_pw = lambda a: a.reshape(-1, a.shape[-1])
    (_, d_pw, m_pw, v_pw), = _adamw([flat_pw(pool_w)], [g_pw], [flat_pw(m_pool_w)], [flat_pw(v_pool_w)], "pool_w")
    (_, d_vec, m_vec, v_vec), = _adamw([_pack_vec(w, d, width)], [g_vec], [_pack_vec(m, d, width)],
                                       [_pack_vec(v, d, width)], "vectors")
    grads.update(_unpack_vec(g_vec, width), pool_w=g_pw.reshape(pool_w.shape))
    delta.update(_unpack_vec(d_vec, width), pool_w=d_pw.reshape(pool_w.shape))
    new_m.update(_unpack_vec(m_vec, width), pool_w=m_pw.reshape(pool_w.shape))
    new_v.update(_unpack_vec(v_vec, width), pool_w=v_pw.reshape(pool_w.shape))
    return (loss, gx.reshape(x.shape), *[grads[n] for n in WEIGHT_NAMES], *[delta[n] for n in WEIGHT_NAMES],
            *[new_m[n] for n in WEIGHT_NAMES], *[new_v[n] for n in WEIGHT_NAMES])
```

```python
import functools

import jax
import jax.numpy as jnp
from jax import lax
from jax.experimental import pallas as pl
from jax.experimental.pallas import tpu as pltpu

F32 = jnp.float32
BF16 = jnp.bfloat16
EPS = 1e-6
NEG = -1e30
ADAM_LR = 0.001
ADAM_B1 = 0.9
ADAM_B2 = 0.999
ADAM_EPS = 1e-08
ADAM_WD = 0.01
ADAM_STEP = 10
POOL_WINDOWS = (2, 4, 8, 16)
HEAD_DIM = 64
N_HEADS = 8
LANES = 128
N_CHIPS = 4
ATT_BLOCK = 512
ATT_SUB = 128
FFN_TILE = 1024
VMEM_LIMIT = 62 * 1024 * 1024
ANY = pl.BlockSpec(memory_space=pl.ANY)
VM = pl.BlockSpec(memory_space=pltpu.VMEM)


def _params(**kw):
    return pltpu.CompilerParams(vmem_limit_bytes=VMEM_LIMIT, **kw)


def _dot(a, b):
    return jnp.dot(a, b, preferred_element_type=F32)


def _dot_nt(a, b):
    return lax.dot_general(a, b, (((1,), (1,)), ((), ())), preferred_element_type=F32)


def _dot_tn(a, b):
    return lax.dot_general(a, b, (((0,), (0,)), ((), ())), preferred_element_type=F32)


def _sigmoid(z):
    return 1.0 / (1.0 + jnp.exp(-z))


def _rms(xf):
    return lax.rsqrt(jnp.mean(xf * xf, axis=-1, keepdims=True) + EPS)


def _rms_bwd(xf, r, gain, dh):
    xh = xf * r
    dyg = dh * gain
    return r * (dyg - xh * jnp.mean(dyg * xh, axis=-1, keepdims=True)), dh * xh


def _total(v):
    return jnp.sum(jnp.sum(v, axis=1, keepdims=True), axis=0, keepdims=True)


def _ffn_fwd(x, gain, wg, wu, wd, target=None, plan=None):
    t, d = x.shape
    nch, fc, _ = wg.shape
    tm = min(FFN_TILE, t)
    nt = t // tm
    with_loss = target is not None

    def body(*refs):
        if with_loss:
            x_ref, g_ref, wg_ref, wu_ref, wd_ref, t_ref, o_ref, h_ref, a_ref, b_ref, s_ref, l_ref, oh_ref, acc_ref = refs
        else:
            x_ref, g_ref, wg_ref, wu_ref, wd_ref, o_ref, h_ref, a_ref, b_ref, s_ref, acc_ref = refs
        k = pl.program_id(1)

        @pl.when(k == 0)
        def _():
            xf = x_ref[...]
            h_ref[...] = ((xf * _rms(xf)) * g_ref[...]).astype(BF16)
            acc_ref[...] = jnp.zeros_like(acc_ref)

        for rows in _row_halves(tm):
            h = h_ref[rows, :]
            a = _dot_nt(h, wg_ref[...])
            b = _dot_nt(h, wu_ref[...])
            sb = ((a * (0.5 * jnp.tanh(0.5 * a) + 0.5)) * b).astype(BF16)
            a_ref[rows, :] = a.astype(BF16)
            b_ref[rows, :] = b.astype(BF16)
            s_ref[rows, :] = sb
            acc_ref[rows, :] += _dot(sb, wd_ref[...])

        @pl.when(k == nch - 1)
        def _():
            y = x_ref[...] + 0.5 * acc_ref[...]
            if with_loss:
                e = y - t_ref[...]
                o_ref[...] = e * (1.0 / d)
                oh_ref[...] = (e * (0.5 / d)).astype(BF16)
                l_ref[...] = jnp.broadcast_to(_total(e * e) * (0.5 / d), l_ref.shape)
            else:
                o_ref[...] = y

    row = pl.BlockSpec((tm, d), lambda i, k: (i, 0))
    chunk = pl.BlockSpec((None, fc, d), lambda i, k: (k, 0, 0))
    act = pl.BlockSpec((None, tm, fc), lambda i, k: (k, i, 0))
    in_specs = [row, pl.BlockSpec((1, d), lambda i, k: (0, 0)), chunk, chunk, chunk]
    out_shape = [jax.ShapeDtypeStruct((t, d), F32), jax.ShapeDtypeStruct((t, d), BF16)]
    out_shape += [jax.ShapeDtypeStruct((nch, t, fc), BF16)] * 3
    out_specs = [row, row, act, act, act]
    args = [x, gain, wg, wu, wd]
    if with_loss:
        in_specs.append(row)
        args.append(target)
        out_shape += [jax.ShapeDtypeStruct((nt, 8, LANES), F32), jax.ShapeDtypeStruct((t, d), BF16)]
        out_specs += [pl.BlockSpec((None, 8, LANES), lambda i, k: (i, 0, 0)), row]
    return _pallas(body, name="ffn_fwd_loss" if with_loss else "ffn_fwd", args=args, in_specs=in_specs,
                   out_shape=out_shape, out_specs=out_specs, grid=(nt, nch),
                   scratch_shapes=[pltpu.VMEM((tm, d), F32)], plan=plan)


def _row_halves(n):
    return [slice(0, n // 2), slice(n // 2, n)]


def _swiglu_grads(dyh, a_ref, b_ref, wd_ref, rows):
    ds = _dot_nt(dyh, wd_ref[...])
    av = a_ref[rows, :].astype(F32)
    bv = b_ref[rows, :].astype(F32)
    th = jnp.tanh(0.5 * av)
    sig = 0.5 * th + 0.5
    dab = ((ds * bv) * (sig * (1.0 + av * (0.5 - 0.5 * th)))).astype(BF16)
    return dab, (ds * (av * sig)).astype(BF16)


def _ffn_bwd_a(dyh, a, b, wd, name, plan=None):
    t, d = dyh.shape
    nch, fc, _ = wd.shape
    tm = min(FFN_TILE, t)

    def body(dyh_ref, a_ref, b_ref, wd_ref, da_ref, db_ref):
        for rows in _row_halves(tm):
            da_ref[rows, :], db_ref[rows, :] = _swiglu_grads(dyh_ref[rows, :], a_ref, b_ref, wd_ref, rows)

    act = pl.BlockSpec((None, tm, fc), lambda i, k: (k, i, 0))
    return _pallas(
        body, name=name, args=[dyh, a, b, wd], out_shape=[jax.ShapeDtypeStruct((nch, t, fc), BF16)] * 2, grid=(t // tm, nch),
        in_specs=[pl.BlockSpec((tm, d), lambda i, k: (i, 0)), act, act, pl.BlockSpec((None, fc, d), lambda i, k: (k, 0, 0))],
        out_specs=[act, act], plan=plan)


def _ffn_bwd_h(dy, x, gain, da, db, wg, wu, name, tiles, prev=None, plan=None):
    t, d = x.shape
    nch, fc, _ = wg.shape
    tm = min(FFN_TILE, t)
    nt = t // tm
    t0, t1 = tiles

    def body(*refs):
        dy_ref, x_ref, g_ref, da_ref, db_ref, wg_ref, wu_ref = refs[:7]
        dx_ref, dg_ref, acc_ref = refs[-3:]
        k = pl.program_id(1)

        @pl.when(k == 0)
        def _():
            acc_ref[...] = jnp.zeros_like(acc_ref)

        acc_ref[...] += _dot(da_ref[...], wg_ref[...]) + _dot(db_ref[...], wu_ref[...])

        @pl.when(k == nch - 1)
        def _():
            xf = x_ref[...]
            dxn, dgr = _rms_bwd(xf, _rms(xf), g_ref[...], acc_ref[...])
            dx_ref[...] = dy_ref[...] + dxn
            dg_ref[...] = jnp.sum(dgr, axis=0, keepdims=True)

    row = pl.BlockSpec((tm, d), lambda i, k: (i + t0, 0))
    chunk = pl.BlockSpec((None, fc, d), lambda i, k: (k, 0, 0))
    act = pl.BlockSpec((None, tm, fc), lambda i, k: (k, i + t0, 0))
    args = [dy, x, gain, da, db, wg, wu]
    in_specs = [row, row, pl.BlockSpec((1, d), lambda i, k: (0, 0)), act, act, chunk, chunk]
    aliases = {}
    if prev is not None:
        aliases = {len(args): 0, len(args) + 1: 1}
        args += list(prev)
        in_specs += [ANY, ANY]
    return _pallas(
        body, name=name, args=args, out_shape=[jax.ShapeDtypeStruct((t, d), F32), jax.ShapeDtypeStruct((nt, 1, d), F32)],
        grid=(t1 - t0, nch), in_specs=in_specs,
        out_specs=[row, pl.BlockSpec((None, 1, d), lambda i, k: (i + t0, 0, 0))],
        scratch_shapes=[pltpu.VMEM((tm, d), F32)], plan=plan, aliases=aliases)


def _ffn_bwd_x(dy, x, gain, a, b, wg, wu, wd, name, plan=None):
    t, d = x.shape
    nch, fc, _ = wg.shape
    tm = min(FFN_TILE, t)
    nt = t // tm

    def body(dy_ref, x_ref, g_ref, a_ref, b_ref, wg_ref, wu_ref, wd_ref, dx_ref, da_ref, db_ref, dg_ref, acc_ref):
        k = pl.program_id(1)

        @pl.when(k == 0)
        def _():
            acc_ref[...] = jnp.zeros_like(acc_ref)

        for rows in _row_halves(tm):
            dab, dbb = _swiglu_grads((0.5 * dy_ref[rows, :]).astype(BF16), a_ref, b_ref, wd_ref, rows)
            da_ref[rows, :] = dab
            db_ref[rows, :] = dbb
            acc_ref[rows, :] += _dot(dab, wg_ref[...]) + _dot(dbb, wu_ref[...])

        @pl.when(k == nch - 1)
        def _():
            xf = x_ref[...]
            dxn, dgr = _rms_bwd(xf, _rms(xf), g_ref[...], acc_ref[...])
            dx_ref[...] = dy_ref[...] + dxn
            dg_ref[...] = jnp.sum(dgr, axis=0, keepdims=True)

    row = pl.BlockSpec((tm, d), lambda i, k: (i, 0))
    chunk = pl.BlockSpec((None, fc, d), lambda i, k: (k, 0, 0))
    act = pl.BlockSpec((None, tm, fc), lambda i, k: (k, i, 0))
    return _pallas(
        body, name=name, args=[dy, x, gain, a, b, wg, wu, wd],
        out_shape=[jax.ShapeDtypeStruct((t, d), F32), jax.ShapeDtypeStruct((nch, t, fc), BF16),
                   jax.ShapeDtypeStruct((nch, t, fc), BF16), jax.ShapeDtypeStruct((nt, 1, d), F32)],
        grid=(nt, nch),
        in_specs=[row, row, pl.BlockSpec((1, d), lambda i, k: (0, 0)), act, act, chunk, chunk, chunk],
        out_specs=[row, act, act, pl.BlockSpec((None, 1, d), lambda i, k: (i, 0, 0))],
        scratch_shapes=[pltpu.VMEM((tm, d), F32)], plan=plan)


def _ffn_bwd_w(pairs, name, plan=None):
    n = len(pairs)
    nch, t, fc = pairs[0][0].shape
    d = pairs[0][1].shape[1]
    tm = min(FFN_TILE, t)

    def body(*refs):
        @pl.when(pl.program_id(1) == 0)
        def _():
            for o_ref in refs[2 * n:]:
                o_ref[...] = jnp.zeros_like(o_ref)

        for j in range(n):
            refs[2 * n + j][...] += _dot_tn(refs[j][...], refs[n + j][...])

    row = pl.BlockSpec((tm, d), lambda k, i: (i, 0))
    act = pl.BlockSpec((None, tm, fc), lambda k, i: (k, i, 0))
    chunk = pl.BlockSpec((None, fc, d), lambda k, i: (k, 0, 0))
    return _pallas(body, name=name, args=[p[0] for p in pairs] + [p[1] for p in pairs],
                   out_shape=[jax.ShapeDtypeStruct((nch, fc, d), F32)] * n, grid=(nch, t // tm),
                   in_specs=[act] * n + [row] * n, out_specs=[chunk] * n, plan=plan)


def _head_masks():
    lane = lax.broadcasted_iota(jnp.int32, (1, LANES), 1)
    return lane < HEAD_DIM


def _head_rms(x, lo):
    x2 = x * x
    s0 = jnp.sum(jnp.where(lo, x2, 0.0), axis=1, keepdims=True)
    s1 = jnp.sum(jnp.where(lo, 0.0, x2), axis=1, keepdims=True)
    return jnp.where(lo, lax.rsqrt(s0 * (1.0 / HEAD_DIM) + EPS), lax.rsqrt(s1 * (1.0 / HEAD_DIM) + EPS))


def _head_mean(v, lo):
    s0 = jnp.sum(jnp.where(lo, v, 0.0), axis=1, keepdims=True)
    s1 = jnp.sum(jnp.where(lo, 0.0, v), axis=1, keepdims=True)
    return jnp.where(lo, s0, s1) * (1.0 / HEAD_DIM)


def _mix_proj(x1, gain, wt, qn, kn, pool_width, attn_width):
    t, d = x1.shape
    tm = min(512, t)
    nt = t // tm
    scale = HEAD_DIM ** -0.5
    c_q, c_k, c_v = pool_width, pool_width + attn_width, pool_width + 2 * attn_width
    c_f = c_v + attn_width

    def body(x_ref, g_ref, wt_ref, qn_ref, kn_ref, hm_ref, pv_ref, q_ref, k_ref, qh_ref, kh_ref, vb_ref, f_ref):
        lo = _head_masks()
        for rows in _row_halves(tm):
            xf = x_ref[rows, :]
            hm = ((xf * _rms(xf)) * g_ref[...]).astype(BF16)
            hm_ref[rows, :] = hm
            f_ref[rows, :] = _dot_nt(hm, wt_ref[c_f:c_f + LANES, :])
            pv_ref[rows, :] = _dot_nt(hm, wt_ref[0:pool_width, :])
            vb_ref[rows, :] = _dot_nt(hm, wt_ref[c_v:c_v + attn_width, :]).astype(BF16)
            for c0, raw_ref, hat_ref, n_ref, mul in ((c_q, q_ref, qh_ref, qn_ref, scale), (c_k, k_ref, kh_ref, kn_ref, 1.0)):
                raw = _dot_nt(hm, wt_ref[c0:c0 + attn_width, :])
                raw_ref[rows, :] = raw
                for blk in range(attn_width // LANES):
                    sl = slice(blk * LANES, (blk + 1) * LANES)
                    xb = raw[:, sl]
                    hat_ref[rows, sl] = (((xb * _head_rms(xb, lo)) * n_ref[:, sl]) * mul).astype(BF16)

    row = pl.BlockSpec((tm, d), lambda i: (i, 0))
    half = pl.BlockSpec((tm, attn_width), lambda i: (i, 0))
    const = lambda shape: pl.BlockSpec(shape, lambda i: (0, 0))
    return _pallas(
        body, name="mix_proj", args=[x1, gain, wt, qn, kn],
        out_shape=[jax.ShapeDtypeStruct((t, d), BF16), jax.ShapeDtypeStruct((t, pool_width), F32),
                   jax.ShapeDtypeStruct((t, attn_width), F32), jax.ShapeDtypeStruct((t, attn_width), F32),
                   jax.ShapeDtypeStruct((t, attn_width), BF16), jax.ShapeDtypeStruct((t, attn_width), BF16),
                   jax.ShapeDtypeStruct((t, attn_width), BF16), jax.ShapeDtypeStruct((t, LANES), F32)],
        grid=(nt,),
        in_specs=[row, const((1, d)), const(wt.shape), const((1, attn_width)), const((1, attn_width))],
        out_specs=[row, pl.BlockSpec((tm, pool_width), lambda i: (i, 0)), half, half, half, half, half,
                   pl.BlockSpec((tm, LANES), lambda i: (i, 0))])[0]


def _shift_down(v, dist, row):
    return jnp.where(row >= dist, pltpu.roll(v, dist, 0), 0.0)


def _shift_up(v, dist, row, n):
    return jnp.where(row + dist < n, pltpu.roll(v, n - dist, 0), 0.0)


def _aug_lane(e):
    return HEAD_DIM if e == 0 else 0


def _forget_prefix(f, bias, qh, kh, n_batch, seq):
    def body(f_ref, b_ref, q_ref, k_ref, qa_ref, ka_ref):
        z = f_ref[...] + b_ref[...]
        acc = jnp.minimum(z, 0.0) - jnp.log(1.0 + jnp.exp(-jnp.abs(z)))
        row = lax.broadcasted_iota(jnp.int32, (seq, 1), 0)
        dist = 1
        while dist < seq:
            acc = acc + _shift_down(acc, dist, row)
            dist *= 2
        lane = lax.broadcasted_iota(jnp.int32, (1, LANES), 1)
        for h in range(N_HEADS):
            pair, e = divmod(h, 2)
            a0 = _aug_lane(e)
            own = (lane < HEAD_DIM) if e == 0 else (lane >= HEAD_DIM)
            fh = _pick_lane(acc, h)
            hi = fh.astype(BF16).astype(F32)
            rest = fh - hi
            mid = rest.astype(BF16).astype(F32)
            low = rest - mid
            q_ones = (lane >= a0 + 3) & (lane < a0 + 6)
            k_ones = (lane >= a0) & (lane < a0 + 3)
            q_aug = jnp.where(lane == a0, hi, jnp.where(lane == a0 + 1, mid, jnp.where(lane == a0 + 2, low,
                              jnp.where(q_ones, 1.0, 0.0))))
            k_aug = jnp.where(k_ones, 1.0, jnp.where(lane == a0 + 3, -hi, jnp.where(lane == a0 + 4, -mid,
                              jnp.where(lane == a0 + 5, -low, 0.0))))
            src = slice(pair * LANES, (pair + 1) * LANES)
            dst = slice(h * LANES, (h + 1) * LANES)
            qa_ref[:, dst] = jnp.where(own, q_ref[:, src].astype(F32), q_aug).astype(BF16)
            ka_ref[:, dst] = jnp.where(own, k_ref[:, src].astype(F32), k_aug).astype(BF16)

    width = qh.shape[1]
    tok = pl.BlockSpec((seq, width), lambda b: (b, 0))
    aug = pl.BlockSpec((seq, N_HEADS * LANES), lambda b: (b, 0))
    return pl.pallas_call(
        body, out_shape=[jax.ShapeDtypeStruct((n_batch * seq, N_HEADS * LANES), BF16)] * 2, grid=(n_batch,),
        in_specs=[pl.BlockSpec((seq, LANES), lambda b: (b, 0)), pl.BlockSpec((1, LANES), lambda b: (0, 0)), tok, tok],
        out_specs=[aug, aug], compiler_params=_params(), name="forget_prefix",
    )(f, bias, qh, kh)


def _pool_groups(pv_ref, pw_ref, ps_ref, seq):
    row = lax.broadcasted_iota(jnp.int32, (seq, 1), 0)
    pos = (row + 1).astype(F32)
    out = []
    for g, win in enumerate(POOL_WINDOWS):
        sl = slice(g * LANES, (g + 1) * LANES)
        xg = pv_ref[:, sl]
        acc = xg
        dist = 1
        while dist < win:
            acc = acc + _shift_down(acc, dist, row)
            dist *= 2
        pooled = (acc / jnp.minimum(pos, float(win)) - xg).astype(BF16)
        mixed = _dot(pooled, pw_ref[g])
        out.append((pooled, mixed, mixed * ps_ref[:, sl]))
    return out


def _pool_fwd(pv, pw, ps, onp, n_batch, seq):
    width = pv.shape[1]

    def body(pv_ref, pw_ref, ps_ref, on_ref, y_ref):
        groups = _pool_groups(pv_ref, pw_ref, ps_ref, seq)
        ssq = sum(jnp.sum(ms * ms, axis=1, keepdims=True) for _, _, ms in groups)
        r = lax.rsqrt(ssq * (1.0 / width) + EPS)
        for g, (_, _, ms) in enumerate(groups):
            sl = slice(g * LANES, (g + 1) * LANES)
            y_ref[:, sl] = ((ms * r) * on_ref[:, sl]).astype(BF16)

    return pl.pallas_call(
        body, out_shape=jax.ShapeDtypeStruct((n_batch * seq, width), BF16), grid=(n_batch,),
        in_specs=[pl.BlockSpec((seq, width), lambda b: (b, 0)), pl.BlockSpec(pw.shape, lambda b: (0, 0, 0)),
                  pl.BlockSpec((1, width), lambda b: (0, 0)), pl.BlockSpec((1, width), lambda b: (0, 0))],
        out_specs=pl.BlockSpec((seq, width), lambda b: (b, 0)),
        compiler_params=_params(), name="pool_fwd",
    )(pv, pw, ps, onp)


def _pool_bwd(pv, dyp, pw, ps, onp, n_batch, seq):
    width = pv.shape[1]

    def body(pv_ref, dy_ref, pw_ref, ps_ref, on_ref, dpv_ref, dpw_ref, dps_ref, don_ref):
        groups = _pool_groups(pv_ref, pw_ref, ps_ref, seq)
        ssq = sum(jnp.sum(ms * ms, axis=1, keepdims=True) for _, _, ms in groups)
        r = lax.rsqrt(ssq * (1.0 / width) + EPS)
        mean = sum(jnp.sum((dy_ref[:, g * LANES:(g + 1) * LANES] * on_ref[:, g * LANES:(g + 1) * LANES]) * (ms * r),
                           axis=1, keepdims=True) for g, (_, _, ms) in enumerate(groups)) * (1.0 / width)
        row = lax.broadcasted_iota(jnp.int32, (seq, 1), 0)
        pos = (row + 1).astype(F32)
        for g, (pooled, mixed, ms) in enumerate(groups):
            sl = slice(g * LANES, (g + 1) * LANES)
            dy = dy_ref[:, sl]
            xh = ms * r
            don_ref[:, sl] = jnp.sum(dy * xh, axis=0, keepdims=True)
            dms = r * (dy * on_ref[:, sl] - xh * mean)
            dps_ref[:, sl] = jnp.sum(dms * mixed, axis=0, keepdims=True)
            dmix = (dms * ps_ref[:, sl]).astype(BF16)
            dpw_ref[g] = _dot_tn(pooled, dmix)
            dpool = _dot_nt(dmix, pw_ref[g])
            win = POOL_WINDOWS[g]
            acc = dpool / jnp.minimum(pos, float(win))
            dist = 1
            while dist < win:
                acc = acc + _shift_up(acc, dist, row, seq)
                dist *= 2
            dpv_ref[:, sl] = (acc - dpool).astype(BF16)

    tok = pl.BlockSpec((seq, width), lambda b: (b, 0))
    vec = pl.BlockSpec((1, width), lambda b: (0, 0))
    pvec = pl.BlockSpec((None, 1, width), lambda b: (b, 0, 0))
    return pl.pallas_call(
        body,
        out_shape=[jax.ShapeDtypeStruct((n_batch * seq, width), BF16),
                   jax.ShapeDtypeStruct((n_batch,) + pw.shape, F32),
                   jax.ShapeDtypeStruct((n_batch, 1, width), F32), jax.ShapeDtypeStruct((n_batch, 1, width), F32)],
        grid=(n_batch,),
        in_specs=[tok, tok, pl.BlockSpec(pw.shape, lambda b: (0, 0, 0)), vec, vec],
        out_specs=[tok, pl.BlockSpec((None,) + pw.shape, lambda b: (b, 0, 0, 0)), pvec, pvec],
        compiler_params=_params(), name="pool_bwd",
    )(pv, dyp, pw, ps, onp)


def _pick_lane(tile, idx):
    lane = lax.broadcasted_iota(jnp.int32, (1, LANES), 1)
    return jnp.sum(jnp.where(lane == idx, tile, 0.0), axis=1, keepdims=True)


def _pick_row(tile, idx):
    sub = lax.broadcasted_iota(jnp.int32, (tile.shape[0], 1), 0)
    return jnp.sum(jnp.where(sub == idx, tile, 0.0), axis=0, keepdims=True)


def _put_lane(col, idx):
    lane = lax.broadcasted_iota(jnp.int32, (1, LANES), 1)
    return jnp.where(lane == idx, col, 0.0)


def _head_select(e):
    lo = _head_masks()
    return lo if e == 0 else jnp.logical_not(lo)


def _causal(st, shift):
    row = lax.broadcasted_iota(jnp.int32, st.shape, 0)
    col = lax.broadcasted_iota(jnp.int32, st.shape, 1) + shift
    return jnp.where(col >= row, st, NEG)


def _transpose_blocks(a):
    rows, cols = a.shape
    return jnp.concatenate(
        [jnp.concatenate([a[r:r + LANES, c:c + LANES].T for r in range(0, rows, LANES)], axis=1)
         for c in range(0, cols, LANES)], axis=0)


def _stat_rows(ref, head, nsub):
    return jnp.concatenate([_pick_row(ref[a], head) for a in range(nsub)], axis=1)


def _accumulate(ref, value, first):
    @pl.when(first)
    def _():
        ref[...] = value

    @pl.when(jnp.logical_not(first))
    def _():
        ref[...] += value


def _attn_fwd(qa, ka, vb, n_batch, seq, plan=None):
    tq = min(ATT_BLOCK, seq)
    nq, nsub, tk = seq // tq, tq // ATT_SUB, tq
    pairs = vb.shape[1] // LANES

    def body(q_ref, k_ref, v_ref, o_ref, lse_ref, acc_ref):
        i, p = pl.program_id(1), pl.program_id(2)
        row_lo = lax.broadcasted_iota(jnp.int32, (LANES, 1), 0) < HEAD_DIM
        qs = [q_ref[:, e * LANES:(e + 1) * LANES] for e in range(2)]
        acc_ref[...] = jnp.zeros_like(acc_ref)

        def tile(off, stats, diagonal):
            vj = v_ref[pl.ds(off, tk), :]
            new, alphas, pvs = [], [], []
            for e in range(2):
                st = _dot_nt(k_ref[pl.ds(off, tk), e * LANES:(e + 1) * LANES], qs[e])
                if diagonal:
                    st = _causal(st, 0)
                m, l = stats[e]
                m_new = jnp.maximum(m, jnp.max(st, axis=0, keepdims=True))
                alpha = jnp.exp(m - m_new)
                pt = jnp.exp(st - m_new)
                new.append((m_new, alpha * l + jnp.sum(pt, axis=0, keepdims=True)))
                alphas.append(alpha)
                pvs.append(_dot_tn(jnp.where(_head_select(e), vj, jnp.zeros_like(vj)), pt.astype(BF16)))
            acc_ref[...] = acc_ref[...] * jnp.where(row_lo, alphas[0], alphas[1]) + (pvs[0] + pvs[1])
            return tuple(new)

        init = ((jnp.full((1, tq), NEG, F32), jnp.zeros((1, tq), F32)),) * 2
        stats = lax.fori_loop(0, i, lambda j, st: tile(pl.multiple_of(j * tk, tk), st, False), init)
        (m0, l0), (m1, l1) = tile(pl.multiple_of(i * tk, tk), stats, True)
        out_t = acc_ref[...] / jnp.where(row_lo, l0, l1)
        sub = lax.broadcasted_iota(jnp.int32, (8, 1), 0)
        lse0, lse1 = m0 + jnp.log(l0), m1 + jnp.log(l1)
        for a in range(nsub):
            sl = slice(a * ATT_SUB, (a + 1) * ATT_SUB)
            o_ref[sl, :] = out_t[:, sl].T
            rows = jnp.where(sub == 2 * p, lse0[:, sl], 0.0) + jnp.where(sub == 2 * p + 1, lse1[:, sl], 0.0)
            _accumulate(lse_ref.at[a], rows, p == 0)

    return _pallas(
        body, name="attn_fwd", args=[qa, ka, vb],
        out_shape=[jax.ShapeDtypeStruct((n_batch * seq, pairs * LANES), F32),
                   jax.ShapeDtypeStruct((n_batch * seq // ATT_SUB, 8, ATT_SUB), F32)],
        grid=(n_batch, nq, pairs),
        in_specs=[pl.BlockSpec((tq, 2 * LANES), lambda b, i, p: (b * nq + i, p)),
                  pl.BlockSpec((seq, 2 * LANES), lambda b, i, p: (b, p)),
                  pl.BlockSpec((seq, LANES), lambda b, i, p: (b, p))],
        out_specs=[pl.BlockSpec((tq, LANES), lambda b, i, p: (b * nq + i, p)),
                   pl.BlockSpec((nsub, 8, ATT_SUB), lambda b, i, p: (b * nq + i, 0, 0))],
        scratch_shapes=[pltpu.VMEM((LANES, tq), F32)], plan=plan)


def _attn_bwd(qa, ka, vb, do, lse, delta, n_batch, seq, plan=None):
    tq = min(ATT_BLOCK, seq)
    nq, nsub = seq // tq, tq // ATT_SUB
    n_tiles = seq // ATT_SUB
    pairs = vb.shape[1] // LANES

    def body(q_ref, k_ref, v_ref, do_ref, lse_ref, dl_ref, dq_ref, dk_ref, dv_ref, dfq_ref, dfk_ref,
             dq0_ref, dq1_ref, dk0_ref, dk1_ref, dva_ref):
        p = pl.program_id(1)
        dqs, dks = (dq0_ref, dq1_ref), (dk0_ref, dk1_ref)
        for acc in (dk0_ref, dk1_ref, dva_ref):
            acc[...] = jnp.zeros_like(acc)
        dfq_cols = []
        for i in range(nq):
            rows_i = slice(i * tq, (i + 1) * tq)
            qs = [q_ref[rows_i, e * LANES:(e + 1) * LANES] for e in range(2)]
            dov = do_ref[rows_i, :]
            does = [jnp.where(_head_select(e), dov, jnp.zeros_like(dov)) for e in range(2)]
            stat = lambda ref, e: jnp.concatenate([_pick_row(ref[i * nsub + a], 2 * p + e) for a in range(nsub)], axis=1)
            ls, dl = [stat(lse_ref, e) for e in range(2)], [stat(dl_ref, e) for e in range(2)]
            for acc in dqs:
                acc[...] = jnp.zeros_like(acc)

            def tile(off, diagonal, qs=qs, dov=dov, does=does, ls=ls, dl=dl):
                vj = v_ref[pl.ds(off, tq), :]
                for e in range(2):
                    kj = k_ref[pl.ds(off, tq), e * LANES:(e + 1) * LANES]
                    st = _dot_nt(kj, qs[e])
                    if diagonal:
                        st = _causal(st, 0)
                    pt = jnp.exp(st - ls[e])
                    dva_ref[pl.ds(off, tq), :] += _dot(pt.astype(BF16), does[e])
                    dpt = _dot_nt(jnp.where(_head_select(e), vj, jnp.zeros_like(vj)), dov)
                    dst = (pt * (dpt - dl[e])).astype(BF16)
                    dks[e][pl.ds(off, tq), :] += _dot(dst, qs[e])
                    dqs[e][...] += _dot(_transpose_blocks(kj), dst)

            def step(j, carry, tile=tile):
                tile(pl.multiple_of(j * tq, tq), False)
                return carry

            lax.fori_loop(0, i, step, 0)
            tile(i * tq, True)
            dq0, dq1 = _transpose_blocks(dq0_ref[...]), _transpose_blocks(dq1_ref[...])
            dq_ref[rows_i, :] = jnp.where(_head_masks(), dq0, dq1)
            dfq_cols.append(_put_lane(_pick_lane(dq0, _aug_lane(0)), 2 * p) + _put_lane(_pick_lane(dq1, _aug_lane(1)), 2 * p + 1))
        dk0, dk1 = dk0_ref[...], dk1_ref[...]
        dk_ref[...] = jnp.where(_head_masks(), dk0, dk1)
        dv_ref[...] = dva_ref[...].astype(BF16)
        dfk = _put_lane(_pick_lane(dk0, _aug_lane(0) + 3), 2 * p) + _put_lane(_pick_lane(dk1, _aug_lane(1) + 3), 2 * p + 1)
        _accumulate(dfq_ref, jnp.concatenate(dfq_cols, axis=0), p == 0)
        _accumulate(dfk_ref, -dfk, p == 0)

    wide = pl.BlockSpec((seq, 2 * LANES), lambda b, p: (b, p))
    blk = pl.BlockSpec((seq, LANES), lambda b, p: (b, p))
    col = pl.BlockSpec((seq, LANES), lambda b, p: (b, 0))
    stat = pl.BlockSpec((n_tiles, 8, ATT_SUB), lambda b, p: (b, 0, 0))
    f32_blk, acc = jax.ShapeDtypeStruct((n_batch * seq, pairs * LANES), F32), pltpu.VMEM((seq, LANES), F32)
    return _pallas(
        body, name="attn_bwd", args=[qa, ka, vb, do, lse, delta],
        out_shape=[f32_blk, f32_blk, jax.ShapeDtypeStruct((n_batch * seq, pairs * LANES), BF16),
                   jax.ShapeDtypeStruct((n_batch * seq, LANES), F32), jax.ShapeDtypeStruct((n_batch * seq, LANES), F32)],
        grid=(n_batch, pairs), in_specs=[wide, wide, blk, blk, stat, stat], out_specs=[blk, blk, blk, col, col],
        scratch_shapes=[pltpu.VMEM((LANES, tq), F32), pltpu.VMEM((LANES, tq), F32), acc, acc, acc], plan=plan)


def _forget_bwd(dfq, dfk, f, bias, n_batch, seq):
    def body(dfq_ref, dfk_ref, f_ref, b_ref, df_ref, db_ref):
        acc = dfq_ref[...] + dfk_ref[...]
        row = lax.broadcasted_iota(jnp.int32, (seq, 1), 0)
        dist = 1
        while dist < seq:
            acc = acc + _shift_up(acc, dist, row, seq)
            dist *= 2
        df = acc * _sigmoid(-(f_ref[...] + b_ref[...]))
        df_ref[...] = df
        db_ref[...] = jnp.sum(df, axis=0, keepdims=True)

    col = pl.BlockSpec((seq, LANES), lambda b: (b, 0))
    return pl.pallas_call(
        body,
        out_shape=[jax.ShapeDtypeStruct((n_batch * seq, LANES), F32), jax.ShapeDtypeStruct((n_batch, 1, LANES), F32)],
        grid=(n_batch,), in_specs=[col, col, col, pl.BlockSpec((1, LANES), lambda b: (0, 0))],
        out_specs=[col, pl.BlockSpec((None, 1, LANES), lambda b: (b, 0, 0))],
        compiler_params=_params(), name="forget_bwd",
    )(dfq, dfk, f, bias)


def _mix_out(x1, yp, o, ona, woa, wob):
    t, d = x1.shape
    width = o.shape[1]
    tm = min(512, t)

    def body(x_ref, yp_ref, o_ref, on_ref, wa_ref, wb_ref, x2_ref, ya_ref):
        of = o_ref[...]
        ya = ((of * _rms(of)) * on_ref[...]).astype(BF16)
        ya_ref[...] = ya
        x2_ref[...] = x_ref[...] + (_dot(yp_ref[...], wa_ref[...]) + _dot(ya, wb_ref[...]))

    row = pl.BlockSpec((tm, d), lambda i: (i, 0))
    half = pl.BlockSpec((tm, width), lambda i: (i, 0))
    wspec = pl.BlockSpec((width, d), lambda i: (0, 0))
    return pl.pallas_call(
        body, out_shape=[jax.ShapeDtypeStruct((t, d), F32), jax.ShapeDtypeStruct((t, width), BF16)],
        grid=(t // tm,), in_specs=[row, half, half, pl.BlockSpec((1, width), lambda i: (0, 0)), wspec, wspec],
        out_specs=[row, half], compiler_params=_params(), name="mix_out",
    )(x1, yp, o, ona, woa, wob)


def _mix_out_bwd(dx2, o, yp, ya, ona, woa, wob, plan=None):
    t, d = dx2.shape
    width = o.shape[1]
    tm = min(512, t)
    nt = t // tm

    def body(dx_ref, o_ref, yp_ref, ya_ref, on_ref, wa_ref, wb_ref, dyp_ref, do_ref, dl_ref, dwa_ref, dwb_ref, don_ref):
        @pl.when(pl.program_id(0) == 0)
        def _():
            dwa_ref[...] = jnp.zeros_like(dwa_ref)
            dwb_ref[...] = jnp.zeros_like(dwb_ref)

        dxb = dx_ref[...].astype(BF16)
        dwa_ref[...] += _dot_tn(yp_ref[...], dxb)
        dwb_ref[...] += _dot_tn(ya_ref[...], dxb)
        dyp_ref[...] = _dot_nt(dxb, wa_ref[...])
        of = o_ref[...]
        dov, dgr = _rms_bwd(of, _rms(of), on_ref[...], _dot_nt(dxb, wb_ref[...]))
        don_ref[...] = jnp.sum(dgr, axis=0, keepdims=True)
        do_ref[...] = dov.astype(BF16)
        lo = _head_masks()
        prod = dov * of
        delta = jnp.zeros((tm, LANES), F32)
        for blk in range(width // LANES):
            pb = prod[:, blk * LANES:(blk + 1) * LANES]
            delta = delta + _put_lane(jnp.sum(jnp.where(lo, pb, 0.0), axis=1, keepdims=True), 2 * blk)
            delta = delta + _put_lane(jnp.sum(jnp.where(lo, 0.0, pb), axis=1, keepdims=True), 2 * blk + 1)
        for c in range(tm // ATT_SUB):
            dl_ref[c] = delta[c * ATT_SUB:(c + 1) * ATT_SUB, :].T[0:8, :]

    row = pl.BlockSpec((tm, d), lambda i: (i, 0))
    half = pl.BlockSpec((tm, width), lambda i: (i, 0))
    wspec = pl.BlockSpec((width, d), lambda i: (0, 0))
    return _pallas(
        body, name="mix_out_bwd", args=[dx2, o, yp, ya, ona, woa, wob],
        out_shape=[jax.ShapeDtypeStruct((t, width), F32), jax.ShapeDtypeStruct((t, width), BF16),
                   jax.ShapeDtypeStruct((t // ATT_SUB, 8, ATT_SUB), F32), jax.ShapeDtypeStruct((width, d), F32),
                   jax.ShapeDtypeStruct((width, d), F32), jax.ShapeDtypeStruct((nt, 1, width), F32)],
        grid=(nt,),
        in_specs=[row, half, half, half, pl.BlockSpec((1, width), lambda i: (0, 0)), wspec, wspec],
        out_specs=[half, half, pl.BlockSpec((tm // ATT_SUB, 8, ATT_SUB), lambda i: (i, 0, 0)), wspec, wspec,
                   pl.BlockSpec((None, 1, width), lambda i: (i, 0, 0))], plan=plan)


def _mix_in_bwd(dx2, x1, gain, hm, dpv, dqh, q, dkh, k, dv, df, qn, kn, wt):
    t, d = x1.shape
    width = q.shape[1]
    pool_width = dpv.shape[1]
    tm = min(512, t)
    nt = t // tm
    scale = HEAD_DIM ** -0.5
    c_q, c_k, c_v = pool_width, pool_width + width, pool_width + 2 * width
    c_f = c_v + width

    def body(dx2_ref, x_ref, g_ref, hm_ref, dpv_ref, dqh_ref, q_ref, dkh_ref, k_ref, dv_ref, df_ref, qn_ref, kn_ref,
             wt_ref, dx_ref, dxh_ref, dwt_ref, dg_ref, dqn_ref, dkn_ref):
        @pl.when(pl.program_id(0) == 0)
        def _():
            dwt_ref[...] = jnp.zeros_like(dwt_ref)

        lo = _head_masks()
        for part, rows in enumerate(_row_halves(tm)):
            def put(ref, sl, value):
                ref[:, sl] = value if part == 0 else ref[:, sl] + value

            hm = hm_ref[rows, :]
            pieces = [(0, dpv_ref[rows, :])]
            for c0, raw_ref, dh_ref, n_ref, dn_ref, mul in ((c_q, q_ref, dqh_ref, qn_ref, dqn_ref, scale),
                                                           (c_k, k_ref, dkh_ref, kn_ref, dkn_ref, 1.0)):
                cols = []
                for blk in range(width // LANES):
                    sl = slice(blk * LANES, (blk + 1) * LANES)
                    xb = raw_ref[rows, sl]
                    gb = dh_ref[rows, sl] * mul
                    r = _head_rms(xb, lo)
                    xh = xb * r
                    dyg = gb * n_ref[:, sl]
                    cols.append((r * (dyg - xh * _head_mean(dyg * xh, lo))).astype(BF16))
                    put(dn_ref, sl, jnp.sum(gb * xh, axis=0, keepdims=True))
                pieces.append((c0, jnp.concatenate(cols, axis=1)))
            pieces.append((c_v, dv_ref[rows, :]))
            pieces.append((c_f, df_ref[rows, :].astype(BF16)))
            dhm = jnp.zeros((tm // 2, d), F32)
            for c0, piece in pieces:
                dwt_ref[c0:c0 + piece.shape[1], :] += _dot_tn(piece, hm)
                dhm = dhm + _dot(piece, wt_ref[c0:c0 + piece.shape[1], :])
            xf = x_ref[rows, :]
            dxn, dgr = _rms_bwd(xf, _rms(xf), g_ref[...], dhm)
            dx = dx2_ref[rows, :] + dxn
            dx_ref[rows, :] = dx
            dxh_ref[rows, :] = (0.5 * dx).astype(BF16)
            put(dg_ref, slice(None), jnp.sum(dgr, axis=0, keepdims=True))

    row = pl.BlockSpec((tm, d), lambda i: (i, 0))
    half = pl.BlockSpec((tm, width), lambda i: (i, 0))
    const = lambda shape: pl.BlockSpec(shape, lambda i: (0, 0))
    pvec = lambda n: pl.BlockSpec((None, 1, n), lambda i: (i, 0, 0))
    return pl.pallas_call(
        body,
        out_shape=[jax.ShapeDtypeStruct((t, d), F32), jax.ShapeDtypeStruct((t, d), BF16), jax.ShapeDtypeStruct(wt.shape, F32),
                   jax.ShapeDtypeStruct((nt, 1, d), F32),
                   jax.ShapeDtypeStruct((nt, 1, width), F32), jax.ShapeDtypeStruct((nt, 1, width), F32)],
        grid=(nt,),
        in_specs=[row, row, const((1, d)), row, pl.BlockSpec((tm, pool_width), lambda i: (i, 0)), half, half, half, half,
                  half, pl.BlockSpec((tm, LANES), lambda i: (i, 0)), const((1, width)), const((1, width)),
                  const(wt.shape)],
        out_specs=[row, row, const(wt.shape), pvec(d), pvec(width), pvec(width)],
        compiler_params=_params(), name="mix_in_bwd",
    )(dx2, x1, gain, hm, dpv, dqh, q, dkh, k, dv, df, qn, kn, wt)


def _mesh_pos():
    return lax.axis_index("x"), lax.axis_index("y"), lax.axis_index("c")


def _other_chips(x, y):
    return [(1 - x, y), (x, 1 - y), (1 - x, 1 - y)]


def _remote(src, dst, send_sem, recv_sem, device):
    return pltpu.make_async_remote_copy(src_ref=src, dst_ref=dst, send_sem=send_sem, recv_sem=recv_sem,
                                        device_id=device, device_id_type=pl.DeviceIdType.MESH)


def _half_rows(n_rows, which):
    half = n_rows // 2
    return pl.ds(pl.multiple_of(which * half, 8), half)


def _row_block(rows, cols, itemsize=4):
    rb = rows
    while rb * cols * itemsize > (1 << 20) and rb % 32 == 0:
        rb //= 2
    return rb


def _place_cast(ws, chip, tag):
    n = len(ws)
    rows, cols = ws[0].shape
    rb = _row_block(rows, cols)

    def body(k_ref, *refs):
        for w_ref, o_ref in zip(refs[:n], refs[n:]):
            o_ref[...] = w_ref[...].astype(BF16)

    return pl.pallas_call(
        body, out_shape=[jax.ShapeDtypeStruct((N_CHIPS, rows, cols), BF16)] * n,
        grid_spec=pltpu.PrefetchScalarGridSpec(
            num_scalar_prefetch=1, grid=(rows // rb,),
            in_specs=[pl.BlockSpec((rb, cols), lambda i, k: (i, 0))] * n,
            out_specs=[pl.BlockSpec((None, rb, cols), lambda i, k: (k[0], i, 0))] * n),
        compiler_params=_params(), name="place_" + tag,
    )(chip, *ws)


class _Plan:
    def __init__(self, ins, outs, alias, sems, start, finish, middle=None):
        self.ins, self.outs, self.alias, self.sems = ins, outs, alias, sems
        self.start, self.middle, self.finish = start, middle, finish


def _merge_plans(a, b):
    ni, no, ns = len(a.ins), len(a.outs), len(a.sems)
    alias = dict(a.alias)
    alias.update({ni + i: no + o for i, o in b.alias.items()})

    def both(which):
        stage_a, stage_b = getattr(a, which), getattr(b, which)
        if stage_a is None and stage_b is None:
            return None

        def run(ins, outs, sems):
            if stage_a is not None:
                stage_a(ins[:ni], outs[:no], sems[:ns])
            if stage_b is not None:
                stage_b(ins[ni:], outs[no:], sems[ns:])
        return run

    return _Plan(list(a.ins) + list(b.ins), list(a.outs) + list(b.outs), alias, list(a.sems) + list(b.sems),
                 both("start"), both("finish"), both("middle"))


def _run_plan(plan, name):
    n_in, n_out = len(plan.ins), len(plan.outs)

    def body(*refs):
        parts = refs[:n_in], refs[n_in:n_in + n_out], refs[n_in + n_out:]
        plan.start(*parts)
        if plan.middle is not None:
            plan.middle(*parts)
        plan.finish(*parts)

    return pl.pallas_call(
        body, out_shape=plan.outs, in_specs=[ANY] * n_in, out_specs=[ANY] * n_out, scratch_shapes=plan.sems,
        input_output_aliases=plan.alias, name=name,
    )(*plan.ins)


def _pallas(body, *, name, args, in_specs, out_shape, out_specs, grid, scratch_shapes=(), plan=None, aliases=None):
    n_in, n_out, n_scr = len(args), len(out_shape), len(scratch_shapes)
    plan = plan or _Plan([], [], {}, [], None, None)
    p_in, p_out = len(plan.ins), len(plan.outs)

    def carrying(*refs):
        ins, p_ins = refs[:n_in], refs[n_in:n_in + p_in]
        o0 = n_in + p_in
        outs, p_outs = refs[o0:o0 + n_out], refs[o0 + n_out:o0 + n_out + p_out]
        s0 = o0 + n_out + p_out
        scr, p_sems = refs[s0:s0 + n_scr], refs[s0 + n_scr:]
        ids = [pl.program_id(a) for a in range(len(grid))]

        if plan.start is not None:
            @pl.when(functools.reduce(jnp.logical_and, [i == 0 for i in ids]))
            def _():
                plan.start(p_ins, p_outs, p_sems)

        body(*ins, *outs, *scr)

        if plan.middle is not None:
            step, n_steps = 0, 1
            for i, g in zip(ids, grid):
                step, n_steps = step * g + i, n_steps * g

            @pl.when(step == (3 * n_steps) // 4)
            def _():
                plan.middle(p_ins, p_outs, p_sems)

        if plan.finish is not None:
            @pl.when(functools.reduce(jnp.logical_and, [i == g - 1 for i, g in zip(ids, grid)]))
            def _():
                plan.finish(p_ins, p_outs, p_sems)

    aliases = dict(aliases or {})
    aliases.update({n_in + i: n_out + o for i, o in plan.alias.items()})
    res = pl.pallas_call(
        carrying, out_shape=list(out_shape) + list(plan.outs), grid=grid,
        in_specs=list(in_specs) + [ANY] * p_in, out_specs=list(out_specs) + [ANY] * p_out,
        scratch_shapes=list(scratch_shapes) + list(plan.sems),
        input_output_aliases=aliases, compiler_params=_params(), name=name,
    )(*args, *plan.ins)
    return list(res[:n_out]), list(res[n_out:])


def _plan_gather(stacks):
    n = len(stacks)
    relations = range(3)

    def ici_copies(outs, sems):
        x, y, c = _mesh_pos()
        chips = _other_chips(x, y)
        cps = []
        for w in range(n):
            own = outs[w].at[2 * x + y, _half_rows(stacks[w].shape[1], c)]
            cps += [_remote(own, own, sems[0].at[w, j], sems[1].at[w, j], (*chips[j], c)) for j in relations]
        return cps

    def start(ins, outs, sems):
        for cp in ici_copies(outs, sems):
            cp.start()

    def forwards(outs, sems, core):
        x, y, c = _mesh_pos()
        slots = [2 * cx + cy for cx, cy in _other_chips(x, y)]
        cps = []
        for w in range(n):
            rows = _half_rows(stacks[w].shape[1], core)
            for j in relations:
                landed = outs[w].at[slots[j], rows]
                cps.append((_remote(landed, landed, sems[0].at[w, j], sems[1].at[w, j], (x, y, 1 - c)),
                            _remote(landed, landed, sems[2].at[w, j], sems[3].at[w, j], (x, y, 1 - c))))
        return cps

    def middle(ins, outs, sems):
        c = _mesh_pos()[2]
        for arrival, forward in forwards(outs, sems, c):
            arrival.wait_recv()
            forward.start()

    def finish(ins, outs, sems):
        c = _mesh_pos()[2]
        for _, forward in forwards(outs, sems, 1 - c):
            forward.wait_recv()
        for cp in ici_copies(outs, sems) + [forward for _, forward in forwards(outs, sems, c)]:
            cp.wait_send()

    return _Plan(stacks, [jax.ShapeDtypeStruct(s.shape, s.dtype) for s in stacks], {w: w for w in range(n)},
                 [pltpu.SemaphoreType.DMA((n, 3))] * 4, start, finish, middle)


def _plan_gather_relay(stacks):
    n = len(stacks)

    def finish(ins, outs, sems):
        send, recv, relay_send, relay_recv, d2d_send, d2d_recv = sems
        x, y, c = _mesh_pos()
        sibling = (x, y, 1 - c)
        near = [(1 - x, y), (x, 1 - y)]
        far = 2 * (1 - x) + (1 - y)
        started = []

        def go(cp):
            cp.start()
            started.append(cp)

        def piece(w, slot, core, quarter=None):
            rh = stacks[w].shape[1] // 2
            if quarter is None:
                return outs[w].at[slot, _half_rows(2 * rh, core)]
            return outs[w].at[slot, pl.ds(pl.multiple_of(core * rh + quarter * (rh // 2), 8), rh // 2)]

        for w in range(n):
            own = piece(w, 2 * x + y, c)
            for j, chip in enumerate(near):
                go(_remote(own, own, send.at[w, j], recv.at[w, j], (*chip, c)))
        for w in range(n):
            for j, (cx, cy) in enumerate(near):
                landed = piece(w, 2 * cx + cy, c)
                _remote(landed, landed, send.at[w, j], recv.at[w, j], sibling).wait_recv()
                part = piece(w, 2 * cx + cy, c, quarter=j)
                go(_remote(part, part, relay_send.at[w, j], relay_recv.at[w, j], (*near[1 - j], c)))
                go(_remote(landed, landed, d2d_send.at[w, j], d2d_recv.at[w, j], sibling))
        for w in range(n):
            for j in range(2):
                part = piece(w, far, c, quarter=j)
                _remote(part, part, relay_send.at[w, j], relay_recv.at[w, j], sibling).wait_recv()
            landed = piece(w, far, c)
            go(_remote(landed, landed, d2d_send.at[w, 2], d2d_recv.at[w, 2], sibling))
        for w in range(n):
            for j, slot in enumerate([2 * cx + cy for cx, cy in near] + [far]):
                landed = piece(w, slot, 1 - c)
                _remote(landed, landed, d2d_send.at[w, j], d2d_recv.at[w, j], sibling).wait_recv()
        for cp in started:
            cp.wait_send()

    return _Plan(stacks, [jax.ShapeDtypeStruct(s.shape, s.dtype) for s in stacks], {w: w for w in range(n)},
                 [pltpu.SemaphoreType.DMA((n, 2))] * 4 + [pltpu.SemaphoreType.DMA((n, 3))] * 2,
                 lambda ins, outs, sems: None, finish)


def _plan_sibling_halves(gs):
    n = len(gs)

    def copies(ins, outs, sems):
        x, y, c = _mesh_pos()
        return [_remote(ins[w].at[:, _half_rows(gs[w].shape[1], 1 - c), :], outs[w], sems[0].at[w], sems[1].at[w],
                        (x, y, 1 - c)) for w in range(n)]

    def start(ins, outs, sems):
        for cp in copies(ins, outs, sems):
            cp.start()

    def finish(ins, outs, sems):
        for cp in copies(ins, outs, sems):
            cp.wait()

    return _Plan(gs, [jax.ShapeDtypeStruct((g.shape[0], g.shape[1] // 2, g.shape[2]), g.dtype) for g in gs], {},
                 [pltpu.SemaphoreType.DMA((n,))] * 2, start, finish)


def _plan_chip_exchange(ps):
    n = len(ps)

    def copies(ins, outs, sems):
        x, y, c = _mesh_pos()
        return [_remote(ins[w].at[2 * cx + cy], outs[w].at[j], sems[0].at[w, j], sems[1].at[w, j], (cx, cy, c))
                for w in range(n) for j, (cx, cy) in enumerate(_other_chips(x, y))]

    def start(ins, outs, sems):
        for cp in copies(ins, outs, sems):
            cp.start()

    def finish(ins, outs, sems):
        for cp in copies(ins, outs, sems):
            cp.wait()

    return _Plan(ps, [jax.ShapeDtypeStruct((3,) + p.shape[1:], p.dtype) for p in ps], {},
                 [pltpu.SemaphoreType.DMA((n, 3))] * 2, start, finish)


def _plan_sibling_share(gs):
    n = len(gs)

    def copies(outs, sems, which):
        x, y, c = _mesh_pos()
        cps = []
        for w in range(n):
            rows = outs[w].at[_half_rows(gs[w].shape[0], c if which == "mine" else 1 - c)]
            cps.append(_remote(rows, rows, sems[0].at[w], sems[1].at[w], (x, y, 1 - c)))
        return cps

    def start(ins, outs, sems):
        for cp in copies(outs, sems, "mine"):
            cp.start()

    def finish(ins, outs, sems):
        for cp in copies(outs, sems, "mine"):
            cp.wait_send()
        for cp in copies(outs, sems, "theirs"):
            cp.wait_recv()

    return _Plan(gs, [jax.ShapeDtypeStruct(g.shape, g.dtype) for g in gs], {w: w for w in range(n)},
                 [pltpu.SemaphoreType.DMA((n,))] * 2, start, finish)


def _same_shape_groups(arrays):
    groups = {}
    for i, a in enumerate(arrays):
        groups.setdefault(a.shape, []).append(i)
    return list(groups.values())


def _add_sibling(gs, r1s, ids, tag):
    n = len(gs)
    nch, rh, cols = r1s[0].shape

    def body(ids_ref, *refs):
        for g_ref, r_ref, o_ref in zip(refs[:n], refs[n:2 * n], refs[2 * n:]):
            o_ref[...] = (g_ref[...] + r_ref[...]).astype(BF16)

    blk = lambda fn: pl.BlockSpec((None, rh, cols), fn)
    return pl.pallas_call(
        body, out_shape=[jax.ShapeDtypeStruct(r1s[0].shape, BF16)] * n,
        grid_spec=pltpu.PrefetchScalarGridSpec(
            num_scalar_prefetch=1, grid=(nch,),
            in_specs=[blk(lambda k, ids: (k, ids[1], 0))] * n + [blk(lambda k, ids: (k, 0, 0))] * n,
            out_specs=[blk(lambda k, ids: (k, 0, 0))] * n),
        compiler_params=_params(), name="add_sibling_" + tag,
    )(ids, *gs, *r1s)


def _add_chips(gs, r1s, r2s, ids, tag):
    n = len(gs)
    _, rh, cols = r1s[0].shape
    nb = 2 if rh % 32 == 0 else 1
    rb = rh // nb

    def body(ids_ref, *refs):
        for g_ref, r1_ref, r2_ref, o_ref in zip(refs[:n], refs[n:2 * n], refs[2 * n:3 * n], refs[3 * n:]):
            own = g_ref[...] + r1_ref[...]
            o_ref[...] = ((own + r2_ref[0].astype(F32)) + r2_ref[1].astype(F32)) + r2_ref[2].astype(F32)

    return pl.pallas_call(
        body, out_shape=[jax.ShapeDtypeStruct((2 * rh, cols), F32)] * n,
        grid_spec=pltpu.PrefetchScalarGridSpec(
            num_scalar_prefetch=1, grid=(nb,),
            in_specs=[pl.BlockSpec((None, rb, cols), lambda i, ids: (ids[0], ids[1] * nb + i, 0))] * n
            + [pl.BlockSpec((None, rb, cols), lambda i, ids: (ids[0], i, 0))] * n
            + [pl.BlockSpec((3, rb, cols), lambda i, ids: (0, i, 0))] * n,
            out_specs=[pl.BlockSpec((rb, cols), lambda i, ids: (ids[1] * nb + i, 0))] * n),
        compiler_params=_params(), name="add_chips_" + tag,
    )(ids, *gs, *r1s, *r2s)


VEC_ROWS = 8


N_DEVICES = 8


def _small_pack(part, d, width):
    names = ("ffn1_norm", "mix_norm", "ffn2_norm", "pool_scale", "out_norm_pool", "out_norm_attn", "qn", "kn", "b_forget",
             "pool_w", "loss")
    args = [part[k] for k in names]
    pw_shape = part["pool_w"].shape[1:]

    def body(g1_ref, gm_ref, g2_ref, ps_ref, onp_ref, ona_ref, qn_ref, kn_ref, bf_ref, pw_ref, loss_ref, vbuf, pbuf):
        lo = _head_masks()

        def fold_heads(ref):
            v = jnp.sum(ref[...], axis=0)
            acc = jnp.zeros((VEC_ROWS, LANES), F32)
            for blk in range(width // LANES):
                vb = jnp.broadcast_to(v[:, blk * LANES:(blk + 1) * LANES], (VEC_ROWS, LANES))
                acc = acc + vb + pltpu.roll(vb, HEAD_DIM, 1)
            return jnp.where(lo, acc, 0.0)[0:1, :]

        vbuf[0] = jnp.zeros((VEC_ROWS, d), F32)
        vbuf[0, 0:1, :] = jnp.sum(g1_ref[...], axis=0)
        vbuf[0, 1:2, :] = jnp.sum(gm_ref[...], axis=0)
        vbuf[0, 2:3, :] = jnp.sum(g2_ref[...], axis=0)
        vbuf[0, 5:6, 0:LANES] = jnp.sum(loss_ref[...], axis=0)[0:1, :]
        vbuf[0, 3:4, 0:width] = jnp.sum(ps_ref[...], axis=0)
        vbuf[0, 3:4, width:2 * width] = jnp.sum(onp_ref[...], axis=0)
        vbuf[0, 4:5, 0:width] = jnp.sum(ona_ref[...], axis=0)
        vbuf[0, 4:5, width:width + LANES] = fold_heads(qn_ref)
        vbuf[0, 4:5, width + LANES:width + 2 * LANES] = fold_heads(kn_ref)
        vbuf[0, 4:5, width + 2 * LANES:width + 3 * LANES] = jnp.sum(bf_ref[...], axis=0)
        pbuf[0] = jnp.sum(pw_ref[...], axis=0)

    return pl.pallas_call(
        body, out_shape=[jax.ShapeDtypeStruct((N_DEVICES, VEC_ROWS, d), F32), jax.ShapeDtypeStruct((N_DEVICES,) + pw_shape, F32)],
        in_specs=[VM] * len(args), out_specs=[VM, VM], compiler_params=_params(), name="small_pack",
    )(*args)


def _plan_all_to_all(stacks):
    n = len(stacks)

    def copies(outs, sems):
        x, y, c = _mesh_pos()
        cps = []
        for r in range(1, N_DEVICES):
            peer = (x if not r & 4 else 1 - x, y if not r & 2 else 1 - y, c if not r & 1 else 1 - c)
            cps += [_remote(outs[w].at[0], outs[w].at[r], sems[0].at[w, r - 1], sems[1].at[w, r - 1], peer) for w in range(n)]
        return cps

    def start(ins, outs, sems):
        for cp in copies(outs, sems):
            cp.start()

    def finish(ins, outs, sems):
        for cp in copies(outs, sems):
            cp.wait()

    return _Plan(stacks, [jax.ShapeDtypeStruct(s.shape, s.dtype) for s in stacks], {w: w for w in range(n)},
                 [pltpu.SemaphoreType.DMA((n, N_DEVICES - 1))] * 2, start, finish)


def _small_sum(vstack, pstack, me):
    def body(me_ref, vbuf, pbuf, vec_ref, pw_ref):
        vec = vbuf[me_ref[0]]
        pw = pbuf[me_ref[0]]
        for dev in range(1, N_DEVICES):
            vec = vec + vbuf[jnp.bitwise_xor(me_ref[0], dev)]
            pw = pw + pbuf[jnp.bitwise_xor(me_ref[0], dev)]
        vec_ref[...] = vec
        pw_ref[...] = pw

    full = lambda s: pl.BlockSpec(s.shape, lambda i, me: (0,) * len(s.shape))
    outs = [jax.ShapeDtypeStruct(vstack.shape[1:], F32), jax.ShapeDtypeStruct(pstack.shape[1:], F32)]
    return pl.pallas_call(
        body, out_shape=outs,
        grid_spec=pltpu.PrefetchScalarGridSpec(num_scalar_prefetch=1, grid=(1,), in_specs=[full(vstack), full(pstack)],
                                               out_specs=[full(o) for o in outs]),
        compiler_params=_params(), name="small_sum",
    )(me, vstack, pstack)


def _adamw(ws, gs, ms, vs, tag):
    n = len(ws)
    rows, cols = ws[0].shape
    rb = rows
    while rb * cols * 4 * n > (1 << 20) and rb % 16 == 0:
        rb //= 2

    def body(*refs):
        for j in range(n):
            w_ref, g_ref, m_ref, v_ref = (refs[k * n + j] for k in range(4))
            go_ref, d_ref, mo_ref, vo_ref = (refs[(4 + k) * n + j] for k in range(4))
            gv = g_ref[...]
            go_ref[...] = gv
            m2 = ADAM_B1 * m_ref[...] + (1.0 - ADAM_B1) * gv
            v2 = ADAM_B2 * v_ref[...] + (1.0 - ADAM_B2) * (gv * gv)
            m_hat = m2 / (1.0 - ADAM_B1 ** ADAM_STEP)
            v_hat = v2 / (1.0 - ADAM_B2 ** ADAM_STEP)
            d_ref[...] = -ADAM_LR * (m_hat / (jnp.sqrt(v_hat) + ADAM_EPS) + ADAM_WD * w_ref[...])
            mo_ref[...] = m2
            vo_ref[...] = v2

    spec = pl.BlockSpec((rb, cols), lambda i: (i, 0))
    res, _ = _pallas(
        body, name="adamw_" + tag, args=[*ws, *gs, *ms, *vs], out_shape=[jax.ShapeDtypeStruct(ws[0].shape, F32)] * (4 * n),
        grid=(rows // rb,), in_specs=[spec] * (4 * n), out_specs=[spec] * (4 * n))
    return [tuple(res[k * n + j] for k in range(4)) for j in range(n)]


def _pack_vec(p, d, width):
    pad = lambda v: jnp.pad(v, (0, LANES - v.shape[0]))
    row3 = jnp.concatenate([p["pool_scale"], p["out_norm_pool"]])
    row4 = jnp.concatenate([p["out_norm_attn"], pad(p["q_norm"]), pad(p["k_norm"]), pad(p["b_forget"]),
                            jnp.zeros((d - width - 3 * LANES,), F32)])
    rows = [p["ffn1_norm"], p["mix_norm"], p["ffn2_norm"], row3, row4]
    return jnp.pad(jnp.stack(rows), ((0, VEC_ROWS - len(rows)), (0, 0)))


def _unpack_vec(vec, width):
    return dict(ffn1_norm=vec[0], mix_norm=vec[1], ffn2_norm=vec[2], pool_scale=vec[3, :width],
                out_norm_pool=vec[3, width:2 * width], out_norm_attn=vec[4, :width],
                q_norm=vec[4, width:width + HEAD_DIM], k_norm=vec[4, width + LANES:width + LANES + HEAD_DIM],
                b_forget=vec[4, width + 2 * LANES:width + 2 * LANES + N_HEADS])


WEIGHT_NAMES = ("ffn1_norm", "ffn1_w_gate", "ffn1_w_up", "ffn1_w_down", "mix_norm", "w_in", "b_forget", "pool_w",
                "pool_scale", "q_norm", "k_norm", "out_norm_pool", "out_norm_attn", "w_out", "ffn2_norm",
                "ffn2_w_gate", "ffn2_w_up", "ffn2_w_down")
BIG_NAMES = ("ffn1_w_gate", "ffn1_w_up", "ffn1_w_down", "w_in", "w_out", "ffn2_w_gate", "ffn2_w_up", "ffn2_w_down")
TRANSPOSED_NAMES = ("ffn1_w_gate", "ffn1_w_up", "w_in", "ffn2_w_gate", "ffn2_w_up")
FFN1_NAMES = ("ffn1_w_gate", "ffn1_w_up", "ffn1_w_down")
MIX_NAMES = ("w_in", "w_out")
FFN2_NAMES = ("ffn2_w_gate", "ffn2_w_up", "ffn2_w_down")


def kernel(x, ffn1_norm, ffn1_w_gate, ffn1_w_up, ffn1_w_down, mix_norm, w_in, b_forget, pool_w, pool_scale, q_norm, k_norm, out_norm_pool, out_norm_attn, w_out, ffn2_norm, ffn2_w_gate, ffn2_w_up, ffn2_w_down, loss_target, m_ffn1_norm, m_ffn1_w_gate, m_ffn1_w_up, m_ffn1_w_down, m_mix_norm, m_w_in, m_b_forget, m_pool_w, m_pool_scale, m_q_norm, m_k_norm, m_out_norm_pool, m_out_norm_attn, m_w_out, m_ffn2_norm, m_ffn2_w_gate, m_ffn2_w_up, m_ffn2_w_down, v_ffn1_norm, v_ffn1_w_gate, v_ffn1_w_up, v_ffn1_w_down, v_mix_norm, v_w_in, v_b_forget, v_pool_w, v_pool_scale, v_q_norm, v_k_norm, v_out_norm_pool, v_out_norm_attn, v_w_out, v_ffn2_norm, v_ffn2_w_gate, v_ffn2_w_up, v_ffn2_w_down):
    given = dict(locals())
    w = {n: given[n] for n in WEIGHT_NAMES}
    m = {n: given["m_" + n] for n in WEIGHT_NAMES}
    v = {n: given["v_" + n] for n in WEIGHT_NAMES}
    n_batch, seq, d = x.shape
    width = pool_scale.shape[0]
    in_rows = w_in.shape[1]
    in_cols = N_CHIPS * in_rows
    in_pad = -(-in_rows // 32) * 32
    in_cols_pad = in_cols - N_HEADS + LANES

    work = lambda a, n: a.T if n in TRANSPOSED_NAMES else a
    exchanged = lambda a, n: jnp.pad(a, ((0, in_pad - in_rows), (0, 0))) if n == "w_in" else a

    mesh_x, mesh_y, mesh_c = _mesh_pos()
    ids = jnp.stack([2 * mesh_x + mesh_y, mesh_c]).astype(jnp.int32)

    row = lambda a: a.reshape(1, -1)
    g1, gm, g2, ps, onp, ona = (row(a) for a in (ffn1_norm, mix_norm, ffn2_norm, pool_scale, out_norm_pool, out_norm_attn))
    qn, kn = row(jnp.tile(q_norm, N_HEADS)), row(jnp.tile(k_norm, N_HEADS))
    bf = row(jnp.pad(b_forget, (0, LANES - N_HEADS)))
    pwb = pool_w.astype(BF16)
    xf, tgt = x.reshape(n_batch * seq, d), loss_target.reshape(n_batch * seq, d)

    def grouped(call, names, *lists):
        out = [None] * len(names)
        for idx in _same_shape_groups(lists[0]):
            res = call(*[[lst[i] for i in idx] for lst in lists], names[idx[0]])
            for i, r in zip(idx, res):
                out[i] = r
        return out

    placed = dict(zip(BIG_NAMES, grouped(lambda ws, tag: _place_cast(ws, ids, tag), BIG_NAMES,
                                         [exchanged(work(w[n], n), n) for n in BIG_NAMES])))
    wg1, wu1, wd1 = _run_plan(_plan_gather_relay([placed[n] for n in FFN1_NAMES]), "gather_ffn1")
    (x1, h1, a1, b1, s1), (w_in_all, w_out_all, wd2) = _ffn_fwd(
        xf, g1, wg1, wu1, wd1, plan=_plan_gather([placed[n] for n in MIX_NAMES + FFN2_NAMES[2:]]))
    w_in_t = jnp.pad(w_in_all[:, :in_rows].reshape(in_cols, d), ((0, in_cols_pad - in_cols), (0, 0)))
    w_out_full = w_out_all.reshape(N_CHIPS * w_out.shape[0], d)
    woa, wob = w_out_full[:width], w_out_full[width:]

    hm, pv, q, k, qh, kh, vb, f = _mix_proj(x1, gm, w_in_t, qn, kn, width, width)
    qa, ka = _forget_prefix(f, bf, qh, kh, n_batch, seq)
    yp = _pool_fwd(pv, pwb, ps, onp, n_batch, seq)
    (o, lse), (wg2, wu2) = _attn_fwd(qa, ka, vb, n_batch, seq, plan=_plan_gather([placed[n] for n in FFN2_NAMES[:2]]))
    x2, ya = _mix_out(x1, yp, o, ona, woa, wob)
    (dy, h2, a2, b2, s2, lpart, dyh), _ = _ffn_fwd(x2, g2, wg2, wu2, wd2, target=tgt)

    def to_chips(gs, arrived, tags):
        return grouped(lambda g, r, tag: _add_sibling(g, r, ids, tag), tags, gs, arrived)

    def own_rows(gs, from_sibling, from_chips, tags):
        return grouped(lambda g, ra, rb, tag: _add_chips(g, ra, rb, ids, tag), tags, gs, from_sibling, from_chips)

    (dx2, da2, db2, dg2), _ = _ffn_bwd_x(dy, x2, g2, a2, b2, wg2, wu2, wd2, "ffn2_bwd_x")
    dw2, _ = _ffn_bwd_w([(da2, h2), (db2, h2), (s2, dyh)], "ffn2_bwd_w")
    (dyp, do, delta, dwoa, dwob, dona), sib2 = _mix_out_bwd(dx2, o, yp, ya, ona, woa, wob, plan=_plan_sibling_halves(dw2))
    dpv, dpw, dps, donp = _pool_bwd(pv, dyp, pwb, ps, onp, n_batch, seq)
    (dqh, dkh, dv, dfq, dfk), chips2 = _attn_bwd(qa, ka, vb, do, lse, delta, n_batch, seq,
                                                 plan=_plan_chip_exchange(to_chips(dw2, sib2, FFN2_NAMES)))
    df, dbf = _forget_bwd(dfq, dfk, f, bf, n_batch, seq)
    dx1, dx1h, dw_in_t, dgm, dqn, dkn = _mix_in_bwd(dx2, x1, gm, hm, dpv, dqh, q, dkh, k, dv, df, qn, kn, w_in_t)
    in_base = [in_rows * k // 8 * 8 for k in range(N_CHIPS)]
    d_w_in = jnp.stack([dw_in_t[b:b + in_pad] for b in in_base])
    d_w_out = jnp.concatenate([dwoa, dwob], axis=0).reshape(N_CHIPS, w_out.shape[0], d)
    dwm = [d_w_in, d_w_out]
    down = FFN1_NAMES[2:]
    dwd1, sibm = _ffn_bwd_w([(s1, dx1h)], "ffn1_bwd_w_down", plan=_plan_sibling_halves(dwm))
    (da1, db1), arrived = _ffn_bwd_a(dx1h, a1, b1, wd1, "ffn1_bwd_a",
                                     plan=_merge_plans(_plan_sibling_halves(dwd1),
                                                       _plan_chip_exchange(to_chips(dwm, sibm, MIX_NAMES))))
    sibd, chipsm = arrived[:1], arrived[1:]
    gate, up = FFN1_NAMES[:1], FFN1_NAMES[1:2]
    dwg1, chipsd = _ffn_bwd_w([(da1, h1)], "ffn1_bwd_w_gate", plan=_plan_chip_exchange(to_chips(dwd1, sibd, down)))
    dwu1, sibg = _ffn_bwd_w([(db1, h1)], "ffn1_bwd_w_up", plan=_plan_sibling_halves(dwg1))
    n_tiles = (n_batch * seq) // min(FFN_TILE, n_batch * seq)
    first = max(n_tiles // 2, 1)
    begun, arrived = _ffn_bwd_h(dx1, xf, g1, da1, db1, wg1, wu1, "ffn1_bwd_h_first", (0, first),
                                plan=_merge_plans(_plan_sibling_halves(dwu1),
                                                  _plan_chip_exchange(to_chips(dwg1, sibg, gate))))
    sibu, chipsg = arrived[:1], arrived[1:]
    (gx, dg1), chipsu = _ffn_bwd_h(dx1, xf, g1, da1, db1, wg1, wu1, "ffn1_bwd_h_rest", (first, n_tiles), prev=begun,
                                   plan=_plan_chip_exchange(to_chips(dwu1, sibu, up)))

    part = dict(ffn1_norm=dg1, mix_norm=dgm, ffn2_norm=dg2, b_forget=dbf, pool_scale=dps, out_norm_pool=donp,
                out_norm_attn=dona, qn=dqn, kn=dkn, pool_w=dpw.reshape(n_batch, -1, pool_w.shape[-1]), loss=lpart)
    mine = (own_rows(dwg1, sibg, chipsg, gate) + own_rows(dwu1, sibu, chipsu, up) + own_rows(dwd1, sibd, chipsd, down)
            + own_rows(dwm, sibm, chipsm, MIX_NAMES) + own_rows(dw2, sib2, chips2, FFN2_NAMES))
    last = _run_plan(_merge_plans(_plan_sibling_share(mine), _plan_all_to_all(_small_pack(part, d, width))), "last_exchange")
    vstack, pstack = last[len(mine):]
    g_vec, g_pw = _small_sum(vstack, pstack, jnp.reshape(4 * mesh_x + 2 * mesh_y + mesh_c, (1,)).astype(jnp.int32))
    loss = g_vec[5, 0]
    reduced = dict(zip(FFN1_NAMES + MIX_NAMES + FFN2_NAMES, last[:len(mine)]))
    reduced["w_in"] = lax.dynamic_slice(reduced["w_in"], ((in_rows * ids[0]) % 8, 0), (in_rows, d))

    grads, delta, new_m, new_v = {}, {}, {}, {}
    for names in (FFN2_NAMES, FFN1_NAMES, ("w_in",), ("w_out",)):
        stepped = _adamw([work(w[n], n) for n in names], [reduced[n] for n in names], [work(m[n], n) for n in names],
                         [work(v[n], n) for n in names], names[0])
        for n, step in zip(names, stepped):
            grads[n], delta[n], new_m[n], new_v[n] = (work(a, n) for a in step)
    flat_pw = lambda a: a.reshape(-1, a.shape[-1])
    (_, d_pw, m_pw, v_pw), = _adamw([flat_pw(pool_w)], [g_pw], [flat_pw(m_pool_w)], [flat_pw(v_pool_w)], "pool_w")
    (_, d_vec, m_vec, v_vec), = _adamw([_pack_vec(w, d, width)], [g_vec], [_pack_vec(m, d, width)],
                                       [_pack_vec(v, d, width)], "vectors")
    grads.update(_unpack_vec(g_vec, width), pool_w=g_pw.reshape(pool_w.shape))
    delta.update(_unpack_vec(d_vec, width), pool_w=d_pw.reshape(pool_w.shape))
    new_m.update(_unpack_vec(m_vec, width), pool_w=m_pw.reshape(pool_w.shape))
    new_v.update(_unpack_vec(v_vec, width), pool_w=v_pw.reshape(pool_w.shape))
    return (loss, gx.reshape(x.shape), *[grads[n] for n in WEIGHT_NAMES], *[delta[n] for n in WEIGHT_NAMES],
            *[new_m[n] for n in WEIGHT_NAMES], *[new_v[n] for n in WEIGHT_NAMES])
```

```python
import functools

import jax
import jax.numpy as jnp
from jax import lax
from jax.experimental import pallas as pl
from jax.experimental.pallas import tpu as pltpu

F32 = jnp.float32
BF16 = jnp.bfloat16
EPS = 1e-6
NEG = -1e30
ADAM_LR = 0.001
ADAM_B1 = 0.9
ADAM_B2 = 0.999
ADAM_EPS = 1e-08
ADAM_WD = 0.01
ADAM_STEP = 10
POOL_WINDOWS = (2, 4, 8, 16)
HEAD_DIM = 64
N_HEADS = 8
LANES = 128
N_CHIPS = 4
ATT_BLOCK = 512
ATT_SUB = 128
FFN_TILE = 1024
FFN_STAGED_TILE = 512
VMEM_LIMIT = 62 * 1024 * 1024
ANY = pl.BlockSpec(memory_space=pl.ANY)
VM = pl.BlockSpec(memory_space=pltpu.VMEM)


def _params(**kw):
    return pltpu.CompilerParams(vmem_limit_bytes=VMEM_LIMIT, **kw)


def _dot(a, b):
    return jnp.dot(a, b, preferred_element_type=F32)


def _dot_nt(a, b):
    return lax.dot_general(a, b, (((1,), (1,)), ((), ())), preferred_element_type=F32)


def _dot_tn(a, b):
    return lax.dot_general(a, b, (((0,), (0,)), ((), ())), preferred_element_type=F32)


def _sigmoid(z):
    return 1.0 / (1.0 + jnp.exp(-z))


def _rms(xf):
    return lax.rsqrt(jnp.mean(xf * xf, axis=-1, keepdims=True) + EPS)


def _rms_bwd(xf, r, gain, dh):
    xh = xf * r
    dyg = dh * gain
    return r * (dyg - xh * jnp.mean(dyg * xh, axis=-1, keepdims=True)), dh * xh


def _total(v):
    return jnp.sum(jnp.sum(v, axis=1, keepdims=True), axis=0, keepdims=True)


def _ffn_fwd(x, gain, wg, wu, wd, target=None, plan=None):
    t, d = x.shape
    nch, fc, _ = wg.shape
    tm = min(FFN_TILE, t)
    nt = t // tm
    with_loss = target is not None

    def body(*refs):
        if with_loss:
            x_ref, g_ref, wg_ref, wu_ref, wd_ref, t_ref, o_ref, h_ref, a_ref, b_ref, s_ref, l_ref, oh_ref, acc_ref = refs
        else:
            x_ref, g_ref, wg_ref, wu_ref, wd_ref, o_ref, h_ref, a_ref, b_ref, s_ref, acc_ref = refs
        k = pl.program_id(1)

        @pl.when(k == 0)
        def _():
            xf = x_ref[...]
            h_ref[...] = ((xf * _rms(xf)) * g_ref[...]).astype(BF16)
            acc_ref[...] = jnp.zeros_like(acc_ref)

        for rows in _row_halves(tm):
            h = h_ref[rows, :]
            a = _dot_nt(h, wg_ref[...])
            b = _dot_nt(h, wu_ref[...])
            sb = ((a * (0.5 * jnp.tanh(0.5 * a) + 0.5)) * b).astype(BF16)
            a_ref[rows, :] = a.astype(BF16)
            b_ref[rows, :] = b.astype(BF16)
            s_ref[rows, :] = sb
            acc_ref[rows, :] += _dot(sb, wd_ref[...])

        @pl.when(k == nch - 1)
        def _():
            y = x_ref[...] + 0.5 * acc_ref[...]
            if with_loss:
                e = y - t_ref[...]
                o_ref[...] = e * (1.0 / d)
                oh_ref[...] = (e * (0.5 / d)).astype(BF16)
                l_ref[...] = jnp.broadcast_to(_total(e * e) * (0.5 / d), l_ref.shape)
            else:
                o_ref[...] = y

    row = pl.BlockSpec((tm, d), lambda i, k: (i, 0))
    chunk = pl.BlockSpec((None, fc, d), lambda i, k: (k, 0, 0))
    act = pl.BlockSpec((None, tm, fc), lambda i, k: (k, i, 0))
    in_specs = [row, pl.BlockSpec((1, d), lambda i, k: (0, 0)), chunk, chunk, chunk]
    out_shape = [jax.ShapeDtypeStruct((t, d), F32), jax.ShapeDtypeStruct((t, d), BF16)]
    out_shape += [jax.ShapeDtypeStruct((nch, t, fc), BF16)] * 3
    out_specs = [row, row, act, act, act]
    args = [x, gain, wg, wu, wd]
    if with_loss:
        in_specs.append(row)
        args.append(target)
        out_shape += [jax.ShapeDtypeStruct((nt, 8, LANES), F32), jax.ShapeDtypeStruct((t, d), BF16)]
        out_specs += [pl.BlockSpec((None, 8, LANES), lambda i, k: (i, 0, 0)), row]
    return _pallas(body, name="ffn_fwd_loss" if with_loss else "ffn_fwd", args=args, in_specs=in_specs,
                   out_shape=out_shape, out_specs=out_specs, grid=(nt, nch),
                   scratch_shapes=[pltpu.VMEM((tm, d), F32)], plan=plan)


def _row_halves(n):
    return [slice(0, n // 2), slice(n // 2, n)]


def _swiglu_grads(dyh, a_ref, b_ref, wd_ref, rows):
    ds = _dot_nt(dyh, wd_ref[...])
    av = a_ref[rows, :].astype(F32)
    bv = b_ref[rows, :].astype(F32)
    th = jnp.tanh(0.5 * av)
    sig = 0.5 * th + 0.5
    dab = ((ds * bv) * (sig * (1.0 + av * (0.5 - 0.5 * th)))).astype(BF16)
    return dab, (ds * (av * sig)).astype(BF16)


def _ffn_bwd_a(dyh, a, b, wd, name, plan=None):
    t, d = dyh.shape
    nch, fc, _ = wd.shape
    tm = min(FFN_TILE, t)

    def body(dyh_ref, a_ref, b_ref, wd_ref, da_ref, db_ref):
        for rows in _row_halves(tm):
            da_ref[rows, :], db_ref[rows, :] = _swiglu_grads(dyh_ref[rows, :], a_ref, b_ref, wd_ref, rows)

    act = pl.BlockSpec((None, tm, fc), lambda i, k: (k, i, 0))
    return _pallas(
        body, name=name, args=[dyh, a, b, wd], out_shape=[jax.ShapeDtypeStruct((nch, t, fc), BF16)] * 2, grid=(t // tm, nch),
        in_specs=[pl.BlockSpec((tm, d), lambda i, k: (i, 0)), act, act, pl.BlockSpec((None, fc, d), lambda i, k: (k, 0, 0))],
        out_specs=[act, act], plan=plan)


def _ffn_bwd_h(dy, x, gain, da, db, wg, wu, name, tiles, prev=None, plan=None):
    t, d = x.shape
    nch, fc, _ = wg.shape
    tm = min(FFN_TILE, t)
    nt = t // tm
    t0, t1 = tiles

    def body(*refs):
        dy_ref, x_ref, g_ref, da_ref, db_ref, wg_ref, wu_ref = refs[:7]
        dx_ref, dg_ref, acc_ref = refs[-3:]
        k = pl.program_id(1)

        @pl.when(k == 0)
        def _():
            acc_ref[...] = jnp.zeros_like(acc_ref)

        acc_ref[...] += _dot(da_ref[...], wg_ref[...]) + _dot(db_ref[...], wu_ref[...])

        @pl.when(k == nch - 1)
        def _():
            xf = x_ref[...]
            dxn, dgr = _rms_bwd(xf, _rms(xf), g_ref[...], acc_ref[...])
            dx_ref[...] = dy_ref[...] + dxn
            dg_ref[...] = jnp.sum(dgr, axis=0, keepdims=True)

    row = pl.BlockSpec((tm, d), lambda i, k: (i + t0, 0))
    chunk = pl.BlockSpec((None, fc, d), lambda i, k: (k, 0, 0))
    act = pl.BlockSpec((None, tm, fc), lambda i, k: (k, i + t0, 0))
    args = [dy, x, gain, da, db, wg, wu]
    in_specs = [row, row, pl.BlockSpec((1, d), lambda i, k: (0, 0)), act, act, chunk, chunk]
    aliases = {}
    if prev is not None:
        aliases = {len(args): 0, len(args) + 1: 1}
        args += list(prev)
        in_specs += [ANY, ANY]
    return _pallas(
        body, name=name, args=args, out_shape=[jax.ShapeDtypeStruct((t, d), F32), jax.ShapeDtypeStruct((nt, 1, d), F32)],
        grid=(t1 - t0, nch), in_specs=in_specs,
        out_specs=[row, pl.BlockSpec((None, 1, d), lambda i, k: (i + t0, 0, 0))],
        scratch_shapes=[pltpu.VMEM((tm, d), F32)], plan=plan, aliases=aliases)


def _ffn_bwd_x(dy, x, gain, a, b, wg, wu, wd, name, plan=None):
    t, d = x.shape
    nch, fc, _ = wg.shape
    tm = min(FFN_TILE, t)
    nt = t // tm

    def body(dy_ref, x_ref, g_ref, a_ref, b_ref, wg_ref, wu_ref, wd_ref, dx_ref, da_ref, db_ref, dg_ref, acc_ref):
        k = pl.program_id(1)

        @pl.when(k == 0)
        def _():
            acc_ref[...] = jnp.zeros_like(acc_ref)

        for rows in _row_halves(tm):
            dab, dbb = _swiglu_grads((0.5 * dy_ref[rows, :]).astype(BF16), a_ref, b_ref, wd_ref, rows)
            da_ref[rows, :] = dab
            db_ref[rows, :] = dbb
            acc_ref[rows, :] += _dot(dab, wg_ref[...]) + _dot(dbb, wu_ref[...])

        @pl.when(k == nch - 1)
        def _():
            xf = x_ref[...]
            dxn, dgr = _rms_bwd(xf, _rms(xf), g_ref[...], acc_ref[...])
            dx_ref[...] = dy_ref[...] + dxn
            dg_ref[...] = jnp.sum(dgr, axis=0, keepdims=True)

    row = pl.BlockSpec((tm, d), lambda i, k: (i, 0))
    chunk = pl.BlockSpec((None, fc, d), lambda i, k: (k, 0, 0))
    act = pl.BlockSpec((None, tm, fc), lambda i, k: (k, i, 0))
    return _pallas(
        body, name=name, args=[dy, x, gain, a, b, wg, wu, wd],
        out_shape=[jax.ShapeDtypeStruct((t, d), F32), jax.ShapeDtypeStruct((nch, t, fc), BF16),
                   jax.ShapeDtypeStruct((nch, t, fc), BF16), jax.ShapeDtypeStruct((nt, 1, d), F32)],
        grid=(nt, nch),
        in_specs=[row, row, pl.BlockSpec((1, d), lambda i, k: (0, 0)), act, act, chunk, chunk, chunk],
        out_specs=[row, act, act, pl.BlockSpec((None, 1, d), lambda i, k: (i, 0, 0))],
        scratch_shapes=[pltpu.VMEM((tm, d), F32)], plan=plan)


def _ffn_bwd_w(pairs, name, plan=None):
    n = len(pairs)
    nch, t, fc = pairs[0][0].shape
    d = pairs[0][1].shape[1]
    tm = min(FFN_TILE, t)

    def body(*refs):
        @pl.when(pl.program_id(1) == 0)
        def _():
            for o_ref in refs[2 * n:]:
                o_ref[...] = jnp.zeros_like(o_ref)

        for j in range(n):
            refs[2 * n + j][...] += _dot_tn(refs[j][...], refs[n + j][...])

    row = pl.BlockSpec((tm, d), lambda k, i: (i, 0))
    act = pl.BlockSpec((None, tm, fc), lambda k, i: (k, i, 0))
    chunk = pl.BlockSpec((None, fc, d), lambda k, i: (k, 0, 0))
    return _pallas(body, name=name, args=[p[0] for p in pairs] + [p[1] for p in pairs],
                   out_shape=[jax.ShapeDtypeStruct((nch, fc, d), F32)] * n, grid=(nch, t // tm),
                   in_specs=[act] * n + [row] * n, out_specs=[chunk] * n, plan=plan)


def _head_masks():
    lane = lax.broadcasted_iota(jnp.int32, (1, LANES), 1)
    return lane < HEAD_DIM


def _head_rms(x, lo):
    x2 = x * x
    s0 = jnp.sum(jnp.where(lo, x2, 0.0), axis=1, keepdims=True)
    s1 = jnp.sum(jnp.where(lo, 0.0, x2), axis=1, keepdims=True)
    return jnp.where(lo, lax.rsqrt(s0 * (1.0 / HEAD_DIM) + EPS), lax.rsqrt(s1 * (1.0 / HEAD_DIM) + EPS))


def _head_mean(v, lo):
    s0 = jnp.sum(jnp.where(lo, v, 0.0), axis=1, keepdims=True)
    s1 = jnp.sum(jnp.where(lo, 0.0, v), axis=1, keepdims=True)
    return jnp.where(lo, s0, s1) * (1.0 / HEAD_DIM)


def _mix_proj(x1, gain, wt, qn, kn, pool_width, attn_width):
    t, d = x1.shape
    tm = min(512, t)
    nt = t // tm
    scale = HEAD_DIM ** -0.5
    c_q, c_k, c_v = pool_width, pool_width + attn_width, pool_width + 2 * attn_width
    c_f = c_v + attn_width

    def body(x_ref, g_ref, wt_ref, qn_ref, kn_ref, hm_ref, pv_ref, q_ref, k_ref, qh_ref, kh_ref, vb_ref, f_ref):
        lo = _head_masks()
        for rows in _row_halves(tm):
            xf = x_ref[rows, :]
            hm = ((xf * _rms(xf)) * g_ref[...]).astype(BF16)
            hm_ref[rows, :] = hm
            f_ref[rows, :] = _dot_nt(hm, wt_ref[c_f:c_f + LANES, :])
            pv_ref[rows, :] = _dot_nt(hm, wt_ref[0:pool_width, :])
            vb_ref[rows, :] = _dot_nt(hm, wt_ref[c_v:c_v + attn_width, :]).astype(BF16)
            for c0, raw_ref, hat_ref, n_ref, mul in ((c_q, q_ref, qh_ref, qn_ref, scale), (c_k, k_ref, kh_ref, kn_ref, 1.0)):
                raw = _dot_nt(hm, wt_ref[c0:c0 + attn_width, :])
                raw_ref[rows, :] = raw
                for blk in range(attn_width // LANES):
                    sl = slice(blk * LANES, (blk + 1) * LANES)
                    xb = raw[:, sl]
                    hat_ref[rows, sl] = (((xb * _head_rms(xb, lo)) * n_ref[:, sl]) * mul).astype(BF16)

    row = pl.BlockSpec((tm, d), lambda i: (i, 0))
    half = pl.BlockSpec((tm, attn_width), lambda i: (i, 0))
    const = lambda shape: pl.BlockSpec(shape, lambda i: (0, 0))
    return _pallas(
        body, name="mix_proj", args=[x1, gain, wt, qn, kn],
        out_shape=[jax.ShapeDtypeStruct((t, d), BF16), jax.ShapeDtypeStruct((t, pool_width), F32),
                   jax.ShapeDtypeStruct((t, attn_width), F32), jax.ShapeDtypeStruct((t, attn_width), F32),
                   jax.ShapeDtypeStruct((t, attn_width), BF16), jax.ShapeDtypeStruct((t, attn_width), BF16),
                   jax.ShapeDtypeStruct((t, attn_width), BF16), jax.ShapeDtypeStruct((t, LANES), F32)],
        grid=(nt,),
        in_specs=[row, const((1, d)), const(wt.shape), const((1, attn_width)), const((1, attn_width))],
        out_specs=[row, pl.BlockSpec((tm, pool_width), lambda i: (i, 0)), half, half, half, half, half,
                   pl.BlockSpec((tm, LANES), lambda i: (i, 0))])[0]


def _shift_down(v, dist, row):
    return jnp.where(row >= dist, pltpu.roll(v, dist, 0), 0.0)


def _shift_up(v, dist, row, n):
    return jnp.where(row + dist < n, pltpu.roll(v, n - dist, 0), 0.0)


def _aug_lane(e):
    return HEAD_DIM if e == 0 else 0


def _forget_prefix(f, bias, qh, kh, n_batch, seq):
    def body(f_ref, b_ref, q_ref, k_ref, qa_ref, ka_ref):
        z = f_ref[...] + b_ref[...]
        acc = jnp.minimum(z, 0.0) - jnp.log(1.0 + jnp.exp(-jnp.abs(z)))
        row = lax.broadcasted_iota(jnp.int32, (seq, 1), 0)
        dist = 1
        while dist < seq:
            acc = acc + _shift_down(acc, dist, row)
            dist *= 2
        lane = lax.broadcasted_iota(jnp.int32, (1, LANES), 1)
        for h in range(N_HEADS):
            pair, e = divmod(h, 2)
            a0 = _aug_lane(e)
            own = (lane < HEAD_DIM) if e == 0 else (lane >= HEAD_DIM)
            fh = _pick_lane(acc, h)
            hi = fh.astype(BF16).astype(F32)
            rest = fh - hi
            mid = rest.astype(BF16).astype(F32)
            low = rest - mid
            q_ones = (lane >= a0 + 3) & (lane < a0 + 6)
            k_ones = (lane >= a0) & (lane < a0 + 3)
            q_aug = jnp.where(lane == a0, hi, jnp.where(lane == a0 + 1, mid, jnp.where(lane == a0 + 2, low,
                              jnp.where(q_ones, 1.0, 0.0))))
            k_aug = jnp.where(k_ones, 1.0, jnp.where(lane == a0 + 3, -hi, jnp.where(lane == a0 + 4, -mid,
                              jnp.where(lane == a0 + 5, -low, 0.0))))
            src = slice(pair * LANES, (pair + 1) * LANES)
            dst = slice(h * LANES, (h + 1) * LANES)
            qa_ref[:, dst] = jnp.where(own, q_ref[:, src].astype(F32), q_aug).astype(BF16)
            ka_ref[:, dst] = jnp.where(own, k_ref[:, src].astype(F32), k_aug).astype(BF16)

    width = qh.shape[1]
    tok = pl.BlockSpec((seq, width), lambda b: (b, 0))
    aug = pl.BlockSpec((seq, N_HEADS * LANES), lambda b: (b, 0))
    return pl.pallas_call(
        body, out_shape=[jax.ShapeDtypeStruct((n_batch * seq, N_HEADS * LANES), BF16)] * 2, grid=(n_batch,),
        in_specs=[pl.BlockSpec((seq, LANES), lambda b: (b, 0)), pl.BlockSpec((1, LANES), lambda b: (0, 0)), tok, tok],
        out_specs=[aug, aug], compiler_params=_params(), name="forget_prefix",
    )(f, bias, qh, kh)


def _pool_groups(pv_ref, pw_ref, ps_ref, seq):
    row = lax.broadcasted_iota(jnp.int32, (seq, 1), 0)
    pos = (row + 1).astype(F32)
    out = []
    for g, win in enumerate(POOL_WINDOWS):
        sl = slice(g * LANES, (g + 1) * LANES)
        xg = pv_ref[:, sl]
        acc = xg
        dist = 1
        while dist < win:
            acc = acc + _shift_down(acc, dist, row)
            dist *= 2
        pooled = (acc / jnp.minimum(pos, float(win)) - xg).astype(BF16)
        mixed = _dot(pooled, pw_ref[g])
        out.append((pooled, mixed, mixed * ps_ref[:, sl]))
    return out


def _pool_fwd(pv, pw, ps, onp, n_batch, seq):
    width = pv.shape[1]

    def body(pv_ref, pw_ref, ps_ref, on_ref, y_ref):
        groups = _pool_groups(pv_ref, pw_ref, ps_ref, seq)
        ssq = sum(jnp.sum(ms * ms, axis=1, keepdims=True) for _, _, ms in groups)
        r = lax.rsqrt(ssq * (1.0 / width) + EPS)
        for g, (_, _, ms) in enumerate(groups):
            sl = slice(g * LANES, (g + 1) * LANES)
            y_ref[:, sl] = ((ms * r) * on_ref[:, sl]).astype(BF16)

    return pl.pallas_call(
        body, out_shape=jax.ShapeDtypeStruct((n_batch * seq, width), BF16), grid=(n_batch,),
        in_specs=[pl.BlockSpec((seq, width), lambda b: (b, 0)), pl.BlockSpec(pw.shape, lambda b: (0, 0, 0)),
                  pl.BlockSpec((1, width), lambda b: (0, 0)), pl.BlockSpec((1, width), lambda b: (0, 0))],
        out_specs=pl.BlockSpec((seq, width), lambda b: (b, 0)),
        compiler_params=_params(), name="pool_fwd",
    )(pv, pw, ps, onp)


def _pool_bwd(pv, dyp, pw, ps, onp, n_batch, seq):
    width = pv.shape[1]

    def body(pv_ref, dy_ref, pw_ref, ps_ref, on_ref, dpv_ref, dpw_ref, dps_ref, don_ref):
        groups = _pool_groups(pv_ref, pw_ref, ps_ref, seq)
        ssq = sum(jnp.sum(ms * ms, axis=1, keepdims=True) for _, _, ms in groups)
        r = lax.rsqrt(ssq * (1.0 / width) + EPS)
        mean = sum(jnp.sum((dy_ref[:, g * LANES:(g + 1) * LANES] * on_ref[:, g * LANES:(g + 1) * LANES]) * (ms * r),
                           axis=1, keepdims=True) for g, (_, _, ms) in enumerate(groups)) * (1.0 / width)
        row = lax.broadcasted_iota(jnp.int32, (seq, 1), 0)
        pos = (row + 1).astype(F32)
        for g, (pooled, mixed, ms) in enumerate(groups):
            sl = slice(g * LANES, (g + 1) * LANES)
            dy = dy_ref[:, sl]
            xh = ms * r
            don_ref[:, sl] = jnp.sum(dy * xh, axis=0, keepdims=True)
            dms = r * (dy * on_ref[:, sl] - xh * mean)
            dps_ref[:, sl] = jnp.sum(dms * mixed, axis=0, keepdims=True)
            dmix = (dms * ps_ref[:, sl]).astype(BF16)
            dpw_ref[g] = _dot_tn(pooled, dmix)
            dpool = _dot_nt(dmix, pw_ref[g])
            win = POOL_WINDOWS[g]
            acc = dpool / jnp.minimum(pos, float(win))
            dist = 1
            while dist < win:
                acc = acc + _shift_up(acc, dist, row, seq)
                dist *= 2
            dpv_ref[:, sl] = (acc - dpool).astype(BF16)

    tok = pl.BlockSpec((seq, width), lambda b: (b, 0))
    vec = pl.BlockSpec((1, width), lambda b: (0, 0))
    pvec = pl.BlockSpec((None, 1, width), lambda b: (b, 0, 0))
    return pl.pallas_call(
        body,
        out_shape=[jax.ShapeDtypeStruct((n_batch * seq, width), BF16),
                   jax.ShapeDtypeStruct((n_batch,) + pw.shape, F32),
                   jax.ShapeDtypeStruct((n_batch, 1, width), F32), jax.ShapeDtypeStruct((n_batch, 1, width), F32)],
        grid=(n_batch,),
        in_specs=[tok, tok, pl.BlockSpec(pw.shape, lambda b: (0, 0, 0)), vec, vec],
        out_specs=[tok, pl.BlockSpec((None,) + pw.shape, lambda b: (b, 0, 0, 0)), pvec, pvec],
        compiler_params=_params(), name="pool_bwd",
    )(pv, dyp, pw, ps, onp)


def _pick_lane(tile, idx):
    lane = lax.broadcasted_iota(jnp.int32, (1, LANES), 1)
    return jnp.sum(jnp.where(lane == idx, tile, 0.0), axis=1, keepdims=True)


def _pick_row(tile, idx):
    sub = lax.broadcasted_iota(jnp.int32, (tile.shape[0], 1), 0)
    return jnp.sum(jnp.where(sub == idx, tile, 0.0), axis=0, keepdims=True)


def _put_lane(col, idx):
    lane = lax.broadcasted_iota(jnp.int32, (1, LANES), 1)
    return jnp.where(lane == idx, col, 0.0)


def _head_select(e):
    lo = _head_masks()
    return lo if e == 0 else jnp.logical_not(lo)


def _causal(st, shift):
    row = lax.broadcasted_iota(jnp.int32, st.shape, 0)
    col = lax.broadcasted_iota(jnp.int32, st.shape, 1) + shift
    return jnp.where(col >= row, st, NEG)


def _transpose_blocks(a):
    rows, cols = a.shape
    return jnp.concatenate(
        [jnp.concatenate([a[r:r + LANES, c:c + LANES].T for r in range(0, rows, LANES)], axis=1)
         for c in range(0, cols, LANES)], axis=0)


def _accumulate(ref, value, first):
    @pl.when(first)
    def _():
        ref[...] = value

    @pl.when(jnp.logical_not(first))
    def _():
        ref[...] += value


def _attn_fwd(qa, ka, vb, n_batch, seq, plan=None):
    tq = min(ATT_BLOCK, seq)
    nq, nsub, tk = seq // tq, tq // ATT_SUB, tq
    pairs = vb.shape[1] // LANES

    def body(q_ref, k_ref, v_ref, o_ref, lse_ref, acc_ref):
        i, p = pl.program_id(1), pl.program_id(2)
        row_lo = lax.broadcasted_iota(jnp.int32, (LANES, 1), 0) < HEAD_DIM
        qs = [q_ref[:, e * LANES:(e + 1) * LANES] for e in range(2)]
        acc_ref[...] = jnp.zeros_like(acc_ref)

        def tile(off, stats, diagonal):
            vj = v_ref[pl.ds(off, tk), :]
            new, alphas, pvs = [], [], []
            for e in range(2):
                st = _dot_nt(k_ref[pl.ds(off, tk), e * LANES:(e + 1) * LANES], qs[e])
                if diagonal:
                    st = _causal(st, 0)
                m, l = stats[e]
                m_new = jnp.maximum(m, jnp.max(st, axis=0, keepdims=True))
                alpha = jnp.exp(m - m_new)
                pt = jnp.exp(st - m_new)
                new.append((m_new, alpha * l + jnp.sum(pt, axis=0, keepdims=True)))
                alphas.append(alpha)
                pvs.append(_dot_tn(jnp.where(_head_select(e), vj, jnp.zeros_like(vj)), pt.astype(BF16)))
            acc_ref[...] = acc_ref[...] * jnp.where(row_lo, alphas[0], alphas[1]) + (pvs[0] + pvs[1])
            return tuple(new)

        init = ((jnp.full((1, tq), NEG, F32), jnp.zeros((1, tq), F32)),) * 2
        stats = lax.fori_loop(0, i, lambda j, st: tile(pl.multiple_of(j * tk, tk), st, False), init)
        (m0, l0), (m1, l1) = tile(pl.multiple_of(i * tk, tk), stats, True)
        out_t = acc_ref[...] / jnp.where(row_lo, l0, l1)
        sub = lax.broadcasted_iota(jnp.int32, (8, 1), 0)
        lse0, lse1 = m0 + jnp.log(l0), m1 + jnp.log(l1)
        for a in range(nsub):
            sl = slice(a * ATT_SUB, (a + 1) * ATT_SUB)
            o_ref[sl, :] = out_t[:, sl].T
            rows = jnp.where(sub == 2 * p, lse0[:, sl], 0.0) + jnp.where(sub == 2 * p + 1, lse1[:, sl], 0.0)
            _accumulate(lse_ref.at[a], rows, p == 0)

    return _pallas(
        body, name="attn_fwd", args=[qa, ka, vb],
        out_shape=[jax.ShapeDtypeStruct((n_batch * seq, pairs * LANES), F32),
                   jax.ShapeDtypeStruct((n_batch * seq // ATT_SUB, 8, ATT_SUB), F32)],
        grid=(n_batch, nq, pairs),
        in_specs=[pl.BlockSpec((tq, 2 * LANES), lambda b, i, p: (b * nq + i, p)),
                  pl.BlockSpec((seq, 2 * LANES), lambda b, i, p: (b, p)),
                  pl.BlockSpec((seq, LANES), lambda b, i, p: (b, p))],
        out_specs=[pl.BlockSpec((tq, LANES), lambda b, i, p: (b * nq + i, p)),
                   pl.BlockSpec((nsub, 8, ATT_SUB), lambda b, i, p: (b * nq + i, 0, 0))],
        scratch_shapes=[pltpu.VMEM((LANES, tq), F32)], plan=plan)


def _attn_bwd(qa, ka, vb, do, lse, delta, n_batch, seq, plan=None):
    tq = min(ATT_BLOCK, seq)
    nq, nsub = seq // tq, tq // ATT_SUB
    n_tiles = seq // ATT_SUB
    pairs = vb.shape[1] // LANES

    def body(q_ref, k_ref, v_ref, do_ref, lse_ref, dl_ref, dq_ref, dk_ref, dv_ref, dfq_ref, dfk_ref,
             dq0_ref, dq1_ref, dk0_ref, dk1_ref, dva_ref):
        p = pl.program_id(1)
        dqs, dks = (dq0_ref, dq1_ref), (dk0_ref, dk1_ref)
        for acc in (dk0_ref, dk1_ref, dva_ref):
            acc[...] = jnp.zeros_like(acc)
        dfq_cols = []
        for i in range(nq):
            rows_i = slice(i * tq, (i + 1) * tq)
            qs = [q_ref[rows_i, e * LANES:(e + 1) * LANES] for e in range(2)]
            dov = do_ref[rows_i, :]
            does = [jnp.where(_head_select(e), dov, jnp.zeros_like(dov)) for e in range(2)]
            stat = lambda ref, e: jnp.concatenate([_pick_row(ref[i * nsub + a], 2 * p + e) for a in range(nsub)], axis=1)
            ls, dl = [stat(lse_ref, e) for e in range(2)], [stat(dl_ref, e) for e in range(2)]
            for acc in dqs:
                acc[...] = jnp.zeros_like(acc)

            def tile(off, diagonal, qs=qs, dov=dov, does=does, ls=ls, dl=dl):
                vj = v_ref[pl.ds(off, tq), :]
                for e in range(2):
                    kj = k_ref[pl.ds(off, tq), e * LANES:(e + 1) * LANES]
                    st = _dot_nt(kj, qs[e])
                    if diagonal:
                        st = _causal(st, 0)
                    pt = jnp.exp(st - ls[e])
                    dva_ref[pl.ds(off, tq), :] += _dot(pt.astype(BF16), does[e])
                    dpt = _dot_nt(jnp.where(_head_select(e), vj, jnp.zeros_like(vj)), dov)
                    dst = (pt * (dpt - dl[e])).astype(BF16)
                    dks[e][pl.ds(off, tq), :] += _dot(dst, qs[e])
                    dqs[e][...] += _dot(_transpose_blocks(kj), dst)

            def step(j, carry, tile=tile):
                tile(pl.multiple_of(j * tq, tq), False)
                return carry

            lax.fori_loop(0, i, step, 0)
            tile(i * tq, True)
            dq0, dq1 = _transpose_blocks(dq0_ref[...]), _transpose_blocks(dq1_ref[...])
            dq_ref[rows_i, :] = jnp.where(_head_masks(), dq0, dq1)
            dfq_cols.append(_put_lane(_pick_lane(dq0, _aug_lane(0)), 2 * p) + _put_lane(_pick_lane(dq1, _aug_lane(1)), 2 * p + 1))
        dk0, dk1 = dk0_ref[...], dk1_ref[...]
        dk_ref[...] = jnp.where(_head_masks(), dk0, dk1)
        dv_ref[...] = dva_ref[...].astype(BF16)
        dfk = _put_lane(_pick_lane(dk0, _aug_lane(0) + 3), 2 * p) + _put_lane(_pick_lane(dk1, _aug_lane(1) + 3), 2 * p + 1)
        _accumulate(dfq_ref, jnp.concatenate(dfq_cols, axis=0), p == 0)
        _accumulate(dfk_ref, -dfk, p == 0)

    wide = pl.BlockSpec((seq, 2 * LANES), lambda b, p: (b, p))
    blk = pl.BlockSpec((seq, LANES), lambda b, p: (b, p))
    col = pl.BlockSpec((seq, LANES), lambda b, p: (b, 0))
    stat = pl.BlockSpec((n_tiles, 8, ATT_SUB), lambda b, p: (b, 0, 0))
    f32_blk, acc = jax.ShapeDtypeStruct((n_batch * seq, pairs * LANES), F32), pltpu.VMEM((seq, LANES), F32)
    return _pallas(
        body, name="attn_bwd", args=[qa, ka, vb, do, lse, delta],
        out_shape=[f32_blk, f32_blk, jax.ShapeDtypeStruct((n_batch * seq, pairs * LANES), BF16),
                   jax.ShapeDtypeStruct((n_batch * seq, LANES), F32), jax.ShapeDtypeStruct((n_batch * seq, LANES), F32)],
        grid=(n_batch, pairs), in_specs=[wide, wide, blk, blk, stat, stat], out_specs=[blk, blk, blk, col, col],
        scratch_shapes=[pltpu.VMEM((LANES, tq), F32), pltpu.VMEM((LANES, tq), F32), acc, acc, acc], plan=plan)


def _forget_bwd(dfq, dfk, f, bias, n_batch, seq):
    def body(dfq_ref, dfk_ref, f_ref, b_ref, df_ref, db_ref):
        acc = dfq_ref[...] + dfk_ref[...]
        row = lax.broadcasted_iota(jnp.int32, (seq, 1), 0)
        dist = 1
        while dist < seq:
            acc = acc + _shift_up(acc, dist, row, seq)
            dist *= 2
        df = acc * _sigmoid(-(f_ref[...] + b_ref[...]))
        df_ref[...] = df
        db_ref[...] = jnp.sum(df, axis=0, keepdims=True)

    col = pl.BlockSpec((seq, LANES), lambda b: (b, 0))
    return pl.pallas_call(
        body,
        out_shape=[jax.ShapeDtypeStruct((n_batch * seq, LANES), F32), jax.ShapeDtypeStruct((n_batch, 1, LANES), F32)],
        grid=(n_batch,), in_specs=[col, col, col, pl.BlockSpec((1, LANES), lambda b: (0, 0))],
        out_specs=[col, pl.BlockSpec((None, 1, LANES), lambda b: (b, 0, 0))],
        compiler_params=_params(), name="forget_bwd",
    )(dfq, dfk, f, bias)


def _mix_out(x1, yp, o, ona, woa, wob):
    t, d = x1.shape
    width = o.shape[1]
    tm = min(512, t)

    def body(x_ref, yp_ref, o_ref, on_ref, wa_ref, wb_ref, x2_ref, ya_ref):
        of = o_ref[...]
        ya = ((of * _rms(of)) * on_ref[...]).astype(BF16)
        ya_ref[...] = ya
        x2_ref[...] = x_ref[...] + (_dot(yp_ref[...], wa_ref[...]) + _dot(ya, wb_ref[...]))

    row = pl.BlockSpec((tm, d), lambda i: (i, 0))
    half = pl.BlockSpec((tm, width), lambda i: (i, 0))
    wspec = pl.BlockSpec((width, d), lambda i: (0, 0))
    return pl.pallas_call(
        body, out_shape=[jax.ShapeDtypeStruct((t, d), F32), jax.ShapeDtypeStruct((t, width), BF16)],
        grid=(t // tm,), in_specs=[row, half, half, pl.BlockSpec((1, width), lambda i: (0, 0)), wspec, wspec],
        out_specs=[row, half], compiler_params=_params(), name="mix_out",
    )(x1, yp, o, ona, woa, wob)


def _mix_out_bwd(dx2, o, yp, ya, ona, woa, wob, plan=None):
    t, d = dx2.shape
    width = o.shape[1]
    tm = min(512, t)
    nt = t // tm

    def body(dx_ref, o_ref, yp_ref, ya_ref, on_ref, wa_ref, wb_ref, dyp_ref, do_ref, dl_ref, dwa_ref, dwb_ref, don_ref):
        @pl.when(pl.program_id(0) == 0)
        def _():
            dwa_ref[...] = jnp.zeros_like(dwa_ref)
            dwb_ref[...] = jnp.zeros_like(dwb_ref)

        dxb = dx_ref[...].astype(BF16)
        dwa_ref[...] += _dot_tn(yp_ref[...], dxb)
        dwb_ref[...] += _dot_tn(ya_ref[...], dxb)
        dyp_ref[...] = _dot_nt(dxb, wa_ref[...])
        of = o_ref[...]
        dov, dgr = _rms_bwd(of, _rms(of), on_ref[...], _dot_nt(dxb, wb_ref[...]))
        don_ref[...] = jnp.sum(dgr, axis=0, keepdims=True)
        do_ref[...] = dov.astype(BF16)
        lo = _head_masks()
        prod = dov * of
        delta = jnp.zeros((tm, LANES), F32)
        for blk in range(width // LANES):
            pb = prod[:, blk * LANES:(blk + 1) * LANES]
            delta = delta + _put_lane(jnp.sum(jnp.where(lo, pb, 0.0), axis=1, keepdims=True), 2 * blk)
            delta = delta + _put_lane(jnp.sum(jnp.where(lo, 0.0, pb), axis=1, keepdims=True), 2 * blk + 1)
        for c in range(tm // ATT_SUB):
            dl_ref[c] = delta[c * ATT_SUB:(c + 1) * ATT_SUB, :].T[0:8, :]

    row = pl.BlockSpec((tm, d), lambda i: (i, 0))
    half = pl.BlockSpec((tm, width), lambda i: (i, 0))
    wspec = pl.BlockSpec((width, d), lambda i: (0, 0))
    return _pallas(
        body, name="mix_out_bwd", args=[dx2, o, yp, ya, ona, woa, wob],
        out_shape=[jax.ShapeDtypeStruct((t, width), F32), jax.ShapeDtypeStruct((t, width), BF16),
                   jax.ShapeDtypeStruct((t // ATT_SUB, 8, ATT_SUB), F32), jax.ShapeDtypeStruct((width, d), F32),
                   jax.ShapeDtypeStruct((width, d), F32), jax.ShapeDtypeStruct((nt, 1, width), F32)],
        grid=(nt,),
        in_specs=[row, half, half, half, pl.BlockSpec((1, width), lambda i: (0, 0)), wspec, wspec],
        out_specs=[half, half, pl.BlockSpec((tm // ATT_SUB, 8, ATT_SUB), lambda i: (i, 0, 0)), wspec, wspec,
                   pl.BlockSpec((None, 1, width), lambda i: (i, 0, 0))], plan=plan)


def _mix_in_bwd(dx2, x1, gain, hm, dpv, dqh, q, dkh, k, dv, df, qn, kn, wt):
    t, d = x1.shape
    width = q.shape[1]
    pool_width = dpv.shape[1]
    tm = min(512, t)
    nt = t // tm
    scale = HEAD_DIM ** -0.5
    c_q, c_k, c_v = pool_width, pool_width + width, pool_width + 2 * width
    c_f = c_v + width

    def body(dx2_ref, x_ref, g_ref, hm_ref, dpv_ref, dqh_ref, q_ref, dkh_ref, k_ref, dv_ref, df_ref, qn_ref, kn_ref,
             wt_ref, dx_ref, dxh_ref, dwt_ref, dg_ref, dqn_ref, dkn_ref):
        @pl.when(pl.program_id(0) == 0)
        def _():
            dwt_ref[...] = jnp.zeros_like(dwt_ref)

        lo = _head_masks()
        for part, rows in enumerate(_row_halves(tm)):
            def put(ref, sl, value):
                ref[:, sl] = value if part == 0 else ref[:, sl] + value

            hm = hm_ref[rows, :]
            pieces = [(0, dpv_ref[rows, :])]
            for c0, raw_ref, dh_ref, n_ref, dn_ref, mul in ((c_q, q_ref, dqh_ref, qn_ref, dqn_ref, scale),
                                                           (c_k, k_ref, dkh_ref, kn_ref, dkn_ref, 1.0)):
                cols = []
                for blk in range(width // LANES):
                    sl = slice(blk * LANES, (blk + 1) * LANES)
                    xb = raw_ref[rows, sl]
                    gb = dh_ref[rows, sl] * mul
                    r = _head_rms(xb, lo)
                    xh = xb * r
                    dyg = gb * n_ref[:, sl]
                    cols.append((r * (dyg - xh * _head_mean(dyg * xh, lo))).astype(BF16))
                    put(dn_ref, sl, jnp.sum(gb * xh, axis=0, keepdims=True))
                pieces.append((c0, jnp.concatenate(cols, axis=1)))
            pieces.append((c_v, dv_ref[rows, :]))
            pieces.append((c_f, df_ref[rows, :].astype(BF16)))
            dhm = jnp.zeros((tm // 2, d), F32)
            for c0, piece in pieces:
                dwt_ref[c0:c0 + piece.shape[1], :] += _dot_tn(piece, hm)
                dhm = dhm + _dot(piece, wt_ref[c0:c0 + piece.shape[1], :])
            xf = x_ref[rows, :]
            dxn, dgr = _rms_bwd(xf, _rms(xf), g_ref[...], dhm)
            dx = dx2_ref[rows, :] + dxn
            dx_ref[rows, :] = dx
            dxh_ref[rows, :] = (0.5 * dx).astype(BF16)
            put(dg_ref, slice(None), jnp.sum(dgr, axis=0, keepdims=True))

    row = pl.BlockSpec((tm, d), lambda i: (i, 0))
    half = pl.BlockSpec((tm, width), lambda i: (i, 0))
    const = lambda shape: pl.BlockSpec(shape, lambda i: (0, 0))
    pvec = lambda n: pl.BlockSpec((None, 1, n), lambda i: (i, 0, 0))
    return pl.pallas_call(
        body,
        out_shape=[jax.ShapeDtypeStruct((t, d), F32), jax.ShapeDtypeStruct((t, d), BF16), jax.ShapeDtypeStruct(wt.shape, F32),
                   jax.ShapeDtypeStruct((nt, 1, d), F32),
                   jax.ShapeDtypeStruct((nt, 1, width), F32), jax.ShapeDtypeStruct((nt, 1, width), F32)],
        grid=(nt,),
        in_specs=[row, row, const((1, d)), row, pl.BlockSpec((tm, pool_width), lambda i: (i, 0)), half, half, half, half,
                  half, pl.BlockSpec((tm, LANES), lambda i: (i, 0)), const((1, width)), const((1, width)),
                  const(wt.shape)],
        out_specs=[row, row, const(wt.shape), pvec(d), pvec(width), pvec(width)],
        compiler_params=_params(), name="mix_in_bwd",
    )(dx2, x1, gain, hm, dpv, dqh, q, dkh, k, dv, df, qn, kn, wt)


def _mesh_pos():
    return lax.axis_index("x"), lax.axis_index("y"), lax.axis_index("c")


def _other_chips(x, y):
    return [(1 - x, y), (x, 1 - y), (1 - x, 1 - y)]


def _remote(src, dst, send_sem, recv_sem, device):
    return pltpu.make_async_remote_copy(src_ref=src, dst_ref=dst, send_sem=send_sem, recv_sem=recv_sem,
                                        device_id=device, device_id_type=pl.DeviceIdType.MESH)


def _half_rows(n_rows, which):
    half = n_rows // 2
    return pl.ds(pl.multiple_of(which * half, 8), half)


def _row_block(rows, cols, itemsize=4):
    rb = rows
    while rb * cols * itemsize > (1 << 20) and rb % 32 == 0:
        rb //= 2
    return rb


def _place_cast(ws, chip, tag):
    n = len(ws)
    rows, cols = ws[0].shape
    rb = _row_block(rows, cols)

    def body(k_ref, *refs):
        for w_ref, o_ref in zip(refs[:n], refs[n:]):
            o_ref[...] = w_ref[...].astype(BF16)

    return pl.pallas_call(
        body, out_shape=[jax.ShapeDtypeStruct((N_CHIPS, rows, cols), BF16)] * n,
        grid_spec=pltpu.PrefetchScalarGridSpec(
            num_scalar_prefetch=1, grid=(rows // rb,),
            in_specs=[pl.BlockSpec((rb, cols), lambda i, k: (i, 0))] * n,
            out_specs=[pl.BlockSpec((None, rb, cols), lambda i, k: (k[0], i, 0))] * n),
        compiler_params=_params(), name="place_" + tag,
    )(chip, *ws)


class _Plan:
    def __init__(self, ins, outs, alias, sems, start, finish, middle=None):
        self.ins, self.outs, self.alias, self.sems = ins, outs, alias, sems
        self.start, self.middle, self.finish = start, middle, finish


def _merge_plans(a, b):
    ni, no, ns = len(a.ins), len(a.outs), len(a.sems)
    alias = dict(a.alias)
    alias.update({ni + i: no + o for i, o in b.alias.items()})

    def both(which):
        stage_a, stage_b = getattr(a, which), getattr(b, which)
        if stage_a is None and stage_b is None:
            return None

        def run(ins, outs, sems):
            if stage_a is not None:
                stage_a(ins[:ni], outs[:no], sems[:ns])
            if stage_b is not None:
                stage_b(ins[ni:], outs[no:], sems[ns:])
        return run

    return _Plan(list(a.ins) + list(b.ins), list(a.outs) + list(b.outs), alias, list(a.sems) + list(b.sems),
                 both("start"), both("finish"), both("middle"))


def _run_plan(plan, name):
    n_in, n_out = len(plan.ins), len(plan.outs)

    def body(*refs):
        parts = refs[:n_in], refs[n_in:n_in + n_out], refs[n_in + n_out:]
        plan.start(*parts)
        if plan.middle is not None:
            plan.middle(*parts)
        plan.finish(*parts)

    return pl.pallas_call(
        body, out_shape=plan.outs, in_specs=[ANY] * n_in, out_specs=[ANY] * n_out, scratch_shapes=plan.sems,
        input_output_aliases=plan.alias, name=name,
    )(*plan.ins)


def _pallas(body, *, name, args, in_specs, out_shape, out_specs, grid, scratch_shapes=(), plan=None, aliases=None):
    n_in, n_out, n_scr = len(args), len(out_shape), len(scratch_shapes)
    plan = plan or _Plan([], [], {}, [], None, None)
    p_in, p_out = len(plan.ins), len(plan.outs)

    def carrying(*refs):
        ins, p_ins = refs[:n_in], refs[n_in:n_in + p_in]
        o0 = n_in + p_in
        outs, p_outs = refs[o0:o0 + n_out], refs[o0 + n_out:o0 + n_out + p_out]
        s0 = o0 + n_out + p_out
        scr, p_sems = refs[s0:s0 + n_scr], refs[s0 + n_scr:]
        ids = [pl.program_id(a) for a in range(len(grid))]

        if plan.start is not None:
            @pl.when(functools.reduce(jnp.logical_and, [i == 0 for i in ids]))
            def _():
                plan.start(p_ins, p_outs, p_sems)

        body(*ins, *outs, *scr)

        if plan.middle is not None:
            step, n_steps = 0, 1
            for i, g in zip(ids, grid):
                step, n_steps = step * g + i, n_steps * g

            @pl.when(step == (3 * n_steps) // 4)
            def _():
                plan.middle(p_ins, p_outs, p_sems)

        if plan.finish is not None:
            @pl.when(functools.reduce(jnp.logical_and, [i == g - 1 for i, g in zip(ids, grid)]))
            def _():
                plan.finish(p_ins, p_outs, p_sems)

    aliases = dict(aliases or {})
    aliases.update({n_in + i: n_out + o for i, o in plan.alias.items()})
    res = pl.pallas_call(
        carrying, out_shape=list(out_shape) + list(plan.outs), grid=grid,
        in_specs=list(in_specs) + [ANY] * p_in, out_specs=list(out_specs) + [ANY] * p_out,
        scratch_shapes=list(scratch_shapes) + list(plan.sems),
        input_output_aliases=aliases, compiler_params=_params(), name=name,
    )(*args, *plan.ins)
    return list(res[:n_out]), list(res[n_out:])


def _plan_gather(stacks):
    n = len(stacks)
    relations = range(3)

    def ici_copies(outs, sems):
        x, y, c = _mesh_pos()
        chips = _other_chips(x, y)
        cps = []
        for w in range(n):
            own = outs[w].at[2 * x + y, _half_rows(stacks[w].shape[1], c)]
            cps += [_remote(own, own, sems[0].at[w, j], sems[1].at[w, j], (*chips[j], c)) for j in relations]
        return cps

    def start(ins, outs, sems):
        for cp in ici_copies(outs, sems):
            cp.start()

    def forwards(outs, sems, core):
        x, y, c = _mesh_pos()
        slots = [2 * cx + cy for cx, cy in _other_chips(x, y)]
        cps = []
        for w in range(n):
            rows = _half_rows(stacks[w].shape[1], core)
            for j in relations:
                landed = outs[w].at[slots[j], rows]
                cps.append((_remote(landed, landed, sems[0].at[w, j], sems[1].at[w, j], (x, y, 1 - c)),
                            _remote(landed, landed, sems[2].at[w, j], sems[3].at[w, j], (x, y, 1 - c))))
        return cps

    def middle(ins, outs, sems):
        c = _mesh_pos()[2]
        for arrival, forward in forwards(outs, sems, c):
            arrival.wait_recv()
            forward.start()

    def finish(ins, outs, sems):
        c = _mesh_pos()[2]
        for _, forward in forwards(outs, sems, 1 - c):
            forward.wait_recv()
        for cp in ici_copies(outs, sems) + [forward for _, forward in forwards(outs, sems, c)]:
            cp.wait_send()

    return _Plan(stacks, [jax.ShapeDtypeStruct(s.shape, s.dtype) for s in stacks], {w: w for w in range(n)},
                 [pltpu.SemaphoreType.DMA((n, 3))] * 4, start, finish, middle)


def _ffn_fwd_gathering(x, gain, stacks, order, later):
    t, d = x.shape
    nch, fc, _ = stacks[0].shape
    n = len(stacks)
    tm = min(FFN_STAGED_TILE, t)
    nt = t // tm
    p_in, p_out = len(later.ins), len(later.outs)
    rh = fc // 2

    def relay(stage, outs, sems):
        send, recv, relay_send, relay_recv, d2d_send, d2d_recv = sems
        mx, my, c = _mesh_pos()
        sibling = (mx, my, 1 - c)
        near = [(1 - mx, my), (mx, 1 - my)]
        slots = [2 * cx + cy for cx, cy in near] + [2 * (1 - mx) + (1 - my)]

        def piece(w, slot, core, quarter=None):
            if quarter is None:
                return outs[w].at[slot, _half_rows(fc, core)]
            return outs[w].at[slot, pl.ds(pl.multiple_of(core * rh + quarter * (rh // 2), 8), rh // 2)]

        def to_near(w, j):
            own = piece(w, 2 * mx + my, c)
            return _remote(own, own, send.at[w, j], recv.at[w, j], (*near[j], c))

        def from_near(w, j):
            landed = piece(w, slots[j], c)
            return _remote(landed, landed, send.at[w, j], recv.at[w, j], sibling)

        def onward(w, j, slot):
            part = piece(w, slot, c, quarter=j)
            return _remote(part, part, relay_send.at[w, j], relay_recv.at[w, j], (*near[1 - j], c))

        def to_sibling(w, j, core):
            landed = piece(w, slots[j], core)
            return _remote(landed, landed, d2d_send.at[w, j], d2d_recv.at[w, j], sibling)

        if stage == 0:
            for w in range(n):
                for j in range(2):
                    to_near(w, j).start()
        elif stage == 1:
            for w in range(n):
                for j in range(2):
                    from_near(w, j).wait_recv()
                    onward(w, j, slots[j]).start()
                    to_sibling(w, j, c).start()
            for w in range(n):
                to_sibling(w, 0, 1 - c).wait_recv()
        elif stage == 2:
            for w in range(n):
                to_sibling(w, 1, 1 - c).wait_recv()
        elif stage == 3:
            for w in range(n):
                for j in range(2):
                    onward(w, j, slots[2]).wait_recv()
                to_sibling(w, 2, c).start()
            for w in range(n):
                to_sibling(w, 2, 1 - c).wait_recv()
        else:
            for w in range(n):
                for j in range(2):
                    to_near(w, j).wait_send()
                    onward(w, j, slots[j]).wait_send()
                for j in range(3):
                    to_sibling(w, j, c).wait_send()

    def body(order_ref, x_ref, g_ref, *refs):
        later_in, refs = refs[n:n + p_in], refs[n + p_in:]
        o_ref, h_ref, a_ref, b_ref, s_ref = refs[:5]
        stack_refs, later_out, refs = refs[5:5 + n], refs[5 + n:5 + n + p_out], refs[5 + n + p_out:]
        w_ref, hs_ref, acc_ref, w_sem = refs[:4]
        relay_sems, later_sems = refs[4:10], refs[10:]
        k, i = pl.program_id(0), pl.program_id(1)
        tile = pl.ds(pl.multiple_of(i * tm, tm), tm)

        @pl.when(i == 0)
        def _():
            for stage in range(nch):
                @pl.when(k == stage)
                def _():
                    relay(stage, stack_refs, relay_sems)
                    if stage == nch - 1 and later.start is not None:
                        later.start(later_in, later_out, later_sems)
            loads = [pltpu.make_async_copy(stack_refs[w].at[order_ref[k]], w_ref.at[w], w_sem.at[w]) for w in range(n)]
            for cp in loads:
                cp.start()
            for cp in loads:
                cp.wait()

        @pl.when(k == 0)
        def _():
            xf = x_ref[...]
            hb = ((xf * _rms(xf)) * g_ref[...]).astype(BF16)
            h_ref[...] = hb
            hs_ref[tile, :] = hb
            acc_ref[tile, :] = jnp.zeros((tm, d), F32)

        for rows in _row_halves(tm):
            part = pl.ds(pl.multiple_of(i * tm + rows.start, tm // 2), tm // 2)
            h = hs_ref[part, :]
            a = _dot_nt(h, w_ref[0])
            b = _dot_nt(h, w_ref[1])
            sb = ((a * (0.5 * jnp.tanh(0.5 * a) + 0.5)) * b).astype(BF16)
            a_ref[rows, :] = a.astype(BF16)
            b_ref[rows, :] = b.astype(BF16)
            s_ref[rows, :] = sb
            acc_ref[part, :] += _dot(sb, w_ref[2])

        @pl.when(k == nch - 1)
        def _():
            o_ref[...] = x_ref[...] + 0.5 * acc_ref[tile, :]

        @pl.when((k == nch - 1) & (i == nt - 1))
        def _():
            relay(nch, stack_refs, relay_sems)
            for stage in (later.middle, later.finish):
                if stage is not None:
                    stage(later_in, later_out, later_sems)

    ends = lambda k, i: jnp.where((k == 0) | (k == nch - 1), i, 0)
    act = pl.BlockSpec((None, tm, fc), lambda k, i, order: (order[k], i, 0))
    out_shape = [jax.ShapeDtypeStruct((t, d), F32), jax.ShapeDtypeStruct((t, d), BF16)]
    out_shape += [jax.ShapeDtypeStruct((nch, t, fc), BF16)] * 3
    out_shape += [jax.ShapeDtypeStruct(s.shape, s.dtype) for s in stacks] + list(later.outs)
    aliases = {3 + w: 5 + w for w in range(n)}
    aliases.update({3 + n + i: 5 + n + o for i, o in later.alias.items()})
    res = pl.pallas_call(
        body, out_shape=out_shape,
        grid_spec=pltpu.PrefetchScalarGridSpec(
            num_scalar_prefetch=1, grid=(nch, nt),
            in_specs=[pl.BlockSpec((tm, d), lambda k, i, order: (ends(k, i), 0)),
                      pl.BlockSpec((1, d), lambda k, i, order: (0, 0))] + [ANY] * (n + p_in),
            out_specs=[pl.BlockSpec((tm, d), lambda k, i, order: (jnp.where(k == nch - 1, i, 0), 0)),
                       pl.BlockSpec((tm, d), lambda k, i, order: (jnp.where(k == 0, i, nt - 1), 0)),
                       act, act, act] + [ANY] * (n + p_out),
            scratch_shapes=[pltpu.VMEM((n, fc, d), BF16), pltpu.VMEM((t, d), BF16), pltpu.VMEM((t, d), F32),
                            pltpu.SemaphoreType.DMA((n,))]
            + [pltpu.SemaphoreType.DMA((n, 2))] * 4 + [pltpu.SemaphoreType.DMA((n, 3))] * 2 + list(later.sems)),
        input_output_aliases=aliases, compiler_params=_params(), name="ffn_fwd",
    )(order, x, gain, *stacks, *later.ins)
    return list(res[:5]), list(res[5:5 + n]), list(res[5 + n:])


def _plan_gather_relay(stacks):
    n = len(stacks)

    def finish(ins, outs, sems):
        send, recv, relay_send, relay_recv, d2d_send, d2d_recv = sems
        x, y, c = _mesh_pos()
        sibling = (x, y, 1 - c)
        near = [(1 - x, y), (x, 1 - y)]
        far = 2 * (1 - x) + (1 - y)
        started = []

        def go(cp):
            cp.start()
            started.append(cp)

        def piece(w, slot, core, quarter=None):
            rh = stacks[w].shape[1] // 2
            if quarter is None:
                return outs[w].at[slot, _half_rows(2 * rh, core)]
            return outs[w].at[slot, pl.ds(pl.multiple_of(core * rh + quarter * (rh // 2), 8), rh // 2)]

        for w in range(n):
            own = piece(w, 2 * x + y, c)
            for j, chip in enumerate(near):
                go(_remote(own, own, send.at[w, j], recv.at[w, j], (*chip, c)))
        for w in range(n):
            for j, (cx, cy) in enumerate(near):
                landed = piece(w, 2 * cx + cy, c)
                _remote(landed, landed, send.at[w, j], recv.at[w, j], sibling).wait_recv()
                part = piece(w, 2 * cx + cy, c, quarter=j)
                go(_remote(part, part, relay_send.at[w, j], relay_recv.at[w, j], (*near[1 - j], c)))
                go(_remote(landed, landed, d2d_send.at[w, j], d2d_recv.at[w, j], sibling))
        for w in range(n):
            for j in range(2):
                part = piece(w, far, c, quarter=j)
                _remote(part, part, relay_send.at[w, j], relay_recv.at[w, j], sibling).wait_recv()
            landed = piece(w, far, c)
            go(_remote(landed, landed, d2d_send.at[w, 2], d2d_recv.at[w, 2], sibling))
        for w in range(n):
            for j, slot in enumerate([2 * cx + cy for cx, cy in near] + [far]):
                landed = piece(w, slot, 1 - c)
                _remote(landed, landed, d2d_send.at[w, j], d2d_recv.at[w, j], sibling).wait_recv()
        for cp in started:
            cp.wait_send()

    return _Plan(stacks, [jax.ShapeDtypeStruct(s.shape, s.dtype) for s in stacks], {w: w for w in range(n)},
                 [pltpu.SemaphoreType.DMA((n, 2))] * 4 + [pltpu.SemaphoreType.DMA((n, 3))] * 2,
                 lambda ins, outs, sems: None, finish)


def _plan_sibling_halves(gs):
    n = len(gs)

    def copies(ins, outs, sems):
        x, y, c = _mesh_pos()
        return [_remote(ins[w].at[:, _half_rows(gs[w].shape[1], 1 - c), :], outs[w], sems[0].at[w], sems[1].at[w],
                        (x, y, 1 - c)) for w in range(n)]

    def start(ins, outs, sems):
        for cp in copies(ins, outs, sems):
            cp.start()

    def finish(ins, outs, sems):
        for cp in copies(ins, outs, sems):
            cp.wait()

    return _Plan(gs, [jax.ShapeDtypeStruct((g.shape[0], g.shape[1] // 2, g.shape[2]), g.dtype) for g in gs], {},
                 [pltpu.SemaphoreType.DMA((n,))] * 2, start, finish)


def _plan_chip_exchange(ps):
    n = len(ps)

    def copies(ins, outs, sems):
        x, y, c = _mesh_pos()
        return [_remote(ins[w].at[2 * cx + cy], outs[w].at[j], sems[0].at[w, j], sems[1].at[w, j], (cx, cy, c))
                for w in range(n) for j, (cx, cy) in enumerate(_other_chips(x, y))]

    def start(ins, outs, sems):
        for cp in copies(ins, outs, sems):
            cp.start()

    def finish(ins, outs, sems):
        for cp in copies(ins, outs, sems):
            cp.wait()

    return _Plan(ps, [jax.ShapeDtypeStruct((3,) + p.shape[1:], p.dtype) for p in ps], {},
                 [pltpu.SemaphoreType.DMA((n, 3))] * 2, start, finish)


def _plan_sibling_share(gs):
    n = len(gs)

    def copies(outs, sems, which):
        x, y, c = _mesh_pos()
        cps = []
        for w in range(n):
            rows = outs[w].at[_half_rows(gs[w].shape[0], c if which == "mine" else 1 - c)]
            cps.append(_remote(rows, rows, sems[0].at[w], sems[1].at[w], (x, y, 1 - c)))
        return cps

    def start(ins, outs, sems):
        for cp in copies(outs, sems, "mine"):
            cp.start()

    def finish(ins, outs, sems):
        for cp in copies(outs, sems, "mine"):
            cp.wait_send()
        for cp in copies(outs, sems, "theirs"):
            cp.wait_recv()

    return _Plan(gs, [jax.ShapeDtypeStruct(g.shape, g.dtype) for g in gs], {w: w for w in range(n)},
                 [pltpu.SemaphoreType.DMA((n,))] * 2, start, finish)


def _same_shape_groups(arrays):
    groups = {}
    for i, a in enumerate(arrays):
        groups.setdefault(a.shape, []).append(i)
    return list(groups.values())


def _add_sibling(gs, r1s, ids, tag):
    n = len(gs)
    nch, rh, cols = r1s[0].shape

    def body(ids_ref, *refs):
        for g_ref, r_ref, o_ref in zip(refs[:n], refs[n:2 * n], refs[2 * n:]):
            o_ref[...] = (g_ref[...] + r_ref[...]).astype(BF16)

    blk = lambda fn: pl.BlockSpec((None, rh, cols), fn)
    return pl.pallas_call(
        body, out_shape=[jax.ShapeDtypeStruct(r1s[0].shape, BF16)] * n,
        grid_spec=pltpu.PrefetchScalarGridSpec(
            num_scalar_prefetch=1, grid=(nch,),
            in_specs=[blk(lambda k, ids: (k, ids[1], 0))] * n + [blk(lambda k, ids: (k, 0, 0))] * n,
            out_specs=[blk(lambda k, ids: (k, 0, 0))] * n),
        compiler_params=_params(), name="add_sibling_" + tag,
    )(ids, *gs, *r1s)


def _add_chips(gs, r1s, r2s, ids, tag):
    n = len(gs)
    _, rh, cols = r1s[0].shape
    nb = 2 if rh % 32 == 0 else 1
    rb = rh // nb

    def body(ids_ref, *refs):
        for g_ref, r1_ref, r2_ref, o_ref in zip(refs[:n], refs[n:2 * n], refs[2 * n:3 * n], refs[3 * n:]):
            own = g_ref[...] + r1_ref[...]
            o_ref[...] = ((own + r2_ref[0].astype(F32)) + r2_ref[1].astype(F32)) + r2_ref[2].astype(F32)

    return pl.pallas_call(
        body, out_shape=[jax.ShapeDtypeStruct((2 * rh, cols), F32)] * n,
        grid_spec=pltpu.PrefetchScalarGridSpec(
            num_scalar_prefetch=1, grid=(nb,),
            in_specs=[pl.BlockSpec((None, rb, cols), lambda i, ids: (ids[0], ids[1] * nb + i, 0))] * n
            + [pl.BlockSpec((None, rb, cols), lambda i, ids: (ids[0], i, 0))] * n
            + [pl.BlockSpec((3, rb, cols), lambda i, ids: (0, i, 0))] * n,
            out_specs=[pl.BlockSpec((rb, cols), lambda i, ids: (ids[1] * nb + i, 0))] * n),
        compiler_params=_params(), name="add_chips_" + tag,
    )(ids, *gs, *r1s, *r2s)


VEC_ROWS = 8


N_DEVICES = 8


def _small_pack(part, d, width):
    names = ("ffn1_norm", "mix_norm", "ffn2_norm", "pool_scale", "out_norm_pool", "out_norm_attn", "qn", "kn", "b_forget",
             "pool_w", "loss")
    args = [part[k] for k in names]
    pw_shape = part["pool_w"].shape[1:]

    def body(g1_ref, gm_ref, g2_ref, ps_ref, onp_ref, ona_ref, qn_ref, kn_ref, bf_ref, pw_ref, loss_ref, vbuf, pbuf):
        lo = _head_masks()

        def fold_heads(ref):
            v = jnp.sum(ref[...], axis=0)
            acc = jnp.zeros((VEC_ROWS, LANES), F32)
            for blk in range(width // LANES):
                vb = jnp.broadcast_to(v[:, blk * LANES:(blk + 1) * LANES], (VEC_ROWS, LANES))
                acc = acc + vb + pltpu.roll(vb, HEAD_DIM, 1)
            return jnp.where(lo, acc, 0.0)[0:1, :]

        vbuf[0] = jnp.zeros((VEC_ROWS, d), F32)
        vbuf[0, 0:1, :] = jnp.sum(g1_ref[...], axis=0)
        vbuf[0, 1:2, :] = jnp.sum(gm_ref[...], axis=0)
        vbuf[0, 2:3, :] = jnp.sum(g2_ref[...], axis=0)
        vbuf[0, 5:6, 0:LANES] = jnp.sum(loss_ref[...], axis=0)[0:1, :]
        vbuf[0, 3:4, 0:width] = jnp.sum(ps_ref[...], axis=0)
        vbuf[0, 3:4, width:2 * width] = jnp.sum(onp_ref[...], axis=0)
        vbuf[0, 4:5, 0:width] = jnp.sum(ona_ref[...], axis=0)
        vbuf[0, 4:5, width:width + LANES] = fold_heads(qn_ref)
        vbuf[0, 4:5, width + LANES:width + 2 * LANES] = fold_heads(kn_ref)
        vbuf[0, 4:5, width + 2 * LANES:width + 3 * LANES] = jnp.sum(bf_ref[...], axis=0)
        pbuf[0] = jnp.sum(pw_ref[...], axis=0)

    return pl.pallas_call(
        body, out_shape=[jax.ShapeDtypeStruct((N_DEVICES, VEC_ROWS, d), F32), jax.ShapeDtypeStruct((N_DEVICES,) + pw_shape, F32)],
        in_specs=[VM] * len(args), out_specs=[VM, VM], compiler_params=_params(), name="small_pack",
    )(*args)


def _plan_all_to_all(stacks):
    n = len(stacks)

    def copies(outs, sems):
        x, y, c = _mesh_pos()
        cps = []
        for r in range(1, N_DEVICES):
            peer = (x if not r & 4 else 1 - x, y if not r & 2 else 1 - y, c if not r & 1 else 1 - c)
            cps += [_remote(outs[w].at[0], outs[w].at[r], sems[0].at[w, r - 1], sems[1].at[w, r - 1], peer) for w in range(n)]
        return cps

    def start(ins, outs, sems):
        for cp in copies(outs, sems):
            cp.start()

    def finish(ins, outs, sems):
        for cp in copies(outs, sems):
            cp.wait()

    return _Plan(stacks, [jax.ShapeDtypeStruct(s.shape, s.dtype) for s in stacks], {w: w for w in range(n)},
                 [pltpu.SemaphoreType.DMA((n, N_DEVICES - 1))] * 2, start, finish)


def _small_sum(vstack, pstack, me):
    def body(me_ref, vbuf, pbuf, vec_ref, pw_ref):
        vec = vbuf[me_ref[0]]
        pw = pbuf[me_ref[0]]
        for dev in range(1, N_DEVICES):
            vec = vec + vbuf[jnp.bitwise_xor(me_ref[0], dev)]
            pw = pw + pbuf[jnp.bitwise_xor(me_ref[0], dev)]
        vec_ref[...] = vec
        pw_ref[...] = pw

    full = lambda s: pl.BlockSpec(s.shape, lambda i, me: (0,) * len(s.shape))
    outs = [jax.ShapeDtypeStruct(vstack.shape[1:], F32), jax.ShapeDtypeStruct(pstack.shape[1:], F32)]
    return pl.pallas_call(
        body, out_shape=outs,
        grid_spec=pltpu.PrefetchScalarGridSpec(num_scalar_prefetch=1, grid=(1,), in_specs=[full(vstack), full(pstack)],
                                               out_specs=[full(o) for o in outs]),
        compiler_params=_params(), name="small_sum",
    )(me, vstack, pstack)


def _adamw(ws, gs, ms, vs, tag):
    n = len(ws)
    rows, cols = ws[0].shape
    rb = rows
    while rb * cols * 4 * n > (1 << 20) and rb % 16 == 0:
        rb //= 2

    def body(*refs):
        for j in range(n):
            w_ref, g_ref, m_ref, v_ref = (refs[k * n + j] for k in range(4))
            go_ref, d_ref, mo_ref, vo_ref = (refs[(4 + k) * n + j] for k in range(4))
            gv = g_ref[...]
            go_ref[...] = gv
            m2 = ADAM_B1 * m_ref[...] + (1.0 - ADAM_B1) * gv
            v2 = ADAM_B2 * v_ref[...] + (1.0 - ADAM_B2) * (gv * gv)
            m_hat = m2 / (1.0 - ADAM_B1 ** ADAM_STEP)
            v_hat = v2 / (1.0 - ADAM_B2 ** ADAM_STEP)
            d_ref[...] = -ADAM_LR * (m_hat / (jnp.sqrt(v_hat) + ADAM_EPS) + ADAM_WD * w_ref[...])
            mo_ref[...] = m2
            vo_ref[...] = v2

    spec = pl.BlockSpec((rb, cols), lambda i: (i, 0))
    res, _ = _pallas(
        body, name="adamw_" + tag, args=[*ws, *gs, *ms, *vs], out_shape=[jax.ShapeDtypeStruct(ws[0].shape, F32)] * (4 * n),
        grid=(rows // rb,), in_specs=[spec] * (4 * n), out_specs=[spec] * (4 * n))
    return [tuple(res[k * n + j] for k in range(4)) for j in range(n)]


def _pack_vec(p, d, width):
    pad = lambda v: jnp.pad(v, (0, LANES - v.shape[0]))
    row3 = jnp.concatenate([p["pool_scale"], p["out_norm_pool"]])
    row4 = jnp.concatenate([p["out_norm_attn"], pad(p["q_norm"]), pad(p["k_norm"]), pad(p["b_forget"]),
                            jnp.zeros((d - width - 3 * LANES,), F32)])
    rows = [p["ffn1_norm"], p["mix_norm"], p["ffn2_norm"], row3, row4]
    return jnp.pad(jnp.stack(rows), ((0, VEC_ROWS - len(rows)), (0, 0)))


def _unpack_vec(vec, width):
    return dict(ffn1_norm=vec[0], mix_norm=vec[1], ffn2_norm=vec[2], pool_scale=vec[3, :width],
                out_norm_pool=vec[3, width:2 * width], out_norm_attn=vec[4, :width],
                q_norm=vec[4, width:width + HEAD_DIM], k_norm=vec[4, width + LANES:width + LANES + HEAD_DIM],
                b_forget=vec[4, width + 2 * LANES:width + 2 * LANES + N_HEADS])


WEIGHT_NAMES = ("ffn1_norm", "ffn1_w_gate", "ffn1_w_up", "ffn1_w_down", "mix_norm", "w_in", "b_forget", "pool_w",
                "pool_scale", "q_norm", "k_norm", "out_norm_pool", "out_norm_attn", "w_out", "ffn2_norm",
                "ffn2_w_gate", "ffn2_w_up", "ffn2_w_down")
BIG_NAMES = ("ffn1_w_gate", "ffn1_w_up", "ffn1_w_down", "w_in", "w_out", "ffn2_w_gate", "ffn2_w_up", "ffn2_w_down")
TRANSPOSED_NAMES = ("ffn1_w_gate", "ffn1_w_up", "w_in", "ffn2_w_gate", "ffn2_w_up")
FFN1_NAMES = ("ffn1_w_gate", "ffn1_w_up", "ffn1_w_down")
MIX_NAMES = ("w_in", "w_out")
FFN2_NAMES = ("ffn2_w_gate", "ffn2_w_up", "ffn2_w_down")


def kernel(x, ffn1_norm, ffn1_w_gate, ffn1_w_up, ffn1_w_down, mix_norm, w_in, b_forget, pool_w, pool_scale, q_norm, k_norm, out_norm_pool, out_norm_attn, w_out, ffn2_norm, ffn2_w_gate, ffn2_w_up, ffn2_w_down, loss_target, m_ffn1_norm, m_ffn1_w_gate, m_ffn1_w_up, m_ffn1_w_down, m_mix_norm, m_w_in, m_b_forget, m_pool_w, m_pool_scale, m_q_norm, m_k_norm, m_out_norm_pool, m_out_norm_attn, m_w_out, m_ffn2_norm, m_ffn2_w_gate, m_ffn2_w_up, m_ffn2_w_down, v_ffn1_norm, v_ffn1_w_gate, v_ffn1_w_up, v_ffn1_w_down, v_mix_norm, v_w_in, v_b_forget, v_pool_w, v_pool_scale, v_q_norm, v_k_norm, v_out_norm_pool, v_out_norm_attn, v_w_out, v_ffn2_norm, v_ffn2_w_gate, v_ffn2_w_up, v_ffn2_w_down):
    given = dict(locals())
    w = {n: given[n] for n in WEIGHT_NAMES}
    m = {n: given["m_" + n] for n in WEIGHT_NAMES}
    v = {n: given["v_" + n] for n in WEIGHT_NAMES}
    n_batch, seq, d = x.shape
    width = pool_scale.shape[0]
    in_rows = w_in.shape[1]
    in_cols = N_CHIPS * in_rows
    in_pad = -(-in_rows // 32) * 32
    in_cols_pad = in_cols - N_HEADS + LANES

    work = lambda a, n: a.T if n in TRANSPOSED_NAMES else a
    exchanged = lambda a, n: jnp.pad(a, ((0, in_pad - in_rows), (0, 0))) if n == "w_in" else a

    mesh_x, mesh_y, mesh_c = _mesh_pos()
    ids = jnp.stack([2 * mesh_x + mesh_y, mesh_c]).astype(jnp.int32)

    row = lambda a: a.reshape(1, -1)
    g1, gm, g2, ps, onp, ona = (row(a) for a in (ffn1_norm, mix_norm, ffn2_norm, pool_scale, out_norm_pool, out_norm_attn))
    qn, kn = row(jnp.tile(q_norm, N_HEADS)), row(jnp.tile(k_norm, N_HEADS))
    bf = row(jnp.pad(b_forget, (0, LANES - N_HEADS)))
    pwb = pool_w.astype(BF16)
    xf, tgt = x.reshape(n_batch * seq, d), loss_target.reshape(n_batch * seq, d)

    def grouped(call, names, *lists):
        out = [None] * len(names)
        for idx in _same_shape_groups(lists[0]):
            res = call(*[[lst[i] for i in idx] for lst in lists], names[idx[0]])
            for i, r in zip(idx, res):
                out[i] = r
        return out

    placed = dict(zip(BIG_NAMES, grouped(lambda ws, tag: _place_cast(ws, ids, tag), BIG_NAMES,
                                         [exchanged(work(w[n], n), n) for n in BIG_NAMES])))
    landing = jnp.stack([2 * cx + cy for cx, cy in [(mesh_x, mesh_y)] + _other_chips(mesh_x, mesh_y)]).astype(jnp.int32)
    (x1, h1, a1, b1, s1), (wg1, wu1, wd1), (w_in_all, w_out_all, wd2) = _ffn_fwd_gathering(
        xf, g1, [placed[n] for n in FFN1_NAMES], landing, _plan_gather([placed[n] for n in MIX_NAMES + FFN2_NAMES[2:]]))
    w_in_t = jnp.pad(w_in_all[:, :in_rows].reshape(in_cols, d), ((0, in_cols_pad - in_cols), (0, 0)))
    w_out_full = w_out_all.reshape(N_CHIPS * w_out.shape[0], d)
    woa, wob = w_out_full[:width], w_out_full[width:]

    hm, pv, q, k, qh, kh, vb, f = _mix_proj(x1, gm, w_in_t, qn, kn, width, width)
    qa, ka = _forget_prefix(f, bf, qh, kh, n_batch, seq)
    yp = _pool_fwd(pv, pwb, ps, onp, n_batch, seq)
    (o, lse), (wg2, wu2) = _attn_fwd(qa, ka, vb, n_batch, seq, plan=_plan_gather([placed[n] for n in FFN2_NAMES[:2]]))
    x2, ya = _mix_out(x1, yp, o, ona, woa, wob)
    (dy, h2, a2, b2, s2, lpart, dyh), _ = _ffn_fwd(x2, g2, wg2, wu2, wd2, target=tgt)

    def to_chips(gs, arrived, tags):
        return grouped(lambda g, r, tag: _add_sibling(g, r, ids, tag), tags, gs, arrived)

    def own_rows(gs, from_sibling, from_chips, tags):
        return grouped(lambda g, ra, rb, tag: _add_chips(g, ra, rb, ids, tag), tags, gs, from_sibling, from_chips)

    (dx2, da2, db2, dg2), _ = _ffn_bwd_x(dy, x2, g2, a2, b2, wg2, wu2, wd2, "ffn2_bwd_x")
    dw2, _ = _ffn_bwd_w([(da2, h2), (db2, h2), (s2, dyh)], "ffn2_bwd_w")
    (dyp, do, delta, dwoa, dwob, dona), sib2 = _mix_out_bwd(dx2, o, yp, ya, ona, woa, wob, plan=_plan_sibling_halves(dw2))
    dpv, dpw, dps, donp = _pool_bwd(pv, dyp, pwb, ps, onp, n_batch, seq)
    (dqh, dkh, dv, dfq, dfk), chips2 = _attn_bwd(qa, ka, vb, do, lse, delta, n_batch, seq,
                                                 plan=_plan_chip_exchange(to_chips(dw2, sib2, FFN2_NAMES)))
    df, dbf = _forget_bwd(dfq, dfk, f, bf, n_batch, seq)
    dx1, dx1h, dw_in_t, dgm, dqn, dkn = _mix_in_bwd(dx2, x1, gm, hm, dpv, dqh, q, dkh, k, dv, df, qn, kn, w_in_t)
    in_base = [in_rows * k // 8 * 8 for k in range(N_CHIPS)]
    d_w_in = jnp.stack([dw_in_t[b:b + in_pad] for b in in_base])
    d_w_out = jnp.concatenate([dwoa, dwob], axis=0).reshape(N_CHIPS, w_out.shape[0], d)
    dwm = [d_w_in, d_w_out]
    down = FFN1_NAMES[2:]
    dwd1, sibm = _ffn_bwd_w([(s1, dx1h)], "ffn1_bwd_w_down", plan=_plan_sibling_halves(dwm))
    (da1, db1), arrived = _ffn_bwd_a(dx1h, a1, b1, wd1, "ffn1_bwd_a",
                                     plan=_merge_plans(_plan_sibling_halves(dwd1),
                                                       _plan_chip_exchange(to_chips(dwm, sibm, MIX_NAMES))))
    sibd, chipsm = arrived[:1], arrived[1:]
    gate_up = FFN1_NAMES[:2]
    dwgu1, chipsd = _ffn_bwd_w([(da1, h1), (db1, h1)], "ffn1_bwd_w_gate_up",
                               plan=_plan_chip_exchange(to_chips(dwd1, sibd, down)))
    n_tiles = (n_batch * seq) // min(FFN_TILE, n_batch * seq)
    first = max(n_tiles // 4, 1)
    begun, sibgu = _ffn_bwd_h(dx1, xf, g1, da1, db1, wg1, wu1, "ffn1_bwd_h_first", (0, first),
                              plan=_plan_sibling_halves(dwgu1))
    (gx, dg1), chipsgu = _ffn_bwd_h(dx1, xf, g1, da1, db1, wg1, wu1, "ffn1_bwd_h_rest", (first, n_tiles), prev=begun,
                                    plan=_plan_chip_exchange(to_chips(dwgu1, sibgu, gate_up)))

    part = dict(ffn1_norm=dg1, mix_norm=dgm, ffn2_norm=dg2, b_forget=dbf, pool_scale=dps, out_norm_pool=donp,
                out_norm_attn=dona, qn=dqn, kn=dkn, pool_w=dpw.reshape(n_batch, -1, pool_w.shape[-1]), loss=lpart)
    mine = (own_rows(dwgu1, sibgu, chipsgu, gate_up) + own_rows(dwd1, sibd, chipsd, down)
            + own_rows(dwm, sibm, chipsm, MIX_NAMES) + own_rows(dw2, sib2, chips2, FFN2_NAMES))
    last = _run_plan(_merge_plans(_plan_sibling_share(mine), _plan_all_to_all(_small_pack(part, d, width))), "last_exchange")
    vstack, pstack = last[len(mine):]
    g_vec, g_pw = _small_sum(vstack, pstack, jnp.reshape(4 * mesh_x + 2 * mesh_y + mesh_c, (1,)).astype(jnp.int32))
    loss = g_vec[5, 0]
    reduced = dict(zip(FFN1_NAMES + MIX_NAMES + FFN2_NAMES, last[:len(mine)]))
    reduced["w_in"] = lax.dynamic_slice(reduced["w_in"], ((in_rows * ids[0]) % 8, 0), (in_rows, d))

    grads, delta, new_m, new_v = {}, {}, {}, {}
    for names in (FFN2_NAMES, FFN1_NAMES, ("w_in",), ("w_out",)):
        stepped = _adamw([work(w[n], n) for n in names], [reduced[n] for n in names], [work(m[n], n) for n in names],
                         [work(v[n], n) for n in names], names[0])
        for n, step in zip(names, stepped):
            grads[n], delta[n], new_m[n], new_v[n] = (work(a, n) for a in step)
    flat_pw = lambda a: a.reshape(-1, a.shape[-1])
    (_, d_pw, m_pw, v_pw), = _adamw([flat_pw(pool_w)], [g_pw], [flat_pw(m_pool_w)], [flat_pw(v_pool_w)], "pool_w")
    (_, d_vec, m_vec, v_vec), = _adamw([_pack_vec(w, d, width)], [g_vec], [_pack_vec(m, d, width)],
                                       [_pack_vec(v, d, width)], "vectors")
    grads.update(_unpack_vec(g_vec, width), pool_w=g_pw.reshape(pool_w.shape))
    delta.update(_unpack_vec(d_vec, width), pool_w=d_pw.reshape(pool_w.shape))
    new_m.update(_unpack_vec(m_vec, width), pool_w=m_pw.reshape(pool_w.shape))
    new_v.update(_unpack_vec(v_vec, width), pool_w=v_pw.reshape(pool_w.shape))
    return (loss, gx.reshape(x.shape), *[grads[n] for n in WEIGHT_NAMES], *[delta[n] for n in WEIGHT_NAMES],
            *[new_m[n] for n in WEIGHT_NAMES], *[new_v[n] for n in WEIGHT_NAMES])
```

```python
import functools

import jax
import jax.numpy as jnp
from jax import lax
from jax.experimental import pallas as pl
from jax.experimental.pallas import tpu as pltpu

F32 = jnp.float32
BF16 = jnp.bfloat16
EPS = 1e-6
NEG = -1e30
ADAM_LR = 0.001
ADAM_B1 = 0.9
ADAM_B2 = 0.999
ADAM_EPS = 1e-08
ADAM_WD = 0.01
ADAM_STEP = 10
POOL_WINDOWS = (2, 4, 8, 16)
HEAD_DIM = 64
N_HEADS = 8
LANES = 128
N_CHIPS = 4
ATT_BLOCK = 512
ATT_SUB = 128
FFN_TILE = 1024
FFN_STAGED_TILE = 512
VMEM_LIMIT = 62 * 1024 * 1024
ANY = pl.BlockSpec(memory_space=pl.ANY)
VM = pl.BlockSpec(memory_space=pltpu.VMEM)


def _params(**kw):
    return pltpu.CompilerParams(vmem_limit_bytes=VMEM_LIMIT, **kw)


def _dot(a, b):
    return jnp.dot(a, b, preferred_element_type=F32)


def _dot_nt(a, b):
    return lax.dot_general(a, b, (((1,), (1,)), ((), ())), preferred_element_type=F32)


def _dot_tn(a, b):
    return lax.dot_general(a, b, (((0,), (0,)), ((), ())), preferred_element_type=F32)


def _sigmoid(z):
    return 1.0 / (1.0 + jnp.exp(-z))


def _rms(xf):
    return lax.rsqrt(jnp.mean(xf * xf, axis=-1, keepdims=True) + EPS)


def _rms_bwd(xf, r, gain, dh):
    xh = xf * r
    dyg = dh * gain
    return r * (dyg - xh * jnp.mean(dyg * xh, axis=-1, keepdims=True)), dh * xh


def _total(v):
    return jnp.sum(jnp.sum(v, axis=1, keepdims=True), axis=0, keepdims=True)


def _ffn_fwd(x, gain, wg, wu, wd, target=None, plan=None):
    t, d = x.shape
    nch, fc, _ = wg.shape
    tm = min(FFN_TILE, t)
    nt = t // tm
    with_loss = target is not None

    def body(*refs):
        if with_loss:
            x_ref, g_ref, wg_ref, wu_ref, wd_ref, t_ref, o_ref, h_ref, a_ref, b_ref, s_ref, l_ref, oh_ref, acc_ref = refs
        else:
            x_ref, g_ref, wg_ref, wu_ref, wd_ref, o_ref, h_ref, a_ref, b_ref, s_ref, acc_ref = refs
        k = pl.program_id(1)

        @pl.when(k == 0)
        def _():
            xf = x_ref[...]
            h_ref[...] = ((xf * _rms(xf)) * g_ref[...]).astype(BF16)
            acc_ref[...] = jnp.zeros_like(acc_ref)

        for rows in _row_halves(tm):
            h = h_ref[rows, :]
            a = _dot_nt(h, wg_ref[...])
            b = _dot_nt(h, wu_ref[...])
            sb = ((a * (0.5 * jnp.tanh(0.5 * a) + 0.5)) * b).astype(BF16)
            a_ref[rows, :] = a.astype(BF16)
            b_ref[rows, :] = b.astype(BF16)
            s_ref[rows, :] = sb
            acc_ref[rows, :] += _dot(sb, wd_ref[...])

        @pl.when(k == nch - 1)
        def _():
            y = x_ref[...] + 0.5 * acc_ref[...]
            if with_loss:
                e = y - t_ref[...]
                o_ref[...] = e * (1.0 / d)
                oh_ref[...] = (e * (0.5 / d)).astype(BF16)
                l_ref[...] = jnp.broadcast_to(_total(e * e) * (0.5 / d), l_ref.shape)
            else:
                o_ref[...] = y

    row = pl.BlockSpec((tm, d), lambda i, k: (i, 0))
    chunk = pl.BlockSpec((None, fc, d), lambda i, k: (k, 0, 0))
    act = pl.BlockSpec((None, tm, fc), lambda i, k: (k, i, 0))
    in_specs = [row, pl.BlockSpec((1, d), lambda i, k: (0, 0)), chunk, chunk, chunk]
    out_shape = [jax.ShapeDtypeStruct((t, d), F32), jax.ShapeDtypeStruct((t, d), BF16)]
    out_shape += [jax.ShapeDtypeStruct((nch, t, fc), BF16)] * 3
    out_specs = [row, row, act, act, act]
    args = [x, gain, wg, wu, wd]
    if with_loss:
        in_specs.append(row)
        args.append(target)
        out_shape += [jax.ShapeDtypeStruct((nt, 8, LANES), F32), jax.ShapeDtypeStruct((t, d), BF16)]
        out_specs += [pl.BlockSpec((None, 8, LANES), lambda i, k: (i, 0, 0)), row]
    return _pallas(body, name="ffn_fwd_loss" if with_loss else "ffn_fwd", args=args, in_specs=in_specs,
                   out_shape=out_shape, out_specs=out_specs, grid=(nt, nch),
                   scratch_shapes=[pltpu.VMEM((tm, d), F32)], plan=plan)


def _row_halves(n):
    return [slice(0, n // 2), slice(n // 2, n)]


def _swiglu_grads(dyh, a_ref, b_ref, wd_ref, rows):
    ds = _dot_nt(dyh, wd_ref[...])
    av = a_ref[rows, :].astype(F32)
    bv = b_ref[rows, :].astype(F32)
    th = jnp.tanh(0.5 * av)
    sig = 0.5 * th + 0.5
    dab = ((ds * bv) * (sig * (1.0 + av * (0.5 - 0.5 * th)))).astype(BF16)
    return dab, (ds * (av * sig)).astype(BF16)


def _ffn_bwd_a(dyh, a, b, wd, name, plan=None):
    t, d = dyh.shape
    nch, fc, _ = wd.shape
    tm = min(FFN_TILE, t)

    def body(dyh_ref, a_ref, b_ref, wd_ref, da_ref, db_ref):
        for rows in _row_halves(tm):
            da_ref[rows, :], db_ref[rows, :] = _swiglu_grads(dyh_ref[rows, :], a_ref, b_ref, wd_ref, rows)

    act = pl.BlockSpec((None, tm, fc), lambda i, k: (k, i, 0))
    return _pallas(
        body, name=name, args=[dyh, a, b, wd], out_shape=[jax.ShapeDtypeStruct((nch, t, fc), BF16)] * 2, grid=(t // tm, nch),
        in_specs=[pl.BlockSpec((tm, d), lambda i, k: (i, 0)), act, act, pl.BlockSpec((None, fc, d), lambda i, k: (k, 0, 0))],
        out_specs=[act, act], plan=plan)


def _ffn_bwd_h(dy, x, gain, da, db, wg, wu, name, tiles, prev=None, plan=None):
    t, d = x.shape
    nch, fc, _ = wg.shape
    tm = min(FFN_TILE, t)
    nt = t // tm
    t0, t1 = tiles

    def body(*refs):
        dy_ref, x_ref, g_ref, da_ref, db_ref, wg_ref, wu_ref = refs[:7]
        dx_ref, dg_ref, acc_ref = refs[-3:]
        k = pl.program_id(1)

        @pl.when(k == 0)
        def _():
            acc_ref[...] = jnp.zeros_like(acc_ref)

        acc_ref[...] += _dot(da_ref[...], wg_ref[...]) + _dot(db_ref[...], wu_ref[...])

        @pl.when(k == nch - 1)
        def _():
            xf = x_ref[...]
            dxn, dgr = _rms_bwd(xf, _rms(xf), g_ref[...], acc_ref[...])
            dx_ref[...] = dy_ref[...] + dxn
            dg_ref[...] = jnp.sum(dgr, axis=0, keepdims=True)

    row = pl.BlockSpec((tm, d), lambda i, k: (i + t0, 0))
    chunk = pl.BlockSpec((None, fc, d), lambda i, k: (k, 0, 0))
    act = pl.BlockSpec((None, tm, fc), lambda i, k: (k, i + t0, 0))
    args = [dy, x, gain, da, db, wg, wu]
    in_specs = [row, row, pl.BlockSpec((1, d), lambda i, k: (0, 0)), act, act, chunk, chunk]
    aliases = {}
    if prev is not None:
        aliases = {len(args): 0, len(args) + 1: 1}
        args += list(prev)
        in_specs += [ANY, ANY]
    return _pallas(
        body, name=name, args=args, out_shape=[jax.ShapeDtypeStruct((t, d), F32), jax.ShapeDtypeStruct((nt, 1, d), F32)],
        grid=(t1 - t0, nch), in_specs=in_specs,
        out_specs=[row, pl.BlockSpec((None, 1, d), lambda i, k: (i + t0, 0, 0))],
        scratch_shapes=[pltpu.VMEM((tm, d), F32)], plan=plan, aliases=aliases)


def _ffn_bwd_x(dy, x, gain, a, b, wg, wu, wd, name, plan=None):
    t, d = x.shape
    nch, fc, _ = wg.shape
    tm = min(FFN_TILE, t)
    nt = t // tm

    def body(dy_ref, x_ref, g_ref, a_ref, b_ref, wg_ref, wu_ref, wd_ref, dx_ref, da_ref, db_ref, dg_ref, acc_ref):
        k = pl.program_id(1)

        @pl.when(k == 0)
        def _():
            acc_ref[...] = jnp.zeros_like(acc_ref)

        for rows in _row_halves(tm):
            dab, dbb = _swiglu_grads((0.5 * dy_ref[rows, :]).astype(BF16), a_ref, b_ref, wd_ref, rows)
            da_ref[rows, :] = dab
            db_ref[rows, :] = dbb
            acc_ref[rows, :] += _dot(dab, wg_ref[...]) + _dot(dbb, wu_ref[...])

        @pl.when(k == nch - 1)
        def _():
            xf = x_ref[...]
            dxn, dgr = _rms_bwd(xf, _rms(xf), g_ref[...], acc_ref[...])
            dx_ref[...] = dy_ref[...] + dxn
            dg_ref[...] = jnp.sum(dgr, axis=0, keepdims=True)

    row = pl.BlockSpec((tm, d), lambda i, k: (i, 0))
    chunk = pl.BlockSpec((None, fc, d), lambda i, k: (k, 0, 0))
    act = pl.BlockSpec((None, tm, fc), lambda i, k: (k, i, 0))
    return _pallas(
        body, name=name, args=[dy, x, gain, a, b, wg, wu, wd],
        out_shape=[jax.ShapeDtypeStruct((t, d), F32), jax.ShapeDtypeStruct((nch, t, fc), BF16),
                   jax.ShapeDtypeStruct((nch, t, fc), BF16), jax.ShapeDtypeStruct((nt, 1, d), F32)],
        grid=(nt, nch),
        in_specs=[row, row, pl.BlockSpec((1, d), lambda i, k: (0, 0)), act, act, chunk, chunk, chunk],
        out_specs=[row, act, act, pl.BlockSpec((None, 1, d), lambda i, k: (i, 0, 0))],
        scratch_shapes=[pltpu.VMEM((tm, d), F32)], plan=plan)


def _ffn_bwd_w(pairs, name, plan=None):
    n = len(pairs)
    nch, t, fc = pairs[0][0].shape
    d = pairs[0][1].shape[1]
    tm = min(FFN_TILE, t)

    def body(*refs):
        @pl.when(pl.program_id(1) == 0)
        def _():
            for o_ref in refs[2 * n:]:
                o_ref[...] = jnp.zeros_like(o_ref)

        for j in range(n):
            refs[2 * n + j][...] += _dot_tn(refs[j][...], refs[n + j][...])

    row = pl.BlockSpec((tm, d), lambda k, i: (i, 0))
    act = pl.BlockSpec((None, tm, fc), lambda k, i: (k, i, 0))
    chunk = pl.BlockSpec((None, fc, d), lambda k, i: (k, 0, 0))
    return _pallas(body, name=name, args=[p[0] for p in pairs] + [p[1] for p in pairs],
                   out_shape=[jax.ShapeDtypeStruct((nch, fc, d), F32)] * n, grid=(nch, t // tm),
                   in_specs=[act] * n + [row] * n, out_specs=[chunk] * n, plan=plan)


def _head_masks():
    lane = lax.broadcasted_iota(jnp.int32, (1, LANES), 1)
    return lane < HEAD_DIM


def _head_rms(x, lo):
    x2 = x * x
    s0 = jnp.sum(jnp.where(lo, x2, 0.0), axis=1, keepdims=True)
    s1 = jnp.sum(jnp.where(lo, 0.0, x2), axis=1, keepdims=True)
    return jnp.where(lo, lax.rsqrt(s0 * (1.0 / HEAD_DIM) + EPS), lax.rsqrt(s1 * (1.0 / HEAD_DIM) + EPS))


def _head_mean(v, lo):
    s0 = jnp.sum(jnp.where(lo, v, 0.0), axis=1, keepdims=True)
    s1 = jnp.sum(jnp.where(lo, 0.0, v), axis=1, keepdims=True)
    return jnp.where(lo, s0, s1) * (1.0 / HEAD_DIM)


def _mix_proj(x1, gain, wt, qn, kn, pool_width, attn_width):
    t, d = x1.shape
    tm = min(512, t)
    nt = t // tm
    scale = HEAD_DIM ** -0.5
    c_q, c_k, c_v = pool_width, pool_width + attn_width, pool_width + 2 * attn_width
    c_f = c_v + attn_width

    def body(x_ref, g_ref, wt_ref, qn_ref, kn_ref, hm_ref, pv_ref, q_ref, k_ref, qh_ref, kh_ref, vb_ref, f_ref):
        lo = _head_masks()
        for rows in _row_halves(tm):
            xf = x_ref[rows, :]
            hm = ((xf * _rms(xf)) * g_ref[...]).astype(BF16)
            hm_ref[rows, :] = hm
            f_ref[rows, :] = _dot_nt(hm, wt_ref[c_f:c_f + LANES, :])
            pv_ref[rows, :] = _dot_nt(hm, wt_ref[0:pool_width, :])
            vb_ref[rows, :] = _dot_nt(hm, wt_ref[c_v:c_v + attn_width, :]).astype(BF16)
            for c0, raw_ref, hat_ref, n_ref, mul in ((c_q, q_ref, qh_ref, qn_ref, scale), (c_k, k_ref, kh_ref, kn_ref, 1.0)):
                raw = _dot_nt(hm, wt_ref[c0:c0 + attn_width, :])
                raw_ref[rows, :] = raw
                for blk in range(attn_width // LANES):
                    sl = slice(blk * LANES, (blk + 1) * LANES)
                    xb = raw[:, sl]
                    hat_ref[rows, sl] = (((xb * _head_rms(xb, lo)) * n_ref[:, sl]) * mul).astype(BF16)

    row = pl.BlockSpec((tm, d), lambda i: (i, 0))
    half = pl.BlockSpec((tm, attn_width), lambda i: (i, 0))
    const = lambda shape: pl.BlockSpec(shape, lambda i: (0, 0))
    return _pallas(
        body, name="mix_proj", args=[x1, gain, wt, qn, kn],
        out_shape=[jax.ShapeDtypeStruct((t, d), BF16), jax.ShapeDtypeStruct((t, pool_width), F32),
                   jax.ShapeDtypeStruct((t, attn_width), F32), jax.ShapeDtypeStruct((t, attn_width), F32),
                   jax.ShapeDtypeStruct((t, attn_width), BF16), jax.ShapeDtypeStruct((t, attn_width), BF16),
                   jax.ShapeDtypeStruct((t, attn_width), BF16), jax.ShapeDtypeStruct((t, LANES), F32)],
        grid=(nt,),
        in_specs=[row, const((1, d)), const(wt.shape), const((1, attn_width)), const((1, attn_width))],
        out_specs=[row, pl.BlockSpec((tm, pool_width), lambda i: (i, 0)), half, half, half, half, half,
                   pl.BlockSpec((tm, LANES), lambda i: (i, 0))])[0]


def _shift_down(v, dist, row):
    return jnp.where(row >= dist, pltpu.roll(v, dist, 0), 0.0)


def _shift_up(v, dist, row, n):
    return jnp.where(row + dist < n, pltpu.roll(v, n - dist, 0), 0.0)


def _aug_lane(e):
    return HEAD_DIM if e == 0 else 0


def _forget_prefix(f, bias, qh, kh, n_batch, seq):
    def body(f_ref, b_ref, q_ref, k_ref, qa_ref, ka_ref):
        z = f_ref[...] + b_ref[...]
        acc = jnp.minimum(z, 0.0) - jnp.log(1.0 + jnp.exp(-jnp.abs(z)))
        row = lax.broadcasted_iota(jnp.int32, (seq, 1), 0)
        dist = 1
        while dist < seq:
            acc = acc + _shift_down(acc, dist, row)
            dist *= 2
        lane = lax.broadcasted_iota(jnp.int32, (1, LANES), 1)
        for h in range(N_HEADS):
            pair, e = divmod(h, 2)
            a0 = _aug_lane(e)
            own = (lane < HEAD_DIM) if e == 0 else (lane >= HEAD_DIM)
            fh = _pick_lane(acc, h)
            hi = fh.astype(BF16).astype(F32)
            rest = fh - hi
            mid = rest.astype(BF16).astype(F32)
            low = rest - mid
            q_ones = (lane >= a0 + 3) & (lane < a0 + 6)
            k_ones = (lane >= a0) & (lane < a0 + 3)
            q_aug = jnp.where(lane == a0, hi, jnp.where(lane == a0 + 1, mid, jnp.where(lane == a0 + 2, low,
                              jnp.where(q_ones, 1.0, 0.0))))
            k_aug = jnp.where(k_ones, 1.0, jnp.where(lane == a0 + 3, -hi, jnp.where(lane == a0 + 4, -mid,
                              jnp.where(lane == a0 + 5, -low, 0.0))))
            src = slice(pair * LANES, (pair + 1) * LANES)
            dst = slice(h * LANES, (h + 1) * LANES)
            qa_ref[:, dst] = jnp.where(own, q_ref[:, src].astype(F32), q_aug).astype(BF16)
            ka_ref[:, dst] = jnp.where(own, k_ref[:, src].astype(F32), k_aug).astype(BF16)

    width = qh.shape[1]
    tok = pl.BlockSpec((seq, width), lambda b: (b, 0))
    aug = pl.BlockSpec((seq, N_HEADS * LANES), lambda b: (b, 0))
    return pl.pallas_call(
        body, out_shape=[jax.ShapeDtypeStruct((n_batch * seq, N_HEADS * LANES), BF16)] * 2, grid=(n_batch,),
        in_specs=[pl.BlockSpec((seq, LANES), lambda b: (b, 0)), pl.BlockSpec((1, LANES), lambda b: (0, 0)), tok, tok],
        out_specs=[aug, aug], compiler_params=_params(), name="forget_prefix",
    )(f, bias, qh, kh)


def _pool_groups(pv_ref, pw_ref, ps_ref, seq):
    row = lax.broadcasted_iota(jnp.int32, (seq, 1), 0)
    pos = (row + 1).astype(F32)
    out = []
    for g, win in enumerate(POOL_WINDOWS):
        sl = slice(g * LANES, (g + 1) * LANES)
        xg = pv_ref[:, sl]
        acc = xg
        dist = 1
        while dist < win:
            acc = acc + _shift_down(acc, dist, row)
            dist *= 2
        pooled = (acc / jnp.minimum(pos, float(win)) - xg).astype(BF16)
        mixed = _dot(pooled, pw_ref[g])
        out.append((pooled, mixed, mixed * ps_ref[:, sl]))
    return out


def _pool_fwd(pv, pw, ps, onp, n_batch, seq):
    width = pv.shape[1]

    def body(pv_ref, pw_ref, ps_ref, on_ref, y_ref):
        groups = _pool_groups(pv_ref, pw_ref, ps_ref, seq)
        ssq = sum(jnp.sum(ms * ms, axis=1, keepdims=True) for _, _, ms in groups)
        r = lax.rsqrt(ssq * (1.0 / width) + EPS)
        for g, (_, _, ms) in enumerate(groups):
            sl = slice(g * LANES, (g + 1) * LANES)
            y_ref[:, sl] = ((ms * r) * on_ref[:, sl]).astype(BF16)

    return pl.pallas_call(
        body, out_shape=jax.ShapeDtypeStruct((n_batch * seq, width), BF16), grid=(n_batch,),
        in_specs=[pl.BlockSpec((seq, width), lambda b: (b, 0)), pl.BlockSpec(pw.shape, lambda b: (0, 0, 0)),
                  pl.BlockSpec((1, width), lambda b: (0, 0)), pl.BlockSpec((1, width), lambda b: (0, 0))],
        out_specs=pl.BlockSpec((seq, width), lambda b: (b, 0)),
        compiler_params=_params(), name="pool_fwd",
    )(pv, pw, ps, onp)


def _pool_bwd(pv, dyp, pw, ps, onp, n_batch, seq):
    width = pv.shape[1]

    def body(pv_ref, dy_ref, pw_ref, ps_ref, on_ref, dpv_ref, dpw_ref, dps_ref, don_ref):
        groups = _pool_groups(pv_ref, pw_ref, ps_ref, seq)
        ssq = sum(jnp.sum(ms * ms, axis=1, keepdims=True) for _, _, ms in groups)
        r = lax.rsqrt(ssq * (1.0 / width) + EPS)
        mean = sum(jnp.sum((dy_ref[:, g * LANES:(g + 1) * LANES] * on_ref[:, g * LANES:(g + 1) * LANES]) * (ms * r),
                           axis=1, keepdims=True) for g, (_, _, ms) in enumerate(groups)) * (1.0 / width)
        row = lax.broadcasted_iota(jnp.int32, (seq, 1), 0)
        pos = (row + 1).astype(F32)
        for g, (pooled, mixed, ms) in enumerate(groups):
            sl = slice(g * LANES, (g + 1) * LANES)
            dy = dy_ref[:, sl]
            xh = ms * r
            don_ref[:, sl] = jnp.sum(dy * xh, axis=0, keepdims=True)
            dms = r * (dy * on_ref[:, sl] - xh * mean)
            dps_ref[:, sl] = jnp.sum(dms * mixed, axis=0, keepdims=True)
            dmix = (dms * ps_ref[:, sl]).astype(BF16)
            dpw_ref[g] = _dot_tn(pooled, dmix)
            dpool = _dot_nt(dmix, pw_ref[g])
            win = POOL_WINDOWS[g]
            acc = dpool / jnp.minimum(pos, float(win))
            dist = 1
            while dist < win:
                acc = acc + _shift_up(acc, dist, row, seq)
                dist *= 2
            dpv_ref[:, sl] = (acc - dpool).astype(BF16)

    tok = pl.BlockSpec((seq, width), lambda b: (b, 0))
    vec = pl.BlockSpec((1, width), lambda b: (0, 0))
    pvec = pl.BlockSpec((None, 1, width), lambda b: (b, 0, 0))
    return pl.pallas_call(
        body,
        out_shape=[jax.ShapeDtypeStruct((n_batch * seq, width), BF16),
                   jax.ShapeDtypeStruct((n_batch,) + pw.shape, F32),
                   jax.ShapeDtypeStruct((n_batch, 1, width), F32), jax.ShapeDtypeStruct((n_batch, 1, width), F32)],
        grid=(n_batch,),
        in_specs=[tok, tok, pl.BlockSpec(pw.shape, lambda b: (0, 0, 0)), vec, vec],
        out_specs=[tok, pl.BlockSpec((None,) + pw.shape, lambda b: (b, 0, 0, 0)), pvec, pvec],
        compiler_params=_params(), name="pool_bwd",
    )(pv, dyp, pw, ps, onp)


def _pick_lane(tile, idx):
    lane = lax.broadcasted_iota(jnp.int32, (1, LANES), 1)
    return jnp.sum(jnp.where(lane == idx, tile, 0.0), axis=1, keepdims=True)


def _pick_row(tile, idx):
    sub = lax.broadcasted_iota(jnp.int32, (tile.shape[0], 1), 0)
    return jnp.sum(jnp.where(sub == idx, tile, 0.0), axis=0, keepdims=True)


def _put_lane(col, idx):
    lane = lax.broadcasted_iota(jnp.int32, (1, LANES), 1)
    return jnp.where(lane == idx, col, 0.0)


def _head_select(e):
    lo = _head_masks()
    return lo if e == 0 else jnp.logical_not(lo)


def _causal(st, shift):
    row = lax.broadcasted_iota(jnp.int32, st.shape, 0)
    col = lax.broadcasted_iota(jnp.int32, st.shape, 1) + shift
    return jnp.where(col >= row, st, NEG)


def _transpose_blocks(a):
    rows, cols = a.shape
    return jnp.concatenate(
        [jnp.concatenate([a[r:r + LANES, c:c + LANES].T for r in range(0, rows, LANES)], axis=1)
         for c in range(0, cols, LANES)], axis=0)


def _accumulate(ref, value, first):
    @pl.when(first)
    def _():
        ref[...] = value

    @pl.when(jnp.logical_not(first))
    def _():
        ref[...] += value


def _attn_fwd(qa, ka, vb, n_batch, seq, plan=None):
    tq = min(ATT_BLOCK, seq)
    nq, nsub, tk = seq // tq, tq // ATT_SUB, tq
    pairs = vb.shape[1] // LANES

    def body(q_ref, k_ref, v_ref, o_ref, lse_ref, acc_ref):
        i, p = pl.program_id(1), pl.program_id(2)
        row_lo = lax.broadcasted_iota(jnp.int32, (LANES, 1), 0) < HEAD_DIM
        qs = [q_ref[:, e * LANES:(e + 1) * LANES] for e in range(2)]
        acc_ref[...] = jnp.zeros_like(acc_ref)

        def tile(off, stats, diagonal):
            vj = v_ref[pl.ds(off, tk), :]
            new, alphas, pvs = [], [], []
            for e in range(2):
                st = _dot_nt(k_ref[pl.ds(off, tk), e * LANES:(e + 1) * LANES], qs[e])
                if diagonal:
                    st = _causal(st, 0)
                m, l = stats[e]
                m_new = jnp.maximum(m, jnp.max(st, axis=0, keepdims=True))
                alpha = jnp.exp(m - m_new)
                pt = jnp.exp(st - m_new)
                new.append((m_new, alpha * l + jnp.sum(pt, axis=0, keepdims=True)))
                alphas.append(alpha)
                pvs.append(_dot_tn(jnp.where(_head_select(e), vj, jnp.zeros_like(vj)), pt.astype(BF16)))
            acc_ref[...] = acc_ref[...] * jnp.where(row_lo, alphas[0], alphas[1]) + (pvs[0] + pvs[1])
            return tuple(new)

        init = ((jnp.full((1, tq), NEG, F32), jnp.zeros((1, tq), F32)),) * 2
        stats = lax.fori_loop(0, i, lambda j, st: tile(pl.multiple_of(j * tk, tk), st, False), init)
        (m0, l0), (m1, l1) = tile(pl.multiple_of(i * tk, tk), stats, True)
        out_t = acc_ref[...] / jnp.where(row_lo, l0, l1)
        sub = lax.broadcasted_iota(jnp.int32, (8, 1), 0)
        lse0, lse1 = m0 + jnp.log(l0), m1 + jnp.log(l1)
        for a in range(nsub):
            sl = slice(a * ATT_SUB, (a + 1) * ATT_SUB)
            o_ref[sl, :] = out_t[:, sl].T
            rows = jnp.where(sub == 2 * p, lse0[:, sl], 0.0) + jnp.where(sub == 2 * p + 1, lse1[:, sl], 0.0)
            _accumulate(lse_ref.at[a], rows, p == 0)

    return _pallas(
        body, name="attn_fwd", args=[qa, ka, vb],
        out_shape=[jax.ShapeDtypeStruct((n_batch * seq, pairs * LANES), F32),
                   jax.ShapeDtypeStruct((n_batch * seq // ATT_SUB, 8, ATT_SUB), F32)],
        grid=(n_batch, nq, pairs),
        in_specs=[pl.BlockSpec((tq, 2 * LANES), lambda b, i, p: (b * nq + i, p)),
                  pl.BlockSpec((seq, 2 * LANES), lambda b, i, p: (b, p)),
                  pl.BlockSpec((seq, LANES), lambda b, i, p: (b, p))],
        out_specs=[pl.BlockSpec((tq, LANES), lambda b, i, p: (b * nq + i, p)),
                   pl.BlockSpec((nsub, 8, ATT_SUB), lambda b, i, p: (b * nq + i, 0, 0))],
        scratch_shapes=[pltpu.VMEM((LANES, tq), F32)], plan=plan)


def _attn_bwd(qa, ka, vb, do, lse, delta, n_batch, seq, plan=None):
    tq = min(ATT_BLOCK, seq)
    nq, nsub = seq // tq, tq // ATT_SUB
    n_tiles = seq // ATT_SUB
    pairs = vb.shape[1] // LANES

    def body(q_ref, k_ref, v_ref, do_ref, lse_ref, dl_ref, dq_ref, dk_ref, dv_ref, dfq_ref, dfk_ref,
             dq0_ref, dq1_ref, dk0_ref, dk1_ref, dva_ref):
        p = pl.program_id(1)
        dqs, dks = (dq0_ref, dq1_ref), (dk0_ref, dk1_ref)
        for acc in (dk0_ref, dk1_ref, dva_ref):
            acc[...] = jnp.zeros_like(acc)
        dfq_cols = []
        for i in range(nq):
            rows_i = slice(i * tq, (i + 1) * tq)
            qs = [q_ref[rows_i, e * LANES:(e + 1) * LANES] for e in range(2)]
            dov = do_ref[rows_i, :]
            does = [jnp.where(_head_select(e), dov, jnp.zeros_like(dov)) for e in range(2)]
            stat = lambda ref, e: jnp.concatenate([_pick_row(ref[i * nsub + a], 2 * p + e) for a in range(nsub)], axis=1)
            ls, dl = [stat(lse_ref, e) for e in range(2)], [stat(dl_ref, e) for e in range(2)]
            for acc in dqs:
                acc[...] = jnp.zeros_like(acc)

            def tile(off, diagonal, qs=qs, dov=dov, does=does, ls=ls, dl=dl):
                vj = v_ref[pl.ds(off, tq), :]
                for e in range(2):
                    kj = k_ref[pl.ds(off, tq), e * LANES:(e + 1) * LANES]
                    st = _dot_nt(kj, qs[e])
                    if diagonal:
                        st = _causal(st, 0)
                    pt = jnp.exp(st - ls[e])
                    dva_ref[pl.ds(off, tq), :] += _dot(pt.astype(BF16), does[e])
                    dpt = _dot_nt(jnp.where(_head_select(e), vj, jnp.zeros_like(vj)), dov)
                    dst = (pt * (dpt - dl[e])).astype(BF16)
                    dks[e][pl.ds(off, tq), :] += _dot(dst, qs[e])
                    dqs[e][...] += _dot(_transpose_blocks(kj), dst)

            def step(j, carry, tile=tile):
                tile(pl.multiple_of(j * tq, tq), False)
                return carry

            lax.fori_loop(0, i, step, 0)
            tile(i * tq, True)
            dq0, dq1 = _transpose_blocks(dq0_ref[...]), _transpose_blocks(dq1_ref[...])
            dq_ref[rows_i, :] = jnp.where(_head_masks(), dq0, dq1)
            dfq_cols.append(_put_lane(_pick_lane(dq0, _aug_lane(0)), 2 * p) + _put_lane(_pick_lane(dq1, _aug_lane(1)), 2 * p + 1))
        dk0, dk1 = dk0_ref[...], dk1_ref[...]
        dk_ref[...] = jnp.where(_head_masks(), dk0, dk1)
        dv_ref[...] = dva_ref[...].astype(BF16)
        dfk = _put_lane(_pick_lane(dk0, _aug_lane(0) + 3), 2 * p) + _put_lane(_pick_lane(dk1, _aug_lane(1) + 3), 2 * p + 1)
        _accumulate(dfq_ref, jnp.concatenate(dfq_cols, axis=0), p == 0)
        _accumulate(dfk_ref, -dfk, p == 0)

    wide = pl.BlockSpec((seq, 2 * LANES), lambda b, p: (b, p))
    blk = pl.BlockSpec((seq, LANES), lambda b, p: (b, p))
    col = pl.BlockSpec((seq, LANES), lambda b, p: (b, 0))
    stat = pl.BlockSpec((n_tiles, 8, ATT_SUB), lambda b, p: (b, 0, 0))
    f32_blk, acc = jax.ShapeDtypeStruct((n_batch * seq, pairs * LANES), F32), pltpu.VMEM((seq, LANES), F32)
    return _pallas(
        body, name="attn_bwd", args=[qa, ka, vb, do, lse, delta],
        out_shape=[f32_blk, f32_blk, jax.ShapeDtypeStruct((n_batch * seq, pairs * LANES), BF16),
                   jax.ShapeDtypeStruct((n_batch * seq, LANES), F32), jax.ShapeDtypeStruct((n_batch * seq, LANES), F32)],
        grid=(n_batch, pairs), in_specs=[wide, wide, blk, blk, stat, stat], out_specs=[blk, blk, blk, col, col],
        scratch_shapes=[pltpu.VMEM((LANES, tq), F32), pltpu.VMEM((LANES, tq), F32), acc, acc, acc], plan=plan)


def _forget_bwd(dfq, dfk, f, bias, n_batch, seq):
    def body(dfq_ref, dfk_ref, f_ref, b_ref, df_ref, db_ref):
        acc = dfq_ref[...] + dfk_ref[...]
        row = lax.broadcasted_iota(jnp.int32, (seq, 1), 0)
        dist = 1
        while dist < seq:
            acc = acc + _shift_up(acc, dist, row, seq)
            dist *= 2
        df = acc * _sigmoid(-(f_ref[...] + b_ref[...]))
        df_ref[...] = df
        db_ref[...] = jnp.sum(df, axis=0, keepdims=True)

    col = pl.BlockSpec((seq, LANES), lambda b: (b, 0))
    return pl.pallas_call(
        body,
        out_shape=[jax.ShapeDtypeStruct((n_batch * seq, LANES), F32), jax.ShapeDtypeStruct((n_batch, 1, LANES), F32)],
        grid=(n_batch,), in_specs=[col, col, col, pl.BlockSpec((1, LANES), lambda b: (0, 0))],
        out_specs=[col, pl.BlockSpec((None, 1, LANES), lambda b: (b, 0, 0))],
        compiler_params=_params(), name="forget_bwd",
    )(dfq, dfk, f, bias)


def _mix_out(x1, yp, o, ona, woa, wob):
    t, d = x1.shape
    width = o.shape[1]
    tm = min(512, t)

    def body(x_ref, yp_ref, o_ref, on_ref, wa_ref, wb_ref, x2_ref, ya_ref):
        of = o_ref[...]
        ya = ((of * _rms(of)) * on_ref[...]).astype(BF16)
        ya_ref[...] = ya
        x2_ref[...] = x_ref[...] + (_dot(yp_ref[...], wa_ref[...]) + _dot(ya, wb_ref[...]))

    row = pl.BlockSpec((tm, d), lambda i: (i, 0))
    half = pl.BlockSpec((tm, width), lambda i: (i, 0))
    wspec = pl.BlockSpec((width, d), lambda i: (0, 0))
    return pl.pallas_call(
        body, out_shape=[jax.ShapeDtypeStruct((t, d), F32), jax.ShapeDtypeStruct((t, width), BF16)],
        grid=(t // tm,), in_specs=[row, half, half, pl.BlockSpec((1, width), lambda i: (0, 0)), wspec, wspec],
        out_specs=[row, half], compiler_params=_params(), name="mix_out",
    )(x1, yp, o, ona, woa, wob)


def _mix_out_bwd(dx2, o, yp, ya, ona, woa, wob, plan=None):
    t, d = dx2.shape
    width = o.shape[1]
    tm = min(512, t)
    nt = t // tm

    def body(dx_ref, o_ref, yp_ref, ya_ref, on_ref, wa_ref, wb_ref, dyp_ref, do_ref, dl_ref, dwa_ref, dwb_ref, don_ref):
        @pl.when(pl.program_id(0) == 0)
        def _():
            dwa_ref[...] = jnp.zeros_like(dwa_ref)
            dwb_ref[...] = jnp.zeros_like(dwb_ref)

        dxb = dx_ref[...].astype(BF16)
        dwa_ref[...] += _dot_tn(yp_ref[...], dxb)
        dwb_ref[...] += _dot_tn(ya_ref[...], dxb)
        dyp_ref[...] = _dot_nt(dxb, wa_ref[...])
        of = o_ref[...]
        dov, dgr = _rms_bwd(of, _rms(of), on_ref[...], _dot_nt(dxb, wb_ref[...]))
        don_ref[...] = jnp.sum(dgr, axis=0, keepdims=True)
        do_ref[...] = dov.astype(BF16)
        lo = _head_masks()
        prod = dov * of
        delta = jnp.zeros((tm, LANES), F32)
        for blk in range(width // LANES):
            pb = prod[:, blk * LANES:(blk + 1) * LANES]
            delta = delta + _put_lane(jnp.sum(jnp.where(lo, pb, 0.0), axis=1, keepdims=True), 2 * blk)
            delta = delta + _put_lane(jnp.sum(jnp.where(lo, 0.0, pb), axis=1, keepdims=True), 2 * blk + 1)
        for c in range(tm // ATT_SUB):
            dl_ref[c] = delta[c * ATT_SUB:(c + 1) * ATT_SUB, :].T[0:8, :]

    row = pl.BlockSpec((tm, d), lambda i: (i, 0))
    half = pl.BlockSpec((tm, width), lambda i: (i, 0))
    wspec = pl.BlockSpec((width, d), lambda i: (0, 0))
    return _pallas(
        body, name="mix_out_bwd", args=[dx2, o, yp, ya, ona, woa, wob],
        out_shape=[jax.ShapeDtypeStruct((t, width), F32), jax.ShapeDtypeStruct((t, width), BF16),
                   jax.ShapeDtypeStruct((t // ATT_SUB, 8, ATT_SUB), F32), jax.ShapeDtypeStruct((width, d), F32),
                   jax.ShapeDtypeStruct((width, d), F32), jax.ShapeDtypeStruct((nt, 1, width), F32)],
        grid=(nt,),
        in_specs=[row, half, half, half, pl.BlockSpec((1, width), lambda i: (0, 0)), wspec, wspec],
        out_specs=[half, half, pl.BlockSpec((tm // ATT_SUB, 8, ATT_SUB), lambda i: (i, 0, 0)), wspec, wspec,
                   pl.BlockSpec((None, 1, width), lambda i: (i, 0, 0))], plan=plan)


def _mix_in_bwd(dx2, x1, gain, hm, dpv, dqh, q, dkh, k, dv, df, qn, kn, wt):
    t, d = x1.shape
    width = q.shape[1]
    pool_width = dpv.shape[1]
    tm = min(512, t)
    nt = t // tm
    scale = HEAD_DIM ** -0.5
    c_q, c_k, c_v = pool_width, pool_width + width, pool_width + 2 * width
    c_f = c_v + width

    def body(dx2_ref, x_ref, g_ref, hm_ref, dpv_ref, dqh_ref, q_ref, dkh_ref, k_ref, dv_ref, df_ref, qn_ref, kn_ref,
             wt_ref, dx_ref, dxh_ref, dwt_ref, dg_ref, dqn_ref, dkn_ref):
        @pl.when(pl.program_id(0) == 0)
        def _():
            dwt_ref[...] = jnp.zeros_like(dwt_ref)

        lo = _head_masks()
        for part, rows in enumerate(_row_halves(tm)):
            def put(ref, sl, value):
                ref[:, sl] = value if part == 0 else ref[:, sl] + value

            hm = hm_ref[rows, :]
            pieces = [(0, dpv_ref[rows, :])]
            for c0, raw_ref, dh_ref, n_ref, dn_ref, mul in ((c_q, q_ref, dqh_ref, qn_ref, dqn_ref, scale),
                                                           (c_k, k_ref, dkh_ref, kn_ref, dkn_ref, 1.0)):
                cols = []
                for blk in range(width // LANES):
                    sl = slice(blk * LANES, (blk + 1) * LANES)
                    xb = raw_ref[rows, sl]
                    gb = dh_ref[rows, sl] * mul
                    r = _head_rms(xb, lo)
                    xh = xb * r
                    dyg = gb * n_ref[:, sl]
                    cols.append((r * (dyg - xh * _head_mean(dyg * xh, lo))).astype(BF16))
                    put(dn_ref, sl, jnp.sum(gb * xh, axis=0, keepdims=True))
                pieces.append((c0, jnp.concatenate(cols, axis=1)))
            pieces.append((c_v, dv_ref[rows, :]))
            pieces.append((c_f, df_ref[rows, :].astype(BF16)))
            dhm = jnp.zeros((tm // 2, d), F32)
            for c0, piece in pieces:
                dwt_ref[c0:c0 + piece.shape[1], :] += _dot_tn(piece, hm)
                dhm = dhm + _dot(piece, wt_ref[c0:c0 + piece.shape[1], :])
            xf = x_ref[rows, :]
            dxn, dgr = _rms_bwd(xf, _rms(xf), g_ref[...], dhm)
            dx = dx2_ref[rows, :] + dxn
            dx_ref[rows, :] = dx
            dxh_ref[rows, :] = (0.5 * dx).astype(BF16)
            put(dg_ref, slice(None), jnp.sum(dgr, axis=0, keepdims=True))

    row = pl.BlockSpec((tm, d), lambda i: (i, 0))
    half = pl.BlockSpec((tm, width), lambda i: (i, 0))
    const = lambda shape: pl.BlockSpec(shape, lambda i: (0, 0))
    pvec = lambda n: pl.BlockSpec((None, 1, n), lambda i: (i, 0, 0))
    return pl.pallas_call(
        body,
        out_shape=[jax.ShapeDtypeStruct((t, d), F32), jax.ShapeDtypeStruct((t, d), BF16), jax.ShapeDtypeStruct(wt.shape, F32),
                   jax.ShapeDtypeStruct((nt, 1, d), F32),
                   jax.ShapeDtypeStruct((nt, 1, width), F32), jax.ShapeDtypeStruct((nt, 1, width), F32)],
        grid=(nt,),
        in_specs=[row, row, const((1, d)), row, pl.BlockSpec((tm, pool_width), lambda i: (i, 0)), half, half, half, half,
                  half, pl.BlockSpec((tm, LANES), lambda i: (i, 0)), const((1, width)), const((1, width)),
                  const(wt.shape)],
        out_specs=[row, row, const(wt.shape), pvec(d), pvec(width), pvec(width)],
        compiler_params=_params(), name="mix_in_bwd",
    )(dx2, x1, gain, hm, dpv, dqh, q, dkh, k, dv, df, qn, kn, wt)


def _mesh_pos():
    return lax.axis_index("x"), lax.axis_index("y"), lax.axis_index("c")


def _other_chips(x, y):
    return [(1 - x, y), (x, 1 - y), (1 - x, 1 - y)]


def _remote(src, dst, send_sem, recv_sem, device):
    return pltpu.make_async_remote_copy(src_ref=src, dst_ref=dst, send_sem=send_sem, recv_sem=recv_sem,
                                        device_id=device, device_id_type=pl.DeviceIdType.MESH)


def _half_rows(n_rows, which):
    half = n_rows // 2
    return pl.ds(pl.multiple_of(which * half, 8), half)


def _row_block(rows, cols, itemsize=4):
    rb = rows
    while rb * cols * itemsize > (1 << 20) and rb % 32 == 0:
        rb //= 2
    return rb


def _place_cast(ws, chip, tag):
    n = len(ws)
    rows, cols = ws[0].shape
    rb = _row_block(rows, cols)

    def body(k_ref, *refs):
        for w_ref, o_ref in zip(refs[:n], refs[n:]):
            o_ref[...] = w_ref[...].astype(BF16)

    return pl.pallas_call(
        body, out_shape=[jax.ShapeDtypeStruct((N_CHIPS, rows, cols), BF16)] * n,
        grid_spec=pltpu.PrefetchScalarGridSpec(
            num_scalar_prefetch=1, grid=(rows // rb,),
            in_specs=[pl.BlockSpec((rb, cols), lambda i, k: (i, 0))] * n,
            out_specs=[pl.BlockSpec((None, rb, cols), lambda i, k: (k[0], i, 0))] * n),
        compiler_params=_params(), name="place_" + tag,
    )(chip, *ws)


class _Plan:
    def __init__(self, ins, outs, alias, sems, start, finish, middle=None, middle_at=(3, 4)):
        self.ins, self.outs, self.alias, self.sems = ins, outs, alias, sems
        self.start, self.middle, self.finish, self.middle_at = start, middle, finish, middle_at


def _merge_plans(a, b):
    ni, no, ns = len(a.ins), len(a.outs), len(a.sems)
    alias = dict(a.alias)
    alias.update({ni + i: no + o for i, o in b.alias.items()})

    def both(which):
        stage_a, stage_b = getattr(a, which), getattr(b, which)
        if stage_a is None and stage_b is None:
            return None

        def run(ins, outs, sems):
            if stage_a is not None:
                stage_a(ins[:ni], outs[:no], sems[:ns])
            if stage_b is not None:
                stage_b(ins[ni:], outs[no:], sems[ns:])
        return run

    return _Plan(list(a.ins) + list(b.ins), list(a.outs) + list(b.outs), alias, list(a.sems) + list(b.sems),
                 both("start"), both("finish"), both("middle"), a.middle_at if a.middle is not None else b.middle_at)


def _run_plan(plan, name):
    n_in, n_out = len(plan.ins), len(plan.outs)

    def body(*refs):
        parts = refs[:n_in], refs[n_in:n_in + n_out], refs[n_in + n_out:]
        plan.start(*parts)
        if plan.middle is not None:
            plan.middle(*parts)
        plan.finish(*parts)

    return pl.pallas_call(
        body, out_shape=plan.outs, in_specs=[ANY] * n_in, out_specs=[ANY] * n_out, scratch_shapes=plan.sems,
        input_output_aliases=plan.alias, name=name,
    )(*plan.ins)


def _pallas(body, *, name, args, in_specs, out_shape, out_specs, grid, scratch_shapes=(), plan=None, aliases=None):
    n_in, n_out, n_scr = len(args), len(out_shape), len(scratch_shapes)
    plan = plan or _Plan([], [], {}, [], None, None)
    p_in, p_out = len(plan.ins), len(plan.outs)

    def carrying(*refs):
        ins, p_ins = refs[:n_in], refs[n_in:n_in + p_in]
        o0 = n_in + p_in
        outs, p_outs = refs[o0:o0 + n_out], refs[o0 + n_out:o0 + n_out + p_out]
        s0 = o0 + n_out + p_out
        scr, p_sems = refs[s0:s0 + n_scr], refs[s0 + n_scr:]
        ids = [pl.program_id(a) for a in range(len(grid))]

        if plan.start is not None:
            @pl.when(functools.reduce(jnp.logical_and, [i == 0 for i in ids]))
            def _():
                plan.start(p_ins, p_outs, p_sems)

        body(*ins, *outs, *scr)

        if plan.middle is not None:
            step, n_steps = 0, 1
            for i, g in zip(ids, grid):
                step, n_steps = step * g + i, n_steps * g

            @pl.when(step == (plan.middle_at[0] * n_steps) // plan.middle_at[1])
            def _():
                plan.middle(p_ins, p_outs, p_sems)

        if plan.finish is not None:
            @pl.when(functools.reduce(jnp.logical_and, [i == g - 1 for i, g in zip(ids, grid)]))
            def _():
                plan.finish(p_ins, p_outs, p_sems)

    aliases = dict(aliases or {})
    aliases.update({n_in + i: n_out + o for i, o in plan.alias.items()})
    res = pl.pallas_call(
        carrying, out_shape=list(out_shape) + list(plan.outs), grid=grid,
        in_specs=list(in_specs) + [ANY] * p_in, out_specs=list(out_specs) + [ANY] * p_out,
        scratch_shapes=list(scratch_shapes) + list(plan.sems),
        input_output_aliases=aliases, compiler_params=_params(), name=name,
    )(*args, *plan.ins)
    return list(res[:n_out]), list(res[n_out:])


def _plan_gather(stacks):
    n = len(stacks)
    relations = range(3)

    def ici_copies(outs, sems):
        x, y, c = _mesh_pos()
        chips = _other_chips(x, y)
        cps = []
        for w in range(n):
            own = outs[w].at[2 * x + y, _half_rows(stacks[w].shape[1], c)]
            cps += [_remote(own, own, sems[0].at[w, j], sems[1].at[w, j], (*chips[j], c)) for j in relations]
        return cps

    def start(ins, outs, sems):
        for cp in ici_copies(outs, sems):
            cp.start()

    def forwards(outs, sems, core):
        x, y, c = _mesh_pos()
        slots = [2 * cx + cy for cx, cy in _other_chips(x, y)]
        cps = []
        for w in range(n):
            rows = _half_rows(stacks[w].shape[1], core)
            for j in relations:
                landed = outs[w].at[slots[j], rows]
                cps.append((_remote(landed, landed, sems[0].at[w, j], sems[1].at[w, j], (x, y, 1 - c)),
                            _remote(landed, landed, sems[2].at[w, j], sems[3].at[w, j], (x, y, 1 - c))))
        return cps

    def middle(ins, outs, sems):
        c = _mesh_pos()[2]
        for arrival, forward in forwards(outs, sems, c):
            arrival.wait_recv()
            forward.start()

    def finish(ins, outs, sems):
        c = _mesh_pos()[2]
        for _, forward in forwards(outs, sems, 1 - c):
            forward.wait_recv()
        for cp in ici_copies(outs, sems) + [forward for _, forward in forwards(outs, sems, c)]:
            cp.wait_send()

    return _Plan(stacks, [jax.ShapeDtypeStruct(s.shape, s.dtype) for s in stacks], {w: w for w in range(n)},
                 [pltpu.SemaphoreType.DMA((n, 3))] * 4, start, finish, middle)


RELAY_SEMS = [pltpu.SemaphoreType.DMA((3, 2))] * 4 + [pltpu.SemaphoreType.DMA((3, 3))] * 2


def _relay_gather_stage(stage, outs, sems):
    send, recv, relay_send, relay_recv, d2d_send, d2d_recv = sems
    n = len(outs)
    rh = outs[0].shape[1] // 2
    mx, my, c = _mesh_pos()
    sibling = (mx, my, 1 - c)
    near = [(1 - mx, my), (mx, 1 - my)]
    slots = [2 * cx + cy for cx, cy in near] + [2 * (1 - mx) + (1 - my)]

    def piece(w, slot, core, quarter=None):
        if quarter is None:
            return outs[w].at[slot, _half_rows(2 * rh, core)]
        return outs[w].at[slot, pl.ds(pl.multiple_of(core * rh + quarter * (rh // 2), 8), rh // 2)]

    def to_near(w, j):
        own = piece(w, 2 * mx + my, c)
        return _remote(own, own, send.at[w, j], recv.at[w, j], (*near[j], c))

    def from_near(w, j):
        landed = piece(w, slots[j], c)
        return _remote(landed, landed, send.at[w, j], recv.at[w, j], sibling)

    def onward(w, j, slot):
        part = piece(w, slot, c, quarter=j)
        return _remote(part, part, relay_send.at[w, j], relay_recv.at[w, j], (*near[1 - j], c))

    def to_sibling(w, j, core):
        landed = piece(w, slots[j], core)
        return _remote(landed, landed, d2d_send.at[w, j], d2d_recv.at[w, j], sibling)

    if stage == 0:
        for w in range(n):
            for j in range(2):
                to_near(w, j).start()
    elif stage == 1:
        for w in range(n):
            for j in range(2):
                from_near(w, j).wait_recv()
                onward(w, j, slots[j]).start()
                to_sibling(w, j, c).start()
        for w in range(n):
            to_sibling(w, 0, 1 - c).wait_recv()
    elif stage == 2:
        for w in range(n):
            to_sibling(w, 1, 1 - c).wait_recv()
    elif stage == 3:
        for w in range(n):
            for j in range(2):
                onward(w, j, slots[2]).wait_recv()
            to_sibling(w, 2, c).start()
        for w in range(n):
            to_sibling(w, 2, 1 - c).wait_recv()
    else:
        for w in range(n):
            for j in range(2):
                to_near(w, j).wait_send()
                onward(w, j, slots[j]).wait_send()
            for j in range(3):
                to_sibling(w, j, c).wait_send()


def _plan_gather_relay(stacks):
    def stages(which):
        def run(ins, outs, sems):
            for stage in which:
                _relay_gather_stage(stage, outs, sems)
        return run

    return _Plan(stacks, [jax.ShapeDtypeStruct(s.shape, s.dtype) for s in stacks], {w: w for w in range(len(stacks))},
                 RELAY_SEMS, stages([0]), stages([3, 4]), stages([1, 2]), middle_at=(1, 2))


def _ffn_fwd_gathering(x, gain, stacks, order, later):
    t, d = x.shape
    nch, fc, _ = stacks[0].shape
    n = len(stacks)
    tm = min(FFN_STAGED_TILE, t)
    nt = t // tm
    p_in, p_out = len(later.ins), len(later.outs)
    relay = _relay_gather_stage

    def body(order_ref, x_ref, g_ref, *refs):
        later_in, refs = refs[n:n + p_in], refs[n + p_in:]
        o_ref, h_ref, a_ref, b_ref, s_ref = refs[:5]
        stack_refs, later_out, refs = refs[5:5 + n], refs[5 + n:5 + n + p_out], refs[5 + n + p_out:]
        w_ref, hs_ref, acc_ref, w_sem = refs[:4]
        relay_sems, later_sems = refs[4:10], refs[10:]
        k, i = pl.program_id(0), pl.program_id(1)
        tile = pl.ds(pl.multiple_of(i * tm, tm), tm)

        @pl.when(i == 0)
        def _():
            for stage in range(nch):
                @pl.when(k == stage)
                def _():
                    relay(stage, stack_refs, relay_sems)
                    if stage == 1 and later.start is not None:
                        later.start(later_in, later_out, later_sems)
            loads = [pltpu.make_async_copy(stack_refs[w].at[order_ref[k]], w_ref.at[w], w_sem.at[w]) for w in range(n)]
            for cp in loads:
                cp.start()
            for cp in loads:
                cp.wait()

        @pl.when(k == 0)
        def _():
            xf = x_ref[...]
            hb = ((xf * _rms(xf)) * g_ref[...]).astype(BF16)
            h_ref[...] = hb
            hs_ref[tile, :] = hb
            acc_ref[tile, :] = jnp.zeros((tm, d), F32)

        for rows in _row_halves(tm):
            part = pl.ds(pl.multiple_of(i * tm + rows.start, tm // 2), tm // 2)
            h = hs_ref[part, :]
            a = _dot_nt(h, w_ref[0])
            b = _dot_nt(h, w_ref[1])
            sb = ((a * (0.5 * jnp.tanh(0.5 * a) + 0.5)) * b).astype(BF16)
            a_ref[rows, :] = a.astype(BF16)
            b_ref[rows, :] = b.astype(BF16)
            s_ref[rows, :] = sb
            acc_ref[part, :] += _dot(sb, w_ref[2])

        @pl.when(k == nch - 1)
        def _():
            o_ref[...] = x_ref[...] + 0.5 * acc_ref[tile, :]

        @pl.when((k == nch - 1) & (i == nt - 1))
        def _():
            relay(nch, stack_refs, relay_sems)
            for stage in (later.middle, later.finish):
                if stage is not None:
                    stage(later_in, later_out, later_sems)

    ends = lambda k, i: jnp.where((k == 0) | (k == nch - 1), i, 0)
    act = pl.BlockSpec((None, tm, fc), lambda k, i, order: (order[k], i, 0))
    out_shape = [jax.ShapeDtypeStruct((t, d), F32), jax.ShapeDtypeStruct((t, d), BF16)]
    out_shape += [jax.ShapeDtypeStruct((nch, t, fc), BF16)] * 3
    out_shape += [jax.ShapeDtypeStruct(s.shape, s.dtype) for s in stacks] + list(later.outs)
    aliases = {3 + w: 5 + w for w in range(n)}
    aliases.update({3 + n + i: 5 + n + o for i, o in later.alias.items()})
    res = pl.pallas_call(
        body, out_shape=out_shape,
        grid_spec=pltpu.PrefetchScalarGridSpec(
            num_scalar_prefetch=1, grid=(nch, nt),
            in_specs=[pl.BlockSpec((tm, d), lambda k, i, order: (ends(k, i), 0)),
                      pl.BlockSpec((1, d), lambda k, i, order: (0, 0))] + [ANY] * (n + p_in),
            out_specs=[pl.BlockSpec((tm, d), lambda k, i, order: (jnp.where(k == nch - 1, i, 0), 0)),
                       pl.BlockSpec((tm, d), lambda k, i, order: (jnp.where(k == 0, i, nt - 1), 0)),
                       act, act, act] + [ANY] * (n + p_out),
            scratch_shapes=[pltpu.VMEM((n, fc, d), BF16), pltpu.VMEM((t, d), BF16), pltpu.VMEM((t, d), F32),
                            pltpu.SemaphoreType.DMA((n,))] + RELAY_SEMS + list(later.sems)),
        input_output_aliases=aliases, compiler_params=_params(), name="ffn_fwd",
    )(order, x, gain, *stacks, *later.ins)
    return list(res[:5]), list(res[5:5 + n]), list(res[5 + n:])


def _plan_sibling_halves(gs):
    n = len(gs)

    def copies(ins, outs, sems):
        x, y, c = _mesh_pos()
        return [_remote(ins[w].at[:, _half_rows(gs[w].shape[1], 1 - c), :], outs[w], sems[0].at[w], sems[1].at[w],
                        (x, y, 1 - c)) for w in range(n)]

    def start(ins, outs, sems):
        for cp in copies(ins, outs, sems):
            cp.start()

    def finish(ins, outs, sems):
        for cp in copies(ins, outs, sems):
            cp.wait()

    return _Plan(gs, [jax.ShapeDtypeStruct((g.shape[0], g.shape[1] // 2, g.shape[2]), g.dtype) for g in gs], {},
                 [pltpu.SemaphoreType.DMA((n,))] * 2, start, finish)


def _plan_chip_exchange(ps):
    n = len(ps)

    def copies(ins, outs, sems):
        x, y, c = _mesh_pos()
        return [_remote(ins[w].at[2 * cx + cy], outs[w].at[j], sems[0].at[w, j], sems[1].at[w, j], (cx, cy, c))
                for w in range(n) for j, (cx, cy) in enumerate(_other_chips(x, y))]

    def start(ins, outs, sems):
        for cp in copies(ins, outs, sems):
            cp.start()

    def finish(ins, outs, sems):
        for cp in copies(ins, outs, sems):
            cp.wait()

    return _Plan(ps, [jax.ShapeDtypeStruct((3,) + p.shape[1:], p.dtype) for p in ps], {},
                 [pltpu.SemaphoreType.DMA((n, 3))] * 2, start, finish)


def _plan_sibling_share(gs):
    n = len(gs)

    def copies(outs, sems, which):
        x, y, c = _mesh_pos()
        cps = []
        for w in range(n):
            rows = outs[w].at[_half_rows(gs[w].shape[0], c if which == "mine" else 1 - c)]
            cps.append(_remote(rows, rows, sems[0].at[w], sems[1].at[w], (x, y, 1 - c)))
        return cps

    def start(ins, outs, sems):
        for cp in copies(outs, sems, "mine"):
            cp.start()

    def finish(ins, outs, sems):
        for cp in copies(outs, sems, "mine"):
            cp.wait_send()
        for cp in copies(outs, sems, "theirs"):
            cp.wait_recv()

    return _Plan(gs, [jax.ShapeDtypeStruct(g.shape, g.dtype) for g in gs], {w: w for w in range(n)},
                 [pltpu.SemaphoreType.DMA((n,))] * 2, start, finish)


def _same_shape_groups(arrays):
    groups = {}
    for i, a in enumerate(arrays):
        groups.setdefault(a.shape, []).append(i)
    return list(groups.values())


def _add_sibling(gs, r1s, ids, tag):
    n = len(gs)
    nch, rh, cols = r1s[0].shape

    def body(ids_ref, *refs):
        for g_ref, r_ref, o_ref in zip(refs[:n], refs[n:2 * n], refs[2 * n:]):
            o_ref[...] = (g_ref[...] + r_ref[...]).astype(BF16)

    blk = lambda fn: pl.BlockSpec((None, rh, cols), fn)
    return pl.pallas_call(
        body, out_shape=[jax.ShapeDtypeStruct(r1s[0].shape, BF16)] * n,
        grid_spec=pltpu.PrefetchScalarGridSpec(
            num_scalar_prefetch=1, grid=(nch,),
            in_specs=[blk(lambda k, ids: (k, ids[1], 0))] * n + [blk(lambda k, ids: (k, 0, 0))] * n,
            out_specs=[blk(lambda k, ids: (k, 0, 0))] * n),
        compiler_params=_params(), name="add_sibling_" + tag,
    )(ids, *gs, *r1s)


def _add_chips(gs, r1s, r2s, ids, tag):
    n = len(gs)
    _, rh, cols = r1s[0].shape
    nb = 2 if rh % 32 == 0 else 1
    rb = rh // nb

    def body(ids_ref, *refs):
        for g_ref, r1_ref, r2_ref, o_ref in zip(refs[:n], refs[n:2 * n], refs[2 * n:3 * n], refs[3 * n:]):
            own = g_ref[...] + r1_ref[...]
            o_ref[...] = ((own + r2_ref[0].astype(F32)) + r2_ref[1].astype(F32)) + r2_ref[2].astype(F32)

    return pl.pallas_call(
        body, out_shape=[jax.ShapeDtypeStruct((2 * rh, cols), F32)] * n,
        grid_spec=pltpu.PrefetchScalarGridSpec(
            num_scalar_prefetch=1, grid=(nb,),
            in_specs=[pl.BlockSpec((None, rb, cols), lambda i, ids: (ids[0], ids[1] * nb + i, 0))] * n
            + [pl.BlockSpec((None, rb, cols), lambda i, ids: (ids[0], i, 0))] * n
            + [pl.BlockSpec((3, rb, cols), lambda i, ids: (0, i, 0))] * n,
            out_specs=[pl.BlockSpec((rb, cols), lambda i, ids: (ids[1] * nb + i, 0))] * n),
        compiler_params=_params(), name="add_chips_" + tag,
    )(ids, *gs, *r1s, *r2s)


VEC_ROWS = 8


N_DEVICES = 8


def _small_pack(part, d, width):
    names = ("ffn1_norm", "mix_norm", "ffn2_norm", "pool_scale", "out_norm_pool", "out_norm_attn", "qn", "kn", "b_forget",
             "pool_w", "loss")
    args = [part[k] for k in names]
    pw_shape = part["pool_w"].shape[1:]

    def body(g1_ref, gm_ref, g2_ref, ps_ref, onp_ref, ona_ref, qn_ref, kn_ref, bf_ref, pw_ref, loss_ref, vbuf, pbuf):
        lo = _head_masks()

        def fold_heads(ref):
            v = jnp.sum(ref[...], axis=0)
            acc = jnp.zeros((VEC_ROWS, LANES), F32)
            for blk in range(width // LANES):
                vb = jnp.broadcast_to(v[:, blk * LANES:(blk + 1) * LANES], (VEC_ROWS, LANES))
                acc = acc + vb + pltpu.roll(vb, HEAD_DIM, 1)
            return jnp.where(lo, acc, 0.0)[0:1, :]

        vbuf[0] = jnp.zeros((VEC_ROWS, d), F32)
        vbuf[0, 0:1, :] = jnp.sum(g1_ref[...], axis=0)
        vbuf[0, 1:2, :] = jnp.sum(gm_ref[...], axis=0)
        vbuf[0, 2:3, :] = jnp.sum(g2_ref[...], axis=0)
        vbuf[0, 5:6, 0:LANES] = jnp.sum(loss_ref[...], axis=0)[0:1, :]
        vbuf[0, 3:4, 0:width] = jnp.sum(ps_ref[...], axis=0)
        vbuf[0, 3:4, width:2 * width] = jnp.sum(onp_ref[...], axis=0)
        vbuf[0, 4:5, 0:width] = jnp.sum(ona_ref[...], axis=0)
        vbuf[0, 4:5, width:width + LANES] = fold_heads(qn_ref)
        vbuf[0, 4:5, width + LANES:width + 2 * LANES] = fold_heads(kn_ref)
        vbuf[0, 4:5, width + 2 * LANES:width + 3 * LANES] = jnp.sum(bf_ref[...], axis=0)
        pbuf[0] = jnp.sum(pw_ref[...], axis=0)

    return pl.pallas_call(
        body, out_shape=[jax.ShapeDtypeStruct((N_DEVICES, VEC_ROWS, d), F32), jax.ShapeDtypeStruct((N_DEVICES,) + pw_shape, F32)],
        in_specs=[VM] * len(args), out_specs=[VM, VM], compiler_params=_params(), name="small_pack",
    )(*args)


def _plan_all_to_all(stacks):
    n = len(stacks)

    def copies(outs, sems):
        x, y, c = _mesh_pos()
        cps = []
        for r in range(1, N_DEVICES):
            peer = (x if not r & 4 else 1 - x, y if not r & 2 else 1 - y, c if not r & 1 else 1 - c)
            cps += [_remote(outs[w].at[0], outs[w].at[r], sems[0].at[w, r - 1], sems[1].at[w, r - 1], peer) for w in range(n)]
        return cps

    def start(ins, outs, sems):
        for cp in copies(outs, sems):
            cp.start()

    def finish(ins, outs, sems):
        for cp in copies(outs, sems):
            cp.wait()

    return _Plan(stacks, [jax.ShapeDtypeStruct(s.shape, s.dtype) for s in stacks], {w: w for w in range(n)},
                 [pltpu.SemaphoreType.DMA((n, N_DEVICES - 1))] * 2, start, finish)


def _small_sum(vstack, pstack, me):
    def body(me_ref, vbuf, pbuf, vec_ref, pw_ref):
        vec = vbuf[me_ref[0]]
        pw = pbuf[me_ref[0]]
        for dev in range(1, N_DEVICES):
            vec = vec + vbuf[jnp.bitwise_xor(me_ref[0], dev)]
            pw = pw + pbuf[jnp.bitwise_xor(me_ref[0], dev)]
        vec_ref[...] = vec
        pw_ref[...] = pw

    full = lambda s: pl.BlockSpec(s.shape, lambda i, me: (0,) * len(s.shape))
    outs = [jax.ShapeDtypeStruct(vstack.shape[1:], F32), jax.ShapeDtypeStruct(pstack.shape[1:], F32)]
    return pl.pallas_call(
        body, out_shape=outs,
        grid_spec=pltpu.PrefetchScalarGridSpec(num_scalar_prefetch=1, grid=(1,), in_specs=[full(vstack), full(pstack)],
                                               out_specs=[full(o) for o in outs]),
        compiler_params=_params(), name="small_sum",
    )(me, vstack, pstack)


def _adamw(ws, gs, ms, vs, tag):
    n = len(ws)
    rows, cols = ws[0].shape
    rb = rows
    while rb * cols * 4 * n > (1 << 20) and rb % 16 == 0:
        rb //= 2

    def body(*refs):
        for j in range(n):
            w_ref, g_ref, m_ref, v_ref = (refs[k * n + j] for k in range(4))
            go_ref, d_ref, mo_ref, vo_ref = (refs[(4 + k) * n + j] for k in range(4))
            gv = g_ref[...]
            go_ref[...] = gv
            m2 = ADAM_B1 * m_ref[...] + (1.0 - ADAM_B1) * gv
            v2 = ADAM_B2 * v_ref[...] + (1.0 - ADAM_B2) * (gv * gv)
            m_hat = m2 / (1.0 - ADAM_B1 ** ADAM_STEP)
            v_hat = v2 / (1.0 - ADAM_B2 ** ADAM_STEP)
            d_ref[...] = -ADAM_LR * (m_hat / (jnp.sqrt(v_hat) + ADAM_EPS) + ADAM_WD * w_ref[...])
            mo_ref[...] = m2
            vo_ref[...] = v2

    spec = pl.BlockSpec((rb, cols), lambda i: (i, 0))
    res, _ = _pallas(
        body, name="adamw_" + tag, args=[*ws, *gs, *ms, *vs], out_shape=[jax.ShapeDtypeStruct(ws[0].shape, F32)] * (4 * n),
        grid=(rows // rb,), in_specs=[spec] * (4 * n), out_specs=[spec] * (4 * n))
    return [tuple(res[k * n + j] for k in range(4)) for j in range(n)]


def _pack_vec(p, d, width):
    pad = lambda v: jnp.pad(v, (0, LANES - v.shape[0]))
    row3 = jnp.concatenate([p["pool_scale"], p["out_norm_pool"]])
    row4 = jnp.concatenate([p["out_norm_attn"], pad(p["q_norm"]), pad(p["k_norm"]), pad(p["b_forget"]),
                            jnp.zeros((d - width - 3 * LANES,), F32)])
    rows = [p["ffn1_norm"], p["mix_norm"], p["ffn2_norm"], row3, row4]
    return jnp.pad(jnp.stack(rows), ((0, VEC_ROWS - len(rows)), (0, 0)))


def _unpack_vec(vec, width):
    return dict(ffn1_norm=vec[0], mix_norm=vec[1], ffn2_norm=vec[2], pool_scale=vec[3, :width],
                out_norm_pool=vec[3, width:2 * width], out_norm_attn=vec[4, :width],
                q_norm=vec[4, width:width + HEAD_DIM], k_norm=vec[4, width + LANES:width + LANES + HEAD_DIM],
                b_forget=vec[4, width + 2 * LANES:width + 2 * LANES + N_HEADS])


WEIGHT_NAMES = ("ffn1_norm", "ffn1_w_gate", "ffn1_w_up", "ffn1_w_down", "mix_norm", "w_in", "b_forget", "pool_w",
                "pool_scale", "q_norm", "k_norm", "out_norm_pool", "out_norm_attn", "w_out", "ffn2_norm",
                "ffn2_w_gate", "ffn2_w_up", "ffn2_w_down")
BIG_NAMES = ("ffn1_w_gate", "ffn1_w_up", "ffn1_w_down", "w_in", "w_out", "ffn2_w_gate", "ffn2_w_up", "ffn2_w_down")
TRANSPOSED_NAMES = ("ffn1_w_gate", "ffn1_w_up", "w_in", "ffn2_w_gate", "ffn2_w_up")
FFN1_NAMES = ("ffn1_w_gate", "ffn1_w_up", "ffn1_w_down")
MIX_NAMES = ("w_in", "w_out")
FFN2_NAMES = ("ffn2_w_gate", "ffn2_w_up", "ffn2_w_down")


def kernel(x, ffn1_norm, ffn1_w_gate, ffn1_w_up, ffn1_w_down, mix_norm, w_in, b_forget, pool_w, pool_scale, q_norm, k_norm, out_norm_pool, out_norm_attn, w_out, ffn2_norm, ffn2_w_gate, ffn2_w_up, ffn2_w_down, loss_target, m_ffn1_norm, m_ffn1_w_gate, m_ffn1_w_up, m_ffn1_w_down, m_mix_norm, m_w_in, m_b_forget, m_pool_w, m_pool_scale, m_q_norm, m_k_norm, m_out_norm_pool, m_out_norm_attn, m_w_out, m_ffn2_norm, m_ffn2_w_gate, m_ffn2_w_up, m_ffn2_w_down, v_ffn1_norm, v_ffn1_w_gate, v_ffn1_w_up, v_ffn1_w_down, v_mix_norm, v_w_in, v_b_forget, v_pool_w, v_pool_scale, v_q_norm, v_k_norm, v_out_norm_pool, v_out_norm_attn, v_w_out, v_ffn2_norm, v_ffn2_w_gate, v_ffn2_w_up, v_ffn2_w_down):
    given = dict(locals())
    w = {n: given[n] for n in WEIGHT_NAMES}
    m = {n: given["m_" + n] for n in WEIGHT_NAMES}
    v = {n: given["v_" + n] for n in WEIGHT_NAMES}
    n_batch, seq, d = x.shape
    width = pool_scale.shape[0]
    in_rows = w_in.shape[1]
    in_cols = N_CHIPS * in_rows
    in_pad = -(-in_rows // 32) * 32
    in_cols_pad = in_cols - N_HEADS + LANES

    work = lambda a, n: a.T if n in TRANSPOSED_NAMES else a
    exchanged = lambda a, n: jnp.pad(a, ((0, in_pad - in_rows), (0, 0))) if n == "w_in" else a

    mesh_x, mesh_y, mesh_c = _mesh_pos()
    ids = jnp.stack([2 * mesh_x + mesh_y, mesh_c]).astype(jnp.int32)

    row = lambda a: a.reshape(1, -1)
    g1, gm, g2, ps, onp, ona = (row(a) for a in (ffn1_norm, mix_norm, ffn2_norm, pool_scale, out_norm_pool, out_norm_attn))
    qn, kn = row(jnp.tile(q_norm, N_HEADS)), row(jnp.tile(k_norm, N_HEADS))
    bf = row(jnp.pad(b_forget, (0, LANES - N_HEADS)))
    pwb = pool_w.astype(BF16)
    xf, tgt = x.reshape(n_batch * seq, d), loss_target.reshape(n_batch * seq, d)

    def grouped(call, names, *lists):
        out = [None] * len(names)
        for idx in _same_shape_groups(lists[0]):
            res = call(*[[lst[i] for i in idx] for lst in lists], names[idx[0]])
            for i, r in zip(idx, res):
                out[i] = r
        return out

    placed = dict(zip(BIG_NAMES, grouped(lambda ws, tag: _place_cast(ws, ids, tag), BIG_NAMES,
                                         [exchanged(work(w[n], n), n) for n in BIG_NAMES])))
    landing = jnp.stack([2 * cx + cy for cx, cy in [(mesh_x, mesh_y)] + _other_chips(mesh_x, mesh_y)]).astype(jnp.int32)
    (x1, h1, a1, b1, s1), (wg1, wu1, wd1), (w_in_all, w_out_all) = _ffn_fwd_gathering(
        xf, g1, [placed[n] for n in FFN1_NAMES], landing, _plan_gather([placed[n] for n in MIX_NAMES]))
    w_in_t = jnp.pad(w_in_all[:, :in_rows].reshape(in_cols, d), ((0, in_cols_pad - in_cols), (0, 0)))
    w_out_full = w_out_all.reshape(N_CHIPS * w_out.shape[0], d)
    woa, wob = w_out_full[:width], w_out_full[width:]

    hm, pv, q, k, qh, kh, vb, f = _mix_proj(x1, gm, w_in_t, qn, kn, width, width)
    qa, ka = _forget_prefix(f, bf, qh, kh, n_batch, seq)
    yp = _pool_fwd(pv, pwb, ps, onp, n_batch, seq)
    (o, lse), (wg2, wu2, wd2) = _attn_fwd(qa, ka, vb, n_batch, seq, plan=_plan_gather_relay([placed[n] for n in FFN2_NAMES]))
    x2, ya = _mix_out(x1, yp, o, ona, woa, wob)
    (dy, h2, a2, b2, s2, lpart, dyh), _ = _ffn_fwd(x2, g2, wg2, wu2, wd2, target=tgt)

    def to_chips(gs, arrived, tags):
        return grouped(lambda g, r, tag: _add_sibling(g, r, ids, tag), tags, gs, arrived)

    def own_rows(gs, from_sibling, from_chips, tags):
        return grouped(lambda g, ra, rb, tag: _add_chips(g, ra, rb, ids, tag), tags, gs, from_sibling, from_chips)

    (dx2, da2, db2, dg2), _ = _ffn_bwd_x(dy, x2, g2, a2, b2, wg2, wu2, wd2, "ffn2_bwd_x")
    dw2, _ = _ffn_bwd_w([(da2, h2), (db2, h2), (s2, dyh)], "ffn2_bwd_w")
    (dyp, do, delta, dwoa, dwob, dona), sib2 = _mix_out_bwd(dx2, o, yp, ya, ona, woa, wob, plan=_plan_sibling_halves(dw2))
    dpv, dpw, dps, donp = _pool_bwd(pv, dyp, pwb, ps, onp, n_batch, seq)
    (dqh, dkh, dv, dfq, dfk), chips2 = _attn_bwd(qa, ka, vb, do, lse, delta, n_batch, seq,
                                                 plan=_plan_chip_exchange(to_chips(dw2, sib2, FFN2_NAMES)))
    df, dbf = _forget_bwd(dfq, dfk, f, bf, n_batch, seq)
    dx1, dx1h, dw_in_t, dgm, dqn, dkn = _mix_in_bwd(dx2, x1, gm, hm, dpv, dqh, q, dkh, k, dv, df, qn, kn, w_in_t)
    in_base = [in_rows * k // 8 * 8 for k in range(N_CHIPS)]
    d_w_in = jnp.stack([dw_in_t[b:b + in_pad] for b in in_base])
    d_w_out = jnp.concatenate([dwoa, dwob], axis=0).reshape(N_CHIPS, w_out.shape[0], d)
    dwm = [d_w_in, d_w_out]
    down = FFN1_NAMES[2:]
    dwd1, sibm = _ffn_bwd_w([(s1, dx1h)], "ffn1_bwd_w_down", plan=_plan_sibling_halves(dwm))
    (da1, db1), arrived = _ffn_bwd_a(dx1h, a1, b1, wd1, "ffn1_bwd_a",
                                     plan=_merge_plans(_plan_sibling_halves(dwd1),
                                                       _plan_chip_exchange(to_chips(dwm, sibm, MIX_NAMES))))
    sibd, chipsm = arrived[:1], arrived[1:]
    gate_up = FFN1_NAMES[:2]
    dwgu1, chipsd = _ffn_bwd_w([(da1, h1), (db1, h1)], "ffn1_bwd_w_gate_up",
                               plan=_plan_chip_exchange(to_chips(dwd1, sibd, down)))
    n_tiles = (n_batch * seq) // min(FFN_TILE, n_batch * seq)
    first = max(n_tiles // 4, 1)
    begun, sibgu = _ffn_bwd_h(dx1, xf, g1, da1, db1, wg1, wu1, "ffn1_bwd_h_first", (0, first),
                              plan=_plan_sibling_halves(dwgu1))
    (gx, dg1), chipsgu = _ffn_bwd_h(dx1, xf, g1, da1, db1, wg1, wu1, "ffn1_bwd_h_rest", (first, n_tiles), prev=begun,
                                    plan=_plan_chip_exchange(to_chips(dwgu1, sibgu, gate_up)))

    part = dict(ffn1_norm=dg1, mix_norm=dgm, ffn2_norm=dg2, b_forget=dbf, pool_scale=dps, out_norm_pool=donp,
                out_norm_attn=dona, qn=dqn, kn=dkn, pool_w=dpw.reshape(n_batch, -1, pool_w.shape[-1]), loss=lpart)
    mine = (own_rows(dwgu1, sibgu, chipsgu, gate_up) + own_rows(dwd1, sibd, chipsd, down)
            + own_rows(dwm, sibm, chipsm, MIX_NAMES) + own_rows(dw2, sib2, chips2, FFN2_NAMES))
    last = _run_plan(_merge_plans(_plan_sibling_share(mine), _plan_all_to_all(_small_pack(part, d, width))), "last_exchange")
    vstack, pstack = last[len(mine):]
    g_vec, g_pw = _small_sum(vstack, pstack, jnp.reshape(4 * mesh_x + 2 * mesh_y + mesh_c, (1,)).astype(jnp.int32))
    loss = g_vec[5, 0]
    reduced = dict(zip(FFN1_NAMES + MIX_NAMES + FFN2_NAMES, last[:len(mine)]))
    reduced["w_in"] = lax.dynamic_slice(reduced["w_in"], ((in_rows * ids[0]) % 8, 0), (in_rows, d))

    grads, delta, new_m, new_v = {}, {}, {}, {}
    for names in (FFN2_NAMES, FFN1_NAMES, ("w_in",), ("w_out",)):
        stepped = _adamw([work(w[n], n) for n in names], [reduced[n] for n in names], [work(m[n], n) for n in names],
                         [work(v[n], n) for n in names], names[0])
        for n, step in zip(names, stepped):
            grads[n], delta[n], new_m[n], new_v[n] = (work(a, n) for a in step)
    flat_pw = lambda a: a.reshape(-1, a.shape[-1])
    (_, d_pw, m_pw, v_pw), = _adamw([flat_pw(pool_w)], [g_pw], [flat_pw(m_pool_w)], [flat_pw(v_pool_w)], "pool_w")
    (_, d_vec, m_vec, v_vec), = _adamw([_pack_vec(w, d, width)], [g_vec], [_pack_vec(m, d, width)],
                                       [_pack_vec(v, d, width)], "vectors")
    grads.update(_unpack_vec(g_vec, width), pool_w=g_pw.reshape(pool_w.shape))
    delta.update(_unpack_vec(d_vec, width), pool_w=d_pw.reshape(pool_w.shape))
    new_m.update(_unpack_vec(m_vec, width), pool_w=m_pw.reshape(pool_w.shape))
    new_v.update(_unpack_vec(v_vec, width), pool_w=v_pw.reshape(pool_w.shape))
    return (loss, gx.reshape(x.shape), *[grads[n] for n in WEIGHT_NAMES], *[delta[n] for n in WEIGHT_NAMES],
            *[new_m[n] for n in WEIGHT_NAMES], *[new_v[n] for n in WEIGHT_NAMES])
```

```python
import functools

import jax
import jax.numpy as jnp
from jax import lax
from jax.experimental import pallas as pl
from jax.experimental.pallas import tpu as pltpu

F32 = jnp.float32
BF16 = jnp.bfloat16
EPS = 1e-6
NEG = -1e30
ADAM_LR = 0.001
ADAM_B1 = 0.9
ADAM_B2 = 0.999
ADAM_EPS = 1e-08
ADAM_WD = 0.01
ADAM_STEP = 10
POOL_WINDOWS = (2, 4, 8, 16)
HEAD_DIM = 64
N_HEADS = 8
LANES = 128
N_CHIPS = 4
ATT_BLOCK = 512
ATT_SUB = 128
FFN_TILE = 1024
FFN_STAGED_TILE = 512
VMEM_LIMIT = 62 * 1024 * 1024
ANY = pl.BlockSpec(memory_space=pl.ANY)
VM = pl.BlockSpec(memory_space=pltpu.VMEM)


def _params(**kw):
    return pltpu.CompilerParams(vmem_limit_bytes=VMEM_LIMIT, **kw)


def _dot(a, b):
    return jnp.dot(a, b, preferred_element_type=F32)


def _dot_nt(a, b):
    return lax.dot_general(a, b, (((1,), (1,)), ((), ())), preferred_element_type=F32)


def _dot_tn(a, b):
    return lax.dot_general(a, b, (((0,), (0,)), ((), ())), preferred_element_type=F32)


def _sigmoid(z):
    return 1.0 / (1.0 + jnp.exp(-z))


def _rms(xf):
    return lax.rsqrt(jnp.mean(xf * xf, axis=-1, keepdims=True) + EPS)


def _rms_bwd(xf, r, gain, dh):
    xh = xf * r
    dyg = dh * gain
    return r * (dyg - xh * jnp.mean(dyg * xh, axis=-1, keepdims=True)), dh * xh


def _total(v):
    return jnp.sum(jnp.sum(v, axis=1, keepdims=True), axis=0, keepdims=True)


def _ffn_fwd(x, gain, wg, wu, wd, target=None, plan=None):
    t, d = x.shape
    nch, fc, _ = wg.shape
    tm = min(FFN_TILE, t)
    nt = t // tm
    with_loss = target is not None

    def body(*refs):
        if with_loss:
            x_ref, g_ref, wg_ref, wu_ref, wd_ref, t_ref, o_ref, h_ref, a_ref, b_ref, s_ref, l_ref, oh_ref, acc_ref = refs
        else:
            x_ref, g_ref, wg_ref, wu_ref, wd_ref, o_ref, h_ref, a_ref, b_ref, s_ref, acc_ref = refs
        k = pl.program_id(1)

        @pl.when(k == 0)
        def _():
            xf = x_ref[...]
            h_ref[...] = ((xf * _rms(xf)) * g_ref[...]).astype(BF16)
            acc_ref[...] = jnp.zeros_like(acc_ref)

        for rows in _row_halves(tm):
            h = h_ref[rows, :]
            a = _dot_nt(h, wg_ref[...])
            b = _dot_nt(h, wu_ref[...])
            sb = ((a * (0.5 * jnp.tanh(0.5 * a) + 0.5)) * b).astype(BF16)
            a_ref[rows, :] = a.astype(BF16)
            b_ref[rows, :] = b.astype(BF16)
            s_ref[rows, :] = sb
            acc_ref[rows, :] += _dot(sb, wd_ref[...])

        @pl.when(k == nch - 1)
        def _():
            y = x_ref[...] + 0.5 * acc_ref[...]
            if with_loss:
                e = y - t_ref[...]
                o_ref[...] = e * (1.0 / d)
                oh_ref[...] = (e * (0.5 / d)).astype(BF16)
                l_ref[...] = jnp.broadcast_to(_total(e * e) * (0.5 / d), l_ref.shape)
            else:
                o_ref[...] = y

    row = pl.BlockSpec((tm, d), lambda i, k: (i, 0))
    chunk = pl.BlockSpec((None, fc, d), lambda i, k: (k, 0, 0))
    act = pl.BlockSpec((None, tm, fc), lambda i, k: (k, i, 0))
    in_specs = [row, pl.BlockSpec((1, d), lambda i, k: (0, 0)), chunk, chunk, chunk]
    out_shape = [jax.ShapeDtypeStruct((t, d), F32), jax.ShapeDtypeStruct((t, d), BF16)]
    out_shape += [jax.ShapeDtypeStruct((nch, t, fc), BF16)] * 3
    out_specs = [row, row, act, act, act]
    args = [x, gain, wg, wu, wd]
    if with_loss:
        in_specs.append(row)
        args.append(target)
        out_shape += [jax.ShapeDtypeStruct((nt, 8, LANES), F32), jax.ShapeDtypeStruct((t, d), BF16)]
        out_specs += [pl.BlockSpec((None, 8, LANES), lambda i, k: (i, 0, 0)), row]
    return _pallas(body, name="ffn_fwd_loss" if with_loss else "ffn_fwd", args=args, in_specs=in_specs,
                   out_shape=out_shape, out_specs=out_specs, grid=(nt, nch),
                   scratch_shapes=[pltpu.VMEM((tm, d), F32)], plan=plan)


def _row_halves(n):
    return [slice(0, n // 2), slice(n // 2, n)]


def _swiglu_grads(dyh, a_ref, b_ref, wd_ref, rows):
    ds = _dot_nt(dyh, wd_ref[...])
    av = a_ref[rows, :].astype(F32)
    bv = b_ref[rows, :].astype(F32)
    th = jnp.tanh(0.5 * av)
    sig = 0.5 * th + 0.5
    dab = ((ds * bv) * (sig * (1.0 + av * (0.5 - 0.5 * th)))).astype(BF16)
    return dab, (ds * (av * sig)).astype(BF16)


def _ffn_bwd_a(dyh, a, b, wd, name, plan=None):
    t, d = dyh.shape
    nch, fc, _ = wd.shape
    tm = min(FFN_TILE, t)

    def body(dyh_ref, a_ref, b_ref, wd_ref, da_ref, db_ref):
        for rows in _row_halves(tm):
            da_ref[rows, :], db_ref[rows, :] = _swiglu_grads(dyh_ref[rows, :], a_ref, b_ref, wd_ref, rows)

    act = pl.BlockSpec((None, tm, fc), lambda i, k: (k, i, 0))
    return _pallas(
        body, name=name, args=[dyh, a, b, wd], out_shape=[jax.ShapeDtypeStruct((nch, t, fc), BF16)] * 2, grid=(t // tm, nch),
        in_specs=[pl.BlockSpec((tm, d), lambda i, k: (i, 0)), act, act, pl.BlockSpec((None, fc, d), lambda i, k: (k, 0, 0))],
        out_specs=[act, act], plan=plan)


def _ffn_bwd_h(dy, x, gain, da, db, wg, wu, name, tiles, prev=None, plan=None):
    t, d = x.shape
    nch, fc, _ = wg.shape
    tm = min(FFN_TILE, t)
    nt = t // tm
    t0, t1 = tiles

    def body(*refs):
        dy_ref, x_ref, g_ref, da_ref, db_ref, wg_ref, wu_ref = refs[:7]
        dx_ref, dg_ref, acc_ref = refs[-3:]
        k = pl.program_id(1)

        @pl.when(k == 0)
        def _():
            acc_ref[...] = jnp.zeros_like(acc_ref)

        acc_ref[...] += _dot(da_ref[...], wg_ref[...]) + _dot(db_ref[...], wu_ref[...])

        @pl.when(k == nch - 1)
        def _():
            xf = x_ref[...]
            dxn, dgr = _rms_bwd(xf, _rms(xf), g_ref[...], acc_ref[...])
            dx_ref[...] = dy_ref[...] + dxn
            dg_ref[...] = jnp.sum(dgr, axis=0, keepdims=True)

    row = pl.BlockSpec((tm, d), lambda i, k: (i + t0, 0))
    chunk = pl.BlockSpec((None, fc, d), lambda i, k: (k, 0, 0))
    act = pl.BlockSpec((None, tm, fc), lambda i, k: (k, i + t0, 0))
    args = [dy, x, gain, da, db, wg, wu]
    in_specs = [row, row, pl.BlockSpec((1, d), lambda i, k: (0, 0)), act, act, chunk, chunk]
    aliases = {}
    if prev is not None:
        aliases = {len(args): 0, len(args) + 1: 1}
        args += list(prev)
        in_specs += [ANY, ANY]
    return _pallas(
        body, name=name, args=args, out_shape=[jax.ShapeDtypeStruct((t, d), F32), jax.ShapeDtypeStruct((nt, 1, d), F32)],
        grid=(t1 - t0, nch), in_specs=in_specs,
        out_specs=[row, pl.BlockSpec((None, 1, d), lambda i, k: (i + t0, 0, 0))],
        scratch_shapes=[pltpu.VMEM((tm, d), F32)], plan=plan, aliases=aliases)


def _ffn_bwd_x(dy, x, gain, a, b, wg, wu, wd, name, plan=None):
    t, d = x.shape
    nch, fc, _ = wg.shape
    tm = min(FFN_TILE, t)
    nt = t // tm

    def body(dy_ref, x_ref, g_ref, a_ref, b_ref, wg_ref, wu_ref, wd_ref, dx_ref, da_ref, db_ref, dg_ref, acc_ref):
        k = pl.program_id(1)

        @pl.when(k == 0)
        def _():
            acc_ref[...] = jnp.zeros_like(acc_ref)

        for rows in _row_halves(tm):
            dab, dbb = _swiglu_grads((0.5 * dy_ref[rows, :]).astype(BF16), a_ref, b_ref, wd_ref, rows)
            da_ref[rows, :] = dab
            db_ref[rows, :] = dbb
            acc_ref[rows, :] += _dot(dab, wg_ref[...]) + _dot(dbb, wu_ref[...])

        @pl.when(k == nch - 1)
        def _():
            xf = x_ref[...]
            dxn, dgr = _rms_bwd(xf, _rms(xf), g_ref[...], acc_ref[...])
            dx_ref[...] = dy_ref[...] + dxn
            dg_ref[...] = jnp.sum(dgr, axis=0, keepdims=True)

    row = pl.BlockSpec((tm, d), lambda i, k: (i, 0))
    chunk = pl.BlockSpec((None, fc, d), lambda i, k: (k, 0, 0))
    act = pl.BlockSpec((None, tm, fc), lambda i, k: (k, i, 0))
    return _pallas(
        body, name=name, args=[dy, x, gain, a, b, wg, wu, wd],
        out_shape=[jax.ShapeDtypeStruct((t, d), F32), jax.ShapeDtypeStruct((nch, t, fc), BF16),
                   jax.ShapeDtypeStruct((nch, t, fc), BF16), jax.ShapeDtypeStruct((nt, 1, d), F32)],
        grid=(nt, nch),
        in_specs=[row, row, pl.BlockSpec((1, d), lambda i, k: (0, 0)), act, act, chunk, chunk, chunk],
        out_specs=[row, act, act, pl.BlockSpec((None, 1, d), lambda i, k: (i, 0, 0))],
        scratch_shapes=[pltpu.VMEM((tm, d), F32)], plan=plan)


def _ffn_bwd_w(pairs, name, plan=None):
    n = len(pairs)
    nch, t, fc = pairs[0][0].shape
    d = pairs[0][1].shape[1]
    tm = min(FFN_TILE, t)

    def body(*refs):
        @pl.when(pl.program_id(1) == 0)
        def _():
            for o_ref in refs[2 * n:]:
                o_ref[...] = jnp.zeros_like(o_ref)

        for j in range(n):
            refs[2 * n + j][...] += _dot_tn(refs[j][...], refs[n + j][...])

    row = pl.BlockSpec((tm, d), lambda k, i: (i, 0))
    act = pl.BlockSpec((None, tm, fc), lambda k, i: (k, i, 0))
    chunk = pl.BlockSpec((None, fc, d), lambda k, i: (k, 0, 0))
    return _pallas(body, name=name, args=[p[0] for p in pairs] + [p[1] for p in pairs],
                   out_shape=[jax.ShapeDtypeStruct((nch, fc, d), F32)] * n, grid=(nch, t // tm),
                   in_specs=[act] * n + [row] * n, out_specs=[chunk] * n, plan=plan)


def _head_masks():
    lane = lax.broadcasted_iota(jnp.int32, (1, LANES), 1)
    return lane < HEAD_DIM


def _head_rms(x, lo):
    x2 = x * x
    s0 = jnp.sum(jnp.where(lo, x2, 0.0), axis=1, keepdims=True)
    s1 = jnp.sum(jnp.where(lo, 0.0, x2), axis=1, keepdims=True)
    return jnp.where(lo, lax.rsqrt(s0 * (1.0 / HEAD_DIM) + EPS), lax.rsqrt(s1 * (1.0 / HEAD_DIM) + EPS))


def _head_mean(v, lo):
    s0 = jnp.sum(jnp.where(lo, v, 0.0), axis=1, keepdims=True)
    s1 = jnp.sum(jnp.where(lo, 0.0, v), axis=1, keepdims=True)
    return jnp.where(lo, s0, s1) * (1.0 / HEAD_DIM)


def _mix_proj(x1, gain, wt, qn, kn, pool_width, attn_width):
    t, d = x1.shape
    tm = min(512, t)
    nt = t // tm
    scale = HEAD_DIM ** -0.5
    c_q, c_k, c_v = pool_width, pool_width + attn_width, pool_width + 2 * attn_width
    c_f = c_v + attn_width

    def body(x_ref, g_ref, wt_ref, qn_ref, kn_ref, hm_ref, pv_ref, q_ref, k_ref, qh_ref, kh_ref, vb_ref, f_ref):
        lo = _head_masks()
        for rows in _row_halves(tm):
            xf = x_ref[rows, :]
            hm = ((xf * _rms(xf)) * g_ref[...]).astype(BF16)
            hm_ref[rows, :] = hm
            f_ref[rows, :] = _dot_nt(hm, wt_ref[c_f:c_f + LANES, :])
            pv_ref[rows, :] = _dot_nt(hm, wt_ref[0:pool_width, :])
            vb_ref[rows, :] = _dot_nt(hm, wt_ref[c_v:c_v + attn_width, :]).astype(BF16)
            for c0, raw_ref, hat_ref, n_ref, mul in ((c_q, q_ref, qh_ref, qn_ref, scale), (c_k, k_ref, kh_ref, kn_ref, 1.0)):
                raw = _dot_nt(hm, wt_ref[c0:c0 + attn_width, :])
                raw_ref[rows, :] = raw
                for blk in range(attn_width // LANES):
                    sl = slice(blk * LANES, (blk + 1) * LANES)
                    xb = raw[:, sl]
                    hat_ref[rows, sl] = (((xb * _head_rms(xb, lo)) * n_ref[:, sl]) * mul).astype(BF16)

    row = pl.BlockSpec((tm, d), lambda i: (i, 0))
    half = pl.BlockSpec((tm, attn_width), lambda i: (i, 0))
    const = lambda shape: pl.BlockSpec(shape, lambda i: (0, 0))
    return _pallas(
        body, name="mix_proj", args=[x1, gain, wt, qn, kn],
        out_shape=[jax.ShapeDtypeStruct((t, d), BF16), jax.ShapeDtypeStruct((t, pool_width), F32),
                   jax.ShapeDtypeStruct((t, attn_width), F32), jax.ShapeDtypeStruct((t, attn_width), F32),
                   jax.ShapeDtypeStruct((t, attn_width), BF16), jax.ShapeDtypeStruct((t, attn_width), BF16),
                   jax.ShapeDtypeStruct((t, attn_width), BF16), jax.ShapeDtypeStruct((t, LANES), F32)],
        grid=(nt,),
        in_specs=[row, const((1, d)), const(wt.shape), const((1, attn_width)), const((1, attn_width))],
        out_specs=[row, pl.BlockSpec((tm, pool_width), lambda i: (i, 0)), half, half, half, half, half,
                   pl.BlockSpec((tm, LANES), lambda i: (i, 0))])[0]


def _shift_down(v, dist, row):
    return jnp.where(row >= dist, pltpu.roll(v, dist, 0), 0.0)


def _shift_up(v, dist, row, n):
    return jnp.where(row + dist < n, pltpu.roll(v, n - dist, 0), 0.0)


def _aug_lane(e):
    return HEAD_DIM if e == 0 else 0


def _forget_prefix(f, bias, qh, kh, n_batch, seq):
    def body(f_ref, b_ref, q_ref, k_ref, qa_ref, ka_ref):
        z = f_ref[...] + b_ref[...]
        acc = jnp.minimum(z, 0.0) - jnp.log(1.0 + jnp.exp(-jnp.abs(z)))
        row = lax.broadcasted_iota(jnp.int32, (seq, 1), 0)
        dist = 1
        while dist < seq:
            acc = acc + _shift_down(acc, dist, row)
            dist *= 2
        lane = lax.broadcasted_iota(jnp.int32, (1, LANES), 1)
        for h in range(N_HEADS):
            pair, e = divmod(h, 2)
            a0 = _aug_lane(e)
            own = (lane < HEAD_DIM) if e == 0 else (lane >= HEAD_DIM)
            fh = _pick_lane(acc, h)
            hi = fh.astype(BF16).astype(F32)
            rest = fh - hi
            mid = rest.astype(BF16).astype(F32)
            low = rest - mid
            q_ones = (lane >= a0 + 3) & (lane < a0 + 6)
            k_ones = (lane >= a0) & (lane < a0 + 3)
            q_aug = jnp.where(lane == a0, hi, jnp.where(lane == a0 + 1, mid, jnp.where(lane == a0 + 2, low,
                              jnp.where(q_ones, 1.0, 0.0))))
            k_aug = jnp.where(k_ones, 1.0, jnp.where(lane == a0 + 3, -hi, jnp.where(lane == a0 + 4, -mid,
                              jnp.where(lane == a0 + 5, -low, 0.0))))
            src = slice(pair * LANES, (pair + 1) * LANES)
            dst = slice(h * LANES, (h + 1) * LANES)
            qa_ref[:, dst] = jnp.where(own, q_ref[:, src].astype(F32), q_aug).astype(BF16)
            ka_ref[:, dst] = jnp.where(own, k_ref[:, src].astype(F32), k_aug).astype(BF16)

    width = qh.shape[1]
    tok = pl.BlockSpec((seq, width), lambda b: (b, 0))
    aug = pl.BlockSpec((seq, N_HEADS * LANES), lambda b: (b, 0))
    return pl.pallas_call(
        body, out_shape=[jax.ShapeDtypeStruct((n_batch * seq, N_HEADS * LANES), BF16)] * 2, grid=(n_batch,),
        in_specs=[pl.BlockSpec((seq, LANES), lambda b: (b, 0)), pl.BlockSpec((1, LANES), lambda b: (0, 0)), tok, tok],
        out_specs=[aug, aug], compiler_params=_params(), name="forget_prefix",
    )(f, bias, qh, kh)


def _pool_groups(pv_ref, pw_ref, ps_ref, seq):
    row = lax.broadcasted_iota(jnp.int32, (seq, 1), 0)
    pos = (row + 1).astype(F32)
    out = []
    for g, win in enumerate(POOL_WINDOWS):
        sl = slice(g * LANES, (g + 1) * LANES)
        xg = pv_ref[:, sl]
        acc = xg
        dist = 1
        while dist < win:
            acc = acc + _shift_down(acc, dist, row)
            dist *= 2
        pooled = (acc / jnp.minimum(pos, float(win)) - xg).astype(BF16)
        mixed = _dot(pooled, pw_ref[g])
        out.append((pooled, mixed, mixed * ps_ref[:, sl]))
    return out


def _pool_fwd(pv, pw, ps, onp, n_batch, seq):
    width = pv.shape[1]

    def body(pv_ref, pw_ref, ps_ref, on_ref, y_ref):
        groups = _pool_groups(pv_ref, pw_ref, ps_ref, seq)
        ssq = sum(jnp.sum(ms * ms, axis=1, keepdims=True) for _, _, ms in groups)
        r = lax.rsqrt(ssq * (1.0 / width) + EPS)
        for g, (_, _, ms) in enumerate(groups):
            sl = slice(g * LANES, (g + 1) * LANES)
            y_ref[:, sl] = ((ms * r) * on_ref[:, sl]).astype(BF16)

    return pl.pallas_call(
        body, out_shape=jax.ShapeDtypeStruct((n_batch * seq, width), BF16), grid=(n_batch,),
        in_specs=[pl.BlockSpec((seq, width), lambda b: (b, 0)), pl.BlockSpec(pw.shape, lambda b: (0, 0, 0)),
                  pl.BlockSpec((1, width), lambda b: (0, 0)), pl.BlockSpec((1, width), lambda b: (0, 0))],
        out_specs=pl.BlockSpec((seq, width), lambda b: (b, 0)),
        compiler_params=_params(), name="pool_fwd",
    )(pv, pw, ps, onp)


def _pool_bwd(pv, dyp, pw, ps, onp, n_batch, seq):
    width = pv.shape[1]

    def body(pv_ref, dy_ref, pw_ref, ps_ref, on_ref, dpv_ref, dpw_ref, dps_ref, don_ref):
        groups = _pool_groups(pv_ref, pw_ref, ps_ref, seq)
        ssq = sum(jnp.sum(ms * ms, axis=1, keepdims=True) for _, _, ms in groups)
        r = lax.rsqrt(ssq * (1.0 / width) + EPS)
        mean = sum(jnp.sum((dy_ref[:, g * LANES:(g + 1) * LANES] * on_ref[:, g * LANES:(g + 1) * LANES]) * (ms * r),
                           axis=1, keepdims=True) for g, (_, _, ms) in enumerate(groups)) * (1.0 / width)
        row = lax.broadcasted_iota(jnp.int32, (seq, 1), 0)
        pos = (row + 1).astype(F32)
        for g, (pooled, mixed, ms) in enumerate(groups):
            sl = slice(g * LANES, (g + 1) * LANES)
            dy = dy_ref[:, sl]
            xh = ms * r
            don_ref[:, sl] = jnp.sum(dy * xh, axis=0, keepdims=True)
            dms = r * (dy * on_ref[:, sl] - xh * mean)
            dps_ref[:, sl] = jnp.sum(dms * mixed, axis=0, keepdims=True)
            dmix = (dms * ps_ref[:, sl]).astype(BF16)
            dpw_ref[g] = _dot_tn(pooled, dmix)
            dpool = _dot_nt(dmix, pw_ref[g])
            win = POOL_WINDOWS[g]
            acc = dpool / jnp.minimum(pos, float(win))
            dist = 1
            while dist < win:
                acc = acc + _shift_up(acc, dist, row, seq)
                dist *= 2
            dpv_ref[:, sl] = (acc - dpool).astype(BF16)

    tok = pl.BlockSpec((seq, width), lambda b: (b, 0))
    vec = pl.BlockSpec((1, width), lambda b: (0, 0))
    pvec = pl.BlockSpec((None, 1, width), lambda b: (b, 0, 0))
    return pl.pallas_call(
        body,
        out_shape=[jax.ShapeDtypeStruct((n_batch * seq, width), BF16),
                   jax.ShapeDtypeStruct((n_batch,) + pw.shape, F32),
                   jax.ShapeDtypeStruct((n_batch, 1, width), F32), jax.ShapeDtypeStruct((n_batch, 1, width), F32)],
        grid=(n_batch,),
        in_specs=[tok, tok, pl.BlockSpec(pw.shape, lambda b: (0, 0, 0)), vec, vec],
        out_specs=[tok, pl.BlockSpec((None,) + pw.shape, lambda b: (b, 0, 0, 0)), pvec, pvec],
        compiler_params=_params(), name="pool_bwd",
    )(pv, dyp, pw, ps, onp)


def _pick_lane(tile, idx):
    lane = lax.broadcasted_iota(jnp.int32, (1, LANES), 1)
    return jnp.sum(jnp.where(lane == idx, tile, 0.0), axis=1, keepdims=True)


def _pick_row(tile, idx):
    sub = lax.broadcasted_iota(jnp.int32, (tile.shape[0], 1), 0)
    return jnp.sum(jnp.where(sub == idx, tile, 0.0), axis=0, keepdims=True)


def _put_lane(col, idx):
    lane = lax.broadcasted_iota(jnp.int32, (1, LANES), 1)
    return jnp.where(lane == idx, col, 0.0)


def _head_select(e):
    lo = _head_masks()
    return lo if e == 0 else jnp.logical_not(lo)


def _causal(st, shift):
    row = lax.broadcasted_iota(jnp.int32, st.shape, 0)
    col = lax.broadcasted_iota(jnp.int32, st.shape, 1) + shift
    return jnp.where(col >= row, st, NEG)


def _transpose_blocks(a):
    rows, cols = a.shape
    return jnp.concatenate(
        [jnp.concatenate([a[r:r + LANES, c:c + LANES].T for r in range(0, rows, LANES)], axis=1)
         for c in range(0, cols, LANES)], axis=0)


def _accumulate(ref, value, first):
    @pl.when(first)
    def _():
        ref[...] = value

    @pl.when(jnp.logical_not(first))
    def _():
        ref[...] += value


def _attn_fwd(qa, ka, vb, n_batch, seq, plan=None):
    tq = min(ATT_BLOCK, seq)
    nq, nsub, tk = seq // tq, tq // ATT_SUB, tq
    pairs = vb.shape[1] // LANES

    def body(q_ref, k_ref, v_ref, o_ref, lse_ref, acc_ref):
        i, p = pl.program_id(1), pl.program_id(2)
        row_lo = lax.broadcasted_iota(jnp.int32, (LANES, 1), 0) < HEAD_DIM
        qs = [q_ref[:, e * LANES:(e + 1) * LANES] for e in range(2)]
        acc_ref[...] = jnp.zeros_like(acc_ref)

        def tile(off, stats, diagonal):
            vj = v_ref[pl.ds(off, tk), :]
            new, alphas, pvs = [], [], []
            for e in range(2):
                st = _dot_nt(k_ref[pl.ds(off, tk), e * LANES:(e + 1) * LANES], qs[e])
                if diagonal:
                    st = _causal(st, 0)
                m, l = stats[e]
                m_new = jnp.maximum(m, jnp.max(st, axis=0, keepdims=True))
                alpha = jnp.exp(m - m_new)
                pt = jnp.exp(st - m_new)
                new.append((m_new, alpha * l + jnp.sum(pt, axis=0, keepdims=True)))
                alphas.append(alpha)
                pvs.append(_dot_tn(jnp.where(_head_select(e), vj, jnp.zeros_like(vj)), pt.astype(BF16)))
            acc_ref[...] = acc_ref[...] * jnp.where(row_lo, alphas[0], alphas[1]) + (pvs[0] + pvs[1])
            return tuple(new)

        init = ((jnp.full((1, tq), NEG, F32), jnp.zeros((1, tq), F32)),) * 2
        stats = lax.fori_loop(0, i, lambda j, st: tile(pl.multiple_of(j * tk, tk), st, False), init)
        (m0, l0), (m1, l1) = tile(pl.multiple_of(i * tk, tk), stats, True)
        out_t = acc_ref[...] / jnp.where(row_lo, l0, l1)
        sub = lax.broadcasted_iota(jnp.int32, (8, 1), 0)
        lse0, lse1 = m0 + jnp.log(l0), m1 + jnp.log(l1)
        for a in range(nsub):
            sl = slice(a * ATT_SUB, (a + 1) * ATT_SUB)
            o_ref[sl, :] = out_t[:, sl].T
            rows = jnp.where(sub == 2 * p, lse0[:, sl], 0.0) + jnp.where(sub == 2 * p + 1, lse1[:, sl], 0.0)
            _accumulate(lse_ref.at[a], rows, p == 0)

    return _pallas(
        body, name="attn_fwd", args=[qa, ka, vb],
        out_shape=[jax.ShapeDtypeStruct((n_batch * seq, pairs * LANES), F32),
                   jax.ShapeDtypeStruct((n_batch * seq // ATT_SUB, 8, ATT_SUB), F32)],
        grid=(n_batch, nq, pairs),
        in_specs=[pl.BlockSpec((tq, 2 * LANES), lambda b, i, p: (b * nq + i, p)),
                  pl.BlockSpec((seq, 2 * LANES), lambda b, i, p: (b, p)),
                  pl.BlockSpec((seq, LANES), lambda b, i, p: (b, p))],
        out_specs=[pl.BlockSpec((tq, LANES), lambda b, i, p: (b * nq + i, p)),
                   pl.BlockSpec((nsub, 8, ATT_SUB), lambda b, i, p: (b * nq + i, 0, 0))],
        scratch_shapes=[pltpu.VMEM((LANES, tq), F32)], plan=plan)


def _attn_bwd(qa, ka, vb, do, lse, delta, n_batch, seq, plan=None):
    tq = min(ATT_BLOCK, seq)
    nq, nsub = seq // tq, tq // ATT_SUB
    n_tiles = seq // ATT_SUB
    pairs = vb.shape[1] // LANES

    def body(q_ref, k_ref, v_ref, do_ref, lse_ref, dl_ref, dq_ref, dk_ref, dv_ref, dfq_ref, dfk_ref,
             dq0_ref, dq1_ref, dk0_ref, dk1_ref, dva_ref):
        p = pl.program_id(1)
        dqs, dks = (dq0_ref, dq1_ref), (dk0_ref, dk1_ref)
        for acc in (dk0_ref, dk1_ref, dva_ref):
            acc[...] = jnp.zeros_like(acc)
        dfq_cols = []
        for i in range(nq):
            rows_i = slice(i * tq, (i + 1) * tq)
            qs = [q_ref[rows_i, e * LANES:(e + 1) * LANES] for e in range(2)]
            dov = do_ref[rows_i, :]
            does = [jnp.where(_head_select(e), dov, jnp.zeros_like(dov)) for e in range(2)]
            stat = lambda ref, e: jnp.concatenate([_pick_row(ref[i * nsub + a], 2 * p + e) for a in range(nsub)], axis=1)
            ls, dl = [stat(lse_ref, e) for e in range(2)], [stat(dl_ref, e) for e in range(2)]
            for acc in dqs:
                acc[...] = jnp.zeros_like(acc)

            def tile(off, diagonal, qs=qs, dov=dov, does=does, ls=ls, dl=dl):
                vj = v_ref[pl.ds(off, tq), :]
                for e in range(2):
                    kj = k_ref[pl.ds(off, tq), e * LANES:(e + 1) * LANES]
                    st = _dot_nt(kj, qs[e])
                    if diagonal:
                        st = _causal(st, 0)
                    pt = jnp.exp(st - ls[e])
                    dva_ref[pl.ds(off, tq), :] += _dot(pt.astype(BF16), does[e])
                    dpt = _dot_nt(jnp.where(_head_select(e), vj, jnp.zeros_like(vj)), dov)
                    dst = (pt * (dpt - dl[e])).astype(BF16)
                    dks[e][pl.ds(off, tq), :] += _dot(dst, qs[e])
                    dqs[e][...] += _dot(_transpose_blocks(kj), dst)

            def step(j, carry, tile=tile):
                tile(pl.multiple_of(j * tq, tq), False)
                return carry

            lax.fori_loop(0, i, step, 0)
            tile(i * tq, True)
            dq0, dq1 = _transpose_blocks(dq0_ref[...]), _transpose_blocks(dq1_ref[...])
            dq_ref[rows_i, :] = jnp.where(_head_masks(), dq0, dq1)
            dfq_cols.append(_put_lane(_pick_lane(dq0, _aug_lane(0)), 2 * p) + _put_lane(_pick_lane(dq1, _aug_lane(1)), 2 * p + 1))
        dk0, dk1 = dk0_ref[...], dk1_ref[...]
        dk_ref[...] = jnp.where(_head_masks(), dk0, dk1)
        dv_ref[...] = dva_ref[...].astype(BF16)
        dfk = _put_lane(_pick_lane(dk0, _aug_lane(0) + 3), 2 * p) + _put_lane(_pick_lane(dk1, _aug_lane(1) + 3), 2 * p + 1)
        _accumulate(dfq_ref, jnp.concatenate(dfq_cols, axis=0), p == 0)
        _accumulate(dfk_ref, -dfk, p == 0)

    wide = pl.BlockSpec((seq, 2 * LANES), lambda b, p: (b, p))
    blk = pl.BlockSpec((seq, LANES), lambda b, p: (b, p))
    col = pl.BlockSpec((seq, LANES), lambda b, p: (b, 0))
    stat = pl.BlockSpec((n_tiles, 8, ATT_SUB), lambda b, p: (b, 0, 0))
    f32_blk, acc = jax.ShapeDtypeStruct((n_batch * seq, pairs * LANES), F32), pltpu.VMEM((seq, LANES), F32)
    return _pallas(
        body, name="attn_bwd", args=[qa, ka, vb, do, lse, delta],
        out_shape=[f32_blk, f32_blk, jax.ShapeDtypeStruct((n_batch * seq, pairs * LANES), BF16),
                   jax.ShapeDtypeStruct((n_batch * seq, LANES), F32), jax.ShapeDtypeStruct((n_batch * seq, LANES), F32)],
        grid=(n_batch, pairs), in_specs=[wide, wide, blk, blk, stat, stat], out_specs=[blk, blk, blk, col, col],
        scratch_shapes=[pltpu.VMEM((LANES, tq), F32), pltpu.VMEM((LANES, tq), F32), acc, acc, acc], plan=plan)


def _forget_bwd(dfq, dfk, f, bias, n_batch, seq):
    def body(dfq_ref, dfk_ref, f_ref, b_ref, df_ref, db_ref):
        acc = dfq_ref[...] + dfk_ref[...]
        row = lax.broadcasted_iota(jnp.int32, (seq, 1), 0)
        dist = 1
        while dist < seq:
            acc = acc + _shift_up(acc, dist, row, seq)
            dist *= 2
        df = acc * _sigmoid(-(f_ref[...] + b_ref[...]))
        df_ref[...] = df
        db_ref[...] = jnp.sum(df, axis=0, keepdims=True)

    col = pl.BlockSpec((seq, LANES), lambda b: (b, 0))
    return pl.pallas_call(
        body,
        out_shape=[jax.ShapeDtypeStruct((n_batch * seq, LANES), F32), jax.ShapeDtypeStruct((n_batch, 1, LANES), F32)],
        grid=(n_batch,), in_specs=[col, col, col, pl.BlockSpec((1, LANES), lambda b: (0, 0))],
        out_specs=[col, pl.BlockSpec((None, 1, LANES), lambda b: (b, 0, 0))],
        compiler_params=_params(), name="forget_bwd",
    )(dfq, dfk, f, bias)


def _mix_out(x1, yp, o, ona, woa, wob):
    t, d = x1.shape
    width = o.shape[1]
    tm = min(512, t)

    def body(x_ref, yp_ref, o_ref, on_ref, wa_ref, wb_ref, x2_ref, ya_ref):
        of = o_ref[...]
        ya = ((of * _rms(of)) * on_ref[...]).astype(BF16)
        ya_ref[...] = ya
        x2_ref[...] = x_ref[...] + (_dot(yp_ref[...], wa_ref[...]) + _dot(ya, wb_ref[...]))

    row = pl.BlockSpec((tm, d), lambda i: (i, 0))
    half = pl.BlockSpec((tm, width), lambda i: (i, 0))
    wspec = pl.BlockSpec((width, d), lambda i: (0, 0))
    return pl.pallas_call(
        body, out_shape=[jax.ShapeDtypeStruct((t, d), F32), jax.ShapeDtypeStruct((t, width), BF16)],
        grid=(t // tm,), in_specs=[row, half, half, pl.BlockSpec((1, width), lambda i: (0, 0)), wspec, wspec],
        out_specs=[row, half], compiler_params=_params(), name="mix_out",
    )(x1, yp, o, ona, woa, wob)


def _mix_out_bwd(dx2, o, yp, ya, ona, woa, wob, plan=None):
    t, d = dx2.shape
    width = o.shape[1]
    tm = min(512, t)
    nt = t // tm

    def body(dx_ref, o_ref, yp_ref, ya_ref, on_ref, wa_ref, wb_ref, dyp_ref, do_ref, dl_ref, dwa_ref, dwb_ref, don_ref):
        @pl.when(pl.program_id(0) == 0)
        def _():
            dwa_ref[...] = jnp.zeros_like(dwa_ref)
            dwb_ref[...] = jnp.zeros_like(dwb_ref)

        dxb = dx_ref[...].astype(BF16)
        dwa_ref[...] += _dot_tn(yp_ref[...], dxb)
        dwb_ref[...] += _dot_tn(ya_ref[...], dxb)
        dyp_ref[...] = _dot_nt(dxb, wa_ref[...])
        of = o_ref[...]
        dov, dgr = _rms_bwd(of, _rms(of), on_ref[...], _dot_nt(dxb, wb_ref[...]))
        don_ref[...] = jnp.sum(dgr, axis=0, keepdims=True)
        do_ref[...] = dov.astype(BF16)
        lo = _head_masks()
        prod = dov * of
        delta = jnp.zeros((tm, LANES), F32)
        for blk in range(width // LANES):
            pb = prod[:, blk * LANES:(blk + 1) * LANES]
            delta = delta + _put_lane(jnp.sum(jnp.where(lo, pb, 0.0), axis=1, keepdims=True), 2 * blk)
            delta = delta + _put_lane(jnp.sum(jnp.where(lo, 0.0, pb), axis=1, keepdims=True), 2 * blk + 1)
        for c in range(tm // ATT_SUB):
            dl_ref[c] = delta[c * ATT_SUB:(c + 1) * ATT_SUB, :].T[0:8, :]

    row = pl.BlockSpec((tm, d), lambda i: (i, 0))
    half = pl.BlockSpec((tm, width), lambda i: (i, 0))
    wspec = pl.BlockSpec((width, d), lambda i: (0, 0))
    return _pallas(
        body, name="mix_out_bwd", args=[dx2, o, yp, ya, ona, woa, wob],
        out_shape=[jax.ShapeDtypeStruct((t, width), F32), jax.ShapeDtypeStruct((t, width), BF16),
                   jax.ShapeDtypeStruct((t // ATT_SUB, 8, ATT_SUB), F32), jax.ShapeDtypeStruct((width, d), F32),
                   jax.ShapeDtypeStruct((width, d), F32), jax.ShapeDtypeStruct((nt, 1, width), F32)],
        grid=(nt,),
        in_specs=[row, half, half, half, pl.BlockSpec((1, width), lambda i: (0, 0)), wspec, wspec],
        out_specs=[half, half, pl.BlockSpec((tm // ATT_SUB, 8, ATT_SUB), lambda i: (i, 0, 0)), wspec, wspec,
                   pl.BlockSpec((None, 1, width), lambda i: (i, 0, 0))], plan=plan)


def _mix_in_bwd(dx2, x1, gain, hm, dpv, dqh, q, dkh, k, dv, df, qn, kn, wt):
    t, d = x1.shape
    width = q.shape[1]
    pool_width = dpv.shape[1]
    tm = min(512, t)
    nt = t // tm
    scale = HEAD_DIM ** -0.5
    c_q, c_k, c_v = pool_width, pool_width + width, pool_width + 2 * width
    c_f = c_v + width

    def body(dx2_ref, x_ref, g_ref, hm_ref, dpv_ref, dqh_ref, q_ref, dkh_ref, k_ref, dv_ref, df_ref, qn_ref, kn_ref,
             wt_ref, dx_ref, dxh_ref, dwt_ref, dg_ref, dqn_ref, dkn_ref):
        @pl.when(pl.program_id(0) == 0)
        def _():
            dwt_ref[...] = jnp.zeros_like(dwt_ref)

        lo = _head_masks()
        for part, rows in enumerate(_row_halves(tm)):
            def put(ref, sl, value):
                ref[:, sl] = value if part == 0 else ref[:, sl] + value

            hm = hm_ref[rows, :]
            pieces = [(0, dpv_ref[rows, :])]
            for c0, raw_ref, dh_ref, n_ref, dn_ref, mul in ((c_q, q_ref, dqh_ref, qn_ref, dqn_ref, scale),
                                                           (c_k, k_ref, dkh_ref, kn_ref, dkn_ref, 1.0)):
                cols = []
                for blk in range(width // LANES):
                    sl = slice(blk * LANES, (blk + 1) * LANES)
                    xb = raw_ref[rows, sl]
                    gb = dh_ref[rows, sl] * mul
                    r = _head_rms(xb, lo)
                    xh = xb * r
                    dyg = gb * n_ref[:, sl]
                    cols.append((r * (dyg - xh * _head_mean(dyg * xh, lo))).astype(BF16))
                    put(dn_ref, sl, jnp.sum(gb * xh, axis=0, keepdims=True))
                pieces.append((c0, jnp.concatenate(cols, axis=1)))
            pieces.append((c_v, dv_ref[rows, :]))
            pieces.append((c_f, df_ref[rows, :].astype(BF16)))
            dhm = jnp.zeros((tm // 2, d), F32)
            for c0, piece in pieces:
                dwt_ref[c0:c0 + piece.shape[1], :] += _dot_tn(piece, hm)
                dhm = dhm + _dot(piece, wt_ref[c0:c0 + piece.shape[1], :])
            xf = x_ref[rows, :]
            dxn, dgr = _rms_bwd(xf, _rms(xf), g_ref[...], dhm)
            dx = dx2_ref[rows, :] + dxn
            dx_ref[rows, :] = dx
            dxh_ref[rows, :] = (0.5 * dx).astype(BF16)
            put(dg_ref, slice(None), jnp.sum(dgr, axis=0, keepdims=True))

    row = pl.BlockSpec((tm, d), lambda i: (i, 0))
    half = pl.BlockSpec((tm, width), lambda i: (i, 0))
    const = lambda shape: pl.BlockSpec(shape, lambda i: (0, 0))
    pvec = lambda n: pl.BlockSpec((None, 1, n), lambda i: (i, 0, 0))
    return pl.pallas_call(
        body,
        out_shape=[jax.ShapeDtypeStruct((t, d), F32), jax.ShapeDtypeStruct((t, d), BF16), jax.ShapeDtypeStruct(wt.shape, F32),
                   jax.ShapeDtypeStruct((nt, 1, d), F32),
                   jax.ShapeDtypeStruct((nt, 1, width), F32), jax.ShapeDtypeStruct((nt, 1, width), F32)],
        grid=(nt,),
        in_specs=[row, row, const((1, d)), row, pl.BlockSpec((tm, pool_width), lambda i: (i, 0)), half, half, half, half,
                  half, pl.BlockSpec((tm, LANES), lambda i: (i, 0)), const((1, width)), const((1, width)),
                  const(wt.shape)],
        out_specs=[row, row, const(wt.shape), pvec(d), pvec(width), pvec(width)],
        compiler_params=_params(), name="mix_in_bwd",
    )(dx2, x1, gain, hm, dpv, dqh, q, dkh, k, dv, df, qn, kn, wt)


def _mesh_pos():
    return lax.axis_index("x"), lax.axis_index("y"), lax.axis_index("c")


def _other_chips(x, y):
    return [(1 - x, y), (x, 1 - y), (1 - x, 1 - y)]


def _remote(src, dst, send_sem, recv_sem, device):
    return pltpu.make_async_remote_copy(src_ref=src, dst_ref=dst, send_sem=send_sem, recv_sem=recv_sem,
                                        device_id=device, device_id_type=pl.DeviceIdType.MESH)


def _half_rows(n_rows, which):
    half = n_rows // 2
    return pl.ds(pl.multiple_of(which * half, 8), half)


def _row_block(rows, cols, itemsize=4):
    rb = rows
    while rb * cols * itemsize > (1 << 20) and rb % 32 == 0:
        rb //= 2
    return rb


def _place_cast(ws, chip, tag):
    n = len(ws)
    rows, cols = ws[0].shape
    rb = _row_block(rows, cols)

    def body(k_ref, *refs):
        for w_ref, o_ref in zip(refs[:n], refs[n:]):
            o_ref[...] = w_ref[...].astype(BF16)

    return pl.pallas_call(
        body, out_shape=[jax.ShapeDtypeStruct((N_CHIPS, rows, cols), BF16)] * n,
        grid_spec=pltpu.PrefetchScalarGridSpec(
            num_scalar_prefetch=1, grid=(rows // rb,),
            in_specs=[pl.BlockSpec((rb, cols), lambda i, k: (i, 0))] * n,
            out_specs=[pl.BlockSpec((None, rb, cols), lambda i, k: (k[0], i, 0))] * n),
        compiler_params=_params(), name="place_" + tag,
    )(chip, *ws)


class _Plan:
    def __init__(self, ins, outs, alias, sems, start, finish, middle=None, middle_at=(3, 4)):
        self.ins, self.outs, self.alias, self.sems = ins, outs, alias, sems
        self.start, self.middle, self.finish, self.middle_at = start, middle, finish, middle_at


def _merge_plans(a, b):
    ni, no, ns = len(a.ins), len(a.outs), len(a.sems)
    alias = dict(a.alias)
    alias.update({ni + i: no + o for i, o in b.alias.items()})

    def both(which):
        stage_a, stage_b = getattr(a, which), getattr(b, which)
        if stage_a is None and stage_b is None:
            return None

        def run(ins, outs, sems):
            if stage_a is not None:
                stage_a(ins[:ni], outs[:no], sems[:ns])
            if stage_b is not None:
                stage_b(ins[ni:], outs[no:], sems[ns:])
        return run

    return _Plan(list(a.ins) + list(b.ins), list(a.outs) + list(b.outs), alias, list(a.sems) + list(b.sems),
                 both("start"), both("finish"), both("middle"), a.middle_at if a.middle is not None else b.middle_at)


def _run_plan(plan, name):
    n_in, n_out = len(plan.ins), len(plan.outs)

    def body(*refs):
        parts = refs[:n_in], refs[n_in:n_in + n_out], refs[n_in + n_out:]
        plan.start(*parts)
        if plan.middle is not None:
            plan.middle(*parts)
        plan.finish(*parts)

    return pl.pallas_call(
        body, out_shape=plan.outs, in_specs=[ANY] * n_in, out_specs=[ANY] * n_out, scratch_shapes=plan.sems,
        input_output_aliases=plan.alias, name=name,
    )(*plan.ins)


def _pallas(body, *, name, args, in_specs, out_shape, out_specs, grid, scratch_shapes=(), plan=None, aliases=None):
    n_in, n_out, n_scr = len(args), len(out_shape), len(scratch_shapes)
    plan = plan or _Plan([], [], {}, [], None, None)
    p_in, p_out = len(plan.ins), len(plan.outs)

    def carrying(*refs):
        ins, p_ins = refs[:n_in], refs[n_in:n_in + p_in]
        o0 = n_in + p_in
        outs, p_outs = refs[o0:o0 + n_out], refs[o0 + n_out:o0 + n_out + p_out]
        s0 = o0 + n_out + p_out
        scr, p_sems = refs[s0:s0 + n_scr], refs[s0 + n_scr:]
        ids = [pl.program_id(a) for a in range(len(grid))]

        if plan.start is not None:
            @pl.when(functools.reduce(jnp.logical_and, [i == 0 for i in ids]))
            def _():
                plan.start(p_ins, p_outs, p_sems)

        body(*ins, *outs, *scr)

        if plan.middle is not None:
            step, n_steps = 0, 1
            for i, g in zip(ids, grid):
                step, n_steps = step * g + i, n_steps * g

            @pl.when(step == (plan.middle_at[0] * n_steps) // plan.middle_at[1])
            def _():
                plan.middle(p_ins, p_outs, p_sems)

        if plan.finish is not None:
            @pl.when(functools.reduce(jnp.logical_and, [i == g - 1 for i, g in zip(ids, grid)]))
            def _():
                plan.finish(p_ins, p_outs, p_sems)

    aliases = dict(aliases or {})
    aliases.update({n_in + i: n_out + o for i, o in plan.alias.items()})
    res = pl.pallas_call(
        carrying, out_shape=list(out_shape) + list(plan.outs), grid=grid,
        in_specs=list(in_specs) + [ANY] * p_in, out_specs=list(out_specs) + [ANY] * p_out,
        scratch_shapes=list(scratch_shapes) + list(plan.sems),
        input_output_aliases=aliases, compiler_params=_params(), name=name,
    )(*args, *plan.ins)
    return list(res[:n_out]), list(res[n_out:])


def _plan_gather(stacks):
    n = len(stacks)
    relations = range(3)

    def ici_copies(outs, sems):
        x, y, c = _mesh_pos()
        chips = _other_chips(x, y)
        cps = []
        for w in range(n):
            own = outs[w].at[2 * x + y, _half_rows(stacks[w].shape[1], c)]
            cps += [_remote(own, own, sems[0].at[w, j], sems[1].at[w, j], (*chips[j], c)) for j in relations]
        return cps

    def start(ins, outs, sems):
        for cp in ici_copies(outs, sems):
            cp.start()

    def forwards(outs, sems, core):
        x, y, c = _mesh_pos()
        slots = [2 * cx + cy for cx, cy in _other_chips(x, y)]
        cps = []
        for w in range(n):
            rows = _half_rows(stacks[w].shape[1], core)
            for j in relations:
                landed = outs[w].at[slots[j], rows]
                cps.append((_remote(landed, landed, sems[0].at[w, j], sems[1].at[w, j], (x, y, 1 - c)),
                            _remote(landed, landed, sems[2].at[w, j], sems[3].at[w, j], (x, y, 1 - c))))
        return cps

    def middle(ins, outs, sems):
        c = _mesh_pos()[2]
        for arrival, forward in forwards(outs, sems, c):
            arrival.wait_recv()
            forward.start()

    def finish(ins, outs, sems):
        c = _mesh_pos()[2]
        for _, forward in forwards(outs, sems, 1 - c):
            forward.wait_recv()
        for cp in ici_copies(outs, sems) + [forward for _, forward in forwards(outs, sems, c)]:
            cp.wait_send()

    return _Plan(stacks, [jax.ShapeDtypeStruct(s.shape, s.dtype) for s in stacks], {w: w for w in range(n)},
                 [pltpu.SemaphoreType.DMA((n, 3))] * 4, start, finish, middle)


RELAY_SEMS = [pltpu.SemaphoreType.DMA((3, 2))] * 4 + [pltpu.SemaphoreType.DMA((3, 3))] * 2


def _relay_gather_stage(stage, outs, sems):
    send, recv, relay_send, relay_recv, d2d_send, d2d_recv = sems
    n = len(outs)
    rh = outs[0].shape[1] // 2
    mx, my, c = _mesh_pos()
    sibling = (mx, my, 1 - c)
    near = [(1 - mx, my), (mx, 1 - my)]
    slots = [2 * cx + cy for cx, cy in near] + [2 * (1 - mx) + (1 - my)]

    def piece(w, slot, core, quarter=None):
        if quarter is None:
            return outs[w].at[slot, _half_rows(2 * rh, core)]
        return outs[w].at[slot, pl.ds(pl.multiple_of(core * rh + quarter * (rh // 2), 8), rh // 2)]

    def to_near(w, j):
        own = piece(w, 2 * mx + my, c)
        return _remote(own, own, send.at[w, j], recv.at[w, j], (*near[j], c))

    def from_near(w, j):
        landed = piece(w, slots[j], c)
        return _remote(landed, landed, send.at[w, j], recv.at[w, j], sibling)

    def onward(w, j, slot):
        part = piece(w, slot, c, quarter=j)
        return _remote(part, part, relay_send.at[w, j], relay_recv.at[w, j], (*near[1 - j], c))

    def to_sibling(w, j, core):
        landed = piece(w, slots[j], core)
        return _remote(landed, landed, d2d_send.at[w, j], d2d_recv.at[w, j], sibling)

    if stage == 0:
        for w in range(n):
            for j in range(2):
                to_near(w, j).start()
    elif stage == 1:
        for w in range(n):
            for j in range(2):
                from_near(w, j).wait_recv()
                onward(w, j, slots[j]).start()
                to_sibling(w, j, c).start()
        for w in range(n):
            to_sibling(w, 0, 1 - c).wait_recv()
    elif stage == 2:
        for w in range(n):
            to_sibling(w, 1, 1 - c).wait_recv()
    elif stage == 3:
        for w in range(n):
            for j in range(2):
                onward(w, j, slots[2]).wait_recv()
            to_sibling(w, 2, c).start()
        for w in range(n):
            to_sibling(w, 2, 1 - c).wait_recv()
    else:
        for w in range(n):
            for j in range(2):
                to_near(w, j).wait_send()
                onward(w, j, slots[j]).wait_send()
            for j in range(3):
                to_sibling(w, j, c).wait_send()


def _plan_gather_relay(stacks):
    def stages(which):
        def run(ins, outs, sems):
            for stage in which:
                _relay_gather_stage(stage, outs, sems)
        return run

    return _Plan(stacks, [jax.ShapeDtypeStruct(s.shape, s.dtype) for s in stacks], {w: w for w in range(len(stacks))},
                 RELAY_SEMS, stages([0]), stages([3, 4]), stages([1, 2]), middle_at=(5, 8))


def _ffn_fwd_gathering(x, gain, stacks, order, later):
    t, d = x.shape
    nch, fc, _ = stacks[0].shape
    n = len(stacks)
    tm = min(FFN_STAGED_TILE, t)
    nt = t // tm
    p_in, p_out = len(later.ins), len(later.outs)
    relay = _relay_gather_stage

    def body(order_ref, x_ref, g_ref, *refs):
        later_in, refs = refs[n:n + p_in], refs[n + p_in:]
        o_ref, h_ref, a_ref, b_ref, s_ref = refs[:5]
        stack_refs, later_out, refs = refs[5:5 + n], refs[5 + n:5 + n + p_out], refs[5 + n + p_out:]
        w_ref, hs_ref, acc_ref, w_sem = refs[:4]
        relay_sems, later_sems = refs[4:10], refs[10:]
        k, i = pl.program_id(0), pl.program_id(1)
        tile = pl.ds(pl.multiple_of(i * tm, tm), tm)

        @pl.when(i == 0)
        def _():
            for stage in range(nch):
                @pl.when(k == stage)
                def _():
                    relay(stage, stack_refs, relay_sems)
                    if stage == 1 and later.start is not None:
                        later.start(later_in, later_out, later_sems)
            loads = [pltpu.make_async_copy(stack_refs[w].at[order_ref[k]], w_ref.at[w], w_sem.at[w]) for w in range(n)]
            for cp in loads:
                cp.start()
            for cp in loads:
                cp.wait()

        @pl.when(k == 0)
        def _():
            xf = x_ref[...]
            hb = ((xf * _rms(xf)) * g_ref[...]).astype(BF16)
            h_ref[...] = hb
            hs_ref[tile, :] = hb
            acc_ref[tile, :] = jnp.zeros((tm, d), F32)

        for rows in _row_halves(tm):
            part = pl.ds(pl.multiple_of(i * tm + rows.start, tm // 2), tm // 2)
            h = hs_ref[part, :]
            a = _dot_nt(h, w_ref[0])
            b = _dot_nt(h, w_ref[1])
            sb = ((a * (0.5 * jnp.tanh(0.5 * a) + 0.5)) * b).astype(BF16)
            a_ref[rows, :] = a.astype(BF16)
            b_ref[rows, :] = b.astype(BF16)
            s_ref[rows, :] = sb
            acc_ref[part, :] += _dot(sb, w_ref[2])

        @pl.when(k == nch - 1)
        def _():
            o_ref[...] = x_ref[...] + 0.5 * acc_ref[tile, :]

        if later.middle is not None:
            @pl.when((k == nch - 1) & (i == nt // 2))
            def _():
                later.middle(later_in, later_out, later_sems)

        @pl.when((k == nch - 1) & (i == nt - 1))
        def _():
            relay(nch, stack_refs, relay_sems)
            if later.finish is not None:
                later.finish(later_in, later_out, later_sems)

    ends = lambda k, i: jnp.where((k == 0) | (k == nch - 1), i, 0)
    act = pl.BlockSpec((None, tm, fc), lambda k, i, order: (order[k], i, 0))
    out_shape = [jax.ShapeDtypeStruct((t, d), F32), jax.ShapeDtypeStruct((t, d), BF16)]
    out_shape += [jax.ShapeDtypeStruct((nch, t, fc), BF16)] * 3
    out_shape += [jax.ShapeDtypeStruct(s.shape, s.dtype) for s in stacks] + list(later.outs)
    aliases = {3 + w: 5 + w for w in range(n)}
    aliases.update({3 + n + i: 5 + n + o for i, o in later.alias.items()})
    res = pl.pallas_call(
        body, out_shape=out_shape,
        grid_spec=pltpu.PrefetchScalarGridSpec(
            num_scalar_prefetch=1, grid=(nch, nt),
            in_specs=[pl.BlockSpec((tm, d), lambda k, i, order: (ends(k, i), 0)),
                      pl.BlockSpec((1, d), lambda k, i, order: (0, 0))] + [ANY] * (n + p_in),
            out_specs=[pl.BlockSpec((tm, d), lambda k, i, order: (jnp.where(k == nch - 1, i, 0), 0)),
                       pl.BlockSpec((tm, d), lambda k, i, order: (jnp.where(k == 0, i, nt - 1), 0)),
                       act, act, act] + [ANY] * (n + p_out),
            scratch_shapes=[pltpu.VMEM((n, fc, d), BF16), pltpu.VMEM((t, d), BF16), pltpu.VMEM((t, d), F32),
                            pltpu.SemaphoreType.DMA((n,))] + RELAY_SEMS + list(later.sems)),
        input_output_aliases=aliases, compiler_params=_params(), name="ffn_fwd",
    )(order, x, gain, *stacks, *later.ins)
    return list(res[:5]), list(res[5:5 + n]), list(res[5 + n:])


def _plan_sibling_halves(gs):
    n = len(gs)

    def copies(ins, outs, sems):
        x, y, c = _mesh_pos()
        return [_remote(ins[w].at[:, _half_rows(gs[w].shape[1], 1 - c), :], outs[w], sems[0].at[w], sems[1].at[w],
                        (x, y, 1 - c)) for w in range(n)]

    def start(ins, outs, sems):
        for cp in copies(ins, outs, sems):
            cp.start()

    def finish(ins, outs, sems):
        for cp in copies(ins, outs, sems):
            cp.wait()

    return _Plan(gs, [jax.ShapeDtypeStruct((g.shape[0], g.shape[1] // 2, g.shape[2]), g.dtype) for g in gs], {},
                 [pltpu.SemaphoreType.DMA((n,))] * 2, start, finish)


def _plan_chip_exchange(ps):
    n = len(ps)

    def copies(ins, outs, sems):
        x, y, c = _mesh_pos()
        return [_remote(ins[w].at[2 * cx + cy], outs[w].at[j], sems[0].at[w, j], sems[1].at[w, j], (cx, cy, c))
                for w in range(n) for j, (cx, cy) in enumerate(_other_chips(x, y))]

    def start(ins, outs, sems):
        for cp in copies(ins, outs, sems):
            cp.start()

    def finish(ins, outs, sems):
        for cp in copies(ins, outs, sems):
            cp.wait()

    return _Plan(ps, [jax.ShapeDtypeStruct((3,) + p.shape[1:], p.dtype) for p in ps], {},
                 [pltpu.SemaphoreType.DMA((n, 3))] * 2, start, finish)


def _plan_sibling_share(gs):
    n = len(gs)

    def copies(outs, sems, which):
        x, y, c = _mesh_pos()
        cps = []
        for w in range(n):
            rows = outs[w].at[_half_rows(gs[w].shape[0], c if which == "mine" else 1 - c)]
            cps.append(_remote(rows, rows, sems[0].at[w], sems[1].at[w], (x, y, 1 - c)))
        return cps

    def start(ins, outs, sems):
        for cp in copies(outs, sems, "mine"):
            cp.start()

    def finish(ins, outs, sems):
        for cp in copies(outs, sems, "mine"):
            cp.wait_send()
        for cp in copies(outs, sems, "theirs"):
            cp.wait_recv()

    return _Plan(gs, [jax.ShapeDtypeStruct(g.shape, g.dtype) for g in gs], {w: w for w in range(n)},
                 [pltpu.SemaphoreType.DMA((n,))] * 2, start, finish)


def _same_shape_groups(arrays):
    groups = {}
    for i, a in enumerate(arrays):
        groups.setdefault(a.shape, []).append(i)
    return list(groups.values())


def _add_sibling(gs, r1s, ids, tag):
    n = len(gs)
    nch, rh, cols = r1s[0].shape

    def body(ids_ref, *refs):
        for g_ref, r_ref, o_ref in zip(refs[:n], refs[n:2 * n], refs[2 * n:]):
            o_ref[...] = (g_ref[...] + r_ref[...]).astype(BF16)

    blk = lambda fn: pl.BlockSpec((None, rh, cols), fn)
    return pl.pallas_call(
        body, out_shape=[jax.ShapeDtypeStruct(r1s[0].shape, BF16)] * n,
        grid_spec=pltpu.PrefetchScalarGridSpec(
            num_scalar_prefetch=1, grid=(nch,),
            in_specs=[blk(lambda k, ids: (k, ids[1], 0))] * n + [blk(lambda k, ids: (k, 0, 0))] * n,
            out_specs=[blk(lambda k, ids: (k, 0, 0))] * n),
        compiler_params=_params(), name="add_sibling_" + tag,
    )(ids, *gs, *r1s)


def _add_chips(gs, r1s, r2s, ids, tag):
    n = len(gs)
    _, rh, cols = r1s[0].shape
    nb = 2 if rh % 32 == 0 else 1
    rb = rh // nb

    def body(ids_ref, *refs):
        for g_ref, r1_ref, r2_ref, o_ref in zip(refs[:n], refs[n:2 * n], refs[2 * n:3 * n], refs[3 * n:]):
            own = g_ref[...] + r1_ref[...]
            o_ref[...] = ((own + r2_ref[0].astype(F32)) + r2_ref[1].astype(F32)) + r2_ref[2].astype(F32)

    return pl.pallas_call(
        body, out_shape=[jax.ShapeDtypeStruct((2 * rh, cols), F32)] * n,
        grid_spec=pltpu.PrefetchScalarGridSpec(
            num_scalar_prefetch=1, grid=(nb,),
            in_specs=[pl.BlockSpec((None, rb, cols), lambda i, ids: (ids[0], ids[1] * nb + i, 0))] * n
            + [pl.BlockSpec((None, rb, cols), lambda i, ids: (ids[0], i, 0))] * n
            + [pl.BlockSpec((3, rb, cols), lambda i, ids: (0, i, 0))] * n,
            out_specs=[pl.BlockSpec((rb, cols), lambda i, ids: (ids[1] * nb + i, 0))] * n),
        compiler_params=_params(), name="add_chips_" + tag,
    )(ids, *gs, *r1s, *r2s)


VEC_ROWS = 8


N_DEVICES = 8


def _small_pack(part, d, width):
    names = ("ffn1_norm", "mix_norm", "ffn2_norm", "pool_scale", "out_norm_pool", "out_norm_attn", "qn", "kn", "b_forget",
             "pool_w", "loss")
    args = [part[k] for k in names]
    pw_shape = part["pool_w"].shape[1:]

    def body(g1_ref, gm_ref, g2_ref, ps_ref, onp_ref, ona_ref, qn_ref, kn_ref, bf_ref, pw_ref, loss_ref, vbuf, pbuf):
        lo = _head_masks()

        def fold_heads(ref):
            v = jnp.sum(ref[...], axis=0)
            acc = jnp.zeros((VEC_ROWS, LANES), F32)
            for blk in range(width // LANES):
                vb = jnp.broadcast_to(v[:, blk * LANES:(blk + 1) * LANES], (VEC_ROWS, LANES))
                acc = acc + vb + pltpu.roll(vb, HEAD_DIM, 1)
            return jnp.where(lo, acc, 0.0)[0:1, :]

        vbuf[0] = jnp.zeros((VEC_ROWS, d), F32)
        vbuf[0, 0:1, :] = jnp.sum(g1_ref[...], axis=0)
        vbuf[0, 1:2, :] = jnp.sum(gm_ref[...], axis=0)
        vbuf[0, 2:3, :] = jnp.sum(g2_ref[...], axis=0)
        vbuf[0, 5:6, 0:LANES] = jnp.sum(loss_ref[...], axis=0)[0:1, :]
        vbuf[0, 3:4, 0:width] = jnp.sum(ps_ref[...], axis=0)
        vbuf[0, 3:4, width:2 * width] = jnp.sum(onp_ref[...], axis=0)
        vbuf[0, 4:5, 0:width] = jnp.sum(ona_ref[...], axis=0)
        vbuf[0, 4:5, width:width + LANES] = fold_heads(qn_ref)
        vbuf[0, 4:5, width + LANES:width + 2 * LANES] = fold_heads(kn_ref)
        vbuf[0, 4:5, width + 2 * LANES:width + 3 * LANES] = jnp.sum(bf_ref[...], axis=0)
        pbuf[0] = jnp.sum(pw_ref[...], axis=0)

    return pl.pallas_call(
        body, out_shape=[jax.ShapeDtypeStruct((N_DEVICES, VEC_ROWS, d), F32), jax.ShapeDtypeStruct((N_DEVICES,) + pw_shape, F32)],
        in_specs=[VM] * len(args), out_specs=[VM, VM], compiler_params=_params(), name="small_pack",
    )(*args)


def _plan_all_to_all(stacks):
    n = len(stacks)

    def copies(outs, sems):
        x, y, c = _mesh_pos()
        cps = []
        for r in range(1, N_DEVICES):
            peer = (x if not r & 4 else 1 - x, y if not r & 2 else 1 - y, c if not r & 1 else 1 - c)
            cps += [_remote(outs[w].at[0], outs[w].at[r], sems[0].at[w, r - 1], sems[1].at[w, r - 1], peer) for w in range(n)]
        return cps

    def start(ins, outs, sems):
        for cp in copies(outs, sems):
            cp.start()

    def finish(ins, outs, sems):
        for cp in copies(outs, sems):
            cp.wait()

    return _Plan(stacks, [jax.ShapeDtypeStruct(s.shape, s.dtype) for s in stacks], {w: w for w in range(n)},
                 [pltpu.SemaphoreType.DMA((n, N_DEVICES - 1))] * 2, start, finish)


def _small_sum(vstack, pstack, me):
    def body(me_ref, vbuf, pbuf, vec_ref, pw_ref):
        vec = vbuf[me_ref[0]]
        pw = pbuf[me_ref[0]]
        for dev in range(1, N_DEVICES):
            vec = vec + vbuf[jnp.bitwise_xor(me_ref[0], dev)]
            pw = pw + pbuf[jnp.bitwise_xor(me_ref[0], dev)]
        vec_ref[...] = vec
        pw_ref[...] = pw

    full = lambda s: pl.BlockSpec(s.shape, lambda i, me: (0,) * len(s.shape))
    outs = [jax.ShapeDtypeStruct(vstack.shape[1:], F32), jax.ShapeDtypeStruct(pstack.shape[1:], F32)]
    return pl.pallas_call(
        body, out_shape=outs,
        grid_spec=pltpu.PrefetchScalarGridSpec(num_scalar_prefetch=1, grid=(1,), in_specs=[full(vstack), full(pstack)],
                                               out_specs=[full(o) for o in outs]),
        compiler_params=_params(), name="small_sum",
    )(me, vstack, pstack)


def _adamw(ws, gs, ms, vs, tag):
    n = len(ws)
    rows, cols = ws[0].shape
    rb = rows
    while rb * cols * 4 * n > (1 << 20) and rb % 16 == 0:
        rb //= 2

    def body(*refs):
        for j in range(n):
            w_ref, g_ref, m_ref, v_ref = (refs[k * n + j] for k in range(4))
            go_ref, d_ref, mo_ref, vo_ref = (refs[(4 + k) * n + j] for k in range(4))
            gv = g_ref[...]
            go_ref[...] = gv
            m2 = ADAM_B1 * m_ref[...] + (1.0 - ADAM_B1) * gv
            v2 = ADAM_B2 * v_ref[...] + (1.0 - ADAM_B2) * (gv * gv)
            m_hat = m2 / (1.0 - ADAM_B1 ** ADAM_STEP)
            v_hat = v2 / (1.0 - ADAM_B2 ** ADAM_STEP)
            d_ref[...] = -ADAM_LR * (m_hat / (jnp.sqrt(v_hat) + ADAM_EPS) + ADAM_WD * w_ref[...])
            mo_ref[...] = m2
            vo_ref[...] = v2

    spec = pl.BlockSpec((rb, cols), lambda i: (i, 0))
    res, _ = _pallas(
        body, name="adamw_" + tag, args=[*ws, *gs, *ms, *vs], out_shape=[jax.ShapeDtypeStruct(ws[0].shape, F32)] * (4 * n),
        grid=(rows // rb,), in_specs=[spec] * (4 * n), out_specs=[spec] * (4 * n))
    return [tuple(res[k * n + j] for k in range(4)) for j in range(n)]


def _pack_vec(p, d, width):
    pad = lambda v: jnp.pad(v, (0, LANES - v.shape[0]))
    row3 = jnp.concatenate([p["pool_scale"], p["out_norm_pool"]])
    row4 = jnp.concatenate([p["out_norm_attn"], pad(p["q_norm"]), pad(p["k_norm"]), pad(p["b_forget"]),
                            jnp.zeros((d - width - 3 * LANES,), F32)])
    rows = [p["ffn1_norm"], p["mix_norm"], p["ffn2_norm"], row3, row4]
    return jnp.pad(jnp.stack(rows), ((0, VEC_ROWS - len(rows)), (0, 0)))


def _unpack_vec(vec, width):
    return dict(ffn1_norm=vec[0], mix_norm=vec[1], ffn2_norm=vec[2], pool_scale=vec[3, :width],
                out_norm_pool=vec[3, width:2 * width], out_norm_attn=vec[4, :width],
                q_norm=vec[4, width:width + HEAD_DIM], k_norm=vec[4, width + LANES:width + LANES + HEAD_DIM],
                b_forget=vec[4, width + 2 * LANES:width + 2 * LANES + N_HEADS])


WEIGHT_NAMES = ("ffn1_norm", "ffn1_w_gate", "ffn1_w_up", "ffn1_w_down", "mix_norm", "w_in", "b_forget", "pool_w",
                "pool_scale", "q_norm", "k_norm", "out_norm_pool", "out_norm_attn", "w_out", "ffn2_norm",
                "ffn2_w_gate", "ffn2_w_up", "ffn2_w_down")
BIG_NAMES = ("ffn1_w_gate", "ffn1_w_up", "ffn1_w_down", "w_in", "w_out", "ffn2_w_gate", "ffn2_w_up", "ffn2_w_down")
TRANSPOSED_NAMES = ("ffn1_w_gate", "ffn1_w_up", "w_in", "ffn2_w_gate", "ffn2_w_up")
FFN1_NAMES = ("ffn1_w_gate", "ffn1_w_up", "ffn1_w_down")
MIX_NAMES = ("w_in", "w_out")
FFN2_NAMES = ("ffn2_w_gate", "ffn2_w_up", "ffn2_w_down")


def kernel(x, ffn1_norm, ffn1_w_gate, ffn1_w_up, ffn1_w_down, mix_norm, w_in, b_forget, pool_w, pool_scale, q_norm, k_norm, out_norm_pool, out_norm_attn, w_out, ffn2_norm, ffn2_w_gate, ffn2_w_up, ffn2_w_down, loss_target, m_ffn1_norm, m_ffn1_w_gate, m_ffn1_w_up, m_ffn1_w_down, m_mix_norm, m_w_in, m_b_forget, m_pool_w, m_pool_scale, m_q_norm, m_k_norm, m_out_norm_pool, m_out_norm_attn, m_w_out, m_ffn2_norm, m_ffn2_w_gate, m_ffn2_w_up, m_ffn2_w_down, v_ffn1_norm, v_ffn1_w_gate, v_ffn1_w_up, v_ffn1_w_down, v_mix_norm, v_w_in, v_b_forget, v_pool_w, v_pool_scale, v_q_norm, v_k_norm, v_out_norm_pool, v_out_norm_attn, v_w_out, v_ffn2_norm, v_ffn2_w_gate, v_ffn2_w_up, v_ffn2_w_down):
    given = dict(locals())
    w = {n: given[n] for n in WEIGHT_NAMES}
    m = {n: given["m_" + n] for n in WEIGHT_NAMES}
    v = {n: given["v_" + n] for n in WEIGHT_NAMES}
    n_batch, seq, d = x.shape
    width = pool_scale.shape[0]
    in_rows = w_in.shape[1]
    in_cols = N_CHIPS * in_rows
    in_pad = -(-in_rows // 32) * 32
    in_cols_pad = in_cols - N_HEADS + LANES

    work = lambda a, n: a.T if n in TRANSPOSED_NAMES else a
    exchanged = lambda a, n: jnp.pad(a, ((0, in_pad - in_rows), (0, 0))) if n == "w_in" else a

    mesh_x, mesh_y, mesh_c = _mesh_pos()
    ids = jnp.stack([2 * mesh_x + mesh_y, mesh_c]).astype(jnp.int32)

    row = lambda a: a.reshape(1, -1)
    g1, gm, g2, ps, onp, ona = (row(a) for a in (ffn1_norm, mix_norm, ffn2_norm, pool_scale, out_norm_pool, out_norm_attn))
    qn, kn = row(jnp.tile(q_norm, N_HEADS)), row(jnp.tile(k_norm, N_HEADS))
    bf = row(jnp.pad(b_forget, (0, LANES - N_HEADS)))
    pwb = pool_w.astype(BF16)
    xf, tgt = x.reshape(n_batch * seq, d), loss_target.reshape(n_batch * seq, d)

    def grouped(call, names, *lists):
        out = [None] * len(names)
        for idx in _same_shape_groups(lists[0]):
            res = call(*[[lst[i] for i in idx] for lst in lists], names[idx[0]])
            for i, r in zip(idx, res):
                out[i] = r
        return out

    placed = dict(zip(BIG_NAMES, grouped(lambda ws, tag: _place_cast(ws, ids, tag), BIG_NAMES,
                                         [exchanged(work(w[n], n), n) for n in BIG_NAMES])))
    landing = jnp.stack([2 * cx + cy for cx, cy in [(mesh_x, mesh_y)] + _other_chips(mesh_x, mesh_y)]).astype(jnp.int32)
    (x1, h1, a1, b1, s1), (wg1, wu1, wd1), (w_in_all, w_out_all) = _ffn_fwd_gathering(
        xf, g1, [placed[n] for n in FFN1_NAMES], landing, _plan_gather([placed[n] for n in MIX_NAMES]))
    w_in_t = jnp.pad(w_in_all[:, :in_rows].reshape(in_cols, d), ((0, in_cols_pad - in_cols), (0, 0)))
    w_out_full = w_out_all.reshape(N_CHIPS * w_out.shape[0], d)
    woa, wob = w_out_full[:width], w_out_full[width:]

    hm, pv, q, k, qh, kh, vb, f = _mix_proj(x1, gm, w_in_t, qn, kn, width, width)
    qa, ka = _forget_prefix(f, bf, qh, kh, n_batch, seq)
    yp = _pool_fwd(pv, pwb, ps, onp, n_batch, seq)
    (o, lse), (wg2, wu2, wd2) = _attn_fwd(qa, ka, vb, n_batch, seq, plan=_plan_gather_relay([placed[n] for n in FFN2_NAMES]))
    x2, ya = _mix_out(x1, yp, o, ona, woa, wob)
    (dy, h2, a2, b2, s2, lpart, dyh), _ = _ffn_fwd(x2, g2, wg2, wu2, wd2, target=tgt)

    def to_chips(gs, arrived, tags):
        return grouped(lambda g, r, tag: _add_sibling(g, r, ids, tag), tags, gs, arrived)

    def own_rows(gs, from_sibling, from_chips, tags):
        return grouped(lambda g, ra, rb, tag: _add_chips(g, ra, rb, ids, tag), tags, gs, from_sibling, from_chips)

    (dx2, da2, db2, dg2), _ = _ffn_bwd_x(dy, x2, g2, a2, b2, wg2, wu2, wd2, "ffn2_bwd_x")
    dw2, _ = _ffn_bwd_w([(da2, h2), (db2, h2), (s2, dyh)], "ffn2_bwd_w")
    (dyp, do, delta, dwoa, dwob, dona), sib2 = _mix_out_bwd(dx2, o, yp, ya, ona, woa, wob, plan=_plan_sibling_halves(dw2))
    dpv, dpw, dps, donp = _pool_bwd(pv, dyp, pwb, ps, onp, n_batch, seq)
    (dqh, dkh, dv, dfq, dfk), chips2 = _attn_bwd(qa, ka, vb, do, lse, delta, n_batch, seq,
                                                 plan=_plan_chip_exchange(to_chips(dw2, sib2, FFN2_NAMES)))
    df, dbf = _forget_bwd(dfq, dfk, f, bf, n_batch, seq)
    dx1, dx1h, dw_in_t, dgm, dqn, dkn = _mix_in_bwd(dx2, x1, gm, hm, dpv, dqh, q, dkh, k, dv, df, qn, kn, w_in_t)
    in_base = [in_rows * k // 8 * 8 for k in range(N_CHIPS)]
    d_w_in = jnp.stack([dw_in_t[b:b + in_pad] for b in in_base])
    d_w_out = jnp.concatenate([dwoa, dwob], axis=0).reshape(N_CHIPS, w_out.shape[0], d)
    dwm = [d_w_in, d_w_out]
    down = FFN1_NAMES[2:]
    dwd1, sibm = _ffn_bwd_w([(s1, dx1h)], "ffn1_bwd_w_down", plan=_plan_sibling_halves(dwm))
    (da1, db1), arrived = _ffn_bwd_a(dx1h, a1, b1, wd1, "ffn1_bwd_a",
                                     plan=_merge_plans(_plan_sibling_halves(dwd1),
                                                       _plan_chip_exchange(to_chips(dwm, sibm, MIX_NAMES))))
    sibd, chipsm = arrived[:1], arrived[1:]
    gate_up = FFN1_NAMES[:2]
    dwgu1, chipsd = _ffn_bwd_w([(da1, h1), (db1, h1)], "ffn1_bwd_w_gate_up",
                               plan=_plan_chip_exchange(to_chips(dwd1, sibd, down)))
    n_tiles = (n_batch * seq) // min(FFN_TILE, n_batch * seq)
    first = max(n_tiles // 4, 1)
    begun, sibgu = _ffn_bwd_h(dx1, xf, g1, da1, db1, wg1, wu1, "ffn1_bwd_h_first", (0, first),
                              plan=_plan_sibling_halves(dwgu1))
    (gx, dg1), chipsgu = _ffn_bwd_h(dx1, xf, g1, da1, db1, wg1, wu1, "ffn1_bwd_h_rest", (first, n_tiles), prev=begun,
                                    plan=_plan_chip_exchange(to_chips(dwgu1, sibgu, gate_up)))

    part = dict(ffn1_norm=dg1, mix_norm=dgm, ffn2_norm=dg2, b_forget=dbf, pool_scale=dps, out_norm_pool=donp,
                out_norm_attn=dona, qn=dqn, kn=dkn, pool_w=dpw.reshape(n_batch, -1, pool_w.shape[-1]), loss=lpart)
    mine = (own_rows(dwgu1, sibgu, chipsgu, gate_up) + own_rows(dwd1, sibd, chipsd, down)
            + own_rows(dwm, sibm, chipsm, MIX_NAMES) + own_rows(dw2, sib2, chips2, FFN2_NAMES))
    last = _run_plan(_merge_plans(_plan_sibling_share(mine), _plan_all_to_all(_small_pack(part, d, width))), "last_exchange")
    vstack, pstack = last[len(mine):]
    g_vec, g_pw = _small_sum(vstack, pstack, jnp.reshape(4 * mesh_x + 2 * mesh_y + mesh_c, (1,)).astype(jnp.int32))
    loss = g_vec[5, 0]
    reduced = dict(zip(FFN1_NAMES + MIX_NAMES + FFN2_NAMES, last[:len(mine)]))
    reduced["w_in"] = lax.dynamic_slice(reduced["w_in"], ((in_rows * ids[0]) % 8, 0), (in_rows, d))

    grads, delta, new_m, new_v = {}, {}, {}, {}
    for names in (FFN2_NAMES, FFN1_NAMES, ("w_in",), ("w_out",)):
        stepped = _adamw([work(w[n], n) for n in names], [reduced[n] for n in names], [work(m[n], n) for n in names],
                         [work(v[n], n) for n in names], names[0])
        for n, step in zip(names, stepped):
            grads[n], delta[n], new_m[n], new_v[n] = (work(a, n) for a in step)
    flat_pw = lambda a: a.reshape(-1, a.shape[-1])
    (_, d_pw, m_pw, v_pw), = _adamw([flat_pw(pool_w)], [g_pw], [flat_pw(m_pool_w)], [flat_pw(v_pool_w)], "pool_w")
    (_, d_vec, m_vec, v_vec), = _adamw([_pack_vec(w, d, width)], [g_vec], [_pack_vec(m, d, width)],
                                       [_pack_vec(v, d, width)], "vectors")
    grads.update(_unpack_vec(g_vec, width), pool_w=g_pw.reshape(pool_w.shape))
    delta.update(_unpack_vec(d_vec, width), pool_w=d_pw.reshape(pool_w.shape))
    new_m.update(_unpack_vec(m_vec, width), pool_w=m_pw.reshape(pool_w.shape))
    new_v.update(_unpack_vec(v_vec, width), pool_w=v_pw.reshape(pool_w.shape))
    return (loss, gx.reshape(x.shape), *[grads[n] for n in WEIGHT_NAMES], *[delta[n] for n in WEIGHT_NAMES],
            *[new_m[n] for n in WEIGHT_NAMES], *[new_v[n] for n in WEIGHT_NAMES])
```

```python
import functools

import jax
import jax.numpy as jnp
from jax import lax
from jax.experimental import pallas as pl
from jax.experimental.pallas import tpu as pltpu

F32 = jnp.float32
BF16 = jnp.bfloat16
EPS = 1e-6
NEG = -1e30
ADAM_LR = 0.001
ADAM_B1 = 0.9
ADAM_B2 = 0.999
ADAM_EPS = 1e-08
ADAM_WD = 0.01
ADAM_STEP = 10
POOL_WINDOWS = (2, 4, 8, 16)
HEAD_DIM = 64
N_HEADS = 8
LANES = 128
N_CHIPS = 4
ATT_BLOCK = 512
ATT_SUB = 128
FFN_TILE = 1024
FFN_STAGED_TILE = 512
VMEM_LIMIT = 62 * 1024 * 1024
ANY = pl.BlockSpec(memory_space=pl.ANY)
VM = pl.BlockSpec(memory_space=pltpu.VMEM)


def _params(**kw):
    return pltpu.CompilerParams(vmem_limit_bytes=VMEM_LIMIT, **kw)


def _dot(a, b):
    return jnp.dot(a, b, preferred_element_type=F32)


def _dot_nt(a, b):
    return lax.dot_general(a, b, (((1,), (1,)), ((), ())), preferred_element_type=F32)


def _dot_tn(a, b):
    return lax.dot_general(a, b, (((0,), (0,)), ((), ())), preferred_element_type=F32)


def _sigmoid(z):
    return 1.0 / (1.0 + jnp.exp(-z))


def _rms(xf):
    return lax.rsqrt(jnp.mean(xf * xf, axis=-1, keepdims=True) + EPS)


def _rms_bwd(xf, r, gain, dh):
    xh = xf * r
    dyg = dh * gain
    return r * (dyg - xh * jnp.mean(dyg * xh, axis=-1, keepdims=True)), dh * xh


def _total(v):
    return jnp.sum(jnp.sum(v, axis=1, keepdims=True), axis=0, keepdims=True)


def _ffn_fwd(x, gain, wg, wu, wd, target=None, plan=None):
    t, d = x.shape
    nch, fc, _ = wg.shape
    tm = min(FFN_TILE, t)
    nt = t // tm
    with_loss = target is not None

    def body(*refs):
        if with_loss:
            x_ref, g_ref, wg_ref, wu_ref, wd_ref, t_ref, o_ref, h_ref, a_ref, b_ref, s_ref, l_ref, oh_ref, acc_ref = refs
        else:
            x_ref, g_ref, wg_ref, wu_ref, wd_ref, o_ref, h_ref, a_ref, b_ref, s_ref, acc_ref = refs
        k = pl.program_id(1)

        @pl.when(k == 0)
        def _():
            xf = x_ref[...]
            h_ref[...] = ((xf * _rms(xf)) * g_ref[...]).astype(BF16)
            acc_ref[...] = jnp.zeros_like(acc_ref)

        for rows in _row_halves(tm):
            h = h_ref[rows, :]
            a = _dot_nt(h, wg_ref[...])
            b = _dot_nt(h, wu_ref[...])
            sb = ((a * (0.5 * jnp.tanh(0.5 * a) + 0.5)) * b).astype(BF16)
            a_ref[rows, :] = a.astype(BF16)
            b_ref[rows, :] = b.astype(BF16)
            s_ref[rows, :] = sb
            acc_ref[rows, :] += _dot(sb, wd_ref[...])

        @pl.when(k == nch - 1)
        def _():
            y = x_ref[...] + 0.5 * acc_ref[...]
            if with_loss:
                e = y - t_ref[...]
                o_ref[...] = e * (1.0 / d)
                oh_ref[...] = (e * (0.5 / d)).astype(BF16)
                l_ref[...] = jnp.broadcast_to(_total(e * e) * (0.5 / d), l_ref.shape)
            else:
                o_ref[...] = y

    row = pl.BlockSpec((tm, d), lambda i, k: (i, 0))
    chunk = pl.BlockSpec((None, fc, d), lambda i, k: (k, 0, 0))
    act = pl.BlockSpec((None, tm, fc), lambda i, k: (k, i, 0))
    in_specs = [row, pl.BlockSpec((1, d), lambda i, k: (0, 0)), chunk, chunk, chunk]
    out_shape = [jax.ShapeDtypeStruct((t, d), F32), jax.ShapeDtypeStruct((t, d), BF16)]
    out_shape += [jax.ShapeDtypeStruct((nch, t, fc), BF16)] * 3
    out_specs = [row, row, act, act, act]
    args = [x, gain, wg, wu, wd]
    if with_loss:
        in_specs.append(row)
        args.append(target)
        out_shape += [jax.ShapeDtypeStruct((nt, 8, LANES), F32), jax.ShapeDtypeStruct((t, d), BF16)]
        out_specs += [pl.BlockSpec((None, 8, LANES), lambda i, k: (i, 0, 0)), row]
    return _pallas(body, name="ffn_fwd_loss" if with_loss else "ffn_fwd", args=args, in_specs=in_specs,
                   out_shape=out_shape, out_specs=out_specs, grid=(nt, nch),
                   scratch_shapes=[pltpu.VMEM((tm, d), F32)], plan=plan)


def _row_halves(n):
    return [slice(0, n // 2), slice(n // 2, n)]


def _swiglu_grads(dyh, a_ref, b_ref, wd_ref, rows):
    ds = _dot_nt(dyh, wd_ref[...])
    av = a_ref[rows, :].astype(F32)
    bv = b_ref[rows, :].astype(F32)
    th = jnp.tanh(0.5 * av)
    sig = 0.5 * th + 0.5
    dab = ((ds * bv) * (sig * (1.0 + av * (0.5 - 0.5 * th)))).astype(BF16)
    return dab, (ds * (av * sig)).astype(BF16)


def _ffn_bwd_a(dyh, a, b, wd, name, plan=None):
    t, d = dyh.shape
    nch, fc, _ = wd.shape
    tm = min(FFN_TILE, t)

    def body(dyh_ref, a_ref, b_ref, wd_ref, da_ref, db_ref):
        for rows in _row_halves(tm):
            da_ref[rows, :], db_ref[rows, :] = _swiglu_grads(dyh_ref[rows, :], a_ref, b_ref, wd_ref, rows)

    act = pl.BlockSpec((None, tm, fc), lambda i, k: (k, i, 0))
    return _pallas(
        body, name=name, args=[dyh, a, b, wd], out_shape=[jax.ShapeDtypeStruct((nch, t, fc), BF16)] * 2, grid=(t // tm, nch),
        in_specs=[pl.BlockSpec((tm, d), lambda i, k: (i, 0)), act, act, pl.BlockSpec((None, fc, d), lambda i, k: (k, 0, 0))],
        out_specs=[act, act], plan=plan)


def _ffn_bwd_h(dy, x, gain, da, db, wg, wu, name, tiles, prev=None, plan=None):
    t, d = x.shape
    nch, fc, _ = wg.shape
    tm = min(FFN_TILE, t)
    nt = t // tm
    t0, t1 = tiles

    def body(*refs):
        dy_ref, x_ref, g_ref, da_ref, db_ref, wg_ref, wu_ref = refs[:7]
        dx_ref, dg_ref, acc_ref = refs[-3:]
        k = pl.program_id(1)

        @pl.when(k == 0)
        def _():
            acc_ref[...] = jnp.zeros_like(acc_ref)

        acc_ref[...] += _dot(da_ref[...], wg_ref[...]) + _dot(db_ref[...], wu_ref[...])

        @pl.when(k == nch - 1)
        def _():
            xf = x_ref[...]
            dxn, dgr = _rms_bwd(xf, _rms(xf), g_ref[...], acc_ref[...])
            dx_ref[...] = dy_ref[...] + dxn
            dg_ref[...] = jnp.sum(dgr, axis=0, keepdims=True)

    row = pl.BlockSpec((tm, d), lambda i, k: (i + t0, 0))
    chunk = pl.BlockSpec((None, fc, d), lambda i, k: (k, 0, 0))
    act = pl.BlockSpec((None, tm, fc), lambda i, k: (k, i + t0, 0))
    args = [dy, x, gain, da, db, wg, wu]
    in_specs = [row, row, pl.BlockSpec((1, d), lambda i, k: (0, 0)), act, act, chunk, chunk]
    aliases = {}
    if prev is not None:
        aliases = {len(args): 0, len(args) + 1: 1}
        args += list(prev)
        in_specs += [ANY, ANY]
    return _pallas(
        body, name=name, args=args, out_shape=[jax.ShapeDtypeStruct((t, d), F32), jax.ShapeDtypeStruct((nt, 1, d), F32)],
        grid=(t1 - t0, nch), in_specs=in_specs,
        out_specs=[row, pl.BlockSpec((None, 1, d), lambda i, k: (i + t0, 0, 0))],
        scratch_shapes=[pltpu.VMEM((tm, d), F32)], plan=plan, aliases=aliases)


def _ffn_bwd_x(dy, x, gain, a, b, wg, wu, wd, name, plan=None):
    t, d = x.shape
    nch, fc, _ = wg.shape
    tm = min(FFN_TILE, t)
    nt = t // tm

    def body(dy_ref, x_ref, g_ref, a_ref, b_ref, wg_ref, wu_ref, wd_ref, dx_ref, da_ref, db_ref, dg_ref, acc_ref):
        k = pl.program_id(1)

        @pl.when(k == 0)
        def _():
            acc_ref[...] = jnp.zeros_like(acc_ref)

        for rows in _row_halves(tm):
            dab, dbb = _swiglu_grads((0.5 * dy_ref[rows, :]).astype(BF16), a_ref, b_ref, wd_ref, rows)
            da_ref[rows, :] = dab
            db_ref[rows, :] = dbb
            acc_ref[rows, :] += _dot(dab, wg_ref[...]) + _dot(dbb, wu_ref[...])

        @pl.when(k == nch - 1)
        def _():
            xf = x_ref[...]
            dxn, dgr = _rms_bwd(xf, _rms(xf), g_ref[...], acc_ref[...])
            dx_ref[...] = dy_ref[...] + dxn
            dg_ref[...] = jnp.sum(dgr, axis=0, keepdims=True)

    row = pl.BlockSpec((tm, d), lambda i, k: (i, 0))
    chunk = pl.BlockSpec((None, fc, d), lambda i, k: (k, 0, 0))
    act = pl.BlockSpec((None, tm, fc), lambda i, k: (k, i, 0))
    return _pallas(
        body, name=name, args=[dy, x, gain, a, b, wg, wu, wd],
        out_shape=[jax.ShapeDtypeStruct((t, d), F32), jax.ShapeDtypeStruct((nch, t, fc), BF16),
                   jax.ShapeDtypeStruct((nch, t, fc), BF16), jax.ShapeDtypeStruct((nt, 1, d), F32)],
        grid=(nt, nch),
        in_specs=[row, row, pl.BlockSpec((1, d), lambda i, k: (0, 0)), act, act, chunk, chunk, chunk],
        out_specs=[row, act, act, pl.BlockSpec((None, 1, d), lambda i, k: (i, 0, 0))],
        scratch_shapes=[pltpu.VMEM((tm, d), F32)], plan=plan)


def _ffn_bwd_w(pairs, name, plan=None):
    n = len(pairs)
    nch, t, fc = pairs[0][0].shape
    d = pairs[0][1].shape[1]
    tm = min(FFN_TILE, t)

    def body(*refs):
        @pl.when(pl.program_id(1) == 0)
        def _():
            for o_ref in refs[2 * n:]:
                o_ref[...] = jnp.zeros_like(o_ref)

        for j in range(n):
            refs[2 * n + j][...] += _dot_tn(refs[j][...], refs[n + j][...])

    row = pl.BlockSpec((tm, d), lambda k, i: (i, 0))
    act = pl.BlockSpec((None, tm, fc), lambda k, i: (k, i, 0))
    chunk = pl.BlockSpec((None, fc, d), lambda k, i: (k, 0, 0))
    return _pallas(body, name=name, args=[p[0] for p in pairs] + [p[1] for p in pairs],
                   out_shape=[jax.ShapeDtypeStruct((nch, fc, d), F32)] * n, grid=(nch, t // tm),
                   in_specs=[act] * n + [row] * n, out_specs=[chunk] * n, plan=plan)


def _head_masks():
    lane = lax.broadcasted_iota(jnp.int32, (1, LANES), 1)
    return lane < HEAD_DIM


def _head_rms(x, lo):
    x2 = x * x
    s0 = jnp.sum(jnp.where(lo, x2, 0.0), axis=1, keepdims=True)
    s1 = jnp.sum(jnp.where(lo, 0.0, x2), axis=1, keepdims=True)
    return jnp.where(lo, lax.rsqrt(s0 * (1.0 / HEAD_DIM) + EPS), lax.rsqrt(s1 * (1.0 / HEAD_DIM) + EPS))


def _head_mean(v, lo):
    s0 = jnp.sum(jnp.where(lo, v, 0.0), axis=1, keepdims=True)
    s1 = jnp.sum(jnp.where(lo, 0.0, v), axis=1, keepdims=True)
    return jnp.where(lo, s0, s1) * (1.0 / HEAD_DIM)


def _mix_proj(x1, gain, wt, qn, kn, pool_width, attn_width):
    t, d = x1.shape
    tm = min(512, t)
    nt = t // tm
    scale = HEAD_DIM ** -0.5
    c_q, c_k, c_v = pool_width, pool_width + attn_width, pool_width + 2 * attn_width
    c_f = c_v + attn_width

    def body(x_ref, g_ref, wt_ref, qn_ref, kn_ref, hm_ref, pv_ref, q_ref, k_ref, qh_ref, kh_ref, vb_ref, f_ref):
        lo = _head_masks()
        for rows in _row_halves(tm):
            xf = x_ref[rows, :]
            hm = ((xf * _rms(xf)) * g_ref[...]).astype(BF16)
            hm_ref[rows, :] = hm
            f_ref[rows, :] = _dot_nt(hm, wt_ref[c_f:c_f + LANES, :])
            pv_ref[rows, :] = _dot_nt(hm, wt_ref[0:pool_width, :])
            vb_ref[rows, :] = _dot_nt(hm, wt_ref[c_v:c_v + attn_width, :]).astype(BF16)
            for c0, raw_ref, hat_ref, n_ref, mul in ((c_q, q_ref, qh_ref, qn_ref, scale), (c_k, k_ref, kh_ref, kn_ref, 1.0)):
                raw = _dot_nt(hm, wt_ref[c0:c0 + attn_width, :])
                raw_ref[rows, :] = raw
                for blk in range(attn_width // LANES):
                    sl = slice(blk * LANES, (blk + 1) * LANES)
                    xb = raw[:, sl]
                    hat_ref[rows, sl] = (((xb * _head_rms(xb, lo)) * n_ref[:, sl]) * mul).astype(BF16)

    row = pl.BlockSpec((tm, d), lambda i: (i, 0))
    half = pl.BlockSpec((tm, attn_width), lambda i: (i, 0))
    const = lambda shape: pl.BlockSpec(shape, lambda i: (0, 0))
    return _pallas(
        body, name="mix_proj", args=[x1, gain, wt, qn, kn],
        out_shape=[jax.ShapeDtypeStruct((t, d), BF16), jax.ShapeDtypeStruct((t, pool_width), F32),
                   jax.ShapeDtypeStruct((t, attn_width), F32), jax.ShapeDtypeStruct((t, attn_width), F32),
                   jax.ShapeDtypeStruct((t, attn_width), BF16), jax.ShapeDtypeStruct((t, attn_width), BF16),
                   jax.ShapeDtypeStruct((t, attn_width), BF16), jax.ShapeDtypeStruct((t, LANES), F32)],
        grid=(nt,),
        in_specs=[row, const((1, d)), const(wt.shape), const((1, attn_width)), const((1, attn_width))],
        out_specs=[row, pl.BlockSpec((tm, pool_width), lambda i: (i, 0)), half, half, half, half, half,
                   pl.BlockSpec((tm, LANES), lambda i: (i, 0))])[0]


def _shift_down(v, dist, row):
    return jnp.where(row >= dist, pltpu.roll(v, dist, 0), 0.0)


def _shift_up(v, dist, row, n):
    return jnp.where(row + dist < n, pltpu.roll(v, n - dist, 0), 0.0)


def _aug_lane(e):
    return HEAD_DIM if e == 0 else 0


def _forget_prefix(f, bias, qh, kh, n_batch, seq):
    def body(f_ref, b_ref, q_ref, k_ref, qa_ref, ka_ref):
        z = f_ref[...] + b_ref[...]
        acc = jnp.minimum(z, 0.0) - jnp.log(1.0 + jnp.exp(-jnp.abs(z)))
        row = lax.broadcasted_iota(jnp.int32, (seq, 1), 0)
        dist = 1
        while dist < seq:
            acc = acc + _shift_down(acc, dist, row)
            dist *= 2
        lane = lax.broadcasted_iota(jnp.int32, (1, LANES), 1)
        for h in range(N_HEADS):
            pair, e = divmod(h, 2)
            a0 = _aug_lane(e)
            own = (lane < HEAD_DIM) if e == 0 else (lane >= HEAD_DIM)
            fh = _pick_lane(acc, h)
            hi = fh.astype(BF16).astype(F32)
            rest = fh - hi
            mid = rest.astype(BF16).astype(F32)
            low = rest - mid
            q_ones = (lane >= a0 + 3) & (lane < a0 + 6)
            k_ones = (lane >= a0) & (lane < a0 + 3)
            q_aug = jnp.where(lane == a0, hi, jnp.where(lane == a0 + 1, mid, jnp.where(lane == a0 + 2, low,
                              jnp.where(q_ones, 1.0, 0.0))))
            k_aug = jnp.where(k_ones, 1.0, jnp.where(lane == a0 + 3, -hi, jnp.where(lane == a0 + 4, -mid,
                              jnp.where(lane == a0 + 5, -low, 0.0))))
            src = slice(pair * LANES, (pair + 1) * LANES)
            dst = slice(h * LANES, (h + 1) * LANES)
            qa_ref[:, dst] = jnp.where(own, q_ref[:, src].astype(F32), q_aug).astype(BF16)
            ka_ref[:, dst] = jnp.where(own, k_ref[:, src].astype(F32), k_aug).astype(BF16)

    width = qh.shape[1]
    tok = pl.BlockSpec((seq, width), lambda b: (b, 0))
    aug = pl.BlockSpec((seq, N_HEADS * LANES), lambda b: (b, 0))
    return pl.pallas_call(
        body, out_shape=[jax.ShapeDtypeStruct((n_batch * seq, N_HEADS * LANES), BF16)] * 2, grid=(n_batch,),
        in_specs=[pl.BlockSpec((seq, LANES), lambda b: (b, 0)), pl.BlockSpec((1, LANES), lambda b: (0, 0)), tok, tok],
        out_specs=[aug, aug], compiler_params=_params(), name="forget_prefix",
    )(f, bias, qh, kh)


def _pool_groups(pv_ref, pw_ref, ps_ref, seq):
    row = lax.broadcasted_iota(jnp.int32, (seq, 1), 0)
    pos = (row + 1).astype(F32)
    out = []
    for g, win in enumerate(POOL_WINDOWS):
        sl = slice(g * LANES, (g + 1) * LANES)
        xg = pv_ref[:, sl]
        acc = xg
        dist = 1
        while dist < win:
            acc = acc + _shift_down(acc, dist, row)
            dist *= 2
        pooled = (acc / jnp.minimum(pos, float(win)) - xg).astype(BF16)
        mixed = _dot(pooled, pw_ref[g])
        out.append((pooled, mixed, mixed * ps_ref[:, sl]))
    return out


def _pool_fwd(pv, pw, ps, onp, n_batch, seq):
    width = pv.shape[1]

    def body(pv_ref, pw_ref, ps_ref, on_ref, y_ref):
        groups = _pool_groups(pv_ref, pw_ref, ps_ref, seq)
        ssq = sum(jnp.sum(ms * ms, axis=1, keepdims=True) for _, _, ms in groups)
        r = lax.rsqrt(ssq * (1.0 / width) + EPS)
        for g, (_, _, ms) in enumerate(groups):
            sl = slice(g * LANES, (g + 1) * LANES)
            y_ref[:, sl] = ((ms * r) * on_ref[:, sl]).astype(BF16)

    return pl.pallas_call(
        body, out_shape=jax.ShapeDtypeStruct((n_batch * seq, width), BF16), grid=(n_batch,),
        in_specs=[pl.BlockSpec((seq, width), lambda b: (b, 0)), pl.BlockSpec(pw.shape, lambda b: (0, 0, 0)),
                  pl.BlockSpec((1, width), lambda b: (0, 0)), pl.BlockSpec((1, width), lambda b: (0, 0))],
        out_specs=pl.BlockSpec((seq, width), lambda b: (b, 0)),
        compiler_params=_params(), name="pool_fwd",
    )(pv, pw, ps, onp)


def _pool_bwd(pv, dyp, pw, ps, onp, n_batch, seq):
    width = pv.shape[1]

    def body(pv_ref, dy_ref, pw_ref, ps_ref, on_ref, dpv_ref, dpw_ref, dps_ref, don_ref):
        groups = _pool_groups(pv_ref, pw_ref, ps_ref, seq)
        ssq = sum(jnp.sum(ms * ms, axis=1, keepdims=True) for _, _, ms in groups)
        r = lax.rsqrt(ssq * (1.0 / width) + EPS)
        mean = sum(jnp.sum((dy_ref[:, g * LANES:(g + 1) * LANES] * on_ref[:, g * LANES:(g + 1) * LANES]) * (ms * r),
                           axis=1, keepdims=True) for g, (_, _, ms) in enumerate(groups)) * (1.0 / width)
        row = lax.broadcasted_iota(jnp.int32, (seq, 1), 0)
        pos = (row + 1).astype(F32)
        for g, (pooled, mixed, ms) in enumerate(groups):
            sl = slice(g * LANES, (g + 1) * LANES)
            dy = dy_ref[:, sl]
            xh = ms * r
            don_ref[:, sl] = jnp.sum(dy * xh, axis=0, keepdims=True)
            dms = r * (dy * on_ref[:, sl] - xh * mean)
            dps_ref[:, sl] = jnp.sum(dms * mixed, axis=0, keepdims=True)
            dmix = (dms * ps_ref[:, sl]).astype(BF16)
            dpw_ref[g] = _dot_tn(pooled, dmix)
            dpool = _dot_nt(dmix, pw_ref[g])
            win = POOL_WINDOWS[g]
            acc = dpool / jnp.minimum(pos, float(win))
            dist = 1
            while dist < win:
                acc = acc + _shift_up(acc, dist, row, seq)
                dist *= 2
            dpv_ref[:, sl] = (acc - dpool).astype(BF16)

    tok = pl.BlockSpec((seq, width), lambda b: (b, 0))
    vec = pl.BlockSpec((1, width), lambda b: (0, 0))
    pvec = pl.BlockSpec((None, 1, width), lambda b: (b, 0, 0))
    return pl.pallas_call(
        body,
        out_shape=[jax.ShapeDtypeStruct((n_batch * seq, width), BF16),
                   jax.ShapeDtypeStruct((n_batch,) + pw.shape, F32),
                   jax.ShapeDtypeStruct((n_batch, 1, width), F32), jax.ShapeDtypeStruct((n_batch, 1, width), F32)],
        grid=(n_batch,),
        in_specs=[tok, tok, pl.BlockSpec(pw.shape, lambda b: (0, 0, 0)), vec, vec],
        out_specs=[tok, pl.BlockSpec((None,) + pw.shape, lambda b: (b, 0, 0, 0)), pvec, pvec],
        compiler_params=_params(), name="pool_bwd",
    )(pv, dyp, pw, ps, onp)


def _pick_lane(tile, idx):
    lane = lax.broadcasted_iota(jnp.int32, (1, LANES), 1)
    return jnp.sum(jnp.where(lane == idx, tile, 0.0), axis=1, keepdims=True)


def _pick_row(tile, idx):
    sub = lax.broadcasted_iota(jnp.int32, (tile.shape[0], 1), 0)
    return jnp.sum(jnp.where(sub == idx, tile, 0.0), axis=0, keepdims=True)


def _put_lane(col, idx):
    lane = lax.broadcasted_iota(jnp.int32, (1, LANES), 1)
    return jnp.where(lane == idx, col, 0.0)


def _head_select(e):
    lo = _head_masks()
    return lo if e == 0 else jnp.logical_not(lo)


def _causal(st, shift):
    row = lax.broadcasted_iota(jnp.int32, st.shape, 0)
    col = lax.broadcasted_iota(jnp.int32, st.shape, 1) + shift
    return jnp.where(col >= row, st, NEG)


def _transpose_blocks(a):
    rows, cols = a.shape
    return jnp.concatenate(
        [jnp.concatenate([a[r:r + LANES, c:c + LANES].T for r in range(0, rows, LANES)], axis=1)
         for c in range(0, cols, LANES)], axis=0)


def _accumulate(ref, value, first):
    @pl.when(first)
    def _():
        ref[...] = value

    @pl.when(jnp.logical_not(first))
    def _():
        ref[...] += value


def _attn_fwd(qa, ka, vb, n_batch, seq, plan=None):
    tq = min(ATT_BLOCK, seq)
    nq, nsub, tk = seq // tq, tq // ATT_SUB, tq
    pairs = vb.shape[1] // LANES

    def body(q_ref, k_ref, v_ref, o_ref, lse_ref, acc_ref):
        i, p = pl.program_id(1), pl.program_id(2)
        row_lo = lax.broadcasted_iota(jnp.int32, (LANES, 1), 0) < HEAD_DIM
        qs = [q_ref[:, e * LANES:(e + 1) * LANES] for e in range(2)]
        acc_ref[...] = jnp.zeros_like(acc_ref)

        def tile(off, stats, diagonal):
            vj = v_ref[pl.ds(off, tk), :]
            new, alphas, pvs = [], [], []
            for e in range(2):
                st = _dot_nt(k_ref[pl.ds(off, tk), e * LANES:(e + 1) * LANES], qs[e])
                if diagonal:
                    st = _causal(st, 0)
                m, l = stats[e]
                m_new = jnp.maximum(m, jnp.max(st, axis=0, keepdims=True))
                alpha = jnp.exp(m - m_new)
                pt = jnp.exp(st - m_new)
                new.append((m_new, alpha * l + jnp.sum(pt, axis=0, keepdims=True)))
                alphas.append(alpha)
                pvs.append(_dot_tn(jnp.where(_head_select(e), vj, jnp.zeros_like(vj)), pt.astype(BF16)))
            acc_ref[...] = acc_ref[...] * jnp.where(row_lo, alphas[0], alphas[1]) + (pvs[0] + pvs[1])
            return tuple(new)

        init = ((jnp.full((1, tq), NEG, F32), jnp.zeros((1, tq), F32)),) * 2
        stats = lax.fori_loop(0, i, lambda j, st: tile(pl.multiple_of(j * tk, tk), st, False), init)
        (m0, l0), (m1, l1) = tile(pl.multiple_of(i * tk, tk), stats, True)
        out_t = acc_ref[...] / jnp.where(row_lo, l0, l1)
        sub = lax.broadcasted_iota(jnp.int32, (8, 1), 0)
        lse0, lse1 = m0 + jnp.log(l0), m1 + jnp.log(l1)
        for a in range(nsub):
            sl = slice(a * ATT_SUB, (a + 1) * ATT_SUB)
            o_ref[sl, :] = out_t[:, sl].T
            rows = jnp.where(sub == 2 * p, lse0[:, sl], 0.0) + jnp.where(sub == 2 * p + 1, lse1[:, sl], 0.0)
            _accumulate(lse_ref.at[a], rows, p == 0)

    return _pallas(
        body, name="attn_fwd", args=[qa, ka, vb],
        out_shape=[jax.ShapeDtypeStruct((n_batch * seq, pairs * LANES), F32),
                   jax.ShapeDtypeStruct((n_batch * seq // ATT_SUB, 8, ATT_SUB), F32)],
        grid=(n_batch, nq, pairs),
        in_specs=[pl.BlockSpec((tq, 2 * LANES), lambda b, i, p: (b * nq + i, p)),
                  pl.BlockSpec((seq, 2 * LANES), lambda b, i, p: (b, p)),
                  pl.BlockSpec((seq, LANES), lambda b, i, p: (b, p))],
        out_specs=[pl.BlockSpec((tq, LANES), lambda b, i, p: (b * nq + i, p)),
                   pl.BlockSpec((nsub, 8, ATT_SUB), lambda b, i, p: (b * nq + i, 0, 0))],
        scratch_shapes=[pltpu.VMEM((LANES, tq), F32)], plan=plan)


def _attn_bwd(qa, ka, vb, do, lse, delta, n_batch, seq, plan=None):
    tq = min(ATT_BLOCK, seq)
    nq, nsub = seq // tq, tq // ATT_SUB
    n_tiles = seq // ATT_SUB
    pairs = vb.shape[1] // LANES

    def body(q_ref, k_ref, v_ref, do_ref, lse_ref, dl_ref, dq_ref, dk_ref, dv_ref, dfq_ref, dfk_ref,
             dq0_ref, dq1_ref, dk0_ref, dk1_ref, dva_ref):
        p = pl.program_id(1)
        dqs, dks = (dq0_ref, dq1_ref), (dk0_ref, dk1_ref)
        for acc in (dk0_ref, dk1_ref, dva_ref):
            acc[...] = jnp.zeros_like(acc)
        dfq_cols = []
        for i in range(nq):
            rows_i = slice(i * tq, (i + 1) * tq)
            qs = [q_ref[rows_i, e * LANES:(e + 1) * LANES] for e in range(2)]
            dov = do_ref[rows_i, :]
            does = [jnp.where(_head_select(e), dov, jnp.zeros_like(dov)) for e in range(2)]
            stat = lambda ref, e: jnp.concatenate([_pick_row(ref[i * nsub + a], 2 * p + e) for a in range(nsub)], axis=1)
            ls, dl = [stat(lse_ref, e) for e in range(2)], [stat(dl_ref, e) for e in range(2)]
            for acc in dqs:
                acc[...] = jnp.zeros_like(acc)

            def tile(off, diagonal, qs=qs, dov=dov, does=does, ls=ls, dl=dl):
                vj = v_ref[pl.ds(off, tq), :]
                for e in range(2):
                    kj = k_ref[pl.ds(off, tq), e * LANES:(e + 1) * LANES]
                    st = _dot_nt(kj, qs[e])
                    if diagonal:
                        st = _causal(st, 0)
                    pt = jnp.exp(st - ls[e])
                    dva_ref[pl.ds(off, tq), :] += _dot(pt.astype(BF16), does[e])
                    dpt = _dot_nt(jnp.where(_head_select(e), vj, jnp.zeros_like(vj)), dov)
                    dst = (pt * (dpt - dl[e])).astype(BF16)
                    dks[e][pl.ds(off, tq), :] += _dot(dst, qs[e])
                    dqs[e][...] += _dot(_transpose_blocks(kj), dst)

            def step(j, carry, tile=tile):
                tile(pl.multiple_of(j * tq, tq), False)
                return carry

            lax.fori_loop(0, i, step, 0)
            tile(i * tq, True)
            dq0, dq1 = _transpose_blocks(dq0_ref[...]), _transpose_blocks(dq1_ref[...])
            dq_ref[rows_i, :] = jnp.where(_head_masks(), dq0, dq1)
            dfq_cols.append(_put_lane(_pick_lane(dq0, _aug_lane(0)), 2 * p) + _put_lane(_pick_lane(dq1, _aug_lane(1)), 2 * p + 1))
        dk0, dk1 = dk0_ref[...], dk1_ref[...]
        dk_ref[...] = jnp.where(_head_masks(), dk0, dk1)
        dv_ref[...] = dva_ref[...].astype(BF16)
        dfk = _put_lane(_pick_lane(dk0, _aug_lane(0) + 3), 2 * p) + _put_lane(_pick_lane(dk1, _aug_lane(1) + 3), 2 * p + 1)
        _accumulate(dfq_ref, jnp.concatenate(dfq_cols, axis=0), p == 0)
        _accumulate(dfk_ref, -dfk, p == 0)

    wide = pl.BlockSpec((seq, 2 * LANES), lambda b, p: (b, p))
    blk = pl.BlockSpec((seq, LANES), lambda b, p: (b, p))
    col = pl.BlockSpec((seq, LANES), lambda b, p: (b, 0))
    stat = pl.BlockSpec((n_tiles, 8, ATT_SUB), lambda b, p: (b, 0, 0))
    f32_blk, acc = jax.ShapeDtypeStruct((n_batch * seq, pairs * LANES), F32), pltpu.VMEM((seq, LANES), F32)
    return _pallas(
        body, name="attn_bwd", args=[qa, ka, vb, do, lse, delta],
        out_shape=[f32_blk, f32_blk, jax.ShapeDtypeStruct((n_batch * seq, pairs * LANES), BF16),
                   jax.ShapeDtypeStruct((n_batch * seq, LANES), F32), jax.ShapeDtypeStruct((n_batch * seq, LANES), F32)],
        grid=(n_batch, pairs), in_specs=[wide, wide, blk, blk, stat, stat], out_specs=[blk, blk, blk, col, col],
        scratch_shapes=[pltpu.VMEM((LANES, tq), F32), pltpu.VMEM((LANES, tq), F32), acc, acc, acc], plan=plan)


def _forget_bwd(dfq, dfk, f, bias, n_batch, seq):
    def body(dfq_ref, dfk_ref, f_ref, b_ref, df_ref, db_ref):
        acc = dfq_ref[...] + dfk_ref[...]
        row = lax.broadcasted_iota(jnp.int32, (seq, 1), 0)
        dist = 1
        while dist < seq:
            acc = acc + _shift_up(acc, dist, row, seq)
            dist *= 2
        df = acc * _sigmoid(-(f_ref[...] + b_ref[...]))
        df_ref[...] = df
        db_ref[...] = jnp.sum(df, axis=0, keepdims=True)

    col = pl.BlockSpec((seq, LANES), lambda b: (b, 0))
    return pl.pallas_call(
        body,
        out_shape=[jax.ShapeDtypeStruct((n_batch * seq, LANES), F32), jax.ShapeDtypeStruct((n_batch, 1, LANES), F32)],
        grid=(n_batch,), in_specs=[col, col, col, pl.BlockSpec((1, LANES), lambda b: (0, 0))],
        out_specs=[col, pl.BlockSpec((None, 1, LANES), lambda b: (b, 0, 0))],
        compiler_params=_params(), name="forget_bwd",
    )(dfq, dfk, f, bias)


def _mix_out(x1, yp, o, ona, woa, wob):
    t, d = x1.shape
    width = o.shape[1]
    tm = min(512, t)

    def body(x_ref, yp_ref, o_ref, on_ref, wa_ref, wb_ref, x2_ref, ya_ref):
        of = o_ref[...]
        ya = ((of * _rms(of)) * on_ref[...]).astype(BF16)
        ya_ref[...] = ya
        x2_ref[...] = x_ref[...] + (_dot(yp_ref[...], wa_ref[...]) + _dot(ya, wb_ref[...]))

    row = pl.BlockSpec((tm, d), lambda i: (i, 0))
    half = pl.BlockSpec((tm, width), lambda i: (i, 0))
    wspec = pl.BlockSpec((width, d), lambda i: (0, 0))
    return pl.pallas_call(
        body, out_shape=[jax.ShapeDtypeStruct((t, d), F32), jax.ShapeDtypeStruct((t, width), BF16)],
        grid=(t // tm,), in_specs=[row, half, half, pl.BlockSpec((1, width), lambda i: (0, 0)), wspec, wspec],
        out_specs=[row, half], compiler_params=_params(), name="mix_out",
    )(x1, yp, o, ona, woa, wob)


def _mix_out_bwd(dx2, o, yp, ya, ona, woa, wob, plan=None):
    t, d = dx2.shape
    width = o.shape[1]
    tm = min(512, t)
    nt = t // tm

    def body(dx_ref, o_ref, yp_ref, ya_ref, on_ref, wa_ref, wb_ref, dyp_ref, do_ref, dl_ref, dwa_ref, dwb_ref, don_ref):
        @pl.when(pl.program_id(0) == 0)
        def _():
            dwa_ref[...] = jnp.zeros_like(dwa_ref)
            dwb_ref[...] = jnp.zeros_like(dwb_ref)

        dxb = dx_ref[...].astype(BF16)
        dwa_ref[...] += _dot_tn(yp_ref[...], dxb)
        dwb_ref[...] += _dot_tn(ya_ref[...], dxb)
        dyp_ref[...] = _dot_nt(dxb, wa_ref[...])
        of = o_ref[...]
        dov, dgr = _rms_bwd(of, _rms(of), on_ref[...], _dot_nt(dxb, wb_ref[...]))
        don_ref[...] = jnp.sum(dgr, axis=0, keepdims=True)
        do_ref[...] = dov.astype(BF16)
        lo = _head_masks()
        prod = dov * of
        delta = jnp.zeros((tm, LANES), F32)
        for blk in range(width // LANES):
            pb = prod[:, blk * LANES:(blk + 1) * LANES]
            delta = delta + _put_lane(jnp.sum(jnp.where(lo, pb, 0.0), axis=1, keepdims=True), 2 * blk)
            delta = delta + _put_lane(jnp.sum(jnp.where(lo, 0.0, pb), axis=1, keepdims=True), 2 * blk + 1)
        for c in range(tm // ATT_SUB):
            dl_ref[c] = delta[c * ATT_SUB:(c + 1) * ATT_SUB, :].T[0:8, :]

    row = pl.BlockSpec((tm, d), lambda i: (i, 0))
    half = pl.BlockSpec((tm, width), lambda i: (i, 0))
    wspec = pl.BlockSpec((width, d), lambda i: (0, 0))
    return _pallas(
        body, name="mix_out_bwd", args=[dx2, o, yp, ya, ona, woa, wob],
        out_shape=[jax.ShapeDtypeStruct((t, width), F32), jax.ShapeDtypeStruct((t, width), BF16),
                   jax.ShapeDtypeStruct((t // ATT_SUB, 8, ATT_SUB), F32), jax.ShapeDtypeStruct((width, d), F32),
                   jax.ShapeDtypeStruct((width, d), F32), jax.ShapeDtypeStruct((nt, 1, width), F32)],
        grid=(nt,),
        in_specs=[row, half, half, half, pl.BlockSpec((1, width), lambda i: (0, 0)), wspec, wspec],
        out_specs=[half, half, pl.BlockSpec((tm // ATT_SUB, 8, ATT_SUB), lambda i: (i, 0, 0)), wspec, wspec,
                   pl.BlockSpec((None, 1, width), lambda i: (i, 0, 0))], plan=plan)


def _mix_in_bwd(dx2, x1, gain, hm, dpv, dqh, q, dkh, k, dv, df, qn, kn, wt):
    t, d = x1.shape
    width = q.shape[1]
    pool_width = dpv.shape[1]
    tm = min(512, t)
    nt = t // tm
    scale = HEAD_DIM ** -0.5
    c_q, c_k, c_v = pool_width, pool_width + width, pool_width + 2 * width
    c_f = c_v + width

    def body(dx2_ref, x_ref, g_ref, hm_ref, dpv_ref, dqh_ref, q_ref, dkh_ref, k_ref, dv_ref, df_ref, qn_ref, kn_ref,
             wt_ref, dx_ref, dxh_ref, dwt_ref, dg_ref, dqn_ref, dkn_ref):
        @pl.when(pl.program_id(0) == 0)
        def _():
            dwt_ref[...] = jnp.zeros_like(dwt_ref)

        lo = _head_masks()
        for part, rows in enumerate(_row_halves(tm)):
            def put(ref, sl, value):
                ref[:, sl] = value if part == 0 else ref[:, sl] + value

            hm = hm_ref[rows, :]
            pieces = [(0, dpv_ref[rows, :])]
            for c0, raw_ref, dh_ref, n_ref, dn_ref, mul in ((c_q, q_ref, dqh_ref, qn_ref, dqn_ref, scale),
                                                           (c_k, k_ref, dkh_ref, kn_ref, dkn_ref, 1.0)):
                cols = []
                for blk in range(width // LANES):
                    sl = slice(blk * LANES, (blk + 1) * LANES)
                    xb = raw_ref[rows, sl]
                    gb = dh_ref[rows, sl] * mul
                    r = _head_rms(xb, lo)
                    xh = xb * r
                    dyg = gb * n_ref[:, sl]
                    cols.append((r * (dyg - xh * _head_mean(dyg * xh, lo))).astype(BF16))
                    put(dn_ref, sl, jnp.sum(gb * xh, axis=0, keepdims=True))
                pieces.append((c0, jnp.concatenate(cols, axis=1)))
            pieces.append((c_v, dv_ref[rows, :]))
            pieces.append((c_f, df_ref[rows, :].astype(BF16)))
            dhm = jnp.zeros((tm // 2, d), F32)
            for c0, piece in pieces:
                dwt_ref[c0:c0 + piece.shape[1], :] += _dot_tn(piece, hm)
                dhm = dhm + _dot(piece, wt_ref[c0:c0 + piece.shape[1], :])
            xf = x_ref[rows, :]
            dxn, dgr = _rms_bwd(xf, _rms(xf), g_ref[...], dhm)
            dx = dx2_ref[rows, :] + dxn
            dx_ref[rows, :] = dx
            dxh_ref[rows, :] = (0.5 * dx).astype(BF16)
            put(dg_ref, slice(None), jnp.sum(dgr, axis=0, keepdims=True))

    row = pl.BlockSpec((tm, d), lambda i: (i, 0))
    half = pl.BlockSpec((tm, width), lambda i: (i, 0))
    const = lambda shape: pl.BlockSpec(shape, lambda i: (0, 0))
    pvec = lambda n: pl.BlockSpec((None, 1, n), lambda i: (i, 0, 0))
    return pl.pallas_call(
        body,
        out_shape=[jax.ShapeDtypeStruct((t, d), F32), jax.ShapeDtypeStruct((t, d), BF16), jax.ShapeDtypeStruct(wt.shape, F32),
                   jax.ShapeDtypeStruct((nt, 1, d), F32),
                   jax.ShapeDtypeStruct((nt, 1, width), F32), jax.ShapeDtypeStruct((nt, 1, width), F32)],
        grid=(nt,),
        in_specs=[row, row, const((1, d)), row, pl.BlockSpec((tm, pool_width), lambda i: (i, 0)), half, half, half, half,
                  half, pl.BlockSpec((tm, LANES), lambda i: (i, 0)), const((1, width)), const((1, width)),
                  const(wt.shape)],
        out_specs=[row, row, const(wt.shape), pvec(d), pvec(width), pvec(width)],
        compiler_params=_params(), name="mix_in_bwd",
    )(dx2, x1, gain, hm, dpv, dqh, q, dkh, k, dv, df, qn, kn, wt)


def _mesh_pos():
    return lax.axis_index("x"), lax.axis_index("y"), lax.axis_index("c")


def _other_chips(x, y):
    return [(1 - x, y), (x, 1 - y), (1 - x, 1 - y)]


def _remote(src, dst, send_sem, recv_sem, device):
    return pltpu.make_async_remote_copy(src_ref=src, dst_ref=dst, send_sem=send_sem, recv_sem=recv_sem,
                                        device_id=device, device_id_type=pl.DeviceIdType.MESH)


def _half_rows(n_rows, which):
    half = n_rows // 2
    return pl.ds(pl.multiple_of(which * half, 8), half)


def _row_block(rows, cols, itemsize=4):
    rb = rows
    while rb * cols * itemsize > (1 << 20) and rb % 32 == 0:
        rb //= 2
    return rb


def _place_cast(ws, chip, tag):
    n = len(ws)
    rows, cols = ws[0].shape
    rb = _row_block(rows, cols)

    def body(k_ref, *refs):
        for w_ref, o_ref in zip(refs[:n], refs[n:]):
            o_ref[...] = w_ref[...].astype(BF16)

    return pl.pallas_call(
        body, out_shape=[jax.ShapeDtypeStruct((N_CHIPS, rows, cols), BF16)] * n,
        grid_spec=pltpu.PrefetchScalarGridSpec(
            num_scalar_prefetch=1, grid=(rows // rb,),
            in_specs=[pl.BlockSpec((rb, cols), lambda i, k: (i, 0))] * n,
            out_specs=[pl.BlockSpec((None, rb, cols), lambda i, k: (k[0], i, 0))] * n),
        compiler_params=_params(), name="place_" + tag,
    )(chip, *ws)


class _Plan:
    def __init__(self, ins, outs, alias, sems, start, finish, middle=None, middle_at=(3, 4)):
        self.ins, self.outs, self.alias, self.sems = ins, outs, alias, sems
        self.start, self.middle, self.finish, self.middle_at = start, middle, finish, middle_at


def _merge_plans(a, b):
    ni, no, ns = len(a.ins), len(a.outs), len(a.sems)
    alias = dict(a.alias)
    alias.update({ni + i: no + o for i, o in b.alias.items()})

    def both(which):
        stage_a, stage_b = getattr(a, which), getattr(b, which)
        if stage_a is None and stage_b is None:
            return None

        def run(ins, outs, sems):
            if stage_a is not None:
                stage_a(ins[:ni], outs[:no], sems[:ns])
            if stage_b is not None:
                stage_b(ins[ni:], outs[no:], sems[ns:])
        return run

    return _Plan(list(a.ins) + list(b.ins), list(a.outs) + list(b.outs), alias, list(a.sems) + list(b.sems),
                 both("start"), both("finish"), both("middle"), a.middle_at if a.middle is not None else b.middle_at)


def _run_plan(plan, name):
    n_in, n_out = len(plan.ins), len(plan.outs)

    def body(*refs):
        parts = refs[:n_in], refs[n_in:n_in + n_out], refs[n_in + n_out:]
        plan.start(*parts)
        if plan.middle is not None:
            plan.middle(*parts)
        plan.finish(*parts)

    return pl.pallas_call(
        body, out_shape=plan.outs, in_specs=[ANY] * n_in, out_specs=[ANY] * n_out, scratch_shapes=plan.sems,
        input_output_aliases=plan.alias, name=name,
    )(*plan.ins)


def _pallas(body, *, name, args, in_specs, out_shape, out_specs, grid, scratch_shapes=(), plan=None, aliases=None):
    n_in, n_out, n_scr = len(args), len(out_shape), len(scratch_shapes)
    plan = plan or _Plan([], [], {}, [], None, None)
    p_in, p_out = len(plan.ins), len(plan.outs)

    def carrying(*refs):
        ins, p_ins = refs[:n_in], refs[n_in:n_in + p_in]
        o0 = n_in + p_in
        outs, p_outs = refs[o0:o0 + n_out], refs[o0 + n_out:o0 + n_out + p_out]
        s0 = o0 + n_out + p_out
        scr, p_sems = refs[s0:s0 + n_scr], refs[s0 + n_scr:]
        ids = [pl.program_id(a) for a in range(len(grid))]

        if plan.start is not None:
            @pl.when(functools.reduce(jnp.logical_and, [i == 0 for i in ids]))
            def _():
                plan.start(p_ins, p_outs, p_sems)

        body(*ins, *outs, *scr)

        if plan.middle is not None:
            step, n_steps = 0, 1
            for i, g in zip(ids, grid):
                step, n_steps = step * g + i, n_steps * g

            @pl.when(step == (plan.middle_at[0] * n_steps) // plan.middle_at[1])
            def _():
                plan.middle(p_ins, p_outs, p_sems)

        if plan.finish is not None:
            @pl.when(functools.reduce(jnp.logical_and, [i == g - 1 for i, g in zip(ids, grid)]))
            def _():
                plan.finish(p_ins, p_outs, p_sems)

    aliases = dict(aliases or {})
    aliases.update({n_in + i: n_out + o for i, o in plan.alias.items()})
    res = pl.pallas_call(
        carrying, out_shape=list(out_shape) + list(plan.outs), grid=grid,
        in_specs=list(in_specs) + [ANY] * p_in, out_specs=list(out_specs) + [ANY] * p_out,
        scratch_shapes=list(scratch_shapes) + list(plan.sems),
        input_output_aliases=aliases, compiler_params=_params(), name=name,
    )(*args, *plan.ins)
    return list(res[:n_out]), list(res[n_out:])


def _plan_gather(stacks):
    n = len(stacks)
    relations = range(3)

    def ici_copies(outs, sems):
        x, y, c = _mesh_pos()
        chips = _other_chips(x, y)
        cps = []
        for w in range(n):
            own = outs[w].at[2 * x + y, _half_rows(stacks[w].shape[1], c)]
            cps += [_remote(own, own, sems[0].at[w, j], sems[1].at[w, j], (*chips[j], c)) for j in relations]
        return cps

    def start(ins, outs, sems):
        for cp in ici_copies(outs, sems):
            cp.start()

    def forwards(outs, sems, core):
        x, y, c = _mesh_pos()
        slots = [2 * cx + cy for cx, cy in _other_chips(x, y)]
        cps = []
        for w in range(n):
            rows = _half_rows(stacks[w].shape[1], core)
            for j in relations:
                landed = outs[w].at[slots[j], rows]
                cps.append((_remote(landed, landed, sems[0].at[w, j], sems[1].at[w, j], (x, y, 1 - c)),
                            _remote(landed, landed, sems[2].at[w, j], sems[3].at[w, j], (x, y, 1 - c))))
        return cps

    def middle(ins, outs, sems):
        c = _mesh_pos()[2]
        for arrival, forward in forwards(outs, sems, c):
            arrival.wait_recv()
            forward.start()

    def finish(ins, outs, sems):
        c = _mesh_pos()[2]
        for _, forward in forwards(outs, sems, 1 - c):
            forward.wait_recv()
        for cp in ici_copies(outs, sems) + [forward for _, forward in forwards(outs, sems, c)]:
            cp.wait_send()

    return _Plan(stacks, [jax.ShapeDtypeStruct(s.shape, s.dtype) for s in stacks], {w: w for w in range(n)},
                 [pltpu.SemaphoreType.DMA((n, 3))] * 4, start, finish, middle)


RELAY_SEMS = [pltpu.SemaphoreType.DMA((3, 2))] * 4 + [pltpu.SemaphoreType.DMA((3, 3))] * 2


def _relay_gather_stage(stage, outs, sems):
    send, recv, relay_send, relay_recv, d2d_send, d2d_recv = sems
    n = len(outs)
    rh = outs[0].shape[1] // 2
    mx, my, c = _mesh_pos()
    sibling = (mx, my, 1 - c)
    near = [(1 - mx, my), (mx, 1 - my)]
    slots = [2 * cx + cy for cx, cy in near] + [2 * (1 - mx) + (1 - my)]

    def piece(w, slot, core, quarter=None):
        if quarter is None:
            return outs[w].at[slot, _half_rows(2 * rh, core)]
        return outs[w].at[slot, pl.ds(pl.multiple_of(core * rh + quarter * (rh // 2), 8), rh // 2)]

    def to_near(w, j):
        own = piece(w, 2 * mx + my, c)
        return _remote(own, own, send.at[w, j], recv.at[w, j], (*near[j], c))

    def from_near(w, j):
        landed = piece(w, slots[j], c)
        return _remote(landed, landed, send.at[w, j], recv.at[w, j], sibling)

    def onward(w, j, slot):
        part = piece(w, slot, c, quarter=j)
        return _remote(part, part, relay_send.at[w, j], relay_recv.at[w, j], (*near[1 - j], c))

    def to_sibling(w, j, core):
        landed = piece(w, slots[j], core)
        return _remote(landed, landed, d2d_send.at[w, j], d2d_recv.at[w, j], sibling)

    if stage == 0:
        for w in range(n):
            for j in range(2):
                to_near(w, j).start()
    elif stage == 1:
        for w in range(n):
            for j in range(2):
                from_near(w, j).wait_recv()
                onward(w, j, slots[j]).start()
                to_sibling(w, j, c).start()
        for w in range(n):
            to_sibling(w, 0, 1 - c).wait_recv()
    elif stage == 2:
        for w in range(n):
            to_sibling(w, 1, 1 - c).wait_recv()
    elif stage == 3:
        for w in range(n):
            for j in range(2):
                onward(w, j, slots[2]).wait_recv()
            to_sibling(w, 2, c).start()
        for w in range(n):
            to_sibling(w, 2, 1 - c).wait_recv()
    else:
        for w in range(n):
            for j in range(2):
                to_near(w, j).wait_send()
                onward(w, j, slots[j]).wait_send()
            for j in range(3):
                to_sibling(w, j, c).wait_send()


def _plan_gather_relay(stacks):
    def stages(which):
        def run(ins, outs, sems):
            for stage in which:
                _relay_gather_stage(stage, outs, sems)
        return run

    return _Plan(stacks, [jax.ShapeDtypeStruct(s.shape, s.dtype) for s in stacks], {w: w for w in range(len(stacks))},
                 RELAY_SEMS, stages([0]), stages([3, 4]), stages([1, 2]), middle_at=(5, 8))


def _ffn_fwd_gathering(x, gain, stacks, order, later):
    t, d = x.shape
    nch, fc, _ = stacks[0].shape
    n = len(stacks)
    tm = min(FFN_STAGED_TILE, t)
    nt = t // tm
    p_in, p_out = len(later.ins), len(later.outs)
    relay = _relay_gather_stage

    def body(order_ref, x_ref, g_ref, *refs):
        later_in, refs = refs[n:n + p_in], refs[n + p_in:]
        o_ref, h_ref, a_ref, b_ref, s_ref = refs[:5]
        stack_refs, later_out, refs = refs[5:5 + n], refs[5 + n:5 + n + p_out], refs[5 + n + p_out:]
        w_ref, hs_ref, acc_ref, w_sem = refs[:4]
        relay_sems, later_sems = refs[4:10], refs[10:]
        k, i = pl.program_id(0), pl.program_id(1)
        tile = pl.ds(pl.multiple_of(i * tm, tm), tm)

        def loads(chunk):
            return [pltpu.make_async_copy(stack_refs[w].at[order_ref[chunk]], w_ref.at[chunk % 2, w], w_sem.at[chunk % 2, w])
                    for w in range(n)]

        @pl.when((k == 0) & (i == 0))
        def _():
            relay(0, stack_refs, relay_sems)
            for cp in loads(k):
                cp.start()

        @pl.when(i == 0)
        def _():
            for cp in loads(k):
                cp.wait()

        @pl.when((i == nt - 1) & (k < nch - 1))
        def _():
            for stage in range(1, nch):
                @pl.when(k + 1 == stage)
                def _():
                    relay(stage, stack_refs, relay_sems)
                    if stage == 1 and later.start is not None:
                        later.start(later_in, later_out, later_sems)
            for cp in loads(k + 1):
                cp.start()

        wg_ref, wu_ref, wd_ref = (w_ref.at[k % 2, w] for w in range(n))

        @pl.when(k == 0)
        def _():
            xf = x_ref[...]
            hb = ((xf * _rms(xf)) * g_ref[...]).astype(BF16)
            h_ref[...] = hb
            hs_ref[tile, :] = hb
            acc_ref[tile, :] = jnp.zeros((tm, d), F32)

        for rows in _row_halves(tm):
            part = pl.ds(pl.multiple_of(i * tm + rows.start, tm // 2), tm // 2)
            h = hs_ref[part, :]
            a = _dot_nt(h, wg_ref[...])
            b = _dot_nt(h, wu_ref[...])
            sb = ((a * (0.5 * jnp.tanh(0.5 * a) + 0.5)) * b).astype(BF16)
            a_ref[rows, :] = a.astype(BF16)
            b_ref[rows, :] = b.astype(BF16)
            s_ref[rows, :] = sb
            acc_ref[part, :] += _dot(sb, wd_ref[...])

        @pl.when(k == nch - 1)
        def _():
            o_ref[...] = x_ref[...] + 0.5 * acc_ref[tile, :]

        if later.middle is not None:
            @pl.when((k == nch - 1) & (i == nt // 2))
            def _():
                later.middle(later_in, later_out, later_sems)

        @pl.when((k == nch - 1) & (i == nt - 1))
        def _():
            relay(nch, stack_refs, relay_sems)
            if later.finish is not None:
                later.finish(later_in, later_out, later_sems)

    ends = lambda k, i: jnp.where((k == 0) | (k == nch - 1), i, 0)
    act = pl.BlockSpec((None, tm, fc), lambda k, i, order: (order[k], i, 0))
    out_shape = [jax.ShapeDtypeStruct((t, d), F32), jax.ShapeDtypeStruct((t, d), BF16)]
    out_shape += [jax.ShapeDtypeStruct((nch, t, fc), BF16)] * 3
    out_shape += [jax.ShapeDtypeStruct(s.shape, s.dtype) for s in stacks] + list(later.outs)
    aliases = {3 + w: 5 + w for w in range(n)}
    aliases.update({3 + n + i: 5 + n + o for i, o in later.alias.items()})
    res = pl.pallas_call(
        body, out_shape=out_shape,
        grid_spec=pltpu.PrefetchScalarGridSpec(
            num_scalar_prefetch=1, grid=(nch, nt),
            in_specs=[pl.BlockSpec((tm, d), lambda k, i, order: (ends(k, i), 0)),
                      pl.BlockSpec((1, d), lambda k, i, order: (0, 0))] + [ANY] * (n + p_in),
            out_specs=[pl.BlockSpec((tm, d), lambda k, i, order: (jnp.where(k == nch - 1, i, 0), 0)),
                       pl.BlockSpec((tm, d), lambda k, i, order: (jnp.where(k == 0, i, nt - 1), 0)),
                       act, act, act] + [ANY] * (n + p_out),
            scratch_shapes=[pltpu.VMEM((2, n, fc, d), BF16), pltpu.VMEM((t, d), BF16), pltpu.VMEM((t, d), F32),
                            pltpu.SemaphoreType.DMA((2, n))] + RELAY_SEMS + list(later.sems)),
        input_output_aliases=aliases, compiler_params=_params(), name="ffn_fwd",
    )(order, x, gain, *stacks, *later.ins)
    return list(res[:5]), list(res[5:5 + n]), list(res[5 + n:])


def _plan_sibling_halves(gs):
    n = len(gs)

    def copies(ins, outs, sems):
        x, y, c = _mesh_pos()
        return [_remote(ins[w].at[:, _half_rows(gs[w].shape[1], 1 - c), :], outs[w], sems[0].at[w], sems[1].at[w],
                        (x, y, 1 - c)) for w in range(n)]

    def start(ins, outs, sems):
        for cp in copies(ins, outs, sems):
            cp.start()

    def finish(ins, outs, sems):
        for cp in copies(ins, outs, sems):
            cp.wait()

    return _Plan(gs, [jax.ShapeDtypeStruct((g.shape[0], g.shape[1] // 2, g.shape[2]), g.dtype) for g in gs], {},
                 [pltpu.SemaphoreType.DMA((n,))] * 2, start, finish)


def _plan_chip_exchange(ps):
    n = len(ps)

    def copies(ins, outs, sems):
        x, y, c = _mesh_pos()
        return [_remote(ins[w].at[2 * cx + cy], outs[w].at[j], sems[0].at[w, j], sems[1].at[w, j], (cx, cy, c))
                for w in range(n) for j, (cx, cy) in enumerate(_other_chips(x, y))]

    def start(ins, outs, sems):
        for cp in copies(ins, outs, sems):
            cp.start()

    def finish(ins, outs, sems):
        for cp in copies(ins, outs, sems):
            cp.wait()

    return _Plan(ps, [jax.ShapeDtypeStruct((3,) + p.shape[1:], p.dtype) for p in ps], {},
                 [pltpu.SemaphoreType.DMA((n, 3))] * 2, start, finish)


def _plan_sibling_share(gs):
    n = len(gs)

    def copies(outs, sems, which):
        x, y, c = _mesh_pos()
        cps = []
        for w in range(n):
            rows = outs[w].at[_half_rows(gs[w].shape[0], c if which == "mine" else 1 - c)]
            cps.append(_remote(rows, rows, sems[0].at[w], sems[1].at[w], (x, y, 1 - c)))
        return cps

    def start(ins, outs, sems):
        for cp in copies(outs, sems, "mine"):
            cp.start()

    def finish(ins, outs, sems):
        for cp in copies(outs, sems, "mine"):
            cp.wait_send()
        for cp in copies(outs, sems, "theirs"):
            cp.wait_recv()

    return _Plan(gs, [jax.ShapeDtypeStruct(g.shape, g.dtype) for g in gs], {w: w for w in range(n)},
                 [pltpu.SemaphoreType.DMA((n,))] * 2, start, finish)


def _same_shape_groups(arrays):
    groups = {}
    for i, a in enumerate(arrays):
        groups.setdefault(a.shape, []).append(i)
    return list(groups.values())


def _add_sibling(gs, r1s, ids, tag):
    n = len(gs)
    nch, rh, cols = r1s[0].shape

    def body(ids_ref, *refs):
        for g_ref, r_ref, o_ref in zip(refs[:n], refs[n:2 * n], refs[2 * n:]):
            o_ref[...] = (g_ref[...] + r_ref[...]).astype(BF16)

    blk = lambda fn: pl.BlockSpec((None, rh, cols), fn)
    return pl.pallas_call(
        body, out_shape=[jax.ShapeDtypeStruct(r1s[0].shape, BF16)] * n,
        grid_spec=pltpu.PrefetchScalarGridSpec(
            num_scalar_prefetch=1, grid=(nch,),
            in_specs=[blk(lambda k, ids: (k, ids[1], 0))] * n + [blk(lambda k, ids: (k, 0, 0))] * n,
            out_specs=[blk(lambda k, ids: (k, 0, 0))] * n),
        compiler_params=_params(), name="add_sibling_" + tag,
    )(ids, *gs, *r1s)


def _add_chips(gs, r1s, r2s, ids, tag):
    n = len(gs)
    _, rh, cols = r1s[0].shape
    nb = 2 if rh % 32 == 0 else 1
    rb = rh // nb

    def body(ids_ref, *refs):
        for g_ref, r1_ref, r2_ref, o_ref in zip(refs[:n], refs[n:2 * n], refs[2 * n:3 * n], refs[3 * n:]):
            own = g_ref[...] + r1_ref[...]
            o_ref[...] = ((own + r2_ref[0].astype(F32)) + r2_ref[1].astype(F32)) + r2_ref[2].astype(F32)

    return pl.pallas_call(
        body, out_shape=[jax.ShapeDtypeStruct((2 * rh, cols), F32)] * n,
        grid_spec=pltpu.PrefetchScalarGridSpec(
            num_scalar_prefetch=1, grid=(nb,),
            in_specs=[pl.BlockSpec((None, rb, cols), lambda i, ids: (ids[0], ids[1] * nb + i, 0))] * n
            + [pl.BlockSpec((None, rb, cols), lambda i, ids: (ids[0], i, 0))] * n
            + [pl.BlockSpec((3, rb, cols), lambda i, ids: (0, i, 0))] * n,
            out_specs=[pl.BlockSpec((rb, cols), lambda i, ids: (ids[1] * nb + i, 0))] * n),
        compiler_params=_params(), name="add_chips_" + tag,
    )(ids, *gs, *r1s, *r2s)


VEC_ROWS = 8


N_DEVICES = 8


def _small_pack(part, d, width):
    names = ("ffn1_norm", "mix_norm", "ffn2_norm", "pool_scale", "out_norm_pool", "out_norm_attn", "qn", "kn", "b_forget",
             "pool_w", "loss")
    args = [part[k] for k in names]
    pw_shape = part["pool_w"].shape[1:]

    def body(g1_ref, gm_ref, g2_ref, ps_ref, onp_ref, ona_ref, qn_ref, kn_ref, bf_ref, pw_ref, loss_ref, vbuf, pbuf):
        lo = _head_masks()

        def fold_heads(ref):
            v = jnp.sum(ref[...], axis=0)
            acc = jnp.zeros((VEC_ROWS, LANES), F32)
            for blk in range(width // LANES):
                vb = jnp.broadcast_to(v[:, blk * LANES:(blk + 1) * LANES], (VEC_ROWS, LANES))
                acc = acc + vb + pltpu.roll(vb, HEAD_DIM, 1)
            return jnp.where(lo, acc, 0.0)[0:1, :]

        vbuf[0] = jnp.zeros((VEC_ROWS, d), F32)
        vbuf[0, 0:1, :] = jnp.sum(g1_ref[...], axis=0)
        vbuf[0, 1:2, :] = jnp.sum(gm_ref[...], axis=0)
        vbuf[0, 2:3, :] = jnp.sum(g2_ref[...], axis=0)
        vbuf[0, 5:6, 0:LANES] = jnp.sum(loss_ref[...], axis=0)[0:1, :]
        vbuf[0, 3:4, 0:width] = jnp.sum(ps_ref[...], axis=0)
        vbuf[0, 3:4, width:2 * width] = jnp.sum(onp_ref[...], axis=0)
        vbuf[0, 4:5, 0:width] = jnp.sum(ona_ref[...], axis=0)
        vbuf[0, 4:5, width:width + LANES] = fold_heads(qn_ref)
        vbuf[0, 4:5, width + LANES:width + 2 * LANES] = fold_heads(kn_ref)
        vbuf[0, 4:5, width + 2 * LANES:width + 3 * LANES] = jnp.sum(bf_ref[...], axis=0)
        pbuf[0] = jnp.sum(pw_ref[...], axis=0)

    return pl.pallas_call(
        body, out_shape=[jax.ShapeDtypeStruct((N_DEVICES, VEC_ROWS, d), F32), jax.ShapeDtypeStruct((N_DEVICES,) + pw_shape, F32)],
        in_specs=[VM] * len(args), out_specs=[VM, VM], compiler_params=_params(), name="small_pack",
    )(*args)


def _plan_all_to_all(stacks):
    n = len(stacks)

    def copies(outs, sems):
        x, y, c = _mesh_pos()
        cps = []
        for r in range(1, N_DEVICES):
            peer = (x if not r & 4 else 1 - x, y if not r & 2 else 1 - y, c if not r & 1 else 1 - c)
            cps += [_remote(outs[w].at[0], outs[w].at[r], sems[0].at[w, r - 1], sems[1].at[w, r - 1], peer) for w in range(n)]
        return cps

    def start(ins, outs, sems):
        for cp in copies(outs, sems):
            cp.start()

    def finish(ins, outs, sems):
        for cp in copies(outs, sems):
            cp.wait()

    return _Plan(stacks, [jax.ShapeDtypeStruct(s.shape, s.dtype) for s in stacks], {w: w for w in range(n)},
                 [pltpu.SemaphoreType.DMA((n, N_DEVICES - 1))] * 2, start, finish)


def _small_sum(vstack, pstack, me):
    def body(me_ref, vbuf, pbuf, vec_ref, pw_ref):
        vec = vbuf[me_ref[0]]
        pw = pbuf[me_ref[0]]
        for dev in range(1, N_DEVICES):
            vec = vec + vbuf[jnp.bitwise_xor(me_ref[0], dev)]
            pw = pw + pbuf[jnp.bitwise_xor(me_ref[0], dev)]
        vec_ref[...] = vec
        pw_ref[...] = pw

    full = lambda s: pl.BlockSpec(s.shape, lambda i, me: (0,) * len(s.shape))
    outs = [jax.ShapeDtypeStruct(vstack.shape[1:], F32), jax.ShapeDtypeStruct(pstack.shape[1:], F32)]
    return pl.pallas_call(
        body, out_shape=outs,
        grid_spec=pltpu.PrefetchScalarGridSpec(num_scalar_prefetch=1, grid=(1,), in_specs=[full(vstack), full(pstack)],
                                               out_specs=[full(o) for o in outs]),
        compiler_params=_params(), name="small_sum",
    )(me, vstack, pstack)


def _adamw(ws, gs, ms, vs, tag):
    n = len(ws)
    rows, cols = ws[0].shape
    rb = rows
    while rb * cols * 4 * n > (1 << 20) and rb % 16 == 0:
        rb //= 2

    def body(*refs):
        for j in range(n):
            w_ref, g_ref, m_ref, v_ref = (refs[k * n + j] for k in range(4))
            go_ref, d_ref, mo_ref, vo_ref = (refs[(4 + k) * n + j] for k in range(4))
            gv = g_ref[...]
            go_ref[...] = gv
            m2 = ADAM_B1 * m_ref[...] + (1.0 - ADAM_B1) * gv
            v2 = ADAM_B2 * v_ref[...] + (1.0 - ADAM_B2) * (gv * gv)
            m_hat = m2 / (1.0 - ADAM_B1 ** ADAM_STEP)
            v_hat = v2 / (1.0 - ADAM_B2 ** ADAM_STEP)
            d_ref[...] = -ADAM_LR * (m_hat / (jnp.sqrt(v_hat) + ADAM_EPS) + ADAM_WD * w_ref[...])
            mo_ref[...] = m2
            vo_ref[...] = v2

    spec = pl.BlockSpec((rb, cols), lambda i: (i, 0))
    res, _ = _pallas(
        body, name="adamw_" + tag, args=[*ws, *gs, *ms, *vs], out_shape=[jax.ShapeDtypeStruct(ws[0].shape, F32)] * (4 * n),
        grid=(rows // rb,), in_specs=[spec] * (4 * n), out_specs=[spec] * (4 * n))
    return [tuple(res[k * n + j] for k in range(4)) for j in range(n)]


def _pack_vec(p, d, width):
    pad = lambda v: jnp.pad(v, (0, LANES - v.shape[0]))
    row3 = jnp.concatenate([p["pool_scale"], p["out_norm_pool"]])
    row4 = jnp.concatenate([p["out_norm_attn"], pad(p["q_norm"]), pad(p["k_norm"]), pad(p["b_forget"]),
                            jnp.zeros((d - width - 3 * LANES,), F32)])
    rows = [p["ffn1_norm"], p["mix_norm"], p["ffn2_norm"], row3, row4]
    return jnp.pad(jnp.stack(rows), ((0, VEC_ROWS - len(rows)), (0, 0)))


def _unpack_vec(vec, width):
    return dict(ffn1_norm=vec[0], mix_norm=vec[1], ffn2_norm=vec[2], pool_scale=vec[3, :width],
                out_norm_pool=vec[3, width:2 * width], out_norm_attn=vec[4, :width],
                q_norm=vec[4, width:width + HEAD_DIM], k_norm=vec[4, width + LANES:width + LANES + HEAD_DIM],
                b_forget=vec[4, width + 2 * LANES:width + 2 * LANES + N_HEADS])


WEIGHT_NAMES = ("ffn1_norm", "ffn1_w_gate", "ffn1_w_up", "ffn1_w_down", "mix_norm", "w_in", "b_forget", "pool_w",
                "pool_scale", "q_norm", "k_norm", "out_norm_pool", "out_norm_attn", "w_out", "ffn2_norm",
                "ffn2_w_gate", "ffn2_w_up", "ffn2_w_down")
BIG_NAMES = ("ffn1_w_gate", "ffn1_w_up", "ffn1_w_down", "w_in", "w_out", "ffn2_w_gate", "ffn2_w_up", "ffn2_w_down")
TRANSPOSED_NAMES = ("ffn1_w_gate", "ffn1_w_up", "w_in", "ffn2_w_gate", "ffn2_w_up")
FFN1_NAMES = ("ffn1_w_gate", "ffn1_w_up", "ffn1_w_down")
MIX_NAMES = ("w_in", "w_out")
FFN2_NAMES = ("ffn2_w_gate", "ffn2_w_up", "ffn2_w_down")


def kernel(x, ffn1_norm, ffn1_w_gate, ffn1_w_up, ffn1_w_down, mix_norm, w_in, b_forget, pool_w, pool_scale, q_norm, k_norm, out_norm_pool, out_norm_attn, w_out, ffn2_norm, ffn2_w_gate, ffn2_w_up, ffn2_w_down, loss_target, m_ffn1_norm, m_ffn1_w_gate, m_ffn1_w_up, m_ffn1_w_down, m_mix_norm, m_w_in, m_b_forget, m_pool_w, m_pool_scale, m_q_norm, m_k_norm, m_out_norm_pool, m_out_norm_attn, m_w_out, m_ffn2_norm, m_ffn2_w_gate, m_ffn2_w_up, m_ffn2_w_down, v_ffn1_norm, v_ffn1_w_gate, v_ffn1_w_up, v_ffn1_w_down, v_mix_norm, v_w_in, v_b_forget, v_pool_w, v_pool_scale, v_q_norm, v_k_norm, v_out_norm_pool, v_out_norm_attn, v_w_out, v_ffn2_norm, v_ffn2_w_gate, v_ffn2_w_up, v_ffn2_w_down):
    given = dict(locals())
    w = {n: given[n] for n in WEIGHT_NAMES}
    m = {n: given["m_" + n] for n in WEIGHT_NAMES}
    v = {n: given["v_" + n] for n in WEIGHT_NAMES}
    n_batch, seq, d = x.shape
    width = pool_scale.shape[0]
    in_rows = w_in.shape[1]
    in_cols = N_CHIPS * in_rows
    in_pad = -(-in_rows // 32) * 32
    in_cols_pad = in_cols - N_HEADS + LANES

    work = lambda a, n: a.T if n in TRANSPOSED_NAMES else a
    exchanged = lambda a, n: jnp.pad(a, ((0, in_pad - in_rows), (0, 0))) if n == "w_in" else a

    mesh_x, mesh_y, mesh_c = _mesh_pos()
    ids = jnp.stack([2 * mesh_x + mesh_y, mesh_c]).astype(jnp.int32)

    row = lambda a: a.reshape(1, -1)
    g1, gm, g2, ps, onp, ona = (row(a) for a in (ffn1_norm, mix_norm, ffn2_norm, pool_scale, out_norm_pool, out_norm_attn))
    qn, kn = row(jnp.tile(q_norm, N_HEADS)), row(jnp.tile(k_norm, N_HEADS))
    bf = row(jnp.pad(b_forget, (0, LANES - N_HEADS)))
    pwb = pool_w.astype(BF16)
    xf, tgt = x.reshape(n_batch * seq, d), loss_target.reshape(n_batch * seq, d)

    def grouped(call, names, *lists):
        out = [None] * len(names)
        for idx in _same_shape_groups(lists[0]):
            res = call(*[[lst[i] for i in idx] for lst in lists], names[idx[0]])
            for i, r in zip(idx, res):
                out[i] = r
        return out

    placed = dict(zip(BIG_NAMES, grouped(lambda ws, tag: _place_cast(ws, ids, tag), BIG_NAMES,
                                         [exchanged(work(w[n], n), n) for n in BIG_NAMES])))
    landing = jnp.stack([2 * cx + cy for cx, cy in [(mesh_x, mesh_y)] + _other_chips(mesh_x, mesh_y)]).astype(jnp.int32)
    (x1, h1, a1, b1, s1), (wg1, wu1, wd1), (w_in_all, w_out_all) = _ffn_fwd_gathering(
        xf, g1, [placed[n] for n in FFN1_NAMES], landing, _plan_gather([placed[n] for n in MIX_NAMES]))
    w_in_t = jnp.pad(w_in_all[:, :in_rows].reshape(in_cols, d), ((0, in_cols_pad - in_cols), (0, 0)))
    w_out_full = w_out_all.reshape(N_CHIPS * w_out.shape[0], d)
    woa, wob = w_out_full[:width], w_out_full[width:]

    hm, pv, q, k, qh, kh, vb, f = _mix_proj(x1, gm, w_in_t, qn, kn, width, width)
    qa, ka = _forget_prefix(f, bf, qh, kh, n_batch, seq)
    yp = _pool_fwd(pv, pwb, ps, onp, n_batch, seq)
    (o, lse), (wg2, wu2, wd2) = _attn_fwd(qa, ka, vb, n_batch, seq, plan=_plan_gather_relay([placed[n] for n in FFN2_NAMES]))
    x2, ya = _mix_out(x1, yp, o, ona, woa, wob)
    (dy, h2, a2, b2, s2, lpart, dyh), _ = _ffn_fwd(x2, g2, wg2, wu2, wd2, target=tgt)

    def to_chips(gs, arrived, tags):
        return grouped(lambda g, r, tag: _add_sibling(g, r, ids, tag), tags, gs, arrived)

    def own_rows(gs, from_sibling, from_chips, tags):
        return grouped(lambda g, ra, rb, tag: _add_chips(g, ra, rb, ids, tag), tags, gs, from_sibling, from_chips)

    (dx2, da2, db2, dg2), _ = _ffn_bwd_x(dy, x2, g2, a2, b2, wg2, wu2, wd2, "ffn2_bwd_x")
    dw2, _ = _ffn_bwd_w([(da2, h2), (db2, h2), (s2, dyh)], "ffn2_bwd_w")
    (dyp, do, delta, dwoa, dwob, dona), sib2 = _mix_out_bwd(dx2, o, yp, ya, ona, woa, wob, plan=_plan_sibling_halves(dw2))
    dpv, dpw, dps, donp = _pool_bwd(pv, dyp, pwb, ps, onp, n_batch, seq)
    (dqh, dkh, dv, dfq, dfk), chips2 = _attn_bwd(qa, ka, vb, do, lse, delta, n_batch, seq,
                                                 plan=_plan_chip_exchange(to_chips(dw2, sib2, FFN2_NAMES)))
    df, dbf = _forget_bwd(dfq, dfk, f, bf, n_batch, seq)
    dx1, dx1h, dw_in_t, dgm, dqn, dkn = _mix_in_bwd(dx2, x1, gm, hm, dpv, dqh, q, dkh, k, dv, df, qn, kn, w_in_t)
    in_base = [in_rows * k // 8 * 8 for k in range(N_CHIPS)]
    d_w_in = jnp.stack([dw_in_t[b:b + in_pad] for b in in_base])
    d_w_out = jnp.concatenate([dwoa, dwob], axis=0).reshape(N_CHIPS, w_out.shape[0], d)
    dwm = [d_w_in, d_w_out]
    down = FFN1_NAMES[2:]
    dwd1, sibm = _ffn_bwd_w([(s1, dx1h)], "ffn1_bwd_w_down", plan=_plan_sibling_halves(dwm))
    (da1, db1), arrived = _ffn_bwd_a(dx1h, a1, b1, wd1, "ffn1_bwd_a",
                                     plan=_merge_plans(_plan_sibling_halves(dwd1),
                                                       _plan_chip_exchange(to_chips(dwm, sibm, MIX_NAMES))))
    sibd, chipsm = arrived[:1], arrived[1:]
    gate_up = FFN1_NAMES[:2]
    dwgu1, chipsd = _ffn_bwd_w([(da1, h1), (db1, h1)], "ffn1_bwd_w_gate_up",
                               plan=_plan_chip_exchange(to_chips(dwd1, sibd, down)))
    n_tiles = (n_batch * seq) // min(FFN_TILE, n_batch * seq)
    first = max(n_tiles // 4, 1)
    begun, sibgu = _ffn_bwd_h(dx1, xf, g1, da1, db1, wg1, wu1, "ffn1_bwd_h_first", (0, first),
                              plan=_plan_sibling_halves(dwgu1))
    (gx, dg1), chipsgu = _ffn_bwd_h(dx1, xf, g1, da1, db1, wg1, wu1, "ffn1_bwd_h_rest", (first, n_tiles), prev=begun,
                                    plan=_plan_chip_exchange(to_chips(dwgu1, sibgu, gate_up)))

    part = dict(ffn1_norm=dg1, mix_norm=dgm, ffn2_norm=dg2, b_forget=dbf, pool_scale=dps, out_norm_pool=donp,
                out_norm_attn=dona, qn=dqn, kn=dkn, pool_w=dpw.reshape(n_batch, -1, pool_w.shape[-1]), loss=lpart)
    mine = (own_rows(dwgu1, sibgu, chipsgu, gate_up) + own_rows(dwd1, sibd, chipsd, down)
            + own_rows(dwm, sibm, chipsm, MIX_NAMES) + own_rows(dw2, sib2, chips2, FFN2_NAMES))
    last = _run_plan(_merge_plans(_plan_sibling_share(mine), _plan_all_to_all(_small_pack(part, d, width))), "last_exchange")
    vstack, pstack = last[len(mine):]
    g_vec, g_pw = _small_sum(vstack, pstack, jnp.reshape(4 * mesh_x + 2 * mesh_y + mesh_c, (1,)).astype(jnp.int32))
    loss = g_vec[5, 0]
    reduced = dict(zip(FFN1_NAMES + MIX_NAMES + FFN2_NAMES, last[:len(mine)]))
    reduced["w_in"] = lax.dynamic_slice(reduced["w_in"], ((in_rows * ids[0]) % 8, 0), (in_rows, d))

    grads, delta, new_m, new_v = {}, {}, {}, {}
    for names in (FFN2_NAMES, FFN1_NAMES, ("w_in",), ("w_out",)):
        stepped = _adamw([work(w[n], n) for n in names], [reduced[n] for n in names], [work(m[n], n) for n in names],
                         [work(v[n], n) for n in names], names[0])
        for n, step in zip(names, stepped):
            grads[n], delta[n], new_m[n], new_v[n] = (work(a, n) for a in step)
    flat_pw = lambda a: a.reshape(-1, a.shape[-1])
    (_, d_pw, m_pw, v_pw), = _adamw([flat_pw(pool_w)], [g_pw], [flat_pw(m_pool_w)], [flat_pw(v_pool_w)], "pool_w")
    (_, d_vec, m_vec, v_vec), = _adamw([_pack_vec(w, d, width)], [g_vec], [_pack_vec(m, d, width)],
                                       [_pack_vec(v, d, width)], "vectors")
    grads.update(_unpack_vec(g_vec, width), pool_w=g_pw.reshape(pool_w.shape))
    delta.update(_unpack_vec(d_vec, width), pool_w=d_pw.reshape(pool_w.shape))
    new_m.update(_unpack_vec(m_vec, width), pool_w=m_pw.reshape(pool_w.shape))
    new_v.update(_unpack_vec(v_vec, width), pool_w=v_pw.reshape(pool_w.shape))
    return (loss, gx.reshape(x.shape), *[grads[n] for n in WEIGHT_NAMES], *[delta[n] for n in WEIGHT_NAMES],
            *[new_m[n] for n in WEIGHT_NAMES], *[new_v[n] for n in WEIGHT_NAMES])
```

```python
import functools

import jax
import jax.numpy as jnp
from jax import lax
from jax.experimental import pallas as pl
from jax.experimental.pallas import tpu as pltpu

F32 = jnp.float32
BF16 = jnp.bfloat16
EPS = 1e-6
NEG = -1e30
ADAM_LR = 0.001
ADAM_B1 = 0.9
ADAM_B2 = 0.999
ADAM_EPS = 1e-08
ADAM_WD = 0.01
ADAM_STEP = 10
POOL_WINDOWS = (2, 4, 8, 16)
HEAD_DIM = 64
N_HEADS = 8
LANES = 128
N_CHIPS = 4
ATT_BLOCK = 512
ATT_SUB = 128
FFN_TILE = 1024
FFN_STAGED_TILE = 512
VMEM_LIMIT = 62 * 1024 * 1024
ANY = pl.BlockSpec(memory_space=pl.ANY)
VM = pl.BlockSpec(memory_space=pltpu.VMEM)


def _params(**kw):
    return pltpu.CompilerParams(vmem_limit_bytes=VMEM_LIMIT, **kw)


def _dot(a, b):
    return jnp.dot(a, b, preferred_element_type=F32)


def _dot_nt(a, b):
    return lax.dot_general(a, b, (((1,), (1,)), ((), ())), preferred_element_type=F32)


def _dot_tn(a, b):
    return lax.dot_general(a, b, (((0,), (0,)), ((), ())), preferred_element_type=F32)


def _sigmoid(z):
    return 1.0 / (1.0 + jnp.exp(-z))


def _rms(xf):
    return lax.rsqrt(jnp.mean(xf * xf, axis=-1, keepdims=True) + EPS)


def _rms_bwd(xf, r, gain, dh):
    xh = xf * r
    dyg = dh * gain
    return r * (dyg - xh * jnp.mean(dyg * xh, axis=-1, keepdims=True)), dh * xh


def _total(v):
    return jnp.sum(jnp.sum(v, axis=1, keepdims=True), axis=0, keepdims=True)


def _ffn_fwd(x, gain, wg, wu, wd, target=None, plan=None):
    t, d = x.shape
    nch, fc, _ = wg.shape
    tm = min(FFN_TILE, t)
    nt = t // tm
    with_loss = target is not None

    def body(*refs):
        if with_loss:
            x_ref, g_ref, wg_ref, wu_ref, wd_ref, t_ref, o_ref, h_ref, a_ref, b_ref, s_ref, l_ref, oh_ref, acc_ref = refs
        else:
            x_ref, g_ref, wg_ref, wu_ref, wd_ref, o_ref, h_ref, a_ref, b_ref, s_ref, acc_ref = refs
        k = pl.program_id(1)

        @pl.when(k == 0)
        def _():
            xf = x_ref[...]
            h_ref[...] = ((xf * _rms(xf)) * g_ref[...]).astype(BF16)
            acc_ref[...] = jnp.zeros_like(acc_ref)

        for rows in _row_halves(tm):
            h = h_ref[rows, :]
            a = _dot_nt(h, wg_ref[...])
            b = _dot_nt(h, wu_ref[...])
            sb = ((a * (0.5 * jnp.tanh(0.5 * a) + 0.5)) * b).astype(BF16)
            a_ref[rows, :] = a.astype(BF16)
            b_ref[rows, :] = b.astype(BF16)
            s_ref[rows, :] = sb
            acc_ref[rows, :] += _dot(sb, wd_ref[...])

        @pl.when(k == nch - 1)
        def _():
            y = x_ref[...] + 0.5 * acc_ref[...]
            if with_loss:
                e = y - t_ref[...]
                o_ref[...] = e * (1.0 / d)
                oh_ref[...] = (e * (0.5 / d)).astype(BF16)
                l_ref[...] = jnp.broadcast_to(_total(e * e) * (0.5 / d), l_ref.shape)
            else:
                o_ref[...] = y

    row = pl.BlockSpec((tm, d), lambda i, k: (i, 0))
    chunk = pl.BlockSpec((None, fc, d), lambda i, k: (k, 0, 0))
    act = pl.BlockSpec((None, tm, fc), lambda i, k: (k, i, 0))
    in_specs = [row, pl.BlockSpec((1, d), lambda i, k: (0, 0)), chunk, chunk, chunk]
    out_shape = [jax.ShapeDtypeStruct((t, d), F32), jax.ShapeDtypeStruct((t, d), BF16)]
    out_shape += [jax.ShapeDtypeStruct((nch, t, fc), BF16)] * 3
    out_specs = [row, row, act, act, act]
    args = [x, gain, wg, wu, wd]
    if with_loss:
        in_specs.append(row)
        args.append(target)
        out_shape += [jax.ShapeDtypeStruct((nt, 8, LANES), F32), jax.ShapeDtypeStruct((t, d), BF16)]
        out_specs += [pl.BlockSpec((None, 8, LANES), lambda i, k: (i, 0, 0)), row]
    return _pallas(body, name="ffn_fwd_loss" if with_loss else "ffn_fwd", args=args, in_specs=in_specs,
                   out_shape=out_shape, out_specs=out_specs, grid=(nt, nch),
                   scratch_shapes=[pltpu.VMEM((tm, d), F32)], plan=plan)


def _row_halves(n):
    return [slice(0, n // 2), slice(n // 2, n)]


def _swiglu_grads(dyh, a_ref, b_ref, wd_ref, rows):
    ds = _dot_nt(dyh, wd_ref[...])
    av = a_ref[rows, :].astype(F32)
    bv = b_ref[rows, :].astype(F32)
    th = jnp.tanh(0.5 * av)
    sig = 0.5 * th + 0.5
    dab = ((ds * bv) * (sig * (1.0 + av * (0.5 - 0.5 * th)))).astype(BF16)
    return dab, (ds * (av * sig)).astype(BF16)


def _ffn_bwd_a(dyh, a, b, wd, name, plan=None):
    t, d = dyh.shape
    nch, fc, _ = wd.shape
    tm = min(FFN_TILE, t)

    def body(dyh_ref, a_ref, b_ref, wd_ref, da_ref, db_ref):
        for rows in _row_halves(tm):
            da_ref[rows, :], db_ref[rows, :] = _swiglu_grads(dyh_ref[rows, :], a_ref, b_ref, wd_ref, rows)

    act = pl.BlockSpec((None, tm, fc), lambda i, k: (k, i, 0))
    return _pallas(
        body, name=name, args=[dyh, a, b, wd], out_shape=[jax.ShapeDtypeStruct((nch, t, fc), BF16)] * 2, grid=(t // tm, nch),
        in_specs=[pl.BlockSpec((tm, d), lambda i, k: (i, 0)), act, act, pl.BlockSpec((None, fc, d), lambda i, k: (k, 0, 0))],
        out_specs=[act, act], plan=plan)


def _ffn_bwd_h(dy, x, gain, da, db, wg, wu, name, tiles, prev=None, plan=None):
    t, d = x.shape
    nch, fc, _ = wg.shape
    tm = min(FFN_TILE, t)
    nt = t // tm
    t0, t1 = tiles

    def body(*refs):
        dy_ref, x_ref, g_ref, da_ref, db_ref, wg_ref, wu_ref = refs[:7]
        dx_ref, dg_ref, acc_ref = refs[-3:]
        k = pl.program_id(1)

        @pl.when(k == 0)
        def _():
            acc_ref[...] = jnp.zeros_like(acc_ref)

        acc_ref[...] += _dot(da_ref[...], wg_ref[...]) + _dot(db_ref[...], wu_ref[...])

        @pl.when(k == nch - 1)
        def _():
            xf = x_ref[...]
            dxn, dgr = _rms_bwd(xf, _rms(xf), g_ref[...], acc_ref[...])
            dx_ref[...] = dy_ref[...] + dxn
            dg_ref[...] = jnp.sum(dgr, axis=0, keepdims=True)

    row = pl.BlockSpec((tm, d), lambda i, k: (i + t0, 0))
    chunk = pl.BlockSpec((None, fc, d), lambda i, k: (k, 0, 0))
    act = pl.BlockSpec((None, tm, fc), lambda i, k: (k, i + t0, 0))
    args = [dy, x, gain, da, db, wg, wu]
    in_specs = [row, row, pl.BlockSpec((1, d), lambda i, k: (0, 0)), act, act, chunk, chunk]
    aliases = {}
    if prev is not None:
        aliases = {len(args): 0, len(args) + 1: 1}
        args += list(prev)
        in_specs += [ANY, ANY]
    return _pallas(
        body, name=name, args=args, out_shape=[jax.ShapeDtypeStruct((t, d), F32), jax.ShapeDtypeStruct((nt, 1, d), F32)],
        grid=(t1 - t0, nch), in_specs=in_specs,
        out_specs=[row, pl.BlockSpec((None, 1, d), lambda i, k: (i + t0, 0, 0))],
        scratch_shapes=[pltpu.VMEM((tm, d), F32)], plan=plan, aliases=aliases)


def _ffn_bwd_x(dy, x, gain, a, b, wg, wu, wd, name, plan=None):
    t, d = x.shape
    nch, fc, _ = wg.shape
    tm = min(FFN_TILE, t)
    nt = t // tm

    def body(dy_ref, x_ref, g_ref, a_ref, b_ref, wg_ref, wu_ref, wd_ref, dx_ref, da_ref, db_ref, dg_ref, acc_ref):
        k = pl.program_id(1)

        @pl.when(k == 0)
        def _():
            acc_ref[...] = jnp.zeros_like(acc_ref)

        for rows in _row_halves(tm):
            dab, dbb = _swiglu_grads((0.5 * dy_ref[rows, :]).astype(BF16), a_ref, b_ref, wd_ref, rows)
            da_ref[rows, :] = dab
            db_ref[rows, :] = dbb
            acc_ref[rows, :] += _dot(dab, wg_ref[...]) + _dot(dbb, wu_ref[...])

        @pl.when(k == nch - 1)
        def _():
            xf = x_ref[...]
            dxn, dgr = _rms_bwd(xf, _rms(xf), g_ref[...], acc_ref[...])
            dx_ref[...] = dy_ref[...] + dxn
            dg_ref[...] = jnp.sum(dgr, axis=0, keepdims=True)

    row = pl.BlockSpec((tm, d), lambda i, k: (i, 0))
    chunk = pl.BlockSpec((None, fc, d), lambda i, k: (k, 0, 0))
    act = pl.BlockSpec((None, tm, fc), lambda i, k: (k, i, 0))
    return _pallas(
        body, name=name, args=[dy, x, gain, a, b, wg, wu, wd],
        out_shape=[jax.ShapeDtypeStruct((t, d), F32), jax.ShapeDtypeStruct((nch, t, fc), BF16),
                   jax.ShapeDtypeStruct((nch, t, fc), BF16), jax.ShapeDtypeStruct((nt, 1, d), F32)],
        grid=(nt, nch),
        in_specs=[row, row, pl.BlockSpec((1, d), lambda i, k: (0, 0)), act, act, chunk, chunk, chunk],
        out_specs=[row, act, act, pl.BlockSpec((None, 1, d), lambda i, k: (i, 0, 0))],
        scratch_shapes=[pltpu.VMEM((tm, d), F32)], plan=plan)


def _ffn_bwd_w(pairs, name, plan=None):
    n = len(pairs)
    nch, t, fc = pairs[0][0].shape
    d = pairs[0][1].shape[1]
    tm = min(FFN_TILE, t)

    def body(*refs):
        @pl.when(pl.program_id(1) == 0)
        def _():
            for o_ref in refs[2 * n:]:
                o_ref[...] = jnp.zeros_like(o_ref)

        for j in range(n):
            refs[2 * n + j][...] += _dot_tn(refs[j][...], refs[n + j][...])

    row = pl.BlockSpec((tm, d), lambda k, i: (i, 0))
    act = pl.BlockSpec((None, tm, fc), lambda k, i: (k, i, 0))
    chunk = pl.BlockSpec((None, fc, d), lambda k, i: (k, 0, 0))
    return _pallas(body, name=name, args=[p[0] for p in pairs] + [p[1] for p in pairs],
                   out_shape=[jax.ShapeDtypeStruct((nch, fc, d), F32)] * n, grid=(nch, t // tm),
                   in_specs=[act] * n + [row] * n, out_specs=[chunk] * n, plan=plan)


def _head_masks():
    lane = lax.broadcasted_iota(jnp.int32, (1, LANES), 1)
    return lane < HEAD_DIM


def _head_rms(x, lo):
    x2 = x * x
    s0 = jnp.sum(jnp.where(lo, x2, 0.0), axis=1, keepdims=True)
    s1 = jnp.sum(jnp.where(lo, 0.0, x2), axis=1, keepdims=True)
    return jnp.where(lo, lax.rsqrt(s0 * (1.0 / HEAD_DIM) + EPS), lax.rsqrt(s1 * (1.0 / HEAD_DIM) + EPS))


def _head_mean(v, lo):
    s0 = jnp.sum(jnp.where(lo, v, 0.0), axis=1, keepdims=True)
    s1 = jnp.sum(jnp.where(lo, 0.0, v), axis=1, keepdims=True)
    return jnp.where(lo, s0, s1) * (1.0 / HEAD_DIM)


def _mix_proj(x1, gain, wt, qn, kn, pool_width, attn_width):
    t, d = x1.shape
    tm = min(512, t)
    nt = t // tm
    scale = HEAD_DIM ** -0.5
    c_q, c_k, c_v = pool_width, pool_width + attn_width, pool_width + 2 * attn_width
    c_f = c_v + attn_width

    def body(x_ref, g_ref, wt_ref, qn_ref, kn_ref, hm_ref, pv_ref, q_ref, k_ref, qh_ref, kh_ref, vb_ref, f_ref):
        lo = _head_masks()
        for rows in _row_halves(tm):
            xf = x_ref[rows, :]
            hm = ((xf * _rms(xf)) * g_ref[...]).astype(BF16)
            hm_ref[rows, :] = hm
            f_ref[rows, :] = _dot_nt(hm, wt_ref[c_f:c_f + LANES, :])
            pv_ref[rows, :] = _dot_nt(hm, wt_ref[0:pool_width, :])
            vb_ref[rows, :] = _dot_nt(hm, wt_ref[c_v:c_v + attn_width, :]).astype(BF16)
            for c0, raw_ref, hat_ref, n_ref, mul in ((c_q, q_ref, qh_ref, qn_ref, scale), (c_k, k_ref, kh_ref, kn_ref, 1.0)):
                raw = _dot_nt(hm, wt_ref[c0:c0 + attn_width, :])
                raw_ref[rows, :] = raw
                for blk in range(attn_width // LANES):
                    sl = slice(blk * LANES, (blk + 1) * LANES)
                    xb = raw[:, sl]
                    hat_ref[rows, sl] = (((xb * _head_rms(xb, lo)) * n_ref[:, sl]) * mul).astype(BF16)

    row = pl.BlockSpec((tm, d), lambda i: (i, 0))
    half = pl.BlockSpec((tm, attn_width), lambda i: (i, 0))
    const = lambda shape: pl.BlockSpec(shape, lambda i: (0, 0))
    return _pallas(
        body, name="mix_proj", args=[x1, gain, wt, qn, kn],
        out_shape=[jax.ShapeDtypeStruct((t, d), BF16), jax.ShapeDtypeStruct((t, pool_width), F32),
                   jax.ShapeDtypeStruct((t, attn_width), F32), jax.ShapeDtypeStruct((t, attn_width), F32),
                   jax.ShapeDtypeStruct((t, attn_width), BF16), jax.ShapeDtypeStruct((t, attn_width), BF16),
                   jax.ShapeDtypeStruct((t, attn_width), BF16), jax.ShapeDtypeStruct((t, LANES), F32)],
        grid=(nt,),
        in_specs=[row, const((1, d)), const(wt.shape), const((1, attn_width)), const((1, attn_width))],
        out_specs=[row, pl.BlockSpec((tm, pool_width), lambda i: (i, 0)), half, half, half, half, half,
                   pl.BlockSpec((tm, LANES), lambda i: (i, 0))])[0]


def _shift_down(v, dist, row):
    return jnp.where(row >= dist, pltpu.roll(v, dist, 0), 0.0)


def _shift_up(v, dist, row, n):
    return jnp.where(row + dist < n, pltpu.roll(v, n - dist, 0), 0.0)


def _aug_lane(e):
    return HEAD_DIM if e == 0 else 0


def _forget_prefix(f, bias, qh, kh, n_batch, seq):
    def body(f_ref, b_ref, q_ref, k_ref, qa_ref, ka_ref):
        z = f_ref[...] + b_ref[...]
        acc = jnp.minimum(z, 0.0) - jnp.log(1.0 + jnp.exp(-jnp.abs(z)))
        row = lax.broadcasted_iota(jnp.int32, (seq, 1), 0)
        dist = 1
        while dist < seq:
            acc = acc + _shift_down(acc, dist, row)
            dist *= 2
        lane = lax.broadcasted_iota(jnp.int32, (1, LANES), 1)
        for h in range(N_HEADS):
            pair, e = divmod(h, 2)
            a0 = _aug_lane(e)
            own = (lane < HEAD_DIM) if e == 0 else (lane >= HEAD_DIM)
            fh = _pick_lane(acc, h)
            hi = fh.astype(BF16).astype(F32)
            rest = fh - hi
            mid = rest.astype(BF16).astype(F32)
            low = rest - mid
            q_ones = (lane >= a0 + 3) & (lane < a0 + 6)
            k_ones = (lane >= a0) & (lane < a0 + 3)
            q_aug = jnp.where(lane == a0, hi, jnp.where(lane == a0 + 1, mid, jnp.where(lane == a0 + 2, low,
                              jnp.where(q_ones, 1.0, 0.0))))
            k_aug = jnp.where(k_ones, 1.0, jnp.where(lane == a0 + 3, -hi, jnp.where(lane == a0 + 4, -mid,
                              jnp.where(lane == a0 + 5, -low, 0.0))))
            src = slice(pair * LANES, (pair + 1) * LANES)
            dst = slice(h * LANES, (h + 1) * LANES)
            qa_ref[:, dst] = jnp.where(own, q_ref[:, src].astype(F32), q_aug).astype(BF16)
            ka_ref[:, dst] = jnp.where(own, k_ref[:, src].astype(F32), k_aug).astype(BF16)

    width = qh.shape[1]
    tok = pl.BlockSpec((seq, width), lambda b: (b, 0))
    aug = pl.BlockSpec((seq, N_HEADS * LANES), lambda b: (b, 0))
    return pl.pallas_call(
        body, out_shape=[jax.ShapeDtypeStruct((n_batch * seq, N_HEADS * LANES), BF16)] * 2, grid=(n_batch,),
        in_specs=[pl.BlockSpec((seq, LANES), lambda b: (b, 0)), pl.BlockSpec((1, LANES), lambda b: (0, 0)), tok, tok],
        out_specs=[aug, aug], compiler_params=_params(), name="forget_prefix",
    )(f, bias, qh, kh)


def _pool_groups(pv_ref, pw_ref, ps_ref, seq):
    row = lax.broadcasted_iota(jnp.int32, (seq, 1), 0)
    pos = (row + 1).astype(F32)
    out = []
    for g, win in enumerate(POOL_WINDOWS):
        sl = slice(g * LANES, (g + 1) * LANES)
        xg = pv_ref[:, sl]
        acc = xg
        dist = 1
        while dist < win:
            acc = acc + _shift_down(acc, dist, row)
            dist *= 2
        pooled = (acc / jnp.minimum(pos, float(win)) - xg).astype(BF16)
        mixed = _dot(pooled, pw_ref[g])
        out.append((pooled, mixed, mixed * ps_ref[:, sl]))
    return out


def _pool_fwd(pv, pw, ps, onp, n_batch, seq):
    width = pv.shape[1]

    def body(pv_ref, pw_ref, ps_ref, on_ref, y_ref):
        groups = _pool_groups(pv_ref, pw_ref, ps_ref, seq)
        ssq = sum(jnp.sum(ms * ms, axis=1, keepdims=True) for _, _, ms in groups)
        r = lax.rsqrt(ssq * (1.0 / width) + EPS)
        for g, (_, _, ms) in enumerate(groups):
            sl = slice(g * LANES, (g + 1) * LANES)
            y_ref[:, sl] = ((ms * r) * on_ref[:, sl]).astype(BF16)

    return pl.pallas_call(
        body, out_shape=jax.ShapeDtypeStruct((n_batch * seq, width), BF16), grid=(n_batch,),
        in_specs=[pl.BlockSpec((seq, width), lambda b: (b, 0)), pl.BlockSpec(pw.shape, lambda b: (0, 0, 0)),
                  pl.BlockSpec((1, width), lambda b: (0, 0)), pl.BlockSpec((1, width), lambda b: (0, 0))],
        out_specs=pl.BlockSpec((seq, width), lambda b: (b, 0)),
        compiler_params=_params(), name="pool_fwd",
    )(pv, pw, ps, onp)


def _pool_bwd(pv, dyp, pw, ps, onp, n_batch, seq):
    width = pv.shape[1]

    def body(pv_ref, dy_ref, pw_ref, ps_ref, on_ref, dpv_ref, dpw_ref, dps_ref, don_ref):
        groups = _pool_groups(pv_ref, pw_ref, ps_ref, seq)
        ssq = sum(jnp.sum(ms * ms, axis=1, keepdims=True) for _, _, ms in groups)
        r = lax.rsqrt(ssq * (1.0 / width) + EPS)
        mean = sum(jnp.sum((dy_ref[:, g * LANES:(g + 1) * LANES] * on_ref[:, g * LANES:(g + 1) * LANES]) * (ms * r),
                           axis=1, keepdims=True) for g, (_, _, ms) in enumerate(groups)) * (1.0 / width)
        row = lax.broadcasted_iota(jnp.int32, (seq, 1), 0)
        pos = (row + 1).astype(F32)
        for g, (pooled, mixed, ms) in enumerate(groups):
            sl = slice(g * LANES, (g + 1) * LANES)
            dy = dy_ref[:, sl]
            xh = ms * r
            don_ref[:, sl] = jnp.sum(dy * xh, axis=0, keepdims=True)
            dms = r * (dy * on_ref[:, sl] - xh * mean)
            dps_ref[:, sl] = jnp.sum(dms * mixed, axis=0, keepdims=True)
            dmix = (dms * ps_ref[:, sl]).astype(BF16)
            dpw_ref[g] = _dot_tn(pooled, dmix)
            dpool = _dot_nt(dmix, pw_ref[g])
            win = POOL_WINDOWS[g]
            acc = dpool / jnp.minimum(pos, float(win))
            dist = 1
            while dist < win:
                acc = acc + _shift_up(acc, dist, row, seq)
                dist *= 2
            dpv_ref[:, sl] = (acc - dpool).astype(BF16)

    tok = pl.BlockSpec((seq, width), lambda b: (b, 0))
    vec = pl.BlockSpec((1, width), lambda b: (0, 0))
    pvec = pl.BlockSpec((None, 1, width), lambda b: (b, 0, 0))
    return pl.pallas_call(
        body,
        out_shape=[jax.ShapeDtypeStruct((n_batch * seq, width), BF16),
                   jax.ShapeDtypeStruct((n_batch,) + pw.shape, F32),
                   jax.ShapeDtypeStruct((n_batch, 1, width), F32), jax.ShapeDtypeStruct((n_batch, 1, width), F32)],
        grid=(n_batch,),
        in_specs=[tok, tok, pl.BlockSpec(pw.shape, lambda b: (0, 0, 0)), vec, vec],
        out_specs=[tok, pl.BlockSpec((None,) + pw.shape, lambda b: (b, 0, 0, 0)), pvec, pvec],
        compiler_params=_params(), name="pool_bwd",
    )(pv, dyp, pw, ps, onp)


def _pick_lane(tile, idx):
    lane = lax.broadcasted_iota(jnp.int32, (1, LANES), 1)
    return jnp.sum(jnp.where(lane == idx, tile, 0.0), axis=1, keepdims=True)


def _pick_row(tile, idx):
    sub = lax.broadcasted_iota(jnp.int32, (tile.shape[0], 1), 0)
    return jnp.sum(jnp.where(sub == idx, tile, 0.0), axis=0, keepdims=True)


def _put_lane(col, idx):
    lane = lax.broadcasted_iota(jnp.int32, (1, LANES), 1)
    return jnp.where(lane == idx, col, 0.0)


def _head_select(e):
    lo = _head_masks()
    return lo if e == 0 else jnp.logical_not(lo)


def _causal(st, shift):
    row = lax.broadcasted_iota(jnp.int32, st.shape, 0)
    col = lax.broadcasted_iota(jnp.int32, st.shape, 1) + shift
    return jnp.where(col >= row, st, NEG)


def _transpose_blocks(a):
    rows, cols = a.shape
    return jnp.concatenate(
        [jnp.concatenate([a[r:r + LANES, c:c + LANES].T for r in range(0, rows, LANES)], axis=1)
         for c in range(0, cols, LANES)], axis=0)


def _accumulate(ref, value, first):
    @pl.when(first)
    def _():
        ref[...] = value

    @pl.when(jnp.logical_not(first))
    def _():
        ref[...] += value


def _attn_fwd(qa, ka, vb, n_batch, seq, plan=None):
    tq = min(ATT_BLOCK, seq)
    nq, nsub, tk = seq // tq, tq // ATT_SUB, tq
    pairs = vb.shape[1] // LANES

    def body(q_ref, k_ref, v_ref, o_ref, lse_ref, acc_ref):
        i, p = pl.program_id(1), pl.program_id(2)
        row_lo = lax.broadcasted_iota(jnp.int32, (LANES, 1), 0) < HEAD_DIM
        qs = [q_ref[:, e * LANES:(e + 1) * LANES] for e in range(2)]
        acc_ref[...] = jnp.zeros_like(acc_ref)

        def tile(off, stats, diagonal):
            vj = v_ref[pl.ds(off, tk), :]
            new, alphas, pvs = [], [], []
            for e in range(2):
                st = _dot_nt(k_ref[pl.ds(off, tk), e * LANES:(e + 1) * LANES], qs[e])
                if diagonal:
                    st = _causal(st, 0)
                m, l = stats[e]
                m_new = jnp.maximum(m, jnp.max(st, axis=0, keepdims=True))
                alpha = jnp.exp(m - m_new)
                pt = jnp.exp(st - m_new)
                new.append((m_new, alpha * l + jnp.sum(pt, axis=0, keepdims=True)))
                alphas.append(alpha)
                pvs.append(_dot_tn(jnp.where(_head_select(e), vj, jnp.zeros_like(vj)), pt.astype(BF16)))
            acc_ref[...] = acc_ref[...] * jnp.where(row_lo, alphas[0], alphas[1]) + (pvs[0] + pvs[1])
            return tuple(new)

        init = ((jnp.full((1, tq), NEG, F32), jnp.zeros((1, tq), F32)),) * 2
        stats = lax.fori_loop(0, i, lambda j, st: tile(pl.multiple_of(j * tk, tk), st, False), init)
        (m0, l0), (m1, l1) = tile(pl.multiple_of(i * tk, tk), stats, True)
        out_t = acc_ref[...] / jnp.where(row_lo, l0, l1)
        sub = lax.broadcasted_iota(jnp.int32, (8, 1), 0)
        lse0, lse1 = m0 + jnp.log(l0), m1 + jnp.log(l1)
        for a in range(nsub):
            sl = slice(a * ATT_SUB, (a + 1) * ATT_SUB)
            o_ref[sl, :] = out_t[:, sl].T
            rows = jnp.where(sub == 2 * p, lse0[:, sl], 0.0) + jnp.where(sub == 2 * p + 1, lse1[:, sl], 0.0)
            _accumulate(lse_ref.at[a], rows, p == 0)

    return _pallas(
        body, name="attn_fwd", args=[qa, ka, vb],
        out_shape=[jax.ShapeDtypeStruct((n_batch * seq, pairs * LANES), F32),
                   jax.ShapeDtypeStruct((n_batch * seq // ATT_SUB, 8, ATT_SUB), F32)],
        grid=(n_batch, nq, pairs),
        in_specs=[pl.BlockSpec((tq, 2 * LANES), lambda b, i, p: (b * nq + i, p)),
                  pl.BlockSpec((seq, 2 * LANES), lambda b, i, p: (b, p)),
                  pl.BlockSpec((seq, LANES), lambda b, i, p: (b, p))],
        out_specs=[pl.BlockSpec((tq, LANES), lambda b, i, p: (b * nq + i, p)),
                   pl.BlockSpec((nsub, 8, ATT_SUB), lambda b, i, p: (b * nq + i, 0, 0))],
        scratch_shapes=[pltpu.VMEM((LANES, tq), F32)], plan=plan)


def _attn_bwd(qa, ka, vb, do, lse, delta, n_batch, seq, plan=None):
    tq = min(ATT_BLOCK, seq)
    nq, nsub = seq // tq, tq // ATT_SUB
    n_tiles = seq // ATT_SUB
    pairs = vb.shape[1] // LANES

    def body(q_ref, k_ref, v_ref, do_ref, lse_ref, dl_ref, dq_ref, dk_ref, dv_ref, dfq_ref, dfk_ref,
             dq0_ref, dq1_ref, dk0_ref, dk1_ref, dva_ref):
        p = pl.program_id(1)
        dqs, dks = (dq0_ref, dq1_ref), (dk0_ref, dk1_ref)
        for acc in (dk0_ref, dk1_ref, dva_ref):
            acc[...] = jnp.zeros_like(acc)
        dfq_cols = []
        for i in range(nq):
            rows_i = slice(i * tq, (i + 1) * tq)
            qs = [q_ref[rows_i, e * LANES:(e + 1) * LANES] for e in range(2)]
            dov = do_ref[rows_i, :]
            does = [jnp.where(_head_select(e), dov, jnp.zeros_like(dov)) for e in range(2)]
            stat = lambda ref, e: jnp.concatenate([_pick_row(ref[i * nsub + a], 2 * p + e) for a in range(nsub)], axis=1)
            ls, dl = [stat(lse_ref, e) for e in range(2)], [stat(dl_ref, e) for e in range(2)]
            for acc in dqs:
                acc[...] = jnp.zeros_like(acc)

            def tile(off, diagonal, qs=qs, dov=dov, does=does, ls=ls, dl=dl):
                vj = v_ref[pl.ds(off, tq), :]
                for e in range(2):
                    kj = k_ref[pl.ds(off, tq), e * LANES:(e + 1) * LANES]
                    st = _dot_nt(kj, qs[e])
                    if diagonal:
                        st = _causal(st, 0)
                    pt = jnp.exp(st - ls[e])
                    dva_ref[pl.ds(off, tq), :] += _dot(pt.astype(BF16), does[e])
                    dpt = _dot_nt(jnp.where(_head_select(e), vj, jnp.zeros_like(vj)), dov)
                    dst = (pt * (dpt - dl[e])).astype(BF16)
                    dks[e][pl.ds(off, tq), :] += _dot(dst, qs[e])
                    dqs[e][...] += _dot(_transpose_blocks(kj), dst)

            def step(j, carry, tile=tile):
                tile(pl.multiple_of(j * tq, tq), False)
                return carry

            lax.fori_loop(0, i, step, 0)
            tile(i * tq, True)
            dq0, dq1 = _transpose_blocks(dq0_ref[...]), _transpose_blocks(dq1_ref[...])
            dq_ref[rows_i, :] = jnp.where(_head_masks(), dq0, dq1)
            dfq_cols.append(_put_lane(_pick_lane(dq0, _aug_lane(0)), 2 * p) + _put_lane(_pick_lane(dq1, _aug_lane(1)), 2 * p + 1))
        dk0, dk1 = dk0_ref[...], dk1_ref[...]
        dk_ref[...] = jnp.where(_head_masks(), dk0, dk1)
        dv_ref[...] = dva_ref[...].astype(BF16)
        dfk = _put_lane(_pick_lane(dk0, _aug_lane(0) + 3), 2 * p) + _put_lane(_pick_lane(dk1, _aug_lane(1) + 3), 2 * p + 1)
        _accumulate(dfq_ref, jnp.concatenate(dfq_cols, axis=0), p == 0)
        _accumulate(dfk_ref, -dfk, p == 0)

    wide = pl.BlockSpec((seq, 2 * LANES), lambda b, p: (b, p))
    blk = pl.BlockSpec((seq, LANES), lambda b, p: (b, p))
    col = pl.BlockSpec((seq, LANES), lambda b, p: (b, 0))
    stat = pl.BlockSpec((n_tiles, 8, ATT_SUB), lambda b, p: (b, 0, 0))
    f32_blk, acc = jax.ShapeDtypeStruct((n_batch * seq, pairs * LANES), F32), pltpu.VMEM((seq, LANES), F32)
    return _pallas(
        body, name="attn_bwd", args=[qa, ka, vb, do, lse, delta],
        out_shape=[f32_blk, f32_blk, jax.ShapeDtypeStruct((n_batch * seq, pairs * LANES), BF16),
                   jax.ShapeDtypeStruct((n_batch * seq, LANES), F32), jax.ShapeDtypeStruct((n_batch * seq, LANES), F32)],
        grid=(n_batch, pairs), in_specs=[wide, wide, blk, blk, stat, stat], out_specs=[blk, blk, blk, col, col],
        scratch_shapes=[pltpu.VMEM((LANES, tq), F32), pltpu.VMEM((LANES, tq), F32), acc, acc, acc], plan=plan)


def _forget_bwd(dfq, dfk, f, bias, n_batch, seq):
    def body(dfq_ref, dfk_ref, f_ref, b_ref, df_ref, db_ref):
        acc = dfq_ref[...] + dfk_ref[...]
        row = lax.broadcasted_iota(jnp.int32, (seq, 1), 0)
        dist = 1
        while dist < seq:
            acc = acc + _shift_up(acc, dist, row, seq)
            dist *= 2
        df = acc * _sigmoid(-(f_ref[...] + b_ref[...]))
        df_ref[...] = df
        db_ref[...] = jnp.sum(df, axis=0, keepdims=True)

    col = pl.BlockSpec((seq, LANES), lambda b: (b, 0))
    return pl.pallas_call(
        body,
        out_shape=[jax.ShapeDtypeStruct((n_batch * seq, LANES), F32), jax.ShapeDtypeStruct((n_batch, 1, LANES), F32)],
        grid=(n_batch,), in_specs=[col, col, col, pl.BlockSpec((1, LANES), lambda b: (0, 0))],
        out_specs=[col, pl.BlockSpec((None, 1, LANES), lambda b: (b, 0, 0))],
        compiler_params=_params(), name="forget_bwd",
    )(dfq, dfk, f, bias)


def _mix_out(x1, yp, o, ona, woa, wob):
    t, d = x1.shape
    width = o.shape[1]
    tm = min(512, t)

    def body(x_ref, yp_ref, o_ref, on_ref, wa_ref, wb_ref, x2_ref, ya_ref):
        of = o_ref[...]
        ya = ((of * _rms(of)) * on_ref[...]).astype(BF16)
        ya_ref[...] = ya
        x2_ref[...] = x_ref[...] + (_dot(yp_ref[...], wa_ref[...]) + _dot(ya, wb_ref[...]))

    row = pl.BlockSpec((tm, d), lambda i: (i, 0))
    half = pl.BlockSpec((tm, width), lambda i: (i, 0))
    wspec = pl.BlockSpec((width, d), lambda i: (0, 0))
    return pl.pallas_call(
        body, out_shape=[jax.ShapeDtypeStruct((t, d), F32), jax.ShapeDtypeStruct((t, width), BF16)],
        grid=(t // tm,), in_specs=[row, half, half, pl.BlockSpec((1, width), lambda i: (0, 0)), wspec, wspec],
        out_specs=[row, half], compiler_params=_params(), name="mix_out",
    )(x1, yp, o, ona, woa, wob)


def _mix_out_bwd(dx2, o, yp, ya, ona, woa, wob, plan=None):
    t, d = dx2.shape
    width = o.shape[1]
    tm = min(512, t)
    nt = t // tm

    def body(dx_ref, o_ref, yp_ref, ya_ref, on_ref, wa_ref, wb_ref, dyp_ref, do_ref, dl_ref, dwa_ref, dwb_ref, don_ref):
        @pl.when(pl.program_id(0) == 0)
        def _():
            dwa_ref[...] = jnp.zeros_like(dwa_ref)
            dwb_ref[...] = jnp.zeros_like(dwb_ref)

        dxb = dx_ref[...].astype(BF16)
        dwa_ref[...] += _dot_tn(yp_ref[...], dxb)
        dwb_ref[...] += _dot_tn(ya_ref[...], dxb)
        dyp_ref[...] = _dot_nt(dxb, wa_ref[...])
        of = o_ref[...]
        dov, dgr = _rms_bwd(of, _rms(of), on_ref[...], _dot_nt(dxb, wb_ref[...]))
        don_ref[...] = jnp.sum(dgr, axis=0, keepdims=True)
        do_ref[...] = dov.astype(BF16)
        lo = _head_masks()
        prod = dov * of
        delta = jnp.zeros((tm, LANES), F32)
        for blk in range(width // LANES):
            pb = prod[:, blk * LANES:(blk + 1) * LANES]
            delta = delta + _put_lane(jnp.sum(jnp.where(lo, pb, 0.0), axis=1, keepdims=True), 2 * blk)
            delta = delta + _put_lane(jnp.sum(jnp.where(lo, 0.0, pb), axis=1, keepdims=True), 2 * blk + 1)
        for c in range(tm // ATT_SUB):
            dl_ref[c] = delta[c * ATT_SUB:(c + 1) * ATT_SUB, :].T[0:8, :]

    row = pl.BlockSpec((tm, d), lambda i: (i, 0))
    half = pl.BlockSpec((tm, width), lambda i: (i, 0))
    wspec = pl.BlockSpec((width, d), lambda i: (0, 0))
    return _pallas(
        body, name="mix_out_bwd", args=[dx2, o, yp, ya, ona, woa, wob],
        out_shape=[jax.ShapeDtypeStruct((t, width), F32), jax.ShapeDtypeStruct((t, width), BF16),
                   jax.ShapeDtypeStruct((t // ATT_SUB, 8, ATT_SUB), F32), jax.ShapeDtypeStruct((width, d), F32),
                   jax.ShapeDtypeStruct((width, d), F32), jax.ShapeDtypeStruct((nt, 1, width), F32)],
        grid=(nt,),
        in_specs=[row, half, half, half, pl.BlockSpec((1, width), lambda i: (0, 0)), wspec, wspec],
        out_specs=[half, half, pl.BlockSpec((tm // ATT_SUB, 8, ATT_SUB), lambda i: (i, 0, 0)), wspec, wspec,
                   pl.BlockSpec((None, 1, width), lambda i: (i, 0, 0))], plan=plan)


def _mix_in_bwd(dx2, x1, gain, hm, dpv, dqh, q, dkh, k, dv, df, qn, kn, wt):
    t, d = x1.shape
    width = q.shape[1]
    pool_width = dpv.shape[1]
    tm = min(512, t)
    nt = t // tm
    scale = HEAD_DIM ** -0.5
    c_q, c_k, c_v = pool_width, pool_width + width, pool_width + 2 * width
    c_f = c_v + width

    def body(dx2_ref, x_ref, g_ref, hm_ref, dpv_ref, dqh_ref, q_ref, dkh_ref, k_ref, dv_ref, df_ref, qn_ref, kn_ref,
             wt_ref, dx_ref, dxh_ref, dwt_ref, dg_ref, dqn_ref, dkn_ref):
        @pl.when(pl.program_id(0) == 0)
        def _():
            dwt_ref[...] = jnp.zeros_like(dwt_ref)

        lo = _head_masks()
        for part, rows in enumerate(_row_halves(tm)):
            def put(ref, sl, value):
                ref[:, sl] = value if part == 0 else ref[:, sl] + value

            hm = hm_ref[rows, :]
            pieces = [(0, dpv_ref[rows, :])]
            for c0, raw_ref, dh_ref, n_ref, dn_ref, mul in ((c_q, q_ref, dqh_ref, qn_ref, dqn_ref, scale),
                                                           (c_k, k_ref, dkh_ref, kn_ref, dkn_ref, 1.0)):
                cols = []
                for blk in range(width // LANES):
                    sl = slice(blk * LANES, (blk + 1) * LANES)
                    xb = raw_ref[rows, sl]
                    gb = dh_ref[rows, sl] * mul
                    r = _head_rms(xb, lo)
                    xh = xb * r
                    dyg = gb * n_ref[:, sl]
                    cols.append((r * (dyg - xh * _head_mean(dyg * xh, lo))).astype(BF16))
                    put(dn_ref, sl, jnp.sum(gb * xh, axis=0, keepdims=True))
                pieces.append((c0, jnp.concatenate(cols, axis=1)))
            pieces.append((c_v, dv_ref[rows, :]))
            pieces.append((c_f, df_ref[rows, :].astype(BF16)))
            dhm = jnp.zeros((tm // 2, d), F32)
            for c0, piece in pieces:
                dwt_ref[c0:c0 + piece.shape[1], :] += _dot_tn(piece, hm)
                dhm = dhm + _dot(piece, wt_ref[c0:c0 + piece.shape[1], :])
            xf = x_ref[rows, :]
            dxn, dgr = _rms_bwd(xf, _rms(xf), g_ref[...], dhm)
            dx = dx2_ref[rows, :] + dxn
            dx_ref[rows, :] = dx
            dxh_ref[rows, :] = (0.5 * dx).astype(BF16)
            put(dg_ref, slice(None), jnp.sum(dgr, axis=0, keepdims=True))

    row = pl.BlockSpec((tm, d), lambda i: (i, 0))
    half = pl.BlockSpec((tm, width), lambda i: (i, 0))
    const = lambda shape: pl.BlockSpec(shape, lambda i: (0, 0))
    pvec = lambda n: pl.BlockSpec((None, 1, n), lambda i: (i, 0, 0))
    return pl.pallas_call(
        body,
        out_shape=[jax.ShapeDtypeStruct((t, d), F32), jax.ShapeDtypeStruct((t, d), BF16), jax.ShapeDtypeStruct(wt.shape, F32),
                   jax.ShapeDtypeStruct((nt, 1, d), F32),
                   jax.ShapeDtypeStruct((nt, 1, width), F32), jax.ShapeDtypeStruct((nt, 1, width), F32)],
        grid=(nt,),
        in_specs=[row, row, const((1, d)), row, pl.BlockSpec((tm, pool_width), lambda i: (i, 0)), half, half, half, half,
                  half, pl.BlockSpec((tm, LANES), lambda i: (i, 0)), const((1, width)), const((1, width)),
                  const(wt.shape)],
        out_specs=[row, row, const(wt.shape), pvec(d), pvec(width), pvec(width)],
        compiler_params=_params(), name="mix_in_bwd",
    )(dx2, x1, gain, hm, dpv, dqh, q, dkh, k, dv, df, qn, kn, wt)


def _mesh_pos():
    return lax.axis_index("x"), lax.axis_index("y"), lax.axis_index("c")


def _other_chips(x, y):
    return [(1 - x, y), (x, 1 - y), (1 - x, 1 - y)]


def _remote(src, dst, send_sem, recv_sem, device):
    return pltpu.make_async_remote_copy(src_ref=src, dst_ref=dst, send_sem=send_sem, recv_sem=recv_sem,
                                        device_id=device, device_id_type=pl.DeviceIdType.MESH)


def _half_rows(n_rows, which):
    half = n_rows // 2
    return pl.ds(pl.multiple_of(which * half, 8), half)


def _row_block(rows, cols, itemsize=4):
    rb = rows
    while rb * cols * itemsize > (1 << 20) and rb % 32 == 0:
        rb //= 2
    return rb


def _place_cast(ws, chip, tag):
    n = len(ws)
    rows, cols = ws[0].shape
    rb = _row_block(rows, cols)

    def body(k_ref, *refs):
        for w_ref, o_ref in zip(refs[:n], refs[n:]):
            o_ref[...] = w_ref[...].astype(BF16)

    return pl.pallas_call(
        body, out_shape=[jax.ShapeDtypeStruct((N_CHIPS, rows, cols), BF16)] * n,
        grid_spec=pltpu.PrefetchScalarGridSpec(
            num_scalar_prefetch=1, grid=(rows // rb,),
            in_specs=[pl.BlockSpec((rb, cols), lambda i, k: (i, 0))] * n,
            out_specs=[pl.BlockSpec((None, rb, cols), lambda i, k: (k[0], i, 0))] * n),
        compiler_params=_params(), name="place_" + tag,
    )(chip, *ws)


class _Plan:
    def __init__(self, ins, outs, alias, sems, start, finish, middle=None, middle_at=(3, 4)):
        self.ins, self.outs, self.alias, self.sems = ins, outs, alias, sems
        self.start, self.middle, self.finish, self.middle_at = start, middle, finish, middle_at


def _merge_plans(a, b):
    ni, no, ns = len(a.ins), len(a.outs), len(a.sems)
    alias = dict(a.alias)
    alias.update({ni + i: no + o for i, o in b.alias.items()})

    def both(which):
        stage_a, stage_b = getattr(a, which), getattr(b, which)
        if stage_a is None and stage_b is None:
            return None

        def run(ins, outs, sems):
            if stage_a is not None:
                stage_a(ins[:ni], outs[:no], sems[:ns])
            if stage_b is not None:
                stage_b(ins[ni:], outs[no:], sems[ns:])
        return run

    return _Plan(list(a.ins) + list(b.ins), list(a.outs) + list(b.outs), alias, list(a.sems) + list(b.sems),
                 both("start"), both("finish"), both("middle"), a.middle_at if a.middle is not None else b.middle_at)


def _run_plan(plan, name):
    n_in, n_out = len(plan.ins), len(plan.outs)

    def body(*refs):
        parts = refs[:n_in], refs[n_in:n_in + n_out], refs[n_in + n_out:]
        plan.start(*parts)
        if plan.middle is not None:
            plan.middle(*parts)
        plan.finish(*parts)

    return pl.pallas_call(
        body, out_shape=plan.outs, in_specs=[ANY] * n_in, out_specs=[ANY] * n_out, scratch_shapes=plan.sems,
        input_output_aliases=plan.alias, name=name,
    )(*plan.ins)


def _pallas(body, *, name, args, in_specs, out_shape, out_specs, grid, scratch_shapes=(), plan=None, aliases=None):
    n_in, n_out, n_scr = len(args), len(out_shape), len(scratch_shapes)
    plan = plan or _Plan([], [], {}, [], None, None)
    p_in, p_out = len(plan.ins), len(plan.outs)

    def carrying(*refs):
        ins, p_ins = refs[:n_in], refs[n_in:n_in + p_in]
        o0 = n_in + p_in
        outs, p_outs = refs[o0:o0 + n_out], refs[o0 + n_out:o0 + n_out + p_out]
        s0 = o0 + n_out + p_out
        scr, p_sems = refs[s0:s0 + n_scr], refs[s0 + n_scr:]
        ids = [pl.program_id(a) for a in range(len(grid))]

        if plan.start is not None:
            @pl.when(functools.reduce(jnp.logical_and, [i == 0 for i in ids]))
            def _():
                plan.start(p_ins, p_outs, p_sems)

        body(*ins, *outs, *scr)

        if plan.middle is not None:
            step, n_steps = 0, 1
            for i, g in zip(ids, grid):
                step, n_steps = step * g + i, n_steps * g

            @pl.when(step == (plan.middle_at[0] * n_steps) // plan.middle_at[1])
            def _():
                plan.middle(p_ins, p_outs, p_sems)

        if plan.finish is not None:
            @pl.when(functools.reduce(jnp.logical_and, [i == g - 1 for i, g in zip(ids, grid)]))
            def _():
                plan.finish(p_ins, p_outs, p_sems)

    aliases = dict(aliases or {})
    aliases.update({n_in + i: n_out + o for i, o in plan.alias.items()})
    res = pl.pallas_call(
        carrying, out_shape=list(out_shape) + list(plan.outs), grid=grid,
        in_specs=list(in_specs) + [ANY] * p_in, out_specs=list(out_specs) + [ANY] * p_out,
        scratch_shapes=list(scratch_shapes) + list(plan.sems),
        input_output_aliases=aliases, compiler_params=_params(), name=name,
    )(*args, *plan.ins)
    return list(res[:n_out]), list(res[n_out:])


def _plan_gather(stacks):
    n = len(stacks)
    relations = range(3)

    def ici_copies(outs, sems):
        x, y, c = _mesh_pos()
        chips = _other_chips(x, y)
        cps = []
        for w in range(n):
            own = outs[w].at[2 * x + y, _half_rows(stacks[w].shape[1], c)]
            cps += [_remote(own, own, sems[0].at[w, j], sems[1].at[w, j], (*chips[j], c)) for j in relations]
        return cps

    def start(ins, outs, sems):
        for cp in ici_copies(outs, sems):
            cp.start()

    def forwards(outs, sems, core):
        x, y, c = _mesh_pos()
        slots = [2 * cx + cy for cx, cy in _other_chips(x, y)]
        cps = []
        for w in range(n):
            rows = _half_rows(stacks[w].shape[1], core)
            for j in relations:
                landed = outs[w].at[slots[j], rows]
                cps.append((_remote(landed, landed, sems[0].at[w, j], sems[1].at[w, j], (x, y, 1 - c)),
                            _remote(landed, landed, sems[2].at[w, j], sems[3].at[w, j], (x, y, 1 - c))))
        return cps

    def middle(ins, outs, sems):
        c = _mesh_pos()[2]
        for arrival, forward in forwards(outs, sems, c):
            arrival.wait_recv()
            forward.start()

    def finish(ins, outs, sems):
        c = _mesh_pos()[2]
        for _, forward in forwards(outs, sems, 1 - c):
            forward.wait_recv()
        for cp in ici_copies(outs, sems) + [forward for _, forward in forwards(outs, sems, c)]:
            cp.wait_send()

    return _Plan(stacks, [jax.ShapeDtypeStruct(s.shape, s.dtype) for s in stacks], {w: w for w in range(n)},
                 [pltpu.SemaphoreType.DMA((n, 3))] * 4, start, finish, middle)


RELAY_SEMS = [pltpu.SemaphoreType.DMA((3, 2))] * 4 + [pltpu.SemaphoreType.DMA((3, 3))] * 2
RELAY_STAGES = ("send", "pass on", "x neighbour", "y neighbour", "diagonal", "end")


def _relay_gather_stage(stage, outs, sems):
    assert stage in RELAY_STAGES
    send, recv, relay_send, relay_recv, d2d_send, d2d_recv = sems
    n = len(outs)
    rh = outs[0].shape[1] // 2
    mx, my, c = _mesh_pos()
    sibling = (mx, my, 1 - c)
    near = [(1 - mx, my), (mx, 1 - my)]
    slots = [2 * cx + cy for cx, cy in near] + [2 * (1 - mx) + (1 - my)]

    def piece(w, slot, core, quarter=None):
        if quarter is None:
            return outs[w].at[slot, _half_rows(2 * rh, core)]
        return outs[w].at[slot, pl.ds(pl.multiple_of(core * rh + quarter * (rh // 2), 8), rh // 2)]

    def to_near(w, j):
        own = piece(w, 2 * mx + my, c)
        return _remote(own, own, send.at[w, j], recv.at[w, j], (*near[j], c))

    def from_near(w, j):
        landed = piece(w, slots[j], c)
        return _remote(landed, landed, send.at[w, j], recv.at[w, j], sibling)

    def onward(w, j, slot):
        part = piece(w, slot, c, quarter=j)
        return _remote(part, part, relay_send.at[w, j], relay_recv.at[w, j], (*near[1 - j], c))

    def to_sibling(w, j, core):
        landed = piece(w, slots[j], core)
        return _remote(landed, landed, d2d_send.at[w, j], d2d_recv.at[w, j], sibling)

    if stage == "send":
        for w in range(n):
            for j in range(2):
                to_near(w, j).start()
    elif stage == "pass on":
        for w in range(n):
            for j in range(2):
                from_near(w, j).wait_recv()
                onward(w, j, slots[j]).start()
                to_sibling(w, j, c).start()
    elif stage in ("x neighbour", "y neighbour"):
        for w in range(n):
            to_sibling(w, ("x neighbour", "y neighbour").index(stage), 1 - c).wait_recv()
    elif stage == "diagonal":
        for w in range(n):
            for j in range(2):
                onward(w, j, slots[2]).wait_recv()
            to_sibling(w, 2, c).start()
        for w in range(n):
            to_sibling(w, 2, 1 - c).wait_recv()
    else:
        for w in range(n):
            for j in range(2):
                to_near(w, j).wait_send()
                onward(w, j, slots[j]).wait_send()
            for j in range(3):
                to_sibling(w, j, c).wait_send()


def _plan_gather_relay(stacks):
    def stages(which):
        def run(ins, outs, sems):
            for stage in which:
                _relay_gather_stage(stage, outs, sems)
        return run

    return _Plan(stacks, [jax.ShapeDtypeStruct(s.shape, s.dtype) for s in stacks], {w: w for w in range(len(stacks))},
                 RELAY_SEMS, stages(RELAY_STAGES[:1]), stages(RELAY_STAGES[2:]), stages(RELAY_STAGES[1:2]), middle_at=(5, 8))


def _ffn_fwd_gathering(x, gain, stacks, order, later):
    t, d = x.shape
    nch, fc, _ = stacks[0].shape
    n = len(stacks)
    tm = min(FFN_STAGED_TILE, t)
    nt = t // tm
    p_in, p_out = len(later.ins), len(later.outs)
    relay = _relay_gather_stage

    def body(order_ref, x_ref, g_ref, *refs):
        later_in, refs = refs[n:n + p_in], refs[n + p_in:]
        o_ref, h_ref, a_ref, b_ref, s_ref = refs[:5]
        stack_refs, later_out, refs = refs[5:5 + n], refs[5 + n:5 + n + p_out], refs[5 + n + p_out:]
        w_ref, hs_ref, acc_ref, w_sem = refs[:4]
        relay_sems, later_sems = refs[4:10], refs[10:]
        k, i = pl.program_id(0), pl.program_id(1)
        tile = pl.ds(pl.multiple_of(i * tm, tm), tm)

        @pl.when(i == 0)
        def _():
            for chunk, stages in enumerate([("send",), ("pass on", "x neighbour"), ("y neighbour",), ("diagonal",)]):
                @pl.when(k == chunk)
                def _():
                    for stage in stages:
                        relay(stage, stack_refs, relay_sems)
                    if chunk == 1 and later.start is not None:
                        later.start(later_in, later_out, later_sems)
            loads = [pltpu.make_async_copy(stack_refs[w].at[order_ref[k]], w_ref.at[w], w_sem.at[w]) for w in range(n)]
            for cp in loads:
                cp.start()
            for cp in loads:
                cp.wait()

        @pl.when(k == 0)
        def _():
            xf = x_ref[...]
            hb = ((xf * _rms(xf)) * g_ref[...]).astype(BF16)
            h_ref[...] = hb
            hs_ref[tile, :] = hb
            acc_ref[tile, :] = jnp.zeros((tm, d), F32)

        for rows in _row_halves(tm):
            part = pl.ds(pl.multiple_of(i * tm + rows.start, tm // 2), tm // 2)
            h = hs_ref[part, :]
            a = _dot_nt(h, w_ref[0])
            b = _dot_nt(h, w_ref[1])
            sb = ((a * (0.5 * jnp.tanh(0.5 * a) + 0.5)) * b).astype(BF16)
            a_ref[rows, :] = a.astype(BF16)
            b_ref[rows, :] = b.astype(BF16)
            s_ref[rows, :] = sb
            acc_ref[part, :] += _dot(sb, w_ref[2])

        @pl.when(k == nch - 1)
        def _():
            o_ref[...] = x_ref[...] + 0.5 * acc_ref[tile, :]

        if later.middle is not None:
            @pl.when((k == nch - 1) & (i == nt // 2))
            def _():
                later.middle(later_in, later_out, later_sems)

        @pl.when((k == nch - 1) & (i == nt - 1))
        def _():
            relay("end", stack_refs, relay_sems)
            if later.finish is not None:
                later.finish(later_in, later_out, later_sems)

    ends = lambda k, i: jnp.where((k == 0) | (k == nch - 1), i, 0)
    act = pl.BlockSpec((None, tm, fc), lambda k, i, order: (order[k], i, 0))
    out_shape = [jax.ShapeDtypeStruct((t, d), F32), jax.ShapeDtypeStruct((t, d), BF16)]
    out_shape += [jax.ShapeDtypeStruct((nch, t, fc), BF16)] * 3
    out_shape += [jax.ShapeDtypeStruct(s.shape, s.dtype) for s in stacks] + list(later.outs)
    aliases = {3 + w: 5 + w for w in range(n)}
    aliases.update({3 + n + i: 5 + n + o for i, o in later.alias.items()})
    res = pl.pallas_call(
        body, out_shape=out_shape,
        grid_spec=pltpu.PrefetchScalarGridSpec(
            num_scalar_prefetch=1, grid=(nch, nt),
            in_specs=[pl.BlockSpec((tm, d), lambda k, i, order: (ends(k, i), 0)),
                      pl.BlockSpec((1, d), lambda k, i, order: (0, 0))] + [ANY] * (n + p_in),
            out_specs=[pl.BlockSpec((tm, d), lambda k, i, order: (jnp.where(k == nch - 1, i, 0), 0)),
                       pl.BlockSpec((tm, d), lambda k, i, order: (jnp.where(k == 0, i, nt - 1), 0)),
                       act, act, act] + [ANY] * (n + p_out),
            scratch_shapes=[pltpu.VMEM((n, fc, d), BF16), pltpu.VMEM((t, d), BF16), pltpu.VMEM((t, d), F32),
                            pltpu.SemaphoreType.DMA((n,))] + RELAY_SEMS + list(later.sems)),
        input_output_aliases=aliases, compiler_params=_params(), name="ffn_fwd",
    )(order, x, gain, *stacks, *later.ins)
    return list(res[:5]), list(res[5:5 + n]), list(res[5 + n:])


def _plan_sibling_halves(gs):
    n = len(gs)

    def copies(ins, outs, sems):
        x, y, c = _mesh_pos()
        return [_remote(ins[w].at[:, _half_rows(gs[w].shape[1], 1 - c), :], outs[w], sems[0].at[w], sems[1].at[w],
                        (x, y, 1 - c)) for w in range(n)]

    def start(ins, outs, sems):
        for cp in copies(ins, outs, sems):
            cp.start()

    def finish(ins, outs, sems):
        for cp in copies(ins, outs, sems):
            cp.wait()

    return _Plan(gs, [jax.ShapeDtypeStruct((g.shape[0], g.shape[1] // 2, g.shape[2]), g.dtype) for g in gs], {},
                 [pltpu.SemaphoreType.DMA((n,))] * 2, start, finish)


def _plan_chip_exchange(ps):
    n = len(ps)

    def copies(ins, outs, sems):
        x, y, c = _mesh_pos()
        return [_remote(ins[w].at[2 * cx + cy], outs[w].at[j], sems[0].at[w, j], sems[1].at[w, j], (cx, cy, c))
                for w in range(n) for j, (cx, cy) in enumerate(_other_chips(x, y))]

    def start(ins, outs, sems):
        for cp in copies(ins, outs, sems):
            cp.start()

    def finish(ins, outs, sems):
        for cp in copies(ins, outs, sems):
            cp.wait()

    return _Plan(ps, [jax.ShapeDtypeStruct((3,) + p.shape[1:], p.dtype) for p in ps], {},
                 [pltpu.SemaphoreType.DMA((n, 3))] * 2, start, finish)


def _plan_sibling_share(gs):
    n = len(gs)

    def copies(outs, sems, which):
        x, y, c = _mesh_pos()
        cps = []
        for w in range(n):
            rows = outs[w].at[_half_rows(gs[w].shape[0], c if which == "mine" else 1 - c)]
            cps.append(_remote(rows, rows, sems[0].at[w], sems[1].at[w], (x, y, 1 - c)))
        return cps

    def start(ins, outs, sems):
        for cp in copies(outs, sems, "mine"):
            cp.start()

    def finish(ins, outs, sems):
        for cp in copies(outs, sems, "mine"):
            cp.wait_send()
        for cp in copies(outs, sems, "theirs"):
            cp.wait_recv()

    return _Plan(gs, [jax.ShapeDtypeStruct(g.shape, g.dtype) for g in gs], {w: w for w in range(n)},
                 [pltpu.SemaphoreType.DMA((n,))] * 2, start, finish)


def _same_shape_groups(arrays):
    groups = {}
    for i, a in enumerate(arrays):
        groups.setdefault(a.shape, []).append(i)
    return list(groups.values())


def _add_sibling(gs, r1s, ids, tag):
    n = len(gs)
    nch, rh, cols = r1s[0].shape

    def body(ids_ref, *refs):
        for g_ref, r_ref, o_ref in zip(refs[:n], refs[n:2 * n], refs[2 * n:]):
            o_ref[...] = (g_ref[...] + r_ref[...]).astype(BF16)

    blk = lambda fn: pl.BlockSpec((None, rh, cols), fn)
    return pl.pallas_call(
        body, out_shape=[jax.ShapeDtypeStruct(r1s[0].shape, BF16)] * n,
        grid_spec=pltpu.PrefetchScalarGridSpec(
            num_scalar_prefetch=1, grid=(nch,),
            in_specs=[blk(lambda k, ids: (k, ids[1], 0))] * n + [blk(lambda k, ids: (k, 0, 0))] * n,
            out_specs=[blk(lambda k, ids: (k, 0, 0))] * n),
        compiler_params=_params(), name="add_sibling_" + tag,
    )(ids, *gs, *r1s)


def _add_chips(gs, r1s, r2s, ids, tag):
    n = len(gs)
    _, rh, cols = r1s[0].shape
    nb = 2 if rh % 32 == 0 else 1
    rb = rh // nb

    def body(ids_ref, *refs):
        for g_ref, r1_ref, r2_ref, o_ref in zip(refs[:n], refs[n:2 * n], refs[2 * n:3 * n], refs[3 * n:]):
            own = g_ref[...] + r1_ref[...]
            o_ref[...] = ((own + r2_ref[0].astype(F32)) + r2_ref[1].astype(F32)) + r2_ref[2].astype(F32)

    return pl.pallas_call(
        body, out_shape=[jax.ShapeDtypeStruct((2 * rh, cols), F32)] * n,
        grid_spec=pltpu.PrefetchScalarGridSpec(
            num_scalar_prefetch=1, grid=(nb,),
            in_specs=[pl.BlockSpec((None, rb, cols), lambda i, ids: (ids[0], ids[1] * nb + i, 0))] * n
            + [pl.BlockSpec((None, rb, cols), lambda i, ids: (ids[0], i, 0))] * n
            + [pl.BlockSpec((3, rb, cols), lambda i, ids: (0, i, 0))] * n,
            out_specs=[pl.BlockSpec((rb, cols), lambda i, ids: (ids[1] * nb + i, 0))] * n),
        compiler_params=_params(), name="add_chips_" + tag,
    )(ids, *gs, *r1s, *r2s)


VEC_ROWS = 8


N_DEVICES = 8


def _small_pack(part, d, width):
    names = ("ffn1_norm", "mix_norm", "ffn2_norm", "pool_scale", "out_norm_pool", "out_norm_attn", "qn", "kn", "b_forget",
             "pool_w", "loss")
    args = [part[k] for k in names]
    pw_shape = part["pool_w"].shape[1:]

    def body(g1_ref, gm_ref, g2_ref, ps_ref, onp_ref, ona_ref, qn_ref, kn_ref, bf_ref, pw_ref, loss_ref, vbuf, pbuf):
        lo = _head_masks()

        def fold_heads(ref):
            v = jnp.sum(ref[...], axis=0)
            acc = jnp.zeros((VEC_ROWS, LANES), F32)
            for blk in range(width // LANES):
                vb = jnp.broadcast_to(v[:, blk * LANES:(blk + 1) * LANES], (VEC_ROWS, LANES))
                acc = acc + vb + pltpu.roll(vb, HEAD_DIM, 1)
            return jnp.where(lo, acc, 0.0)[0:1, :]

        vbuf[0] = jnp.zeros((VEC_ROWS, d), F32)
        vbuf[0, 0:1, :] = jnp.sum(g1_ref[...], axis=0)
        vbuf[0, 1:2, :] = jnp.sum(gm_ref[...], axis=0)
        vbuf[0, 2:3, :] = jnp.sum(g2_ref[...], axis=0)
        vbuf[0, 5:6, 0:LANES] = jnp.sum(loss_ref[...], axis=0)[0:1, :]
        vbuf[0, 3:4, 0:width] = jnp.sum(ps_ref[...], axis=0)
        vbuf[0, 3:4, width:2 * width] = jnp.sum(onp_ref[...], axis=0)
        vbuf[0, 4:5, 0:width] = jnp.sum(ona_ref[...], axis=0)
        vbuf[0, 4:5, width:width + LANES] = fold_heads(qn_ref)
        vbuf[0, 4:5, width + LANES:width + 2 * LANES] = fold_heads(kn_ref)
        vbuf[0, 4:5, width + 2 * LANES:width + 3 * LANES] = jnp.sum(bf_ref[...], axis=0)
        pbuf[0] = jnp.sum(pw_ref[...], axis=0)

    return pl.pallas_call(
        body, out_shape=[jax.ShapeDtypeStruct((N_DEVICES, VEC_ROWS, d), F32), jax.ShapeDtypeStruct((N_DEVICES,) + pw_shape, F32)],
        in_specs=[VM] * len(args), out_specs=[VM, VM], compiler_params=_params(), name="small_pack",
    )(*args)


def _plan_all_to_all(stacks):
    n = len(stacks)

    def copies(outs, sems):
        x, y, c = _mesh_pos()
        cps = []
        for r in range(1, N_DEVICES):
            peer = (x if not r & 4 else 1 - x, y if not r & 2 else 1 - y, c if not r & 1 else 1 - c)
            cps += [_remote(outs[w].at[0], outs[w].at[r], sems[0].at[w, r - 1], sems[1].at[w, r - 1], peer) for w in range(n)]
        return cps

    def start(ins, outs, sems):
        for cp in copies(outs, sems):
            cp.start()

    def finish(ins, outs, sems):
        for cp in copies(outs, sems):
            cp.wait()

    return _Plan(stacks, [jax.ShapeDtypeStruct(s.shape, s.dtype) for s in stacks], {w: w for w in range(n)},
                 [pltpu.SemaphoreType.DMA((n, N_DEVICES - 1))] * 2, start, finish)


def _small_sum(vstack, pstack, me):
    def body(me_ref, vbuf, pbuf, vec_ref, pw_ref):
        vec = vbuf[me_ref[0]]
        pw = pbuf[me_ref[0]]
        for dev in range(1, N_DEVICES):
            vec = vec + vbuf[jnp.bitwise_xor(me_ref[0], dev)]
            pw = pw + pbuf[jnp.bitwise_xor(me_ref[0], dev)]
        vec_ref[...] = vec
        pw_ref[...] = pw

    full = lambda s: pl.BlockSpec(s.shape, lambda i, me: (0,) * len(s.shape))
    outs = [jax.ShapeDtypeStruct(vstack.shape[1:], F32), jax.ShapeDtypeStruct(pstack.shape[1:], F32)]
    return pl.pallas_call(
        body, out_shape=outs,
        grid_spec=pltpu.PrefetchScalarGridSpec(num_scalar_prefetch=1, grid=(1,), in_specs=[full(vstack), full(pstack)],
                                               out_specs=[full(o) for o in outs]),
        compiler_params=_params(), name="small_sum",
    )(me, vstack, pstack)


def _adamw(ws, gs, ms, vs, tag):
    n = len(ws)
    rows, cols = ws[0].shape
    rb = rows
    while rb * cols * 4 * n > (1 << 20) and rb % 16 == 0:
        rb //= 2

    def body(*refs):
        for j in range(n):
            w_ref, g_ref, m_ref, v_ref = (refs[k * n + j] for k in range(4))
            go_ref, d_ref, mo_ref, vo_ref = (refs[(4 + k) * n + j] for k in range(4))
            gv = g_ref[...]
            go_ref[...] = gv
            m2 = ADAM_B1 * m_ref[...] + (1.0 - ADAM_B1) * gv
            v2 = ADAM_B2 * v_ref[...] + (1.0 - ADAM_B2) * (gv * gv)
            m_hat = m2 / (1.0 - ADAM_B1 ** ADAM_STEP)
            v_hat = v2 / (1.0 - ADAM_B2 ** ADAM_STEP)
            d_ref[...] = -ADAM_LR * (m_hat / (jnp.sqrt(v_hat) + ADAM_EPS) + ADAM_WD * w_ref[...])
            mo_ref[...] = m2
            vo_ref[...] = v2

    spec = pl.BlockSpec((rb, cols), lambda i: (i, 0))
    res, _ = _pallas(
        body, name="adamw_" + tag, args=[*ws, *gs, *ms, *vs], out_shape=[jax.ShapeDtypeStruct(ws[0].shape, F32)] * (4 * n),
        grid=(rows // rb,), in_specs=[spec] * (4 * n), out_specs=[spec] * (4 * n))
    return [tuple(res[k * n + j] for k in range(4)) for j in range(n)]


def _pack_vec(p, d, width):
    pad = lambda v: jnp.pad(v, (0, LANES - v.shape[0]))
    row3 = jnp.concatenate([p["pool_scale"], p["out_norm_pool"]])
    row4 = jnp.concatenate([p["out_norm_attn"], pad(p["q_norm"]), pad(p["k_norm"]), pad(p["b_forget"]),
                            jnp.zeros((d - width - 3 * LANES,), F32)])
    rows = [p["ffn1_norm"], p["mix_norm"], p["ffn2_norm"], row3, row4]
    return jnp.pad(jnp.stack(rows), ((0, VEC_ROWS - len(rows)), (0, 0)))


def _unpack_vec(vec, width):
    return dict(ffn1_norm=vec[0], mix_norm=vec[1], ffn2_norm=vec[2], pool_scale=vec[3, :width],
                out_norm_pool=vec[3, width:2 * width], out_norm_attn=vec[4, :width],
                q_norm=vec[4, width:width + HEAD_DIM], k_norm=vec[4, width + LANES:width + LANES + HEAD_DIM],
                b_forget=vec[4, width + 2 * LANES:width + 2 * LANES + N_HEADS])


WEIGHT_NAMES = ("ffn1_norm", "ffn1_w_gate", "ffn1_w_up", "ffn1_w_down", "mix_norm", "w_in", "b_forget", "pool_w",
                "pool_scale", "q_norm", "k_norm", "out_norm_pool", "out_norm_attn", "w_out", "ffn2_norm",
                "ffn2_w_gate", "ffn2_w_up", "ffn2_w_down")
BIG_NAMES = ("ffn1_w_gate", "ffn1_w_up", "ffn1_w_down", "w_in", "w_out", "ffn2_w_gate", "ffn2_w_up", "ffn2_w_down")
TRANSPOSED_NAMES = ("ffn1_w_gate", "ffn1_w_up", "w_in", "ffn2_w_gate", "ffn2_w_up")
FFN1_NAMES = ("ffn1_w_gate", "ffn1_w_up", "ffn1_w_down")
MIX_NAMES = ("w_in", "w_out")
FFN2_NAMES = ("ffn2_w_gate", "ffn2_w_up", "ffn2_w_down")


def kernel(x, ffn1_norm, ffn1_w_gate, ffn1_w_up, ffn1_w_down, mix_norm, w_in, b_forget, pool_w, pool_scale, q_norm, k_norm, out_norm_pool, out_norm_attn, w_out, ffn2_norm, ffn2_w_gate, ffn2_w_up, ffn2_w_down, loss_target, m_ffn1_norm, m_ffn1_w_gate, m_ffn1_w_up, m_ffn1_w_down, m_mix_norm, m_w_in, m_b_forget, m_pool_w, m_pool_scale, m_q_norm, m_k_norm, m_out_norm_pool, m_out_norm_attn, m_w_out, m_ffn2_norm, m_ffn2_w_gate, m_ffn2_w_up, m_ffn2_w_down, v_ffn1_norm, v_ffn1_w_gate, v_ffn1_w_up, v_ffn1_w_down, v_mix_norm, v_w_in, v_b_forget, v_pool_w, v_pool_scale, v_q_norm, v_k_norm, v_out_norm_pool, v_out_norm_attn, v_w_out, v_ffn2_norm, v_ffn2_w_gate, v_ffn2_w_up, v_ffn2_w_down):
    given = dict(locals())
    w = {n: given[n] for n in WEIGHT_NAMES}
    m = {n: given["m_" + n] for n in WEIGHT_NAMES}
    v = {n: given["v_" + n] for n in WEIGHT_NAMES}
    n_batch, seq, d = x.shape
    width = pool_scale.shape[0]
    in_rows = w_in.shape[1]
    in_cols = N_CHIPS * in_rows
    in_pad = -(-in_rows // 32) * 32
    in_cols_pad = in_cols - N_HEADS + LANES

    work = lambda a, n: a.T if n in TRANSPOSED_NAMES else a
    exchanged = lambda a, n: jnp.pad(a, ((0, in_pad - in_rows), (0, 0))) if n == "w_in" else a

    mesh_x, mesh_y, mesh_c = _mesh_pos()
    ids = jnp.stack([2 * mesh_x + mesh_y, mesh_c]).astype(jnp.int32)

    row = lambda a: a.reshape(1, -1)
    g1, gm, g2, ps, onp, ona = (row(a) for a in (ffn1_norm, mix_norm, ffn2_norm, pool_scale, out_norm_pool, out_norm_attn))
    qn, kn = row(jnp.tile(q_norm, N_HEADS)), row(jnp.tile(k_norm, N_HEADS))
    bf = row(jnp.pad(b_forget, (0, LANES - N_HEADS)))
    pwb = pool_w.astype(BF16)
    xf, tgt = x.reshape(n_batch * seq, d), loss_target.reshape(n_batch * seq, d)

    def grouped(call, names, *lists):
        out = [None] * len(names)
        for idx in _same_shape_groups(lists[0]):
            res = call(*[[lst[i] for i in idx] for lst in lists], names[idx[0]])
            for i, r in zip(idx, res):
                out[i] = r
        return out

    placed = dict(zip(BIG_NAMES, grouped(lambda ws, tag: _place_cast(ws, ids, tag), BIG_NAMES,
                                         [exchanged(work(w[n], n), n) for n in BIG_NAMES])))
    landing = jnp.stack([2 * cx + cy for cx, cy in [(mesh_x, mesh_y)] + _other_chips(mesh_x, mesh_y)]).astype(jnp.int32)
    (x1, h1, a1, b1, s1), (wg1, wu1, wd1), (w_in_all, w_out_all) = _ffn_fwd_gathering(
        xf, g1, [placed[n] for n in FFN1_NAMES], landing, _plan_gather([placed[n] for n in MIX_NAMES]))
    w_in_t = jnp.pad(w_in_all[:, :in_rows].reshape(in_cols, d), ((0, in_cols_pad - in_cols), (0, 0)))
    w_out_full = w_out_all.reshape(N_CHIPS * w_out.shape[0], d)
    woa, wob = w_out_full[:width], w_out_full[width:]

    hm, pv, q, k, qh, kh, vb, f = _mix_proj(x1, gm, w_in_t, qn, kn, width, width)
    qa, ka = _forget_prefix(f, bf, qh, kh, n_batch, seq)
    yp = _pool_fwd(pv, pwb, ps, onp, n_batch, seq)
    (o, lse), (wg2, wu2, wd2) = _attn_fwd(qa, ka, vb, n_batch, seq, plan=_plan_gather_relay([placed[n] for n in FFN2_NAMES]))
    x2, ya = _mix_out(x1, yp, o, ona, woa, wob)
    (dy, h2, a2, b2, s2, lpart, dyh), _ = _ffn_fwd(x2, g2, wg2, wu2, wd2, target=tgt)

    def to_chips(gs, arrived, tags):
        return grouped(lambda g, r, tag: _add_sibling(g, r, ids, tag), tags, gs, arrived)

    def own_rows(gs, from_sibling, from_chips, tags):
        return grouped(lambda g, ra, rb, tag: _add_chips(g, ra, rb, ids, tag), tags, gs, from_sibling, from_chips)

    (dx2, da2, db2, dg2), _ = _ffn_bwd_x(dy, x2, g2, a2, b2, wg2, wu2, wd2, "ffn2_bwd_x")
    dw2, _ = _ffn_bwd_w([(da2, h2), (db2, h2), (s2, dyh)], "ffn2_bwd_w")
    (dyp, do, delta, dwoa, dwob, dona), sib2 = _mix_out_bwd(dx2, o, yp, ya, ona, woa, wob, plan=_plan_sibling_halves(dw2))
    dpv, dpw, dps, donp = _pool_bwd(pv, dyp, pwb, ps, onp, n_batch, seq)
    (dqh, dkh, dv, dfq, dfk), chips2 = _attn_bwd(qa, ka, vb, do, lse, delta, n_batch, seq,
                                                 plan=_plan_chip_exchange(to_chips(dw2, sib2, FFN2_NAMES)))
    df, dbf = _forget_bwd(dfq, dfk, f, bf, n_batch, seq)
    dx1, dx1h, dw_in_t, dgm, dqn, dkn = _mix_in_bwd(dx2, x1, gm, hm, dpv, dqh, q, dkh, k, dv, df, qn, kn, w_in_t)
    in_base = [in_rows * k // 8 * 8 for k in range(N_CHIPS)]
    d_w_in = jnp.stack([dw_in_t[b:b + in_pad] for b in in_base])
    d_w_out = jnp.concatenate([dwoa, dwob], axis=0).reshape(N_CHIPS, w_out.shape[0], d)
    dwm = [d_w_in, d_w_out]
    down = FFN1_NAMES[2:]
    dwd1, sibm = _ffn_bwd_w([(s1, dx1h)], "ffn1_bwd_w_down", plan=_plan_sibling_halves(dwm))
    (da1, db1), arrived = _ffn_bwd_a(dx1h, a1, b1, wd1, "ffn1_bwd_a",
                                     plan=_merge_plans(_plan_sibling_halves(dwd1),
                                                       _plan_chip_exchange(to_chips(dwm, sibm, MIX_NAMES))))
    sibd, chipsm = arrived[:1], arrived[1:]
    gate_up = FFN1_NAMES[:2]
    dwgu1, chipsd = _ffn_bwd_w([(da1, h1), (db1, h1)], "ffn1_bwd_w_gate_up",
                               plan=_plan_chip_exchange(to_chips(dwd1, sibd, down)))
    n_tiles = (n_batch * seq) // min(FFN_TILE, n_batch * seq)
    first = max(n_tiles // 4, 1)
    begun, sibgu = _ffn_bwd_h(dx1, xf, g1, da1, db1, wg1, wu1, "ffn1_bwd_h_first", (0, first),
                              plan=_plan_sibling_halves(dwgu1))
    earlier = (own_rows(dwd1, sibd, chipsd, down) + own_rows(dwm, sibm, chipsm, MIX_NAMES)
               + own_rows(dw2, sib2, chips2, FFN2_NAMES))
    (gx, dg1), arrived = _ffn_bwd_h(dx1, xf, g1, da1, db1, wg1, wu1, "ffn1_bwd_h_rest", (first, n_tiles), prev=begun,
                                    plan=_merge_plans(_plan_chip_exchange(to_chips(dwgu1, sibgu, gate_up)),
                                                      _plan_sibling_share(earlier)))
    chipsgu, shared = arrived[:len(gate_up)], arrived[len(gate_up):]

    part = dict(ffn1_norm=dg1, mix_norm=dgm, ffn2_norm=dg2, b_forget=dbf, pool_scale=dps, out_norm_pool=donp,
                out_norm_attn=dona, qn=dqn, kn=dkn, pool_w=dpw.reshape(n_batch, -1, pool_w.shape[-1]), loss=lpart)
    mine = own_rows(dwgu1, sibgu, chipsgu, gate_up)
    last = _run_plan(_merge_plans(_plan_sibling_share(mine), _plan_all_to_all(_small_pack(part, d, width))), "last_exchange")
    vstack, pstack = last[len(mine):]
    g_vec, g_pw = _small_sum(vstack, pstack, jnp.reshape(4 * mesh_x + 2 * mesh_y + mesh_c, (1,)).astype(jnp.int32))
    loss = g_vec[5, 0]
    reduced = dict(zip(gate_up + down + MIX_NAMES + FFN2_NAMES, list(last[:len(mine)]) + list(shared)))
    reduced["w_in"] = lax.dynamic_slice(reduced["w_in"], ((in_rows * ids[0]) % 8, 0), (in_rows, d))

    grads, delta, new_m, new_v = {}, {}, {}, {}
    for names in (FFN2_NAMES, FFN1_NAMES, ("w_in",), ("w_out",)):
        stepped = _adamw([work(w[n], n) for n in names], [reduced[n] for n in names], [work(m[n], n) for n in names],
                         [work(v[n], n) for n in names], names[0])
        for n, step in zip(names, stepped):
            grads[n], delta[n], new_m[n], new_v[n] = (work(a, n) for a in step)
    flat_pw = lambda a: a.reshape(-1, a.shape[-1])
    (_, d_pw, m_pw, v_pw), = _adamw([flat_pw(pool_w)], [g_pw], [flat_pw(m_pool_w)], [flat_pw(v_pool_w)], "pool_w")
    (_, d_vec, m_vec, v_vec), = _adamw([_pack_vec(w, d, width)], [g_vec], [_pack_vec(m, d, width)],
                                       [_pack_vec(v, d, width)], "vectors")
    grads.update(_unpack_vec(g_vec, width), pool_w=g_pw.reshape(pool_w.shape))
    delta.update(_unpack_vec(d_vec, width), pool_w=d_pw.reshape(pool_w.shape))
    new_m.update(_unpack_vec(m_vec, width), pool_w=m_pw.reshape(pool_w.shape))
    new_v.update(_unpack_vec(v_vec, width), pool_w=v_pw.reshape(pool_w.shape))
    return (loss, gx.reshape(x.shape), *[grads[n] for n in WEIGHT_NAMES], *[delta[n] for n in WEIGHT_NAMES],
            *[new_m[n] for n in WEIGHT_NAMES], *[new_v[n] for n in WEIGHT_NAMES])
```

```python
import functools

import jax
import jax.numpy as jnp
from jax import lax
from jax.experimental import pallas as pl
from jax.experimental.pallas import tpu as pltpu

F32 = jnp.float32
BF16 = jnp.bfloat16
EPS = 1e-6
NEG = -1e30
ADAM_LR = 0.001
ADAM_B1 = 0.9
ADAM_B2 = 0.999
ADAM_EPS = 1e-08
ADAM_WD = 0.01
ADAM_STEP = 10
POOL_WINDOWS = (2, 4, 8, 16)
HEAD_DIM = 64
N_HEADS = 8
LANES = 128
N_CHIPS = 4
ATT_BLOCK = 512
ATT_SUB = 128
FFN_TILE = 1024
FFN_STAGED_TILE = 512
VMEM_LIMIT = 62 * 1024 * 1024
ANY = pl.BlockSpec(memory_space=pl.ANY)
VM = pl.BlockSpec(memory_space=pltpu.VMEM)


def _params(**kw):
    return pltpu.CompilerParams(vmem_limit_bytes=VMEM_LIMIT, **kw)


def _dot(a, b):
    return jnp.dot(a, b, preferred_element_type=F32)


def _dot_nt(a, b):
    return lax.dot_general(a, b, (((1,), (1,)), ((), ())), preferred_element_type=F32)


def _dot_tn(a, b):
    return lax.dot_general(a, b, (((0,), (0,)), ((), ())), preferred_element_type=F32)


def _sigmoid(z):
    return 1.0 / (1.0 + jnp.exp(-z))


def _rms(xf):
    return lax.rsqrt(jnp.mean(xf * xf, axis=-1, keepdims=True) + EPS)


def _rms_bwd(xf, r, gain, dh):
    xh = xf * r
    dyg = dh * gain
    return r * (dyg - xh * jnp.mean(dyg * xh, axis=-1, keepdims=True)), dh * xh


def _total(v):
    return jnp.sum(jnp.sum(v, axis=1, keepdims=True), axis=0, keepdims=True)


def _ffn_fwd(x, gain, wg, wu, wd, target=None, plan=None):
    t, d = x.shape
    nch, fc, _ = wg.shape
    tm = min(FFN_TILE, t)
    nt = t // tm
    with_loss = target is not None

    def body(*refs):
        if with_loss:
            x_ref, g_ref, wg_ref, wu_ref, wd_ref, t_ref, o_ref, h_ref, a_ref, b_ref, s_ref, l_ref, oh_ref, acc_ref = refs
        else:
            x_ref, g_ref, wg_ref, wu_ref, wd_ref, o_ref, h_ref, a_ref, b_ref, s_ref, acc_ref = refs
        k = pl.program_id(1)

        @pl.when(k == 0)
        def _():
            xf = x_ref[...]
            h_ref[...] = ((xf * _rms(xf)) * g_ref[...]).astype(BF16)
            acc_ref[...] = jnp.zeros_like(acc_ref)

        for rows in _row_halves(tm):
            h = h_ref[rows, :]
            a = _dot_nt(h, wg_ref[...])
            b = _dot_nt(h, wu_ref[...])
            sb = ((a * (0.5 * jnp.tanh(0.5 * a) + 0.5)) * b).astype(BF16)
            a_ref[rows, :] = a.astype(BF16)
            b_ref[rows, :] = b.astype(BF16)
            s_ref[rows, :] = sb
            acc_ref[rows, :] += _dot(sb, wd_ref[...])

        @pl.when(k == nch - 1)
        def _():
            y = x_ref[...] + 0.5 * acc_ref[...]
            if with_loss:
                e = y - t_ref[...]
                o_ref[...] = e * (1.0 / d)
                oh_ref[...] = (e * (0.5 / d)).astype(BF16)
                l_ref[...] = jnp.broadcast_to(_total(e * e) * (0.5 / d), l_ref.shape)
            else:
                o_ref[...] = y

    row = pl.BlockSpec((tm, d), lambda i, k: (i, 0))
    chunk = pl.BlockSpec((None, fc, d), lambda i, k: (k, 0, 0))
    act = pl.BlockSpec((None, tm, fc), lambda i, k: (k, i, 0))
    in_specs = [row, pl.BlockSpec((1, d), lambda i, k: (0, 0)), chunk, chunk, chunk]
    out_shape = [jax.ShapeDtypeStruct((t, d), F32), jax.ShapeDtypeStruct((t, d), BF16)]
    out_shape += [jax.ShapeDtypeStruct((nch, t, fc), BF16)] * 3
    out_specs = [row, row, act, act, act]
    args = [x, gain, wg, wu, wd]
    if with_loss:
        in_specs.append(row)
        args.append(target)
        out_shape += [jax.ShapeDtypeStruct((nt, 8, LANES), F32), jax.ShapeDtypeStruct((t, d), BF16)]
        out_specs += [pl.BlockSpec((None, 8, LANES), lambda i, k: (i, 0, 0)), row]
    return _pallas(body, name="ffn_fwd_loss" if with_loss else "ffn_fwd", args=args, in_specs=in_specs,
                   out_shape=out_shape, out_specs=out_specs, grid=(nt, nch),
                   scratch_shapes=[pltpu.VMEM((tm, d), F32)], plan=plan)


def _row_halves(n):
    return [slice(0, n // 2), slice(n // 2, n)]


def _swiglu_grads(dyh, a_ref, b_ref, wd_ref, rows):
    ds = _dot_nt(dyh, wd_ref[...])
    av = a_ref[rows, :].astype(F32)
    bv = b_ref[rows, :].astype(F32)
    th = jnp.tanh(0.5 * av)
    sig = 0.5 * th + 0.5
    dab = ((ds * bv) * (sig * (1.0 + av * (0.5 - 0.5 * th)))).astype(BF16)
    return dab, (ds * (av * sig)).astype(BF16)


def _ffn_bwd_a(dyh, a, b, wd, name, plan=None):
    t, d = dyh.shape
    nch, fc, _ = wd.shape
    tm = min(FFN_TILE, t)

    def body(dyh_ref, a_ref, b_ref, wd_ref, da_ref, db_ref):
        for rows in _row_halves(tm):
            da_ref[rows, :], db_ref[rows, :] = _swiglu_grads(dyh_ref[rows, :], a_ref, b_ref, wd_ref, rows)

    act = pl.BlockSpec((None, tm, fc), lambda i, k: (k, i, 0))
    return _pallas(
        body, name=name, args=[dyh, a, b, wd], out_shape=[jax.ShapeDtypeStruct((nch, t, fc), BF16)] * 2, grid=(t // tm, nch),
        in_specs=[pl.BlockSpec((tm, d), lambda i, k: (i, 0)), act, act, pl.BlockSpec((None, fc, d), lambda i, k: (k, 0, 0))],
        out_specs=[act, act], plan=plan)


def _ffn_bwd_h(dy, x, gain, da, db, wg, wu, name, tiles, prev=None, plan=None):
    t, d = x.shape
    nch, fc, _ = wg.shape
    tm = min(FFN_TILE, t)
    nt = t // tm
    t0, t1 = tiles

    def body(*refs):
        dy_ref, x_ref, g_ref, da_ref, db_ref, wg_ref, wu_ref = refs[:7]
        dx_ref, dg_ref, acc_ref = refs[-3:]
        k = pl.program_id(1)

        @pl.when(k == 0)
        def _():
            acc_ref[...] = jnp.zeros_like(acc_ref)

        acc_ref[...] += _dot(da_ref[...], wg_ref[...]) + _dot(db_ref[...], wu_ref[...])

        @pl.when(k == nch - 1)
        def _():
            xf = x_ref[...]
            dxn, dgr = _rms_bwd(xf, _rms(xf), g_ref[...], acc_ref[...])
            dx_ref[...] = dy_ref[...] + dxn
            dg_ref[...] = jnp.sum(dgr, axis=0, keepdims=True)

    row = pl.BlockSpec((tm, d), lambda i, k: (i + t0, 0))
    chunk = pl.BlockSpec((None, fc, d), lambda i, k: (k, 0, 0))
    act = pl.BlockSpec((None, tm, fc), lambda i, k: (k, i + t0, 0))
    args = [dy, x, gain, da, db, wg, wu]
    in_specs = [row, row, pl.BlockSpec((1, d), lambda i, k: (0, 0)), act, act, chunk, chunk]
    aliases = {}
    if prev is not None:
        aliases = {len(args): 0, len(args) + 1: 1}
        args += list(prev)
        in_specs += [ANY, ANY]
    return _pallas(
        body, name=name, args=args, out_shape=[jax.ShapeDtypeStruct((t, d), F32), jax.ShapeDtypeStruct((nt, 1, d), F32)],
        grid=(t1 - t0, nch), in_specs=in_specs,
        out_specs=[row, pl.BlockSpec((None, 1, d), lambda i, k: (i + t0, 0, 0))],
        scratch_shapes=[pltpu.VMEM((tm, d), F32)], plan=plan, aliases=aliases)


def _ffn_bwd_x(dy, x, gain, a, b, wg, wu, wd, name, plan=None):
    t, d = x.shape
    nch, fc, _ = wg.shape
    tm = min(FFN_TILE, t)
    nt = t // tm

    def body(dy_ref, x_ref, g_ref, a_ref, b_ref, wg_ref, wu_ref, wd_ref, dx_ref, da_ref, db_ref, dg_ref, acc_ref):
        k = pl.program_id(1)

        @pl.when(k == 0)
        def _():
            acc_ref[...] = jnp.zeros_like(acc_ref)

        for rows in _row_halves(tm):
            dab, dbb = _swiglu_grads((0.5 * dy_ref[rows, :]).astype(BF16), a_ref, b_ref, wd_ref, rows)
            da_ref[rows, :] = dab
            db_ref[rows, :] = dbb
            acc_ref[rows, :] += _dot(dab, wg_ref[...]) + _dot(dbb, wu_ref[...])

        @pl.when(k == nch - 1)
        def _():
            xf = x_ref[...]
            dxn, dgr = _rms_bwd(xf, _rms(xf), g_ref[...], acc_ref[...])
            dx_ref[...] = dy_ref[...] + dxn
            dg_ref[...] = jnp.sum(dgr, axis=0, keepdims=True)

    row = pl.BlockSpec((tm, d), lambda i, k: (i, 0))
    chunk = pl.BlockSpec((None, fc, d), lambda i, k: (k, 0, 0))
    act = pl.BlockSpec((None, tm, fc), lambda i, k: (k, i, 0))
    return _pallas(
        body, name=name, args=[dy, x, gain, a, b, wg, wu, wd],
        out_shape=[jax.ShapeDtypeStruct((t, d), F32), jax.ShapeDtypeStruct((nch, t, fc), BF16),
                   jax.ShapeDtypeStruct((nch, t, fc), BF16), jax.ShapeDtypeStruct((nt, 1, d), F32)],
        grid=(nt, nch),
        in_specs=[row, row, pl.BlockSpec((1, d), lambda i, k: (0, 0)), act, act, chunk, chunk, chunk],
        out_specs=[row, act, act, pl.BlockSpec((None, 1, d), lambda i, k: (i, 0, 0))],
        scratch_shapes=[pltpu.VMEM((tm, d), F32)], plan=plan)


def _ffn_bwd_w(pairs, name, plan=None):
    n = len(pairs)
    nch, t, fc = pairs[0][0].shape
    d = pairs[0][1].shape[1]
    tm = min(FFN_TILE, t)

    def body(*refs):
        @pl.when(pl.program_id(1) == 0)
        def _():
            for o_ref in refs[2 * n:]:
                o_ref[...] = jnp.zeros_like(o_ref)

        for j in range(n):
            refs[2 * n + j][...] += _dot_tn(refs[j][...], refs[n + j][...])

    row = pl.BlockSpec((tm, d), lambda k, i: (i, 0))
    act = pl.BlockSpec((None, tm, fc), lambda k, i: (k, i, 0))
    chunk = pl.BlockSpec((None, fc, d), lambda k, i: (k, 0, 0))
    return _pallas(body, name=name, args=[p[0] for p in pairs] + [p[1] for p in pairs],
                   out_shape=[jax.ShapeDtypeStruct((nch, fc, d), F32)] * n, grid=(nch, t // tm),
                   in_specs=[act] * n + [row] * n, out_specs=[chunk] * n, plan=plan)


def _head_masks():
    lane = lax.broadcasted_iota(jnp.int32, (1, LANES), 1)
    return lane < HEAD_DIM


def _head_rms(x, lo):
    x2 = x * x
    s0 = jnp.sum(jnp.where(lo, x2, 0.0), axis=1, keepdims=True)
    s1 = jnp.sum(jnp.where(lo, 0.0, x2), axis=1, keepdims=True)
    return jnp.where(lo, lax.rsqrt(s0 * (1.0 / HEAD_DIM) + EPS), lax.rsqrt(s1 * (1.0 / HEAD_DIM) + EPS))


def _head_mean(v, lo):
    s0 = jnp.sum(jnp.where(lo, v, 0.0), axis=1, keepdims=True)
    s1 = jnp.sum(jnp.where(lo, 0.0, v), axis=1, keepdims=True)
    return jnp.where(lo, s0, s1) * (1.0 / HEAD_DIM)


def _mix_proj(x1, gain, wt, qn, kn, pool_width, attn_width):
    t, d = x1.shape
    tm = min(512, t)
    nt = t // tm
    scale = HEAD_DIM ** -0.5
    c_q, c_k, c_v = pool_width, pool_width + attn_width, pool_width + 2 * attn_width
    c_f = c_v + attn_width

    def body(x_ref, g_ref, wt_ref, qn_ref, kn_ref, hm_ref, pv_ref, q_ref, k_ref, qh_ref, kh_ref, vb_ref, f_ref):
        lo = _head_masks()
        for rows in _row_halves(tm):
            xf = x_ref[rows, :]
            hm = ((xf * _rms(xf)) * g_ref[...]).astype(BF16)
            hm_ref[rows, :] = hm
            f_ref[rows, :] = _dot_nt(hm, wt_ref[c_f:c_f + LANES, :])
            pv_ref[rows, :] = _dot_nt(hm, wt_ref[0:pool_width, :])
            vb_ref[rows, :] = _dot_nt(hm, wt_ref[c_v:c_v + attn_width, :]).astype(BF16)
            for c0, raw_ref, hat_ref, n_ref, mul in ((c_q, q_ref, qh_ref, qn_ref, scale), (c_k, k_ref, kh_ref, kn_ref, 1.0)):
                raw = _dot_nt(hm, wt_ref[c0:c0 + attn_width, :])
                raw_ref[rows, :] = raw
                for blk in range(attn_width // LANES):
                    sl = slice(blk * LANES, (blk + 1) * LANES)
                    xb = raw[:, sl]
                    hat_ref[rows, sl] = (((xb * _head_rms(xb, lo)) * n_ref[:, sl]) * mul).astype(BF16)

    row = pl.BlockSpec((tm, d), lambda i: (i, 0))
    half = pl.BlockSpec((tm, attn_width), lambda i: (i, 0))
    const = lambda shape: pl.BlockSpec(shape, lambda i: (0, 0))
    return _pallas(
        body, name="mix_proj", args=[x1, gain, wt, qn, kn],
        out_shape=[jax.ShapeDtypeStruct((t, d), BF16), jax.ShapeDtypeStruct((t, pool_width), F32),
                   jax.ShapeDtypeStruct((t, attn_width), F32), jax.ShapeDtypeStruct((t, attn_width), F32),
                   jax.ShapeDtypeStruct((t, attn_width), BF16), jax.ShapeDtypeStruct((t, attn_width), BF16),
                   jax.ShapeDtypeStruct((t, attn_width), BF16), jax.ShapeDtypeStruct((t, LANES), F32)],
        grid=(nt,),
        in_specs=[row, const((1, d)), const(wt.shape), const((1, attn_width)), const((1, attn_width))],
        out_specs=[row, pl.BlockSpec((tm, pool_width), lambda i: (i, 0)), half, half, half, half, half,
                   pl.BlockSpec((tm, LANES), lambda i: (i, 0))])[0]


def _shift_down(v, dist, row):
    return jnp.where(row >= dist, pltpu.roll(v, dist, 0), 0.0)


def _shift_up(v, dist, row, n):
    return jnp.where(row + dist < n, pltpu.roll(v, n - dist, 0), 0.0)


def _aug_lane(e):
    return HEAD_DIM if e == 0 else 0


def _forget_prefix(f, bias, qh, kh, n_batch, seq):
    def body(f_ref, b_ref, q_ref, k_ref, qa_ref, ka_ref):
        z = f_ref[...] + b_ref[...]
        acc = jnp.minimum(z, 0.0) - jnp.log(1.0 + jnp.exp(-jnp.abs(z)))
        row = lax.broadcasted_iota(jnp.int32, (seq, 1), 0)
        dist = 1
        while dist < seq:
            acc = acc + _shift_down(acc, dist, row)
            dist *= 2
        lane = lax.broadcasted_iota(jnp.int32, (1, LANES), 1)
        for h in range(N_HEADS):
            pair, e = divmod(h, 2)
            a0 = _aug_lane(e)
            own = (lane < HEAD_DIM) if e == 0 else (lane >= HEAD_DIM)
            fh = _pick_lane(acc, h)
            hi = fh.astype(BF16).astype(F32)
            rest = fh - hi
            mid = rest.astype(BF16).astype(F32)
            low = rest - mid
            q_ones = (lane >= a0 + 3) & (lane < a0 + 6)
            k_ones = (lane >= a0) & (lane < a0 + 3)
            q_aug = jnp.where(lane == a0, hi, jnp.where(lane == a0 + 1, mid, jnp.where(lane == a0 + 2, low,
                              jnp.where(q_ones, 1.0, 0.0))))
            k_aug = jnp.where(k_ones, 1.0, jnp.where(lane == a0 + 3, -hi, jnp.where(lane == a0 + 4, -mid,
                              jnp.where(lane == a0 + 5, -low, 0.0))))
            src = slice(pair * LANES, (pair + 1) * LANES)
            dst = slice(h * LANES, (h + 1) * LANES)
            qa_ref[:, dst] = jnp.where(own, q_ref[:, src].astype(F32), q_aug).astype(BF16)
            ka_ref[:, dst] = jnp.where(own, k_ref[:, src].astype(F32), k_aug).astype(BF16)

    width = qh.shape[1]
    tok = pl.BlockSpec((seq, width), lambda b: (b, 0))
    aug = pl.BlockSpec((seq, N_HEADS * LANES), lambda b: (b, 0))
    return pl.pallas_call(
        body, out_shape=[jax.ShapeDtypeStruct((n_batch * seq, N_HEADS * LANES), BF16)] * 2, grid=(n_batch,),
        in_specs=[pl.BlockSpec((seq, LANES), lambda b: (b, 0)), pl.BlockSpec((1, LANES), lambda b: (0, 0)), tok, tok],
        out_specs=[aug, aug], compiler_params=_params(), name="forget_prefix",
    )(f, bias, qh, kh)


def _pool_groups(pv_ref, pw_ref, ps_ref, seq):
    row = lax.broadcasted_iota(jnp.int32, (seq, 1), 0)
    pos = (row + 1).astype(F32)
    out = []
    for g, win in enumerate(POOL_WINDOWS):
        sl = slice(g * LANES, (g + 1) * LANES)
        xg = pv_ref[:, sl]
        acc = xg
        dist = 1
        while dist < win:
            acc = acc + _shift_down(acc, dist, row)
            dist *= 2
        pooled = (acc / jnp.minimum(pos, float(win)) - xg).astype(BF16)
        mixed = _dot(pooled, pw_ref[g])
        out.append((pooled, mixed, mixed * ps_ref[:, sl]))
    return out


def _pool_fwd(pv, pw, ps, onp, n_batch, seq):
    width = pv.shape[1]

    def body(pv_ref, pw_ref, ps_ref, on_ref, y_ref):
        groups = _pool_groups(pv_ref, pw_ref, ps_ref, seq)
        ssq = sum(jnp.sum(ms * ms, axis=1, keepdims=True) for _, _, ms in groups)
        r = lax.rsqrt(ssq * (1.0 / width) + EPS)
        for g, (_, _, ms) in enumerate(groups):
            sl = slice(g * LANES, (g + 1) * LANES)
            y_ref[:, sl] = ((ms * r) * on_ref[:, sl]).astype(BF16)

    return pl.pallas_call(
        body, out_shape=jax.ShapeDtypeStruct((n_batch * seq, width), BF16), grid=(n_batch,),
        in_specs=[pl.BlockSpec((seq, width), lambda b: (b, 0)), pl.BlockSpec(pw.shape, lambda b: (0, 0, 0)),
                  pl.BlockSpec((1, width), lambda b: (0, 0)), pl.BlockSpec((1, width), lambda b: (0, 0))],
        out_specs=pl.BlockSpec((seq, width), lambda b: (b, 0)),
        compiler_params=_params(), name="pool_fwd",
    )(pv, pw, ps, onp)


def _pool_bwd(pv, dyp, pw, ps, onp, n_batch, seq):
    width = pv.shape[1]

    def body(pv_ref, dy_ref, pw_ref, ps_ref, on_ref, dpv_ref, dpw_ref, dps_ref, don_ref):
        groups = _pool_groups(pv_ref, pw_ref, ps_ref, seq)
        ssq = sum(jnp.sum(ms * ms, axis=1, keepdims=True) for _, _, ms in groups)
        r = lax.rsqrt(ssq * (1.0 / width) + EPS)
        mean = sum(jnp.sum((dy_ref[:, g * LANES:(g + 1) * LANES] * on_ref[:, g * LANES:(g + 1) * LANES]) * (ms * r),
                           axis=1, keepdims=True) for g, (_, _, ms) in enumerate(groups)) * (1.0 / width)
        row = lax.broadcasted_iota(jnp.int32, (seq, 1), 0)
        pos = (row + 1).astype(F32)
        for g, (pooled, mixed, ms) in enumerate(groups):
            sl = slice(g * LANES, (g + 1) * LANES)
            dy = dy_ref[:, sl]
            xh = ms * r
            don_ref[:, sl] = jnp.sum(dy * xh, axis=0, keepdims=True)
            dms = r * (dy * on_ref[:, sl] - xh * mean)
            dps_ref[:, sl] = jnp.sum(dms * mixed, axis=0, keepdims=True)
            dmix = (dms * ps_ref[:, sl]).astype(BF16)
            dpw_ref[g] = _dot_tn(pooled, dmix)
            dpool = _dot_nt(dmix, pw_ref[g])
            win = POOL_WINDOWS[g]
            acc = dpool / jnp.minimum(pos, float(win))
            dist = 1
            while dist < win:
                acc = acc + _shift_up(acc, dist, row, seq)
                dist *= 2
            dpv_ref[:, sl] = (acc - dpool).astype(BF16)

    tok = pl.BlockSpec((seq, width), lambda b: (b, 0))
    vec = pl.BlockSpec((1, width), lambda b: (0, 0))
    pvec = pl.BlockSpec((None, 1, width), lambda b: (b, 0, 0))
    return pl.pallas_call(
        body,
        out_shape=[jax.ShapeDtypeStruct((n_batch * seq, width), BF16),
                   jax.ShapeDtypeStruct((n_batch,) + pw.shape, F32),
                   jax.ShapeDtypeStruct((n_batch, 1, width), F32), jax.ShapeDtypeStruct((n_batch, 1, width), F32)],
        grid=(n_batch,),
        in_specs=[tok, tok, pl.BlockSpec(pw.shape, lambda b: (0, 0, 0)), vec, vec],
        out_specs=[tok, pl.BlockSpec((None,) + pw.shape, lambda b: (b, 0, 0, 0)), pvec, pvec],
        compiler_params=_params(), name="pool_bwd",
    )(pv, dyp, pw, ps, onp)


def _pick_lane(tile, idx):
    lane = lax.broadcasted_iota(jnp.int32, (1, LANES), 1)
    return jnp.sum(jnp.where(lane == idx, tile, 0.0), axis=1, keepdims=True)


def _pick_row(tile, idx):
    sub = lax.broadcasted_iota(jnp.int32, (tile.shape[0], 1), 0)
    return jnp.sum(jnp.where(sub == idx, tile, 0.0), axis=0, keepdims=True)


def _put_lane(col, idx):
    lane = lax.broadcasted_iota(jnp.int32, (1, LANES), 1)
    return jnp.where(lane == idx, col, 0.0)


def _head_select(e):
    lo = _head_masks()
    return lo if e == 0 else jnp.logical_not(lo)


def _causal(st, shift):
    row = lax.broadcasted_iota(jnp.int32, st.shape, 0)
    col = lax.broadcasted_iota(jnp.int32, st.shape, 1) + shift
    return jnp.where(col >= row, st, NEG)


def _transpose_blocks(a):
    rows, cols = a.shape
    return jnp.concatenate(
        [jnp.concatenate([a[r:r + LANES, c:c + LANES].T for r in range(0, rows, LANES)], axis=1)
         for c in range(0, cols, LANES)], axis=0)


def _accumulate(ref, value, first):
    @pl.when(first)
    def _():
        ref[...] = value

    @pl.when(jnp.logical_not(first))
    def _():
        ref[...] += value


def _attn_fwd(qa, ka, vb, n_batch, seq, plan=None):
    tq = min(ATT_BLOCK, seq)
    nq, nsub, tk = seq // tq, tq // ATT_SUB, tq
    pairs = vb.shape[1] // LANES

    def body(q_ref, k_ref, v_ref, o_ref, lse_ref, acc_ref):
        i, p = pl.program_id(1), pl.program_id(2)
        row_lo = lax.broadcasted_iota(jnp.int32, (LANES, 1), 0) < HEAD_DIM
        qs = [q_ref[:, e * LANES:(e + 1) * LANES] for e in range(2)]
        acc_ref[...] = jnp.zeros_like(acc_ref)

        def tile(off, stats, diagonal):
            vj = v_ref[pl.ds(off, tk), :]
            new, alphas, pvs = [], [], []
            for e in range(2):
                st = _dot_nt(k_ref[pl.ds(off, tk), e * LANES:(e + 1) * LANES], qs[e])
                if diagonal:
                    st = _causal(st, 0)
                m, l = stats[e]
                m_new = jnp.maximum(m, jnp.max(st, axis=0, keepdims=True))
                alpha = jnp.exp(m - m_new)
                pt = jnp.exp(st - m_new)
                new.append((m_new, alpha * l + jnp.sum(pt, axis=0, keepdims=True)))
                alphas.append(alpha)
                pvs.append(_dot_tn(jnp.where(_head_select(e), vj, jnp.zeros_like(vj)), pt.astype(BF16)))
            acc_ref[...] = acc_ref[...] * jnp.where(row_lo, alphas[0], alphas[1]) + (pvs[0] + pvs[1])
            return tuple(new)

        init = ((jnp.full((1, tq), NEG, F32), jnp.zeros((1, tq), F32)),) * 2
        stats = lax.fori_loop(0, i, lambda j, st: tile(pl.multiple_of(j * tk, tk), st, False), init)
        (m0, l0), (m1, l1) = tile(pl.multiple_of(i * tk, tk), stats, True)
        out_t = acc_ref[...] / jnp.where(row_lo, l0, l1)
        sub = lax.broadcasted_iota(jnp.int32, (8, 1), 0)
        lse0, lse1 = m0 + jnp.log(l0), m1 + jnp.log(l1)
        for a in range(nsub):
            sl = slice(a * ATT_SUB, (a + 1) * ATT_SUB)
            o_ref[sl, :] = out_t[:, sl].T
            rows = jnp.where(sub == 2 * p, lse0[:, sl], 0.0) + jnp.where(sub == 2 * p + 1, lse1[:, sl], 0.0)
            _accumulate(lse_ref.at[a], rows, p == 0)

    return _pallas(
        body, name="attn_fwd", args=[qa, ka, vb],
        out_shape=[jax.ShapeDtypeStruct((n_batch * seq, pairs * LANES), F32),
                   jax.ShapeDtypeStruct((n_batch * seq // ATT_SUB, 8, ATT_SUB), F32)],
        grid=(n_batch, nq, pairs),
        in_specs=[pl.BlockSpec((tq, 2 * LANES), lambda b, i, p: (b * nq + i, p)),
                  pl.BlockSpec((seq, 2 * LANES), lambda b, i, p: (b, p)),
                  pl.BlockSpec((seq, LANES), lambda b, i, p: (b, p))],
        out_specs=[pl.BlockSpec((tq, LANES), lambda b, i, p: (b * nq + i, p)),
                   pl.BlockSpec((nsub, 8, ATT_SUB), lambda b, i, p: (b * nq + i, 0, 0))],
        scratch_shapes=[pltpu.VMEM((LANES, tq), F32)], plan=plan)


def _attn_bwd(qa, ka, vb, do, lse, delta, n_batch, seq, plan=None):
    tq = min(ATT_BLOCK, seq)
    nq, nsub = seq // tq, tq // ATT_SUB
    n_tiles = seq // ATT_SUB
    pairs = vb.shape[1] // LANES

    def body(q_ref, k_ref, v_ref, do_ref, lse_ref, dl_ref, dq_ref, dk_ref, dv_ref, dfq_ref, dfk_ref,
             dq0_ref, dq1_ref, dk0_ref, dk1_ref, dva_ref):
        p = pl.program_id(1)
        dqs, dks = (dq0_ref, dq1_ref), (dk0_ref, dk1_ref)
        for acc in (dk0_ref, dk1_ref, dva_ref):
            acc[...] = jnp.zeros_like(acc)
        dfq_cols = []
        for i in range(nq):
            rows_i = slice(i * tq, (i + 1) * tq)
            qs = [q_ref[rows_i, e * LANES:(e + 1) * LANES] for e in range(2)]
            dov = do_ref[rows_i, :]
            does = [jnp.where(_head_select(e), dov, jnp.zeros_like(dov)) for e in range(2)]
            stat = lambda ref, e: jnp.concatenate([_pick_row(ref[i * nsub + a], 2 * p + e) for a in range(nsub)], axis=1)
            ls, dl = [stat(lse_ref, e) for e in range(2)], [stat(dl_ref, e) for e in range(2)]
            for acc in dqs:
                acc[...] = jnp.zeros_like(acc)

            def tile(off, diagonal, qs=qs, dov=dov, does=does, ls=ls, dl=dl):
                vj = v_ref[pl.ds(off, tq), :]
                for e in range(2):
                    kj = k_ref[pl.ds(off, tq), e * LANES:(e + 1) * LANES]
                    st = _dot_nt(kj, qs[e])
                    if diagonal:
                        st = _causal(st, 0)
                    pt = jnp.exp(st - ls[e])
                    dva_ref[pl.ds(off, tq), :] += _dot(pt.astype(BF16), does[e])
                    dpt = _dot_nt(jnp.where(_head_select(e), vj, jnp.zeros_like(vj)), dov)
                    dst = (pt * (dpt - dl[e])).astype(BF16)
                    dks[e][pl.ds(off, tq), :] += _dot(dst, qs[e])
                    dqs[e][...] += _dot(_transpose_blocks(kj), dst)

            def step(j, carry, tile=tile):
                tile(pl.multiple_of(j * tq, tq), False)
                return carry

            lax.fori_loop(0, i, step, 0)
            tile(i * tq, True)
            dq0, dq1 = _transpose_blocks(dq0_ref[...]), _transpose_blocks(dq1_ref[...])
            dq_ref[rows_i, :] = jnp.where(_head_masks(), dq0, dq1)
            dfq_cols.append(_put_lane(_pick_lane(dq0, _aug_lane(0)), 2 * p) + _put_lane(_pick_lane(dq1, _aug_lane(1)), 2 * p + 1))
        dk0, dk1 = dk0_ref[...], dk1_ref[...]
        dk_ref[...] = jnp.where(_head_masks(), dk0, dk1)
        dv_ref[...] = dva_ref[...].astype(BF16)
        dfk = _put_lane(_pick_lane(dk0, _aug_lane(0) + 3), 2 * p) + _put_lane(_pick_lane(dk1, _aug_lane(1) + 3), 2 * p + 1)
        _accumulate(dfq_ref, jnp.concatenate(dfq_cols, axis=0), p == 0)
        _accumulate(dfk_ref, -dfk, p == 0)

    wide = pl.BlockSpec((seq, 2 * LANES), lambda b, p: (b, p))
    blk = pl.BlockSpec((seq, LANES), lambda b, p: (b, p))
    col = pl.BlockSpec((seq, LANES), lambda b, p: (b, 0))
    stat = pl.BlockSpec((n_tiles, 8, ATT_SUB), lambda b, p: (b, 0, 0))
    f32_blk, acc = jax.ShapeDtypeStruct((n_batch * seq, pairs * LANES), F32), pltpu.VMEM((seq, LANES), F32)
    return _pallas(
        body, name="attn_bwd", args=[qa, ka, vb, do, lse, delta],
        out_shape=[f32_blk, f32_blk, jax.ShapeDtypeStruct((n_batch * seq, pairs * LANES), BF16),
                   jax.ShapeDtypeStruct((n_batch * seq, LANES), F32), jax.ShapeDtypeStruct((n_batch * seq, LANES), F32)],
        grid=(n_batch, pairs), in_specs=[wide, wide, blk, blk, stat, stat], out_specs=[blk, blk, blk, col, col],
        scratch_shapes=[pltpu.VMEM((LANES, tq), F32), pltpu.VMEM((LANES, tq), F32), acc, acc, acc], plan=plan)


def _forget_bwd(dfq, dfk, f, bias, n_batch, seq):
    def body(dfq_ref, dfk_ref, f_ref, b_ref, df_ref, db_ref):
        acc = dfq_ref[...] + dfk_ref[...]
        row = lax.broadcasted_iota(jnp.int32, (seq, 1), 0)
        dist = 1
        while dist < seq:
            acc = acc + _shift_up(acc, dist, row, seq)
            dist *= 2
        df = acc * _sigmoid(-(f_ref[...] + b_ref[...]))
        df_ref[...] = df
        db_ref[...] = jnp.sum(df, axis=0, keepdims=True)

    col = pl.BlockSpec((seq, LANES), lambda b: (b, 0))
    return pl.pallas_call(
        body,
        out_shape=[jax.ShapeDtypeStruct((n_batch * seq, LANES), F32), jax.ShapeDtypeStruct((n_batch, 1, LANES), F32)],
        grid=(n_batch,), in_specs=[col, col, col, pl.BlockSpec((1, LANES), lambda b: (0, 0))],
        out_specs=[col, pl.BlockSpec((None, 1, LANES), lambda b: (b, 0, 0))],
        compiler_params=_params(), name="forget_bwd",
    )(dfq, dfk, f, bias)


def _mix_out(x1, yp, o, ona, woa, wob):
    t, d = x1.shape
    width = o.shape[1]
    tm = min(512, t)

    def body(x_ref, yp_ref, o_ref, on_ref, wa_ref, wb_ref, x2_ref, ya_ref):
        of = o_ref[...]
        ya = ((of * _rms(of)) * on_ref[...]).astype(BF16)
        ya_ref[...] = ya
        x2_ref[...] = x_ref[...] + (_dot(yp_ref[...], wa_ref[...]) + _dot(ya, wb_ref[...]))

    row = pl.BlockSpec((tm, d), lambda i: (i, 0))
    half = pl.BlockSpec((tm, width), lambda i: (i, 0))
    wspec = pl.BlockSpec((width, d), lambda i: (0, 0))
    return pl.pallas_call(
        body, out_shape=[jax.ShapeDtypeStruct((t, d), F32), jax.ShapeDtypeStruct((t, width), BF16)],
        grid=(t // tm,), in_specs=[row, half, half, pl.BlockSpec((1, width), lambda i: (0, 0)), wspec, wspec],
        out_specs=[row, half], compiler_params=_params(), name="mix_out",
    )(x1, yp, o, ona, woa, wob)


def _mix_out_bwd(dx2, o, yp, ya, ona, woa, wob, plan=None):
    t, d = dx2.shape
    width = o.shape[1]
    tm = min(512, t)
    nt = t // tm

    def body(dx_ref, o_ref, yp_ref, ya_ref, on_ref, wa_ref, wb_ref, dyp_ref, do_ref, dl_ref, dwa_ref, dwb_ref, don_ref):
        @pl.when(pl.program_id(0) == 0)
        def _():
            dwa_ref[...] = jnp.zeros_like(dwa_ref)
            dwb_ref[...] = jnp.zeros_like(dwb_ref)

        dxb = dx_ref[...].astype(BF16)
        dwa_ref[...] += _dot_tn(yp_ref[...], dxb)
        dwb_ref[...] += _dot_tn(ya_ref[...], dxb)
        dyp_ref[...] = _dot_nt(dxb, wa_ref[...])
        of = o_ref[...]
        dov, dgr = _rms_bwd(of, _rms(of), on_ref[...], _dot_nt(dxb, wb_ref[...]))
        don_ref[...] = jnp.sum(dgr, axis=0, keepdims=True)
        do_ref[...] = dov.astype(BF16)
        lo = _head_masks()
        prod = dov * of
        delta = jnp.zeros((tm, LANES), F32)
        for blk in range(width // LANES):
            pb = prod[:, blk * LANES:(blk + 1) * LANES]
            delta = delta + _put_lane(jnp.sum(jnp.where(lo, pb, 0.0), axis=1, keepdims=True), 2 * blk)
            delta = delta + _put_lane(jnp.sum(jnp.where(lo, 0.0, pb), axis=1, keepdims=True), 2 * blk + 1)
        for c in range(tm // ATT_SUB):
            dl_ref[c] = delta[c * ATT_SUB:(c + 1) * ATT_SUB, :].T[0:8, :]

    row = pl.BlockSpec((tm, d), lambda i: (i, 0))
    half = pl.BlockSpec((tm, width), lambda i: (i, 0))
    wspec = pl.BlockSpec((width, d), lambda i: (0, 0))
    return _pallas(
        body, name="mix_out_bwd", args=[dx2, o, yp, ya, ona, woa, wob],
        out_shape=[jax.ShapeDtypeStruct((t, width), F32), jax.ShapeDtypeStruct((t, width), BF16),
                   jax.ShapeDtypeStruct((t // ATT_SUB, 8, ATT_SUB), F32), jax.ShapeDtypeStruct((width, d), F32),
                   jax.ShapeDtypeStruct((width, d), F32), jax.ShapeDtypeStruct((nt, 1, width), F32)],
        grid=(nt,),
        in_specs=[row, half, half, half, pl.BlockSpec((1, width), lambda i: (0, 0)), wspec, wspec],
        out_specs=[half, half, pl.BlockSpec((tm // ATT_SUB, 8, ATT_SUB), lambda i: (i, 0, 0)), wspec, wspec,
                   pl.BlockSpec((None, 1, width), lambda i: (i, 0, 0))], plan=plan)


def _mix_in_bwd(dx2, x1, gain, hm, dpv, dqh, q, dkh, k, dv, df, qn, kn, wt):
    t, d = x1.shape
    width = q.shape[1]
    pool_width = dpv.shape[1]
    tm = min(512, t)
    nt = t // tm
    scale = HEAD_DIM ** -0.5
    c_q, c_k, c_v = pool_width, pool_width + width, pool_width + 2 * width
    c_f = c_v + width

    def body(dx2_ref, x_ref, g_ref, hm_ref, dpv_ref, dqh_ref, q_ref, dkh_ref, k_ref, dv_ref, df_ref, qn_ref, kn_ref,
             wt_ref, dx_ref, dxh_ref, dwt_ref, dg_ref, dqn_ref, dkn_ref):
        @pl.when(pl.program_id(0) == 0)
        def _():
            dwt_ref[...] = jnp.zeros_like(dwt_ref)

        lo = _head_masks()
        for part, rows in enumerate(_row_halves(tm)):
            def put(ref, sl, value):
                ref[:, sl] = value if part == 0 else ref[:, sl] + value

            hm = hm_ref[rows, :]
            pieces = [(0, dpv_ref[rows, :])]
            for c0, raw_ref, dh_ref, n_ref, dn_ref, mul in ((c_q, q_ref, dqh_ref, qn_ref, dqn_ref, scale),
                                                           (c_k, k_ref, dkh_ref, kn_ref, dkn_ref, 1.0)):
                cols = []
                for blk in range(width // LANES):
                    sl = slice(blk * LANES, (blk + 1) * LANES)
                    xb = raw_ref[rows, sl]
                    gb = dh_ref[rows, sl] * mul
                    r = _head_rms(xb, lo)
                    xh = xb * r
                    dyg = gb * n_ref[:, sl]
                    cols.append((r * (dyg - xh * _head_mean(dyg * xh, lo))).astype(BF16))
                    put(dn_ref, sl, jnp.sum(gb * xh, axis=0, keepdims=True))
                pieces.append((c0, jnp.concatenate(cols, axis=1)))
            pieces.append((c_v, dv_ref[rows, :]))
            pieces.append((c_f, df_ref[rows, :].astype(BF16)))
            dhm = jnp.zeros((tm // 2, d), F32)
            for c0, piece in pieces:
                dwt_ref[c0:c0 + piece.shape[1], :] += _dot_tn(piece, hm)
                dhm = dhm + _dot(piece, wt_ref[c0:c0 + piece.shape[1], :])
            xf = x_ref[rows, :]
            dxn, dgr = _rms_bwd(xf, _rms(xf), g_ref[...], dhm)
            dx = dx2_ref[rows, :] + dxn
            dx_ref[rows, :] = dx
            dxh_ref[rows, :] = (0.5 * dx).astype(BF16)
            put(dg_ref, slice(None), jnp.sum(dgr, axis=0, keepdims=True))

    row = pl.BlockSpec((tm, d), lambda i: (i, 0))
    half = pl.BlockSpec((tm, width), lambda i: (i, 0))
    const = lambda shape: pl.BlockSpec(shape, lambda i: (0, 0))
    pvec = lambda n: pl.BlockSpec((None, 1, n), lambda i: (i, 0, 0))
    return pl.pallas_call(
        body,
        out_shape=[jax.ShapeDtypeStruct((t, d), F32), jax.ShapeDtypeStruct((t, d), BF16), jax.ShapeDtypeStruct(wt.shape, F32),
                   jax.ShapeDtypeStruct((nt, 1, d), F32),
                   jax.ShapeDtypeStruct((nt, 1, width), F32), jax.ShapeDtypeStruct((nt, 1, width), F32)],
        grid=(nt,),
        in_specs=[row, row, const((1, d)), row, pl.BlockSpec((tm, pool_width), lambda i: (i, 0)), half, half, half, half,
                  half, pl.BlockSpec((tm, LANES), lambda i: (i, 0)), const((1, width)), const((1, width)),
                  const(wt.shape)],
        out_specs=[row, row, const(wt.shape), pvec(d), pvec(width), pvec(width)],
        compiler_params=_params(), name="mix_in_bwd",
    )(dx2, x1, gain, hm, dpv, dqh, q, dkh, k, dv, df, qn, kn, wt)


def _mesh_pos():
    return lax.axis_index("x"), lax.axis_index("y"), lax.axis_index("c")


def _other_chips(x, y):
    return [(1 - x, y), (x, 1 - y), (1 - x, 1 - y)]


def _remote(src, dst, send_sem, recv_sem, device):
    return pltpu.make_async_remote_copy(src_ref=src, dst_ref=dst, send_sem=send_sem, recv_sem=recv_sem,
                                        device_id=device, device_id_type=pl.DeviceIdType.MESH)


def _half_rows(n_rows, which):
    half = n_rows // 2
    return pl.ds(pl.multiple_of(which * half, 8), half)


def _row_block(rows, cols, itemsize=4):
    rb = rows
    while rb * cols * itemsize > (1 << 20) and rb % 32 == 0:
        rb //= 2
    return rb


def _place_cast(ws, chip, tag):
    n = len(ws)
    rows, cols = ws[0].shape
    rb = _row_block(rows, cols)

    def body(k_ref, *refs):
        for w_ref, o_ref in zip(refs[:n], refs[n:]):
            o_ref[...] = w_ref[...].astype(BF16)

    return pl.pallas_call(
        body, out_shape=[jax.ShapeDtypeStruct((N_CHIPS, rows, cols), BF16)] * n,
        grid_spec=pltpu.PrefetchScalarGridSpec(
            num_scalar_prefetch=1, grid=(rows // rb,),
            in_specs=[pl.BlockSpec((rb, cols), lambda i, k: (i, 0))] * n,
            out_specs=[pl.BlockSpec((None, rb, cols), lambda i, k: (k[0], i, 0))] * n),
        compiler_params=_params(), name="place_" + tag,
    )(chip, *ws)


class _Plan:
    def __init__(self, ins, outs, alias, sems, start, finish, middle=None, middle_at=(3, 4)):
        self.ins, self.outs, self.alias, self.sems = ins, outs, alias, sems
        self.start, self.middle, self.finish, self.middle_at = start, middle, finish, middle_at


def _merge_plans(a, b):
    ni, no, ns = len(a.ins), len(a.outs), len(a.sems)
    alias = dict(a.alias)
    alias.update({ni + i: no + o for i, o in b.alias.items()})

    def both(which):
        stage_a, stage_b = getattr(a, which), getattr(b, which)
        if stage_a is None and stage_b is None:
            return None

        def run(ins, outs, sems):
            if stage_a is not None:
                stage_a(ins[:ni], outs[:no], sems[:ns])
            if stage_b is not None:
                stage_b(ins[ni:], outs[no:], sems[ns:])
        return run

    return _Plan(list(a.ins) + list(b.ins), list(a.outs) + list(b.outs), alias, list(a.sems) + list(b.sems),
                 both("start"), both("finish"), both("middle"), a.middle_at if a.middle is not None else b.middle_at)


def _run_plan(plan, name):
    n_in, n_out = len(plan.ins), len(plan.outs)

    def body(*refs):
        parts = refs[:n_in], refs[n_in:n_in + n_out], refs[n_in + n_out:]
        plan.start(*parts)
        if plan.middle is not None:
            plan.middle(*parts)
        plan.finish(*parts)

    return pl.pallas_call(
        body, out_shape=plan.outs, in_specs=[ANY] * n_in, out_specs=[ANY] * n_out, scratch_shapes=plan.sems,
        input_output_aliases=plan.alias, name=name,
    )(*plan.ins)


def _pallas(body, *, name, args, in_specs, out_shape, out_specs, grid, scratch_shapes=(), plan=None, aliases=None):
    n_in, n_out, n_scr = len(args), len(out_shape), len(scratch_shapes)
    plan = plan or _Plan([], [], {}, [], None, None)
    p_in, p_out = len(plan.ins), len(plan.outs)

    def carrying(*refs):
        ins, p_ins = refs[:n_in], refs[n_in:n_in + p_in]
        o0 = n_in + p_in
        outs, p_outs = refs[o0:o0 + n_out], refs[o0 + n_out:o0 + n_out + p_out]
        s0 = o0 + n_out + p_out
        scr, p_sems = refs[s0:s0 + n_scr], refs[s0 + n_scr:]
        ids = [pl.program_id(a) for a in range(len(grid))]

        if plan.start is not None:
            @pl.when(functools.reduce(jnp.logical_and, [i == 0 for i in ids]))
            def _():
                plan.start(p_ins, p_outs, p_sems)

        body(*ins, *outs, *scr)

        if plan.middle is not None:
            step, n_steps = 0, 1
            for i, g in zip(ids, grid):
                step, n_steps = step * g + i, n_steps * g

            @pl.when(step == (plan.middle_at[0] * n_steps) // plan.middle_at[1])
            def _():
                plan.middle(p_ins, p_outs, p_sems)

        if plan.finish is not None:
            @pl.when(functools.reduce(jnp.logical_and, [i == g - 1 for i, g in zip(ids, grid)]))
            def _():
                plan.finish(p_ins, p_outs, p_sems)

    aliases = dict(aliases or {})
    aliases.update({n_in + i: n_out + o for i, o in plan.alias.items()})
    res = pl.pallas_call(
        carrying, out_shape=list(out_shape) + list(plan.outs), grid=grid,
        in_specs=list(in_specs) + [ANY] * p_in, out_specs=list(out_specs) + [ANY] * p_out,
        scratch_shapes=list(scratch_shapes) + list(plan.sems),
        input_output_aliases=aliases, compiler_params=_params(), name=name,
    )(*args, *plan.ins)
    return list(res[:n_out]), list(res[n_out:])


def _plan_gather(stacks):
    n = len(stacks)
    relations = range(3)

    def ici_copies(outs, sems):
        x, y, c = _mesh_pos()
        chips = _other_chips(x, y)
        cps = []
        for w in range(n):
            own = outs[w].at[2 * x + y, _half_rows(stacks[w].shape[1], c)]
            cps += [_remote(own, own, sems[0].at[w, j], sems[1].at[w, j], (*chips[j], c)) for j in relations]
        return cps

    def start(ins, outs, sems):
        for cp in ici_copies(outs, sems):
            cp.start()

    def forwards(outs, sems, core):
        x, y, c = _mesh_pos()
        slots = [2 * cx + cy for cx, cy in _other_chips(x, y)]
        cps = []
        for w in range(n):
            rows = _half_rows(stacks[w].shape[1], core)
            for j in relations:
                landed = outs[w].at[slots[j], rows]
                cps.append((_remote(landed, landed, sems[0].at[w, j], sems[1].at[w, j], (x, y, 1 - c)),
                            _remote(landed, landed, sems[2].at[w, j], sems[3].at[w, j], (x, y, 1 - c))))
        return cps

    def middle(ins, outs, sems):
        c = _mesh_pos()[2]
        for arrival, forward in forwards(outs, sems, c):
            arrival.wait_recv()
            forward.start()

    def finish(ins, outs, sems):
        c = _mesh_pos()[2]
        for _, forward in forwards(outs, sems, 1 - c):
            forward.wait_recv()
        for cp in ici_copies(outs, sems) + [forward for _, forward in forwards(outs, sems, c)]:
            cp.wait_send()

    return _Plan(stacks, [jax.ShapeDtypeStruct(s.shape, s.dtype) for s in stacks], {w: w for w in range(n)},
                 [pltpu.SemaphoreType.DMA((n, 3))] * 4, start, finish, middle)


RELAY_SEMS = [pltpu.SemaphoreType.DMA((3, 2))] * 4 + [pltpu.SemaphoreType.DMA((3, 3))] * 2
RELAY_STAGES = ("send", "pass on", "x neighbour", "y neighbour", "diagonal", "end")


def _relay_gather_stage(stage, outs, sems):
    assert stage in RELAY_STAGES
    send, recv, relay_send, relay_recv, d2d_send, d2d_recv = sems
    n = len(outs)
    rh = outs[0].shape[1] // 2
    mx, my, c = _mesh_pos()
    sibling = (mx, my, 1 - c)
    near = [(1 - mx, my), (mx, 1 - my)]
    slots = [2 * cx + cy for cx, cy in near] + [2 * (1 - mx) + (1 - my)]

    def piece(w, slot, core, quarter=None):
        if quarter is None:
            return outs[w].at[slot, _half_rows(2 * rh, core)]
        return outs[w].at[slot, pl.ds(pl.multiple_of(core * rh + quarter * (rh // 2), 8), rh // 2)]

    def to_near(w, j):
        own = piece(w, 2 * mx + my, c)
        return _remote(own, own, send.at[w, j], recv.at[w, j], (*near[j], c))

    def from_near(w, j):
        landed = piece(w, slots[j], c)
        return _remote(landed, landed, send.at[w, j], recv.at[w, j], sibling)

    def onward(w, j, slot):
        part = piece(w, slot, c, quarter=j)
        return _remote(part, part, relay_send.at[w, j], relay_recv.at[w, j], (*near[1 - j], c))

    def to_sibling(w, j, core):
        landed = piece(w, slots[j], core)
        return _remote(landed, landed, d2d_send.at[w, j], d2d_recv.at[w, j], sibling)

    if stage == "send":
        for w in range(n):
            for j in range(2):
                to_near(w, j).start()
    elif stage == "pass on":
        for w in range(n):
            for j in range(2):
                from_near(w, j).wait_recv()
                onward(w, j, slots[j]).start()
                to_sibling(w, j, c).start()
    elif stage in ("x neighbour", "y neighbour"):
        for w in range(n):
            to_sibling(w, ("x neighbour", "y neighbour").index(stage), 1 - c).wait_recv()
    elif stage == "diagonal":
        for w in range(n):
            for j in range(2):
                onward(w, j, slots[2]).wait_recv()
            to_sibling(w, 2, c).start()
        for w in range(n):
            to_sibling(w, 2, 1 - c).wait_recv()
    else:
        for w in range(n):
            for j in range(2):
                to_near(w, j).wait_send()
                onward(w, j, slots[j]).wait_send()
            for j in range(3):
                to_sibling(w, j, c).wait_send()


def _plan_gather_relay(stacks):
    def stages(which):
        def run(ins, outs, sems):
            for stage in which:
                _relay_gather_stage(stage, outs, sems)
        return run

    return _Plan(stacks, [jax.ShapeDtypeStruct(s.shape, s.dtype) for s in stacks], {w: w for w in range(len(stacks))},
                 RELAY_SEMS, stages(RELAY_STAGES[:1]), stages(RELAY_STAGES[2:]), stages(RELAY_STAGES[1:2]), middle_at=(5, 8))


def _ffn_fwd_gathering(x, gain, stacks, order, later):
    t, d = x.shape
    nch, fc, _ = stacks[0].shape
    n = len(stacks)
    tm = min(FFN_STAGED_TILE, t)
    nt = t // tm
    p_in, p_out = len(later.ins), len(later.outs)
    relay = _relay_gather_stage

    def body(order_ref, x_ref, g_ref, *refs):
        later_in, refs = refs[n:n + p_in], refs[n + p_in:]
        o_ref, h_ref, a_ref, b_ref, s_ref = refs[:5]
        stack_refs, later_out, refs = refs[5:5 + n], refs[5 + n:5 + n + p_out], refs[5 + n + p_out:]
        w_ref, hs_ref, acc_ref, w_sem = refs[:4]
        relay_sems, later_sems = refs[4:10], refs[10:]
        k, i = pl.program_id(0), pl.program_id(1)
        tile = pl.ds(pl.multiple_of(i * tm, tm), tm)

        @pl.when(i == 0)
        def _():
            for chunk, stages in enumerate([("send",), ("pass on", "x neighbour"), ("y neighbour",), ("diagonal",)]):
                @pl.when(k == chunk)
                def _():
                    for stage in stages:
                        relay(stage, stack_refs, relay_sems)
                    if chunk == 1 and later.start is not None:
                        later.start(later_in, later_out, later_sems)
            loads = [pltpu.make_async_copy(stack_refs[w].at[order_ref[k]], w_ref.at[w], w_sem.at[w]) for w in range(n)]
            for cp in loads:
                cp.start()
            for cp in loads:
                cp.wait()

        @pl.when(k == 0)
        def _():
            xf = x_ref[...]
            hb = ((xf * _rms(xf)) * g_ref[...]).astype(BF16)
            h_ref[...] = hb
            hs_ref[tile, :] = hb
            acc_ref[tile, :] = jnp.zeros((tm, d), F32)

        for rows in _row_halves(tm):
            part = pl.ds(pl.multiple_of(i * tm + rows.start, tm // 2), tm // 2)
            h = hs_ref[part, :]
            a = _dot_nt(h, w_ref[0])
            b = _dot_nt(h, w_ref[1])
            sb = ((a * (0.5 * jnp.tanh(0.5 * a) + 0.5)) * b).astype(BF16)
            a_ref[rows, :] = a.astype(BF16)
            b_ref[rows, :] = b.astype(BF16)
            s_ref[rows, :] = sb
            acc_ref[part, :] += _dot(sb, w_ref[2])

        @pl.when(k == nch - 1)
        def _():
            o_ref[...] = x_ref[...] + 0.5 * acc_ref[tile, :]

        if later.middle is not None:
            @pl.when((k == nch - 1) & (i == nt // 2))
            def _():
                later.middle(later_in, later_out, later_sems)

        @pl.when((k == nch - 1) & (i == nt - 1))
        def _():
            relay("end", stack_refs, relay_sems)
            if later.finish is not None:
                later.finish(later_in, later_out, later_sems)

    ends = lambda k, i: jnp.where((k == 0) | (k == nch - 1), i, 0)
    act = pl.BlockSpec((None, tm, fc), lambda k, i, order: (order[k], i, 0))
    out_shape = [jax.ShapeDtypeStruct((t, d), F32), jax.ShapeDtypeStruct((t, d), BF16)]
    out_shape += [jax.ShapeDtypeStruct((nch, t, fc), BF16)] * 3
    out_shape += [jax.ShapeDtypeStruct(s.shape, s.dtype) for s in stacks] + list(later.outs)
    aliases = {3 + w: 5 + w for w in range(n)}
    aliases.update({3 + n + i: 5 + n + o for i, o in later.alias.items()})
    res = pl.pallas_call(
        body, out_shape=out_shape,
        grid_spec=pltpu.PrefetchScalarGridSpec(
            num_scalar_prefetch=1, grid=(nch, nt),
            in_specs=[pl.BlockSpec((tm, d), lambda k, i, order: (ends(k, i), 0)),
                      pl.BlockSpec((1, d), lambda k, i, order: (0, 0))] + [ANY] * (n + p_in),
            out_specs=[pl.BlockSpec((tm, d), lambda k, i, order: (jnp.where(k == nch - 1, i, 0), 0)),
                       pl.BlockSpec((tm, d), lambda k, i, order: (jnp.where(k == 0, i, nt - 1), 0)),
                       act, act, act] + [ANY] * (n + p_out),
            scratch_shapes=[pltpu.VMEM((n, fc, d), BF16), pltpu.VMEM((t, d), BF16), pltpu.VMEM((t, d), F32),
                            pltpu.SemaphoreType.DMA((n,))] + RELAY_SEMS + list(later.sems)),
        input_output_aliases=aliases, compiler_params=_params(), name="ffn_fwd",
    )(order, x, gain, *stacks, *later.ins)
    return list(res[:5]), list(res[5:5 + n]), list(res[5 + n:])


def _plan_sibling_halves(gs):
    n = len(gs)

    def copies(ins, outs, sems):
        x, y, c = _mesh_pos()
        return [_remote(ins[w].at[:, _half_rows(gs[w].shape[1], 1 - c), :], outs[w], sems[0].at[w], sems[1].at[w],
                        (x, y, 1 - c)) for w in range(n)]

    def start(ins, outs, sems):
        for cp in copies(ins, outs, sems):
            cp.start()

    def finish(ins, outs, sems):
        for cp in copies(ins, outs, sems):
            cp.wait()

    return _Plan(gs, [jax.ShapeDtypeStruct((g.shape[0], g.shape[1] // 2, g.shape[2]), g.dtype) for g in gs], {},
                 [pltpu.SemaphoreType.DMA((n,))] * 2, start, finish)


def _plan_chip_exchange(ps):
    n = len(ps)

    def copies(ins, outs, sems):
        x, y, c = _mesh_pos()
        return [_remote(ins[w].at[2 * cx + cy], outs[w].at[j], sems[0].at[w, j], sems[1].at[w, j], (cx, cy, c))
                for w in range(n) for j, (cx, cy) in enumerate(_other_chips(x, y))]

    def start(ins, outs, sems):
        for cp in copies(ins, outs, sems):
            cp.start()

    def finish(ins, outs, sems):
        for cp in copies(ins, outs, sems):
            cp.wait()

    return _Plan(ps, [jax.ShapeDtypeStruct((3,) + p.shape[1:], p.dtype) for p in ps], {},
                 [pltpu.SemaphoreType.DMA((n, 3))] * 2, start, finish)


def _plan_sibling_share(gs):
    n = len(gs)

    def copies(outs, sems, which):
        x, y, c = _mesh_pos()
        cps = []
        for w in range(n):
            rows = outs[w].at[_half_rows(gs[w].shape[0], c if which == "mine" else 1 - c)]
            cps.append(_remote(rows, rows, sems[0].at[w], sems[1].at[w], (x, y, 1 - c)))
        return cps

    def start(ins, outs, sems):
        for cp in copies(outs, sems, "mine"):
            cp.start()

    def finish(ins, outs, sems):
        for cp in copies(outs, sems, "mine"):
            cp.wait_send()
        for cp in copies(outs, sems, "theirs"):
            cp.wait_recv()

    return _Plan(gs, [jax.ShapeDtypeStruct(g.shape, g.dtype) for g in gs], {w: w for w in range(n)},
                 [pltpu.SemaphoreType.DMA((n,))] * 2, start, finish)


def _same_shape_groups(arrays):
    groups = {}
    for i, a in enumerate(arrays):
        groups.setdefault(a.shape, []).append(i)
    return list(groups.values())


def _add_sibling(gs, r1s, ids, tag):
    n = len(gs)
    nch, rh, cols = r1s[0].shape

    def body(ids_ref, *refs):
        for g_ref, r_ref, o_ref in zip(refs[:n], refs[n:2 * n], refs[2 * n:]):
            o_ref[...] = (g_ref[...] + r_ref[...]).astype(BF16)

    blk = lambda fn: pl.BlockSpec((None, rh, cols), fn)
    return pl.pallas_call(
        body, out_shape=[jax.ShapeDtypeStruct(r1s[0].shape, BF16)] * n,
        grid_spec=pltpu.PrefetchScalarGridSpec(
            num_scalar_prefetch=1, grid=(nch,),
            in_specs=[blk(lambda k, ids: (k, ids[1], 0))] * n + [blk(lambda k, ids: (k, 0, 0))] * n,
            out_specs=[blk(lambda k, ids: (k, 0, 0))] * n),
        compiler_params=_params(), name="add_sibling_" + tag,
    )(ids, *gs, *r1s)


def _add_chips(gs, r1s, r2s, ids, tag):
    n = len(gs)
    _, rh, cols = r1s[0].shape
    nb = 2 if rh % 32 == 0 else 1
    rb = rh // nb

    def body(ids_ref, *refs):
        for g_ref, r1_ref, r2_ref, o_ref in zip(refs[:n], refs[n:2 * n], refs[2 * n:3 * n], refs[3 * n:]):
            own = g_ref[...] + r1_ref[...]
            o_ref[...] = ((own + r2_ref[0].astype(F32)) + r2_ref[1].astype(F32)) + r2_ref[2].astype(F32)

    return pl.pallas_call(
        body, out_shape=[jax.ShapeDtypeStruct((2 * rh, cols), F32)] * n,
        grid_spec=pltpu.PrefetchScalarGridSpec(
            num_scalar_prefetch=1, grid=(nb,),
            in_specs=[pl.BlockSpec((None, rb, cols), lambda i, ids: (ids[0], ids[1] * nb + i, 0))] * n
            + [pl.BlockSpec((None, rb, cols), lambda i, ids: (ids[0], i, 0))] * n
            + [pl.BlockSpec((3, rb, cols), lambda i, ids: (0, i, 0))] * n,
            out_specs=[pl.BlockSpec((rb, cols), lambda i, ids: (ids[1] * nb + i, 0))] * n),
        compiler_params=_params(), name="add_chips_" + tag,
    )(ids, *gs, *r1s, *r2s)


VEC_ROWS = 8


N_DEVICES = 8


def _small_pack(part, d, width):
    names = ("ffn1_norm", "mix_norm", "ffn2_norm", "pool_scale", "out_norm_pool", "out_norm_attn", "qn", "kn", "b_forget",
             "loss")
    args = [part[k] for k in names]

    def body(g1_ref, gm_ref, g2_ref, ps_ref, onp_ref, ona_ref, qn_ref, kn_ref, bf_ref, loss_ref, vbuf):
        lo = _head_masks()

        def fold_heads(ref):
            v = jnp.sum(ref[...], axis=0)
            acc = jnp.zeros((VEC_ROWS, LANES), F32)
            for blk in range(width // LANES):
                vb = jnp.broadcast_to(v[:, blk * LANES:(blk + 1) * LANES], (VEC_ROWS, LANES))
                acc = acc + vb + pltpu.roll(vb, HEAD_DIM, 1)
            return jnp.where(lo, acc, 0.0)[0:1, :]

        vbuf[0] = jnp.zeros((VEC_ROWS, d), F32)
        vbuf[0, 0:1, :] = jnp.sum(g1_ref[...], axis=0)
        vbuf[0, 1:2, :] = jnp.sum(gm_ref[...], axis=0)
        vbuf[0, 2:3, :] = jnp.sum(g2_ref[...], axis=0)
        vbuf[0, 5:6, 0:LANES] = jnp.sum(loss_ref[...], axis=0)[0:1, :]
        vbuf[0, 3:4, 0:width] = jnp.sum(ps_ref[...], axis=0)
        vbuf[0, 3:4, width:2 * width] = jnp.sum(onp_ref[...], axis=0)
        vbuf[0, 4:5, 0:width] = jnp.sum(ona_ref[...], axis=0)
        vbuf[0, 4:5, width:width + LANES] = fold_heads(qn_ref)
        vbuf[0, 4:5, width + LANES:width + 2 * LANES] = fold_heads(kn_ref)
        vbuf[0, 4:5, width + 2 * LANES:width + 3 * LANES] = jnp.sum(bf_ref[...], axis=0)

    return pl.pallas_call(
        body, out_shape=jax.ShapeDtypeStruct((N_DEVICES, VEC_ROWS, d), F32),
        in_specs=[VM] * len(args), out_specs=VM, compiler_params=_params(), name="small_pack",
    )(*args)


def _pool_pack(dpw):
    def body(pw_ref, pbuf):
        pbuf[0] = jnp.sum(pw_ref[...], axis=0)

    return pl.pallas_call(
        body, out_shape=jax.ShapeDtypeStruct((N_DEVICES,) + dpw.shape[1:], F32),
        in_specs=[VM], out_specs=VM, compiler_params=_params(), name="pool_pack",
    )(dpw)


def _plan_all_to_all(stacks):
    n = len(stacks)

    def copies(outs, sems):
        x, y, c = _mesh_pos()
        cps = []
        for r in range(1, N_DEVICES):
            peer = (x if not r & 4 else 1 - x, y if not r & 2 else 1 - y, c if not r & 1 else 1 - c)
            cps += [_remote(outs[w].at[0], outs[w].at[r], sems[0].at[w, r - 1], sems[1].at[w, r - 1], peer) for w in range(n)]
        return cps

    def start(ins, outs, sems):
        for cp in copies(outs, sems):
            cp.start()

    def finish(ins, outs, sems):
        for cp in copies(outs, sems):
            cp.wait()

    return _Plan(stacks, [jax.ShapeDtypeStruct(s.shape, s.dtype) for s in stacks], {w: w for w in range(n)},
                 [pltpu.SemaphoreType.DMA((n, N_DEVICES - 1))] * 2, start, finish)


def _small_sum(vstack, pstack, me):
    def body(me_ref, vbuf, pbuf, vec_ref, pw_ref):
        vec = vbuf[me_ref[0]]
        pw = pbuf[me_ref[0]]
        for dev in range(1, N_DEVICES):
            vec = vec + vbuf[jnp.bitwise_xor(me_ref[0], dev)]
            pw = pw + pbuf[jnp.bitwise_xor(me_ref[0], dev)]
        vec_ref[...] = vec
        pw_ref[...] = pw

    full = lambda s: pl.BlockSpec(s.shape, lambda i, me: (0,) * len(s.shape))
    outs = [jax.ShapeDtypeStruct(vstack.shape[1:], F32), jax.ShapeDtypeStruct(pstack.shape[1:], F32)]
    return pl.pallas_call(
        body, out_shape=outs,
        grid_spec=pltpu.PrefetchScalarGridSpec(num_scalar_prefetch=1, grid=(1,), in_specs=[full(vstack), full(pstack)],
                                               out_specs=[full(o) for o in outs]),
        compiler_params=_params(), name="small_sum",
    )(me, vstack, pstack)


def _adamw(ws, gs, ms, vs, tag):
    n = len(ws)
    rows, cols = ws[0].shape
    rb = rows
    while rb * cols * 4 * n > (1 << 20) and rb % 16 == 0:
        rb //= 2

    def body(*refs):
        for j in range(n):
            w_ref, g_ref, m_ref, v_ref = (refs[k * n + j] for k in range(4))
            go_ref, d_ref, mo_ref, vo_ref = (refs[(4 + k) * n + j] for k in range(4))
            gv = g_ref[...]
            go_ref[...] = gv
            m2 = ADAM_B1 * m_ref[...] + (1.0 - ADAM_B1) * gv
            v2 = ADAM_B2 * v_ref[...] + (1.0 - ADAM_B2) * (gv * gv)
            m_hat = m2 / (1.0 - ADAM_B1 ** ADAM_STEP)
            v_hat = v2 / (1.0 - ADAM_B2 ** ADAM_STEP)
            d_ref[...] = -ADAM_LR * (m_hat / (jnp.sqrt(v_hat) + ADAM_EPS) + ADAM_WD * w_ref[...])
            mo_ref[...] = m2
            vo_ref[...] = v2

    spec = pl.BlockSpec((rb, cols), lambda i: (i, 0))
    res, _ = _pallas(
        body, name="adamw_" + tag, args=[*ws, *gs, *ms, *vs], out_shape=[jax.ShapeDtypeStruct(ws[0].shape, F32)] * (4 * n),
        grid=(rows // rb,), in_specs=[spec] * (4 * n), out_specs=[spec] * (4 * n))
    return [tuple(res[k * n + j] for k in range(4)) for j in range(n)]


def _pack_vec(p, d, width):
    pad = lambda v: jnp.pad(v, (0, LANES - v.shape[0]))
    row3 = jnp.concatenate([p["pool_scale"], p["out_norm_pool"]])
    row4 = jnp.concatenate([p["out_norm_attn"], pad(p["q_norm"]), pad(p["k_norm"]), pad(p["b_forget"]),
                            jnp.zeros((d - width - 3 * LANES,), F32)])
    rows = [p["ffn1_norm"], p["mix_norm"], p["ffn2_norm"], row3, row4]
    return jnp.pad(jnp.stack(rows), ((0, VEC_ROWS - len(rows)), (0, 0)))


def _unpack_vec(vec, width):
    return dict(ffn1_norm=vec[0], mix_norm=vec[1], ffn2_norm=vec[2], pool_scale=vec[3, :width],
                out_norm_pool=vec[3, width:2 * width], out_norm_attn=vec[4, :width],
                q_norm=vec[4, width:width + HEAD_DIM], k_norm=vec[4, width + LANES:width + LANES + HEAD_DIM],
                b_forget=vec[4, width + 2 * LANES:width + 2 * LANES + N_HEADS])


WEIGHT_NAMES = ("ffn1_norm", "ffn1_w_gate", "ffn1_w_up", "ffn1_w_down", "mix_norm", "w_in", "b_forget", "pool_w",
                "pool_scale", "q_norm", "k_norm", "out_norm_pool", "out_norm_attn", "w_out", "ffn2_norm",
                "ffn2_w_gate", "ffn2_w_up", "ffn2_w_down")
BIG_NAMES = ("ffn1_w_gate", "ffn1_w_up", "ffn1_w_down", "w_in", "w_out", "ffn2_w_gate", "ffn2_w_up", "ffn2_w_down")
TRANSPOSED_NAMES = ("ffn1_w_gate", "ffn1_w_up", "w_in", "ffn2_w_gate", "ffn2_w_up")
FFN1_NAMES = ("ffn1_w_gate", "ffn1_w_up", "ffn1_w_down")
MIX_NAMES = ("w_in", "w_out")
FFN2_NAMES = ("ffn2_w_gate", "ffn2_w_up", "ffn2_w_down")


def kernel(x, ffn1_norm, ffn1_w_gate, ffn1_w_up, ffn1_w_down, mix_norm, w_in, b_forget, pool_w, pool_scale, q_norm, k_norm, out_norm_pool, out_norm_attn, w_out, ffn2_norm, ffn2_w_gate, ffn2_w_up, ffn2_w_down, loss_target, m_ffn1_norm, m_ffn1_w_gate, m_ffn1_w_up, m_ffn1_w_down, m_mix_norm, m_w_in, m_b_forget, m_pool_w, m_pool_scale, m_q_norm, m_k_norm, m_out_norm_pool, m_out_norm_attn, m_w_out, m_ffn2_norm, m_ffn2_w_gate, m_ffn2_w_up, m_ffn2_w_down, v_ffn1_norm, v_ffn1_w_gate, v_ffn1_w_up, v_ffn1_w_down, v_mix_norm, v_w_in, v_b_forget, v_pool_w, v_pool_scale, v_q_norm, v_k_norm, v_out_norm_pool, v_out_norm_attn, v_w_out, v_ffn2_norm, v_ffn2_w_gate, v_ffn2_w_up, v_ffn2_w_down):
    given = dict(locals())
    w = {n: given[n] for n in WEIGHT_NAMES}
    m = {n: given["m_" + n] for n in WEIGHT_NAMES}
    v = {n: given["v_" + n] for n in WEIGHT_NAMES}
    n_batch, seq, d = x.shape
    width = pool_scale.shape[0]
    in_rows = w_in.shape[1]
    in_cols = N_CHIPS * in_rows
    in_pad = -(-in_rows // 32) * 32
    in_cols_pad = in_cols - N_HEADS + LANES

    work = lambda a, n: a.T if n in TRANSPOSED_NAMES else a
    exchanged = lambda a, n: jnp.pad(a, ((0, in_pad - in_rows), (0, 0))) if n == "w_in" else a

    mesh_x, mesh_y, mesh_c = _mesh_pos()
    ids = jnp.stack([2 * mesh_x + mesh_y, mesh_c]).astype(jnp.int32)

    row = lambda a: a.reshape(1, -1)
    g1, gm, g2, ps, onp, ona = (row(a) for a in (ffn1_norm, mix_norm, ffn2_norm, pool_scale, out_norm_pool, out_norm_attn))
    qn, kn = row(jnp.tile(q_norm, N_HEADS)), row(jnp.tile(k_norm, N_HEADS))
    bf = row(jnp.pad(b_forget, (0, LANES - N_HEADS)))
    pwb = pool_w.astype(BF16)
    xf, tgt = x.reshape(n_batch * seq, d), loss_target.reshape(n_batch * seq, d)

    def grouped(call, names, *lists):
        out = [None] * len(names)
        for idx in _same_shape_groups(lists[0]):
            res = call(*[[lst[i] for i in idx] for lst in lists], names[idx[0]])
            for i, r in zip(idx, res):
                out[i] = r
        return out

    placed = dict(zip(BIG_NAMES, grouped(lambda ws, tag: _place_cast(ws, ids, tag), BIG_NAMES,
                                         [exchanged(work(w[n], n), n) for n in BIG_NAMES])))
    landing = jnp.stack([2 * cx + cy for cx, cy in [(mesh_x, mesh_y)] + _other_chips(mesh_x, mesh_y)]).astype(jnp.int32)
    (x1, h1, a1, b1, s1), (wg1, wu1, wd1), (w_in_all, w_out_all) = _ffn_fwd_gathering(
        xf, g1, [placed[n] for n in FFN1_NAMES], landing, _plan_gather([placed[n] for n in MIX_NAMES]))
    w_in_t = jnp.pad(w_in_all[:, :in_rows].reshape(in_cols, d), ((0, in_cols_pad - in_cols), (0, 0)))
    w_out_full = w_out_all.reshape(N_CHIPS * w_out.shape[0], d)
    woa, wob = w_out_full[:width], w_out_full[width:]

    hm, pv, q, k, qh, kh, vb, f = _mix_proj(x1, gm, w_in_t, qn, kn, width, width)
    qa, ka = _forget_prefix(f, bf, qh, kh, n_batch, seq)
    yp = _pool_fwd(pv, pwb, ps, onp, n_batch, seq)
    (o, lse), (wg2, wu2, wd2) = _attn_fwd(qa, ka, vb, n_batch, seq, plan=_plan_gather_relay([placed[n] for n in FFN2_NAMES]))
    x2, ya = _mix_out(x1, yp, o, ona, woa, wob)
    (dy, h2, a2, b2, s2, lpart, dyh), _ = _ffn_fwd(x2, g2, wg2, wu2, wd2, target=tgt)

    def to_chips(gs, arrived, tags):
        return grouped(lambda g, r, tag: _add_sibling(g, r, ids, tag), tags, gs, arrived)

    def own_rows(gs, from_sibling, from_chips, tags):
        return grouped(lambda g, ra, rb, tag: _add_chips(g, ra, rb, ids, tag), tags, gs, from_sibling, from_chips)

    (dx2, da2, db2, dg2), _ = _ffn_bwd_x(dy, x2, g2, a2, b2, wg2, wu2, wd2, "ffn2_bwd_x")
    dw2, _ = _ffn_bwd_w([(da2, h2), (db2, h2), (s2, dyh)], "ffn2_bwd_w")
    (dyp, do, delta, dwoa, dwob, dona), sib2 = _mix_out_bwd(dx2, o, yp, ya, ona, woa, wob, plan=_plan_sibling_halves(dw2))
    dpv, dpw, dps, donp = _pool_bwd(pv, dyp, pwb, ps, onp, n_batch, seq)
    (dqh, dkh, dv, dfq, dfk), chips2 = _attn_bwd(qa, ka, vb, do, lse, delta, n_batch, seq,
                                                 plan=_plan_chip_exchange(to_chips(dw2, sib2, FFN2_NAMES)))
    df, dbf = _forget_bwd(dfq, dfk, f, bf, n_batch, seq)
    dx1, dx1h, dw_in_t, dgm, dqn, dkn = _mix_in_bwd(dx2, x1, gm, hm, dpv, dqh, q, dkh, k, dv, df, qn, kn, w_in_t)
    in_base = [in_rows * k // 8 * 8 for k in range(N_CHIPS)]
    d_w_in = jnp.stack([dw_in_t[b:b + in_pad] for b in in_base])
    d_w_out = jnp.concatenate([dwoa, dwob], axis=0).reshape(N_CHIPS, w_out.shape[0], d)
    dwm = [d_w_in, d_w_out]
    down = FFN1_NAMES[2:]
    dwd1, arrived = _ffn_bwd_w([(s1, dx1h)], "ffn1_bwd_w_down",
                               plan=_merge_plans(_plan_sibling_halves(dwm), _plan_all_to_all(
                                   [_pool_pack(dpw.reshape(n_batch, -1, pool_w.shape[-1]))])))
    sibm, pstack = arrived[:len(dwm)], arrived[len(dwm)]
    (da1, db1), arrived = _ffn_bwd_a(dx1h, a1, b1, wd1, "ffn1_bwd_a",
                                     plan=_merge_plans(_plan_sibling_halves(dwd1),
                                                       _plan_chip_exchange(to_chips(dwm, sibm, MIX_NAMES))))
    sibd, chipsm = arrived[:1], arrived[1:]
    gate_up = FFN1_NAMES[:2]
    dwgu1, chipsd = _ffn_bwd_w([(da1, h1), (db1, h1)], "ffn1_bwd_w_gate_up",
                               plan=_plan_chip_exchange(to_chips(dwd1, sibd, down)))
    n_tiles = (n_batch * seq) // min(FFN_TILE, n_batch * seq)
    first = max(n_tiles // 4, 1)
    begun, sibgu = _ffn_bwd_h(dx1, xf, g1, da1, db1, wg1, wu1, "ffn1_bwd_h_first", (0, first),
                              plan=_plan_sibling_halves(dwgu1))
    earlier = (own_rows(dwd1, sibd, chipsd, down) + own_rows(dwm, sibm, chipsm, MIX_NAMES)
               + own_rows(dw2, sib2, chips2, FFN2_NAMES))
    (gx, dg1), arrived = _ffn_bwd_h(dx1, xf, g1, da1, db1, wg1, wu1, "ffn1_bwd_h_rest", (first, n_tiles), prev=begun,
                                    plan=_merge_plans(_plan_chip_exchange(to_chips(dwgu1, sibgu, gate_up)),
                                                      _plan_sibling_share(earlier)))
    chipsgu, shared = arrived[:len(gate_up)], arrived[len(gate_up):]

    part = dict(ffn1_norm=dg1, mix_norm=dgm, ffn2_norm=dg2, b_forget=dbf, pool_scale=dps, out_norm_pool=donp,
                out_norm_attn=dona, qn=dqn, kn=dkn, loss=lpart)
    mine = own_rows(dwgu1, sibgu, chipsgu, gate_up)
    last = _run_plan(_merge_plans(_plan_sibling_share(mine), _plan_all_to_all([_small_pack(part, d, width)])), "last_exchange")
    vstack = last[len(mine)]
    g_vec, g_pw = _small_sum(vstack, pstack, jnp.reshape(4 * mesh_x + 2 * mesh_y + mesh_c, (1,)).astype(jnp.int32))
    loss = g_vec[5, 0]
    reduced = dict(zip(gate_up + down + MIX_NAMES + FFN2_NAMES, list(last[:len(mine)]) + list(shared)))
    reduced["w_in"] = lax.dynamic_slice(reduced["w_in"], ((in_rows * ids[0]) % 8, 0), (in_rows, d))

    grads, delta, new_m, new_v = {}, {}, {}, {}
    for names in (FFN2_NAMES, FFN1_NAMES, ("w_in",), ("w_out",)):
        stepped = _adamw([work(w[n], n) for n in names], [reduced[n] for n in names], [work(m[n], n) for n in names],
                         [work(v[n], n) for n in names], names[0])
        for n, step in zip(names, stepped):
            grads[n], delta[n], new_m[n], new_v[n] = (work(a, n) for a in step)
    flat_pw = lambda a: a.reshape(-1, a.shape[-1])
    (_, d_pw, m_pw, v_pw), = _adamw([flat_pw(pool_w)], [g_pw], [flat_pw(m_pool_w)], [flat_pw(v_pool_w)], "pool_w")
    (_, d_vec, m_vec, v_vec), = _adamw([_pack_vec(w, d, width)], [g_vec], [_pack_vec(m, d, width)],
                                       [_pack_vec(v, d, width)], "vectors")
    grads.update(_unpack_vec(g_vec, width), pool_w=g_pw.reshape(pool_w.shape))
    delta.update(_unpack_vec(d_vec, width), pool_w=d_pw.reshape(pool_w.shape))
    new_m.update(_unpack_vec(m_vec, width), pool_w=m_pw.reshape(pool_w.shape))
    new_v.update(_unpack_vec(v_vec, width), pool_w=v_pw.reshape(pool_w.shape))
    return (loss, gx.reshape(x.shape), *[grads[n] for n in WEIGHT_NAMES], *[delta[n] for n in WEIGHT_NAMES],
            *[new_m[n] for n in WEIGHT_NAMES], *[new_v[n] for n in WEIGHT_NAMES])
```

```python
import functools

import jax
import jax.numpy as jnp
from jax import lax
from jax.experimental import pallas as pl
from jax.experimental.pallas import tpu as pltpu

F32 = jnp.float32
BF16 = jnp.bfloat16
EPS = 1e-6
NEG = -1e30
ADAM_LR = 0.001
ADAM_B1 = 0.9
ADAM_B2 = 0.999
ADAM_EPS = 1e-08
ADAM_WD = 0.01
ADAM_STEP = 10
POOL_WINDOWS = (2, 4, 8, 16)
HEAD_DIM = 64
N_HEADS = 8
LANES = 128
N_CHIPS = 4
ATT_BLOCK = 512
ATT_SUB = 128
FFN_TILE = 1024
FFN_STAGED_TILE = 512
VMEM_LIMIT = 62 * 1024 * 1024
ANY = pl.BlockSpec(memory_space=pl.ANY)
VM = pl.BlockSpec(memory_space=pltpu.VMEM)


def _params(**kw):
    return pltpu.CompilerParams(vmem_limit_bytes=VMEM_LIMIT, **kw)


def _dot(a, b):
    return jnp.dot(a, b, preferred_element_type=F32)


def _dot_nt(a, b):
    return lax.dot_general(a, b, (((1,), (1,)), ((), ())), preferred_element_type=F32)


def _dot_tn(a, b):
    return lax.dot_general(a, b, (((0,), (0,)), ((), ())), preferred_element_type=F32)


def _sigmoid(z):
    return 1.0 / (1.0 + jnp.exp(-z))


def _rms(xf):
    return lax.rsqrt(jnp.mean(xf * xf, axis=-1, keepdims=True) + EPS)


def _rms_bwd(xf, r, gain, dh):
    xh = xf * r
    dyg = dh * gain
    return r * (dyg - xh * jnp.mean(dyg * xh, axis=-1, keepdims=True)), dh * xh


def _total(v):
    return jnp.sum(jnp.sum(v, axis=1, keepdims=True), axis=0, keepdims=True)


def _ffn_fwd(x, gain, wg, wu, wd, target=None, plan=None):
    t, d = x.shape
    nch, fc, _ = wg.shape
    tm = min(FFN_TILE, t)
    nt = t // tm
    with_loss = target is not None

    def body(*refs):
        if with_loss:
            x_ref, g_ref, wg_ref, wu_ref, wd_ref, t_ref, o_ref, h_ref, a_ref, b_ref, s_ref, l_ref, oh_ref, acc_ref = refs
        else:
            x_ref, g_ref, wg_ref, wu_ref, wd_ref, o_ref, h_ref, a_ref, b_ref, s_ref, acc_ref = refs
        k = pl.program_id(1)

        @pl.when(k == 0)
        def _():
            xf = x_ref[...]
            h_ref[...] = ((xf * _rms(xf)) * g_ref[...]).astype(BF16)
            acc_ref[...] = jnp.zeros_like(acc_ref)

        for rows in _row_halves(tm):
            h = h_ref[rows, :]
            a = _dot_nt(h, wg_ref[...])
            b = _dot_nt(h, wu_ref[...])
            sb = ((a * (0.5 * jnp.tanh(0.5 * a) + 0.5)) * b).astype(BF16)
            a_ref[rows, :] = a.astype(BF16)
            b_ref[rows, :] = b.astype(BF16)
            s_ref[rows, :] = sb
            acc_ref[rows, :] += _dot(sb, wd_ref[...])

        @pl.when(k == nch - 1)
        def _():
            y = x_ref[...] + 0.5 * acc_ref[...]
            if with_loss:
                e = y - t_ref[...]
                o_ref[...] = e * (1.0 / d)
                oh_ref[...] = (e * (0.5 / d)).astype(BF16)
                l_ref[...] = jnp.broadcast_to(_total(e * e) * (0.5 / d), l_ref.shape)
            else:
                o_ref[...] = y

    row = pl.BlockSpec((tm, d), lambda i, k: (i, 0))
    chunk = pl.BlockSpec((None, fc, d), lambda i, k: (k, 0, 0))
    act = pl.BlockSpec((None, tm, fc), lambda i, k: (k, i, 0))
    in_specs = [row, pl.BlockSpec((1, d), lambda i, k: (0, 0)), chunk, chunk, chunk]
    out_shape = [jax.ShapeDtypeStruct((t, d), F32), jax.ShapeDtypeStruct((t, d), BF16)]
    out_shape += [jax.ShapeDtypeStruct((nch, t, fc), BF16)] * 3
    out_specs = [row, row, act, act, act]
    args = [x, gain, wg, wu, wd]
    if with_loss:
        in_specs.append(row)
        args.append(target)
        out_shape += [jax.ShapeDtypeStruct((nt, 8, LANES), F32), jax.ShapeDtypeStruct((t, d), BF16)]
        out_specs += [pl.BlockSpec((None, 8, LANES), lambda i, k: (i, 0, 0)), row]
    return _pallas(body, name="ffn_fwd_loss" if with_loss else "ffn_fwd", args=args, in_specs=in_specs,
                   out_shape=out_shape, out_specs=out_specs, grid=(nt, nch),
                   scratch_shapes=[pltpu.VMEM((tm, d), F32)], plan=plan)


def _row_halves(n):
    return [slice(0, n // 2), slice(n // 2, n)]


def _swiglu_grads(dyh, a_ref, b_ref, wd_ref, rows):
    ds = _dot_nt(dyh, wd_ref[...])
    av = a_ref[rows, :].astype(F32)
    bv = b_ref[rows, :].astype(F32)
    th = jnp.tanh(0.5 * av)
    sig = 0.5 * th + 0.5
    dab = ((ds * bv) * (sig * (1.0 + av * (0.5 - 0.5 * th)))).astype(BF16)
    return dab, (ds * (av * sig)).astype(BF16)


def _ffn_bwd_a(dyh, a, b, wd, name, plan=None):
    t, d = dyh.shape
    nch, fc, _ = wd.shape
    tm = min(FFN_TILE, t)

    def body(dyh_ref, a_ref, b_ref, wd_ref, da_ref, db_ref):
        for rows in _row_halves(tm):
            da_ref[rows, :], db_ref[rows, :] = _swiglu_grads(dyh_ref[rows, :], a_ref, b_ref, wd_ref, rows)

    act = pl.BlockSpec((None, tm, fc), lambda i, k: (k, i, 0))
    return _pallas(
        body, name=name, args=[dyh, a, b, wd], out_shape=[jax.ShapeDtypeStruct((nch, t, fc), BF16)] * 2, grid=(t // tm, nch),
        in_specs=[pl.BlockSpec((tm, d), lambda i, k: (i, 0)), act, act, pl.BlockSpec((None, fc, d), lambda i, k: (k, 0, 0))],
        out_specs=[act, act], plan=plan)


def _ffn_bwd_h(dy, x, gain, da, db, wg, wu, name, tiles, prev=None, plan=None):
    t, d = x.shape
    nch, fc, _ = wg.shape
    tm = min(FFN_TILE, t)
    nt = t // tm
    t0, t1 = tiles

    def body(*refs):
        dy_ref, x_ref, g_ref, da_ref, db_ref, wg_ref, wu_ref = refs[:7]
        dx_ref, dg_ref, acc_ref = refs[-3:]
        k = pl.program_id(1)

        @pl.when(k == 0)
        def _():
            acc_ref[...] = jnp.zeros_like(acc_ref)

        acc_ref[...] += _dot(da_ref[...], wg_ref[...]) + _dot(db_ref[...], wu_ref[...])

        @pl.when(k == nch - 1)
        def _():
            xf = x_ref[...]
            dxn, dgr = _rms_bwd(xf, _rms(xf), g_ref[...], acc_ref[...])
            dx_ref[...] = dy_ref[...] + dxn
            dg_ref[...] = jnp.sum(dgr, axis=0, keepdims=True)

    row = pl.BlockSpec((tm, d), lambda i, k: (i + t0, 0))
    chunk = pl.BlockSpec((None, fc, d), lambda i, k: (k, 0, 0))
    act = pl.BlockSpec((None, tm, fc), lambda i, k: (k, i + t0, 0))
    args = [dy, x, gain, da, db, wg, wu]
    in_specs = [row, row, pl.BlockSpec((1, d), lambda i, k: (0, 0)), act, act, chunk, chunk]
    aliases = {}
    if prev is not None:
        aliases = {len(args): 0, len(args) + 1: 1}
        args += list(prev)
        in_specs += [ANY, ANY]
    return _pallas(
        body, name=name, args=args, out_shape=[jax.ShapeDtypeStruct((t, d), F32), jax.ShapeDtypeStruct((nt, 1, d), F32)],
        grid=(t1 - t0, nch), in_specs=in_specs,
        out_specs=[row, pl.BlockSpec((None, 1, d), lambda i, k: (i + t0, 0, 0))],
        scratch_shapes=[pltpu.VMEM((tm, d), F32)], plan=plan, aliases=aliases)


def _ffn_bwd_x(dy, x, gain, a, b, wg, wu, wd, name, plan=None):
    t, d = x.shape
    nch, fc, _ = wg.shape
    tm = min(FFN_TILE, t)
    nt = t // tm

    def body(dy_ref, x_ref, g_ref, a_ref, b_ref, wg_ref, wu_ref, wd_ref, dx_ref, da_ref, db_ref, dg_ref, acc_ref):
        k = pl.program_id(1)

        @pl.when(k == 0)
        def _():
            acc_ref[...] = jnp.zeros_like(acc_ref)

        for rows in _row_halves(tm):
            dab, dbb = _swiglu_grads((0.5 * dy_ref[rows, :]).astype(BF16), a_ref, b_ref, wd_ref, rows)
            da_ref[rows, :] = dab
            db_ref[rows, :] = dbb
            acc_ref[rows, :] += _dot(dab, wg_ref[...]) + _dot(dbb, wu_ref[...])

        @pl.when(k == nch - 1)
        def _():
            xf = x_ref[...]
            dxn, dgr = _rms_bwd(xf, _rms(xf), g_ref[...], acc_ref[...])
            dx_ref[...] = dy_ref[...] + dxn
            dg_ref[...] = jnp.sum(dgr, axis=0, keepdims=True)

    row = pl.BlockSpec((tm, d), lambda i, k: (i, 0))
    chunk = pl.BlockSpec((None, fc, d), lambda i, k: (k, 0, 0))
    act = pl.BlockSpec((None, tm, fc), lambda i, k: (k, i, 0))
    return _pallas(
        body, name=name, args=[dy, x, gain, a, b, wg, wu, wd],
        out_shape=[jax.ShapeDtypeStruct((t, d), F32), jax.ShapeDtypeStruct((nch, t, fc), BF16),
                   jax.ShapeDtypeStruct((nch, t, fc), BF16), jax.ShapeDtypeStruct((nt, 1, d), F32)],
        grid=(nt, nch),
        in_specs=[row, row, pl.BlockSpec((1, d), lambda i, k: (0, 0)), act, act, chunk, chunk, chunk],
        out_specs=[row, act, act, pl.BlockSpec((None, 1, d), lambda i, k: (i, 0, 0))],
        scratch_shapes=[pltpu.VMEM((tm, d), F32)], plan=plan)


def _ffn_bwd_w(pairs, name, plan=None):
    n = len(pairs)
    nch, t, fc = pairs[0][0].shape
    d = pairs[0][1].shape[1]
    tm = min(FFN_TILE, t)

    def body(*refs):
        @pl.when(pl.program_id(1) == 0)
        def _():
            for o_ref in refs[2 * n:]:
                o_ref[...] = jnp.zeros_like(o_ref)

        for j in range(n):
            refs[2 * n + j][...] += _dot_tn(refs[j][...], refs[n + j][...])

    row = pl.BlockSpec((tm, d), lambda k, i: (i, 0))
    act = pl.BlockSpec((None, tm, fc), lambda k, i: (k, i, 0))
    chunk = pl.BlockSpec((None, fc, d), lambda k, i: (k, 0, 0))
    return _pallas(body, name=name, args=[p[0] for p in pairs] + [p[1] for p in pairs],
                   out_shape=[jax.ShapeDtypeStruct((nch, fc, d), F32)] * n, grid=(nch, t // tm),
                   in_specs=[act] * n + [row] * n, out_specs=[chunk] * n, plan=plan)


def _head_masks():
    lane = lax.broadcasted_iota(jnp.int32, (1, LANES), 1)
    return lane < HEAD_DIM


def _head_rms(x, lo):
    x2 = x * x
    s0 = jnp.sum(jnp.where(lo, x2, 0.0), axis=1, keepdims=True)
    s1 = jnp.sum(jnp.where(lo, 0.0, x2), axis=1, keepdims=True)
    return jnp.where(lo, lax.rsqrt(s0 * (1.0 / HEAD_DIM) + EPS), lax.rsqrt(s1 * (1.0 / HEAD_DIM) + EPS))


def _head_mean(v, lo):
    s0 = jnp.sum(jnp.where(lo, v, 0.0), axis=1, keepdims=True)
    s1 = jnp.sum(jnp.where(lo, 0.0, v), axis=1, keepdims=True)
    return jnp.where(lo, s0, s1) * (1.0 / HEAD_DIM)


def _mix_proj(x1, gain, wt, qn, kn, pool_width, attn_width):
    t, d = x1.shape
    tm = min(512, t)
    nt = t // tm
    scale = HEAD_DIM ** -0.5
    c_q, c_k, c_v = pool_width, pool_width + attn_width, pool_width + 2 * attn_width
    c_f = c_v + attn_width

    def body(x_ref, g_ref, wt_ref, qn_ref, kn_ref, hm_ref, pv_ref, q_ref, k_ref, qh_ref, kh_ref, vb_ref, f_ref):
        lo = _head_masks()
        for rows in _row_halves(tm):
            xf = x_ref[rows, :]
            hm = ((xf * _rms(xf)) * g_ref[...]).astype(BF16)
            hm_ref[rows, :] = hm
            f_ref[rows, :] = _dot_nt(hm, wt_ref[c_f:c_f + LANES, :])
            pv_ref[rows, :] = _dot_nt(hm, wt_ref[0:pool_width, :])
            vb_ref[rows, :] = _dot_nt(hm, wt_ref[c_v:c_v + attn_width, :]).astype(BF16)
            for c0, raw_ref, hat_ref, n_ref, mul in ((c_q, q_ref, qh_ref, qn_ref, scale), (c_k, k_ref, kh_ref, kn_ref, 1.0)):
                raw = _dot_nt(hm, wt_ref[c0:c0 + attn_width, :])
                raw_ref[rows, :] = raw
                for blk in range(attn_width // LANES):
                    sl = slice(blk * LANES, (blk + 1) * LANES)
                    xb = raw[:, sl]
                    hat_ref[rows, sl] = (((xb * _head_rms(xb, lo)) * n_ref[:, sl]) * mul).astype(BF16)

    row = pl.BlockSpec((tm, d), lambda i: (i, 0))
    half = pl.BlockSpec((tm, attn_width), lambda i: (i, 0))
    const = lambda shape: pl.BlockSpec(shape, lambda i: (0, 0))
    return _pallas(
        body, name="mix_proj", args=[x1, gain, wt, qn, kn],
        out_shape=[jax.ShapeDtypeStruct((t, d), BF16), jax.ShapeDtypeStruct((t, pool_width), F32),
                   jax.ShapeDtypeStruct((t, attn_width), F32), jax.ShapeDtypeStruct((t, attn_width), F32),
                   jax.ShapeDtypeStruct((t, attn_width), BF16), jax.ShapeDtypeStruct((t, attn_width), BF16),
                   jax.ShapeDtypeStruct((t, attn_width), BF16), jax.ShapeDtypeStruct((t, LANES), F32)],
        grid=(nt,),
        in_specs=[row, const((1, d)), const(wt.shape), const((1, attn_width)), const((1, attn_width))],
        out_specs=[row, pl.BlockSpec((tm, pool_width), lambda i: (i, 0)), half, half, half, half, half,
                   pl.BlockSpec((tm, LANES), lambda i: (i, 0))])[0]


def _shift_down(v, dist, row):
    return jnp.where(row >= dist, pltpu.roll(v, dist, 0), 0.0)


def _shift_up(v, dist, row, n):
    return jnp.where(row + dist < n, pltpu.roll(v, n - dist, 0), 0.0)


def _aug_lane(e):
    return HEAD_DIM if e == 0 else 0


def _forget_prefix(f, bias, qh, kh, n_batch, seq):
    def body(f_ref, b_ref, q_ref, k_ref, qa_ref, ka_ref):
        z = f_ref[...] + b_ref[...]
        acc = jnp.minimum(z, 0.0) - jnp.log(1.0 + jnp.exp(-jnp.abs(z)))
        row = lax.broadcasted_iota(jnp.int32, (seq, 1), 0)
        dist = 1
        while dist < seq:
            acc = acc + _shift_down(acc, dist, row)
            dist *= 2
        lane = lax.broadcasted_iota(jnp.int32, (1, LANES), 1)
        for h in range(N_HEADS):
            pair, e = divmod(h, 2)
            a0 = _aug_lane(e)
            own = (lane < HEAD_DIM) if e == 0 else (lane >= HEAD_DIM)
            fh = _pick_lane(acc, h)
            hi = fh.astype(BF16).astype(F32)
            rest = fh - hi
            mid = rest.astype(BF16).astype(F32)
            low = rest - mid
            q_ones = (lane >= a0 + 3) & (lane < a0 + 6)
            k_ones = (lane >= a0) & (lane < a0 + 3)
            q_aug = jnp.where(lane == a0, hi, jnp.where(lane == a0 + 1, mid, jnp.where(lane == a0 + 2, low,
                              jnp.where(q_ones, 1.0, 0.0))))
            k_aug = jnp.where(k_ones, 1.0, jnp.where(lane == a0 + 3, -hi, jnp.where(lane == a0 + 4, -mid,
                              jnp.where(lane == a0 + 5, -low, 0.0))))
            src = slice(pair * LANES, (pair + 1) * LANES)
            dst = slice(h * LANES, (h + 1) * LANES)
            qa_ref[:, dst] = jnp.where(own, q_ref[:, src].astype(F32), q_aug).astype(BF16)
            ka_ref[:, dst] = jnp.where(own, k_ref[:, src].astype(F32), k_aug).astype(BF16)

    width = qh.shape[1]
    tok = pl.BlockSpec((seq, width), lambda b: (b, 0))
    aug = pl.BlockSpec((seq, N_HEADS * LANES), lambda b: (b, 0))
    return pl.pallas_call(
        body, out_shape=[jax.ShapeDtypeStruct((n_batch * seq, N_HEADS * LANES), BF16)] * 2, grid=(n_batch,),
        in_specs=[pl.BlockSpec((seq, LANES), lambda b: (b, 0)), pl.BlockSpec((1, LANES), lambda b: (0, 0)), tok, tok],
        out_specs=[aug, aug], compiler_params=_params(), name="forget_prefix",
    )(f, bias, qh, kh)


def _pool_groups(pv_ref, pw_ref, ps_ref, seq):
    row = lax.broadcasted_iota(jnp.int32, (seq, 1), 0)
    pos = (row + 1).astype(F32)
    out = []
    for g, win in enumerate(POOL_WINDOWS):
        sl = slice(g * LANES, (g + 1) * LANES)
        xg = pv_ref[:, sl]
        acc = xg
        dist = 1
        while dist < win:
            acc = acc + _shift_down(acc, dist, row)
            dist *= 2
        pooled = (acc / jnp.minimum(pos, float(win)) - xg).astype(BF16)
        mixed = _dot(pooled, pw_ref[g])
        out.append((pooled, mixed, mixed * ps_ref[:, sl]))
    return out


def _pool_fwd(pv, pw, ps, onp, n_batch, seq):
    width = pv.shape[1]

    def body(pv_ref, pw_ref, ps_ref, on_ref, y_ref):
        groups = _pool_groups(pv_ref, pw_ref, ps_ref, seq)
        ssq = sum(jnp.sum(ms * ms, axis=1, keepdims=True) for _, _, ms in groups)
        r = lax.rsqrt(ssq * (1.0 / width) + EPS)
        for g, (_, _, ms) in enumerate(groups):
            sl = slice(g * LANES, (g + 1) * LANES)
            y_ref[:, sl] = ((ms * r) * on_ref[:, sl]).astype(BF16)

    return pl.pallas_call(
        body, out_shape=jax.ShapeDtypeStruct((n_batch * seq, width), BF16), grid=(n_batch,),
        in_specs=[pl.BlockSpec((seq, width), lambda b: (b, 0)), pl.BlockSpec(pw.shape, lambda b: (0, 0, 0)),
                  pl.BlockSpec((1, width), lambda b: (0, 0)), pl.BlockSpec((1, width), lambda b: (0, 0))],
        out_specs=pl.BlockSpec((seq, width), lambda b: (b, 0)),
        compiler_params=_params(), name="pool_fwd",
    )(pv, pw, ps, onp)


def _pool_bwd(pv, dyp, pw, ps, onp, n_batch, seq):
    width = pv.shape[1]

    def body(pv_ref, dy_ref, pw_ref, ps_ref, on_ref, dpv_ref, dpw_ref, dps_ref, don_ref):
        groups = _pool_groups(pv_ref, pw_ref, ps_ref, seq)
        ssq = sum(jnp.sum(ms * ms, axis=1, keepdims=True) for _, _, ms in groups)
        r = lax.rsqrt(ssq * (1.0 / width) + EPS)
        mean = sum(jnp.sum((dy_ref[:, g * LANES:(g + 1) * LANES] * on_ref[:, g * LANES:(g + 1) * LANES]) * (ms * r),
                           axis=1, keepdims=True) for g, (_, _, ms) in enumerate(groups)) * (1.0 / width)
        row = lax.broadcasted_iota(jnp.int32, (seq, 1), 0)
        pos = (row + 1).astype(F32)
        for g, (pooled, mixed, ms) in enumerate(groups):
            sl = slice(g * LANES, (g + 1) * LANES)
            dy = dy_ref[:, sl]
            xh = ms * r
            don_ref[:, sl] = jnp.sum(dy * xh, axis=0, keepdims=True)
            dms = r * (dy * on_ref[:, sl] - xh * mean)
            dps_ref[:, sl] = jnp.sum(dms * mixed, axis=0, keepdims=True)
            dmix = (dms * ps_ref[:, sl]).astype(BF16)
            dpw_ref[g] = _dot_tn(pooled, dmix)
            dpool = _dot_nt(dmix, pw_ref[g])
            win = POOL_WINDOWS[g]
            acc = dpool / jnp.minimum(pos, float(win))
            dist = 1
            while dist < win:
                acc = acc + _shift_up(acc, dist, row, seq)
                dist *= 2
            dpv_ref[:, sl] = (acc - dpool).astype(BF16)

    tok = pl.BlockSpec((seq, width), lambda b: (b, 0))
    vec = pl.BlockSpec((1, width), lambda b: (0, 0))
    pvec = pl.BlockSpec((None, 1, width), lambda b: (b, 0, 0))
    return pl.pallas_call(
        body,
        out_shape=[jax.ShapeDtypeStruct((n_batch * seq, width), BF16),
                   jax.ShapeDtypeStruct((n_batch,) + pw.shape, F32),
                   jax.ShapeDtypeStruct((n_batch, 1, width), F32), jax.ShapeDtypeStruct((n_batch, 1, width), F32)],
        grid=(n_batch,),
        in_specs=[tok, tok, pl.BlockSpec(pw.shape, lambda b: (0, 0, 0)), vec, vec],
        out_specs=[tok, pl.BlockSpec((None,) + pw.shape, lambda b: (b, 0, 0, 0)), pvec, pvec],
        compiler_params=_params(), name="pool_bwd",
    )(pv, dyp, pw, ps, onp)


def _pick_lane(tile, idx):
    lane = lax.broadcasted_iota(jnp.int32, (1, LANES), 1)
    return jnp.sum(jnp.where(lane == idx, tile, 0.0), axis=1, keepdims=True)


def _pick_row(tile, idx):
    sub = lax.broadcasted_iota(jnp.int32, (tile.shape[0], 1), 0)
    return jnp.sum(jnp.where(sub == idx, tile, 0.0), axis=0, keepdims=True)


def _put_lane(col, idx):
    lane = lax.broadcasted_iota(jnp.int32, (1, LANES), 1)
    return jnp.where(lane == idx, col, 0.0)


def _head_select(e):
    lo = _head_masks()
    return lo if e == 0 else jnp.logical_not(lo)


def _causal(st, shift):
    row = lax.broadcasted_iota(jnp.int32, st.shape, 0)
    col = lax.broadcasted_iota(jnp.int32, st.shape, 1) + shift
    return jnp.where(col >= row, st, NEG)


def _transpose_blocks(a):
    rows, cols = a.shape
    return jnp.concatenate(
        [jnp.concatenate([a[r:r + LANES, c:c + LANES].T for r in range(0, rows, LANES)], axis=1)
         for c in range(0, cols, LANES)], axis=0)


def _accumulate(ref, value, first):
    @pl.when(first)
    def _():
        ref[...] = value

    @pl.when(jnp.logical_not(first))
    def _():
        ref[...] += value


def _attn_fwd(qa, ka, vb, n_batch, seq, plan=None):
    tq = min(ATT_BLOCK, seq)
    nq, nsub, tk = seq // tq, tq // ATT_SUB, tq
    pairs = vb.shape[1] // LANES

    def body(q_ref, k_ref, v_ref, o_ref, lse_ref, acc_ref):
        i, p = pl.program_id(1), pl.program_id(2)
        row_lo = lax.broadcasted_iota(jnp.int32, (LANES, 1), 0) < HEAD_DIM
        qs = [q_ref[:, e * LANES:(e + 1) * LANES] for e in range(2)]
        acc_ref[...] = jnp.zeros_like(acc_ref)

        def tile(off, stats, diagonal):
            vj = v_ref[pl.ds(off, tk), :]
            new, alphas, pvs = [], [], []
            for e in range(2):
                st = _dot_nt(k_ref[pl.ds(off, tk), e * LANES:(e + 1) * LANES], qs[e])
                if diagonal:
                    st = _causal(st, 0)
                m, l = stats[e]
                m_new = jnp.maximum(m, jnp.max(st, axis=0, keepdims=True))
                alpha = jnp.exp(m - m_new)
                pt = jnp.exp(st - m_new)
                new.append((m_new, alpha * l + jnp.sum(pt, axis=0, keepdims=True)))
                alphas.append(alpha)
                pvs.append(_dot_tn(jnp.where(_head_select(e), vj, jnp.zeros_like(vj)), pt.astype(BF16)))
            acc_ref[...] = acc_ref[...] * jnp.where(row_lo, alphas[0], alphas[1]) + (pvs[0] + pvs[1])
            return tuple(new)

        init = ((jnp.full((1, tq), NEG, F32), jnp.zeros((1, tq), F32)),) * 2
        stats = lax.fori_loop(0, i, lambda j, st: tile(pl.multiple_of(j * tk, tk), st, False), init)
        (m0, l0), (m1, l1) = tile(pl.multiple_of(i * tk, tk), stats, True)
        out_t = acc_ref[...] / jnp.where(row_lo, l0, l1)
        sub = lax.broadcasted_iota(jnp.int32, (8, 1), 0)
        lse0, lse1 = m0 + jnp.log(l0), m1 + jnp.log(l1)
        for a in range(nsub):
            sl = slice(a * ATT_SUB, (a + 1) * ATT_SUB)
            o_ref[sl, :] = out_t[:, sl].T
            rows = jnp.where(sub == 2 * p, lse0[:, sl], 0.0) + jnp.where(sub == 2 * p + 1, lse1[:, sl], 0.0)
            _accumulate(lse_ref.at[a], rows, p == 0)

    return _pallas(
        body, name="attn_fwd", args=[qa, ka, vb],
        out_shape=[jax.ShapeDtypeStruct((n_batch * seq, pairs * LANES), F32),
                   jax.ShapeDtypeStruct((n_batch * seq // ATT_SUB, 8, ATT_SUB), F32)],
        grid=(n_batch, nq, pairs),
        in_specs=[pl.BlockSpec((tq, 2 * LANES), lambda b, i, p: (b * nq + i, p)),
                  pl.BlockSpec((seq, 2 * LANES), lambda b, i, p: (b, p)),
                  pl.BlockSpec((seq, LANES), lambda b, i, p: (b, p))],
        out_specs=[pl.BlockSpec((tq, LANES), lambda b, i, p: (b * nq + i, p)),
                   pl.BlockSpec((nsub, 8, ATT_SUB), lambda b, i, p: (b * nq + i, 0, 0))],
        scratch_shapes=[pltpu.VMEM((LANES, tq), F32)], plan=plan)


def _attn_bwd(qa, ka, vb, do, lse, delta, n_batch, seq, plan=None):
    tq = min(ATT_BLOCK, seq)
    nq, nsub = seq // tq, tq // ATT_SUB
    n_tiles = seq // ATT_SUB
    pairs = vb.shape[1] // LANES

    def body(q_ref, k_ref, v_ref, do_ref, lse_ref, dl_ref, dq_ref, dk_ref, dv_ref, dfq_ref, dfk_ref,
             dq0_ref, dq1_ref, dk0_ref, dk1_ref, dva_ref):
        p = pl.program_id(1)
        dqs, dks = (dq0_ref, dq1_ref), (dk0_ref, dk1_ref)
        for acc in (dk0_ref, dk1_ref, dva_ref):
            acc[...] = jnp.zeros_like(acc)
        dfq_cols = []
        for i in range(nq):
            rows_i = slice(i * tq, (i + 1) * tq)
            qs = [q_ref[rows_i, e * LANES:(e + 1) * LANES] for e in range(2)]
            dov = do_ref[rows_i, :]
            does = [jnp.where(_head_select(e), dov, jnp.zeros_like(dov)) for e in range(2)]
            stat = lambda ref, e: jnp.concatenate([_pick_row(ref[i * nsub + a], 2 * p + e) for a in range(nsub)], axis=1)
            ls, dl = [stat(lse_ref, e) for e in range(2)], [stat(dl_ref, e) for e in range(2)]
            for acc in dqs:
                acc[...] = jnp.zeros_like(acc)

            def tile(off, diagonal, qs=qs, dov=dov, does=does, ls=ls, dl=dl):
                vj = v_ref[pl.ds(off, tq), :]
                for e in range(2):
                    kj = k_ref[pl.ds(off, tq), e * LANES:(e + 1) * LANES]
                    st = _dot_nt(kj, qs[e])
                    if diagonal:
                        st = _causal(st, 0)
                    pt = jnp.exp(st - ls[e])
                    dva_ref[pl.ds(off, tq), :] += _dot(pt.astype(BF16), does[e])
                    dpt = _dot_nt(jnp.where(_head_select(e), vj, jnp.zeros_like(vj)), dov)
                    dst = (pt * (dpt - dl[e])).astype(BF16)
                    dks[e][pl.ds(off, tq), :] += _dot(dst, qs[e])
                    dqs[e][...] += _dot(_transpose_blocks(kj), dst)

            def step(j, carry, tile=tile):
                tile(pl.multiple_of(j * tq, tq), False)
                return carry

            lax.fori_loop(0, i, step, 0)
            tile(i * tq, True)
            dq0, dq1 = _transpose_blocks(dq0_ref[...]), _transpose_blocks(dq1_ref[...])
            dq_ref[rows_i, :] = jnp.where(_head_masks(), dq0, dq1)
            dfq_cols.append(_put_lane(_pick_lane(dq0, _aug_lane(0)), 2 * p) + _put_lane(_pick_lane(dq1, _aug_lane(1)), 2 * p + 1))
        dk0, dk1 = dk0_ref[...], dk1_ref[...]
        dk_ref[...] = jnp.where(_head_masks(), dk0, dk1)
        dv_ref[...] = dva_ref[...].astype(BF16)
        dfk = _put_lane(_pick_lane(dk0, _aug_lane(0) + 3), 2 * p) + _put_lane(_pick_lane(dk1, _aug_lane(1) + 3), 2 * p + 1)
        _accumulate(dfq_ref, jnp.concatenate(dfq_cols, axis=0), p == 0)
        _accumulate(dfk_ref, -dfk, p == 0)

    wide = pl.BlockSpec((seq, 2 * LANES), lambda b, p: (b, p))
    blk = pl.BlockSpec((seq, LANES), lambda b, p: (b, p))
    col = pl.BlockSpec((seq, LANES), lambda b, p: (b, 0))
    stat = pl.BlockSpec((n_tiles, 8, ATT_SUB), lambda b, p: (b, 0, 0))
    f32_blk, acc = jax.ShapeDtypeStruct((n_batch * seq, pairs * LANES), F32), pltpu.VMEM((seq, LANES), F32)
    return _pallas(
        body, name="attn_bwd", args=[qa, ka, vb, do, lse, delta],
        out_shape=[f32_blk, f32_blk, jax.ShapeDtypeStruct((n_batch * seq, pairs * LANES), BF16),
                   jax.ShapeDtypeStruct((n_batch * seq, LANES), F32), jax.ShapeDtypeStruct((n_batch * seq, LANES), F32)],
        grid=(n_batch, pairs), in_specs=[wide, wide, blk, blk, stat, stat], out_specs=[blk, blk, blk, col, col],
        scratch_shapes=[pltpu.VMEM((LANES, tq), F32), pltpu.VMEM((LANES, tq), F32), acc, acc, acc], plan=plan)


def _forget_bwd(dfq, dfk, f, bias, n_batch, seq):
    def body(dfq_ref, dfk_ref, f_ref, b_ref, df_ref, db_ref):
        acc = dfq_ref[...] + dfk_ref[...]
        row = lax.broadcasted_iota(jnp.int32, (seq, 1), 0)
        dist = 1
        while dist < seq:
            acc = acc + _shift_up(acc, dist, row, seq)
            dist *= 2
        df = acc * _sigmoid(-(f_ref[...] + b_ref[...]))
        df_ref[...] = df
        db_ref[...] = jnp.sum(df, axis=0, keepdims=True)

    col = pl.BlockSpec((seq, LANES), lambda b: (b, 0))
    return pl.pallas_call(
        body,
        out_shape=[jax.ShapeDtypeStruct((n_batch * seq, LANES), F32), jax.ShapeDtypeStruct((n_batch, 1, LANES), F32)],
        grid=(n_batch,), in_specs=[col, col, col, pl.BlockSpec((1, LANES), lambda b: (0, 0))],
        out_specs=[col, pl.BlockSpec((None, 1, LANES), lambda b: (b, 0, 0))],
        compiler_params=_params(), name="forget_bwd",
    )(dfq, dfk, f, bias)


def _mix_out(x1, yp, o, ona, woa, wob):
    t, d = x1.shape
    width = o.shape[1]
    tm = min(512, t)

    def body(x_ref, yp_ref, o_ref, on_ref, wa_ref, wb_ref, x2_ref, ya_ref):
        of = o_ref[...]
        ya = ((of * _rms(of)) * on_ref[...]).astype(BF16)
        ya_ref[...] = ya
        x2_ref[...] = x_ref[...] + (_dot(yp_ref[...], wa_ref[...]) + _dot(ya, wb_ref[...]))

    row = pl.BlockSpec((tm, d), lambda i: (i, 0))
    half = pl.BlockSpec((tm, width), lambda i: (i, 0))
    wspec = pl.BlockSpec((width, d), lambda i: (0, 0))
    return pl.pallas_call(
        body, out_shape=[jax.ShapeDtypeStruct((t, d), F32), jax.ShapeDtypeStruct((t, width), BF16)],
        grid=(t // tm,), in_specs=[row, half, half, pl.BlockSpec((1, width), lambda i: (0, 0)), wspec, wspec],
        out_specs=[row, half], compiler_params=_params(), name="mix_out",
    )(x1, yp, o, ona, woa, wob)


def _mix_out_bwd(dx2, o, yp, ya, ona, woa, wob, plan=None):
    t, d = dx2.shape
    width = o.shape[1]
    tm = min(512, t)
    nt = t // tm

    def body(dx_ref, o_ref, yp_ref, ya_ref, on_ref, wa_ref, wb_ref, dyp_ref, do_ref, dl_ref, dwa_ref, dwb_ref, don_ref):
        @pl.when(pl.program_id(0) == 0)
        def _():
            dwa_ref[...] = jnp.zeros_like(dwa_ref)
            dwb_ref[...] = jnp.zeros_like(dwb_ref)

        dxb = dx_ref[...].astype(BF16)
        dwa_ref[...] += _dot_tn(yp_ref[...], dxb)
        dwb_ref[...] += _dot_tn(ya_ref[...], dxb)
        dyp_ref[...] = _dot_nt(dxb, wa_ref[...])
        of = o_ref[...]
        dov, dgr = _rms_bwd(of, _rms(of), on_ref[...], _dot_nt(dxb, wb_ref[...]))
        don_ref[...] = jnp.sum(dgr, axis=0, keepdims=True)
        do_ref[...] = dov.astype(BF16)
        lo = _head_masks()
        prod = dov * of
        delta = jnp.zeros((tm, LANES), F32)
        for blk in range(width // LANES):
            pb = prod[:, blk * LANES:(blk + 1) * LANES]
            delta = delta + _put_lane(jnp.sum(jnp.where(lo, pb, 0.0), axis=1, keepdims=True), 2 * blk)
            delta = delta + _put_lane(jnp.sum(jnp.where(lo, 0.0, pb), axis=1, keepdims=True), 2 * blk + 1)
        for c in range(tm // ATT_SUB):
            dl_ref[c] = delta[c * ATT_SUB:(c + 1) * ATT_SUB, :].T[0:8, :]

    row = pl.BlockSpec((tm, d), lambda i: (i, 0))
    half = pl.BlockSpec((tm, width), lambda i: (i, 0))
    wspec = pl.BlockSpec((width, d), lambda i: (0, 0))
    return _pallas(
        body, name="mix_out_bwd", args=[dx2, o, yp, ya, ona, woa, wob],
        out_shape=[jax.ShapeDtypeStruct((t, width), F32), jax.ShapeDtypeStruct((t, width), BF16),
                   jax.ShapeDtypeStruct((t // ATT_SUB, 8, ATT_SUB), F32), jax.ShapeDtypeStruct((width, d), F32),
                   jax.ShapeDtypeStruct((width, d), F32), jax.ShapeDtypeStruct((nt, 1, width), F32)],
        grid=(nt,),
        in_specs=[row, half, half, half, pl.BlockSpec((1, width), lambda i: (0, 0)), wspec, wspec],
        out_specs=[half, half, pl.BlockSpec((tm // ATT_SUB, 8, ATT_SUB), lambda i: (i, 0, 0)), wspec, wspec,
                   pl.BlockSpec((None, 1, width), lambda i: (i, 0, 0))], plan=plan)


def _mix_in_bwd(dx2, x1, gain, hm, dpv, dqh, q, dkh, k, dv, df, qn, kn, wt):
    t, d = x1.shape
    width = q.shape[1]
    pool_width = dpv.shape[1]
    tm = min(512, t)
    nt = t // tm
    scale = HEAD_DIM ** -0.5
    c_q, c_k, c_v = pool_width, pool_width + width, pool_width + 2 * width
    c_f = c_v + width

    def body(dx2_ref, x_ref, g_ref, hm_ref, dpv_ref, dqh_ref, q_ref, dkh_ref, k_ref, dv_ref, df_ref, qn_ref, kn_ref,
             wt_ref, dx_ref, dxh_ref, dwt_ref, dg_ref, dqn_ref, dkn_ref):
        @pl.when(pl.program_id(0) == 0)
        def _():
            dwt_ref[...] = jnp.zeros_like(dwt_ref)

        lo = _head_masks()
        for part, rows in enumerate(_row_halves(tm)):
            def put(ref, sl, value):
                ref[:, sl] = value if part == 0 else ref[:, sl] + value

            hm = hm_ref[rows, :]
            pieces = [(0, dpv_ref[rows, :])]
            for c0, raw_ref, dh_ref, n_ref, dn_ref, mul in ((c_q, q_ref, dqh_ref, qn_ref, dqn_ref, scale),
                                                           (c_k, k_ref, dkh_ref, kn_ref, dkn_ref, 1.0)):
                cols = []
                for blk in range(width // LANES):
                    sl = slice(blk * LANES, (blk + 1) * LANES)
                    xb = raw_ref[rows, sl]
                    gb = dh_ref[rows, sl] * mul
                    r = _head_rms(xb, lo)
                    xh = xb * r
                    dyg = gb * n_ref[:, sl]
                    cols.append((r * (dyg - xh * _head_mean(dyg * xh, lo))).astype(BF16))
                    put(dn_ref, sl, jnp.sum(gb * xh, axis=0, keepdims=True))
                pieces.append((c0, jnp.concatenate(cols, axis=1)))
            pieces.append((c_v, dv_ref[rows, :]))
            pieces.append((c_f, df_ref[rows, :].astype(BF16)))
            dhm = jnp.zeros((tm // 2, d), F32)
            for c0, piece in pieces:
                dwt_ref[c0:c0 + piece.shape[1], :] += _dot_tn(piece, hm)
                dhm = dhm + _dot(piece, wt_ref[c0:c0 + piece.shape[1], :])
            xf = x_ref[rows, :]
            dxn, dgr = _rms_bwd(xf, _rms(xf), g_ref[...], dhm)
            dx = dx2_ref[rows, :] + dxn
            dx_ref[rows, :] = dx
            dxh_ref[rows, :] = (0.5 * dx).astype(BF16)
            put(dg_ref, slice(None), jnp.sum(dgr, axis=0, keepdims=True))

    row = pl.BlockSpec((tm, d), lambda i: (i, 0))
    half = pl.BlockSpec((tm, width), lambda i: (i, 0))
    const = lambda shape: pl.BlockSpec(shape, lambda i: (0, 0))
    pvec = lambda n: pl.BlockSpec((None, 1, n), lambda i: (i, 0, 0))
    return pl.pallas_call(
        body,
        out_shape=[jax.ShapeDtypeStruct((t, d), F32), jax.ShapeDtypeStruct((t, d), BF16), jax.ShapeDtypeStruct(wt.shape, F32),
                   jax.ShapeDtypeStruct((nt, 1, d), F32),
                   jax.ShapeDtypeStruct((nt, 1, width), F32), jax.ShapeDtypeStruct((nt, 1, width), F32)],
        grid=(nt,),
        in_specs=[row, row, const((1, d)), row, pl.BlockSpec((tm, pool_width), lambda i: (i, 0)), half, half, half, half,
                  half, pl.BlockSpec((tm, LANES), lambda i: (i, 0)), const((1, width)), const((1, width)),
                  const(wt.shape)],
        out_specs=[row, row, const(wt.shape), pvec(d), pvec(width), pvec(width)],
        compiler_params=_params(), name="mix_in_bwd",
    )(dx2, x1, gain, hm, dpv, dqh, q, dkh, k, dv, df, qn, kn, wt)


def _mesh_pos():
    return lax.axis_index("x"), lax.axis_index("y"), lax.axis_index("c")


def _other_chips(x, y):
    return [(1 - x, y), (x, 1 - y), (1 - x, 1 - y)]


def _remote(src, dst, send_sem, recv_sem, device):
    return pltpu.make_async_remote_copy(src_ref=src, dst_ref=dst, send_sem=send_sem, recv_sem=recv_sem,
                                        device_id=device, device_id_type=pl.DeviceIdType.MESH)


def _half_rows(n_rows, which):
    half = n_rows // 2
    return pl.ds(pl.multiple_of(which * half, 8), half)


def _row_block(rows, cols, itemsize=4):
    rb = rows
    while rb * cols * itemsize > (2 << 20) and rb % 32 == 0:
        rb //= 2
    return rb


def _place_cast(ws, chip, tag):
    n = len(ws)
    rows, cols = ws[0].shape
    rb = _row_block(rows, cols)

    def body(k_ref, *refs):
        for w_ref, o_ref in zip(refs[:n], refs[n:]):
            o_ref[...] = w_ref[...].astype(BF16)

    return pl.pallas_call(
        body, out_shape=[jax.ShapeDtypeStruct((N_CHIPS, rows, cols), BF16)] * n,
        grid_spec=pltpu.PrefetchScalarGridSpec(
            num_scalar_prefetch=1, grid=(rows // rb,),
            in_specs=[pl.BlockSpec((rb, cols), lambda i, k: (i, 0))] * n,
            out_specs=[pl.BlockSpec((None, rb, cols), lambda i, k: (k[0], i, 0))] * n),
        compiler_params=_params(), name="place_" + tag,
    )(chip, *ws)


class _Plan:
    def __init__(self, ins, outs, alias, sems, start, finish, middle=None, middle_at=(3, 4)):
        self.ins, self.outs, self.alias, self.sems = ins, outs, alias, sems
        self.start, self.middle, self.finish, self.middle_at = start, middle, finish, middle_at


def _merge_plans(a, b):
    ni, no, ns = len(a.ins), len(a.outs), len(a.sems)
    alias = dict(a.alias)
    alias.update({ni + i: no + o for i, o in b.alias.items()})

    def both(which):
        stage_a, stage_b = getattr(a, which), getattr(b, which)
        if stage_a is None and stage_b is None:
            return None

        def run(ins, outs, sems):
            if stage_a is not None:
                stage_a(ins[:ni], outs[:no], sems[:ns])
            if stage_b is not None:
                stage_b(ins[ni:], outs[no:], sems[ns:])
        return run

    return _Plan(list(a.ins) + list(b.ins), list(a.outs) + list(b.outs), alias, list(a.sems) + list(b.sems),
                 both("start"), both("finish"), both("middle"), a.middle_at if a.middle is not None else b.middle_at)


def _run_plan(plan, name):
    n_in, n_out = len(plan.ins), len(plan.outs)

    def body(*refs):
        parts = refs[:n_in], refs[n_in:n_in + n_out], refs[n_in + n_out:]
        plan.start(*parts)
        if plan.middle is not None:
            plan.middle(*parts)
        plan.finish(*parts)

    return pl.pallas_call(
        body, out_shape=plan.outs, in_specs=[ANY] * n_in, out_specs=[ANY] * n_out, scratch_shapes=plan.sems,
        input_output_aliases=plan.alias, name=name,
    )(*plan.ins)


def _pallas(body, *, name, args, in_specs, out_shape, out_specs, grid, scratch_shapes=(), plan=None, aliases=None):
    n_in, n_out, n_scr = len(args), len(out_shape), len(scratch_shapes)
    plan = plan or _Plan([], [], {}, [], None, None)
    p_in, p_out = len(plan.ins), len(plan.outs)

    def carrying(*refs):
        ins, p_ins = refs[:n_in], refs[n_in:n_in + p_in]
        o0 = n_in + p_in
        outs, p_outs = refs[o0:o0 + n_out], refs[o0 + n_out:o0 + n_out + p_out]
        s0 = o0 + n_out + p_out
        scr, p_sems = refs[s0:s0 + n_scr], refs[s0 + n_scr:]
        ids = [pl.program_id(a) for a in range(len(grid))]

        if plan.start is not None:
            @pl.when(functools.reduce(jnp.logical_and, [i == 0 for i in ids]))
            def _():
                plan.start(p_ins, p_outs, p_sems)

        body(*ins, *outs, *scr)

        if plan.middle is not None:
            step, n_steps = 0, 1
            for i, g in zip(ids, grid):
                step, n_steps = step * g + i, n_steps * g

            @pl.when(step == (plan.middle_at[0] * n_steps) // plan.middle_at[1])
            def _():
                plan.middle(p_ins, p_outs, p_sems)

        if plan.finish is not None:
            @pl.when(functools.reduce(jnp.logical_and, [i == g - 1 for i, g in zip(ids, grid)]))
            def _():
                plan.finish(p_ins, p_outs, p_sems)

    aliases = dict(aliases or {})
    aliases.update({n_in + i: n_out + o for i, o in plan.alias.items()})
    res = pl.pallas_call(
        carrying, out_shape=list(out_shape) + list(plan.outs), grid=grid,
        in_specs=list(in_specs) + [ANY] * p_in, out_specs=list(out_specs) + [ANY] * p_out,
        scratch_shapes=list(scratch_shapes) + list(plan.sems),
        input_output_aliases=aliases, compiler_params=_params(), name=name,
    )(*args, *plan.ins)
    return list(res[:n_out]), list(res[n_out:])


def _plan_gather(stacks):
    n = len(stacks)
    relations = range(3)

    def ici_copies(outs, sems):
        x, y, c = _mesh_pos()
        chips = _other_chips(x, y)
        cps = []
        for w in range(n):
            own = outs[w].at[2 * x + y, _half_rows(stacks[w].shape[1], c)]
            cps += [_remote(own, own, sems[0].at[w, j], sems[1].at[w, j], (*chips[j], c)) for j in relations]
        return cps

    def start(ins, outs, sems):
        for cp in ici_copies(outs, sems):
            cp.start()

    def forwards(outs, sems, core):
        x, y, c = _mesh_pos()
        slots = [2 * cx + cy for cx, cy in _other_chips(x, y)]
        cps = []
        for w in range(n):
            rows = _half_rows(stacks[w].shape[1], core)
            for j in relations:
                landed = outs[w].at[slots[j], rows]
                cps.append((_remote(landed, landed, sems[0].at[w, j], sems[1].at[w, j], (x, y, 1 - c)),
                            _remote(landed, landed, sems[2].at[w, j], sems[3].at[w, j], (x, y, 1 - c))))
        return cps

    def middle(ins, outs, sems):
        c = _mesh_pos()[2]
        for arrival, forward in forwards(outs, sems, c):
            arrival.wait_recv()
            forward.start()

    def finish(ins, outs, sems):
        c = _mesh_pos()[2]
        for _, forward in forwards(outs, sems, 1 - c):
            forward.wait_recv()
        for cp in ici_copies(outs, sems) + [forward for _, forward in forwards(outs, sems, c)]:
            cp.wait_send()

    return _Plan(stacks, [jax.ShapeDtypeStruct(s.shape, s.dtype) for s in stacks], {w: w for w in range(n)},
                 [pltpu.SemaphoreType.DMA((n, 3))] * 4, start, finish, middle)


RELAY_SEMS = [pltpu.SemaphoreType.DMA((3, 2))] * 4 + [pltpu.SemaphoreType.DMA((3, 3))] * 2
RELAY_STAGES = ("send", "pass on", "x neighbour", "y neighbour", "diagonal", "end")


def _relay_gather_stage(stage, outs, sems):
    assert stage in RELAY_STAGES
    send, recv, relay_send, relay_recv, d2d_send, d2d_recv = sems
    n = len(outs)
    rh = outs[0].shape[1] // 2
    mx, my, c = _mesh_pos()
    sibling = (mx, my, 1 - c)
    near = [(1 - mx, my), (mx, 1 - my)]
    slots = [2 * cx + cy for cx, cy in near] + [2 * (1 - mx) + (1 - my)]

    def piece(w, slot, core, quarter=None):
        if quarter is None:
            return outs[w].at[slot, _half_rows(2 * rh, core)]
        return outs[w].at[slot, pl.ds(pl.multiple_of(core * rh + quarter * (rh // 2), 8), rh // 2)]

    def to_near(w, j):
        own = piece(w, 2 * mx + my, c)
        return _remote(own, own, send.at[w, j], recv.at[w, j], (*near[j], c))

    def from_near(w, j):
        landed = piece(w, slots[j], c)
        return _remote(landed, landed, send.at[w, j], recv.at[w, j], sibling)

    def onward(w, j, slot):
        part = piece(w, slot, c, quarter=j)
        return _remote(part, part, relay_send.at[w, j], relay_recv.at[w, j], (*near[1 - j], c))

    def to_sibling(w, j, core):
        landed = piece(w, slots[j], core)
        return _remote(landed, landed, d2d_send.at[w, j], d2d_recv.at[w, j], sibling)

    if stage == "send":
        for w in range(n):
            for j in range(2):
                to_near(w, j).start()
    elif stage == "pass on":
        for w in range(n):
            for j in range(2):
                from_near(w, j).wait_recv()
                onward(w, j, slots[j]).start()
                to_sibling(w, j, c).start()
    elif stage in ("x neighbour", "y neighbour"):
        for w in range(n):
            to_sibling(w, ("x neighbour", "y neighbour").index(stage), 1 - c).wait_recv()
    elif stage == "diagonal":
        for w in range(n):
            for j in range(2):
                onward(w, j, slots[2]).wait_recv()
            to_sibling(w, 2, c).start()
        for w in range(n):
            to_sibling(w, 2, 1 - c).wait_recv()
    else:
        for w in range(n):
            for j in range(2):
                to_near(w, j).wait_send()
                onward(w, j, slots[j]).wait_send()
            for j in range(3):
                to_sibling(w, j, c).wait_send()


def _plan_gather_relay(stacks):
    def stages(which):
        def run(ins, outs, sems):
            for stage in which:
                _relay_gather_stage(stage, outs, sems)
        return run

    return _Plan(stacks, [jax.ShapeDtypeStruct(s.shape, s.dtype) for s in stacks], {w: w for w in range(len(stacks))},
                 RELAY_SEMS, stages(RELAY_STAGES[:1]), stages(RELAY_STAGES[2:]), stages(RELAY_STAGES[1:2]), middle_at=(5, 8))


def _ffn_fwd_gathering(x, gain, stacks, order, later):
    t, d = x.shape
    nch, fc, _ = stacks[0].shape
    assert nch == N_CHIPS
    n = len(stacks)
    tm = min(FFN_STAGED_TILE, t)
    nt = t // tm
    p_in, p_out = len(later.ins), len(later.outs)
    relay = _relay_gather_stage

    def body(order_ref, x_ref, g_ref, *refs):
        later_in, refs = refs[n:n + p_in], refs[n + p_in:]
        o_ref, h_ref, a_ref, b_ref, s_ref = refs[:5]
        stack_refs, later_out, refs = refs[5:5 + n], refs[5 + n:5 + n + p_out], refs[5 + n + p_out:]
        w_ref, hs_ref, acc_ref, w_sem = refs[:4]
        relay_sems, later_sems = refs[4:10], refs[10:]
        k, i = pl.program_id(0), pl.program_id(1)
        tile = pl.ds(pl.multiple_of(i * tm, tm), tm)

        @pl.when(i == 0)
        def _():
            for chunk, stages in enumerate([("send",), ("pass on", "x neighbour"), ("y neighbour",), ("diagonal",)]):
                @pl.when(k == chunk)
                def _():
                    for stage in stages:
                        relay(stage, stack_refs, relay_sems)
                    if chunk == 1 and later.start is not None:
                        later.start(later_in, later_out, later_sems)
            loads = [pltpu.make_async_copy(stack_refs[w].at[order_ref[k]], w_ref.at[w], w_sem.at[w]) for w in range(n)]
            for cp in loads:
                cp.start()
            for cp in loads:
                cp.wait()

        @pl.when(k == 0)
        def _():
            xf = x_ref[...]
            hb = ((xf * _rms(xf)) * g_ref[...]).astype(BF16)
            h_ref[...] = hb
            hs_ref[tile, :] = hb
            acc_ref[tile, :] = jnp.zeros((tm, d), F32)

        for rows in _row_halves(tm):
            part = pl.ds(pl.multiple_of(i * tm + rows.start, tm // 2), tm // 2)
            h = hs_ref[part, :]
            a = _dot_nt(h, w_ref[0])
            b = _dot_nt(h, w_ref[1])
            sb = ((a * (0.5 * jnp.tanh(0.5 * a) + 0.5)) * b).astype(BF16)
            a_ref[rows, :] = a.astype(BF16)
            b_ref[rows, :] = b.astype(BF16)
            s_ref[rows, :] = sb
            acc_ref[part, :] += _dot(sb, w_ref[2])

        @pl.when(k == nch - 1)
        def _():
            o_ref[...] = x_ref[...] + 0.5 * acc_ref[tile, :]

        if later.middle is not None:
            @pl.when((k == nch - 1) & (i == nt // 2))
            def _():
                later.middle(later_in, later_out, later_sems)

        @pl.when((k == nch - 1) & (i == nt - 1))
        def _():
            relay("end", stack_refs, relay_sems)
            if later.finish is not None:
                later.finish(later_in, later_out, later_sems)

    ends = lambda k, i: jnp.where((k == 0) | (k == nch - 1), i, 0)
    act = pl.BlockSpec((None, tm, fc), lambda k, i, order: (order[k], i, 0))
    out_shape = [jax.ShapeDtypeStruct((t, d), F32), jax.ShapeDtypeStruct((t, d), BF16)]
    out_shape += [jax.ShapeDtypeStruct((nch, t, fc), BF16)] * 3
    out_shape += [jax.ShapeDtypeStruct(s.shape, s.dtype) for s in stacks] + list(later.outs)
    aliases = {3 + w: 5 + w for w in range(n)}
    aliases.update({3 + n + i: 5 + n + o for i, o in later.alias.items()})
    res = pl.pallas_call(
        body, out_shape=out_shape,
        grid_spec=pltpu.PrefetchScalarGridSpec(
            num_scalar_prefetch=1, grid=(nch, nt),
            in_specs=[pl.BlockSpec((tm, d), lambda k, i, order: (ends(k, i), 0)),
                      pl.BlockSpec((1, d), lambda k, i, order: (0, 0))] + [ANY] * (n + p_in),
            out_specs=[pl.BlockSpec((tm, d), lambda k, i, order: (jnp.where(k == nch - 1, i, 0), 0)),
                       pl.BlockSpec((tm, d), lambda k, i, order: (jnp.where(k == 0, i, nt - 1), 0)),
                       act, act, act] + [ANY] * (n + p_out),
            scratch_shapes=[pltpu.VMEM((n, fc, d), BF16), pltpu.VMEM((t, d), BF16), pltpu.VMEM((t, d), F32),
                            pltpu.SemaphoreType.DMA((n,))] + RELAY_SEMS + list(later.sems)),
        input_output_aliases=aliases, compiler_params=_params(), name="ffn_fwd",
    )(order, x, gain, *stacks, *later.ins)
    return list(res[:5]), list(res[5:5 + n]), list(res[5 + n:])


def _plan_sibling_halves(gs):
    n = len(gs)

    def copies(ins, outs, sems):
        x, y, c = _mesh_pos()
        return [_remote(ins[w].at[:, _half_rows(gs[w].shape[1], 1 - c), :], outs[w], sems[0].at[w], sems[1].at[w],
                        (x, y, 1 - c)) for w in range(n)]

    def start(ins, outs, sems):
        for cp in copies(ins, outs, sems):
            cp.start()

    def finish(ins, outs, sems):
        for cp in copies(ins, outs, sems):
            cp.wait()

    return _Plan(gs, [jax.ShapeDtypeStruct((g.shape[0], g.shape[1] // 2, g.shape[2]), g.dtype) for g in gs], {},
                 [pltpu.SemaphoreType.DMA((n,))] * 2, start, finish)


def _plan_chip_exchange(ps):
    n = len(ps)

    def copies(ins, outs, sems):
        x, y, c = _mesh_pos()
        return [_remote(ins[w].at[2 * cx + cy], outs[w].at[j], sems[0].at[w, j], sems[1].at[w, j], (cx, cy, c))
                for w in range(n) for j, (cx, cy) in enumerate(_other_chips(x, y))]

    def start(ins, outs, sems):
        for cp in copies(ins, outs, sems):
            cp.start()

    def finish(ins, outs, sems):
        for cp in copies(ins, outs, sems):
            cp.wait()

    return _Plan(ps, [jax.ShapeDtypeStruct((3,) + p.shape[1:], p.dtype) for p in ps], {},
                 [pltpu.SemaphoreType.DMA((n, 3))] * 2, start, finish)


def _plan_sibling_share(gs):
    n = len(gs)

    def copies(outs, sems, which):
        x, y, c = _mesh_pos()
        cps = []
        for w in range(n):
            rows = outs[w].at[_half_rows(gs[w].shape[0], c if which == "mine" else 1 - c)]
            cps.append(_remote(rows, rows, sems[0].at[w], sems[1].at[w], (x, y, 1 - c)))
        return cps

    def start(ins, outs, sems):
        for cp in copies(outs, sems, "mine"):
            cp.start()

    def finish(ins, outs, sems):
        for cp in copies(outs, sems, "mine"):
            cp.wait_send()
        for cp in copies(outs, sems, "theirs"):
            cp.wait_recv()

    return _Plan(gs, [jax.ShapeDtypeStruct(g.shape, g.dtype) for g in gs], {w: w for w in range(n)},
                 [pltpu.SemaphoreType.DMA((n,))] * 2, start, finish)


def _same_shape_groups(arrays):
    groups = {}
    for i, a in enumerate(arrays):
        groups.setdefault(a.shape, []).append(i)
    return list(groups.values())


def _add_sibling(gs, r1s, ids, tag):
    n = len(gs)
    nch, rh, cols = r1s[0].shape

    def body(ids_ref, *refs):
        for g_ref, r_ref, o_ref in zip(refs[:n], refs[n:2 * n], refs[2 * n:]):
            o_ref[...] = (g_ref[...] + r_ref[...]).astype(BF16)

    blk = lambda fn: pl.BlockSpec((None, rh, cols), fn)
    return pl.pallas_call(
        body, out_shape=[jax.ShapeDtypeStruct(r1s[0].shape, BF16)] * n,
        grid_spec=pltpu.PrefetchScalarGridSpec(
            num_scalar_prefetch=1, grid=(nch,),
            in_specs=[blk(lambda k, ids: (k, ids[1], 0))] * n + [blk(lambda k, ids: (k, 0, 0))] * n,
            out_specs=[blk(lambda k, ids: (k, 0, 0))] * n),
        compiler_params=_params(), name="add_sibling_" + tag,
    )(ids, *gs, *r1s)


def _add_chips(gs, r1s, r2s, ids, tag):
    n = len(gs)
    _, rh, cols = r1s[0].shape
    nb = 2 if rh % 32 == 0 else 1
    rb = rh // nb

    def body(ids_ref, *refs):
        for g_ref, r1_ref, r2_ref, o_ref in zip(refs[:n], refs[n:2 * n], refs[2 * n:3 * n], refs[3 * n:]):
            own = g_ref[...] + r1_ref[...]
            o_ref[...] = ((own + r2_ref[0].astype(F32)) + r2_ref[1].astype(F32)) + r2_ref[2].astype(F32)

    return pl.pallas_call(
        body, out_shape=[jax.ShapeDtypeStruct((2 * rh, cols), F32)] * n,
        grid_spec=pltpu.PrefetchScalarGridSpec(
            num_scalar_prefetch=1, grid=(nb,),
            in_specs=[pl.BlockSpec((None, rb, cols), lambda i, ids: (ids[0], ids[1] * nb + i, 0))] * n
            + [pl.BlockSpec((None, rb, cols), lambda i, ids: (ids[0], i, 0))] * n
            + [pl.BlockSpec((3, rb, cols), lambda i, ids: (0, i, 0))] * n,
            out_specs=[pl.BlockSpec((rb, cols), lambda i, ids: (ids[1] * nb + i, 0))] * n),
        compiler_params=_params(), name="add_chips_" + tag,
    )(ids, *gs, *r1s, *r2s)


VEC_ROWS = 8


N_DEVICES = 8


def _small_pack(part, d, width):
    names = ("ffn1_norm", "mix_norm", "ffn2_norm", "pool_scale", "out_norm_pool", "out_norm_attn", "qn", "kn", "b_forget",
             "loss")
    args = [part[k] for k in names]

    def body(g1_ref, gm_ref, g2_ref, ps_ref, onp_ref, ona_ref, qn_ref, kn_ref, bf_ref, loss_ref, vbuf):
        lo = _head_masks()

        def fold_heads(ref):
            v = jnp.sum(ref[...], axis=0)
            acc = jnp.zeros((VEC_ROWS, LANES), F32)
            for blk in range(width // LANES):
                vb = jnp.broadcast_to(v[:, blk * LANES:(blk + 1) * LANES], (VEC_ROWS, LANES))
                acc = acc + vb + pltpu.roll(vb, HEAD_DIM, 1)
            return jnp.where(lo, acc, 0.0)[0:1, :]

        vbuf[0] = jnp.zeros((VEC_ROWS, d), F32)
        vbuf[0, 0:1, :] = jnp.sum(g1_ref[...], axis=0)
        vbuf[0, 1:2, :] = jnp.sum(gm_ref[...], axis=0)
        vbuf[0, 2:3, :] = jnp.sum(g2_ref[...], axis=0)
        vbuf[0, 5:6, 0:LANES] = jnp.sum(loss_ref[...], axis=0)[0:1, :]
        vbuf[0, 3:4, 0:width] = jnp.sum(ps_ref[...], axis=0)
        vbuf[0, 3:4, width:2 * width] = jnp.sum(onp_ref[...], axis=0)
        vbuf[0, 4:5, 0:width] = jnp.sum(ona_ref[...], axis=0)
        vbuf[0, 4:5, width:width + LANES] = fold_heads(qn_ref)
        vbuf[0, 4:5, width + LANES:width + 2 * LANES] = fold_heads(kn_ref)
        vbuf[0, 4:5, width + 2 * LANES:width + 3 * LANES] = jnp.sum(bf_ref[...], axis=0)

    return pl.pallas_call(
        body, out_shape=jax.ShapeDtypeStruct((N_DEVICES, VEC_ROWS, d), F32),
        in_specs=[VM] * len(args), out_specs=VM, compiler_params=_params(), name="small_pack",
    )(*args)


def _pool_pack(dpw):
    def body(pw_ref, pbuf):
        pbuf[0] = jnp.sum(pw_ref[...], axis=0)

    return pl.pallas_call(
        body, out_shape=jax.ShapeDtypeStruct((N_DEVICES,) + dpw.shape[1:], F32),
        in_specs=[VM], out_specs=VM, compiler_params=_params(), name="pool_pack",
    )(dpw)


def _plan_all_to_all(stacks):
    n = len(stacks)

    def copies(outs, sems):
        x, y, c = _mesh_pos()
        cps = []
        for r in range(1, N_DEVICES):
            peer = (x if not r & 4 else 1 - x, y if not r & 2 else 1 - y, c if not r & 1 else 1 - c)
            cps += [_remote(outs[w].at[0], outs[w].at[r], sems[0].at[w, r - 1], sems[1].at[w, r - 1], peer) for w in range(n)]
        return cps

    def start(ins, outs, sems):
        for cp in copies(outs, sems):
            cp.start()

    def finish(ins, outs, sems):
        for cp in copies(outs, sems):
            cp.wait()

    return _Plan(stacks, [jax.ShapeDtypeStruct(s.shape, s.dtype) for s in stacks], {w: w for w in range(n)},
                 [pltpu.SemaphoreType.DMA((n, N_DEVICES - 1))] * 2, start, finish)


def _small_sum(vstack, pstack, me):
    def body(me_ref, vbuf, pbuf, vec_ref, pw_ref):
        vec = vbuf[me_ref[0]]
        pw = pbuf[me_ref[0]]
        for dev in range(1, N_DEVICES):
            vec = vec + vbuf[jnp.bitwise_xor(me_ref[0], dev)]
            pw = pw + pbuf[jnp.bitwise_xor(me_ref[0], dev)]
        vec_ref[...] = vec
        pw_ref[...] = pw

    full = lambda s: pl.BlockSpec(s.shape, lambda i, me: (0,) * len(s.shape))
    outs = [jax.ShapeDtypeStruct(vstack.shape[1:], F32), jax.ShapeDtypeStruct(pstack.shape[1:], F32)]
    return pl.pallas_call(
        body, out_shape=outs,
        grid_spec=pltpu.PrefetchScalarGridSpec(num_scalar_prefetch=1, grid=(1,), in_specs=[full(vstack), full(pstack)],
                                               out_specs=[full(o) for o in outs]),
        compiler_params=_params(), name="small_sum",
    )(me, vstack, pstack)


def _adamw(ws, gs, ms, vs, tag):
    n = len(ws)
    rows, cols = ws[0].shape
    rb = rows
    while rb * cols * 4 * n > (3 << 20) and rb % 16 == 0:
        rb //= 2

    def body(*refs):
        for j in range(n):
            w_ref, g_ref, m_ref, v_ref = (refs[k * n + j] for k in range(4))
            go_ref, d_ref, mo_ref, vo_ref = (refs[(4 + k) * n + j] for k in range(4))
            gv = g_ref[...]
            go_ref[...] = gv
            m2 = ADAM_B1 * m_ref[...] + (1.0 - ADAM_B1) * gv
            v2 = ADAM_B2 * v_ref[...] + (1.0 - ADAM_B2) * (gv * gv)
            m_hat = m2 / (1.0 - ADAM_B1 ** ADAM_STEP)
            v_hat = v2 / (1.0 - ADAM_B2 ** ADAM_STEP)
            d_ref[...] = -ADAM_LR * (m_hat / (jnp.sqrt(v_hat) + ADAM_EPS) + ADAM_WD * w_ref[...])
            mo_ref[...] = m2
            vo_ref[...] = v2

    spec = pl.BlockSpec((rb, cols), lambda i: (i, 0))
    res, _ = _pallas(
        body, name="adamw_" + tag, args=[*ws, *gs, *ms, *vs], out_shape=[jax.ShapeDtypeStruct(ws[0].shape, F32)] * (4 * n),
        grid=(rows // rb,), in_specs=[spec] * (4 * n), out_specs=[spec] * (4 * n))
    return [tuple(res[k * n + j] for k in range(4)) for j in range(n)]


def _pack_vec(p, d, width):
    pad = lambda v: jnp.pad(v, (0, LANES - v.shape[0]))
    row3 = jnp.concatenate([p["pool_scale"], p["out_norm_pool"]])
    row4 = jnp.concatenate([p["out_norm_attn"], pad(p["q_norm"]), pad(p["k_norm"]), pad(p["b_forget"]),
                            jnp.zeros((d - width - 3 * LANES,), F32)])
    rows = [p["ffn1_norm"], p["mix_norm"], p["ffn2_norm"], row3, row4]
    return jnp.pad(jnp.stack(rows), ((0, VEC_ROWS - len(rows)), (0, 0)))


def _unpack_vec(vec, width):
    return dict(ffn1_norm=vec[0], mix_norm=vec[1], ffn2_norm=vec[2], pool_scale=vec[3, :width],
                out_norm_pool=vec[3, width:2 * width], out_norm_attn=vec[4, :width],
                q_norm=vec[4, width:width + HEAD_DIM], k_norm=vec[4, width + LANES:width + LANES + HEAD_DIM],
                b_forget=vec[4, width + 2 * LANES:width + 2 * LANES + N_HEADS])


WEIGHT_NAMES = ("ffn1_norm", "ffn1_w_gate", "ffn1_w_up", "ffn1_w_down", "mix_norm", "w_in", "b_forget", "pool_w",
                "pool_scale", "q_norm", "k_norm", "out_norm_pool", "out_norm_attn", "w_out", "ffn2_norm",
                "ffn2_w_gate", "ffn2_w_up", "ffn2_w_down")
BIG_NAMES = ("ffn1_w_gate", "ffn1_w_up", "ffn1_w_down", "w_in", "w_out", "ffn2_w_gate", "ffn2_w_up", "ffn2_w_down")
TRANSPOSED_NAMES = ("ffn1_w_gate", "ffn1_w_up", "w_in", "ffn2_w_gate", "ffn2_w_up")
FFN1_NAMES = ("ffn1_w_gate", "ffn1_w_up", "ffn1_w_down")
MIX_NAMES = ("w_in", "w_out")
FFN2_NAMES = ("ffn2_w_gate", "ffn2_w_up", "ffn2_w_down")


def kernel(x, ffn1_norm, ffn1_w_gate, ffn1_w_up, ffn1_w_down, mix_norm, w_in, b_forget, pool_w, pool_scale, q_norm, k_norm, out_norm_pool, out_norm_attn, w_out, ffn2_norm, ffn2_w_gate, ffn2_w_up, ffn2_w_down, loss_target, m_ffn1_norm, m_ffn1_w_gate, m_ffn1_w_up, m_ffn1_w_down, m_mix_norm, m_w_in, m_b_forget, m_pool_w, m_pool_scale, m_q_norm, m_k_norm, m_out_norm_pool, m_out_norm_attn, m_w_out, m_ffn2_norm, m_ffn2_w_gate, m_ffn2_w_up, m_ffn2_w_down, v_ffn1_norm, v_ffn1_w_gate, v_ffn1_w_up, v_ffn1_w_down, v_mix_norm, v_w_in, v_b_forget, v_pool_w, v_pool_scale, v_q_norm, v_k_norm, v_out_norm_pool, v_out_norm_attn, v_w_out, v_ffn2_norm, v_ffn2_w_gate, v_ffn2_w_up, v_ffn2_w_down):
    given = dict(locals())
    w = {n: given[n] for n in WEIGHT_NAMES}
    m = {n: given["m_" + n] for n in WEIGHT_NAMES}
    v = {n: given["v_" + n] for n in WEIGHT_NAMES}
    n_batch, seq, d = x.shape
    width = pool_scale.shape[0]
    in_rows = w_in.shape[1]
    in_cols = N_CHIPS * in_rows
    in_pad = -(-in_rows // 32) * 32
    in_cols_pad = in_cols - N_HEADS + LANES

    work = lambda a, n: a.T if n in TRANSPOSED_NAMES else a
    exchanged = lambda a, n: jnp.pad(a, ((0, in_pad - in_rows), (0, 0))) if n == "w_in" else a

    mesh_x, mesh_y, mesh_c = _mesh_pos()
    ids = jnp.stack([2 * mesh_x + mesh_y, mesh_c]).astype(jnp.int32)

    row = lambda a: a.reshape(1, -1)
    g1, gm, g2, ps, onp, ona = (row(a) for a in (ffn1_norm, mix_norm, ffn2_norm, pool_scale, out_norm_pool, out_norm_attn))
    qn, kn = row(jnp.tile(q_norm, N_HEADS)), row(jnp.tile(k_norm, N_HEADS))
    bf = row(jnp.pad(b_forget, (0, LANES - N_HEADS)))
    pwb = pool_w.astype(BF16)
    xf, tgt = x.reshape(n_batch * seq, d), loss_target.reshape(n_batch * seq, d)

    def grouped(call, names, *lists):
        out = [None] * len(names)
        for idx in _same_shape_groups(lists[0]):
            res = call(*[[lst[i] for i in idx] for lst in lists], names[idx[0]])
            for i, r in zip(idx, res):
                out[i] = r
        return out

    placed = dict(zip(BIG_NAMES, grouped(lambda ws, tag: _place_cast(ws, ids, tag), BIG_NAMES,
                                         [exchanged(work(w[n], n), n) for n in BIG_NAMES])))
    landing = jnp.stack([2 * cx + cy for cx, cy in [(mesh_x, mesh_y)] + _other_chips(mesh_x, mesh_y)]).astype(jnp.int32)
    (x1, h1, a1, b1, s1), (wg1, wu1, wd1), (w_in_all, w_out_all) = _ffn_fwd_gathering(
        xf, g1, [placed[n] for n in FFN1_NAMES], landing, _plan_gather([placed[n] for n in MIX_NAMES]))
    w_in_t = jnp.pad(w_in_all[:, :in_rows].reshape(in_cols, d), ((0, in_cols_pad - in_cols), (0, 0)))
    w_out_full = w_out_all.reshape(N_CHIPS * w_out.shape[0], d)
    woa, wob = w_out_full[:width], w_out_full[width:]

    hm, pv, q, k, qh, kh, vb, f = _mix_proj(x1, gm, w_in_t, qn, kn, width, width)
    qa, ka = _forget_prefix(f, bf, qh, kh, n_batch, seq)
    yp = _pool_fwd(pv, pwb, ps, onp, n_batch, seq)
    (o, lse), (wg2, wu2, wd2) = _attn_fwd(qa, ka, vb, n_batch, seq, plan=_plan_gather_relay([placed[n] for n in FFN2_NAMES]))
    x2, ya = _mix_out(x1, yp, o, ona, woa, wob)
    (dy, h2, a2, b2, s2, lpart, dyh), _ = _ffn_fwd(x2, g2, wg2, wu2, wd2, target=tgt)

    def to_chips(gs, arrived, tags):
        return grouped(lambda g, r, tag: _add_sibling(g, r, ids, tag), tags, gs, arrived)

    def own_rows(gs, from_sibling, from_chips, tags):
        return grouped(lambda g, ra, rb, tag: _add_chips(g, ra, rb, ids, tag), tags, gs, from_sibling, from_chips)

    (dx2, da2, db2, dg2), _ = _ffn_bwd_x(dy, x2, g2, a2, b2, wg2, wu2, wd2, "ffn2_bwd_x")
    dw2, _ = _ffn_bwd_w([(da2, h2), (db2, h2), (s2, dyh)], "ffn2_bwd_w")
    (dyp, do, delta, dwoa, dwob, dona), sib2 = _mix_out_bwd(dx2, o, yp, ya, ona, woa, wob, plan=_plan_sibling_halves(dw2))
    dpv, dpw, dps, donp = _pool_bwd(pv, dyp, pwb, ps, onp, n_batch, seq)
    (dqh, dkh, dv, dfq, dfk), chips2 = _attn_bwd(qa, ka, vb, do, lse, delta, n_batch, seq,
                                                 plan=_plan_chip_exchange(to_chips(dw2, sib2, FFN2_NAMES)))
    df, dbf = _forget_bwd(dfq, dfk, f, bf, n_batch, seq)
    dx1, dx1h, dw_in_t, dgm, dqn, dkn = _mix_in_bwd(dx2, x1, gm, hm, dpv, dqh, q, dkh, k, dv, df, qn, kn, w_in_t)
    in_base = [in_rows * k // 8 * 8 for k in range(N_CHIPS)]
    d_w_in = jnp.stack([dw_in_t[b:b + in_pad] for b in in_base])
    d_w_out = jnp.concatenate([dwoa, dwob], axis=0).reshape(N_CHIPS, w_out.shape[0], d)
    dwm = [d_w_in, d_w_out]
    down = FFN1_NAMES[2:]
    dwd1, arrived = _ffn_bwd_w([(s1, dx1h)], "ffn1_bwd_w_down",
                               plan=_merge_plans(_plan_sibling_halves(dwm), _plan_all_to_all(
                                   [_pool_pack(dpw.reshape(n_batch, -1, pool_w.shape[-1]))])))
    sibm, pstack = arrived[:len(dwm)], arrived[len(dwm)]
    (da1, db1), arrived = _ffn_bwd_a(dx1h, a1, b1, wd1, "ffn1_bwd_a",
                                     plan=_merge_plans(_plan_sibling_halves(dwd1),
                                                       _plan_chip_exchange(to_chips(dwm, sibm, MIX_NAMES))))
    sibd, chipsm = arrived[:1], arrived[1:]
    gate_up = FFN1_NAMES[:2]
    dwgu1, chipsd = _ffn_bwd_w([(da1, h1), (db1, h1)], "ffn1_bwd_w_gate_up",
                               plan=_plan_chip_exchange(to_chips(dwd1, sibd, down)))
    n_tiles = (n_batch * seq) // min(FFN_TILE, n_batch * seq)
    first = max(n_tiles // 4, 1)
    begun, sibgu = _ffn_bwd_h(dx1, xf, g1, da1, db1, wg1, wu1, "ffn1_bwd_h_first", (0, first),
                              plan=_plan_sibling_halves(dwgu1))
    earlier = (own_rows(dwd1, sibd, chipsd, down) + own_rows(dwm, sibm, chipsm, MIX_NAMES)
               + own_rows(dw2, sib2, chips2, FFN2_NAMES))
    (gx, dg1), arrived = _ffn_bwd_h(dx1, xf, g1, da1, db1, wg1, wu1, "ffn1_bwd_h_rest", (first, n_tiles), prev=begun,
                                    plan=_merge_plans(_plan_chip_exchange(to_chips(dwgu1, sibgu, gate_up)),
                                                      _plan_sibling_share(earlier)))
    chipsgu, shared = arrived[:len(gate_up)], arrived[len(gate_up):]

    part = dict(ffn1_norm=dg1, mix_norm=dgm, ffn2_norm=dg2, b_forget=dbf, pool_scale=dps, out_norm_pool=donp,
                out_norm_attn=dona, qn=dqn, kn=dkn, loss=lpart)
    mine = own_rows(dwgu1, sibgu, chipsgu, gate_up)
    last = _run_plan(_merge_plans(_plan_sibling_share(mine), _plan_all_to_all([_small_pack(part, d, width)])), "last_exchange")
    vstack = last[len(mine)]
    g_vec, g_pw = _small_sum(vstack, pstack, jnp.reshape(4 * mesh_x + 2 * mesh_y + mesh_c, (1,)).astype(jnp.int32))
    loss = g_vec[5, 0]
    reduced = dict(zip(gate_up + down + MIX_NAMES + FFN2_NAMES, list(last[:len(mine)]) + list(shared)))
    reduced["w_in"] = lax.dynamic_slice(reduced["w_in"], ((in_rows * ids[0]) % 8, 0), (in_rows, d))

    grads, delta, new_m, new_v = {}, {}, {}, {}
    for names in (FFN2_NAMES, FFN1_NAMES, ("w_in",), ("w_out",)):
        stepped = _adamw([work(w[n], n) for n in names], [reduced[n] for n in names], [work(m[n], n) for n in names],
                         [work(v[n], n) for n in names], names[0])
        for n, step in zip(names, stepped):
            grads[n], delta[n], new_m[n], new_v[n] = (work(a, n) for a in step)
    flat_pw = lambda a: a.reshape(-1, a.shape[-1])
    (_, d_pw, m_pw, v_pw), = _adamw([flat_pw(pool_w)], [g_pw], [flat_pw(m_pool_w)], [flat_pw(v_pool_w)], "pool_w")
    (_, d_vec, m_vec, v_vec), = _adamw([_pack_vec(w, d, width)], [g_vec], [_pack_vec(m, d, width)],
                                       [_pack_vec(v, d, width)], "vectors")
    grads.update(_unpack_vec(g_vec, width), pool_w=g_pw.reshape(pool_w.shape))
    delta.update(_unpack_vec(d_vec, width), pool_w=d_pw.reshape(pool_w.shape))
    new_m.update(_unpack_vec(m_vec, width), pool_w=m_pw.reshape(pool_w.shape))
    new_v.update(_unpack_vec(v_vec, width), pool_w=v_pw.reshape(pool_w.shape))
    return (loss, gx.reshape(x.shape), *[grads[n] for n in WEIGHT_NAMES], *[delta[n] for n in WEIGHT_NAMES],
            *[new_m[n] for n in WEIGHT_NAMES], *[new_v[n] for n in WEIGHT_NAMES])
```

```python
import functools

import jax
import jax.numpy as jnp
from jax import lax
from jax.experimental import pallas as pl
from jax.experimental.pallas import tpu as pltpu

F32 = jnp.float32
BF16 = jnp.bfloat16
EPS = 1e-6
NEG = -1e30
ADAM_LR = 0.001
ADAM_B1 = 0.9
ADAM_B2 = 0.999
ADAM_EPS = 1e-08
ADAM_WD = 0.01
ADAM_STEP = 10
POOL_WINDOWS = (2, 4, 8, 16)
HEAD_DIM = 64
N_HEADS = 8
LANES = 128
N_CHIPS = 4
ATT_BLOCK = 512
ATT_SUB = 128
FFN_TILE = 1024
FFN_STAGED_TILE = 512
VMEM_LIMIT = 62 * 1024 * 1024
ANY = pl.BlockSpec(memory_space=pl.ANY)
VM = pl.BlockSpec(memory_space=pltpu.VMEM)


def _params(**kw):
    return pltpu.CompilerParams(vmem_limit_bytes=VMEM_LIMIT, **kw)


def _dot(a, b):
    return jnp.dot(a, b, preferred_element_type=F32)


def _dot_nt(a, b):
    return lax.dot_general(a, b, (((1,), (1,)), ((), ())), preferred_element_type=F32)


def _dot_tn(a, b):
    return lax.dot_general(a, b, (((0,), (0,)), ((), ())), preferred_element_type=F32)


def _sigmoid(z):
    return 1.0 / (1.0 + jnp.exp(-z))


def _rms(xf):
    return lax.rsqrt(jnp.mean(xf * xf, axis=-1, keepdims=True) + EPS)


def _rms_bwd(xf, r, gain, dh):
    xh = xf * r
    dyg = dh * gain
    return r * (dyg - xh * jnp.mean(dyg * xh, axis=-1, keepdims=True)), dh * xh


def _total(v):
    return jnp.sum(jnp.sum(v, axis=1, keepdims=True), axis=0, keepdims=True)


def _ffn_fwd(x, gain, wg, wu, wd, target=None, plan=None):
    t, d = x.shape
    nch, fc, _ = wg.shape
    tm = min(FFN_TILE, t)
    nt = t // tm
    with_loss = target is not None

    def body(*refs):
        if with_loss:
            x_ref, g_ref, wg_ref, wu_ref, wd_ref, t_ref, o_ref, h_ref, a_ref, b_ref, s_ref, l_ref, oh_ref, acc_ref = refs
        else:
            x_ref, g_ref, wg_ref, wu_ref, wd_ref, o_ref, h_ref, a_ref, b_ref, s_ref, acc_ref = refs
        k = pl.program_id(1)

        @pl.when(k == 0)
        def _():
            xf = x_ref[...]
            h_ref[...] = ((xf * _rms(xf)) * g_ref[...]).astype(BF16)
            acc_ref[...] = jnp.zeros_like(acc_ref)

        for rows in _row_halves(tm):
            h = h_ref[rows, :]
            a = _dot_nt(h, wg_ref[...])
            b = _dot_nt(h, wu_ref[...])
            sb = ((a * (0.5 * jnp.tanh(0.5 * a) + 0.5)) * b).astype(BF16)
            a_ref[rows, :] = a.astype(BF16)
            b_ref[rows, :] = b.astype(BF16)
            s_ref[rows, :] = sb
            acc_ref[rows, :] += _dot(sb, wd_ref[...])

        @pl.when(k == nch - 1)
        def _():
            y = x_ref[...] + 0.5 * acc_ref[...]
            if with_loss:
                e = y - t_ref[...]
                o_ref[...] = e * (1.0 / d)
                oh_ref[...] = (e * (0.5 / d)).astype(BF16)
                l_ref[...] = jnp.broadcast_to(_total(e * e) * (0.5 / d), l_ref.shape)
            else:
                o_ref[...] = y

    row = pl.BlockSpec((tm, d), lambda i, k: (i, 0))
    chunk = pl.BlockSpec((None, fc, d), lambda i, k: (k, 0, 0))
    act = pl.BlockSpec((None, tm, fc), lambda i, k: (k, i, 0))
    in_specs = [row, pl.BlockSpec((1, d), lambda i, k: (0, 0)), chunk, chunk, chunk]
    out_shape = [jax.ShapeDtypeStruct((t, d), F32), jax.ShapeDtypeStruct((t, d), BF16)]
    out_shape += [jax.ShapeDtypeStruct((nch, t, fc), BF16)] * 3
    out_specs = [row, row, act, act, act]
    args = [x, gain, wg, wu, wd]
    if with_loss:
        in_specs.append(row)
        args.append(target)
        out_shape += [jax.ShapeDtypeStruct((nt, 8, LANES), F32), jax.ShapeDtypeStruct((t, d), BF16)]
        out_specs += [pl.BlockSpec((None, 8, LANES), lambda i, k: (i, 0, 0)), row]
    return _pallas(body, name="ffn_fwd_loss" if with_loss else "ffn_fwd", args=args, in_specs=in_specs,
                   out_shape=out_shape, out_specs=out_specs, grid=(nt, nch),
                   scratch_shapes=[pltpu.VMEM((tm, d), F32)], plan=plan)


def _row_halves(n):
    return [slice(0, n // 2), slice(n // 2, n)]


def _swiglu_grads(dyh, a_ref, b_ref, wd_ref, rows):
    ds = _dot_nt(dyh, wd_ref[...])
    av = a_ref[rows, :].astype(F32)
    bv = b_ref[rows, :].astype(F32)
    th = jnp.tanh(0.5 * av)
    sig = 0.5 * th + 0.5
    dab = ((ds * bv) * (sig * (1.0 + av * (0.5 - 0.5 * th)))).astype(BF16)
    return dab, (ds * (av * sig)).astype(BF16)


def _ffn_bwd_a(dyh, a, b, wd, name, plan=None):
    t, d = dyh.shape
    nch, fc, _ = wd.shape
    tm = min(FFN_TILE, t)

    def body(dyh_ref, a_ref, b_ref, wd_ref, da_ref, db_ref):
        for rows in _row_halves(tm):
            da_ref[rows, :], db_ref[rows, :] = _swiglu_grads(dyh_ref[rows, :], a_ref, b_ref, wd_ref, rows)

    act = pl.BlockSpec((None, tm, fc), lambda i, k: (k, i, 0))
    return _pallas(
        body, name=name, args=[dyh, a, b, wd], out_shape=[jax.ShapeDtypeStruct((nch, t, fc), BF16)] * 2, grid=(t // tm, nch),
        in_specs=[pl.BlockSpec((tm, d), lambda i, k: (i, 0)), act, act, pl.BlockSpec((None, fc, d), lambda i, k: (k, 0, 0))],
        out_specs=[act, act], plan=plan)


def _ffn_bwd_h(dy, x, gain, da, db, wg, wu, name, tiles, prev=None, plan=None):
    t, d = x.shape
    nch, fc, _ = wg.shape
    tm = min(FFN_TILE, t)
    nt = t // tm
    t0, t1 = tiles

    def body(*refs):
        dy_ref, x_ref, g_ref, da_ref, db_ref, wg_ref, wu_ref = refs[:7]
        dx_ref, dg_ref, acc_ref = refs[-3:]
        k = pl.program_id(1)

        @pl.when(k == 0)
        def _():
            acc_ref[...] = jnp.zeros_like(acc_ref)

        acc_ref[...] += _dot(da_ref[...], wg_ref[...]) + _dot(db_ref[...], wu_ref[...])

        @pl.when(k == nch - 1)
        def _():
            xf = x_ref[...]
            dxn, dgr = _rms_bwd(xf, _rms(xf), g_ref[...], acc_ref[...])
            dx_ref[...] = dy_ref[...] + dxn
            dg_ref[...] = jnp.sum(dgr, axis=0, keepdims=True)

    row = pl.BlockSpec((tm, d), lambda i, k: (i + t0, 0))
    chunk = pl.BlockSpec((None, fc, d), lambda i, k: (k, 0, 0))
    act = pl.BlockSpec((None, tm, fc), lambda i, k: (k, i + t0, 0))
    args = [dy, x, gain, da, db, wg, wu]
    in_specs = [row, row, pl.BlockSpec((1, d), lambda i, k: (0, 0)), act, act, chunk, chunk]
    aliases = {}
    if prev is not None:
        aliases = {len(args): 0, len(args) + 1: 1}
        args += list(prev)
        in_specs += [ANY, ANY]
    return _pallas(
        body, name=name, args=args, out_shape=[jax.ShapeDtypeStruct((t, d), F32), jax.ShapeDtypeStruct((nt, 1, d), F32)],
        grid=(t1 - t0, nch), in_specs=in_specs,
        out_specs=[row, pl.BlockSpec((None, 1, d), lambda i, k: (i + t0, 0, 0))],
        scratch_shapes=[pltpu.VMEM((tm, d), F32)], plan=plan, aliases=aliases)


def _ffn_bwd_x(dy, x, gain, a, b, wg, wu, wd, name, plan=None):
    t, d = x.shape
    nch, fc, _ = wg.shape
    tm = min(FFN_TILE, t)
    nt = t // tm

    def body(dy_ref, x_ref, g_ref, a_ref, b_ref, wg_ref, wu_ref, wd_ref, dx_ref, da_ref, db_ref, dg_ref, acc_ref):
        k = pl.program_id(1)

        @pl.when(k == 0)
        def _():
            acc_ref[...] = jnp.zeros_like(acc_ref)

        for rows in _row_halves(tm):
            dab, dbb = _swiglu_grads((0.5 * dy_ref[rows, :]).astype(BF16), a_ref, b_ref, wd_ref, rows)
            da_ref[rows, :] = dab
            db_ref[rows, :] = dbb
            acc_ref[rows, :] += _dot(dab, wg_ref[...]) + _dot(dbb, wu_ref[...])

        @pl.when(k == nch - 1)
        def _():
            xf = x_ref[...]
            dxn, dgr = _rms_bwd(xf, _rms(xf), g_ref[...], acc_ref[...])
            dx_ref[...] = dy_ref[...] + dxn
            dg_ref[...] = jnp.sum(dgr, axis=0, keepdims=True)

    row = pl.BlockSpec((tm, d), lambda i, k: (i, 0))
    chunk = pl.BlockSpec((None, fc, d), lambda i, k: (k, 0, 0))
    act = pl.BlockSpec((None, tm, fc), lambda i, k: (k, i, 0))
    return _pallas(
        body, name=name, args=[dy, x, gain, a, b, wg, wu, wd],
        out_shape=[jax.ShapeDtypeStruct((t, d), F32), jax.ShapeDtypeStruct((nch, t, fc), BF16),
                   jax.ShapeDtypeStruct((nch, t, fc), BF16), jax.ShapeDtypeStruct((nt, 1, d), F32)],
        grid=(nt, nch),
        in_specs=[row, row, pl.BlockSpec((1, d), lambda i, k: (0, 0)), act, act, chunk, chunk, chunk],
        out_specs=[row, act, act, pl.BlockSpec((None, 1, d), lambda i, k: (i, 0, 0))],
        scratch_shapes=[pltpu.VMEM((tm, d), F32)], plan=plan)


def _ffn_bwd_w(pairs, name, plan=None):
    n = len(pairs)
    nch, t, fc = pairs[0][0].shape
    d = pairs[0][1].shape[1]
    tm = min(FFN_TILE, t)

    def body(*refs):
        @pl.when(pl.program_id(1) == 0)
        def _():
            for o_ref in refs[2 * n:]:
                o_ref[...] = jnp.zeros_like(o_ref)

        for j in range(n):
            refs[2 * n + j][...] += _dot_tn(refs[j][...], refs[n + j][...])

    row = pl.BlockSpec((tm, d), lambda k, i: (i, 0))
    act = pl.BlockSpec((None, tm, fc), lambda k, i: (k, i, 0))
    chunk = pl.BlockSpec((None, fc, d), lambda k, i: (k, 0, 0))
    return _pallas(body, name=name, args=[p[0] for p in pairs] + [p[1] for p in pairs],
                   out_shape=[jax.ShapeDtypeStruct((nch, fc, d), F32)] * n, grid=(nch, t // tm),
                   in_specs=[act] * n + [row] * n, out_specs=[chunk] * n, plan=plan)


def _head_masks():
    lane = lax.broadcasted_iota(jnp.int32, (1, LANES), 1)
    return lane < HEAD_DIM


def _head_rms(x, lo):
    x2 = x * x
    s0 = jnp.sum(jnp.where(lo, x2, 0.0), axis=1, keepdims=True)
    s1 = jnp.sum(jnp.where(lo, 0.0, x2), axis=1, keepdims=True)
    return jnp.where(lo, lax.rsqrt(s0 * (1.0 / HEAD_DIM) + EPS), lax.rsqrt(s1 * (1.0 / HEAD_DIM) + EPS))


def _head_mean(v, lo):
    s0 = jnp.sum(jnp.where(lo, v, 0.0), axis=1, keepdims=True)
    s1 = jnp.sum(jnp.where(lo, 0.0, v), axis=1, keepdims=True)
    return jnp.where(lo, s0, s1) * (1.0 / HEAD_DIM)


def _mix_proj(x1, gain, wt, qn, kn, pool_width, attn_width):
    t, d = x1.shape
    tm = min(512, t)
    nt = t // tm
    scale = HEAD_DIM ** -0.5
    c_q, c_k, c_v = pool_width, pool_width + attn_width, pool_width + 2 * attn_width
    c_f = c_v + attn_width

    def body(x_ref, g_ref, wt_ref, qn_ref, kn_ref, hm_ref, pv_ref, q_ref, k_ref, qh_ref, kh_ref, vb_ref, f_ref):
        lo = _head_masks()
        for rows in _row_halves(tm):
            xf = x_ref[rows, :]
            hm = ((xf * _rms(xf)) * g_ref[...]).astype(BF16)
            hm_ref[rows, :] = hm
            f_ref[rows, :] = _dot_nt(hm, wt_ref[c_f:c_f + LANES, :])
            pv_ref[rows, :] = _dot_nt(hm, wt_ref[0:pool_width, :])
            vb_ref[rows, :] = _dot_nt(hm, wt_ref[c_v:c_v + attn_width, :]).astype(BF16)
            for c0, raw_ref, hat_ref, n_ref, mul in ((c_q, q_ref, qh_ref, qn_ref, scale), (c_k, k_ref, kh_ref, kn_ref, 1.0)):
                raw = _dot_nt(hm, wt_ref[c0:c0 + attn_width, :])
                raw_ref[rows, :] = raw
                for blk in range(attn_width // LANES):
                    sl = slice(blk * LANES, (blk + 1) * LANES)
                    xb = raw[:, sl]
                    hat_ref[rows, sl] = (((xb * _head_rms(xb, lo)) * n_ref[:, sl]) * mul).astype(BF16)

    row = pl.BlockSpec((tm, d), lambda i: (i, 0))
    half = pl.BlockSpec((tm, attn_width), lambda i: (i, 0))
    const = lambda shape: pl.BlockSpec(shape, lambda i: (0, 0))
    return _pallas(
        body, name="mix_proj", args=[x1, gain, wt, qn, kn],
        out_shape=[jax.ShapeDtypeStruct((t, d), BF16), jax.ShapeDtypeStruct((t, pool_width), F32),
                   jax.ShapeDtypeStruct((t, attn_width), F32), jax.ShapeDtypeStruct((t, attn_width), F32),
                   jax.ShapeDtypeStruct((t, attn_width), BF16), jax.ShapeDtypeStruct((t, attn_width), BF16),
                   jax.ShapeDtypeStruct((t, attn_width), BF16), jax.ShapeDtypeStruct((t, LANES), F32)],
        grid=(nt,),
        in_specs=[row, const((1, d)), const(wt.shape), const((1, attn_width)), const((1, attn_width))],
        out_specs=[row, pl.BlockSpec((tm, pool_width), lambda i: (i, 0)), half, half, half, half, half,
                   pl.BlockSpec((tm, LANES), lambda i: (i, 0))])[0]


def _shift_down(v, dist, row):
    return jnp.where(row >= dist, pltpu.roll(v, dist, 0), 0.0)


def _shift_up(v, dist, row, n):
    return jnp.where(row + dist < n, pltpu.roll(v, n - dist, 0), 0.0)


def _aug_lane(e):
    return HEAD_DIM if e == 0 else 0


def _forget_prefix(f, bias, qh, kh, n_batch, seq):
    def body(f_ref, b_ref, q_ref, k_ref, qa_ref, ka_ref):
        z = f_ref[...] + b_ref[...]
        acc = jnp.minimum(z, 0.0) - jnp.log(1.0 + jnp.exp(-jnp.abs(z)))
        row = lax.broadcasted_iota(jnp.int32, (seq, 1), 0)
        dist = 1
        while dist < seq:
            acc = acc + _shift_down(acc, dist, row)
            dist *= 2
        lane = lax.broadcasted_iota(jnp.int32, (1, LANES), 1)
        for h in range(N_HEADS):
            pair, e = divmod(h, 2)
            a0 = _aug_lane(e)
            own = (lane < HEAD_DIM) if e == 0 else (lane >= HEAD_DIM)
            fh = _pick_lane(acc, h)
            hi = fh.astype(BF16).astype(F32)
            rest = fh - hi
            mid = rest.astype(BF16).astype(F32)
            low = rest - mid
            q_ones = (lane >= a0 + 3) & (lane < a0 + 6)
            k_ones = (lane >= a0) & (lane < a0 + 3)
            q_aug = jnp.where(lane == a0, hi, jnp.where(lane == a0 + 1, mid, jnp.where(lane == a0 + 2, low,
                              jnp.where(q_ones, 1.0, 0.0))))
            k_aug = jnp.where(k_ones, 1.0, jnp.where(lane == a0 + 3, -hi, jnp.where(lane == a0 + 4, -mid,
                              jnp.where(lane == a0 + 5, -low, 0.0))))
            src = slice(pair * LANES, (pair + 1) * LANES)
            dst = slice(h * LANES, (h + 1) * LANES)
            qa_ref[:, dst] = jnp.where(own, q_ref[:, src].astype(F32), q_aug).astype(BF16)
            ka_ref[:, dst] = jnp.where(own, k_ref[:, src].astype(F32), k_aug).astype(BF16)

    width = qh.shape[1]
    tok = pl.BlockSpec((seq, width), lambda b: (b, 0))
    aug = pl.BlockSpec((seq, N_HEADS * LANES), lambda b: (b, 0))
    return pl.pallas_call(
        body, out_shape=[jax.ShapeDtypeStruct((n_batch * seq, N_HEADS * LANES), BF16)] * 2, grid=(n_batch,),
        in_specs=[pl.BlockSpec((seq, LANES), lambda b: (b, 0)), pl.BlockSpec((1, LANES), lambda b: (0, 0)), tok, tok],
        out_specs=[aug, aug], compiler_params=_params(), name="forget_prefix",
    )(f, bias, qh, kh)


def _pool_groups(pv_ref, pw_ref, ps_ref, seq):
    row = lax.broadcasted_iota(jnp.int32, (seq, 1), 0)
    pos = (row + 1).astype(F32)
    out = []
    for g, win in enumerate(POOL_WINDOWS):
        sl = slice(g * LANES, (g + 1) * LANES)
        xg = pv_ref[:, sl]
        acc = xg
        dist = 1
        while dist < win:
            acc = acc + _shift_down(acc, dist, row)
            dist *= 2
        pooled = (acc / jnp.minimum(pos, float(win)) - xg).astype(BF16)
        mixed = _dot(pooled, pw_ref[g])
        out.append((pooled, mixed, mixed * ps_ref[:, sl]))
    return out


def _pool_fwd(pv, pw, ps, onp, n_batch, seq):
    width = pv.shape[1]

    def body(pv_ref, pw_ref, ps_ref, on_ref, y_ref):
        groups = _pool_groups(pv_ref, pw_ref, ps_ref, seq)
        ssq = sum(jnp.sum(ms * ms, axis=1, keepdims=True) for _, _, ms in groups)
        r = lax.rsqrt(ssq * (1.0 / width) + EPS)
        for g, (_, _, ms) in enumerate(groups):
            sl = slice(g * LANES, (g + 1) * LANES)
            y_ref[:, sl] = ((ms * r) * on_ref[:, sl]).astype(BF16)

    return pl.pallas_call(
        body, out_shape=jax.ShapeDtypeStruct((n_batch * seq, width), BF16), grid=(n_batch,),
        in_specs=[pl.BlockSpec((seq, width), lambda b: (b, 0)), pl.BlockSpec(pw.shape, lambda b: (0, 0, 0)),
                  pl.BlockSpec((1, width), lambda b: (0, 0)), pl.BlockSpec((1, width), lambda b: (0, 0))],
        out_specs=pl.BlockSpec((seq, width), lambda b: (b, 0)),
        compiler_params=_params(), name="pool_fwd",
    )(pv, pw, ps, onp)


def _pool_bwd(pv, dyp, pw, ps, onp, n_batch, seq):
    width = pv.shape[1]

    def body(pv_ref, dy_ref, pw_ref, ps_ref, on_ref, dpv_ref, dpw_ref, dps_ref, don_ref):
        groups = _pool_groups(pv_ref, pw_ref, ps_ref, seq)
        ssq = sum(jnp.sum(ms * ms, axis=1, keepdims=True) for _, _, ms in groups)
        r = lax.rsqrt(ssq * (1.0 / width) + EPS)
        mean = sum(jnp.sum((dy_ref[:, g * LANES:(g + 1) * LANES] * on_ref[:, g * LANES:(g + 1) * LANES]) * (ms * r),
                           axis=1, keepdims=True) for g, (_, _, ms) in enumerate(groups)) * (1.0 / width)
        row = lax.broadcasted_iota(jnp.int32, (seq, 1), 0)
        pos = (row + 1).astype(F32)
        for g, (pooled, mixed, ms) in enumerate(groups):
            sl = slice(g * LANES, (g + 1) * LANES)
            dy = dy_ref[:, sl]
            xh = ms * r
            don_ref[:, sl] = jnp.sum(dy * xh, axis=0, keepdims=True)
            dms = r * (dy * on_ref[:, sl] - xh * mean)
            dps_ref[:, sl] = jnp.sum(dms * mixed, axis=0, keepdims=True)
            dmix = (dms * ps_ref[:, sl]).astype(BF16)
            dpw_ref[g] = _dot_tn(pooled, dmix)
            dpool = _dot_nt(dmix, pw_ref[g])
            win = POOL_WINDOWS[g]
            acc = dpool / jnp.minimum(pos, float(win))
            dist = 1
            while dist < win:
                acc = acc + _shift_up(acc, dist, row, seq)
                dist *= 2
            dpv_ref[:, sl] = (acc - dpool).astype(BF16)

    tok = pl.BlockSpec((seq, width), lambda b: (b, 0))
    vec = pl.BlockSpec((1, width), lambda b: (0, 0))
    pvec = pl.BlockSpec((None, 1, width), lambda b: (b, 0, 0))
    return pl.pallas_call(
        body,
        out_shape=[jax.ShapeDtypeStruct((n_batch * seq, width), BF16),
                   jax.ShapeDtypeStruct((n_batch,) + pw.shape, F32),
                   jax.ShapeDtypeStruct((n_batch, 1, width), F32), jax.ShapeDtypeStruct((n_batch, 1, width), F32)],
        grid=(n_batch,),
        in_specs=[tok, tok, pl.BlockSpec(pw.shape, lambda b: (0, 0, 0)), vec, vec],
        out_specs=[tok, pl.BlockSpec((None,) + pw.shape, lambda b: (b, 0, 0, 0)), pvec, pvec],
        compiler_params=_params(), name="pool_bwd",
    )(pv, dyp, pw, ps, onp)


def _pick_lane(tile, idx):
    lane = lax.broadcasted_iota(jnp.int32, (1, LANES), 1)
    return jnp.sum(jnp.where(lane == idx, tile, 0.0), axis=1, keepdims=True)


def _pick_row(tile, idx):
    sub = lax.broadcasted_iota(jnp.int32, (tile.shape[0], 1), 0)
    return jnp.sum(jnp.where(sub == idx, tile, 0.0), axis=0, keepdims=True)


def _put_lane(col, idx):
    lane = lax.broadcasted_iota(jnp.int32, (1, LANES), 1)
    return jnp.where(lane == idx, col, 0.0)


def _head_select(e):
    lo = _head_masks()
    return lo if e == 0 else jnp.logical_not(lo)


def _causal(st, shift):
    row = lax.broadcasted_iota(jnp.int32, st.shape, 0)
    col = lax.broadcasted_iota(jnp.int32, st.shape, 1) + shift
    return jnp.where(col >= row, st, NEG)


def _transpose_blocks(a):
    rows, cols = a.shape
    return jnp.concatenate(
        [jnp.concatenate([a[r:r + LANES, c:c + LANES].T for r in range(0, rows, LANES)], axis=1)
         for c in range(0, cols, LANES)], axis=0)


def _accumulate(ref, value, first):
    @pl.when(first)
    def _():
        ref[...] = value

    @pl.when(jnp.logical_not(first))
    def _():
        ref[...] += value


def _attn_fwd(qa, ka, vb, n_batch, seq, plan=None):
    tq = min(ATT_BLOCK, seq)
    nq, nsub, tk = seq // tq, tq // ATT_SUB, tq
    pairs = vb.shape[1] // LANES

    def body(q_ref, k_ref, v_ref, o_ref, lse_ref, acc_ref):
        i, p = pl.program_id(1), pl.program_id(2)
        row_lo = lax.broadcasted_iota(jnp.int32, (LANES, 1), 0) < HEAD_DIM
        qs = [q_ref[:, e * LANES:(e + 1) * LANES] for e in range(2)]
        acc_ref[...] = jnp.zeros_like(acc_ref)

        def tile(off, stats, diagonal):
            vj = v_ref[pl.ds(off, tk), :]
            new, alphas, pvs = [], [], []
            for e in range(2):
                st = _dot_nt(k_ref[pl.ds(off, tk), e * LANES:(e + 1) * LANES], qs[e])
                if diagonal:
                    st = _causal(st, 0)
                m, l = stats[e]
                m_new = jnp.maximum(m, jnp.max(st, axis=0, keepdims=True))
                alpha = jnp.exp(m - m_new)
                pt = jnp.exp(st - m_new)
                new.append((m_new, alpha * l + jnp.sum(pt, axis=0, keepdims=True)))
                alphas.append(alpha)
                pvs.append(_dot_tn(jnp.where(_head_select(e), vj, jnp.zeros_like(vj)), pt.astype(BF16)))
            acc_ref[...] = acc_ref[...] * jnp.where(row_lo, alphas[0], alphas[1]) + (pvs[0] + pvs[1])
            return tuple(new)

        init = ((jnp.full((1, tq), NEG, F32), jnp.zeros((1, tq), F32)),) * 2
        stats = lax.fori_loop(0, i, lambda j, st: tile(pl.multiple_of(j * tk, tk), st, False), init)
        (m0, l0), (m1, l1) = tile(pl.multiple_of(i * tk, tk), stats, True)
        out_t = acc_ref[...] / jnp.where(row_lo, l0, l1)
        sub = lax.broadcasted_iota(jnp.int32, (8, 1), 0)
        lse0, lse1 = m0 + jnp.log(l0), m1 + jnp.log(l1)
        for a in range(nsub):
            sl = slice(a * ATT_SUB, (a + 1) * ATT_SUB)
            o_ref[sl, :] = out_t[:, sl].T
            rows = jnp.where(sub == 2 * p, lse0[:, sl], 0.0) + jnp.where(sub == 2 * p + 1, lse1[:, sl], 0.0)
            _accumulate(lse_ref.at[a], rows, p == 0)

    return _pallas(
        body, name="attn_fwd", args=[qa, ka, vb],
        out_shape=[jax.ShapeDtypeStruct((n_batch * seq, pairs * LANES), F32),
                   jax.ShapeDtypeStruct((n_batch * seq // ATT_SUB, 8, ATT_SUB), F32)],
        grid=(n_batch, nq, pairs),
        in_specs=[pl.BlockSpec((tq, 2 * LANES), lambda b, i, p: (b * nq + i, p)),
                  pl.BlockSpec((seq, 2 * LANES), lambda b, i, p: (b, p)),
                  pl.BlockSpec((seq, LANES), lambda b, i, p: (b, p))],
        out_specs=[pl.BlockSpec((tq, LANES), lambda b, i, p: (b * nq + i, p)),
                   pl.BlockSpec((nsub, 8, ATT_SUB), lambda b, i, p: (b * nq + i, 0, 0))],
        scratch_shapes=[pltpu.VMEM((LANES, tq), F32)], plan=plan)


def _attn_bwd(qa, ka, vb, do, lse, delta, n_batch, seq, plan=None):
    tq = min(ATT_BLOCK, seq)
    nq, nsub = seq // tq, tq // ATT_SUB
    n_tiles = seq // ATT_SUB
    pairs = vb.shape[1] // LANES

    def body(q_ref, k_ref, v_ref, do_ref, lse_ref, dl_ref, dq_ref, dk_ref, dv_ref, dfq_ref, dfk_ref,
             dq0_ref, dq1_ref, dk0_ref, dk1_ref, dva_ref):
        p = pl.program_id(1)
        dqs, dks = (dq0_ref, dq1_ref), (dk0_ref, dk1_ref)
        for acc in (dk0_ref, dk1_ref, dva_ref):
            acc[...] = jnp.zeros_like(acc)
        dfq_cols = []
        for i in range(nq):
            rows_i = slice(i * tq, (i + 1) * tq)
            qs = [q_ref[rows_i, e * LANES:(e + 1) * LANES] for e in range(2)]
            dov = do_ref[rows_i, :]
            does = [jnp.where(_head_select(e), dov, jnp.zeros_like(dov)) for e in range(2)]
            stat = lambda ref, e: jnp.concatenate([_pick_row(ref[i * nsub + a], 2 * p + e) for a in range(nsub)], axis=1)
            ls, dl = [stat(lse_ref, e) for e in range(2)], [stat(dl_ref, e) for e in range(2)]
            for acc in dqs:
                acc[...] = jnp.zeros_like(acc)

            def tile(off, diagonal, qs=qs, dov=dov, does=does, ls=ls, dl=dl):
                vj = v_ref[pl.ds(off, tq), :]
                for e in range(2):
                    kj = k_ref[pl.ds(off, tq), e * LANES:(e + 1) * LANES]
                    st = _dot_nt(kj, qs[e])
                    if diagonal:
                        st = _causal(st, 0)
                    pt = jnp.exp(st - ls[e])
                    dva_ref[pl.ds(off, tq), :] += _dot(pt.astype(BF16), does[e])
                    dpt = _dot_nt(jnp.where(_head_select(e), vj, jnp.zeros_like(vj)), dov)
                    dst = (pt * (dpt - dl[e])).astype(BF16)
                    dks[e][pl.ds(off, tq), :] += _dot(dst, qs[e])
                    dqs[e][...] += _dot(_transpose_blocks(kj), dst)

            def step(j, carry, tile=tile):
                tile(pl.multiple_of(j * tq, tq), False)
                return carry

            lax.fori_loop(0, i, step, 0)
            tile(i * tq, True)
            dq0, dq1 = _transpose_blocks(dq0_ref[...]), _transpose_blocks(dq1_ref[...])
            dq_ref[rows_i, :] = jnp.where(_head_masks(), dq0, dq1)
            dfq_cols.append(_put_lane(_pick_lane(dq0, _aug_lane(0)), 2 * p) + _put_lane(_pick_lane(dq1, _aug_lane(1)), 2 * p + 1))
        dk0, dk1 = dk0_ref[...], dk1_ref[...]
        dk_ref[...] = jnp.where(_head_masks(), dk0, dk1)
        dv_ref[...] = dva_ref[...].astype(BF16)
        dfk = _put_lane(_pick_lane(dk0, _aug_lane(0) + 3), 2 * p) + _put_lane(_pick_lane(dk1, _aug_lane(1) + 3), 2 * p + 1)
        _accumulate(dfq_ref, jnp.concatenate(dfq_cols, axis=0), p == 0)
        _accumulate(dfk_ref, -dfk, p == 0)

    wide = pl.BlockSpec((seq, 2 * LANES), lambda b, p: (b, p))
    blk = pl.BlockSpec((seq, LANES), lambda b, p: (b, p))
    col = pl.BlockSpec((seq, LANES), lambda b, p: (b, 0))
    stat = pl.BlockSpec((n_tiles, 8, ATT_SUB), lambda b, p: (b, 0, 0))
    f32_blk, acc = jax.ShapeDtypeStruct((n_batch * seq, pairs * LANES), F32), pltpu.VMEM((seq, LANES), F32)
    return _pallas(
        body, name="attn_bwd", args=[qa, ka, vb, do, lse, delta],
        out_shape=[f32_blk, f32_blk, jax.ShapeDtypeStruct((n_batch * seq, pairs * LANES), BF16),
                   jax.ShapeDtypeStruct((n_batch * seq, LANES), F32), jax.ShapeDtypeStruct((n_batch * seq, LANES), F32)],
        grid=(n_batch, pairs), in_specs=[wide, wide, blk, blk, stat, stat], out_specs=[blk, blk, blk, col, col],
        scratch_shapes=[pltpu.VMEM((LANES, tq), F32), pltpu.VMEM((LANES, tq), F32), acc, acc, acc], plan=plan)


def _forget_bwd(dfq, dfk, f, bias, n_batch, seq):
    def body(dfq_ref, dfk_ref, f_ref, b_ref, df_ref, db_ref):
        acc = dfq_ref[...] + dfk_ref[...]
        row = lax.broadcasted_iota(jnp.int32, (seq, 1), 0)
        dist = 1
        while dist < seq:
            acc = acc + _shift_up(acc, dist, row, seq)
            dist *= 2
        df = acc * _sigmoid(-(f_ref[...] + b_ref[...]))
        df_ref[...] = df
        db_ref[...] = jnp.sum(df, axis=0, keepdims=True)

    col = pl.BlockSpec((seq, LANES), lambda b: (b, 0))
    return pl.pallas_call(
        body,
        out_shape=[jax.ShapeDtypeStruct((n_batch * seq, LANES), F32), jax.ShapeDtypeStruct((n_batch, 1, LANES), F32)],
        grid=(n_batch,), in_specs=[col, col, col, pl.BlockSpec((1, LANES), lambda b: (0, 0))],
        out_specs=[col, pl.BlockSpec((None, 1, LANES), lambda b: (b, 0, 0))],
        compiler_params=_params(), name="forget_bwd",
    )(dfq, dfk, f, bias)


def _mix_out(x1, yp, o, ona, woa, wob):
    t, d = x1.shape
    width = o.shape[1]
    tm = min(512, t)

    def body(x_ref, yp_ref, o_ref, on_ref, wa_ref, wb_ref, x2_ref, ya_ref):
        of = o_ref[...]
        ya = ((of * _rms(of)) * on_ref[...]).astype(BF16)
        ya_ref[...] = ya
        x2_ref[...] = x_ref[...] + (_dot(yp_ref[...], wa_ref[...]) + _dot(ya, wb_ref[...]))

    row = pl.BlockSpec((tm, d), lambda i: (i, 0))
    half = pl.BlockSpec((tm, width), lambda i: (i, 0))
    wspec = pl.BlockSpec((width, d), lambda i: (0, 0))
    return pl.pallas_call(
        body, out_shape=[jax.ShapeDtypeStruct((t, d), F32), jax.ShapeDtypeStruct((t, width), BF16)],
        grid=(t // tm,), in_specs=[row, half, half, pl.BlockSpec((1, width), lambda i: (0, 0)), wspec, wspec],
        out_specs=[row, half], compiler_params=_params(), name="mix_out",
    )(x1, yp, o, ona, woa, wob)


def _mix_out_bwd(dx2, o, yp, ya, ona, woa, wob, plan=None):
    t, d = dx2.shape
    width = o.shape[1]
    tm = min(512, t)
    nt = t // tm

    def body(dx_ref, o_ref, yp_ref, ya_ref, on_ref, wa_ref, wb_ref, dyp_ref, do_ref, dl_ref, dwa_ref, dwb_ref, don_ref):
        @pl.when(pl.program_id(0) == 0)
        def _():
            dwa_ref[...] = jnp.zeros_like(dwa_ref)
            dwb_ref[...] = jnp.zeros_like(dwb_ref)

        dxb = dx_ref[...].astype(BF16)
        dwa_ref[...] += _dot_tn(yp_ref[...], dxb)
        dwb_ref[...] += _dot_tn(ya_ref[...], dxb)
        dyp_ref[...] = _dot_nt(dxb, wa_ref[...])
        of = o_ref[...]
        dov, dgr = _rms_bwd(of, _rms(of), on_ref[...], _dot_nt(dxb, wb_ref[...]))
        don_ref[...] = jnp.sum(dgr, axis=0, keepdims=True)
        do_ref[...] = dov.astype(BF16)
        lo = _head_masks()
        prod = dov * of
        delta = jnp.zeros((tm, LANES), F32)
        for blk in range(width // LANES):
            pb = prod[:, blk * LANES:(blk + 1) * LANES]
            delta = delta + _put_lane(jnp.sum(jnp.where(lo, pb, 0.0), axis=1, keepdims=True), 2 * blk)
            delta = delta + _put_lane(jnp.sum(jnp.where(lo, 0.0, pb), axis=1, keepdims=True), 2 * blk + 1)
        for c in range(tm // ATT_SUB):
            dl_ref[c] = delta[c * ATT_SUB:(c + 1) * ATT_SUB, :].T[0:8, :]

    row = pl.BlockSpec((tm, d), lambda i: (i, 0))
    half = pl.BlockSpec((tm, width), lambda i: (i, 0))
    wspec = pl.BlockSpec((width, d), lambda i: (0, 0))
    return _pallas(
        body, name="mix_out_bwd", args=[dx2, o, yp, ya, ona, woa, wob],
        out_shape=[jax.ShapeDtypeStruct((t, width), F32), jax.ShapeDtypeStruct((t, width), BF16),
                   jax.ShapeDtypeStruct((t // ATT_SUB, 8, ATT_SUB), F32), jax.ShapeDtypeStruct((width, d), F32),
                   jax.ShapeDtypeStruct((width, d), F32), jax.ShapeDtypeStruct((nt, 1, width), F32)],
        grid=(nt,),
        in_specs=[row, half, half, half, pl.BlockSpec((1, width), lambda i: (0, 0)), wspec, wspec],
        out_specs=[half, half, pl.BlockSpec((tm // ATT_SUB, 8, ATT_SUB), lambda i: (i, 0, 0)), wspec, wspec,
                   pl.BlockSpec((None, 1, width), lambda i: (i, 0, 0))], plan=plan)


def _mix_in_bwd(dx2, x1, gain, hm, dpv, dqh, q, dkh, k, dv, df, qn, kn, wt):
    t, d = x1.shape
    width = q.shape[1]
    pool_width = dpv.shape[1]
    tm = min(512, t)
    nt = t // tm
    scale = HEAD_DIM ** -0.5
    c_q, c_k, c_v = pool_width, pool_width + width, pool_width + 2 * width
    c_f = c_v + width

    def body(dx2_ref, x_ref, g_ref, hm_ref, dpv_ref, dqh_ref, q_ref, dkh_ref, k_ref, dv_ref, df_ref, qn_ref, kn_ref,
             wt_ref, dx_ref, dxh_ref, dwt_ref, dg_ref, dqn_ref, dkn_ref):
        @pl.when(pl.program_id(0) == 0)
        def _():
            dwt_ref[...] = jnp.zeros_like(dwt_ref)

        lo = _head_masks()
        for part, rows in enumerate(_row_halves(tm)):
            def put(ref, sl, value):
                ref[:, sl] = value if part == 0 else ref[:, sl] + value

            hm = hm_ref[rows, :]
            pieces = [(0, dpv_ref[rows, :])]
            for c0, raw_ref, dh_ref, n_ref, dn_ref, mul in ((c_q, q_ref, dqh_ref, qn_ref, dqn_ref, scale),
                                                           (c_k, k_ref, dkh_ref, kn_ref, dkn_ref, 1.0)):
                cols = []
                for blk in range(width // LANES):
                    sl = slice(blk * LANES, (blk + 1) * LANES)
                    xb = raw_ref[rows, sl]
                    gb = dh_ref[rows, sl] * mul
                    r = _head_rms(xb, lo)
                    xh = xb * r
                    dyg = gb * n_ref[:, sl]
                    cols.append((r * (dyg - xh * _head_mean(dyg * xh, lo))).astype(BF16))
                    put(dn_ref, sl, jnp.sum(gb * xh, axis=0, keepdims=True))
                pieces.append((c0, jnp.concatenate(cols, axis=1)))
            pieces.append((c_v, dv_ref[rows, :]))
            pieces.append((c_f, df_ref[rows, :].astype(BF16)))
            dhm = jnp.zeros((tm // 2, d), F32)
            for c0, piece in pieces:
                dwt_ref[c0:c0 + piece.shape[1], :] += _dot_tn(piece, hm)
                dhm = dhm + _dot(piece, wt_ref[c0:c0 + piece.shape[1], :])
            xf = x_ref[rows, :]
            dxn, dgr = _rms_bwd(xf, _rms(xf), g_ref[...], dhm)
            dx = dx2_ref[rows, :] + dxn
            dx_ref[rows, :] = dx
            dxh_ref[rows, :] = (0.5 * dx).astype(BF16)
            put(dg_ref, slice(None), jnp.sum(dgr, axis=0, keepdims=True))

    row = pl.BlockSpec((tm, d), lambda i: (i, 0))
    half = pl.BlockSpec((tm, width), lambda i: (i, 0))
    const = lambda shape: pl.BlockSpec(shape, lambda i: (0, 0))
    pvec = lambda n: pl.BlockSpec((None, 1, n), lambda i: (i, 0, 0))
    return pl.pallas_call(
        body,
        out_shape=[jax.ShapeDtypeStruct((t, d), F32), jax.ShapeDtypeStruct((t, d), BF16), jax.ShapeDtypeStruct(wt.shape, F32),
                   jax.ShapeDtypeStruct((nt, 1, d), F32),
                   jax.ShapeDtypeStruct((nt, 1, width), F32), jax.ShapeDtypeStruct((nt, 1, width), F32)],
        grid=(nt,),
        in_specs=[row, row, const((1, d)), row, pl.BlockSpec((tm, pool_width), lambda i: (i, 0)), half, half, half, half,
                  half, pl.BlockSpec((tm, LANES), lambda i: (i, 0)), const((1, width)), const((1, width)),
                  const(wt.shape)],
        out_specs=[row, row, const(wt.shape), pvec(d), pvec(width), pvec(width)],
        compiler_params=_params(), name="mix_in_bwd",
    )(dx2, x1, gain, hm, dpv, dqh, q, dkh, k, dv, df, qn, kn, wt)


def _mesh_pos():
    return lax.axis_index("x"), lax.axis_index("y"), lax.axis_index("c")


def _other_chips(x, y):
    return [(1 - x, y), (x, 1 - y), (1 - x, 1 - y)]


def _remote(src, dst, send_sem, recv_sem, device):
    return pltpu.make_async_remote_copy(src_ref=src, dst_ref=dst, send_sem=send_sem, recv_sem=recv_sem,
                                        device_id=device, device_id_type=pl.DeviceIdType.MESH)


def _half_rows(n_rows, which):
    half = n_rows // 2
    return pl.ds(pl.multiple_of(which * half, 8), half)


def _row_block(rows, cols, itemsize=4):
    rb = rows
    while rb * cols * itemsize > (2 << 20) and rb % 32 == 0:
        rb //= 2
    return rb


def _place_cast(ws, chip, tag):
    n = len(ws)
    rows, cols = ws[0].shape
    rb = _row_block(rows, cols)

    def body(k_ref, *refs):
        for w_ref, o_ref in zip(refs[:n], refs[n:]):
            o_ref[...] = w_ref[...].astype(BF16)

    return pl.pallas_call(
        body, out_shape=[jax.ShapeDtypeStruct((N_CHIPS, rows, cols), BF16)] * n,
        grid_spec=pltpu.PrefetchScalarGridSpec(
            num_scalar_prefetch=1, grid=(rows // rb,),
            in_specs=[pl.BlockSpec((rb, cols), lambda i, k: (i, 0))] * n,
            out_specs=[pl.BlockSpec((None, rb, cols), lambda i, k: (k[0], i, 0))] * n),
        compiler_params=_params(), name="place_" + tag,
    )(chip, *ws)


class _Plan:
    def __init__(self, ins, outs, alias, sems, start, finish, middle=None, middle_at=(3, 4)):
        self.ins, self.outs, self.alias, self.sems = ins, outs, alias, sems
        self.start, self.middle, self.finish, self.middle_at = start, middle, finish, middle_at


def _merge_plans(a, b):
    ni, no, ns = len(a.ins), len(a.outs), len(a.sems)
    alias = dict(a.alias)
    alias.update({ni + i: no + o for i, o in b.alias.items()})

    def both(which):
        stage_a, stage_b = getattr(a, which), getattr(b, which)
        if stage_a is None and stage_b is None:
            return None

        def run(ins, outs, sems):
            if stage_a is not None:
                stage_a(ins[:ni], outs[:no], sems[:ns])
            if stage_b is not None:
                stage_b(ins[ni:], outs[no:], sems[ns:])
        return run

    return _Plan(list(a.ins) + list(b.ins), list(a.outs) + list(b.outs), alias, list(a.sems) + list(b.sems),
                 both("start"), both("finish"), both("middle"), a.middle_at if a.middle is not None else b.middle_at)


def _run_plan(plan, name):
    n_in, n_out = len(plan.ins), len(plan.outs)

    def body(*refs):
        parts = refs[:n_in], refs[n_in:n_in + n_out], refs[n_in + n_out:]
        plan.start(*parts)
        if plan.middle is not None:
            plan.middle(*parts)
        plan.finish(*parts)

    return pl.pallas_call(
        body, out_shape=plan.outs, in_specs=[ANY] * n_in, out_specs=[ANY] * n_out, scratch_shapes=plan.sems,
        input_output_aliases=plan.alias, name=name,
    )(*plan.ins)


def _pallas(body, *, name, args, in_specs, out_shape, out_specs, grid, scratch_shapes=(), plan=None, aliases=None):
    n_in, n_out, n_scr = len(args), len(out_shape), len(scratch_shapes)
    plan = plan or _Plan([], [], {}, [], None, None)
    p_in, p_out = len(plan.ins), len(plan.outs)

    def carrying(*refs):
        ins, p_ins = refs[:n_in], refs[n_in:n_in + p_in]
        o0 = n_in + p_in
        outs, p_outs = refs[o0:o0 + n_out], refs[o0 + n_out:o0 + n_out + p_out]
        s0 = o0 + n_out + p_out
        scr, p_sems = refs[s0:s0 + n_scr], refs[s0 + n_scr:]
        ids = [pl.program_id(a) for a in range(len(grid))]

        if plan.start is not None:
            @pl.when(functools.reduce(jnp.logical_and, [i == 0 for i in ids]))
            def _():
                plan.start(p_ins, p_outs, p_sems)

        body(*ins, *outs, *scr)

        if plan.middle is not None:
            step, n_steps = 0, 1
            for i, g in zip(ids, grid):
                step, n_steps = step * g + i, n_steps * g

            @pl.when(step == (plan.middle_at[0] * n_steps) // plan.middle_at[1])
            def _():
                plan.middle(p_ins, p_outs, p_sems)

        if plan.finish is not None:
            @pl.when(functools.reduce(jnp.logical_and, [i == g - 1 for i, g in zip(ids, grid)]))
            def _():
                plan.finish(p_ins, p_outs, p_sems)

    aliases = dict(aliases or {})
    aliases.update({n_in + i: n_out + o for i, o in plan.alias.items()})
    res = pl.pallas_call(
        carrying, out_shape=list(out_shape) + list(plan.outs), grid=grid,
        in_specs=list(in_specs) + [ANY] * p_in, out_specs=list(out_specs) + [ANY] * p_out,
        scratch_shapes=list(scratch_shapes) + list(plan.sems),
        input_output_aliases=aliases, compiler_params=_params(), name=name,
    )(*args, *plan.ins)
    return list(res[:n_out]), list(res[n_out:])


def _plan_gather(stacks):
    n = len(stacks)
    relations = range(3)

    def ici_copies(outs, sems):
        x, y, c = _mesh_pos()
        chips = _other_chips(x, y)
        cps = []
        for w in range(n):
            own = outs[w].at[2 * x + y, _half_rows(stacks[w].shape[1], c)]
            cps += [_remote(own, own, sems[0].at[w, j], sems[1].at[w, j], (*chips[j], c)) for j in relations]
        return cps

    def start(ins, outs, sems):
        for cp in ici_copies(outs, sems):
            cp.start()

    def forwards(outs, sems, core):
        x, y, c = _mesh_pos()
        slots = [2 * cx + cy for cx, cy in _other_chips(x, y)]
        cps = []
        for w in range(n):
            rows = _half_rows(stacks[w].shape[1], core)
            for j in relations:
                landed = outs[w].at[slots[j], rows]
                cps.append((_remote(landed, landed, sems[0].at[w, j], sems[1].at[w, j], (x, y, 1 - c)),
                            _remote(landed, landed, sems[2].at[w, j], sems[3].at[w, j], (x, y, 1 - c))))
        return cps

    def middle(ins, outs, sems):
        c = _mesh_pos()[2]
        for arrival, forward in forwards(outs, sems, c):
            arrival.wait_recv()
            forward.start()

    def finish(ins, outs, sems):
        c = _mesh_pos()[2]
        for _, forward in forwards(outs, sems, 1 - c):
            forward.wait_recv()
        for cp in ici_copies(outs, sems) + [forward for _, forward in forwards(outs, sems, c)]:
            cp.wait_send()

    return _Plan(stacks, [jax.ShapeDtypeStruct(s.shape, s.dtype) for s in stacks], {w: w for w in range(n)},
                 [pltpu.SemaphoreType.DMA((n, 3))] * 4, start, finish, middle)


RELAY_SEMS = [pltpu.SemaphoreType.DMA((3, 2))] * 4 + [pltpu.SemaphoreType.DMA((3, 3))] * 2
RELAY_STAGES = ("send", "pass on", "x neighbour", "y neighbour", "diagonal", "end")


def _relay_gather_stage(stage, outs, sems):
    assert stage in RELAY_STAGES
    send, recv, relay_send, relay_recv, d2d_send, d2d_recv = sems
    n = len(outs)
    rh = outs[0].shape[1] // 2
    mx, my, c = _mesh_pos()
    sibling = (mx, my, 1 - c)
    near = [(1 - mx, my), (mx, 1 - my)]
    slots = [2 * cx + cy for cx, cy in near] + [2 * (1 - mx) + (1 - my)]

    def piece(w, slot, core, quarter=None):
        if quarter is None:
            return outs[w].at[slot, _half_rows(2 * rh, core)]
        return outs[w].at[slot, pl.ds(pl.multiple_of(core * rh + quarter * (rh // 2), 8), rh // 2)]

    def to_near(w, j):
        own = piece(w, 2 * mx + my, c)
        return _remote(own, own, send.at[w, j], recv.at[w, j], (*near[j], c))

    def from_near(w, j):
        landed = piece(w, slots[j], c)
        return _remote(landed, landed, send.at[w, j], recv.at[w, j], sibling)

    def onward(w, j, slot):
        part = piece(w, slot, c, quarter=j)
        return _remote(part, part, relay_send.at[w, j], relay_recv.at[w, j], (*near[1 - j], c))

    def to_sibling(w, j, core):
        landed = piece(w, slots[j], core)
        return _remote(landed, landed, d2d_send.at[w, j], d2d_recv.at[w, j], sibling)

    if stage == "send":
        for w in range(n):
            for j in range(2):
                to_near(w, j).start()
    elif stage == "pass on":
        for w in range(n):
            for j in range(2):
                from_near(w, j).wait_recv()
                onward(w, j, slots[j]).start()
                to_sibling(w, j, c).start()
    elif stage in ("x neighbour", "y neighbour"):
        for w in range(n):
            to_sibling(w, ("x neighbour", "y neighbour").index(stage), 1 - c).wait_recv()
    elif stage == "diagonal":
        for w in range(n):
            for j in range(2):
                onward(w, j, slots[2]).wait_recv()
            to_sibling(w, 2, c).start()
        for w in range(n):
            to_sibling(w, 2, 1 - c).wait_recv()
    else:
        for w in range(n):
            for j in range(2):
                to_near(w, j).wait_send()
                onward(w, j, slots[j]).wait_send()
            for j in range(3):
                to_sibling(w, j, c).wait_send()


def _plan_gather_relay(stacks):
    def stages(which):
        def run(ins, outs, sems):
            for stage in which:
                _relay_gather_stage(stage, outs, sems)
        return run

    return _Plan(stacks, [jax.ShapeDtypeStruct(s.shape, s.dtype) for s in stacks], {w: w for w in range(len(stacks))},
                 RELAY_SEMS, stages(RELAY_STAGES[:1]), stages(RELAY_STAGES[2:]), stages(RELAY_STAGES[1:2]), middle_at=(1, 2))


def _ffn_fwd_gathering(x, gain, stacks, order, later):
    t, d = x.shape
    nch, fc, _ = stacks[0].shape
    assert nch == N_CHIPS
    n = len(stacks)
    tm = min(FFN_STAGED_TILE, t)
    nt = t // tm
    p_in, p_out = len(later.ins), len(later.outs)
    relay = _relay_gather_stage

    def body(order_ref, x_ref, g_ref, *refs):
        later_in, refs = refs[n:n + p_in], refs[n + p_in:]
        o_ref, h_ref, a_ref, b_ref, s_ref = refs[:5]
        stack_refs, later_out, refs = refs[5:5 + n], refs[5 + n:5 + n + p_out], refs[5 + n + p_out:]
        w_ref, hs_ref, acc_ref, w_sem = refs[:4]
        relay_sems, later_sems = refs[4:10], refs[10:]
        k, i = pl.program_id(0), pl.program_id(1)
        tile = pl.ds(pl.multiple_of(i * tm, tm), tm)

        @pl.when(i == 0)
        def _():
            for chunk, stages in enumerate([("send",), ("pass on", "x neighbour"), ("y neighbour",), ("diagonal",)]):
                @pl.when(k == chunk)
                def _():
                    for stage in stages:
                        relay(stage, stack_refs, relay_sems)
                    if chunk == 1 and later.start is not None:
                        later.start(later_in, later_out, later_sems)
            loads = [pltpu.make_async_copy(stack_refs[w].at[order_ref[k]], w_ref.at[w], w_sem.at[w]) for w in range(n)]
            for cp in loads:
                cp.start()
            for cp in loads:
                cp.wait()

        @pl.when(k == 0)
        def _():
            xf = x_ref[...]
            hb = ((xf * _rms(xf)) * g_ref[...]).astype(BF16)
            h_ref[...] = hb
            hs_ref[tile, :] = hb
            acc_ref[tile, :] = jnp.zeros((tm, d), F32)

        for rows in _row_halves(tm):
            part = pl.ds(pl.multiple_of(i * tm + rows.start, tm // 2), tm // 2)
            h = hs_ref[part, :]
            a = _dot_nt(h, w_ref[0])
            b = _dot_nt(h, w_ref[1])
            sb = ((a * (0.5 * jnp.tanh(0.5 * a) + 0.5)) * b).astype(BF16)
            a_ref[rows, :] = a.astype(BF16)
            b_ref[rows, :] = b.astype(BF16)
            s_ref[rows, :] = sb
            acc_ref[part, :] += _dot(sb, w_ref[2])

        @pl.when(k == nch - 1)
        def _():
            o_ref[...] = x_ref[...] + 0.5 * acc_ref[tile, :]

        if later.middle is not None:
            @pl.when((k == nch - 1) & (i == nt // 2))
            def _():
                later.middle(later_in, later_out, later_sems)

        @pl.when((k == nch - 1) & (i == nt - 1))
        def _():
            relay("end", stack_refs, relay_sems)
            if later.finish is not None:
                later.finish(later_in, later_out, later_sems)

    ends = lambda k, i: jnp.where((k == 0) | (k == nch - 1), i, 0)
    act = pl.BlockSpec((None, tm, fc), lambda k, i, order: (order[k], i, 0))
    out_shape = [jax.ShapeDtypeStruct((t, d), F32), jax.ShapeDtypeStruct((t, d), BF16)]
    out_shape += [jax.ShapeDtypeStruct((nch, t, fc), BF16)] * 3
    out_shape += [jax.ShapeDtypeStruct(s.shape, s.dtype) for s in stacks] + list(later.outs)
    aliases = {3 + w: 5 + w for w in range(n)}
    aliases.update({3 + n + i: 5 + n + o for i, o in later.alias.items()})
    res = pl.pallas_call(
        body, out_shape=out_shape,
        grid_spec=pltpu.PrefetchScalarGridSpec(
            num_scalar_prefetch=1, grid=(nch, nt),
            in_specs=[pl.BlockSpec((tm, d), lambda k, i, order: (ends(k, i), 0)),
                      pl.BlockSpec((1, d), lambda k, i, order: (0, 0))] + [ANY] * (n + p_in),
            out_specs=[pl.BlockSpec((tm, d), lambda k, i, order: (jnp.where(k == nch - 1, i, 0), 0)),
                       pl.BlockSpec((tm, d), lambda k, i, order: (jnp.where(k == 0, i, nt - 1), 0)),
                       act, act, act] + [ANY] * (n + p_out),
            scratch_shapes=[pltpu.VMEM((n, fc, d), BF16), pltpu.VMEM((t, d), BF16), pltpu.VMEM((t, d), F32),
                            pltpu.SemaphoreType.DMA((n,))] + RELAY_SEMS + list(later.sems)),
        input_output_aliases=aliases, compiler_params=_params(), name="ffn_fwd",
    )(order, x, gain, *stacks, *later.ins)
    return list(res[:5]), list(res[5:5 + n]), list(res[5 + n:])


def _plan_sibling_halves(gs):
    n = len(gs)

    def copies(ins, outs, sems):
        x, y, c = _mesh_pos()
        return [_remote(ins[w].at[:, _half_rows(gs[w].shape[1], 1 - c), :], outs[w], sems[0].at[w], sems[1].at[w],
                        (x, y, 1 - c)) for w in range(n)]

    def start(ins, outs, sems):
        for cp in copies(ins, outs, sems):
            cp.start()

    def finish(ins, outs, sems):
        for cp in copies(ins, outs, sems):
            cp.wait()

    return _Plan(gs, [jax.ShapeDtypeStruct((g.shape[0], g.shape[1] // 2, g.shape[2]), g.dtype) for g in gs], {},
                 [pltpu.SemaphoreType.DMA((n,))] * 2, start, finish)


def _plan_chip_exchange(ps):
    n = len(ps)

    def copies(ins, outs, sems):
        x, y, c = _mesh_pos()
        return [_remote(ins[w].at[2 * cx + cy], outs[w].at[j], sems[0].at[w, j], sems[1].at[w, j], (cx, cy, c))
                for w in range(n) for j, (cx, cy) in enumerate(_other_chips(x, y))]

    def start(ins, outs, sems):
        for cp in copies(ins, outs, sems):
            cp.start()

    def finish(ins, outs, sems):
        for cp in copies(ins, outs, sems):
            cp.wait()

    return _Plan(ps, [jax.ShapeDtypeStruct((3,) + p.shape[1:], p.dtype) for p in ps], {},
                 [pltpu.SemaphoreType.DMA((n, 3))] * 2, start, finish)


def _plan_sibling_share(gs):
    n = len(gs)

    def copies(outs, sems, which):
        x, y, c = _mesh_pos()
        cps = []
        for w in range(n):
            rows = outs[w].at[_half_rows(gs[w].shape[0], c if which == "mine" else 1 - c)]
            cps.append(_remote(rows, rows, sems[0].at[w], sems[1].at[w], (x, y, 1 - c)))
        return cps

    def start(ins, outs, sems):
        for cp in copies(outs, sems, "mine"):
            cp.start()

    def finish(ins, outs, sems):
        for cp in copies(outs, sems, "mine"):
            cp.wait_send()
        for cp in copies(outs, sems, "theirs"):
            cp.wait_recv()

    return _Plan(gs, [jax.ShapeDtypeStruct(g.shape, g.dtype) for g in gs], {w: w for w in range(n)},
                 [pltpu.SemaphoreType.DMA((n,))] * 2, start, finish)


def _same_shape_groups(arrays):
    groups = {}
    for i, a in enumerate(arrays):
        groups.setdefault(a.shape, []).append(i)
    return list(groups.values())


def _add_sibling(gs, r1s, ids, tag):
    n = len(gs)
    nch, rh, cols = r1s[0].shape

    def body(ids_ref, *refs):
        for g_ref, r_ref, o_ref in zip(refs[:n], refs[n:2 * n], refs[2 * n:]):
            o_ref[...] = (g_ref[...] + r_ref[...]).astype(BF16)

    blk = lambda fn: pl.BlockSpec((None, rh, cols), fn)
    return pl.pallas_call(
        body, out_shape=[jax.ShapeDtypeStruct(r1s[0].shape, BF16)] * n,
        grid_spec=pltpu.PrefetchScalarGridSpec(
            num_scalar_prefetch=1, grid=(nch,),
            in_specs=[blk(lambda k, ids: (k, ids[1], 0))] * n + [blk(lambda k, ids: (k, 0, 0))] * n,
            out_specs=[blk(lambda k, ids: (k, 0, 0))] * n),
        compiler_params=_params(), name="add_sibling_" + tag,
    )(ids, *gs, *r1s)


def _add_chips(gs, r1s, r2s, ids, tag):
    n = len(gs)
    _, rh, cols = r1s[0].shape
    nb = 2 if rh % 32 == 0 else 1
    rb = rh // nb

    def body(ids_ref, *refs):
        for g_ref, r1_ref, r2_ref, o_ref in zip(refs[:n], refs[n:2 * n], refs[2 * n:3 * n], refs[3 * n:]):
            own = g_ref[...] + r1_ref[...]
            o_ref[...] = ((own + r2_ref[0].astype(F32)) + r2_ref[1].astype(F32)) + r2_ref[2].astype(F32)

    return pl.pallas_call(
        body, out_shape=[jax.ShapeDtypeStruct((2 * rh, cols), F32)] * n,
        grid_spec=pltpu.PrefetchScalarGridSpec(
            num_scalar_prefetch=1, grid=(nb,),
            in_specs=[pl.BlockSpec((None, rb, cols), lambda i, ids: (ids[0], ids[1] * nb + i, 0))] * n
            + [pl.BlockSpec((None, rb, cols), lambda i, ids: (ids[0], i, 0))] * n
            + [pl.BlockSpec((3, rb, cols), lambda i, ids: (0, i, 0))] * n,
            out_specs=[pl.BlockSpec((rb, cols), lambda i, ids: (ids[1] * nb + i, 0))] * n),
        compiler_params=_params(), name="add_chips_" + tag,
    )(ids, *gs, *r1s, *r2s)


VEC_ROWS = 8


N_DEVICES = 8


def _small_pack(part, d, width):
    names = ("ffn1_norm", "mix_norm", "ffn2_norm", "pool_scale", "out_norm_pool", "out_norm_attn", "qn", "kn", "b_forget",
             "loss")
    args = [part[k] for k in names]

    def body(g1_ref, gm_ref, g2_ref, ps_ref, onp_ref, ona_ref, qn_ref, kn_ref, bf_ref, loss_ref, vbuf):
        lo = _head_masks()

        def fold_heads(ref):
            v = jnp.sum(ref[...], axis=0)
            acc = jnp.zeros((VEC_ROWS, LANES), F32)
            for blk in range(width // LANES):
                vb = jnp.broadcast_to(v[:, blk * LANES:(blk + 1) * LANES], (VEC_ROWS, LANES))
                acc = acc + vb + pltpu.roll(vb, HEAD_DIM, 1)
            return jnp.where(lo, acc, 0.0)[0:1, :]

        vbuf[0] = jnp.zeros((VEC_ROWS, d), F32)
        vbuf[0, 0:1, :] = jnp.sum(g1_ref[...], axis=0)
        vbuf[0, 1:2, :] = jnp.sum(gm_ref[...], axis=0)
        vbuf[0, 2:3, :] = jnp.sum(g2_ref[...], axis=0)
        vbuf[0, 5:6, 0:LANES] = jnp.sum(loss_ref[...], axis=0)[0:1, :]
        vbuf[0, 3:4, 0:width] = jnp.sum(ps_ref[...], axis=0)
        vbuf[0, 3:4, width:2 * width] = jnp.sum(onp_ref[...], axis=0)
        vbuf[0, 4:5, 0:width] = jnp.sum(ona_ref[...], axis=0)
        vbuf[0, 4:5, width:width + LANES] = fold_heads(qn_ref)
        vbuf[0, 4:5, width + LANES:width + 2 * LANES] = fold_heads(kn_ref)
        vbuf[0, 4:5, width + 2 * LANES:width + 3 * LANES] = jnp.sum(bf_ref[...], axis=0)

    return pl.pallas_call(
        body, out_shape=jax.ShapeDtypeStruct((N_DEVICES, VEC_ROWS, d), F32),
        in_specs=[VM] * len(args), out_specs=VM, compiler_params=_params(), name="small_pack",
    )(*args)


def _pool_pack(dpw):
    def body(pw_ref, pbuf):
        pbuf[0] = jnp.sum(pw_ref[...], axis=0)

    return pl.pallas_call(
        body, out_shape=jax.ShapeDtypeStruct((N_DEVICES,) + dpw.shape[1:], F32),
        in_specs=[VM], out_specs=VM, compiler_params=_params(), name="pool_pack",
    )(dpw)


def _plan_all_to_all(stacks):
    n = len(stacks)

    def copies(outs, sems):
        x, y, c = _mesh_pos()
        cps = []
        for r in range(1, N_DEVICES):
            peer = (x if not r & 4 else 1 - x, y if not r & 2 else 1 - y, c if not r & 1 else 1 - c)
            cps += [_remote(outs[w].at[0], outs[w].at[r], sems[0].at[w, r - 1], sems[1].at[w, r - 1], peer) for w in range(n)]
        return cps

    def start(ins, outs, sems):
        for cp in copies(outs, sems):
            cp.start()

    def finish(ins, outs, sems):
        for cp in copies(outs, sems):
            cp.wait()

    return _Plan(stacks, [jax.ShapeDtypeStruct(s.shape, s.dtype) for s in stacks], {w: w for w in range(n)},
                 [pltpu.SemaphoreType.DMA((n, N_DEVICES - 1))] * 2, start, finish)


def _small_sum(vstack, pstack, me):
    def body(me_ref, vbuf, pbuf, vec_ref, pw_ref):
        vec = vbuf[me_ref[0]]
        pw = pbuf[me_ref[0]]
        for dev in range(1, N_DEVICES):
            vec = vec + vbuf[jnp.bitwise_xor(me_ref[0], dev)]
            pw = pw + pbuf[jnp.bitwise_xor(me_ref[0], dev)]
        vec_ref[...] = vec
        pw_ref[...] = pw

    full = lambda s: pl.BlockSpec(s.shape, lambda i, me: (0,) * len(s.shape))
    outs = [jax.ShapeDtypeStruct(vstack.shape[1:], F32), jax.ShapeDtypeStruct(pstack.shape[1:], F32)]
    return pl.pallas_call(
        body, out_shape=outs,
        grid_spec=pltpu.PrefetchScalarGridSpec(num_scalar_prefetch=1, grid=(1,), in_specs=[full(vstack), full(pstack)],
                                               out_specs=[full(o) for o in outs]),
        compiler_params=_params(), name="small_sum",
    )(me, vstack, pstack)


def _adamw(ws, gs, ms, vs, tag):
    n = len(ws)
    rows, cols = ws[0].shape
    rb = rows
    while rb * cols * 4 * n > (3 << 20) and rb % 16 == 0:
        rb //= 2

    def body(*refs):
        for j in range(n):
            w_ref, g_ref, m_ref, v_ref = (refs[k * n + j] for k in range(4))
            go_ref, d_ref, mo_ref, vo_ref = (refs[(4 + k) * n + j] for k in range(4))
            gv = g_ref[...]
            go_ref[...] = gv
            m2 = ADAM_B1 * m_ref[...] + (1.0 - ADAM_B1) * gv
            v2 = ADAM_B2 * v_ref[...] + (1.0 - ADAM_B2) * (gv * gv)
            m_hat = m2 / (1.0 - ADAM_B1 ** ADAM_STEP)
            v_hat = v2 / (1.0 - ADAM_B2 ** ADAM_STEP)
            d_ref[...] = -ADAM_LR * (m_hat / (jnp.sqrt(v_hat) + ADAM_EPS) + ADAM_WD * w_ref[...])
            mo_ref[...] = m2
            vo_ref[...] = v2

    spec = pl.BlockSpec((rb, cols), lambda i: (i, 0))
    res, _ = _pallas(
        body, name="adamw_" + tag, args=[*ws, *gs, *ms, *vs], out_shape=[jax.ShapeDtypeStruct(ws[0].shape, F32)] * (4 * n),
        grid=(rows // rb,), in_specs=[spec] * (4 * n), out_specs=[spec] * (4 * n))
    return [tuple(res[k * n + j] for k in range(4)) for j in range(n)]


def _pack_vec(p, d, width):
    pad = lambda v: jnp.pad(v, (0, LANES - v.shape[0]))
    row3 = jnp.concatenate([p["pool_scale"], p["out_norm_pool"]])
    row4 = jnp.concatenate([p["out_norm_attn"], pad(p["q_norm"]), pad(p["k_norm"]), pad(p["b_forget"]),
                            jnp.zeros((d - width - 3 * LANES,), F32)])
    rows = [p["ffn1_norm"], p["mix_norm"], p["ffn2_norm"], row3, row4]
    return jnp.pad(jnp.stack(rows), ((0, VEC_ROWS - len(rows)), (0, 0)))


def _unpack_vec(vec, width):
    return dict(ffn1_norm=vec[0], mix_norm=vec[1], ffn2_norm=vec[2], pool_scale=vec[3, :width],
                out_norm_pool=vec[3, width:2 * width], out_norm_attn=vec[4, :width],
                q_norm=vec[4, width:width + HEAD_DIM], k_norm=vec[4, width + LANES:width + LANES + HEAD_DIM],
                b_forget=vec[4, width + 2 * LANES:width + 2 * LANES + N_HEADS])


WEIGHT_NAMES = ("ffn1_norm", "ffn1_w_gate", "ffn1_w_up", "ffn1_w_down", "mix_norm", "w_in", "b_forget", "pool_w",
                "pool_scale", "q_norm", "k_norm", "out_norm_pool", "out_norm_attn", "w_out", "ffn2_norm",
                "ffn2_w_gate", "ffn2_w_up", "ffn2_w_down")
BIG_NAMES = ("ffn1_w_gate", "ffn1_w_up", "ffn1_w_down", "w_in", "w_out", "ffn2_w_gate", "ffn2_w_up", "ffn2_w_down")
TRANSPOSED_NAMES = ("ffn1_w_gate", "ffn1_w_up", "w_in", "ffn2_w_gate", "ffn2_w_up")
FFN1_NAMES = ("ffn1_w_gate", "ffn1_w_up", "ffn1_w_down")
MIX_NAMES = ("w_in", "w_out")
FFN2_NAMES = ("ffn2_w_gate", "ffn2_w_up", "ffn2_w_down")


def kernel(x, ffn1_norm, ffn1_w_gate, ffn1_w_up, ffn1_w_down, mix_norm, w_in, b_forget, pool_w, pool_scale, q_norm, k_norm, out_norm_pool, out_norm_attn, w_out, ffn2_norm, ffn2_w_gate, ffn2_w_up, ffn2_w_down, loss_target, m_ffn1_norm, m_ffn1_w_gate, m_ffn1_w_up, m_ffn1_w_down, m_mix_norm, m_w_in, m_b_forget, m_pool_w, m_pool_scale, m_q_norm, m_k_norm, m_out_norm_pool, m_out_norm_attn, m_w_out, m_ffn2_norm, m_ffn2_w_gate, m_ffn2_w_up, m_ffn2_w_down, v_ffn1_norm, v_ffn1_w_gate, v_ffn1_w_up, v_ffn1_w_down, v_mix_norm, v_w_in, v_b_forget, v_pool_w, v_pool_scale, v_q_norm, v_k_norm, v_out_norm_pool, v_out_norm_attn, v_w_out, v_ffn2_norm, v_ffn2_w_gate, v_ffn2_w_up, v_ffn2_w_down):
    given = dict(locals())
    w = {n: given[n] for n in WEIGHT_NAMES}
    m = {n: given["m_" + n] for n in WEIGHT_NAMES}
    v = {n: given["v_" + n] for n in WEIGHT_NAMES}
    n_batch, seq, d = x.shape
    width = pool_scale.shape[0]
    in_rows = w_in.shape[1]
    in_cols = N_CHIPS * in_rows
    in_pad = -(-in_rows // 32) * 32
    in_cols_pad = in_cols - N_HEADS + LANES

    work = lambda a, n: a.T if n in TRANSPOSED_NAMES else a
    exchanged = lambda a, n: jnp.pad(a, ((0, in_pad - in_rows), (0, 0))) if n == "w_in" else a

    mesh_x, mesh_y, mesh_c = _mesh_pos()
    ids = jnp.stack([2 * mesh_x + mesh_y, mesh_c]).astype(jnp.int32)

    row = lambda a: a.reshape(1, -1)
    g1, gm, g2, ps, onp, ona = (row(a) for a in (ffn1_norm, mix_norm, ffn2_norm, pool_scale, out_norm_pool, out_norm_attn))
    qn, kn = row(jnp.tile(q_norm, N_HEADS)), row(jnp.tile(k_norm, N_HEADS))
    bf = row(jnp.pad(b_forget, (0, LANES - N_HEADS)))
    pwb = pool_w.astype(BF16)
    xf, tgt = x.reshape(n_batch * seq, d), loss_target.reshape(n_batch * seq, d)

    def grouped(call, names, *lists):
        out = [None] * len(names)
        for idx in _same_shape_groups(lists[0]):
            res = call(*[[lst[i] for i in idx] for lst in lists], names[idx[0]])
            for i, r in zip(idx, res):
                out[i] = r
        return out

    placed = dict(zip(BIG_NAMES, grouped(lambda ws, tag: _place_cast(ws, ids, tag), BIG_NAMES,
                                         [exchanged(work(w[n], n), n) for n in BIG_NAMES])))
    landing = jnp.stack([2 * cx + cy for cx, cy in [(mesh_x, mesh_y)] + _other_chips(mesh_x, mesh_y)]).astype(jnp.int32)
    (x1, h1, a1, b1, s1), (wg1, wu1, wd1), (w_in_all, w_out_all) = _ffn_fwd_gathering(
        xf, g1, [placed[n] for n in FFN1_NAMES], landing, _plan_gather([placed[n] for n in MIX_NAMES]))
    w_in_t = jnp.pad(w_in_all[:, :in_rows].reshape(in_cols, d), ((0, in_cols_pad - in_cols), (0, 0)))
    w_out_full = w_out_all.reshape(N_CHIPS * w_out.shape[0], d)
    woa, wob = w_out_full[:width], w_out_full[width:]

    hm, pv, q, k, qh, kh, vb, f = _mix_proj(x1, gm, w_in_t, qn, kn, width, width)
    qa, ka = _forget_prefix(f, bf, qh, kh, n_batch, seq)
    yp = _pool_fwd(pv, pwb, ps, onp, n_batch, seq)
    (o, lse), (wg2, wu2, wd2) = _attn_fwd(qa, ka, vb, n_batch, seq, plan=_plan_gather_relay([placed[n] for n in FFN2_NAMES]))
    x2, ya = _mix_out(x1, yp, o, ona, woa, wob)
    (dy, h2, a2, b2, s2, lpart, dyh), _ = _ffn_fwd(x2, g2, wg2, wu2, wd2, target=tgt)

    def to_chips(gs, arrived, tags):
        return grouped(lambda g, r, tag: _add_sibling(g, r, ids, tag), tags, gs, arrived)

    def own_rows(gs, from_sibling, from_chips, tags):
        return grouped(lambda g, ra, rb, tag: _add_chips(g, ra, rb, ids, tag), tags, gs, from_sibling, from_chips)

    (dx2, da2, db2, dg2), _ = _ffn_bwd_x(dy, x2, g2, a2, b2, wg2, wu2, wd2, "ffn2_bwd_x")
    dw2, _ = _ffn_bwd_w([(da2, h2), (db2, h2), (s2, dyh)], "ffn2_bwd_w")
    (dyp, do, delta, dwoa, dwob, dona), sib2 = _mix_out_bwd(dx2, o, yp, ya, ona, woa, wob, plan=_plan_sibling_halves(dw2))
    dpv, dpw, dps, donp = _pool_bwd(pv, dyp, pwb, ps, onp, n_batch, seq)
    (dqh, dkh, dv, dfq, dfk), chips2 = _attn_bwd(qa, ka, vb, do, lse, delta, n_batch, seq,
                                                 plan=_plan_chip_exchange(to_chips(dw2, sib2, FFN2_NAMES)))
    df, dbf = _forget_bwd(dfq, dfk, f, bf, n_batch, seq)
    dx1, dx1h, dw_in_t, dgm, dqn, dkn = _mix_in_bwd(dx2, x1, gm, hm, dpv, dqh, q, dkh, k, dv, df, qn, kn, w_in_t)
    in_base = [in_rows * k // 8 * 8 for k in range(N_CHIPS)]
    d_w_in = jnp.stack([dw_in_t[b:b + in_pad] for b in in_base])
    d_w_out = jnp.concatenate([dwoa, dwob], axis=0).reshape(N_CHIPS, w_out.shape[0], d)
    dwm = [d_w_in, d_w_out]
    down = FFN1_NAMES[2:]
    dwd1, arrived = _ffn_bwd_w([(s1, dx1h)], "ffn1_bwd_w_down",
                               plan=_merge_plans(_plan_sibling_halves(dwm), _plan_all_to_all(
                                   [_pool_pack(dpw.reshape(n_batch, -1, pool_w.shape[-1]))])))
    sibm, pstack = arrived[:len(dwm)], arrived[len(dwm)]
    (da1, db1), arrived = _ffn_bwd_a(dx1h, a1, b1, wd1, "ffn1_bwd_a",
                                     plan=_merge_plans(_plan_sibling_halves(dwd1),
                                                       _plan_chip_exchange(to_chips(dwm, sibm, MIX_NAMES))))
    sibd, chipsm = arrived[:1], arrived[1:]
    gate_up = FFN1_NAMES[:2]
    dwgu1, chipsd = _ffn_bwd_w([(da1, h1), (db1, h1)], "ffn1_bwd_w_gate_up",
                               plan=_plan_chip_exchange(to_chips(dwd1, sibd, down)))
    n_tiles = (n_batch * seq) // min(FFN_TILE, n_batch * seq)
    first = max(n_tiles // 4, 1)
    begun, sibgu = _ffn_bwd_h(dx1, xf, g1, da1, db1, wg1, wu1, "ffn1_bwd_h_first", (0, first),
                              plan=_plan_sibling_halves(dwgu1))
    earlier = (own_rows(dwd1, sibd, chipsd, down) + own_rows(dwm, sibm, chipsm, MIX_NAMES)
               + own_rows(dw2, sib2, chips2, FFN2_NAMES))
    (gx, dg1), arrived = _ffn_bwd_h(dx1, xf, g1, da1, db1, wg1, wu1, "ffn1_bwd_h_rest", (first, n_tiles), prev=begun,
                                    plan=_merge_plans(_plan_chip_exchange(to_chips(dwgu1, sibgu, gate_up)),
                                                      _plan_sibling_share(earlier)))
    chipsgu, shared = arrived[:len(gate_up)], arrived[len(gate_up):]

    part = dict(ffn1_norm=dg1, mix_norm=dgm, ffn2_norm=dg2, b_forget=dbf, pool_scale=dps, out_norm_pool=donp,
                out_norm_attn=dona, qn=dqn, kn=dkn, loss=lpart)
    mine = own_rows(dwgu1, sibgu, chipsgu, gate_up)
    last = _run_plan(_merge_plans(_plan_sibling_share(mine), _plan_all_to_all([_small_pack(part, d, width)])), "last_exchange")
    vstack = last[len(mine)]
    g_vec, g_pw = _small_sum(vstack, pstack, jnp.reshape(4 * mesh_x + 2 * mesh_y + mesh_c, (1,)).astype(jnp.int32))
    loss = g_vec[5, 0]
    reduced = dict(zip(gate_up + down + MIX_NAMES + FFN2_NAMES, list(last[:len(mine)]) + list(shared)))
    reduced["w_in"] = lax.dynamic_slice(reduced["w_in"], ((in_rows * ids[0]) % 8, 0), (in_rows, d))

    grads, delta, new_m, new_v = {}, {}, {}, {}
    for names in (FFN2_NAMES, FFN1_NAMES, ("w_in",), ("w_out",)):
        stepped = _adamw([work(w[n], n) for n in names], [reduced[n] for n in names], [work(m[n], n) for n in names],
                         [work(v[n], n) for n in names], names[0])
        for n, step in zip(names, stepped):
            grads[n], delta[n], new_m[n], new_v[n] = (work(a, n) for a in step)
    flat_pw = lambda a: a.reshape(-1, a.shape[-1])
    (_, d_pw, m_pw, v_pw), = _adamw([flat_pw(pool_w)], [g_pw], [flat_pw(m_pool_w)], [flat_pw(v_pool_w)], "pool_w")
    (_, d_vec, m_vec, v_vec), = _adamw([_pack_vec(w, d, width)], [g_vec], [_pack_vec(m, d, width)],
                                       [_pack_vec(v, d, width)], "vectors")
    grads.update(_unpack_vec(g_vec, width), pool_w=g_pw.reshape(pool_w.shape))
    delta.update(_unpack_vec(d_vec, width), pool_w=d_pw.reshape(pool_w.shape))
    new_m.update(_unpack_vec(m_vec, width), pool_w=m_pw.reshape(pool_w.shape))
    new_v.update(_unpack_vec(v_vec, width), pool_w=v_pw.reshape(pool_w.shape))
    return (loss, gx.reshape(x.shape), *[grads[n] for n in WEIGHT_NAMES], *[delta[n] for n in WEIGHT_NAMES],
            *[new_m[n] for n in WEIGHT_NAMES], *[new_v[n] for n in WEIGHT_NAMES])
```

```python
import functools

import jax
import jax.numpy as jnp
from jax import lax
from jax.experimental import pallas as pl
from jax.experimental.pallas import tpu as pltpu

F32 = jnp.float32
BF16 = jnp.bfloat16
EPS = 1e-6
NEG = -1e30
ADAM_LR = 0.001
ADAM_B1 = 0.9
ADAM_B2 = 0.999
ADAM_EPS = 1e-08
ADAM_WD = 0.01
ADAM_STEP = 10
POOL_WINDOWS = (2, 4, 8, 16)
HEAD_DIM = 64
N_HEADS = 8
LANES = 128
N_CHIPS = 4
ATT_BLOCK = 512
ATT_SUB = 128
FFN_TILE = 1024
FFN_STAGED_TILE = 512
PLACE_STEP_BYTES = 2 << 20
ADAMW_STEP_BYTES = 3 << 20
VMEM_LIMIT = 62 * 1024 * 1024
ANY = pl.BlockSpec(memory_space=pl.ANY)
VM = pl.BlockSpec(memory_space=pltpu.VMEM)


def _params(**kw):
    return pltpu.CompilerParams(vmem_limit_bytes=VMEM_LIMIT, **kw)


def _dot(a, b):
    return jnp.dot(a, b, preferred_element_type=F32)


def _dot_nt(a, b):
    return lax.dot_general(a, b, (((1,), (1,)), ((), ())), preferred_element_type=F32)


def _dot_tn(a, b):
    return lax.dot_general(a, b, (((0,), (0,)), ((), ())), preferred_element_type=F32)


def _sigmoid(z):
    return 1.0 / (1.0 + jnp.exp(-z))


def _rms(xf):
    return lax.rsqrt(jnp.mean(xf * xf, axis=-1, keepdims=True) + EPS)


def _rms_bwd(xf, r, gain, dh):
    xh = xf * r
    dyg = dh * gain
    return r * (dyg - xh * jnp.mean(dyg * xh, axis=-1, keepdims=True)), dh * xh


def _total(v):
    return jnp.sum(jnp.sum(v, axis=1, keepdims=True), axis=0, keepdims=True)


def _ffn_fwd(x, gain, wg, wu, wd, target=None, plan=None):
    t, d = x.shape
    nch, fc, _ = wg.shape
    tm = min(FFN_TILE, t)
    nt = t // tm
    with_loss = target is not None

    def body(*refs):
        if with_loss:
            x_ref, g_ref, wg_ref, wu_ref, wd_ref, t_ref, o_ref, h_ref, a_ref, b_ref, s_ref, l_ref, oh_ref, acc_ref = refs
        else:
            x_ref, g_ref, wg_ref, wu_ref, wd_ref, o_ref, h_ref, a_ref, b_ref, s_ref, acc_ref = refs
        k = pl.program_id(1)

        @pl.when(k == 0)
        def _():
            xf = x_ref[...]
            h_ref[...] = ((xf * _rms(xf)) * g_ref[...]).astype(BF16)
            acc_ref[...] = jnp.zeros_like(acc_ref)

        for rows in _row_halves(tm):
            h = h_ref[rows, :]
            a = _dot_nt(h, wg_ref[...])
            b = _dot_nt(h, wu_ref[...])
            sb = ((a * (0.5 * jnp.tanh(0.5 * a) + 0.5)) * b).astype(BF16)
            a_ref[rows, :] = a.astype(BF16)
            b_ref[rows, :] = b.astype(BF16)
            s_ref[rows, :] = sb
            acc_ref[rows, :] += _dot(sb, wd_ref[...])

        @pl.when(k == nch - 1)
        def _():
            y = x_ref[...] + 0.5 * acc_ref[...]
            if with_loss:
                e = y - t_ref[...]
                o_ref[...] = e * (1.0 / d)
                oh_ref[...] = (e * (0.5 / d)).astype(BF16)
                l_ref[...] = jnp.broadcast_to(_total(e * e) * (0.5 / d), l_ref.shape)
            else:
                o_ref[...] = y

    row = pl.BlockSpec((tm, d), lambda i, k: (i, 0))
    chunk = pl.BlockSpec((None, fc, d), lambda i, k: (k, 0, 0))
    act = pl.BlockSpec((None, tm, fc), lambda i, k: (k, i, 0))
    in_specs = [row, pl.BlockSpec((1, d), lambda i, k: (0, 0)), chunk, chunk, chunk]
    out_shape = [jax.ShapeDtypeStruct((t, d), F32), jax.ShapeDtypeStruct((t, d), BF16)]
    out_shape += [jax.ShapeDtypeStruct((nch, t, fc), BF16)] * 3
    out_specs = [row, row, act, act, act]
    args = [x, gain, wg, wu, wd]
    if with_loss:
        in_specs.append(row)
        args.append(target)
        out_shape += [jax.ShapeDtypeStruct((nt, 8, LANES), F32), jax.ShapeDtypeStruct((t, d), BF16)]
        out_specs += [pl.BlockSpec((None, 8, LANES), lambda i, k: (i, 0, 0)), row]
    return _pallas(body, name="ffn_fwd_loss" if with_loss else "ffn_fwd", args=args, in_specs=in_specs,
                   out_shape=out_shape, out_specs=out_specs, grid=(nt, nch),
                   scratch_shapes=[pltpu.VMEM((tm, d), F32)], plan=plan)


def _row_halves(n):
    return [slice(0, n // 2), slice(n // 2, n)]


def _swiglu_grads(dyh, a_ref, b_ref, wd_ref, rows):
    ds = _dot_nt(dyh, wd_ref[...])
    av = a_ref[rows, :].astype(F32)
    bv = b_ref[rows, :].astype(F32)
    th = jnp.tanh(0.5 * av)
    sig = 0.5 * th + 0.5
    dab = ((ds * bv) * (sig * (1.0 + av * (0.5 - 0.5 * th)))).astype(BF16)
    return dab, (ds * (av * sig)).astype(BF16)


def _ffn_bwd_a(dyh, a, b, wd, name, plan=None):
    t, d = dyh.shape
    nch, fc, _ = wd.shape
    tm = min(FFN_TILE, t)

    def body(dyh_ref, a_ref, b_ref, wd_ref, da_ref, db_ref):
        for rows in _row_halves(tm):
            da_ref[rows, :], db_ref[rows, :] = _swiglu_grads(dyh_ref[rows, :], a_ref, b_ref, wd_ref, rows)

    act = pl.BlockSpec((None, tm, fc), lambda i, k: (k, i, 0))
    return _pallas(
        body, name=name, args=[dyh, a, b, wd], out_shape=[jax.ShapeDtypeStruct((nch, t, fc), BF16)] * 2, grid=(t // tm, nch),
        in_specs=[pl.BlockSpec((tm, d), lambda i, k: (i, 0)), act, act, pl.BlockSpec((None, fc, d), lambda i, k: (k, 0, 0))],
        out_specs=[act, act], plan=plan)


def _ffn_bwd_h(dy, x, gain, da, db, wg, wu, name, tiles, prev=None, plan=None):
    t, d = x.shape
    nch, fc, _ = wg.shape
    tm = min(FFN_TILE, t)
    nt = t // tm
    t0, t1 = tiles

    def body(*refs):
        dy_ref, x_ref, g_ref, da_ref, db_ref, wg_ref, wu_ref = refs[:7]
        dx_ref, dg_ref, acc_ref = refs[-3:]
        k = pl.program_id(1)

        @pl.when(k == 0)
        def _():
            acc_ref[...] = jnp.zeros_like(acc_ref)

        acc_ref[...] += _dot(da_ref[...], wg_ref[...]) + _dot(db_ref[...], wu_ref[...])

        @pl.when(k == nch - 1)
        def _():
            xf = x_ref[...]
            dxn, dgr = _rms_bwd(xf, _rms(xf), g_ref[...], acc_ref[...])
            dx_ref[...] = dy_ref[...] + dxn
            dg_ref[...] = jnp.sum(dgr, axis=0, keepdims=True)

    row = pl.BlockSpec((tm, d), lambda i, k: (i + t0, 0))
    chunk = pl.BlockSpec((None, fc, d), lambda i, k: (k, 0, 0))
    act = pl.BlockSpec((None, tm, fc), lambda i, k: (k, i + t0, 0))
    args = [dy, x, gain, da, db, wg, wu]
    in_specs = [row, row, pl.BlockSpec((1, d), lambda i, k: (0, 0)), act, act, chunk, chunk]
    aliases = {}
    if prev is not None:
        aliases = {len(args): 0, len(args) + 1: 1}
        args += list(prev)
        in_specs += [ANY, ANY]
    return _pallas(
        body, name=name, args=args, out_shape=[jax.ShapeDtypeStruct((t, d), F32), jax.ShapeDtypeStruct((nt, 1, d), F32)],
        grid=(t1 - t0, nch), in_specs=in_specs,
        out_specs=[row, pl.BlockSpec((None, 1, d), lambda i, k: (i + t0, 0, 0))],
        scratch_shapes=[pltpu.VMEM((tm, d), F32)], plan=plan, aliases=aliases)


def _ffn_bwd_x(dy, x, gain, a, b, wg, wu, wd, name, plan=None):
    t, d = x.shape
    nch, fc, _ = wg.shape
    tm = min(FFN_TILE, t)
    nt = t // tm

    def body(dy_ref, x_ref, g_ref, a_ref, b_ref, wg_ref, wu_ref, wd_ref, dx_ref, da_ref, db_ref, dg_ref, acc_ref):
        k = pl.program_id(1)

        @pl.when(k == 0)
        def _():
            acc_ref[...] = jnp.zeros_like(acc_ref)

        for rows in _row_halves(tm):
            dab, dbb = _swiglu_grads((0.5 * dy_ref[rows, :]).astype(BF16), a_ref, b_ref, wd_ref, rows)
            da_ref[rows, :] = dab
            db_ref[rows, :] = dbb
            acc_ref[rows, :] += _dot(dab, wg_ref[...]) + _dot(dbb, wu_ref[...])

        @pl.when(k == nch - 1)
        def _():
            xf = x_ref[...]
            dxn, dgr = _rms_bwd(xf, _rms(xf), g_ref[...], acc_ref[...])
            dx_ref[...] = dy_ref[...] + dxn
            dg_ref[...] = jnp.sum(dgr, axis=0, keepdims=True)

    row = pl.BlockSpec((tm, d), lambda i, k: (i, 0))
    chunk = pl.BlockSpec((None, fc, d), lambda i, k: (k, 0, 0))
    act = pl.BlockSpec((None, tm, fc), lambda i, k: (k, i, 0))
    return _pallas(
        body, name=name, args=[dy, x, gain, a, b, wg, wu, wd],
        out_shape=[jax.ShapeDtypeStruct((t, d), F32), jax.ShapeDtypeStruct((nch, t, fc), BF16),
                   jax.ShapeDtypeStruct((nch, t, fc), BF16), jax.ShapeDtypeStruct((nt, 1, d), F32)],
        grid=(nt, nch),
        in_specs=[row, row, pl.BlockSpec((1, d), lambda i, k: (0, 0)), act, act, chunk, chunk, chunk],
        out_specs=[row, act, act, pl.BlockSpec((None, 1, d), lambda i, k: (i, 0, 0))],
        scratch_shapes=[pltpu.VMEM((tm, d), F32)], plan=plan)


def _ffn_bwd_w(pairs, name, plan=None):
    n = len(pairs)
    nch, t, fc = pairs[0][0].shape
    d = pairs[0][1].shape[1]
    tm = min(FFN_TILE, t)

    def body(*refs):
        @pl.when(pl.program_id(1) == 0)
        def _():
            for o_ref in refs[2 * n:]:
                o_ref[...] = jnp.zeros_like(o_ref)

        for j in range(n):
            refs[2 * n + j][...] += _dot_tn(refs[j][...], refs[n + j][...])

    row = pl.BlockSpec((tm, d), lambda k, i: (i, 0))
    act = pl.BlockSpec((None, tm, fc), lambda k, i: (k, i, 0))
    chunk = pl.BlockSpec((None, fc, d), lambda k, i: (k, 0, 0))
    return _pallas(body, name=name, args=[p[0] for p in pairs] + [p[1] for p in pairs],
                   out_shape=[jax.ShapeDtypeStruct((nch, fc, d), F32)] * n, grid=(nch, t // tm),
                   in_specs=[act] * n + [row] * n, out_specs=[chunk] * n, plan=plan)


def _head_masks():
    lane = lax.broadcasted_iota(jnp.int32, (1, LANES), 1)
    return lane < HEAD_DIM


def _head_rms(x, lo):
    x2 = x * x
    s0 = jnp.sum(jnp.where(lo, x2, 0.0), axis=1, keepdims=True)
    s1 = jnp.sum(jnp.where(lo, 0.0, x2), axis=1, keepdims=True)
    return jnp.where(lo, lax.rsqrt(s0 * (1.0 / HEAD_DIM) + EPS), lax.rsqrt(s1 * (1.0 / HEAD_DIM) + EPS))


def _head_mean(v, lo):
    s0 = jnp.sum(jnp.where(lo, v, 0.0), axis=1, keepdims=True)
    s1 = jnp.sum(jnp.where(lo, 0.0, v), axis=1, keepdims=True)
    return jnp.where(lo, s0, s1) * (1.0 / HEAD_DIM)


def _mix_proj(x1, gain, wt, qn, kn, pool_width, attn_width):
    t, d = x1.shape
    tm = min(512, t)
    nt = t // tm
    scale = HEAD_DIM ** -0.5
    c_q, c_k, c_v = pool_width, pool_width + attn_width, pool_width + 2 * attn_width
    c_f = c_v + attn_width

    def body(x_ref, g_ref, wt_ref, qn_ref, kn_ref, hm_ref, pv_ref, q_ref, k_ref, qh_ref, kh_ref, vb_ref, f_ref):
        lo = _head_masks()
        for rows in _row_halves(tm):
            xf = x_ref[rows, :]
            hm = ((xf * _rms(xf)) * g_ref[...]).astype(BF16)
            hm_ref[rows, :] = hm
            f_ref[rows, :] = _dot_nt(hm, wt_ref[c_f:c_f + LANES, :])
            pv_ref[rows, :] = _dot_nt(hm, wt_ref[0:pool_width, :])
            vb_ref[rows, :] = _dot_nt(hm, wt_ref[c_v:c_v + attn_width, :]).astype(BF16)
            for c0, raw_ref, hat_ref, n_ref, mul in ((c_q, q_ref, qh_ref, qn_ref, scale), (c_k, k_ref, kh_ref, kn_ref, 1.0)):
                raw = _dot_nt(hm, wt_ref[c0:c0 + attn_width, :])
                raw_ref[rows, :] = raw
                for blk in range(attn_width // LANES):
                    sl = slice(blk * LANES, (blk + 1) * LANES)
                    xb = raw[:, sl]
                    hat_ref[rows, sl] = (((xb * _head_rms(xb, lo)) * n_ref[:, sl]) * mul).astype(BF16)

    row = pl.BlockSpec((tm, d), lambda i: (i, 0))
    half = pl.BlockSpec((tm, attn_width), lambda i: (i, 0))
    const = lambda shape: pl.BlockSpec(shape, lambda i: (0, 0))
    return _pallas(
        body, name="mix_proj", args=[x1, gain, wt, qn, kn],
        out_shape=[jax.ShapeDtypeStruct((t, d), BF16), jax.ShapeDtypeStruct((t, pool_width), F32),
                   jax.ShapeDtypeStruct((t, attn_width), F32), jax.ShapeDtypeStruct((t, attn_width), F32),
                   jax.ShapeDtypeStruct((t, attn_width), BF16), jax.ShapeDtypeStruct((t, attn_width), BF16),
                   jax.ShapeDtypeStruct((t, attn_width), BF16), jax.ShapeDtypeStruct((t, LANES), F32)],
        grid=(nt,),
        in_specs=[row, const((1, d)), const(wt.shape), const((1, attn_width)), const((1, attn_width))],
        out_specs=[row, pl.BlockSpec((tm, pool_width), lambda i: (i, 0)), half, half, half, half, half,
                   pl.BlockSpec((tm, LANES), lambda i: (i, 0))])[0]


def _shift_down(v, dist, row):
    return jnp.where(row >= dist, pltpu.roll(v, dist, 0), 0.0)


def _shift_up(v, dist, row, n):
    return jnp.where(row + dist < n, pltpu.roll(v, n - dist, 0), 0.0)


def _aug_lane(e):
    return HEAD_DIM if e == 0 else 0


def _forget_prefix(f, bias, qh, kh, n_batch, seq):
    def body(f_ref, b_ref, q_ref, k_ref, qa_ref, ka_ref):
        z = f_ref[...] + b_ref[...]
        acc = jnp.minimum(z, 0.0) - jnp.log(1.0 + jnp.exp(-jnp.abs(z)))
        row = lax.broadcasted_iota(jnp.int32, (seq, 1), 0)
        dist = 1
        while dist < seq:
            acc = acc + _shift_down(acc, dist, row)
            dist *= 2
        lane = lax.broadcasted_iota(jnp.int32, (1, LANES), 1)
        for h in range(N_HEADS):
            pair, e = divmod(h, 2)
            a0 = _aug_lane(e)
            own = (lane < HEAD_DIM) if e == 0 else (lane >= HEAD_DIM)
            fh = _pick_lane(acc, h)
            hi = fh.astype(BF16).astype(F32)
            rest = fh - hi
            mid = rest.astype(BF16).astype(F32)
            low = rest - mid
            q_ones = (lane >= a0 + 3) & (lane < a0 + 6)
            k_ones = (lane >= a0) & (lane < a0 + 3)
            q_aug = jnp.where(lane == a0, hi, jnp.where(lane == a0 + 1, mid, jnp.where(lane == a0 + 2, low,
                              jnp.where(q_ones, 1.0, 0.0))))
            k_aug = jnp.where(k_ones, 1.0, jnp.where(lane == a0 + 3, -hi, jnp.where(lane == a0 + 4, -mid,
                              jnp.where(lane == a0 + 5, -low, 0.0))))
            src = slice(pair * LANES, (pair + 1) * LANES)
            dst = slice(h * LANES, (h + 1) * LANES)
            qa_ref[:, dst] = jnp.where(own, q_ref[:, src].astype(F32), q_aug).astype(BF16)
            ka_ref[:, dst] = jnp.where(own, k_ref[:, src].astype(F32), k_aug).astype(BF16)

    width = qh.shape[1]
    tok = pl.BlockSpec((seq, width), lambda b: (b, 0))
    aug = pl.BlockSpec((seq, N_HEADS * LANES), lambda b: (b, 0))
    return pl.pallas_call(
        body, out_shape=[jax.ShapeDtypeStruct((n_batch * seq, N_HEADS * LANES), BF16)] * 2, grid=(n_batch,),
        in_specs=[pl.BlockSpec((seq, LANES), lambda b: (b, 0)), pl.BlockSpec((1, LANES), lambda b: (0, 0)), tok, tok],
        out_specs=[aug, aug], compiler_params=_params(), name="forget_prefix",
    )(f, bias, qh, kh)


def _pool_groups(pv_ref, pw_ref, ps_ref, seq):
    row = lax.broadcasted_iota(jnp.int32, (seq, 1), 0)
    pos = (row + 1).astype(F32)
    out = []
    for g, win in enumerate(POOL_WINDOWS):
        sl = slice(g * LANES, (g + 1) * LANES)
        xg = pv_ref[:, sl]
        acc = xg
        dist = 1
        while dist < win:
            acc = acc + _shift_down(acc, dist, row)
            dist *= 2
        pooled = (acc / jnp.minimum(pos, float(win)) - xg).astype(BF16)
        mixed = _dot(pooled, pw_ref[g])
        out.append((pooled, mixed, mixed * ps_ref[:, sl]))
    return out


def _pool_fwd(pv, pw, ps, onp, n_batch, seq):
    width = pv.shape[1]

    def body(pv_ref, pw_ref, ps_ref, on_ref, y_ref):
        groups = _pool_groups(pv_ref, pw_ref, ps_ref, seq)
        ssq = sum(jnp.sum(ms * ms, axis=1, keepdims=True) for _, _, ms in groups)
        r = lax.rsqrt(ssq * (1.0 / width) + EPS)
        for g, (_, _, ms) in enumerate(groups):
            sl = slice(g * LANES, (g + 1) * LANES)
            y_ref[:, sl] = ((ms * r) * on_ref[:, sl]).astype(BF16)

    return pl.pallas_call(
        body, out_shape=jax.ShapeDtypeStruct((n_batch * seq, width), BF16), grid=(n_batch,),
        in_specs=[pl.BlockSpec((seq, width), lambda b: (b, 0)), pl.BlockSpec(pw.shape, lambda b: (0, 0, 0)),
                  pl.BlockSpec((1, width), lambda b: (0, 0)), pl.BlockSpec((1, width), lambda b: (0, 0))],
        out_specs=pl.BlockSpec((seq, width), lambda b: (b, 0)),
        compiler_params=_params(), name="pool_fwd",
    )(pv, pw, ps, onp)


def _pool_bwd(pv, dyp, pw, ps, onp, n_batch, seq):
    width = pv.shape[1]

    def body(pv_ref, dy_ref, pw_ref, ps_ref, on_ref, dpv_ref, dpw_ref, dps_ref, don_ref):
        groups = _pool_groups(pv_ref, pw_ref, ps_ref, seq)
        ssq = sum(jnp.sum(ms * ms, axis=1, keepdims=True) for _, _, ms in groups)
        r = lax.rsqrt(ssq * (1.0 / width) + EPS)
        mean = sum(jnp.sum((dy_ref[:, g * LANES:(g + 1) * LANES] * on_ref[:, g * LANES:(g + 1) * LANES]) * (ms * r),
                           axis=1, keepdims=True) for g, (_, _, ms) in enumerate(groups)) * (1.0 / width)
        row = lax.broadcasted_iota(jnp.int32, (seq, 1), 0)
        pos = (row + 1).astype(F32)
        for g, (pooled, mixed, ms) in enumerate(groups):
            sl = slice(g * LANES, (g + 1) * LANES)
            dy = dy_ref[:, sl]
            xh = ms * r
            don_ref[:, sl] = jnp.sum(dy * xh, axis=0, keepdims=True)
            dms = r * (dy * on_ref[:, sl] - xh * mean)
            dps_ref[:, sl] = jnp.sum(dms * mixed, axis=0, keepdims=True)
            dmix = (dms * ps_ref[:, sl]).astype(BF16)
            dpw_ref[g] = _dot_tn(pooled, dmix)
            dpool = _dot_nt(dmix, pw_ref[g])
            win = POOL_WINDOWS[g]
            acc = dpool / jnp.minimum(pos, float(win))
            dist = 1
            while dist < win:
                acc = acc + _shift_up(acc, dist, row, seq)
                dist *= 2
            dpv_ref[:, sl] = (acc - dpool).astype(BF16)

    tok = pl.BlockSpec((seq, width), lambda b: (b, 0))
    vec = pl.BlockSpec((1, width), lambda b: (0, 0))
    pvec = pl.BlockSpec((None, 1, width), lambda b: (b, 0, 0))
    return pl.pallas_call(
        body,
        out_shape=[jax.ShapeDtypeStruct((n_batch * seq, width), BF16),
                   jax.ShapeDtypeStruct((n_batch,) + pw.shape, F32),
                   jax.ShapeDtypeStruct((n_batch, 1, width), F32), jax.ShapeDtypeStruct((n_batch, 1, width), F32)],
        grid=(n_batch,),
        in_specs=[tok, tok, pl.BlockSpec(pw.shape, lambda b: (0, 0, 0)), vec, vec],
        out_specs=[tok, pl.BlockSpec((None,) + pw.shape, lambda b: (b, 0, 0, 0)), pvec, pvec],
        compiler_params=_params(), name="pool_bwd",
    )(pv, dyp, pw, ps, onp)


def _pick_lane(tile, idx):
    lane = lax.broadcasted_iota(jnp.int32, (1, LANES), 1)
    return jnp.sum(jnp.where(lane == idx, tile, 0.0), axis=1, keepdims=True)


def _pick_row(tile, idx):
    sub = lax.broadcasted_iota(jnp.int32, (tile.shape[0], 1), 0)
    return jnp.sum(jnp.where(sub == idx, tile, 0.0), axis=0, keepdims=True)


def _put_lane(col, idx):
    lane = lax.broadcasted_iota(jnp.int32, (1, LANES), 1)
    return jnp.where(lane == idx, col, 0.0)


def _head_select(e):
    lo = _head_masks()
    return lo if e == 0 else jnp.logical_not(lo)


def _causal(st, shift):
    row = lax.broadcasted_iota(jnp.int32, st.shape, 0)
    col = lax.broadcasted_iota(jnp.int32, st.shape, 1) + shift
    return jnp.where(col >= row, st, NEG)


def _transpose_blocks(a):
    rows, cols = a.shape
    return jnp.concatenate(
        [jnp.concatenate([a[r:r + LANES, c:c + LANES].T for r in range(0, rows, LANES)], axis=1)
         for c in range(0, cols, LANES)], axis=0)


def _accumulate(ref, value, first):
    @pl.when(first)
    def _():
        ref[...] = value

    @pl.when(jnp.logical_not(first))
    def _():
        ref[...] += value


def _attn_fwd(qa, ka, vb, n_batch, seq, plan=None):
    tq = min(ATT_BLOCK, seq)
    nq, nsub, tk = seq // tq, tq // ATT_SUB, tq
    pairs = vb.shape[1] // LANES

    def body(q_ref, k_ref, v_ref, o_ref, lse_ref, acc_ref):
        i, p = pl.program_id(1), pl.program_id(2)
        row_lo = lax.broadcasted_iota(jnp.int32, (LANES, 1), 0) < HEAD_DIM
        qs = [q_ref[:, e * LANES:(e + 1) * LANES] for e in range(2)]
        acc_ref[...] = jnp.zeros_like(acc_ref)

        def tile(off, stats, diagonal):
            vj = v_ref[pl.ds(off, tk), :]
            new, alphas, pvs = [], [], []
            for e in range(2):
                st = _dot_nt(k_ref[pl.ds(off, tk), e * LANES:(e + 1) * LANES], qs[e])
                if diagonal:
                    st = _causal(st, 0)
                m, l = stats[e]
                m_new = jnp.maximum(m, jnp.max(st, axis=0, keepdims=True))
                alpha = jnp.exp(m - m_new)
                pt = jnp.exp(st - m_new)
                new.append((m_new, alpha * l + jnp.sum(pt, axis=0, keepdims=True)))
                alphas.append(alpha)
                pvs.append(_dot_tn(jnp.where(_head_select(e), vj, jnp.zeros_like(vj)), pt.astype(BF16)))
            acc_ref[...] = acc_ref[...] * jnp.where(row_lo, alphas[0], alphas[1]) + (pvs[0] + pvs[1])
            return tuple(new)

        init = ((jnp.full((1, tq), NEG, F32), jnp.zeros((1, tq), F32)),) * 2
        stats = lax.fori_loop(0, i, lambda j, st: tile(pl.multiple_of(j * tk, tk), st, False), init)
        (m0, l0), (m1, l1) = tile(pl.multiple_of(i * tk, tk), stats, True)
        out_t = acc_ref[...] / jnp.where(row_lo, l0, l1)
        sub = lax.broadcasted_iota(jnp.int32, (8, 1), 0)
        lse0, lse1 = m0 + jnp.log(l0), m1 + jnp.log(l1)
        for a in range(nsub):
            sl = slice(a * ATT_SUB, (a + 1) * ATT_SUB)
            o_ref[sl, :] = out_t[:, sl].T
            rows = jnp.where(sub == 2 * p, lse0[:, sl], 0.0) + jnp.where(sub == 2 * p + 1, lse1[:, sl], 0.0)
            _accumulate(lse_ref.at[a], rows, p == 0)

    return _pallas(
        body, name="attn_fwd", args=[qa, ka, vb],
        out_shape=[jax.ShapeDtypeStruct((n_batch * seq, pairs * LANES), F32),
                   jax.ShapeDtypeStruct((n_batch * seq // ATT_SUB, 8, ATT_SUB), F32)],
        grid=(n_batch, nq, pairs),
        in_specs=[pl.BlockSpec((tq, 2 * LANES), lambda b, i, p: (b * nq + i, p)),
                  pl.BlockSpec((seq, 2 * LANES), lambda b, i, p: (b, p)),
                  pl.BlockSpec((seq, LANES), lambda b, i, p: (b, p))],
        out_specs=[pl.BlockSpec((tq, LANES), lambda b, i, p: (b * nq + i, p)),
                   pl.BlockSpec((nsub, 8, ATT_SUB), lambda b, i, p: (b * nq + i, 0, 0))],
        scratch_shapes=[pltpu.VMEM((LANES, tq), F32)], plan=plan)


def _attn_bwd(qa, ka, vb, do, lse, delta, n_batch, seq, plan=None):
    tq = min(ATT_BLOCK, seq)
    nq, nsub = seq // tq, tq // ATT_SUB
    n_tiles = seq // ATT_SUB
    pairs = vb.shape[1] // LANES

    def body(q_ref, k_ref, v_ref, do_ref, lse_ref, dl_ref, dq_ref, dk_ref, dv_ref, dfq_ref, dfk_ref,
             dq0_ref, dq1_ref, dk0_ref, dk1_ref, dva_ref):
        p = pl.program_id(1)
        dqs, dks = (dq0_ref, dq1_ref), (dk0_ref, dk1_ref)
        for acc in (dk0_ref, dk1_ref, dva_ref):
            acc[...] = jnp.zeros_like(acc)
        dfq_cols = []
        for i in range(nq):
            rows_i = slice(i * tq, (i + 1) * tq)
            qs = [q_ref[rows_i, e * LANES:(e + 1) * LANES] for e in range(2)]
            dov = do_ref[rows_i, :]
            does = [jnp.where(_head_select(e), dov, jnp.zeros_like(dov)) for e in range(2)]
            stat = lambda ref, e: jnp.concatenate([_pick_row(ref[i * nsub + a], 2 * p + e) for a in range(nsub)], axis=1)
            ls, dl = [stat(lse_ref, e) for e in range(2)], [stat(dl_ref, e) for e in range(2)]
            for acc in dqs:
                acc[...] = jnp.zeros_like(acc)

            def tile(off, diagonal, qs=qs, dov=dov, does=does, ls=ls, dl=dl):
                vj = v_ref[pl.ds(off, tq), :]
                for e in range(2):
                    kj = k_ref[pl.ds(off, tq), e * LANES:(e + 1) * LANES]
                    st = _dot_nt(kj, qs[e])
                    if diagonal:
                        st = _causal(st, 0)
                    pt = jnp.exp(st - ls[e])
                    dva_ref[pl.ds(off, tq), :] += _dot(pt.astype(BF16), does[e])
                    dpt = _dot_nt(jnp.where(_head_select(e), vj, jnp.zeros_like(vj)), dov)
                    dst = (pt * (dpt - dl[e])).astype(BF16)
                    dks[e][pl.ds(off, tq), :] += _dot(dst, qs[e])
                    dqs[e][...] += _dot(_transpose_blocks(kj), dst)

            def step(j, carry, tile=tile):
                tile(pl.multiple_of(j * tq, tq), False)
                return carry

            lax.fori_loop(0, i, step, 0)
            tile(i * tq, True)
            dq0, dq1 = _transpose_blocks(dq0_ref[...]), _transpose_blocks(dq1_ref[...])
            dq_ref[rows_i, :] = jnp.where(_head_masks(), dq0, dq1)
            dfq_cols.append(_put_lane(_pick_lane(dq0, _aug_lane(0)), 2 * p) + _put_lane(_pick_lane(dq1, _aug_lane(1)), 2 * p + 1))
        dk0, dk1 = dk0_ref[...], dk1_ref[...]
        dk_ref[...] = jnp.where(_head_masks(), dk0, dk1)
        dv_ref[...] = dva_ref[...].astype(BF16)
        dfk = _put_lane(_pick_lane(dk0, _aug_lane(0) + 3), 2 * p) + _put_lane(_pick_lane(dk1, _aug_lane(1) + 3), 2 * p + 1)
        _accumulate(dfq_ref, jnp.concatenate(dfq_cols, axis=0), p == 0)
        _accumulate(dfk_ref, -dfk, p == 0)

    wide = pl.BlockSpec((seq, 2 * LANES), lambda b, p: (b, p))
    blk = pl.BlockSpec((seq, LANES), lambda b, p: (b, p))
    col = pl.BlockSpec((seq, LANES), lambda b, p: (b, 0))
    stat = pl.BlockSpec((n_tiles, 8, ATT_SUB), lambda b, p: (b, 0, 0))
    f32_blk, acc = jax.ShapeDtypeStruct((n_batch * seq, pairs * LANES), F32), pltpu.VMEM((seq, LANES), F32)
    return _pallas(
        body, name="attn_bwd", args=[qa, ka, vb, do, lse, delta],
        out_shape=[f32_blk, f32_blk, jax.ShapeDtypeStruct((n_batch * seq, pairs * LANES), BF16),
                   jax.ShapeDtypeStruct((n_batch * seq, LANES), F32), jax.ShapeDtypeStruct((n_batch * seq, LANES), F32)],
        grid=(n_batch, pairs), in_specs=[wide, wide, blk, blk, stat, stat], out_specs=[blk, blk, blk, col, col],
        scratch_shapes=[pltpu.VMEM((LANES, tq), F32), pltpu.VMEM((LANES, tq), F32), acc, acc, acc], plan=plan)


def _forget_bwd(dfq, dfk, f, bias, n_batch, seq):
    def body(dfq_ref, dfk_ref, f_ref, b_ref, df_ref, db_ref):
        acc = dfq_ref[...] + dfk_ref[...]
        row = lax.broadcasted_iota(jnp.int32, (seq, 1), 0)
        dist = 1
        while dist < seq:
            acc = acc + _shift_up(acc, dist, row, seq)
            dist *= 2
        df = acc * _sigmoid(-(f_ref[...] + b_ref[...]))
        df_ref[...] = df
        db_ref[...] = jnp.sum(df, axis=0, keepdims=True)

    col = pl.BlockSpec((seq, LANES), lambda b: (b, 0))
    return pl.pallas_call(
        body,
        out_shape=[jax.ShapeDtypeStruct((n_batch * seq, LANES), F32), jax.ShapeDtypeStruct((n_batch, 1, LANES), F32)],
        grid=(n_batch,), in_specs=[col, col, col, pl.BlockSpec((1, LANES), lambda b: (0, 0))],
        out_specs=[col, pl.BlockSpec((None, 1, LANES), lambda b: (b, 0, 0))],
        compiler_params=_params(), name="forget_bwd",
    )(dfq, dfk, f, bias)


def _mix_out(x1, yp, o, ona, woa, wob):
    t, d = x1.shape
    width = o.shape[1]
    tm = min(512, t)

    def body(x_ref, yp_ref, o_ref, on_ref, wa_ref, wb_ref, x2_ref, ya_ref):
        of = o_ref[...]
        ya = ((of * _rms(of)) * on_ref[...]).astype(BF16)
        ya_ref[...] = ya
        x2_ref[...] = x_ref[...] + (_dot(yp_ref[...], wa_ref[...]) + _dot(ya, wb_ref[...]))

    row = pl.BlockSpec((tm, d), lambda i: (i, 0))
    half = pl.BlockSpec((tm, width), lambda i: (i, 0))
    wspec = pl.BlockSpec((width, d), lambda i: (0, 0))
    return pl.pallas_call(
        body, out_shape=[jax.ShapeDtypeStruct((t, d), F32), jax.ShapeDtypeStruct((t, width), BF16)],
        grid=(t // tm,), in_specs=[row, half, half, pl.BlockSpec((1, width), lambda i: (0, 0)), wspec, wspec],
        out_specs=[row, half], compiler_params=_params(), name="mix_out",
    )(x1, yp, o, ona, woa, wob)


def _mix_out_bwd(dx2, o, yp, ya, ona, woa, wob, plan=None):
    t, d = dx2.shape
    width = o.shape[1]
    tm = min(512, t)
    nt = t // tm

    def body(dx_ref, o_ref, yp_ref, ya_ref, on_ref, wa_ref, wb_ref, dyp_ref, do_ref, dl_ref, dwa_ref, dwb_ref, don_ref):
        @pl.when(pl.program_id(0) == 0)
        def _():
            dwa_ref[...] = jnp.zeros_like(dwa_ref)
            dwb_ref[...] = jnp.zeros_like(dwb_ref)

        dxb = dx_ref[...].astype(BF16)
        dwa_ref[...] += _dot_tn(yp_ref[...], dxb)
        dwb_ref[...] += _dot_tn(ya_ref[...], dxb)
        dyp_ref[...] = _dot_nt(dxb, wa_ref[...])
        of = o_ref[...]
        dov, dgr = _rms_bwd(of, _rms(of), on_ref[...], _dot_nt(dxb, wb_ref[...]))
        don_ref[...] = jnp.sum(dgr, axis=0, keepdims=True)
        do_ref[...] = dov.astype(BF16)
        lo = _head_masks()
        prod = dov * of
        delta = jnp.zeros((tm, LANES), F32)
        for blk in range(width // LANES):
            pb = prod[:, blk * LANES:(blk + 1) * LANES]
            delta = delta + _put_lane(jnp.sum(jnp.where(lo, pb, 0.0), axis=1, keepdims=True), 2 * blk)
            delta = delta + _put_lane(jnp.sum(jnp.where(lo, 0.0, pb), axis=1, keepdims=True), 2 * blk + 1)
        for c in range(tm // ATT_SUB):
            dl_ref[c] = delta[c * ATT_SUB:(c + 1) * ATT_SUB, :].T[0:8, :]

    row = pl.BlockSpec((tm, d), lambda i: (i, 0))
    half = pl.BlockSpec((tm, width), lambda i: (i, 0))
    wspec = pl.BlockSpec((width, d), lambda i: (0, 0))
    return _pallas(
        body, name="mix_out_bwd", args=[dx2, o, yp, ya, ona, woa, wob],
        out_shape=[jax.ShapeDtypeStruct((t, width), F32), jax.ShapeDtypeStruct((t, width), BF16),
                   jax.ShapeDtypeStruct((t // ATT_SUB, 8, ATT_SUB), F32), jax.ShapeDtypeStruct((width, d), F32),
                   jax.ShapeDtypeStruct((width, d), F32), jax.ShapeDtypeStruct((nt, 1, width), F32)],
        grid=(nt,),
        in_specs=[row, half, half, half, pl.BlockSpec((1, width), lambda i: (0, 0)), wspec, wspec],
        out_specs=[half, half, pl.BlockSpec((tm // ATT_SUB, 8, ATT_SUB), lambda i: (i, 0, 0)), wspec, wspec,
                   pl.BlockSpec((None, 1, width), lambda i: (i, 0, 0))], plan=plan)


def _mix_in_bwd(dx2, x1, gain, hm, dpv, dqh, q, dkh, k, dv, df, qn, kn, wt):
    t, d = x1.shape
    width = q.shape[1]
    pool_width = dpv.shape[1]
    tm = min(512, t)
    nt = t // tm
    scale = HEAD_DIM ** -0.5
    c_q, c_k, c_v = pool_width, pool_width + width, pool_width + 2 * width
    c_f = c_v + width

    def body(dx2_ref, x_ref, g_ref, hm_ref, dpv_ref, dqh_ref, q_ref, dkh_ref, k_ref, dv_ref, df_ref, qn_ref, kn_ref,
             wt_ref, dx_ref, dxh_ref, dwt_ref, dg_ref, dqn_ref, dkn_ref):
        @pl.when(pl.program_id(0) == 0)
        def _():
            dwt_ref[...] = jnp.zeros_like(dwt_ref)

        lo = _head_masks()
        for part, rows in enumerate(_row_halves(tm)):
            def put(ref, sl, value):
                ref[:, sl] = value if part == 0 else ref[:, sl] + value

            hm = hm_ref[rows, :]
            pieces = [(0, dpv_ref[rows, :])]
            for c0, raw_ref, dh_ref, n_ref, dn_ref, mul in ((c_q, q_ref, dqh_ref, qn_ref, dqn_ref, scale),
                                                           (c_k, k_ref, dkh_ref, kn_ref, dkn_ref, 1.0)):
                cols = []
                for blk in range(width // LANES):
                    sl = slice(blk * LANES, (blk + 1) * LANES)
                    xb = raw_ref[rows, sl]
                    gb = dh_ref[rows, sl] * mul
                    r = _head_rms(xb, lo)
                    xh = xb * r
                    dyg = gb * n_ref[:, sl]
                    cols.append((r * (dyg - xh * _head_mean(dyg * xh, lo))).astype(BF16))
                    put(dn_ref, sl, jnp.sum(gb * xh, axis=0, keepdims=True))
                pieces.append((c0, jnp.concatenate(cols, axis=1)))
            pieces.append((c_v, dv_ref[rows, :]))
            pieces.append((c_f, df_ref[rows, :].astype(BF16)))
            dhm = jnp.zeros((tm // 2, d), F32)
            for c0, piece in pieces:
                dwt_ref[c0:c0 + piece.shape[1], :] += _dot_tn(piece, hm)
                dhm = dhm + _dot(piece, wt_ref[c0:c0 + piece.shape[1], :])
            xf = x_ref[rows, :]
            dxn, dgr = _rms_bwd(xf, _rms(xf), g_ref[...], dhm)
            dx = dx2_ref[rows, :] + dxn
            dx_ref[rows, :] = dx
            dxh_ref[rows, :] = (0.5 * dx).astype(BF16)
            put(dg_ref, slice(None), jnp.sum(dgr, axis=0, keepdims=True))

    row = pl.BlockSpec((tm, d), lambda i: (i, 0))
    half = pl.BlockSpec((tm, width), lambda i: (i, 0))
    const = lambda shape: pl.BlockSpec(shape, lambda i: (0, 0))
    pvec = lambda n: pl.BlockSpec((None, 1, n), lambda i: (i, 0, 0))
    return pl.pallas_call(
        body,
        out_shape=[jax.ShapeDtypeStruct((t, d), F32), jax.ShapeDtypeStruct((t, d), BF16), jax.ShapeDtypeStruct(wt.shape, F32),
                   jax.ShapeDtypeStruct((nt, 1, d), F32),
                   jax.ShapeDtypeStruct((nt, 1, width), F32), jax.ShapeDtypeStruct((nt, 1, width), F32)],
        grid=(nt,),
        in_specs=[row, row, const((1, d)), row, pl.BlockSpec((tm, pool_width), lambda i: (i, 0)), half, half, half, half,
                  half, pl.BlockSpec((tm, LANES), lambda i: (i, 0)), const((1, width)), const((1, width)),
                  const(wt.shape)],
        out_specs=[row, row, const(wt.shape), pvec(d), pvec(width), pvec(width)],
        compiler_params=_params(), name="mix_in_bwd",
    )(dx2, x1, gain, hm, dpv, dqh, q, dkh, k, dv, df, qn, kn, wt)


def _mesh_pos():
    return lax.axis_index("x"), lax.axis_index("y"), lax.axis_index("c")


def _other_chips(x, y):
    return [(1 - x, y), (x, 1 - y), (1 - x, 1 - y)]


def _remote(src, dst, send_sem, recv_sem, device):
    return pltpu.make_async_remote_copy(src_ref=src, dst_ref=dst, send_sem=send_sem, recv_sem=recv_sem,
                                        device_id=device, device_id_type=pl.DeviceIdType.MESH)


def _half_rows(n_rows, which):
    half = n_rows // 2
    return pl.ds(pl.multiple_of(which * half, 8), half)


def _row_block(rows, cols, itemsize=4):
    rb = rows
    while rb * cols * itemsize > PLACE_STEP_BYTES and rb % 32 == 0:
        rb //= 2
    return rb


def _place_cast(ws, chip, tag):
    n = len(ws)
    rows, cols = ws[0].shape
    rb = _row_block(rows, cols)

    def body(k_ref, *refs):
        for w_ref, o_ref in zip(refs[:n], refs[n:]):
            o_ref[...] = w_ref[...].astype(BF16)

    return pl.pallas_call(
        body, out_shape=[jax.ShapeDtypeStruct((N_CHIPS, rows, cols), BF16)] * n,
        grid_spec=pltpu.PrefetchScalarGridSpec(
            num_scalar_prefetch=1, grid=(rows // rb,),
            in_specs=[pl.BlockSpec((rb, cols), lambda i, k: (i, 0))] * n,
            out_specs=[pl.BlockSpec((None, rb, cols), lambda i, k: (k[0], i, 0))] * n),
        compiler_params=_params(), name="place_" + tag,
    )(chip, *ws)


class _Plan:
    def __init__(self, ins, outs, alias, sems, start, finish, middle=None, middle_at=(3, 4)):
        self.ins, self.outs, self.alias, self.sems = ins, outs, alias, sems
        self.start, self.middle, self.finish, self.middle_at = start, middle, finish, middle_at


def _merge_plans(a, b):
    ni, no, ns = len(a.ins), len(a.outs), len(a.sems)
    alias = dict(a.alias)
    alias.update({ni + i: no + o for i, o in b.alias.items()})

    def both(which):
        stage_a, stage_b = getattr(a, which), getattr(b, which)
        if stage_a is None and stage_b is None:
            return None

        def run(ins, outs, sems):
            if stage_a is not None:
                stage_a(ins[:ni], outs[:no], sems[:ns])
            if stage_b is not None:
                stage_b(ins[ni:], outs[no:], sems[ns:])
        return run

    return _Plan(list(a.ins) + list(b.ins), list(a.outs) + list(b.outs), alias, list(a.sems) + list(b.sems),
                 both("start"), both("finish"), both("middle"), a.middle_at if a.middle is not None else b.middle_at)


def _run_plan(plan, name):
    n_in, n_out = len(plan.ins), len(plan.outs)

    def body(*refs):
        parts = refs[:n_in], refs[n_in:n_in + n_out], refs[n_in + n_out:]
        plan.start(*parts)
        if plan.middle is not None:
            plan.middle(*parts)
        plan.finish(*parts)

    return pl.pallas_call(
        body, out_shape=plan.outs, in_specs=[ANY] * n_in, out_specs=[ANY] * n_out, scratch_shapes=plan.sems,
        input_output_aliases=plan.alias, name=name,
    )(*plan.ins)


def _pallas(body, *, name, args, in_specs, out_shape, out_specs, grid, scratch_shapes=(), plan=None, aliases=None):
    n_in, n_out, n_scr = len(args), len(out_shape), len(scratch_shapes)
    plan = plan or _Plan([], [], {}, [], None, None)
    p_in, p_out = len(plan.ins), len(plan.outs)

    def carrying(*refs):
        ins, p_ins = refs[:n_in], refs[n_in:n_in + p_in]
        o0 = n_in + p_in
        outs, p_outs = refs[o0:o0 + n_out], refs[o0 + n_out:o0 + n_out + p_out]
        s0 = o0 + n_out + p_out
        scr, p_sems = refs[s0:s0 + n_scr], refs[s0 + n_scr:]
        ids = [pl.program_id(a) for a in range(len(grid))]

        if plan.start is not None:
            @pl.when(functools.reduce(jnp.logical_and, [i == 0 for i in ids]))
            def _():
                plan.start(p_ins, p_outs, p_sems)

        body(*ins, *outs, *scr)

        if plan.middle is not None:
            step, n_steps = 0, 1
            for i, g in zip(ids, grid):
                step, n_steps = step * g + i, n_steps * g

            @pl.when(step == (plan.middle_at[0] * n_steps) // plan.middle_at[1])
            def _():
                plan.middle(p_ins, p_outs, p_sems)

        if plan.finish is not None:
            @pl.when(functools.reduce(jnp.logical_and, [i == g - 1 for i, g in zip(ids, grid)]))
            def _():
                plan.finish(p_ins, p_outs, p_sems)

    aliases = dict(aliases or {})
    aliases.update({n_in + i: n_out + o for i, o in plan.alias.items()})
    res = pl.pallas_call(
        carrying, out_shape=list(out_shape) + list(plan.outs), grid=grid,
        in_specs=list(in_specs) + [ANY] * p_in, out_specs=list(out_specs) + [ANY] * p_out,
        scratch_shapes=list(scratch_shapes) + list(plan.sems),
        input_output_aliases=aliases, compiler_params=_params(), name=name,
    )(*args, *plan.ins)
    return list(res[:n_out]), list(res[n_out:])


def _plan_gather(stacks):
    n = len(stacks)
    relations = range(3)

    def ici_copies(outs, sems):
        x, y, c = _mesh_pos()
        chips = _other_chips(x, y)
        cps = []
        for w in range(n):
            own = outs[w].at[2 * x + y, _half_rows(stacks[w].shape[1], c)]
            cps += [_remote(own, own, sems[0].at[w, j], sems[1].at[w, j], (*chips[j], c)) for j in relations]
        return cps

    def start(ins, outs, sems):
        for cp in ici_copies(outs, sems):
            cp.start()

    def forwards(outs, sems, core):
        x, y, c = _mesh_pos()
        slots = [2 * cx + cy for cx, cy in _other_chips(x, y)]
        cps = []
        for w in range(n):
            rows = _half_rows(stacks[w].shape[1], core)
            for j in relations:
                landed = outs[w].at[slots[j], rows]
                cps.append((_remote(landed, landed, sems[0].at[w, j], sems[1].at[w, j], (x, y, 1 - c)),
                            _remote(landed, landed, sems[2].at[w, j], sems[3].at[w, j], (x, y, 1 - c))))
        return cps

    def middle(ins, outs, sems):
        c = _mesh_pos()[2]
        for arrival, forward in forwards(outs, sems, c):
            arrival.wait_recv()
            forward.start()

    def finish(ins, outs, sems):
        c = _mesh_pos()[2]
        for _, forward in forwards(outs, sems, 1 - c):
            forward.wait_recv()
        for cp in ici_copies(outs, sems) + [forward for _, forward in forwards(outs, sems, c)]:
            cp.wait_send()

    return _Plan(stacks, [jax.ShapeDtypeStruct(s.shape, s.dtype) for s in stacks], {w: w for w in range(n)},
                 [pltpu.SemaphoreType.DMA((n, 3))] * 4, start, finish, middle)


RELAY_SEMS = [pltpu.SemaphoreType.DMA((3, 2))] * 4 + [pltpu.SemaphoreType.DMA((3, 3))] * 2
RELAY_STAGES = ("send", "pass on", "x neighbour", "y neighbour", "diagonal", "end")


def _relay_gather_stage(stage, outs, sems):
    assert stage in RELAY_STAGES
    send, recv, relay_send, relay_recv, d2d_send, d2d_recv = sems
    n = len(outs)
    rh = outs[0].shape[1] // 2
    mx, my, c = _mesh_pos()
    sibling = (mx, my, 1 - c)
    near = [(1 - mx, my), (mx, 1 - my)]
    slots = [2 * cx + cy for cx, cy in near] + [2 * (1 - mx) + (1 - my)]

    def piece(w, slot, core, quarter=None):
        if quarter is None:
            return outs[w].at[slot, _half_rows(2 * rh, core)]
        return outs[w].at[slot, pl.ds(pl.multiple_of(core * rh + quarter * (rh // 2), 8), rh // 2)]

    def to_near(w, j):
        own = piece(w, 2 * mx + my, c)
        return _remote(own, own, send.at[w, j], recv.at[w, j], (*near[j], c))

    def from_near(w, j):
        landed = piece(w, slots[j], c)
        return _remote(landed, landed, send.at[w, j], recv.at[w, j], sibling)

    def onward(w, j, slot):
        part = piece(w, slot, c, quarter=j)
        return _remote(part, part, relay_send.at[w, j], relay_recv.at[w, j], (*near[1 - j], c))

    def to_sibling(w, j, core):
        landed = piece(w, slots[j], core)
        return _remote(landed, landed, d2d_send.at[w, j], d2d_recv.at[w, j], sibling)

    if stage == "send":
        for w in range(n):
            for j in range(2):
                to_near(w, j).start()
    elif stage == "pass on":
        for w in range(n):
            for j in range(2):
                from_near(w, j).wait_recv()
                onward(w, j, slots[j]).start()
                to_sibling(w, j, c).start()
    elif stage in ("x neighbour", "y neighbour"):
        for w in range(n):
            to_sibling(w, ("x neighbour", "y neighbour").index(stage), 1 - c).wait_recv()
    elif stage == "diagonal":
        for w in range(n):
            for j in range(2):
                onward(w, j, slots[2]).wait_recv()
            to_sibling(w, 2, c).start()
        for w in range(n):
            to_sibling(w, 2, 1 - c).wait_recv()
    else:
        for w in range(n):
            for j in range(2):
                to_near(w, j).wait_send()
                onward(w, j, slots[j]).wait_send()
            for j in range(3):
                to_sibling(w, j, c).wait_send()


def _plan_gather_relay(stacks):
    def stages(which):
        def run(ins, outs, sems):
            for stage in which:
                _relay_gather_stage(stage, outs, sems)
        return run

    return _Plan(stacks, [jax.ShapeDtypeStruct(s.shape, s.dtype) for s in stacks], {w: w for w in range(len(stacks))},
                 RELAY_SEMS, stages(RELAY_STAGES[:1]), stages(RELAY_STAGES[2:]), stages(RELAY_STAGES[1:2]), middle_at=(5, 8))


def _ffn_fwd_gathering(x, gain, stacks, order, later):
    t, d = x.shape
    nch, fc, _ = stacks[0].shape
    assert nch == N_CHIPS
    n = len(stacks)
    tm = min(FFN_STAGED_TILE, t)
    nt = t // tm
    p_in, p_out = len(later.ins), len(later.outs)
    relay = _relay_gather_stage

    def body(order_ref, x_ref, g_ref, *refs):
        later_in, refs = refs[n:n + p_in], refs[n + p_in:]
        o_ref, h_ref, a_ref, b_ref, s_ref = refs[:5]
        stack_refs, later_out, refs = refs[5:5 + n], refs[5 + n:5 + n + p_out], refs[5 + n + p_out:]
        w_ref, hs_ref, acc_ref, w_sem = refs[:4]
        relay_sems, later_sems = refs[4:10], refs[10:]
        k, i = pl.program_id(0), pl.program_id(1)
        tile = pl.ds(pl.multiple_of(i * tm, tm), tm)

        @pl.when(i == 0)
        def _():
            for chunk, stages in enumerate([("send",), ("pass on", "x neighbour"), ("y neighbour",), ("diagonal",)]):
                @pl.when(k == chunk)
                def _():
                    for stage in stages:
                        relay(stage, stack_refs, relay_sems)
                    if chunk == 1 and later.start is not None:
                        later.start(later_in, later_out, later_sems)
            loads = [pltpu.make_async_copy(stack_refs[w].at[order_ref[k]], w_ref.at[w], w_sem.at[w]) for w in range(n)]
            for cp in loads:
                cp.start()
            for cp in loads:
                cp.wait()

        @pl.when(k == 0)
        def _():
            xf = x_ref[...]
            hb = ((xf * _rms(xf)) * g_ref[...]).astype(BF16)
            h_ref[...] = hb
            hs_ref[tile, :] = hb
            acc_ref[tile, :] = jnp.zeros((tm, d), F32)

        for rows in _row_halves(tm):
            part = pl.ds(pl.multiple_of(i * tm + rows.start, tm // 2), tm // 2)
            h = hs_ref[part, :]
            a = _dot_nt(h, w_ref[0])
            b = _dot_nt(h, w_ref[1])
            sb = ((a * (0.5 * jnp.tanh(0.5 * a) + 0.5)) * b).astype(BF16)
            a_ref[rows, :] = a.astype(BF16)
            b_ref[rows, :] = b.astype(BF16)
            s_ref[rows, :] = sb
            acc_ref[part, :] += _dot(sb, w_ref[2])

        @pl.when(k == nch - 1)
        def _():
            o_ref[...] = x_ref[...] + 0.5 * acc_ref[tile, :]

        if later.middle is not None:
            @pl.when((k == nch - 1) & (i == nt // 2))
            def _():
                later.middle(later_in, later_out, later_sems)

        @pl.when((k == nch - 1) & (i == nt - 1))
        def _():
            relay("end", stack_refs, relay_sems)
            if later.finish is not None:
                later.finish(later_in, later_out, later_sems)

    ends = lambda k, i: jnp.where((k == 0) | (k == nch - 1), i, 0)
    act = pl.BlockSpec((None, tm, fc), lambda k, i, order: (order[k], i, 0))
    out_shape = [jax.ShapeDtypeStruct((t, d), F32), jax.ShapeDtypeStruct((t, d), BF16)]
    out_shape += [jax.ShapeDtypeStruct((nch, t, fc), BF16)] * 3
    out_shape += [jax.ShapeDtypeStruct(s.shape, s.dtype) for s in stacks] + list(later.outs)
    aliases = {3 + w: 5 + w for w in range(n)}
    aliases.update({3 + n + i: 5 + n + o for i, o in later.alias.items()})
    res = pl.pallas_call(
        body, out_shape=out_shape,
        grid_spec=pltpu.PrefetchScalarGridSpec(
            num_scalar_prefetch=1, grid=(nch, nt),
            in_specs=[pl.BlockSpec((tm, d), lambda k, i, order: (ends(k, i), 0)),
                      pl.BlockSpec((1, d), lambda k, i, order: (0, 0))] + [ANY] * (n + p_in),
            out_specs=[pl.BlockSpec((tm, d), lambda k, i, order: (jnp.where(k == nch - 1, i, 0), 0)),
                       pl.BlockSpec((tm, d), lambda k, i, order: (jnp.where(k == 0, i, nt - 1), 0)),
                       act, act, act] + [ANY] * (n + p_out),
            scratch_shapes=[pltpu.VMEM((n, fc, d), BF16), pltpu.VMEM((t, d), BF16), pltpu.VMEM((t, d), F32),
                            pltpu.SemaphoreType.DMA((n,))] + RELAY_SEMS + list(later.sems)),
        input_output_aliases=aliases, compiler_params=_params(), name="ffn_fwd",
    )(order, x, gain, *stacks, *later.ins)
    return list(res[:5]), list(res[5:5 + n]), list(res[5 + n:])


def _plan_sibling_halves(gs):
    n = len(gs)

    def copies(ins, outs, sems):
        x, y, c = _mesh_pos()
        return [_remote(ins[w].at[:, _half_rows(gs[w].shape[1], 1 - c), :], outs[w], sems[0].at[w], sems[1].at[w],
                        (x, y, 1 - c)) for w in range(n)]

    def start(ins, outs, sems):
        for cp in copies(ins, outs, sems):
            cp.start()

    def finish(ins, outs, sems):
        for cp in copies(ins, outs, sems):
            cp.wait()

    return _Plan(gs, [jax.ShapeDtypeStruct((g.shape[0], g.shape[1] // 2, g.shape[2]), g.dtype) for g in gs], {},
                 [pltpu.SemaphoreType.DMA((n,))] * 2, start, finish)


def _plan_chip_exchange(ps):
    n = len(ps)

    def copies(ins, outs, sems):
        x, y, c = _mesh_pos()
        return [_remote(ins[w].at[2 * cx + cy], outs[w].at[j], sems[0].at[w, j], sems[1].at[w, j], (cx, cy, c))
                for w in range(n) for j, (cx, cy) in enumerate(_other_chips(x, y))]

    def start(ins, outs, sems):
        for cp in copies(ins, outs, sems):
            cp.start()

    def finish(ins, outs, sems):
        for cp in copies(ins, outs, sems):
            cp.wait()

    return _Plan(ps, [jax.ShapeDtypeStruct((3,) + p.shape[1:], p.dtype) for p in ps], {},
                 [pltpu.SemaphoreType.DMA((n, 3))] * 2, start, finish)


def _plan_sibling_share(gs):
    n = len(gs)

    def copies(outs, sems, which):
        x, y, c = _mesh_pos()
        cps = []
        for w in range(n):
            rows = outs[w].at[_half_rows(gs[w].shape[0], c if which == "mine" else 1 - c)]
            cps.append(_remote(rows, rows, sems[0].at[w], sems[1].at[w], (x, y, 1 - c)))
        return cps

    def start(ins, outs, sems):
        for cp in copies(outs, sems, "mine"):
            cp.start()

    def finish(ins, outs, sems):
        for cp in copies(outs, sems, "mine"):
            cp.wait_send()
        for cp in copies(outs, sems, "theirs"):
            cp.wait_recv()

    return _Plan(gs, [jax.ShapeDtypeStruct(g.shape, g.dtype) for g in gs], {w: w for w in range(n)},
                 [pltpu.SemaphoreType.DMA((n,))] * 2, start, finish)


def _same_shape_groups(arrays):
    groups = {}
    for i, a in enumerate(arrays):
        groups.setdefault(a.shape, []).append(i)
    return list(groups.values())


def _add_sibling(gs, r1s, ids, tag):
    n = len(gs)
    nch, rh, cols = r1s[0].shape

    def body(ids_ref, *refs):
        for g_ref, r_ref, o_ref in zip(refs[:n], refs[n:2 * n], refs[2 * n:]):
            o_ref[...] = (g_ref[...] + r_ref[...]).astype(BF16)

    blk = lambda fn: pl.BlockSpec((None, rh, cols), fn)
    return pl.pallas_call(
        body, out_shape=[jax.ShapeDtypeStruct(r1s[0].shape, BF16)] * n,
        grid_spec=pltpu.PrefetchScalarGridSpec(
            num_scalar_prefetch=1, grid=(nch,),
            in_specs=[blk(lambda k, ids: (k, ids[1], 0))] * n + [blk(lambda k, ids: (k, 0, 0))] * n,
            out_specs=[blk(lambda k, ids: (k, 0, 0))] * n),
        compiler_params=_params(), name="add_sibling_" + tag,
    )(ids, *gs, *r1s)


def _add_chips(gs, r1s, r2s, ids, tag):
    n = len(gs)
    _, rh, cols = r1s[0].shape
    nb = 2 if rh % 32 == 0 else 1
    rb = rh // nb

    def body(ids_ref, *refs):
        for g_ref, r1_ref, r2_ref, o_ref in zip(refs[:n], refs[n:2 * n], refs[2 * n:3 * n], refs[3 * n:]):
            own = g_ref[...] + r1_ref[...]
            o_ref[...] = ((own + r2_ref[0].astype(F32)) + r2_ref[1].astype(F32)) + r2_ref[2].astype(F32)

    return pl.pallas_call(
        body, out_shape=[jax.ShapeDtypeStruct((2 * rh, cols), F32)] * n,
        grid_spec=pltpu.PrefetchScalarGridSpec(
            num_scalar_prefetch=1, grid=(nb,),
            in_specs=[pl.BlockSpec((None, rb, cols), lambda i, ids: (ids[0], ids[1] * nb + i, 0))] * n
            + [pl.BlockSpec((None, rb, cols), lambda i, ids: (ids[0], i, 0))] * n
            + [pl.BlockSpec((3, rb, cols), lambda i, ids: (0, i, 0))] * n,
            out_specs=[pl.BlockSpec((rb, cols), lambda i, ids: (ids[1] * nb + i, 0))] * n),
        compiler_params=_params(), name="add_chips_" + tag,
    )(ids, *gs, *r1s, *r2s)


VEC_ROWS = 8


N_DEVICES = 8


def _small_pack(part, d, width):
    names = ("ffn1_norm", "mix_norm", "ffn2_norm", "pool_scale", "out_norm_pool", "out_norm_attn", "qn", "kn", "b_forget",
             "loss")
    args = [part[k] for k in names]

    def body(g1_ref, gm_ref, g2_ref, ps_ref, onp_ref, ona_ref, qn_ref, kn_ref, bf_ref, loss_ref, vbuf):
        lo = _head_masks()

        def fold_heads(ref):
            v = jnp.sum(ref[...], axis=0)
            acc = jnp.zeros((VEC_ROWS, LANES), F32)
            for blk in range(width // LANES):
                vb = jnp.broadcast_to(v[:, blk * LANES:(blk + 1) * LANES], (VEC_ROWS, LANES))
                acc = acc + vb + pltpu.roll(vb, HEAD_DIM, 1)
            return jnp.where(lo, acc, 0.0)[0:1, :]

        vbuf[0] = jnp.zeros((VEC_ROWS, d), F32)
        vbuf[0, 0:1, :] = jnp.sum(g1_ref[...], axis=0)
        vbuf[0, 1:2, :] = jnp.sum(gm_ref[...], axis=0)
        vbuf[0, 2:3, :] = jnp.sum(g2_ref[...], axis=0)
        vbuf[0, 5:6, 0:LANES] = jnp.sum(loss_ref[...], axis=0)[0:1, :]
        vbuf[0, 3:4, 0:width] = jnp.sum(ps_ref[...], axis=0)
        vbuf[0, 3:4, width:2 * width] = jnp.sum(onp_ref[...], axis=0)
        vbuf[0, 4:5, 0:width] = jnp.sum(ona_ref[...], axis=0)
        vbuf[0, 4:5, width:width + LANES] = fold_heads(qn_ref)
        vbuf[0, 4:5, width + LANES:width + 2 * LANES] = fold_heads(kn_ref)
        vbuf[0, 4:5, width + 2 * LANES:width + 3 * LANES] = jnp.sum(bf_ref[...], axis=0)

    return pl.pallas_call(
        body, out_shape=jax.ShapeDtypeStruct((N_DEVICES, VEC_ROWS, d), F32),
        in_specs=[VM] * len(args), out_specs=VM, compiler_params=_params(), name="small_pack",
    )(*args)


def _pool_pack(dpw):
    def body(pw_ref, pbuf):
        pbuf[0] = jnp.sum(pw_ref[...], axis=0)

    return pl.pallas_call(
        body, out_shape=jax.ShapeDtypeStruct((N_DEVICES,) + dpw.shape[1:], F32),
        in_specs=[VM], out_specs=VM, compiler_params=_params(), name="pool_pack",
    )(dpw)


def _plan_all_to_all(stacks):
    n = len(stacks)

    def copies(outs, sems):
        x, y, c = _mesh_pos()
        cps = []
        for r in range(1, N_DEVICES):
            peer = (x if not r & 4 else 1 - x, y if not r & 2 else 1 - y, c if not r & 1 else 1 - c)
            cps += [_remote(outs[w].at[0], outs[w].at[r], sems[0].at[w, r - 1], sems[1].at[w, r - 1], peer) for w in range(n)]
        return cps

    def start(ins, outs, sems):
        for cp in copies(outs, sems):
            cp.start()

    def finish(ins, outs, sems):
        for cp in copies(outs, sems):
            cp.wait()

    return _Plan(stacks, [jax.ShapeDtypeStruct(s.shape, s.dtype) for s in stacks], {w: w for w in range(n)},
                 [pltpu.SemaphoreType.DMA((n, N_DEVICES - 1))] * 2, start, finish)


def _small_sum(vstack, pstack, me):
    def body(me_ref, vbuf, pbuf, vec_ref, pw_ref):
        vec = vbuf[me_ref[0]]
        pw = pbuf[me_ref[0]]
        for dev in range(1, N_DEVICES):
            vec = vec + vbuf[jnp.bitwise_xor(me_ref[0], dev)]
            pw = pw + pbuf[jnp.bitwise_xor(me_ref[0], dev)]
        vec_ref[...] = vec
        pw_ref[...] = pw

    full = lambda s: pl.BlockSpec(s.shape, lambda i, me: (0,) * len(s.shape))
    outs = [jax.ShapeDtypeStruct(vstack.shape[1:], F32), jax.ShapeDtypeStruct(pstack.shape[1:], F32)]
    return pl.pallas_call(
        body, out_shape=outs,
        grid_spec=pltpu.PrefetchScalarGridSpec(num_scalar_prefetch=1, grid=(1,), in_specs=[full(vstack), full(pstack)],
                                               out_specs=[full(o) for o in outs]),
        compiler_params=_params(), name="small_sum",
    )(me, vstack, pstack)


def _adamw(ws, gs, ms, vs, tag):
    n = len(ws)
    rows, cols = ws[0].shape
    rb = rows
    while rb * cols * 4 * n > ADAMW_STEP_BYTES and rb % 16 == 0:
        rb //= 2

    def body(*refs):
        for j in range(n):
            w_ref, g_ref, m_ref, v_ref = (refs[k * n + j] for k in range(4))
            go_ref, d_ref, mo_ref, vo_ref = (refs[(4 + k) * n + j] for k in range(4))
            gv = g_ref[...]
            go_ref[...] = gv
            m2 = ADAM_B1 * m_ref[...] + (1.0 - ADAM_B1) * gv
            v2 = ADAM_B2 * v_ref[...] + (1.0 - ADAM_B2) * (gv * gv)
            m_hat = m2 / (1.0 - ADAM_B1 ** ADAM_STEP)
            v_hat = v2 / (1.0 - ADAM_B2 ** ADAM_STEP)
            d_ref[...] = -ADAM_LR * (m_hat / (jnp.sqrt(v_hat) + ADAM_EPS) + ADAM_WD * w_ref[...])
            mo_ref[...] = m2
            vo_ref[...] = v2

    spec = pl.BlockSpec((rb, cols), lambda i: (i, 0))
    res, _ = _pallas(
        body, name="adamw_" + tag, args=[*ws, *gs, *ms, *vs], out_shape=[jax.ShapeDtypeStruct(ws[0].shape, F32)] * (4 * n),
        grid=(rows // rb,), in_specs=[spec] * (4 * n), out_specs=[spec] * (4 * n))
    return [tuple(res[k * n + j] for k in range(4)) for j in range(n)]


def _pack_vec(p, d, width):
    pad = lambda v: jnp.pad(v, (0, LANES - v.shape[0]))
    row3 = jnp.concatenate([p["pool_scale"], p["out_norm_pool"]])
    row4 = jnp.concatenate([p["out_norm_attn"], pad(p["q_norm"]), pad(p["k_norm"]), pad(p["b_forget"]),
                            jnp.zeros((d - width - 3 * LANES,), F32)])
    rows = [p["ffn1_norm"], p["mix_norm"], p["ffn2_norm"], row3, row4]
    return jnp.pad(jnp.stack(rows), ((0, VEC_ROWS - len(rows)), (0, 0)))


def _unpack_vec(vec, width):
    return dict(ffn1_norm=vec[0], mix_norm=vec[1], ffn2_norm=vec[2], pool_scale=vec[3, :width],
                out_norm_pool=vec[3, width:2 * width], out_norm_attn=vec[4, :width],
                q_norm=vec[4, width:width + HEAD_DIM], k_norm=vec[4, width + LANES:width + LANES + HEAD_DIM],
                b_forget=vec[4, width + 2 * LANES:width + 2 * LANES + N_HEADS])


WEIGHT_NAMES = ("ffn1_norm", "ffn1_w_gate", "ffn1_w_up", "ffn1_w_down", "mix_norm", "w_in", "b_forget", "pool_w",
                "pool_scale", "q_norm", "k_norm", "out_norm_pool", "out_norm_attn", "w_out", "ffn2_norm",
                "ffn2_w_gate", "ffn2_w_up", "ffn2_w_down")
BIG_NAMES = ("ffn1_w_gate", "ffn1_w_up", "ffn1_w_down", "w_in", "w_out", "ffn2_w_gate", "ffn2_w_up", "ffn2_w_down")
TRANSPOSED_NAMES = ("ffn1_w_gate", "ffn1_w_up", "w_in", "ffn2_w_gate", "ffn2_w_up")
FFN1_NAMES = ("ffn1_w_gate", "ffn1_w_up", "ffn1_w_down")
MIX_NAMES = ("w_in", "w_out")
FFN2_NAMES = ("ffn2_w_gate", "ffn2_w_up", "ffn2_w_down")


def kernel(x, ffn1_norm, ffn1_w_gate, ffn1_w_up, ffn1_w_down, mix_norm, w_in, b_forget, pool_w, pool_scale, q_norm, k_norm, out_norm_pool, out_norm_attn, w_out, ffn2_norm, ffn2_w_gate, ffn2_w_up, ffn2_w_down, loss_target, m_ffn1_norm, m_ffn1_w_gate, m_ffn1_w_up, m_ffn1_w_down, m_mix_norm, m_w_in, m_b_forget, m_pool_w, m_pool_scale, m_q_norm, m_k_norm, m_out_norm_pool, m_out_norm_attn, m_w_out, m_ffn2_norm, m_ffn2_w_gate, m_ffn2_w_up, m_ffn2_w_down, v_ffn1_norm, v_ffn1_w_gate, v_ffn1_w_up, v_ffn1_w_down, v_mix_norm, v_w_in, v_b_forget, v_pool_w, v_pool_scale, v_q_norm, v_k_norm, v_out_norm_pool, v_out_norm_attn, v_w_out, v_ffn2_norm, v_ffn2_w_gate, v_ffn2_w_up, v_ffn2_w_down):
    given = dict(locals())
    w = {n: given[n] for n in WEIGHT_NAMES}
    m = {n: given["m_" + n] for n in WEIGHT_NAMES}
    v = {n: given["v_" + n] for n in WEIGHT_NAMES}
    n_batch, seq, d = x.shape
    width = pool_scale.shape[0]
    in_rows = w_in.shape[1]
    in_cols = N_CHIPS * in_rows
    in_pad = -(-in_rows // 32) * 32
    in_cols_pad = in_cols - N_HEADS + LANES

    work = lambda a, n: a.T if n in TRANSPOSED_NAMES else a
    exchanged = lambda a, n: jnp.pad(a, ((0, in_pad - in_rows), (0, 0))) if n == "w_in" else a

    mesh_x, mesh_y, mesh_c = _mesh_pos()
    ids = jnp.stack([2 * mesh_x + mesh_y, mesh_c]).astype(jnp.int32)

    row = lambda a: a.reshape(1, -1)
    g1, gm, g2, ps, onp, ona = (row(a) for a in (ffn1_norm, mix_norm, ffn2_norm, pool_scale, out_norm_pool, out_norm_attn))
    qn, kn = row(jnp.tile(q_norm, N_HEADS)), row(jnp.tile(k_norm, N_HEADS))
    bf = row(jnp.pad(b_forget, (0, LANES - N_HEADS)))
    pwb = pool_w.astype(BF16)
    xf, tgt = x.reshape(n_batch * seq, d), loss_target.reshape(n_batch * seq, d)

    def grouped(call, names, *lists):
        out = [None] * len(names)
        for idx in _same_shape_groups(lists[0]):
            res = call(*[[lst[i] for i in idx] for lst in lists], names[idx[0]])
            for i, r in zip(idx, res):
                out[i] = r
        return out

    placed = dict(zip(BIG_NAMES, grouped(lambda ws, tag: _place_cast(ws, ids, tag), BIG_NAMES,
                                         [exchanged(work(w[n], n), n) for n in BIG_NAMES])))
    landing = jnp.stack([2 * cx + cy for cx, cy in [(mesh_x, mesh_y)] + _other_chips(mesh_x, mesh_y)]).astype(jnp.int32)
    (x1, h1, a1, b1, s1), (wg1, wu1, wd1), (w_in_all, w_out_all) = _ffn_fwd_gathering(
        xf, g1, [placed[n] for n in FFN1_NAMES], landing, _plan_gather([placed[n] for n in MIX_NAMES]))
    w_in_t = jnp.pad(w_in_all[:, :in_rows].reshape(in_cols, d), ((0, in_cols_pad - in_cols), (0, 0)))
    w_out_full = w_out_all.reshape(N_CHIPS * w_out.shape[0], d)
    woa, wob = w_out_full[:width], w_out_full[width:]

    hm, pv, q, k, qh, kh, vb, f = _mix_proj(x1, gm, w_in_t, qn, kn, width, width)
    qa, ka = _forget_prefix(f, bf, qh, kh, n_batch, seq)
    yp = _pool_fwd(pv, pwb, ps, onp, n_batch, seq)
    (o, lse), (wg2, wu2, wd2) = _attn_fwd(qa, ka, vb, n_batch, seq, plan=_plan_gather_relay([placed[n] for n in FFN2_NAMES]))
    x2, ya = _mix_out(x1, yp, o, ona, woa, wob)
    (dy, h2, a2, b2, s2, lpart, dyh), _ = _ffn_fwd(x2, g2, wg2, wu2, wd2, target=tgt)

    def to_chips(gs, arrived, tags):
        return grouped(lambda g, r, tag: _add_sibling(g, r, ids, tag), tags, gs, arrived)

    def own_rows(gs, from_sibling, from_chips, tags):
        return grouped(lambda g, ra, rb, tag: _add_chips(g, ra, rb, ids, tag), tags, gs, from_sibling, from_chips)

    (dx2, da2, db2, dg2), _ = _ffn_bwd_x(dy, x2, g2, a2, b2, wg2, wu2, wd2, "ffn2_bwd_x")
    dw2, _ = _ffn_bwd_w([(da2, h2), (db2, h2), (s2, dyh)], "ffn2_bwd_w")
    (dyp, do, delta, dwoa, dwob, dona), sib2 = _mix_out_bwd(dx2, o, yp, ya, ona, woa, wob, plan=_plan_sibling_halves(dw2))
    dpv, dpw, dps, donp = _pool_bwd(pv, dyp, pwb, ps, onp, n_batch, seq)
    (dqh, dkh, dv, dfq, dfk), chips2 = _attn_bwd(qa, ka, vb, do, lse, delta, n_batch, seq,
                                                 plan=_plan_chip_exchange(to_chips(dw2, sib2, FFN2_NAMES)))
    df, dbf = _forget_bwd(dfq, dfk, f, bf, n_batch, seq)
    dx1, dx1h, dw_in_t, dgm, dqn, dkn = _mix_in_bwd(dx2, x1, gm, hm, dpv, dqh, q, dkh, k, dv, df, qn, kn, w_in_t)
    in_base = [in_rows * k // 8 * 8 for k in range(N_CHIPS)]
    d_w_in = jnp.stack([dw_in_t[b:b + in_pad] for b in in_base])
    d_w_out = jnp.concatenate([dwoa, dwob], axis=0).reshape(N_CHIPS, w_out.shape[0], d)
    dwm = [d_w_in, d_w_out]
    down = FFN1_NAMES[2:]
    dwd1, arrived = _ffn_bwd_w([(s1, dx1h)], "ffn1_bwd_w_down",
                               plan=_merge_plans(_plan_sibling_halves(dwm), _plan_all_to_all(
                                   [_pool_pack(dpw.reshape(n_batch, -1, pool_w.shape[-1]))])))
    sibm, pstack = arrived[:len(dwm)], arrived[len(dwm)]
    (da1, db1), arrived = _ffn_bwd_a(dx1h, a1, b1, wd1, "ffn1_bwd_a",
                                     plan=_merge_plans(_plan_sibling_halves(dwd1),
                                                       _plan_chip_exchange(to_chips(dwm, sibm, MIX_NAMES))))
    sibd, chipsm = arrived[:1], arrived[1:]
    gate_up = FFN1_NAMES[:2]
    dwgu1, chipsd = _ffn_bwd_w([(da1, h1), (db1, h1)], "ffn1_bwd_w_gate_up",
                               plan=_plan_chip_exchange(to_chips(dwd1, sibd, down)))
    n_tiles = (n_batch * seq) // min(FFN_TILE, n_batch * seq)
    first = max(n_tiles // 2, 1)
    begun, sibgu = _ffn_bwd_h(dx1, xf, g1, da1, db1, wg1, wu1, "ffn1_bwd_h_first", (0, first),
                              plan=_plan_sibling_halves(dwgu1))
    earlier = (own_rows(dwd1, sibd, chipsd, down) + own_rows(dwm, sibm, chipsm, MIX_NAMES)
               + own_rows(dw2, sib2, chips2, FFN2_NAMES))
    (gx, dg1), arrived = _ffn_bwd_h(dx1, xf, g1, da1, db1, wg1, wu1, "ffn1_bwd_h_rest", (first, n_tiles), prev=begun,
                                    plan=_merge_plans(_plan_chip_exchange(to_chips(dwgu1, sibgu, gate_up)),
                                                      _plan_sibling_share(earlier)))
    chipsgu, shared = arrived[:len(gate_up)], arrived[len(gate_up):]

    part = dict(ffn1_norm=dg1, mix_norm=dgm, ffn2_norm=dg2, b_forget=dbf, pool_scale=dps, out_norm_pool=donp,
                out_norm_attn=dona, qn=dqn, kn=dkn, loss=lpart)
    mine = own_rows(dwgu1, sibgu, chipsgu, gate_up)
    last = _run_plan(_merge_plans(_plan_sibling_share(mine), _plan_all_to_all([_small_pack(part, d, width)])), "last_exchange")
    vstack = last[len(mine)]
    g_vec, g_pw = _small_sum(vstack, pstack, jnp.reshape(4 * mesh_x + 2 * mesh_y + mesh_c, (1,)).astype(jnp.int32))
    loss = g_vec[5, 0]
    reduced = dict(zip(gate_up + down + MIX_NAMES + FFN2_NAMES, list(last[:len(mine)]) + list(shared)))
    reduced["w_in"] = lax.dynamic_slice(reduced["w_in"], ((in_rows * ids[0]) % 8, 0), (in_rows, d))

    grads, delta, new_m, new_v = {}, {}, {}, {}
    for names in (FFN2_NAMES, FFN1_NAMES, ("w_in",), ("w_out",)):
        stepped = _adamw([work(w[n], n) for n in names], [reduced[n] for n in names], [work(m[n], n) for n in names],
                         [work(v[n], n) for n in names], names[0])
        for n, step in zip(names, stepped):
            grads[n], delta[n], new_m[n], new_v[n] = (work(a, n) for a in step)
    flat_pw = lambda a: a.reshape(-1, a.shape[-1])
    (_, d_pw, m_pw, v_pw), = _adamw([flat_pw(pool_w)], [g_pw], [flat_pw(m_pool_w)], [flat_pw(v_pool_w)], "pool_w")
    (_, d_vec, m_vec, v_vec), = _adamw([_pack_vec(w, d, width)], [g_vec], [_pack_vec(m, d, width)],
                                       [_pack_vec(v, d, width)], "vectors")
    grads.update(_unpack_vec(g_vec, width), pool_w=g_pw.reshape(pool_w.shape))
    delta.update(_unpack_vec(d_vec, width), pool_w=d_pw.reshape(pool_w.shape))
    new_m.update(_unpack_vec(m_vec, width), pool_w=m_pw.reshape(pool_w.shape))
    new_v.update(_unpack_vec(v_vec, width), pool_w=v_pw.reshape(pool_w.shape))
    return (loss, gx.reshape(x.shape), *[grads[n] for n in WEIGHT_NAMES], *[delta[n] for n in WEIGHT_NAMES],
            *[new_m[n] for n in WEIGHT_NAMES], *[new_v[n] for n in WEIGHT_NAMES])
```
